```python
import math
import jax, jax.numpy as jnp
from jax import lax
import numpy as np

D_MODEL = 1024
BATCH = 16
SEQ = 2048
DEPTH = 1

HEAD_DIM = 128
HEADS_PER_GROUP = 4
DILATED_GROUPS = ((128, 1), (512, 4), (2048, 16))
N_GROUPS = 3
N_ATTN_HEADS = N_GROUPS * HEADS_PER_GROUP
QKV_WIDTH = N_ATTN_HEADS * HEAD_DIM
ATTN_WIDTH = HEADS_PER_GROUP * HEAD_DIM
CONV_WIDTH = D_MODEL
CONV_K = 3
N_BUCKETS = 32
MAX_EXACT = 16
MAX_DISTANCE = 2048
BLOCK = 128
DEEPNORM_ALPHA = (2.0 * DEPTH) ** 0.25
DEEPNORM_BETA = (8.0 * DEPTH) ** -0.25
LN_EPS = 1e-5
NEG_INF = -1e30
COL_WIDTHS = (QKV_WIDTH, QKV_WIDTH, QKV_WIDTH, ATTN_WIDTH,
              CONV_WIDTH, CONV_WIDTH, CONV_WIDTH, CONV_WIDTH,
              D_MODEL, D_MODEL)
IN_COLS = 4 * QKV_WIDTH // 4 * 3 // 3 * 1 * 3 + ATTN_WIDTH + 4 * CONV_WIDTH + 2 * D_MODEL

kernel_name = "hybrid_dilated_attn_shortconv_gated_merge"


def _split_cols(proj):
    offsets = []
    acc = 0
    for w in COL_WIDTHS[:-1]:
        acc += w
        offsets.append(acc)
    return jnp.split(proj, offsets, axis=-1)


def _t5_bucket(dist):
    n = jnp.maximum(dist, 1).astype(jnp.float32)
    large = MAX_EXACT + (jnp.log(n / MAX_EXACT) / math.log(MAX_DISTANCE / MAX_EXACT)
                         * (N_BUCKETS - MAX_EXACT)).astype(jnp.int32)
    large = jnp.minimum(large, N_BUCKETS - 1)
    return jnp.where(dist < MAX_EXACT, dist, large)


def _dilated_group_attention(q, k, v, bias_tab, window, dilation):
    bsz, seq, n_h, e = q.shape
    sub_len = seq // dilation
    n_blk = -(-sub_len // BLOCK)
    sub_pad = n_blk * BLOCK
    n_steps = window // dilation

    def to_sub(t):
        t = t.reshape(bsz, sub_len, dilation, n_h, e).transpose(0, 2, 3, 1, 4)
        return jnp.pad(t, ((0, 0), (0, 0), (0, 0), (0, sub_pad - sub_len), (0, 0)))

    def band(t):
        t = jnp.pad(t, ((0, 0), (0, 0), (0, 0), (BLOCK, 0), (0, 0)))
        t = t.reshape(bsz, dilation, n_h, n_blk + 1, BLOCK, e)
        return jnp.concatenate([t[:, :, :, :-1], t[:, :, :, 1:]], axis=4)

    qb = to_sub(q).reshape(bsz, dilation, n_h, n_blk, BLOCK, e)
    kb = band(to_sub(k))
    vb = band(to_sub(v))

    a_idx = jnp.arange(BLOCK)[:, None]
    b_idx = jnp.arange(2 * BLOCK)[None, :]
    steps = a_idx + BLOCK - b_idx
    key_sub = jnp.arange(n_blk)[:, None, None] * BLOCK - BLOCK + b_idx[None]
    valid = ((steps >= 0) & (steps <= n_steps))[None] & (key_sub >= 0)
    bucket = _t5_bucket(jnp.maximum(steps, 0) * dilation)
    bias = bias_tab[bucket].transpose(2, 0, 1).astype(jnp.float32)

    s = jnp.einsum('bdhnqe,bdhnke->bdhnqk', qb, kb).astype(jnp.float32) * (HEAD_DIM ** -0.5)
    s = s + bias[None, None, :, None]
    s = jnp.where(valid[None, None, None], s, NEG_INF)
    lse = jax.nn.logsumexp(s, axis=-1)
    p = jnp.exp(s - lse[..., None]).astype(v.dtype)
    o = jnp.einsum('bdhnqk,bdhnke->bdhnqe', p, vb)
    o = o.reshape(bsz, dilation, n_h, sub_pad, e)[:, :, :, :sub_len]
    lse = lse.reshape(bsz, dilation, n_h, sub_pad)[..., :sub_len]
    o = o.transpose(0, 3, 1, 2, 4).reshape(bsz, seq, n_h, e)
    lse = lse.transpose(0, 3, 1, 2).reshape(bsz, seq, n_h)
    return o, lse


def _layer_norm(x, g, b):
    xf = x.astype(jnp.float32)
    mu = jnp.mean(xf, axis=-1, keepdims=True)
    var = jnp.mean(jnp.square(xf - mu), axis=-1, keepdims=True)
    return ((xf - mu) * lax.rsqrt(var + LN_EPS) * g + b).astype(x.dtype)


def _fwd_setup_inputs(seed: int = 0) -> dict:
    key = jax.random.key(seed)
    ks = jax.random.split(key, 14)
    f32 = jnp.float32
    x = jax.random.normal(ks[0], (BATCH, SEQ, D_MODEL), f32)
    c = jax.random.normal(ks[1], (BATCH, D_MODEL), f32)
    w_ada = jax.random.normal(ks[2], (DEPTH, D_MODEL, 3 * D_MODEL), f32) * (0.1 * D_MODEL ** -0.5)
    b_ada = jax.random.normal(ks[3], (DEPTH, 3 * D_MODEL), f32) * 0.01
    n_cols = sum(COL_WIDTHS)
    col_scale = np.ones((n_cols,), np.float32)
    col_scale[2 * QKV_WIDTH:3 * QKV_WIDTH] = DEEPNORM_BETA
    w_in = jax.random.normal(ks[4], (DEPTH, D_MODEL, n_cols), f32) * (D_MODEL ** -0.5) * jnp.asarray(col_scale)
    conv_w = jax.random.normal(ks[5], (DEPTH, CONV_K, CONV_WIDTH), f32) * (CONV_K ** -0.5)
    conv_b = jax.random.normal(ks[6], (DEPTH, CONV_WIDTH), f32) * 0.01
    rel_bias = jax.random.normal(ks[7], (N_BUCKETS, N_ATTN_HEADS), f32) * 0.5
    w_attn_out = jax.random.normal(ks[8], (DEPTH, ATTN_WIDTH, D_MODEL), f32) * (ATTN_WIDTH ** -0.5) * DEEPNORM_BETA
    w_conv_out = jax.random.normal(ks[9], (DEPTH, CONV_WIDTH, D_MODEL), f32) * (CONV_WIDTH ** -0.5) * DEEPNORM_BETA
    w_o = jax.random.normal(ks[10], (DEPTH, D_MODEL, D_MODEL), f32) * (D_MODEL ** -0.5) * DEEPNORM_BETA
    ln_g = 1.0 + jax.random.normal(ks[11], (DEPTH, D_MODEL), f32) * 0.01
    ln_b = jax.random.normal(ks[12], (DEPTH, D_MODEL), f32) * 0.01
    return {"x": x, "c": c, "w_ada": w_ada, "b_ada": b_ada, "w_in": w_in,
            "conv_w": conv_w, "conv_b": conv_b, "rel_bias": rel_bias,
            "w_attn_out": w_attn_out, "w_conv_out": w_conv_out, "w_o": w_o,
            "ln_g": ln_g, "ln_b": ln_b}


def _fwd_reference(x, c, w_ada, b_ada, w_in, conv_w, conv_b, rel_bias,
              w_attn_out, w_conv_out, w_o, ln_g, ln_b):
    bsz, seq, _ = x.shape
    for layer in range(DEPTH):
        mod = jax.nn.silu(c) @ w_ada[layer] + b_ada[layer]
        shift, scale, gate = jnp.split(mod, 3, axis=-1)
        h = x * (1.0 + scale[:, None]) + shift[:, None]

        proj = h @ w_in[layer]
        q, k, v, g_attn, u, b_gate, c_gate, g_conv, m_attn, m_conv = _split_cols(proj)

        q = q.reshape(bsz, seq, N_ATTN_HEADS, HEAD_DIM)
        k = k.reshape(bsz, seq, N_ATTN_HEADS, HEAD_DIM)
        v = v.reshape(bsz, seq, N_ATTN_HEADS, HEAD_DIM)
        outs, lses = [], []
        for gi, (window, dilation) in enumerate(DILATED_GROUPS):
            hs = slice(gi * HEADS_PER_GROUP, (gi + 1) * HEADS_PER_GROUP)
            o_g, lse_g = _dilated_group_attention(q[:, :, hs], k[:, :, hs], v[:, :, hs],
                                                  rel_bias[:, hs], window, dilation)
            outs.append(o_g)
            lses.append(lse_g)
        o_all = jnp.stack(outs, axis=0)
        wts = jax.nn.softmax(jnp.stack(lses, axis=0), axis=0)
        o = jnp.sum(wts[..., None].astype(o_all.dtype) * o_all, axis=0).reshape(bsz, seq, ATTN_WIDTH)
        a_out = (o * jax.nn.silu(g_attn)) @ w_attn_out[layer]

        z = c_gate * u
        zp = jnp.pad(z, ((0, 0), (CONV_K - 1, 0), (0, 0)))
        cw = conv_w[layer]
        y_conv = (cw[0] * zp[:, :-2] + cw[1] * zp[:, 1:-1] + cw[2] * zp[:, 2:]) + conv_b[layer]
        s_out = (b_gate * y_conv * jax.nn.silu(g_conv)) @ w_conv_out[layer]

        merged = jax.nn.sigmoid(m_attn) * a_out + jax.nn.sigmoid(m_conv) * s_out
        y = merged @ w_o[layer]

        x = _layer_norm(DEEPNORM_ALPHA * x + (1.0 + gate[:, None]) * y, ln_g[layer], ln_b[layer])
    return x


import jax as _jax
import jax.numpy as _jnp

TWIN_FORMAT = 'train_step'
FWD_PARAMS = ['x', 'c', 'w_ada', 'b_ada', 'w_in', 'conv_w', 'conv_b', 'rel_bias', 'w_attn_out', 'w_conv_out', 'w_o', 'ln_g', 'ln_b']
TWIN_WEIGHTS = ['w_ada', 'b_ada', 'w_in', 'conv_w', 'conv_b', 'rel_bias', 'w_attn_out', 'w_conv_out', 'w_o', 'ln_g', 'ln_b']
TWIN_DIFF_INPUT = 'x'
TWIN_INPUTS = ['x', 'c', 'w_ada', 'b_ada', 'w_in', 'conv_w', 'conv_b', 'rel_bias', 'w_attn_out', 'w_conv_out', 'w_o', 'ln_g', 'ln_b', 'loss_target', 'm_w_ada', 'm_b_ada', 'm_w_in', 'm_conv_w', 'm_conv_b', 'm_rel_bias', 'm_w_attn_out', 'm_w_conv_out', 'm_w_o', 'm_ln_g', 'm_ln_b', 'v_w_ada', 'v_b_ada', 'v_w_in', 'v_conv_w', 'v_conv_b', 'v_rel_bias', 'v_w_attn_out', 'v_w_conv_out', 'v_w_o', 'v_ln_g', 'v_ln_b']
TWIN_OUTPUTS = ['loss', 'grad_x', 'grad_w_ada', 'grad_b_ada', 'grad_w_in', 'grad_conv_w', 'grad_conv_b', 'grad_rel_bias', 'grad_w_attn_out', 'grad_w_conv_out', 'grad_w_o', 'grad_ln_g', 'grad_ln_b', 'delta_w_ada', 'delta_b_ada', 'delta_w_in', 'delta_conv_w', 'delta_conv_b', 'delta_rel_bias', 'delta_w_attn_out', 'delta_w_conv_out', 'delta_w_o', 'delta_ln_g', 'delta_ln_b', 'new_m_w_ada', 'new_m_b_ada', 'new_m_w_in', 'new_m_conv_w', 'new_m_conv_b', 'new_m_rel_bias', 'new_m_w_attn_out', 'new_m_w_conv_out', 'new_m_w_o', 'new_m_ln_g', 'new_m_ln_b', 'new_v_w_ada', 'new_v_b_ada', 'new_v_w_in', 'new_v_conv_w', 'new_v_conv_b', 'new_v_rel_bias', 'new_v_w_attn_out', 'new_v_w_conv_out', 'new_v_w_o', 'new_v_ln_g', 'new_v_ln_b']
TWIN_LEAF_KINDS = {'loss': 'loss', 'grad_x': 'grad_x', 'grad_w_ada': 'grad_w', 'grad_b_ada': 'grad_w', 'grad_w_in': 'grad_w', 'grad_conv_w': 'grad_w', 'grad_conv_b': 'grad_w', 'grad_rel_bias': 'grad_w', 'grad_w_attn_out': 'grad_w', 'grad_w_conv_out': 'grad_w', 'grad_w_o': 'grad_w', 'grad_ln_g': 'grad_w', 'grad_ln_b': 'grad_w', 'delta_w_ada': 'delta_w', 'delta_b_ada': 'delta_w', 'delta_w_in': 'delta_w', 'delta_conv_w': 'delta_w', 'delta_conv_b': 'delta_w', 'delta_rel_bias': 'delta_w', 'delta_w_attn_out': 'delta_w', 'delta_w_conv_out': 'delta_w', 'delta_w_o': 'delta_w', 'delta_ln_g': 'delta_w', 'delta_ln_b': 'delta_w', 'new_m_w_ada': 'new_m', 'new_m_b_ada': 'new_m', 'new_m_w_in': 'new_m', 'new_m_conv_w': 'new_m', 'new_m_conv_b': 'new_m', 'new_m_rel_bias': 'new_m', 'new_m_w_attn_out': 'new_m', 'new_m_w_conv_out': 'new_m', 'new_m_w_o': 'new_m', 'new_m_ln_g': 'new_m', 'new_m_ln_b': 'new_m', 'new_v_w_ada': 'new_v', 'new_v_b_ada': 'new_v', 'new_v_w_in': 'new_v', 'new_v_conv_w': 'new_v', 'new_v_conv_b': 'new_v', 'new_v_rel_bias': 'new_v', 'new_v_w_attn_out': 'new_v', 'new_v_w_conv_out': 'new_v', 'new_v_w_o': 'new_v', 'new_v_ln_g': 'new_v', 'new_v_ln_b': 'new_v'}


def _forward(args):
    return _fwd_reference(*[args[k] for k in FWD_PARAMS])


def _output_shape():
    out = _jax.eval_shape(lambda: _forward(_fwd_setup_inputs(0)))
    return out.shape, out.dtype

N_MICROBATCH = 1
ADAM_LR = 0.001
ADAM_B1 = 0.9
ADAM_B2 = 0.999
ADAM_EPS = 1e-08
ADAM_WD = 0.01
ADAM_STEP = 10
PER_EXAMPLE_BATCH_AXIS = {'x': 0, 'c': 0, 'loss_target': 0}
SHARED_INPUTS = []
_WEIGHT_DTYPES = {'w_ada': _jnp.float32, 'b_ada': _jnp.float32, 'w_in': _jnp.float32, 'conv_w': _jnp.float32, 'conv_b': _jnp.float32, 'rel_bias': _jnp.float32, 'w_attn_out': _jnp.float32, 'w_conv_out': _jnp.float32, 'w_o': _jnp.float32, 'ln_g': _jnp.float32, 'ln_b': _jnp.float32}
MOMENT_SCALE = {'w_ada': 1.942260e-02, 'b_ada': 3.281208e-02, 'w_in': 1.074743e-02, 'conv_w': 1.758595e-02, 'conv_b': 1.686709e-02, 'rel_bias': 2.638223e-03, 'w_attn_out': 3.632913e-03, 'w_conv_out': 2.888306e-02, 'w_o': 2.916022e-02, 'ln_g': 3.197153e+01, 'ln_b': 3.091025e-01}


def _to_microbatches(a, axis):
    t = _jnp.moveaxis(a, axis, 0)
    t = t.reshape((N_MICROBATCH, t.shape[0] // N_MICROBATCH) + t.shape[1:])
    return _jnp.moveaxis(t, 1, axis + 1)


def setup_inputs(seed: int = 0) -> dict:
    inp = _fwd_setup_inputs(seed)
    key = _jax.random.fold_in(_jax.random.key(seed), 7919)
    shape, _ = _output_shape()
    out = dict(inp)
    out["loss_target"] = _jax.random.normal(_jax.random.fold_in(key, 0), shape, _jnp.float32)
    for i, name in enumerate(TWIN_WEIGHTS):
        w = inp[name].astype(_jnp.float32)
        if MOMENT_SCALE is None:
            s = _jnp.sqrt(_jnp.mean(_jnp.square(w)) + 1e-30)
        else:
            s = MOMENT_SCALE[name]
        km, kv = _jax.random.split(_jax.random.fold_in(key, i + 1))
        out[name] = w
        out["m_" + name] = s * _jax.random.normal(km, w.shape, _jnp.float32)
        out["v_" + name] = (s * s) * _jax.random.uniform(kv, w.shape, _jnp.float32, 0.5, 1.5)
    if N_MICROBATCH > 1:
        for name, axis in PER_EXAMPLE_BATCH_AXIS.items():
            out[name] = _to_microbatches(out[name], axis)
    return {'x': out['x'], 'c': out['c'], 'w_ada': out['w_ada'], 'b_ada': out['b_ada'], 'w_in': out['w_in'], 'conv_w': out['conv_w'], 'conv_b': out['conv_b'], 'rel_bias': out['rel_bias'], 'w_attn_out': out['w_attn_out'], 'w_conv_out': out['w_conv_out'], 'w_o': out['w_o'], 'ln_g': out['ln_g'], 'ln_b': out['ln_b'], 'loss_target': out['loss_target'], 'm_w_ada': out['m_w_ada'], 'm_b_ada': out['m_b_ada'], 'm_w_in': out['m_w_in'], 'm_conv_w': out['m_conv_w'], 'm_conv_b': out['m_conv_b'], 'm_rel_bias': out['m_rel_bias'], 'm_w_attn_out': out['m_w_attn_out'], 'm_w_conv_out': out['m_w_conv_out'], 'm_w_o': out['m_w_o'], 'm_ln_g': out['m_ln_g'], 'm_ln_b': out['m_ln_b'], 'v_w_ada': out['v_w_ada'], 'v_b_ada': out['v_b_ada'], 'v_w_in': out['v_w_in'], 'v_conv_w': out['v_conv_w'], 'v_conv_b': out['v_conv_b'], 'v_rel_bias': out['v_rel_bias'], 'v_w_attn_out': out['v_w_attn_out'], 'v_w_conv_out': out['v_w_conv_out'], 'v_w_o': out['v_w_o'], 'v_ln_g': out['v_ln_g'], 'v_ln_b': out['v_ln_b']}


def _loss(weights, diff, rest, loss_target):
    with _jax.named_scope("forward"):
        args = {**rest, TWIN_DIFF_INPUT: diff, **{k: w.astype(_WEIGHT_DTYPES[k]) for k, w in weights.items()}}
        y = _forward(args)
    with _jax.named_scope("loss_head"):
        err = _jnp.square(y.astype(_jnp.float32) - loss_target)
        return 0.5 * _jnp.sum(_jnp.mean(err, axis=-1)) if err.ndim else 0.5 * err


def _adamw(w, g, m, v):
    m = ADAM_B1 * m + (1.0 - ADAM_B1) * g
    v = ADAM_B2 * v + (1.0 - ADAM_B2) * _jnp.square(g)
    m_hat = m / (1.0 - ADAM_B1 ** ADAM_STEP)
    v_hat = v / (1.0 - ADAM_B2 ** ADAM_STEP)
    delta = -ADAM_LR * (m_hat / (_jnp.sqrt(v_hat) + ADAM_EPS) + ADAM_WD * w)
    return delta, m, v


def reference(x, c, w_ada, b_ada, w_in, conv_w, conv_b, rel_bias, w_attn_out, w_conv_out, w_o, ln_g, ln_b, loss_target, m_w_ada, m_b_ada, m_w_in, m_conv_w, m_conv_b, m_rel_bias, m_w_attn_out, m_w_conv_out, m_w_o, m_ln_g, m_ln_b, v_w_ada, v_b_ada, v_w_in, v_conv_w, v_conv_b, v_rel_bias, v_w_attn_out, v_w_conv_out, v_w_o, v_ln_g, v_ln_b):
    given = dict(x=x, c=c, w_ada=w_ada, b_ada=b_ada, w_in=w_in, conv_w=conv_w, conv_b=conv_b, rel_bias=rel_bias, w_attn_out=w_attn_out, w_conv_out=w_conv_out, w_o=w_o, ln_g=ln_g, ln_b=ln_b, loss_target=loss_target, m_w_ada=m_w_ada, m_b_ada=m_b_ada, m_w_in=m_w_in, m_conv_w=m_conv_w, m_conv_b=m_conv_b, m_rel_bias=m_rel_bias, m_w_attn_out=m_w_attn_out, m_w_conv_out=m_w_conv_out, m_w_o=m_w_o, m_ln_g=m_ln_g, m_ln_b=m_ln_b, v_w_ada=v_w_ada, v_b_ada=v_b_ada, v_w_in=v_w_in, v_conv_w=v_conv_w, v_conv_b=v_conv_b, v_rel_bias=v_rel_bias, v_w_attn_out=v_w_attn_out, v_w_conv_out=v_w_conv_out, v_w_o=v_w_o, v_ln_g=v_ln_g, v_ln_b=v_ln_b)
    weights = {n: given[n] for n in TWIN_WEIGHTS}
    shared = {n: given[n] for n in SHARED_INPUTS}
    per_example = {n: given[n] for n in ['x', 'c']}
    grad_fn = _jax.value_and_grad(_loss, argnums=(0, 1))

    def one_microbatch(ex, loss_target):
        ex = dict(ex)
        diff = ex.pop(TWIN_DIFF_INPUT)
        return grad_fn(weights, diff, {**shared, **ex}, loss_target)

    if N_MICROBATCH == 1:
        loss, (grad_w, grad_x) = one_microbatch(per_example, given["loss_target"])
    else:
        def body(carry, xs):
            loss_sum, grad_sum = carry
            l_k, (gw_k, gx_k) = one_microbatch(xs[0], xs[1])
            with _jax.named_scope("update"):
                return (loss_sum + l_k, _jax.tree.map(_jnp.add, grad_sum, gw_k)), gx_k

        init = (_jnp.zeros((), _jnp.float32), _jax.tree.map(_jnp.zeros_like, weights))
        (loss, grad_w), grad_x = _jax.lax.scan(body, init, (per_example, given["loss_target"]))
    with _jax.named_scope("update"):
        delta_w, new_m, new_v = {}, {}, {}
        for n in TWIN_WEIGHTS:
            delta_w[n], new_m[n], new_v[n] = _adamw(weights[n], grad_w[n], given["m_" + n], given["v_" + n])
    return (loss, grad_x, *[grad_w[n] for n in TWIN_WEIGHTS], *[delta_w[n] for n in TWIN_WEIGHTS],
            *[new_m[n] for n in TWIN_WEIGHTS], *[new_v[n] for n in TWIN_WEIGHTS])
```

```python
import functools
import math

import numpy as np
import jax
import jax.numpy as jnp
from jax import lax
from jax.experimental import pallas as pl
from jax.experimental.pallas import tpu as pltpu

F32 = jnp.float32
BF16 = jnp.bfloat16
MESH = pl.DeviceIdType.MESH

N_DEV = 8
D = 1024
S = 2048
BL = 2
T = BL * S
NCOL = 11264
SHARD = NCOL // N_DEV
CB = 512
NCB = NCOL // CB
HD = 128
GW = 512
QB = 128
DILATIONS = (1, 4, 16)
N_STEPS = 128
N_BUCKETS = 32
N_HEADS = 12
ALPHA = 2.0 ** 0.25
LN_EPS = 1e-5
NEG_INF = -1e30
SCALE = HD ** -0.5
ADA_SHARD = 3 * D // N_DEV

CB_Q, CB_K, CB_V, CB_GA = 0, 3, 6, 9
KB_U, KB_BG, KB_CG, KB_GC, KB_MA, KB_MC = 5, 6, 7, 8, 9, 10

ADAM_LR, ADAM_B1, ADAM_B2, ADAM_EPS, ADAM_WD, ADAM_STEP = 0.001, 0.9, 0.999, 1e-08, 0.01, 10

VMEM_LIMIT = 56 * 1024 * 1024


def _dot(a, b):
    return jnp.dot(a, b, preferred_element_type=F32)


def _dot_nt(a, b):
    return lax.dot_general(a, b, (((1,), (1,)), ((), ())), preferred_element_type=F32)


def _dot_tn(a, b):
    return lax.dot_general(a, b, (((0,), (0,)), ((), ())), preferred_element_type=F32)


def _sigmoid(v):
    return 1.0 / (1.0 + jnp.exp(-v))


def _perm(jj):
    return jnp.where(jj < 9, (jj % 3) * 3 + jj // 3,
           jnp.where(jj < 11, jj + 1,
           jnp.where(jj < 13, jj + 3,
           jnp.where(jj == 13, 9,
           jnp.where(jj < 16, jj - 2, jj)))))


def _my_index():
    return 4 * lax.axis_index("x") + 2 * lax.axis_index("y") + lax.axis_index("c")


def _slot_leading(ref, slot):
    return ref.at[slot]


def _slot_columns(width):
    def f(ref, slot):
        return ref.at[:, pl.ds(pl.multiple_of(slot * width, 128), width)]
    return f


def _all_gather(arrs, out_shapes, slot_fns, name):
    n = len(arrs)

    def body(*refs):
        ins, outs = refs[:n], refs[n:2 * n]
        send_sems, recv_sems, local_sems = refs[2 * n:]
        x, y, c = lax.axis_index("x"), lax.axis_index("y"), lax.axis_index("c")
        me, sibling = (x, y, c), (x, y, 1 - c)
        chips = [(1 - x, y), (x, 1 - y), (1 - x, 1 - y)]

        def blk(a, dev):
            return slot_fns[a](outs[a], 4 * dev[0] + 2 * dev[1] + dev[2])

        def copy(a, k, block, to, src=None):
            dst = blk(a, block)
            return pltpu.make_async_remote_copy(
                src_ref=dst if src is None else src, dst_ref=dst,
                send_sem=send_sems.at[a * 7 + k], recv_sem=recv_sems.at[a * 7 + k],
                device_id=to, device_id_type=MESH)

        mine = [pltpu.make_async_copy(ins[a], blk(a, me), local_sems.at[a]) for a in range(n)]
        for cp in mine:
            cp.start()
        first = []
        for a in range(n):
            first.append(copy(a, 0, me, sibling, src=ins[a]))
            first += [copy(a, 1 + j, me, (*chip, c), src=ins[a]) for j, chip in enumerate(chips)]
        for cp in first:
            cp.start()
        passed = []
        for j, chip in enumerate(chips):
            for a in range(n):
                copy(a, 1 + j, (*chip, c), me).wait_recv()
                fwd = copy(a, 4 + j, (*chip, c), sibling)
                fwd.start()
                passed.append(fwd)
        for a in range(n):
            copy(a, 0, sibling, me).wait_recv()
        for j, chip in enumerate(chips):
            for a in range(n):
                copy(a, 4 + j, (*chip, 1 - c), me).wait_recv()
        for cp in first + passed:
            cp.wait_send()
        for cp in mine:
            cp.wait()

    any_spec = pl.BlockSpec(memory_space=pl.ANY)
    return pl.pallas_call(
        body, name=name,
        out_shape=tuple(out_shapes),
        in_specs=[any_spec] * n,
        out_specs=tuple([any_spec] * n),
        scratch_shapes=[pltpu.SemaphoreType.DMA((7 * n,)), pltpu.SemaphoreType.DMA((7 * n,)),
                        pltpu.SemaphoreType.DMA((n,))],
    )(*arrs)


def _all_to_all(arrs, out_shapes, src_fns, name):
    n = len(arrs)

    def body(*refs):
        ins, outs = refs[:n], refs[n:2 * n]
        send_sems, recv_sems, local_sems = refs[2 * n:]
        x, y, c = lax.axis_index("x"), lax.axis_index("y"), lax.axis_index("c")
        my_slot = 4 * x + 2 * y + c

        def peer_of(k):
            dx, dy, dc = (k >> 2) & 1, (k >> 1) & 1, k & 1
            return ((1 - x) if dx else x, (1 - y) if dy else y, (1 - c) if dc else c)

        def copy(a, k):
            peer = peer_of(k)
            peer_slot = 4 * peer[0] + 2 * peer[1] + peer[2]
            return pltpu.make_async_remote_copy(
                src_ref=src_fns[a](ins[a], peer_slot), dst_ref=outs[a].at[my_slot],
                send_sem=send_sems.at[a * 7 + k - 1], recv_sem=recv_sems.at[a * 7 + k - 1],
                device_id=peer, device_id_type=MESH)

        def landed(a, k):
            peer = peer_of(k)
            peer_slot = 4 * peer[0] + 2 * peer[1] + peer[2]
            return pltpu.make_async_remote_copy(
                src_ref=src_fns[a](ins[a], my_slot), dst_ref=outs[a].at[peer_slot],
                send_sem=send_sems.at[a * 7 + k - 1], recv_sem=recv_sems.at[a * 7 + k - 1],
                device_id=peer, device_id_type=MESH)

        mine = [pltpu.make_async_copy(src_fns[a](ins[a], my_slot), outs[a].at[my_slot], local_sems.at[a])
                for a in range(n)]
        for cp in mine:
            cp.start()
        sends = [copy(a, k) for k in range(1, 8) for a in range(n)]
        for cp in sends:
            cp.start()
        for k in range(1, 8):
            for a in range(n):
                landed(a, k).wait_recv()
        for cp in sends:
            cp.wait_send()
        for cp in mine:
            cp.wait()

    any_spec = pl.BlockSpec(memory_space=pl.ANY)
    return pl.pallas_call(
        body, name=name,
        out_shape=tuple(out_shapes),
        in_specs=[any_spec] * n,
        out_specs=tuple([any_spec] * n),
        scratch_shapes=[pltpu.SemaphoreType.DMA((7 * n,)), pltpu.SemaphoreType.DMA((7 * n,)),
                        pltpu.SemaphoreType.DMA((n,))],
    )(*arrs)


def _ada_fwd(c_all, w_ada, b_cols):
    def body(c_ref, w_ref, b_ref, o_ref):
        cv = c_ref[...]
        sc = cv * _sigmoid(cv)
        o_ref[...] = jnp.dot(sc, w_ref[...], preferred_element_type=F32,
                             precision=lax.Precision.HIGHEST) + b_ref[...]

    return pl.pallas_call(
        body, name="ada_fwd",
        out_shape=jax.ShapeDtypeStruct((c_all.shape[0], w_ada.shape[1]), F32),
    )(c_all, w_ada, b_cols)


def _ada_bwd(c_all_t, dmod_cols):
    def body(c_ref, d_ref, o_ref):
        cv = c_ref[...]
        sc = cv * _sigmoid(cv)
        o_ref[...] = jnp.dot(sc, d_ref[...], preferred_element_type=F32,
                             precision=lax.Precision.HIGHEST)

    return pl.pallas_call(
        body, name="ada_bwd",
        out_shape=jax.ShapeDtypeStruct((c_all_t.shape[0], dmod_cols.shape[1]), F32),
    )(c_all_t, dmod_cols)


def _prep_h(x2, mod3):
    ts = 512
    per_seq = S // ts

    def body(x_ref, mod_ref, h_ref):
        shift = mod_ref[0, 0:1, :]
        scale = mod_ref[0, 1:2, :]
        h_ref[...] = (x_ref[...] * (1.0 + scale) + shift).astype(BF16)

    return pl.pallas_call(
        body, name="prep_h",
        grid=(T // ts,),
        in_specs=[pl.BlockSpec((ts, D), lambda i: (i, 0)),
                  pl.BlockSpec((1, 3, D), lambda i: (i // per_seq, 0, 0))],
        out_specs=pl.BlockSpec((ts, D), lambda i: (i, 0)),
        out_shape=jax.ShapeDtypeStruct((T, D), BF16),
    )(x2, mod3)


def _proj(h, w_in_full):
    tm, tn = 512, SHARD

    def body(h_ref, w_ref, o_ref):
        o_ref[...] = _dot(h_ref[...], w_ref[...]).astype(BF16)

    return pl.pallas_call(
        body, name="proj",
        grid=(NCOL // tn, T // tm),
        in_specs=[pl.BlockSpec((tm, D), lambda j, i: (i, 0)),
                  pl.BlockSpec((D, tn), lambda j, i: (0, j))],
        out_specs=pl.BlockSpec((tm, tn), lambda j, i: (i, j)),
        out_shape=jax.ShapeDtypeStruct((T, NCOL), BF16),
        compiler_params=pltpu.CompilerParams(vmem_limit_bytes=VMEM_LIMIT),
    )(h, w_in_full)


def _bucket_maps():
    a = np.arange(QB)[:, None]
    b = np.arange(2 * QB)[None, :]
    steps = a + QB - b
    maps = []
    for dil in DILATIONS:
        dist = np.maximum(steps, 0) * dil
        nf = np.maximum(dist, 1).astype(np.float32)
        large = 16 + (np.log(nf / np.float32(16)) / np.float32(math.log(128.0))
                      * np.float32(16)).astype(np.int32)
        large = np.minimum(large, N_BUCKETS - 1)
        maps.append(np.where(dist < 16, dist, large).astype(np.int32))
    band = (steps >= 0) & (steps <= N_STEPS)
    first = band & (b >= QB)
    masks = np.stack([first, band]).astype(np.int32)
    return np.stack(maps), masks


def _bias_expand(rel_bias, buckets, masks):
    def body(tab_ref, bk_ref, mk_ref, o_ref):
        for g in range(3):
            bk = bk_ref[g]
            for h in range(4):
                col = 4 * g + h
                val = jnp.zeros((QB, 2 * QB), F32)
                for k in range(N_BUCKETS):
                    val = jnp.where(bk == k, tab_ref[k, col], val)
                o_ref[g, 0, h] = jnp.where(mk_ref[0] != 0, val, NEG_INF)
                o_ref[g, 1, h] = jnp.where(mk_ref[1] != 0, val, NEG_INF)

    return pl.pallas_call(
        body, name="bias_expand",
        in_specs=[pl.BlockSpec(memory_space=pltpu.SMEM),
                  pl.BlockSpec(memory_space=pltpu.VMEM),
                  pl.BlockSpec(memory_space=pltpu.VMEM)],
        out_shape=jax.ShapeDtypeStruct((3, 2, 4, QB, 2 * QB), F32),
    )(rel_bias, buckets, masks)


def _bias_grad(ds1, ds2, ds3, buckets):
    def body(d1_ref, d2_ref, d3_ref, bk_ref, o_ref):
        for g, d_ref in enumerate((d1_ref, d2_ref, d3_ref)):
            bk = bk_ref[g]
            for h in range(4):
                dv = d_ref[h]
                for k in range(N_BUCKETS):
                    o_ref[k, 4 * g + h] = jnp.sum(jnp.where(bk == k, dv, 0.0))

    return pl.pallas_call(
        body, name="bias_grad",
        in_specs=[pl.BlockSpec(memory_space=pltpu.VMEM)] * 4,
        out_specs=pl.BlockSpec(memory_space=pltpu.SMEM),
        out_shape=jax.ShapeDtypeStruct((N_BUCKETS, N_HEADS), F32),
    )(ds1, ds2, ds3, buckets)


def _attn_fwd(proj, bias, g):
    dil = DILATIONS[g]
    sub = S // dil
    nb = sub // QB
    pv = proj.reshape(BL * sub, dil * NCOL)

    def body(q_ref, kp_ref, kc_ref, vp_ref, vc_ref, b_ref, o_ref, l_ref):
        lane = lax.broadcasted_iota(jnp.int32, (QB, 128), 1)
        lse_all = jnp.zeros((QB, 128), F32)
        for h in range(4):
            sl = slice(h * HD, (h + 1) * HD)
            q = q_ref[:, sl]
            s1 = _dot_nt(q, kp_ref[:, sl]) * SCALE + b_ref[h, :, :QB]
            s2 = _dot_nt(q, kc_ref[:, sl]) * SCALE + b_ref[h, :, QB:]
            m = jnp.maximum(jnp.max(s1, axis=1, keepdims=True), jnp.max(s2, axis=1, keepdims=True))
            p1 = jnp.exp(s1 - m)
            p2 = jnp.exp(s2 - m)
            l = jnp.sum(p1, axis=1, keepdims=True) + jnp.sum(p2, axis=1, keepdims=True)
            o = _dot(p1.astype(BF16), vp_ref[:, sl]) + _dot(p2.astype(BF16), vc_ref[:, sl])
            o_ref[:, sl] = o / l
            lse_all = jnp.where(lane == h, m + jnp.log(l), lse_all)
        l_ref[...] = lse_all

    def row(b, n):
        return b * nb + n

    def prev(b, n):
        return b * nb + jnp.maximum(n - 1, 0)

    in_specs = [
        pl.BlockSpec((QB, GW), lambda b, r, n: (row(b, n), r * NCB + CB_Q + g)),
        pl.BlockSpec((QB, GW), lambda b, r, n: (prev(b, n), r * NCB + CB_K + g)),
        pl.BlockSpec((QB, GW), lambda b, r, n: (row(b, n), r * NCB + CB_K + g)),
        pl.BlockSpec((QB, GW), lambda b, r, n: (prev(b, n), r * NCB + CB_V + g)),
        pl.BlockSpec((QB, GW), lambda b, r, n: (row(b, n), r * NCB + CB_V + g)),
        pl.BlockSpec((None, None, 4, QB, 2 * QB), lambda b, r, n: (g, jnp.minimum(n, 1), 0, 0, 0)),
    ]
    o, lse = pl.pallas_call(
        body, name=f"attn_fwd{g}",
        grid=(BL, dil, nb),
        in_specs=in_specs,
        out_specs=(pl.BlockSpec((QB, GW), lambda b, r, n: (row(b, n), r)),
                   pl.BlockSpec((QB, 128), lambda b, r, n: (row(b, n), r))),
        out_shape=(jax.ShapeDtypeStruct((BL * sub, dil * GW), F32),
                   jax.ShapeDtypeStruct((BL * sub, dil * 128), F32)),
    )(pv, pv, pv, pv, pv, bias)
    return o.reshape(T, GW), lse.reshape(T, 128)


def _attn_bwd(proj, d_out, stats, bias, dqkv, g):
    dil = DILATIONS[g]
    sub = S // dil
    nb = sub // QB
    pv = proj.reshape(BL * sub, dil * NCOL)
    dov = d_out.reshape(BL * sub, dil * GW)
    stv = stats.reshape(BL * sub, dil * 128)

    def body(*refs):
        if dqkv is None:
            (q_ref, kp_ref, kc_ref, vp_ref, vc_ref, do_ref, st_ref, b_ref,
             out_ref, db_ref, cq, ck, cv) = refs
        else:
            (q_ref, kp_ref, kc_ref, vp_ref, vc_ref, do_ref, st_ref, b_ref, _,
             out_ref, db_ref, cq, ck, cv) = refs
        b, r, n = pl.program_id(0), pl.program_id(1), pl.program_id(2)

        @pl.when((b == 0) & (r == 0) & (n == 0))
        def _():
            db_ref[...] = jnp.zeros_like(db_ref)

        @pl.when(n == nb)
        def _():
            out_ref[:, 0:GW] = cq[...].astype(BF16)
            out_ref[:, GW:2 * GW] = ck[...].astype(BF16)
            out_ref[:, 2 * GW:3 * GW] = cv[...].astype(BF16)

        @pl.when(n < nb)
        def _():
            for h in range(4):
                sl = slice(h * HD, (h + 1) * HD)
                q, kp, kc = q_ref[:, sl], kp_ref[:, sl], kc_ref[:, sl]
                vp, vc, do = vp_ref[:, sl], vc_ref[:, sl], do_ref[:, sl]
                lse = st_ref[:, h:h + 1]
                delta = st_ref[:, 4 + h:5 + h]
                s1 = _dot_nt(q, kp) * SCALE + b_ref[h, :, :QB]
                s2 = _dot_nt(q, kc) * SCALE + b_ref[h, :, QB:]
                p1 = jnp.exp(s1 - lse)
                p2 = jnp.exp(s2 - lse)
                ds1 = p1 * (_dot_nt(do, vp) - delta)
                ds2 = p2 * (_dot_nt(do, vc) - delta)
                db_ref[h, :, :QB] += ds1
                db_ref[h, :, QB:] += ds2
                ds1b, ds2b = ds1.astype(BF16), ds2.astype(BF16)
                p1b, p2b = p1.astype(BF16), p2.astype(BF16)
                dq = (_dot(ds1b, kp) + _dot(ds2b, kc)) * SCALE
                dk_prev = _dot_tn(ds1b, q) * SCALE
                dk_cur = _dot_tn(ds2b, q) * SCALE
                dv_prev = _dot_tn(p1b, do)
                dv_cur = _dot_tn(p2b, do)

                @pl.when(n > 0)
                def _():
                    out_ref[:, sl] = cq[:, sl].astype(BF16)
                    out_ref[:, GW + h * HD:GW + (h + 1) * HD] = (ck[:, sl] + dk_prev).astype(BF16)
                    out_ref[:, 2 * GW + h * HD:2 * GW + (h + 1) * HD] = (cv[:, sl] + dv_prev).astype(BF16)

                cq[:, sl] = dq
                ck[:, sl] = dk_cur
                cv[:, sl] = dv_cur

    def row(b, n):
        return b * nb + jnp.minimum(n, nb - 1)

    def prev(b, n):
        return b * nb + jnp.maximum(jnp.minimum(n, nb - 1) - 1, 0)

    def done(b, n):
        return b * nb + jnp.maximum(n - 1, 0)

    in_specs = [
        pl.BlockSpec((QB, GW), lambda b, r, n: (row(b, n), r * NCB + CB_Q + g)),
        pl.BlockSpec((QB, GW), lambda b, r, n: (prev(b, n), r * NCB + CB_K + g)),
        pl.BlockSpec((QB, GW), lambda b, r, n: (row(b, n), r * NCB + CB_K + g)),
        pl.BlockSpec((QB, GW), lambda b, r, n: (prev(b, n), r * NCB + CB_V + g)),
        pl.BlockSpec((QB, GW), lambda b, r, n: (row(b, n), r * NCB + CB_V + g)),
        pl.BlockSpec((QB, GW), lambda b, r, n: (row(b, n), r)),
        pl.BlockSpec((QB, 128), lambda b, r, n: (row(b, n), r)),
        pl.BlockSpec((None, None, 4, QB, 2 * QB),
                     lambda b, r, n: (g, jnp.minimum(jnp.minimum(n, nb - 1), 1), 0, 0, 0)),
    ]
    args = [pv, pv, pv, pv, pv, dov, stv, bias]
    aliases = {}
    if dqkv is not None:
        in_specs.append(pl.BlockSpec(memory_space=pl.ANY))
        args.append(dqkv.reshape(BL * sub, dil * 3 * 3 * GW))
        aliases = {8: 0}
    out, dbias = pl.pallas_call(
        body, name=f"attn_bwd{g}",
        grid=(BL, dil, nb + 1),
        in_specs=in_specs,
        out_specs=(pl.BlockSpec((QB, 3 * GW), lambda b, r, n: (done(b, n), r * 3 + g)),
                   pl.BlockSpec((4, QB, 2 * QB), lambda b, r, n: (0, 0, 0))),
        out_shape=(jax.ShapeDtypeStruct((BL * sub, dil * 9 * GW), BF16),
                   jax.ShapeDtypeStruct((4, QB, 2 * QB), F32)),
        scratch_shapes=[pltpu.VMEM((QB, GW), F32)] * 3,
        input_output_aliases=aliases,
    )(*args)
    return out.reshape(T, 9 * GW), dbias


def _tail(x2, tgt2, mod3, o_g, lse_g, proj, w_ao, w_co, w_o, conv_w, conv_b, ln_g, ln_b):
    tm = 256
    per_seq = S // tm
    halo = 16

    def body(x_ref, t_ref, mod_ref, o1_ref, o2_ref, o3_ref, l1_ref, l2_ref, l3_ref,
             ga_ref, u_ref, bg_ref, cg_ref, gc_ref, ma_ref, mc_ref, up_ref, cp_ref,
             wao_ref, wco_ref, wo_ref, cw_ref, cb_ref, lg_ref, lb_ref,
             dtail_ref, dyc_ref, do_ref, st_ref, dxd_ref,
             mg_ref, dy_ref, ain_ref, dao_ref, sin_ref, dso_ref, vec_ref):
        i = pl.program_id(0)
        bidx = i // per_seq
        first = (i % per_seq) == 0

        @pl.when(i == 0)
        def _():
            vec_ref[...] = jnp.zeros_like(vec_ref)

        l1, l2, l3 = l1_ref[...], l2_ref[...], l3_ref[...]
        mx = jnp.maximum(jnp.maximum(l1, l2), l3)
        e1, e2, e3 = jnp.exp(l1 - mx), jnp.exp(l2 - mx), jnp.exp(l3 - mx)
        esum = e1 + e2 + e3
        lse_tot = mx + jnp.log(esum)
        w1, w2, w3 = e1 / esum, e2 / esum, e3 / esum

        def per_head(wv):
            return jnp.concatenate([jnp.broadcast_to(wv[:, h:h + 1], (tm, HD)) for h in range(4)], axis=1)

        o = per_head(w1) * o1_ref[...] + per_head(w2) * o2_ref[...] + per_head(w3) * o3_ref[...]

        ga = ga_ref[...].astype(F32)
        sig_ga = _sigmoid(ga)
        silu_ga = ga * sig_ga
        a_in = (o * silu_ga).astype(BF16)
        a_out = _dot(a_in, wao_ref[...])

        u = u_ref[...].astype(F32)
        cg = cg_ref[...].astype(F32)
        z = cg * u
        zp = cp_ref[...].astype(F32) * up_ref[...].astype(F32)
        zp = jnp.where(first, 0.0, zp)
        zcat = jnp.concatenate([zp, z], axis=0)
        z1 = pltpu.roll(zcat, 1, 0)[halo:]
        z2 = pltpu.roll(zcat, 2, 0)[halo:]
        y_conv = cw_ref[0:1, :] * z2 + cw_ref[1:2, :] * z1 + cw_ref[2:3, :] * z + cb_ref[...]
        gc = gc_ref[...].astype(F32)
        sig_gc = _sigmoid(gc)
        silu_gc = gc * sig_gc
        bg = bg_ref[...].astype(F32)
        s_in = (bg * y_conv * silu_gc).astype(BF16)
        s_out = _dot(s_in, wco_ref[...])

        sa = _sigmoid(ma_ref[...].astype(F32))
        sc = _sigmoid(mc_ref[...].astype(F32))
        merged = (sa * a_out + sc * s_out).astype(BF16)
        y = _dot(merged, wo_ref[...])
        gate1 = 1.0 + mod_ref[0, 2:3, :]
        xv = x_ref[...]
        resid = ALPHA * xv + gate1 * y
        mu = jnp.mean(resid, axis=1, keepdims=True)
        xc = resid - mu
        var = jnp.mean(xc * xc, axis=1, keepdims=True)
        rstd = lax.rsqrt(var + LN_EPS)
        xhat = xc * rstd
        lg = lg_ref[...]
        err = xhat * lg + lb_ref[...] - t_ref[...]
        vec_ref[3:4, :] += (0.5 / D) * jnp.sum(err * err, axis=0, keepdims=True)

        dout = err * (1.0 / D)
        vec_ref[1:2, :] += jnp.sum(dout * xhat, axis=0, keepdims=True)
        vec_ref[2:3, :] += jnp.sum(dout, axis=0, keepdims=True)
        dxh = dout * lg
        dres = rstd * (dxh - jnp.mean(dxh, axis=1, keepdims=True)
                       - xhat * jnp.mean(dxh * xhat, axis=1, keepdims=True))
        dxd_ref[...] = ALPHA * dres
        dgate = jnp.sum(dres * y, axis=0, keepdims=True)
        vec_ref[4:5, :] += jnp.where(bidx == 0, dgate, 0.0)
        vec_ref[5:6, :] += jnp.where(bidx == 1, dgate, 0.0)
        dy = (dres * gate1).astype(BF16)

        dmerged = _dot_nt(dy, wo_ref[...])
        da_out = (dmerged * sa).astype(BF16)
        ds_out = (dmerged * sc).astype(BF16)
        dtail_ref[:, 3584:4608] = (dmerged * s_out * sc * (1.0 - sc)).astype(BF16)
        dtail_ref[:, 2560:3584] = (dmerged * a_out * sa * (1.0 - sa)).astype(BF16)
        da_in = _dot_nt(da_out, wao_ref[...])
        ds_in = _dot_nt(ds_out, wco_ref[...])

        d_o = da_in * silu_ga
        do_ref[...] = d_o.astype(BF16)
        dtail_ref[:, 0:512] = (da_in * o * (sig_ga * (1.0 + ga * (1.0 - sig_ga)))).astype(BF16)
        lane = lax.broadcasted_iota(jnp.int32, (tm, 128), 1)
        stats = lse_tot
        od = o * d_o
        for h in range(4):
            delta = jnp.sum(od[:, h * HD:(h + 1) * HD], axis=1, keepdims=True)
            stats = jnp.where(lane == 4 + h, delta, stats)
        st_ref[...] = stats

        dtail_ref[:, 512:1536] = (ds_in * y_conv * silu_gc).astype(BF16)
        dyc = ds_in * bg * silu_gc
        dyc_ref[...] = dyc
        vec_ref[0:1, :] += jnp.sum(dyc, axis=0, keepdims=True)
        dtail_ref[:, 1536:2560] = (ds_in * bg * y_conv * (sig_gc * (1.0 + gc * (1.0 - sig_gc)))).astype(BF16)

        mg_ref[...] = merged
        dy_ref[...] = dy
        ain_ref[...] = a_in
        dao_ref[...] = da_out
        sin_ref[...] = s_in
        dso_ref[...] = ds_out

    def tile(width, cblk=0):
        return pl.BlockSpec((tm, width), lambda i: (i, cblk))

    def whole(shape):
        return pl.BlockSpec(shape, lambda i: tuple(0 for _ in shape))

    prev_rows = lambda i: (jnp.maximum(i * (tm // halo) - 1, 0),)
    in_specs = [
        tile(D), tile(D), pl.BlockSpec((1, 3, D), lambda i: (i // per_seq, 0, 0)),
        tile(GW), tile(GW), tile(GW), tile(128), tile(128), tile(128),
        tile(GW, CB_GA), tile(D, KB_U), tile(D, KB_BG), tile(D, KB_CG), tile(D, KB_GC),
        tile(D, KB_MA), tile(D, KB_MC),
        pl.BlockSpec((halo, D), lambda i: (*prev_rows(i), KB_U)),
        pl.BlockSpec((halo, D), lambda i: (*prev_rows(i), KB_CG)),
        whole((GW, D)), whole((D, D)), whole((D, D)),
        whole((3, D)), whole((1, D)), whole((1, D)), whole((1, D)),
    ]
    out_specs = (
        tile(9 * CB), tile(D), tile(GW), tile(128), tile(D),
        tile(D), tile(D), tile(GW), tile(D), tile(D), tile(D),
        pl.BlockSpec((8, D), lambda i: (0, 0)),
    )
    out_shape = (
        jax.ShapeDtypeStruct((T, 9 * CB), BF16),
        jax.ShapeDtypeStruct((T, D), F32),
        jax.ShapeDtypeStruct((T, GW), BF16),
        jax.ShapeDtypeStruct((T, 128), F32),
        jax.ShapeDtypeStruct((T, D), F32),
        jax.ShapeDtypeStruct((T, D), BF16),
        jax.ShapeDtypeStruct((T, D), BF16),
        jax.ShapeDtypeStruct((T, GW), BF16),
        jax.ShapeDtypeStruct((T, D), BF16),
        jax.ShapeDtypeStruct((T, D), BF16),
        jax.ShapeDtypeStruct((T, D), BF16),
        jax.ShapeDtypeStruct((8, D), F32),
    )
    return pl.pallas_call(
        body, name="tail",
        grid=(T // tm,),
        in_specs=in_specs, out_specs=out_specs, out_shape=out_shape,
        compiler_params=pltpu.CompilerParams(vmem_limit_bytes=VMEM_LIMIT),
    )(x2, tgt2, mod3, *o_g, *lse_g, proj, proj, proj, proj, proj, proj, proj, proj, proj,
      w_ao, w_co, w_o, conv_w, conv_b, ln_g, ln_b)


def _conv_bwd(dyc, proj, conv_w):
    tm = 512
    per_seq = S // tm
    halo = 16

    def body(d_ref, dn_ref, u_ref, c_ref, up_ref, cp_ref, cw_ref, o_ref, g_ref):
        i = pl.program_id(0)
        first = (i % per_seq) == 0
        last = (i % per_seq) == per_seq - 1

        @pl.when(i == 0)
        def _():
            g_ref[...] = jnp.zeros_like(g_ref)

        d = d_ref[...]
        dn = jnp.where(last, 0.0, dn_ref[...])
        dcat = jnp.concatenate([d, dn], axis=0)
        d1 = pltpu.roll(dcat, tm + 8 - 1, 0)[:tm]
        d2 = pltpu.roll(dcat, tm + 8 - 2, 0)[:tm]
        dz = cw_ref[2:3, :] * d + cw_ref[1:2, :] * d1 + cw_ref[0:1, :] * d2
        u = u_ref[...].astype(F32)
        cg = c_ref[...].astype(F32)
        o_ref[:, 0:D] = (dz * cg).astype(BF16)
        o_ref[:, D:2 * D] = (dz * u).astype(BF16)

        z = cg * u
        zp = jnp.where(first, 0.0, cp_ref[...].astype(F32) * up_ref[...].astype(F32))
        zcat = jnp.concatenate([zp, z], axis=0)
        z1 = pltpu.roll(zcat, 1, 0)[halo:]
        z2 = pltpu.roll(zcat, 2, 0)[halo:]
        g_ref[0:1, :] += jnp.sum(d * z2, axis=0, keepdims=True)
        g_ref[1:2, :] += jnp.sum(d * z1, axis=0, keepdims=True)
        g_ref[2:3, :] += jnp.sum(d * z, axis=0, keepdims=True)

    n_tiles = T // tm
    prev_rows = lambda i: jnp.maximum(i * (tm // halo) - 1, 0)
    next_rows = lambda i: jnp.minimum((i + 1) * (tm // 8), T // 8 - 1)
    return pl.pallas_call(
        body, name="conv_bwd",
        grid=(n_tiles,),
        in_specs=[pl.BlockSpec((tm, D), lambda i: (i, 0)),
                  pl.BlockSpec((8, D), lambda i: (next_rows(i), 0)),
                  pl.BlockSpec((tm, D), lambda i: (i, KB_U)),
                  pl.BlockSpec((tm, D), lambda i: (i, KB_CG)),
                  pl.BlockSpec((halo, D), lambda i: (prev_rows(i), KB_U)),
                  pl.BlockSpec((halo, D), lambda i: (prev_rows(i), KB_CG)),
                  pl.BlockSpec((3, D), lambda i: (0, 0))],
        out_specs=(pl.BlockSpec((tm, 2 * D), lambda i: (i, 0)),
                   pl.BlockSpec((8, D), lambda i: (0, 0))),
        out_shape=(jax.ShapeDtypeStruct((T, 2 * D), BF16),
                   jax.ShapeDtypeStruct((8, D), F32)),
        compiler_params=pltpu.CompilerParams(vmem_limit_bytes=VMEM_LIMIT),
    )(dyc, dyc, proj, proj, proj, proj, conv_w)


def _d_specs(tile_rows, row_of, jj_of):
    return [
        pl.BlockSpec((tile_rows, CB), lambda *ids: (row_of(*ids), jnp.minimum(jj_of(*ids), 8))),
        pl.BlockSpec((tile_rows, CB), lambda *ids: (row_of(*ids), jnp.clip(jj_of(*ids) - 9, 0, 3))),
        pl.BlockSpec((tile_rows, CB), lambda *ids: (row_of(*ids), jnp.clip(jj_of(*ids) - 13, 0, 8))),
    ]


def _dh_dx(dqkv, dconv, dtail, w_in_full, x2, dxd, mod3):
    tm = 512
    per_seq = S // tm

    def body(dq_ref, dc_ref, dt_ref, w_ref, x_ref, dxd_ref, mod_ref, gx_ref, vec_ref, acc):
        i, jj = pl.program_id(0), pl.program_id(1)

        @pl.when((i == 0) & (jj == 0))
        def _():
            vec_ref[...] = jnp.zeros_like(vec_ref)

        @pl.when(jj == 0)
        def _():
            acc[...] = jnp.zeros_like(acc)

        @pl.when(jj < 9)
        def _():
            acc[...] += _dot_nt(dq_ref[...], w_ref[...])

        @pl.when((jj >= 9) & (jj < 13))
        def _():
            acc[...] += _dot_nt(dc_ref[...], w_ref[...])

        @pl.when(jj >= 13)
        def _():
            acc[...] += _dot_nt(dt_ref[...], w_ref[...])

        @pl.when(jj == NCB - 1)
        def _():
            dh = acc[...]
            bidx = i // per_seq
            gx_ref[...] = dxd_ref[...] + dh * (1.0 + mod_ref[0, 1:2, :])
            dshift = jnp.sum(dh, axis=0, keepdims=True)
            dscale = jnp.sum(dh * x_ref[...], axis=0, keepdims=True)
            vec_ref[0:1, :] += jnp.where(bidx == 0, dshift, 0.0)
            vec_ref[1:2, :] += jnp.where(bidx == 1, dshift, 0.0)
            vec_ref[2:3, :] += jnp.where(bidx == 0, dscale, 0.0)
            vec_ref[3:4, :] += jnp.where(bidx == 1, dscale, 0.0)

    return pl.pallas_call(
        body, name="dh_dx",
        grid=(T // tm, NCB),
        in_specs=_d_specs(tm, lambda i, jj: i, lambda i, jj: jj) + [
            pl.BlockSpec((D, CB), lambda i, jj: (0, _perm(jj))),
            pl.BlockSpec((tm, D), lambda i, jj: (i, 0)),
            pl.BlockSpec((tm, D), lambda i, jj: (i, 0)),
            pl.BlockSpec((1, 3, D), lambda i, jj: (i // per_seq, 0, 0))],
        out_specs=(pl.BlockSpec((tm, D), lambda i, jj: (i, 0)),
                   pl.BlockSpec((8, D), lambda i, jj: (0, 0))),
        out_shape=(jax.ShapeDtypeStruct((T, D), F32), jax.ShapeDtypeStruct((8, D), F32)),
        scratch_shapes=[pltpu.VMEM((tm, D), F32)],
        compiler_params=pltpu.CompilerParams(vmem_limit_bytes=VMEM_LIMIT),
    )(dqkv, dconv, dtail, w_in_full, x2, dxd, mod3)


def _gw_in(h, dqkv, dconv, dtail):
    tk = 1024

    def body(h_ref, dq_ref, dc_ref, dt_ref, o_ref):
        jj, k = pl.program_id(0), pl.program_id(1)

        @pl.when(k == 0)
        def _():
            o_ref[...] = jnp.zeros_like(o_ref)

        @pl.when(jj < 9)
        def _():
            o_ref[...] += _dot_tn(h_ref[...], dq_ref[...])

        @pl.when((jj >= 9) & (jj < 13))
        def _():
            o_ref[...] += _dot_tn(h_ref[...], dc_ref[...])

        @pl.when(jj >= 13)
        def _():
            o_ref[...] += _dot_tn(h_ref[...], dt_ref[...])

    return pl.pallas_call(
        body, name="gw_in",
        grid=(NCB, T // tk),
        in_specs=[pl.BlockSpec((tk, D), lambda jj, k: (k, 0))]
                 + _d_specs(tk, lambda jj, k: k, lambda jj, k: jj),
        out_specs=pl.BlockSpec((D, CB), lambda jj, k: (0, _perm(jj))),
        out_shape=jax.ShapeDtypeStruct((D, NCOL), F32),
        compiler_params=pltpu.CompilerParams(vmem_limit_bytes=VMEM_LIMIT),
    )(h, dqkv, dconv, dtail)


def _mm_tn(a, b, tn, blocks_leading, name):
    kk, m = a.shape
    n = b.shape[1]
    tk = 1024

    def body(a_ref, b_ref, o_ref):
        @pl.when(pl.program_id(1) == 0)
        def _():
            o_ref[...] = jnp.zeros_like(o_ref)

        o_ref[...] += _dot_tn(a_ref[...], b_ref[...])

    if blocks_leading:
        out_spec = pl.BlockSpec((None, m, tn), lambda j, k: (j, 0, 0))
        out_shape = jax.ShapeDtypeStruct((n // tn, m, tn), F32)
    else:
        out_spec = pl.BlockSpec((m, tn), lambda j, k: (0, j))
        out_shape = jax.ShapeDtypeStruct((m, n), F32)
    return pl.pallas_call(
        body, name=name,
        grid=(n // tn, kk // tk),
        in_specs=[pl.BlockSpec((tk, m), lambda j, k: (k, 0)),
                  pl.BlockSpec((tk, tn), lambda j, k: (k, j))],
        out_specs=out_spec, out_shape=out_shape,
        compiler_params=pltpu.CompilerParams(vmem_limit_bytes=VMEM_LIMIT),
    )(a, b)


def _adamw(parts, w, m, v, name, row_tile=None):
    n_parts, rows, cols = parts.shape
    tr = rows if row_tile is None else row_tile
    c1 = 1.0 - ADAM_B1 ** ADAM_STEP
    c2 = 1.0 - ADAM_B2 ** ADAM_STEP

    def body(p_ref, w_ref, m_ref, v_ref, g_ref, d_ref, nm_ref, nv_ref):
        g = p_ref[0]
        for s in range(1, n_parts):
            g = g + p_ref[s]
        nm = ADAM_B1 * m_ref[...] + (1.0 - ADAM_B1) * g
        nv = ADAM_B2 * v_ref[...] + (1.0 - ADAM_B2) * (g * g)
        m_hat = nm / c1
        v_hat = nv / c2
        g_ref[...] = g
        d_ref[...] = -ADAM_LR * (m_hat / (jnp.sqrt(v_hat) + ADAM_EPS) + ADAM_WD * w_ref[...])
        nm_ref[...] = nm
        nv_ref[...] = nv

    blk = pl.BlockSpec((tr, cols), lambda i: (i, 0))
    shp = jax.ShapeDtypeStruct((rows, cols), F32)
    return pl.pallas_call(
        body, name=name,
        grid=(rows // tr,),
        in_specs=[pl.BlockSpec((n_parts, tr, cols), lambda i: (0, i, 0)), blk, blk, blk],
        out_specs=(blk, blk, blk, blk),
        out_shape=(shp, shp, shp, shp),
        compiler_params=pltpu.CompilerParams(vmem_limit_bytes=VMEM_LIMIT),
    )(parts, w, m, v)


def _loss_sum(rows):
    def body(r_ref, o_ref):
        o_ref[...] = jnp.sum(jnp.sum(r_ref[...], axis=0, keepdims=True), axis=1, keepdims=True)

    return pl.pallas_call(body, name="loss_sum", out_shape=jax.ShapeDtypeStruct((1, 1), F32))(rows)


def _local_step(x2, tgt2, mod3, w_in_full, w_ao, w_co, w_o, conv_w, conv_b, rel_bias, ln_g, ln_b):
    buckets_np, masks_np = _bucket_maps()
    buckets, masks = jnp.asarray(buckets_np), jnp.asarray(masks_np)

    h = _prep_h(x2, mod3)
    proj = _proj(h, w_in_full)
    bias = _bias_expand(rel_bias, buckets, masks)
    fwd = [_attn_fwd(proj, bias, g) for g in range(3)]
    o_g = [f[0] for f in fwd]
    lse_g = [f[1] for f in fwd]

    (dtail, dyc, d_o, stats, dxd, merged, dy, a_in, da_out, s_in, ds_out, tail_vec) = _tail(
        x2, tgt2, mod3, o_g, lse_g, proj, w_ao, w_co, w_o, conv_w, conv_b, ln_g, ln_b)

    dqkv = None
    dbias = []
    for g in range(3):
        dqkv, db = _attn_bwd(proj, d_o, stats, bias, dqkv, g)
        dbias.append(db)
    g_rel_bias = _bias_grad(*dbias, buckets)
    dconv, conv_vec = _conv_bwd(dyc, proj, conv_w)

    grad_x, mod_vec = _dh_dx(dqkv, dconv, dtail, w_in_full, x2, dxd, mod3)
    gw_in = _gw_in(h, dqkv, dconv, dtail)
    gw_o = _mm_tn(merged, dy, 512, False, "gw_o")
    gw_co = _mm_tn(s_in, ds_out, 512, False, "gw_conv_out")
    gw_ao = _mm_tn(a_in, da_out, 128, True, "gw_attn_out")
    return grad_x, gw_in, gw_ao, gw_co, gw_o, conv_vec, g_rel_bias, tail_vec, mod_vec


def kernel(x, c, w_ada, b_ada, w_in, conv_w, conv_b, rel_bias, w_attn_out, w_conv_out, w_o, ln_g, ln_b, loss_target, m_w_ada, m_b_ada, m_w_in, m_conv_w, m_conv_b, m_rel_bias, m_w_attn_out, m_w_conv_out, m_w_o, m_ln_g, m_ln_b, v_w_ada, v_b_ada, v_w_in, v_conv_w, v_conv_b, v_rel_bias, v_w_attn_out, v_w_conv_out, v_w_o, v_ln_g, v_ln_b):
    me = _my_index()
    x2 = x.reshape(T, D)
    tgt2 = loss_target.reshape(T, D)

    w_in_full, w_ao_g, w_co_g, w_o_g, conv_w_g, c_g = _all_gather(
        [w_in[0].astype(BF16), w_attn_out[0].astype(BF16), w_conv_out[0].astype(BF16),
         w_o[0].astype(BF16), conv_w[0], c],
        [jax.ShapeDtypeStruct((D, NCOL), BF16),
         jax.ShapeDtypeStruct((N_DEV, GW, D // N_DEV), BF16),
         jax.ShapeDtypeStruct((N_DEV, D // N_DEV, D), BF16),
         jax.ShapeDtypeStruct((N_DEV, D // N_DEV, D), BF16),
         jax.ShapeDtypeStruct((N_DEV, 3, D // N_DEV), F32),
         jax.ShapeDtypeStruct((N_DEV, BL, D), F32)],
        [_slot_columns(SHARD)] + [_slot_leading] * 5,
        "gather_weights")
    w_ao_full = jnp.transpose(w_ao_g, (1, 0, 2)).reshape(GW, D)
    w_co_full = w_co_g.reshape(D, D)
    w_o_full = w_o_g.reshape(D, D)
    conv_w_full = jnp.transpose(conv_w_g, (1, 0, 2)).reshape(3, D)
    c_all = c_g.reshape(N_DEV * BL, D)

    b_cols = lax.dynamic_slice(b_ada, (0, me * ADA_SHARD), (1, ADA_SHARD))
    mod_cols = _ada_fwd(c_all, w_ada[0], b_cols)
    (mod_g,) = _all_gather([mod_cols], [jax.ShapeDtypeStruct((N_DEV, N_DEV * BL, ADA_SHARD), F32)],
                           [_slot_leading], "gather_mod")
    mod_all = jnp.transpose(mod_g, (1, 0, 2)).reshape(N_DEV * BL, 3 * D)
    mod3 = lax.dynamic_slice(mod_all, (me * BL, 0), (BL, 3 * D)).reshape(BL, 3, D)

    (grad_x, gw_in, gw_ao, gw_co, gw_o, conv_vec, g_rel_bias, tail_vec, mod_vec) = _local_step(
        x2, tgt2, mod3, w_in_full, w_ao_full, w_co_full, w_o_full,
        conv_w_full, conv_b, rel_bias, ln_g, ln_b)

    g_conv_w_blocks = jnp.transpose(conv_vec[0:3].reshape(3, N_DEV, D // N_DEV), (1, 0, 2))
    r_in, r_ao, r_co, r_o, r_cw = _all_to_all(
        [gw_in, gw_ao, gw_co.reshape(N_DEV, D // N_DEV, D), gw_o.reshape(N_DEV, D // N_DEV, D),
         g_conv_w_blocks],
        [jax.ShapeDtypeStruct((N_DEV, D, SHARD), F32),
         jax.ShapeDtypeStruct((N_DEV, GW, D // N_DEV), F32),
         jax.ShapeDtypeStruct((N_DEV, D // N_DEV, D), F32),
         jax.ShapeDtypeStruct((N_DEV, D // N_DEV, D), F32),
         jax.ShapeDtypeStruct((N_DEV, 3, D // N_DEV), F32)],
        [_slot_columns(SHARD)] + [_slot_leading] * 4,
        "scatter_grads")

    small = jnp.concatenate([
        tail_vec[0:4],
        jnp.pad(g_rel_bias.reshape(1, N_BUCKETS * N_HEADS), ((0, 0), (0, D - N_BUCKETS * N_HEADS))),
        jnp.zeros((3, D), F32)], axis=0)
    dmod = jnp.concatenate([mod_vec[0:2], mod_vec[2:4], tail_vec[4:6]], axis=1)
    small_g, dmod_g = _all_gather(
        [small, dmod],
        [jax.ShapeDtypeStruct((N_DEV, 8, D), F32), jax.ShapeDtypeStruct((N_DEV, BL, 3 * D), F32)],
        [_slot_leading] * 2, "gather_small")
    dmod_all = dmod_g.reshape(N_DEV * BL, 3 * D)
    loss = _loss_sum(small_g[:, 3, :]).reshape(())
    g_w_ada = _ada_bwd(jnp.transpose(c_all), lax.dynamic_slice(dmod_all, (0, me * ADA_SHARD),
                                                               (N_DEV * BL, ADA_SHARD)))

    def upd(parts, w, m, v, name, row_tile=None):
        shape = w.shape
        w2, m2, v2 = (t.reshape(parts.shape[1:]) for t in (w, m, v))
        return tuple(t.reshape(shape) for t in _adamw(parts, w2, m2, v2, name, row_tile))

    res = {
        "w_ada": upd(g_w_ada[None], w_ada, m_w_ada, v_w_ada, "adam_w_ada", 256),
        "b_ada": upd(dmod_all[:, None, :], b_ada, m_b_ada, v_b_ada, "adam_b_ada"),
        "w_in": upd(r_in, w_in, m_w_in, v_w_in, "adam_w_in", 128),
        "conv_w": upd(r_cw, conv_w, m_conv_w, v_conv_w, "adam_conv_w"),
        "conv_b": upd(small_g[:, 0:1, :], conv_b, m_conv_b, v_conv_b, "adam_conv_b"),
        "rel_bias": upd(small_g[:, 4, :N_BUCKETS * N_HEADS].reshape(N_DEV, N_BUCKETS, N_HEADS),
                        rel_bias, m_rel_bias, v_rel_bias, "adam_rel_bias"),
        "w_attn_out": upd(r_ao, w_attn_out, m_w_attn_out, v_w_attn_out, "adam_w_attn_out"),
        "w_conv_out": upd(r_co, w_conv_out, m_w_conv_out, v_w_conv_out, "adam_w_conv_out"),
        "w_o": upd(r_o, w_o, m_w_o, v_w_o, "adam_w_o"),
        "ln_g": upd(small_g[:, 1:2, :], ln_g, m_ln_g, v_ln_g, "adam_ln_g"),
        "ln_b": upd(small_g[:, 2:3, :], ln_b, m_ln_b, v_ln_b, "adam_ln_b"),
    }
    order = ["w_ada", "b_ada", "w_in", "conv_w", "conv_b", "rel_bias", "w_attn_out", "w_conv_out",
             "w_o", "ln_g", "ln_b"]
    outs = [loss, grad_x.reshape(BL, S, D)]
    for k in range(4):
        outs += [res[name][k] for name in order]
    return tuple(outs)
```

```python
import functools
import math

import numpy as np
import jax
import jax.numpy as jnp
from jax import lax
from jax.experimental import pallas as pl
from jax.experimental.pallas import tpu as pltpu

F32 = jnp.float32
BF16 = jnp.bfloat16
MESH = pl.DeviceIdType.MESH

N_DEV = 8
D = 1024
S = 2048
BL = 2
T = BL * S
NCOL = 11264
SHARD = NCOL // N_DEV
CB = 512
NCB = NCOL // CB
HD = 128
GW = 512
QB = 128
DILATIONS = (1, 4, 16)
N_STEPS = 128
N_BUCKETS = 32
N_HEADS = 12
ALPHA = 2.0 ** 0.25
LN_EPS = 1e-5
NEG_INF = -1e30
SCALE = HD ** -0.5
ADA_SHARD = 3 * D // N_DEV

CB_Q, CB_K, CB_V, CB_GA = 0, 3, 6, 9
KB_U, KB_BG, KB_CG, KB_GC, KB_MA, KB_MC = 5, 6, 7, 8, 9, 10

ADAM_LR, ADAM_B1, ADAM_B2, ADAM_EPS, ADAM_WD, ADAM_STEP = 0.001, 0.9, 0.999, 1e-08, 0.01, 10

VMEM_LIMIT = 56 * 1024 * 1024


def _dot(a, b):
    return jnp.dot(a, b, preferred_element_type=F32)


def _dot_nt(a, b):
    return lax.dot_general(a, b, (((1,), (1,)), ((), ())), preferred_element_type=F32)


def _dot_tn(a, b):
    return lax.dot_general(a, b, (((0,), (0,)), ((), ())), preferred_element_type=F32)


def _sigmoid(v):
    return 1.0 / (1.0 + jnp.exp(-v))


def _perm(jj):
    return jnp.where(jj < 9, (jj % 3) * 3 + jj // 3,
           jnp.where(jj < 11, jj + 1,
           jnp.where(jj < 13, jj + 3,
           jnp.where(jj == 13, 9,
           jnp.where(jj < 16, jj - 2, jj)))))


def _my_index():
    return 4 * lax.axis_index("x") + 2 * lax.axis_index("y") + lax.axis_index("c")


def _slot_leading(ref, slot):
    return ref.at[slot]


def _slot_columns(width):
    def f(ref, slot):
        return ref.at[:, pl.ds(pl.multiple_of(slot * width, 128), width)]
    return f


def _all_gather(arrs, out_shapes, slot_fns, name):
    n = len(arrs)

    def body(*refs):
        ins, outs = refs[:n], refs[n:2 * n]
        send_sems, recv_sems, local_sems = refs[2 * n:]
        x, y, c = lax.axis_index("x"), lax.axis_index("y"), lax.axis_index("c")
        me, sibling = (x, y, c), (x, y, 1 - c)
        chips = [(1 - x, y), (x, 1 - y), (1 - x, 1 - y)]

        def blk(a, dev):
            return slot_fns[a](outs[a], 4 * dev[0] + 2 * dev[1] + dev[2])

        def copy(a, k, block, to, src=None):
            dst = blk(a, block)
            return pltpu.make_async_remote_copy(
                src_ref=dst if src is None else src, dst_ref=dst,
                send_sem=send_sems.at[a * 7 + k], recv_sem=recv_sems.at[a * 7 + k],
                device_id=to, device_id_type=MESH)

        mine = [pltpu.make_async_copy(ins[a], blk(a, me), local_sems.at[a]) for a in range(n)]
        for cp in mine:
            cp.start()
        first = []
        for a in range(n):
            first.append(copy(a, 0, me, sibling, src=ins[a]))
            first += [copy(a, 1 + j, me, (*chip, c), src=ins[a]) for j, chip in enumerate(chips)]
        for cp in first:
            cp.start()
        passed = []
        for j, chip in enumerate(chips):
            for a in range(n):
                copy(a, 1 + j, (*chip, c), me).wait_recv()
                fwd = copy(a, 4 + j, (*chip, c), sibling)
                fwd.start()
                passed.append(fwd)
        for a in range(n):
            copy(a, 0, sibling, me).wait_recv()
        for j, chip in enumerate(chips):
            for a in range(n):
                copy(a, 4 + j, (*chip, 1 - c), me).wait_recv()
        for cp in first + passed:
            cp.wait_send()
        for cp in mine:
            cp.wait()

    any_spec = pl.BlockSpec(memory_space=pl.ANY)
    return pl.pallas_call(
        body, name=name,
        out_shape=tuple(out_shapes),
        in_specs=[any_spec] * n,
        out_specs=tuple([any_spec] * n),
        scratch_shapes=[pltpu.SemaphoreType.DMA((7 * n,)), pltpu.SemaphoreType.DMA((7 * n,)),
                        pltpu.SemaphoreType.DMA((n,))],
    )(*arrs)


def _all_to_all(arrs, out_shapes, src_fns, name):
    n = len(arrs)

    def body(*refs):
        ins, outs = refs[:n], refs[n:2 * n]
        send_sems, recv_sems, local_sems = refs[2 * n:]
        x, y, c = lax.axis_index("x"), lax.axis_index("y"), lax.axis_index("c")
        my_slot = 4 * x + 2 * y + c

        def peer_of(k):
            dx, dy, dc = (k >> 2) & 1, (k >> 1) & 1, k & 1
            return ((1 - x) if dx else x, (1 - y) if dy else y, (1 - c) if dc else c)

        def copy(a, k):
            peer = peer_of(k)
            peer_slot = 4 * peer[0] + 2 * peer[1] + peer[2]
            return pltpu.make_async_remote_copy(
                src_ref=src_fns[a](ins[a], peer_slot), dst_ref=outs[a].at[my_slot],
                send_sem=send_sems.at[a * 7 + k - 1], recv_sem=recv_sems.at[a * 7 + k - 1],
                device_id=peer, device_id_type=MESH)

        def landed(a, k):
            peer = peer_of(k)
            peer_slot = 4 * peer[0] + 2 * peer[1] + peer[2]
            return pltpu.make_async_remote_copy(
                src_ref=src_fns[a](ins[a], my_slot), dst_ref=outs[a].at[peer_slot],
                send_sem=send_sems.at[a * 7 + k - 1], recv_sem=recv_sems.at[a * 7 + k - 1],
                device_id=peer, device_id_type=MESH)

        mine = [pltpu.make_async_copy(src_fns[a](ins[a], my_slot), outs[a].at[my_slot], local_sems.at[a])
                for a in range(n)]
        for cp in mine:
            cp.start()
        sends = [copy(a, k) for k in range(1, 8) for a in range(n)]
        for cp in sends:
            cp.start()
        for k in range(1, 8):
            for a in range(n):
                landed(a, k).wait_recv()
        for cp in sends:
            cp.wait_send()
        for cp in mine:
            cp.wait()

    any_spec = pl.BlockSpec(memory_space=pl.ANY)
    return pl.pallas_call(
        body, name=name,
        out_shape=tuple(out_shapes),
        in_specs=[any_spec] * n,
        out_specs=tuple([any_spec] * n),
        scratch_shapes=[pltpu.SemaphoreType.DMA((7 * n,)), pltpu.SemaphoreType.DMA((7 * n,)),
                        pltpu.SemaphoreType.DMA((n,))],
    )(*arrs)


def _ada_fwd(c_all, w_ada, b_cols):
    def body(c_ref, w_ref, b_ref, o_ref):
        cv = c_ref[...]
        sc = cv * _sigmoid(cv)
        o_ref[...] = jnp.dot(sc, w_ref[...], preferred_element_type=F32,
                             precision=lax.Precision.HIGHEST) + b_ref[...]

    return pl.pallas_call(
        body, name="ada_fwd",
        out_shape=jax.ShapeDtypeStruct((c_all.shape[0], w_ada.shape[1]), F32),
    )(c_all, w_ada, b_cols)


def _ada_bwd(c_all_t, dmod_cols):
    def body(c_ref, d_ref, o_ref):
        cv = c_ref[...]
        sc = cv * _sigmoid(cv)
        o_ref[...] = jnp.dot(sc, d_ref[...], preferred_element_type=F32,
                             precision=lax.Precision.HIGHEST)

    return pl.pallas_call(
        body, name="ada_bwd",
        out_shape=jax.ShapeDtypeStruct((c_all_t.shape[0], dmod_cols.shape[1]), F32),
    )(c_all_t, dmod_cols)


def _prep_h(x2, mod3):
    ts = 512
    per_seq = S // ts

    def body(x_ref, mod_ref, h_ref):
        shift = mod_ref[0, 0:1, :]
        scale = mod_ref[0, 1:2, :]
        h_ref[...] = (x_ref[...] * (1.0 + scale) + shift).astype(BF16)

    return pl.pallas_call(
        body, name="prep_h",
        grid=(T // ts,),
        in_specs=[pl.BlockSpec((ts, D), lambda i: (i, 0)),
                  pl.BlockSpec((1, 3, D), lambda i: (i // per_seq, 0, 0))],
        out_specs=pl.BlockSpec((ts, D), lambda i: (i, 0)),
        out_shape=jax.ShapeDtypeStruct((T, D), BF16),
    )(x2, mod3)


def _proj(h, w_in_full):
    tm, tn = 512, SHARD

    def body(h_ref, w_ref, o_ref):
        o_ref[...] = _dot(h_ref[...], w_ref[...]).astype(BF16)

    return pl.pallas_call(
        body, name="proj",
        grid=(NCOL // tn, T // tm),
        in_specs=[pl.BlockSpec((tm, D), lambda j, i: (i, 0)),
                  pl.BlockSpec((D, tn), lambda j, i: (0, j))],
        out_specs=pl.BlockSpec((tm, tn), lambda j, i: (i, j)),
        out_shape=jax.ShapeDtypeStruct((T, NCOL), BF16),
        compiler_params=pltpu.CompilerParams(vmem_limit_bytes=VMEM_LIMIT),
    )(h, w_in_full)


def _bucket_maps():
    a = np.arange(QB)[:, None]
    b = np.arange(2 * QB)[None, :]
    steps = a + QB - b
    maps = []
    for dil in DILATIONS:
        dist = np.maximum(steps, 0) * dil
        nf = np.maximum(dist, 1).astype(np.float32)
        large = 16 + (np.log(nf / np.float32(16)) / np.float32(math.log(128.0))
                      * np.float32(16)).astype(np.int32)
        large = np.minimum(large, N_BUCKETS - 1)
        maps.append(np.where(dist < 16, dist, large).astype(np.int32))
    band = (steps >= 0) & (steps <= N_STEPS)
    first = band & (b >= QB)
    masks = np.stack([first, band]).astype(np.int32)
    return np.stack(maps), masks


def _bias_expand(rel_bias, buckets, masks):
    def body(tab_ref, bk_ref, mk_ref, o_ref):
        for g in range(3):
            bk = bk_ref[g]
            for h in range(4):
                col = 4 * g + h
                val = jnp.zeros((QB, 2 * QB), F32)
                for k in range(N_BUCKETS):
                    val = jnp.where(bk == k, tab_ref[k, col], val)
                o_ref[g, 0, h] = jnp.where(mk_ref[0] != 0, val, NEG_INF)
                o_ref[g, 1, h] = jnp.where(mk_ref[1] != 0, val, NEG_INF)

    return pl.pallas_call(
        body, name="bias_expand",
        in_specs=[pl.BlockSpec(memory_space=pltpu.SMEM),
                  pl.BlockSpec(memory_space=pltpu.VMEM),
                  pl.BlockSpec(memory_space=pltpu.VMEM)],
        out_shape=jax.ShapeDtypeStruct((3, 2, 4, QB, 2 * QB), F32),
    )(rel_bias, buckets, masks)


def _bias_grad(ds1, ds2, ds3, buckets):
    def body(d1_ref, d2_ref, d3_ref, bk_ref, o_ref):
        for g, d_ref in enumerate((d1_ref, d2_ref, d3_ref)):
            bk = bk_ref[g]
            for h in range(4):
                dv = d_ref[h]
                for k in range(N_BUCKETS):
                    o_ref[k, 4 * g + h] = jnp.sum(jnp.where(bk == k, dv, 0.0))

    return pl.pallas_call(
        body, name="bias_grad",
        in_specs=[pl.BlockSpec(memory_space=pltpu.VMEM)] * 4,
        out_specs=pl.BlockSpec(memory_space=pltpu.SMEM),
        out_shape=jax.ShapeDtypeStruct((N_BUCKETS, N_HEADS), F32),
    )(ds1, ds2, ds3, buckets)


def _attn_fwd(proj, bias, g):
    dil = DILATIONS[g]
    rows = QB * dil
    nsb = S // rows
    has_prev = nsb > 1

    def residue(r):
        return pl.ds(r, QB, stride=dil) if dil > 1 else pl.ds(0, QB)

    def body(*refs):
        if has_prev:
            (q_ref, kc_ref, vc_ref, kp_ref, vp_ref, b_ref, o_ref, l_ref,
             qs, kcs, vcs, ao, ls, kps, vps) = refs
        else:
            q_ref, kc_ref, vc_ref, b_ref, o_ref, l_ref, qs, kcs, vcs, ao, ls = refs
        lane = lax.broadcasted_iota(jnp.int32, (QB, 128), 1)
        for h in range(4):
            sl = slice(h * HD, (h + 1) * HD)
            qs[...] = q_ref[:, sl].astype(F32)
            kcs[...] = kc_ref[:, sl].astype(F32)
            vcs[...] = vc_ref[:, sl].astype(F32)
            if has_prev:
                kps[...] = kp_ref[:, sl].astype(F32)
                vps[...] = vp_ref[:, sl].astype(F32)
            for r in range(dil):
                rr = residue(r)
                q = qs[rr, :].astype(BF16)
                s2 = _dot_nt(q, kcs[rr, :].astype(BF16)) * SCALE + b_ref[h, :, QB:]
                m = jnp.max(s2, axis=1, keepdims=True)
                if has_prev:
                    s1 = _dot_nt(q, kps[rr, :].astype(BF16)) * SCALE + b_ref[h, :, :QB]
                    m = jnp.maximum(m, jnp.max(s1, axis=1, keepdims=True))
                p2 = jnp.exp(s2 - m)
                l = jnp.sum(p2, axis=1, keepdims=True)
                o = _dot(p2.astype(BF16), vcs[rr, :].astype(BF16))
                if has_prev:
                    p1 = jnp.exp(s1 - m)
                    l = l + jnp.sum(p1, axis=1, keepdims=True)
                    o = o + _dot(p1.astype(BF16), vps[rr, :].astype(BF16))
                ao[rr, :] = o / l
                blk = slice(r * QB, (r + 1) * QB)
                lse = m + jnp.log(l)
                ls[blk, :] = jnp.where(lane == h, lse, 0.0 if h == 0 else ls[blk, :])
            o_ref[:, sl] = ao[...]
        for r in range(dil):
            l_ref[residue(r), :] = ls[r * QB:(r + 1) * QB, :]

    def row(b, n):
        return b * nsb + n

    def prev(b, n):
        return b * nsb + jnp.maximum(n - 1, 0)

    in_specs = [
        pl.BlockSpec((rows, GW), lambda b, n: (row(b, n), CB_Q + g)),
        pl.BlockSpec((rows, GW), lambda b, n: (row(b, n), CB_K + g)),
        pl.BlockSpec((rows, GW), lambda b, n: (row(b, n), CB_V + g)),
    ]
    args = [proj, proj, proj]
    scratch = [pltpu.VMEM((rows, HD), F32)] * 4 + [pltpu.VMEM((rows, 128), F32)]
    if has_prev:
        in_specs += [pl.BlockSpec((rows, GW), lambda b, n: (prev(b, n), CB_K + g)),
                     pl.BlockSpec((rows, GW), lambda b, n: (prev(b, n), CB_V + g))]
        args += [proj, proj]
        scratch += [pltpu.VMEM((rows, HD), F32)] * 2
    in_specs.append(pl.BlockSpec((None, None, 4, QB, 2 * QB),
                                 lambda b, n: (g, jnp.minimum(n, 1), 0, 0, 0)))
    args.append(bias)
    return pl.pallas_call(
        body, name=f"attn_fwd{g}",
        grid=(BL, nsb),
        in_specs=in_specs,
        out_specs=(pl.BlockSpec((rows, GW), lambda b, n: (row(b, n), 0)),
                   pl.BlockSpec((rows, 128), lambda b, n: (row(b, n), 0))),
        out_shape=(jax.ShapeDtypeStruct((T, GW), F32), jax.ShapeDtypeStruct((T, 128), F32)),
        scratch_shapes=scratch,
        compiler_params=pltpu.CompilerParams(vmem_limit_bytes=VMEM_LIMIT),
    )(*args)


def _attn_bwd(proj, d_out, stats, bias, dqkv, g):
    dil = DILATIONS[g]
    rows = QB * dil
    nsb = S // rows
    has_prev = nsb > 1
    n_steps = nsb + 1 if has_prev else 1
    n_in = 6 + (2 if has_prev else 0) + (0 if dqkv is None else 1)

    def residue(r):
        return pl.ds(r, QB, stride=dil) if dil > 1 else pl.ds(0, QB)

    def body(*refs):
        q_ref, kc_ref, vc_ref, do_ref, st_ref, b_ref = refs[:6]
        if has_prev:
            kp_ref, vp_ref = refs[6:8]
        out_ref, db_ref = refs[n_in], refs[n_in + 1]
        scr = refs[n_in + 2:]
        qs, kcs, vcs, dos, sts, aq, ak, av = scr[:8]
        if has_prev:
            kps, vps, carry = scr[8:]
        b, n = pl.program_id(0), pl.program_id(1)

        @pl.when((b == 0) & (n == 0))
        def _():
            db_ref[...] = jnp.zeros_like(db_ref)

        def flush_head(h):
            out_ref[:, h * HD:(h + 1) * HD] = aq[...].astype(BF16)
            out_ref[:, GW + h * HD:GW + (h + 1) * HD] = ak[...].astype(BF16)
            out_ref[:, 2 * GW + h * HD:2 * GW + (h + 1) * HD] = av[...].astype(BF16)

        if has_prev:
            @pl.when(n == nsb)
            def _():
                for h in range(4):
                    for r in range(dil):
                        blk = slice(r * QB, (r + 1) * QB)
                        aq[residue(r), :] = carry[blk, h * HD:(h + 1) * HD]
                        ak[residue(r), :] = carry[blk, GW + h * HD:GW + (h + 1) * HD]
                        av[residue(r), :] = carry[blk, 2 * GW + h * HD:2 * GW + (h + 1) * HD]
                    flush_head(h)

        @pl.when(n < nsb)
        def _():
            for r in range(dil):
                sts[r * QB:(r + 1) * QB, :] = st_ref[residue(r), :]
            for h in range(4):
                sl = slice(h * HD, (h + 1) * HD)
                qs[...] = q_ref[:, sl].astype(F32)
                kcs[...] = kc_ref[:, sl].astype(F32)
                vcs[...] = vc_ref[:, sl].astype(F32)
                dos[...] = do_ref[:, sl].astype(F32)
                if has_prev:
                    kps[...] = kp_ref[:, sl].astype(F32)
                    vps[...] = vp_ref[:, sl].astype(F32)
                for r in range(dil):
                    rr = residue(r)
                    blk = slice(r * QB, (r + 1) * QB)
                    q, kc, vc = qs[rr, :].astype(BF16), kcs[rr, :].astype(BF16), vcs[rr, :].astype(BF16)
                    do = dos[rr, :].astype(BF16)
                    lse = sts[blk, h:h + 1]
                    delta = sts[blk, 4 + h:5 + h]
                    s2 = _dot_nt(q, kc) * SCALE + b_ref[h, :, QB:]
                    p2 = jnp.exp(s2 - lse)
                    ds2 = p2 * (_dot_nt(do, vc) - delta)
                    db_ref[h, :, QB:] += ds2
                    ds2b, p2b = ds2.astype(BF16), p2.astype(BF16)
                    dq = _dot(ds2b, kc)
                    dk_cur = _dot_tn(ds2b, q) * SCALE
                    dv_cur = _dot_tn(p2b, do)
                    if has_prev:
                        kp, vp = kps[rr, :].astype(BF16), vps[rr, :].astype(BF16)
                        s1 = _dot_nt(q, kp) * SCALE + b_ref[h, :, :QB]
                        p1 = jnp.exp(s1 - lse)
                        ds1 = p1 * (_dot_nt(do, vp) - delta)
                        db_ref[h, :, :QB] += ds1
                        ds1b, p1b = ds1.astype(BF16), p1.astype(BF16)
                        dq = dq + _dot(ds1b, kp)
                        dk_prev = _dot_tn(ds1b, q) * SCALE
                        dv_prev = _dot_tn(p1b, do)
                        cq = slice(h * HD, (h + 1) * HD)
                        ck = slice(GW + h * HD, GW + (h + 1) * HD)
                        cv = slice(2 * GW + h * HD, 2 * GW + (h + 1) * HD)

                        @pl.when(n > 0)
                        def _():
                            aq[rr, :] = carry[blk, cq]
                            ak[rr, :] = carry[blk, ck] + dk_prev
                            av[rr, :] = carry[blk, cv] + dv_prev

                        carry[blk, cq] = dq * SCALE
                        carry[blk, ck] = dk_cur
                        carry[blk, cv] = dv_cur
                    else:
                        aq[rr, :] = dq * SCALE
                        ak[rr, :] = dk_cur
                        av[rr, :] = dv_cur
                if has_prev:
                    @pl.when(n > 0)
                    def _():
                        flush_head(h)
                else:
                    flush_head(h)

    def row(b, n):
        return b * nsb + jnp.minimum(n, nsb - 1)

    def prev(b, n):
        return b * nsb + jnp.maximum(jnp.minimum(n, nsb - 1) - 1, 0)

    def done(b, n):
        return b * nsb + jnp.maximum(n - 1, 0)

    in_specs = [
        pl.BlockSpec((rows, GW), lambda b, n: (row(b, n), CB_Q + g)),
        pl.BlockSpec((rows, GW), lambda b, n: (row(b, n), CB_K + g)),
        pl.BlockSpec((rows, GW), lambda b, n: (row(b, n), CB_V + g)),
        pl.BlockSpec((rows, GW), lambda b, n: (row(b, n), 0)),
        pl.BlockSpec((rows, 128), lambda b, n: (row(b, n), 0)),
        pl.BlockSpec((None, None, 4, QB, 2 * QB),
                     lambda b, n: (g, jnp.minimum(jnp.minimum(n, nsb - 1), 1), 0, 0, 0)),
    ]
    args = [proj, proj, proj, d_out, stats, bias]
    scratch = [pltpu.VMEM((rows, HD), F32)] * 4 + [pltpu.VMEM((rows, 128), F32)] \
        + [pltpu.VMEM((rows, HD), F32)] * 3
    if has_prev:
        in_specs += [pl.BlockSpec((rows, GW), lambda b, n: (prev(b, n), CB_K + g)),
                     pl.BlockSpec((rows, GW), lambda b, n: (prev(b, n), CB_V + g))]
        args += [proj, proj]
        scratch += [pltpu.VMEM((rows, HD), F32)] * 2 + [pltpu.VMEM((rows, 3 * GW), F32)]
    aliases = {}
    if dqkv is not None:
        in_specs.append(pl.BlockSpec(memory_space=pl.ANY))
        args.append(dqkv)
        aliases = {len(args) - 1: 0}
    return pl.pallas_call(
        body, name=f"attn_bwd{g}",
        grid=(BL, n_steps),
        in_specs=in_specs,
        out_specs=(pl.BlockSpec((rows, 3 * GW), lambda b, n: (done(b, n), g)),
                   pl.BlockSpec((4, QB, 2 * QB), lambda b, n: (0, 0, 0))),
        out_shape=(jax.ShapeDtypeStruct((T, 9 * GW), BF16),
                   jax.ShapeDtypeStruct((4, QB, 2 * QB), F32)),
        scratch_shapes=scratch,
        input_output_aliases=aliases,
        compiler_params=pltpu.CompilerParams(vmem_limit_bytes=VMEM_LIMIT),
    )(*args)


def _tail(x2, tgt2, mod3, o_g, lse_g, proj, w_ao, w_co, w_o, conv_w, conv_b, ln_g, ln_b):
    tm = 256
    per_seq = S // tm
    halo = 16

    def body(x_ref, t_ref, mod_ref, o1_ref, o2_ref, o3_ref, l1_ref, l2_ref, l3_ref,
             ga_ref, u_ref, bg_ref, cg_ref, gc_ref, ma_ref, mc_ref, up_ref, cp_ref,
             wao_ref, wco_ref, wo_ref, cw_ref, cb_ref, lg_ref, lb_ref,
             dtail_ref, dyc_ref, do_ref, st_ref, dxd_ref,
             mg_ref, dy_ref, ain_ref, dao_ref, sin_ref, dso_ref, vec_ref):
        i = pl.program_id(0)
        bidx = i // per_seq
        first = (i % per_seq) == 0

        @pl.when(i == 0)
        def _():
            vec_ref[...] = jnp.zeros_like(vec_ref)

        l1, l2, l3 = l1_ref[...], l2_ref[...], l3_ref[...]
        mx = jnp.maximum(jnp.maximum(l1, l2), l3)
        e1, e2, e3 = jnp.exp(l1 - mx), jnp.exp(l2 - mx), jnp.exp(l3 - mx)
        esum = e1 + e2 + e3
        lse_tot = mx + jnp.log(esum)
        w1, w2, w3 = e1 / esum, e2 / esum, e3 / esum

        def per_head(wv):
            return jnp.concatenate([jnp.broadcast_to(wv[:, h:h + 1], (tm, HD)) for h in range(4)], axis=1)

        o = per_head(w1) * o1_ref[...] + per_head(w2) * o2_ref[...] + per_head(w3) * o3_ref[...]

        ga = ga_ref[...].astype(F32)
        sig_ga = _sigmoid(ga)
        silu_ga = ga * sig_ga
        a_in = (o * silu_ga).astype(BF16)
        a_out = _dot(a_in, wao_ref[...])

        u = u_ref[...].astype(F32)
        cg = cg_ref[...].astype(F32)
        z = cg * u
        zp = cp_ref[...].astype(F32) * up_ref[...].astype(F32)
        zp = jnp.where(first, 0.0, zp)
        zcat = jnp.concatenate([zp, z], axis=0)
        z1 = pltpu.roll(zcat, 1, 0)[halo:]
        z2 = pltpu.roll(zcat, 2, 0)[halo:]
        y_conv = cw_ref[0:1, :] * z2 + cw_ref[1:2, :] * z1 + cw_ref[2:3, :] * z + cb_ref[...]
        gc = gc_ref[...].astype(F32)
        sig_gc = _sigmoid(gc)
        silu_gc = gc * sig_gc
        bg = bg_ref[...].astype(F32)
        s_in = (bg * y_conv * silu_gc).astype(BF16)
        s_out = _dot(s_in, wco_ref[...])

        sa = _sigmoid(ma_ref[...].astype(F32))
        sc = _sigmoid(mc_ref[...].astype(F32))
        merged = (sa * a_out + sc * s_out).astype(BF16)
        y = _dot(merged, wo_ref[...])
        gate1 = 1.0 + mod_ref[0, 2:3, :]
        xv = x_ref[...]
        resid = ALPHA * xv + gate1 * y
        mu = jnp.mean(resid, axis=1, keepdims=True)
        xc = resid - mu
        var = jnp.mean(xc * xc, axis=1, keepdims=True)
        rstd = lax.rsqrt(var + LN_EPS)
        xhat = xc * rstd
        lg = lg_ref[...]
        err = xhat * lg + lb_ref[...] - t_ref[...]
        vec_ref[3:4, :] += (0.5 / D) * jnp.sum(err * err, axis=0, keepdims=True)

        dout = err * (1.0 / D)
        vec_ref[1:2, :] += jnp.sum(dout * xhat, axis=0, keepdims=True)
        vec_ref[2:3, :] += jnp.sum(dout, axis=0, keepdims=True)
        dxh = dout * lg
        dres = rstd * (dxh - jnp.mean(dxh, axis=1, keepdims=True)
                       - xhat * jnp.mean(dxh * xhat, axis=1, keepdims=True))
        dxd_ref[...] = ALPHA * dres
        dgate = jnp.sum(dres * y, axis=0, keepdims=True)
        vec_ref[4:5, :] += jnp.where(bidx == 0, dgate, 0.0)
        vec_ref[5:6, :] += jnp.where(bidx == 1, dgate, 0.0)
        dy = (dres * gate1).astype(BF16)

        dmerged = _dot_nt(dy, wo_ref[...])
        da_out = (dmerged * sa).astype(BF16)
        ds_out = (dmerged * sc).astype(BF16)
        dtail_ref[:, 3584:4608] = (dmerged * s_out * sc * (1.0 - sc)).astype(BF16)
        dtail_ref[:, 2560:3584] = (dmerged * a_out * sa * (1.0 - sa)).astype(BF16)
        da_in = _dot_nt(da_out, wao_ref[...])
        ds_in = _dot_nt(ds_out, wco_ref[...])

        d_o = da_in * silu_ga
        do_ref[...] = d_o.astype(BF16)
        dtail_ref[:, 0:512] = (da_in * o * (sig_ga * (1.0 + ga * (1.0 - sig_ga)))).astype(BF16)
        lane = lax.broadcasted_iota(jnp.int32, (tm, 128), 1)
        stats = lse_tot
        od = o * d_o
        for h in range(4):
            delta = jnp.sum(od[:, h * HD:(h + 1) * HD], axis=1, keepdims=True)
            stats = jnp.where(lane == 4 + h, delta, stats)
        st_ref[...] = stats

        dtail_ref[:, 512:1536] = (ds_in * y_conv * silu_gc).astype(BF16)
        dyc = ds_in * bg * silu_gc
        dyc_ref[...] = dyc
        vec_ref[0:1, :] += jnp.sum(dyc, axis=0, keepdims=True)
        dtail_ref[:, 1536:2560] = (ds_in * bg * y_conv * (sig_gc * (1.0 + gc * (1.0 - sig_gc)))).astype(BF16)

        mg_ref[...] = merged
        dy_ref[...] = dy
        ain_ref[...] = a_in
        dao_ref[...] = da_out
        sin_ref[...] = s_in
        dso_ref[...] = ds_out

    def tile(width, cblk=0):
        return pl.BlockSpec((tm, width), lambda i: (i, cblk))

    def whole(shape):
        return pl.BlockSpec(shape, lambda i: tuple(0 for _ in shape))

    prev_rows = lambda i: (jnp.maximum(i * (tm // halo) - 1, 0),)
    in_specs = [
        tile(D), tile(D), pl.BlockSpec((1, 3, D), lambda i: (i // per_seq, 0, 0)),
        tile(GW), tile(GW), tile(GW), tile(128), tile(128), tile(128),
        tile(GW, CB_GA), tile(D, KB_U), tile(D, KB_BG), tile(D, KB_CG), tile(D, KB_GC),
        tile(D, KB_MA), tile(D, KB_MC),
        pl.BlockSpec((halo, D), lambda i: (*prev_rows(i), KB_U)),
        pl.BlockSpec((halo, D), lambda i: (*prev_rows(i), KB_CG)),
        whole((GW, D)), whole((D, D)), whole((D, D)),
        whole((3, D)), whole((1, D)), whole((1, D)), whole((1, D)),
    ]
    out_specs = (
        tile(9 * CB), tile(D), tile(GW), tile(128), tile(D),
        tile(D), tile(D), tile(GW), tile(D), tile(D), tile(D),
        pl.BlockSpec((8, D), lambda i: (0, 0)),
    )
    out_shape = (
        jax.ShapeDtypeStruct((T, 9 * CB), BF16),
        jax.ShapeDtypeStruct((T, D), F32),
        jax.ShapeDtypeStruct((T, GW), BF16),
        jax.ShapeDtypeStruct((T, 128), F32),
        jax.ShapeDtypeStruct((T, D), F32),
        jax.ShapeDtypeStruct((T, D), BF16),
        jax.ShapeDtypeStruct((T, D), BF16),
        jax.ShapeDtypeStruct((T, GW), BF16),
        jax.ShapeDtypeStruct((T, D), BF16),
        jax.ShapeDtypeStruct((T, D), BF16),
        jax.ShapeDtypeStruct((T, D), BF16),
        jax.ShapeDtypeStruct((8, D), F32),
    )
    return pl.pallas_call(
        body, name="tail",
        grid=(T // tm,),
        in_specs=in_specs, out_specs=out_specs, out_shape=out_shape,
        compiler_params=pltpu.CompilerParams(vmem_limit_bytes=VMEM_LIMIT),
    )(x2, tgt2, mod3, *o_g, *lse_g, proj, proj, proj, proj, proj, proj, proj, proj, proj,
      w_ao, w_co, w_o, conv_w, conv_b, ln_g, ln_b)


def _conv_bwd(dyc, proj, conv_w):
    tm = 512
    per_seq = S // tm
    halo = 16

    def body(d_ref, dn_ref, u_ref, c_ref, up_ref, cp_ref, cw_ref, o_ref, g_ref):
        i = pl.program_id(0)
        first = (i % per_seq) == 0
        last = (i % per_seq) == per_seq - 1

        @pl.when(i == 0)
        def _():
            g_ref[...] = jnp.zeros_like(g_ref)

        d = d_ref[...]
        dn = jnp.where(last, 0.0, dn_ref[...])
        dcat = jnp.concatenate([d, dn], axis=0)
        d1 = pltpu.roll(dcat, tm + 8 - 1, 0)[:tm]
        d2 = pltpu.roll(dcat, tm + 8 - 2, 0)[:tm]
        dz = cw_ref[2:3, :] * d + cw_ref[1:2, :] * d1 + cw_ref[0:1, :] * d2
        u = u_ref[...].astype(F32)
        cg = c_ref[...].astype(F32)
        o_ref[:, 0:D] = (dz * cg).astype(BF16)
        o_ref[:, D:2 * D] = (dz * u).astype(BF16)

        z = cg * u
        zp = jnp.where(first, 0.0, cp_ref[...].astype(F32) * up_ref[...].astype(F32))
        zcat = jnp.concatenate([zp, z], axis=0)
        z1 = pltpu.roll(zcat, 1, 0)[halo:]
        z2 = pltpu.roll(zcat, 2, 0)[halo:]
        g_ref[0:1, :] += jnp.sum(d * z2, axis=0, keepdims=True)
        g_ref[1:2, :] += jnp.sum(d * z1, axis=0, keepdims=True)
        g_ref[2:3, :] += jnp.sum(d * z, axis=0, keepdims=True)

    n_tiles = T // tm
    prev_rows = lambda i: jnp.maximum(i * (tm // halo) - 1, 0)
    next_rows = lambda i: jnp.minimum((i + 1) * (tm // 8), T // 8 - 1)
    return pl.pallas_call(
        body, name="conv_bwd",
        grid=(n_tiles,),
        in_specs=[pl.BlockSpec((tm, D), lambda i: (i, 0)),
                  pl.BlockSpec((8, D), lambda i: (next_rows(i), 0)),
                  pl.BlockSpec((tm, D), lambda i: (i, KB_U)),
                  pl.BlockSpec((tm, D), lambda i: (i, KB_CG)),
                  pl.BlockSpec((halo, D), lambda i: (prev_rows(i), KB_U)),
                  pl.BlockSpec((halo, D), lambda i: (prev_rows(i), KB_CG)),
                  pl.BlockSpec((3, D), lambda i: (0, 0))],
        out_specs=(pl.BlockSpec((tm, 2 * D), lambda i: (i, 0)),
                   pl.BlockSpec((8, D), lambda i: (0, 0))),
        out_shape=(jax.ShapeDtypeStruct((T, 2 * D), BF16),
                   jax.ShapeDtypeStruct((8, D), F32)),
        compiler_params=pltpu.CompilerParams(vmem_limit_bytes=VMEM_LIMIT),
    )(dyc, dyc, proj, proj, proj, proj, conv_w)


def _d_specs(tile_rows, row_of, jj_of):
    return [
        pl.BlockSpec((tile_rows, CB), lambda *ids: (row_of(*ids), jnp.minimum(jj_of(*ids), 8))),
        pl.BlockSpec((tile_rows, CB), lambda *ids: (row_of(*ids), jnp.clip(jj_of(*ids) - 9, 0, 3))),
        pl.BlockSpec((tile_rows, CB), lambda *ids: (row_of(*ids), jnp.clip(jj_of(*ids) - 13, 0, 8))),
    ]


def _dh_dx(dqkv, dconv, dtail, w_in_full, x2, dxd, mod3):
    tm = 512
    per_seq = S // tm

    def body(dq_ref, dc_ref, dt_ref, w_ref, x_ref, dxd_ref, mod_ref, gx_ref, vec_ref, acc):
        i, jj = pl.program_id(0), pl.program_id(1)

        @pl.when((i == 0) & (jj == 0))
        def _():
            vec_ref[...] = jnp.zeros_like(vec_ref)

        @pl.when(jj == 0)
        def _():
            acc[...] = jnp.zeros_like(acc)

        @pl.when(jj < 9)
        def _():
            acc[...] += _dot_nt(dq_ref[...], w_ref[...])

        @pl.when((jj >= 9) & (jj < 13))
        def _():
            acc[...] += _dot_nt(dc_ref[...], w_ref[...])

        @pl.when(jj >= 13)
        def _():
            acc[...] += _dot_nt(dt_ref[...], w_ref[...])

        @pl.when(jj == NCB - 1)
        def _():
            dh = acc[...]
            bidx = i // per_seq
            gx_ref[...] = dxd_ref[...] + dh * (1.0 + mod_ref[0, 1:2, :])
            dshift = jnp.sum(dh, axis=0, keepdims=True)
            dscale = jnp.sum(dh * x_ref[...], axis=0, keepdims=True)
            vec_ref[0:1, :] += jnp.where(bidx == 0, dshift, 0.0)
            vec_ref[1:2, :] += jnp.where(bidx == 1, dshift, 0.0)
            vec_ref[2:3, :] += jnp.where(bidx == 0, dscale, 0.0)
            vec_ref[3:4, :] += jnp.where(bidx == 1, dscale, 0.0)

    return pl.pallas_call(
        body, name="dh_dx",
        grid=(T // tm, NCB),
        in_specs=_d_specs(tm, lambda i, jj: i, lambda i, jj: jj) + [
            pl.BlockSpec((D, CB), lambda i, jj: (0, _perm(jj))),
            pl.BlockSpec((tm, D), lambda i, jj: (i, 0)),
            pl.BlockSpec((tm, D), lambda i, jj: (i, 0)),
            pl.BlockSpec((1, 3, D), lambda i, jj: (i // per_seq, 0, 0))],
        out_specs=(pl.BlockSpec((tm, D), lambda i, jj: (i, 0)),
                   pl.BlockSpec((8, D), lambda i, jj: (0, 0))),
        out_shape=(jax.ShapeDtypeStruct((T, D), F32), jax.ShapeDtypeStruct((8, D), F32)),
        scratch_shapes=[pltpu.VMEM((tm, D), F32)],
        compiler_params=pltpu.CompilerParams(vmem_limit_bytes=VMEM_LIMIT),
    )(dqkv, dconv, dtail, w_in_full, x2, dxd, mod3)


def _gw_in(h, dqkv, dconv, dtail):
    tk = 1024

    def body(h_ref, dq_ref, dc_ref, dt_ref, o_ref):
        jj, k = pl.program_id(0), pl.program_id(1)

        @pl.when(k == 0)
        def _():
            o_ref[...] = jnp.zeros_like(o_ref)

        @pl.when(jj < 9)
        def _():
            o_ref[...] += _dot_tn(h_ref[...], dq_ref[...])

        @pl.when((jj >= 9) & (jj < 13))
        def _():
            o_ref[...] += _dot_tn(h_ref[...], dc_ref[...])

        @pl.when(jj >= 13)
        def _():
            o_ref[...] += _dot_tn(h_ref[...], dt_ref[...])

    return pl.pallas_call(
        body, name="gw_in",
        grid=(NCB, T // tk),
        in_specs=[pl.BlockSpec((tk, D), lambda jj, k: (k, 0))]
                 + _d_specs(tk, lambda jj, k: k, lambda jj, k: jj),
        out_specs=pl.BlockSpec((D, CB), lambda jj, k: (0, _perm(jj))),
        out_shape=jax.ShapeDtypeStruct((D, NCOL), F32),
        compiler_params=pltpu.CompilerParams(vmem_limit_bytes=VMEM_LIMIT),
    )(h, dqkv, dconv, dtail)


def _mm_tn(a, b, tn, blocks_leading, name):
    kk, m = a.shape
    n = b.shape[1]
    tk = 1024

    def body(a_ref, b_ref, o_ref):
        @pl.when(pl.program_id(1) == 0)
        def _():
            o_ref[...] = jnp.zeros_like(o_ref)

        o_ref[...] += _dot_tn(a_ref[...], b_ref[...])

    if blocks_leading:
        out_spec = pl.BlockSpec((None, m, tn), lambda j, k: (j, 0, 0))
        out_shape = jax.ShapeDtypeStruct((n // tn, m, tn), F32)
    else:
        out_spec = pl.BlockSpec((m, tn), lambda j, k: (0, j))
        out_shape = jax.ShapeDtypeStruct((m, n), F32)
    return pl.pallas_call(
        body, name=name,
        grid=(n // tn, kk // tk),
        in_specs=[pl.BlockSpec((tk, m), lambda j, k: (k, 0)),
                  pl.BlockSpec((tk, tn), lambda j, k: (k, j))],
        out_specs=out_spec, out_shape=out_shape,
        compiler_params=pltpu.CompilerParams(vmem_limit_bytes=VMEM_LIMIT),
    )(a, b)


def _adamw(parts, w, m, v, name, row_tile=None):
    n_parts, rows, cols = parts.shape
    tr = rows if row_tile is None else row_tile
    c1 = 1.0 - ADAM_B1 ** ADAM_STEP
    c2 = 1.0 - ADAM_B2 ** ADAM_STEP

    def body(p_ref, w_ref, m_ref, v_ref, g_ref, d_ref, nm_ref, nv_ref):
        g = p_ref[0]
        for s in range(1, n_parts):
            g = g + p_ref[s]
        nm = ADAM_B1 * m_ref[...] + (1.0 - ADAM_B1) * g
        nv = ADAM_B2 * v_ref[...] + (1.0 - ADAM_B2) * (g * g)
        m_hat = nm / c1
        v_hat = nv / c2
        g_ref[...] = g
        d_ref[...] = -ADAM_LR * (m_hat / (jnp.sqrt(v_hat) + ADAM_EPS) + ADAM_WD * w_ref[...])
        nm_ref[...] = nm
        nv_ref[...] = nv

    blk = pl.BlockSpec((tr, cols), lambda i: (i, 0))
    shp = jax.ShapeDtypeStruct((rows, cols), F32)
    return pl.pallas_call(
        body, name=name,
        grid=(rows // tr,),
        in_specs=[pl.BlockSpec((n_parts, tr, cols), lambda i: (0, i, 0)), blk, blk, blk],
        out_specs=(blk, blk, blk, blk),
        out_shape=(shp, shp, shp, shp),
        compiler_params=pltpu.CompilerParams(vmem_limit_bytes=VMEM_LIMIT),
    )(parts, w, m, v)


def _loss_sum(rows):
    def body(r_ref, o_ref):
        o_ref[...] = jnp.sum(jnp.sum(r_ref[...], axis=0, keepdims=True), axis=1, keepdims=True)

    return pl.pallas_call(body, name="loss_sum", out_shape=jax.ShapeDtypeStruct((1, 1), F32))(rows)


def _local_step(x2, tgt2, mod3, w_in_full, w_ao, w_co, w_o, conv_w, conv_b, rel_bias, ln_g, ln_b):
    buckets_np, masks_np = _bucket_maps()
    buckets, masks = jnp.asarray(buckets_np), jnp.asarray(masks_np)

    h = _prep_h(x2, mod3)
    proj = _proj(h, w_in_full)
    bias = _bias_expand(rel_bias, buckets, masks)
    fwd = [_attn_fwd(proj, bias, g) for g in range(3)]
    o_g = [f[0] for f in fwd]
    lse_g = [f[1] for f in fwd]

    (dtail, dyc, d_o, stats, dxd, merged, dy, a_in, da_out, s_in, ds_out, tail_vec) = _tail(
        x2, tgt2, mod3, o_g, lse_g, proj, w_ao, w_co, w_o, conv_w, conv_b, ln_g, ln_b)

    dqkv = None
    dbias = []
    for g in range(3):
        dqkv, db = _attn_bwd(proj, d_o, stats, bias, dqkv, g)
        dbias.append(db)
    g_rel_bias = _bias_grad(*dbias, buckets)
    dconv, conv_vec = _conv_bwd(dyc, proj, conv_w)

    grad_x, mod_vec = _dh_dx(dqkv, dconv, dtail, w_in_full, x2, dxd, mod3)
    gw_in = _gw_in(h, dqkv, dconv, dtail)
    gw_o = _mm_tn(merged, dy, 512, False, "gw_o")
    gw_co = _mm_tn(s_in, ds_out, 512, False, "gw_conv_out")
    gw_ao = _mm_tn(a_in, da_out, 128, True, "gw_attn_out")
    return grad_x, gw_in, gw_ao, gw_co, gw_o, conv_vec, g_rel_bias, tail_vec, mod_vec


def kernel(x, c, w_ada, b_ada, w_in, conv_w, conv_b, rel_bias, w_attn_out, w_conv_out, w_o, ln_g, ln_b, loss_target, m_w_ada, m_b_ada, m_w_in, m_conv_w, m_conv_b, m_rel_bias, m_w_attn_out, m_w_conv_out, m_w_o, m_ln_g, m_ln_b, v_w_ada, v_b_ada, v_w_in, v_conv_w, v_conv_b, v_rel_bias, v_w_attn_out, v_w_conv_out, v_w_o, v_ln_g, v_ln_b):
    me = _my_index()
    x2 = x.reshape(T, D)
    tgt2 = loss_target.reshape(T, D)

    w_in_full, w_ao_g, w_co_g, w_o_g, conv_w_g, c_g = _all_gather(
        [w_in[0].astype(BF16), w_attn_out[0].astype(BF16), w_conv_out[0].astype(BF16),
         w_o[0].astype(BF16), conv_w[0], c],
        [jax.ShapeDtypeStruct((D, NCOL), BF16),
         jax.ShapeDtypeStruct((N_DEV, GW, D // N_DEV), BF16),
         jax.ShapeDtypeStruct((N_DEV, D // N_DEV, D), BF16),
         jax.ShapeDtypeStruct((N_DEV, D // N_DEV, D), BF16),
         jax.ShapeDtypeStruct((N_DEV, 3, D // N_DEV), F32),
         jax.ShapeDtypeStruct((N_DEV, BL, D), F32)],
        [_slot_columns(SHARD)] + [_slot_leading] * 5,
        "gather_weights")
    w_ao_full = jnp.transpose(w_ao_g, (1, 0, 2)).reshape(GW, D)
    w_co_full = w_co_g.reshape(D, D)
    w_o_full = w_o_g.reshape(D, D)
    conv_w_full = jnp.transpose(conv_w_g, (1, 0, 2)).reshape(3, D)
    c_all = c_g.reshape(N_DEV * BL, D)

    b_cols = lax.dynamic_slice(b_ada, (0, me * ADA_SHARD), (1, ADA_SHARD))
    mod_cols = _ada_fwd(c_all, w_ada[0], b_cols)
    (mod_g,) = _all_gather([mod_cols], [jax.ShapeDtypeStruct((N_DEV, N_DEV * BL, ADA_SHARD), F32)],
                           [_slot_leading], "gather_mod")
    mod_all = jnp.transpose(mod_g, (1, 0, 2)).reshape(N_DEV * BL, 3 * D)
    mod3 = lax.dynamic_slice(mod_all, (me * BL, 0), (BL, 3 * D)).reshape(BL, 3, D)

    (grad_x, gw_in, gw_ao, gw_co, gw_o, conv_vec, g_rel_bias, tail_vec, mod_vec) = _local_step(
        x2, tgt2, mod3, w_in_full, w_ao_full, w_co_full, w_o_full,
        conv_w_full, conv_b, rel_bias, ln_g, ln_b)

    g_conv_w_blocks = jnp.transpose(conv_vec[0:3].reshape(3, N_DEV, D // N_DEV), (1, 0, 2))
    r_in, r_ao, r_co, r_o, r_cw = _all_to_all(
        [gw_in, gw_ao, gw_co.reshape(N_DEV, D // N_DEV, D), gw_o.reshape(N_DEV, D // N_DEV, D),
         g_conv_w_blocks],
        [jax.ShapeDtypeStruct((N_DEV, D, SHARD), F32),
         jax.ShapeDtypeStruct((N_DEV, GW, D // N_DEV), F32),
         jax.ShapeDtypeStruct((N_DEV, D // N_DEV, D), F32),
         jax.ShapeDtypeStruct((N_DEV, D // N_DEV, D), F32),
         jax.ShapeDtypeStruct((N_DEV, 3, D // N_DEV), F32)],
        [_slot_columns(SHARD)] + [_slot_leading] * 4,
        "scatter_grads")

    small = jnp.concatenate([
        tail_vec[0:4],
        jnp.pad(g_rel_bias.reshape(1, N_BUCKETS * N_HEADS), ((0, 0), (0, D - N_BUCKETS * N_HEADS))),
        jnp.zeros((3, D), F32)], axis=0)
    dmod = jnp.concatenate([mod_vec[0:2], mod_vec[2:4], tail_vec[4:6]], axis=1)
    small_g, dmod_g = _all_gather(
        [small, dmod],
        [jax.ShapeDtypeStruct((N_DEV, 8, D), F32), jax.ShapeDtypeStruct((N_DEV, BL, 3 * D), F32)],
        [_slot_leading] * 2, "gather_small")
    dmod_all = dmod_g.reshape(N_DEV * BL, 3 * D)
    loss = _loss_sum(small_g[:, 3, :]).reshape(())
    g_w_ada = _ada_bwd(jnp.transpose(c_all), lax.dynamic_slice(dmod_all, (0, me * ADA_SHARD),
                                                               (N_DEV * BL, ADA_SHARD)))

    def upd(parts, w, m, v, name, row_tile=None):
        shape = w.shape
        w2, m2, v2 = (t.reshape(parts.shape[1:]) for t in (w, m, v))
        return tuple(t.reshape(shape) for t in _adamw(parts, w2, m2, v2, name, row_tile))

    res = {
        "w_ada": upd(g_w_ada[None], w_ada, m_w_ada, v_w_ada, "adam_w_ada", 256),
        "b_ada": upd(dmod_all[:, None, :], b_ada, m_b_ada, v_b_ada, "adam_b_ada"),
        "w_in": upd(r_in, w_in, m_w_in, v_w_in, "adam_w_in", 128),
        "conv_w": upd(r_cw, conv_w, m_conv_w, v_conv_w, "adam_conv_w"),
        "conv_b": upd(small_g[:, 0:1, :], conv_b, m_conv_b, v_conv_b, "adam_conv_b"),
        "rel_bias": upd(small_g[:, 4, :N_BUCKETS * N_HEADS].reshape(N_DEV, N_BUCKETS, N_HEADS),
                        rel_bias, m_rel_bias, v_rel_bias, "adam_rel_bias"),
        "w_attn_out": upd(r_ao, w_attn_out, m_w_attn_out, v_w_attn_out, "adam_w_attn_out"),
        "w_conv_out": upd(r_co, w_conv_out, m_w_conv_out, v_w_conv_out, "adam_w_conv_out"),
        "w_o": upd(r_o, w_o, m_w_o, v_w_o, "adam_w_o"),
        "ln_g": upd(small_g[:, 1:2, :], ln_g, m_ln_g, v_ln_g, "adam_ln_g"),
        "ln_b": upd(small_g[:, 2:3, :], ln_b, m_ln_b, v_ln_b, "adam_ln_b"),
    }
    order = ["w_ada", "b_ada", "w_in", "conv_w", "conv_b", "rel_bias", "w_attn_out", "w_conv_out",
             "w_o", "ln_g", "ln_b"]
    outs = [loss, grad_x.reshape(BL, S, D)]
    for k in range(4):
        outs += [res[name][k] for name in order]
    return tuple(outs)
```

```python
import functools
import math

import numpy as np
import jax
import jax.numpy as jnp
from jax import lax
from jax.experimental import pallas as pl
from jax.experimental.pallas import tpu as pltpu

F32 = jnp.float32
BF16 = jnp.bfloat16
MESH = pl.DeviceIdType.MESH

N_DEV = 8
D = 1024
S = 2048
BL = 2
T = BL * S
NCOL = 11264
SHARD = NCOL // N_DEV
CB = 512
NCB = NCOL // CB
HD = 128
GW = 512
QB = 128
DILATIONS = (1, 4, 16)
N_STEPS = 128
N_BUCKETS = 32
N_HEADS = 12
ALPHA = 2.0 ** 0.25
LN_EPS = 1e-5
NEG_INF = -1e30
SCALE = HD ** -0.5
ADA_SHARD = 3 * D // N_DEV

CB_Q, CB_K, CB_V, CB_GA = 0, 3, 6, 9
KB_U, KB_BG, KB_CG, KB_GC, KB_MA, KB_MC = 5, 6, 7, 8, 9, 10

ADAM_LR, ADAM_B1, ADAM_B2, ADAM_EPS, ADAM_WD, ADAM_STEP = 0.001, 0.9, 0.999, 1e-08, 0.01, 10

VMEM_LIMIT = 56 * 1024 * 1024


def _dot(a, b):
    return jnp.dot(a, b, preferred_element_type=F32)


def _dot_nt(a, b):
    return lax.dot_general(a, b, (((1,), (1,)), ((), ())), preferred_element_type=F32)


def _dot_tn(a, b):
    return lax.dot_general(a, b, (((0,), (0,)), ((), ())), preferred_element_type=F32)


def _sigmoid(v):
    return 1.0 / (1.0 + jnp.exp(-v))


def _perm(jj):
    return jnp.where(jj < 9, (jj % 3) * 3 + jj // 3,
           jnp.where(jj < 11, jj + 1,
           jnp.where(jj < 13, jj + 3,
           jnp.where(jj == 13, 9,
           jnp.where(jj < 16, jj - 2, jj)))))


def _my_index():
    return 4 * lax.axis_index("x") + 2 * lax.axis_index("y") + lax.axis_index("c")


def _slot_leading(ref, slot):
    return ref.at[slot]


def _slot_columns(width):
    def f(ref, slot):
        return ref.at[:, pl.ds(pl.multiple_of(slot * width, 128), width)]
    return f


def _all_gather(arrs, out_shapes, slot_fns, name):
    n = len(arrs)

    def body(*refs):
        ins, outs = refs[:n], refs[n:2 * n]
        send_sems, recv_sems, local_sems = refs[2 * n:]
        x, y, c = lax.axis_index("x"), lax.axis_index("y"), lax.axis_index("c")
        me, sibling = (x, y, c), (x, y, 1 - c)
        chips = [(1 - x, y), (x, 1 - y), (1 - x, 1 - y)]

        def blk(a, dev):
            return slot_fns[a](outs[a], 4 * dev[0] + 2 * dev[1] + dev[2])

        def copy(a, k, block, to, src=None):
            dst = blk(a, block)
            return pltpu.make_async_remote_copy(
                src_ref=dst if src is None else src, dst_ref=dst,
                send_sem=send_sems.at[a * 7 + k], recv_sem=recv_sems.at[a * 7 + k],
                device_id=to, device_id_type=MESH)

        mine = [pltpu.make_async_copy(ins[a], blk(a, me), local_sems.at[a]) for a in range(n)]
        for cp in mine:
            cp.start()
        first = []
        for a in range(n):
            first.append(copy(a, 0, me, sibling, src=ins[a]))
            first += [copy(a, 1 + j, me, (*chip, c), src=ins[a]) for j, chip in enumerate(chips)]
        for cp in first:
            cp.start()
        passed = []
        for j, chip in enumerate(chips):
            for a in range(n):
                copy(a, 1 + j, (*chip, c), me).wait_recv()
                fwd = copy(a, 4 + j, (*chip, c), sibling)
                fwd.start()
                passed.append(fwd)
        for a in range(n):
            copy(a, 0, sibling, me).wait_recv()
        for j, chip in enumerate(chips):
            for a in range(n):
                copy(a, 4 + j, (*chip, 1 - c), me).wait_recv()
        for cp in first + passed:
            cp.wait_send()
        for cp in mine:
            cp.wait()

    any_spec = pl.BlockSpec(memory_space=pl.ANY)
    return pl.pallas_call(
        body, name=name,
        out_shape=tuple(out_shapes),
        in_specs=[any_spec] * n,
        out_specs=tuple([any_spec] * n),
        scratch_shapes=[pltpu.SemaphoreType.DMA((7 * n,)), pltpu.SemaphoreType.DMA((7 * n,)),
                        pltpu.SemaphoreType.DMA((n,))],
    )(*arrs)


def _pair_exchange(arrs, shapes4, src_fns, name):
    n = len(arrs)

    def body(*refs):
        ins, own, recv = refs[:n], refs[n:2 * n], refs[2 * n:3 * n]
        send_sems, recv_sems, local_sems = refs[3 * n:]
        x, y, c = lax.axis_index("x"), lax.axis_index("y"), lax.axis_index("c")
        sibling = (x, y, 1 - c)
        local, remote = [], []
        for a in range(n):
            for q in range(4):
                local.append(pltpu.make_async_copy(
                    src_fns[a](ins[a], 2 * q + c), own[a].at[q], local_sems.at[a * 4 + q]))
                remote.append(pltpu.make_async_remote_copy(
                    src_ref=src_fns[a](ins[a], 2 * q + 1 - c), dst_ref=recv[a].at[q],
                    send_sem=send_sems.at[a * 4 + q], recv_sem=recv_sems.at[a * 4 + q],
                    device_id=sibling, device_id_type=MESH))
        for cp in remote + local:
            cp.start()
        for cp in remote + local:
            cp.wait()

    any_spec = pl.BlockSpec(memory_space=pl.ANY)
    outs = pl.pallas_call(
        body, name=name,
        out_shape=tuple(shapes4) * 2,
        in_specs=[any_spec] * n,
        out_specs=tuple([any_spec] * (2 * n)),
        scratch_shapes=[pltpu.SemaphoreType.DMA((4 * n,))] * 3,
    )(*arrs)
    return outs[:n], outs[n:]


def _chip_exchange(arrs, name):
    n = len(arrs)

    def body(*refs):
        ins, outs = refs[:n], refs[n:2 * n]
        send_sems, recv_sems, local_sems = refs[2 * n:]
        x, y, c = lax.axis_index("x"), lax.axis_index("y"), lax.axis_index("c")
        my_chip = 2 * x + y

        def peer_of(k):
            return ((1 - x) if (k >> 1) & 1 else x, (1 - y) if k & 1 else y, c)

        def copy(a, k, out_chip):
            peer = peer_of(k)
            return pltpu.make_async_remote_copy(
                src_ref=ins[a].at[2 * peer[0] + peer[1]], dst_ref=outs[a].at[out_chip],
                send_sem=send_sems.at[a * 3 + k - 1], recv_sem=recv_sems.at[a * 3 + k - 1],
                device_id=peer, device_id_type=MESH)

        mine = [pltpu.make_async_copy(ins[a].at[my_chip], outs[a].at[my_chip], local_sems.at[a])
                for a in range(n)]
        sends = [copy(a, k, my_chip) for k in range(1, 4) for a in range(n)]
        for cp in sends + mine:
            cp.start()
        for k in range(1, 4):
            peer = peer_of(k)
            for a in range(n):
                copy(a, k, 2 * peer[0] + peer[1]).wait_recv()
        for cp in sends:
            cp.wait_send()
        for cp in mine:
            cp.wait()

    any_spec = pl.BlockSpec(memory_space=pl.ANY)
    return pl.pallas_call(
        body, name=name,
        out_shape=tuple(jax.ShapeDtypeStruct(a.shape, a.dtype) for a in arrs),
        in_specs=[any_spec] * n,
        out_specs=tuple([any_spec] * n),
        scratch_shapes=[pltpu.SemaphoreType.DMA((3 * n,)), pltpu.SemaphoreType.DMA((3 * n,)),
                        pltpu.SemaphoreType.DMA((n,))],
    )(*arrs)


def _pair_add(a, b, row_tile, name):
    _, rows, cols = a.shape
    tr = rows if row_tile is None else row_tile

    def body(a_ref, b_ref, o_ref):
        o_ref[...] = (a_ref[...].astype(F32) + b_ref[...].astype(F32)).astype(o_ref.dtype)

    blk = pl.BlockSpec((None, tr, cols), lambda q, i: (q, i, 0))
    return pl.pallas_call(
        body, name=name,
        grid=(4, rows // tr),
        in_specs=[blk, blk], out_specs=blk,
        out_shape=jax.ShapeDtypeStruct(a.shape, a.dtype),
    )(a, b)


def _ada_fwd(c_all, w_ada, b_cols):
    def body(c_ref, w_ref, b_ref, o_ref):
        cv = c_ref[...]
        sc = cv * _sigmoid(cv)
        o_ref[...] = jnp.dot(sc, w_ref[...], preferred_element_type=F32,
                             precision=lax.Precision.HIGHEST) + b_ref[...]

    return pl.pallas_call(
        body, name="ada_fwd",
        out_shape=jax.ShapeDtypeStruct((c_all.shape[0], w_ada.shape[1]), F32),
    )(c_all, w_ada, b_cols)


def _ada_bwd(c_all_t, dmod_cols):
    def body(c_ref, d_ref, o_ref):
        cv = c_ref[...]
        sc = cv * _sigmoid(cv)
        o_ref[...] = jnp.dot(sc, d_ref[...], preferred_element_type=F32,
                             precision=lax.Precision.HIGHEST)

    return pl.pallas_call(
        body, name="ada_bwd",
        out_shape=jax.ShapeDtypeStruct((c_all_t.shape[0], dmod_cols.shape[1]), F32),
    )(c_all_t, dmod_cols)


def _prep_h(x2, mod3):
    ts = 512
    per_seq = S // ts

    def body(x_ref, mod_ref, h_ref):
        shift = mod_ref[0, 0:1, :]
        scale = mod_ref[0, 1:2, :]
        h_ref[...] = (x_ref[...] * (1.0 + scale) + shift).astype(BF16)

    return pl.pallas_call(
        body, name="prep_h",
        grid=(T // ts,),
        in_specs=[pl.BlockSpec((ts, D), lambda i: (i, 0)),
                  pl.BlockSpec((1, 3, D), lambda i: (i // per_seq, 0, 0))],
        out_specs=pl.BlockSpec((ts, D), lambda i: (i, 0)),
        out_shape=jax.ShapeDtypeStruct((T, D), BF16),
    )(x2, mod3)


def _proj(h, w_in_full):
    tm, tn = 512, SHARD

    def body(h_ref, w_ref, o_ref):
        o_ref[...] = _dot(h_ref[...], w_ref[...]).astype(BF16)

    return pl.pallas_call(
        body, name="proj",
        grid=(NCOL // tn, T // tm),
        in_specs=[pl.BlockSpec((tm, D), lambda j, i: (i, 0)),
                  pl.BlockSpec((D, tn), lambda j, i: (0, j))],
        out_specs=pl.BlockSpec((tm, tn), lambda j, i: (i, j)),
        out_shape=jax.ShapeDtypeStruct((T, NCOL), BF16),
        compiler_params=pltpu.CompilerParams(vmem_limit_bytes=VMEM_LIMIT),
    )(h, w_in_full)


def _bucket_maps():
    a = np.arange(QB)[:, None]
    b = np.arange(2 * QB)[None, :]
    steps = a + QB - b
    maps = []
    for dil in DILATIONS:
        dist = np.maximum(steps, 0) * dil
        nf = np.maximum(dist, 1).astype(np.float32)
        large = 16 + (np.log(nf / np.float32(16)) / np.float32(math.log(128.0))
                      * np.float32(16)).astype(np.int32)
        large = np.minimum(large, N_BUCKETS - 1)
        maps.append(np.where(dist < 16, dist, large).astype(np.int32))
    band = (steps >= 0) & (steps <= N_STEPS)
    first = band & (b >= QB)
    masks = np.stack([first, band]).astype(np.int32)
    return np.stack(maps), masks


def _bias_expand(rel_bias, buckets, masks):
    def body(tab_ref, bk_ref, mk_ref, o_ref):
        for g in range(3):
            bk = bk_ref[g]
            for h in range(4):
                col = 4 * g + h
                val = jnp.zeros((QB, 2 * QB), F32)
                for k in range(N_BUCKETS):
                    val = jnp.where(bk == k, tab_ref[k, col], val)
                o_ref[g, 0, h] = jnp.where(mk_ref[0] != 0, val, NEG_INF)
                o_ref[g, 1, h] = jnp.where(mk_ref[1] != 0, val, NEG_INF)

    return pl.pallas_call(
        body, name="bias_expand",
        in_specs=[pl.BlockSpec(memory_space=pltpu.SMEM),
                  pl.BlockSpec(memory_space=pltpu.VMEM),
                  pl.BlockSpec(memory_space=pltpu.VMEM)],
        out_shape=jax.ShapeDtypeStruct((3, 2, 4, QB, 2 * QB), F32),
    )(rel_bias, buckets, masks)


def _bias_grad(ds1, ds2, ds3, buckets):
    def body(d1_ref, d2_ref, d3_ref, bk_ref, o_ref):
        for g, d_ref in enumerate((d1_ref, d2_ref, d3_ref)):
            bk = bk_ref[g]
            for h in range(4):
                dv = d_ref[h]
                for k in range(N_BUCKETS):
                    o_ref[k, 4 * g + h] = jnp.sum(jnp.where(bk == k, dv, 0.0))

    return pl.pallas_call(
        body, name="bias_grad",
        in_specs=[pl.BlockSpec(memory_space=pltpu.VMEM)] * 4,
        out_specs=pl.BlockSpec(memory_space=pltpu.SMEM),
        out_shape=jax.ShapeDtypeStruct((N_BUCKETS, N_HEADS), F32),
    )(ds1, ds2, ds3, buckets)


def _attn_fwd(proj, bias, g):
    dil = DILATIONS[g]
    rows = QB * dil
    nsb = S // rows
    has_prev = nsb > 1

    def residue(r):
        return pl.ds(r, QB, stride=dil) if dil > 1 else pl.ds(0, QB)

    def body(*refs):
        if has_prev:
            (q_ref, kc_ref, vc_ref, kp_ref, vp_ref, b_ref, o_ref, l_ref,
             qs, kcs, vcs, ao, ls, kps, vps) = refs
        else:
            q_ref, kc_ref, vc_ref, b_ref, o_ref, l_ref, qs, kcs, vcs, ao, ls = refs
        lane = lax.broadcasted_iota(jnp.int32, (QB, 128), 1)
        for h in range(4):
            sl = slice(h * HD, (h + 1) * HD)
            qs[...] = q_ref[:, sl].astype(F32)
            kcs[...] = kc_ref[:, sl].astype(F32)
            vcs[...] = vc_ref[:, sl].astype(F32)
            if has_prev:
                kps[...] = kp_ref[:, sl].astype(F32)
                vps[...] = vp_ref[:, sl].astype(F32)
            for r in range(dil):
                rr = residue(r)
                q = qs[rr, :].astype(BF16)
                s2 = _dot_nt(q, kcs[rr, :].astype(BF16)) * SCALE + b_ref[h, :, QB:]
                m = jnp.max(s2, axis=1, keepdims=True)
                if has_prev:
                    s1 = _dot_nt(q, kps[rr, :].astype(BF16)) * SCALE + b_ref[h, :, :QB]
                    m = jnp.maximum(m, jnp.max(s1, axis=1, keepdims=True))
                p2 = jnp.exp(s2 - m)
                l = jnp.sum(p2, axis=1, keepdims=True)
                o = _dot(p2.astype(BF16), vcs[rr, :].astype(BF16))
                if has_prev:
                    p1 = jnp.exp(s1 - m)
                    l = l + jnp.sum(p1, axis=1, keepdims=True)
                    o = o + _dot(p1.astype(BF16), vps[rr, :].astype(BF16))
                ao[rr, :] = o / l
                blk = slice(r * QB, (r + 1) * QB)
                lse = m + jnp.log(l)
                ls[blk, :] = jnp.where(lane == h, lse, 0.0 if h == 0 else ls[blk, :])
            o_ref[:, sl] = ao[...]
        for r in range(dil):
            l_ref[residue(r), :] = ls[r * QB:(r + 1) * QB, :]

    def row(b, n):
        return b * nsb + n

    def prev(b, n):
        return b * nsb + jnp.maximum(n - 1, 0)

    in_specs = [
        pl.BlockSpec((rows, GW), lambda b, n: (row(b, n), CB_Q + g)),
        pl.BlockSpec((rows, GW), lambda b, n: (row(b, n), CB_K + g)),
        pl.BlockSpec((rows, GW), lambda b, n: (row(b, n), CB_V + g)),
    ]
    args = [proj, proj, proj]
    scratch = [pltpu.VMEM((rows, HD), F32)] * 4 + [pltpu.VMEM((rows, 128), F32)]
    if has_prev:
        in_specs += [pl.BlockSpec((rows, GW), lambda b, n: (prev(b, n), CB_K + g)),
                     pl.BlockSpec((rows, GW), lambda b, n: (prev(b, n), CB_V + g))]
        args += [proj, proj]
        scratch += [pltpu.VMEM((rows, HD), F32)] * 2
    in_specs.append(pl.BlockSpec((None, None, 4, QB, 2 * QB),
                                 lambda b, n: (g, jnp.minimum(n, 1), 0, 0, 0)))
    args.append(bias)
    return pl.pallas_call(
        body, name=f"attn_fwd{g}",
        grid=(BL, nsb),
        in_specs=in_specs,
        out_specs=(pl.BlockSpec((rows, GW), lambda b, n: (row(b, n), 0)),
                   pl.BlockSpec((rows, 128), lambda b, n: (row(b, n), 0))),
        out_shape=(jax.ShapeDtypeStruct((T, GW), F32), jax.ShapeDtypeStruct((T, 128), F32)),
        scratch_shapes=scratch,
        compiler_params=pltpu.CompilerParams(vmem_limit_bytes=VMEM_LIMIT),
    )(*args)


def _attn_bwd(proj, d_out, stats, bias, dqkv, g):
    dil = DILATIONS[g]
    rows = QB * dil
    nsb = S // rows
    has_prev = nsb > 1
    n_steps = nsb + 1 if has_prev else 1
    n_in = 6 + (2 if has_prev else 0) + (0 if dqkv is None else 1)

    def residue(r):
        return pl.ds(r, QB, stride=dil) if dil > 1 else pl.ds(0, QB)

    def body(*refs):
        q_ref, kc_ref, vc_ref, do_ref, st_ref, b_ref = refs[:6]
        if has_prev:
            kp_ref, vp_ref = refs[6:8]
        out_ref, db_ref = refs[n_in], refs[n_in + 1]
        scr = refs[n_in + 2:]
        qs, kcs, vcs, dos, sts, aq, ak, av = scr[:8]
        if has_prev:
            kps, vps, carry = scr[8:]
        b, n = pl.program_id(0), pl.program_id(1)

        @pl.when((b == 0) & (n == 0))
        def _():
            db_ref[...] = jnp.zeros_like(db_ref)

        def flush_head(h):
            out_ref[:, h * HD:(h + 1) * HD] = aq[...].astype(BF16)
            out_ref[:, GW + h * HD:GW + (h + 1) * HD] = ak[...].astype(BF16)
            out_ref[:, 2 * GW + h * HD:2 * GW + (h + 1) * HD] = av[...].astype(BF16)

        if has_prev:
            @pl.when(n == nsb)
            def _():
                for h in range(4):
                    for r in range(dil):
                        blk = slice(r * QB, (r + 1) * QB)
                        aq[residue(r), :] = carry[blk, h * HD:(h + 1) * HD]
                        ak[residue(r), :] = carry[blk, GW + h * HD:GW + (h + 1) * HD]
                        av[residue(r), :] = carry[blk, 2 * GW + h * HD:2 * GW + (h + 1) * HD]
                    flush_head(h)

        @pl.when(n < nsb)
        def _():
            for r in range(dil):
                sts[r * QB:(r + 1) * QB, :] = st_ref[residue(r), :]
            for h in range(4):
                sl = slice(h * HD, (h + 1) * HD)
                qs[...] = q_ref[:, sl].astype(F32)
                kcs[...] = kc_ref[:, sl].astype(F32)
                vcs[...] = vc_ref[:, sl].astype(F32)
                dos[...] = do_ref[:, sl].astype(F32)
                if has_prev:
                    kps[...] = kp_ref[:, sl].astype(F32)
                    vps[...] = vp_ref[:, sl].astype(F32)
                for r in range(dil):
                    rr = residue(r)
                    blk = slice(r * QB, (r + 1) * QB)
                    q, kc, vc = qs[rr, :].astype(BF16), kcs[rr, :].astype(BF16), vcs[rr, :].astype(BF16)
                    do = dos[rr, :].astype(BF16)
                    lse = sts[blk, h:h + 1]
                    delta = sts[blk, 4 + h:5 + h]
                    s2 = _dot_nt(q, kc) * SCALE + b_ref[h, :, QB:]
                    p2 = jnp.exp(s2 - lse)
                    ds2 = p2 * (_dot_nt(do, vc) - delta)
                    db_ref[h, :, QB:] += ds2
                    ds2b, p2b = ds2.astype(BF16), p2.astype(BF16)
                    dq = _dot(ds2b, kc)
                    dk_cur = _dot_tn(ds2b, q) * SCALE
                    dv_cur = _dot_tn(p2b, do)
                    if has_prev:
                        kp, vp = kps[rr, :].astype(BF16), vps[rr, :].astype(BF16)
                        s1 = _dot_nt(q, kp) * SCALE + b_ref[h, :, :QB]
                        p1 = jnp.exp(s1 - lse)
                        ds1 = p1 * (_dot_nt(do, vp) - delta)
                        db_ref[h, :, :QB] += ds1
                        ds1b, p1b = ds1.astype(BF16), p1.astype(BF16)
                        dq = dq + _dot(ds1b, kp)
                        dk_prev = _dot_tn(ds1b, q) * SCALE
                        dv_prev = _dot_tn(p1b, do)
                        cq = slice(h * HD, (h + 1) * HD)
                        ck = slice(GW + h * HD, GW + (h + 1) * HD)
                        cv = slice(2 * GW + h * HD, 2 * GW + (h + 1) * HD)

                        @pl.when(n > 0)
                        def _():
                            aq[rr, :] = carry[blk, cq]
                            ak[rr, :] = carry[blk, ck] + dk_prev
                            av[rr, :] = carry[blk, cv] + dv_prev

                        carry[blk, cq] = dq * SCALE
                        carry[blk, ck] = dk_cur
                        carry[blk, cv] = dv_cur
                    else:
                        aq[rr, :] = dq * SCALE
                        ak[rr, :] = dk_cur
                        av[rr, :] = dv_cur
                if has_prev:
                    @pl.when(n > 0)
                    def _():
                        flush_head(h)
                else:
                    flush_head(h)

    def row(b, n):
        return b * nsb + jnp.minimum(n, nsb - 1)

    def prev(b, n):
        return b * nsb + jnp.maximum(jnp.minimum(n, nsb - 1) - 1, 0)

    def done(b, n):
        return b * nsb + jnp.maximum(n - 1, 0)

    in_specs = [
        pl.BlockSpec((rows, GW), lambda b, n: (row(b, n), CB_Q + g)),
        pl.BlockSpec((rows, GW), lambda b, n: (row(b, n), CB_K + g)),
        pl.BlockSpec((rows, GW), lambda b, n: (row(b, n), CB_V + g)),
        pl.BlockSpec((rows, GW), lambda b, n: (row(b, n), 0)),
        pl.BlockSpec((rows, 128), lambda b, n: (row(b, n), 0)),
        pl.BlockSpec((None, None, 4, QB, 2 * QB),
                     lambda b, n: (g, jnp.minimum(jnp.minimum(n, nsb - 1), 1), 0, 0, 0)),
    ]
    args = [proj, proj, proj, d_out, stats, bias]
    scratch = [pltpu.VMEM((rows, HD), F32)] * 4 + [pltpu.VMEM((rows, 128), F32)] \
        + [pltpu.VMEM((rows, HD), F32)] * 3
    if has_prev:
        in_specs += [pl.BlockSpec((rows, GW), lambda b, n: (prev(b, n), CB_K + g)),
                     pl.BlockSpec((rows, GW), lambda b, n: (prev(b, n), CB_V + g))]
        args += [proj, proj]
        scratch += [pltpu.VMEM((rows, HD), F32)] * 2 + [pltpu.VMEM((rows, 3 * GW), F32)]
    aliases = {}
    if dqkv is not None:
        in_specs.append(pl.BlockSpec(memory_space=pl.ANY))
        args.append(dqkv)
        aliases = {len(args) - 1: 0}
    return pl.pallas_call(
        body, name=f"attn_bwd{g}",
        grid=(BL, n_steps),
        in_specs=in_specs,
        out_specs=(pl.BlockSpec((rows, 3 * GW), lambda b, n: (done(b, n), g)),
                   pl.BlockSpec((4, QB, 2 * QB), lambda b, n: (0, 0, 0))),
        out_shape=(jax.ShapeDtypeStruct((T, 9 * GW), BF16),
                   jax.ShapeDtypeStruct((4, QB, 2 * QB), F32)),
        scratch_shapes=scratch,
        input_output_aliases=aliases,
        compiler_params=pltpu.CompilerParams(vmem_limit_bytes=VMEM_LIMIT),
    )(*args)


def _tail(x2, tgt2, mod3, o_g, lse_g, proj, w_ao, w_co, w_o, conv_w, conv_b, ln_g, ln_b):
    tm = 256
    per_seq = S // tm
    halo = 16

    def body(x_ref, t_ref, mod_ref, o1_ref, o2_ref, o3_ref, l1_ref, l2_ref, l3_ref,
             ga_ref, u_ref, bg_ref, cg_ref, gc_ref, ma_ref, mc_ref, up_ref, cp_ref,
             wao_ref, wco_ref, wo_ref, cw_ref, cb_ref, lg_ref, lb_ref,
             dtail_ref, dyc_ref, do_ref, st_ref, dxd_ref,
             mg_ref, dy_ref, ain_ref, dao_ref, sin_ref, dso_ref, vec_ref):
        i = pl.program_id(0)
        bidx = i // per_seq
        first = (i % per_seq) == 0

        @pl.when(i == 0)
        def _():
            vec_ref[...] = jnp.zeros_like(vec_ref)

        l1, l2, l3 = l1_ref[...], l2_ref[...], l3_ref[...]
        mx = jnp.maximum(jnp.maximum(l1, l2), l3)
        e1, e2, e3 = jnp.exp(l1 - mx), jnp.exp(l2 - mx), jnp.exp(l3 - mx)
        esum = e1 + e2 + e3
        lse_tot = mx + jnp.log(esum)
        w1, w2, w3 = e1 / esum, e2 / esum, e3 / esum

        def per_head(wv):
            return jnp.concatenate([jnp.broadcast_to(wv[:, h:h + 1], (tm, HD)) for h in range(4)], axis=1)

        o = per_head(w1) * o1_ref[...] + per_head(w2) * o2_ref[...] + per_head(w3) * o3_ref[...]

        ga = ga_ref[...].astype(F32)
        sig_ga = _sigmoid(ga)
        silu_ga = ga * sig_ga
        a_in = (o * silu_ga).astype(BF16)
        a_out = _dot(a_in, wao_ref[...])

        u = u_ref[...].astype(F32)
        cg = cg_ref[...].astype(F32)
        z = cg * u
        zp = cp_ref[...].astype(F32) * up_ref[...].astype(F32)
        zp = jnp.where(first, 0.0, zp)
        zcat = jnp.concatenate([zp, z], axis=0)
        z1 = pltpu.roll(zcat, 1, 0)[halo:]
        z2 = pltpu.roll(zcat, 2, 0)[halo:]
        y_conv = cw_ref[0:1, :] * z2 + cw_ref[1:2, :] * z1 + cw_ref[2:3, :] * z + cb_ref[...]
        gc = gc_ref[...].astype(F32)
        sig_gc = _sigmoid(gc)
        silu_gc = gc * sig_gc
        bg = bg_ref[...].astype(F32)
        s_in = (bg * y_conv * silu_gc).astype(BF16)
        s_out = _dot(s_in, wco_ref[...])

        sa = _sigmoid(ma_ref[...].astype(F32))
        sc = _sigmoid(mc_ref[...].astype(F32))
        merged = (sa * a_out + sc * s_out).astype(BF16)
        y = _dot(merged, wo_ref[...])
        gate1 = 1.0 + mod_ref[0, 2:3, :]
        xv = x_ref[...]
        resid = ALPHA * xv + gate1 * y
        mu = jnp.mean(resid, axis=1, keepdims=True)
        xc = resid - mu
        var = jnp.mean(xc * xc, axis=1, keepdims=True)
        rstd = lax.rsqrt(var + LN_EPS)
        xhat = xc * rstd
        lg = lg_ref[...]
        err = xhat * lg + lb_ref[...] - t_ref[...]
        vec_ref[3:4, :] += (0.5 / D) * jnp.sum(err * err, axis=0, keepdims=True)

        dout = err * (1.0 / D)
        vec_ref[1:2, :] += jnp.sum(dout * xhat, axis=0, keepdims=True)
        vec_ref[2:3, :] += jnp.sum(dout, axis=0, keepdims=True)
        dxh = dout * lg
        dres = rstd * (dxh - jnp.mean(dxh, axis=1, keepdims=True)
                       - xhat * jnp.mean(dxh * xhat, axis=1, keepdims=True))
        dxd_ref[...] = ALPHA * dres
        dgate = jnp.sum(dres * y, axis=0, keepdims=True)
        vec_ref[4:5, :] += jnp.where(bidx == 0, dgate, 0.0)
        vec_ref[5:6, :] += jnp.where(bidx == 1, dgate, 0.0)
        dy = (dres * gate1).astype(BF16)

        dmerged = _dot_nt(dy, wo_ref[...])
        da_out = (dmerged * sa).astype(BF16)
        ds_out = (dmerged * sc).astype(BF16)
        dtail_ref[:, 3584:4608] = (dmerged * s_out * sc * (1.0 - sc)).astype(BF16)
        dtail_ref[:, 2560:3584] = (dmerged * a_out * sa * (1.0 - sa)).astype(BF16)
        da_in = _dot_nt(da_out, wao_ref[...])
        ds_in = _dot_nt(ds_out, wco_ref[...])

        d_o = da_in * silu_ga
        do_ref[...] = d_o.astype(BF16)
        dtail_ref[:, 0:512] = (da_in * o * (sig_ga * (1.0 + ga * (1.0 - sig_ga)))).astype(BF16)
        lane = lax.broadcasted_iota(jnp.int32, (tm, 128), 1)
        stats = lse_tot
        od = o * d_o
        for h in range(4):
            delta = jnp.sum(od[:, h * HD:(h + 1) * HD], axis=1, keepdims=True)
            stats = jnp.where(lane == 4 + h, delta, stats)
        st_ref[...] = stats

        dtail_ref[:, 512:1536] = (ds_in * y_conv * silu_gc).astype(BF16)
        dyc = ds_in * bg * silu_gc
        dyc_ref[...] = dyc
        vec_ref[0:1, :] += jnp.sum(dyc, axis=0, keepdims=True)
        dtail_ref[:, 1536:2560] = (ds_in * bg * y_conv * (sig_gc * (1.0 + gc * (1.0 - sig_gc)))).astype(BF16)

        mg_ref[...] = merged
        dy_ref[...] = dy
        ain_ref[...] = a_in
        dao_ref[...] = da_out
        sin_ref[...] = s_in
        dso_ref[...] = ds_out

    def tile(width, cblk=0):
        return pl.BlockSpec((tm, width), lambda i: (i, cblk))

    def whole(shape):
        return pl.BlockSpec(shape, lambda i: tuple(0 for _ in shape))

    prev_rows = lambda i: (jnp.maximum(i * (tm // halo) - 1, 0),)
    in_specs = [
        tile(D), tile(D), pl.BlockSpec((1, 3, D), lambda i: (i // per_seq, 0, 0)),
        tile(GW), tile(GW), tile(GW), tile(128), tile(128), tile(128),
        tile(GW, CB_GA), tile(D, KB_U), tile(D, KB_BG), tile(D, KB_CG), tile(D, KB_GC),
        tile(D, KB_MA), tile(D, KB_MC),
        pl.BlockSpec((halo, D), lambda i: (*prev_rows(i), KB_U)),
        pl.BlockSpec((halo, D), lambda i: (*prev_rows(i), KB_CG)),
        whole((GW, D)), whole((D, D)), whole((D, D)),
        whole((3, D)), whole((1, D)), whole((1, D)), whole((1, D)),
    ]
    out_specs = (
        tile(9 * CB), tile(D), tile(GW), tile(128), tile(D),
        tile(D), tile(D), tile(GW), tile(D), tile(D), tile(D),
        pl.BlockSpec((8, D), lambda i: (0, 0)),
    )
    out_shape = (
        jax.ShapeDtypeStruct((T, 9 * CB), BF16),
        jax.ShapeDtypeStruct((T, D), F32),
        jax.ShapeDtypeStruct((T, GW), BF16),
        jax.ShapeDtypeStruct((T, 128), F32),
        jax.ShapeDtypeStruct((T, D), F32),
        jax.ShapeDtypeStruct((T, D), BF16),
        jax.ShapeDtypeStruct((T, D), BF16),
        jax.ShapeDtypeStruct((T, GW), BF16),
        jax.ShapeDtypeStruct((T, D), BF16),
        jax.ShapeDtypeStruct((T, D), BF16),
        jax.ShapeDtypeStruct((T, D), BF16),
        jax.ShapeDtypeStruct((8, D), F32),
    )
    return pl.pallas_call(
        body, name="tail",
        grid=(T // tm,),
        in_specs=in_specs, out_specs=out_specs, out_shape=out_shape,
        compiler_params=pltpu.CompilerParams(vmem_limit_bytes=VMEM_LIMIT),
    )(x2, tgt2, mod3, *o_g, *lse_g, proj, proj, proj, proj, proj, proj, proj, proj, proj,
      w_ao, w_co, w_o, conv_w, conv_b, ln_g, ln_b)


def _conv_bwd(dyc, proj, conv_w):
    tm = 512
    per_seq = S // tm
    halo = 16

    def body(d_ref, dn_ref, u_ref, c_ref, up_ref, cp_ref, cw_ref, o_ref, g_ref):
        i = pl.program_id(0)
        first = (i % per_seq) == 0
        last = (i % per_seq) == per_seq - 1

        @pl.when(i == 0)
        def _():
            g_ref[...] = jnp.zeros_like(g_ref)

        d = d_ref[...]
        dn = jnp.where(last, 0.0, dn_ref[...])
        dcat = jnp.concatenate([d, dn], axis=0)
        d1 = pltpu.roll(dcat, tm + 8 - 1, 0)[:tm]
        d2 = pltpu.roll(dcat, tm + 8 - 2, 0)[:tm]
        dz = cw_ref[2:3, :] * d + cw_ref[1:2, :] * d1 + cw_ref[0:1, :] * d2
        u = u_ref[...].astype(F32)
        cg = c_ref[...].astype(F32)
        o_ref[:, 0:D] = (dz * cg).astype(BF16)
        o_ref[:, D:2 * D] = (dz * u).astype(BF16)

        z = cg * u
        zp = jnp.where(first, 0.0, cp_ref[...].astype(F32) * up_ref[...].astype(F32))
        zcat = jnp.concatenate([zp, z], axis=0)
        z1 = pltpu.roll(zcat, 1, 0)[halo:]
        z2 = pltpu.roll(zcat, 2, 0)[halo:]
        g_ref[0:1, :] += jnp.sum(d * z2, axis=0, keepdims=True)
        g_ref[1:2, :] += jnp.sum(d * z1, axis=0, keepdims=True)
        g_ref[2:3, :] += jnp.sum(d * z, axis=0, keepdims=True)

    n_tiles = T // tm
    prev_rows = lambda i: jnp.maximum(i * (tm // halo) - 1, 0)
    next_rows = lambda i: jnp.minimum((i + 1) * (tm // 8), T // 8 - 1)
    return pl.pallas_call(
        body, name="conv_bwd",
        grid=(n_tiles,),
        in_specs=[pl.BlockSpec((tm, D), lambda i: (i, 0)),
                  pl.BlockSpec((8, D), lambda i: (next_rows(i), 0)),
                  pl.BlockSpec((tm, D), lambda i: (i, KB_U)),
                  pl.BlockSpec((tm, D), lambda i: (i, KB_CG)),
                  pl.BlockSpec((halo, D), lambda i: (prev_rows(i), KB_U)),
                  pl.BlockSpec((halo, D), lambda i: (prev_rows(i), KB_CG)),
                  pl.BlockSpec((3, D), lambda i: (0, 0))],
        out_specs=(pl.BlockSpec((tm, 2 * D), lambda i: (i, 0)),
                   pl.BlockSpec((8, D), lambda i: (0, 0))),
        out_shape=(jax.ShapeDtypeStruct((T, 2 * D), BF16),
                   jax.ShapeDtypeStruct((8, D), F32)),
        compiler_params=pltpu.CompilerParams(vmem_limit_bytes=VMEM_LIMIT),
    )(dyc, dyc, proj, proj, proj, proj, conv_w)


def _d_specs(tile_rows, row_of, jj_of):
    return [
        pl.BlockSpec((tile_rows, CB), lambda *ids: (row_of(*ids), jnp.minimum(jj_of(*ids), 8))),
        pl.BlockSpec((tile_rows, CB), lambda *ids: (row_of(*ids), jnp.clip(jj_of(*ids) - 9, 0, 3))),
        pl.BlockSpec((tile_rows, CB), lambda *ids: (row_of(*ids), jnp.clip(jj_of(*ids) - 13, 0, 8))),
    ]


def _dh_dx(dqkv, dconv, dtail, w_in_full, x2, dxd, mod3):
    tm = 512
    per_seq = S // tm

    def body(dq_ref, dc_ref, dt_ref, w_ref, x_ref, dxd_ref, mod_ref, gx_ref, vec_ref, acc):
        i, jj = pl.program_id(0), pl.program_id(1)

        @pl.when((i == 0) & (jj == 0))
        def _():
            vec_ref[...] = jnp.zeros_like(vec_ref)

        @pl.when(jj == 0)
        def _():
            acc[...] = jnp.zeros_like(acc)

        @pl.when(jj < 9)
        def _():
            acc[...] += _dot_nt(dq_ref[...], w_ref[...])

        @pl.when((jj >= 9) & (jj < 13))
        def _():
            acc[...] += _dot_nt(dc_ref[...], w_ref[...])

        @pl.when(jj >= 13)
        def _():
            acc[...] += _dot_nt(dt_ref[...], w_ref[...])

        @pl.when(jj == NCB - 1)
        def _():
            dh = acc[...]
            bidx = i // per_seq
            gx_ref[...] = dxd_ref[...] + dh * (1.0 + mod_ref[0, 1:2, :])
            dshift = jnp.sum(dh, axis=0, keepdims=True)
            dscale = jnp.sum(dh * x_ref[...], axis=0, keepdims=True)
            vec_ref[0:1, :] += jnp.where(bidx == 0, dshift, 0.0)
            vec_ref[1:2, :] += jnp.where(bidx == 1, dshift, 0.0)
            vec_ref[2:3, :] += jnp.where(bidx == 0, dscale, 0.0)
            vec_ref[3:4, :] += jnp.where(bidx == 1, dscale, 0.0)

    return pl.pallas_call(
        body, name="dh_dx",
        grid=(T // tm, NCB),
        in_specs=_d_specs(tm, lambda i, jj: i, lambda i, jj: jj) + [
            pl.BlockSpec((D, CB), lambda i, jj: (0, _perm(jj))),
            pl.BlockSpec((tm, D), lambda i, jj: (i, 0)),
            pl.BlockSpec((tm, D), lambda i, jj: (i, 0)),
            pl.BlockSpec((1, 3, D), lambda i, jj: (i // per_seq, 0, 0))],
        out_specs=(pl.BlockSpec((tm, D), lambda i, jj: (i, 0)),
                   pl.BlockSpec((8, D), lambda i, jj: (0, 0))),
        out_shape=(jax.ShapeDtypeStruct((T, D), F32), jax.ShapeDtypeStruct((8, D), F32)),
        scratch_shapes=[pltpu.VMEM((tm, D), F32)],
        compiler_params=pltpu.CompilerParams(vmem_limit_bytes=VMEM_LIMIT),
    )(dqkv, dconv, dtail, w_in_full, x2, dxd, mod3)


def _gw_in(h, dqkv, dconv, dtail):
    tk = 1024

    def body(h_ref, dq_ref, dc_ref, dt_ref, o_ref, acc):
        jj, k = pl.program_id(0), pl.program_id(1)

        @pl.when(k == 0)
        def _():
            acc[...] = jnp.zeros_like(acc)

        @pl.when(jj < 9)
        def _():
            acc[...] += _dot_tn(h_ref[...], dq_ref[...])

        @pl.when((jj >= 9) & (jj < 13))
        def _():
            acc[...] += _dot_tn(h_ref[...], dc_ref[...])

        @pl.when(jj >= 13)
        def _():
            acc[...] += _dot_tn(h_ref[...], dt_ref[...])

        @pl.when(k == T // tk - 1)
        def _():
            o_ref[...] = acc[...].astype(BF16)

    return pl.pallas_call(
        body, name="gw_in",
        grid=(NCB, T // tk),
        in_specs=[pl.BlockSpec((tk, D), lambda jj, k: (k, 0))]
                 + _d_specs(tk, lambda jj, k: k, lambda jj, k: jj),
        out_specs=pl.BlockSpec((D, CB), lambda jj, k: (0, _perm(jj))),
        out_shape=jax.ShapeDtypeStruct((D, NCOL), BF16),
        scratch_shapes=[pltpu.VMEM((D, CB), F32)],
        compiler_params=pltpu.CompilerParams(vmem_limit_bytes=VMEM_LIMIT),
    )(h, dqkv, dconv, dtail)


def _mm_tn(a, b, tn, blocks_leading, name):
    kk, m = a.shape
    n = b.shape[1]
    tk = 1024

    def body(a_ref, b_ref, o_ref, acc):
        @pl.when(pl.program_id(1) == 0)
        def _():
            acc[...] = jnp.zeros_like(acc)

        acc[...] += _dot_tn(a_ref[...], b_ref[...])

        @pl.when(pl.program_id(1) == kk // tk - 1)
        def _():
            o_ref[...] = acc[...].astype(BF16)

    if blocks_leading:
        out_spec = pl.BlockSpec((None, m, tn), lambda j, k: (j, 0, 0))
        out_shape = jax.ShapeDtypeStruct((n // tn, m, tn), BF16)
    else:
        out_spec = pl.BlockSpec((m, tn), lambda j, k: (0, j))
        out_shape = jax.ShapeDtypeStruct((m, n), BF16)
    return pl.pallas_call(
        body, name=name,
        grid=(n // tn, kk // tk),
        in_specs=[pl.BlockSpec((tk, m), lambda j, k: (k, 0)),
                  pl.BlockSpec((tk, tn), lambda j, k: (k, j))],
        out_specs=out_spec, out_shape=out_shape,
        scratch_shapes=[pltpu.VMEM((m, tn), F32)],
        compiler_params=pltpu.CompilerParams(vmem_limit_bytes=VMEM_LIMIT),
    )(a, b)


def _adamw(parts, w, m, v, name, row_tile=None):
    n_parts, rows, cols = parts.shape
    tr = rows if row_tile is None else row_tile
    c1 = 1.0 - ADAM_B1 ** ADAM_STEP
    c2 = 1.0 - ADAM_B2 ** ADAM_STEP

    def body(p_ref, w_ref, m_ref, v_ref, g_ref, d_ref, nm_ref, nv_ref):
        g = p_ref[0].astype(F32)
        for s in range(1, n_parts):
            g = g + p_ref[s].astype(F32)
        nm = ADAM_B1 * m_ref[...] + (1.0 - ADAM_B1) * g
        nv = ADAM_B2 * v_ref[...] + (1.0 - ADAM_B2) * (g * g)
        m_hat = nm / c1
        v_hat = nv / c2
        g_ref[...] = g
        d_ref[...] = -ADAM_LR * (m_hat / (jnp.sqrt(v_hat) + ADAM_EPS) + ADAM_WD * w_ref[...])
        nm_ref[...] = nm
        nv_ref[...] = nv

    blk = pl.BlockSpec((tr, cols), lambda i: (i, 0))
    shp = jax.ShapeDtypeStruct((rows, cols), F32)
    return pl.pallas_call(
        body, name=name,
        grid=(rows // tr,),
        in_specs=[pl.BlockSpec((n_parts, tr, cols), lambda i: (0, i, 0)), blk, blk, blk],
        out_specs=(blk, blk, blk, blk),
        out_shape=(shp, shp, shp, shp),
        compiler_params=pltpu.CompilerParams(vmem_limit_bytes=VMEM_LIMIT),
    )(parts, w, m, v)


def _loss_sum(rows):
    def body(r_ref, o_ref):
        o_ref[...] = jnp.sum(jnp.sum(r_ref[...], axis=0, keepdims=True), axis=1, keepdims=True)

    return pl.pallas_call(body, name="loss_sum", out_shape=jax.ShapeDtypeStruct((1, 1), F32))(rows)


def _local_step(x2, tgt2, mod3, w_in_full, w_ao, w_co, w_o, conv_w, conv_b, rel_bias, ln_g, ln_b):
    buckets_np, masks_np = _bucket_maps()
    buckets, masks = jnp.asarray(buckets_np), jnp.asarray(masks_np)

    h = _prep_h(x2, mod3)
    proj = _proj(h, w_in_full)
    bias = _bias_expand(rel_bias, buckets, masks)
    fwd = [_attn_fwd(proj, bias, g) for g in range(3)]
    o_g = [f[0] for f in fwd]
    lse_g = [f[1] for f in fwd]

    (dtail, dyc, d_o, stats, dxd, merged, dy, a_in, da_out, s_in, ds_out, tail_vec) = _tail(
        x2, tgt2, mod3, o_g, lse_g, proj, w_ao, w_co, w_o, conv_w, conv_b, ln_g, ln_b)

    dqkv = None
    dbias = []
    for g in range(3):
        dqkv, db = _attn_bwd(proj, d_o, stats, bias, dqkv, g)
        dbias.append(db)
    g_rel_bias = _bias_grad(*dbias, buckets)
    dconv, conv_vec = _conv_bwd(dyc, proj, conv_w)

    grad_x, mod_vec = _dh_dx(dqkv, dconv, dtail, w_in_full, x2, dxd, mod3)
    gw_in = _gw_in(h, dqkv, dconv, dtail)
    gw_o = _mm_tn(merged, dy, 512, False, "gw_o")
    gw_co = _mm_tn(s_in, ds_out, 512, False, "gw_conv_out")
    gw_ao = _mm_tn(a_in, da_out, 128, True, "gw_attn_out")
    return grad_x, gw_in, gw_ao, gw_co, gw_o, conv_vec, g_rel_bias, tail_vec, mod_vec


def kernel(x, c, w_ada, b_ada, w_in, conv_w, conv_b, rel_bias, w_attn_out, w_conv_out, w_o, ln_g, ln_b, loss_target, m_w_ada, m_b_ada, m_w_in, m_conv_w, m_conv_b, m_rel_bias, m_w_attn_out, m_w_conv_out, m_w_o, m_ln_g, m_ln_b, v_w_ada, v_b_ada, v_w_in, v_conv_w, v_conv_b, v_rel_bias, v_w_attn_out, v_w_conv_out, v_w_o, v_ln_g, v_ln_b):
    me = _my_index()
    x2 = x.reshape(T, D)
    tgt2 = loss_target.reshape(T, D)

    w_in_full, w_ao_g, w_co_g, w_o_g, conv_w_g, c_g = _all_gather(
        [w_in[0].astype(BF16), w_attn_out[0].astype(BF16), w_conv_out[0].astype(BF16),
         w_o[0].astype(BF16), conv_w[0], c],
        [jax.ShapeDtypeStruct((D, NCOL), BF16),
         jax.ShapeDtypeStruct((N_DEV, GW, D // N_DEV), BF16),
         jax.ShapeDtypeStruct((N_DEV, D // N_DEV, D), BF16),
         jax.ShapeDtypeStruct((N_DEV, D // N_DEV, D), BF16),
         jax.ShapeDtypeStruct((N_DEV, 3, D // N_DEV), F32),
         jax.ShapeDtypeStruct((N_DEV, BL, D), F32)],
        [_slot_columns(SHARD)] + [_slot_leading] * 5,
        "gather_weights")
    w_ao_full = jnp.transpose(w_ao_g, (1, 0, 2)).reshape(GW, D)
    w_co_full = w_co_g.reshape(D, D)
    w_o_full = w_o_g.reshape(D, D)
    conv_w_full = jnp.transpose(conv_w_g, (1, 0, 2)).reshape(3, D)
    c_all = c_g.reshape(N_DEV * BL, D)

    b_cols = lax.dynamic_slice(b_ada, (0, me * ADA_SHARD), (1, ADA_SHARD))
    mod_cols = _ada_fwd(c_all, w_ada[0], b_cols)
    (mod_g,) = _all_gather([mod_cols], [jax.ShapeDtypeStruct((N_DEV, N_DEV * BL, ADA_SHARD), F32)],
                           [_slot_leading], "gather_mod")
    mod_all = jnp.transpose(mod_g, (1, 0, 2)).reshape(N_DEV * BL, 3 * D)
    mod3 = lax.dynamic_slice(mod_all, (me * BL, 0), (BL, 3 * D)).reshape(BL, 3, D)

    (grad_x, gw_in, gw_ao, gw_co, gw_o, conv_vec, g_rel_bias, tail_vec, mod_vec) = _local_step(
        x2, tgt2, mod3, w_in_full, w_ao_full, w_co_full, w_o_full,
        conv_w_full, conv_b, rel_bias, ln_g, ln_b)

    g_conv_w_blocks = jnp.transpose(conv_vec[0:3].reshape(3, N_DEV, D // N_DEV), (1, 0, 2))
    own, sib = _pair_exchange(
        [gw_in, gw_ao, gw_co.reshape(N_DEV, D // N_DEV, D), gw_o.reshape(N_DEV, D // N_DEV, D),
         g_conv_w_blocks],
        [jax.ShapeDtypeStruct((4, D, SHARD), BF16),
         jax.ShapeDtypeStruct((4, GW, D // N_DEV), BF16),
         jax.ShapeDtypeStruct((4, D // N_DEV, D), BF16),
         jax.ShapeDtypeStruct((4, D // N_DEV, D), BF16),
         jax.ShapeDtypeStruct((4, 3, D // N_DEV), F32)],
        [_slot_columns(SHARD)] + [_slot_leading] * 4,
        "pair_grads")
    tiles = [256, None, None, None, None]
    names = ["w_in", "w_attn_out", "w_conv_out", "w_o", "conv_w"]
    chip_sums = [_pair_add(own[a], sib[a], tiles[a], "pair_add_" + names[a]) for a in range(5)]
    r_in, r_ao, r_co, r_o, r_cw = _chip_exchange(chip_sums, "chip_grads")

    small = jnp.concatenate([
        tail_vec[0:4],
        jnp.pad(g_rel_bias.reshape(1, N_BUCKETS * N_HEADS), ((0, 0), (0, D - N_BUCKETS * N_HEADS))),
        jnp.zeros((3, D), F32)], axis=0)
    dmod = jnp.concatenate([mod_vec[0:2], mod_vec[2:4], tail_vec[4:6]], axis=1)
    small_g, dmod_g = _all_gather(
        [small, dmod],
        [jax.ShapeDtypeStruct((N_DEV, 8, D), F32), jax.ShapeDtypeStruct((N_DEV, BL, 3 * D), F32)],
        [_slot_leading] * 2, "gather_small")
    dmod_all = dmod_g.reshape(N_DEV * BL, 3 * D)
    loss = _loss_sum(small_g[:, 3, :]).reshape(())
    g_w_ada = _ada_bwd(jnp.transpose(c_all), lax.dynamic_slice(dmod_all, (0, me * ADA_SHARD),
                                                               (N_DEV * BL, ADA_SHARD)))

    def upd(parts, w, m, v, name, row_tile=None):
        shape = w.shape
        w2, m2, v2 = (t.reshape(parts.shape[1:]) for t in (w, m, v))
        return tuple(t.reshape(shape) for t in _adamw(parts, w2, m2, v2, name, row_tile))

    res = {
        "w_ada": upd(g_w_ada[None], w_ada, m_w_ada, v_w_ada, "adam_w_ada", 256),
        "b_ada": upd(dmod_all[:, None, :], b_ada, m_b_ada, v_b_ada, "adam_b_ada"),
        "w_in": upd(r_in, w_in, m_w_in, v_w_in, "adam_w_in", 128),
        "conv_w": upd(r_cw, conv_w, m_conv_w, v_conv_w, "adam_conv_w"),
        "conv_b": upd(small_g[:, 0:1, :], conv_b, m_conv_b, v_conv_b, "adam_conv_b"),
        "rel_bias": upd(small_g[:, 4, :N_BUCKETS * N_HEADS].reshape(N_DEV, N_BUCKETS, N_HEADS),
                        rel_bias, m_rel_bias, v_rel_bias, "adam_rel_bias"),
        "w_attn_out": upd(r_ao, w_attn_out, m_w_attn_out, v_w_attn_out, "adam_w_attn_out"),
        "w_conv_out": upd(r_co, w_conv_out, m_w_conv_out, v_w_conv_out, "adam_w_conv_out"),
        "w_o": upd(r_o, w_o, m_w_o, v_w_o, "adam_w_o"),
        "ln_g": upd(small_g[:, 1:2, :], ln_g, m_ln_g, v_ln_g, "adam_ln_g"),
        "ln_b": upd(small_g[:, 2:3, :], ln_b, m_ln_b, v_ln_b, "adam_ln_b"),
    }
    order = ["w_ada", "b_ada", "w_in", "conv_w", "conv_b", "rel_bias", "w_attn_out", "w_conv_out",
             "w_o", "ln_g", "ln_b"]
    outs = [loss, grad_x.reshape(BL, S, D)]
    for k in range(4):
        outs += [res[name][k] for name in order]
    return tuple(outs)
```

```python
import functools
import math

import numpy as np
import jax
import jax.numpy as jnp
from jax import lax
from jax.experimental import pallas as pl
from jax.experimental.pallas import tpu as pltpu

F32 = jnp.float32
BF16 = jnp.bfloat16
MESH = pl.DeviceIdType.MESH

N_DEV = 8
D = 1024
S = 2048
BL = 2
T = BL * S
NCOL = 11264
SHARD = NCOL // N_DEV
CB = 512
NCB = NCOL // CB
HD = 128
GW = 512
QB = 128
DILATIONS = (1, 4, 16)
N_STEPS = 128
N_BUCKETS = 32
N_HEADS = 12
ALPHA = 2.0 ** 0.25
LN_EPS = 1e-5
NEG_INF = -1e30
SCALE = HD ** -0.5
ADA_SHARD = 3 * D // N_DEV

CB_Q, CB_K, CB_V, CB_GA = 0, 3, 6, 9
KB_U, KB_BG, KB_CG, KB_GC, KB_MA, KB_MC = 5, 6, 7, 8, 9, 10

ADAM_LR, ADAM_B1, ADAM_B2, ADAM_EPS, ADAM_WD, ADAM_STEP = 0.001, 0.9, 0.999, 1e-08, 0.01, 10

VMEM_LIMIT = 56 * 1024 * 1024


def _dot(a, b):
    return jnp.dot(a, b, preferred_element_type=F32)


def _dot_nt(a, b):
    return lax.dot_general(a, b, (((1,), (1,)), ((), ())), preferred_element_type=F32)


def _dot_tn(a, b):
    return lax.dot_general(a, b, (((0,), (0,)), ((), ())), preferred_element_type=F32)


def _sigmoid(v):
    return 1.0 / (1.0 + jnp.exp(-v))


def _write_columns(pieces, dst_hbm, row0, sems):
    copies = []
    for k, (src, col0) in enumerate(pieces):
        rows, width = src.shape
        copies.append(pltpu.make_async_copy(
            src, dst_hbm.at[pl.ds(row0, rows), pl.ds(col0, width)], sems.at[k]))
    for cp in copies:
        cp.start()
    for cp in copies:
        cp.wait()


def _my_index():
    return 4 * lax.axis_index("x") + 2 * lax.axis_index("y") + lax.axis_index("c")


def _slot_leading(ref, slot):
    return ref.at[slot]


def _all_gather(arrs, out_shapes, slot_fns, name):
    n = len(arrs)

    def body(*refs):
        ins, outs = refs[:n], refs[n:2 * n]
        send_sems, recv_sems, local_sems = refs[2 * n:]
        x, y, c = lax.axis_index("x"), lax.axis_index("y"), lax.axis_index("c")
        me, sibling = (x, y, c), (x, y, 1 - c)
        chips = [(1 - x, y), (x, 1 - y), (1 - x, 1 - y)]

        def blk(a, dev):
            return slot_fns[a](outs[a], 4 * dev[0] + 2 * dev[1] + dev[2])

        def copy(a, k, block, to, src=None):
            dst = blk(a, block)
            return pltpu.make_async_remote_copy(
                src_ref=dst if src is None else src, dst_ref=dst,
                send_sem=send_sems.at[a * 7 + k], recv_sem=recv_sems.at[a * 7 + k],
                device_id=to, device_id_type=MESH)

        mine = [pltpu.make_async_copy(ins[a], blk(a, me), local_sems.at[a]) for a in range(n)]
        for cp in mine:
            cp.start()
        first = []
        for a in range(n):
            first.append(copy(a, 0, me, sibling, src=ins[a]))
            first += [copy(a, 1 + j, me, (*chip, c), src=ins[a]) for j, chip in enumerate(chips)]
        for cp in first:
            cp.start()
        passed = []
        for j, chip in enumerate(chips):
            for a in range(n):
                copy(a, 1 + j, (*chip, c), me).wait_recv()
                fwd = copy(a, 4 + j, (*chip, c), sibling)
                fwd.start()
                passed.append(fwd)
        for a in range(n):
            copy(a, 0, sibling, me).wait_recv()
        for j, chip in enumerate(chips):
            for a in range(n):
                copy(a, 4 + j, (*chip, 1 - c), me).wait_recv()
        for cp in first + passed:
            cp.wait_send()
        for cp in mine:
            cp.wait()

    any_spec = pl.BlockSpec(memory_space=pl.ANY)
    return pl.pallas_call(
        body, name=name,
        out_shape=tuple(out_shapes),
        in_specs=[any_spec] * n,
        out_specs=tuple([any_spec] * n),
        scratch_shapes=[pltpu.SemaphoreType.DMA((7 * n,)), pltpu.SemaphoreType.DMA((7 * n,)),
                        pltpu.SemaphoreType.DMA((n,))],
    )(*arrs)


def _pair_exchange(arrs, shapes4, src_fns, name):
    n = len(arrs)

    def body(*refs):
        ins, own, recv = refs[:n], refs[n:2 * n], refs[2 * n:3 * n]
        send_sems, recv_sems, local_sems = refs[3 * n:]
        x, y, c = lax.axis_index("x"), lax.axis_index("y"), lax.axis_index("c")
        sibling = (x, y, 1 - c)
        local, remote = [], []
        for a in range(n):
            for q in range(4):
                local.append(pltpu.make_async_copy(
                    src_fns[a](ins[a], 2 * q + c), own[a].at[q], local_sems.at[a * 4 + q]))
                remote.append(pltpu.make_async_remote_copy(
                    src_ref=src_fns[a](ins[a], 2 * q + 1 - c), dst_ref=recv[a].at[q],
                    send_sem=send_sems.at[a * 4 + q], recv_sem=recv_sems.at[a * 4 + q],
                    device_id=sibling, device_id_type=MESH))
        for cp in remote + local:
            cp.start()
        for cp in remote + local:
            cp.wait()

    any_spec = pl.BlockSpec(memory_space=pl.ANY)
    outs = pl.pallas_call(
        body, name=name,
        out_shape=tuple(shapes4) * 2,
        in_specs=[any_spec] * n,
        out_specs=tuple([any_spec] * (2 * n)),
        scratch_shapes=[pltpu.SemaphoreType.DMA((4 * n,))] * 3,
    )(*arrs)
    return outs[:n], outs[n:]


def _chip_exchange(arrs, name):
    n = len(arrs)

    def body(*refs):
        ins, outs = refs[:n], refs[n:2 * n]
        send_sems, recv_sems, local_sems = refs[2 * n:]
        x, y, c = lax.axis_index("x"), lax.axis_index("y"), lax.axis_index("c")
        my_chip = 2 * x + y

        def peer_of(k):
            return ((1 - x) if (k >> 1) & 1 else x, (1 - y) if k & 1 else y, c)

        def copy(a, k, out_chip):
            peer = peer_of(k)
            return pltpu.make_async_remote_copy(
                src_ref=ins[a].at[2 * peer[0] + peer[1]], dst_ref=outs[a].at[out_chip],
                send_sem=send_sems.at[a * 3 + k - 1], recv_sem=recv_sems.at[a * 3 + k - 1],
                device_id=peer, device_id_type=MESH)

        mine = [pltpu.make_async_copy(ins[a].at[my_chip], outs[a].at[my_chip], local_sems.at[a])
                for a in range(n)]
        sends = [copy(a, k, my_chip) for k in range(1, 4) for a in range(n)]
        for cp in sends + mine:
            cp.start()
        for k in range(1, 4):
            peer = peer_of(k)
            for a in range(n):
                copy(a, k, 2 * peer[0] + peer[1]).wait_recv()
        for cp in sends:
            cp.wait_send()
        for cp in mine:
            cp.wait()

    any_spec = pl.BlockSpec(memory_space=pl.ANY)
    return pl.pallas_call(
        body, name=name,
        out_shape=tuple(jax.ShapeDtypeStruct(a.shape, a.dtype) for a in arrs),
        in_specs=[any_spec] * n,
        out_specs=tuple([any_spec] * n),
        scratch_shapes=[pltpu.SemaphoreType.DMA((3 * n,)), pltpu.SemaphoreType.DMA((3 * n,)),
                        pltpu.SemaphoreType.DMA((n,))],
    )(*arrs)


def _pair_add(a, b, row_tile, name):
    _, rows, cols = a.shape
    tr = rows if row_tile is None else row_tile

    def body(a_ref, b_ref, o_ref):
        o_ref[...] = (a_ref[...].astype(F32) + b_ref[...].astype(F32)).astype(o_ref.dtype)

    blk = pl.BlockSpec((None, tr, cols), lambda q, i: (q, i, 0))
    return pl.pallas_call(
        body, name=name,
        grid=(4, rows // tr),
        in_specs=[blk, blk], out_specs=blk,
        out_shape=jax.ShapeDtypeStruct(a.shape, a.dtype),
    )(a, b)


def _ada_fwd(c_all, w_ada, b_cols):
    def body(c_ref, w_ref, b_ref, o_ref):
        cv = c_ref[...]
        sc = cv * _sigmoid(cv)
        o_ref[...] = jnp.dot(sc, w_ref[...], preferred_element_type=F32,
                             precision=lax.Precision.HIGHEST) + b_ref[...]

    return pl.pallas_call(
        body, name="ada_fwd",
        out_shape=jax.ShapeDtypeStruct((c_all.shape[0], w_ada.shape[1]), F32),
    )(c_all, w_ada, b_cols)


def _ada_bwd(c_all_t, dmod_cols):
    def body(c_ref, d_ref, o_ref):
        cv = c_ref[...]
        sc = cv * _sigmoid(cv)
        o_ref[...] = jnp.dot(sc, d_ref[...], preferred_element_type=F32,
                             precision=lax.Precision.HIGHEST)

    return pl.pallas_call(
        body, name="ada_bwd",
        out_shape=jax.ShapeDtypeStruct((c_all_t.shape[0], dmod_cols.shape[1]), F32),
    )(c_all_t, dmod_cols)


def _prep_h(x2, mod3):
    ts = 512
    per_seq = S // ts

    def body(x_ref, mod_ref, h_ref):
        shift = mod_ref[0, 0:1, :]
        scale = mod_ref[0, 1:2, :]
        h_ref[...] = (x_ref[...] * (1.0 + scale) + shift).astype(BF16)

    return pl.pallas_call(
        body, name="prep_h",
        grid=(T // ts,),
        in_specs=[pl.BlockSpec((ts, D), lambda i: (i, 0)),
                  pl.BlockSpec((1, 3, D), lambda i: (i // per_seq, 0, 0))],
        out_specs=pl.BlockSpec((ts, D), lambda i: (i, 0)),
        out_shape=jax.ShapeDtypeStruct((T, D), BF16),
    )(x2, mod3)


def _proj(h, w_in_all):
    tm = 512

    def body(h_ref, w_ref, o_ref):
        o_ref[...] = _dot(h_ref[...], w_ref[...]).astype(BF16)

    return pl.pallas_call(
        body, name="proj",
        grid=(N_DEV, T // tm),
        in_specs=[pl.BlockSpec((tm, D), lambda j, i: (i, 0)),
                  pl.BlockSpec((None, D, SHARD), lambda j, i: (j, 0, 0))],
        out_specs=pl.BlockSpec((tm, SHARD), lambda j, i: (i, j)),
        out_shape=jax.ShapeDtypeStruct((T, NCOL), BF16),
        compiler_params=pltpu.CompilerParams(vmem_limit_bytes=VMEM_LIMIT),
    )(h, w_in_all)


def _bucket_maps():
    a = np.arange(QB)[:, None]
    b = np.arange(2 * QB)[None, :]
    steps = a + QB - b
    maps = []
    for dil in DILATIONS:
        dist = np.maximum(steps, 0) * dil
        nf = np.maximum(dist, 1).astype(np.float32)
        large = 16 + (np.log(nf / np.float32(16)) / np.float32(math.log(128.0))
                      * np.float32(16)).astype(np.int32)
        large = np.minimum(large, N_BUCKETS - 1)
        maps.append(np.where(dist < 16, dist, large).astype(np.int32))
    band = (steps >= 0) & (steps <= N_STEPS)
    first = band & (b >= QB)
    masks = np.stack([first, band]).astype(np.int32)
    return np.stack(maps), masks


def _bias_expand(rel_bias, buckets, masks):
    def body(tab_ref, bk_ref, mk_ref, o_ref):
        for g in range(3):
            bk = bk_ref[g]
            for h in range(4):
                col = 4 * g + h
                val = jnp.zeros((QB, 2 * QB), F32)
                for k in range(N_BUCKETS):
                    val = jnp.where(bk == k, tab_ref[k, col], val)
                o_ref[g, 0, h] = jnp.where(mk_ref[0] != 0, val, NEG_INF)
                o_ref[g, 1, h] = jnp.where(mk_ref[1] != 0, val, NEG_INF)

    return pl.pallas_call(
        body, name="bias_expand",
        in_specs=[pl.BlockSpec(memory_space=pltpu.SMEM),
                  pl.BlockSpec(memory_space=pltpu.VMEM),
                  pl.BlockSpec(memory_space=pltpu.VMEM)],
        out_shape=jax.ShapeDtypeStruct((3, 2, 4, QB, 2 * QB), F32),
    )(rel_bias, buckets, masks)


def _bias_grad(ds1, ds2, ds3, buckets):
    def body(d1_ref, d2_ref, d3_ref, bk_ref, o_ref):
        for g, d_ref in enumerate((d1_ref, d2_ref, d3_ref)):
            bk = bk_ref[g]
            for h in range(4):
                dv = d_ref[h]
                for k in range(N_BUCKETS):
                    o_ref[k, 4 * g + h] = jnp.sum(jnp.where(bk == k, dv, 0.0))

    return pl.pallas_call(
        body, name="bias_grad",
        in_specs=[pl.BlockSpec(memory_space=pltpu.VMEM)] * 4,
        out_specs=pl.BlockSpec(memory_space=pltpu.SMEM),
        out_shape=jax.ShapeDtypeStruct((N_BUCKETS, N_HEADS), F32),
    )(ds1, ds2, ds3, buckets)


def _attn_fwd(proj, bias, g):
    dil = DILATIONS[g]
    rows = QB * dil
    nsb = S // rows
    has_prev = nsb > 1

    def residue(r):
        return pl.ds(r, QB, stride=dil) if dil > 1 else pl.ds(0, QB)

    def body(*refs):
        if has_prev:
            (q_ref, kc_ref, vc_ref, kp_ref, vp_ref, b_ref, o_ref, l_ref,
             qs, kcs, vcs, ao, ls, kps, vps) = refs
        else:
            q_ref, kc_ref, vc_ref, b_ref, o_ref, l_ref, qs, kcs, vcs, ao, ls = refs
        lane = lax.broadcasted_iota(jnp.int32, (QB, 128), 1)
        for h in range(4):
            sl = slice(h * HD, (h + 1) * HD)
            qs[...] = q_ref[:, sl].astype(F32)
            kcs[...] = kc_ref[:, sl].astype(F32)
            vcs[...] = vc_ref[:, sl].astype(F32)
            if has_prev:
                kps[...] = kp_ref[:, sl].astype(F32)
                vps[...] = vp_ref[:, sl].astype(F32)
            for r in range(dil):
                rr = residue(r)
                q = qs[rr, :].astype(BF16)
                s2 = _dot_nt(q, kcs[rr, :].astype(BF16)) * SCALE + b_ref[h, :, QB:]
                m = jnp.max(s2, axis=1, keepdims=True)
                if has_prev:
                    s1 = _dot_nt(q, kps[rr, :].astype(BF16)) * SCALE + b_ref[h, :, :QB]
                    m = jnp.maximum(m, jnp.max(s1, axis=1, keepdims=True))
                p2 = jnp.exp(s2 - m)
                l = jnp.sum(p2, axis=1, keepdims=True)
                o = _dot(p2.astype(BF16), vcs[rr, :].astype(BF16))
                if has_prev:
                    p1 = jnp.exp(s1 - m)
                    l = l + jnp.sum(p1, axis=1, keepdims=True)
                    o = o + _dot(p1.astype(BF16), vps[rr, :].astype(BF16))
                ao[rr, :] = o / l
                blk = slice(r * QB, (r + 1) * QB)
                lse = m + jnp.log(l)
                ls[blk, :] = jnp.where(lane == h, lse, 0.0 if h == 0 else ls[blk, :])
            o_ref[:, sl] = ao[...]
        for r in range(dil):
            l_ref[residue(r), :] = ls[r * QB:(r + 1) * QB, :]

    def row(b, n):
        return b * nsb + n

    def prev(b, n):
        return b * nsb + jnp.maximum(n - 1, 0)

    in_specs = [
        pl.BlockSpec((rows, GW), lambda b, n: (row(b, n), CB_Q + g)),
        pl.BlockSpec((rows, GW), lambda b, n: (row(b, n), CB_K + g)),
        pl.BlockSpec((rows, GW), lambda b, n: (row(b, n), CB_V + g)),
    ]
    args = [proj, proj, proj]
    scratch = [pltpu.VMEM((rows, HD), F32)] * 4 + [pltpu.VMEM((rows, 128), F32)]
    if has_prev:
        in_specs += [pl.BlockSpec((rows, GW), lambda b, n: (prev(b, n), CB_K + g)),
                     pl.BlockSpec((rows, GW), lambda b, n: (prev(b, n), CB_V + g))]
        args += [proj, proj]
        scratch += [pltpu.VMEM((rows, HD), F32)] * 2
    in_specs.append(pl.BlockSpec((None, None, 4, QB, 2 * QB),
                                 lambda b, n: (g, jnp.minimum(n, 1), 0, 0, 0)))
    args.append(bias)
    return pl.pallas_call(
        body, name=f"attn_fwd{g}",
        grid=(BL, nsb),
        in_specs=in_specs,
        out_specs=(pl.BlockSpec((rows, GW), lambda b, n: (row(b, n), 0)),
                   pl.BlockSpec((rows, 128), lambda b, n: (row(b, n), 0))),
        out_shape=(jax.ShapeDtypeStruct((T, GW), F32), jax.ShapeDtypeStruct((T, 128), F32)),
        scratch_shapes=scratch,
        compiler_params=pltpu.CompilerParams(vmem_limit_bytes=VMEM_LIMIT),
    )(*args)


def _attn_bwd(proj, d_out, stats, bias, dproj, g):
    dil = DILATIONS[g]
    rows = QB * dil
    nsb = S // rows
    has_prev = nsb > 1
    n_steps = nsb + 1 if has_prev else 1
    n_in = 7 + (2 if has_prev else 0)

    def residue(r):
        return pl.ds(r, QB, stride=dil) if dil > 1 else pl.ds(0, QB)

    def body(*refs):
        q_ref, kc_ref, vc_ref, do_ref, st_ref, b_ref = refs[:6]
        if has_prev:
            kp_ref, vp_ref = refs[6:8]
        out_ref, db_ref = refs[n_in], refs[n_in + 1]
        scr = refs[n_in + 2:]
        qs, kcs, vcs, dos, sts, aq, ak, av, sq, sk, sv, sems = scr[:12]
        if has_prev:
            kps, vps, carry = scr[12:]
        b, n = pl.program_id(0), pl.program_id(1)

        @pl.when((b == 0) & (n == 0))
        def _():
            db_ref[...] = jnp.zeros_like(db_ref)

        def flush_head(h):
            sq[:, h * HD:(h + 1) * HD] = aq[...].astype(BF16)
            sk[:, h * HD:(h + 1) * HD] = ak[...].astype(BF16)
            sv[:, h * HD:(h + 1) * HD] = av[...].astype(BF16)

        def write_block(blk_idx):
            row0 = pl.multiple_of(blk_idx * rows, rows)
            _write_columns([(sq, CB * (CB_Q + g)), (sk, CB * (CB_K + g)), (sv, CB * (CB_V + g))],
                           out_ref, row0, sems)

        if has_prev:
            @pl.when(n == nsb)
            def _():
                for h in range(4):
                    for r in range(dil):
                        blk = slice(r * QB, (r + 1) * QB)
                        aq[residue(r), :] = carry[blk, h * HD:(h + 1) * HD]
                        ak[residue(r), :] = carry[blk, GW + h * HD:GW + (h + 1) * HD]
                        av[residue(r), :] = carry[blk, 2 * GW + h * HD:2 * GW + (h + 1) * HD]
                    flush_head(h)
                write_block(b * nsb + nsb - 1)

        @pl.when(n < nsb)
        def _():
            for r in range(dil):
                sts[r * QB:(r + 1) * QB, :] = st_ref[residue(r), :]
            for h in range(4):
                sl = slice(h * HD, (h + 1) * HD)
                qs[...] = q_ref[:, sl].astype(F32)
                kcs[...] = kc_ref[:, sl].astype(F32)
                vcs[...] = vc_ref[:, sl].astype(F32)
                dos[...] = do_ref[:, sl].astype(F32)
                if has_prev:
                    kps[...] = kp_ref[:, sl].astype(F32)
                    vps[...] = vp_ref[:, sl].astype(F32)
                for r in range(dil):
                    rr = residue(r)
                    blk = slice(r * QB, (r + 1) * QB)
                    q, kc, vc = qs[rr, :].astype(BF16), kcs[rr, :].astype(BF16), vcs[rr, :].astype(BF16)
                    do = dos[rr, :].astype(BF16)
                    lse = sts[blk, h:h + 1]
                    delta = sts[blk, 4 + h:5 + h]
                    s2 = _dot_nt(q, kc) * SCALE + b_ref[h, :, QB:]
                    p2 = jnp.exp(s2 - lse)
                    ds2 = p2 * (_dot_nt(do, vc) - delta)
                    db_ref[h, :, QB:] += ds2
                    ds2b, p2b = ds2.astype(BF16), p2.astype(BF16)
                    dq = _dot(ds2b, kc)
                    dk_cur = _dot_tn(ds2b, q) * SCALE
                    dv_cur = _dot_tn(p2b, do)
                    if has_prev:
                        kp, vp = kps[rr, :].astype(BF16), vps[rr, :].astype(BF16)
                        s1 = _dot_nt(q, kp) * SCALE + b_ref[h, :, :QB]
                        p1 = jnp.exp(s1 - lse)
                        ds1 = p1 * (_dot_nt(do, vp) - delta)
                        db_ref[h, :, :QB] += ds1
                        ds1b, p1b = ds1.astype(BF16), p1.astype(BF16)
                        dq = dq + _dot(ds1b, kp)
                        dk_prev = _dot_tn(ds1b, q) * SCALE
                        dv_prev = _dot_tn(p1b, do)
                        cq = slice(h * HD, (h + 1) * HD)
                        ck = slice(GW + h * HD, GW + (h + 1) * HD)
                        cv = slice(2 * GW + h * HD, 2 * GW + (h + 1) * HD)

                        @pl.when(n > 0)
                        def _():
                            aq[rr, :] = carry[blk, cq]
                            ak[rr, :] = carry[blk, ck] + dk_prev
                            av[rr, :] = carry[blk, cv] + dv_prev

                        carry[blk, cq] = dq * SCALE
                        carry[blk, ck] = dk_cur
                        carry[blk, cv] = dv_cur
                    else:
                        aq[rr, :] = dq * SCALE
                        ak[rr, :] = dk_cur
                        av[rr, :] = dv_cur
                if has_prev:
                    @pl.when(n > 0)
                    def _():
                        flush_head(h)
                else:
                    flush_head(h)
            if has_prev:
                @pl.when(n > 0)
                def _():
                    write_block(b * nsb + n - 1)
            else:
                write_block(b)

    def row(b, n):
        return b * nsb + jnp.minimum(n, nsb - 1)

    def prev(b, n):
        return b * nsb + jnp.maximum(jnp.minimum(n, nsb - 1) - 1, 0)

    in_specs = [
        pl.BlockSpec((rows, GW), lambda b, n: (row(b, n), CB_Q + g)),
        pl.BlockSpec((rows, GW), lambda b, n: (row(b, n), CB_K + g)),
        pl.BlockSpec((rows, GW), lambda b, n: (row(b, n), CB_V + g)),
        pl.BlockSpec((rows, GW), lambda b, n: (row(b, n), 0)),
        pl.BlockSpec((rows, 128), lambda b, n: (row(b, n), 0)),
        pl.BlockSpec((None, None, 4, QB, 2 * QB),
                     lambda b, n: (g, jnp.minimum(jnp.minimum(n, nsb - 1), 1), 0, 0, 0)),
    ]
    args = [proj, proj, proj, d_out, stats, bias]
    scratch = [pltpu.VMEM((rows, HD), F32)] * 4 + [pltpu.VMEM((rows, 128), F32)] \
        + [pltpu.VMEM((rows, HD), F32)] * 3 + [pltpu.VMEM((rows, GW), BF16)] * 3 \
        + [pltpu.SemaphoreType.DMA((3,))]
    if has_prev:
        in_specs += [pl.BlockSpec((rows, GW), lambda b, n: (prev(b, n), CB_K + g)),
                     pl.BlockSpec((rows, GW), lambda b, n: (prev(b, n), CB_V + g))]
        args += [proj, proj]
        scratch += [pltpu.VMEM((rows, HD), F32)] * 2 + [pltpu.VMEM((rows, 3 * GW), F32)]
    in_specs.append(pl.BlockSpec(memory_space=pl.ANY))
    args.append(dproj)
    return pl.pallas_call(
        body, name=f"attn_bwd{g}",
        grid=(BL, n_steps),
        in_specs=in_specs,
        out_specs=(pl.BlockSpec(memory_space=pl.ANY),
                   pl.BlockSpec((4, QB, 2 * QB), lambda b, n: (0, 0, 0))),
        out_shape=(jax.ShapeDtypeStruct((T, NCOL), BF16),
                   jax.ShapeDtypeStruct((4, QB, 2 * QB), F32)),
        scratch_shapes=scratch,
        input_output_aliases={len(args) - 1: 0},
        compiler_params=pltpu.CompilerParams(vmem_limit_bytes=VMEM_LIMIT),
    )(*args)


def _tail(x2, tgt2, mod3, o_g, lse_g, proj, w_ao, w_co, w_o, conv_w, conv_b, ln_g, ln_b):
    tm = 256
    per_seq = S // tm
    halo = 16

    def body(x_ref, t_ref, mod_ref, o1_ref, o2_ref, o3_ref, l1_ref, l2_ref, l3_ref,
             ga_ref, u_ref, bg_ref, cg_ref, gc_ref, ma_ref, mc_ref, up_ref, cp_ref,
             wao_ref, wco_ref, wo_ref, cw_ref, cb_ref, lg_ref, lb_ref,
             dproj_ref, dyc_ref, do_ref, st_ref, dxd_ref,
             mg_ref, dy_ref, ain_ref, dao_ref, sin_ref, dso_ref, vec_ref,
             dga_s, dbg_s, dgm_s, sems):
        i = pl.program_id(0)
        bidx = i // per_seq
        first = (i % per_seq) == 0

        @pl.when(i == 0)
        def _():
            vec_ref[...] = jnp.zeros_like(vec_ref)

        l1, l2, l3 = l1_ref[...], l2_ref[...], l3_ref[...]
        mx = jnp.maximum(jnp.maximum(l1, l2), l3)
        e1, e2, e3 = jnp.exp(l1 - mx), jnp.exp(l2 - mx), jnp.exp(l3 - mx)
        esum = e1 + e2 + e3
        lse_tot = mx + jnp.log(esum)
        w1, w2, w3 = e1 / esum, e2 / esum, e3 / esum

        def per_head(wv):
            return jnp.concatenate([jnp.broadcast_to(wv[:, h:h + 1], (tm, HD)) for h in range(4)], axis=1)

        o = per_head(w1) * o1_ref[...] + per_head(w2) * o2_ref[...] + per_head(w3) * o3_ref[...]

        ga = ga_ref[...].astype(F32)
        sig_ga = _sigmoid(ga)
        silu_ga = ga * sig_ga
        a_in = (o * silu_ga).astype(BF16)
        a_out = _dot(a_in, wao_ref[...])

        u = u_ref[...].astype(F32)
        cg = cg_ref[...].astype(F32)
        z = cg * u
        zp = cp_ref[...].astype(F32) * up_ref[...].astype(F32)
        zp = jnp.where(first, 0.0, zp)
        zcat = jnp.concatenate([zp, z], axis=0)
        z1 = pltpu.roll(zcat, 1, 0)[halo:]
        z2 = pltpu.roll(zcat, 2, 0)[halo:]
        y_conv = cw_ref[0:1, :] * z2 + cw_ref[1:2, :] * z1 + cw_ref[2:3, :] * z + cb_ref[...]
        gc = gc_ref[...].astype(F32)
        sig_gc = _sigmoid(gc)
        silu_gc = gc * sig_gc
        bg = bg_ref[...].astype(F32)
        s_in = (bg * y_conv * silu_gc).astype(BF16)
        s_out = _dot(s_in, wco_ref[...])

        sa = _sigmoid(ma_ref[...].astype(F32))
        sc = _sigmoid(mc_ref[...].astype(F32))
        merged = (sa * a_out + sc * s_out).astype(BF16)
        y = _dot(merged, wo_ref[...])
        gate1 = 1.0 + mod_ref[0, 2:3, :]
        xv = x_ref[...]
        resid = ALPHA * xv + gate1 * y
        mu = jnp.mean(resid, axis=1, keepdims=True)
        xc = resid - mu
        var = jnp.mean(xc * xc, axis=1, keepdims=True)
        rstd = lax.rsqrt(var + LN_EPS)
        xhat = xc * rstd
        lg = lg_ref[...]
        err = xhat * lg + lb_ref[...] - t_ref[...]
        vec_ref[3:4, :] += (0.5 / D) * jnp.sum(err * err, axis=0, keepdims=True)

        dout = err * (1.0 / D)
        vec_ref[1:2, :] += jnp.sum(dout * xhat, axis=0, keepdims=True)
        vec_ref[2:3, :] += jnp.sum(dout, axis=0, keepdims=True)
        dxh = dout * lg
        dres = rstd * (dxh - jnp.mean(dxh, axis=1, keepdims=True)
                       - xhat * jnp.mean(dxh * xhat, axis=1, keepdims=True))
        dxd_ref[...] = ALPHA * dres
        dgate = jnp.sum(dres * y, axis=0, keepdims=True)
        vec_ref[4:5, :] += jnp.where(bidx == 0, dgate, 0.0)
        vec_ref[5:6, :] += jnp.where(bidx == 1, dgate, 0.0)
        dy = (dres * gate1).astype(BF16)

        dmerged = _dot_nt(dy, wo_ref[...])
        da_out = (dmerged * sa).astype(BF16)
        ds_out = (dmerged * sc).astype(BF16)
        dgm_s[:, 2 * D:3 * D] =(dmerged * s_out * sc * (1.0 - sc)).astype(BF16)
        dgm_s[:, D:2 * D] =(dmerged * a_out * sa * (1.0 - sa)).astype(BF16)
        da_in = _dot_nt(da_out, wao_ref[...])
        ds_in = _dot_nt(ds_out, wco_ref[...])

        d_o = da_in * silu_ga
        do_ref[...] = d_o.astype(BF16)
        dga_s[...] =(da_in * o * (sig_ga * (1.0 + ga * (1.0 - sig_ga)))).astype(BF16)
        lane = lax.broadcasted_iota(jnp.int32, (tm, 128), 1)
        stats = lse_tot
        od = o * d_o
        for h in range(4):
            delta = jnp.sum(od[:, h * HD:(h + 1) * HD], axis=1, keepdims=True)
            stats = jnp.where(lane == 4 + h, delta, stats)
        st_ref[...] = stats

        dbg_s[...] =(ds_in * y_conv * silu_gc).astype(BF16)
        dyc = ds_in * bg * silu_gc
        dyc_ref[...] = dyc
        vec_ref[0:1, :] += jnp.sum(dyc, axis=0, keepdims=True)
        dgm_s[:, 0:D] =(ds_in * bg * y_conv * (sig_gc * (1.0 + gc * (1.0 - sig_gc)))).astype(BF16)

        mg_ref[...] = merged
        dy_ref[...] = dy
        ain_ref[...] = a_in
        dao_ref[...] = da_out
        sin_ref[...] = s_in
        dso_ref[...] = ds_out
        _write_columns([(dga_s, CB * CB_GA), (dbg_s, D * KB_BG), (dgm_s, D * KB_GC)],
                       dproj_ref, pl.multiple_of(i * tm, tm), sems)

    def tile(width, cblk=0):
        return pl.BlockSpec((tm, width), lambda i: (i, cblk))

    def whole(shape):
        return pl.BlockSpec(shape, lambda i: tuple(0 for _ in shape))

    prev_rows = lambda i: (jnp.maximum(i * (tm // halo) - 1, 0),)
    in_specs = [
        tile(D), tile(D), pl.BlockSpec((1, 3, D), lambda i: (i // per_seq, 0, 0)),
        tile(GW), tile(GW), tile(GW), tile(128), tile(128), tile(128),
        tile(GW, CB_GA), tile(D, KB_U), tile(D, KB_BG), tile(D, KB_CG), tile(D, KB_GC),
        tile(D, KB_MA), tile(D, KB_MC),
        pl.BlockSpec((halo, D), lambda i: (*prev_rows(i), KB_U)),
        pl.BlockSpec((halo, D), lambda i: (*prev_rows(i), KB_CG)),
        whole((GW, D)), whole((D, D)), whole((D, D)),
        whole((3, D)), whole((1, D)), whole((1, D)), whole((1, D)),
    ]
    out_specs = (
        pl.BlockSpec(memory_space=pl.ANY), tile(D), tile(GW), tile(128), tile(D),
        tile(D), tile(D), tile(GW), tile(D), tile(D), tile(D),
        pl.BlockSpec((8, D), lambda i: (0, 0)),
    )
    out_shape = (
        jax.ShapeDtypeStruct((T, NCOL), BF16),
        jax.ShapeDtypeStruct((T, D), F32),
        jax.ShapeDtypeStruct((T, GW), BF16),
        jax.ShapeDtypeStruct((T, 128), F32),
        jax.ShapeDtypeStruct((T, D), F32),
        jax.ShapeDtypeStruct((T, D), BF16),
        jax.ShapeDtypeStruct((T, D), BF16),
        jax.ShapeDtypeStruct((T, GW), BF16),
        jax.ShapeDtypeStruct((T, D), BF16),
        jax.ShapeDtypeStruct((T, D), BF16),
        jax.ShapeDtypeStruct((T, D), BF16),
        jax.ShapeDtypeStruct((8, D), F32),
    )
    return pl.pallas_call(
        body, name="tail",
        grid=(T // tm,),
        in_specs=in_specs, out_specs=out_specs, out_shape=out_shape,
        scratch_shapes=[pltpu.VMEM((tm, GW), BF16), pltpu.VMEM((tm, D), BF16), pltpu.VMEM((tm, 3 * D), BF16),
                        pltpu.SemaphoreType.DMA((3,))],
        compiler_params=pltpu.CompilerParams(vmem_limit_bytes=VMEM_LIMIT),
    )(x2, tgt2, mod3, *o_g, *lse_g, proj, proj, proj, proj, proj, proj, proj, proj, proj,
      w_ao, w_co, w_o, conv_w, conv_b, ln_g, ln_b)


def _conv_bwd(dyc, proj, conv_w, dproj):
    tm = 512
    per_seq = S // tm
    halo = 16

    def body(d_ref, dn_ref, u_ref, c_ref, up_ref, cp_ref, cw_ref, _, dproj_ref, g_ref, du_s, dc_s, sems):
        i = pl.program_id(0)
        first = (i % per_seq) == 0
        last = (i % per_seq) == per_seq - 1

        @pl.when(i == 0)
        def _():
            g_ref[...] = jnp.zeros_like(g_ref)

        d = d_ref[...]
        dn = jnp.where(last, 0.0, dn_ref[...])
        dcat = jnp.concatenate([d, dn], axis=0)
        d1 = pltpu.roll(dcat, tm + 8 - 1, 0)[:tm]
        d2 = pltpu.roll(dcat, tm + 8 - 2, 0)[:tm]
        dz = cw_ref[2:3, :] * d + cw_ref[1:2, :] * d1 + cw_ref[0:1, :] * d2
        u = u_ref[...].astype(F32)
        cg = c_ref[...].astype(F32)
        du_s[...] = (dz * cg).astype(BF16)
        dc_s[...] = (dz * u).astype(BF16)
        _write_columns([(du_s, D * KB_U), (dc_s, D * KB_CG)], dproj_ref, pl.multiple_of(i * tm, tm), sems)

        z = cg * u
        zp = jnp.where(first, 0.0, cp_ref[...].astype(F32) * up_ref[...].astype(F32))
        zcat = jnp.concatenate([zp, z], axis=0)
        z1 = pltpu.roll(zcat, 1, 0)[halo:]
        z2 = pltpu.roll(zcat, 2, 0)[halo:]
        g_ref[0:1, :] += jnp.sum(d * z2, axis=0, keepdims=True)
        g_ref[1:2, :] += jnp.sum(d * z1, axis=0, keepdims=True)
        g_ref[2:3, :] += jnp.sum(d * z, axis=0, keepdims=True)

    n_tiles = T // tm
    prev_rows = lambda i: jnp.maximum(i * (tm // halo) - 1, 0)
    next_rows = lambda i: jnp.minimum((i + 1) * (tm // 8), T // 8 - 1)
    return pl.pallas_call(
        body, name="conv_bwd",
        grid=(n_tiles,),
        in_specs=[pl.BlockSpec((tm, D), lambda i: (i, 0)),
                  pl.BlockSpec((8, D), lambda i: (next_rows(i), 0)),
                  pl.BlockSpec((tm, D), lambda i: (i, KB_U)),
                  pl.BlockSpec((tm, D), lambda i: (i, KB_CG)),
                  pl.BlockSpec((halo, D), lambda i: (prev_rows(i), KB_U)),
                  pl.BlockSpec((halo, D), lambda i: (prev_rows(i), KB_CG)),
                  pl.BlockSpec((3, D), lambda i: (0, 0)),
                  pl.BlockSpec(memory_space=pl.ANY)],
        out_specs=(pl.BlockSpec(memory_space=pl.ANY),
                   pl.BlockSpec((8, D), lambda i: (0, 0))),
        out_shape=(jax.ShapeDtypeStruct((T, NCOL), BF16),
                   jax.ShapeDtypeStruct((8, D), F32)),
        scratch_shapes=[pltpu.VMEM((tm, D), BF16), pltpu.VMEM((tm, D), BF16), pltpu.SemaphoreType.DMA((2,))],
        input_output_aliases={7: 0},
        compiler_params=pltpu.CompilerParams(vmem_limit_bytes=VMEM_LIMIT),
    )(dyc, dyc, proj, proj, proj, proj, conv_w, dproj)


def _dh_dx(dproj, w_in_all, x2, dxd, mod3):
    tm = 512
    per_seq = S // tm

    def body(d_ref, w_ref, x_ref, dxd_ref, mod_ref, gx_ref, vec_ref, acc):
        i, jj = pl.program_id(0), pl.program_id(1)

        @pl.when((i == 0) & (jj == 0))
        def _():
            vec_ref[...] = jnp.zeros_like(vec_ref)

        @pl.when(jj == 0)
        def _():
            acc[...] = jnp.zeros_like(acc)

        acc[...] += _dot_nt(d_ref[...], w_ref[...])

        @pl.when(jj == N_DEV - 1)
        def _():
            dh = acc[...]
            bidx = i // per_seq
            gx_ref[...] = dxd_ref[...] + dh * (1.0 + mod_ref[0, 1:2, :])
            dshift = jnp.sum(dh, axis=0, keepdims=True)
            dscale = jnp.sum(dh * x_ref[...], axis=0, keepdims=True)
            vec_ref[0:1, :] += jnp.where(bidx == 0, dshift, 0.0)
            vec_ref[1:2, :] += jnp.where(bidx == 1, dshift, 0.0)
            vec_ref[2:3, :] += jnp.where(bidx == 0, dscale, 0.0)
            vec_ref[3:4, :] += jnp.where(bidx == 1, dscale, 0.0)

    return pl.pallas_call(
        body, name="dh_dx",
        grid=(T // tm, N_DEV),
        in_specs=[
            pl.BlockSpec((tm, SHARD), lambda i, jj: (i, jj)),
            pl.BlockSpec((None, D, SHARD), lambda i, jj: (jj, 0, 0)),
            pl.BlockSpec((tm, D), lambda i, jj: (i, 0)),
            pl.BlockSpec((tm, D), lambda i, jj: (i, 0)),
            pl.BlockSpec((1, 3, D), lambda i, jj: (i // per_seq, 0, 0))],
        out_specs=(pl.BlockSpec((tm, D), lambda i, jj: (i, 0)),
                   pl.BlockSpec((8, D), lambda i, jj: (0, 0))),
        out_shape=(jax.ShapeDtypeStruct((T, D), F32), jax.ShapeDtypeStruct((8, D), F32)),
        scratch_shapes=[pltpu.VMEM((tm, D), F32)],
        compiler_params=pltpu.CompilerParams(vmem_limit_bytes=VMEM_LIMIT),
    )(dproj, w_in_all, x2, dxd, mod3)


def _mm_tn(a, b, tn, blocks_leading, name):
    kk, m = a.shape
    n = b.shape[1]
    tk = 1024

    def body(a_ref, b_ref, o_ref, acc):
        @pl.when(pl.program_id(1) == 0)
        def _():
            acc[...] = jnp.zeros_like(acc)

        acc[...] += _dot_tn(a_ref[...], b_ref[...])

        @pl.when(pl.program_id(1) == kk // tk - 1)
        def _():
            o_ref[...] = acc[...].astype(BF16)

    if blocks_leading:
        out_spec = pl.BlockSpec((None, m, tn), lambda j, k: (j, 0, 0))
        out_shape = jax.ShapeDtypeStruct((n // tn, m, tn), BF16)
    else:
        out_spec = pl.BlockSpec((m, tn), lambda j, k: (0, j))
        out_shape = jax.ShapeDtypeStruct((m, n), BF16)
    return pl.pallas_call(
        body, name=name,
        grid=(n // tn, kk // tk),
        in_specs=[pl.BlockSpec((tk, m), lambda j, k: (k, 0)),
                  pl.BlockSpec((tk, tn), lambda j, k: (k, j))],
        out_specs=out_spec, out_shape=out_shape,
        scratch_shapes=[pltpu.VMEM((m, tn), F32)],
        compiler_params=pltpu.CompilerParams(vmem_limit_bytes=VMEM_LIMIT),
    )(a, b)


def _adamw(parts, w, m, v, name, row_tile=None):
    n_parts, rows, cols = parts.shape
    tr = rows if row_tile is None else row_tile
    c1 = 1.0 - ADAM_B1 ** ADAM_STEP
    c2 = 1.0 - ADAM_B2 ** ADAM_STEP

    def body(p_ref, w_ref, m_ref, v_ref, g_ref, d_ref, nm_ref, nv_ref):
        g = p_ref[0].astype(F32)
        for s in range(1, n_parts):
            g = g + p_ref[s].astype(F32)
        nm = ADAM_B1 * m_ref[...] + (1.0 - ADAM_B1) * g
        nv = ADAM_B2 * v_ref[...] + (1.0 - ADAM_B2) * (g * g)
        m_hat = nm / c1
        v_hat = nv / c2
        g_ref[...] = g
        d_ref[...] = -ADAM_LR * (m_hat / (jnp.sqrt(v_hat) + ADAM_EPS) + ADAM_WD * w_ref[...])
        nm_ref[...] = nm
        nv_ref[...] = nv

    blk = pl.BlockSpec((tr, cols), lambda i: (i, 0))
    shp = jax.ShapeDtypeStruct((rows, cols), F32)
    return pl.pallas_call(
        body, name=name,
        grid=(rows // tr,),
        in_specs=[pl.BlockSpec((n_parts, tr, cols), lambda i: (0, i, 0)), blk, blk, blk],
        out_specs=(blk, blk, blk, blk),
        out_shape=(shp, shp, shp, shp),
        compiler_params=pltpu.CompilerParams(vmem_limit_bytes=VMEM_LIMIT),
    )(parts, w, m, v)


def _loss_sum(rows):
    def body(r_ref, o_ref):
        o_ref[...] = jnp.sum(jnp.sum(r_ref[...], axis=0, keepdims=True), axis=1, keepdims=True)

    return pl.pallas_call(body, name="loss_sum", out_shape=jax.ShapeDtypeStruct((1, 1), F32))(rows)


def _local_step(x2, tgt2, mod3, w_in_all, w_ao, w_co, w_o, conv_w, conv_b, rel_bias, ln_g, ln_b):
    buckets_np, masks_np = _bucket_maps()
    buckets, masks = jnp.asarray(buckets_np), jnp.asarray(masks_np)

    h = _prep_h(x2, mod3)
    proj = _proj(h, w_in_all)
    bias = _bias_expand(rel_bias, buckets, masks)
    fwd = [_attn_fwd(proj, bias, g) for g in range(3)]
    o_g = [f[0] for f in fwd]
    lse_g = [f[1] for f in fwd]

    (dproj, dyc, d_o, stats, dxd, merged, dy, a_in, da_out, s_in, ds_out, tail_vec) = _tail(
        x2, tgt2, mod3, o_g, lse_g, proj, w_ao, w_co, w_o, conv_w, conv_b, ln_g, ln_b)

    dbias = []
    for g in range(3):
        dproj, db = _attn_bwd(proj, d_o, stats, bias, dproj, g)
        dbias.append(db)
    g_rel_bias = _bias_grad(*dbias, buckets)
    dproj, conv_vec = _conv_bwd(dyc, proj, conv_w, dproj)

    grad_x, mod_vec = _dh_dx(dproj, w_in_all, x2, dxd, mod3)
    gw_in = _mm_tn(h, dproj, SHARD, True, "gw_in")
    gw_o = _mm_tn(merged, dy, 512, False, "gw_o")
    gw_co = _mm_tn(s_in, ds_out, 512, False, "gw_conv_out")
    gw_ao = _mm_tn(a_in, da_out, 128, True, "gw_attn_out")
    return grad_x, gw_in, gw_ao, gw_co, gw_o, conv_vec, g_rel_bias, tail_vec, mod_vec


def kernel(x, c, w_ada, b_ada, w_in, conv_w, conv_b, rel_bias, w_attn_out, w_conv_out, w_o, ln_g, ln_b, loss_target, m_w_ada, m_b_ada, m_w_in, m_conv_w, m_conv_b, m_rel_bias, m_w_attn_out, m_w_conv_out, m_w_o, m_ln_g, m_ln_b, v_w_ada, v_b_ada, v_w_in, v_conv_w, v_conv_b, v_rel_bias, v_w_attn_out, v_w_conv_out, v_w_o, v_ln_g, v_ln_b):
    me = _my_index()
    x2 = x.reshape(T, D)
    tgt2 = loss_target.reshape(T, D)

    w_in_all, w_ao_g, w_co_g, w_o_g, conv_w_g, c_g = _all_gather(
        [w_in[0].astype(BF16), w_attn_out[0].astype(BF16), w_conv_out[0].astype(BF16),
         w_o[0].astype(BF16), conv_w[0], c],
        [jax.ShapeDtypeStruct((N_DEV, D, SHARD), BF16),
         jax.ShapeDtypeStruct((N_DEV, GW, D // N_DEV), BF16),
         jax.ShapeDtypeStruct((N_DEV, D // N_DEV, D), BF16),
         jax.ShapeDtypeStruct((N_DEV, D // N_DEV, D), BF16),
         jax.ShapeDtypeStruct((N_DEV, 3, D // N_DEV), F32),
         jax.ShapeDtypeStruct((N_DEV, BL, D), F32)],
        [_slot_leading] * 6,
        "gather_weights")
    w_ao_full = jnp.transpose(w_ao_g, (1, 0, 2)).reshape(GW, D)
    w_co_full = w_co_g.reshape(D, D)
    w_o_full = w_o_g.reshape(D, D)
    conv_w_full = jnp.transpose(conv_w_g, (1, 0, 2)).reshape(3, D)
    c_all = c_g.reshape(N_DEV * BL, D)

    b_cols = lax.dynamic_slice(b_ada, (0, me * ADA_SHARD), (1, ADA_SHARD))
    mod_cols = _ada_fwd(c_all, w_ada[0], b_cols)
    (mod_g,) = _all_gather([mod_cols], [jax.ShapeDtypeStruct((N_DEV, N_DEV * BL, ADA_SHARD), F32)],
                           [_slot_leading], "gather_mod")
    mod_all = jnp.transpose(mod_g, (1, 0, 2)).reshape(N_DEV * BL, 3 * D)
    mod3 = lax.dynamic_slice(mod_all, (me * BL, 0), (BL, 3 * D)).reshape(BL, 3, D)

    (grad_x, gw_in, gw_ao, gw_co, gw_o, conv_vec, g_rel_bias, tail_vec, mod_vec) = _local_step(
        x2, tgt2, mod3, w_in_all, w_ao_full, w_co_full, w_o_full,
        conv_w_full, conv_b, rel_bias, ln_g, ln_b)

    g_conv_w_blocks = jnp.transpose(conv_vec[0:3].reshape(3, N_DEV, D // N_DEV), (1, 0, 2))
    own, sib = _pair_exchange(
        [gw_in, gw_ao, gw_co.reshape(N_DEV, D // N_DEV, D), gw_o.reshape(N_DEV, D // N_DEV, D),
         g_conv_w_blocks],
        [jax.ShapeDtypeStruct((4, D, SHARD), BF16),
         jax.ShapeDtypeStruct((4, GW, D // N_DEV), BF16),
         jax.ShapeDtypeStruct((4, D // N_DEV, D), BF16),
         jax.ShapeDtypeStruct((4, D // N_DEV, D), BF16),
         jax.ShapeDtypeStruct((4, 3, D // N_DEV), F32)],
        [_slot_leading] * 5,
        "pair_grads")
    tiles = [256, None, None, None, None]
    names = ["w_in", "w_attn_out", "w_conv_out", "w_o", "conv_w"]
    chip_sums = [_pair_add(own[a], sib[a], tiles[a], "pair_add_" + names[a]) for a in range(5)]
    r_in, r_ao, r_co, r_o, r_cw = _chip_exchange(chip_sums, "chip_grads")

    small = jnp.concatenate([
        tail_vec[0:4],
        jnp.pad(g_rel_bias.reshape(1, N_BUCKETS * N_HEADS), ((0, 0), (0, D - N_BUCKETS * N_HEADS))),
        jnp.zeros((3, D), F32)], axis=0)
    dmod = jnp.concatenate([mod_vec[0:2], mod_vec[2:4], tail_vec[4:6]], axis=1)
    small_g, dmod_g = _all_gather(
        [small, dmod],
        [jax.ShapeDtypeStruct((N_DEV, 8, D), F32), jax.ShapeDtypeStruct((N_DEV, BL, 3 * D), F32)],
        [_slot_leading] * 2, "gather_small")
    dmod_all = dmod_g.reshape(N_DEV * BL, 3 * D)
    loss = _loss_sum(small_g[:, 3, :]).reshape(())
    g_w_ada = _ada_bwd(jnp.transpose(c_all), lax.dynamic_slice(dmod_all, (0, me * ADA_SHARD),
                                                               (N_DEV * BL, ADA_SHARD)))

    def upd(parts, w, m, v, name, row_tile=None):
        shape = w.shape
        w2, m2, v2 = (t.reshape(parts.shape[1:]) for t in (w, m, v))
        return tuple(t.reshape(shape) for t in _adamw(parts, w2, m2, v2, name, row_tile))

    res = {
        "w_ada": upd(g_w_ada[None], w_ada, m_w_ada, v_w_ada, "adam_w_ada", 256),
        "b_ada": upd(dmod_all[:, None, :], b_ada, m_b_ada, v_b_ada, "adam_b_ada"),
        "w_in": upd(r_in, w_in, m_w_in, v_w_in, "adam_w_in", 128),
        "conv_w": upd(r_cw, conv_w, m_conv_w, v_conv_w, "adam_conv_w"),
        "conv_b": upd(small_g[:, 0:1, :], conv_b, m_conv_b, v_conv_b, "adam_conv_b"),
        "rel_bias": upd(small_g[:, 4, :N_BUCKETS * N_HEADS].reshape(N_DEV, N_BUCKETS, N_HEADS),
                        rel_bias, m_rel_bias, v_rel_bias, "adam_rel_bias"),
        "w_attn_out": upd(r_ao, w_attn_out, m_w_attn_out, v_w_attn_out, "adam_w_attn_out"),
        "w_conv_out": upd(r_co, w_conv_out, m_w_conv_out, v_w_conv_out, "adam_w_conv_out"),
        "w_o": upd(r_o, w_o, m_w_o, v_w_o, "adam_w_o"),
        "ln_g": upd(small_g[:, 1:2, :], ln_g, m_ln_g, v_ln_g, "adam_ln_g"),
        "ln_b": upd(small_g[:, 2:3, :], ln_b, m_ln_b, v_ln_b, "adam_ln_b"),
    }
    order = ["w_ada", "b_ada", "w_in", "conv_w", "conv_b", "rel_bias", "w_attn_out", "w_conv_out",
             "w_o", "ln_g", "ln_b"]
    outs = [loss, grad_x.reshape(BL, S, D)]
    for k in range(4):
        outs += [res[name][k] for name in order]
    return tuple(outs)
```

```python
import functools
import math

import numpy as np
import jax
import jax.numpy as jnp
from jax import lax
from jax.experimental import pallas as pl
from jax.experimental.pallas import tpu as pltpu

F32 = jnp.float32
BF16 = jnp.bfloat16
MESH = pl.DeviceIdType.MESH

N_DEV = 8
D = 1024
S = 2048
BL = 2
T = BL * S
NCOL = 11264
SHARD = NCOL // N_DEV
CB = 512
NCB = NCOL // CB
HD = 128
GW = 512
QB = 128
DILATIONS = (1, 4, 16)
N_STEPS = 128
N_BUCKETS = 32
N_HEADS = 12
ALPHA = 2.0 ** 0.25
LN_EPS = 1e-5
NEG_INF = -1e30
SCALE = HD ** -0.5
ADA_SHARD = 3 * D // N_DEV

CB_Q, CB_K, CB_V, CB_GA = 0, 3, 6, 9
KB_U, KB_BG, KB_CG, KB_GC, KB_MA, KB_MC = 5, 6, 7, 8, 9, 10

ADAM_LR, ADAM_B1, ADAM_B2, ADAM_EPS, ADAM_WD, ADAM_STEP = 0.001, 0.9, 0.999, 1e-08, 0.01, 10

VMEM_LIMIT = 56 * 1024 * 1024


def _dot(a, b):
    return jnp.dot(a, b, preferred_element_type=F32)


def _dot_nt(a, b):
    return lax.dot_general(a, b, (((1,), (1,)), ((), ())), preferred_element_type=F32)


def _dot_tn(a, b):
    return lax.dot_general(a, b, (((0,), (0,)), ((), ())), preferred_element_type=F32)


def _sigmoid(v):
    return 1.0 / (1.0 + jnp.exp(-v))


def _write_columns(pieces, dst_hbm, row0, sems):
    copies = []
    for k, (src, col0) in enumerate(pieces):
        rows, width = src.shape
        copies.append(pltpu.make_async_copy(
            src, dst_hbm.at[pl.ds(row0, rows), pl.ds(col0, width)], sems.at[k]))
    for cp in copies:
        cp.start()
    for cp in copies:
        cp.wait()


def _my_index():
    return 4 * lax.axis_index("x") + 2 * lax.axis_index("y") + lax.axis_index("c")


def _slot_leading(ref, slot):
    return ref.at[slot]


def _all_gather(arrs, out_shapes, slot_fns, name):
    n = len(arrs)

    def body(*refs):
        ins, outs = refs[:n], refs[n:2 * n]
        send_sems, recv_sems, local_sems = refs[2 * n:]
        x, y, c = lax.axis_index("x"), lax.axis_index("y"), lax.axis_index("c")
        me, sibling = (x, y, c), (x, y, 1 - c)
        chips = [(1 - x, y), (x, 1 - y), (1 - x, 1 - y)]

        def blk(a, dev):
            return slot_fns[a](outs[a], 4 * dev[0] + 2 * dev[1] + dev[2])

        def copy(a, k, block, to, src=None):
            dst = blk(a, block)
            return pltpu.make_async_remote_copy(
                src_ref=dst if src is None else src, dst_ref=dst,
                send_sem=send_sems.at[a * 7 + k], recv_sem=recv_sems.at[a * 7 + k],
                device_id=to, device_id_type=MESH)

        mine = [pltpu.make_async_copy(ins[a], blk(a, me), local_sems.at[a]) for a in range(n)]
        for cp in mine:
            cp.start()
        first = []
        for a in range(n):
            first.append(copy(a, 0, me, sibling, src=ins[a]))
            first += [copy(a, 1 + j, me, (*chip, c), src=ins[a]) for j, chip in enumerate(chips)]
        for cp in first:
            cp.start()
        passed = []
        for j, chip in enumerate(chips):
            for a in range(n):
                copy(a, 1 + j, (*chip, c), me).wait_recv()
                fwd = copy(a, 4 + j, (*chip, c), sibling)
                fwd.start()
                passed.append(fwd)
        for a in range(n):
            copy(a, 0, sibling, me).wait_recv()
        for j, chip in enumerate(chips):
            for a in range(n):
                copy(a, 4 + j, (*chip, 1 - c), me).wait_recv()
        for cp in first + passed:
            cp.wait_send()
        for cp in mine:
            cp.wait()

    any_spec = pl.BlockSpec(memory_space=pl.ANY)
    return pl.pallas_call(
        body, name=name,
        out_shape=tuple(out_shapes),
        in_specs=[any_spec] * n,
        out_specs=tuple([any_spec] * n),
        scratch_shapes=[pltpu.SemaphoreType.DMA((7 * n,)), pltpu.SemaphoreType.DMA((7 * n,)),
                        pltpu.SemaphoreType.DMA((n,))],
    )(*arrs)


def _pair_exchange(arrs, shapes4, src_fns, name):
    n = len(arrs)

    def body(*refs):
        ins, recv = refs[:n], refs[n:2 * n]
        send_sems, recv_sems = refs[2 * n:]
        x, y, c = lax.axis_index("x"), lax.axis_index("y"), lax.axis_index("c")
        sibling = (x, y, 1 - c)
        remote = []
        for a in range(n):
            for q in range(4):
                remote.append(pltpu.make_async_remote_copy(
                    src_ref=src_fns[a](ins[a], 2 * q + 1 - c), dst_ref=recv[a].at[q],
                    send_sem=send_sems.at[a * 4 + q], recv_sem=recv_sems.at[a * 4 + q],
                    device_id=sibling, device_id_type=MESH))
        for cp in remote:
            cp.start()
        for cp in remote:
            cp.wait()

    any_spec = pl.BlockSpec(memory_space=pl.ANY)
    return pl.pallas_call(
        body, name=name,
        out_shape=tuple(shapes4),
        in_specs=[any_spec] * n,
        out_specs=tuple([any_spec] * n),
        scratch_shapes=[pltpu.SemaphoreType.DMA((4 * n,))] * 2,
    )(*arrs)


def _chip_exchange(arrs, name):
    n = len(arrs)

    def body(*refs):
        ins, outs = refs[:n], refs[n:2 * n]
        send_sems, recv_sems, local_sems = refs[2 * n:]
        x, y, c = lax.axis_index("x"), lax.axis_index("y"), lax.axis_index("c")
        my_chip = 2 * x + y

        def peer_of(k):
            return ((1 - x) if (k >> 1) & 1 else x, (1 - y) if k & 1 else y, c)

        def copy(a, k, out_chip):
            peer = peer_of(k)
            return pltpu.make_async_remote_copy(
                src_ref=ins[a].at[2 * peer[0] + peer[1]], dst_ref=outs[a].at[out_chip],
                send_sem=send_sems.at[a * 3 + k - 1], recv_sem=recv_sems.at[a * 3 + k - 1],
                device_id=peer, device_id_type=MESH)

        mine = [pltpu.make_async_copy(ins[a].at[my_chip], outs[a].at[my_chip], local_sems.at[a])
                for a in range(n)]
        sends = [copy(a, k, my_chip) for k in range(1, 4) for a in range(n)]
        for cp in sends + mine:
            cp.start()
        for k in range(1, 4):
            peer = peer_of(k)
            for a in range(n):
                copy(a, k, 2 * peer[0] + peer[1]).wait_recv()
        for cp in sends:
            cp.wait_send()
        for cp in mine:
            cp.wait()

    any_spec = pl.BlockSpec(memory_space=pl.ANY)
    return pl.pallas_call(
        body, name=name,
        out_shape=tuple(jax.ShapeDtypeStruct(a.shape, a.dtype) for a in arrs),
        in_specs=[any_spec] * n,
        out_specs=tuple([any_spec] * n),
        scratch_shapes=[pltpu.SemaphoreType.DMA((3 * n,)), pltpu.SemaphoreType.DMA((3 * n,)),
                        pltpu.SemaphoreType.DMA((n,))],
    )(*arrs)


def _pair_add(core, mine, theirs, row_tile, name):
    _, rows, cols = theirs.shape
    tr = rows if row_tile is None else row_tile

    def body(core_ref, a_ref, b_ref, o_ref):
        o_ref[...] = (a_ref[...].astype(F32) + b_ref[...].astype(F32)).astype(o_ref.dtype)

    blk = pl.BlockSpec((None, tr, cols), lambda q, i, core_ref: (q, i, 0))
    return pl.pallas_call(
        body, name=name,
        grid_spec=pltpu.PrefetchScalarGridSpec(
            num_scalar_prefetch=1,
            grid=(4, rows // tr),
            in_specs=[pl.BlockSpec((None, tr, cols), lambda q, i, core_ref: (2 * q + core_ref[0], i, 0)), blk],
            out_specs=blk),
        out_shape=jax.ShapeDtypeStruct(theirs.shape, theirs.dtype),
    )(core, mine, theirs)


def _ada_fwd(c_all, w_ada, b_cols):
    def body(c_ref, w_ref, b_ref, o_ref):
        cv = c_ref[...]
        sc = cv * _sigmoid(cv)
        o_ref[...] = jnp.dot(sc, w_ref[...], preferred_element_type=F32,
                             precision=lax.Precision.HIGHEST) + b_ref[...]

    return pl.pallas_call(
        body, name="ada_fwd",
        out_shape=jax.ShapeDtypeStruct((c_all.shape[0], w_ada.shape[1]), F32),
    )(c_all, w_ada, b_cols)


def _ada_bwd(c_all_t, dmod_cols):
    def body(c_ref, d_ref, o_ref):
        cv = c_ref[...]
        sc = cv * _sigmoid(cv)
        o_ref[...] = jnp.dot(sc, d_ref[...], preferred_element_type=F32,
                             precision=lax.Precision.HIGHEST)

    return pl.pallas_call(
        body, name="ada_bwd",
        out_shape=jax.ShapeDtypeStruct((c_all_t.shape[0], dmod_cols.shape[1]), F32),
    )(c_all_t, dmod_cols)


def _prep_h(x2, mod3):
    ts = 512
    per_seq = S // ts

    def body(x_ref, mod_ref, h_ref):
        shift = mod_ref[0, 0:1, :]
        scale = mod_ref[0, 1:2, :]
        h_ref[...] = (x_ref[...] * (1.0 + scale) + shift).astype(BF16)

    return pl.pallas_call(
        body, name="prep_h",
        grid=(T // ts,),
        in_specs=[pl.BlockSpec((ts, D), lambda i: (i, 0)),
                  pl.BlockSpec((1, 3, D), lambda i: (i // per_seq, 0, 0))],
        out_specs=pl.BlockSpec((ts, D), lambda i: (i, 0)),
        out_shape=jax.ShapeDtypeStruct((T, D), BF16),
    )(x2, mod3)


def _proj(h, w_in_all):
    tm = 512

    def body(h_ref, w_ref, o_ref):
        o_ref[...] = _dot(h_ref[...], w_ref[...]).astype(BF16)

    return pl.pallas_call(
        body, name="proj",
        grid=(N_DEV, T // tm),
        in_specs=[pl.BlockSpec((tm, D), lambda j, i: (i, 0)),
                  pl.BlockSpec((None, D, SHARD), lambda j, i: (j, 0, 0))],
        out_specs=pl.BlockSpec((tm, SHARD), lambda j, i: (i, j)),
        out_shape=jax.ShapeDtypeStruct((T, NCOL), BF16),
        compiler_params=pltpu.CompilerParams(vmem_limit_bytes=VMEM_LIMIT),
    )(h, w_in_all)


def _bucket_maps():
    a = np.arange(QB)[:, None]
    b = np.arange(2 * QB)[None, :]
    steps = a + QB - b
    maps = []
    for dil in DILATIONS:
        dist = np.maximum(steps, 0) * dil
        nf = np.maximum(dist, 1).astype(np.float32)
        large = 16 + (np.log(nf / np.float32(16)) / np.float32(math.log(128.0))
                      * np.float32(16)).astype(np.int32)
        large = np.minimum(large, N_BUCKETS - 1)
        maps.append(np.where(dist < 16, dist, large).astype(np.int32))
    band = (steps >= 0) & (steps <= N_STEPS)
    first = band & (b >= QB)
    masks = np.stack([first, band]).astype(np.int32)
    return np.stack(maps), masks


def _bias_expand(rel_bias, buckets, masks):
    def body(tab_ref, bk_ref, mk_ref, o_ref):
        for g in range(3):
            bk = bk_ref[g]
            for h in range(4):
                col = 4 * g + h
                val = jnp.zeros((QB, 2 * QB), F32)
                for k in range(N_BUCKETS):
                    val = jnp.where(bk == k, tab_ref[k, col], val)
                o_ref[g, 0, h] = jnp.where(mk_ref[0] != 0, val, NEG_INF)
                o_ref[g, 1, h] = jnp.where(mk_ref[1] != 0, val, NEG_INF)

    return pl.pallas_call(
        body, name="bias_expand",
        in_specs=[pl.BlockSpec(memory_space=pltpu.SMEM),
                  pl.BlockSpec(memory_space=pltpu.VMEM),
                  pl.BlockSpec(memory_space=pltpu.VMEM)],
        out_shape=jax.ShapeDtypeStruct((3, 2, 4, QB, 2 * QB), F32),
    )(rel_bias, buckets, masks)


def _bias_grad(ds1, ds2, ds3, buckets):
    def body(d1_ref, d2_ref, d3_ref, bk_ref, o_ref):
        for g, d_ref in enumerate((d1_ref, d2_ref, d3_ref)):
            bk = bk_ref[g]
            for h in range(4):
                dv = d_ref[h]
                for k in range(N_BUCKETS):
                    o_ref[k, 4 * g + h] = jnp.sum(jnp.where(bk == k, dv, 0.0))

    return pl.pallas_call(
        body, name="bias_grad",
        in_specs=[pl.BlockSpec(memory_space=pltpu.VMEM)] * 4,
        out_specs=pl.BlockSpec(memory_space=pltpu.SMEM),
        out_shape=jax.ShapeDtypeStruct((N_BUCKETS, N_HEADS), F32),
    )(ds1, ds2, ds3, buckets)


def _attn_fwd(proj, bias, g):
    dil = DILATIONS[g]
    rows = QB * dil
    nsb = S // rows
    has_prev = nsb > 1

    def residue(r):
        return pl.ds(r, QB, stride=dil) if dil > 1 else pl.ds(0, QB)

    def body(*refs):
        if has_prev:
            (q_ref, kc_ref, vc_ref, kp_ref, vp_ref, b_ref, o_ref, l_ref,
             qs, kcs, vcs, ao, ls, kps, vps) = refs
        else:
            q_ref, kc_ref, vc_ref, b_ref, o_ref, l_ref, qs, kcs, vcs, ao, ls = refs
        lane = lax.broadcasted_iota(jnp.int32, (QB, 128), 1)
        for h in range(4):
            sl = slice(h * HD, (h + 1) * HD)
            qs[...] = q_ref[:, sl].astype(F32)
            kcs[...] = kc_ref[:, sl].astype(F32)
            vcs[...] = vc_ref[:, sl].astype(F32)
            if has_prev:
                kps[...] = kp_ref[:, sl].astype(F32)
                vps[...] = vp_ref[:, sl].astype(F32)
            for r in range(dil):
                rr = residue(r)
                q = qs[rr, :].astype(BF16)
                s2 = _dot_nt(q, kcs[rr, :].astype(BF16)) * SCALE + b_ref[h, :, QB:]
                m = jnp.max(s2, axis=1, keepdims=True)
                if has_prev:
                    s1 = _dot_nt(q, kps[rr, :].astype(BF16)) * SCALE + b_ref[h, :, :QB]
                    m = jnp.maximum(m, jnp.max(s1, axis=1, keepdims=True))
                p2 = jnp.exp(s2 - m)
                l = jnp.sum(p2, axis=1, keepdims=True)
                o = _dot(p2.astype(BF16), vcs[rr, :].astype(BF16))
                if has_prev:
                    p1 = jnp.exp(s1 - m)
                    l = l + jnp.sum(p1, axis=1, keepdims=True)
                    o = o + _dot(p1.astype(BF16), vps[rr, :].astype(BF16))
                ao[rr, :] = o / l
                blk = slice(r * QB, (r + 1) * QB)
                lse = m + jnp.log(l)
                ls[blk, :] = jnp.where(lane == h, lse, 0.0 if h == 0 else ls[blk, :])
            o_ref[:, sl] = ao[...]
        for r in range(dil):
            l_ref[residue(r), :] = ls[r * QB:(r + 1) * QB, :]

    def row(b, n):
        return b * nsb + n

    def prev(b, n):
        return b * nsb + jnp.maximum(n - 1, 0)

    in_specs = [
        pl.BlockSpec((rows, GW), lambda b, n: (row(b, n), CB_Q + g)),
        pl.BlockSpec((rows, GW), lambda b, n: (row(b, n), CB_K + g)),
        pl.BlockSpec((rows, GW), lambda b, n: (row(b, n), CB_V + g)),
    ]
    args = [proj, proj, proj]
    scratch = [pltpu.VMEM((rows, HD), F32)] * 4 + [pltpu.VMEM((rows, 128), F32)]
    if has_prev:
        in_specs += [pl.BlockSpec((rows, GW), lambda b, n: (prev(b, n), CB_K + g)),
                     pl.BlockSpec((rows, GW), lambda b, n: (prev(b, n), CB_V + g))]
        args += [proj, proj]
        scratch += [pltpu.VMEM((rows, HD), F32)] * 2
    in_specs.append(pl.BlockSpec((None, None, 4, QB, 2 * QB),
                                 lambda b, n: (g, jnp.minimum(n, 1), 0, 0, 0)))
    args.append(bias)
    return pl.pallas_call(
        body, name=f"attn_fwd{g}",
        grid=(BL, nsb),
        in_specs=in_specs,
        out_specs=(pl.BlockSpec((rows, GW), lambda b, n: (row(b, n), 0)),
                   pl.BlockSpec((rows, 128), lambda b, n: (row(b, n), 0))),
        out_shape=(jax.ShapeDtypeStruct((T, GW), F32), jax.ShapeDtypeStruct((T, 128), F32)),
        scratch_shapes=scratch,
        compiler_params=pltpu.CompilerParams(vmem_limit_bytes=VMEM_LIMIT),
    )(*args)


def _attn_bwd(proj, d_out, stats, bias, dproj, g):
    dil = DILATIONS[g]
    rows = QB * dil
    nsb = S // rows
    has_prev = nsb > 1
    n_steps = nsb + 1 if has_prev else 1
    n_in = 7 + (2 if has_prev else 0)

    def residue(r):
        return pl.ds(r, QB, stride=dil) if dil > 1 else pl.ds(0, QB)

    def body(*refs):
        q_ref, kc_ref, vc_ref, do_ref, st_ref, b_ref = refs[:6]
        if has_prev:
            kp_ref, vp_ref = refs[6:8]
        out_ref, db_ref = refs[n_in], refs[n_in + 1]
        scr = refs[n_in + 2:]
        qs, kcs, vcs, dos, sts, aq, ak, av, sq, sk, sv, sems = scr[:12]
        if has_prev:
            kps, vps, carry = scr[12:]
        b, n = pl.program_id(0), pl.program_id(1)

        @pl.when((b == 0) & (n == 0))
        def _():
            db_ref[...] = jnp.zeros_like(db_ref)

        def flush_head(h):
            sq[:, h * HD:(h + 1) * HD] = aq[...].astype(BF16)
            sk[:, h * HD:(h + 1) * HD] = ak[...].astype(BF16)
            sv[:, h * HD:(h + 1) * HD] = av[...].astype(BF16)

        def write_block(blk_idx):
            row0 = pl.multiple_of(blk_idx * rows, rows)
            _write_columns([(sq, CB * (CB_Q + g)), (sk, CB * (CB_K + g)), (sv, CB * (CB_V + g))],
                           out_ref, row0, sems)

        if has_prev:
            @pl.when(n == nsb)
            def _():
                for h in range(4):
                    for r in range(dil):
                        blk = slice(r * QB, (r + 1) * QB)
                        aq[residue(r), :] = carry[blk, h * HD:(h + 1) * HD]
                        ak[residue(r), :] = carry[blk, GW + h * HD:GW + (h + 1) * HD]
                        av[residue(r), :] = carry[blk, 2 * GW + h * HD:2 * GW + (h + 1) * HD]
                    flush_head(h)
                write_block(b * nsb + nsb - 1)

        @pl.when(n < nsb)
        def _():
            for r in range(dil):
                sts[r * QB:(r + 1) * QB, :] = st_ref[residue(r), :]
            for h in range(4):
                sl = slice(h * HD, (h + 1) * HD)
                qs[...] = q_ref[:, sl].astype(F32)
                kcs[...] = kc_ref[:, sl].astype(F32)
                vcs[...] = vc_ref[:, sl].astype(F32)
                dos[...] = do_ref[:, sl].astype(F32)
                if has_prev:
                    kps[...] = kp_ref[:, sl].astype(F32)
                    vps[...] = vp_ref[:, sl].astype(F32)
                for r in range(dil):
                    rr = residue(r)
                    blk = slice(r * QB, (r + 1) * QB)
                    q, kc, vc = qs[rr, :].astype(BF16), kcs[rr, :].astype(BF16), vcs[rr, :].astype(BF16)
                    do = dos[rr, :].astype(BF16)
                    lse = sts[blk, h:h + 1]
                    delta = sts[blk, 4 + h:5 + h]
                    s2 = _dot_nt(q, kc) * SCALE + b_ref[h, :, QB:]
                    p2 = jnp.exp(s2 - lse)
                    ds2 = p2 * (_dot_nt(do, vc) - delta)
                    db_ref[h, :, QB:] += ds2
                    ds2b, p2b = ds2.astype(BF16), p2.astype(BF16)
                    dq = _dot(ds2b, kc)
                    dk_cur = _dot_tn(ds2b, q) * SCALE
                    dv_cur = _dot_tn(p2b, do)
                    if has_prev:
                        kp, vp = kps[rr, :].astype(BF16), vps[rr, :].astype(BF16)
                        s1 = _dot_nt(q, kp) * SCALE + b_ref[h, :, :QB]
                        p1 = jnp.exp(s1 - lse)
                        ds1 = p1 * (_dot_nt(do, vp) - delta)
                        db_ref[h, :, :QB] += ds1
                        ds1b, p1b = ds1.astype(BF16), p1.astype(BF16)
                        dq = dq + _dot(ds1b, kp)
                        dk_prev = _dot_tn(ds1b, q) * SCALE
                        dv_prev = _dot_tn(p1b, do)
                        cq = slice(h * HD, (h + 1) * HD)
                        ck = slice(GW + h * HD, GW + (h + 1) * HD)
                        cv = slice(2 * GW + h * HD, 2 * GW + (h + 1) * HD)

                        @pl.when(n > 0)
                        def _():
                            aq[rr, :] = carry[blk, cq]
                            ak[rr, :] = carry[blk, ck] + dk_prev
                            av[rr, :] = carry[blk, cv] + dv_prev

                        carry[blk, cq] = dq * SCALE
                        carry[blk, ck] = dk_cur
                        carry[blk, cv] = dv_cur
                    else:
                        aq[rr, :] = dq * SCALE
                        ak[rr, :] = dk_cur
                        av[rr, :] = dv_cur
                if has_prev:
                    @pl.when(n > 0)
                    def _():
                        flush_head(h)
                else:
                    flush_head(h)
            if has_prev:
                @pl.when(n > 0)
                def _():
                    write_block(b * nsb + n - 1)
            else:
                write_block(b)

    def row(b, n):
        return b * nsb + jnp.minimum(n, nsb - 1)

    def prev(b, n):
        return b * nsb + jnp.maximum(jnp.minimum(n, nsb - 1) - 1, 0)

    in_specs = [
        pl.BlockSpec((rows, GW), lambda b, n: (row(b, n), CB_Q + g)),
        pl.BlockSpec((rows, GW), lambda b, n: (row(b, n), CB_K + g)),
        pl.BlockSpec((rows, GW), lambda b, n: (row(b, n), CB_V + g)),
        pl.BlockSpec((rows, GW), lambda b, n: (row(b, n), 0)),
        pl.BlockSpec((rows, 128), lambda b, n: (row(b, n), 0)),
        pl.BlockSpec((None, None, 4, QB, 2 * QB),
                     lambda b, n: (g, jnp.minimum(jnp.minimum(n, nsb - 1), 1), 0, 0, 0)),
    ]
    args = [proj, proj, proj, d_out, stats, bias]
    scratch = [pltpu.VMEM((rows, HD), F32)] * 4 + [pltpu.VMEM((rows, 128), F32)] \
        + [pltpu.VMEM((rows, HD), F32)] * 3 + [pltpu.VMEM((rows, GW), BF16)] * 3 \
        + [pltpu.SemaphoreType.DMA((3,))]
    if has_prev:
        in_specs += [pl.BlockSpec((rows, GW), lambda b, n: (prev(b, n), CB_K + g)),
                     pl.BlockSpec((rows, GW), lambda b, n: (prev(b, n), CB_V + g))]
        args += [proj, proj]
        scratch += [pltpu.VMEM((rows, HD), F32)] * 2 + [pltpu.VMEM((rows, 3 * GW), F32)]
    in_specs.append(pl.BlockSpec(memory_space=pl.ANY))
    args.append(dproj)
    return pl.pallas_call(
        body, name=f"attn_bwd{g}",
        grid=(BL, n_steps),
        in_specs=in_specs,
        out_specs=(pl.BlockSpec(memory_space=pl.ANY),
                   pl.BlockSpec((4, QB, 2 * QB), lambda b, n: (0, 0, 0))),
        out_shape=(jax.ShapeDtypeStruct((T, NCOL), BF16),
                   jax.ShapeDtypeStruct((4, QB, 2 * QB), F32)),
        scratch_shapes=scratch,
        input_output_aliases={len(args) - 1: 0},
        compiler_params=pltpu.CompilerParams(vmem_limit_bytes=VMEM_LIMIT),
    )(*args)


def _tail(x2, tgt2, mod3, o_g, lse_g, proj, w_ao, w_co, w_o, conv_w, conv_b, ln_g, ln_b):
    tm = 256
    per_seq = S // tm
    halo = 16

    def body(x_ref, t_ref, mod_ref, o1_ref, o2_ref, o3_ref, l1_ref, l2_ref, l3_ref,
             ga_ref, u_ref, bg_ref, cg_ref, gc_ref, ma_ref, mc_ref, up_ref, cp_ref,
             wao_ref, wco_ref, wo_ref, cw_ref, cb_ref, lg_ref, lb_ref,
             dproj_ref, dyc_ref, do_ref, st_ref, dxd_ref,
             mg_ref, dy_ref, ain_ref, dao_ref, sin_ref, dso_ref, vec_ref,
             dga_s, dbg_s, dgm_s, sems):
        i = pl.program_id(0)
        bidx = i // per_seq
        first = (i % per_seq) == 0

        @pl.when(i == 0)
        def _():
            vec_ref[...] = jnp.zeros_like(vec_ref)

        l1, l2, l3 = l1_ref[...], l2_ref[...], l3_ref[...]
        mx = jnp.maximum(jnp.maximum(l1, l2), l3)
        e1, e2, e3 = jnp.exp(l1 - mx), jnp.exp(l2 - mx), jnp.exp(l3 - mx)
        esum = e1 + e2 + e3
        lse_tot = mx + jnp.log(esum)
        w1, w2, w3 = e1 / esum, e2 / esum, e3 / esum

        def per_head(wv):
            return jnp.concatenate([jnp.broadcast_to(wv[:, h:h + 1], (tm, HD)) for h in range(4)], axis=1)

        o = per_head(w1) * o1_ref[...] + per_head(w2) * o2_ref[...] + per_head(w3) * o3_ref[...]

        ga = ga_ref[...].astype(F32)
        sig_ga = _sigmoid(ga)
        silu_ga = ga * sig_ga
        a_in = (o * silu_ga).astype(BF16)
        a_out = _dot(a_in, wao_ref[...])

        u = u_ref[...].astype(F32)
        cg = cg_ref[...].astype(F32)
        z = cg * u
        zp = cp_ref[...].astype(F32) * up_ref[...].astype(F32)
        zp = jnp.where(first, 0.0, zp)
        zcat = jnp.concatenate([zp, z], axis=0)
        z1 = pltpu.roll(zcat, 1, 0)[halo:]
        z2 = pltpu.roll(zcat, 2, 0)[halo:]
        y_conv = cw_ref[0:1, :] * z2 + cw_ref[1:2, :] * z1 + cw_ref[2:3, :] * z + cb_ref[...]
        gc = gc_ref[...].astype(F32)
        sig_gc = _sigmoid(gc)
        silu_gc = gc * sig_gc
        bg = bg_ref[...].astype(F32)
        s_in = (bg * y_conv * silu_gc).astype(BF16)
        s_out = _dot(s_in, wco_ref[...])

        sa = _sigmoid(ma_ref[...].astype(F32))
        sc = _sigmoid(mc_ref[...].astype(F32))
        merged = (sa * a_out + sc * s_out).astype(BF16)
        y = _dot(merged, wo_ref[...])
        gate1 = 1.0 + mod_ref[0, 2:3, :]
        xv = x_ref[...]
        resid = ALPHA * xv + gate1 * y
        mu = jnp.mean(resid, axis=1, keepdims=True)
        xc = resid - mu
        var = jnp.mean(xc * xc, axis=1, keepdims=True)
        rstd = lax.rsqrt(var + LN_EPS)
        xhat = xc * rstd
        lg = lg_ref[...]
        err = xhat * lg + lb_ref[...] - t_ref[...]
        vec_ref[3:4, :] += (0.5 / D) * jnp.sum(err * err, axis=0, keepdims=True)

        dout = err * (1.0 / D)
        vec_ref[1:2, :] += jnp.sum(dout * xhat, axis=0, keepdims=True)
        vec_ref[2:3, :] += jnp.sum(dout, axis=0, keepdims=True)
        dxh = dout * lg
        dres = rstd * (dxh - jnp.mean(dxh, axis=1, keepdims=True)
                       - xhat * jnp.mean(dxh * xhat, axis=1, keepdims=True))
        dxd_ref[...] = ALPHA * dres
        dgate = jnp.sum(dres * y, axis=0, keepdims=True)
        vec_ref[4:5, :] += jnp.where(bidx == 0, dgate, 0.0)
        vec_ref[5:6, :] += jnp.where(bidx == 1, dgate, 0.0)
        dy = (dres * gate1).astype(BF16)

        dmerged = _dot_nt(dy, wo_ref[...])
        da_out = (dmerged * sa).astype(BF16)
        ds_out = (dmerged * sc).astype(BF16)
        dgm_s[:, 2 * D:3 * D] =(dmerged * s_out * sc * (1.0 - sc)).astype(BF16)
        dgm_s[:, D:2 * D] =(dmerged * a_out * sa * (1.0 - sa)).astype(BF16)
        da_in = _dot_nt(da_out, wao_ref[...])
        ds_in = _dot_nt(ds_out, wco_ref[...])

        d_o = da_in * silu_ga
        do_ref[...] = d_o.astype(BF16)
        dga_s[...] =(da_in * o * (sig_ga * (1.0 + ga * (1.0 - sig_ga)))).astype(BF16)
        lane = lax.broadcasted_iota(jnp.int32, (tm, 128), 1)
        stats = lse_tot
        od = o * d_o
        for h in range(4):
            delta = jnp.sum(od[:, h * HD:(h + 1) * HD], axis=1, keepdims=True)
            stats = jnp.where(lane == 4 + h, delta, stats)
        st_ref[...] = stats

        dbg_s[...] =(ds_in * y_conv * silu_gc).astype(BF16)
        dyc = ds_in * bg * silu_gc
        dyc_ref[...] = dyc
        vec_ref[0:1, :] += jnp.sum(dyc, axis=0, keepdims=True)
        dgm_s[:, 0:D] =(ds_in * bg * y_conv * (sig_gc * (1.0 + gc * (1.0 - sig_gc)))).astype(BF16)

        mg_ref[...] = merged
        dy_ref[...] = dy
        ain_ref[...] = a_in
        dao_ref[...] = da_out
        sin_ref[...] = s_in
        dso_ref[...] = ds_out
        _write_columns([(dga_s, CB * CB_GA), (dbg_s, D * KB_BG), (dgm_s, D * KB_GC)],
                       dproj_ref, pl.multiple_of(i * tm, tm), sems)

    def tile(width, cblk=0):
        return pl.BlockSpec((tm, width), lambda i: (i, cblk))

    def whole(shape):
        return pl.BlockSpec(shape, lambda i: tuple(0 for _ in shape))

    prev_rows = lambda i: (jnp.maximum(i * (tm // halo) - 1, 0),)
    in_specs = [
        tile(D), tile(D), pl.BlockSpec((1, 3, D), lambda i: (i // per_seq, 0, 0)),
        tile(GW), tile(GW), tile(GW), tile(128), tile(128), tile(128),
        tile(GW, CB_GA), tile(D, KB_U), tile(D, KB_BG), tile(D, KB_CG), tile(D, KB_GC),
        tile(D, KB_MA), tile(D, KB_MC),
        pl.BlockSpec((halo, D), lambda i: (*prev_rows(i), KB_U)),
        pl.BlockSpec((halo, D), lambda i: (*prev_rows(i), KB_CG)),
        whole((GW, D)), whole((D, D)), whole((D, D)),
        whole((3, D)), whole((1, D)), whole((1, D)), whole((1, D)),
    ]
    out_specs = (
        pl.BlockSpec(memory_space=pl.ANY), tile(D), tile(GW), tile(128), tile(D),
        tile(D), tile(D), tile(GW), tile(D), tile(D), tile(D),
        pl.BlockSpec((8, D), lambda i: (0, 0)),
    )
    out_shape = (
        jax.ShapeDtypeStruct((T, NCOL), BF16),
        jax.ShapeDtypeStruct((T, D), F32),
        jax.ShapeDtypeStruct((T, GW), BF16),
        jax.ShapeDtypeStruct((T, 128), F32),
        jax.ShapeDtypeStruct((T, D), F32),
        jax.ShapeDtypeStruct((T, D), BF16),
        jax.ShapeDtypeStruct((T, D), BF16),
        jax.ShapeDtypeStruct((T, GW), BF16),
        jax.ShapeDtypeStruct((T, D), BF16),
        jax.ShapeDtypeStruct((T, D), BF16),
        jax.ShapeDtypeStruct((T, D), BF16),
        jax.ShapeDtypeStruct((8, D), F32),
    )
    return pl.pallas_call(
        body, name="tail",
        grid=(T // tm,),
        in_specs=in_specs, out_specs=out_specs, out_shape=out_shape,
        scratch_shapes=[pltpu.VMEM((tm, GW), BF16), pltpu.VMEM((tm, D), BF16), pltpu.VMEM((tm, 3 * D), BF16),
                        pltpu.SemaphoreType.DMA((3,))],
        compiler_params=pltpu.CompilerParams(vmem_limit_bytes=VMEM_LIMIT),
    )(x2, tgt2, mod3, *o_g, *lse_g, proj, proj, proj, proj, proj, proj, proj, proj, proj,
      w_ao, w_co, w_o, conv_w, conv_b, ln_g, ln_b)


def _conv_bwd(dyc, proj, conv_w, dproj):
    tm = 512
    per_seq = S // tm
    halo = 16

    def body(d_ref, dn_ref, u_ref, c_ref, up_ref, cp_ref, cw_ref, _, dproj_ref, g_ref, du_s, dc_s, sems):
        i = pl.program_id(0)
        first = (i % per_seq) == 0
        last = (i % per_seq) == per_seq - 1

        @pl.when(i == 0)
        def _():
            g_ref[...] = jnp.zeros_like(g_ref)

        d = d_ref[...]
        dn = jnp.where(last, 0.0, dn_ref[...])
        dcat = jnp.concatenate([d, dn], axis=0)
        d1 = pltpu.roll(dcat, tm + 8 - 1, 0)[:tm]
        d2 = pltpu.roll(dcat, tm + 8 - 2, 0)[:tm]
        dz = cw_ref[2:3, :] * d + cw_ref[1:2, :] * d1 + cw_ref[0:1, :] * d2
        u = u_ref[...].astype(F32)
        cg = c_ref[...].astype(F32)
        du_s[...] = (dz * cg).astype(BF16)
        dc_s[...] = (dz * u).astype(BF16)
        _write_columns([(du_s, D * KB_U), (dc_s, D * KB_CG)], dproj_ref, pl.multiple_of(i * tm, tm), sems)

        z = cg * u
        zp = jnp.where(first, 0.0, cp_ref[...].astype(F32) * up_ref[...].astype(F32))
        zcat = jnp.concatenate([zp, z], axis=0)
        z1 = pltpu.roll(zcat, 1, 0)[halo:]
        z2 = pltpu.roll(zcat, 2, 0)[halo:]
        g_ref[0:1, :] += jnp.sum(d * z2, axis=0, keepdims=True)
        g_ref[1:2, :] += jnp.sum(d * z1, axis=0, keepdims=True)
        g_ref[2:3, :] += jnp.sum(d * z, axis=0, keepdims=True)

    n_tiles = T // tm
    prev_rows = lambda i: jnp.maximum(i * (tm // halo) - 1, 0)
    next_rows = lambda i: jnp.minimum((i + 1) * (tm // 8), T // 8 - 1)
    return pl.pallas_call(
        body, name="conv_bwd",
        grid=(n_tiles,),
        in_specs=[pl.BlockSpec((tm, D), lambda i: (i, 0)),
                  pl.BlockSpec((8, D), lambda i: (next_rows(i), 0)),
                  pl.BlockSpec((tm, D), lambda i: (i, KB_U)),
                  pl.BlockSpec((tm, D), lambda i: (i, KB_CG)),
                  pl.BlockSpec((halo, D), lambda i: (prev_rows(i), KB_U)),
                  pl.BlockSpec((halo, D), lambda i: (prev_rows(i), KB_CG)),
                  pl.BlockSpec((3, D), lambda i: (0, 0)),
                  pl.BlockSpec(memory_space=pl.ANY)],
        out_specs=(pl.BlockSpec(memory_space=pl.ANY),
                   pl.BlockSpec((8, D), lambda i: (0, 0))),
        out_shape=(jax.ShapeDtypeStruct((T, NCOL), BF16),
                   jax.ShapeDtypeStruct((8, D), F32)),
        scratch_shapes=[pltpu.VMEM((tm, D), BF16), pltpu.VMEM((tm, D), BF16), pltpu.SemaphoreType.DMA((2,))],
        input_output_aliases={7: 0},
        compiler_params=pltpu.CompilerParams(vmem_limit_bytes=VMEM_LIMIT),
    )(dyc, dyc, proj, proj, proj, proj, conv_w, dproj)


def _dh_dx(dproj, w_in_all, x2, dxd, mod3):
    tm = 512
    per_seq = S // tm

    def body(d_ref, w_ref, x_ref, dxd_ref, mod_ref, gx_ref, vec_ref, acc):
        i, jj = pl.program_id(0), pl.program_id(1)

        @pl.when((i == 0) & (jj == 0))
        def _():
            vec_ref[...] = jnp.zeros_like(vec_ref)

        @pl.when(jj == 0)
        def _():
            acc[...] = jnp.zeros_like(acc)

        acc[...] += _dot_nt(d_ref[...], w_ref[...])

        @pl.when(jj == N_DEV - 1)
        def _():
            dh = acc[...]
            bidx = i // per_seq
            gx_ref[...] = dxd_ref[...] + dh * (1.0 + mod_ref[0, 1:2, :])
            dshift = jnp.sum(dh, axis=0, keepdims=True)
            dscale = jnp.sum(dh * x_ref[...], axis=0, keepdims=True)
            vec_ref[0:1, :] += jnp.where(bidx == 0, dshift, 0.0)
            vec_ref[1:2, :] += jnp.where(bidx == 1, dshift, 0.0)
            vec_ref[2:3, :] += jnp.where(bidx == 0, dscale, 0.0)
            vec_ref[3:4, :] += jnp.where(bidx == 1, dscale, 0.0)

    return pl.pallas_call(
        body, name="dh_dx",
        grid=(T // tm, N_DEV),
        in_specs=[
            pl.BlockSpec((tm, SHARD), lambda i, jj: (i, jj)),
            pl.BlockSpec((None, D, SHARD), lambda i, jj: (jj, 0, 0)),
            pl.BlockSpec((tm, D), lambda i, jj: (i, 0)),
            pl.BlockSpec((tm, D), lambda i, jj: (i, 0)),
            pl.BlockSpec((1, 3, D), lambda i, jj: (i // per_seq, 0, 0))],
        out_specs=(pl.BlockSpec((tm, D), lambda i, jj: (i, 0)),
                   pl.BlockSpec((8, D), lambda i, jj: (0, 0))),
        out_shape=(jax.ShapeDtypeStruct((T, D), F32), jax.ShapeDtypeStruct((8, D), F32)),
        scratch_shapes=[pltpu.VMEM((tm, D), F32)],
        compiler_params=pltpu.CompilerParams(vmem_limit_bytes=VMEM_LIMIT),
    )(dproj, w_in_all, x2, dxd, mod3)


def _mm_tn(a, b, tn, blocks_leading, name):
    kk, m = a.shape
    n = b.shape[1]
    tk = 1024

    def body(a_ref, b_ref, o_ref, acc):
        @pl.when(pl.program_id(1) == 0)
        def _():
            acc[...] = jnp.zeros_like(acc)

        acc[...] += _dot_tn(a_ref[...], b_ref[...])

        @pl.when(pl.program_id(1) == kk // tk - 1)
        def _():
            o_ref[...] = acc[...].astype(BF16)

    if blocks_leading:
        out_spec = pl.BlockSpec((None, m, tn), lambda j, k: (j, 0, 0))
        out_shape = jax.ShapeDtypeStruct((n // tn, m, tn), BF16)
    else:
        out_spec = pl.BlockSpec((m, tn), lambda j, k: (0, j))
        out_shape = jax.ShapeDtypeStruct((m, n), BF16)
    return pl.pallas_call(
        body, name=name,
        grid=(n // tn, kk // tk),
        in_specs=[pl.BlockSpec((tk, m), lambda j, k: (k, 0)),
                  pl.BlockSpec((tk, tn), lambda j, k: (k, j))],
        out_specs=out_spec, out_shape=out_shape,
        scratch_shapes=[pltpu.VMEM((m, tn), F32)],
        compiler_params=pltpu.CompilerParams(vmem_limit_bytes=VMEM_LIMIT),
    )(a, b)


def _adamw(parts, w, m, v, name, row_tile=None):
    n_parts, rows, cols = parts.shape
    tr = rows if row_tile is None else row_tile
    c1 = 1.0 - ADAM_B1 ** ADAM_STEP
    c2 = 1.0 - ADAM_B2 ** ADAM_STEP

    def body(p_ref, w_ref, m_ref, v_ref, g_ref, d_ref, nm_ref, nv_ref):
        g = p_ref[0].astype(F32)
        for s in range(1, n_parts):
            g = g + p_ref[s].astype(F32)
        nm = ADAM_B1 * m_ref[...] + (1.0 - ADAM_B1) * g
        nv = ADAM_B2 * v_ref[...] + (1.0 - ADAM_B2) * (g * g)
        m_hat = nm / c1
        v_hat = nv / c2
        g_ref[...] = g
        d_ref[...] = -ADAM_LR * (m_hat / (jnp.sqrt(v_hat) + ADAM_EPS) + ADAM_WD * w_ref[...])
        nm_ref[...] = nm
        nv_ref[...] = nv

    blk = pl.BlockSpec((tr, cols), lambda i: (i, 0))
    shp = jax.ShapeDtypeStruct((rows, cols), F32)
    return pl.pallas_call(
        body, name=name,
        grid=(rows // tr,),
        in_specs=[pl.BlockSpec((n_parts, tr, cols), lambda i: (0, i, 0)), blk, blk, blk],
        out_specs=(blk, blk, blk, blk),
        out_shape=(shp, shp, shp, shp),
        compiler_params=pltpu.CompilerParams(vmem_limit_bytes=VMEM_LIMIT),
    )(parts, w, m, v)


def _loss_sum(rows):
    def body(r_ref, o_ref):
        o_ref[...] = jnp.sum(jnp.sum(r_ref[...], axis=0, keepdims=True), axis=1, keepdims=True)

    return pl.pallas_call(body, name="loss_sum", out_shape=jax.ShapeDtypeStruct((1, 1), F32))(rows)


def _local_step(x2, tgt2, mod3, w_in_all, w_ao, w_co, w_o, conv_w, conv_b, rel_bias, ln_g, ln_b):
    buckets_np, masks_np = _bucket_maps()
    buckets, masks = jnp.asarray(buckets_np), jnp.asarray(masks_np)

    h = _prep_h(x2, mod3)
    proj = _proj(h, w_in_all)
    bias = _bias_expand(rel_bias, buckets, masks)
    fwd = [_attn_fwd(proj, bias, g) for g in range(3)]
    o_g = [f[0] for f in fwd]
    lse_g = [f[1] for f in fwd]

    (dproj, dyc, d_o, stats, dxd, merged, dy, a_in, da_out, s_in, ds_out, tail_vec) = _tail(
        x2, tgt2, mod3, o_g, lse_g, proj, w_ao, w_co, w_o, conv_w, conv_b, ln_g, ln_b)

    dbias = []
    for g in range(3):
        dproj, db = _attn_bwd(proj, d_o, stats, bias, dproj, g)
        dbias.append(db)
    g_rel_bias = _bias_grad(*dbias, buckets)
    dproj, conv_vec = _conv_bwd(dyc, proj, conv_w, dproj)

    grad_x, mod_vec = _dh_dx(dproj, w_in_all, x2, dxd, mod3)
    gw_in = _mm_tn(h, dproj, SHARD, True, "gw_in")
    gw_o = _mm_tn(merged, dy, 512, False, "gw_o")
    gw_co = _mm_tn(s_in, ds_out, 512, False, "gw_conv_out")
    gw_ao = _mm_tn(a_in, da_out, 128, True, "gw_attn_out")
    return grad_x, gw_in, gw_ao, gw_co, gw_o, conv_vec, g_rel_bias, tail_vec, mod_vec


def kernel(x, c, w_ada, b_ada, w_in, conv_w, conv_b, rel_bias, w_attn_out, w_conv_out, w_o, ln_g, ln_b, loss_target, m_w_ada, m_b_ada, m_w_in, m_conv_w, m_conv_b, m_rel_bias, m_w_attn_out, m_w_conv_out, m_w_o, m_ln_g, m_ln_b, v_w_ada, v_b_ada, v_w_in, v_conv_w, v_conv_b, v_rel_bias, v_w_attn_out, v_w_conv_out, v_w_o, v_ln_g, v_ln_b):
    me = _my_index()
    x2 = x.reshape(T, D)
    tgt2 = loss_target.reshape(T, D)

    w_in_all, w_ao_g, w_co_g, w_o_g, conv_w_g, c_g = _all_gather(
        [w_in[0].astype(BF16), w_attn_out[0].astype(BF16), w_conv_out[0].astype(BF16),
         w_o[0].astype(BF16), conv_w[0], c],
        [jax.ShapeDtypeStruct((N_DEV, D, SHARD), BF16),
         jax.ShapeDtypeStruct((N_DEV, GW, D // N_DEV), BF16),
         jax.ShapeDtypeStruct((N_DEV, D // N_DEV, D), BF16),
         jax.ShapeDtypeStruct((N_DEV, D // N_DEV, D), BF16),
         jax.ShapeDtypeStruct((N_DEV, 3, D // N_DEV), F32),
         jax.ShapeDtypeStruct((N_DEV, BL, D), F32)],
        [_slot_leading] * 6,
        "gather_weights")
    w_ao_full = jnp.transpose(w_ao_g, (1, 0, 2)).reshape(GW, D)
    w_co_full = w_co_g.reshape(D, D)
    w_o_full = w_o_g.reshape(D, D)
    conv_w_full = jnp.transpose(conv_w_g, (1, 0, 2)).reshape(3, D)
    c_all = c_g.reshape(N_DEV * BL, D)

    b_cols = lax.dynamic_slice(b_ada, (0, me * ADA_SHARD), (1, ADA_SHARD))
    mod_cols = _ada_fwd(c_all, w_ada[0], b_cols)
    (mod_g,) = _all_gather([mod_cols], [jax.ShapeDtypeStruct((N_DEV, N_DEV * BL, ADA_SHARD), F32)],
                           [_slot_leading], "gather_mod")
    mod_all = jnp.transpose(mod_g, (1, 0, 2)).reshape(N_DEV * BL, 3 * D)
    mod3 = lax.dynamic_slice(mod_all, (me * BL, 0), (BL, 3 * D)).reshape(BL, 3, D)

    (grad_x, gw_in, gw_ao, gw_co, gw_o, conv_vec, g_rel_bias, tail_vec, mod_vec) = _local_step(
        x2, tgt2, mod3, w_in_all, w_ao_full, w_co_full, w_o_full,
        conv_w_full, conv_b, rel_bias, ln_g, ln_b)

    g_conv_w_blocks = jnp.transpose(conv_vec[0:3].reshape(3, N_DEV, D // N_DEV), (1, 0, 2))
    partials = [gw_in, gw_ao, gw_co.reshape(N_DEV, D // N_DEV, D), gw_o.reshape(N_DEV, D // N_DEV, D),
                g_conv_w_blocks]
    sib = _pair_exchange(
        partials,
        [jax.ShapeDtypeStruct((4, D, SHARD), BF16),
         jax.ShapeDtypeStruct((4, GW, D // N_DEV), BF16),
         jax.ShapeDtypeStruct((4, D // N_DEV, D), BF16),
         jax.ShapeDtypeStruct((4, D // N_DEV, D), BF16),
         jax.ShapeDtypeStruct((4, 3, D // N_DEV), F32)],
        [_slot_leading] * 5,
        "pair_grads")
    tiles = [256, None, None, None, None]
    names = ["w_in", "w_attn_out", "w_conv_out", "w_o", "conv_w"]
    core = lax.axis_index("c").astype(jnp.int32).reshape(1)
    chip_sums = [_pair_add(core, partials[a], sib[a], tiles[a], "pair_add_" + names[a]) for a in range(5)]
    r_in, r_ao, r_co, r_o, r_cw = _chip_exchange(chip_sums, "chip_grads")

    small = jnp.concatenate([
        tail_vec[0:4],
        jnp.pad(g_rel_bias.reshape(1, N_BUCKETS * N_HEADS), ((0, 0), (0, D - N_BUCKETS * N_HEADS))),
        jnp.zeros((3, D), F32)], axis=0)
    dmod = jnp.concatenate([mod_vec[0:2], mod_vec[2:4], tail_vec[4:6]], axis=1)
    small_g, dmod_g = _all_gather(
        [small, dmod],
        [jax.ShapeDtypeStruct((N_DEV, 8, D), F32), jax.ShapeDtypeStruct((N_DEV, BL, 3 * D), F32)],
        [_slot_leading] * 2, "gather_small")
    dmod_all = dmod_g.reshape(N_DEV * BL, 3 * D)
    loss = _loss_sum(small_g[:, 3, :]).reshape(())
    g_w_ada = _ada_bwd(jnp.transpose(c_all), lax.dynamic_slice(dmod_all, (0, me * ADA_SHARD),
                                                               (N_DEV * BL, ADA_SHARD)))

    def upd(parts, w, m, v, name, row_tile=None):
        shape = w.shape
        w2, m2, v2 = (t.reshape(parts.shape[1:]) for t in (w, m, v))
        return tuple(t.reshape(shape) for t in _adamw(parts, w2, m2, v2, name, row_tile))

    res = {
        "w_ada": upd(g_w_ada[None], w_ada, m_w_ada, v_w_ada, "adam_w_ada", 256),
        "b_ada": upd(dmod_all[:, None, :], b_ada, m_b_ada, v_b_ada, "adam_b_ada"),
        "w_in": upd(r_in, w_in, m_w_in, v_w_in, "adam_w_in", 128),
        "conv_w": upd(r_cw, conv_w, m_conv_w, v_conv_w, "adam_conv_w"),
        "conv_b": upd(small_g[:, 0:1, :], conv_b, m_conv_b, v_conv_b, "adam_conv_b"),
        "rel_bias": upd(small_g[:, 4, :N_BUCKETS * N_HEADS].reshape(N_DEV, N_BUCKETS, N_HEADS),
                        rel_bias, m_rel_bias, v_rel_bias, "adam_rel_bias"),
        "w_attn_out": upd(r_ao, w_attn_out, m_w_attn_out, v_w_attn_out, "adam_w_attn_out"),
        "w_conv_out": upd(r_co, w_conv_out, m_w_conv_out, v_w_conv_out, "adam_w_conv_out"),
        "w_o": upd(r_o, w_o, m_w_o, v_w_o, "adam_w_o"),
        "ln_g": upd(small_g[:, 1:2, :], ln_g, m_ln_g, v_ln_g, "adam_ln_g"),
        "ln_b": upd(small_g[:, 2:3, :], ln_b, m_ln_b, v_ln_b, "adam_ln_b"),
    }
    order = ["w_ada", "b_ada", "w_in", "conv_w", "conv_b", "rel_bias", "w_attn_out", "w_conv_out",
             "w_o", "ln_g", "ln_b"]
    outs = [loss, grad_x.reshape(BL, S, D)]
    for k in range(4):
        outs += [res[name][k] for name in order]
    return tuple(outs)
```

```python
import functools
import math

import numpy as np
import jax
import jax.numpy as jnp
from jax import lax
from jax.experimental import pallas as pl
from jax.experimental.pallas import tpu as pltpu

F32 = jnp.float32
BF16 = jnp.bfloat16
MESH = pl.DeviceIdType.MESH

N_DEV = 8
D = 1024
S = 2048
BL = 2
T = BL * S
NCOL = 11264
SHARD = NCOL // N_DEV
CB = 512
NCB = NCOL // CB
HD = 128
GW = 512
QB = 128
DILATIONS = (1, 4, 16)
N_STEPS = 128
N_BUCKETS = 32
N_HEADS = 12
ALPHA = 2.0 ** 0.25
LN_EPS = 1e-5
NEG_INF = -1e30
SCALE = HD ** -0.5
ADA_SHARD = 3 * D // N_DEV

CB_Q, CB_K, CB_V, CB_GA = 0, 3, 6, 9
KB_U, KB_BG, KB_CG, KB_GC, KB_MA, KB_MC = 5, 6, 7, 8, 9, 10

ADAM_LR, ADAM_B1, ADAM_B2, ADAM_EPS, ADAM_WD, ADAM_STEP = 0.001, 0.9, 0.999, 1e-08, 0.01, 10

VMEM_LIMIT = 56 * 1024 * 1024


def _dot(a, b):
    return jnp.dot(a, b, preferred_element_type=F32)


def _dot_nt(a, b):
    return lax.dot_general(a, b, (((1,), (1,)), ((), ())), preferred_element_type=F32)


def _dot_tn(a, b):
    return lax.dot_general(a, b, (((0,), (0,)), ((), ())), preferred_element_type=F32)


def _sigmoid(v):
    return 1.0 / (1.0 + jnp.exp(-v))


def _write_columns(pieces, dst_hbm, row0, sems):
    copies = []
    for k, (src, col0) in enumerate(pieces):
        rows, width = src.shape
        copies.append(pltpu.make_async_copy(
            src, dst_hbm.at[pl.ds(row0, rows), pl.ds(col0, width)], sems.at[k]))
    for cp in copies:
        cp.start()
    for cp in copies:
        cp.wait()


def _my_index():
    return 4 * lax.axis_index("x") + 2 * lax.axis_index("y") + lax.axis_index("c")


def _slot_leading(ref, slot):
    return ref.at[slot]


def _all_gather(arrs, out_shapes, slot_fns, name):
    n = len(arrs)

    def body(*refs):
        ins, outs = refs[:n], refs[n:2 * n]
        send_sems, recv_sems, local_sems = refs[2 * n:2 * n + 3]
        stage = refs[2 * n + 3:]
        x, y, c = lax.axis_index("x"), lax.axis_index("y"), lax.axis_index("c")
        me, sibling = (x, y, c), (x, y, 1 - c)
        chips = [(1 - x, y), (x, 1 - y), (1 - x, 1 - y)]

        def blk(a, dev):
            return slot_fns[a](outs[a], 4 * dev[0] + 2 * dev[1] + dev[2])

        def copy(a, k, block, to, src=None):
            dst = blk(a, block)
            return pltpu.make_async_remote_copy(
                src_ref=dst if src is None else src, dst_ref=dst,
                send_sem=send_sems.at[a * 7 + k], recv_sem=recv_sems.at[a * 7 + k],
                device_id=to, device_id_type=MESH)

        first = []
        for a in range(n):
            first.append(copy(a, 0, me, sibling, src=ins[a]))
            first += [copy(a, 1 + j, me, (*chip, c), src=ins[a]) for j, chip in enumerate(chips)]
        for cp in first:
            cp.start()
        loads = [pltpu.make_async_copy(ins[a], stage[a], local_sems.at[a]) for a in range(n)]
        for cp in loads:
            cp.start()
        for cp in loads:
            cp.wait()
        mine = [pltpu.make_async_copy(stage[a], blk(a, me), local_sems.at[a]) for a in range(n)]
        for cp in mine:
            cp.start()
        passed = []
        for j, chip in enumerate(chips):
            for a in range(n):
                copy(a, 1 + j, (*chip, c), me).wait_recv()
                fwd = copy(a, 4 + j, (*chip, c), sibling)
                fwd.start()
                passed.append(fwd)
        for a in range(n):
            copy(a, 0, sibling, me).wait_recv()
        for j, chip in enumerate(chips):
            for a in range(n):
                copy(a, 4 + j, (*chip, 1 - c), me).wait_recv()
        for cp in first + passed:
            cp.wait_send()
        for cp in mine:
            cp.wait()

    any_spec = pl.BlockSpec(memory_space=pl.ANY)
    return pl.pallas_call(
        body, name=name,
        out_shape=tuple(out_shapes),
        in_specs=[any_spec] * n,
        out_specs=tuple([any_spec] * n),
        scratch_shapes=[pltpu.SemaphoreType.DMA((7 * n,)), pltpu.SemaphoreType.DMA((7 * n,)),
                        pltpu.SemaphoreType.DMA((n,))]
                       + [pltpu.VMEM(a.shape, a.dtype) for a in arrs],
    )(*arrs)


def _pair_exchange(arrs, shapes4, src_fns, name):
    n = len(arrs)

    def body(*refs):
        ins, recv = refs[:n], refs[n:2 * n]
        send_sems, recv_sems = refs[2 * n:]
        x, y, c = lax.axis_index("x"), lax.axis_index("y"), lax.axis_index("c")
        sibling = (x, y, 1 - c)
        remote = []
        for a in range(n):
            for q in range(4):
                remote.append(pltpu.make_async_remote_copy(
                    src_ref=src_fns[a](ins[a], 2 * q + 1 - c), dst_ref=recv[a].at[q],
                    send_sem=send_sems.at[a * 4 + q], recv_sem=recv_sems.at[a * 4 + q],
                    device_id=sibling, device_id_type=MESH))
        for cp in remote:
            cp.start()
        for cp in remote:
            cp.wait()

    any_spec = pl.BlockSpec(memory_space=pl.ANY)
    return pl.pallas_call(
        body, name=name,
        out_shape=tuple(shapes4),
        in_specs=[any_spec] * n,
        out_specs=tuple([any_spec] * n),
        scratch_shapes=[pltpu.SemaphoreType.DMA((4 * n,))] * 2,
    )(*arrs)


def _chip_copies(ins, outs, send_sems, recv_sems, local_sems):
    n = len(ins)
    x, y, c = lax.axis_index("x"), lax.axis_index("y"), lax.axis_index("c")
    my_chip = 2 * x + y

    def peer_of(k):
        return ((1 - x) if (k >> 1) & 1 else x, (1 - y) if k & 1 else y, c)

    def copy(a, k, out_chip):
        peer = peer_of(k)
        return pltpu.make_async_remote_copy(
            src_ref=ins[a].at[2 * peer[0] + peer[1]], dst_ref=outs[a].at[out_chip],
            send_sem=send_sems.at[a * 3 + k - 1], recv_sem=recv_sems.at[a * 3 + k - 1],
            device_id=peer, device_id_type=MESH)

    sends = [copy(a, k, my_chip) for k in range(1, 4) for a in range(n)]
    arrivals = []
    for k in range(1, 4):
        peer = peer_of(k)
        arrivals += [copy(a, k, 2 * peer[0] + peer[1]) for a in range(n)]
    mine = [pltpu.make_async_copy(ins[a].at[my_chip], outs[a].at[my_chip], local_sems.at[a])
            for a in range(n)]
    return sends, arrivals, mine


def _pair_add(core, mine, theirs, row_tile, name):
    _, rows, cols = theirs.shape
    tr = rows if row_tile is None else row_tile

    def body(core_ref, a_ref, b_ref, o_ref):
        o_ref[...] = (a_ref[...].astype(F32) + b_ref[...].astype(F32)).astype(o_ref.dtype)

    blk = pl.BlockSpec((None, tr, cols), lambda q, i, core_ref: (q, i, 0))
    return pl.pallas_call(
        body, name=name,
        grid_spec=pltpu.PrefetchScalarGridSpec(
            num_scalar_prefetch=1,
            grid=(4, rows // tr),
            in_specs=[pl.BlockSpec((None, tr, cols), lambda q, i, core_ref: (2 * q + core_ref[0], i, 0)), blk],
            out_specs=blk),
        out_shape=jax.ShapeDtypeStruct(theirs.shape, theirs.dtype),
    )(core, mine, theirs)


def _ada_fwd(c_all, w_ada, b_cols):
    def body(c_ref, w_ref, b_ref, o_ref):
        cv = c_ref[...]
        sc = cv * _sigmoid(cv)
        o_ref[...] = jnp.dot(sc, w_ref[...], preferred_element_type=F32,
                             precision=lax.Precision.HIGHEST) + b_ref[...]

    return pl.pallas_call(
        body, name="ada_fwd",
        out_shape=jax.ShapeDtypeStruct((c_all.shape[0], w_ada.shape[1]), F32),
    )(c_all, w_ada, b_cols)


def _ada_bwd(c_all_t, dmod_cols):
    def body(c_ref, d_ref, o_ref):
        cv = c_ref[...]
        sc = cv * _sigmoid(cv)
        o_ref[...] = jnp.dot(sc, d_ref[...], preferred_element_type=F32,
                             precision=lax.Precision.HIGHEST)

    return pl.pallas_call(
        body, name="ada_bwd",
        out_shape=jax.ShapeDtypeStruct((c_all_t.shape[0], dmod_cols.shape[1]), F32),
    )(c_all_t, dmod_cols)


def _prep_h(x2, mod3):
    ts = 512
    per_seq = S // ts

    def body(x_ref, mod_ref, h_ref):
        shift = mod_ref[0, 0:1, :]
        scale = mod_ref[0, 1:2, :]
        h_ref[...] = (x_ref[...] * (1.0 + scale) + shift).astype(BF16)

    return pl.pallas_call(
        body, name="prep_h",
        grid=(T // ts,),
        in_specs=[pl.BlockSpec((ts, D), lambda i: (i, 0)),
                  pl.BlockSpec((1, 3, D), lambda i: (i // per_seq, 0, 0))],
        out_specs=pl.BlockSpec((ts, D), lambda i: (i, 0)),
        out_shape=jax.ShapeDtypeStruct((T, D), BF16),
    )(x2, mod3)


def _proj(h, w_in_all):
    tm = 512

    def body(h_ref, w_ref, o_ref):
        o_ref[...] = _dot(h_ref[...], w_ref[...]).astype(BF16)

    return pl.pallas_call(
        body, name="proj",
        grid=(N_DEV, T // tm),
        in_specs=[pl.BlockSpec((tm, D), lambda j, i: (i, 0)),
                  pl.BlockSpec((None, D, SHARD), lambda j, i: (j, 0, 0))],
        out_specs=pl.BlockSpec((tm, SHARD), lambda j, i: (i, j)),
        out_shape=jax.ShapeDtypeStruct((T, NCOL), BF16),
        compiler_params=pltpu.CompilerParams(vmem_limit_bytes=VMEM_LIMIT),
    )(h, w_in_all)


def _bucket_maps():
    a = np.arange(QB)[:, None]
    b = np.arange(2 * QB)[None, :]
    steps = a + QB - b
    maps = []
    for dil in DILATIONS:
        dist = np.maximum(steps, 0) * dil
        nf = np.maximum(dist, 1).astype(np.float32)
        large = 16 + (np.log(nf / np.float32(16)) / np.float32(math.log(128.0))
                      * np.float32(16)).astype(np.int32)
        large = np.minimum(large, N_BUCKETS - 1)
        maps.append(np.where(dist < 16, dist, large).astype(np.int32))
    band = (steps >= 0) & (steps <= N_STEPS)
    first = band & (b >= QB)
    masks = np.stack([first, band]).astype(np.int32)
    return np.stack(maps), masks


def _bias_expand(rel_bias, buckets, masks):
    def body(tab_ref, bk_ref, mk_ref, o_ref):
        for g in range(3):
            bk = bk_ref[g]
            for h in range(4):
                col = 4 * g + h
                val = jnp.zeros((QB, 2 * QB), F32)
                for k in range(N_BUCKETS):
                    val = jnp.where(bk == k, tab_ref[k, col], val)
                o_ref[g, 0, h] = jnp.where(mk_ref[0] != 0, val, NEG_INF)
                o_ref[g, 1, h] = jnp.where(mk_ref[1] != 0, val, NEG_INF)

    return pl.pallas_call(
        body, name="bias_expand",
        in_specs=[pl.BlockSpec(memory_space=pltpu.SMEM),
                  pl.BlockSpec(memory_space=pltpu.VMEM),
                  pl.BlockSpec(memory_space=pltpu.VMEM)],
        out_shape=jax.ShapeDtypeStruct((3, 2, 4, QB, 2 * QB), F32),
    )(rel_bias, buckets, masks)


def _bias_grad(ds1, ds2, ds3, buckets):
    def body(d1_ref, d2_ref, d3_ref, bk_ref, o_ref):
        for g, d_ref in enumerate((d1_ref, d2_ref, d3_ref)):
            bk = bk_ref[g]
            for h in range(4):
                dv = d_ref[h]
                for k in range(N_BUCKETS):
                    o_ref[k, 4 * g + h] = jnp.sum(jnp.where(bk == k, dv, 0.0))

    return pl.pallas_call(
        body, name="bias_grad",
        in_specs=[pl.BlockSpec(memory_space=pltpu.VMEM)] * 4,
        out_specs=pl.BlockSpec(memory_space=pltpu.SMEM),
        out_shape=jax.ShapeDtypeStruct((N_BUCKETS, N_HEADS), F32),
    )(ds1, ds2, ds3, buckets)


def _attn_fwd(proj, bias, g):
    dil = DILATIONS[g]
    rows = QB * dil
    nsb = S // rows
    has_prev = nsb > 1

    def residue(r):
        return pl.ds(r, QB, stride=dil) if dil > 1 else pl.ds(0, QB)

    def body(*refs):
        if has_prev:
            (q_ref, kc_ref, vc_ref, kp_ref, vp_ref, b_ref, o_ref, l_ref,
             qs, kcs, vcs, ao, ls, kps, vps) = refs
        else:
            q_ref, kc_ref, vc_ref, b_ref, o_ref, l_ref, qs, kcs, vcs, ao, ls = refs
        lane = lax.broadcasted_iota(jnp.int32, (QB, 128), 1)
        for h in range(4):
            sl = slice(h * HD, (h + 1) * HD)
            qs[...] = q_ref[:, sl].astype(F32)
            kcs[...] = kc_ref[:, sl].astype(F32)
            vcs[...] = vc_ref[:, sl].astype(F32)
            if has_prev:
                kps[...] = kp_ref[:, sl].astype(F32)
                vps[...] = vp_ref[:, sl].astype(F32)
            for r in range(dil):
                rr = residue(r)
                q = qs[rr, :].astype(BF16)
                s2 = _dot_nt(q, kcs[rr, :].astype(BF16)) * SCALE + b_ref[h, :, QB:]
                m = jnp.max(s2, axis=1, keepdims=True)
                if has_prev:
                    s1 = _dot_nt(q, kps[rr, :].astype(BF16)) * SCALE + b_ref[h, :, :QB]
                    m = jnp.maximum(m, jnp.max(s1, axis=1, keepdims=True))
                p2 = jnp.exp(s2 - m)
                l = jnp.sum(p2, axis=1, keepdims=True)
                o = _dot(p2.astype(BF16), vcs[rr, :].astype(BF16))
                if has_prev:
                    p1 = jnp.exp(s1 - m)
                    l = l + jnp.sum(p1, axis=1, keepdims=True)
                    o = o + _dot(p1.astype(BF16), vps[rr, :].astype(BF16))
                ao[rr, :] = o / l
                blk = slice(r * QB, (r + 1) * QB)
                lse = m + jnp.log(l)
                ls[blk, :] = jnp.where(lane == h, lse, 0.0 if h == 0 else ls[blk, :])
            o_ref[:, sl] = ao[...]
        for r in range(dil):
            l_ref[residue(r), :] = ls[r * QB:(r + 1) * QB, :]

    def row(b, n):
        return b * nsb + n

    def prev(b, n):
        return b * nsb + jnp.maximum(n - 1, 0)

    in_specs = [
        pl.BlockSpec((rows, GW), lambda b, n: (row(b, n), CB_Q + g)),
        pl.BlockSpec((rows, GW), lambda b, n: (row(b, n), CB_K + g)),
        pl.BlockSpec((rows, GW), lambda b, n: (row(b, n), CB_V + g)),
    ]
    args = [proj, proj, proj]
    scratch = [pltpu.VMEM((rows, HD), F32)] * 4 + [pltpu.VMEM((rows, 128), F32)]
    if has_prev:
        in_specs += [pl.BlockSpec((rows, GW), lambda b, n: (prev(b, n), CB_K + g)),
                     pl.BlockSpec((rows, GW), lambda b, n: (prev(b, n), CB_V + g))]
        args += [proj, proj]
        scratch += [pltpu.VMEM((rows, HD), F32)] * 2
    in_specs.append(pl.BlockSpec((None, None, 4, QB, 2 * QB),
                                 lambda b, n: (g, jnp.minimum(n, 1), 0, 0, 0)))
    args.append(bias)
    return pl.pallas_call(
        body, name=f"attn_fwd{g}",
        grid=(BL, nsb),
        in_specs=in_specs,
        out_specs=(pl.BlockSpec((rows, GW), lambda b, n: (row(b, n), 0)),
                   pl.BlockSpec((rows, 128), lambda b, n: (row(b, n), 0))),
        out_shape=(jax.ShapeDtypeStruct((T, GW), F32), jax.ShapeDtypeStruct((T, 128), F32)),
        scratch_shapes=scratch,
        compiler_params=pltpu.CompilerParams(vmem_limit_bytes=VMEM_LIMIT),
    )(*args)


def _attn_bwd(proj, d_out, stats, bias, dproj, g):
    dil = DILATIONS[g]
    rows = QB * dil
    nsb = S // rows
    has_prev = nsb > 1
    n_steps = nsb + 1 if has_prev else 1
    n_in = 7 + (2 if has_prev else 0)

    def residue(r):
        return pl.ds(r, QB, stride=dil) if dil > 1 else pl.ds(0, QB)

    def body(*refs):
        q_ref, kc_ref, vc_ref, do_ref, st_ref, b_ref = refs[:6]
        if has_prev:
            kp_ref, vp_ref = refs[6:8]
        out_ref, db_ref = refs[n_in], refs[n_in + 1]
        scr = refs[n_in + 2:]
        qs, kcs, vcs, dos, sts, aq, ak, av, sq, sk, sv, sems = scr[:12]
        if has_prev:
            kps, vps, carry = scr[12:]
        b, n = pl.program_id(0), pl.program_id(1)

        @pl.when((b == 0) & (n == 0))
        def _():
            db_ref[...] = jnp.zeros_like(db_ref)

        def flush_head(h):
            sq[:, h * HD:(h + 1) * HD] = aq[...].astype(BF16)
            sk[:, h * HD:(h + 1) * HD] = ak[...].astype(BF16)
            sv[:, h * HD:(h + 1) * HD] = av[...].astype(BF16)

        def write_block(blk_idx):
            row0 = pl.multiple_of(blk_idx * rows, rows)
            _write_columns([(sq, CB * (CB_Q + g)), (sk, CB * (CB_K + g)), (sv, CB * (CB_V + g))],
                           out_ref, row0, sems)

        if has_prev:
            @pl.when(n == nsb)
            def _():
                for h in range(4):
                    for r in range(dil):
                        blk = slice(r * QB, (r + 1) * QB)
                        aq[residue(r), :] = carry[blk, h * HD:(h + 1) * HD]
                        ak[residue(r), :] = carry[blk, GW + h * HD:GW + (h + 1) * HD]
                        av[residue(r), :] = carry[blk, 2 * GW + h * HD:2 * GW + (h + 1) * HD]
                    flush_head(h)
                write_block(b * nsb + nsb - 1)

        @pl.when(n < nsb)
        def _():
            for r in range(dil):
                sts[r * QB:(r + 1) * QB, :] = st_ref[residue(r), :]
            for h in range(4):
                sl = slice(h * HD, (h + 1) * HD)
                qs[...] = q_ref[:, sl].astype(F32)
                kcs[...] = kc_ref[:, sl].astype(F32)
                vcs[...] = vc_ref[:, sl].astype(F32)
                dos[...] = do_ref[:, sl].astype(F32)
                if has_prev:
                    kps[...] = kp_ref[:, sl].astype(F32)
                    vps[...] = vp_ref[:, sl].astype(F32)
                for r in range(dil):
                    rr = residue(r)
                    blk = slice(r * QB, (r + 1) * QB)
                    q, kc, vc = qs[rr, :].astype(BF16), kcs[rr, :].astype(BF16), vcs[rr, :].astype(BF16)
                    do = dos[rr, :].astype(BF16)
                    lse = sts[blk, h:h + 1]
                    delta = sts[blk, 4 + h:5 + h]
                    s2 = _dot_nt(q, kc) * SCALE + b_ref[h, :, QB:]
                    p2 = jnp.exp(s2 - lse)
                    ds2 = p2 * (_dot_nt(do, vc) - delta)
                    db_ref[h, :, QB:] += ds2
                    ds2b, p2b = ds2.astype(BF16), p2.astype(BF16)
                    dq = _dot(ds2b, kc)
                    dk_cur = _dot_tn(ds2b, q) * SCALE
                    dv_cur = _dot_tn(p2b, do)
                    if has_prev:
                        kp, vp = kps[rr, :].astype(BF16), vps[rr, :].astype(BF16)
                        s1 = _dot_nt(q, kp) * SCALE + b_ref[h, :, :QB]
                        p1 = jnp.exp(s1 - lse)
                        ds1 = p1 * (_dot_nt(do, vp) - delta)
                        db_ref[h, :, :QB] += ds1
                        ds1b, p1b = ds1.astype(BF16), p1.astype(BF16)
                        dq = dq + _dot(ds1b, kp)
                        dk_prev = _dot_tn(ds1b, q) * SCALE
                        dv_prev = _dot_tn(p1b, do)
                        cq = slice(h * HD, (h + 1) * HD)
                        ck = slice(GW + h * HD, GW + (h + 1) * HD)
                        cv = slice(2 * GW + h * HD, 2 * GW + (h + 1) * HD)

                        @pl.when(n > 0)
                        def _():
                            aq[rr, :] = carry[blk, cq]
                            ak[rr, :] = carry[blk, ck] + dk_prev
                            av[rr, :] = carry[blk, cv] + dv_prev

                        carry[blk, cq] = dq * SCALE
                        carry[blk, ck] = dk_cur
                        carry[blk, cv] = dv_cur
                    else:
                        aq[rr, :] = dq * SCALE
                        ak[rr, :] = dk_cur
                        av[rr, :] = dv_cur
                if has_prev:
                    @pl.when(n > 0)
                    def _():
                        flush_head(h)
                else:
                    flush_head(h)
            if has_prev:
                @pl.when(n > 0)
                def _():
                    write_block(b * nsb + n - 1)
            else:
                write_block(b)

    def row(b, n):
        return b * nsb + jnp.minimum(n, nsb - 1)

    def prev(b, n):
        return b * nsb + jnp.maximum(jnp.minimum(n, nsb - 1) - 1, 0)

    in_specs = [
        pl.BlockSpec((rows, GW), lambda b, n: (row(b, n), CB_Q + g)),
        pl.BlockSpec((rows, GW), lambda b, n: (row(b, n), CB_K + g)),
        pl.BlockSpec((rows, GW), lambda b, n: (row(b, n), CB_V + g)),
        pl.BlockSpec((rows, GW), lambda b, n: (row(b, n), 0)),
        pl.BlockSpec((rows, 128), lambda b, n: (row(b, n), 0)),
        pl.BlockSpec((None, None, 4, QB, 2 * QB),
                     lambda b, n: (g, jnp.minimum(jnp.minimum(n, nsb - 1), 1), 0, 0, 0)),
    ]
    args = [proj, proj, proj, d_out, stats, bias]
    scratch = [pltpu.VMEM((rows, HD), F32)] * 4 + [pltpu.VMEM((rows, 128), F32)] \
        + [pltpu.VMEM((rows, HD), F32)] * 3 + [pltpu.VMEM((rows, GW), BF16)] * 3 \
        + [pltpu.SemaphoreType.DMA((3,))]
    if has_prev:
        in_specs += [pl.BlockSpec((rows, GW), lambda b, n: (prev(b, n), CB_K + g)),
                     pl.BlockSpec((rows, GW), lambda b, n: (prev(b, n), CB_V + g))]
        args += [proj, proj]
        scratch += [pltpu.VMEM((rows, HD), F32)] * 2 + [pltpu.VMEM((rows, 3 * GW), F32)]
    in_specs.append(pl.BlockSpec(memory_space=pl.ANY))
    args.append(dproj)
    return pl.pallas_call(
        body, name=f"attn_bwd{g}",
        grid=(BL, n_steps),
        in_specs=in_specs,
        out_specs=(pl.BlockSpec(memory_space=pl.ANY),
                   pl.BlockSpec((4, QB, 2 * QB), lambda b, n: (0, 0, 0))),
        out_shape=(jax.ShapeDtypeStruct((T, NCOL), BF16),
                   jax.ShapeDtypeStruct((4, QB, 2 * QB), F32)),
        scratch_shapes=scratch,
        input_output_aliases={len(args) - 1: 0},
        compiler_params=pltpu.CompilerParams(vmem_limit_bytes=VMEM_LIMIT),
    )(*args)


def _tail(x2, tgt2, mod3, o_g, lse_g, proj, w_ao, w_co, w_o, conv_w, conv_b, ln_g, ln_b):
    tm = 256
    per_seq = S // tm
    halo = 16

    def body(x_ref, t_ref, mod_ref, o1_ref, o2_ref, o3_ref, l1_ref, l2_ref, l3_ref,
             ga_ref, u_ref, bg_ref, cg_ref, gc_ref, ma_ref, mc_ref, up_ref, cp_ref,
             wao_ref, wco_ref, wo_ref, cw_ref, cb_ref, lg_ref, lb_ref,
             dproj_ref, dyc_ref, do_ref, st_ref, dxd_ref,
             mg_ref, dy_ref, ain_ref, dao_ref, sin_ref, dso_ref, vec_ref,
             dga_s, dbg_s, dgm_s, sems):
        i = pl.program_id(0)
        bidx = i // per_seq
        first = (i % per_seq) == 0

        @pl.when(i == 0)
        def _():
            vec_ref[...] = jnp.zeros_like(vec_ref)

        l1, l2, l3 = l1_ref[...], l2_ref[...], l3_ref[...]
        mx = jnp.maximum(jnp.maximum(l1, l2), l3)
        e1, e2, e3 = jnp.exp(l1 - mx), jnp.exp(l2 - mx), jnp.exp(l3 - mx)
        esum = e1 + e2 + e3
        lse_tot = mx + jnp.log(esum)
        w1, w2, w3 = e1 / esum, e2 / esum, e3 / esum

        def per_head(wv):
            return jnp.concatenate([jnp.broadcast_to(wv[:, h:h + 1], (tm, HD)) for h in range(4)], axis=1)

        o = per_head(w1) * o1_ref[...] + per_head(w2) * o2_ref[...] + per_head(w3) * o3_ref[...]

        ga = ga_ref[...].astype(F32)
        sig_ga = _sigmoid(ga)
        silu_ga = ga * sig_ga
        a_in = (o * silu_ga).astype(BF16)
        a_out = _dot(a_in, wao_ref[...])

        u = u_ref[...].astype(F32)
        cg = cg_ref[...].astype(F32)
        z = cg * u
        zp = cp_ref[...].astype(F32) * up_ref[...].astype(F32)
        zp = jnp.where(first, 0.0, zp)
        zcat = jnp.concatenate([zp, z], axis=0)
        z1 = pltpu.roll(zcat, 1, 0)[halo:]
        z2 = pltpu.roll(zcat, 2, 0)[halo:]
        y_conv = cw_ref[0:1, :] * z2 + cw_ref[1:2, :] * z1 + cw_ref[2:3, :] * z + cb_ref[...]
        gc = gc_ref[...].astype(F32)
        sig_gc = _sigmoid(gc)
        silu_gc = gc * sig_gc
        bg = bg_ref[...].astype(F32)
        s_in = (bg * y_conv * silu_gc).astype(BF16)
        s_out = _dot(s_in, wco_ref[...])

        sa = _sigmoid(ma_ref[...].astype(F32))
        sc = _sigmoid(mc_ref[...].astype(F32))
        merged = (sa * a_out + sc * s_out).astype(BF16)
        y = _dot(merged, wo_ref[...])
        gate1 = 1.0 + mod_ref[0, 2:3, :]
        xv = x_ref[...]
        resid = ALPHA * xv + gate1 * y
        mu = jnp.mean(resid, axis=1, keepdims=True)
        xc = resid - mu
        var = jnp.mean(xc * xc, axis=1, keepdims=True)
        rstd = lax.rsqrt(var + LN_EPS)
        xhat = xc * rstd
        lg = lg_ref[...]
        err = xhat * lg + lb_ref[...] - t_ref[...]
        vec_ref[3:4, :] += (0.5 / D) * jnp.sum(err * err, axis=0, keepdims=True)

        dout = err * (1.0 / D)
        vec_ref[1:2, :] += jnp.sum(dout * xhat, axis=0, keepdims=True)
        vec_ref[2:3, :] += jnp.sum(dout, axis=0, keepdims=True)
        dxh = dout * lg
        dres = rstd * (dxh - jnp.mean(dxh, axis=1, keepdims=True)
                       - xhat * jnp.mean(dxh * xhat, axis=1, keepdims=True))
        dxd_ref[...] = ALPHA * dres
        dgate = jnp.sum(dres * y, axis=0, keepdims=True)
        vec_ref[4:5, :] += jnp.where(bidx == 0, dgate, 0.0)
        vec_ref[5:6, :] += jnp.where(bidx == 1, dgate, 0.0)
        dy = (dres * gate1).astype(BF16)

        dmerged = _dot_nt(dy, wo_ref[...])
        da_out = (dmerged * sa).astype(BF16)
        ds_out = (dmerged * sc).astype(BF16)
        dgm_s[:, 2 * D:3 * D] =(dmerged * s_out * sc * (1.0 - sc)).astype(BF16)
        dgm_s[:, D:2 * D] =(dmerged * a_out * sa * (1.0 - sa)).astype(BF16)
        da_in = _dot_nt(da_out, wao_ref[...])
        ds_in = _dot_nt(ds_out, wco_ref[...])

        d_o = da_in * silu_ga
        do_ref[...] = d_o.astype(BF16)
        dga_s[...] =(da_in * o * (sig_ga * (1.0 + ga * (1.0 - sig_ga)))).astype(BF16)
        lane = lax.broadcasted_iota(jnp.int32, (tm, 128), 1)
        stats = lse_tot
        od = o * d_o
        for h in range(4):
            delta = jnp.sum(od[:, h * HD:(h + 1) * HD], axis=1, keepdims=True)
            stats = jnp.where(lane == 4 + h, delta, stats)
        st_ref[...] = stats

        dbg_s[...] =(ds_in * y_conv * silu_gc).astype(BF16)
        dyc = ds_in * bg * silu_gc
        dyc_ref[...] = dyc
        vec_ref[0:1, :] += jnp.sum(dyc, axis=0, keepdims=True)
        dgm_s[:, 0:D] =(ds_in * bg * y_conv * (sig_gc * (1.0 + gc * (1.0 - sig_gc)))).astype(BF16)

        mg_ref[...] = merged
        dy_ref[...] = dy
        ain_ref[...] = a_in
        dao_ref[...] = da_out
        sin_ref[...] = s_in
        dso_ref[...] = ds_out
        _write_columns([(dga_s, CB * CB_GA), (dbg_s, D * KB_BG), (dgm_s, D * KB_GC)],
                       dproj_ref, pl.multiple_of(i * tm, tm), sems)

    def tile(width, cblk=0):
        return pl.BlockSpec((tm, width), lambda i: (i, cblk))

    def whole(shape):
        return pl.BlockSpec(shape, lambda i: tuple(0 for _ in shape))

    prev_rows = lambda i: (jnp.maximum(i * (tm // halo) - 1, 0),)
    in_specs = [
        tile(D), tile(D), pl.BlockSpec((1, 3, D), lambda i: (i // per_seq, 0, 0)),
        tile(GW), tile(GW), tile(GW), tile(128), tile(128), tile(128),
        tile(GW, CB_GA), tile(D, KB_U), tile(D, KB_BG), tile(D, KB_CG), tile(D, KB_GC),
        tile(D, KB_MA), tile(D, KB_MC),
        pl.BlockSpec((halo, D), lambda i: (*prev_rows(i), KB_U)),
        pl.BlockSpec((halo, D), lambda i: (*prev_rows(i), KB_CG)),
        whole((GW, D)), whole((D, D)), whole((D, D)),
        whole((3, D)), whole((1, D)), whole((1, D)), whole((1, D)),
    ]
    out_specs = (
        pl.BlockSpec(memory_space=pl.ANY), tile(D), tile(GW), tile(128), tile(D),
        tile(D), tile(D), tile(GW), tile(D), tile(D), tile(D),
        pl.BlockSpec((8, D), lambda i: (0, 0)),
    )
    out_shape = (
        jax.ShapeDtypeStruct((T, NCOL), BF16),
        jax.ShapeDtypeStruct((T, D), F32),
        jax.ShapeDtypeStruct((T, GW), BF16),
        jax.ShapeDtypeStruct((T, 128), F32),
        jax.ShapeDtypeStruct((T, D), F32),
        jax.ShapeDtypeStruct((T, D), BF16),
        jax.ShapeDtypeStruct((T, D), BF16),
        jax.ShapeDtypeStruct((T, GW), BF16),
        jax.ShapeDtypeStruct((T, D), BF16),
        jax.ShapeDtypeStruct((T, D), BF16),
        jax.ShapeDtypeStruct((T, D), BF16),
        jax.ShapeDtypeStruct((8, D), F32),
    )
    return pl.pallas_call(
        body, name="tail",
        grid=(T // tm,),
        in_specs=in_specs, out_specs=out_specs, out_shape=out_shape,
        scratch_shapes=[pltpu.VMEM((tm, GW), BF16), pltpu.VMEM((tm, D), BF16), pltpu.VMEM((tm, 3 * D), BF16),
                        pltpu.SemaphoreType.DMA((3,))],
        compiler_params=pltpu.CompilerParams(vmem_limit_bytes=VMEM_LIMIT),
    )(x2, tgt2, mod3, *o_g, *lse_g, proj, proj, proj, proj, proj, proj, proj, proj, proj,
      w_ao, w_co, w_o, conv_w, conv_b, ln_g, ln_b)


def _conv_bwd(dyc, proj, conv_w, dproj):
    tm = 512
    per_seq = S // tm
    halo = 16

    def body(d_ref, dn_ref, u_ref, c_ref, up_ref, cp_ref, cw_ref, _, dproj_ref, g_ref, du_s, dc_s, sems):
        i = pl.program_id(0)
        first = (i % per_seq) == 0
        last = (i % per_seq) == per_seq - 1

        @pl.when(i == 0)
        def _():
            g_ref[...] = jnp.zeros_like(g_ref)

        d = d_ref[...]
        dn = jnp.where(last, 0.0, dn_ref[...])
        dcat = jnp.concatenate([d, dn], axis=0)
        d1 = pltpu.roll(dcat, tm + 8 - 1, 0)[:tm]
        d2 = pltpu.roll(dcat, tm + 8 - 2, 0)[:tm]
        dz = cw_ref[2:3, :] * d + cw_ref[1:2, :] * d1 + cw_ref[0:1, :] * d2
        u = u_ref[...].astype(F32)
        cg = c_ref[...].astype(F32)
        du_s[...] = (dz * cg).astype(BF16)
        dc_s[...] = (dz * u).astype(BF16)
        _write_columns([(du_s, D * KB_U), (dc_s, D * KB_CG)], dproj_ref, pl.multiple_of(i * tm, tm), sems)

        z = cg * u
        zp = jnp.where(first, 0.0, cp_ref[...].astype(F32) * up_ref[...].astype(F32))
        zcat = jnp.concatenate([zp, z], axis=0)
        z1 = pltpu.roll(zcat, 1, 0)[halo:]
        z2 = pltpu.roll(zcat, 2, 0)[halo:]
        g_ref[0:1, :] += jnp.sum(d * z2, axis=0, keepdims=True)
        g_ref[1:2, :] += jnp.sum(d * z1, axis=0, keepdims=True)
        g_ref[2:3, :] += jnp.sum(d * z, axis=0, keepdims=True)

    n_tiles = T // tm
    prev_rows = lambda i: jnp.maximum(i * (tm // halo) - 1, 0)
    next_rows = lambda i: jnp.minimum((i + 1) * (tm // 8), T // 8 - 1)
    return pl.pallas_call(
        body, name="conv_bwd",
        grid=(n_tiles,),
        in_specs=[pl.BlockSpec((tm, D), lambda i: (i, 0)),
                  pl.BlockSpec((8, D), lambda i: (next_rows(i), 0)),
                  pl.BlockSpec((tm, D), lambda i: (i, KB_U)),
                  pl.BlockSpec((tm, D), lambda i: (i, KB_CG)),
                  pl.BlockSpec((halo, D), lambda i: (prev_rows(i), KB_U)),
                  pl.BlockSpec((halo, D), lambda i: (prev_rows(i), KB_CG)),
                  pl.BlockSpec((3, D), lambda i: (0, 0)),
                  pl.BlockSpec(memory_space=pl.ANY)],
        out_specs=(pl.BlockSpec(memory_space=pl.ANY),
                   pl.BlockSpec((8, D), lambda i: (0, 0))),
        out_shape=(jax.ShapeDtypeStruct((T, NCOL), BF16),
                   jax.ShapeDtypeStruct((8, D), F32)),
        scratch_shapes=[pltpu.VMEM((tm, D), BF16), pltpu.VMEM((tm, D), BF16), pltpu.SemaphoreType.DMA((2,))],
        input_output_aliases={7: 0},
        compiler_params=pltpu.CompilerParams(vmem_limit_bytes=VMEM_LIMIT),
    )(dyc, dyc, proj, proj, proj, proj, conv_w, dproj)


def _dh_dx(dproj, w_in_all, x2, dxd, mod3, chip_sums):
    tm = 512
    per_seq = S // tm
    n = len(chip_sums)

    def body(*refs):
        d_ref, w_ref, x_ref, dxd_ref, mod_ref = refs[:5]
        ins = refs[5:5 + n]
        gx_ref, vec_ref = refs[5 + n:7 + n]
        outs = refs[7 + n:7 + 2 * n]
        acc, send_sems, recv_sems, local_sems = refs[7 + 2 * n:]
        i, jj = pl.program_id(0), pl.program_id(1)

        @pl.when((i == 0) & (jj == 0))
        def _():
            vec_ref[...] = jnp.zeros_like(vec_ref)
            if n:
                sends, _, mine = _chip_copies(ins, outs, send_sems, recv_sems, local_sems)
                for cp in sends + mine:
                    cp.start()

        if n:
            @pl.when((i == T // tm - 1) & (jj == N_DEV - 1))
            def _():
                sends, arrivals, mine = _chip_copies(ins, outs, send_sems, recv_sems, local_sems)
                for cp in arrivals:
                    cp.wait_recv()
                for cp in sends:
                    cp.wait_send()
                for cp in mine:
                    cp.wait()

        @pl.when(jj == 0)
        def _():
            acc[...] = jnp.zeros_like(acc)

        acc[...] += _dot_nt(d_ref[...], w_ref[...])

        @pl.when(jj == N_DEV - 1)
        def _():
            dh = acc[...]
            bidx = i // per_seq
            gx_ref[...] = dxd_ref[...] + dh * (1.0 + mod_ref[0, 1:2, :])
            dshift = jnp.sum(dh, axis=0, keepdims=True)
            dscale = jnp.sum(dh * x_ref[...], axis=0, keepdims=True)
            vec_ref[0:1, :] += jnp.where(bidx == 0, dshift, 0.0)
            vec_ref[1:2, :] += jnp.where(bidx == 1, dshift, 0.0)
            vec_ref[2:3, :] += jnp.where(bidx == 0, dscale, 0.0)
            vec_ref[3:4, :] += jnp.where(bidx == 1, dscale, 0.0)

    any_spec = pl.BlockSpec(memory_space=pl.ANY)
    res = pl.pallas_call(
        body, name="dh_dx",
        grid=(T // tm, N_DEV),
        in_specs=[
            pl.BlockSpec((tm, SHARD), lambda i, jj: (i, jj)),
            pl.BlockSpec((None, D, SHARD), lambda i, jj: (jj, 0, 0)),
            pl.BlockSpec((tm, D), lambda i, jj: (i, 0)),
            pl.BlockSpec((tm, D), lambda i, jj: (i, 0)),
            pl.BlockSpec((1, 3, D), lambda i, jj: (i // per_seq, 0, 0))] + [any_spec] * n,
        out_specs=(pl.BlockSpec((tm, D), lambda i, jj: (i, 0)),
                   pl.BlockSpec((8, D), lambda i, jj: (0, 0))) + (any_spec,) * n,
        out_shape=(jax.ShapeDtypeStruct((T, D), F32), jax.ShapeDtypeStruct((8, D), F32))
                  + tuple(jax.ShapeDtypeStruct(a.shape, a.dtype) for a in chip_sums),
        scratch_shapes=[pltpu.VMEM((tm, D), F32), pltpu.SemaphoreType.DMA((max(3 * n, 1),)),
                        pltpu.SemaphoreType.DMA((max(3 * n, 1),)), pltpu.SemaphoreType.DMA((max(n, 1),))],
        compiler_params=pltpu.CompilerParams(vmem_limit_bytes=VMEM_LIMIT),
    )(dproj, w_in_all, x2, dxd, mod3, *chip_sums)
    return res[0], res[1], res[2:]


def _mm_tn(a, b, tn, blocks_leading, name):
    kk, m = a.shape
    n = b.shape[1]
    tk = 1024

    def body(a_ref, b_ref, o_ref, acc):
        @pl.when(pl.program_id(1) == 0)
        def _():
            acc[...] = jnp.zeros_like(acc)

        acc[...] += _dot_tn(a_ref[...], b_ref[...])

        @pl.when(pl.program_id(1) == kk // tk - 1)
        def _():
            o_ref[...] = acc[...].astype(BF16)

    if blocks_leading:
        out_spec = pl.BlockSpec((None, m, tn), lambda j, k: (j, 0, 0))
        out_shape = jax.ShapeDtypeStruct((n // tn, m, tn), BF16)
    else:
        out_spec = pl.BlockSpec((m, tn), lambda j, k: (0, j))
        out_shape = jax.ShapeDtypeStruct((m, n), BF16)
    return pl.pallas_call(
        body, name=name,
        grid=(n // tn, kk // tk),
        in_specs=[pl.BlockSpec((tk, m), lambda j, k: (k, 0)),
                  pl.BlockSpec((tk, tn), lambda j, k: (k, j))],
        out_specs=out_spec, out_shape=out_shape,
        scratch_shapes=[pltpu.VMEM((m, tn), F32)],
        compiler_params=pltpu.CompilerParams(vmem_limit_bytes=VMEM_LIMIT),
    )(a, b)


def _adamw(parts, w, m, v, name, row_tile=None):
    n_parts, rows, cols = parts.shape
    tr = rows if row_tile is None else row_tile
    c1 = 1.0 - ADAM_B1 ** ADAM_STEP
    c2 = 1.0 - ADAM_B2 ** ADAM_STEP

    def body(p_ref, w_ref, m_ref, v_ref, g_ref, d_ref, nm_ref, nv_ref):
        g = p_ref[0].astype(F32)
        for s in range(1, n_parts):
            g = g + p_ref[s].astype(F32)
        nm = ADAM_B1 * m_ref[...] + (1.0 - ADAM_B1) * g
        nv = ADAM_B2 * v_ref[...] + (1.0 - ADAM_B2) * (g * g)
        m_hat = nm / c1
        v_hat = nv / c2
        g_ref[...] = g
        d_ref[...] = -ADAM_LR * (m_hat / (jnp.sqrt(v_hat) + ADAM_EPS) + ADAM_WD * w_ref[...])
        nm_ref[...] = nm
        nv_ref[...] = nv

    blk = pl.BlockSpec((tr, cols), lambda i: (i, 0))
    shp = jax.ShapeDtypeStruct((rows, cols), F32)
    return pl.pallas_call(
        body, name=name,
        grid=(rows // tr,),
        in_specs=[pl.BlockSpec((n_parts, tr, cols), lambda i: (0, i, 0)), blk, blk, blk],
        out_specs=(blk, blk, blk, blk),
        out_shape=(shp, shp, shp, shp),
        compiler_params=pltpu.CompilerParams(vmem_limit_bytes=VMEM_LIMIT),
    )(parts, w, m, v)


def _loss_sum(rows):
    def body(r_ref, o_ref):
        o_ref[...] = jnp.sum(jnp.sum(r_ref[...], axis=0, keepdims=True), axis=1, keepdims=True)

    return pl.pallas_call(body, name="loss_sum", out_shape=jax.ShapeDtypeStruct((1, 1), F32))(rows)


def _local_step(x2, tgt2, mod3, w_in_all, w_ao, w_co, w_o, conv_w, conv_b, rel_bias, ln_g, ln_b):
    buckets_np, masks_np = _bucket_maps()
    buckets, masks = jnp.asarray(buckets_np), jnp.asarray(masks_np)

    h = _prep_h(x2, mod3)
    proj = _proj(h, w_in_all)
    bias = _bias_expand(rel_bias, buckets, masks)
    fwd = [_attn_fwd(proj, bias, g) for g in range(3)]
    o_g = [f[0] for f in fwd]
    lse_g = [f[1] for f in fwd]

    (dproj, dyc, d_o, stats, dxd, merged, dy, a_in, da_out, s_in, ds_out, tail_vec) = _tail(
        x2, tgt2, mod3, o_g, lse_g, proj, w_ao, w_co, w_o, conv_w, conv_b, ln_g, ln_b)

    dbias = []
    for g in range(3):
        dproj, db = _attn_bwd(proj, d_o, stats, bias, dproj, g)
        dbias.append(db)
    g_rel_bias = _bias_grad(*dbias, buckets)
    dproj, conv_vec = _conv_bwd(dyc, proj, conv_w, dproj)

    gw_in = _mm_tn(h, dproj, SHARD, True, "gw_in")
    gw_o = _mm_tn(merged, dy, 512, False, "gw_o")
    gw_co = _mm_tn(s_in, ds_out, 512, False, "gw_conv_out")
    gw_ao = _mm_tn(a_in, da_out, 128, True, "gw_attn_out")
    return dproj, dxd, gw_in, gw_ao, gw_co, gw_o, conv_vec, g_rel_bias, tail_vec


def kernel(x, c, w_ada, b_ada, w_in, conv_w, conv_b, rel_bias, w_attn_out, w_conv_out, w_o, ln_g, ln_b, loss_target, m_w_ada, m_b_ada, m_w_in, m_conv_w, m_conv_b, m_rel_bias, m_w_attn_out, m_w_conv_out, m_w_o, m_ln_g, m_ln_b, v_w_ada, v_b_ada, v_w_in, v_conv_w, v_conv_b, v_rel_bias, v_w_attn_out, v_w_conv_out, v_w_o, v_ln_g, v_ln_b):
    me = _my_index()
    x2 = x.reshape(T, D)
    tgt2 = loss_target.reshape(T, D)

    w_in_all, w_ao_g, w_co_g, w_o_g, conv_w_g, c_g = _all_gather(
        [w_in[0].astype(BF16), w_attn_out[0].astype(BF16), w_conv_out[0].astype(BF16),
         w_o[0].astype(BF16), conv_w[0], c],
        [jax.ShapeDtypeStruct((N_DEV, D, SHARD), BF16),
         jax.ShapeDtypeStruct((N_DEV, GW, D // N_DEV), BF16),
         jax.ShapeDtypeStruct((N_DEV, D // N_DEV, D), BF16),
         jax.ShapeDtypeStruct((N_DEV, D // N_DEV, D), BF16),
         jax.ShapeDtypeStruct((N_DEV, 3, D // N_DEV), F32),
         jax.ShapeDtypeStruct((N_DEV, BL, D), F32)],
        [_slot_leading] * 6,
        "gather_weights")
    w_ao_full = jnp.transpose(w_ao_g, (1, 0, 2)).reshape(GW, D)
    w_co_full = w_co_g.reshape(D, D)
    w_o_full = w_o_g.reshape(D, D)
    conv_w_full = jnp.transpose(conv_w_g, (1, 0, 2)).reshape(3, D)
    c_all = c_g.reshape(N_DEV * BL, D)

    b_cols = lax.dynamic_slice(b_ada, (0, me * ADA_SHARD), (1, ADA_SHARD))
    mod_cols = _ada_fwd(c_all, w_ada[0], b_cols)
    (mod_g,) = _all_gather([mod_cols], [jax.ShapeDtypeStruct((N_DEV, N_DEV * BL, ADA_SHARD), F32)],
                           [_slot_leading], "gather_mod")
    mod_all = jnp.transpose(mod_g, (1, 0, 2)).reshape(N_DEV * BL, 3 * D)
    mod3 = lax.dynamic_slice(mod_all, (me * BL, 0), (BL, 3 * D)).reshape(BL, 3, D)

    (dproj, dxd, gw_in, gw_ao, gw_co, gw_o, conv_vec, g_rel_bias, tail_vec) = _local_step(
        x2, tgt2, mod3, w_in_all, w_ao_full, w_co_full, w_o_full,
        conv_w_full, conv_b, rel_bias, ln_g, ln_b)

    g_conv_w_blocks = jnp.transpose(conv_vec[0:3].reshape(3, N_DEV, D // N_DEV), (1, 0, 2))
    partials = [gw_in, gw_ao, gw_co.reshape(N_DEV, D // N_DEV, D), gw_o.reshape(N_DEV, D // N_DEV, D),
                g_conv_w_blocks]
    sib = _pair_exchange(
        partials,
        [jax.ShapeDtypeStruct((4, D, SHARD), BF16),
         jax.ShapeDtypeStruct((4, GW, D // N_DEV), BF16),
         jax.ShapeDtypeStruct((4, D // N_DEV, D), BF16),
         jax.ShapeDtypeStruct((4, D // N_DEV, D), BF16),
         jax.ShapeDtypeStruct((4, 3, D // N_DEV), F32)],
        [_slot_leading] * 5,
        "pair_grads")
    tiles = [256, None, None, None, None]
    names = ["w_in", "w_attn_out", "w_conv_out", "w_o", "conv_w"]
    core = lax.axis_index("c").astype(jnp.int32).reshape(1)
    chip_sums = [_pair_add(core, partials[a], sib[a], tiles[a], "pair_add_" + names[a]) for a in range(5)]
    grad_x, mod_vec, (r_in, r_ao, r_co, r_o, r_cw) = _dh_dx(dproj, w_in_all, x2, dxd, mod3, chip_sums)

    small = jnp.concatenate([
        tail_vec[0:4],
        jnp.pad(g_rel_bias.reshape(1, N_BUCKETS * N_HEADS), ((0, 0), (0, D - N_BUCKETS * N_HEADS))),
        jnp.zeros((3, D), F32)], axis=0)
    dmod = jnp.concatenate([mod_vec[0:2], mod_vec[2:4], tail_vec[4:6]], axis=1)
    small_g, dmod_g = _all_gather(
        [small, dmod],
        [jax.ShapeDtypeStruct((N_DEV, 8, D), F32), jax.ShapeDtypeStruct((N_DEV, BL, 3 * D), F32)],
        [_slot_leading] * 2, "gather_small")
    dmod_all = dmod_g.reshape(N_DEV * BL, 3 * D)
    loss = _loss_sum(small_g[:, 3, :]).reshape(())
    g_w_ada = _ada_bwd(jnp.transpose(c_all), lax.dynamic_slice(dmod_all, (0, me * ADA_SHARD),
                                                               (N_DEV * BL, ADA_SHARD)))

    def upd(parts, w, m, v, name, row_tile=None):
        shape = w.shape
        w2, m2, v2 = (t.reshape(parts.shape[1:]) for t in (w, m, v))
        return tuple(t.reshape(shape) for t in _adamw(parts, w2, m2, v2, name, row_tile))

    res = {
        "w_ada": upd(g_w_ada[None], w_ada, m_w_ada, v_w_ada, "adam_w_ada", 256),
        "b_ada": upd(dmod_all[:, None, :], b_ada, m_b_ada, v_b_ada, "adam_b_ada"),
        "w_in": upd(r_in, w_in, m_w_in, v_w_in, "adam_w_in", 128),
        "conv_w": upd(r_cw, conv_w, m_conv_w, v_conv_w, "adam_conv_w"),
        "conv_b": upd(small_g[:, 0:1, :], conv_b, m_conv_b, v_conv_b, "adam_conv_b"),
        "rel_bias": upd(small_g[:, 4, :N_BUCKETS * N_HEADS].reshape(N_DEV, N_BUCKETS, N_HEADS),
                        rel_bias, m_rel_bias, v_rel_bias, "adam_rel_bias"),
        "w_attn_out": upd(r_ao, w_attn_out, m_w_attn_out, v_w_attn_out, "adam_w_attn_out"),
        "w_conv_out": upd(r_co, w_conv_out, m_w_conv_out, v_w_conv_out, "adam_w_conv_out"),
        "w_o": upd(r_o, w_o, m_w_o, v_w_o, "adam_w_o"),
        "ln_g": upd(small_g[:, 1:2, :], ln_g, m_ln_g, v_ln_g, "adam_ln_g"),
        "ln_b": upd(small_g[:, 2:3, :], ln_b, m_ln_b, v_ln_b, "adam_ln_b"),
    }
    order = ["w_ada", "b_ada", "w_in", "conv_w", "conv_b", "rel_bias", "w_attn_out", "w_conv_out",
             "w_o", "ln_g", "ln_b"]
    outs = [loss, grad_x.reshape(BL, S, D)]
    for k in range(4):
        outs += [res[name][k] for name in order]
    return tuple(outs)
```

```python
import functools
import math

import numpy as np
import jax
import jax.numpy as jnp
from jax import lax
from jax.experimental import pallas as pl
from jax.experimental.pallas import tpu as pltpu

F32 = jnp.float32
BF16 = jnp.bfloat16
MESH = pl.DeviceIdType.MESH

N_DEV = 8
D = 1024
S = 2048
BL = 2
T = BL * S
NCOL = 11264
SHARD = NCOL // N_DEV
CB = 512
NCB = NCOL // CB
HD = 128
GW = 512
QB = 128
DILATIONS = (1, 4, 16)
N_STEPS = 128
N_BUCKETS = 32
N_HEADS = 12
ALPHA = 2.0 ** 0.25
LN_EPS = 1e-5
NEG_INF = -1e30
SCALE = HD ** -0.5
ADA_SHARD = 3 * D // N_DEV

CB_Q, CB_K, CB_V, CB_GA = 0, 3, 6, 9
KB_U, KB_BG, KB_CG, KB_GC, KB_MA, KB_MC = 5, 6, 7, 8, 9, 10

ADAM_LR, ADAM_B1, ADAM_B2, ADAM_EPS, ADAM_WD, ADAM_STEP = 0.001, 0.9, 0.999, 1e-08, 0.01, 10

VMEM_LIMIT = 56 * 1024 * 1024


def _dot(a, b):
    return jnp.dot(a, b, preferred_element_type=F32)


def _dot_nt(a, b):
    return lax.dot_general(a, b, (((1,), (1,)), ((), ())), preferred_element_type=F32)


def _dot_tn(a, b):
    return lax.dot_general(a, b, (((0,), (0,)), ((), ())), preferred_element_type=F32)


def _sigmoid(v):
    return 1.0 / (1.0 + jnp.exp(-v))


def _write_columns(pieces, dst_hbm, row0, sems):
    copies = []
    for k, (src, col0) in enumerate(pieces):
        rows, width = src.shape
        copies.append(pltpu.make_async_copy(
            src, dst_hbm.at[pl.ds(row0, rows), pl.ds(col0, width)], sems.at[k]))
    for cp in copies:
        cp.start()
    for cp in copies:
        cp.wait()


def _my_index():
    return 4 * lax.axis_index("x") + 2 * lax.axis_index("y") + lax.axis_index("c")


def _slot_leading(ref, slot):
    return ref.at[slot]


def _all_gather(arrs, out_shapes, slot_fns, name):
    n = len(arrs)

    def body(*refs):
        ins, outs = refs[:n], refs[n:2 * n]
        send_sems, recv_sems, local_sems = refs[2 * n:2 * n + 3]
        stage = refs[2 * n + 3:]
        x, y, c = lax.axis_index("x"), lax.axis_index("y"), lax.axis_index("c")
        me, sibling = (x, y, c), (x, y, 1 - c)
        chips = [(1 - x, y), (x, 1 - y), (1 - x, 1 - y)]

        def blk(a, dev):
            return slot_fns[a](outs[a], 4 * dev[0] + 2 * dev[1] + dev[2])

        def copy(a, k, block, to, src=None):
            dst = blk(a, block)
            return pltpu.make_async_remote_copy(
                src_ref=dst if src is None else src, dst_ref=dst,
                send_sem=send_sems.at[a * 7 + k], recv_sem=recv_sems.at[a * 7 + k],
                device_id=to, device_id_type=MESH)

        first = []
        for a in range(n):
            first.append(copy(a, 0, me, sibling, src=ins[a]))
            first += [copy(a, 1 + j, me, (*chip, c), src=ins[a]) for j, chip in enumerate(chips)]
        for cp in first:
            cp.start()
        loads = [pltpu.make_async_copy(ins[a], stage[a], local_sems.at[a]) for a in range(n)]
        for cp in loads:
            cp.start()
        for cp in loads:
            cp.wait()
        mine = [pltpu.make_async_copy(stage[a], blk(a, me), local_sems.at[a]) for a in range(n)]
        for cp in mine:
            cp.start()
        passed = []
        for j, chip in enumerate(chips):
            for a in range(n):
                copy(a, 1 + j, (*chip, c), me).wait_recv()
                fwd = copy(a, 4 + j, (*chip, c), sibling)
                fwd.start()
                passed.append(fwd)
        for a in range(n):
            copy(a, 0, sibling, me).wait_recv()
        for j, chip in enumerate(chips):
            for a in range(n):
                copy(a, 4 + j, (*chip, 1 - c), me).wait_recv()
        for cp in first + passed:
            cp.wait_send()
        for cp in mine:
            cp.wait()

    any_spec = pl.BlockSpec(memory_space=pl.ANY)
    return pl.pallas_call(
        body, name=name,
        out_shape=tuple(out_shapes),
        in_specs=[any_spec] * n,
        out_specs=tuple([any_spec] * n),
        scratch_shapes=[pltpu.SemaphoreType.DMA((7 * n,)), pltpu.SemaphoreType.DMA((7 * n,)),
                        pltpu.SemaphoreType.DMA((n,))]
                       + [pltpu.VMEM(a.shape, a.dtype) for a in arrs],
    )(*arrs)


def _pair_exchange(arrs, shapes4, src_fns, name):
    n = len(arrs)

    def body(*refs):
        ins, recv = refs[:n], refs[n:2 * n]
        send_sems, recv_sems = refs[2 * n:]
        x, y, c = lax.axis_index("x"), lax.axis_index("y"), lax.axis_index("c")
        sibling = (x, y, 1 - c)
        remote = []
        for a in range(n):
            for q in range(4):
                remote.append(pltpu.make_async_remote_copy(
                    src_ref=src_fns[a](ins[a], 2 * q + 1 - c), dst_ref=recv[a].at[q],
                    send_sem=send_sems.at[a * 4 + q], recv_sem=recv_sems.at[a * 4 + q],
                    device_id=sibling, device_id_type=MESH))
        for cp in remote:
            cp.start()
        for cp in remote:
            cp.wait()

    any_spec = pl.BlockSpec(memory_space=pl.ANY)
    return pl.pallas_call(
        body, name=name,
        out_shape=tuple(shapes4),
        in_specs=[any_spec] * n,
        out_specs=tuple([any_spec] * n),
        scratch_shapes=[pltpu.SemaphoreType.DMA((4 * n,))] * 2,
    )(*arrs)


def _chip_copies(ins, outs, send_sems, recv_sems, local_sems):
    n = len(ins)
    x, y, c = lax.axis_index("x"), lax.axis_index("y"), lax.axis_index("c")
    my_chip = 2 * x + y

    def peer_of(k):
        return ((1 - x) if (k >> 1) & 1 else x, (1 - y) if k & 1 else y, c)

    def copy(a, k, out_chip):
        peer = peer_of(k)
        return pltpu.make_async_remote_copy(
            src_ref=ins[a].at[2 * peer[0] + peer[1]], dst_ref=outs[a].at[out_chip],
            send_sem=send_sems.at[a * 3 + k - 1], recv_sem=recv_sems.at[a * 3 + k - 1],
            device_id=peer, device_id_type=MESH)

    sends = [copy(a, k, my_chip) for k in range(1, 4) for a in range(n)]
    arrivals = []
    for k in range(1, 4):
        peer = peer_of(k)
        arrivals += [copy(a, k, 2 * peer[0] + peer[1]) for a in range(n)]
    mine = [pltpu.make_async_copy(ins[a].at[my_chip], outs[a].at[my_chip], local_sems.at[a])
            for a in range(n)]
    return sends, arrivals, mine


def _pair_add(core, mine, theirs, row_tile, name):
    _, rows, cols = theirs.shape
    tr = rows if row_tile is None else row_tile

    def body(core_ref, a_ref, b_ref, o_ref):
        o_ref[...] = (a_ref[...].astype(F32) + b_ref[...].astype(F32)).astype(o_ref.dtype)

    blk = pl.BlockSpec((None, tr, cols), lambda q, i, core_ref: (q, i, 0))
    return pl.pallas_call(
        body, name=name,
        grid_spec=pltpu.PrefetchScalarGridSpec(
            num_scalar_prefetch=1,
            grid=(4, rows // tr),
            in_specs=[pl.BlockSpec((None, tr, cols), lambda q, i, core_ref: (2 * q + core_ref[0], i, 0)), blk],
            out_specs=blk),
        out_shape=jax.ShapeDtypeStruct(theirs.shape, theirs.dtype),
    )(core, mine, theirs)


def _ada_fwd(c_all, w_ada, b_cols):
    def body(c_ref, w_ref, b_ref, o_ref):
        cv = c_ref[...]
        sc = cv * _sigmoid(cv)
        o_ref[...] = jnp.dot(sc, w_ref[...], preferred_element_type=F32,
                             precision=lax.Precision.HIGHEST) + b_ref[...]

    return pl.pallas_call(
        body, name="ada_fwd",
        out_shape=jax.ShapeDtypeStruct((c_all.shape[0], w_ada.shape[1]), F32),
    )(c_all, w_ada, b_cols)


def _ada_bwd(c_all_t, dmod_cols):
    def body(c_ref, d_ref, o_ref):
        cv = c_ref[...]
        sc = cv * _sigmoid(cv)
        o_ref[...] = jnp.dot(sc, d_ref[...], preferred_element_type=F32,
                             precision=lax.Precision.HIGHEST)

    return pl.pallas_call(
        body, name="ada_bwd",
        out_shape=jax.ShapeDtypeStruct((c_all_t.shape[0], dmod_cols.shape[1]), F32),
    )(c_all_t, dmod_cols)


def _prep_h(x2, mod3):
    ts = 512
    per_seq = S // ts

    def body(x_ref, mod_ref, h_ref):
        shift = mod_ref[0, 0:1, :]
        scale = mod_ref[0, 1:2, :]
        h_ref[...] = (x_ref[...] * (1.0 + scale) + shift).astype(BF16)

    return pl.pallas_call(
        body, name="prep_h",
        grid=(T // ts,),
        in_specs=[pl.BlockSpec((ts, D), lambda i: (i, 0)),
                  pl.BlockSpec((1, 3, D), lambda i: (i // per_seq, 0, 0))],
        out_specs=pl.BlockSpec((ts, D), lambda i: (i, 0)),
        out_shape=jax.ShapeDtypeStruct((T, D), BF16),
    )(x2, mod3)


def _proj(h, w_in_all):
    tm = 512

    def body(h_ref, w_ref, o_ref):
        o_ref[...] = _dot(h_ref[...], w_ref[...]).astype(BF16)

    return pl.pallas_call(
        body, name="proj",
        grid=(N_DEV, T // tm),
        in_specs=[pl.BlockSpec((tm, D), lambda j, i: (i, 0)),
                  pl.BlockSpec((None, D, SHARD), lambda j, i: (j, 0, 0))],
        out_specs=pl.BlockSpec((tm, SHARD), lambda j, i: (i, j)),
        out_shape=jax.ShapeDtypeStruct((T, NCOL), BF16),
        compiler_params=pltpu.CompilerParams(vmem_limit_bytes=VMEM_LIMIT),
    )(h, w_in_all)


def _bucket_maps():
    a = np.arange(QB)[:, None]
    b = np.arange(2 * QB)[None, :]
    steps = a + QB - b
    maps = []
    for dil in DILATIONS:
        dist = np.maximum(steps, 0) * dil
        nf = np.maximum(dist, 1).astype(np.float32)
        large = 16 + (np.log(nf / np.float32(16)) / np.float32(math.log(128.0))
                      * np.float32(16)).astype(np.int32)
        large = np.minimum(large, N_BUCKETS - 1)
        maps.append(np.where(dist < 16, dist, large).astype(np.int32))
    band = (steps >= 0) & (steps <= N_STEPS)
    first = band & (b >= QB)
    masks = np.stack([first, band]).astype(np.int32)
    return np.stack(maps), masks


def _bias_expand(rel_bias, buckets, masks):
    def body(tab_ref, bk_ref, mk_ref, o_ref):
        for g in range(3):
            bk = bk_ref[g]
            for h in range(4):
                col = 4 * g + h
                val = jnp.zeros((QB, 2 * QB), F32)
                for k in range(N_BUCKETS):
                    val = jnp.where(bk == k, tab_ref[k, col], val)
                o_ref[g, 0, h] = jnp.where(mk_ref[0] != 0, val, NEG_INF)
                o_ref[g, 1, h] = jnp.where(mk_ref[1] != 0, val, NEG_INF)

    return pl.pallas_call(
        body, name="bias_expand",
        in_specs=[pl.BlockSpec(memory_space=pltpu.SMEM),
                  pl.BlockSpec(memory_space=pltpu.VMEM),
                  pl.BlockSpec(memory_space=pltpu.VMEM)],
        out_shape=jax.ShapeDtypeStruct((3, 2, 4, QB, 2 * QB), F32),
    )(rel_bias, buckets, masks)


def _bias_grad(ds1, ds2, ds3, buckets):
    def body(d1_ref, d2_ref, d3_ref, bk_ref, o_ref):
        for g, d_ref in enumerate((d1_ref, d2_ref, d3_ref)):
            bk = bk_ref[g]
            for h in range(4):
                dv = d_ref[h]
                for k in range(N_BUCKETS):
                    o_ref[k, 4 * g + h] = jnp.sum(jnp.where(bk == k, dv, 0.0))

    return pl.pallas_call(
        body, name="bias_grad",
        in_specs=[pl.BlockSpec(memory_space=pltpu.VMEM)] * 4,
        out_specs=pl.BlockSpec(memory_space=pltpu.SMEM),
        out_shape=jax.ShapeDtypeStruct((N_BUCKETS, N_HEADS), F32),
    )(ds1, ds2, ds3, buckets)


def _scratch_sets(rows):
    return 4 if rows <= 512 else 1


def _residue_rows(src_ref, copies, h, residue):
    sl = slice(h * HD, (h + 1) * HD)
    if copies is None:
        return lambda r: src_ref[:, sl]
    buf = copies[h % len(copies)]
    buf[...] = src_ref[:, sl].astype(F32)
    return lambda r: buf[residue(r), :].astype(BF16)


def _attn_fwd(proj, bias, g):
    dil = DILATIONS[g]
    rows = QB * dil
    nsb = S // rows
    has_prev = nsb > 1

    def residue(r):
        return pl.ds(r, QB, stride=dil) if dil > 1 else pl.ds(0, QB)

    strided = dil > 1
    n_sets = _scratch_sets(rows)
    n_in = 6 if has_prev else 4

    def body(*refs):
        q_ref, kc_ref, vc_ref = refs[:3]
        kp_ref, vp_ref = refs[3:5] if has_prev else (None, None)
        b_ref, o_ref, l_ref = refs[n_in - 1:n_in + 2]
        scr = list(refs[n_in + 2:])
        ls = [scr.pop(0) for _ in range(4)]
        copies = {name: [scr.pop(0) for _ in range(n_sets)] if strided else None
                  for name in ("q", "kc", "vc", "o") + (("kp", "vp") if has_prev else ())}
        lane = lax.broadcasted_iota(jnp.int32, (QB, 128), 1)
        for h in range(4):
            sl = slice(h * HD, (h + 1) * HD)
            q_of = _residue_rows(q_ref, copies["q"], h, residue)
            kc_of = _residue_rows(kc_ref, copies["kc"], h, residue)
            vc_of = _residue_rows(vc_ref, copies["vc"], h, residue)
            if has_prev:
                kp_of = _residue_rows(kp_ref, copies["kp"], h, residue)
                vp_of = _residue_rows(vp_ref, copies["vp"], h, residue)
            for r in range(dil):
                q = q_of(r)
                s2 = _dot_nt(q, kc_of(r)) * SCALE + b_ref[h, :, QB:]
                m = jnp.max(s2, axis=1, keepdims=True)
                if has_prev:
                    s1 = _dot_nt(q, kp_of(r)) * SCALE + b_ref[h, :, :QB]
                    m = jnp.maximum(m, jnp.max(s1, axis=1, keepdims=True))
                p2 = jnp.exp(s2 - m)
                l = jnp.sum(p2, axis=1, keepdims=True)
                o = _dot(p2.astype(BF16), vc_of(r))
                if has_prev:
                    p1 = jnp.exp(s1 - m)
                    l = l + jnp.sum(p1, axis=1, keepdims=True)
                    o = o + _dot(p1.astype(BF16), vp_of(r))
                if strided:
                    copies["o"][h % n_sets][residue(r), :] = o / l
                else:
                    o_ref[:, sl] = o / l
                ls[h][r * QB:(r + 1) * QB, :] = jnp.where(lane == h, m + jnp.log(l), 0.0)
            if strided:
                o_ref[:, sl] = copies["o"][h % n_sets][...]
        for r in range(dil):
            blk = slice(r * QB, (r + 1) * QB)
            l_ref[residue(r), :] = (ls[0][blk, :] + ls[1][blk, :]) + (ls[2][blk, :] + ls[3][blk, :])

    def row(b, n):
        return b * nsb + n

    def prev(b, n):
        return b * nsb + jnp.maximum(n - 1, 0)

    in_specs = [
        pl.BlockSpec((rows, GW), lambda b, n: (row(b, n), CB_Q + g)),
        pl.BlockSpec((rows, GW), lambda b, n: (row(b, n), CB_K + g)),
        pl.BlockSpec((rows, GW), lambda b, n: (row(b, n), CB_V + g)),
    ]
    args = [proj, proj, proj]
    n_copied = (4 + (2 if has_prev else 0)) * (n_sets if strided else 0)
    scratch = [pltpu.VMEM((rows, 128), F32)] * (4 + n_copied)
    if has_prev:
        in_specs += [pl.BlockSpec((rows, GW), lambda b, n: (prev(b, n), CB_K + g)),
                     pl.BlockSpec((rows, GW), lambda b, n: (prev(b, n), CB_V + g))]
        args += [proj, proj]
    in_specs.append(pl.BlockSpec((None, None, 4, QB, 2 * QB),
                                 lambda b, n: (g, jnp.minimum(n, 1), 0, 0, 0)))
    args.append(bias)
    return pl.pallas_call(
        body, name=f"attn_fwd{g}",
        grid=(BL, nsb),
        in_specs=in_specs,
        out_specs=(pl.BlockSpec((rows, GW), lambda b, n: (row(b, n), 0)),
                   pl.BlockSpec((rows, 128), lambda b, n: (row(b, n), 0))),
        out_shape=(jax.ShapeDtypeStruct((T, GW), F32), jax.ShapeDtypeStruct((T, 128), F32)),
        scratch_shapes=scratch,
        compiler_params=pltpu.CompilerParams(vmem_limit_bytes=VMEM_LIMIT),
    )(*args)


def _attn_bwd(proj, d_out, stats, bias, dproj, g):
    dil = DILATIONS[g]
    rows = QB * dil
    nsb = S // rows
    has_prev = nsb > 1
    n_steps = nsb + 1 if has_prev else 1
    n_in = 7 + (2 if has_prev else 0)

    def residue(r):
        return pl.ds(r, QB, stride=dil) if dil > 1 else pl.ds(0, QB)

    strided = dil > 1
    n_sets = _scratch_sets(rows)

    def body(*refs):
        q_ref, kc_ref, vc_ref, do_ref, st_ref, b_ref = refs[:6]
        kp_ref, vp_ref = refs[6:8] if has_prev else (None, None)
        out_ref, db_ref = refs[n_in], refs[n_in + 1]
        scr = list(refs[n_in + 2:])
        sq, sk, sv, sems = [scr.pop(0) for _ in range(4)]
        carry = scr.pop(0) if has_prev else None
        sts = scr.pop(0) if strided else st_ref
        copies = {name: [scr.pop(0) for _ in range(n_sets)] if strided else None
                  for name in ("q", "kc", "vc", "do", "dq", "dk", "dv") + (("kp", "vp") if has_prev else ())}
        b, n = pl.program_id(0), pl.program_id(1)

        @pl.when((b == 0) & (n == 0))
        def _():
            db_ref[...] = jnp.zeros_like(db_ref)

        def finish(h, r, dq, dk, dv):
            if strided:
                for name, val in (("dq", dq), ("dk", dk), ("dv", dv)):
                    copies[name][h % n_sets][residue(r), :] = val
            else:
                sl = slice(h * HD, (h + 1) * HD)
                sq[:, sl], sk[:, sl], sv[:, sl] = dq.astype(BF16), dk.astype(BF16), dv.astype(BF16)

        def finish_head(h):
            if strided:
                sl = slice(h * HD, (h + 1) * HD)
                sq[:, sl] = copies["dq"][h % n_sets][...].astype(BF16)
                sk[:, sl] = copies["dk"][h % n_sets][...].astype(BF16)
                sv[:, sl] = copies["dv"][h % n_sets][...].astype(BF16)

        def write_block(blk_idx):
            row0 = pl.multiple_of(blk_idx * rows, rows)
            _write_columns([(sq, CB * (CB_Q + g)), (sk, CB * (CB_K + g)), (sv, CB * (CB_V + g))],
                           out_ref, row0, sems)

        def carried(h, r):
            blk = slice(r * QB, (r + 1) * QB)
            return ((blk, slice(h * HD, (h + 1) * HD)), (blk, slice(GW + h * HD, GW + (h + 1) * HD)),
                    (blk, slice(2 * GW + h * HD, 2 * GW + (h + 1) * HD)))

        if has_prev:
            @pl.when(n == 0)
            def _():
                carry[...] = jnp.zeros_like(carry)

            @pl.when(n == nsb)
            def _():
                for h in range(4):
                    for r in range(dil):
                        cq, ck, cv = carried(h, r)
                        finish(h, r, carry[cq], carry[ck], carry[cv])
                    finish_head(h)
                write_block(b * nsb + nsb - 1)

        @pl.when(n < nsb)
        def _():
            if strided:
                for r in range(dil):
                    sts[r * QB:(r + 1) * QB, :] = st_ref[residue(r), :]
            for h in range(4):
                q_of = _residue_rows(q_ref, copies["q"], h, residue)
                kc_of = _residue_rows(kc_ref, copies["kc"], h, residue)
                vc_of = _residue_rows(vc_ref, copies["vc"], h, residue)
                do_of = _residue_rows(do_ref, copies["do"], h, residue)
                if has_prev:
                    kp_of = _residue_rows(kp_ref, copies["kp"], h, residue)
                    vp_of = _residue_rows(vp_ref, copies["vp"], h, residue)
                for r in range(dil):
                    blk = slice(r * QB, (r + 1) * QB)
                    q, kc, vc, do = q_of(r), kc_of(r), vc_of(r), do_of(r)
                    lse = sts[blk, h:h + 1]
                    delta = sts[blk, 4 + h:5 + h]
                    s2 = _dot_nt(q, kc) * SCALE + b_ref[h, :, QB:]
                    p2 = jnp.exp(s2 - lse)
                    ds2 = p2 * (_dot_nt(do, vc) - delta)
                    db_ref[h, :, QB:] += ds2
                    ds2b, p2b = ds2.astype(BF16), p2.astype(BF16)
                    dq = _dot(ds2b, kc)
                    dk_cur = _dot_tn(ds2b, q) * SCALE
                    dv_cur = _dot_tn(p2b, do)
                    if has_prev:
                        kp, vp = kp_of(r), vp_of(r)
                        s1 = _dot_nt(q, kp) * SCALE + b_ref[h, :, :QB]
                        p1 = jnp.exp(s1 - lse)
                        ds1 = p1 * (_dot_nt(do, vp) - delta)
                        db_ref[h, :, :QB] += ds1
                        ds1b, p1b = ds1.astype(BF16), p1.astype(BF16)
                        dq = dq + _dot(ds1b, kp)
                        cq, ck, cv = carried(h, r)
                        finish(h, r, carry[cq], carry[ck] + _dot_tn(ds1b, q) * SCALE,
                               carry[cv] + _dot_tn(p1b, do))
                        carry[cq] = dq * SCALE
                        carry[ck] = dk_cur
                        carry[cv] = dv_cur
                    else:
                        finish(h, r, dq * SCALE, dk_cur, dv_cur)
                finish_head(h)
            if has_prev:
                @pl.when(n > 0)
                def _():
                    write_block(b * nsb + n - 1)
            else:
                write_block(b)

    def row(b, n):
        return b * nsb + jnp.minimum(n, nsb - 1)

    def prev(b, n):
        return b * nsb + jnp.maximum(jnp.minimum(n, nsb - 1) - 1, 0)

    in_specs = [
        pl.BlockSpec((rows, GW), lambda b, n: (row(b, n), CB_Q + g)),
        pl.BlockSpec((rows, GW), lambda b, n: (row(b, n), CB_K + g)),
        pl.BlockSpec((rows, GW), lambda b, n: (row(b, n), CB_V + g)),
        pl.BlockSpec((rows, GW), lambda b, n: (row(b, n), 0)),
        pl.BlockSpec((rows, 128), lambda b, n: (row(b, n), 0)),
        pl.BlockSpec((None, None, 4, QB, 2 * QB),
                     lambda b, n: (g, jnp.minimum(jnp.minimum(n, nsb - 1), 1), 0, 0, 0)),
    ]
    args = [proj, proj, proj, d_out, stats, bias]
    scratch = [pltpu.VMEM((rows, GW), BF16)] * 3 + [pltpu.SemaphoreType.DMA((3,))]
    if has_prev:
        in_specs += [pl.BlockSpec((rows, GW), lambda b, n: (prev(b, n), CB_K + g)),
                     pl.BlockSpec((rows, GW), lambda b, n: (prev(b, n), CB_V + g))]
        args += [proj, proj]
        scratch.append(pltpu.VMEM((rows, 3 * GW), F32))
    if strided:
        n_copied = (7 + (2 if has_prev else 0)) * n_sets
        scratch += [pltpu.VMEM((rows, 128), F32)] * (1 + n_copied)
    in_specs.append(pl.BlockSpec(memory_space=pl.ANY))
    args.append(dproj)
    return pl.pallas_call(
        body, name=f"attn_bwd{g}",
        grid=(BL, n_steps),
        in_specs=in_specs,
        out_specs=(pl.BlockSpec(memory_space=pl.ANY),
                   pl.BlockSpec((4, QB, 2 * QB), lambda b, n: (0, 0, 0))),
        out_shape=(jax.ShapeDtypeStruct((T, NCOL), BF16),
                   jax.ShapeDtypeStruct((4, QB, 2 * QB), F32)),
        scratch_shapes=scratch,
        input_output_aliases={len(args) - 1: 0},
        compiler_params=pltpu.CompilerParams(vmem_limit_bytes=VMEM_LIMIT),
    )(*args)


def _tail(x2, tgt2, mod3, o_g, lse_g, proj, w_ao, w_co, w_o, conv_w, conv_b, ln_g, ln_b):
    tm = 256
    per_seq = S // tm
    halo = 16

    def body(x_ref, t_ref, mod_ref, o1_ref, o2_ref, o3_ref, l1_ref, l2_ref, l3_ref,
             ga_ref, u_ref, bg_ref, cg_ref, gc_ref, ma_ref, mc_ref, up_ref, cp_ref,
             wao_ref, wco_ref, wo_ref, cw_ref, cb_ref, lg_ref, lb_ref,
             dproj_ref, dyc_ref, do_ref, st_ref, dxd_ref,
             mg_ref, dy_ref, ain_ref, dao_ref, sin_ref, dso_ref, vec_ref,
             dga_s, dbg_s, dgm_s, sems):
        i = pl.program_id(0)
        bidx = i // per_seq
        first = (i % per_seq) == 0

        @pl.when(i == 0)
        def _():
            vec_ref[...] = jnp.zeros_like(vec_ref)

        l1, l2, l3 = l1_ref[...], l2_ref[...], l3_ref[...]
        mx = jnp.maximum(jnp.maximum(l1, l2), l3)
        e1, e2, e3 = jnp.exp(l1 - mx), jnp.exp(l2 - mx), jnp.exp(l3 - mx)
        esum = e1 + e2 + e3
        lse_tot = mx + jnp.log(esum)
        w1, w2, w3 = e1 / esum, e2 / esum, e3 / esum

        def per_head(wv):
            return jnp.concatenate([jnp.broadcast_to(wv[:, h:h + 1], (tm, HD)) for h in range(4)], axis=1)

        o = per_head(w1) * o1_ref[...] + per_head(w2) * o2_ref[...] + per_head(w3) * o3_ref[...]

        ga = ga_ref[...].astype(F32)
        sig_ga = _sigmoid(ga)
        silu_ga = ga * sig_ga
        a_in = (o * silu_ga).astype(BF16)
        a_out = _dot(a_in, wao_ref[...])

        u = u_ref[...].astype(F32)
        cg = cg_ref[...].astype(F32)
        z = cg * u
        zp = cp_ref[...].astype(F32) * up_ref[...].astype(F32)
        zp = jnp.where(first, 0.0, zp)
        zcat = jnp.concatenate([zp, z], axis=0)
        z1 = pltpu.roll(zcat, 1, 0)[halo:]
        z2 = pltpu.roll(zcat, 2, 0)[halo:]
        y_conv = cw_ref[0:1, :] * z2 + cw_ref[1:2, :] * z1 + cw_ref[2:3, :] * z + cb_ref[...]
        gc = gc_ref[...].astype(F32)
        sig_gc = _sigmoid(gc)
        silu_gc = gc * sig_gc
        bg = bg_ref[...].astype(F32)
        s_in = (bg * y_conv * silu_gc).astype(BF16)
        s_out = _dot(s_in, wco_ref[...])

        sa = _sigmoid(ma_ref[...].astype(F32))
        sc = _sigmoid(mc_ref[...].astype(F32))
        merged = (sa * a_out + sc * s_out).astype(BF16)
        y = _dot(merged, wo_ref[...])
        gate1 = 1.0 + mod_ref[0, 2:3, :]
        xv = x_ref[...]
        resid = ALPHA * xv + gate1 * y
        mu = jnp.mean(resid, axis=1, keepdims=True)
        xc = resid - mu
        var = jnp.mean(xc * xc, axis=1, keepdims=True)
        rstd = lax.rsqrt(var + LN_EPS)
        xhat = xc * rstd
        lg = lg_ref[...]
        err = xhat * lg + lb_ref[...] - t_ref[...]
        vec_ref[3:4, :] += (0.5 / D) * jnp.sum(err * err, axis=0, keepdims=True)

        dout = err * (1.0 / D)
        vec_ref[1:2, :] += jnp.sum(dout * xhat, axis=0, keepdims=True)
        vec_ref[2:3, :] += jnp.sum(dout, axis=0, keepdims=True)
        dxh = dout * lg
        dres = rstd * (dxh - jnp.mean(dxh, axis=1, keepdims=True)
                       - xhat * jnp.mean(dxh * xhat, axis=1, keepdims=True))
        dxd_ref[...] = ALPHA * dres
        dgate = jnp.sum(dres * y, axis=0, keepdims=True)
        vec_ref[4:5, :] += jnp.where(bidx == 0, dgate, 0.0)
        vec_ref[5:6, :] += jnp.where(bidx == 1, dgate, 0.0)
        dy = (dres * gate1).astype(BF16)

        dmerged = _dot_nt(dy, wo_ref[...])
        da_out = (dmerged * sa).astype(BF16)
        ds_out = (dmerged * sc).astype(BF16)
        dgm_s[:, 2 * D:3 * D] =(dmerged * s_out * sc * (1.0 - sc)).astype(BF16)
        dgm_s[:, D:2 * D] =(dmerged * a_out * sa * (1.0 - sa)).astype(BF16)
        da_in = _dot_nt(da_out, wao_ref[...])
        ds_in = _dot_nt(ds_out, wco_ref[...])

        d_o = da_in * silu_ga
        do_ref[...] = d_o.astype(BF16)
        dga_s[...] =(da_in * o * (sig_ga * (1.0 + ga * (1.0 - sig_ga)))).astype(BF16)
        lane = lax.broadcasted_iota(jnp.int32, (tm, 128), 1)
        stats = lse_tot
        od = o * d_o
        for h in range(4):
            delta = jnp.sum(od[:, h * HD:(h + 1) * HD], axis=1, keepdims=True)
            stats = jnp.where(lane == 4 + h, delta, stats)
        st_ref[...] = stats

        dbg_s[...] =(ds_in * y_conv * silu_gc).astype(BF16)
        dyc = ds_in * bg * silu_gc
        dyc_ref[...] = dyc
        vec_ref[0:1, :] += jnp.sum(dyc, axis=0, keepdims=True)
        dgm_s[:, 0:D] =(ds_in * bg * y_conv * (sig_gc * (1.0 + gc * (1.0 - sig_gc)))).astype(BF16)

        mg_ref[...] = merged
        dy_ref[...] = dy
        ain_ref[...] = a_in
        dao_ref[...] = da_out
        sin_ref[...] = s_in
        dso_ref[...] = ds_out
        _write_columns([(dga_s, CB * CB_GA), (dbg_s, D * KB_BG), (dgm_s, D * KB_GC)],
                       dproj_ref, pl.multiple_of(i * tm, tm), sems)

    def tile(width, cblk=0):
        return pl.BlockSpec((tm, width), lambda i: (i, cblk))

    def whole(shape):
        return pl.BlockSpec(shape, lambda i: tuple(0 for _ in shape))

    prev_rows = lambda i: (jnp.maximum(i * (tm // halo) - 1, 0),)
    in_specs = [
        tile(D), tile(D), pl.BlockSpec((1, 3, D), lambda i: (i // per_seq, 0, 0)),
        tile(GW), tile(GW), tile(GW), tile(128), tile(128), tile(128),
        tile(GW, CB_GA), tile(D, KB_U), tile(D, KB_BG), tile(D, KB_CG), tile(D, KB_GC),
        tile(D, KB_MA), tile(D, KB_MC),
        pl.BlockSpec((halo, D), lambda i: (*prev_rows(i), KB_U)),
        pl.BlockSpec((halo, D), lambda i: (*prev_rows(i), KB_CG)),
        whole((GW, D)), whole((D, D)), whole((D, D)),
        whole((3, D)), whole((1, D)), whole((1, D)), whole((1, D)),
    ]
    out_specs = (
        pl.BlockSpec(memory_space=pl.ANY), tile(D), tile(GW), tile(128), tile(D),
        tile(D), tile(D), tile(GW), tile(D), tile(D), tile(D),
        pl.BlockSpec((8, D), lambda i: (0, 0)),
    )
    out_shape = (
        jax.ShapeDtypeStruct((T, NCOL), BF16),
        jax.ShapeDtypeStruct((T, D), F32),
        jax.ShapeDtypeStruct((T, GW), BF16),
        jax.ShapeDtypeStruct((T, 128), F32),
        jax.ShapeDtypeStruct((T, D), F32),
        jax.ShapeDtypeStruct((T, D), BF16),
        jax.ShapeDtypeStruct((T, D), BF16),
        jax.ShapeDtypeStruct((T, GW), BF16),
        jax.ShapeDtypeStruct((T, D), BF16),
        jax.ShapeDtypeStruct((T, D), BF16),
        jax.ShapeDtypeStruct((T, D), BF16),
        jax.ShapeDtypeStruct((8, D), F32),
    )
    return pl.pallas_call(
        body, name="tail",
        grid=(T // tm,),
        in_specs=in_specs, out_specs=out_specs, out_shape=out_shape,
        scratch_shapes=[pltpu.VMEM((tm, GW), BF16), pltpu.VMEM((tm, D), BF16), pltpu.VMEM((tm, 3 * D), BF16),
                        pltpu.SemaphoreType.DMA((3,))],
        compiler_params=pltpu.CompilerParams(vmem_limit_bytes=VMEM_LIMIT),
    )(x2, tgt2, mod3, *o_g, *lse_g, proj, proj, proj, proj, proj, proj, proj, proj, proj,
      w_ao, w_co, w_o, conv_w, conv_b, ln_g, ln_b)


def _conv_bwd(dyc, proj, conv_w, dproj):
    tm = 512
    per_seq = S // tm
    halo = 16

    def body(d_ref, dn_ref, u_ref, c_ref, up_ref, cp_ref, cw_ref, _, dproj_ref, g_ref, du_s, dc_s, sems):
        i = pl.program_id(0)
        first = (i % per_seq) == 0
        last = (i % per_seq) == per_seq - 1

        @pl.when(i == 0)
        def _():
            g_ref[...] = jnp.zeros_like(g_ref)

        d = d_ref[...]
        dn = jnp.where(last, 0.0, dn_ref[...])
        dcat = jnp.concatenate([d, dn], axis=0)
        d1 = pltpu.roll(dcat, tm + 8 - 1, 0)[:tm]
        d2 = pltpu.roll(dcat, tm + 8 - 2, 0)[:tm]
        dz = cw_ref[2:3, :] * d + cw_ref[1:2, :] * d1 + cw_ref[0:1, :] * d2
        u = u_ref[...].astype(F32)
        cg = c_ref[...].astype(F32)
        du_s[...] = (dz * cg).astype(BF16)
        dc_s[...] = (dz * u).astype(BF16)
        _write_columns([(du_s, D * KB_U), (dc_s, D * KB_CG)], dproj_ref, pl.multiple_of(i * tm, tm), sems)

        z = cg * u
        zp = jnp.where(first, 0.0, cp_ref[...].astype(F32) * up_ref[...].astype(F32))
        zcat = jnp.concatenate([zp, z], axis=0)
        z1 = pltpu.roll(zcat, 1, 0)[halo:]
        z2 = pltpu.roll(zcat, 2, 0)[halo:]
        g_ref[0:1, :] += jnp.sum(d * z2, axis=0, keepdims=True)
        g_ref[1:2, :] += jnp.sum(d * z1, axis=0, keepdims=True)
        g_ref[2:3, :] += jnp.sum(d * z, axis=0, keepdims=True)

    n_tiles = T // tm
    prev_rows = lambda i: jnp.maximum(i * (tm // halo) - 1, 0)
    next_rows = lambda i: jnp.minimum((i + 1) * (tm // 8), T // 8 - 1)
    return pl.pallas_call(
        body, name="conv_bwd",
        grid=(n_tiles,),
        in_specs=[pl.BlockSpec((tm, D), lambda i: (i, 0)),
                  pl.BlockSpec((8, D), lambda i: (next_rows(i), 0)),
                  pl.BlockSpec((tm, D), lambda i: (i, KB_U)),
                  pl.BlockSpec((tm, D), lambda i: (i, KB_CG)),
                  pl.BlockSpec((halo, D), lambda i: (prev_rows(i), KB_U)),
                  pl.BlockSpec((halo, D), lambda i: (prev_rows(i), KB_CG)),
                  pl.BlockSpec((3, D), lambda i: (0, 0)),
                  pl.BlockSpec(memory_space=pl.ANY)],
        out_specs=(pl.BlockSpec(memory_space=pl.ANY),
                   pl.BlockSpec((8, D), lambda i: (0, 0))),
        out_shape=(jax.ShapeDtypeStruct((T, NCOL), BF16),
                   jax.ShapeDtypeStruct((8, D), F32)),
        scratch_shapes=[pltpu.VMEM((tm, D), BF16), pltpu.VMEM((tm, D), BF16), pltpu.SemaphoreType.DMA((2,))],
        input_output_aliases={7: 0},
        compiler_params=pltpu.CompilerParams(vmem_limit_bytes=VMEM_LIMIT),
    )(dyc, dyc, proj, proj, proj, proj, conv_w, dproj)


def _dh_dx(dproj, w_in_all, x2, dxd, mod3, chip_sums):
    tm = 512
    per_seq = S // tm
    n = len(chip_sums)

    def body(*refs):
        d_ref, w_ref, x_ref, dxd_ref, mod_ref = refs[:5]
        ins = refs[5:5 + n]
        gx_ref, vec_ref = refs[5 + n:7 + n]
        outs = refs[7 + n:7 + 2 * n]
        acc, send_sems, recv_sems, local_sems = refs[7 + 2 * n:]
        i, jj = pl.program_id(0), pl.program_id(1)

        @pl.when((i == 0) & (jj == 0))
        def _():
            vec_ref[...] = jnp.zeros_like(vec_ref)
            if n:
                sends, _, mine = _chip_copies(ins, outs, send_sems, recv_sems, local_sems)
                for cp in sends + mine:
                    cp.start()

        if n:
            @pl.when((i == T // tm - 1) & (jj == N_DEV - 1))
            def _():
                sends, arrivals, mine = _chip_copies(ins, outs, send_sems, recv_sems, local_sems)
                for cp in arrivals:
                    cp.wait_recv()
                for cp in sends:
                    cp.wait_send()
                for cp in mine:
                    cp.wait()

        @pl.when(jj == 0)
        def _():
            acc[...] = jnp.zeros_like(acc)

        acc[...] += _dot_nt(d_ref[...], w_ref[...])

        @pl.when(jj == N_DEV - 1)
        def _():
            dh = acc[...]
            bidx = i // per_seq
            gx_ref[...] = dxd_ref[...] + dh * (1.0 + mod_ref[0, 1:2, :])
            dshift = jnp.sum(dh, axis=0, keepdims=True)
            dscale = jnp.sum(dh * x_ref[...], axis=0, keepdims=True)
            vec_ref[0:1, :] += jnp.where(bidx == 0, dshift, 0.0)
            vec_ref[1:2, :] += jnp.where(bidx == 1, dshift, 0.0)
            vec_ref[2:3, :] += jnp.where(bidx == 0, dscale, 0.0)
            vec_ref[3:4, :] += jnp.where(bidx == 1, dscale, 0.0)

    any_spec = pl.BlockSpec(memory_space=pl.ANY)
    res = pl.pallas_call(
        body, name="dh_dx",
        grid=(T // tm, N_DEV),
        in_specs=[
            pl.BlockSpec((tm, SHARD), lambda i, jj: (i, jj)),
            pl.BlockSpec((None, D, SHARD), lambda i, jj: (jj, 0, 0)),
            pl.BlockSpec((tm, D), lambda i, jj: (i, 0)),
            pl.BlockSpec((tm, D), lambda i, jj: (i, 0)),
            pl.BlockSpec((1, 3, D), lambda i, jj: (i // per_seq, 0, 0))] + [any_spec] * n,
        out_specs=(pl.BlockSpec((tm, D), lambda i, jj: (i, 0)),
                   pl.BlockSpec((8, D), lambda i, jj: (0, 0))) + (any_spec,) * n,
        out_shape=(jax.ShapeDtypeStruct((T, D), F32), jax.ShapeDtypeStruct((8, D), F32))
                  + tuple(jax.ShapeDtypeStruct(a.shape, a.dtype) for a in chip_sums),
        scratch_shapes=[pltpu.VMEM((tm, D), F32), pltpu.SemaphoreType.DMA((max(3 * n, 1),)),
                        pltpu.SemaphoreType.DMA((max(3 * n, 1),)), pltpu.SemaphoreType.DMA((max(n, 1),))],
        compiler_params=pltpu.CompilerParams(vmem_limit_bytes=VMEM_LIMIT),
    )(dproj, w_in_all, x2, dxd, mod3, *chip_sums)
    return res[0], res[1], res[2:]


def _mm_tn(a, b, tn, blocks_leading, name):
    kk, m = a.shape
    n = b.shape[1]
    tk = 1024

    def body(a_ref, b_ref, o_ref, acc):
        @pl.when(pl.program_id(1) == 0)
        def _():
            acc[...] = jnp.zeros_like(acc)

        acc[...] += _dot_tn(a_ref[...], b_ref[...])

        @pl.when(pl.program_id(1) == kk // tk - 1)
        def _():
            o_ref[...] = acc[...].astype(BF16)

    if blocks_leading:
        out_spec = pl.BlockSpec((None, m, tn), lambda j, k: (j, 0, 0))
        out_shape = jax.ShapeDtypeStruct((n // tn, m, tn), BF16)
    else:
        out_spec = pl.BlockSpec((m, tn), lambda j, k: (0, j))
        out_shape = jax.ShapeDtypeStruct((m, n), BF16)
    return pl.pallas_call(
        body, name=name,
        grid=(n // tn, kk // tk),
        in_specs=[pl.BlockSpec((tk, m), lambda j, k: (k, 0)),
                  pl.BlockSpec((tk, tn), lambda j, k: (k, j))],
        out_specs=out_spec, out_shape=out_shape,
        scratch_shapes=[pltpu.VMEM((m, tn), F32)],
        compiler_params=pltpu.CompilerParams(vmem_limit_bytes=VMEM_LIMIT),
    )(a, b)


def _adamw(parts, w, m, v, name, row_tile=None):
    n_parts, rows, cols = parts.shape
    tr = rows if row_tile is None else row_tile
    c1 = 1.0 - ADAM_B1 ** ADAM_STEP
    c2 = 1.0 - ADAM_B2 ** ADAM_STEP

    def body(p_ref, w_ref, m_ref, v_ref, g_ref, d_ref, nm_ref, nv_ref):
        g = p_ref[0].astype(F32)
        for s in range(1, n_parts):
            g = g + p_ref[s].astype(F32)
        nm = ADAM_B1 * m_ref[...] + (1.0 - ADAM_B1) * g
        nv = ADAM_B2 * v_ref[...] + (1.0 - ADAM_B2) * (g * g)
        m_hat = nm / c1
        v_hat = nv / c2
        g_ref[...] = g
        d_ref[...] = -ADAM_LR * (m_hat / (jnp.sqrt(v_hat) + ADAM_EPS) + ADAM_WD * w_ref[...])
        nm_ref[...] = nm
        nv_ref[...] = nv

    blk = pl.BlockSpec((tr, cols), lambda i: (i, 0))
    shp = jax.ShapeDtypeStruct((rows, cols), F32)
    return pl.pallas_call(
        body, name=name,
        grid=(rows // tr,),
        in_specs=[pl.BlockSpec((n_parts, tr, cols), lambda i: (0, i, 0)), blk, blk, blk],
        out_specs=(blk, blk, blk, blk),
        out_shape=(shp, shp, shp, shp),
        compiler_params=pltpu.CompilerParams(vmem_limit_bytes=VMEM_LIMIT),
    )(parts, w, m, v)


def _loss_sum(rows):
    def body(r_ref, o_ref):
        o_ref[...] = jnp.sum(jnp.sum(r_ref[...], axis=0, keepdims=True), axis=1, keepdims=True)

    return pl.pallas_call(body, name="loss_sum", out_shape=jax.ShapeDtypeStruct((1, 1), F32))(rows)


def _local_step(x2, tgt2, mod3, w_in_all, w_ao, w_co, w_o, conv_w, conv_b, rel_bias, ln_g, ln_b):
    buckets_np, masks_np = _bucket_maps()
    buckets, masks = jnp.asarray(buckets_np), jnp.asarray(masks_np)

    h = _prep_h(x2, mod3)
    proj = _proj(h, w_in_all)
    bias = _bias_expand(rel_bias, buckets, masks)
    fwd = [_attn_fwd(proj, bias, g) for g in range(3)]
    o_g = [f[0] for f in fwd]
    lse_g = [f[1] for f in fwd]

    (dproj, dyc, d_o, stats, dxd, merged, dy, a_in, da_out, s_in, ds_out, tail_vec) = _tail(
        x2, tgt2, mod3, o_g, lse_g, proj, w_ao, w_co, w_o, conv_w, conv_b, ln_g, ln_b)

    dbias = []
    for g in range(3):
        dproj, db = _attn_bwd(proj, d_o, stats, bias, dproj, g)
        dbias.append(db)
    g_rel_bias = _bias_grad(*dbias, buckets)
    dproj, conv_vec = _conv_bwd(dyc, proj, conv_w, dproj)

    gw_in = _mm_tn(h, dproj, SHARD, True, "gw_in")
    gw_o = _mm_tn(merged, dy, 512, False, "gw_o")
    gw_co = _mm_tn(s_in, ds_out, 512, False, "gw_conv_out")
    gw_ao = _mm_tn(a_in, da_out, 128, True, "gw_attn_out")
    return dproj, dxd, gw_in, gw_ao, gw_co, gw_o, conv_vec, g_rel_bias, tail_vec


def kernel(x, c, w_ada, b_ada, w_in, conv_w, conv_b, rel_bias, w_attn_out, w_conv_out, w_o, ln_g, ln_b, loss_target, m_w_ada, m_b_ada, m_w_in, m_conv_w, m_conv_b, m_rel_bias, m_w_attn_out, m_w_conv_out, m_w_o, m_ln_g, m_ln_b, v_w_ada, v_b_ada, v_w_in, v_conv_w, v_conv_b, v_rel_bias, v_w_attn_out, v_w_conv_out, v_w_o, v_ln_g, v_ln_b):
    me = _my_index()
    x2 = x.reshape(T, D)
    tgt2 = loss_target.reshape(T, D)

    w_in_all, w_ao_g, w_co_g, w_o_g, conv_w_g, c_g = _all_gather(
        [w_in[0].astype(BF16), w_attn_out[0].astype(BF16), w_conv_out[0].astype(BF16),
         w_o[0].astype(BF16), conv_w[0], c],
        [jax.ShapeDtypeStruct((N_DEV, D, SHARD), BF16),
         jax.ShapeDtypeStruct((N_DEV, GW, D // N_DEV), BF16),
         jax.ShapeDtypeStruct((N_DEV, D // N_DEV, D), BF16),
         jax.ShapeDtypeStruct((N_DEV, D // N_DEV, D), BF16),
         jax.ShapeDtypeStruct((N_DEV, 3, D // N_DEV), F32),
         jax.ShapeDtypeStruct((N_DEV, BL, D), F32)],
        [_slot_leading] * 6,
        "gather_weights")
    w_ao_full = jnp.transpose(w_ao_g, (1, 0, 2)).reshape(GW, D)
    w_co_full = w_co_g.reshape(D, D)
    w_o_full = w_o_g.reshape(D, D)
    conv_w_full = jnp.transpose(conv_w_g, (1, 0, 2)).reshape(3, D)
    c_all = c_g.reshape(N_DEV * BL, D)

    b_cols = lax.dynamic_slice(b_ada, (0, me * ADA_SHARD), (1, ADA_SHARD))
    mod_cols = _ada_fwd(c_all, w_ada[0], b_cols)
    (mod_g,) = _all_gather([mod_cols], [jax.ShapeDtypeStruct((N_DEV, N_DEV * BL, ADA_SHARD), F32)],
                           [_slot_leading], "gather_mod")
    mod_all = jnp.transpose(mod_g, (1, 0, 2)).reshape(N_DEV * BL, 3 * D)
    mod3 = lax.dynamic_slice(mod_all, (me * BL, 0), (BL, 3 * D)).reshape(BL, 3, D)

    (dproj, dxd, gw_in, gw_ao, gw_co, gw_o, conv_vec, g_rel_bias, tail_vec) = _local_step(
        x2, tgt2, mod3, w_in_all, w_ao_full, w_co_full, w_o_full,
        conv_w_full, conv_b, rel_bias, ln_g, ln_b)

    g_conv_w_blocks = jnp.transpose(conv_vec[0:3].reshape(3, N_DEV, D // N_DEV), (1, 0, 2))
    partials = [gw_in, gw_ao, gw_co.reshape(N_DEV, D // N_DEV, D), gw_o.reshape(N_DEV, D // N_DEV, D),
                g_conv_w_blocks]
    sib = _pair_exchange(
        partials,
        [jax.ShapeDtypeStruct((4, D, SHARD), BF16),
         jax.ShapeDtypeStruct((4, GW, D // N_DEV), BF16),
         jax.ShapeDtypeStruct((4, D // N_DEV, D), BF16),
         jax.ShapeDtypeStruct((4, D // N_DEV, D), BF16),
         jax.ShapeDtypeStruct((4, 3, D // N_DEV), F32)],
        [_slot_leading] * 5,
        "pair_grads")
    tiles = [256, None, None, None, None]
    names = ["w_in", "w_attn_out", "w_conv_out", "w_o", "conv_w"]
    core = lax.axis_index("c").astype(jnp.int32).reshape(1)
    chip_sums = [_pair_add(core, partials[a], sib[a], tiles[a], "pair_add_" + names[a]) for a in range(5)]
    grad_x, mod_vec, (r_in, r_ao, r_co, r_o, r_cw) = _dh_dx(dproj, w_in_all, x2, dxd, mod3, chip_sums)

    small = jnp.concatenate([
        tail_vec[0:4],
        jnp.pad(g_rel_bias.reshape(1, N_BUCKETS * N_HEADS), ((0, 0), (0, D - N_BUCKETS * N_HEADS))),
        jnp.zeros((3, D), F32)], axis=0)
    dmod = jnp.concatenate([mod_vec[0:2], mod_vec[2:4], tail_vec[4:6]], axis=1)
    small_g, dmod_g = _all_gather(
        [small, dmod],
        [jax.ShapeDtypeStruct((N_DEV, 8, D), F32), jax.ShapeDtypeStruct((N_DEV, BL, 3 * D), F32)],
        [_slot_leading] * 2, "gather_small")
    dmod_all = dmod_g.reshape(N_DEV * BL, 3 * D)
    loss = _loss_sum(small_g[:, 3, :]).reshape(())
    g_w_ada = _ada_bwd(jnp.transpose(c_all), lax.dynamic_slice(dmod_all, (0, me * ADA_SHARD),
                                                               (N_DEV * BL, ADA_SHARD)))

    def upd(parts, w, m, v, name, row_tile=None):
        shape = w.shape
        w2, m2, v2 = (t.reshape(parts.shape[1:]) for t in (w, m, v))
        return tuple(t.reshape(shape) for t in _adamw(parts, w2, m2, v2, name, row_tile))

    res = {
        "w_ada": upd(g_w_ada[None], w_ada, m_w_ada, v_w_ada, "adam_w_ada", 256),
        "b_ada": upd(dmod_all[:, None, :], b_ada, m_b_ada, v_b_ada, "adam_b_ada"),
        "w_in": upd(r_in, w_in, m_w_in, v_w_in, "adam_w_in", 128),
        "conv_w": upd(r_cw, conv_w, m_conv_w, v_conv_w, "adam_conv_w"),
        "conv_b": upd(small_g[:, 0:1, :], conv_b, m_conv_b, v_conv_b, "adam_conv_b"),
        "rel_bias": upd(small_g[:, 4, :N_BUCKETS * N_HEADS].reshape(N_DEV, N_BUCKETS, N_HEADS),
                        rel_bias, m_rel_bias, v_rel_bias, "adam_rel_bias"),
        "w_attn_out": upd(r_ao, w_attn_out, m_w_attn_out, v_w_attn_out, "adam_w_attn_out"),
        "w_conv_out": upd(r_co, w_conv_out, m_w_conv_out, v_w_conv_out, "adam_w_conv_out"),
        "w_o": upd(r_o, w_o, m_w_o, v_w_o, "adam_w_o"),
        "ln_g": upd(small_g[:, 1:2, :], ln_g, m_ln_g, v_ln_g, "adam_ln_g"),
        "ln_b": upd(small_g[:, 2:3, :], ln_b, m_ln_b, v_ln_b, "adam_ln_b"),
    }
    order = ["w_ada", "b_ada", "w_in", "conv_w", "conv_b", "rel_bias", "w_attn_out", "w_conv_out",
             "w_o", "ln_g", "ln_b"]
    outs = [loss, grad_x.reshape(BL, S, D)]
    for k in range(4):
        outs += [res[name][k] for name in order]
    return tuple(outs)
```

```python
import functools
import math

import numpy as np
import jax
import jax.numpy as jnp
from jax import lax
from jax.experimental import pallas as pl
from jax.experimental.pallas import tpu as pltpu

F32 = jnp.float32
BF16 = jnp.bfloat16
MESH = pl.DeviceIdType.MESH

N_DEV = 8
D = 1024
S = 2048
BL = 2
T = BL * S
NCOL = 11264
SHARD = NCOL // N_DEV
CB = 512
NCB = NCOL // CB
HD = 128
GW = 512
QB = 128
DILATIONS = (1, 4, 16)
N_STEPS = 128
N_BUCKETS = 32
N_HEADS = 12
ALPHA = 2.0 ** 0.25
LN_EPS = 1e-5
NEG_INF = -1e30
SCALE = HD ** -0.5
ADA_SHARD = 3 * D // N_DEV

CB_Q, CB_K, CB_V, CB_GA = 0, 3, 6, 9
KB_U, KB_BG, KB_CG, KB_GC, KB_MA, KB_MC = 5, 6, 7, 8, 9, 10

ADAM_LR, ADAM_B1, ADAM_B2, ADAM_EPS, ADAM_WD, ADAM_STEP = 0.001, 0.9, 0.999, 1e-08, 0.01, 10

VMEM_LIMIT = 56 * 1024 * 1024


def _dot(a, b):
    return jnp.dot(a, b, preferred_element_type=F32)


def _dot_nt(a, b):
    return lax.dot_general(a, b, (((1,), (1,)), ((), ())), preferred_element_type=F32)


def _dot_tn(a, b):
    return lax.dot_general(a, b, (((0,), (0,)), ((), ())), preferred_element_type=F32)


def _sigmoid(v):
    return 1.0 / (1.0 + jnp.exp(-v))


def _write_columns(pieces, dst_hbm, row0, sems):
    copies = []
    for k, (src, col0) in enumerate(pieces):
        rows, width = src.shape
        copies.append(pltpu.make_async_copy(
            src, dst_hbm.at[pl.ds(row0, rows), pl.ds(col0, width)], sems.at[k]))
    for cp in copies:
        cp.start()
    for cp in copies:
        cp.wait()


def _my_index():
    return 4 * lax.axis_index("x") + 2 * lax.axis_index("y") + lax.axis_index("c")


def _slot_leading(ref, slot):
    return ref.at[slot]


def _all_gather(arrs, out_shapes, slot_fns, name):
    n = len(arrs)

    def body(*refs):
        ins, outs = refs[:n], refs[n:2 * n]
        send_sems, recv_sems, local_sems = refs[2 * n:2 * n + 3]
        stage = refs[2 * n + 3:]
        x, y, c = lax.axis_index("x"), lax.axis_index("y"), lax.axis_index("c")
        me, sibling = (x, y, c), (x, y, 1 - c)
        chips = [(1 - x, y), (x, 1 - y), (1 - x, 1 - y)]

        def blk(a, dev):
            return slot_fns[a](outs[a], 4 * dev[0] + 2 * dev[1] + dev[2])

        def copy(a, k, block, to, src=None):
            dst = blk(a, block)
            return pltpu.make_async_remote_copy(
                src_ref=dst if src is None else src, dst_ref=dst,
                send_sem=send_sems.at[a * 7 + k], recv_sem=recv_sems.at[a * 7 + k],
                device_id=to, device_id_type=MESH)

        first = []
        for a in range(n):
            first.append(copy(a, 0, me, sibling, src=ins[a]))
            first += [copy(a, 1 + j, me, (*chip, c), src=ins[a]) for j, chip in enumerate(chips)]
        for cp in first:
            cp.start()
        loads = [pltpu.make_async_copy(ins[a], stage[a], local_sems.at[a]) for a in range(n)]
        for cp in loads:
            cp.start()
        for cp in loads:
            cp.wait()
        mine = [pltpu.make_async_copy(stage[a], blk(a, me), local_sems.at[a]) for a in range(n)]
        for cp in mine:
            cp.start()
        passed = []
        for j, chip in enumerate(chips):
            for a in range(n):
                copy(a, 1 + j, (*chip, c), me).wait_recv()
                fwd = copy(a, 4 + j, (*chip, c), sibling)
                fwd.start()
                passed.append(fwd)
        for a in range(n):
            copy(a, 0, sibling, me).wait_recv()
        for j, chip in enumerate(chips):
            for a in range(n):
                copy(a, 4 + j, (*chip, 1 - c), me).wait_recv()
        for cp in first + passed:
            cp.wait_send()
        for cp in mine:
            cp.wait()

    any_spec = pl.BlockSpec(memory_space=pl.ANY)
    return pl.pallas_call(
        body, name=name,
        out_shape=tuple(out_shapes),
        in_specs=[any_spec] * n,
        out_specs=tuple([any_spec] * n),
        scratch_shapes=[pltpu.SemaphoreType.DMA((7 * n,)), pltpu.SemaphoreType.DMA((7 * n,)),
                        pltpu.SemaphoreType.DMA((n,))]
                       + [pltpu.VMEM(a.shape, a.dtype) for a in arrs],
    )(*arrs)


def _pair_exchange(arrs, shapes4, src_fns, name):
    n = len(arrs)

    def body(*refs):
        ins, recv = refs[:n], refs[n:2 * n]
        send_sems, recv_sems = refs[2 * n:]
        x, y, c = lax.axis_index("x"), lax.axis_index("y"), lax.axis_index("c")
        sibling = (x, y, 1 - c)
        remote = []
        for a in range(n):
            for q in range(4):
                remote.append(pltpu.make_async_remote_copy(
                    src_ref=src_fns[a](ins[a], 2 * q + 1 - c), dst_ref=recv[a].at[q],
                    send_sem=send_sems.at[a * 4 + q], recv_sem=recv_sems.at[a * 4 + q],
                    device_id=sibling, device_id_type=MESH))
        for cp in remote:
            cp.start()
        for cp in remote:
            cp.wait()

    any_spec = pl.BlockSpec(memory_space=pl.ANY)
    return pl.pallas_call(
        body, name=name,
        out_shape=tuple(shapes4),
        in_specs=[any_spec] * n,
        out_specs=tuple([any_spec] * n),
        scratch_shapes=[pltpu.SemaphoreType.DMA((4 * n,))] * 2,
    )(*arrs)


def _chip_copies(ins, outs, send_sems, recv_sems, local_sems):
    n = len(ins)
    x, y, c = lax.axis_index("x"), lax.axis_index("y"), lax.axis_index("c")
    my_chip = 2 * x + y

    def peer_of(k):
        return ((1 - x) if (k >> 1) & 1 else x, (1 - y) if k & 1 else y, c)

    def copy(a, k, out_chip):
        peer = peer_of(k)
        return pltpu.make_async_remote_copy(
            src_ref=ins[a].at[2 * peer[0] + peer[1]], dst_ref=outs[a].at[out_chip],
            send_sem=send_sems.at[a * 3 + k - 1], recv_sem=recv_sems.at[a * 3 + k - 1],
            device_id=peer, device_id_type=MESH)

    sends = [copy(a, k, my_chip) for k in range(1, 4) for a in range(n)]
    arrivals = []
    for k in range(1, 4):
        peer = peer_of(k)
        arrivals += [copy(a, k, 2 * peer[0] + peer[1]) for a in range(n)]
    mine = [pltpu.make_async_copy(ins[a].at[my_chip], outs[a].at[my_chip], local_sems.at[a])
            for a in range(n)]
    return sends, arrivals, mine


def _pair_add(core, mine, theirs, row_tile, name):
    _, rows, cols = theirs.shape
    tr = rows if row_tile is None else row_tile

    def body(core_ref, a_ref, b_ref, o_ref):
        o_ref[...] = (a_ref[...].astype(F32) + b_ref[...].astype(F32)).astype(o_ref.dtype)

    blk = pl.BlockSpec((None, tr, cols), lambda q, i, core_ref: (q, i, 0))
    return pl.pallas_call(
        body, name=name,
        grid_spec=pltpu.PrefetchScalarGridSpec(
            num_scalar_prefetch=1,
            grid=(4, rows // tr),
            in_specs=[pl.BlockSpec((None, tr, cols), lambda q, i, core_ref: (2 * q + core_ref[0], i, 0)), blk],
            out_specs=blk),
        out_shape=jax.ShapeDtypeStruct(theirs.shape, theirs.dtype),
    )(core, mine, theirs)


def _ada_fwd(c_all, w_ada, b_cols):
    def body(c_ref, w_ref, b_ref, o_ref):
        cv = c_ref[...]
        sc = cv * _sigmoid(cv)
        o_ref[...] = jnp.dot(sc, w_ref[...], preferred_element_type=F32,
                             precision=lax.Precision.HIGHEST) + b_ref[...]

    return pl.pallas_call(
        body, name="ada_fwd",
        out_shape=jax.ShapeDtypeStruct((c_all.shape[0], w_ada.shape[1]), F32),
    )(c_all, w_ada, b_cols)


def _ada_bwd(c_all_t, dmod_cols):
    def body(c_ref, d_ref, o_ref):
        cv = c_ref[...]
        sc = cv * _sigmoid(cv)
        o_ref[...] = jnp.dot(sc, d_ref[...], preferred_element_type=F32,
                             precision=lax.Precision.HIGHEST)

    return pl.pallas_call(
        body, name="ada_bwd",
        out_shape=jax.ShapeDtypeStruct((c_all_t.shape[0], dmod_cols.shape[1]), F32),
    )(c_all_t, dmod_cols)


def _prep_h(x2, mod3):
    ts = 512
    per_seq = S // ts

    def body(x_ref, mod_ref, h_ref):
        shift = mod_ref[0, 0:1, :]
        scale = mod_ref[0, 1:2, :]
        h_ref[...] = (x_ref[...] * (1.0 + scale) + shift).astype(BF16)

    return pl.pallas_call(
        body, name="prep_h",
        grid=(T // ts,),
        in_specs=[pl.BlockSpec((ts, D), lambda i: (i, 0)),
                  pl.BlockSpec((1, 3, D), lambda i: (i // per_seq, 0, 0))],
        out_specs=pl.BlockSpec((ts, D), lambda i: (i, 0)),
        out_shape=jax.ShapeDtypeStruct((T, D), BF16),
    )(x2, mod3)


def _proj(h, w_in_all):
    tm = 512

    def body(h_ref, w_ref, o_ref):
        o_ref[...] = _dot(h_ref[...], w_ref[...]).astype(BF16)

    return pl.pallas_call(
        body, name="proj",
        grid=(N_DEV, T // tm),
        in_specs=[pl.BlockSpec((tm, D), lambda j, i: (i, 0)),
                  pl.BlockSpec((None, D, SHARD), lambda j, i: (j, 0, 0))],
        out_specs=pl.BlockSpec((tm, SHARD), lambda j, i: (i, j)),
        out_shape=jax.ShapeDtypeStruct((T, NCOL), BF16),
        compiler_params=pltpu.CompilerParams(vmem_limit_bytes=VMEM_LIMIT),
    )(h, w_in_all)


def _bucket_maps():
    a = np.arange(QB)[:, None]
    b = np.arange(2 * QB)[None, :]
    steps = a + QB - b
    maps = []
    for dil in DILATIONS:
        dist = np.maximum(steps, 0) * dil
        nf = np.maximum(dist, 1).astype(np.float32)
        large = 16 + (np.log(nf / np.float32(16)) / np.float32(math.log(128.0))
                      * np.float32(16)).astype(np.int32)
        large = np.minimum(large, N_BUCKETS - 1)
        maps.append(np.where(dist < 16, dist, large).astype(np.int32))
    band = (steps >= 0) & (steps <= N_STEPS)
    first = band & (b >= QB)
    masks = np.stack([first, band]).astype(np.int32)
    return np.stack(maps), masks


def _bias_expand(rel_bias, buckets, masks):
    def body(tab_ref, bk_ref, mk_ref, o_ref):
        for g in range(3):
            bk = bk_ref[g]
            for h in range(4):
                col = 4 * g + h
                val = jnp.zeros((QB, 2 * QB), F32)
                for k in range(N_BUCKETS):
                    val = jnp.where(bk == k, tab_ref[k, col], val)
                o_ref[g, 0, h] = jnp.where(mk_ref[0] != 0, val, NEG_INF)
                o_ref[g, 1, h] = jnp.where(mk_ref[1] != 0, val, NEG_INF)

    return pl.pallas_call(
        body, name="bias_expand",
        in_specs=[pl.BlockSpec(memory_space=pltpu.SMEM),
                  pl.BlockSpec(memory_space=pltpu.VMEM),
                  pl.BlockSpec(memory_space=pltpu.VMEM)],
        out_shape=jax.ShapeDtypeStruct((3, 2, 4, QB, 2 * QB), F32),
    )(rel_bias, buckets, masks)


def _bias_grad(ds1, ds2, ds3, buckets):
    def body(d1_ref, d2_ref, d3_ref, bk_ref, o_ref):
        for g, d_ref in enumerate((d1_ref, d2_ref, d3_ref)):
            bk = bk_ref[g]
            for h in range(4):
                dv = d_ref[h]
                for k in range(N_BUCKETS):
                    o_ref[k, 4 * g + h] = jnp.sum(jnp.where(bk == k, dv, 0.0))

    return pl.pallas_call(
        body, name="bias_grad",
        in_specs=[pl.BlockSpec(memory_space=pltpu.VMEM)] * 4,
        out_specs=pl.BlockSpec(memory_space=pltpu.SMEM),
        out_shape=jax.ShapeDtypeStruct((N_BUCKETS, N_HEADS), F32),
    )(ds1, ds2, ds3, buckets)


def _scratch_sets(rows):
    return 4 if rows <= 512 else 1


def _unit_chunks(dil, size=16):
    units = [(h, r) for h in range(4) for r in range(dil)]
    return [units[i:i + size] for i in range(0, len(units), size)]


def _residue_rows(src_ref, copies, h, residue):
    sl = slice(h * HD, (h + 1) * HD)
    if copies is None:
        return lambda r: src_ref[:, sl]
    buf = copies[h % len(copies)]
    buf[...] = src_ref[:, sl].astype(F32)
    return lambda r: buf[residue(r), :].astype(BF16)


def _attn_fwd(proj, bias, g):
    dil = DILATIONS[g]
    rows = QB * dil
    nsb = S // rows
    has_prev = nsb > 1

    def residue(r):
        return pl.ds(r, QB, stride=dil) if dil > 1 else pl.ds(0, QB)

    strided = dil > 1
    n_sets = _scratch_sets(rows)
    n_in = 6 if has_prev else 4

    def body(*refs):
        q_ref, kc_ref, vc_ref = refs[:3]
        kp_ref, vp_ref = refs[3:5] if has_prev else (None, None)
        b_ref, o_ref, l_ref = refs[n_in - 1:n_in + 2]
        scr = list(refs[n_in + 2:])
        ls = [scr.pop(0) for _ in range(4)]
        copies = {name: [scr.pop(0) for _ in range(n_sets)] if strided else None
                  for name in ("q", "kc", "vc", "o") + (("kp", "vp") if has_prev else ())}
        lane = lax.broadcasted_iota(jnp.int32, (QB, 128), 1)
        refs_of = {"q": q_ref, "kc": kc_ref, "vc": vc_ref, "kp": kp_ref, "vp": vp_ref}
        for chunk in _unit_chunks(dil):
            rows_of = {h: {name: _residue_rows(refs_of[name], copies[name], h, residue)
                           for name in refs_of if refs_of[name] is not None}
                       for h in sorted({h for h, _ in chunk})}

            def batch(name):
                return jnp.stack([rows_of[h][name](r) for h, r in chunk])

            q, k, v = batch("q"), batch("kc"), batch("vc")
            if has_prev:
                k = jnp.concatenate([batch("kp"), k], axis=1)
                v = jnp.concatenate([batch("vp"), v], axis=1)
                bias_b = jnp.stack([b_ref[h] for h, _ in chunk])
            else:
                bias_b = jnp.stack([b_ref[h, :, QB:] for h, _ in chunk])
            s = jnp.einsum("uqd,ukd->uqk", q, k, preferred_element_type=F32) * SCALE + bias_b
            m = jnp.max(s, axis=-1, keepdims=True)
            p = jnp.exp(s - m)
            l = jnp.sum(p, axis=-1, keepdims=True)
            o = jnp.einsum("uqk,ukd->uqd", p.astype(BF16), v, preferred_element_type=F32) / l
            lse = m + jnp.log(l)
            for i, (h, r) in enumerate(chunk):
                if strided:
                    copies["o"][h % n_sets][residue(r), :] = o[i]
                else:
                    o_ref[:, h * HD:(h + 1) * HD] = o[i]
                ls[h][r * QB:(r + 1) * QB, :] = jnp.where(lane == h, lse[i], 0.0)
            if strided:
                for h in sorted({h for h, _ in chunk}):
                    o_ref[:, h * HD:(h + 1) * HD] = copies["o"][h % n_sets][...]
        for r in range(dil):
            blk = slice(r * QB, (r + 1) * QB)
            l_ref[residue(r), :] = (ls[0][blk, :] + ls[1][blk, :]) + (ls[2][blk, :] + ls[3][blk, :])

    def row(b, n):
        return b * nsb + n

    def prev(b, n):
        return b * nsb + jnp.maximum(n - 1, 0)

    in_specs = [
        pl.BlockSpec((rows, GW), lambda b, n: (row(b, n), CB_Q + g)),
        pl.BlockSpec((rows, GW), lambda b, n: (row(b, n), CB_K + g)),
        pl.BlockSpec((rows, GW), lambda b, n: (row(b, n), CB_V + g)),
    ]
    args = [proj, proj, proj]
    n_copied = (4 + (2 if has_prev else 0)) * (n_sets if strided else 0)
    scratch = [pltpu.VMEM((rows, 128), F32)] * (4 + n_copied)
    if has_prev:
        in_specs += [pl.BlockSpec((rows, GW), lambda b, n: (prev(b, n), CB_K + g)),
                     pl.BlockSpec((rows, GW), lambda b, n: (prev(b, n), CB_V + g))]
        args += [proj, proj]
    in_specs.append(pl.BlockSpec((None, None, 4, QB, 2 * QB),
                                 lambda b, n: (g, jnp.minimum(n, 1), 0, 0, 0)))
    args.append(bias)
    return pl.pallas_call(
        body, name=f"attn_fwd{g}",
        grid=(BL, nsb),
        in_specs=in_specs,
        out_specs=(pl.BlockSpec((rows, GW), lambda b, n: (row(b, n), 0)),
                   pl.BlockSpec((rows, 128), lambda b, n: (row(b, n), 0))),
        out_shape=(jax.ShapeDtypeStruct((T, GW), F32), jax.ShapeDtypeStruct((T, 128), F32)),
        scratch_shapes=scratch,
        compiler_params=pltpu.CompilerParams(vmem_limit_bytes=VMEM_LIMIT),
    )(*args)


def _attn_bwd(proj, d_out, stats, bias, dproj, g):
    dil = DILATIONS[g]
    rows = QB * dil
    nsb = S // rows
    has_prev = nsb > 1
    n_steps = nsb + 1 if has_prev else 1
    n_in = 7 + (2 if has_prev else 0)

    def residue(r):
        return pl.ds(r, QB, stride=dil) if dil > 1 else pl.ds(0, QB)

    strided = dil > 1
    n_sets = _scratch_sets(rows)

    def body(*refs):
        q_ref, kc_ref, vc_ref, do_ref, st_ref, b_ref = refs[:6]
        kp_ref, vp_ref = refs[6:8] if has_prev else (None, None)
        out_ref, db_ref = refs[n_in], refs[n_in + 1]
        scr = list(refs[n_in + 2:])
        sq, sk, sv, sems = [scr.pop(0) for _ in range(4)]
        carry = scr.pop(0) if has_prev else None
        sts = scr.pop(0) if strided else st_ref
        copies = {name: [scr.pop(0) for _ in range(n_sets)] if strided else None
                  for name in ("q", "kc", "vc", "do", "dq", "dk", "dv") + (("kp", "vp") if has_prev else ())}
        b, n = pl.program_id(0), pl.program_id(1)

        @pl.when((b == 0) & (n == 0))
        def _():
            db_ref[...] = jnp.zeros_like(db_ref)

        def finish(h, r, dq, dk, dv):
            if strided:
                for name, val in (("dq", dq), ("dk", dk), ("dv", dv)):
                    copies[name][h % n_sets][residue(r), :] = val
            else:
                sl = slice(h * HD, (h + 1) * HD)
                sq[:, sl], sk[:, sl], sv[:, sl] = dq.astype(BF16), dk.astype(BF16), dv.astype(BF16)

        def finish_head(h):
            if strided:
                sl = slice(h * HD, (h + 1) * HD)
                sq[:, sl] = copies["dq"][h % n_sets][...].astype(BF16)
                sk[:, sl] = copies["dk"][h % n_sets][...].astype(BF16)
                sv[:, sl] = copies["dv"][h % n_sets][...].astype(BF16)

        def write_block(blk_idx):
            row0 = pl.multiple_of(blk_idx * rows, rows)
            _write_columns([(sq, CB * (CB_Q + g)), (sk, CB * (CB_K + g)), (sv, CB * (CB_V + g))],
                           out_ref, row0, sems)

        def carried(h, r):
            blk = slice(r * QB, (r + 1) * QB)
            return ((blk, slice(h * HD, (h + 1) * HD)), (blk, slice(GW + h * HD, GW + (h + 1) * HD)),
                    (blk, slice(2 * GW + h * HD, 2 * GW + (h + 1) * HD)))

        if has_prev:
            @pl.when(n == 0)
            def _():
                carry[...] = jnp.zeros_like(carry)

            @pl.when(n == nsb)
            def _():
                for h in range(4):
                    for r in range(dil):
                        cq, ck, cv = carried(h, r)
                        finish(h, r, carry[cq], carry[ck], carry[cv])
                    finish_head(h)
                write_block(b * nsb + nsb - 1)

        @pl.when(n < nsb)
        def _():
            if strided:
                for r in range(dil):
                    sts[r * QB:(r + 1) * QB, :] = st_ref[residue(r), :]
            refs_of = {"q": q_ref, "kc": kc_ref, "vc": vc_ref, "do": do_ref, "kp": kp_ref, "vp": vp_ref}
            for chunk in _unit_chunks(dil):
                heads = sorted({h for h, _ in chunk})
                rows_of = {h: {name: _residue_rows(refs_of[name], copies[name], h, residue)
                               for name in refs_of if refs_of[name] is not None}
                           for h in heads}

                def batch(name):
                    return jnp.stack([rows_of[h][name](r) for h, r in chunk])

                q, k, v, do = batch("q"), batch("kc"), batch("vc"), batch("do")
                if has_prev:
                    k = jnp.concatenate([batch("kp"), k], axis=1)
                    v = jnp.concatenate([batch("vp"), v], axis=1)
                    bias_b = jnp.stack([b_ref[h] for h, _ in chunk])
                else:
                    bias_b = jnp.stack([b_ref[h, :, QB:] for h, _ in chunk])
                lse = jnp.stack([sts[r * QB:(r + 1) * QB, h:h + 1] for h, r in chunk])
                delta = jnp.stack([sts[r * QB:(r + 1) * QB, 4 + h:5 + h] for h, r in chunk])
                s = jnp.einsum("uqd,ukd->uqk", q, k, preferred_element_type=F32) * SCALE + bias_b
                p = jnp.exp(s - lse)
                ds = p * (jnp.einsum("uqd,ukd->uqk", do, v, preferred_element_type=F32) - delta)
                for h in heads:
                    mine = [ds[i] for i, (hh, _) in enumerate(chunk) if hh == h]
                    tot = mine[0]
                    for extra in mine[1:]:
                        tot = tot + extra
                    if has_prev:
                        db_ref[h] += tot
                    else:
                        db_ref[h, :, QB:] += tot
                dsb, pb = ds.astype(BF16), p.astype(BF16)
                dq = jnp.einsum("uqk,ukd->uqd", dsb, k, preferred_element_type=F32) * SCALE
                dk = jnp.einsum("uqk,uqd->ukd", dsb, q, preferred_element_type=F32) * SCALE
                dv = jnp.einsum("uqk,uqd->ukd", pb, do, preferred_element_type=F32)
                for i, (h, r) in enumerate(chunk):
                    if has_prev:
                        cq, ck, cv = carried(h, r)
                        finish(h, r, carry[cq], carry[ck] + dk[i, :QB], carry[cv] + dv[i, :QB])
                        carry[cq] = dq[i]
                        carry[ck] = dk[i, QB:]
                        carry[cv] = dv[i, QB:]
                    else:
                        finish(h, r, dq[i], dk[i], dv[i])
                for h in heads:
                    finish_head(h)
            if has_prev:
                @pl.when(n > 0)
                def _():
                    write_block(b * nsb + n - 1)
            else:
                write_block(b)

    def row(b, n):
        return b * nsb + jnp.minimum(n, nsb - 1)

    def prev(b, n):
        return b * nsb + jnp.maximum(jnp.minimum(n, nsb - 1) - 1, 0)

    in_specs = [
        pl.BlockSpec((rows, GW), lambda b, n: (row(b, n), CB_Q + g)),
        pl.BlockSpec((rows, GW), lambda b, n: (row(b, n), CB_K + g)),
        pl.BlockSpec((rows, GW), lambda b, n: (row(b, n), CB_V + g)),
        pl.BlockSpec((rows, GW), lambda b, n: (row(b, n), 0)),
        pl.BlockSpec((rows, 128), lambda b, n: (row(b, n), 0)),
        pl.BlockSpec((None, None, 4, QB, 2 * QB),
                     lambda b, n: (g, jnp.minimum(jnp.minimum(n, nsb - 1), 1), 0, 0, 0)),
    ]
    args = [proj, proj, proj, d_out, stats, bias]
    scratch = [pltpu.VMEM((rows, GW), BF16)] * 3 + [pltpu.SemaphoreType.DMA((3,))]
    if has_prev:
        in_specs += [pl.BlockSpec((rows, GW), lambda b, n: (prev(b, n), CB_K + g)),
                     pl.BlockSpec((rows, GW), lambda b, n: (prev(b, n), CB_V + g))]
        args += [proj, proj]
        scratch.append(pltpu.VMEM((rows, 3 * GW), F32))
    if strided:
        n_copied = (7 + (2 if has_prev else 0)) * n_sets
        scratch += [pltpu.VMEM((rows, 128), F32)] * (1 + n_copied)
    in_specs.append(pl.BlockSpec(memory_space=pl.ANY))
    args.append(dproj)
    return pl.pallas_call(
        body, name=f"attn_bwd{g}",
        grid=(BL, n_steps),
        in_specs=in_specs,
        out_specs=(pl.BlockSpec(memory_space=pl.ANY),
                   pl.BlockSpec((4, QB, 2 * QB), lambda b, n: (0, 0, 0))),
        out_shape=(jax.ShapeDtypeStruct((T, NCOL), BF16),
                   jax.ShapeDtypeStruct((4, QB, 2 * QB), F32)),
        scratch_shapes=scratch,
        input_output_aliases={len(args) - 1: 0},
        compiler_params=pltpu.CompilerParams(vmem_limit_bytes=VMEM_LIMIT),
    )(*args)


def _tail(x2, tgt2, mod3, o_g, lse_g, proj, w_ao, w_co, w_o, conv_w, conv_b, ln_g, ln_b):
    tm = 256
    per_seq = S // tm
    halo = 16

    def body(x_ref, t_ref, mod_ref, o1_ref, o2_ref, o3_ref, l1_ref, l2_ref, l3_ref,
             ga_ref, u_ref, bg_ref, cg_ref, gc_ref, ma_ref, mc_ref, up_ref, cp_ref,
             wao_ref, wco_ref, wo_ref, cw_ref, cb_ref, lg_ref, lb_ref,
             dproj_ref, dyc_ref, do_ref, st_ref, dxd_ref,
             mg_ref, dy_ref, ain_ref, dao_ref, sin_ref, dso_ref, vec_ref,
             dga_s, dbg_s, dgm_s, sems):
        i = pl.program_id(0)
        bidx = i // per_seq
        first = (i % per_seq) == 0

        @pl.when(i == 0)
        def _():
            vec_ref[...] = jnp.zeros_like(vec_ref)

        l1, l2, l3 = l1_ref[...], l2_ref[...], l3_ref[...]
        mx = jnp.maximum(jnp.maximum(l1, l2), l3)
        e1, e2, e3 = jnp.exp(l1 - mx), jnp.exp(l2 - mx), jnp.exp(l3 - mx)
        esum = e1 + e2 + e3
        lse_tot = mx + jnp.log(esum)
        w1, w2, w3 = e1 / esum, e2 / esum, e3 / esum

        def per_head(wv):
            return jnp.concatenate([jnp.broadcast_to(wv[:, h:h + 1], (tm, HD)) for h in range(4)], axis=1)

        o = per_head(w1) * o1_ref[...] + per_head(w2) * o2_ref[...] + per_head(w3) * o3_ref[...]

        ga = ga_ref[...].astype(F32)
        sig_ga = _sigmoid(ga)
        silu_ga = ga * sig_ga
        a_in = (o * silu_ga).astype(BF16)
        a_out = _dot(a_in, wao_ref[...])

        u = u_ref[...].astype(F32)
        cg = cg_ref[...].astype(F32)
        z = cg * u
        zp = cp_ref[...].astype(F32) * up_ref[...].astype(F32)
        zp = jnp.where(first, 0.0, zp)
        zcat = jnp.concatenate([zp, z], axis=0)
        z1 = pltpu.roll(zcat, 1, 0)[halo:]
        z2 = pltpu.roll(zcat, 2, 0)[halo:]
        y_conv = cw_ref[0:1, :] * z2 + cw_ref[1:2, :] * z1 + cw_ref[2:3, :] * z + cb_ref[...]
        gc = gc_ref[...].astype(F32)
        sig_gc = _sigmoid(gc)
        silu_gc = gc * sig_gc
        bg = bg_ref[...].astype(F32)
        s_in = (bg * y_conv * silu_gc).astype(BF16)
        s_out = _dot(s_in, wco_ref[...])

        sa = _sigmoid(ma_ref[...].astype(F32))
        sc = _sigmoid(mc_ref[...].astype(F32))
        merged = (sa * a_out + sc * s_out).astype(BF16)
        y = _dot(merged, wo_ref[...])
        gate1 = 1.0 + mod_ref[0, 2:3, :]
        xv = x_ref[...]
        resid = ALPHA * xv + gate1 * y
        mu = jnp.mean(resid, axis=1, keepdims=True)
        xc = resid - mu
        var = jnp.mean(xc * xc, axis=1, keepdims=True)
        rstd = lax.rsqrt(var + LN_EPS)
        xhat = xc * rstd
        lg = lg_ref[...]
        err = xhat * lg + lb_ref[...] - t_ref[...]
        vec_ref[3:4, :] += (0.5 / D) * jnp.sum(err * err, axis=0, keepdims=True)

        dout = err * (1.0 / D)
        vec_ref[1:2, :] += jnp.sum(dout * xhat, axis=0, keepdims=True)
        vec_ref[2:3, :] += jnp.sum(dout, axis=0, keepdims=True)
        dxh = dout * lg
        dres = rstd * (dxh - jnp.mean(dxh, axis=1, keepdims=True)
                       - xhat * jnp.mean(dxh * xhat, axis=1, keepdims=True))
        dxd_ref[...] = ALPHA * dres
        dgate = jnp.sum(dres * y, axis=0, keepdims=True)
        vec_ref[4:5, :] += jnp.where(bidx == 0, dgate, 0.0)
        vec_ref[5:6, :] += jnp.where(bidx == 1, dgate, 0.0)
        dy = (dres * gate1).astype(BF16)

        dmerged = _dot_nt(dy, wo_ref[...])
        da_out = (dmerged * sa).astype(BF16)
        ds_out = (dmerged * sc).astype(BF16)
        dgm_s[:, 2 * D:3 * D] =(dmerged * s_out * sc * (1.0 - sc)).astype(BF16)
        dgm_s[:, D:2 * D] =(dmerged * a_out * sa * (1.0 - sa)).astype(BF16)
        da_in = _dot_nt(da_out, wao_ref[...])
        ds_in = _dot_nt(ds_out, wco_ref[...])

        d_o = da_in * silu_ga
        do_ref[...] = d_o.astype(BF16)
        dga_s[...] =(da_in * o * (sig_ga * (1.0 + ga * (1.0 - sig_ga)))).astype(BF16)
        lane = lax.broadcasted_iota(jnp.int32, (tm, 128), 1)
        stats = lse_tot
        od = o * d_o
        for h in range(4):
            delta = jnp.sum(od[:, h * HD:(h + 1) * HD], axis=1, keepdims=True)
            stats = jnp.where(lane == 4 + h, delta, stats)
        st_ref[...] = stats

        dbg_s[...] =(ds_in * y_conv * silu_gc).astype(BF16)
        dyc = ds_in * bg * silu_gc
        dyc_ref[...] = dyc
        vec_ref[0:1, :] += jnp.sum(dyc, axis=0, keepdims=True)
        dgm_s[:, 0:D] =(ds_in * bg * y_conv * (sig_gc * (1.0 + gc * (1.0 - sig_gc)))).astype(BF16)

        mg_ref[...] = merged
        dy_ref[...] = dy
        ain_ref[...] = a_in
        dao_ref[...] = da_out
        sin_ref[...] = s_in
        dso_ref[...] = ds_out
        _write_columns([(dga_s, CB * CB_GA), (dbg_s, D * KB_BG), (dgm_s, D * KB_GC)],
                       dproj_ref, pl.multiple_of(i * tm, tm), sems)

    def tile(width, cblk=0):
        return pl.BlockSpec((tm, width), lambda i: (i, cblk))

    def whole(shape):
        return pl.BlockSpec(shape, lambda i: tuple(0 for _ in shape))

    prev_rows = lambda i: (jnp.maximum(i * (tm // halo) - 1, 0),)
    in_specs = [
        tile(D), tile(D), pl.BlockSpec((1, 3, D), lambda i: (i // per_seq, 0, 0)),
        tile(GW), tile(GW), tile(GW), tile(128), tile(128), tile(128),
        tile(GW, CB_GA), tile(D, KB_U), tile(D, KB_BG), tile(D, KB_CG), tile(D, KB_GC),
        tile(D, KB_MA), tile(D, KB_MC),
        pl.BlockSpec((halo, D), lambda i: (*prev_rows(i), KB_U)),
        pl.BlockSpec((halo, D), lambda i: (*prev_rows(i), KB_CG)),
        whole((GW, D)), whole((D, D)), whole((D, D)),
        whole((3, D)), whole((1, D)), whole((1, D)), whole((1, D)),
    ]
    out_specs = (
        pl.BlockSpec(memory_space=pl.ANY), tile(D), tile(GW), tile(128), tile(D),
        tile(D), tile(D), tile(GW), tile(D), tile(D), tile(D),
        pl.BlockSpec((8, D), lambda i: (0, 0)),
    )
    out_shape = (
        jax.ShapeDtypeStruct((T, NCOL), BF16),
        jax.ShapeDtypeStruct((T, D), F32),
        jax.ShapeDtypeStruct((T, GW), BF16),
        jax.ShapeDtypeStruct((T, 128), F32),
        jax.ShapeDtypeStruct((T, D), F32),
        jax.ShapeDtypeStruct((T, D), BF16),
        jax.ShapeDtypeStruct((T, D), BF16),
        jax.ShapeDtypeStruct((T, GW), BF16),
        jax.ShapeDtypeStruct((T, D), BF16),
        jax.ShapeDtypeStruct((T, D), BF16),
        jax.ShapeDtypeStruct((T, D), BF16),
        jax.ShapeDtypeStruct((8, D), F32),
    )
    return pl.pallas_call(
        body, name="tail",
        grid=(T // tm,),
        in_specs=in_specs, out_specs=out_specs, out_shape=out_shape,
        scratch_shapes=[pltpu.VMEM((tm, GW), BF16), pltpu.VMEM((tm, D), BF16), pltpu.VMEM((tm, 3 * D), BF16),
                        pltpu.SemaphoreType.DMA((3,))],
        compiler_params=pltpu.CompilerParams(vmem_limit_bytes=VMEM_LIMIT),
    )(x2, tgt2, mod3, *o_g, *lse_g, proj, proj, proj, proj, proj, proj, proj, proj, proj,
      w_ao, w_co, w_o, conv_w, conv_b, ln_g, ln_b)


def _conv_bwd(dyc, proj, conv_w, dproj):
    tm = 512
    per_seq = S // tm
    halo = 16

    def body(d_ref, dn_ref, u_ref, c_ref, up_ref, cp_ref, cw_ref, _, dproj_ref, g_ref, du_s, dc_s, sems):
        i = pl.program_id(0)
        first = (i % per_seq) == 0
        last = (i % per_seq) == per_seq - 1

        @pl.when(i == 0)
        def _():
            g_ref[...] = jnp.zeros_like(g_ref)

        d = d_ref[...]
        dn = jnp.where(last, 0.0, dn_ref[...])
        dcat = jnp.concatenate([d, dn], axis=0)
        d1 = pltpu.roll(dcat, tm + 8 - 1, 0)[:tm]
        d2 = pltpu.roll(dcat, tm + 8 - 2, 0)[:tm]
        dz = cw_ref[2:3, :] * d + cw_ref[1:2, :] * d1 + cw_ref[0:1, :] * d2
        u = u_ref[...].astype(F32)
        cg = c_ref[...].astype(F32)
        du_s[...] = (dz * cg).astype(BF16)
        dc_s[...] = (dz * u).astype(BF16)
        _write_columns([(du_s, D * KB_U), (dc_s, D * KB_CG)], dproj_ref, pl.multiple_of(i * tm, tm), sems)

        z = cg * u
        zp = jnp.where(first, 0.0, cp_ref[...].astype(F32) * up_ref[...].astype(F32))
        zcat = jnp.concatenate([zp, z], axis=0)
        z1 = pltpu.roll(zcat, 1, 0)[halo:]
        z2 = pltpu.roll(zcat, 2, 0)[halo:]
        g_ref[0:1, :] += jnp.sum(d * z2, axis=0, keepdims=True)
        g_ref[1:2, :] += jnp.sum(d * z1, axis=0, keepdims=True)
        g_ref[2:3, :] += jnp.sum(d * z, axis=0, keepdims=True)

    n_tiles = T // tm
    prev_rows = lambda i: jnp.maximum(i * (tm // halo) - 1, 0)
    next_rows = lambda i: jnp.minimum((i + 1) * (tm // 8), T // 8 - 1)
    return pl.pallas_call(
        body, name="conv_bwd",
        grid=(n_tiles,),
        in_specs=[pl.BlockSpec((tm, D), lambda i: (i, 0)),
                  pl.BlockSpec((8, D), lambda i: (next_rows(i), 0)),
                  pl.BlockSpec((tm, D), lambda i: (i, KB_U)),
                  pl.BlockSpec((tm, D), lambda i: (i, KB_CG)),
                  pl.BlockSpec((halo, D), lambda i: (prev_rows(i), KB_U)),
                  pl.BlockSpec((halo, D), lambda i: (prev_rows(i), KB_CG)),
                  pl.BlockSpec((3, D), lambda i: (0, 0)),
                  pl.BlockSpec(memory_space=pl.ANY)],
        out_specs=(pl.BlockSpec(memory_space=pl.ANY),
                   pl.BlockSpec((8, D), lambda i: (0, 0))),
        out_shape=(jax.ShapeDtypeStruct((T, NCOL), BF16),
                   jax.ShapeDtypeStruct((8, D), F32)),
        scratch_shapes=[pltpu.VMEM((tm, D), BF16), pltpu.VMEM((tm, D), BF16), pltpu.SemaphoreType.DMA((2,))],
        input_output_aliases={7: 0},
        compiler_params=pltpu.CompilerParams(vmem_limit_bytes=VMEM_LIMIT),
    )(dyc, dyc, proj, proj, proj, proj, conv_w, dproj)


def _dh_dx(dproj, w_in_all, x2, dxd, mod3, chip_sums):
    tm = 512
    per_seq = S // tm
    n = len(chip_sums)

    def body(*refs):
        d_ref, w_ref, x_ref, dxd_ref, mod_ref = refs[:5]
        ins = refs[5:5 + n]
        gx_ref, vec_ref = refs[5 + n:7 + n]
        outs = refs[7 + n:7 + 2 * n]
        acc, send_sems, recv_sems, local_sems = refs[7 + 2 * n:]
        i, jj = pl.program_id(0), pl.program_id(1)

        @pl.when((i == 0) & (jj == 0))
        def _():
            vec_ref[...] = jnp.zeros_like(vec_ref)
            if n:
                sends, _, mine = _chip_copies(ins, outs, send_sems, recv_sems, local_sems)
                for cp in sends + mine:
                    cp.start()

        if n:
            @pl.when((i == T // tm - 1) & (jj == N_DEV - 1))
            def _():
                sends, arrivals, mine = _chip_copies(ins, outs, send_sems, recv_sems, local_sems)
                for cp in arrivals:
                    cp.wait_recv()
                for cp in sends:
                    cp.wait_send()
                for cp in mine:
                    cp.wait()

        @pl.when(jj == 0)
        def _():
            acc[...] = jnp.zeros_like(acc)

        acc[...] += _dot_nt(d_ref[...], w_ref[...])

        @pl.when(jj == N_DEV - 1)
        def _():
            dh = acc[...]
            bidx = i // per_seq
            gx_ref[...] = dxd_ref[...] + dh * (1.0 + mod_ref[0, 1:2, :])
            dshift = jnp.sum(dh, axis=0, keepdims=True)
            dscale = jnp.sum(dh * x_ref[...], axis=0, keepdims=True)
            vec_ref[0:1, :] += jnp.where(bidx == 0, dshift, 0.0)
            vec_ref[1:2, :] += jnp.where(bidx == 1, dshift, 0.0)
            vec_ref[2:3, :] += jnp.where(bidx == 0, dscale, 0.0)
            vec_ref[3:4, :] += jnp.where(bidx == 1, dscale, 0.0)

    any_spec = pl.BlockSpec(memory_space=pl.ANY)
    res = pl.pallas_call(
        body, name="dh_dx",
        grid=(T // tm, N_DEV),
        in_specs=[
            pl.BlockSpec((tm, SHARD), lambda i, jj: (i, jj)),
            pl.BlockSpec((None, D, SHARD), lambda i, jj: (jj, 0, 0)),
            pl.BlockSpec((tm, D), lambda i, jj: (i, 0)),
            pl.BlockSpec((tm, D), lambda i, jj: (i, 0)),
            pl.BlockSpec((1, 3, D), lambda i, jj: (i // per_seq, 0, 0))] + [any_spec] * n,
        out_specs=(pl.BlockSpec((tm, D), lambda i, jj: (i, 0)),
                   pl.BlockSpec((8, D), lambda i, jj: (0, 0))) + (any_spec,) * n,
        out_shape=(jax.ShapeDtypeStruct((T, D), F32), jax.ShapeDtypeStruct((8, D), F32))
                  + tuple(jax.ShapeDtypeStruct(a.shape, a.dtype) for a in chip_sums),
        scratch_shapes=[pltpu.VMEM((tm, D), F32), pltpu.SemaphoreType.DMA((max(3 * n, 1),)),
                        pltpu.SemaphoreType.DMA((max(3 * n, 1),)), pltpu.SemaphoreType.DMA((max(n, 1),))],
        compiler_params=pltpu.CompilerParams(vmem_limit_bytes=VMEM_LIMIT),
    )(dproj, w_in_all, x2, dxd, mod3, *chip_sums)
    return res[0], res[1], res[2:]


def _mm_tn(a, b, tn, blocks_leading, name):
    kk, m = a.shape
    n = b.shape[1]
    tk = 1024

    def body(a_ref, b_ref, o_ref, acc):
        @pl.when(pl.program_id(1) == 0)
        def _():
            acc[...] = jnp.zeros_like(acc)

        acc[...] += _dot_tn(a_ref[...], b_ref[...])

        @pl.when(pl.program_id(1) == kk // tk - 1)
        def _():
            o_ref[...] = acc[...].astype(BF16)

    if blocks_leading:
        out_spec = pl.BlockSpec((None, m, tn), lambda j, k: (j, 0, 0))
        out_shape = jax.ShapeDtypeStruct((n // tn, m, tn), BF16)
    else:
        out_spec = pl.BlockSpec((m, tn), lambda j, k: (0, j))
        out_shape = jax.ShapeDtypeStruct((m, n), BF16)
    return pl.pallas_call(
        body, name=name,
        grid=(n // tn, kk // tk),
        in_specs=[pl.BlockSpec((tk, m), lambda j, k: (k, 0)),
                  pl.BlockSpec((tk, tn), lambda j, k: (k, j))],
        out_specs=out_spec, out_shape=out_shape,
        scratch_shapes=[pltpu.VMEM((m, tn), F32)],
        compiler_params=pltpu.CompilerParams(vmem_limit_bytes=VMEM_LIMIT),
    )(a, b)


def _adamw(parts, w, m, v, name, row_tile=None):
    n_parts, rows, cols = parts.shape
    tr = rows if row_tile is None else row_tile
    c1 = 1.0 - ADAM_B1 ** ADAM_STEP
    c2 = 1.0 - ADAM_B2 ** ADAM_STEP

    def body(p_ref, w_ref, m_ref, v_ref, g_ref, d_ref, nm_ref, nv_ref):
        g = p_ref[0].astype(F32)
        for s in range(1, n_parts):
            g = g + p_ref[s].astype(F32)
        nm = ADAM_B1 * m_ref[...] + (1.0 - ADAM_B1) * g
        nv = ADAM_B2 * v_ref[...] + (1.0 - ADAM_B2) * (g * g)
        m_hat = nm / c1
        v_hat = nv / c2
        g_ref[...] = g
        d_ref[...] = -ADAM_LR * (m_hat / (jnp.sqrt(v_hat) + ADAM_EPS) + ADAM_WD * w_ref[...])
        nm_ref[...] = nm
        nv_ref[...] = nv

    blk = pl.BlockSpec((tr, cols), lambda i: (i, 0))
    shp = jax.ShapeDtypeStruct((rows, cols), F32)
    return pl.pallas_call(
        body, name=name,
        grid=(rows // tr,),
        in_specs=[pl.BlockSpec((n_parts, tr, cols), lambda i: (0, i, 0)), blk, blk, blk],
        out_specs=(blk, blk, blk, blk),
        out_shape=(shp, shp, shp, shp),
        compiler_params=pltpu.CompilerParams(vmem_limit_bytes=VMEM_LIMIT),
    )(parts, w, m, v)


def _loss_sum(rows):
    def body(r_ref, o_ref):
        o_ref[...] = jnp.sum(jnp.sum(r_ref[...], axis=0, keepdims=True), axis=1, keepdims=True)

    return pl.pallas_call(body, name="loss_sum", out_shape=jax.ShapeDtypeStruct((1, 1), F32))(rows)


def _local_step(x2, tgt2, mod3, w_in_all, w_ao, w_co, w_o, conv_w, conv_b, rel_bias, ln_g, ln_b):
    buckets_np, masks_np = _bucket_maps()
    buckets, masks = jnp.asarray(buckets_np), jnp.asarray(masks_np)

    h = _prep_h(x2, mod3)
    proj = _proj(h, w_in_all)
    bias = _bias_expand(rel_bias, buckets, masks)
    fwd = [_attn_fwd(proj, bias, g) for g in range(3)]
    o_g = [f[0] for f in fwd]
    lse_g = [f[1] for f in fwd]

    (dproj, dyc, d_o, stats, dxd, merged, dy, a_in, da_out, s_in, ds_out, tail_vec) = _tail(
        x2, tgt2, mod3, o_g, lse_g, proj, w_ao, w_co, w_o, conv_w, conv_b, ln_g, ln_b)

    dbias = []
    for g in range(3):
        dproj, db = _attn_bwd(proj, d_o, stats, bias, dproj, g)
        dbias.append(db)
    g_rel_bias = _bias_grad(*dbias, buckets)
    dproj, conv_vec = _conv_bwd(dyc, proj, conv_w, dproj)

    gw_in = _mm_tn(h, dproj, SHARD, True, "gw_in")
    gw_o = _mm_tn(merged, dy, 512, False, "gw_o")
    gw_co = _mm_tn(s_in, ds_out, 512, False, "gw_conv_out")
    gw_ao = _mm_tn(a_in, da_out, 128, True, "gw_attn_out")
    return dproj, dxd, gw_in, gw_ao, gw_co, gw_o, conv_vec, g_rel_bias, tail_vec


def kernel(x, c, w_ada, b_ada, w_in, conv_w, conv_b, rel_bias, w_attn_out, w_conv_out, w_o, ln_g, ln_b, loss_target, m_w_ada, m_b_ada, m_w_in, m_conv_w, m_conv_b, m_rel_bias, m_w_attn_out, m_w_conv_out, m_w_o, m_ln_g, m_ln_b, v_w_ada, v_b_ada, v_w_in, v_conv_w, v_conv_b, v_rel_bias, v_w_attn_out, v_w_conv_out, v_w_o, v_ln_g, v_ln_b):
    me = _my_index()
    x2 = x.reshape(T, D)
    tgt2 = loss_target.reshape(T, D)

    w_in_all, w_ao_g, w_co_g, w_o_g, conv_w_g, c_g = _all_gather(
        [w_in[0].astype(BF16), w_attn_out[0].astype(BF16), w_conv_out[0].astype(BF16),
         w_o[0].astype(BF16), conv_w[0], c],
        [jax.ShapeDtypeStruct((N_DEV, D, SHARD), BF16),
         jax.ShapeDtypeStruct((N_DEV, GW, D // N_DEV), BF16),
         jax.ShapeDtypeStruct((N_DEV, D // N_DEV, D), BF16),
         jax.ShapeDtypeStruct((N_DEV, D // N_DEV, D), BF16),
         jax.ShapeDtypeStruct((N_DEV, 3, D // N_DEV), F32),
         jax.ShapeDtypeStruct((N_DEV, BL, D), F32)],
        [_slot_leading] * 6,
        "gather_weights")
    w_ao_full = jnp.transpose(w_ao_g, (1, 0, 2)).reshape(GW, D)
    w_co_full = w_co_g.reshape(D, D)
    w_o_full = w_o_g.reshape(D, D)
    conv_w_full = jnp.transpose(conv_w_g, (1, 0, 2)).reshape(3, D)
    c_all = c_g.reshape(N_DEV * BL, D)

    b_cols = lax.dynamic_slice(b_ada, (0, me * ADA_SHARD), (1, ADA_SHARD))
    mod_cols = _ada_fwd(c_all, w_ada[0], b_cols)
    (mod_g,) = _all_gather([mod_cols], [jax.ShapeDtypeStruct((N_DEV, N_DEV * BL, ADA_SHARD), F32)],
                           [_slot_leading], "gather_mod")
    mod_all = jnp.transpose(mod_g, (1, 0, 2)).reshape(N_DEV * BL, 3 * D)
    mod3 = lax.dynamic_slice(mod_all, (me * BL, 0), (BL, 3 * D)).reshape(BL, 3, D)

    (dproj, dxd, gw_in, gw_ao, gw_co, gw_o, conv_vec, g_rel_bias, tail_vec) = _local_step(
        x2, tgt2, mod3, w_in_all, w_ao_full, w_co_full, w_o_full,
        conv_w_full, conv_b, rel_bias, ln_g, ln_b)

    g_conv_w_blocks = jnp.transpose(conv_vec[0:3].reshape(3, N_DEV, D // N_DEV), (1, 0, 2))
    partials = [gw_in, gw_ao, gw_co.reshape(N_DEV, D // N_DEV, D), gw_o.reshape(N_DEV, D // N_DEV, D),
                g_conv_w_blocks]
    sib = _pair_exchange(
        partials,
        [jax.ShapeDtypeStruct((4, D, SHARD), BF16),
         jax.ShapeDtypeStruct((4, GW, D // N_DEV), BF16),
         jax.ShapeDtypeStruct((4, D // N_DEV, D), BF16),
         jax.ShapeDtypeStruct((4, D // N_DEV, D), BF16),
         jax.ShapeDtypeStruct((4, 3, D // N_DEV), F32)],
        [_slot_leading] * 5,
        "pair_grads")
    tiles = [256, None, None, None, None]
    names = ["w_in", "w_attn_out", "w_conv_out", "w_o", "conv_w"]
    core = lax.axis_index("c").astype(jnp.int32).reshape(1)
    chip_sums = [_pair_add(core, partials[a], sib[a], tiles[a], "pair_add_" + names[a]) for a in range(5)]
    grad_x, mod_vec, (r_in, r_ao, r_co, r_o, r_cw) = _dh_dx(dproj, w_in_all, x2, dxd, mod3, chip_sums)

    small = jnp.concatenate([
        tail_vec[0:4],
        jnp.pad(g_rel_bias.reshape(1, N_BUCKETS * N_HEADS), ((0, 0), (0, D - N_BUCKETS * N_HEADS))),
        jnp.zeros((3, D), F32)], axis=0)
    dmod = jnp.concatenate([mod_vec[0:2], mod_vec[2:4], tail_vec[4:6]], axis=1)
    small_g, dmod_g = _all_gather(
        [small, dmod],
        [jax.ShapeDtypeStruct((N_DEV, 8, D), F32), jax.ShapeDtypeStruct((N_DEV, BL, 3 * D), F32)],
        [_slot_leading] * 2, "gather_small")
    dmod_all = dmod_g.reshape(N_DEV * BL, 3 * D)
    loss = _loss_sum(small_g[:, 3, :]).reshape(())
    g_w_ada = _ada_bwd(jnp.transpose(c_all), lax.dynamic_slice(dmod_all, (0, me * ADA_SHARD),
                                                               (N_DEV * BL, ADA_SHARD)))

    def upd(parts, w, m, v, name, row_tile=None):
        shape = w.shape
        w2, m2, v2 = (t.reshape(parts.shape[1:]) for t in (w, m, v))
        return tuple(t.reshape(shape) for t in _adamw(parts, w2, m2, v2, name, row_tile))

    res = {
        "w_ada": upd(g_w_ada[None], w_ada, m_w_ada, v_w_ada, "adam_w_ada", 256),
        "b_ada": upd(dmod_all[:, None, :], b_ada, m_b_ada, v_b_ada, "adam_b_ada"),
        "w_in": upd(r_in, w_in, m_w_in, v_w_in, "adam_w_in", 128),
        "conv_w": upd(r_cw, conv_w, m_conv_w, v_conv_w, "adam_conv_w"),
        "conv_b": upd(small_g[:, 0:1, :], conv_b, m_conv_b, v_conv_b, "adam_conv_b"),
        "rel_bias": upd(small_g[:, 4, :N_BUCKETS * N_HEADS].reshape(N_DEV, N_BUCKETS, N_HEADS),
                        rel_bias, m_rel_bias, v_rel_bias, "adam_rel_bias"),
        "w_attn_out": upd(r_ao, w_attn_out, m_w_attn_out, v_w_attn_out, "adam_w_attn_out"),
        "w_conv_out": upd(r_co, w_conv_out, m_w_conv_out, v_w_conv_out, "adam_w_conv_out"),
        "w_o": upd(r_o, w_o, m_w_o, v_w_o, "adam_w_o"),
        "ln_g": upd(small_g[:, 1:2, :], ln_g, m_ln_g, v_ln_g, "adam_ln_g"),
        "ln_b": upd(small_g[:, 2:3, :], ln_b, m_ln_b, v_ln_b, "adam_ln_b"),
    }
    order = ["w_ada", "b_ada", "w_in", "conv_w", "conv_b", "rel_bias", "w_attn_out", "w_conv_out",
             "w_o", "ln_g", "ln_b"]
    outs = [loss, grad_x.reshape(BL, S, D)]
    for k in range(4):
        outs += [res[name][k] for name in order]
    return tuple(outs)
```

```python
import functools
import math

import numpy as np
import jax
import jax.numpy as jnp
from jax import lax
from jax.experimental import pallas as pl
from jax.experimental.pallas import tpu as pltpu

F32 = jnp.float32
BF16 = jnp.bfloat16
MESH = pl.DeviceIdType.MESH

N_DEV = 8
D = 1024
S = 2048
BL = 2
T = BL * S
NCOL = 11264
SHARD = NCOL // N_DEV
CB = 512
NCB = NCOL // CB
HD = 128
GW = 512
QB = 128
DILATIONS = (1, 4, 16)
N_STEPS = 128
N_BUCKETS = 32
N_HEADS = 12
ALPHA = 2.0 ** 0.25
LN_EPS = 1e-5
NEG_INF = -1e30
SCALE = HD ** -0.5
ADA_SHARD = 3 * D // N_DEV

CB_Q, CB_K, CB_V, CB_GA = 0, 3, 6, 9
KB_U, KB_BG, KB_CG, KB_GC, KB_MA, KB_MC = 5, 6, 7, 8, 9, 10

ADAM_LR, ADAM_B1, ADAM_B2, ADAM_EPS, ADAM_WD, ADAM_STEP = 0.001, 0.9, 0.999, 1e-08, 0.01, 10

VMEM_LIMIT = 56 * 1024 * 1024


def _dot(a, b):
    return jnp.dot(a, b, preferred_element_type=F32)


def _dot_nt(a, b):
    return lax.dot_general(a, b, (((1,), (1,)), ((), ())), preferred_element_type=F32)


def _dot_tn(a, b):
    return lax.dot_general(a, b, (((0,), (0,)), ((), ())), preferred_element_type=F32)


def _sigmoid(v):
    return 1.0 / (1.0 + jnp.exp(-v))


def _write_columns(pieces, dst_hbm, row0, sems):
    copies = []
    for k, (src, col0) in enumerate(pieces):
        rows, width = src.shape
        copies.append(pltpu.make_async_copy(
            src, dst_hbm.at[pl.ds(row0, rows), pl.ds(col0, width)], sems.at[k]))
    for cp in copies:
        cp.start()
    for cp in copies:
        cp.wait()


def _my_index():
    return 4 * lax.axis_index("x") + 2 * lax.axis_index("y") + lax.axis_index("c")


def _slot_leading(ref, slot):
    return ref.at[slot]


def _all_gather(arrs, out_shapes, slot_fns, name):
    n = len(arrs)

    def body(*refs):
        ins, outs = refs[:n], refs[n:2 * n]
        send_sems, recv_sems, local_sems = refs[2 * n:2 * n + 3]
        stage = refs[2 * n + 3:]
        x, y, c = lax.axis_index("x"), lax.axis_index("y"), lax.axis_index("c")
        me, sibling = (x, y, c), (x, y, 1 - c)
        chips = [(1 - x, y), (x, 1 - y), (1 - x, 1 - y)]

        def blk(a, dev):
            return slot_fns[a](outs[a], 4 * dev[0] + 2 * dev[1] + dev[2])

        def copy(a, k, block, to, src=None):
            dst = blk(a, block)
            return pltpu.make_async_remote_copy(
                src_ref=dst if src is None else src, dst_ref=dst,
                send_sem=send_sems.at[a * 7 + k], recv_sem=recv_sems.at[a * 7 + k],
                device_id=to, device_id_type=MESH)

        first = []
        for a in range(n):
            first.append(copy(a, 0, me, sibling, src=ins[a]))
            first += [copy(a, 1 + j, me, (*chip, c), src=ins[a]) for j, chip in enumerate(chips)]
        for cp in first:
            cp.start()
        loads = [pltpu.make_async_copy(ins[a], stage[a], local_sems.at[a]) for a in range(n)]
        for cp in loads:
            cp.start()
        for cp in loads:
            cp.wait()
        mine = [pltpu.make_async_copy(stage[a], blk(a, me), local_sems.at[a]) for a in range(n)]
        for cp in mine:
            cp.start()
        passed = []
        for j, chip in enumerate(chips):
            for a in range(n):
                copy(a, 1 + j, (*chip, c), me).wait_recv()
                fwd = copy(a, 4 + j, (*chip, c), sibling)
                fwd.start()
                passed.append(fwd)
        for a in range(n):
            copy(a, 0, sibling, me).wait_recv()
        for j, chip in enumerate(chips):
            for a in range(n):
                copy(a, 4 + j, (*chip, 1 - c), me).wait_recv()
        for cp in first + passed:
            cp.wait_send()
        for cp in mine:
            cp.wait()

    any_spec = pl.BlockSpec(memory_space=pl.ANY)
    return pl.pallas_call(
        body, name=name,
        out_shape=tuple(out_shapes),
        in_specs=[any_spec] * n,
        out_specs=tuple([any_spec] * n),
        scratch_shapes=[pltpu.SemaphoreType.DMA((7 * n,)), pltpu.SemaphoreType.DMA((7 * n,)),
                        pltpu.SemaphoreType.DMA((n,))]
                       + [pltpu.VMEM(a.shape, a.dtype) for a in arrs],
    )(*arrs)


def _pair_exchange(arrs, shapes4, src_fns, name):
    n = len(arrs)

    def body(*refs):
        ins, recv = refs[:n], refs[n:2 * n]
        send_sems, recv_sems = refs[2 * n:]
        x, y, c = lax.axis_index("x"), lax.axis_index("y"), lax.axis_index("c")
        sibling = (x, y, 1 - c)
        remote = []
        for a in range(n):
            for q in range(4):
                remote.append(pltpu.make_async_remote_copy(
                    src_ref=src_fns[a](ins[a], 2 * q + 1 - c), dst_ref=recv[a].at[q],
                    send_sem=send_sems.at[a * 4 + q], recv_sem=recv_sems.at[a * 4 + q],
                    device_id=sibling, device_id_type=MESH))
        for cp in remote:
            cp.start()
        for cp in remote:
            cp.wait()

    any_spec = pl.BlockSpec(memory_space=pl.ANY)
    return pl.pallas_call(
        body, name=name,
        out_shape=tuple(shapes4),
        in_specs=[any_spec] * n,
        out_specs=tuple([any_spec] * n),
        scratch_shapes=[pltpu.SemaphoreType.DMA((4 * n,))] * 2,
    )(*arrs)


def _chip_copies(ins, outs, send_sems, recv_sems, local_sems):
    n = len(ins)
    x, y, c = lax.axis_index("x"), lax.axis_index("y"), lax.axis_index("c")
    my_chip = 2 * x + y

    def peer_of(k):
        return ((1 - x) if (k >> 1) & 1 else x, (1 - y) if k & 1 else y, c)

    def copy(a, k, out_chip):
        peer = peer_of(k)
        return pltpu.make_async_remote_copy(
            src_ref=ins[a].at[2 * peer[0] + peer[1]], dst_ref=outs[a].at[out_chip],
            send_sem=send_sems.at[a * 3 + k - 1], recv_sem=recv_sems.at[a * 3 + k - 1],
            device_id=peer, device_id_type=MESH)

    sends = [copy(a, k, my_chip) for k in range(1, 4) for a in range(n)]
    arrivals = []
    for k in range(1, 4):
        peer = peer_of(k)
        arrivals += [copy(a, k, 2 * peer[0] + peer[1]) for a in range(n)]
    mine = [pltpu.make_async_copy(ins[a].at[my_chip], outs[a].at[my_chip], local_sems.at[a])
            for a in range(n)]
    return sends, arrivals, mine


def _pair_add(core, mine, theirs, row_tile, name):
    _, rows, cols = theirs.shape
    tr = rows if row_tile is None else row_tile

    def body(core_ref, a_ref, b_ref, o_ref):
        o_ref[...] = (a_ref[...].astype(F32) + b_ref[...].astype(F32)).astype(o_ref.dtype)

    blk = pl.BlockSpec((None, tr, cols), lambda q, i, core_ref: (q, i, 0))
    return pl.pallas_call(
        body, name=name,
        grid_spec=pltpu.PrefetchScalarGridSpec(
            num_scalar_prefetch=1,
            grid=(4, rows // tr),
            in_specs=[pl.BlockSpec((None, tr, cols), lambda q, i, core_ref: (2 * q + core_ref[0], i, 0)), blk],
            out_specs=blk),
        out_shape=jax.ShapeDtypeStruct(theirs.shape, theirs.dtype),
    )(core, mine, theirs)


def _ada_fwd(c_all, w_ada, b_cols):
    def body(c_ref, w_ref, b_ref, o_ref):
        cv = c_ref[...]
        sc = cv * _sigmoid(cv)
        o_ref[...] = jnp.dot(sc, w_ref[...], preferred_element_type=F32,
                             precision=lax.Precision.HIGHEST) + b_ref[...]

    return pl.pallas_call(
        body, name="ada_fwd",
        out_shape=jax.ShapeDtypeStruct((c_all.shape[0], w_ada.shape[1]), F32),
    )(c_all, w_ada, b_cols)


def _ada_bwd(c_all_t, dmod_cols):
    def body(c_ref, d_ref, o_ref):
        cv = c_ref[...]
        sc = cv * _sigmoid(cv)
        o_ref[...] = jnp.dot(sc, d_ref[...], preferred_element_type=F32,
                             precision=lax.Precision.HIGHEST)

    return pl.pallas_call(
        body, name="ada_bwd",
        out_shape=jax.ShapeDtypeStruct((c_all_t.shape[0], dmod_cols.shape[1]), F32),
    )(c_all_t, dmod_cols)


def _prep_h(x2, mod3):
    ts = 512
    per_seq = S // ts

    def body(x_ref, mod_ref, h_ref):
        shift = mod_ref[0, 0:1, :]
        scale = mod_ref[0, 1:2, :]
        h_ref[...] = (x_ref[...] * (1.0 + scale) + shift).astype(BF16)

    return pl.pallas_call(
        body, name="prep_h",
        grid=(T // ts,),
        in_specs=[pl.BlockSpec((ts, D), lambda i: (i, 0)),
                  pl.BlockSpec((1, 3, D), lambda i: (i // per_seq, 0, 0))],
        out_specs=pl.BlockSpec((ts, D), lambda i: (i, 0)),
        out_shape=jax.ShapeDtypeStruct((T, D), BF16),
    )(x2, mod3)


def _proj(h, w_in_all):
    tm = 1024

    def body(h_ref, w_ref, o_ref):
        o_ref[...] = _dot(h_ref[...], w_ref[...]).astype(BF16)

    return pl.pallas_call(
        body, name="proj",
        grid=(N_DEV, T // tm),
        in_specs=[pl.BlockSpec((tm, D), lambda j, i: (i, 0)),
                  pl.BlockSpec((None, D, SHARD), lambda j, i: (j, 0, 0))],
        out_specs=pl.BlockSpec((tm, SHARD), lambda j, i: (i, j)),
        out_shape=jax.ShapeDtypeStruct((T, NCOL), BF16),
        compiler_params=pltpu.CompilerParams(vmem_limit_bytes=VMEM_LIMIT),
    )(h, w_in_all)


def _bucket_maps():
    a = np.arange(QB)[:, None]
    b = np.arange(2 * QB)[None, :]
    steps = a + QB - b
    maps = []
    for dil in DILATIONS:
        dist = np.maximum(steps, 0) * dil
        nf = np.maximum(dist, 1).astype(np.float32)
        large = 16 + (np.log(nf / np.float32(16)) / np.float32(math.log(128.0))
                      * np.float32(16)).astype(np.int32)
        large = np.minimum(large, N_BUCKETS - 1)
        maps.append(np.where(dist < 16, dist, large).astype(np.int32))
    band = (steps >= 0) & (steps <= N_STEPS)
    first = band & (b >= QB)
    masks = np.stack([first, band]).astype(np.int32)
    return np.stack(maps), masks


def _bias_expand(rel_bias, buckets, masks):
    def body(tab_ref, bk_ref, mk_ref, o_ref):
        for g in range(3):
            bk = bk_ref[g]
            for h in range(4):
                col = 4 * g + h
                val = jnp.zeros((QB, 2 * QB), F32)
                for k in range(N_BUCKETS):
                    val = jnp.where(bk == k, tab_ref[k, col], val)
                o_ref[g, 0, h] = jnp.where(mk_ref[0] != 0, val, NEG_INF)
                o_ref[g, 1, h] = jnp.where(mk_ref[1] != 0, val, NEG_INF)

    return pl.pallas_call(
        body, name="bias_expand",
        in_specs=[pl.BlockSpec(memory_space=pltpu.SMEM),
                  pl.BlockSpec(memory_space=pltpu.VMEM),
                  pl.BlockSpec(memory_space=pltpu.VMEM)],
        out_shape=jax.ShapeDtypeStruct((3, 2, 4, QB, 2 * QB), F32),
    )(rel_bias, buckets, masks)


def _bias_grad(ds1, ds2, ds3, buckets):
    def body(d1_ref, d2_ref, d3_ref, bk_ref, o_ref):
        for g, d_ref in enumerate((d1_ref, d2_ref, d3_ref)):
            bk = bk_ref[g]
            for h in range(4):
                dv = d_ref[h]
                for k in range(N_BUCKETS):
                    o_ref[k, 4 * g + h] = jnp.sum(jnp.where(bk == k, dv, 0.0))

    return pl.pallas_call(
        body, name="bias_grad",
        in_specs=[pl.BlockSpec(memory_space=pltpu.VMEM)] * 4,
        out_specs=pl.BlockSpec(memory_space=pltpu.SMEM),
        out_shape=jax.ShapeDtypeStruct((N_BUCKETS, N_HEADS), F32),
    )(ds1, ds2, ds3, buckets)


def _scratch_sets(rows):
    return 4 if rows <= 512 else 1


def _unit_chunks(dil, size=16):
    units = [(h, r) for h in range(4) for r in range(dil)]
    return [units[i:i + size] for i in range(0, len(units), size)]


def _residue_rows(src_ref, copies, h, residue):
    sl = slice(h * HD, (h + 1) * HD)
    if copies is None:
        return lambda r: src_ref[:, sl]
    buf = copies[h % len(copies)]
    buf[...] = src_ref[:, sl].astype(F32)
    return lambda r: buf[residue(r), :].astype(BF16)


def _attn_fwd(proj, bias, g):
    dil = DILATIONS[g]
    rows = QB * dil
    nsb = S // rows
    has_prev = nsb > 1

    def residue(r):
        return pl.ds(r, QB, stride=dil) if dil > 1 else pl.ds(0, QB)

    strided = dil > 1
    n_sets = _scratch_sets(rows)
    n_in = 6 if has_prev else 4

    def body(*refs):
        q_ref, kc_ref, vc_ref = refs[:3]
        kp_ref, vp_ref = refs[3:5] if has_prev else (None, None)
        b_ref, o_ref, l_ref = refs[n_in - 1:n_in + 2]
        scr = list(refs[n_in + 2:])
        ls = [scr.pop(0) for _ in range(4)]
        copies = {name: [scr.pop(0) for _ in range(n_sets)] if strided else None
                  for name in ("q", "kc", "vc", "o") + (("kp", "vp") if has_prev else ())}
        lane = lax.broadcasted_iota(jnp.int32, (QB, 128), 1)
        refs_of = {"q": q_ref, "kc": kc_ref, "vc": vc_ref, "kp": kp_ref, "vp": vp_ref}
        for chunk in _unit_chunks(dil):
            rows_of = {h: {name: _residue_rows(refs_of[name], copies[name], h, residue)
                           for name in refs_of if refs_of[name] is not None}
                       for h in sorted({h for h, _ in chunk})}

            def batch(name):
                return jnp.stack([rows_of[h][name](r) for h, r in chunk])

            q, k, v = batch("q"), batch("kc"), batch("vc")
            if has_prev:
                k = jnp.concatenate([batch("kp"), k], axis=1)
                v = jnp.concatenate([batch("vp"), v], axis=1)
                bias_b = jnp.stack([b_ref[h] for h, _ in chunk])
            else:
                bias_b = jnp.stack([b_ref[h, :, QB:] for h, _ in chunk])
            s = jnp.einsum("uqd,ukd->uqk", q, k, preferred_element_type=F32) * SCALE + bias_b
            m = jnp.max(s, axis=-1, keepdims=True)
            p = jnp.exp(s - m)
            l = jnp.sum(p, axis=-1, keepdims=True)
            o = jnp.einsum("uqk,ukd->uqd", p.astype(BF16), v, preferred_element_type=F32) / l
            lse = m + jnp.log(l)
            for i, (h, r) in enumerate(chunk):
                if strided:
                    copies["o"][h % n_sets][residue(r), :] = o[i]
                else:
                    o_ref[:, h * HD:(h + 1) * HD] = o[i]
                ls[h][r * QB:(r + 1) * QB, :] = jnp.where(lane == h, lse[i], 0.0)
            if strided:
                for h in sorted({h for h, _ in chunk}):
                    o_ref[:, h * HD:(h + 1) * HD] = copies["o"][h % n_sets][...]
        for r in range(dil):
            blk = slice(r * QB, (r + 1) * QB)
            l_ref[residue(r), :] = (ls[0][blk, :] + ls[1][blk, :]) + (ls[2][blk, :] + ls[3][blk, :])

    def row(b, n):
        return b * nsb + n

    def prev(b, n):
        return b * nsb + jnp.maximum(n - 1, 0)

    in_specs = [
        pl.BlockSpec((rows, GW), lambda b, n: (row(b, n), CB_Q + g)),
        pl.BlockSpec((rows, GW), lambda b, n: (row(b, n), CB_K + g)),
        pl.BlockSpec((rows, GW), lambda b, n: (row(b, n), CB_V + g)),
    ]
    args = [proj, proj, proj]
    n_copied = (4 + (2 if has_prev else 0)) * (n_sets if strided else 0)
    scratch = [pltpu.VMEM((rows, 128), F32)] * (4 + n_copied)
    if has_prev:
        in_specs += [pl.BlockSpec((rows, GW), lambda b, n: (prev(b, n), CB_K + g)),
                     pl.BlockSpec((rows, GW), lambda b, n: (prev(b, n), CB_V + g))]
        args += [proj, proj]
    in_specs.append(pl.BlockSpec((None, None, 4, QB, 2 * QB),
                                 lambda b, n: (g, jnp.minimum(n, 1), 0, 0, 0)))
    args.append(bias)
    return pl.pallas_call(
        body, name=f"attn_fwd{g}",
        grid=(BL, nsb),
        in_specs=in_specs,
        out_specs=(pl.BlockSpec((rows, GW), lambda b, n: (row(b, n), 0)),
                   pl.BlockSpec((rows, 128), lambda b, n: (row(b, n), 0))),
        out_shape=(jax.ShapeDtypeStruct((T, GW), F32), jax.ShapeDtypeStruct((T, 128), F32)),
        scratch_shapes=scratch,
        compiler_params=pltpu.CompilerParams(vmem_limit_bytes=VMEM_LIMIT),
    )(*args)


def _attn_bwd(proj, d_out, stats, bias, dproj, g):
    dil = DILATIONS[g]
    rows = QB * dil
    nsb = S // rows
    has_prev = nsb > 1
    n_steps = nsb + 1 if has_prev else 1
    n_in = 7 + (2 if has_prev else 0)

    def residue(r):
        return pl.ds(r, QB, stride=dil) if dil > 1 else pl.ds(0, QB)

    strided = dil > 1
    n_sets = _scratch_sets(rows)

    def body(*refs):
        q_ref, kc_ref, vc_ref, do_ref, st_ref, b_ref = refs[:6]
        kp_ref, vp_ref = refs[6:8] if has_prev else (None, None)
        out_ref, db_ref = refs[n_in], refs[n_in + 1]
        scr = list(refs[n_in + 2:])
        sq, sk, sv, sems = [scr.pop(0) for _ in range(4)]
        carry = scr.pop(0) if has_prev else None
        sts = scr.pop(0) if strided else st_ref
        copies = {name: [scr.pop(0) for _ in range(n_sets)] if strided else None
                  for name in ("q", "kc", "vc", "do", "dq", "dk", "dv") + (("kp", "vp") if has_prev else ())}
        b, n = pl.program_id(0), pl.program_id(1)

        @pl.when((b == 0) & (n == 0))
        def _():
            db_ref[...] = jnp.zeros_like(db_ref)

        def finish(h, r, dq, dk, dv):
            if strided:
                for name, val in (("dq", dq), ("dk", dk), ("dv", dv)):
                    copies[name][h % n_sets][residue(r), :] = val
            else:
                sl = slice(h * HD, (h + 1) * HD)
                sq[:, sl], sk[:, sl], sv[:, sl] = dq.astype(BF16), dk.astype(BF16), dv.astype(BF16)

        def finish_head(h):
            if strided:
                sl = slice(h * HD, (h + 1) * HD)
                sq[:, sl] = copies["dq"][h % n_sets][...].astype(BF16)
                sk[:, sl] = copies["dk"][h % n_sets][...].astype(BF16)
                sv[:, sl] = copies["dv"][h % n_sets][...].astype(BF16)

        def write_block(blk_idx):
            row0 = pl.multiple_of(blk_idx * rows, rows)
            _write_columns([(sq, CB * (CB_Q + g)), (sk, CB * (CB_K + g)), (sv, CB * (CB_V + g))],
                           out_ref, row0, sems)

        def carried(h, r):
            blk = slice(r * QB, (r + 1) * QB)
            return ((blk, slice(h * HD, (h + 1) * HD)), (blk, slice(GW + h * HD, GW + (h + 1) * HD)),
                    (blk, slice(2 * GW + h * HD, 2 * GW + (h + 1) * HD)))

        if has_prev:
            @pl.when(n == 0)
            def _():
                carry[...] = jnp.zeros_like(carry)

            @pl.when(n == nsb)
            def _():
                for h in range(4):
                    for r in range(dil):
                        cq, ck, cv = carried(h, r)
                        finish(h, r, carry[cq], carry[ck], carry[cv])
                    finish_head(h)
                write_block(b * nsb + nsb - 1)

        @pl.when(n < nsb)
        def _():
            if strided:
                for r in range(dil):
                    sts[r * QB:(r + 1) * QB, :] = st_ref[residue(r), :]
            refs_of = {"q": q_ref, "kc": kc_ref, "vc": vc_ref, "do": do_ref, "kp": kp_ref, "vp": vp_ref}
            for chunk in _unit_chunks(dil):
                heads = sorted({h for h, _ in chunk})
                rows_of = {h: {name: _residue_rows(refs_of[name], copies[name], h, residue)
                               for name in refs_of if refs_of[name] is not None}
                           for h in heads}

                def batch(name):
                    return jnp.stack([rows_of[h][name](r) for h, r in chunk])

                q, k, v, do = batch("q"), batch("kc"), batch("vc"), batch("do")
                if has_prev:
                    k = jnp.concatenate([batch("kp"), k], axis=1)
                    v = jnp.concatenate([batch("vp"), v], axis=1)
                    bias_b = jnp.stack([b_ref[h] for h, _ in chunk])
                else:
                    bias_b = jnp.stack([b_ref[h, :, QB:] for h, _ in chunk])
                lse = jnp.stack([sts[r * QB:(r + 1) * QB, h:h + 1] for h, r in chunk])
                delta = jnp.stack([sts[r * QB:(r + 1) * QB, 4 + h:5 + h] for h, r in chunk])
                s = jnp.einsum("uqd,ukd->uqk", q, k, preferred_element_type=F32) * SCALE + bias_b
                p = jnp.exp(s - lse)
                ds = p * (jnp.einsum("uqd,ukd->uqk", do, v, preferred_element_type=F32) - delta)
                for h in heads:
                    mine = [ds[i] for i, (hh, _) in enumerate(chunk) if hh == h]
                    tot = mine[0]
                    for extra in mine[1:]:
                        tot = tot + extra
                    if has_prev:
                        db_ref[h] += tot
                    else:
                        db_ref[h, :, QB:] += tot
                dsb, pb = ds.astype(BF16), p.astype(BF16)
                dq = jnp.einsum("uqk,ukd->uqd", dsb, k, preferred_element_type=F32) * SCALE
                dk = jnp.einsum("uqk,uqd->ukd", dsb, q, preferred_element_type=F32) * SCALE
                dv = jnp.einsum("uqk,uqd->ukd", pb, do, preferred_element_type=F32)
                for i, (h, r) in enumerate(chunk):
                    if has_prev:
                        cq, ck, cv = carried(h, r)
                        finish(h, r, carry[cq], carry[ck] + dk[i, :QB], carry[cv] + dv[i, :QB])
                        carry[cq] = dq[i]
                        carry[ck] = dk[i, QB:]
                        carry[cv] = dv[i, QB:]
                    else:
                        finish(h, r, dq[i], dk[i], dv[i])
                for h in heads:
                    finish_head(h)
            if has_prev:
                @pl.when(n > 0)
                def _():
                    write_block(b * nsb + n - 1)
            else:
                write_block(b)

    def row(b, n):
        return b * nsb + jnp.minimum(n, nsb - 1)

    def prev(b, n):
        return b * nsb + jnp.maximum(jnp.minimum(n, nsb - 1) - 1, 0)

    in_specs = [
        pl.BlockSpec((rows, GW), lambda b, n: (row(b, n), CB_Q + g)),
        pl.BlockSpec((rows, GW), lambda b, n: (row(b, n), CB_K + g)),
        pl.BlockSpec((rows, GW), lambda b, n: (row(b, n), CB_V + g)),
        pl.BlockSpec((rows, GW), lambda b, n: (row(b, n), 0)),
        pl.BlockSpec((rows, 128), lambda b, n: (row(b, n), 0)),
        pl.BlockSpec((None, None, 4, QB, 2 * QB),
                     lambda b, n: (g, jnp.minimum(jnp.minimum(n, nsb - 1), 1), 0, 0, 0)),
    ]
    args = [proj, proj, proj, d_out, stats, bias]
    scratch = [pltpu.VMEM((rows, GW), BF16)] * 3 + [pltpu.SemaphoreType.DMA((3,))]
    if has_prev:
        in_specs += [pl.BlockSpec((rows, GW), lambda b, n: (prev(b, n), CB_K + g)),
                     pl.BlockSpec((rows, GW), lambda b, n: (prev(b, n), CB_V + g))]
        args += [proj, proj]
        scratch.append(pltpu.VMEM((rows, 3 * GW), F32))
    if strided:
        n_copied = (7 + (2 if has_prev else 0)) * n_sets
        scratch += [pltpu.VMEM((rows, 128), F32)] * (1 + n_copied)
    in_specs.append(pl.BlockSpec(memory_space=pl.ANY))
    args.append(dproj)
    return pl.pallas_call(
        body, name=f"attn_bwd{g}",
        grid=(BL, n_steps),
        in_specs=in_specs,
        out_specs=(pl.BlockSpec(memory_space=pl.ANY),
                   pl.BlockSpec((4, QB, 2 * QB), lambda b, n: (0, 0, 0))),
        out_shape=(jax.ShapeDtypeStruct((T, NCOL), BF16),
                   jax.ShapeDtypeStruct((4, QB, 2 * QB), F32)),
        scratch_shapes=scratch,
        input_output_aliases={len(args) - 1: 0},
        compiler_params=pltpu.CompilerParams(vmem_limit_bytes=VMEM_LIMIT),
    )(*args)


def _tail(x2, tgt2, mod3, o_g, lse_g, proj, w_ao, w_co, w_o, conv_w, conv_b, ln_g, ln_b):
    tm = 256
    per_seq = S // tm
    halo = 16

    def body(x_ref, t_ref, mod_ref, o1_ref, o2_ref, o3_ref, l1_ref, l2_ref, l3_ref,
             ga_ref, u_ref, bg_ref, cg_ref, gc_ref, ma_ref, mc_ref, up_ref, cp_ref,
             wao_ref, wco_ref, wo_ref, cw_ref, cb_ref, lg_ref, lb_ref,
             dproj_ref, dyc_ref, do_ref, st_ref, dxd_ref,
             mg_ref, dy_ref, ain_ref, dao_ref, sin_ref, dso_ref, vec_ref,
             dga_s, dbg_s, dgm_s, sems):
        i = pl.program_id(0)
        bidx = i // per_seq
        first = (i % per_seq) == 0

        @pl.when(i == 0)
        def _():
            vec_ref[...] = jnp.zeros_like(vec_ref)

        l1, l2, l3 = l1_ref[...], l2_ref[...], l3_ref[...]
        mx = jnp.maximum(jnp.maximum(l1, l2), l3)
        e1, e2, e3 = jnp.exp(l1 - mx), jnp.exp(l2 - mx), jnp.exp(l3 - mx)
        esum = e1 + e2 + e3
        lse_tot = mx + jnp.log(esum)
        w1, w2, w3 = e1 / esum, e2 / esum, e3 / esum

        def per_head(wv):
            return jnp.concatenate([jnp.broadcast_to(wv[:, h:h + 1], (tm, HD)) for h in range(4)], axis=1)

        o = per_head(w1) * o1_ref[...] + per_head(w2) * o2_ref[...] + per_head(w3) * o3_ref[...]

        ga = ga_ref[...].astype(F32)
        sig_ga = _sigmoid(ga)
        silu_ga = ga * sig_ga
        a_in = (o * silu_ga).astype(BF16)
        a_out = _dot(a_in, wao_ref[...])

        u = u_ref[...].astype(F32)
        cg = cg_ref[...].astype(F32)
        z = cg * u
        zp = cp_ref[...].astype(F32) * up_ref[...].astype(F32)
        zp = jnp.where(first, 0.0, zp)
        zcat = jnp.concatenate([zp, z], axis=0)
        z1 = pltpu.roll(zcat, 1, 0)[halo:]
        z2 = pltpu.roll(zcat, 2, 0)[halo:]
        y_conv = cw_ref[0:1, :] * z2 + cw_ref[1:2, :] * z1 + cw_ref[2:3, :] * z + cb_ref[...]
        gc = gc_ref[...].astype(F32)
        sig_gc = _sigmoid(gc)
        silu_gc = gc * sig_gc
        bg = bg_ref[...].astype(F32)
        s_in = (bg * y_conv * silu_gc).astype(BF16)
        s_out = _dot(s_in, wco_ref[...])

        sa = _sigmoid(ma_ref[...].astype(F32))
        sc = _sigmoid(mc_ref[...].astype(F32))
        merged = (sa * a_out + sc * s_out).astype(BF16)
        y = _dot(merged, wo_ref[...])
        gate1 = 1.0 + mod_ref[0, 2:3, :]
        xv = x_ref[...]
        resid = ALPHA * xv + gate1 * y
        mu = jnp.mean(resid, axis=1, keepdims=True)
        xc = resid - mu
        var = jnp.mean(xc * xc, axis=1, keepdims=True)
        rstd = lax.rsqrt(var + LN_EPS)
        xhat = xc * rstd
        lg = lg_ref[...]
        err = xhat * lg + lb_ref[...] - t_ref[...]
        vec_ref[3:4, :] += (0.5 / D) * jnp.sum(err * err, axis=0, keepdims=True)

        dout = err * (1.0 / D)
        vec_ref[1:2, :] += jnp.sum(dout * xhat, axis=0, keepdims=True)
        vec_ref[2:3, :] += jnp.sum(dout, axis=0, keepdims=True)
        dxh = dout * lg
        dres = rstd * (dxh - jnp.mean(dxh, axis=1, keepdims=True)
                       - xhat * jnp.mean(dxh * xhat, axis=1, keepdims=True))
        dxd_ref[...] = ALPHA * dres
        dgate = jnp.sum(dres * y, axis=0, keepdims=True)
        vec_ref[4:5, :] += jnp.where(bidx == 0, dgate, 0.0)
        vec_ref[5:6, :] += jnp.where(bidx == 1, dgate, 0.0)
        dy = (dres * gate1).astype(BF16)

        dmerged = _dot_nt(dy, wo_ref[...])
        da_out = (dmerged * sa).astype(BF16)
        ds_out = (dmerged * sc).astype(BF16)
        dgm_s[:, 2 * D:3 * D] =(dmerged * s_out * sc * (1.0 - sc)).astype(BF16)
        dgm_s[:, D:2 * D] =(dmerged * a_out * sa * (1.0 - sa)).astype(BF16)
        da_in = _dot_nt(da_out, wao_ref[...])
        ds_in = _dot_nt(ds_out, wco_ref[...])

        d_o = da_in * silu_ga
        do_ref[...] = d_o.astype(BF16)
        dga_s[...] =(da_in * o * (sig_ga * (1.0 + ga * (1.0 - sig_ga)))).astype(BF16)
        lane = lax.broadcasted_iota(jnp.int32, (tm, 128), 1)
        stats = lse_tot
        od = o * d_o
        for h in range(4):
            delta = jnp.sum(od[:, h * HD:(h + 1) * HD], axis=1, keepdims=True)
            stats = jnp.where(lane == 4 + h, delta, stats)
        st_ref[...] = stats

        dbg_s[...] =(ds_in * y_conv * silu_gc).astype(BF16)
        dyc = ds_in * bg * silu_gc
        dyc_ref[...] = dyc
        vec_ref[0:1, :] += jnp.sum(dyc, axis=0, keepdims=True)
        dgm_s[:, 0:D] =(ds_in * bg * y_conv * (sig_gc * (1.0 + gc * (1.0 - sig_gc)))).astype(BF16)

        mg_ref[...] = merged
        dy_ref[...] = dy
        ain_ref[...] = a_in
        dao_ref[...] = da_out
        sin_ref[...] = s_in
        dso_ref[...] = ds_out
        _write_columns([(dga_s, CB * CB_GA), (dbg_s, D * KB_BG), (dgm_s, D * KB_GC)],
                       dproj_ref, pl.multiple_of(i * tm, tm), sems)

    def tile(width, cblk=0):
        return pl.BlockSpec((tm, width), lambda i: (i, cblk))

    def whole(shape):
        return pl.BlockSpec(shape, lambda i: tuple(0 for _ in shape))

    prev_rows = lambda i: (jnp.maximum(i * (tm // halo) - 1, 0),)
    in_specs = [
        tile(D), tile(D), pl.BlockSpec((1, 3, D), lambda i: (i // per_seq, 0, 0)),
        tile(GW), tile(GW), tile(GW), tile(128), tile(128), tile(128),
        tile(GW, CB_GA), tile(D, KB_U), tile(D, KB_BG), tile(D, KB_CG), tile(D, KB_GC),
        tile(D, KB_MA), tile(D, KB_MC),
        pl.BlockSpec((halo, D), lambda i: (*prev_rows(i), KB_U)),
        pl.BlockSpec((halo, D), lambda i: (*prev_rows(i), KB_CG)),
        whole((GW, D)), whole((D, D)), whole((D, D)),
        whole((3, D)), whole((1, D)), whole((1, D)), whole((1, D)),
    ]
    out_specs = (
        pl.BlockSpec(memory_space=pl.ANY), tile(D), tile(GW), tile(128), tile(D),
        tile(D), tile(D), tile(GW), tile(D), tile(D), tile(D),
        pl.BlockSpec((8, D), lambda i: (0, 0)),
    )
    out_shape = (
        jax.ShapeDtypeStruct((T, NCOL), BF16),
        jax.ShapeDtypeStruct((T, D), F32),
        jax.ShapeDtypeStruct((T, GW), BF16),
        jax.ShapeDtypeStruct((T, 128), F32),
        jax.ShapeDtypeStruct((T, D), F32),
        jax.ShapeDtypeStruct((T, D), BF16),
        jax.ShapeDtypeStruct((T, D), BF16),
        jax.ShapeDtypeStruct((T, GW), BF16),
        jax.ShapeDtypeStruct((T, D), BF16),
        jax.ShapeDtypeStruct((T, D), BF16),
        jax.ShapeDtypeStruct((T, D), BF16),
        jax.ShapeDtypeStruct((8, D), F32),
    )
    return pl.pallas_call(
        body, name="tail",
        grid=(T // tm,),
        in_specs=in_specs, out_specs=out_specs, out_shape=out_shape,
        scratch_shapes=[pltpu.VMEM((tm, GW), BF16), pltpu.VMEM((tm, D), BF16), pltpu.VMEM((tm, 3 * D), BF16),
                        pltpu.SemaphoreType.DMA((3,))],
        compiler_params=pltpu.CompilerParams(vmem_limit_bytes=VMEM_LIMIT),
    )(x2, tgt2, mod3, *o_g, *lse_g, proj, proj, proj, proj, proj, proj, proj, proj, proj,
      w_ao, w_co, w_o, conv_w, conv_b, ln_g, ln_b)


def _conv_bwd(dyc, proj, conv_w, dproj):
    tm = 512
    per_seq = S // tm
    halo = 16

    def body(d_ref, dn_ref, u_ref, c_ref, up_ref, cp_ref, cw_ref, _, dproj_ref, g_ref, du_s, dc_s, sems):
        i = pl.program_id(0)
        first = (i % per_seq) == 0
        last = (i % per_seq) == per_seq - 1

        @pl.when(i == 0)
        def _():
            g_ref[...] = jnp.zeros_like(g_ref)

        d = d_ref[...]
        dn = jnp.where(last, 0.0, dn_ref[...])
        dcat = jnp.concatenate([d, dn], axis=0)
        d1 = pltpu.roll(dcat, tm + 8 - 1, 0)[:tm]
        d2 = pltpu.roll(dcat, tm + 8 - 2, 0)[:tm]
        dz = cw_ref[2:3, :] * d + cw_ref[1:2, :] * d1 + cw_ref[0:1, :] * d2
        u = u_ref[...].astype(F32)
        cg = c_ref[...].astype(F32)
        du_s[...] = (dz * cg).astype(BF16)
        dc_s[...] = (dz * u).astype(BF16)
        _write_columns([(du_s, D * KB_U), (dc_s, D * KB_CG)], dproj_ref, pl.multiple_of(i * tm, tm), sems)

        z = cg * u
        zp = jnp.where(first, 0.0, cp_ref[...].astype(F32) * up_ref[...].astype(F32))
        zcat = jnp.concatenate([zp, z], axis=0)
        z1 = pltpu.roll(zcat, 1, 0)[halo:]
        z2 = pltpu.roll(zcat, 2, 0)[halo:]
        g_ref[0:1, :] += jnp.sum(d * z2, axis=0, keepdims=True)
        g_ref[1:2, :] += jnp.sum(d * z1, axis=0, keepdims=True)
        g_ref[2:3, :] += jnp.sum(d * z, axis=0, keepdims=True)

    n_tiles = T // tm
    prev_rows = lambda i: jnp.maximum(i * (tm // halo) - 1, 0)
    next_rows = lambda i: jnp.minimum((i + 1) * (tm // 8), T // 8 - 1)
    return pl.pallas_call(
        body, name="conv_bwd",
        grid=(n_tiles,),
        in_specs=[pl.BlockSpec((tm, D), lambda i: (i, 0)),
                  pl.BlockSpec((8, D), lambda i: (next_rows(i), 0)),
                  pl.BlockSpec((tm, D), lambda i: (i, KB_U)),
                  pl.BlockSpec((tm, D), lambda i: (i, KB_CG)),
                  pl.BlockSpec((halo, D), lambda i: (prev_rows(i), KB_U)),
                  pl.BlockSpec((halo, D), lambda i: (prev_rows(i), KB_CG)),
                  pl.BlockSpec((3, D), lambda i: (0, 0)),
                  pl.BlockSpec(memory_space=pl.ANY)],
        out_specs=(pl.BlockSpec(memory_space=pl.ANY),
                   pl.BlockSpec((8, D), lambda i: (0, 0))),
        out_shape=(jax.ShapeDtypeStruct((T, NCOL), BF16),
                   jax.ShapeDtypeStruct((8, D), F32)),
        scratch_shapes=[pltpu.VMEM((tm, D), BF16), pltpu.VMEM((tm, D), BF16), pltpu.SemaphoreType.DMA((2,))],
        input_output_aliases={7: 0},
        compiler_params=pltpu.CompilerParams(vmem_limit_bytes=VMEM_LIMIT),
    )(dyc, dyc, proj, proj, proj, proj, conv_w, dproj)


def _dh_dx(dproj, w_in_all, x2, dxd, mod3, chip_sums):
    tm = 1024
    per_seq = S // tm
    n = len(chip_sums)

    def body(*refs):
        d_ref, w_ref, x_ref, dxd_ref, mod_ref = refs[:5]
        ins = refs[5:5 + n]
        gx_ref, vec_ref = refs[5 + n:7 + n]
        outs = refs[7 + n:7 + 2 * n]
        acc, send_sems, recv_sems, local_sems = refs[7 + 2 * n:]
        i, jj = pl.program_id(0), pl.program_id(1)

        @pl.when((i == 0) & (jj == 0))
        def _():
            vec_ref[...] = jnp.zeros_like(vec_ref)
            if n:
                sends, _, mine = _chip_copies(ins, outs, send_sems, recv_sems, local_sems)
                for cp in sends + mine:
                    cp.start()

        if n:
            @pl.when((i == T // tm - 1) & (jj == N_DEV - 1))
            def _():
                sends, arrivals, mine = _chip_copies(ins, outs, send_sems, recv_sems, local_sems)
                for cp in arrivals:
                    cp.wait_recv()
                for cp in sends:
                    cp.wait_send()
                for cp in mine:
                    cp.wait()

        @pl.when(jj == 0)
        def _():
            acc[...] = jnp.zeros_like(acc)

        acc[...] += _dot_nt(d_ref[...], w_ref[...])

        @pl.when(jj == N_DEV - 1)
        def _():
            dh = acc[...]
            bidx = i // per_seq
            gx_ref[...] = dxd_ref[...] + dh * (1.0 + mod_ref[0, 1:2, :])
            dshift = jnp.sum(dh, axis=0, keepdims=True)
            dscale = jnp.sum(dh * x_ref[...], axis=0, keepdims=True)
            vec_ref[0:1, :] += jnp.where(bidx == 0, dshift, 0.0)
            vec_ref[1:2, :] += jnp.where(bidx == 1, dshift, 0.0)
            vec_ref[2:3, :] += jnp.where(bidx == 0, dscale, 0.0)
            vec_ref[3:4, :] += jnp.where(bidx == 1, dscale, 0.0)

    any_spec = pl.BlockSpec(memory_space=pl.ANY)
    res = pl.pallas_call(
        body, name="dh_dx",
        grid=(T // tm, N_DEV),
        in_specs=[
            pl.BlockSpec((tm, SHARD), lambda i, jj: (i, jj)),
            pl.BlockSpec((None, D, SHARD), lambda i, jj: (jj, 0, 0)),
            pl.BlockSpec((tm, D), lambda i, jj: (i, 0)),
            pl.BlockSpec((tm, D), lambda i, jj: (i, 0)),
            pl.BlockSpec((1, 3, D), lambda i, jj: (i // per_seq, 0, 0))] + [any_spec] * n,
        out_specs=(pl.BlockSpec((tm, D), lambda i, jj: (i, 0)),
                   pl.BlockSpec((8, D), lambda i, jj: (0, 0))) + (any_spec,) * n,
        out_shape=(jax.ShapeDtypeStruct((T, D), F32), jax.ShapeDtypeStruct((8, D), F32))
                  + tuple(jax.ShapeDtypeStruct(a.shape, a.dtype) for a in chip_sums),
        scratch_shapes=[pltpu.VMEM((tm, D), F32), pltpu.SemaphoreType.DMA((max(3 * n, 1),)),
                        pltpu.SemaphoreType.DMA((max(3 * n, 1),)), pltpu.SemaphoreType.DMA((max(n, 1),))],
        compiler_params=pltpu.CompilerParams(vmem_limit_bytes=VMEM_LIMIT),
    )(dproj, w_in_all, x2, dxd, mod3, *chip_sums)
    return res[0], res[1], res[2:]


def _mm_tn(a, b, tn, blocks_leading, name):
    kk, m = a.shape
    n = b.shape[1]
    tk = 2048

    def body(a_ref, b_ref, o_ref, acc):
        @pl.when(pl.program_id(1) == 0)
        def _():
            acc[...] = jnp.zeros_like(acc)

        acc[...] += _dot_tn(a_ref[...], b_ref[...])

        @pl.when(pl.program_id(1) == kk // tk - 1)
        def _():
            o_ref[...] = acc[...].astype(BF16)

    if blocks_leading:
        out_spec = pl.BlockSpec((None, m, tn), lambda j, k: (j, 0, 0))
        out_shape = jax.ShapeDtypeStruct((n // tn, m, tn), BF16)
    else:
        out_spec = pl.BlockSpec((m, tn), lambda j, k: (0, j))
        out_shape = jax.ShapeDtypeStruct((m, n), BF16)
    return pl.pallas_call(
        body, name=name,
        grid=(n // tn, kk // tk),
        in_specs=[pl.BlockSpec((tk, m), lambda j, k: (k, 0)),
                  pl.BlockSpec((tk, tn), lambda j, k: (k, j))],
        out_specs=out_spec, out_shape=out_shape,
        scratch_shapes=[pltpu.VMEM((m, tn), F32)],
        compiler_params=pltpu.CompilerParams(vmem_limit_bytes=VMEM_LIMIT),
    )(a, b)


def _adamw(parts, w, m, v, name, row_tile=None):
    n_parts, rows, cols = parts.shape
    tr = rows if row_tile is None else row_tile
    c1 = 1.0 - ADAM_B1 ** ADAM_STEP
    c2 = 1.0 - ADAM_B2 ** ADAM_STEP

    def body(p_ref, w_ref, m_ref, v_ref, g_ref, d_ref, nm_ref, nv_ref):
        g = p_ref[0].astype(F32)
        for s in range(1, n_parts):
            g = g + p_ref[s].astype(F32)
        nm = ADAM_B1 * m_ref[...] + (1.0 - ADAM_B1) * g
        nv = ADAM_B2 * v_ref[...] + (1.0 - ADAM_B2) * (g * g)
        m_hat = nm / c1
        v_hat = nv / c2
        g_ref[...] = g
        d_ref[...] = -ADAM_LR * (m_hat / (jnp.sqrt(v_hat) + ADAM_EPS) + ADAM_WD * w_ref[...])
        nm_ref[...] = nm
        nv_ref[...] = nv

    blk = pl.BlockSpec((tr, cols), lambda i: (i, 0))
    shp = jax.ShapeDtypeStruct((rows, cols), F32)
    return pl.pallas_call(
        body, name=name,
        grid=(rows // tr,),
        in_specs=[pl.BlockSpec((n_parts, tr, cols), lambda i: (0, i, 0)), blk, blk, blk],
        out_specs=(blk, blk, blk, blk),
        out_shape=(shp, shp, shp, shp),
        compiler_params=pltpu.CompilerParams(vmem_limit_bytes=VMEM_LIMIT),
    )(parts, w, m, v)


def _loss_sum(rows):
    def body(r_ref, o_ref):
        o_ref[...] = jnp.sum(jnp.sum(r_ref[...], axis=0, keepdims=True), axis=1, keepdims=True)

    return pl.pallas_call(body, name="loss_sum", out_shape=jax.ShapeDtypeStruct((1, 1), F32))(rows)


def _local_step(x2, tgt2, mod3, w_in_all, w_ao, w_co, w_o, conv_w, conv_b, rel_bias, ln_g, ln_b):
    buckets_np, masks_np = _bucket_maps()
    buckets, masks = jnp.asarray(buckets_np), jnp.asarray(masks_np)

    h = _prep_h(x2, mod3)
    proj = _proj(h, w_in_all)
    bias = _bias_expand(rel_bias, buckets, masks)
    fwd = [_attn_fwd(proj, bias, g) for g in range(3)]
    o_g = [f[0] for f in fwd]
    lse_g = [f[1] for f in fwd]

    (dproj, dyc, d_o, stats, dxd, merged, dy, a_in, da_out, s_in, ds_out, tail_vec) = _tail(
        x2, tgt2, mod3, o_g, lse_g, proj, w_ao, w_co, w_o, conv_w, conv_b, ln_g, ln_b)

    dbias = []
    for g in range(3):
        dproj, db = _attn_bwd(proj, d_o, stats, bias, dproj, g)
        dbias.append(db)
    g_rel_bias = _bias_grad(*dbias, buckets)
    dproj, conv_vec = _conv_bwd(dyc, proj, conv_w, dproj)

    gw_in = _mm_tn(h, dproj, SHARD, True, "gw_in")
    gw_o = _mm_tn(merged, dy, D, False, "gw_o")
    gw_co = _mm_tn(s_in, ds_out, D, False, "gw_conv_out")
    gw_ao = _mm_tn(a_in, da_out, D, False, "gw_attn_out")
    gw_ao = jnp.transpose(gw_ao.reshape(GW, N_DEV, D // N_DEV), (1, 0, 2))
    return dproj, dxd, gw_in, gw_ao, gw_co, gw_o, conv_vec, g_rel_bias, tail_vec


def kernel(x, c, w_ada, b_ada, w_in, conv_w, conv_b, rel_bias, w_attn_out, w_conv_out, w_o, ln_g, ln_b, loss_target, m_w_ada, m_b_ada, m_w_in, m_conv_w, m_conv_b, m_rel_bias, m_w_attn_out, m_w_conv_out, m_w_o, m_ln_g, m_ln_b, v_w_ada, v_b_ada, v_w_in, v_conv_w, v_conv_b, v_rel_bias, v_w_attn_out, v_w_conv_out, v_w_o, v_ln_g, v_ln_b):
    me = _my_index()
    x2 = x.reshape(T, D)
    tgt2 = loss_target.reshape(T, D)

    w_in_all, w_ao_g, w_co_g, w_o_g, conv_w_g, c_g = _all_gather(
        [w_in[0].astype(BF16), w_attn_out[0].astype(BF16), w_conv_out[0].astype(BF16),
         w_o[0].astype(BF16), conv_w[0], c],
        [jax.ShapeDtypeStruct((N_DEV, D, SHARD), BF16),
         jax.ShapeDtypeStruct((N_DEV, GW, D // N_DEV), BF16),
         jax.ShapeDtypeStruct((N_DEV, D // N_DEV, D), BF16),
         jax.ShapeDtypeStruct((N_DEV, D // N_DEV, D), BF16),
         jax.ShapeDtypeStruct((N_DEV, 3, D // N_DEV), F32),
         jax.ShapeDtypeStruct((N_DEV, BL, D), F32)],
        [_slot_leading] * 6,
        "gather_weights")
    w_ao_full = jnp.transpose(w_ao_g, (1, 0, 2)).reshape(GW, D)
    w_co_full = w_co_g.reshape(D, D)
    w_o_full = w_o_g.reshape(D, D)
    conv_w_full = jnp.transpose(conv_w_g, (1, 0, 2)).reshape(3, D)
    c_all = c_g.reshape(N_DEV * BL, D)

    b_cols = lax.dynamic_slice(b_ada, (0, me * ADA_SHARD), (1, ADA_SHARD))
    mod_cols = _ada_fwd(c_all, w_ada[0], b_cols)
    (mod_g,) = _all_gather([mod_cols], [jax.ShapeDtypeStruct((N_DEV, N_DEV * BL, ADA_SHARD), F32)],
                           [_slot_leading], "gather_mod")
    mod_all = jnp.transpose(mod_g, (1, 0, 2)).reshape(N_DEV * BL, 3 * D)
    mod3 = lax.dynamic_slice(mod_all, (me * BL, 0), (BL, 3 * D)).reshape(BL, 3, D)

    (dproj, dxd, gw_in, gw_ao, gw_co, gw_o, conv_vec, g_rel_bias, tail_vec) = _local_step(
        x2, tgt2, mod3, w_in_all, w_ao_full, w_co_full, w_o_full,
        conv_w_full, conv_b, rel_bias, ln_g, ln_b)

    g_conv_w_blocks = jnp.transpose(conv_vec[0:3].reshape(3, N_DEV, D // N_DEV), (1, 0, 2))
    partials = [gw_in, gw_ao, gw_co.reshape(N_DEV, D // N_DEV, D), gw_o.reshape(N_DEV, D // N_DEV, D),
                g_conv_w_blocks]
    sib = _pair_exchange(
        partials,
        [jax.ShapeDtypeStruct((4, D, SHARD), BF16),
         jax.ShapeDtypeStruct((4, GW, D // N_DEV), BF16),
         jax.ShapeDtypeStruct((4, D // N_DEV, D), BF16),
         jax.ShapeDtypeStruct((4, D // N_DEV, D), BF16),
         jax.ShapeDtypeStruct((4, 3, D // N_DEV), F32)],
        [_slot_leading] * 5,
        "pair_grads")
    tiles = [256, None, None, None, None]
    names = ["w_in", "w_attn_out", "w_conv_out", "w_o", "conv_w"]
    core = lax.axis_index("c").astype(jnp.int32).reshape(1)
    chip_sums = [_pair_add(core, partials[a], sib[a], tiles[a], "pair_add_" + names[a]) for a in range(5)]
    grad_x, mod_vec, (r_in, r_ao, r_co, r_o, r_cw) = _dh_dx(dproj, w_in_all, x2, dxd, mod3, chip_sums)

    small = jnp.concatenate([
        tail_vec[0:4],
        jnp.pad(g_rel_bias.reshape(1, N_BUCKETS * N_HEADS), ((0, 0), (0, D - N_BUCKETS * N_HEADS))),
        jnp.zeros((3, D), F32)], axis=0)
    dmod = jnp.concatenate([mod_vec[0:2], mod_vec[2:4], tail_vec[4:6]], axis=1)
    small_g, dmod_g = _all_gather(
        [small, dmod],
        [jax.ShapeDtypeStruct((N_DEV, 8, D), F32), jax.ShapeDtypeStruct((N_DEV, BL, 3 * D), F32)],
        [_slot_leading] * 2, "gather_small")
    dmod_all = dmod_g.reshape(N_DEV * BL, 3 * D)
    loss = _loss_sum(small_g[:, 3, :]).reshape(())
    g_w_ada = _ada_bwd(jnp.transpose(c_all), lax.dynamic_slice(dmod_all, (0, me * ADA_SHARD),
                                                               (N_DEV * BL, ADA_SHARD)))

    def upd(parts, w, m, v, name, row_tile=None):
        shape = w.shape
        w2, m2, v2 = (t.reshape(parts.shape[1:]) for t in (w, m, v))
        return tuple(t.reshape(shape) for t in _adamw(parts, w2, m2, v2, name, row_tile))

    res = {
        "w_ada": upd(g_w_ada[None], w_ada, m_w_ada, v_w_ada, "adam_w_ada", 256),
        "b_ada": upd(dmod_all[:, None, :], b_ada, m_b_ada, v_b_ada, "adam_b_ada"),
        "w_in": upd(r_in, w_in, m_w_in, v_w_in, "adam_w_in", 128),
        "conv_w": upd(r_cw, conv_w, m_conv_w, v_conv_w, "adam_conv_w"),
        "conv_b": upd(small_g[:, 0:1, :], conv_b, m_conv_b, v_conv_b, "adam_conv_b"),
        "rel_bias": upd(small_g[:, 4, :N_BUCKETS * N_HEADS].reshape(N_DEV, N_BUCKETS, N_HEADS),
                        rel_bias, m_rel_bias, v_rel_bias, "adam_rel_bias"),
        "w_attn_out": upd(r_ao, w_attn_out, m_w_attn_out, v_w_attn_out, "adam_w_attn_out"),
        "w_conv_out": upd(r_co, w_conv_out, m_w_conv_out, v_w_conv_out, "adam_w_conv_out"),
        "w_o": upd(r_o, w_o, m_w_o, v_w_o, "adam_w_o"),
        "ln_g": upd(small_g[:, 1:2, :], ln_g, m_ln_g, v_ln_g, "adam_ln_g"),
        "ln_b": upd(small_g[:, 2:3, :], ln_b, m_ln_b, v_ln_b, "adam_ln_b"),
    }
    order = ["w_ada", "b_ada", "w_in", "conv_w", "conv_b", "rel_bias", "w_attn_out", "w_conv_out",
             "w_o", "ln_g", "ln_b"]
    outs = [loss, grad_x.reshape(BL, S, D)]
    for k in range(4):
        outs += [res[name][k] for name in order]
    return tuple(outs)
```

```python
import functools
import math

import numpy as np
import jax
import jax.numpy as jnp
from jax import lax
from jax.experimental import pallas as pl
from jax.experimental.pallas import tpu as pltpu

F32 = jnp.float32
BF16 = jnp.bfloat16
MESH = pl.DeviceIdType.MESH

N_DEV = 8
D = 1024
S = 2048
BL = 2
T = BL * S
NCOL = 11264
SHARD = NCOL // N_DEV
CB = 512
NCB = NCOL // CB
HD = 128
GW = 512
QB = 128
DILATIONS = (1, 4, 16)
N_STEPS = 128
N_BUCKETS = 32
N_HEADS = 12
ALPHA = 2.0 ** 0.25
LN_EPS = 1e-5
NEG_INF = -1e30
SCALE = HD ** -0.5
ADA_SHARD = 3 * D // N_DEV

CB_Q, CB_K, CB_V, CB_GA = 0, 3, 6, 9
KB_U, KB_BG, KB_CG, KB_GC, KB_MA, KB_MC = 5, 6, 7, 8, 9, 10

ADAM_LR, ADAM_B1, ADAM_B2, ADAM_EPS, ADAM_WD, ADAM_STEP = 0.001, 0.9, 0.999, 1e-08, 0.01, 10

VMEM_LIMIT = 56 * 1024 * 1024


def _dot(a, b):
    return jnp.dot(a, b, preferred_element_type=F32)


def _dot_nt(a, b):
    return lax.dot_general(a, b, (((1,), (1,)), ((), ())), preferred_element_type=F32)


def _dot_tn(a, b):
    return lax.dot_general(a, b, (((0,), (0,)), ((), ())), preferred_element_type=F32)


def _sigmoid(v):
    return 1.0 / (1.0 + jnp.exp(-v))


def _write_columns(pieces, dst_hbm, row0, sems):
    copies = []
    for k, (src, col0) in enumerate(pieces):
        rows, width = src.shape
        copies.append(pltpu.make_async_copy(
            src, dst_hbm.at[pl.ds(row0, rows), pl.ds(col0, width)], sems.at[k]))
    for cp in copies:
        cp.start()
    for cp in copies:
        cp.wait()


def _my_index():
    return 4 * lax.axis_index("x") + 2 * lax.axis_index("y") + lax.axis_index("c")


def _slot_leading(ref, slot):
    return ref.at[slot]


def _all_gather(arrs, out_shapes, slot_fns, name):
    n = len(arrs)

    def body(*refs):
        ins, outs = refs[:n], refs[n:2 * n]
        send_sems, recv_sems, local_sems = refs[2 * n:2 * n + 3]
        stage = refs[2 * n + 3:]
        x, y, c = lax.axis_index("x"), lax.axis_index("y"), lax.axis_index("c")
        me, sibling = (x, y, c), (x, y, 1 - c)
        chips = [(1 - x, y), (x, 1 - y), (1 - x, 1 - y)]

        def blk(a, dev):
            return slot_fns[a](outs[a], 4 * dev[0] + 2 * dev[1] + dev[2])

        def copy(a, k, block, to, src=None):
            dst = blk(a, block)
            return pltpu.make_async_remote_copy(
                src_ref=dst if src is None else src, dst_ref=dst,
                send_sem=send_sems.at[a * 7 + k], recv_sem=recv_sems.at[a * 7 + k],
                device_id=to, device_id_type=MESH)

        first = []
        for a in range(n):
            first.append(copy(a, 0, me, sibling, src=ins[a]))
            first += [copy(a, 1 + j, me, (*chip, c), src=ins[a]) for j, chip in enumerate(chips)]
        for cp in first:
            cp.start()
        loads = [pltpu.make_async_copy(ins[a], stage[a], local_sems.at[a]) for a in range(n)]
        for cp in loads:
            cp.start()
        for cp in loads:
            cp.wait()
        mine = [pltpu.make_async_copy(stage[a], blk(a, me), local_sems.at[a]) for a in range(n)]
        for cp in mine:
            cp.start()
        passed = []
        for j, chip in enumerate(chips):
            for a in range(n):
                copy(a, 1 + j, (*chip, c), me).wait_recv()
                fwd = copy(a, 4 + j, (*chip, c), sibling)
                fwd.start()
                passed.append(fwd)
        for a in range(n):
            copy(a, 0, sibling, me).wait_recv()
        for j, chip in enumerate(chips):
            for a in range(n):
                copy(a, 4 + j, (*chip, 1 - c), me).wait_recv()
        for cp in first + passed:
            cp.wait_send()
        for cp in mine:
            cp.wait()

    any_spec = pl.BlockSpec(memory_space=pl.ANY)
    return pl.pallas_call(
        body, name=name,
        out_shape=tuple(out_shapes),
        in_specs=[any_spec] * n,
        out_specs=tuple([any_spec] * n),
        scratch_shapes=[pltpu.SemaphoreType.DMA((7 * n,)), pltpu.SemaphoreType.DMA((7 * n,)),
                        pltpu.SemaphoreType.DMA((n,))]
                       + [pltpu.VMEM(a.shape, a.dtype) for a in arrs],
    )(*arrs)


def _pair_exchange(arrs, shapes4, src_fns, name):
    n = len(arrs)

    def body(*refs):
        ins, recv = refs[:n], refs[n:2 * n]
        send_sems, recv_sems = refs[2 * n:]
        x, y, c = lax.axis_index("x"), lax.axis_index("y"), lax.axis_index("c")
        sibling = (x, y, 1 - c)
        remote = []
        for a in range(n):
            for q in range(4):
                remote.append(pltpu.make_async_remote_copy(
                    src_ref=src_fns[a](ins[a], 2 * q + 1 - c), dst_ref=recv[a].at[q],
                    send_sem=send_sems.at[a * 4 + q], recv_sem=recv_sems.at[a * 4 + q],
                    device_id=sibling, device_id_type=MESH))
        for cp in remote:
            cp.start()
        for cp in remote:
            cp.wait()

    any_spec = pl.BlockSpec(memory_space=pl.ANY)
    return pl.pallas_call(
        body, name=name,
        out_shape=tuple(shapes4),
        in_specs=[any_spec] * n,
        out_specs=tuple([any_spec] * n),
        scratch_shapes=[pltpu.SemaphoreType.DMA((4 * n,))] * 2,
    )(*arrs)


def _chip_copies(ins, outs, send_sems, recv_sems, local_sems):
    n = len(ins)
    x, y, c = lax.axis_index("x"), lax.axis_index("y"), lax.axis_index("c")
    my_chip = 2 * x + y

    def peer_of(k):
        return ((1 - x) if (k >> 1) & 1 else x, (1 - y) if k & 1 else y, c)

    def copy(a, k, out_chip):
        peer = peer_of(k)
        return pltpu.make_async_remote_copy(
            src_ref=ins[a].at[2 * peer[0] + peer[1]], dst_ref=outs[a].at[out_chip],
            send_sem=send_sems.at[a * 3 + k - 1], recv_sem=recv_sems.at[a * 3 + k - 1],
            device_id=peer, device_id_type=MESH)

    sends = [copy(a, k, my_chip) for k in range(1, 4) for a in range(n)]
    arrivals = []
    for k in range(1, 4):
        peer = peer_of(k)
        arrivals += [copy(a, k, 2 * peer[0] + peer[1]) for a in range(n)]
    mine = [pltpu.make_async_copy(ins[a].at[my_chip], outs[a].at[my_chip], local_sems.at[a])
            for a in range(n)]
    return sends, arrivals, mine


def _pair_add(core, mine, theirs, row_tile, name):
    _, rows, cols = theirs.shape
    tr = rows if row_tile is None else row_tile

    def body(core_ref, a_ref, b_ref, o_ref):
        o_ref[...] = (a_ref[...].astype(F32) + b_ref[...].astype(F32)).astype(o_ref.dtype)

    blk = pl.BlockSpec((None, tr, cols), lambda q, i, core_ref: (q, i, 0))
    return pl.pallas_call(
        body, name=name,
        grid_spec=pltpu.PrefetchScalarGridSpec(
            num_scalar_prefetch=1,
            grid=(4, rows // tr),
            in_specs=[pl.BlockSpec((None, tr, cols), lambda q, i, core_ref: (2 * q + core_ref[0], i, 0)), blk],
            out_specs=blk),
        out_shape=jax.ShapeDtypeStruct(theirs.shape, theirs.dtype),
    )(core, mine, theirs)


def _ada_fwd(c_all, w_ada, b_cols):
    def body(c_ref, w_ref, b_ref, o_ref):
        cv = c_ref[...]
        sc = cv * _sigmoid(cv)
        o_ref[...] = jnp.dot(sc, w_ref[...], preferred_element_type=F32,
                             precision=lax.Precision.HIGHEST) + b_ref[...]

    return pl.pallas_call(
        body, name="ada_fwd",
        out_shape=jax.ShapeDtypeStruct((c_all.shape[0], w_ada.shape[1]), F32),
    )(c_all, w_ada, b_cols)


def _ada_bwd(c_all_t, dmod_cols):
    def body(c_ref, d_ref, o_ref):
        cv = c_ref[...]
        sc = cv * _sigmoid(cv)
        o_ref[...] = jnp.dot(sc, d_ref[...], preferred_element_type=F32,
                             precision=lax.Precision.HIGHEST)

    return pl.pallas_call(
        body, name="ada_bwd",
        out_shape=jax.ShapeDtypeStruct((c_all_t.shape[0], dmod_cols.shape[1]), F32),
    )(c_all_t, dmod_cols)


def _prep_h(x2, mod3):
    ts = 512
    per_seq = S // ts

    def body(x_ref, mod_ref, h_ref):
        shift = mod_ref[0, 0:1, :]
        scale = mod_ref[0, 1:2, :]
        h_ref[...] = (x_ref[...] * (1.0 + scale) + shift).astype(BF16)

    return pl.pallas_call(
        body, name="prep_h",
        grid=(T // ts,),
        in_specs=[pl.BlockSpec((ts, D), lambda i: (i, 0)),
                  pl.BlockSpec((1, 3, D), lambda i: (i // per_seq, 0, 0))],
        out_specs=pl.BlockSpec((ts, D), lambda i: (i, 0)),
        out_shape=jax.ShapeDtypeStruct((T, D), BF16),
    )(x2, mod3)


def _shard_order():
    x, y, c = lax.axis_index("x"), lax.axis_index("y"), lax.axis_index("c")
    devs = [(x, y, c), (x, y, 1 - c)]
    for chip in [(1 - x, y), (x, 1 - y), (1 - x, 1 - y)]:
        devs += [(*chip, c), (*chip, 1 - c)]
    return jnp.stack([4 * d[0] + 2 * d[1] + d[2] for d in devs]).astype(jnp.int32)


def _gather_proj(order, h, w_shard, tm):
    rows, kdim = h.shape
    ncols = w_shard.shape[1]
    n_i = rows // tm

    def body(order_ref, h_ref, mine_hbm, o_ref, all_hbm, wv, send_sems, recv_sems, local_sems):
        j, i = pl.program_id(0), pl.program_id(1)
        x, y, c = lax.axis_index("x"), lax.axis_index("y"), lax.axis_index("c")
        me, sibling = (x, y, c), (x, y, 1 - c)
        chips = [(1 - x, y), (x, 1 - y), (1 - x, 1 - y)]

        def slot(dev):
            return 4 * dev[0] + 2 * dev[1] + dev[2]

        def copy(k, block, to):
            return pltpu.make_async_remote_copy(
                src_ref=wv.at[slot(block)], dst_ref=wv.at[slot(block)],
                send_sem=send_sems.at[k], recv_sem=recv_sems.at[k],
                device_id=to, device_id_type=MESH)

        def keep(step, block):
            return pltpu.make_async_copy(wv.at[slot(block)], all_hbm.at[slot(block)], local_sems.at[step])

        first = [copy(0, me, sibling)] + [copy(1 + q, me, (*chip, c)) for q, chip in enumerate(chips)]
        passed = [copy(4 + q, (*chip, c), sibling) for q, chip in enumerate(chips)]
        due = [(me, None, None), (sibling, copy(0, sibling, me), None)]
        for q, chip in enumerate(chips):
            due.append(((*chip, c), copy(1 + q, (*chip, c), me), passed[q]))
            due.append(((*chip, 1 - c), copy(4 + q, (*chip, 1 - c), me), None))

        @pl.when((j == 0) & (i == 0))
        def _():
            load = pltpu.make_async_copy(mine_hbm, wv.at[slot(me)], local_sems.at[N_DEV])
            load.start()
            load.wait()
            for cp in first:
                cp.start()
            keep(0, me).start()

        for step in range(1, N_DEV):
            block, arrival, forward = due[step]

            @pl.when((j == step) & (i == 0))
            def _():
                arrival.wait_recv()
                if forward is not None:
                    forward.start()
                keep(step, block).start()

        o_ref[...] = _dot(h_ref[...], wv[order_ref[j]]).astype(BF16)

        @pl.when((j == N_DEV - 1) & (i == n_i - 1))
        def _():
            for cp in first + passed:
                cp.wait_send()
            for step in range(N_DEV):
                keep(step, due[step][0]).wait()

    return pl.pallas_call(
        body, name="gather_proj",
        grid_spec=pltpu.PrefetchScalarGridSpec(
            num_scalar_prefetch=1,
            grid=(N_DEV, n_i),
            in_specs=[pl.BlockSpec((tm, kdim), lambda j, i, order_ref: (i, 0)),
                      pl.BlockSpec(memory_space=pl.ANY)],
            out_specs=(pl.BlockSpec((tm, ncols), lambda j, i, order_ref: (i, order_ref[j])),
                       pl.BlockSpec(memory_space=pl.ANY)),
            scratch_shapes=[pltpu.VMEM((N_DEV, kdim, ncols), BF16),
                            pltpu.SemaphoreType.DMA((7,)), pltpu.SemaphoreType.DMA((7,)),
                            pltpu.SemaphoreType.DMA((N_DEV + 1,))]),
        out_shape=(jax.ShapeDtypeStruct((rows, N_DEV * ncols), BF16),
                   jax.ShapeDtypeStruct((N_DEV, kdim, ncols), BF16)),
        compiler_params=pltpu.CompilerParams(vmem_limit_bytes=VMEM_LIMIT),
    )(order, h, w_shard)


def _bucket_maps():
    a = np.arange(QB)[:, None]
    b = np.arange(2 * QB)[None, :]
    steps = a + QB - b
    maps = []
    for dil in DILATIONS:
        dist = np.maximum(steps, 0) * dil
        nf = np.maximum(dist, 1).astype(np.float32)
        large = 16 + (np.log(nf / np.float32(16)) / np.float32(math.log(128.0))
                      * np.float32(16)).astype(np.int32)
        large = np.minimum(large, N_BUCKETS - 1)
        maps.append(np.where(dist < 16, dist, large).astype(np.int32))
    band = (steps >= 0) & (steps <= N_STEPS)
    first = band & (b >= QB)
    masks = np.stack([first, band]).astype(np.int32)
    return np.stack(maps), masks


def _bias_expand(rel_bias, buckets, masks):
    def body(tab_ref, bk_ref, mk_ref, o_ref):
        for g in range(3):
            bk = bk_ref[g]
            for h in range(4):
                col = 4 * g + h
                val = jnp.zeros((QB, 2 * QB), F32)
                for k in range(N_BUCKETS):
                    val = jnp.where(bk == k, tab_ref[k, col], val)
                o_ref[g, 0, h] = jnp.where(mk_ref[0] != 0, val, NEG_INF)
                o_ref[g, 1, h] = jnp.where(mk_ref[1] != 0, val, NEG_INF)

    return pl.pallas_call(
        body, name="bias_expand",
        in_specs=[pl.BlockSpec(memory_space=pltpu.SMEM),
                  pl.BlockSpec(memory_space=pltpu.VMEM),
                  pl.BlockSpec(memory_space=pltpu.VMEM)],
        out_shape=jax.ShapeDtypeStruct((3, 2, 4, QB, 2 * QB), F32),
    )(rel_bias, buckets, masks)


def _bias_grad(ds1, ds2, ds3, buckets):
    def body(d1_ref, d2_ref, d3_ref, bk_ref, o_ref):
        for g, d_ref in enumerate((d1_ref, d2_ref, d3_ref)):
            bk = bk_ref[g]
            for h in range(4):
                dv = d_ref[h]
                for k in range(N_BUCKETS):
                    o_ref[k, 4 * g + h] = jnp.sum(jnp.where(bk == k, dv, 0.0))

    return pl.pallas_call(
        body, name="bias_grad",
        in_specs=[pl.BlockSpec(memory_space=pltpu.VMEM)] * 4,
        out_specs=pl.BlockSpec(memory_space=pltpu.SMEM),
        out_shape=jax.ShapeDtypeStruct((N_BUCKETS, N_HEADS), F32),
    )(ds1, ds2, ds3, buckets)


def _scratch_sets(rows):
    return 4 if rows <= 512 else 1


def _unit_chunks(dil, size=16):
    units = [(h, r) for h in range(4) for r in range(dil)]
    return [units[i:i + size] for i in range(0, len(units), size)]


def _residue_rows(src_ref, copies, h, residue):
    sl = slice(h * HD, (h + 1) * HD)
    if copies is None:
        return lambda r: src_ref[:, sl]
    buf = copies[h % len(copies)]
    buf[...] = src_ref[:, sl].astype(F32)
    return lambda r: buf[residue(r), :].astype(BF16)


def _attn_fwd(proj, bias, g):
    dil = DILATIONS[g]
    rows = QB * dil
    nsb = S // rows
    has_prev = nsb > 1

    def residue(r):
        return pl.ds(r, QB, stride=dil) if dil > 1 else pl.ds(0, QB)

    strided = dil > 1
    n_sets = _scratch_sets(rows)
    n_in = 6 if has_prev else 4

    def body(*refs):
        q_ref, kc_ref, vc_ref = refs[:3]
        kp_ref, vp_ref = refs[3:5] if has_prev else (None, None)
        b_ref, o_ref, l_ref = refs[n_in - 1:n_in + 2]
        scr = list(refs[n_in + 2:])
        ls = [scr.pop(0) for _ in range(4)]
        copies = {name: [scr.pop(0) for _ in range(n_sets)] if strided else None
                  for name in ("q", "kc", "vc", "o") + (("kp", "vp") if has_prev else ())}
        lane = lax.broadcasted_iota(jnp.int32, (QB, 128), 1)
        refs_of = {"q": q_ref, "kc": kc_ref, "vc": vc_ref, "kp": kp_ref, "vp": vp_ref}
        for chunk in _unit_chunks(dil):
            rows_of = {h: {name: _residue_rows(refs_of[name], copies[name], h, residue)
                           for name in refs_of if refs_of[name] is not None}
                       for h in sorted({h for h, _ in chunk})}

            def batch(name):
                return jnp.stack([rows_of[h][name](r) for h, r in chunk])

            q, k, v = batch("q"), batch("kc"), batch("vc")
            if has_prev:
                k = jnp.concatenate([batch("kp"), k], axis=1)
                v = jnp.concatenate([batch("vp"), v], axis=1)
                bias_b = jnp.stack([b_ref[h] for h, _ in chunk])
            else:
                bias_b = jnp.stack([b_ref[h, :, QB:] for h, _ in chunk])
            s = jnp.einsum("uqd,ukd->uqk", q, k, preferred_element_type=F32) * SCALE + bias_b
            m = jnp.max(s, axis=-1, keepdims=True)
            p = jnp.exp(s - m)
            l = jnp.sum(p, axis=-1, keepdims=True)
            o = jnp.einsum("uqk,ukd->uqd", p.astype(BF16), v, preferred_element_type=F32) / l
            lse = m + jnp.log(l)
            for i, (h, r) in enumerate(chunk):
                if strided:
                    copies["o"][h % n_sets][residue(r), :] = o[i]
                else:
                    o_ref[:, h * HD:(h + 1) * HD] = o[i]
                ls[h][r * QB:(r + 1) * QB, :] = jnp.where(lane == h, lse[i], 0.0)
            if strided:
                for h in sorted({h for h, _ in chunk}):
                    o_ref[:, h * HD:(h + 1) * HD] = copies["o"][h % n_sets][...]
        for r in range(dil):
            blk = slice(r * QB, (r + 1) * QB)
            l_ref[residue(r), :] = (ls[0][blk, :] + ls[1][blk, :]) + (ls[2][blk, :] + ls[3][blk, :])

    def row(b, n):
        return b * nsb + n

    def prev(b, n):
        return b * nsb + jnp.maximum(n - 1, 0)

    in_specs = [
        pl.BlockSpec((rows, GW), lambda b, n: (row(b, n), CB_Q + g)),
        pl.BlockSpec((rows, GW), lambda b, n: (row(b, n), CB_K + g)),
        pl.BlockSpec((rows, GW), lambda b, n: (row(b, n), CB_V + g)),
    ]
    args = [proj, proj, proj]
    n_copied = (4 + (2 if has_prev else 0)) * (n_sets if strided else 0)
    scratch = [pltpu.VMEM((rows, 128), F32)] * (4 + n_copied)
    if has_prev:
        in_specs += [pl.BlockSpec((rows, GW), lambda b, n: (prev(b, n), CB_K + g)),
                     pl.BlockSpec((rows, GW), lambda b, n: (prev(b, n), CB_V + g))]
        args += [proj, proj]
    in_specs.append(pl.BlockSpec((None, None, 4, QB, 2 * QB),
                                 lambda b, n: (g, jnp.minimum(n, 1), 0, 0, 0)))
    args.append(bias)
    return pl.pallas_call(
        body, name=f"attn_fwd{g}",
        grid=(BL, nsb),
        in_specs=in_specs,
        out_specs=(pl.BlockSpec((rows, GW), lambda b, n: (row(b, n), 0)),
                   pl.BlockSpec((rows, 128), lambda b, n: (row(b, n), 0))),
        out_shape=(jax.ShapeDtypeStruct((T, GW), F32), jax.ShapeDtypeStruct((T, 128), F32)),
        scratch_shapes=scratch,
        compiler_params=pltpu.CompilerParams(vmem_limit_bytes=VMEM_LIMIT),
    )(*args)


def _attn_bwd(proj, d_out, stats, bias, dproj, g):
    dil = DILATIONS[g]
    rows = QB * dil
    nsb = S // rows
    has_prev = nsb > 1
    n_steps = nsb + 1 if has_prev else 1
    n_in = 7 + (2 if has_prev else 0)

    def residue(r):
        return pl.ds(r, QB, stride=dil) if dil > 1 else pl.ds(0, QB)

    strided = dil > 1
    n_sets = _scratch_sets(rows)

    def body(*refs):
        q_ref, kc_ref, vc_ref, do_ref, st_ref, b_ref = refs[:6]
        kp_ref, vp_ref = refs[6:8] if has_prev else (None, None)
        out_ref, db_ref = refs[n_in], refs[n_in + 1]
        scr = list(refs[n_in + 2:])
        sq, sk, sv, sems = [scr.pop(0) for _ in range(4)]
        carry = scr.pop(0) if has_prev else None
        sts = scr.pop(0) if strided else st_ref
        copies = {name: [scr.pop(0) for _ in range(n_sets)] if strided else None
                  for name in ("q", "kc", "vc", "do", "dq", "dk", "dv") + (("kp", "vp") if has_prev else ())}
        b, n = pl.program_id(0), pl.program_id(1)

        @pl.when((b == 0) & (n == 0))
        def _():
            db_ref[...] = jnp.zeros_like(db_ref)

        def finish(h, r, dq, dk, dv):
            if strided:
                for name, val in (("dq", dq), ("dk", dk), ("dv", dv)):
                    copies[name][h % n_sets][residue(r), :] = val
            else:
                sl = slice(h * HD, (h + 1) * HD)
                sq[:, sl], sk[:, sl], sv[:, sl] = dq.astype(BF16), dk.astype(BF16), dv.astype(BF16)

        def finish_head(h):
            if strided:
                sl = slice(h * HD, (h + 1) * HD)
                sq[:, sl] = copies["dq"][h % n_sets][...].astype(BF16)
                sk[:, sl] = copies["dk"][h % n_sets][...].astype(BF16)
                sv[:, sl] = copies["dv"][h % n_sets][...].astype(BF16)

        def write_block(blk_idx):
            row0 = pl.multiple_of(blk_idx * rows, rows)
            _write_columns([(sq, CB * (CB_Q + g)), (sk, CB * (CB_K + g)), (sv, CB * (CB_V + g))],
                           out_ref, row0, sems)

        def carried(h, r):
            blk = slice(r * QB, (r + 1) * QB)
            return ((blk, slice(h * HD, (h + 1) * HD)), (blk, slice(GW + h * HD, GW + (h + 1) * HD)),
                    (blk, slice(2 * GW + h * HD, 2 * GW + (h + 1) * HD)))

        if has_prev:
            @pl.when(n == 0)
            def _():
                carry[...] = jnp.zeros_like(carry)

            @pl.when(n == nsb)
            def _():
                for h in range(4):
                    for r in range(dil):
                        cq, ck, cv = carried(h, r)
                        finish(h, r, carry[cq], carry[ck], carry[cv])
                    finish_head(h)
                write_block(b * nsb + nsb - 1)

        @pl.when(n < nsb)
        def _():
            if strided:
                for r in range(dil):
                    sts[r * QB:(r + 1) * QB, :] = st_ref[residue(r), :]
            refs_of = {"q": q_ref, "kc": kc_ref, "vc": vc_ref, "do": do_ref, "kp": kp_ref, "vp": vp_ref}
            for chunk in _unit_chunks(dil):
                heads = sorted({h for h, _ in chunk})
                rows_of = {h: {name: _residue_rows(refs_of[name], copies[name], h, residue)
                               for name in refs_of if refs_of[name] is not None}
                           for h in heads}

                def batch(name):
                    return jnp.stack([rows_of[h][name](r) for h, r in chunk])

                q, k, v, do = batch("q"), batch("kc"), batch("vc"), batch("do")
                if has_prev:
                    k = jnp.concatenate([batch("kp"), k], axis=1)
                    v = jnp.concatenate([batch("vp"), v], axis=1)
                    bias_b = jnp.stack([b_ref[h] for h, _ in chunk])
                else:
                    bias_b = jnp.stack([b_ref[h, :, QB:] for h, _ in chunk])
                lse = jnp.stack([sts[r * QB:(r + 1) * QB, h:h + 1] for h, r in chunk])
                delta = jnp.stack([sts[r * QB:(r + 1) * QB, 4 + h:5 + h] for h, r in chunk])
                s = jnp.einsum("uqd,ukd->uqk", q, k, preferred_element_type=F32) * SCALE + bias_b
                p = jnp.exp(s - lse)
                ds = p * (jnp.einsum("uqd,ukd->uqk", do, v, preferred_element_type=F32) - delta)
                for h in heads:
                    mine = [ds[i] for i, (hh, _) in enumerate(chunk) if hh == h]
                    tot = mine[0]
                    for extra in mine[1:]:
                        tot = tot + extra
                    if has_prev:
                        db_ref[h] += tot
                    else:
                        db_ref[h, :, QB:] += tot
                dsb, pb = ds.astype(BF16), p.astype(BF16)
                dq = jnp.einsum("uqk,ukd->uqd", dsb, k, preferred_element_type=F32) * SCALE
                dk = jnp.einsum("uqk,uqd->ukd", dsb, q, preferred_element_type=F32) * SCALE
                dv = jnp.einsum("uqk,uqd->ukd", pb, do, preferred_element_type=F32)
                for i, (h, r) in enumerate(chunk):
                    if has_prev:
                        cq, ck, cv = carried(h, r)
                        finish(h, r, carry[cq], carry[ck] + dk[i, :QB], carry[cv] + dv[i, :QB])
                        carry[cq] = dq[i]
                        carry[ck] = dk[i, QB:]
                        carry[cv] = dv[i, QB:]
                    else:
                        finish(h, r, dq[i], dk[i], dv[i])
                for h in heads:
                    finish_head(h)
            if has_prev:
                @pl.when(n > 0)
                def _():
                    write_block(b * nsb + n - 1)
            else:
                write_block(b)

    def row(b, n):
        return b * nsb + jnp.minimum(n, nsb - 1)

    def prev(b, n):
        return b * nsb + jnp.maximum(jnp.minimum(n, nsb - 1) - 1, 0)

    in_specs = [
        pl.BlockSpec((rows, GW), lambda b, n: (row(b, n), CB_Q + g)),
        pl.BlockSpec((rows, GW), lambda b, n: (row(b, n), CB_K + g)),
        pl.BlockSpec((rows, GW), lambda b, n: (row(b, n), CB_V + g)),
        pl.BlockSpec((rows, GW), lambda b, n: (row(b, n), 0)),
        pl.BlockSpec((rows, 128), lambda b, n: (row(b, n), 0)),
        pl.BlockSpec((None, None, 4, QB, 2 * QB),
                     lambda b, n: (g, jnp.minimum(jnp.minimum(n, nsb - 1), 1), 0, 0, 0)),
    ]
    args = [proj, proj, proj, d_out, stats, bias]
    scratch = [pltpu.VMEM((rows, GW), BF16)] * 3 + [pltpu.SemaphoreType.DMA((3,))]
    if has_prev:
        in_specs += [pl.BlockSpec((rows, GW), lambda b, n: (prev(b, n), CB_K + g)),
                     pl.BlockSpec((rows, GW), lambda b, n: (prev(b, n), CB_V + g))]
        args += [proj, proj]
        scratch.append(pltpu.VMEM((rows, 3 * GW), F32))
    if strided:
        n_copied = (7 + (2 if has_prev else 0)) * n_sets
        scratch += [pltpu.VMEM((rows, 128), F32)] * (1 + n_copied)
    in_specs.append(pl.BlockSpec(memory_space=pl.ANY))
    args.append(dproj)
    return pl.pallas_call(
        body, name=f"attn_bwd{g}",
        grid=(BL, n_steps),
        in_specs=in_specs,
        out_specs=(pl.BlockSpec(memory_space=pl.ANY),
                   pl.BlockSpec((4, QB, 2 * QB), lambda b, n: (0, 0, 0))),
        out_shape=(jax.ShapeDtypeStruct((T, NCOL), BF16),
                   jax.ShapeDtypeStruct((4, QB, 2 * QB), F32)),
        scratch_shapes=scratch,
        input_output_aliases={len(args) - 1: 0},
        compiler_params=pltpu.CompilerParams(vmem_limit_bytes=VMEM_LIMIT),
    )(*args)


def _tail(x2, tgt2, mod3, o_g, lse_g, proj, w_ao, w_co, w_o, conv_w, conv_b, ln_g, ln_b):
    tm = 256
    per_seq = S // tm
    halo = 16

    def body(x_ref, t_ref, mod_ref, o1_ref, o2_ref, o3_ref, l1_ref, l2_ref, l3_ref,
             ga_ref, u_ref, bg_ref, cg_ref, gc_ref, ma_ref, mc_ref, up_ref, cp_ref,
             wao_ref, wco_ref, wo_ref, cw_ref, cb_ref, lg_ref, lb_ref,
             dproj_ref, dyc_ref, do_ref, st_ref, dxd_ref,
             mg_ref, dy_ref, ain_ref, dao_ref, sin_ref, dso_ref, vec_ref,
             dga_s, dbg_s, dgm_s, sems):
        i = pl.program_id(0)
        bidx = i // per_seq
        first = (i % per_seq) == 0

        @pl.when(i == 0)
        def _():
            vec_ref[...] = jnp.zeros_like(vec_ref)

        l1, l2, l3 = l1_ref[...], l2_ref[...], l3_ref[...]
        mx = jnp.maximum(jnp.maximum(l1, l2), l3)
        e1, e2, e3 = jnp.exp(l1 - mx), jnp.exp(l2 - mx), jnp.exp(l3 - mx)
        esum = e1 + e2 + e3
        lse_tot = mx + jnp.log(esum)
        w1, w2, w3 = e1 / esum, e2 / esum, e3 / esum

        def per_head(wv):
            return jnp.concatenate([jnp.broadcast_to(wv[:, h:h + 1], (tm, HD)) for h in range(4)], axis=1)

        o = per_head(w1) * o1_ref[...] + per_head(w2) * o2_ref[...] + per_head(w3) * o3_ref[...]

        ga = ga_ref[...].astype(F32)
        sig_ga = _sigmoid(ga)
        silu_ga = ga * sig_ga
        a_in = (o * silu_ga).astype(BF16)
        a_out = _dot(a_in, wao_ref[...])

        u = u_ref[...].astype(F32)
        cg = cg_ref[...].astype(F32)
        z = cg * u
        zp = cp_ref[...].astype(F32) * up_ref[...].astype(F32)
        zp = jnp.where(first, 0.0, zp)
        zcat = jnp.concatenate([zp, z], axis=0)
        z1 = pltpu.roll(zcat, 1, 0)[halo:]
        z2 = pltpu.roll(zcat, 2, 0)[halo:]
        y_conv = cw_ref[0:1, :] * z2 + cw_ref[1:2, :] * z1 + cw_ref[2:3, :] * z + cb_ref[...]
        gc = gc_ref[...].astype(F32)
        sig_gc = _sigmoid(gc)
        silu_gc = gc * sig_gc
        bg = bg_ref[...].astype(F32)
        s_in = (bg * y_conv * silu_gc).astype(BF16)
        s_out = _dot(s_in, wco_ref[...])

        sa = _sigmoid(ma_ref[...].astype(F32))
        sc = _sigmoid(mc_ref[...].astype(F32))
        merged = (sa * a_out + sc * s_out).astype(BF16)
        y = _dot(merged, wo_ref[...])
        gate1 = 1.0 + mod_ref[0, 2:3, :]
        xv = x_ref[...]
        resid = ALPHA * xv + gate1 * y
        mu = jnp.mean(resid, axis=1, keepdims=True)
        xc = resid - mu
        var = jnp.mean(xc * xc, axis=1, keepdims=True)
        rstd = lax.rsqrt(var + LN_EPS)
        xhat = xc * rstd
        lg = lg_ref[...]
        err = xhat * lg + lb_ref[...] - t_ref[...]
        vec_ref[3:4, :] += (0.5 / D) * jnp.sum(err * err, axis=0, keepdims=True)

        dout = err * (1.0 / D)
        vec_ref[1:2, :] += jnp.sum(dout * xhat, axis=0, keepdims=True)
        vec_ref[2:3, :] += jnp.sum(dout, axis=0, keepdims=True)
        dxh = dout * lg
        dres = rstd * (dxh - jnp.mean(dxh, axis=1, keepdims=True)
                       - xhat * jnp.mean(dxh * xhat, axis=1, keepdims=True))
        dxd_ref[...] = ALPHA * dres
        dgate = jnp.sum(dres * y, axis=0, keepdims=True)
        vec_ref[4:5, :] += jnp.where(bidx == 0, dgate, 0.0)
        vec_ref[5:6, :] += jnp.where(bidx == 1, dgate, 0.0)
        dy = (dres * gate1).astype(BF16)

        dmerged = _dot_nt(dy, wo_ref[...])
        da_out = (dmerged * sa).astype(BF16)
        ds_out = (dmerged * sc).astype(BF16)
        dgm_s[:, 2 * D:3 * D] =(dmerged * s_out * sc * (1.0 - sc)).astype(BF16)
        dgm_s[:, D:2 * D] =(dmerged * a_out * sa * (1.0 - sa)).astype(BF16)
        da_in = _dot_nt(da_out, wao_ref[...])
        ds_in = _dot_nt(ds_out, wco_ref[...])

        d_o = da_in * silu_ga
        do_ref[...] = d_o.astype(BF16)
        dga_s[...] =(da_in * o * (sig_ga * (1.0 + ga * (1.0 - sig_ga)))).astype(BF16)
        lane = lax.broadcasted_iota(jnp.int32, (tm, 128), 1)
        stats = lse_tot
        od = o * d_o
        for h in range(4):
            delta = jnp.sum(od[:, h * HD:(h + 1) * HD], axis=1, keepdims=True)
            stats = jnp.where(lane == 4 + h, delta, stats)
        st_ref[...] = stats

        dbg_s[...] =(ds_in * y_conv * silu_gc).astype(BF16)
        dyc = ds_in * bg * silu_gc
        dyc_ref[...] = dyc
        vec_ref[0:1, :] += jnp.sum(dyc, axis=0, keepdims=True)
        dgm_s[:, 0:D] =(ds_in * bg * y_conv * (sig_gc * (1.0 + gc * (1.0 - sig_gc)))).astype(BF16)

        mg_ref[...] = merged
        dy_ref[...] = dy
        ain_ref[...] = a_in
        dao_ref[...] = da_out
        sin_ref[...] = s_in
        dso_ref[...] = ds_out
        _write_columns([(dga_s, CB * CB_GA), (dbg_s, D * KB_BG), (dgm_s, D * KB_GC)],
                       dproj_ref, pl.multiple_of(i * tm, tm), sems)

    def tile(width, cblk=0):
        return pl.BlockSpec((tm, width), lambda i: (i, cblk))

    def whole(shape):
        return pl.BlockSpec(shape, lambda i: tuple(0 for _ in shape))

    prev_rows = lambda i: (jnp.maximum(i * (tm // halo) - 1, 0),)
    in_specs = [
        tile(D), tile(D), pl.BlockSpec((1, 3, D), lambda i: (i // per_seq, 0, 0)),
        tile(GW), tile(GW), tile(GW), tile(128), tile(128), tile(128),
        tile(GW, CB_GA), tile(D, KB_U), tile(D, KB_BG), tile(D, KB_CG), tile(D, KB_GC),
        tile(D, KB_MA), tile(D, KB_MC),
        pl.BlockSpec((halo, D), lambda i: (*prev_rows(i), KB_U)),
        pl.BlockSpec((halo, D), lambda i: (*prev_rows(i), KB_CG)),
        whole((GW, D)), whole((D, D)), whole((D, D)),
        whole((3, D)), whole((1, D)), whole((1, D)), whole((1, D)),
    ]
    out_specs = (
        pl.BlockSpec(memory_space=pl.ANY), tile(D), tile(GW), tile(128), tile(D),
        tile(D), tile(D), tile(GW), tile(D), tile(D), tile(D),
        pl.BlockSpec((8, D), lambda i: (0, 0)),
    )
    out_shape = (
        jax.ShapeDtypeStruct((T, NCOL), BF16),
        jax.ShapeDtypeStruct((T, D), F32),
        jax.ShapeDtypeStruct((T, GW), BF16),
        jax.ShapeDtypeStruct((T, 128), F32),
        jax.ShapeDtypeStruct((T, D), F32),
        jax.ShapeDtypeStruct((T, D), BF16),
        jax.ShapeDtypeStruct((T, D), BF16),
        jax.ShapeDtypeStruct((T, GW), BF16),
        jax.ShapeDtypeStruct((T, D), BF16),
        jax.ShapeDtypeStruct((T, D), BF16),
        jax.ShapeDtypeStruct((T, D), BF16),
        jax.ShapeDtypeStruct((8, D), F32),
    )
    return pl.pallas_call(
        body, name="tail",
        grid=(T // tm,),
        in_specs=in_specs, out_specs=out_specs, out_shape=out_shape,
        scratch_shapes=[pltpu.VMEM((tm, GW), BF16), pltpu.VMEM((tm, D), BF16), pltpu.VMEM((tm, 3 * D), BF16),
                        pltpu.SemaphoreType.DMA((3,))],
        compiler_params=pltpu.CompilerParams(vmem_limit_bytes=VMEM_LIMIT),
    )(x2, tgt2, mod3, *o_g, *lse_g, proj, proj, proj, proj, proj, proj, proj, proj, proj,
      w_ao, w_co, w_o, conv_w, conv_b, ln_g, ln_b)


def _conv_bwd(dyc, proj, conv_w, dproj):
    tm = 512
    per_seq = S // tm
    halo = 16

    def body(d_ref, dn_ref, u_ref, c_ref, up_ref, cp_ref, cw_ref, _, dproj_ref, g_ref, du_s, dc_s, sems):
        i = pl.program_id(0)
        first = (i % per_seq) == 0
        last = (i % per_seq) == per_seq - 1

        @pl.when(i == 0)
        def _():
            g_ref[...] = jnp.zeros_like(g_ref)

        d = d_ref[...]
        dn = jnp.where(last, 0.0, dn_ref[...])
        dcat = jnp.concatenate([d, dn], axis=0)
        d1 = pltpu.roll(dcat, tm + 8 - 1, 0)[:tm]
        d2 = pltpu.roll(dcat, tm + 8 - 2, 0)[:tm]
        dz = cw_ref[2:3, :] * d + cw_ref[1:2, :] * d1 + cw_ref[0:1, :] * d2
        u = u_ref[...].astype(F32)
        cg = c_ref[...].astype(F32)
        du_s[...] = (dz * cg).astype(BF16)
        dc_s[...] = (dz * u).astype(BF16)
        _write_columns([(du_s, D * KB_U), (dc_s, D * KB_CG)], dproj_ref, pl.multiple_of(i * tm, tm), sems)

        z = cg * u
        zp = jnp.where(first, 0.0, cp_ref[...].astype(F32) * up_ref[...].astype(F32))
        zcat = jnp.concatenate([zp, z], axis=0)
        z1 = pltpu.roll(zcat, 1, 0)[halo:]
        z2 = pltpu.roll(zcat, 2, 0)[halo:]
        g_ref[0:1, :] += jnp.sum(d * z2, axis=0, keepdims=True)
        g_ref[1:2, :] += jnp.sum(d * z1, axis=0, keepdims=True)
        g_ref[2:3, :] += jnp.sum(d * z, axis=0, keepdims=True)

    n_tiles = T // tm
    prev_rows = lambda i: jnp.maximum(i * (tm // halo) - 1, 0)
    next_rows = lambda i: jnp.minimum((i + 1) * (tm // 8), T // 8 - 1)
    return pl.pallas_call(
        body, name="conv_bwd",
        grid=(n_tiles,),
        in_specs=[pl.BlockSpec((tm, D), lambda i: (i, 0)),
                  pl.BlockSpec((8, D), lambda i: (next_rows(i), 0)),
                  pl.BlockSpec((tm, D), lambda i: (i, KB_U)),
                  pl.BlockSpec((tm, D), lambda i: (i, KB_CG)),
                  pl.BlockSpec((halo, D), lambda i: (prev_rows(i), KB_U)),
                  pl.BlockSpec((halo, D), lambda i: (prev_rows(i), KB_CG)),
                  pl.BlockSpec((3, D), lambda i: (0, 0)),
                  pl.BlockSpec(memory_space=pl.ANY)],
        out_specs=(pl.BlockSpec(memory_space=pl.ANY),
                   pl.BlockSpec((8, D), lambda i: (0, 0))),
        out_shape=(jax.ShapeDtypeStruct((T, NCOL), BF16),
                   jax.ShapeDtypeStruct((8, D), F32)),
        scratch_shapes=[pltpu.VMEM((tm, D), BF16), pltpu.VMEM((tm, D), BF16), pltpu.SemaphoreType.DMA((2,))],
        input_output_aliases={7: 0},
        compiler_params=pltpu.CompilerParams(vmem_limit_bytes=VMEM_LIMIT),
    )(dyc, dyc, proj, proj, proj, proj, conv_w, dproj)


def _dh_dx(dproj, w_in_all, x2, dxd, mod3, chip_sums):
    tm = 1024
    per_seq = S // tm
    n = len(chip_sums)

    def body(*refs):
        d_ref, w_ref, x_ref, dxd_ref, mod_ref = refs[:5]
        ins = refs[5:5 + n]
        gx_ref, vec_ref = refs[5 + n:7 + n]
        outs = refs[7 + n:7 + 2 * n]
        acc, send_sems, recv_sems, local_sems = refs[7 + 2 * n:]
        i, jj = pl.program_id(0), pl.program_id(1)

        @pl.when((i == 0) & (jj == 0))
        def _():
            vec_ref[...] = jnp.zeros_like(vec_ref)
            if n:
                sends, _, mine = _chip_copies(ins, outs, send_sems, recv_sems, local_sems)
                for cp in sends + mine:
                    cp.start()

        if n:
            @pl.when((i == T // tm - 1) & (jj == N_DEV - 1))
            def _():
                sends, arrivals, mine = _chip_copies(ins, outs, send_sems, recv_sems, local_sems)
                for cp in arrivals:
                    cp.wait_recv()
                for cp in sends:
                    cp.wait_send()
                for cp in mine:
                    cp.wait()

        @pl.when(jj == 0)
        def _():
            acc[...] = jnp.zeros_like(acc)

        acc[...] += _dot_nt(d_ref[...], w_ref[...])

        @pl.when(jj == N_DEV - 1)
        def _():
            dh = acc[...]
            bidx = i // per_seq
            gx_ref[...] = dxd_ref[...] + dh * (1.0 + mod_ref[0, 1:2, :])
            dshift = jnp.sum(dh, axis=0, keepdims=True)
            dscale = jnp.sum(dh * x_ref[...], axis=0, keepdims=True)
            vec_ref[0:1, :] += jnp.where(bidx == 0, dshift, 0.0)
            vec_ref[1:2, :] += jnp.where(bidx == 1, dshift, 0.0)
            vec_ref[2:3, :] += jnp.where(bidx == 0, dscale, 0.0)
            vec_ref[3:4, :] += jnp.where(bidx == 1, dscale, 0.0)

    any_spec = pl.BlockSpec(memory_space=pl.ANY)
    res = pl.pallas_call(
        body, name="dh_dx",
        grid=(T // tm, N_DEV),
        in_specs=[
            pl.BlockSpec((tm, SHARD), lambda i, jj: (i, jj)),
            pl.BlockSpec((None, D, SHARD), lambda i, jj: (jj, 0, 0)),
            pl.BlockSpec((tm, D), lambda i, jj: (i, 0)),
            pl.BlockSpec((tm, D), lambda i, jj: (i, 0)),
            pl.BlockSpec((1, 3, D), lambda i, jj: (i // per_seq, 0, 0))] + [any_spec] * n,
        out_specs=(pl.BlockSpec((tm, D), lambda i, jj: (i, 0)),
                   pl.BlockSpec((8, D), lambda i, jj: (0, 0))) + (any_spec,) * n,
        out_shape=(jax.ShapeDtypeStruct((T, D), F32), jax.ShapeDtypeStruct((8, D), F32))
                  + tuple(jax.ShapeDtypeStruct(a.shape, a.dtype) for a in chip_sums),
        scratch_shapes=[pltpu.VMEM((tm, D), F32), pltpu.SemaphoreType.DMA((max(3 * n, 1),)),
                        pltpu.SemaphoreType.DMA((max(3 * n, 1),)), pltpu.SemaphoreType.DMA((max(n, 1),))],
        compiler_params=pltpu.CompilerParams(vmem_limit_bytes=VMEM_LIMIT),
    )(dproj, w_in_all, x2, dxd, mod3, *chip_sums)
    return res[0], res[1], res[2:]


def _mm_tn(a, b, tn, blocks_leading, name):
    kk, m = a.shape
    n = b.shape[1]
    tk = 2048

    def body(a_ref, b_ref, o_ref, acc):
        @pl.when(pl.program_id(1) == 0)
        def _():
            acc[...] = jnp.zeros_like(acc)

        acc[...] += _dot_tn(a_ref[...], b_ref[...])

        @pl.when(pl.program_id(1) == kk // tk - 1)
        def _():
            o_ref[...] = acc[...].astype(BF16)

    if blocks_leading:
        out_spec = pl.BlockSpec((None, m, tn), lambda j, k: (j, 0, 0))
        out_shape = jax.ShapeDtypeStruct((n // tn, m, tn), BF16)
    else:
        out_spec = pl.BlockSpec((m, tn), lambda j, k: (0, j))
        out_shape = jax.ShapeDtypeStruct((m, n), BF16)
    return pl.pallas_call(
        body, name=name,
        grid=(n // tn, kk // tk),
        in_specs=[pl.BlockSpec((tk, m), lambda j, k: (k, 0)),
                  pl.BlockSpec((tk, tn), lambda j, k: (k, j))],
        out_specs=out_spec, out_shape=out_shape,
        scratch_shapes=[pltpu.VMEM((m, tn), F32)],
        compiler_params=pltpu.CompilerParams(vmem_limit_bytes=VMEM_LIMIT),
    )(a, b)


def _adamw(parts, w, m, v, name, row_tile=None):
    n_parts, rows, cols = parts.shape
    tr = rows if row_tile is None else row_tile
    c1 = 1.0 - ADAM_B1 ** ADAM_STEP
    c2 = 1.0 - ADAM_B2 ** ADAM_STEP

    def body(p_ref, w_ref, m_ref, v_ref, g_ref, d_ref, nm_ref, nv_ref):
        g = p_ref[0].astype(F32)
        for s in range(1, n_parts):
            g = g + p_ref[s].astype(F32)
        nm = ADAM_B1 * m_ref[...] + (1.0 - ADAM_B1) * g
        nv = ADAM_B2 * v_ref[...] + (1.0 - ADAM_B2) * (g * g)
        m_hat = nm / c1
        v_hat = nv / c2
        g_ref[...] = g
        d_ref[...] = -ADAM_LR * (m_hat / (jnp.sqrt(v_hat) + ADAM_EPS) + ADAM_WD * w_ref[...])
        nm_ref[...] = nm
        nv_ref[...] = nv

    blk = pl.BlockSpec((tr, cols), lambda i: (i, 0))
    shp = jax.ShapeDtypeStruct((rows, cols), F32)
    return pl.pallas_call(
        body, name=name,
        grid=(rows // tr,),
        in_specs=[pl.BlockSpec((n_parts, tr, cols), lambda i: (0, i, 0)), blk, blk, blk],
        out_specs=(blk, blk, blk, blk),
        out_shape=(shp, shp, shp, shp),
        compiler_params=pltpu.CompilerParams(vmem_limit_bytes=VMEM_LIMIT),
    )(parts, w, m, v)


def _loss_sum(rows):
    def body(r_ref, o_ref):
        o_ref[...] = jnp.sum(jnp.sum(r_ref[...], axis=0, keepdims=True), axis=1, keepdims=True)

    return pl.pallas_call(body, name="loss_sum", out_shape=jax.ShapeDtypeStruct((1, 1), F32))(rows)


def _local_step(x2, tgt2, mod3, h, proj, w_ao, w_co, w_o, conv_w, conv_b, rel_bias, ln_g, ln_b):
    buckets_np, masks_np = _bucket_maps()
    buckets, masks = jnp.asarray(buckets_np), jnp.asarray(masks_np)

    bias = _bias_expand(rel_bias, buckets, masks)
    fwd = [_attn_fwd(proj, bias, g) for g in range(3)]
    o_g = [f[0] for f in fwd]
    lse_g = [f[1] for f in fwd]

    (dproj, dyc, d_o, stats, dxd, merged, dy, a_in, da_out, s_in, ds_out, tail_vec) = _tail(
        x2, tgt2, mod3, o_g, lse_g, proj, w_ao, w_co, w_o, conv_w, conv_b, ln_g, ln_b)

    dbias = []
    for g in range(3):
        dproj, db = _attn_bwd(proj, d_o, stats, bias, dproj, g)
        dbias.append(db)
    g_rel_bias = _bias_grad(*dbias, buckets)
    dproj, conv_vec = _conv_bwd(dyc, proj, conv_w, dproj)

    gw_in = _mm_tn(h, dproj, SHARD, True, "gw_in")
    gw_o = _mm_tn(merged, dy, D, False, "gw_o")
    gw_co = _mm_tn(s_in, ds_out, D, False, "gw_conv_out")
    gw_ao = _mm_tn(a_in, da_out, D, False, "gw_attn_out")
    gw_ao = jnp.transpose(gw_ao.reshape(GW, N_DEV, D // N_DEV), (1, 0, 2))
    return dproj, dxd, gw_in, gw_ao, gw_co, gw_o, conv_vec, g_rel_bias, tail_vec


def kernel(x, c, w_ada, b_ada, w_in, conv_w, conv_b, rel_bias, w_attn_out, w_conv_out, w_o, ln_g, ln_b, loss_target, m_w_ada, m_b_ada, m_w_in, m_conv_w, m_conv_b, m_rel_bias, m_w_attn_out, m_w_conv_out, m_w_o, m_ln_g, m_ln_b, v_w_ada, v_b_ada, v_w_in, v_conv_w, v_conv_b, v_rel_bias, v_w_attn_out, v_w_conv_out, v_w_o, v_ln_g, v_ln_b):
    me = _my_index()
    x2 = x.reshape(T, D)
    tgt2 = loss_target.reshape(T, D)

    w_ao_g, w_co_g, w_o_g, conv_w_g, c_g = _all_gather(
        [w_attn_out[0].astype(BF16), w_conv_out[0].astype(BF16),
         w_o[0].astype(BF16), conv_w[0], c],
        [jax.ShapeDtypeStruct((N_DEV, GW, D // N_DEV), BF16),
         jax.ShapeDtypeStruct((N_DEV, D // N_DEV, D), BF16),
         jax.ShapeDtypeStruct((N_DEV, D // N_DEV, D), BF16),
         jax.ShapeDtypeStruct((N_DEV, 3, D // N_DEV), F32),
         jax.ShapeDtypeStruct((N_DEV, BL, D), F32)],
        [_slot_leading] * 5,
        "gather_weights")
    w_ao_full = jnp.transpose(w_ao_g, (1, 0, 2)).reshape(GW, D)
    w_co_full = w_co_g.reshape(D, D)
    w_o_full = w_o_g.reshape(D, D)
    conv_w_full = jnp.transpose(conv_w_g, (1, 0, 2)).reshape(3, D)
    c_all = c_g.reshape(N_DEV * BL, D)

    b_cols = lax.dynamic_slice(b_ada, (0, me * ADA_SHARD), (1, ADA_SHARD))
    mod_cols = _ada_fwd(c_all, w_ada[0], b_cols)
    (mod_g,) = _all_gather([mod_cols], [jax.ShapeDtypeStruct((N_DEV, N_DEV * BL, ADA_SHARD), F32)],
                           [_slot_leading], "gather_mod")
    mod_all = jnp.transpose(mod_g, (1, 0, 2)).reshape(N_DEV * BL, 3 * D)
    mod3 = lax.dynamic_slice(mod_all, (me * BL, 0), (BL, 3 * D)).reshape(BL, 3, D)

    h = _prep_h(x2, mod3)
    proj, w_in_all = _gather_proj(_shard_order(), h, w_in[0].astype(BF16), 1024)
    (dproj, dxd, gw_in, gw_ao, gw_co, gw_o, conv_vec, g_rel_bias, tail_vec) = _local_step(
        x2, tgt2, mod3, h, proj, w_ao_full, w_co_full, w_o_full,
        conv_w_full, conv_b, rel_bias, ln_g, ln_b)

    g_conv_w_blocks = jnp.transpose(conv_vec[0:3].reshape(3, N_DEV, D // N_DEV), (1, 0, 2))
    partials = [gw_in, gw_ao, gw_co.reshape(N_DEV, D // N_DEV, D), gw_o.reshape(N_DEV, D // N_DEV, D),
                g_conv_w_blocks]
    sib = _pair_exchange(
        partials,
        [jax.ShapeDtypeStruct((4, D, SHARD), BF16),
         jax.ShapeDtypeStruct((4, GW, D // N_DEV), BF16),
         jax.ShapeDtypeStruct((4, D // N_DEV, D), BF16),
         jax.ShapeDtypeStruct((4, D // N_DEV, D), BF16),
         jax.ShapeDtypeStruct((4, 3, D // N_DEV), F32)],
        [_slot_leading] * 5,
        "pair_grads")
    tiles = [256, None, None, None, None]
    names = ["w_in", "w_attn_out", "w_conv_out", "w_o", "conv_w"]
    core = lax.axis_index("c").astype(jnp.int32).reshape(1)
    chip_sums = [_pair_add(core, partials[a], sib[a], tiles[a], "pair_add_" + names[a]) for a in range(5)]
    grad_x, mod_vec, (r_in, r_ao, r_co, r_o, r_cw) = _dh_dx(dproj, w_in_all, x2, dxd, mod3, chip_sums)

    small = jnp.concatenate([
        tail_vec[0:4],
        jnp.pad(g_rel_bias.reshape(1, N_BUCKETS * N_HEADS), ((0, 0), (0, D - N_BUCKETS * N_HEADS))),
        jnp.zeros((3, D), F32)], axis=0)
    dmod = jnp.concatenate([mod_vec[0:2], mod_vec[2:4], tail_vec[4:6]], axis=1)
    small_g, dmod_g = _all_gather(
        [small, dmod],
        [jax.ShapeDtypeStruct((N_DEV, 8, D), F32), jax.ShapeDtypeStruct((N_DEV, BL, 3 * D), F32)],
        [_slot_leading] * 2, "gather_small")
    dmod_all = dmod_g.reshape(N_DEV * BL, 3 * D)
    loss = _loss_sum(small_g[:, 3, :]).reshape(())
    g_w_ada = _ada_bwd(jnp.transpose(c_all), lax.dynamic_slice(dmod_all, (0, me * ADA_SHARD),
                                                               (N_DEV * BL, ADA_SHARD)))

    def upd(parts, w, m, v, name, row_tile=None):
        shape = w.shape
        w2, m2, v2 = (t.reshape(parts.shape[1:]) for t in (w, m, v))
        return tuple(t.reshape(shape) for t in _adamw(parts, w2, m2, v2, name, row_tile))

    res = {
        "w_ada": upd(g_w_ada[None], w_ada, m_w_ada, v_w_ada, "adam_w_ada", 256),
        "b_ada": upd(dmod_all[:, None, :], b_ada, m_b_ada, v_b_ada, "adam_b_ada"),
        "w_in": upd(r_in, w_in, m_w_in, v_w_in, "adam_w_in", 128),
        "conv_w": upd(r_cw, conv_w, m_conv_w, v_conv_w, "adam_conv_w"),
        "conv_b": upd(small_g[:, 0:1, :], conv_b, m_conv_b, v_conv_b, "adam_conv_b"),
        "rel_bias": upd(small_g[:, 4, :N_BUCKETS * N_HEADS].reshape(N_DEV, N_BUCKETS, N_HEADS),
                        rel_bias, m_rel_bias, v_rel_bias, "adam_rel_bias"),
        "w_attn_out": upd(r_ao, w_attn_out, m_w_attn_out, v_w_attn_out, "adam_w_attn_out"),
        "w_conv_out": upd(r_co, w_conv_out, m_w_conv_out, v_w_conv_out, "adam_w_conv_out"),
        "w_o": upd(r_o, w_o, m_w_o, v_w_o, "adam_w_o"),
        "ln_g": upd(small_g[:, 1:2, :], ln_g, m_ln_g, v_ln_g, "adam_ln_g"),
        "ln_b": upd(small_g[:, 2:3, :], ln_b, m_ln_b, v_ln_b, "adam_ln_b"),
    }
    order = ["w_ada", "b_ada", "w_in", "conv_w", "conv_b", "rel_bias", "w_attn_out", "w_conv_out",
             "w_o", "ln_g", "ln_b"]
    outs = [loss, grad_x.reshape(BL, S, D)]
    for k in range(4):
        outs += [res[name][k] for name in order]
    return tuple(outs)
```

```python
import functools
import math

import numpy as np
import jax
import jax.numpy as jnp
from jax import lax
from jax.experimental import pallas as pl
from jax.experimental.pallas import tpu as pltpu

F32 = jnp.float32
BF16 = jnp.bfloat16
MESH = pl.DeviceIdType.MESH

N_DEV = 8
D = 1024
S = 2048
BL = 2
T = BL * S
NCOL = 11264
SHARD = NCOL // N_DEV
CB = 512
NCB = NCOL // CB
HD = 128
GW = 512
QB = 128
DILATIONS = (1, 4, 16)
N_STEPS = 128
N_BUCKETS = 32
N_HEADS = 12
ALPHA = 2.0 ** 0.25
LN_EPS = 1e-5
NEG_INF = -1e30
SCALE = HD ** -0.5
ADA_SHARD = 3 * D // N_DEV

CB_Q, CB_K, CB_V, CB_GA = 0, 3, 6, 9
KB_U, KB_BG, KB_CG, KB_GC, KB_MA, KB_MC = 5, 6, 7, 8, 9, 10

ADAM_LR, ADAM_B1, ADAM_B2, ADAM_EPS, ADAM_WD, ADAM_STEP = 0.001, 0.9, 0.999, 1e-08, 0.01, 10

VMEM_LIMIT = 56 * 1024 * 1024


def _dot(a, b):
    return jnp.dot(a, b, preferred_element_type=F32)


def _dot_nt(a, b):
    return lax.dot_general(a, b, (((1,), (1,)), ((), ())), preferred_element_type=F32)


def _dot_tn(a, b):
    return lax.dot_general(a, b, (((0,), (0,)), ((), ())), preferred_element_type=F32)


def _sigmoid(v):
    return 1.0 / (1.0 + jnp.exp(-v))


def _write_columns(pieces, dst_hbm, row0, sems):
    copies = []
    for k, (src, col0) in enumerate(pieces):
        rows, width = src.shape
        copies.append(pltpu.make_async_copy(
            src, dst_hbm.at[pl.ds(row0, rows), pl.ds(col0, width)], sems.at[k]))
    for cp in copies:
        cp.start()
    for cp in copies:
        cp.wait()


def _my_index():
    return 4 * lax.axis_index("x") + 2 * lax.axis_index("y") + lax.axis_index("c")


def _slot_leading(ref, slot):
    return ref.at[slot]


def _all_gather(arrs, out_shapes, slot_fns, name):
    n = len(arrs)

    def body(*refs):
        ins, outs = refs[:n], refs[n:2 * n]
        send_sems, recv_sems, local_sems = refs[2 * n:2 * n + 3]
        stage = refs[2 * n + 3:]
        x, y, c = lax.axis_index("x"), lax.axis_index("y"), lax.axis_index("c")
        me, sibling = (x, y, c), (x, y, 1 - c)
        chips = [(1 - x, y), (x, 1 - y), (1 - x, 1 - y)]

        def blk(a, dev):
            return slot_fns[a](outs[a], 4 * dev[0] + 2 * dev[1] + dev[2])

        def copy(a, k, block, to, src=None):
            dst = blk(a, block)
            return pltpu.make_async_remote_copy(
                src_ref=dst if src is None else src, dst_ref=dst,
                send_sem=send_sems.at[a * 7 + k], recv_sem=recv_sems.at[a * 7 + k],
                device_id=to, device_id_type=MESH)

        first = []
        for a in range(n):
            first.append(copy(a, 0, me, sibling, src=ins[a]))
            first += [copy(a, 1 + j, me, (*chip, c), src=ins[a]) for j, chip in enumerate(chips)]
        for cp in first:
            cp.start()
        loads = [pltpu.make_async_copy(ins[a], stage[a], local_sems.at[a]) for a in range(n)]
        for cp in loads:
            cp.start()
        for cp in loads:
            cp.wait()
        mine = [pltpu.make_async_copy(stage[a], blk(a, me), local_sems.at[a]) for a in range(n)]
        for cp in mine:
            cp.start()
        passed = []
        for j, chip in enumerate(chips):
            for a in range(n):
                copy(a, 1 + j, (*chip, c), me).wait_recv()
                fwd = copy(a, 4 + j, (*chip, c), sibling)
                fwd.start()
                passed.append(fwd)
        for a in range(n):
            copy(a, 0, sibling, me).wait_recv()
        for j, chip in enumerate(chips):
            for a in range(n):
                copy(a, 4 + j, (*chip, 1 - c), me).wait_recv()
        for cp in first + passed:
            cp.wait_send()
        for cp in mine:
            cp.wait()

    any_spec = pl.BlockSpec(memory_space=pl.ANY)
    return pl.pallas_call(
        body, name=name,
        out_shape=tuple(out_shapes),
        in_specs=[any_spec] * n,
        out_specs=tuple([any_spec] * n),
        scratch_shapes=[pltpu.SemaphoreType.DMA((7 * n,)), pltpu.SemaphoreType.DMA((7 * n,)),
                        pltpu.SemaphoreType.DMA((n,))]
                       + [pltpu.VMEM(a.shape, a.dtype) for a in arrs],
    )(*arrs)


def _pair_exchange(arrs, shapes4, src_fns, name):
    n = len(arrs)

    def body(*refs):
        ins, recv = refs[:n], refs[n:2 * n]
        send_sems, recv_sems = refs[2 * n:]
        x, y, c = lax.axis_index("x"), lax.axis_index("y"), lax.axis_index("c")
        sibling = (x, y, 1 - c)
        remote = []
        for a in range(n):
            for q in range(4):
                remote.append(pltpu.make_async_remote_copy(
                    src_ref=src_fns[a](ins[a], 2 * q + 1 - c), dst_ref=recv[a].at[q],
                    send_sem=send_sems.at[a * 4 + q], recv_sem=recv_sems.at[a * 4 + q],
                    device_id=sibling, device_id_type=MESH))
        for cp in remote:
            cp.start()
        for cp in remote:
            cp.wait()

    any_spec = pl.BlockSpec(memory_space=pl.ANY)
    return pl.pallas_call(
        body, name=name,
        out_shape=tuple(shapes4),
        in_specs=[any_spec] * n,
        out_specs=tuple([any_spec] * n),
        scratch_shapes=[pltpu.SemaphoreType.DMA((4 * n,))] * 2,
    )(*arrs)


def _chip_copies(ins, outs, send_sems, recv_sems, local_sems):
    n = len(ins)
    x, y, c = lax.axis_index("x"), lax.axis_index("y"), lax.axis_index("c")
    my_chip = 2 * x + y

    def peer_of(k):
        return ((1 - x) if (k >> 1) & 1 else x, (1 - y) if k & 1 else y, c)

    def copy(a, k, out_chip):
        peer = peer_of(k)
        return pltpu.make_async_remote_copy(
            src_ref=ins[a].at[2 * peer[0] + peer[1]], dst_ref=outs[a].at[out_chip],
            send_sem=send_sems.at[a * 3 + k - 1], recv_sem=recv_sems.at[a * 3 + k - 1],
            device_id=peer, device_id_type=MESH)

    sends = [copy(a, k, my_chip) for k in range(1, 4) for a in range(n)]
    arrivals = []
    for k in range(1, 4):
        peer = peer_of(k)
        arrivals += [copy(a, k, 2 * peer[0] + peer[1]) for a in range(n)]
    mine = [pltpu.make_async_copy(ins[a].at[my_chip], outs[a].at[my_chip], local_sems.at[a])
            for a in range(n)]
    return sends, arrivals, mine


def _pair_add(core, mine, theirs, row_tile, name):
    _, rows, cols = theirs.shape
    tr = rows if row_tile is None else row_tile

    def body(core_ref, a_ref, b_ref, o_ref):
        o_ref[...] = (a_ref[...].astype(F32) + b_ref[...].astype(F32)).astype(o_ref.dtype)

    blk = pl.BlockSpec((None, tr, cols), lambda q, i, core_ref: (q, i, 0))
    return pl.pallas_call(
        body, name=name,
        grid_spec=pltpu.PrefetchScalarGridSpec(
            num_scalar_prefetch=1,
            grid=(4, rows // tr),
            in_specs=[pl.BlockSpec((None, tr, cols), lambda q, i, core_ref: (2 * q + core_ref[0], i, 0)), blk],
            out_specs=blk),
        out_shape=jax.ShapeDtypeStruct(theirs.shape, theirs.dtype),
    )(core, mine, theirs)


def _mod_exchange(c8, w_ada, b_cols):
    cols = w_ada.shape[1]

    def body(c_ref, w_ref, b_ref, call_ref, mod_ref, msend, send1, recv1, send2, recv2):
        x, y, c = lax.axis_index("x"), lax.axis_index("y"), lax.axis_index("c")
        my_slot = 4 * x + 2 * y + c

        def peer_of(k):
            return ((1 - x) if (k >> 2) & 1 else x, (1 - y) if (k >> 1) & 1 else y, (1 - c) if k & 1 else c)

        def slot_of(dev):
            return 4 * dev[0] + 2 * dev[1] + dev[2]

        def exchange(src_of, dst_ref, send_sems, recv_sems):
            sends, arrivals = [], []
            for k in range(1, 8):
                peer = peer_of(k)
                sends.append(pltpu.make_async_remote_copy(
                    src_ref=src_of(slot_of(peer)), dst_ref=dst_ref.at[my_slot],
                    send_sem=send_sems.at[k - 1], recv_sem=recv_sems.at[k - 1],
                    device_id=peer, device_id_type=MESH))
                arrivals.append(pltpu.make_async_remote_copy(
                    src_ref=src_of(my_slot), dst_ref=dst_ref.at[slot_of(peer)],
                    send_sem=send_sems.at[k - 1], recv_sem=recv_sems.at[k - 1],
                    device_id=peer, device_id_type=MESH))
            for cp in sends:
                cp.start()
            for cp in arrivals:
                cp.wait_recv()
            for cp in sends:
                cp.wait_send()

        call_ref[my_slot] = c_ref[...]
        exchange(lambda s: c_ref, call_ref, send1, recv1)
        cv = call_ref[...].reshape(N_DEV * 8, c_ref.shape[1])
        act = cv * _sigmoid(cv)
        mod = jnp.dot(act, w_ref[...], preferred_element_type=F32,
                      precision=lax.Precision.HIGHEST) + b_ref[...]
        msend[...] = mod.reshape(N_DEV, 8, cols)
        mod_ref[my_slot] = msend[my_slot]
        exchange(lambda s: msend.at[s], mod_ref, send2, recv2)

    return pl.pallas_call(
        body, name="mod_exchange",
        out_shape=(jax.ShapeDtypeStruct((N_DEV, 8, c8.shape[1]), F32),
                   jax.ShapeDtypeStruct((N_DEV, 8, cols), F32)),
        scratch_shapes=[pltpu.VMEM((N_DEV, 8, cols), F32)] + [pltpu.SemaphoreType.DMA((7,))] * 4,
    )(c8, w_ada, b_cols)


def _ada_bwd(c_all_t, dmod_cols):
    def body(c_ref, d_ref, o_ref):
        cv = c_ref[...]
        sc = cv * _sigmoid(cv)
        o_ref[...] = jnp.dot(sc, d_ref[...], preferred_element_type=F32,
                             precision=lax.Precision.HIGHEST)

    return pl.pallas_call(
        body, name="ada_bwd",
        out_shape=jax.ShapeDtypeStruct((c_all_t.shape[0], dmod_cols.shape[1]), F32),
    )(c_all_t, dmod_cols)


def _prep_h(x2, mod3):
    ts = 512
    per_seq = S // ts

    def body(x_ref, mod_ref, h_ref):
        shift = mod_ref[0, 0:1, :]
        scale = mod_ref[0, 1:2, :]
        h_ref[...] = (x_ref[...] * (1.0 + scale) + shift).astype(BF16)

    return pl.pallas_call(
        body, name="prep_h",
        grid=(T // ts,),
        in_specs=[pl.BlockSpec((ts, D), lambda i: (i, 0)),
                  pl.BlockSpec((1, 3, D), lambda i: (i // per_seq, 0, 0))],
        out_specs=pl.BlockSpec((ts, D), lambda i: (i, 0)),
        out_shape=jax.ShapeDtypeStruct((T, D), BF16),
    )(x2, mod3)


def _shard_order():
    x, y, c = lax.axis_index("x"), lax.axis_index("y"), lax.axis_index("c")
    devs = [(x, y, c), (x, y, 1 - c)]
    for chip in [(1 - x, y), (x, 1 - y), (1 - x, 1 - y)]:
        devs += [(*chip, c), (*chip, 1 - c)]
    return jnp.stack([4 * d[0] + 2 * d[1] + d[2] for d in devs]).astype(jnp.int32)


def _gather_proj(order, h, w_shard, tm):
    rows, kdim = h.shape
    ncols = w_shard.shape[1]
    n_i = rows // tm

    def body(order_ref, h_ref, mine_hbm, o_ref, all_hbm, wv, send_sems, recv_sems, local_sems):
        j, i = pl.program_id(0), pl.program_id(1)
        x, y, c = lax.axis_index("x"), lax.axis_index("y"), lax.axis_index("c")
        me, sibling = (x, y, c), (x, y, 1 - c)
        chips = [(1 - x, y), (x, 1 - y), (1 - x, 1 - y)]

        def slot(dev):
            return 4 * dev[0] + 2 * dev[1] + dev[2]

        def copy(k, block, to):
            return pltpu.make_async_remote_copy(
                src_ref=wv.at[slot(block)], dst_ref=wv.at[slot(block)],
                send_sem=send_sems.at[k], recv_sem=recv_sems.at[k],
                device_id=to, device_id_type=MESH)

        def keep(step, block):
            return pltpu.make_async_copy(wv.at[slot(block)], all_hbm.at[slot(block)], local_sems.at[step])

        first = [copy(0, me, sibling)] + [copy(1 + q, me, (*chip, c)) for q, chip in enumerate(chips)]
        passed = [copy(4 + q, (*chip, c), sibling) for q, chip in enumerate(chips)]
        due = [(me, None, None), (sibling, copy(0, sibling, me), None)]
        for q, chip in enumerate(chips):
            due.append(((*chip, c), copy(1 + q, (*chip, c), me), passed[q]))
            due.append(((*chip, 1 - c), copy(4 + q, (*chip, 1 - c), me), None))

        @pl.when((j == 0) & (i == 0))
        def _():
            load = pltpu.make_async_copy(mine_hbm, wv.at[slot(me)], local_sems.at[N_DEV])
            load.start()
            load.wait()
            for cp in first:
                cp.start()
            keep(0, me).start()

        for step in range(1, N_DEV):
            block, arrival, forward = due[step]

            @pl.when((j == step) & (i == 0))
            def _():
                arrival.wait_recv()
                if forward is not None:
                    forward.start()
                keep(step, block).start()

        o_ref[...] = _dot(h_ref[...], wv[order_ref[j]]).astype(BF16)

        @pl.when((j == N_DEV - 1) & (i == n_i - 1))
        def _():
            for cp in first + passed:
                cp.wait_send()
            for step in range(N_DEV):
                keep(step, due[step][0]).wait()

    return pl.pallas_call(
        body, name="gather_proj",
        grid_spec=pltpu.PrefetchScalarGridSpec(
            num_scalar_prefetch=1,
            grid=(N_DEV, n_i),
            in_specs=[pl.BlockSpec((tm, kdim), lambda j, i, order_ref: (i, 0)),
                      pl.BlockSpec(memory_space=pl.ANY)],
            out_specs=(pl.BlockSpec((tm, ncols), lambda j, i, order_ref: (i, order_ref[j])),
                       pl.BlockSpec(memory_space=pl.ANY)),
            scratch_shapes=[pltpu.VMEM((N_DEV, kdim, ncols), BF16),
                            pltpu.SemaphoreType.DMA((7,)), pltpu.SemaphoreType.DMA((7,)),
                            pltpu.SemaphoreType.DMA((N_DEV + 1,))]),
        out_shape=(jax.ShapeDtypeStruct((rows, N_DEV * ncols), BF16),
                   jax.ShapeDtypeStruct((N_DEV, kdim, ncols), BF16)),
        compiler_params=pltpu.CompilerParams(vmem_limit_bytes=VMEM_LIMIT),
    )(order, h, w_shard)


def _bucket_maps():
    a = np.arange(QB)[:, None]
    b = np.arange(2 * QB)[None, :]
    steps = a + QB - b
    maps = []
    for dil in DILATIONS:
        dist = np.maximum(steps, 0) * dil
        nf = np.maximum(dist, 1).astype(np.float32)
        large = 16 + (np.log(nf / np.float32(16)) / np.float32(math.log(128.0))
                      * np.float32(16)).astype(np.int32)
        large = np.minimum(large, N_BUCKETS - 1)
        maps.append(np.where(dist < 16, dist, large).astype(np.int32))
    band = (steps >= 0) & (steps <= N_STEPS)
    first = band & (b >= QB)
    masks = np.stack([first, band]).astype(np.int32)
    return np.stack(maps), masks


def _bias_expand(rel_bias, buckets, masks):
    def body(tab_ref, bk_ref, mk_ref, o_ref):
        for g in range(3):
            bk = bk_ref[g]
            for h in range(4):
                col = 4 * g + h
                val = jnp.zeros((QB, 2 * QB), F32)
                for k in range(N_BUCKETS):
                    val = jnp.where(bk == k, tab_ref[k, col], val)
                o_ref[g, 0, h] = jnp.where(mk_ref[0] != 0, val, NEG_INF)
                o_ref[g, 1, h] = jnp.where(mk_ref[1] != 0, val, NEG_INF)

    return pl.pallas_call(
        body, name="bias_expand",
        in_specs=[pl.BlockSpec(memory_space=pltpu.SMEM),
                  pl.BlockSpec(memory_space=pltpu.VMEM),
                  pl.BlockSpec(memory_space=pltpu.VMEM)],
        out_shape=jax.ShapeDtypeStruct((3, 2, 4, QB, 2 * QB), F32),
    )(rel_bias, buckets, masks)


def _bias_grad(ds1, ds2, ds3, buckets):
    def body(d1_ref, d2_ref, d3_ref, bk_ref, o_ref):
        for g, d_ref in enumerate((d1_ref, d2_ref, d3_ref)):
            bk = bk_ref[g]
            for h in range(4):
                dv = d_ref[h]
                for k in range(N_BUCKETS):
                    o_ref[k, 4 * g + h] = jnp.sum(jnp.where(bk == k, dv, 0.0))

    return pl.pallas_call(
        body, name="bias_grad",
        in_specs=[pl.BlockSpec(memory_space=pltpu.VMEM)] * 4,
        out_specs=pl.BlockSpec(memory_space=pltpu.SMEM),
        out_shape=jax.ShapeDtypeStruct((N_BUCKETS, N_HEADS), F32),
    )(ds1, ds2, ds3, buckets)


def _scratch_sets(rows):
    return 4 if rows <= 512 else 1


def _unit_chunks(dil, size=16):
    units = [(h, r) for h in range(4) for r in range(dil)]
    return [units[i:i + size] for i in range(0, len(units), size)]


def _residue_rows(src_ref, copies, h, residue):
    sl = slice(h * HD, (h + 1) * HD)
    if copies is None:
        return lambda r: src_ref[:, sl]
    buf = copies[h % len(copies)]
    buf[...] = src_ref[:, sl].astype(F32)
    return lambda r: buf[residue(r), :].astype(BF16)


def _attn_fwd(proj, bias, g):
    dil = DILATIONS[g]
    rows = QB * dil
    nsb = S // rows
    has_prev = nsb > 1

    def residue(r):
        return pl.ds(r, QB, stride=dil) if dil > 1 else pl.ds(0, QB)

    strided = dil > 1
    n_sets = _scratch_sets(rows)
    n_in = 6 if has_prev else 4

    def body(*refs):
        q_ref, kc_ref, vc_ref = refs[:3]
        kp_ref, vp_ref = refs[3:5] if has_prev else (None, None)
        b_ref, o_ref, l_ref = refs[n_in - 1:n_in + 2]
        scr = list(refs[n_in + 2:])
        ls = [scr.pop(0) for _ in range(4)]
        copies = {name: [scr.pop(0) for _ in range(n_sets)] if strided else None
                  for name in ("q", "kc", "vc", "o") + (("kp", "vp") if has_prev else ())}
        lane = lax.broadcasted_iota(jnp.int32, (QB, 128), 1)
        refs_of = {"q": q_ref, "kc": kc_ref, "vc": vc_ref, "kp": kp_ref, "vp": vp_ref}
        for chunk in _unit_chunks(dil):
            rows_of = {h: {name: _residue_rows(refs_of[name], copies[name], h, residue)
                           for name in refs_of if refs_of[name] is not None}
                       for h in sorted({h for h, _ in chunk})}

            def batch(name):
                return jnp.stack([rows_of[h][name](r) for h, r in chunk])

            q, k, v = batch("q"), batch("kc"), batch("vc")
            if has_prev:
                k = jnp.concatenate([batch("kp"), k], axis=1)
                v = jnp.concatenate([batch("vp"), v], axis=1)
                bias_b = jnp.stack([b_ref[h] for h, _ in chunk])
            else:
                bias_b = jnp.stack([b_ref[h, :, QB:] for h, _ in chunk])
            s = jnp.einsum("uqd,ukd->uqk", q, k, preferred_element_type=F32) * SCALE + bias_b
            m = jnp.max(s, axis=-1, keepdims=True)
            p = jnp.exp(s - m)
            l = jnp.sum(p, axis=-1, keepdims=True)
            o = jnp.einsum("uqk,ukd->uqd", p.astype(BF16), v, preferred_element_type=F32) / l
            lse = m + jnp.log(l)
            for i, (h, r) in enumerate(chunk):
                if strided:
                    copies["o"][h % n_sets][residue(r), :] = o[i]
                else:
                    o_ref[:, h * HD:(h + 1) * HD] = o[i]
                ls[h][r * QB:(r + 1) * QB, :] = jnp.where(lane == h, lse[i], 0.0)
            if strided:
                for h in sorted({h for h, _ in chunk}):
                    o_ref[:, h * HD:(h + 1) * HD] = copies["o"][h % n_sets][...]
        for r in range(dil):
            blk = slice(r * QB, (r + 1) * QB)
            l_ref[residue(r), :] = (ls[0][blk, :] + ls[1][blk, :]) + (ls[2][blk, :] + ls[3][blk, :])

    def row(b, n):
        return b * nsb + n

    def prev(b, n):
        return b * nsb + jnp.maximum(n - 1, 0)

    in_specs = [
        pl.BlockSpec((rows, GW), lambda b, n: (row(b, n), CB_Q + g)),
        pl.BlockSpec((rows, GW), lambda b, n: (row(b, n), CB_K + g)),
        pl.BlockSpec((rows, GW), lambda b, n: (row(b, n), CB_V + g)),
    ]
    args = [proj, proj, proj]
    n_copied = (4 + (2 if has_prev else 0)) * (n_sets if strided else 0)
    scratch = [pltpu.VMEM((rows, 128), F32)] * (4 + n_copied)
    if has_prev:
        in_specs += [pl.BlockSpec((rows, GW), lambda b, n: (prev(b, n), CB_K + g)),
                     pl.BlockSpec((rows, GW), lambda b, n: (prev(b, n), CB_V + g))]
        args += [proj, proj]
    in_specs.append(pl.BlockSpec((None, None, 4, QB, 2 * QB),
                                 lambda b, n: (g, jnp.minimum(n, 1), 0, 0, 0)))
    args.append(bias)
    return pl.pallas_call(
        body, name=f"attn_fwd{g}",
        grid=(BL, nsb),
        in_specs=in_specs,
        out_specs=(pl.BlockSpec((rows, GW), lambda b, n: (row(b, n), 0)),
                   pl.BlockSpec((rows, 128), lambda b, n: (row(b, n), 0))),
        out_shape=(jax.ShapeDtypeStruct((T, GW), F32), jax.ShapeDtypeStruct((T, 128), F32)),
        scratch_shapes=scratch,
        compiler_params=pltpu.CompilerParams(vmem_limit_bytes=VMEM_LIMIT),
    )(*args)


def _attn_bwd(proj, d_out, stats, bias, dproj, g):
    dil = DILATIONS[g]
    rows = QB * dil
    nsb = S // rows
    has_prev = nsb > 1
    n_steps = nsb + 1 if has_prev else 1
    n_in = 7 + (2 if has_prev else 0)

    def residue(r):
        return pl.ds(r, QB, stride=dil) if dil > 1 else pl.ds(0, QB)

    strided = dil > 1
    n_sets = _scratch_sets(rows)

    def body(*refs):
        q_ref, kc_ref, vc_ref, do_ref, st_ref, b_ref = refs[:6]
        kp_ref, vp_ref = refs[6:8] if has_prev else (None, None)
        out_ref, db_ref = refs[n_in], refs[n_in + 1]
        scr = list(refs[n_in + 2:])
        sq, sk, sv, sems = [scr.pop(0) for _ in range(4)]
        carry = scr.pop(0) if has_prev else None
        sts = scr.pop(0) if strided else st_ref
        copies = {name: [scr.pop(0) for _ in range(n_sets)] if strided else None
                  for name in ("q", "kc", "vc", "do", "dq", "dk", "dv") + (("kp", "vp") if has_prev else ())}
        b, n = pl.program_id(0), pl.program_id(1)

        @pl.when((b == 0) & (n == 0))
        def _():
            db_ref[...] = jnp.zeros_like(db_ref)

        def finish(h, r, dq, dk, dv):
            if strided:
                for name, val in (("dq", dq), ("dk", dk), ("dv", dv)):
                    copies[name][h % n_sets][residue(r), :] = val
            else:
                sl = slice(h * HD, (h + 1) * HD)
                sq[:, sl], sk[:, sl], sv[:, sl] = dq.astype(BF16), dk.astype(BF16), dv.astype(BF16)

        def finish_head(h):
            if strided:
                sl = slice(h * HD, (h + 1) * HD)
                sq[:, sl] = copies["dq"][h % n_sets][...].astype(BF16)
                sk[:, sl] = copies["dk"][h % n_sets][...].astype(BF16)
                sv[:, sl] = copies["dv"][h % n_sets][...].astype(BF16)

        def write_block(blk_idx):
            row0 = pl.multiple_of(blk_idx * rows, rows)
            _write_columns([(sq, CB * (CB_Q + g)), (sk, CB * (CB_K + g)), (sv, CB * (CB_V + g))],
                           out_ref, row0, sems)

        def carried(h, r):
            blk = slice(r * QB, (r + 1) * QB)
            return ((blk, slice(h * HD, (h + 1) * HD)), (blk, slice(GW + h * HD, GW + (h + 1) * HD)),
                    (blk, slice(2 * GW + h * HD, 2 * GW + (h + 1) * HD)))

        if has_prev:
            @pl.when(n == 0)
            def _():
                carry[...] = jnp.zeros_like(carry)

            @pl.when(n == nsb)
            def _():
                for h in range(4):
                    for r in range(dil):
                        cq, ck, cv = carried(h, r)
                        finish(h, r, carry[cq], carry[ck], carry[cv])
                    finish_head(h)
                write_block(b * nsb + nsb - 1)

        @pl.when(n < nsb)
        def _():
            if strided:
                for r in range(dil):
                    sts[r * QB:(r + 1) * QB, :] = st_ref[residue(r), :]
            refs_of = {"q": q_ref, "kc": kc_ref, "vc": vc_ref, "do": do_ref, "kp": kp_ref, "vp": vp_ref}
            for chunk in _unit_chunks(dil):
                heads = sorted({h for h, _ in chunk})
                rows_of = {h: {name: _residue_rows(refs_of[name], copies[name], h, residue)
                               for name in refs_of if refs_of[name] is not None}
                           for h in heads}

                def batch(name):
                    return jnp.stack([rows_of[h][name](r) for h, r in chunk])

                q, k, v, do = batch("q"), batch("kc"), batch("vc"), batch("do")
                if has_prev:
                    k = jnp.concatenate([batch("kp"), k], axis=1)
                    v = jnp.concatenate([batch("vp"), v], axis=1)
                    bias_b = jnp.stack([b_ref[h] for h, _ in chunk])
                else:
                    bias_b = jnp.stack([b_ref[h, :, QB:] for h, _ in chunk])
                lse = jnp.stack([sts[r * QB:(r + 1) * QB, h:h + 1] for h, r in chunk])
                delta = jnp.stack([sts[r * QB:(r + 1) * QB, 4 + h:5 + h] for h, r in chunk])
                s = jnp.einsum("uqd,ukd->uqk", q, k, preferred_element_type=F32) * SCALE + bias_b
                p = jnp.exp(s - lse)
                ds = p * (jnp.einsum("uqd,ukd->uqk", do, v, preferred_element_type=F32) - delta)
                for h in heads:
                    mine = [ds[i] for i, (hh, _) in enumerate(chunk) if hh == h]
                    tot = mine[0]
                    for extra in mine[1:]:
                        tot = tot + extra
                    if has_prev:
                        db_ref[h] += tot
                    else:
                        db_ref[h, :, QB:] += tot
                dsb, pb = ds.astype(BF16), p.astype(BF16)
                dq = jnp.einsum("uqk,ukd->uqd", dsb, k, preferred_element_type=F32) * SCALE
                dk = jnp.einsum("uqk,uqd->ukd", dsb, q, preferred_element_type=F32) * SCALE
                dv = jnp.einsum("uqk,uqd->ukd", pb, do, preferred_element_type=F32)
                for i, (h, r) in enumerate(chunk):
                    if has_prev:
                        cq, ck, cv = carried(h, r)
                        finish(h, r, carry[cq], carry[ck] + dk[i, :QB], carry[cv] + dv[i, :QB])
                        carry[cq] = dq[i]
                        carry[ck] = dk[i, QB:]
                        carry[cv] = dv[i, QB:]
                    else:
                        finish(h, r, dq[i], dk[i], dv[i])
                for h in heads:
                    finish_head(h)
            if has_prev:
                @pl.when(n > 0)
                def _():
                    write_block(b * nsb + n - 1)
            else:
                write_block(b)

    def row(b, n):
        return b * nsb + jnp.minimum(n, nsb - 1)

    def prev(b, n):
        return b * nsb + jnp.maximum(jnp.minimum(n, nsb - 1) - 1, 0)

    in_specs = [
        pl.BlockSpec((rows, GW), lambda b, n: (row(b, n), CB_Q + g)),
        pl.BlockSpec((rows, GW), lambda b, n: (row(b, n), CB_K + g)),
        pl.BlockSpec((rows, GW), lambda b, n: (row(b, n), CB_V + g)),
        pl.BlockSpec((rows, GW), lambda b, n: (row(b, n), 0)),
        pl.BlockSpec((rows, 128), lambda b, n: (row(b, n), 0)),
        pl.BlockSpec((None, None, 4, QB, 2 * QB),
                     lambda b, n: (g, jnp.minimum(jnp.minimum(n, nsb - 1), 1), 0, 0, 0)),
    ]
    args = [proj, proj, proj, d_out, stats, bias]
    scratch = [pltpu.VMEM((rows, GW), BF16)] * 3 + [pltpu.SemaphoreType.DMA((3,))]
    if has_prev:
        in_specs += [pl.BlockSpec((rows, GW), lambda b, n: (prev(b, n), CB_K + g)),
                     pl.BlockSpec((rows, GW), lambda b, n: (prev(b, n), CB_V + g))]
        args += [proj, proj]
        scratch.append(pltpu.VMEM((rows, 3 * GW), F32))
    if strided:
        n_copied = (7 + (2 if has_prev else 0)) * n_sets
        scratch += [pltpu.VMEM((rows, 128), F32)] * (1 + n_copied)
    in_specs.append(pl.BlockSpec(memory_space=pl.ANY))
    args.append(dproj)
    return pl.pallas_call(
        body, name=f"attn_bwd{g}",
        grid=(BL, n_steps),
        in_specs=in_specs,
        out_specs=(pl.BlockSpec(memory_space=pl.ANY),
                   pl.BlockSpec((4, QB, 2 * QB), lambda b, n: (0, 0, 0))),
        out_shape=(jax.ShapeDtypeStruct((T, NCOL), BF16),
                   jax.ShapeDtypeStruct((4, QB, 2 * QB), F32)),
        scratch_shapes=scratch,
        input_output_aliases={len(args) - 1: 0},
        compiler_params=pltpu.CompilerParams(vmem_limit_bytes=VMEM_LIMIT),
    )(*args)


def _tail(x2, tgt2, mod3, o_g, lse_g, proj, w_ao, w_co, w_o, conv_w, conv_b, ln_g, ln_b):
    tm = 256
    per_seq = S // tm
    halo = 16

    def body(x_ref, t_ref, mod_ref, o1_ref, o2_ref, o3_ref, l1_ref, l2_ref, l3_ref,
             ga_ref, u_ref, bg_ref, cg_ref, gc_ref, ma_ref, mc_ref, up_ref, cp_ref,
             wao_ref, wco_ref, wo_ref, cw_ref, cb_ref, lg_ref, lb_ref,
             dproj_ref, dyc_ref, do_ref, st_ref, dxd_ref,
             mg_ref, dy_ref, ain_ref, dao_ref, sin_ref, dso_ref, vec_ref,
             dga_s, dbg_s, dgm_s, sems):
        i = pl.program_id(0)
        bidx = i // per_seq
        first = (i % per_seq) == 0

        @pl.when(i == 0)
        def _():
            vec_ref[...] = jnp.zeros_like(vec_ref)

        l1, l2, l3 = l1_ref[...], l2_ref[...], l3_ref[...]
        mx = jnp.maximum(jnp.maximum(l1, l2), l3)
        e1, e2, e3 = jnp.exp(l1 - mx), jnp.exp(l2 - mx), jnp.exp(l3 - mx)
        esum = e1 + e2 + e3
        lse_tot = mx + jnp.log(esum)
        w1, w2, w3 = e1 / esum, e2 / esum, e3 / esum

        def per_head(wv):
            return jnp.concatenate([jnp.broadcast_to(wv[:, h:h + 1], (tm, HD)) for h in range(4)], axis=1)

        o = per_head(w1) * o1_ref[...] + per_head(w2) * o2_ref[...] + per_head(w3) * o3_ref[...]

        ga = ga_ref[...].astype(F32)
        sig_ga = _sigmoid(ga)
        silu_ga = ga * sig_ga
        a_in = (o * silu_ga).astype(BF16)
        a_out = _dot(a_in, wao_ref[...])

        u = u_ref[...].astype(F32)
        cg = cg_ref[...].astype(F32)
        z = cg * u
        zp = cp_ref[...].astype(F32) * up_ref[...].astype(F32)
        zp = jnp.where(first, 0.0, zp)
        zcat = jnp.concatenate([zp, z], axis=0)
        z1 = pltpu.roll(zcat, 1, 0)[halo:]
        z2 = pltpu.roll(zcat, 2, 0)[halo:]
        y_conv = cw_ref[0:1, :] * z2 + cw_ref[1:2, :] * z1 + cw_ref[2:3, :] * z + cb_ref[...]
        gc = gc_ref[...].astype(F32)
        sig_gc = _sigmoid(gc)
        silu_gc = gc * sig_gc
        bg = bg_ref[...].astype(F32)
        s_in = (bg * y_conv * silu_gc).astype(BF16)
        s_out = _dot(s_in, wco_ref[...])

        sa = _sigmoid(ma_ref[...].astype(F32))
        sc = _sigmoid(mc_ref[...].astype(F32))
        merged = (sa * a_out + sc * s_out).astype(BF16)
        y = _dot(merged, wo_ref[...])
        gate1 = 1.0 + mod_ref[0, 2:3, :]
        xv = x_ref[...]
        resid = ALPHA * xv + gate1 * y
        mu = jnp.mean(resid, axis=1, keepdims=True)
        xc = resid - mu
        var = jnp.mean(xc * xc, axis=1, keepdims=True)
        rstd = lax.rsqrt(var + LN_EPS)
        xhat = xc * rstd
        lg = lg_ref[...]
        err = xhat * lg + lb_ref[...] - t_ref[...]
        vec_ref[3:4, :] += (0.5 / D) * jnp.sum(err * err, axis=0, keepdims=True)

        dout = err * (1.0 / D)
        vec_ref[1:2, :] += jnp.sum(dout * xhat, axis=0, keepdims=True)
        vec_ref[2:3, :] += jnp.sum(dout, axis=0, keepdims=True)
        dxh = dout * lg
        dres = rstd * (dxh - jnp.mean(dxh, axis=1, keepdims=True)
                       - xhat * jnp.mean(dxh * xhat, axis=1, keepdims=True))
        dxd_ref[...] = ALPHA * dres
        dgate = jnp.sum(dres * y, axis=0, keepdims=True)
        vec_ref[4:5, :] += jnp.where(bidx == 0, dgate, 0.0)
        vec_ref[5:6, :] += jnp.where(bidx == 1, dgate, 0.0)
        dy = (dres * gate1).astype(BF16)

        dmerged = _dot_nt(dy, wo_ref[...])
        da_out = (dmerged * sa).astype(BF16)
        ds_out = (dmerged * sc).astype(BF16)
        dgm_s[:, 2 * D:3 * D] =(dmerged * s_out * sc * (1.0 - sc)).astype(BF16)
        dgm_s[:, D:2 * D] =(dmerged * a_out * sa * (1.0 - sa)).astype(BF16)
        da_in = _dot_nt(da_out, wao_ref[...])
        ds_in = _dot_nt(ds_out, wco_ref[...])

        d_o = da_in * silu_ga
        do_ref[...] = d_o.astype(BF16)
        dga_s[...] =(da_in * o * (sig_ga * (1.0 + ga * (1.0 - sig_ga)))).astype(BF16)
        lane = lax.broadcasted_iota(jnp.int32, (tm, 128), 1)
        stats = lse_tot
        od = o * d_o
        for h in range(4):
            delta = jnp.sum(od[:, h * HD:(h + 1) * HD], axis=1, keepdims=True)
            stats = jnp.where(lane == 4 + h, delta, stats)
        st_ref[...] = stats

        dbg_s[...] =(ds_in * y_conv * silu_gc).astype(BF16)
        dyc = ds_in * bg * silu_gc
        dyc_ref[...] = dyc
        vec_ref[0:1, :] += jnp.sum(dyc, axis=0, keepdims=True)
        dgm_s[:, 0:D] =(ds_in * bg * y_conv * (sig_gc * (1.0 + gc * (1.0 - sig_gc)))).astype(BF16)

        mg_ref[...] = merged
        dy_ref[...] = dy
        ain_ref[...] = a_in
        dao_ref[...] = da_out
        sin_ref[...] = s_in
        dso_ref[...] = ds_out
        _write_columns([(dga_s, CB * CB_GA), (dbg_s, D * KB_BG), (dgm_s, D * KB_GC)],
                       dproj_ref, pl.multiple_of(i * tm, tm), sems)

    def tile(width, cblk=0):
        return pl.BlockSpec((tm, width), lambda i: (i, cblk))

    def whole(shape):
        return pl.BlockSpec(shape, lambda i: tuple(0 for _ in shape))

    prev_rows = lambda i: (jnp.maximum(i * (tm // halo) - 1, 0),)
    in_specs = [
        tile(D), tile(D), pl.BlockSpec((1, 3, D), lambda i: (i // per_seq, 0, 0)),
        tile(GW), tile(GW), tile(GW), tile(128), tile(128), tile(128),
        tile(GW, CB_GA), tile(D, KB_U), tile(D, KB_BG), tile(D, KB_CG), tile(D, KB_GC),
        tile(D, KB_MA), tile(D, KB_MC),
        pl.BlockSpec((halo, D), lambda i: (*prev_rows(i), KB_U)),
        pl.BlockSpec((halo, D), lambda i: (*prev_rows(i), KB_CG)),
        whole((GW, D)), whole((D, D)), whole((D, D)),
        whole((3, D)), whole((1, D)), whole((1, D)), whole((1, D)),
    ]
    out_specs = (
        pl.BlockSpec(memory_space=pl.ANY), tile(D), tile(GW), tile(128), tile(D),
        tile(D), tile(D), tile(GW), tile(D), tile(D), tile(D),
        pl.BlockSpec((8, D), lambda i: (0, 0)),
    )
    out_shape = (
        jax.ShapeDtypeStruct((T, NCOL), BF16),
        jax.ShapeDtypeStruct((T, D), F32),
        jax.ShapeDtypeStruct((T, GW), BF16),
        jax.ShapeDtypeStruct((T, 128), F32),
        jax.ShapeDtypeStruct((T, D), F32),
        jax.ShapeDtypeStruct((T, D), BF16),
        jax.ShapeDtypeStruct((T, D), BF16),
        jax.ShapeDtypeStruct((T, GW), BF16),
        jax.ShapeDtypeStruct((T, D), BF16),
        jax.ShapeDtypeStruct((T, D), BF16),
        jax.ShapeDtypeStruct((T, D), BF16),
        jax.ShapeDtypeStruct((8, D), F32),
    )
    return pl.pallas_call(
        body, name="tail",
        grid=(T // tm,),
        in_specs=in_specs, out_specs=out_specs, out_shape=out_shape,
        scratch_shapes=[pltpu.VMEM((tm, GW), BF16), pltpu.VMEM((tm, D), BF16), pltpu.VMEM((tm, 3 * D), BF16),
                        pltpu.SemaphoreType.DMA((3,))],
        compiler_params=pltpu.CompilerParams(vmem_limit_bytes=VMEM_LIMIT),
    )(x2, tgt2, mod3, *o_g, *lse_g, proj, proj, proj, proj, proj, proj, proj, proj, proj,
      w_ao, w_co, w_o, conv_w, conv_b, ln_g, ln_b)


def _conv_bwd(dyc, proj, conv_w, dproj):
    tm = 512
    per_seq = S // tm
    halo = 16

    def body(d_ref, dn_ref, u_ref, c_ref, up_ref, cp_ref, cw_ref, _, dproj_ref, g_ref, du_s, dc_s, sems):
        i = pl.program_id(0)
        first = (i % per_seq) == 0
        last = (i % per_seq) == per_seq - 1

        @pl.when(i == 0)
        def _():
            g_ref[...] = jnp.zeros_like(g_ref)

        d = d_ref[...]
        dn = jnp.where(last, 0.0, dn_ref[...])
        dcat = jnp.concatenate([d, dn], axis=0)
        d1 = pltpu.roll(dcat, tm + 8 - 1, 0)[:tm]
        d2 = pltpu.roll(dcat, tm + 8 - 2, 0)[:tm]
        dz = cw_ref[2:3, :] * d + cw_ref[1:2, :] * d1 + cw_ref[0:1, :] * d2
        u = u_ref[...].astype(F32)
        cg = c_ref[...].astype(F32)
        du_s[...] = (dz * cg).astype(BF16)
        dc_s[...] = (dz * u).astype(BF16)
        _write_columns([(du_s, D * KB_U), (dc_s, D * KB_CG)], dproj_ref, pl.multiple_of(i * tm, tm), sems)

        z = cg * u
        zp = jnp.where(first, 0.0, cp_ref[...].astype(F32) * up_ref[...].astype(F32))
        zcat = jnp.concatenate([zp, z], axis=0)
        z1 = pltpu.roll(zcat, 1, 0)[halo:]
        z2 = pltpu.roll(zcat, 2, 0)[halo:]
        g_ref[0:1, :] += jnp.sum(d * z2, axis=0, keepdims=True)
        g_ref[1:2, :] += jnp.sum(d * z1, axis=0, keepdims=True)
        g_ref[2:3, :] += jnp.sum(d * z, axis=0, keepdims=True)

    n_tiles = T // tm
    prev_rows = lambda i: jnp.maximum(i * (tm // halo) - 1, 0)
    next_rows = lambda i: jnp.minimum((i + 1) * (tm // 8), T // 8 - 1)
    return pl.pallas_call(
        body, name="conv_bwd",
        grid=(n_tiles,),
        in_specs=[pl.BlockSpec((tm, D), lambda i: (i, 0)),
                  pl.BlockSpec((8, D), lambda i: (next_rows(i), 0)),
                  pl.BlockSpec((tm, D), lambda i: (i, KB_U)),
                  pl.BlockSpec((tm, D), lambda i: (i, KB_CG)),
                  pl.BlockSpec((halo, D), lambda i: (prev_rows(i), KB_U)),
                  pl.BlockSpec((halo, D), lambda i: (prev_rows(i), KB_CG)),
                  pl.BlockSpec((3, D), lambda i: (0, 0)),
                  pl.BlockSpec(memory_space=pl.ANY)],
        out_specs=(pl.BlockSpec(memory_space=pl.ANY),
                   pl.BlockSpec((8, D), lambda i: (0, 0))),
        out_shape=(jax.ShapeDtypeStruct((T, NCOL), BF16),
                   jax.ShapeDtypeStruct((8, D), F32)),
        scratch_shapes=[pltpu.VMEM((tm, D), BF16), pltpu.VMEM((tm, D), BF16), pltpu.SemaphoreType.DMA((2,))],
        input_output_aliases={7: 0},
        compiler_params=pltpu.CompilerParams(vmem_limit_bytes=VMEM_LIMIT),
    )(dyc, dyc, proj, proj, proj, proj, conv_w, dproj)


def _dh_dx(dproj, w_in_all, x2, dxd, mod3, chip_sums):
    tm = 1024
    per_seq = S // tm
    n = len(chip_sums)

    def body(*refs):
        d_ref, w_ref, x_ref, dxd_ref, mod_ref = refs[:5]
        ins = refs[5:5 + n]
        gx_ref, vec_ref = refs[5 + n:7 + n]
        outs = refs[7 + n:7 + 2 * n]
        acc, send_sems, recv_sems, local_sems = refs[7 + 2 * n:]
        i, jj = pl.program_id(0), pl.program_id(1)

        @pl.when((i == 0) & (jj == 0))
        def _():
            vec_ref[...] = jnp.zeros_like(vec_ref)
            if n:
                sends, _, mine = _chip_copies(ins, outs, send_sems, recv_sems, local_sems)
                for cp in sends + mine:
                    cp.start()

        if n:
            @pl.when((i == T // tm - 1) & (jj == N_DEV - 1))
            def _():
                sends, arrivals, mine = _chip_copies(ins, outs, send_sems, recv_sems, local_sems)
                for cp in arrivals:
                    cp.wait_recv()
                for cp in sends:
                    cp.wait_send()
                for cp in mine:
                    cp.wait()

        @pl.when(jj == 0)
        def _():
            acc[...] = jnp.zeros_like(acc)

        acc[...] += _dot_nt(d_ref[...], w_ref[...])

        @pl.when(jj == N_DEV - 1)
        def _():
            dh = acc[...]
            bidx = i // per_seq
            gx_ref[...] = dxd_ref[...] + dh * (1.0 + mod_ref[0, 1:2, :])
            dshift = jnp.sum(dh, axis=0, keepdims=True)
            dscale = jnp.sum(dh * x_ref[...], axis=0, keepdims=True)
            vec_ref[0:1, :] += jnp.where(bidx == 0, dshift, 0.0)
            vec_ref[1:2, :] += jnp.where(bidx == 1, dshift, 0.0)
            vec_ref[2:3, :] += jnp.where(bidx == 0, dscale, 0.0)
            vec_ref[3:4, :] += jnp.where(bidx == 1, dscale, 0.0)

    any_spec = pl.BlockSpec(memory_space=pl.ANY)
    res = pl.pallas_call(
        body, name="dh_dx",
        grid=(T // tm, N_DEV),
        in_specs=[
            pl.BlockSpec((tm, SHARD), lambda i, jj: (i, jj)),
            pl.BlockSpec((None, D, SHARD), lambda i, jj: (jj, 0, 0)),
            pl.BlockSpec((tm, D), lambda i, jj: (i, 0)),
            pl.BlockSpec((tm, D), lambda i, jj: (i, 0)),
            pl.BlockSpec((1, 3, D), lambda i, jj: (i // per_seq, 0, 0))] + [any_spec] * n,
        out_specs=(pl.BlockSpec((tm, D), lambda i, jj: (i, 0)),
                   pl.BlockSpec((8, D), lambda i, jj: (0, 0))) + (any_spec,) * n,
        out_shape=(jax.ShapeDtypeStruct((T, D), F32), jax.ShapeDtypeStruct((8, D), F32))
                  + tuple(jax.ShapeDtypeStruct(a.shape, a.dtype) for a in chip_sums),
        scratch_shapes=[pltpu.VMEM((tm, D), F32), pltpu.SemaphoreType.DMA((max(3 * n, 1),)),
                        pltpu.SemaphoreType.DMA((max(3 * n, 1),)), pltpu.SemaphoreType.DMA((max(n, 1),))],
        compiler_params=pltpu.CompilerParams(vmem_limit_bytes=VMEM_LIMIT),
    )(dproj, w_in_all, x2, dxd, mod3, *chip_sums)
    return res[0], res[1], res[2:]


def _mm_tn(a, b, tn, blocks_leading, name):
    kk, m = a.shape
    n = b.shape[1]
    tk = 2048

    def body(a_ref, b_ref, o_ref, acc):
        @pl.when(pl.program_id(1) == 0)
        def _():
            acc[...] = jnp.zeros_like(acc)

        acc[...] += _dot_tn(a_ref[...], b_ref[...])

        @pl.when(pl.program_id(1) == kk // tk - 1)
        def _():
            o_ref[...] = acc[...].astype(BF16)

    if blocks_leading:
        out_spec = pl.BlockSpec((None, m, tn), lambda j, k: (j, 0, 0))
        out_shape = jax.ShapeDtypeStruct((n // tn, m, tn), BF16)
    else:
        out_spec = pl.BlockSpec((m, tn), lambda j, k: (0, j))
        out_shape = jax.ShapeDtypeStruct((m, n), BF16)
    return pl.pallas_call(
        body, name=name,
        grid=(n // tn, kk // tk),
        in_specs=[pl.BlockSpec((tk, m), lambda j, k: (k, 0)),
                  pl.BlockSpec((tk, tn), lambda j, k: (k, j))],
        out_specs=out_spec, out_shape=out_shape,
        scratch_shapes=[pltpu.VMEM((m, tn), F32)],
        compiler_params=pltpu.CompilerParams(vmem_limit_bytes=VMEM_LIMIT),
    )(a, b)


def _adamw(parts, w, m, v, name, row_tile=None):
    n_parts, rows, cols = parts.shape
    tr = rows if row_tile is None else row_tile
    c1 = 1.0 - ADAM_B1 ** ADAM_STEP
    c2 = 1.0 - ADAM_B2 ** ADAM_STEP

    def body(p_ref, w_ref, m_ref, v_ref, g_ref, d_ref, nm_ref, nv_ref):
        g = p_ref[0].astype(F32)
        for s in range(1, n_parts):
            g = g + p_ref[s].astype(F32)
        nm = ADAM_B1 * m_ref[...] + (1.0 - ADAM_B1) * g
        nv = ADAM_B2 * v_ref[...] + (1.0 - ADAM_B2) * (g * g)
        m_hat = nm / c1
        v_hat = nv / c2
        g_ref[...] = g
        d_ref[...] = -ADAM_LR * (m_hat / (jnp.sqrt(v_hat) + ADAM_EPS) + ADAM_WD * w_ref[...])
        nm_ref[...] = nm
        nv_ref[...] = nv

    blk = pl.BlockSpec((tr, cols), lambda i: (i, 0))
    shp = jax.ShapeDtypeStruct((rows, cols), F32)
    return pl.pallas_call(
        body, name=name,
        grid=(rows // tr,),
        in_specs=[pl.BlockSpec((n_parts, tr, cols), lambda i: (0, i, 0)), blk, blk, blk],
        out_specs=(blk, blk, blk, blk),
        out_shape=(shp, shp, shp, shp),
        compiler_params=pltpu.CompilerParams(vmem_limit_bytes=VMEM_LIMIT),
    )(parts, w, m, v)


def _loss_sum(rows):
    def body(r_ref, o_ref):
        o_ref[...] = jnp.sum(jnp.sum(r_ref[...], axis=0, keepdims=True), axis=1, keepdims=True)

    return pl.pallas_call(body, name="loss_sum", out_shape=jax.ShapeDtypeStruct((1, 1), F32))(rows)


def _local_step(x2, tgt2, mod3, h, proj, w_ao, w_co, w_o, conv_w, conv_b, rel_bias, ln_g, ln_b):
    buckets_np, masks_np = _bucket_maps()
    buckets, masks = jnp.asarray(buckets_np), jnp.asarray(masks_np)

    bias = _bias_expand(rel_bias, buckets, masks)
    fwd = [_attn_fwd(proj, bias, g) for g in range(3)]
    o_g = [f[0] for f in fwd]
    lse_g = [f[1] for f in fwd]

    (dproj, dyc, d_o, stats, dxd, merged, dy, a_in, da_out, s_in, ds_out, tail_vec) = _tail(
        x2, tgt2, mod3, o_g, lse_g, proj, w_ao, w_co, w_o, conv_w, conv_b, ln_g, ln_b)

    dbias = []
    for g in range(3):
        dproj, db = _attn_bwd(proj, d_o, stats, bias, dproj, g)
        dbias.append(db)
    g_rel_bias = _bias_grad(*dbias, buckets)
    dproj, conv_vec = _conv_bwd(dyc, proj, conv_w, dproj)

    gw_in = _mm_tn(h, dproj, SHARD, True, "gw_in")
    gw_o = _mm_tn(merged, dy, D, False, "gw_o")
    gw_co = _mm_tn(s_in, ds_out, D, False, "gw_conv_out")
    gw_ao = _mm_tn(a_in, da_out, D, False, "gw_attn_out")
    gw_ao = jnp.transpose(gw_ao.reshape(GW, N_DEV, D // N_DEV), (1, 0, 2))
    return dproj, dxd, gw_in, gw_ao, gw_co, gw_o, conv_vec, g_rel_bias, tail_vec


def kernel(x, c, w_ada, b_ada, w_in, conv_w, conv_b, rel_bias, w_attn_out, w_conv_out, w_o, ln_g, ln_b, loss_target, m_w_ada, m_b_ada, m_w_in, m_conv_w, m_conv_b, m_rel_bias, m_w_attn_out, m_w_conv_out, m_w_o, m_ln_g, m_ln_b, v_w_ada, v_b_ada, v_w_in, v_conv_w, v_conv_b, v_rel_bias, v_w_attn_out, v_w_conv_out, v_w_o, v_ln_g, v_ln_b):
    me = _my_index()
    x2 = x.reshape(T, D)
    tgt2 = loss_target.reshape(T, D)

    b_cols = lax.dynamic_slice(b_ada, (0, me * ADA_SHARD), (1, ADA_SHARD))
    c_g, mod_in = _mod_exchange(jnp.pad(c, ((0, 8 - BL), (0, 0))), w_ada[0], b_cols)
    c_all = c_g[:, 0:BL, :].reshape(N_DEV * BL, D)
    mod3 = jnp.transpose(mod_in[:, 0:BL, :], (1, 0, 2)).reshape(BL, 3, D)

    w_ao_g, w_co_g, w_o_g, conv_w_g = _all_gather(
        [w_attn_out[0].astype(BF16), w_conv_out[0].astype(BF16), w_o[0].astype(BF16), conv_w[0]],
        [jax.ShapeDtypeStruct((N_DEV, GW, D // N_DEV), BF16),
         jax.ShapeDtypeStruct((N_DEV, D // N_DEV, D), BF16),
         jax.ShapeDtypeStruct((N_DEV, D // N_DEV, D), BF16),
         jax.ShapeDtypeStruct((N_DEV, 3, D // N_DEV), F32)],
        [_slot_leading] * 4,
        "gather_weights")
    w_ao_full = jnp.transpose(w_ao_g, (1, 0, 2)).reshape(GW, D)
    w_co_full = w_co_g.reshape(D, D)
    w_o_full = w_o_g.reshape(D, D)
    conv_w_full = jnp.transpose(conv_w_g, (1, 0, 2)).reshape(3, D)

    h = _prep_h(x2, mod3)
    proj, w_in_all = _gather_proj(_shard_order(), h, w_in[0].astype(BF16), 1024)
    (dproj, dxd, gw_in, gw_ao, gw_co, gw_o, conv_vec, g_rel_bias, tail_vec) = _local_step(
        x2, tgt2, mod3, h, proj, w_ao_full, w_co_full, w_o_full,
        conv_w_full, conv_b, rel_bias, ln_g, ln_b)

    g_conv_w_blocks = jnp.transpose(conv_vec[0:3].reshape(3, N_DEV, D // N_DEV), (1, 0, 2))
    partials = [gw_in, gw_ao, gw_co.reshape(N_DEV, D // N_DEV, D), gw_o.reshape(N_DEV, D // N_DEV, D),
                g_conv_w_blocks]
    sib = _pair_exchange(
        partials,
        [jax.ShapeDtypeStruct((4, D, SHARD), BF16),
         jax.ShapeDtypeStruct((4, GW, D // N_DEV), BF16),
         jax.ShapeDtypeStruct((4, D // N_DEV, D), BF16),
         jax.ShapeDtypeStruct((4, D // N_DEV, D), BF16),
         jax.ShapeDtypeStruct((4, 3, D // N_DEV), F32)],
        [_slot_leading] * 5,
        "pair_grads")
    tiles = [256, None, None, None, None]
    names = ["w_in", "w_attn_out", "w_conv_out", "w_o", "conv_w"]
    core = lax.axis_index("c").astype(jnp.int32).reshape(1)
    chip_sums = [_pair_add(core, partials[a], sib[a], tiles[a], "pair_add_" + names[a]) for a in range(5)]
    grad_x, mod_vec, (r_in, r_ao, r_co, r_o, r_cw) = _dh_dx(dproj, w_in_all, x2, dxd, mod3, chip_sums)

    small = jnp.concatenate([
        tail_vec[0:4],
        jnp.pad(g_rel_bias.reshape(1, N_BUCKETS * N_HEADS), ((0, 0), (0, D - N_BUCKETS * N_HEADS))),
        jnp.zeros((3, D), F32)], axis=0)
    dmod = jnp.concatenate([mod_vec[0:2], mod_vec[2:4], tail_vec[4:6]], axis=1)
    small_g, dmod_g = _all_gather(
        [small, dmod],
        [jax.ShapeDtypeStruct((N_DEV, 8, D), F32), jax.ShapeDtypeStruct((N_DEV, BL, 3 * D), F32)],
        [_slot_leading] * 2, "gather_small")
    dmod_all = dmod_g.reshape(N_DEV * BL, 3 * D)
    loss = _loss_sum(small_g[:, 3, :]).reshape(())
    g_w_ada = _ada_bwd(jnp.transpose(c_all), lax.dynamic_slice(dmod_all, (0, me * ADA_SHARD),
                                                               (N_DEV * BL, ADA_SHARD)))

    def upd(parts, w, m, v, name, row_tile=None):
        shape = w.shape
        w2, m2, v2 = (t.reshape(parts.shape[1:]) for t in (w, m, v))
        return tuple(t.reshape(shape) for t in _adamw(parts, w2, m2, v2, name, row_tile))

    res = {
        "w_ada": upd(g_w_ada[None], w_ada, m_w_ada, v_w_ada, "adam_w_ada", 256),
        "b_ada": upd(dmod_all[:, None, :], b_ada, m_b_ada, v_b_ada, "adam_b_ada"),
        "w_in": upd(r_in, w_in, m_w_in, v_w_in, "adam_w_in", 128),
        "conv_w": upd(r_cw, conv_w, m_conv_w, v_conv_w, "adam_conv_w"),
        "conv_b": upd(small_g[:, 0:1, :], conv_b, m_conv_b, v_conv_b, "adam_conv_b"),
        "rel_bias": upd(small_g[:, 4, :N_BUCKETS * N_HEADS].reshape(N_DEV, N_BUCKETS, N_HEADS),
                        rel_bias, m_rel_bias, v_rel_bias, "adam_rel_bias"),
        "w_attn_out": upd(r_ao, w_attn_out, m_w_attn_out, v_w_attn_out, "adam_w_attn_out"),
        "w_conv_out": upd(r_co, w_conv_out, m_w_conv_out, v_w_conv_out, "adam_w_conv_out"),
        "w_o": upd(r_o, w_o, m_w_o, v_w_o, "adam_w_o"),
        "ln_g": upd(small_g[:, 1:2, :], ln_g, m_ln_g, v_ln_g, "adam_ln_g"),
        "ln_b": upd(small_g[:, 2:3, :], ln_b, m_ln_b, v_ln_b, "adam_ln_b"),
    }
    order = ["w_ada", "b_ada", "w_in", "conv_w", "conv_b", "rel_bias", "w_attn_out", "w_conv_out",
             "w_o", "ln_g", "ln_b"]
    outs = [loss, grad_x.reshape(BL, S, D)]
    for k in range(4):
        outs += [res[name][k] for name in order]
    return tuple(outs)
```

```python
import functools
import math

import numpy as np
import jax
import jax.numpy as jnp
from jax import lax
from jax.experimental import pallas as pl
from jax.experimental.pallas import tpu as pltpu

F32 = jnp.float32
BF16 = jnp.bfloat16
MESH = pl.DeviceIdType.MESH

N_DEV = 8
D = 1024
S = 2048
BL = 2
T = BL * S
NCOL = 11264
SHARD = NCOL // N_DEV
CB = 512
NCB = NCOL // CB
HD = 128
GW = 512
QB = 128
DILATIONS = (1, 4, 16)
N_STEPS = 128
N_BUCKETS = 32
N_HEADS = 12
ALPHA = 2.0 ** 0.25
LN_EPS = 1e-5
NEG_INF = -1e30
SCALE = HD ** -0.5
ADA_SHARD = 3 * D // N_DEV

CB_Q, CB_K, CB_V, CB_GA = 0, 3, 6, 9
KB_U, KB_BG, KB_CG, KB_GC, KB_MA, KB_MC = 5, 6, 7, 8, 9, 10

ADAM_LR, ADAM_B1, ADAM_B2, ADAM_EPS, ADAM_WD, ADAM_STEP = 0.001, 0.9, 0.999, 1e-08, 0.01, 10

VMEM_LIMIT = 56 * 1024 * 1024


def _dot(a, b):
    return jnp.dot(a, b, preferred_element_type=F32)


def _dot_nt(a, b):
    return lax.dot_general(a, b, (((1,), (1,)), ((), ())), preferred_element_type=F32)


def _dot_tn(a, b):
    return lax.dot_general(a, b, (((0,), (0,)), ((), ())), preferred_element_type=F32)


def _sigmoid(v):
    return 1.0 / (1.0 + jnp.exp(-v))


def _write_columns(pieces, dst_hbm, row0, sems):
    copies = []
    for k, (src, col0) in enumerate(pieces):
        rows, width = src.shape
        copies.append(pltpu.make_async_copy(
            src, dst_hbm.at[pl.ds(row0, rows), pl.ds(col0, width)], sems.at[k]))
    for cp in copies:
        cp.start()
    for cp in copies:
        cp.wait()


def _my_index():
    return 4 * lax.axis_index("x") + 2 * lax.axis_index("y") + lax.axis_index("c")


def _slot_leading(ref, slot):
    return ref.at[slot]


def _all_gather(arrs, out_shapes, slot_fns, name):
    n = len(arrs)

    def body(*refs):
        ins, outs = refs[:n], refs[n:2 * n]
        send_sems, recv_sems, local_sems = refs[2 * n:2 * n + 3]
        stage = refs[2 * n + 3:]
        x, y, c = lax.axis_index("x"), lax.axis_index("y"), lax.axis_index("c")
        me, sibling = (x, y, c), (x, y, 1 - c)
        chips = [(1 - x, y), (x, 1 - y), (1 - x, 1 - y)]

        def blk(a, dev):
            return slot_fns[a](outs[a], 4 * dev[0] + 2 * dev[1] + dev[2])

        def copy(a, k, block, to, src=None):
            dst = blk(a, block)
            return pltpu.make_async_remote_copy(
                src_ref=dst if src is None else src, dst_ref=dst,
                send_sem=send_sems.at[a * 7 + k], recv_sem=recv_sems.at[a * 7 + k],
                device_id=to, device_id_type=MESH)

        first = []
        for a in range(n):
            first.append(copy(a, 0, me, sibling, src=ins[a]))
            first += [copy(a, 1 + j, me, (*chip, c), src=ins[a]) for j, chip in enumerate(chips)]
        for cp in first:
            cp.start()
        loads = [pltpu.make_async_copy(ins[a], stage[a], local_sems.at[a]) for a in range(n)]
        for cp in loads:
            cp.start()
        for cp in loads:
            cp.wait()
        mine = [pltpu.make_async_copy(stage[a], blk(a, me), local_sems.at[a]) for a in range(n)]
        for cp in mine:
            cp.start()
        passed = []
        for j, chip in enumerate(chips):
            for a in range(n):
                copy(a, 1 + j, (*chip, c), me).wait_recv()
                fwd = copy(a, 4 + j, (*chip, c), sibling)
                fwd.start()
                passed.append(fwd)
        for a in range(n):
            copy(a, 0, sibling, me).wait_recv()
        for j, chip in enumerate(chips):
            for a in range(n):
                copy(a, 4 + j, (*chip, 1 - c), me).wait_recv()
        for cp in first + passed:
            cp.wait_send()
        for cp in mine:
            cp.wait()

    any_spec = pl.BlockSpec(memory_space=pl.ANY)
    return pl.pallas_call(
        body, name=name,
        out_shape=tuple(out_shapes),
        in_specs=[any_spec] * n,
        out_specs=tuple([any_spec] * n),
        scratch_shapes=[pltpu.SemaphoreType.DMA((7 * n,)), pltpu.SemaphoreType.DMA((7 * n,)),
                        pltpu.SemaphoreType.DMA((n,))]
                       + [pltpu.VMEM(a.shape, a.dtype) for a in arrs],
    )(*arrs)


def _slice_order():
    c = lax.axis_index("c")
    slots = []
    for q in range(4):
        slots += [2 * q + 1 - c, 2 * q + c]
    return jnp.stack(slots).astype(jnp.int32)


def _gw_in_pair(order, h, dproj, smalls, small_shapes4):
    kk, m = h.shape
    tk = min(kk, 2048)
    nk = kk // tk
    ncols = dproj.shape[1] // N_DEV
    n = len(smalls)

    def body(order_ref, h_ref, d_ref, *rest):
        ins = rest[:n]
        sums_hbm = rest[n]
        sib = rest[n + 1:2 * n + 1]
        acc, sendbuf, recvbuf, sumbuf, send_sems, recv_sems, local_sem, ssend, srecv = rest[2 * n + 1:]
        js, k = pl.program_id(0), pl.program_id(1)
        x, y, c = lax.axis_index("x"), lax.axis_index("y"), lax.axis_index("c")
        sibling = (x, y, 1 - c)
        pair = js // 2
        to_sibling = (js % 2) == 0

        def small_copies():
            return [pltpu.make_async_remote_copy(
                        src_ref=ins[a].at[2 * q + 1 - c], dst_ref=sib[a].at[q],
                        send_sem=ssend.at[a * 4 + q], recv_sem=srecv.at[a * 4 + q],
                        device_id=sibling, device_id_type=MESH)
                    for a in range(n) for q in range(4)]

        def slice_copy(p):
            return pltpu.make_async_remote_copy(
                src_ref=sendbuf, dst_ref=recvbuf.at[p], send_sem=send_sems.at[p], recv_sem=recv_sems.at[p],
                device_id=sibling, device_id_type=MESH)

        def sum_copy(p):
            return pltpu.make_async_copy(sumbuf, sums_hbm.at[p], local_sem)

        @pl.when((js == 0) & (k == 0))
        def _():
            for cp in small_copies():
                cp.start()

        @pl.when(k == 0)
        def _():
            acc[...] = jnp.zeros_like(acc)

        acc[...] += _dot_tn(h_ref[...], d_ref[...])

        for p in range(4):
            @pl.when((js == 2 * p) & (k == nk - 1))
            def _():
                if p > 0:
                    slice_copy(p - 1).wait_send()
                sendbuf[...] = acc[...].astype(BF16)
                slice_copy(p).start()

            @pl.when((js == 2 * p + 1) & (k == nk - 1))
            def _():
                slice_copy(p).wait_recv()
                if p > 0:
                    sum_copy(p - 1).wait()
                sumbuf[...] = (acc[...] + recvbuf[p].astype(F32)).astype(BF16)
                sum_copy(p).start()

        @pl.when((js == N_DEV - 1) & (k == nk - 1))
        def _():
            slice_copy(3).wait_send()
            sum_copy(3).wait()
            for cp in small_copies():
                cp.wait()

    any_spec = pl.BlockSpec(memory_space=pl.ANY)
    res = pl.pallas_call(
        body, name="gw_in_pair",
        grid_spec=pltpu.PrefetchScalarGridSpec(
            num_scalar_prefetch=1,
            grid=(N_DEV, nk),
            in_specs=[pl.BlockSpec((tk, m), lambda js, k, order_ref: (k, 0)),
                      pl.BlockSpec((tk, ncols), lambda js, k, order_ref: (k, order_ref[js]))] + [any_spec] * n,
            out_specs=(any_spec,) * (n + 1),
            scratch_shapes=[pltpu.VMEM((m, ncols), F32), pltpu.VMEM((m, ncols), BF16),
                            pltpu.VMEM((4, m, ncols), BF16), pltpu.VMEM((m, ncols), BF16),
                            pltpu.SemaphoreType.DMA((4,)), pltpu.SemaphoreType.DMA((4,)),
                            pltpu.SemaphoreType.DMA,
                            pltpu.SemaphoreType.DMA((4 * n,)), pltpu.SemaphoreType.DMA((4 * n,))]),
        out_shape=(jax.ShapeDtypeStruct((4, m, ncols), BF16),) + tuple(small_shapes4),
        compiler_params=pltpu.CompilerParams(vmem_limit_bytes=VMEM_LIMIT),
    )(order, h, dproj, *smalls)
    return res[0], res[1:]


def _chip_copies(ins, outs, send_sems, recv_sems, local_sems):
    n = len(ins)
    x, y, c = lax.axis_index("x"), lax.axis_index("y"), lax.axis_index("c")
    my_chip = 2 * x + y

    def peer_of(k):
        return ((1 - x) if (k >> 1) & 1 else x, (1 - y) if k & 1 else y, c)

    def copy(a, k, out_chip):
        peer = peer_of(k)
        return pltpu.make_async_remote_copy(
            src_ref=ins[a].at[2 * peer[0] + peer[1]], dst_ref=outs[a].at[out_chip],
            send_sem=send_sems.at[a * 3 + k - 1], recv_sem=recv_sems.at[a * 3 + k - 1],
            device_id=peer, device_id_type=MESH)

    sends = [copy(a, k, my_chip) for k in range(1, 4) for a in range(n)]
    arrivals = []
    for k in range(1, 4):
        peer = peer_of(k)
        arrivals += [copy(a, k, 2 * peer[0] + peer[1]) for a in range(n)]
    mine = [pltpu.make_async_copy(ins[a].at[my_chip], outs[a].at[my_chip], local_sems.at[a])
            for a in range(n)]
    return sends, arrivals, mine


def _pair_add(core, mine, theirs, row_tile, name):
    _, rows, cols = theirs.shape
    tr = rows if row_tile is None else row_tile

    def body(core_ref, a_ref, b_ref, o_ref):
        o_ref[...] = (a_ref[...].astype(F32) + b_ref[...].astype(F32)).astype(o_ref.dtype)

    blk = pl.BlockSpec((None, tr, cols), lambda q, i, core_ref: (q, i, 0))
    return pl.pallas_call(
        body, name=name,
        grid_spec=pltpu.PrefetchScalarGridSpec(
            num_scalar_prefetch=1,
            grid=(4, rows // tr),
            in_specs=[pl.BlockSpec((None, tr, cols), lambda q, i, core_ref: (2 * q + core_ref[0], i, 0)), blk],
            out_specs=blk),
        out_shape=jax.ShapeDtypeStruct(theirs.shape, theirs.dtype),
    )(core, mine, theirs)


def _mod_exchange(c8, w_ada, b_cols):
    cols = w_ada.shape[1]

    def body(c_ref, w_ref, b_ref, call_ref, mod_ref, msend, send1, recv1, send2, recv2):
        x, y, c = lax.axis_index("x"), lax.axis_index("y"), lax.axis_index("c")
        my_slot = 4 * x + 2 * y + c

        def peer_of(k):
            return ((1 - x) if (k >> 2) & 1 else x, (1 - y) if (k >> 1) & 1 else y, (1 - c) if k & 1 else c)

        def slot_of(dev):
            return 4 * dev[0] + 2 * dev[1] + dev[2]

        def exchange(src_of, dst_ref, send_sems, recv_sems):
            sends, arrivals = [], []
            for k in range(1, 8):
                peer = peer_of(k)
                sends.append(pltpu.make_async_remote_copy(
                    src_ref=src_of(slot_of(peer)), dst_ref=dst_ref.at[my_slot],
                    send_sem=send_sems.at[k - 1], recv_sem=recv_sems.at[k - 1],
                    device_id=peer, device_id_type=MESH))
                arrivals.append(pltpu.make_async_remote_copy(
                    src_ref=src_of(my_slot), dst_ref=dst_ref.at[slot_of(peer)],
                    send_sem=send_sems.at[k - 1], recv_sem=recv_sems.at[k - 1],
                    device_id=peer, device_id_type=MESH))
            for cp in sends:
                cp.start()
            for cp in arrivals:
                cp.wait_recv()
            for cp in sends:
                cp.wait_send()

        call_ref[my_slot] = c_ref[...]
        exchange(lambda s: c_ref, call_ref, send1, recv1)
        cv = call_ref[...].reshape(N_DEV * 8, c_ref.shape[1])
        act = cv * _sigmoid(cv)
        mod = jnp.dot(act, w_ref[...], preferred_element_type=F32,
                      precision=lax.Precision.HIGHEST) + b_ref[...]
        msend[...] = mod.reshape(N_DEV, 8, cols)
        mod_ref[my_slot] = msend[my_slot]
        exchange(lambda s: msend.at[s], mod_ref, send2, recv2)

    return pl.pallas_call(
        body, name="mod_exchange",
        out_shape=(jax.ShapeDtypeStruct((N_DEV, 8, c8.shape[1]), F32),
                   jax.ShapeDtypeStruct((N_DEV, 8, cols), F32)),
        scratch_shapes=[pltpu.VMEM((N_DEV, 8, cols), F32)] + [pltpu.SemaphoreType.DMA((7,))] * 4,
    )(c8, w_ada, b_cols)


def _ada_bwd(c_all_t, dmod_cols):
    def body(c_ref, d_ref, o_ref):
        cv = c_ref[...]
        sc = cv * _sigmoid(cv)
        o_ref[...] = jnp.dot(sc, d_ref[...], preferred_element_type=F32,
                             precision=lax.Precision.HIGHEST)

    return pl.pallas_call(
        body, name="ada_bwd",
        out_shape=jax.ShapeDtypeStruct((c_all_t.shape[0], dmod_cols.shape[1]), F32),
    )(c_all_t, dmod_cols)


def _prep_h(x2, mod3):
    ts = 512
    per_seq = S // ts

    def body(x_ref, mod_ref, h_ref):
        shift = mod_ref[0, 0:1, :]
        scale = mod_ref[0, 1:2, :]
        h_ref[...] = (x_ref[...] * (1.0 + scale) + shift).astype(BF16)

    return pl.pallas_call(
        body, name="prep_h",
        grid=(T // ts,),
        in_specs=[pl.BlockSpec((ts, D), lambda i: (i, 0)),
                  pl.BlockSpec((1, 3, D), lambda i: (i // per_seq, 0, 0))],
        out_specs=pl.BlockSpec((ts, D), lambda i: (i, 0)),
        out_shape=jax.ShapeDtypeStruct((T, D), BF16),
    )(x2, mod3)


def _shard_order():
    x, y, c = lax.axis_index("x"), lax.axis_index("y"), lax.axis_index("c")
    devs = [(x, y, c), (x, y, 1 - c)]
    for chip in [(1 - x, y), (x, 1 - y), (1 - x, 1 - y)]:
        devs += [(*chip, c), (*chip, 1 - c)]
    return jnp.stack([4 * d[0] + 2 * d[1] + d[2] for d in devs]).astype(jnp.int32)


def _gather_proj(order, h, w_shard, tm):
    rows, kdim = h.shape
    ncols = w_shard.shape[1]
    n_i = rows // tm

    def body(order_ref, h_ref, mine_hbm, o_ref, all_hbm, wv, send_sems, recv_sems, local_sems):
        j, i = pl.program_id(0), pl.program_id(1)
        x, y, c = lax.axis_index("x"), lax.axis_index("y"), lax.axis_index("c")
        me, sibling = (x, y, c), (x, y, 1 - c)
        chips = [(1 - x, y), (x, 1 - y), (1 - x, 1 - y)]

        def slot(dev):
            return 4 * dev[0] + 2 * dev[1] + dev[2]

        def copy(k, block, to):
            return pltpu.make_async_remote_copy(
                src_ref=wv.at[slot(block)], dst_ref=wv.at[slot(block)],
                send_sem=send_sems.at[k], recv_sem=recv_sems.at[k],
                device_id=to, device_id_type=MESH)

        def keep(step, block):
            return pltpu.make_async_copy(wv.at[slot(block)], all_hbm.at[slot(block)], local_sems.at[step])

        first = [copy(0, me, sibling)] + [copy(1 + q, me, (*chip, c)) for q, chip in enumerate(chips)]
        passed = [copy(4 + q, (*chip, c), sibling) for q, chip in enumerate(chips)]
        due = [(me, None, None), (sibling, copy(0, sibling, me), None)]
        for q, chip in enumerate(chips):
            due.append(((*chip, c), copy(1 + q, (*chip, c), me), passed[q]))
            due.append(((*chip, 1 - c), copy(4 + q, (*chip, 1 - c), me), None))

        @pl.when((j == 0) & (i == 0))
        def _():
            load = pltpu.make_async_copy(mine_hbm, wv.at[slot(me)], local_sems.at[N_DEV])
            load.start()
            load.wait()
            for cp in first:
                cp.start()
            keep(0, me).start()

        for step in range(1, N_DEV):
            block, arrival, forward = due[step]

            @pl.when((j == step) & (i == 0))
            def _():
                arrival.wait_recv()
                if forward is not None:
                    forward.start()
                keep(step, block).start()

        o_ref[...] = _dot(h_ref[...], wv[order_ref[j]]).astype(BF16)

        @pl.when((j == N_DEV - 1) & (i == n_i - 1))
        def _():
            for cp in first + passed:
                cp.wait_send()
            for step in range(N_DEV):
                keep(step, due[step][0]).wait()

    return pl.pallas_call(
        body, name="gather_proj",
        grid_spec=pltpu.PrefetchScalarGridSpec(
            num_scalar_prefetch=1,
            grid=(N_DEV, n_i),
            in_specs=[pl.BlockSpec((tm, kdim), lambda j, i, order_ref: (i, 0)),
                      pl.BlockSpec(memory_space=pl.ANY)],
            out_specs=(pl.BlockSpec((tm, ncols), lambda j, i, order_ref: (i, order_ref[j])),
                       pl.BlockSpec(memory_space=pl.ANY)),
            scratch_shapes=[pltpu.VMEM((N_DEV, kdim, ncols), BF16),
                            pltpu.SemaphoreType.DMA((7,)), pltpu.SemaphoreType.DMA((7,)),
                            pltpu.SemaphoreType.DMA((N_DEV + 1,))]),
        out_shape=(jax.ShapeDtypeStruct((rows, N_DEV * ncols), BF16),
                   jax.ShapeDtypeStruct((N_DEV, kdim, ncols), BF16)),
        compiler_params=pltpu.CompilerParams(vmem_limit_bytes=VMEM_LIMIT),
    )(order, h, w_shard)


def _bucket_maps():
    a = np.arange(QB)[:, None]
    b = np.arange(2 * QB)[None, :]
    steps = a + QB - b
    maps = []
    for dil in DILATIONS:
        dist = np.maximum(steps, 0) * dil
        nf = np.maximum(dist, 1).astype(np.float32)
        large = 16 + (np.log(nf / np.float32(16)) / np.float32(math.log(128.0))
                      * np.float32(16)).astype(np.int32)
        large = np.minimum(large, N_BUCKETS - 1)
        maps.append(np.where(dist < 16, dist, large).astype(np.int32))
    band = (steps >= 0) & (steps <= N_STEPS)
    first = band & (b >= QB)
    masks = np.stack([first, band]).astype(np.int32)
    return np.stack(maps), masks


def _bias_expand(rel_bias, buckets, masks):
    def body(tab_ref, bk_ref, mk_ref, o_ref):
        for g in range(3):
            bk = bk_ref[g]
            for h in range(4):
                col = 4 * g + h
                val = jnp.zeros((QB, 2 * QB), F32)
                for k in range(N_BUCKETS):
                    val = jnp.where(bk == k, tab_ref[k, col], val)
                o_ref[g, 0, h] = jnp.where(mk_ref[0] != 0, val, NEG_INF)
                o_ref[g, 1, h] = jnp.where(mk_ref[1] != 0, val, NEG_INF)

    return pl.pallas_call(
        body, name="bias_expand",
        in_specs=[pl.BlockSpec(memory_space=pltpu.SMEM),
                  pl.BlockSpec(memory_space=pltpu.VMEM),
                  pl.BlockSpec(memory_space=pltpu.VMEM)],
        out_shape=jax.ShapeDtypeStruct((3, 2, 4, QB, 2 * QB), F32),
    )(rel_bias, buckets, masks)


def _bias_grad(ds1, ds2, ds3, buckets):
    def body(d1_ref, d2_ref, d3_ref, bk_ref, o_ref):
        for g, d_ref in enumerate((d1_ref, d2_ref, d3_ref)):
            bk = bk_ref[g]
            for h in range(4):
                dv = d_ref[h]
                for k in range(N_BUCKETS):
                    o_ref[k, 4 * g + h] = jnp.sum(jnp.where(bk == k, dv, 0.0))

    return pl.pallas_call(
        body, name="bias_grad",
        in_specs=[pl.BlockSpec(memory_space=pltpu.VMEM)] * 4,
        out_specs=pl.BlockSpec(memory_space=pltpu.SMEM),
        out_shape=jax.ShapeDtypeStruct((N_BUCKETS, N_HEADS), F32),
    )(ds1, ds2, ds3, buckets)


def _scratch_sets(rows):
    return 4 if rows <= 512 else 1


def _unit_chunks(dil, size=16):
    units = [(h, r) for h in range(4) for r in range(dil)]
    return [units[i:i + size] for i in range(0, len(units), size)]


def _residue_rows(src_ref, copies, h, residue):
    sl = slice(h * HD, (h + 1) * HD)
    if copies is None:
        return lambda r: src_ref[:, sl]
    buf = copies[h % len(copies)]
    buf[...] = src_ref[:, sl].astype(F32)
    return lambda r: buf[residue(r), :].astype(BF16)


def _attn_fwd(proj, bias, g):
    dil = DILATIONS[g]
    rows = QB * dil
    nsb = S // rows
    has_prev = nsb > 1

    def residue(r):
        return pl.ds(r, QB, stride=dil) if dil > 1 else pl.ds(0, QB)

    strided = dil > 1
    n_sets = _scratch_sets(rows)
    n_in = 6 if has_prev else 4

    def body(*refs):
        q_ref, kc_ref, vc_ref = refs[:3]
        kp_ref, vp_ref = refs[3:5] if has_prev else (None, None)
        b_ref, o_ref, l_ref = refs[n_in - 1:n_in + 2]
        scr = list(refs[n_in + 2:])
        ls = [scr.pop(0) for _ in range(4)]
        copies = {name: [scr.pop(0) for _ in range(n_sets)] if strided else None
                  for name in ("q", "kc", "vc", "o") + (("kp", "vp") if has_prev else ())}
        lane = lax.broadcasted_iota(jnp.int32, (QB, 128), 1)
        refs_of = {"q": q_ref, "kc": kc_ref, "vc": vc_ref, "kp": kp_ref, "vp": vp_ref}
        for chunk in _unit_chunks(dil):
            rows_of = {h: {name: _residue_rows(refs_of[name], copies[name], h, residue)
                           for name in refs_of if refs_of[name] is not None}
                       for h in sorted({h for h, _ in chunk})}

            def batch(name):
                return jnp.stack([rows_of[h][name](r) for h, r in chunk])

            q, k, v = batch("q"), batch("kc"), batch("vc")
            if has_prev:
                k = jnp.concatenate([batch("kp"), k], axis=1)
                v = jnp.concatenate([batch("vp"), v], axis=1)
                bias_b = jnp.stack([b_ref[h] for h, _ in chunk])
            else:
                bias_b = jnp.stack([b_ref[h, :, QB:] for h, _ in chunk])
            s = jnp.einsum("uqd,ukd->uqk", q, k, preferred_element_type=F32) * SCALE + bias_b
            m = jnp.max(s, axis=-1, keepdims=True)
            p = jnp.exp(s - m)
            l = jnp.sum(p, axis=-1, keepdims=True)
            o = jnp.einsum("uqk,ukd->uqd", p.astype(BF16), v, preferred_element_type=F32) / l
            lse = m + jnp.log(l)
            for i, (h, r) in enumerate(chunk):
                if strided:
                    copies["o"][h % n_sets][residue(r), :] = o[i]
                else:
                    o_ref[:, h * HD:(h + 1) * HD] = o[i]
                ls[h][r * QB:(r + 1) * QB, :] = jnp.where(lane == h, lse[i], 0.0)
            if strided:
                for h in sorted({h for h, _ in chunk}):
                    o_ref[:, h * HD:(h + 1) * HD] = copies["o"][h % n_sets][...]
        for r in range(dil):
            blk = slice(r * QB, (r + 1) * QB)
            l_ref[residue(r), :] = (ls[0][blk, :] + ls[1][blk, :]) + (ls[2][blk, :] + ls[3][blk, :])

    def row(b, n):
        return b * nsb + n

    def prev(b, n):
        return b * nsb + jnp.maximum(n - 1, 0)

    in_specs = [
        pl.BlockSpec((rows, GW), lambda b, n: (row(b, n), CB_Q + g)),
        pl.BlockSpec((rows, GW), lambda b, n: (row(b, n), CB_K + g)),
        pl.BlockSpec((rows, GW), lambda b, n: (row(b, n), CB_V + g)),
    ]
    args = [proj, proj, proj]
    n_copied = (4 + (2 if has_prev else 0)) * (n_sets if strided else 0)
    scratch = [pltpu.VMEM((rows, 128), F32)] * (4 + n_copied)
    if has_prev:
        in_specs += [pl.BlockSpec((rows, GW), lambda b, n: (prev(b, n), CB_K + g)),
                     pl.BlockSpec((rows, GW), lambda b, n: (prev(b, n), CB_V + g))]
        args += [proj, proj]
    in_specs.append(pl.BlockSpec((None, None, 4, QB, 2 * QB),
                                 lambda b, n: (g, jnp.minimum(n, 1), 0, 0, 0)))
    args.append(bias)
    return pl.pallas_call(
        body, name=f"attn_fwd{g}",
        grid=(BL, nsb),
        in_specs=in_specs,
        out_specs=(pl.BlockSpec((rows, GW), lambda b, n: (row(b, n), 0)),
                   pl.BlockSpec((rows, 128), lambda b, n: (row(b, n), 0))),
        out_shape=(jax.ShapeDtypeStruct((T, GW), F32), jax.ShapeDtypeStruct((T, 128), F32)),
        scratch_shapes=scratch,
        compiler_params=pltpu.CompilerParams(vmem_limit_bytes=VMEM_LIMIT),
    )(*args)


def _attn_bwd(proj, d_out, stats, bias, dproj, g):
    dil = DILATIONS[g]
    rows = QB * dil
    nsb = S // rows
    has_prev = nsb > 1
    n_steps = nsb + 1 if has_prev else 1
    n_in = 7 + (2 if has_prev else 0)

    def residue(r):
        return pl.ds(r, QB, stride=dil) if dil > 1 else pl.ds(0, QB)

    strided = dil > 1
    n_sets = _scratch_sets(rows)

    def body(*refs):
        q_ref, kc_ref, vc_ref, do_ref, st_ref, b_ref = refs[:6]
        kp_ref, vp_ref = refs[6:8] if has_prev else (None, None)
        out_ref, db_ref = refs[n_in], refs[n_in + 1]
        scr = list(refs[n_in + 2:])
        sq, sk, sv, sems = [scr.pop(0) for _ in range(4)]
        carry = scr.pop(0) if has_prev else None
        sts = scr.pop(0) if strided else st_ref
        copies = {name: [scr.pop(0) for _ in range(n_sets)] if strided else None
                  for name in ("q", "kc", "vc", "do", "dq", "dk", "dv") + (("kp", "vp") if has_prev else ())}
        b, n = pl.program_id(0), pl.program_id(1)

        @pl.when((b == 0) & (n == 0))
        def _():
            db_ref[...] = jnp.zeros_like(db_ref)

        def finish(h, r, dq, dk, dv):
            if strided:
                for name, val in (("dq", dq), ("dk", dk), ("dv", dv)):
                    copies[name][h % n_sets][residue(r), :] = val
            else:
                sl = slice(h * HD, (h + 1) * HD)
                sq[:, sl], sk[:, sl], sv[:, sl] = dq.astype(BF16), dk.astype(BF16), dv.astype(BF16)

        def finish_head(h):
            if strided:
                sl = slice(h * HD, (h + 1) * HD)
                sq[:, sl] = copies["dq"][h % n_sets][...].astype(BF16)
                sk[:, sl] = copies["dk"][h % n_sets][...].astype(BF16)
                sv[:, sl] = copies["dv"][h % n_sets][...].astype(BF16)

        def write_block(blk_idx):
            row0 = pl.multiple_of(blk_idx * rows, rows)
            _write_columns([(sq, CB * (CB_Q + g)), (sk, CB * (CB_K + g)), (sv, CB * (CB_V + g))],
                           out_ref, row0, sems)

        def carried(h, r):
            blk = slice(r * QB, (r + 1) * QB)
            return ((blk, slice(h * HD, (h + 1) * HD)), (blk, slice(GW + h * HD, GW + (h + 1) * HD)),
                    (blk, slice(2 * GW + h * HD, 2 * GW + (h + 1) * HD)))

        if has_prev:
            @pl.when(n == 0)
            def _():
                carry[...] = jnp.zeros_like(carry)

            @pl.when(n == nsb)
            def _():
                for h in range(4):
                    for r in range(dil):
                        cq, ck, cv = carried(h, r)
                        finish(h, r, carry[cq], carry[ck], carry[cv])
                    finish_head(h)
                write_block(b * nsb + nsb - 1)

        @pl.when(n < nsb)
        def _():
            if strided:
                for r in range(dil):
                    sts[r * QB:(r + 1) * QB, :] = st_ref[residue(r), :]
            refs_of = {"q": q_ref, "kc": kc_ref, "vc": vc_ref, "do": do_ref, "kp": kp_ref, "vp": vp_ref}
            for chunk in _unit_chunks(dil):
                heads = sorted({h for h, _ in chunk})
                rows_of = {h: {name: _residue_rows(refs_of[name], copies[name], h, residue)
                               for name in refs_of if refs_of[name] is not None}
                           for h in heads}

                def batch(name):
                    return jnp.stack([rows_of[h][name](r) for h, r in chunk])

                q, k, v, do = batch("q"), batch("kc"), batch("vc"), batch("do")
                if has_prev:
                    k = jnp.concatenate([batch("kp"), k], axis=1)
                    v = jnp.concatenate([batch("vp"), v], axis=1)
                    bias_b = jnp.stack([b_ref[h] for h, _ in chunk])
                else:
                    bias_b = jnp.stack([b_ref[h, :, QB:] for h, _ in chunk])
                lse = jnp.stack([sts[r * QB:(r + 1) * QB, h:h + 1] for h, r in chunk])
                delta = jnp.stack([sts[r * QB:(r + 1) * QB, 4 + h:5 + h] for h, r in chunk])
                s = jnp.einsum("uqd,ukd->uqk", q, k, preferred_element_type=F32) * SCALE + bias_b
                p = jnp.exp(s - lse)
                ds = p * (jnp.einsum("uqd,ukd->uqk", do, v, preferred_element_type=F32) - delta)
                for h in heads:
                    mine = [ds[i] for i, (hh, _) in enumerate(chunk) if hh == h]
                    tot = mine[0]
                    for extra in mine[1:]:
                        tot = tot + extra
                    if has_prev:
                        db_ref[h] += tot
                    else:
                        db_ref[h, :, QB:] += tot
                dsb, pb = ds.astype(BF16), p.astype(BF16)
                dq = jnp.einsum("uqk,ukd->uqd", dsb, k, preferred_element_type=F32) * SCALE
                dk = jnp.einsum("uqk,uqd->ukd", dsb, q, preferred_element_type=F32) * SCALE
                dv = jnp.einsum("uqk,uqd->ukd", pb, do, preferred_element_type=F32)
                for i, (h, r) in enumerate(chunk):
                    if has_prev:
                        cq, ck, cv = carried(h, r)
                        finish(h, r, carry[cq], carry[ck] + dk[i, :QB], carry[cv] + dv[i, :QB])
                        carry[cq] = dq[i]
                        carry[ck] = dk[i, QB:]
                        carry[cv] = dv[i, QB:]
                    else:
                        finish(h, r, dq[i], dk[i], dv[i])
                for h in heads:
                    finish_head(h)
            if has_prev:
                @pl.when(n > 0)
                def _():
                    write_block(b * nsb + n - 1)
            else:
                write_block(b)

    def row(b, n):
        return b * nsb + jnp.minimum(n, nsb - 1)

    def prev(b, n):
        return b * nsb + jnp.maximum(jnp.minimum(n, nsb - 1) - 1, 0)

    in_specs = [
        pl.BlockSpec((rows, GW), lambda b, n: (row(b, n), CB_Q + g)),
        pl.BlockSpec((rows, GW), lambda b, n: (row(b, n), CB_K + g)),
        pl.BlockSpec((rows, GW), lambda b, n: (row(b, n), CB_V + g)),
        pl.BlockSpec((rows, GW), lambda b, n: (row(b, n), 0)),
        pl.BlockSpec((rows, 128), lambda b, n: (row(b, n), 0)),
        pl.BlockSpec((None, None, 4, QB, 2 * QB),
                     lambda b, n: (g, jnp.minimum(jnp.minimum(n, nsb - 1), 1), 0, 0, 0)),
    ]
    args = [proj, proj, proj, d_out, stats, bias]
    scratch = [pltpu.VMEM((rows, GW), BF16)] * 3 + [pltpu.SemaphoreType.DMA((3,))]
    if has_prev:
        in_specs += [pl.BlockSpec((rows, GW), lambda b, n: (prev(b, n), CB_K + g)),
                     pl.BlockSpec((rows, GW), lambda b, n: (prev(b, n), CB_V + g))]
        args += [proj, proj]
        scratch.append(pltpu.VMEM((rows, 3 * GW), F32))
    if strided:
        n_copied = (7 + (2 if has_prev else 0)) * n_sets
        scratch += [pltpu.VMEM((rows, 128), F32)] * (1 + n_copied)
    in_specs.append(pl.BlockSpec(memory_space=pl.ANY))
    args.append(dproj)
    return pl.pallas_call(
        body, name=f"attn_bwd{g}",
        grid=(BL, n_steps),
        in_specs=in_specs,
        out_specs=(pl.BlockSpec(memory_space=pl.ANY),
                   pl.BlockSpec((4, QB, 2 * QB), lambda b, n: (0, 0, 0))),
        out_shape=(jax.ShapeDtypeStruct((T, NCOL), BF16),
                   jax.ShapeDtypeStruct((4, QB, 2 * QB), F32)),
        scratch_shapes=scratch,
        input_output_aliases={len(args) - 1: 0},
        compiler_params=pltpu.CompilerParams(vmem_limit_bytes=VMEM_LIMIT),
    )(*args)


def _tail(x2, tgt2, mod3, o_g, lse_g, proj, w_ao, w_co, w_o, conv_w, conv_b, ln_g, ln_b):
    tm = 256
    per_seq = S // tm
    halo = 16

    def body(x_ref, t_ref, mod_ref, o1_ref, o2_ref, o3_ref, l1_ref, l2_ref, l3_ref,
             ga_ref, u_ref, bg_ref, cg_ref, gc_ref, ma_ref, mc_ref, up_ref, cp_ref,
             wao_ref, wco_ref, wo_ref, cw_ref, cb_ref, lg_ref, lb_ref,
             dproj_ref, dyc_ref, do_ref, st_ref, dxd_ref,
             mg_ref, dy_ref, ain_ref, dao_ref, sin_ref, dso_ref, vec_ref,
             dga_s, dbg_s, dgm_s, sems):
        i = pl.program_id(0)
        bidx = i // per_seq
        first = (i % per_seq) == 0

        @pl.when(i == 0)
        def _():
            vec_ref[...] = jnp.zeros_like(vec_ref)

        l1, l2, l3 = l1_ref[...], l2_ref[...], l3_ref[...]
        mx = jnp.maximum(jnp.maximum(l1, l2), l3)
        e1, e2, e3 = jnp.exp(l1 - mx), jnp.exp(l2 - mx), jnp.exp(l3 - mx)
        esum = e1 + e2 + e3
        lse_tot = mx + jnp.log(esum)
        w1, w2, w3 = e1 / esum, e2 / esum, e3 / esum

        def per_head(wv):
            return jnp.concatenate([jnp.broadcast_to(wv[:, h:h + 1], (tm, HD)) for h in range(4)], axis=1)

        o = per_head(w1) * o1_ref[...] + per_head(w2) * o2_ref[...] + per_head(w3) * o3_ref[...]

        ga = ga_ref[...].astype(F32)
        sig_ga = _sigmoid(ga)
        silu_ga = ga * sig_ga
        a_in = (o * silu_ga).astype(BF16)
        a_out = _dot(a_in, wao_ref[...])

        u = u_ref[...].astype(F32)
        cg = cg_ref[...].astype(F32)
        z = cg * u
        zp = cp_ref[...].astype(F32) * up_ref[...].astype(F32)
        zp = jnp.where(first, 0.0, zp)
        zcat = jnp.concatenate([zp, z], axis=0)
        z1 = pltpu.roll(zcat, 1, 0)[halo:]
        z2 = pltpu.roll(zcat, 2, 0)[halo:]
        y_conv = cw_ref[0:1, :] * z2 + cw_ref[1:2, :] * z1 + cw_ref[2:3, :] * z + cb_ref[...]
        gc = gc_ref[...].astype(F32)
        sig_gc = _sigmoid(gc)
        silu_gc = gc * sig_gc
        bg = bg_ref[...].astype(F32)
        s_in = (bg * y_conv * silu_gc).astype(BF16)
        s_out = _dot(s_in, wco_ref[...])

        sa = _sigmoid(ma_ref[...].astype(F32))
        sc = _sigmoid(mc_ref[...].astype(F32))
        merged = (sa * a_out + sc * s_out).astype(BF16)
        y = _dot(merged, wo_ref[...])
        gate1 = 1.0 + mod_ref[0, 2:3, :]
        xv = x_ref[...]
        resid = ALPHA * xv + gate1 * y
        mu = jnp.mean(resid, axis=1, keepdims=True)
        xc = resid - mu
        var = jnp.mean(xc * xc, axis=1, keepdims=True)
        rstd = lax.rsqrt(var + LN_EPS)
        xhat = xc * rstd
        lg = lg_ref[...]
        err = xhat * lg + lb_ref[...] - t_ref[...]
        vec_ref[3:4, :] += (0.5 / D) * jnp.sum(err * err, axis=0, keepdims=True)

        dout = err * (1.0 / D)
        vec_ref[1:2, :] += jnp.sum(dout * xhat, axis=0, keepdims=True)
        vec_ref[2:3, :] += jnp.sum(dout, axis=0, keepdims=True)
        dxh = dout * lg
        dres = rstd * (dxh - jnp.mean(dxh, axis=1, keepdims=True)
                       - xhat * jnp.mean(dxh * xhat, axis=1, keepdims=True))
        dxd_ref[...] = ALPHA * dres
        dgate = jnp.sum(dres * y, axis=0, keepdims=True)
        vec_ref[4:5, :] += jnp.where(bidx == 0, dgate, 0.0)
        vec_ref[5:6, :] += jnp.where(bidx == 1, dgate, 0.0)
        dy = (dres * gate1).astype(BF16)

        dmerged = _dot_nt(dy, wo_ref[...])
        da_out = (dmerged * sa).astype(BF16)
        ds_out = (dmerged * sc).astype(BF16)
        dgm_s[:, 2 * D:3 * D] =(dmerged * s_out * sc * (1.0 - sc)).astype(BF16)
        dgm_s[:, D:2 * D] =(dmerged * a_out * sa * (1.0 - sa)).astype(BF16)
        da_in = _dot_nt(da_out, wao_ref[...])
        ds_in = _dot_nt(ds_out, wco_ref[...])

        d_o = da_in * silu_ga
        do_ref[...] = d_o.astype(BF16)
        dga_s[...] =(da_in * o * (sig_ga * (1.0 + ga * (1.0 - sig_ga)))).astype(BF16)
        lane = lax.broadcasted_iota(jnp.int32, (tm, 128), 1)
        stats = lse_tot
        od = o * d_o
        for h in range(4):
            delta = jnp.sum(od[:, h * HD:(h + 1) * HD], axis=1, keepdims=True)
            stats = jnp.where(lane == 4 + h, delta, stats)
        st_ref[...] = stats

        dbg_s[...] =(ds_in * y_conv * silu_gc).astype(BF16)
        dyc = ds_in * bg * silu_gc
        dyc_ref[...] = dyc
        vec_ref[0:1, :] += jnp.sum(dyc, axis=0, keepdims=True)
        dgm_s[:, 0:D] =(ds_in * bg * y_conv * (sig_gc * (1.0 + gc * (1.0 - sig_gc)))).astype(BF16)

        mg_ref[...] = merged
        dy_ref[...] = dy
        ain_ref[...] = a_in
        dao_ref[...] = da_out
        sin_ref[...] = s_in
        dso_ref[...] = ds_out
        _write_columns([(dga_s, CB * CB_GA), (dbg_s, D * KB_BG), (dgm_s, D * KB_GC)],
                       dproj_ref, pl.multiple_of(i * tm, tm), sems)

    def tile(width, cblk=0):
        return pl.BlockSpec((tm, width), lambda i: (i, cblk))

    def whole(shape):
        return pl.BlockSpec(shape, lambda i: tuple(0 for _ in shape))

    prev_rows = lambda i: (jnp.maximum(i * (tm // halo) - 1, 0),)
    in_specs = [
        tile(D), tile(D), pl.BlockSpec((1, 3, D), lambda i: (i // per_seq, 0, 0)),
        tile(GW), tile(GW), tile(GW), tile(128), tile(128), tile(128),
        tile(GW, CB_GA), tile(D, KB_U), tile(D, KB_BG), tile(D, KB_CG), tile(D, KB_GC),
        tile(D, KB_MA), tile(D, KB_MC),
        pl.BlockSpec((halo, D), lambda i: (*prev_rows(i), KB_U)),
        pl.BlockSpec((halo, D), lambda i: (*prev_rows(i), KB_CG)),
        whole((GW, D)), whole((D, D)), whole((D, D)),
        whole((3, D)), whole((1, D)), whole((1, D)), whole((1, D)),
    ]
    out_specs = (
        pl.BlockSpec(memory_space=pl.ANY), tile(D), tile(GW), tile(128), tile(D),
        tile(D), tile(D), tile(GW), tile(D), tile(D), tile(D),
        pl.BlockSpec((8, D), lambda i: (0, 0)),
    )
    out_shape = (
        jax.ShapeDtypeStruct((T, NCOL), BF16),
        jax.ShapeDtypeStruct((T, D), F32),
        jax.ShapeDtypeStruct((T, GW), BF16),
        jax.ShapeDtypeStruct((T, 128), F32),
        jax.ShapeDtypeStruct((T, D), F32),
        jax.ShapeDtypeStruct((T, D), BF16),
        jax.ShapeDtypeStruct((T, D), BF16),
        jax.ShapeDtypeStruct((T, GW), BF16),
        jax.ShapeDtypeStruct((T, D), BF16),
        jax.ShapeDtypeStruct((T, D), BF16),
        jax.ShapeDtypeStruct((T, D), BF16),
        jax.ShapeDtypeStruct((8, D), F32),
    )
    return pl.pallas_call(
        body, name="tail",
        grid=(T // tm,),
        in_specs=in_specs, out_specs=out_specs, out_shape=out_shape,
        scratch_shapes=[pltpu.VMEM((tm, GW), BF16), pltpu.VMEM((tm, D), BF16), pltpu.VMEM((tm, 3 * D), BF16),
                        pltpu.SemaphoreType.DMA((3,))],
        compiler_params=pltpu.CompilerParams(vmem_limit_bytes=VMEM_LIMIT),
    )(x2, tgt2, mod3, *o_g, *lse_g, proj, proj, proj, proj, proj, proj, proj, proj, proj,
      w_ao, w_co, w_o, conv_w, conv_b, ln_g, ln_b)


def _conv_bwd(dyc, proj, conv_w, dproj):
    tm = 512
    per_seq = S // tm
    halo = 16

    def body(d_ref, dn_ref, u_ref, c_ref, up_ref, cp_ref, cw_ref, _, dproj_ref, g_ref, du_s, dc_s, sems):
        i = pl.program_id(0)
        first = (i % per_seq) == 0
        last = (i % per_seq) == per_seq - 1

        @pl.when(i == 0)
        def _():
            g_ref[...] = jnp.zeros_like(g_ref)

        d = d_ref[...]
        dn = jnp.where(last, 0.0, dn_ref[...])
        dcat = jnp.concatenate([d, dn], axis=0)
        d1 = pltpu.roll(dcat, tm + 8 - 1, 0)[:tm]
        d2 = pltpu.roll(dcat, tm + 8 - 2, 0)[:tm]
        dz = cw_ref[2:3, :] * d + cw_ref[1:2, :] * d1 + cw_ref[0:1, :] * d2
        u = u_ref[...].astype(F32)
        cg = c_ref[...].astype(F32)
        du_s[...] = (dz * cg).astype(BF16)
        dc_s[...] = (dz * u).astype(BF16)
        _write_columns([(du_s, D * KB_U), (dc_s, D * KB_CG)], dproj_ref, pl.multiple_of(i * tm, tm), sems)

        z = cg * u
        zp = jnp.where(first, 0.0, cp_ref[...].astype(F32) * up_ref[...].astype(F32))
        zcat = jnp.concatenate([zp, z], axis=0)
        z1 = pltpu.roll(zcat, 1, 0)[halo:]
        z2 = pltpu.roll(zcat, 2, 0)[halo:]
        g_ref[0:1, :] += jnp.sum(d * z2, axis=0, keepdims=True)
        g_ref[1:2, :] += jnp.sum(d * z1, axis=0, keepdims=True)
        g_ref[2:3, :] += jnp.sum(d * z, axis=0, keepdims=True)

    n_tiles = T // tm
    prev_rows = lambda i: jnp.maximum(i * (tm // halo) - 1, 0)
    next_rows = lambda i: jnp.minimum((i + 1) * (tm // 8), T // 8 - 1)
    return pl.pallas_call(
        body, name="conv_bwd",
        grid=(n_tiles,),
        in_specs=[pl.BlockSpec((tm, D), lambda i: (i, 0)),
                  pl.BlockSpec((8, D), lambda i: (next_rows(i), 0)),
                  pl.BlockSpec((tm, D), lambda i: (i, KB_U)),
                  pl.BlockSpec((tm, D), lambda i: (i, KB_CG)),
                  pl.BlockSpec((halo, D), lambda i: (prev_rows(i), KB_U)),
                  pl.BlockSpec((halo, D), lambda i: (prev_rows(i), KB_CG)),
                  pl.BlockSpec((3, D), lambda i: (0, 0)),
                  pl.BlockSpec(memory_space=pl.ANY)],
        out_specs=(pl.BlockSpec(memory_space=pl.ANY),
                   pl.BlockSpec((8, D), lambda i: (0, 0))),
        out_shape=(jax.ShapeDtypeStruct((T, NCOL), BF16),
                   jax.ShapeDtypeStruct((8, D), F32)),
        scratch_shapes=[pltpu.VMEM((tm, D), BF16), pltpu.VMEM((tm, D), BF16), pltpu.SemaphoreType.DMA((2,))],
        input_output_aliases={7: 0},
        compiler_params=pltpu.CompilerParams(vmem_limit_bytes=VMEM_LIMIT),
    )(dyc, dyc, proj, proj, proj, proj, conv_w, dproj)


def _dh_dx(dproj, w_in_all, x2, dxd, mod3, chip_sums):
    tm = 1024
    per_seq = S // tm
    n = len(chip_sums)

    def body(*refs):
        d_ref, w_ref, x_ref, dxd_ref, mod_ref = refs[:5]
        ins = refs[5:5 + n]
        gx_ref, vec_ref = refs[5 + n:7 + n]
        outs = refs[7 + n:7 + 2 * n]
        acc, send_sems, recv_sems, local_sems = refs[7 + 2 * n:]
        i, jj = pl.program_id(0), pl.program_id(1)

        @pl.when((i == 0) & (jj == 0))
        def _():
            vec_ref[...] = jnp.zeros_like(vec_ref)
            if n:
                sends, _, mine = _chip_copies(ins, outs, send_sems, recv_sems, local_sems)
                for cp in sends + mine:
                    cp.start()

        if n:
            @pl.when((i == T // tm - 1) & (jj == N_DEV - 1))
            def _():
                sends, arrivals, mine = _chip_copies(ins, outs, send_sems, recv_sems, local_sems)
                for cp in arrivals:
                    cp.wait_recv()
                for cp in sends:
                    cp.wait_send()
                for cp in mine:
                    cp.wait()

        @pl.when(jj == 0)
        def _():
            acc[...] = jnp.zeros_like(acc)

        acc[...] += _dot_nt(d_ref[...], w_ref[...])

        @pl.when(jj == N_DEV - 1)
        def _():
            dh = acc[...]
            bidx = i // per_seq
            gx_ref[...] = dxd_ref[...] + dh * (1.0 + mod_ref[0, 1:2, :])
            dshift = jnp.sum(dh, axis=0, keepdims=True)
            dscale = jnp.sum(dh * x_ref[...], axis=0, keepdims=True)
            vec_ref[0:1, :] += jnp.where(bidx == 0, dshift, 0.0)
            vec_ref[1:2, :] += jnp.where(bidx == 1, dshift, 0.0)
            vec_ref[2:3, :] += jnp.where(bidx == 0, dscale, 0.0)
            vec_ref[3:4, :] += jnp.where(bidx == 1, dscale, 0.0)

    any_spec = pl.BlockSpec(memory_space=pl.ANY)
    res = pl.pallas_call(
        body, name="dh_dx",
        grid=(T // tm, N_DEV),
        in_specs=[
            pl.BlockSpec((tm, SHARD), lambda i, jj: (i, jj)),
            pl.BlockSpec((None, D, SHARD), lambda i, jj: (jj, 0, 0)),
            pl.BlockSpec((tm, D), lambda i, jj: (i, 0)),
            pl.BlockSpec((tm, D), lambda i, jj: (i, 0)),
            pl.BlockSpec((1, 3, D), lambda i, jj: (i // per_seq, 0, 0))] + [any_spec] * n,
        out_specs=(pl.BlockSpec((tm, D), lambda i, jj: (i, 0)),
                   pl.BlockSpec((8, D), lambda i, jj: (0, 0))) + (any_spec,) * n,
        out_shape=(jax.ShapeDtypeStruct((T, D), F32), jax.ShapeDtypeStruct((8, D), F32))
                  + tuple(jax.ShapeDtypeStruct(a.shape, a.dtype) for a in chip_sums),
        scratch_shapes=[pltpu.VMEM((tm, D), F32), pltpu.SemaphoreType.DMA((max(3 * n, 1),)),
                        pltpu.SemaphoreType.DMA((max(3 * n, 1),)), pltpu.SemaphoreType.DMA((max(n, 1),))],
        compiler_params=pltpu.CompilerParams(vmem_limit_bytes=VMEM_LIMIT),
    )(dproj, w_in_all, x2, dxd, mod3, *chip_sums)
    return res[0], res[1], res[2:]


def _mm_tn(a, b, tn, blocks_leading, name):
    kk, m = a.shape
    n = b.shape[1]
    tk = 2048

    def body(a_ref, b_ref, o_ref, acc):
        @pl.when(pl.program_id(1) == 0)
        def _():
            acc[...] = jnp.zeros_like(acc)

        acc[...] += _dot_tn(a_ref[...], b_ref[...])

        @pl.when(pl.program_id(1) == kk // tk - 1)
        def _():
            o_ref[...] = acc[...].astype(BF16)

    if blocks_leading:
        out_spec = pl.BlockSpec((None, m, tn), lambda j, k: (j, 0, 0))
        out_shape = jax.ShapeDtypeStruct((n // tn, m, tn), BF16)
    else:
        out_spec = pl.BlockSpec((m, tn), lambda j, k: (0, j))
        out_shape = jax.ShapeDtypeStruct((m, n), BF16)
    return pl.pallas_call(
        body, name=name,
        grid=(n // tn, kk // tk),
        in_specs=[pl.BlockSpec((tk, m), lambda j, k: (k, 0)),
                  pl.BlockSpec((tk, tn), lambda j, k: (k, j))],
        out_specs=out_spec, out_shape=out_shape,
        scratch_shapes=[pltpu.VMEM((m, tn), F32)],
        compiler_params=pltpu.CompilerParams(vmem_limit_bytes=VMEM_LIMIT),
    )(a, b)


def _adamw(parts, w, m, v, name, row_tile=None):
    n_parts, rows, cols = parts.shape
    tr = rows if row_tile is None else row_tile
    c1 = 1.0 - ADAM_B1 ** ADAM_STEP
    c2 = 1.0 - ADAM_B2 ** ADAM_STEP

    def body(p_ref, w_ref, m_ref, v_ref, g_ref, d_ref, nm_ref, nv_ref):
        g = p_ref[0].astype(F32)
        for s in range(1, n_parts):
            g = g + p_ref[s].astype(F32)
        nm = ADAM_B1 * m_ref[...] + (1.0 - ADAM_B1) * g
        nv = ADAM_B2 * v_ref[...] + (1.0 - ADAM_B2) * (g * g)
        m_hat = nm / c1
        v_hat = nv / c2
        g_ref[...] = g
        d_ref[...] = -ADAM_LR * (m_hat / (jnp.sqrt(v_hat) + ADAM_EPS) + ADAM_WD * w_ref[...])
        nm_ref[...] = nm
        nv_ref[...] = nv

    blk = pl.BlockSpec((tr, cols), lambda i: (i, 0))
    shp = jax.ShapeDtypeStruct((rows, cols), F32)
    return pl.pallas_call(
        body, name=name,
        grid=(rows // tr,),
        in_specs=[pl.BlockSpec((n_parts, tr, cols), lambda i: (0, i, 0)), blk, blk, blk],
        out_specs=(blk, blk, blk, blk),
        out_shape=(shp, shp, shp, shp),
        compiler_params=pltpu.CompilerParams(vmem_limit_bytes=VMEM_LIMIT),
    )(parts, w, m, v)


def _loss_sum(rows):
    def body(r_ref, o_ref):
        o_ref[...] = jnp.sum(jnp.sum(r_ref[...], axis=0, keepdims=True), axis=1, keepdims=True)

    return pl.pallas_call(body, name="loss_sum", out_shape=jax.ShapeDtypeStruct((1, 1), F32))(rows)


def _local_step(x2, tgt2, mod3, h, proj, w_ao, w_co, w_o, conv_w, conv_b, rel_bias, ln_g, ln_b):
    buckets_np, masks_np = _bucket_maps()
    buckets, masks = jnp.asarray(buckets_np), jnp.asarray(masks_np)

    bias = _bias_expand(rel_bias, buckets, masks)
    fwd = [_attn_fwd(proj, bias, g) for g in range(3)]
    o_g = [f[0] for f in fwd]
    lse_g = [f[1] for f in fwd]

    (dproj, dyc, d_o, stats, dxd, merged, dy, a_in, da_out, s_in, ds_out, tail_vec) = _tail(
        x2, tgt2, mod3, o_g, lse_g, proj, w_ao, w_co, w_o, conv_w, conv_b, ln_g, ln_b)

    dbias = []
    for g in range(3):
        dproj, db = _attn_bwd(proj, d_o, stats, bias, dproj, g)
        dbias.append(db)
    g_rel_bias = _bias_grad(*dbias, buckets)
    dproj, conv_vec = _conv_bwd(dyc, proj, conv_w, dproj)

    gw_o = _mm_tn(merged, dy, D, False, "gw_o")
    gw_co = _mm_tn(s_in, ds_out, D, False, "gw_conv_out")
    gw_ao = _mm_tn(a_in, da_out, D, False, "gw_attn_out")
    gw_ao = jnp.transpose(gw_ao.reshape(GW, N_DEV, D // N_DEV), (1, 0, 2))
    return dproj, dxd, gw_ao, gw_co, gw_o, conv_vec, g_rel_bias, tail_vec


def kernel(x, c, w_ada, b_ada, w_in, conv_w, conv_b, rel_bias, w_attn_out, w_conv_out, w_o, ln_g, ln_b, loss_target, m_w_ada, m_b_ada, m_w_in, m_conv_w, m_conv_b, m_rel_bias, m_w_attn_out, m_w_conv_out, m_w_o, m_ln_g, m_ln_b, v_w_ada, v_b_ada, v_w_in, v_conv_w, v_conv_b, v_rel_bias, v_w_attn_out, v_w_conv_out, v_w_o, v_ln_g, v_ln_b):
    me = _my_index()
    x2 = x.reshape(T, D)
    tgt2 = loss_target.reshape(T, D)

    b_cols = lax.dynamic_slice(b_ada, (0, me * ADA_SHARD), (1, ADA_SHARD))
    c_g, mod_in = _mod_exchange(jnp.pad(c, ((0, 8 - BL), (0, 0))), w_ada[0], b_cols)
    c_all = c_g[:, 0:BL, :].reshape(N_DEV * BL, D)
    mod3 = jnp.transpose(mod_in[:, 0:BL, :], (1, 0, 2)).reshape(BL, 3, D)

    w_ao_g, w_co_g, w_o_g, conv_w_g = _all_gather(
        [w_attn_out[0].astype(BF16), w_conv_out[0].astype(BF16), w_o[0].astype(BF16), conv_w[0]],
        [jax.ShapeDtypeStruct((N_DEV, GW, D // N_DEV), BF16),
         jax.ShapeDtypeStruct((N_DEV, D // N_DEV, D), BF16),
         jax.ShapeDtypeStruct((N_DEV, D // N_DEV, D), BF16),
         jax.ShapeDtypeStruct((N_DEV, 3, D // N_DEV), F32)],
        [_slot_leading] * 4,
        "gather_weights")
    w_ao_full = jnp.transpose(w_ao_g, (1, 0, 2)).reshape(GW, D)
    w_co_full = w_co_g.reshape(D, D)
    w_o_full = w_o_g.reshape(D, D)
    conv_w_full = jnp.transpose(conv_w_g, (1, 0, 2)).reshape(3, D)

    h = _prep_h(x2, mod3)
    proj, w_in_all = _gather_proj(_shard_order(), h, w_in[0].astype(BF16), 1024)
    (dproj, dxd, gw_ao, gw_co, gw_o, conv_vec, g_rel_bias, tail_vec) = _local_step(
        x2, tgt2, mod3, h, proj, w_ao_full, w_co_full, w_o_full,
        conv_w_full, conv_b, rel_bias, ln_g, ln_b)

    g_conv_w_blocks = jnp.transpose(conv_vec[0:3].reshape(3, N_DEV, D // N_DEV), (1, 0, 2))
    partials = [gw_ao, gw_co.reshape(N_DEV, D // N_DEV, D), gw_o.reshape(N_DEV, D // N_DEV, D), g_conv_w_blocks]
    w_in_sums, sib = _gw_in_pair(
        _slice_order(), h, dproj, partials,
        [jax.ShapeDtypeStruct((4, GW, D // N_DEV), BF16),
         jax.ShapeDtypeStruct((4, D // N_DEV, D), BF16),
         jax.ShapeDtypeStruct((4, D // N_DEV, D), BF16),
         jax.ShapeDtypeStruct((4, 3, D // N_DEV), F32)])
    names = ["w_attn_out", "w_conv_out", "w_o", "conv_w"]
    core = lax.axis_index("c").astype(jnp.int32).reshape(1)
    chip_sums = [w_in_sums] + [_pair_add(core, partials[a], sib[a], None, "pair_add_" + names[a])
                               for a in range(4)]
    grad_x, mod_vec, (r_in, r_ao, r_co, r_o, r_cw) = _dh_dx(dproj, w_in_all, x2, dxd, mod3, chip_sums)

    small = jnp.concatenate([
        tail_vec[0:4],
        jnp.pad(g_rel_bias.reshape(1, N_BUCKETS * N_HEADS), ((0, 0), (0, D - N_BUCKETS * N_HEADS))),
        jnp.zeros((3, D), F32)], axis=0)
    dmod = jnp.concatenate([mod_vec[0:2], mod_vec[2:4], tail_vec[4:6]], axis=1)
    small_g, dmod_g = _all_gather(
        [small, dmod],
        [jax.ShapeDtypeStruct((N_DEV, 8, D), F32), jax.ShapeDtypeStruct((N_DEV, BL, 3 * D), F32)],
        [_slot_leading] * 2, "gather_small")
    dmod_all = dmod_g.reshape(N_DEV * BL, 3 * D)
    loss = _loss_sum(small_g[:, 3, :]).reshape(())
    g_w_ada = _ada_bwd(jnp.transpose(c_all), lax.dynamic_slice(dmod_all, (0, me * ADA_SHARD),
                                                               (N_DEV * BL, ADA_SHARD)))

    def upd(parts, w, m, v, name, row_tile=None):
        shape = w.shape
        w2, m2, v2 = (t.reshape(parts.shape[1:]) for t in (w, m, v))
        return tuple(t.reshape(shape) for t in _adamw(parts, w2, m2, v2, name, row_tile))

    res = {
        "w_ada": upd(g_w_ada[None], w_ada, m_w_ada, v_w_ada, "adam_w_ada", 256),
        "b_ada": upd(dmod_all[:, None, :], b_ada, m_b_ada, v_b_ada, "adam_b_ada"),
        "w_in": upd(r_in, w_in, m_w_in, v_w_in, "adam_w_in", 128),
        "conv_w": upd(r_cw, conv_w, m_conv_w, v_conv_w, "adam_conv_w"),
        "conv_b": upd(small_g[:, 0:1, :], conv_b, m_conv_b, v_conv_b, "adam_conv_b"),
        "rel_bias": upd(small_g[:, 4, :N_BUCKETS * N_HEADS].reshape(N_DEV, N_BUCKETS, N_HEADS),
                        rel_bias, m_rel_bias, v_rel_bias, "adam_rel_bias"),
        "w_attn_out": upd(r_ao, w_attn_out, m_w_attn_out, v_w_attn_out, "adam_w_attn_out"),
        "w_conv_out": upd(r_co, w_conv_out, m_w_conv_out, v_w_conv_out, "adam_w_conv_out"),
        "w_o": upd(r_o, w_o, m_w_o, v_w_o, "adam_w_o"),
        "ln_g": upd(small_g[:, 1:2, :], ln_g, m_ln_g, v_ln_g, "adam_ln_g"),
        "ln_b": upd(small_g[:, 2:3, :], ln_b, m_ln_b, v_ln_b, "adam_ln_b"),
    }
    order = ["w_ada", "b_ada", "w_in", "conv_w", "conv_b", "rel_bias", "w_attn_out", "w_conv_out",
             "w_o", "ln_g", "ln_b"]
    outs = [loss, grad_x.reshape(BL, S, D)]
    for k in range(4):
        outs += [res[name][k] for name in order]
    return tuple(outs)
```

```python
import functools
import math

import numpy as np
import jax
import jax.numpy as jnp
from jax import lax
from jax.experimental import pallas as pl
from jax.experimental.pallas import tpu as pltpu

F32 = jnp.float32
BF16 = jnp.bfloat16
MESH = pl.DeviceIdType.MESH

N_DEV = 8
D = 1024
S = 2048
BL = 2
T = BL * S
NCOL = 11264
SHARD = NCOL // N_DEV
CB = 512
NCB = NCOL // CB
HD = 128
GW = 512
QB = 128
DILATIONS = (1, 4, 16)
N_STEPS = 128
N_BUCKETS = 32
N_HEADS = 12
ALPHA = 2.0 ** 0.25
LN_EPS = 1e-5
NEG_INF = -1e30
SCALE = HD ** -0.5
ADA_SHARD = 3 * D // N_DEV

CB_Q, CB_K, CB_V, CB_GA = 0, 3, 6, 9
KB_U, KB_BG, KB_CG, KB_GC, KB_MA, KB_MC = 5, 6, 7, 8, 9, 10

ADAM_LR, ADAM_B1, ADAM_B2, ADAM_EPS, ADAM_WD, ADAM_STEP = 0.001, 0.9, 0.999, 1e-08, 0.01, 10

VMEM_LIMIT = 56 * 1024 * 1024


def _dot(a, b):
    return jnp.dot(a, b, preferred_element_type=F32)


def _dot_nt(a, b):
    return lax.dot_general(a, b, (((1,), (1,)), ((), ())), preferred_element_type=F32)


def _dot_tn(a, b):
    return lax.dot_general(a, b, (((0,), (0,)), ((), ())), preferred_element_type=F32)


def _sigmoid(v):
    return 1.0 / (1.0 + jnp.exp(-v))


def _write_columns(pieces, dst_hbm, row0, sems):
    copies = []
    for k, (src, col0) in enumerate(pieces):
        rows, width = src.shape
        copies.append(pltpu.make_async_copy(
            src, dst_hbm.at[pl.ds(row0, rows), pl.ds(col0, width)], sems.at[k]))
    for cp in copies:
        cp.start()
    for cp in copies:
        cp.wait()


def _my_index():
    return 4 * lax.axis_index("x") + 2 * lax.axis_index("y") + lax.axis_index("c")


def _slot_leading(ref, slot):
    return ref.at[slot]


def _all_gather(arrs, out_shapes, slot_fns, name):
    n = len(arrs)

    def body(*refs):
        ins, outs = refs[:n], refs[n:2 * n]
        send_sems, recv_sems, local_sems = refs[2 * n:2 * n + 3]
        stage = refs[2 * n + 3:]
        x, y, c = lax.axis_index("x"), lax.axis_index("y"), lax.axis_index("c")
        me, sibling = (x, y, c), (x, y, 1 - c)
        chips = [(1 - x, y), (x, 1 - y), (1 - x, 1 - y)]

        def blk(a, dev):
            return slot_fns[a](outs[a], 4 * dev[0] + 2 * dev[1] + dev[2])

        def copy(a, k, block, to, src=None):
            dst = blk(a, block)
            return pltpu.make_async_remote_copy(
                src_ref=dst if src is None else src, dst_ref=dst,
                send_sem=send_sems.at[a * 7 + k], recv_sem=recv_sems.at[a * 7 + k],
                device_id=to, device_id_type=MESH)

        first = []
        for a in range(n):
            first.append(copy(a, 0, me, sibling, src=ins[a]))
            first += [copy(a, 1 + j, me, (*chip, c), src=ins[a]) for j, chip in enumerate(chips)]
        for cp in first:
            cp.start()
        loads = [pltpu.make_async_copy(ins[a], stage[a], local_sems.at[a]) for a in range(n)]
        for cp in loads:
            cp.start()
        for cp in loads:
            cp.wait()
        mine = [pltpu.make_async_copy(stage[a], blk(a, me), local_sems.at[a]) for a in range(n)]
        for cp in mine:
            cp.start()
        passed = []
        for j, chip in enumerate(chips):
            for a in range(n):
                copy(a, 1 + j, (*chip, c), me).wait_recv()
                fwd = copy(a, 4 + j, (*chip, c), sibling)
                fwd.start()
                passed.append(fwd)
        for a in range(n):
            copy(a, 0, sibling, me).wait_recv()
        for j, chip in enumerate(chips):
            for a in range(n):
                copy(a, 4 + j, (*chip, 1 - c), me).wait_recv()
        for cp in first + passed:
            cp.wait_send()
        for cp in mine:
            cp.wait()

    any_spec = pl.BlockSpec(memory_space=pl.ANY)
    return pl.pallas_call(
        body, name=name,
        out_shape=tuple(out_shapes),
        in_specs=[any_spec] * n,
        out_specs=tuple([any_spec] * n),
        scratch_shapes=[pltpu.SemaphoreType.DMA((7 * n,)), pltpu.SemaphoreType.DMA((7 * n,)),
                        pltpu.SemaphoreType.DMA((n,))]
                       + [pltpu.VMEM(a.shape, a.dtype) for a in arrs],
    )(*arrs)


def _slice_order():
    x, y, c = lax.axis_index("x"), lax.axis_index("y"), lax.axis_index("c")
    slots = []
    for q in (2 * (1 - x) + y, 2 * x + (1 - y), 2 * (1 - x) + (1 - y), 2 * x + y):
        slots += [2 * q + 1 - c, 2 * q + c]
    return jnp.stack(slots).astype(jnp.int32)


def _gw_in_pair(order, h, dproj, smalls, small_shapes4):
    kk, m = h.shape
    tk = min(kk, 2048)
    nk = kk // tk
    ncols = dproj.shape[1] // N_DEV
    n = len(smalls)

    def body(order_ref, h_ref, d_ref, *rest):
        ins = rest[:n]
        sums_hbm, parts_hbm = rest[n], rest[n + 1]
        sib = rest[n + 2:2 * n + 2]
        (acc, sendbuf, recvbuf, sumbuf, send_sems, recv_sems, local_sem, ssend, srecv,
         isend, irecv) = rest[2 * n + 2:]
        js, k = pl.program_id(0), pl.program_id(1)
        x, y, c = lax.axis_index("x"), lax.axis_index("y"), lax.axis_index("c")
        sibling = (x, y, 1 - c)
        my_chip = 2 * x + y
        near = [(1 - x, y, c), (x, 1 - y, c)]

        def ici_copy(p, out_chip):
            peer = near[p]
            return pltpu.make_async_remote_copy(
                src_ref=sums_hbm.at[2 * peer[0] + peer[1]], dst_ref=parts_hbm.at[out_chip],
                send_sem=isend.at[p], recv_sem=irecv.at[p], device_id=peer, device_id_type=MESH)

        def small_copies():
            return [pltpu.make_async_remote_copy(
                        src_ref=ins[a].at[2 * q + 1 - c], dst_ref=sib[a].at[q],
                        send_sem=ssend.at[a * 4 + q], recv_sem=srecv.at[a * 4 + q],
                        device_id=sibling, device_id_type=MESH)
                    for a in range(n) for q in range(4)]

        def slice_copy(p):
            return pltpu.make_async_remote_copy(
                src_ref=sendbuf, dst_ref=recvbuf.at[p], send_sem=send_sems.at[p], recv_sem=recv_sems.at[p],
                device_id=sibling, device_id_type=MESH)

        def sum_copy(p):
            return pltpu.make_async_copy(sumbuf, sums_hbm.at[order_ref[2 * p] // 2], local_sem)

        @pl.when((js == 0) & (k == 0))
        def _():
            for cp in small_copies():
                cp.start()

        @pl.when(k == 0)
        def _():
            acc[...] = jnp.zeros_like(acc)

        acc[...] += _dot_tn(h_ref[...], d_ref[...])

        for p in range(4):
            @pl.when((js == 2 * p) & (k == nk - 1))
            def _():
                if p > 0:
                    slice_copy(p - 1).wait_send()
                sendbuf[...] = acc[...].astype(BF16)
                slice_copy(p).start()

            @pl.when((js == 2 * p + 1) & (k == nk - 1))
            def _():
                slice_copy(p).wait_recv()
                sumbuf[...] = (acc[...] + recvbuf[p].astype(F32)).astype(BF16)
                sum_copy(p).start()
                sum_copy(p).wait()
                if p < 2:
                    ici_copy(p, my_chip).start()

        @pl.when((js == N_DEV - 1) & (k == nk - 1))
        def _():
            slice_copy(3).wait_send()
            for cp in small_copies():
                cp.wait()
            for p in range(2):
                ici_copy(p, 2 * near[p][0] + near[p][1]).wait_recv()
                ici_copy(p, my_chip).wait_send()

    any_spec = pl.BlockSpec(memory_space=pl.ANY)
    res = pl.pallas_call(
        body, name="gw_in_pair",
        grid_spec=pltpu.PrefetchScalarGridSpec(
            num_scalar_prefetch=1,
            grid=(N_DEV, nk),
            in_specs=[pl.BlockSpec((tk, m), lambda js, k, order_ref: (k, 0)),
                      pl.BlockSpec((tk, ncols), lambda js, k, order_ref: (k, order_ref[js]))] + [any_spec] * n,
            out_specs=(any_spec,) * (n + 2),
            scratch_shapes=[pltpu.VMEM((m, ncols), F32), pltpu.VMEM((m, ncols), BF16),
                            pltpu.VMEM((4, m, ncols), BF16), pltpu.VMEM((m, ncols), BF16),
                            pltpu.SemaphoreType.DMA((4,)), pltpu.SemaphoreType.DMA((4,)),
                            pltpu.SemaphoreType.DMA,
                            pltpu.SemaphoreType.DMA((4 * n,)), pltpu.SemaphoreType.DMA((4 * n,)),
                            pltpu.SemaphoreType.DMA((2,)), pltpu.SemaphoreType.DMA((2,))]),
        out_shape=(jax.ShapeDtypeStruct((4, m, ncols), BF16),) * 2 + tuple(small_shapes4),
        compiler_params=pltpu.CompilerParams(vmem_limit_bytes=VMEM_LIMIT),
    )(order, h, dproj, *smalls)
    return res[0], res[1], res[2:]


def _chip_copies(ins, outs, send_sems, recv_sems, local_sems, hops):
    n = len(ins)
    x, y, c = lax.axis_index("x"), lax.axis_index("y"), lax.axis_index("c")
    my_chip = 2 * x + y

    def peer_of(k):
        return ((1 - x) if (k >> 1) & 1 else x, (1 - y) if k & 1 else y, c)

    def copy(a, k, out_chip):
        peer = peer_of(k)
        return pltpu.make_async_remote_copy(
            src_ref=ins[a].at[2 * peer[0] + peer[1]], dst_ref=outs[a].at[out_chip],
            send_sem=send_sems.at[a * 3 + k - 1], recv_sem=recv_sems.at[a * 3 + k - 1],
            device_id=peer, device_id_type=MESH)

    sends = [copy(a, k, my_chip) for k in range(1, 4) for a in range(n) if k in hops[a]]
    arrivals = []
    for k in range(1, 4):
        peer = peer_of(k)
        arrivals += [copy(a, k, 2 * peer[0] + peer[1]) for a in range(n) if k in hops[a]]
    mine = [pltpu.make_async_copy(ins[a].at[my_chip], outs[a].at[my_chip], local_sems.at[a])
            for a in range(n)]
    return sends, arrivals, mine


def _pair_add(core, mine, theirs, row_tile, name):
    _, rows, cols = theirs.shape
    tr = rows if row_tile is None else row_tile

    def body(core_ref, a_ref, b_ref, o_ref):
        o_ref[...] = (a_ref[...].astype(F32) + b_ref[...].astype(F32)).astype(o_ref.dtype)

    blk = pl.BlockSpec((None, tr, cols), lambda q, i, core_ref: (q, i, 0))
    return pl.pallas_call(
        body, name=name,
        grid_spec=pltpu.PrefetchScalarGridSpec(
            num_scalar_prefetch=1,
            grid=(4, rows // tr),
            in_specs=[pl.BlockSpec((None, tr, cols), lambda q, i, core_ref: (2 * q + core_ref[0], i, 0)), blk],
            out_specs=blk),
        out_shape=jax.ShapeDtypeStruct(theirs.shape, theirs.dtype),
    )(core, mine, theirs)


def _mod_exchange(c8, w_ada, b_cols):
    cols = w_ada.shape[1]

    def body(c_ref, w_ref, b_ref, call_ref, mod_ref, msend, send1, recv1, send2, recv2):
        x, y, c = lax.axis_index("x"), lax.axis_index("y"), lax.axis_index("c")
        my_slot = 4 * x + 2 * y + c

        def peer_of(k):
            return ((1 - x) if (k >> 2) & 1 else x, (1 - y) if (k >> 1) & 1 else y, (1 - c) if k & 1 else c)

        def slot_of(dev):
            return 4 * dev[0] + 2 * dev[1] + dev[2]

        def exchange(src_of, dst_ref, send_sems, recv_sems):
            sends, arrivals = [], []
            for k in range(1, 8):
                peer = peer_of(k)
                sends.append(pltpu.make_async_remote_copy(
                    src_ref=src_of(slot_of(peer)), dst_ref=dst_ref.at[my_slot],
                    send_sem=send_sems.at[k - 1], recv_sem=recv_sems.at[k - 1],
                    device_id=peer, device_id_type=MESH))
                arrivals.append(pltpu.make_async_remote_copy(
                    src_ref=src_of(my_slot), dst_ref=dst_ref.at[slot_of(peer)],
                    send_sem=send_sems.at[k - 1], recv_sem=recv_sems.at[k - 1],
                    device_id=peer, device_id_type=MESH))
            for cp in sends:
                cp.start()
            for cp in arrivals:
                cp.wait_recv()
            for cp in sends:
                cp.wait_send()

        call_ref[my_slot] = c_ref[...]
        exchange(lambda s: c_ref, call_ref, send1, recv1)
        cv = call_ref[...].reshape(N_DEV * 8, c_ref.shape[1])
        act = cv * _sigmoid(cv)
        mod = jnp.dot(act, w_ref[...], preferred_element_type=F32,
                      precision=lax.Precision.HIGHEST) + b_ref[...]
        msend[...] = mod.reshape(N_DEV, 8, cols)
        mod_ref[my_slot] = msend[my_slot]
        exchange(lambda s: msend.at[s], mod_ref, send2, recv2)

    return pl.pallas_call(
        body, name="mod_exchange",
        out_shape=(jax.ShapeDtypeStruct((N_DEV, 8, c8.shape[1]), F32),
                   jax.ShapeDtypeStruct((N_DEV, 8, cols), F32)),
        scratch_shapes=[pltpu.VMEM((N_DEV, 8, cols), F32)] + [pltpu.SemaphoreType.DMA((7,))] * 4,
    )(c8, w_ada, b_cols)


def _ada_bwd(c_all_t, dmod_cols):
    def body(c_ref, d_ref, o_ref):
        cv = c_ref[...]
        sc = cv * _sigmoid(cv)
        o_ref[...] = jnp.dot(sc, d_ref[...], preferred_element_type=F32,
                             precision=lax.Precision.HIGHEST)

    return pl.pallas_call(
        body, name="ada_bwd",
        out_shape=jax.ShapeDtypeStruct((c_all_t.shape[0], dmod_cols.shape[1]), F32),
    )(c_all_t, dmod_cols)


def _prep_h(x2, mod3):
    ts = 512
    per_seq = S // ts

    def body(x_ref, mod_ref, h_ref):
        shift = mod_ref[0, 0:1, :]
        scale = mod_ref[0, 1:2, :]
        h_ref[...] = (x_ref[...] * (1.0 + scale) + shift).astype(BF16)

    return pl.pallas_call(
        body, name="prep_h",
        grid=(T // ts,),
        in_specs=[pl.BlockSpec((ts, D), lambda i: (i, 0)),
                  pl.BlockSpec((1, 3, D), lambda i: (i // per_seq, 0, 0))],
        out_specs=pl.BlockSpec((ts, D), lambda i: (i, 0)),
        out_shape=jax.ShapeDtypeStruct((T, D), BF16),
    )(x2, mod3)


def _shard_order():
    x, y, c = lax.axis_index("x"), lax.axis_index("y"), lax.axis_index("c")
    devs = [(x, y, c), (x, y, 1 - c)]
    for chip in [(1 - x, y), (x, 1 - y), (1 - x, 1 - y)]:
        devs += [(*chip, c), (*chip, 1 - c)]
    return jnp.stack([4 * d[0] + 2 * d[1] + d[2] for d in devs]).astype(jnp.int32)


def _gather_proj(order, h, w_shard, tm):
    rows, kdim = h.shape
    ncols = w_shard.shape[1]
    n_i = rows // tm

    def body(order_ref, h_ref, mine_hbm, o_ref, all_hbm, wv, send_sems, recv_sems, local_sems):
        j, i = pl.program_id(0), pl.program_id(1)
        x, y, c = lax.axis_index("x"), lax.axis_index("y"), lax.axis_index("c")
        me, sibling = (x, y, c), (x, y, 1 - c)
        chips = [(1 - x, y), (x, 1 - y), (1 - x, 1 - y)]

        def slot(dev):
            return 4 * dev[0] + 2 * dev[1] + dev[2]

        def copy(k, block, to):
            return pltpu.make_async_remote_copy(
                src_ref=wv.at[slot(block)], dst_ref=wv.at[slot(block)],
                send_sem=send_sems.at[k], recv_sem=recv_sems.at[k],
                device_id=to, device_id_type=MESH)

        def keep(step, block):
            return pltpu.make_async_copy(wv.at[slot(block)], all_hbm.at[slot(block)], local_sems.at[step])

        first = [copy(0, me, sibling)] + [copy(1 + q, me, (*chip, c)) for q, chip in enumerate(chips)]
        passed = [copy(4 + q, (*chip, c), sibling) for q, chip in enumerate(chips)]
        due = [(me, None, None), (sibling, copy(0, sibling, me), None)]
        for q, chip in enumerate(chips):
            due.append(((*chip, c), copy(1 + q, (*chip, c), me), passed[q]))
            due.append(((*chip, 1 - c), copy(4 + q, (*chip, 1 - c), me), None))

        @pl.when((j == 0) & (i == 0))
        def _():
            load = pltpu.make_async_copy(mine_hbm, wv.at[slot(me)], local_sems.at[N_DEV])
            load.start()
            load.wait()
            for cp in first:
                cp.start()
            keep(0, me).start()

        for step in range(1, N_DEV):
            block, arrival, forward = due[step]

            @pl.when((j == step) & (i == 0))
            def _():
                arrival.wait_recv()
                if forward is not None:
                    forward.start()
                keep(step, block).start()

        o_ref[...] = _dot(h_ref[...], wv[order_ref[j]]).astype(BF16)

        @pl.when((j == N_DEV - 1) & (i == n_i - 1))
        def _():
            for cp in first + passed:
                cp.wait_send()
            for step in range(N_DEV):
                keep(step, due[step][0]).wait()

    return pl.pallas_call(
        body, name="gather_proj",
        grid_spec=pltpu.PrefetchScalarGridSpec(
            num_scalar_prefetch=1,
            grid=(N_DEV, n_i),
            in_specs=[pl.BlockSpec((tm, kdim), lambda j, i, order_ref: (i, 0)),
                      pl.BlockSpec(memory_space=pl.ANY)],
            out_specs=(pl.BlockSpec((tm, ncols), lambda j, i, order_ref: (i, order_ref[j])),
                       pl.BlockSpec(memory_space=pl.ANY)),
            scratch_shapes=[pltpu.VMEM((N_DEV, kdim, ncols), BF16),
                            pltpu.SemaphoreType.DMA((7,)), pltpu.SemaphoreType.DMA((7,)),
                            pltpu.SemaphoreType.DMA((N_DEV + 1,))]),
        out_shape=(jax.ShapeDtypeStruct((rows, N_DEV * ncols), BF16),
                   jax.ShapeDtypeStruct((N_DEV, kdim, ncols), BF16)),
        compiler_params=pltpu.CompilerParams(vmem_limit_bytes=VMEM_LIMIT),
    )(order, h, w_shard)


def _bucket_maps():
    a = np.arange(QB)[:, None]
    b = np.arange(2 * QB)[None, :]
    steps = a + QB - b
    maps = []
    for dil in DILATIONS:
        dist = np.maximum(steps, 0) * dil
        nf = np.maximum(dist, 1).astype(np.float32)
        large = 16 + (np.log(nf / np.float32(16)) / np.float32(math.log(128.0))
                      * np.float32(16)).astype(np.int32)
        large = np.minimum(large, N_BUCKETS - 1)
        maps.append(np.where(dist < 16, dist, large).astype(np.int32))
    band = (steps >= 0) & (steps <= N_STEPS)
    first = band & (b >= QB)
    masks = np.stack([first, band]).astype(np.int32)
    return np.stack(maps), masks


def _bias_expand(rel_bias, buckets, masks):
    def body(tab_ref, bk_ref, mk_ref, o_ref):
        for g in range(3):
            bk = bk_ref[g]
            for h in range(4):
                col = 4 * g + h
                val = jnp.zeros((QB, 2 * QB), F32)
                for k in range(N_BUCKETS):
                    val = jnp.where(bk == k, tab_ref[k, col], val)
                o_ref[g, 0, h] = jnp.where(mk_ref[0] != 0, val, NEG_INF)
                o_ref[g, 1, h] = jnp.where(mk_ref[1] != 0, val, NEG_INF)

    return pl.pallas_call(
        body, name="bias_expand",
        in_specs=[pl.BlockSpec(memory_space=pltpu.SMEM),
                  pl.BlockSpec(memory_space=pltpu.VMEM),
                  pl.BlockSpec(memory_space=pltpu.VMEM)],
        out_shape=jax.ShapeDtypeStruct((3, 2, 4, QB, 2 * QB), F32),
    )(rel_bias, buckets, masks)


def _bias_grad(ds1, ds2, ds3, buckets):
    def body(d1_ref, d2_ref, d3_ref, bk_ref, o_ref):
        for g, d_ref in enumerate((d1_ref, d2_ref, d3_ref)):
            bk = bk_ref[g]
            for h in range(4):
                dv = d_ref[h]
                for k in range(N_BUCKETS):
                    o_ref[k, 4 * g + h] = jnp.sum(jnp.where(bk == k, dv, 0.0))

    return pl.pallas_call(
        body, name="bias_grad",
        in_specs=[pl.BlockSpec(memory_space=pltpu.VMEM)] * 4,
        out_specs=pl.BlockSpec(memory_space=pltpu.SMEM),
        out_shape=jax.ShapeDtypeStruct((N_BUCKETS, N_HEADS), F32),
    )(ds1, ds2, ds3, buckets)


def _scratch_sets(rows):
    return 4 if rows <= 512 else 1


def _unit_chunks(dil, size=16):
    units = [(h, r) for h in range(4) for r in range(dil)]
    return [units[i:i + size] for i in range(0, len(units), size)]


def _residue_rows(src_ref, copies, h, residue):
    sl = slice(h * HD, (h + 1) * HD)
    if copies is None:
        return lambda r: src_ref[:, sl]
    buf = copies[h % len(copies)]
    buf[...] = src_ref[:, sl].astype(F32)
    return lambda r: buf[residue(r), :].astype(BF16)


def _attn_fwd(proj, bias, g):
    dil = DILATIONS[g]
    rows = QB * dil
    nsb = S // rows
    has_prev = nsb > 1

    def residue(r):
        return pl.ds(r, QB, stride=dil) if dil > 1 else pl.ds(0, QB)

    strided = dil > 1
    n_sets = _scratch_sets(rows)
    n_in = 6 if has_prev else 4

    def body(*refs):
        q_ref, kc_ref, vc_ref = refs[:3]
        kp_ref, vp_ref = refs[3:5] if has_prev else (None, None)
        b_ref, o_ref, l_ref = refs[n_in - 1:n_in + 2]
        scr = list(refs[n_in + 2:])
        ls = [scr.pop(0) for _ in range(4)]
        copies = {name: [scr.pop(0) for _ in range(n_sets)] if strided else None
                  for name in ("q", "kc", "vc", "o") + (("kp", "vp") if has_prev else ())}
        lane = lax.broadcasted_iota(jnp.int32, (QB, 128), 1)
        refs_of = {"q": q_ref, "kc": kc_ref, "vc": vc_ref, "kp": kp_ref, "vp": vp_ref}
        for chunk in _unit_chunks(dil):
            rows_of = {h: {name: _residue_rows(refs_of[name], copies[name], h, residue)
                           for name in refs_of if refs_of[name] is not None}
                       for h in sorted({h for h, _ in chunk})}

            def batch(name):
                return jnp.stack([rows_of[h][name](r) for h, r in chunk])

            q, k, v = batch("q"), batch("kc"), batch("vc")
            if has_prev:
                k = jnp.concatenate([batch("kp"), k], axis=1)
                v = jnp.concatenate([batch("vp"), v], axis=1)
                bias_b = jnp.stack([b_ref[h] for h, _ in chunk])
            else:
                bias_b = jnp.stack([b_ref[h, :, QB:] for h, _ in chunk])
            s = jnp.einsum("uqd,ukd->uqk", q, k, preferred_element_type=F32) * SCALE + bias_b
            m = jnp.max(s, axis=-1, keepdims=True)
            p = jnp.exp(s - m)
            l = jnp.sum(p, axis=-1, keepdims=True)
            o = jnp.einsum("uqk,ukd->uqd", p.astype(BF16), v, preferred_element_type=F32) / l
            lse = m + jnp.log(l)
            for i, (h, r) in enumerate(chunk):
                if strided:
                    copies["o"][h % n_sets][residue(r), :] = o[i]
                else:
                    o_ref[:, h * HD:(h + 1) * HD] = o[i]
                ls[h][r * QB:(r + 1) * QB, :] = jnp.where(lane == h, lse[i], 0.0)
            if strided:
                for h in sorted({h for h, _ in chunk}):
                    o_ref[:, h * HD:(h + 1) * HD] = copies["o"][h % n_sets][...]
        for r in range(dil):
            blk = slice(r * QB, (r + 1) * QB)
            l_ref[residue(r), :] = (ls[0][blk, :] + ls[1][blk, :]) + (ls[2][blk, :] + ls[3][blk, :])

    def row(b, n):
        return b * nsb + n

    def prev(b, n):
        return b * nsb + jnp.maximum(n - 1, 0)

    in_specs = [
        pl.BlockSpec((rows, GW), lambda b, n: (row(b, n), CB_Q + g)),
        pl.BlockSpec((rows, GW), lambda b, n: (row(b, n), CB_K + g)),
        pl.BlockSpec((rows, GW), lambda b, n: (row(b, n), CB_V + g)),
    ]
    args = [proj, proj, proj]
    n_copied = (4 + (2 if has_prev else 0)) * (n_sets if strided else 0)
    scratch = [pltpu.VMEM((rows, 128), F32)] * (4 + n_copied)
    if has_prev:
        in_specs += [pl.BlockSpec((rows, GW), lambda b, n: (prev(b, n), CB_K + g)),
                     pl.BlockSpec((rows, GW), lambda b, n: (prev(b, n), CB_V + g))]
        args += [proj, proj]
    in_specs.append(pl.BlockSpec((None, None, 4, QB, 2 * QB),
                                 lambda b, n: (g, jnp.minimum(n, 1), 0, 0, 0)))
    args.append(bias)
    return pl.pallas_call(
        body, name=f"attn_fwd{g}",
        grid=(BL, nsb),
        in_specs=in_specs,
        out_specs=(pl.BlockSpec((rows, GW), lambda b, n: (row(b, n), 0)),
                   pl.BlockSpec((rows, 128), lambda b, n: (row(b, n), 0))),
        out_shape=(jax.ShapeDtypeStruct((T, GW), F32), jax.ShapeDtypeStruct((T, 128), F32)),
        scratch_shapes=scratch,
        compiler_params=pltpu.CompilerParams(vmem_limit_bytes=VMEM_LIMIT),
    )(*args)


def _attn_bwd(proj, d_out, stats, bias, dproj, g):
    dil = DILATIONS[g]
    rows = QB * dil
    nsb = S // rows
    has_prev = nsb > 1
    n_steps = nsb + 1 if has_prev else 1
    n_in = 7 + (2 if has_prev else 0)

    def residue(r):
        return pl.ds(r, QB, stride=dil) if dil > 1 else pl.ds(0, QB)

    strided = dil > 1
    n_sets = _scratch_sets(rows)

    def body(*refs):
        q_ref, kc_ref, vc_ref, do_ref, st_ref, b_ref = refs[:6]
        kp_ref, vp_ref = refs[6:8] if has_prev else (None, None)
        out_ref, db_ref = refs[n_in], refs[n_in + 1]
        scr = list(refs[n_in + 2:])
        sq, sk, sv, sems = [scr.pop(0) for _ in range(4)]
        carry = scr.pop(0) if has_prev else None
        sts = scr.pop(0) if strided else st_ref
        copies = {name: [scr.pop(0) for _ in range(n_sets)] if strided else None
                  for name in ("q", "kc", "vc", "do", "dq", "dk", "dv") + (("kp", "vp") if has_prev else ())}
        b, n = pl.program_id(0), pl.program_id(1)

        @pl.when((b == 0) & (n == 0))
        def _():
            db_ref[...] = jnp.zeros_like(db_ref)

        def finish(h, r, dq, dk, dv):
            if strided:
                for name, val in (("dq", dq), ("dk", dk), ("dv", dv)):
                    copies[name][h % n_sets][residue(r), :] = val
            else:
                sl = slice(h * HD, (h + 1) * HD)
                sq[:, sl], sk[:, sl], sv[:, sl] = dq.astype(BF16), dk.astype(BF16), dv.astype(BF16)

        def finish_head(h):
            if strided:
                sl = slice(h * HD, (h + 1) * HD)
                sq[:, sl] = copies["dq"][h % n_sets][...].astype(BF16)
                sk[:, sl] = copies["dk"][h % n_sets][...].astype(BF16)
                sv[:, sl] = copies["dv"][h % n_sets][...].astype(BF16)

        def write_block(blk_idx):
            row0 = pl.multiple_of(blk_idx * rows, rows)
            _write_columns([(sq, CB * (CB_Q + g)), (sk, CB * (CB_K + g)), (sv, CB * (CB_V + g))],
                           out_ref, row0, sems)

        def carried(h, r):
            blk = slice(r * QB, (r + 1) * QB)
            return ((blk, slice(h * HD, (h + 1) * HD)), (blk, slice(GW + h * HD, GW + (h + 1) * HD)),
                    (blk, slice(2 * GW + h * HD, 2 * GW + (h + 1) * HD)))

        if has_prev:
            @pl.when(n == 0)
            def _():
                carry[...] = jnp.zeros_like(carry)

            @pl.when(n == nsb)
            def _():
                for h in range(4):
                    for r in range(dil):
                        cq, ck, cv = carried(h, r)
                        finish(h, r, carry[cq], carry[ck], carry[cv])
                    finish_head(h)
                write_block(b * nsb + nsb - 1)

        @pl.when(n < nsb)
        def _():
            if strided:
                for r in range(dil):
                    sts[r * QB:(r + 1) * QB, :] = st_ref[residue(r), :]
            refs_of = {"q": q_ref, "kc": kc_ref, "vc": vc_ref, "do": do_ref, "kp": kp_ref, "vp": vp_ref}
            for chunk in _unit_chunks(dil):
                heads = sorted({h for h, _ in chunk})
                rows_of = {h: {name: _residue_rows(refs_of[name], copies[name], h, residue)
                               for name in refs_of if refs_of[name] is not None}
                           for h in heads}

                def batch(name):
                    return jnp.stack([rows_of[h][name](r) for h, r in chunk])

                q, k, v, do = batch("q"), batch("kc"), batch("vc"), batch("do")
                if has_prev:
                    k = jnp.concatenate([batch("kp"), k], axis=1)
                    v = jnp.concatenate([batch("vp"), v], axis=1)
                    bias_b = jnp.stack([b_ref[h] for h, _ in chunk])
                else:
                    bias_b = jnp.stack([b_ref[h, :, QB:] for h, _ in chunk])
                lse = jnp.stack([sts[r * QB:(r + 1) * QB, h:h + 1] for h, r in chunk])
                delta = jnp.stack([sts[r * QB:(r + 1) * QB, 4 + h:5 + h] for h, r in chunk])
                s = jnp.einsum("uqd,ukd->uqk", q, k, preferred_element_type=F32) * SCALE + bias_b
                p = jnp.exp(s - lse)
                ds = p * (jnp.einsum("uqd,ukd->uqk", do, v, preferred_element_type=F32) - delta)
                for h in heads:
                    mine = [ds[i] for i, (hh, _) in enumerate(chunk) if hh == h]
                    tot = mine[0]
                    for extra in mine[1:]:
                        tot = tot + extra
                    if has_prev:
                        db_ref[h] += tot
                    else:
                        db_ref[h, :, QB:] += tot
                dsb, pb = ds.astype(BF16), p.astype(BF16)
                dq = jnp.einsum("uqk,ukd->uqd", dsb, k, preferred_element_type=F32) * SCALE
                dk = jnp.einsum("uqk,uqd->ukd", dsb, q, preferred_element_type=F32) * SCALE
                dv = jnp.einsum("uqk,uqd->ukd", pb, do, preferred_element_type=F32)
                for i, (h, r) in enumerate(chunk):
                    if has_prev:
                        cq, ck, cv = carried(h, r)
                        finish(h, r, carry[cq], carry[ck] + dk[i, :QB], carry[cv] + dv[i, :QB])
                        carry[cq] = dq[i]
                        carry[ck] = dk[i, QB:]
                        carry[cv] = dv[i, QB:]
                    else:
                        finish(h, r, dq[i], dk[i], dv[i])
                for h in heads:
                    finish_head(h)
            if has_prev:
                @pl.when(n > 0)
                def _():
                    write_block(b * nsb + n - 1)
            else:
                write_block(b)

    def row(b, n):
        return b * nsb + jnp.minimum(n, nsb - 1)

    def prev(b, n):
        return b * nsb + jnp.maximum(jnp.minimum(n, nsb - 1) - 1, 0)

    in_specs = [
        pl.BlockSpec((rows, GW), lambda b, n: (row(b, n), CB_Q + g)),
        pl.BlockSpec((rows, GW), lambda b, n: (row(b, n), CB_K + g)),
        pl.BlockSpec((rows, GW), lambda b, n: (row(b, n), CB_V + g)),
        pl.BlockSpec((rows, GW), lambda b, n: (row(b, n), 0)),
        pl.BlockSpec((rows, 128), lambda b, n: (row(b, n), 0)),
        pl.BlockSpec((None, None, 4, QB, 2 * QB),
                     lambda b, n: (g, jnp.minimum(jnp.minimum(n, nsb - 1), 1), 0, 0, 0)),
    ]
    args = [proj, proj, proj, d_out, stats, bias]
    scratch = [pltpu.VMEM((rows, GW), BF16)] * 3 + [pltpu.SemaphoreType.DMA((3,))]
    if has_prev:
        in_specs += [pl.BlockSpec((rows, GW), lambda b, n: (prev(b, n), CB_K + g)),
                     pl.BlockSpec((rows, GW), lambda b, n: (prev(b, n), CB_V + g))]
        args += [proj, proj]
        scratch.append(pltpu.VMEM((rows, 3 * GW), F32))
    if strided:
        n_copied = (7 + (2 if has_prev else 0)) * n_sets
        scratch += [pltpu.VMEM((rows, 128), F32)] * (1 + n_copied)
    in_specs.append(pl.BlockSpec(memory_space=pl.ANY))
    args.append(dproj)
    return pl.pallas_call(
        body, name=f"attn_bwd{g}",
        grid=(BL, n_steps),
        in_specs=in_specs,
        out_specs=(pl.BlockSpec(memory_space=pl.ANY),
                   pl.BlockSpec((4, QB, 2 * QB), lambda b, n: (0, 0, 0))),
        out_shape=(jax.ShapeDtypeStruct((T, NCOL), BF16),
                   jax.ShapeDtypeStruct((4, QB, 2 * QB), F32)),
        scratch_shapes=scratch,
        input_output_aliases={len(args) - 1: 0},
        compiler_params=pltpu.CompilerParams(vmem_limit_bytes=VMEM_LIMIT),
    )(*args)


def _tail(x2, tgt2, mod3, o_g, lse_g, proj, w_ao, w_co, w_o, conv_w, conv_b, ln_g, ln_b):
    tm = 256
    per_seq = S // tm
    halo = 16

    def body(x_ref, t_ref, mod_ref, o1_ref, o2_ref, o3_ref, l1_ref, l2_ref, l3_ref,
             ga_ref, u_ref, bg_ref, cg_ref, gc_ref, ma_ref, mc_ref, up_ref, cp_ref,
             wao_ref, wco_ref, wo_ref, cw_ref, cb_ref, lg_ref, lb_ref,
             dproj_ref, dyc_ref, do_ref, st_ref, dxd_ref,
             mg_ref, dy_ref, ain_ref, dao_ref, sin_ref, dso_ref, vec_ref,
             dga_s, dbg_s, dgm_s, sems):
        i = pl.program_id(0)
        bidx = i // per_seq
        first = (i % per_seq) == 0

        @pl.when(i == 0)
        def _():
            vec_ref[...] = jnp.zeros_like(vec_ref)

        l1, l2, l3 = l1_ref[...], l2_ref[...], l3_ref[...]
        mx = jnp.maximum(jnp.maximum(l1, l2), l3)
        e1, e2, e3 = jnp.exp(l1 - mx), jnp.exp(l2 - mx), jnp.exp(l3 - mx)
        esum = e1 + e2 + e3
        lse_tot = mx + jnp.log(esum)
        w1, w2, w3 = e1 / esum, e2 / esum, e3 / esum

        def per_head(wv):
            return jnp.concatenate([jnp.broadcast_to(wv[:, h:h + 1], (tm, HD)) for h in range(4)], axis=1)

        o = per_head(w1) * o1_ref[...] + per_head(w2) * o2_ref[...] + per_head(w3) * o3_ref[...]

        ga = ga_ref[...].astype(F32)
        sig_ga = _sigmoid(ga)
        silu_ga = ga * sig_ga
        a_in = (o * silu_ga).astype(BF16)
        a_out = _dot(a_in, wao_ref[...])

        u = u_ref[...].astype(F32)
        cg = cg_ref[...].astype(F32)
        z = cg * u
        zp = cp_ref[...].astype(F32) * up_ref[...].astype(F32)
        zp = jnp.where(first, 0.0, zp)
        zcat = jnp.concatenate([zp, z], axis=0)
        z1 = pltpu.roll(zcat, 1, 0)[halo:]
        z2 = pltpu.roll(zcat, 2, 0)[halo:]
        y_conv = cw_ref[0:1, :] * z2 + cw_ref[1:2, :] * z1 + cw_ref[2:3, :] * z + cb_ref[...]
        gc = gc_ref[...].astype(F32)
        sig_gc = _sigmoid(gc)
        silu_gc = gc * sig_gc
        bg = bg_ref[...].astype(F32)
        s_in = (bg * y_conv * silu_gc).astype(BF16)
        s_out = _dot(s_in, wco_ref[...])

        sa = _sigmoid(ma_ref[...].astype(F32))
        sc = _sigmoid(mc_ref[...].astype(F32))
        merged = (sa * a_out + sc * s_out).astype(BF16)
        y = _dot(merged, wo_ref[...])
        gate1 = 1.0 + mod_ref[0, 2:3, :]
        xv = x_ref[...]
        resid = ALPHA * xv + gate1 * y
        mu = jnp.mean(resid, axis=1, keepdims=True)
        xc = resid - mu
        var = jnp.mean(xc * xc, axis=1, keepdims=True)
        rstd = lax.rsqrt(var + LN_EPS)
        xhat = xc * rstd
        lg = lg_ref[...]
        err = xhat * lg + lb_ref[...] - t_ref[...]
        vec_ref[3:4, :] += (0.5 / D) * jnp.sum(err * err, axis=0, keepdims=True)

        dout = err * (1.0 / D)
        vec_ref[1:2, :] += jnp.sum(dout * xhat, axis=0, keepdims=True)
        vec_ref[2:3, :] += jnp.sum(dout, axis=0, keepdims=True)
        dxh = dout * lg
        dres = rstd * (dxh - jnp.mean(dxh, axis=1, keepdims=True)
                       - xhat * jnp.mean(dxh * xhat, axis=1, keepdims=True))
        dxd_ref[...] = ALPHA * dres
        dgate = jnp.sum(dres * y, axis=0, keepdims=True)
        vec_ref[4:5, :] += jnp.where(bidx == 0, dgate, 0.0)
        vec_ref[5:6, :] += jnp.where(bidx == 1, dgate, 0.0)
        dy = (dres * gate1).astype(BF16)

        dmerged = _dot_nt(dy, wo_ref[...])
        da_out = (dmerged * sa).astype(BF16)
        ds_out = (dmerged * sc).astype(BF16)
        dgm_s[:, 2 * D:3 * D] =(dmerged * s_out * sc * (1.0 - sc)).astype(BF16)
        dgm_s[:, D:2 * D] =(dmerged * a_out * sa * (1.0 - sa)).astype(BF16)
        da_in = _dot_nt(da_out, wao_ref[...])
        ds_in = _dot_nt(ds_out, wco_ref[...])

        d_o = da_in * silu_ga
        do_ref[...] = d_o.astype(BF16)
        dga_s[...] =(da_in * o * (sig_ga * (1.0 + ga * (1.0 - sig_ga)))).astype(BF16)
        lane = lax.broadcasted_iota(jnp.int32, (tm, 128), 1)
        stats = lse_tot
        od = o * d_o
        for h in range(4):
            delta = jnp.sum(od[:, h * HD:(h + 1) * HD], axis=1, keepdims=True)
            stats = jnp.where(lane == 4 + h, delta, stats)
        st_ref[...] = stats

        dbg_s[...] =(ds_in * y_conv * silu_gc).astype(BF16)
        dyc = ds_in * bg * silu_gc
        dyc_ref[...] = dyc
        vec_ref[0:1, :] += jnp.sum(dyc, axis=0, keepdims=True)
        dgm_s[:, 0:D] =(ds_in * bg * y_conv * (sig_gc * (1.0 + gc * (1.0 - sig_gc)))).astype(BF16)

        mg_ref[...] = merged
        dy_ref[...] = dy
        ain_ref[...] = a_in
        dao_ref[...] = da_out
        sin_ref[...] = s_in
        dso_ref[...] = ds_out
        _write_columns([(dga_s, CB * CB_GA), (dbg_s, D * KB_BG), (dgm_s, D * KB_GC)],
                       dproj_ref, pl.multiple_of(i * tm, tm), sems)

    def tile(width, cblk=0):
        return pl.BlockSpec((tm, width), lambda i: (i, cblk))

    def whole(shape):
        return pl.BlockSpec(shape, lambda i: tuple(0 for _ in shape))

    prev_rows = lambda i: (jnp.maximum(i * (tm // halo) - 1, 0),)
    in_specs = [
        tile(D), tile(D), pl.BlockSpec((1, 3, D), lambda i: (i // per_seq, 0, 0)),
        tile(GW), tile(GW), tile(GW), tile(128), tile(128), tile(128),
        tile(GW, CB_GA), tile(D, KB_U), tile(D, KB_BG), tile(D, KB_CG), tile(D, KB_GC),
        tile(D, KB_MA), tile(D, KB_MC),
        pl.BlockSpec((halo, D), lambda i: (*prev_rows(i), KB_U)),
        pl.BlockSpec((halo, D), lambda i: (*prev_rows(i), KB_CG)),
        whole((GW, D)), whole((D, D)), whole((D, D)),
        whole((3, D)), whole((1, D)), whole((1, D)), whole((1, D)),
    ]
    out_specs = (
        pl.BlockSpec(memory_space=pl.ANY), tile(D), tile(GW), tile(128), tile(D),
        tile(D), tile(D), tile(GW), tile(D), tile(D), tile(D),
        pl.BlockSpec((8, D), lambda i: (0, 0)),
    )
    out_shape = (
        jax.ShapeDtypeStruct((T, NCOL), BF16),
        jax.ShapeDtypeStruct((T, D), F32),
        jax.ShapeDtypeStruct((T, GW), BF16),
        jax.ShapeDtypeStruct((T, 128), F32),
        jax.ShapeDtypeStruct((T, D), F32),
        jax.ShapeDtypeStruct((T, D), BF16),
        jax.ShapeDtypeStruct((T, D), BF16),
        jax.ShapeDtypeStruct((T, GW), BF16),
        jax.ShapeDtypeStruct((T, D), BF16),
        jax.ShapeDtypeStruct((T, D), BF16),
        jax.ShapeDtypeStruct((T, D), BF16),
        jax.ShapeDtypeStruct((8, D), F32),
    )
    return pl.pallas_call(
        body, name="tail",
        grid=(T // tm,),
        in_specs=in_specs, out_specs=out_specs, out_shape=out_shape,
        scratch_shapes=[pltpu.VMEM((tm, GW), BF16), pltpu.VMEM((tm, D), BF16), pltpu.VMEM((tm, 3 * D), BF16),
                        pltpu.SemaphoreType.DMA((3,))],
        compiler_params=pltpu.CompilerParams(vmem_limit_bytes=VMEM_LIMIT),
    )(x2, tgt2, mod3, *o_g, *lse_g, proj, proj, proj, proj, proj, proj, proj, proj, proj,
      w_ao, w_co, w_o, conv_w, conv_b, ln_g, ln_b)


def _conv_bwd(dyc, proj, conv_w, dproj):
    tm = 512
    per_seq = S // tm
    halo = 16

    def body(d_ref, dn_ref, u_ref, c_ref, up_ref, cp_ref, cw_ref, _, dproj_ref, g_ref, du_s, dc_s, sems):
        i = pl.program_id(0)
        first = (i % per_seq) == 0
        last = (i % per_seq) == per_seq - 1

        @pl.when(i == 0)
        def _():
            g_ref[...] = jnp.zeros_like(g_ref)

        d = d_ref[...]
        dn = jnp.where(last, 0.0, dn_ref[...])
        dcat = jnp.concatenate([d, dn], axis=0)
        d1 = pltpu.roll(dcat, tm + 8 - 1, 0)[:tm]
        d2 = pltpu.roll(dcat, tm + 8 - 2, 0)[:tm]
        dz = cw_ref[2:3, :] * d + cw_ref[1:2, :] * d1 + cw_ref[0:1, :] * d2
        u = u_ref[...].astype(F32)
        cg = c_ref[...].astype(F32)
        du_s[...] = (dz * cg).astype(BF16)
        dc_s[...] = (dz * u).astype(BF16)
        _write_columns([(du_s, D * KB_U), (dc_s, D * KB_CG)], dproj_ref, pl.multiple_of(i * tm, tm), sems)

        z = cg * u
        zp = jnp.where(first, 0.0, cp_ref[...].astype(F32) * up_ref[...].astype(F32))
        zcat = jnp.concatenate([zp, z], axis=0)
        z1 = pltpu.roll(zcat, 1, 0)[halo:]
        z2 = pltpu.roll(zcat, 2, 0)[halo:]
        g_ref[0:1, :] += jnp.sum(d * z2, axis=0, keepdims=True)
        g_ref[1:2, :] += jnp.sum(d * z1, axis=0, keepdims=True)
        g_ref[2:3, :] += jnp.sum(d * z, axis=0, keepdims=True)

    n_tiles = T // tm
    prev_rows = lambda i: jnp.maximum(i * (tm // halo) - 1, 0)
    next_rows = lambda i: jnp.minimum((i + 1) * (tm // 8), T // 8 - 1)
    return pl.pallas_call(
        body, name="conv_bwd",
        grid=(n_tiles,),
        in_specs=[pl.BlockSpec((tm, D), lambda i: (i, 0)),
                  pl.BlockSpec((8, D), lambda i: (next_rows(i), 0)),
                  pl.BlockSpec((tm, D), lambda i: (i, KB_U)),
                  pl.BlockSpec((tm, D), lambda i: (i, KB_CG)),
                  pl.BlockSpec((halo, D), lambda i: (prev_rows(i), KB_U)),
                  pl.BlockSpec((halo, D), lambda i: (prev_rows(i), KB_CG)),
                  pl.BlockSpec((3, D), lambda i: (0, 0)),
                  pl.BlockSpec(memory_space=pl.ANY)],
        out_specs=(pl.BlockSpec(memory_space=pl.ANY),
                   pl.BlockSpec((8, D), lambda i: (0, 0))),
        out_shape=(jax.ShapeDtypeStruct((T, NCOL), BF16),
                   jax.ShapeDtypeStruct((8, D), F32)),
        scratch_shapes=[pltpu.VMEM((tm, D), BF16), pltpu.VMEM((tm, D), BF16), pltpu.SemaphoreType.DMA((2,))],
        input_output_aliases={7: 0},
        compiler_params=pltpu.CompilerParams(vmem_limit_bytes=VMEM_LIMIT),
    )(dyc, dyc, proj, proj, proj, proj, conv_w, dproj)


def _dh_dx(dproj, w_in_all, x2, dxd, mod3, chip_sums, hops=(), parts0=None):
    tm = 1024
    per_seq = S // tm
    n = len(chip_sums)
    n_in = 5 + n + (0 if parts0 is None else 1)

    def body(*refs):
        d_ref, w_ref, x_ref, dxd_ref, mod_ref = refs[:5]
        ins = refs[5:5 + n]
        gx_ref, vec_ref = refs[n_in:n_in + 2]
        outs = refs[n_in + 2:n_in + 2 + n]
        acc, send_sems, recv_sems, local_sems = refs[n_in + 2 + n:]
        i, jj = pl.program_id(0), pl.program_id(1)

        @pl.when((i == 0) & (jj == 0))
        def _():
            vec_ref[...] = jnp.zeros_like(vec_ref)
            if n:
                sends, _, mine = _chip_copies(ins, outs, send_sems, recv_sems, local_sems, hops)
                for cp in sends + mine:
                    cp.start()

        if n:
            @pl.when((i == T // tm - 1) & (jj == N_DEV - 1))
            def _():
                sends, arrivals, mine = _chip_copies(ins, outs, send_sems, recv_sems, local_sems, hops)
                for cp in arrivals:
                    cp.wait_recv()
                for cp in sends:
                    cp.wait_send()
                for cp in mine:
                    cp.wait()

        @pl.when(jj == 0)
        def _():
            acc[...] = jnp.zeros_like(acc)

        acc[...] += _dot_nt(d_ref[...], w_ref[...])

        @pl.when(jj == N_DEV - 1)
        def _():
            dh = acc[...]
            bidx = i // per_seq
            gx_ref[...] = dxd_ref[...] + dh * (1.0 + mod_ref[0, 1:2, :])
            dshift = jnp.sum(dh, axis=0, keepdims=True)
            dscale = jnp.sum(dh * x_ref[...], axis=0, keepdims=True)
            vec_ref[0:1, :] += jnp.where(bidx == 0, dshift, 0.0)
            vec_ref[1:2, :] += jnp.where(bidx == 1, dshift, 0.0)
            vec_ref[2:3, :] += jnp.where(bidx == 0, dscale, 0.0)
            vec_ref[3:4, :] += jnp.where(bidx == 1, dscale, 0.0)

    any_spec = pl.BlockSpec(memory_space=pl.ANY)
    res = pl.pallas_call(
        body, name="dh_dx",
        grid=(T // tm, N_DEV),
        in_specs=[
            pl.BlockSpec((tm, SHARD), lambda i, jj: (i, jj)),
            pl.BlockSpec((None, D, SHARD), lambda i, jj: (jj, 0, 0)),
            pl.BlockSpec((tm, D), lambda i, jj: (i, 0)),
            pl.BlockSpec((tm, D), lambda i, jj: (i, 0)),
            pl.BlockSpec((1, 3, D), lambda i, jj: (i // per_seq, 0, 0))] + [any_spec] * (n_in - 5),
        out_specs=(pl.BlockSpec((tm, D), lambda i, jj: (i, 0)),
                   pl.BlockSpec((8, D), lambda i, jj: (0, 0))) + (any_spec,) * n,
        out_shape=(jax.ShapeDtypeStruct((T, D), F32), jax.ShapeDtypeStruct((8, D), F32))
                  + tuple(jax.ShapeDtypeStruct(a.shape, a.dtype) for a in chip_sums),
        scratch_shapes=[pltpu.VMEM((tm, D), F32), pltpu.SemaphoreType.DMA((max(3 * n, 1),)),
                        pltpu.SemaphoreType.DMA((max(3 * n, 1),)), pltpu.SemaphoreType.DMA((max(n, 1),))],
        input_output_aliases={} if parts0 is None else {5 + n: 2},
        compiler_params=pltpu.CompilerParams(vmem_limit_bytes=VMEM_LIMIT),
    )(dproj, w_in_all, x2, dxd, mod3, *chip_sums, *([] if parts0 is None else [parts0]))
    return res[0], res[1], res[2:]


def _mm_tn(a, b, tn, blocks_leading, name):
    kk, m = a.shape
    n = b.shape[1]
    tk = 2048

    def body(a_ref, b_ref, o_ref, acc):
        @pl.when(pl.program_id(1) == 0)
        def _():
            acc[...] = jnp.zeros_like(acc)

        acc[...] += _dot_tn(a_ref[...], b_ref[...])

        @pl.when(pl.program_id(1) == kk // tk - 1)
        def _():
            o_ref[...] = acc[...].astype(BF16)

    if blocks_leading:
        out_spec = pl.BlockSpec((None, m, tn), lambda j, k: (j, 0, 0))
        out_shape = jax.ShapeDtypeStruct((n // tn, m, tn), BF16)
    else:
        out_spec = pl.BlockSpec((m, tn), lambda j, k: (0, j))
        out_shape = jax.ShapeDtypeStruct((m, n), BF16)
    return pl.pallas_call(
        body, name=name,
        grid=(n // tn, kk // tk),
        in_specs=[pl.BlockSpec((tk, m), lambda j, k: (k, 0)),
                  pl.BlockSpec((tk, tn), lambda j, k: (k, j))],
        out_specs=out_spec, out_shape=out_shape,
        scratch_shapes=[pltpu.VMEM((m, tn), F32)],
        compiler_params=pltpu.CompilerParams(vmem_limit_bytes=VMEM_LIMIT),
    )(a, b)


def _adamw(parts, w, m, v, name, row_tile=None):
    n_parts, rows, cols = parts.shape
    tr = rows if row_tile is None else row_tile
    c1 = 1.0 - ADAM_B1 ** ADAM_STEP
    c2 = 1.0 - ADAM_B2 ** ADAM_STEP

    def body(p_ref, w_ref, m_ref, v_ref, g_ref, d_ref, nm_ref, nv_ref):
        g = p_ref[0].astype(F32)
        for s in range(1, n_parts):
            g = g + p_ref[s].astype(F32)
        nm = ADAM_B1 * m_ref[...] + (1.0 - ADAM_B1) * g
        nv = ADAM_B2 * v_ref[...] + (1.0 - ADAM_B2) * (g * g)
        m_hat = nm / c1
        v_hat = nv / c2
        g_ref[...] = g
        d_ref[...] = -ADAM_LR * (m_hat / (jnp.sqrt(v_hat) + ADAM_EPS) + ADAM_WD * w_ref[...])
        nm_ref[...] = nm
        nv_ref[...] = nv

    blk = pl.BlockSpec((tr, cols), lambda i: (i, 0))
    shp = jax.ShapeDtypeStruct((rows, cols), F32)
    return pl.pallas_call(
        body, name=name,
        grid=(rows // tr,),
        in_specs=[pl.BlockSpec((n_parts, tr, cols), lambda i: (0, i, 0)), blk, blk, blk],
        out_specs=(blk, blk, blk, blk),
        out_shape=(shp, shp, shp, shp),
        compiler_params=pltpu.CompilerParams(vmem_limit_bytes=VMEM_LIMIT),
    )(parts, w, m, v)


def _loss_sum(rows):
    def body(r_ref, o_ref):
        o_ref[...] = jnp.sum(jnp.sum(r_ref[...], axis=0, keepdims=True), axis=1, keepdims=True)

    return pl.pallas_call(body, name="loss_sum", out_shape=jax.ShapeDtypeStruct((1, 1), F32))(rows)


def _local_step(x2, tgt2, mod3, h, proj, w_ao, w_co, w_o, conv_w, conv_b, rel_bias, ln_g, ln_b):
    buckets_np, masks_np = _bucket_maps()
    buckets, masks = jnp.asarray(buckets_np), jnp.asarray(masks_np)

    bias = _bias_expand(rel_bias, buckets, masks)
    fwd = [_attn_fwd(proj, bias, g) for g in range(3)]
    o_g = [f[0] for f in fwd]
    lse_g = [f[1] for f in fwd]

    (dproj, dyc, d_o, stats, dxd, merged, dy, a_in, da_out, s_in, ds_out, tail_vec) = _tail(
        x2, tgt2, mod3, o_g, lse_g, proj, w_ao, w_co, w_o, conv_w, conv_b, ln_g, ln_b)

    dbias = []
    for g in range(3):
        dproj, db = _attn_bwd(proj, d_o, stats, bias, dproj, g)
        dbias.append(db)
    g_rel_bias = _bias_grad(*dbias, buckets)
    dproj, conv_vec = _conv_bwd(dyc, proj, conv_w, dproj)

    gw_o = _mm_tn(merged, dy, D, False, "gw_o")
    gw_co = _mm_tn(s_in, ds_out, D, False, "gw_conv_out")
    gw_ao = _mm_tn(a_in, da_out, D, False, "gw_attn_out")
    gw_ao = jnp.transpose(gw_ao.reshape(GW, N_DEV, D // N_DEV), (1, 0, 2))
    return dproj, dxd, gw_ao, gw_co, gw_o, conv_vec, g_rel_bias, tail_vec


def kernel(x, c, w_ada, b_ada, w_in, conv_w, conv_b, rel_bias, w_attn_out, w_conv_out, w_o, ln_g, ln_b, loss_target, m_w_ada, m_b_ada, m_w_in, m_conv_w, m_conv_b, m_rel_bias, m_w_attn_out, m_w_conv_out, m_w_o, m_ln_g, m_ln_b, v_w_ada, v_b_ada, v_w_in, v_conv_w, v_conv_b, v_rel_bias, v_w_attn_out, v_w_conv_out, v_w_o, v_ln_g, v_ln_b):
    me = _my_index()
    x2 = x.reshape(T, D)
    tgt2 = loss_target.reshape(T, D)

    b_cols = lax.dynamic_slice(b_ada, (0, me * ADA_SHARD), (1, ADA_SHARD))
    c_g, mod_in = _mod_exchange(jnp.pad(c, ((0, 8 - BL), (0, 0))), w_ada[0], b_cols)
    c_all = c_g[:, 0:BL, :].reshape(N_DEV * BL, D)
    mod3 = jnp.transpose(mod_in[:, 0:BL, :], (1, 0, 2)).reshape(BL, 3, D)

    w_ao_g, w_co_g, w_o_g, conv_w_g = _all_gather(
        [w_attn_out[0].astype(BF16), w_conv_out[0].astype(BF16), w_o[0].astype(BF16), conv_w[0]],
        [jax.ShapeDtypeStruct((N_DEV, GW, D // N_DEV), BF16),
         jax.ShapeDtypeStruct((N_DEV, D // N_DEV, D), BF16),
         jax.ShapeDtypeStruct((N_DEV, D // N_DEV, D), BF16),
         jax.ShapeDtypeStruct((N_DEV, 3, D // N_DEV), F32)],
        [_slot_leading] * 4,
        "gather_weights")
    w_ao_full = jnp.transpose(w_ao_g, (1, 0, 2)).reshape(GW, D)
    w_co_full = w_co_g.reshape(D, D)
    w_o_full = w_o_g.reshape(D, D)
    conv_w_full = jnp.transpose(conv_w_g, (1, 0, 2)).reshape(3, D)

    h = _prep_h(x2, mod3)
    proj, w_in_all = _gather_proj(_shard_order(), h, w_in[0].astype(BF16), 1024)
    (dproj, dxd, gw_ao, gw_co, gw_o, conv_vec, g_rel_bias, tail_vec) = _local_step(
        x2, tgt2, mod3, h, proj, w_ao_full, w_co_full, w_o_full,
        conv_w_full, conv_b, rel_bias, ln_g, ln_b)

    g_conv_w_blocks = jnp.transpose(conv_vec[0:3].reshape(3, N_DEV, D // N_DEV), (1, 0, 2))
    partials = [gw_ao, gw_co.reshape(N_DEV, D // N_DEV, D), gw_o.reshape(N_DEV, D // N_DEV, D), g_conv_w_blocks]
    w_in_sums, w_in_parts, sib = _gw_in_pair(
        _slice_order(), h, dproj, partials,
        [jax.ShapeDtypeStruct((4, GW, D // N_DEV), BF16),
         jax.ShapeDtypeStruct((4, D // N_DEV, D), BF16),
         jax.ShapeDtypeStruct((4, D // N_DEV, D), BF16),
         jax.ShapeDtypeStruct((4, 3, D // N_DEV), F32)])
    names = ["w_attn_out", "w_conv_out", "w_o", "conv_w"]
    core = lax.axis_index("c").astype(jnp.int32).reshape(1)
    chip_sums = [w_in_sums] + [_pair_add(core, partials[a], sib[a], None, "pair_add_" + names[a])
                               for a in range(4)]
    hops = [(3,)] + [(1, 2, 3)] * 4
    grad_x, mod_vec, (r_in, r_ao, r_co, r_o, r_cw) = _dh_dx(
        dproj, w_in_all, x2, dxd, mod3, chip_sums, hops, w_in_parts)

    small = jnp.concatenate([
        tail_vec[0:4],
        jnp.pad(g_rel_bias.reshape(1, N_BUCKETS * N_HEADS), ((0, 0), (0, D - N_BUCKETS * N_HEADS))),
        jnp.zeros((3, D), F32)], axis=0)
    dmod = jnp.concatenate([mod_vec[0:2], mod_vec[2:4], tail_vec[4:6]], axis=1)
    small_g, dmod_g = _all_gather(
        [small, dmod],
        [jax.ShapeDtypeStruct((N_DEV, 8, D), F32), jax.ShapeDtypeStruct((N_DEV, BL, 3 * D), F32)],
        [_slot_leading] * 2, "gather_small")
    dmod_all = dmod_g.reshape(N_DEV * BL, 3 * D)
    loss = _loss_sum(small_g[:, 3, :]).reshape(())
    g_w_ada = _ada_bwd(jnp.transpose(c_all), lax.dynamic_slice(dmod_all, (0, me * ADA_SHARD),
                                                               (N_DEV * BL, ADA_SHARD)))

    def upd(parts, w, m, v, name, row_tile=None):
        shape = w.shape
        w2, m2, v2 = (t.reshape(parts.shape[1:]) for t in (w, m, v))
        return tuple(t.reshape(shape) for t in _adamw(parts, w2, m2, v2, name, row_tile))

    res = {
        "w_ada": upd(g_w_ada[None], w_ada, m_w_ada, v_w_ada, "adam_w_ada", 256),
        "b_ada": upd(dmod_all[:, None, :], b_ada, m_b_ada, v_b_ada, "adam_b_ada"),
        "w_in": upd(r_in, w_in, m_w_in, v_w_in, "adam_w_in", 128),
        "conv_w": upd(r_cw, conv_w, m_conv_w, v_conv_w, "adam_conv_w"),
        "conv_b": upd(small_g[:, 0:1, :], conv_b, m_conv_b, v_conv_b, "adam_conv_b"),
        "rel_bias": upd(small_g[:, 4, :N_BUCKETS * N_HEADS].reshape(N_DEV, N_BUCKETS, N_HEADS),
                        rel_bias, m_rel_bias, v_rel_bias, "adam_rel_bias"),
        "w_attn_out": upd(r_ao, w_attn_out, m_w_attn_out, v_w_attn_out, "adam_w_attn_out"),
        "w_conv_out": upd(r_co, w_conv_out, m_w_conv_out, v_w_conv_out, "adam_w_conv_out"),
        "w_o": upd(r_o, w_o, m_w_o, v_w_o, "adam_w_o"),
        "ln_g": upd(small_g[:, 1:2, :], ln_g, m_ln_g, v_ln_g, "adam_ln_g"),
        "ln_b": upd(small_g[:, 2:3, :], ln_b, m_ln_b, v_ln_b, "adam_ln_b"),
    }
    order = ["w_ada", "b_ada", "w_in", "conv_w", "conv_b", "rel_bias", "w_attn_out", "w_conv_out",
             "w_o", "ln_g", "ln_b"]
    outs = [loss, grad_x.reshape(BL, S, D)]
    for k in range(4):
        outs += [res[name][k] for name in order]
    return tuple(outs)
```

```python
import functools
import math

import numpy as np
import jax
import jax.numpy as jnp
from jax import lax
from jax.experimental import pallas as pl
from jax.experimental.pallas import tpu as pltpu

F32 = jnp.float32
BF16 = jnp.bfloat16
MESH = pl.DeviceIdType.MESH

N_DEV = 8
D = 1024
S = 2048
BL = 2
T = BL * S
NCOL = 11264
SHARD = NCOL // N_DEV
CB = 512
NCB = NCOL // CB
HD = 128
GW = 512
QB = 128
DILATIONS = (1, 4, 16)
N_STEPS = 128
N_BUCKETS = 32
N_HEADS = 12
ALPHA = 2.0 ** 0.25
LN_EPS = 1e-5
NEG_INF = -1e30
SCALE = HD ** -0.5
ADA_SHARD = 3 * D // N_DEV

CB_Q, CB_K, CB_V, CB_GA = 0, 3, 6, 9
KB_U, KB_BG, KB_CG, KB_GC, KB_MA, KB_MC = 5, 6, 7, 8, 9, 10

ADAM_LR, ADAM_B1, ADAM_B2, ADAM_EPS, ADAM_WD, ADAM_STEP = 0.001, 0.9, 0.999, 1e-08, 0.01, 10

VMEM_LIMIT = 56 * 1024 * 1024


def _dot(a, b):
    return jnp.dot(a, b, preferred_element_type=F32)


def _dot_nt(a, b):
    return lax.dot_general(a, b, (((1,), (1,)), ((), ())), preferred_element_type=F32)


def _dot_tn(a, b):
    return lax.dot_general(a, b, (((0,), (0,)), ((), ())), preferred_element_type=F32)


def _sigmoid(v):
    return 1.0 / (1.0 + jnp.exp(-v))


def _write_columns(pieces, dst_hbm, row0, sems):
    copies = []
    for k, (src, col0) in enumerate(pieces):
        rows, width = src.shape
        copies.append(pltpu.make_async_copy(
            src, dst_hbm.at[pl.ds(row0, rows), pl.ds(col0, width)], sems.at[k]))
    for cp in copies:
        cp.start()
    for cp in copies:
        cp.wait()


def _my_index():
    return 4 * lax.axis_index("x") + 2 * lax.axis_index("y") + lax.axis_index("c")


class _Gather:
    def __init__(self, ins, outs, stage, send_sems, recv_sems, local_sems):
        self.ins, self.outs, self.stage = ins, outs, stage
        self.send_sems, self.recv_sems, self.local_sems = send_sems, recv_sems, local_sems
        x, y, c = lax.axis_index("x"), lax.axis_index("y"), lax.axis_index("c")
        self.c = c
        self.me, self.sibling = (x, y, c), (x, y, 1 - c)
        self.chips = [(1 - x, y), (x, 1 - y), (1 - x, 1 - y)]

    @staticmethod
    def scratch(arrs):
        n = len(arrs)
        return ([pltpu.SemaphoreType.DMA((7 * n,)), pltpu.SemaphoreType.DMA((7 * n,)),
                 pltpu.SemaphoreType.DMA((n,))] + [pltpu.VMEM(a.shape, a.dtype) for a in arrs])

    def _copy(self, a, k, block, to, src=None):
        dst = self.outs[a].at[4 * block[0] + 2 * block[1] + block[2]]
        return pltpu.make_async_remote_copy(
            src_ref=dst if src is None else src, dst_ref=dst,
            send_sem=self.send_sems.at[a * 7 + k], recv_sem=self.recv_sems.at[a * 7 + k],
            device_id=to, device_id_type=MESH)

    def _first(self):
        first = []
        for a in range(len(self.ins)):
            first.append(self._copy(a, 0, self.me, self.sibling, src=self.ins[a]))
            first += [self._copy(a, 1 + j, self.me, (*chip, self.c), src=self.ins[a])
                      for j, chip in enumerate(self.chips)]
        return first

    def _mine(self):
        me = self.me
        return [pltpu.make_async_copy(self.stage[a], self.outs[a].at[4 * me[0] + 2 * me[1] + me[2]],
                                      self.local_sems.at[a]) for a in range(len(self.ins))]

    def begin(self):
        for cp in self._first():
            cp.start()
        loads = [pltpu.make_async_copy(self.ins[a], self.stage[a], self.local_sems.at[a])
                 for a in range(len(self.ins))]
        for cp in loads:
            cp.start()
        for cp in loads:
            cp.wait()
        for cp in self._mine():
            cp.start()

    def finish(self):
        n, c, me, sibling = len(self.ins), self.c, self.me, self.sibling
        passed = []
        for j, chip in enumerate(self.chips):
            for a in range(n):
                self._copy(a, 1 + j, (*chip, c), me).wait_recv()
                fwd = self._copy(a, 4 + j, (*chip, c), sibling)
                fwd.start()
                passed.append(fwd)
        for a in range(n):
            self._copy(a, 0, sibling, me).wait_recv()
        for j, chip in enumerate(self.chips):
            for a in range(n):
                self._copy(a, 4 + j, (*chip, 1 - c), me).wait_recv()
        for cp in self._first() + passed:
            cp.wait_send()
        for cp in self._mine():
            cp.wait()


def _all_gather(arrs, out_shapes, name):
    n = len(arrs)

    def body(*refs):
        g = _Gather(refs[:n], refs[n:2 * n], refs[2 * n + 3:], *refs[2 * n:2 * n + 3])
        g.begin()
        g.finish()

    any_spec = pl.BlockSpec(memory_space=pl.ANY)
    return pl.pallas_call(
        body, name=name,
        out_shape=tuple(out_shapes),
        in_specs=[any_spec] * n,
        out_specs=tuple([any_spec] * n),
        scratch_shapes=_Gather.scratch(arrs),
    )(*arrs)


def _slice_order():
    x, y, c = lax.axis_index("x"), lax.axis_index("y"), lax.axis_index("c")
    slots = []
    for q in (2 * (1 - x) + y, 2 * x + (1 - y), 2 * (1 - x) + (1 - y), 2 * x + y):
        slots += [2 * q + 1 - c, 2 * q + c]
    return jnp.stack(slots).astype(jnp.int32)


def _gw_in_pair(order, h, dproj, smalls, small_shapes4):
    kk, m = h.shape
    tk = min(kk, 2048)
    nk = kk // tk
    ncols = dproj.shape[1] // N_DEV
    n = len(smalls)

    def body(order_ref, h_ref, d_ref, *rest):
        ins = rest[:n]
        sums_hbm, parts_hbm = rest[n], rest[n + 1]
        sib = rest[n + 2:2 * n + 2]
        (acc, sendbuf, recvbuf, sumbuf, send_sems, recv_sems, local_sem, ssend, srecv,
         isend, irecv) = rest[2 * n + 2:]
        js, k = pl.program_id(0), pl.program_id(1)
        x, y, c = lax.axis_index("x"), lax.axis_index("y"), lax.axis_index("c")
        sibling = (x, y, 1 - c)
        my_chip = 2 * x + y
        near = [(1 - x, y, c), (x, 1 - y, c)]

        def ici_copy(p, out_chip):
            peer = near[p]
            return pltpu.make_async_remote_copy(
                src_ref=sumbuf.at[p], dst_ref=parts_hbm.at[out_chip],
                send_sem=isend.at[p], recv_sem=irecv.at[p], device_id=peer, device_id_type=MESH)

        def small_copies():
            return [pltpu.make_async_remote_copy(
                        src_ref=ins[a].at[2 * q + 1 - c], dst_ref=sib[a].at[q],
                        send_sem=ssend.at[a * 4 + q], recv_sem=srecv.at[a * 4 + q],
                        device_id=sibling, device_id_type=MESH)
                    for a in range(n) for q in range(4)]

        def slice_copy(p):
            return pltpu.make_async_remote_copy(
                src_ref=sendbuf, dst_ref=recvbuf.at[p], send_sem=send_sems.at[p], recv_sem=recv_sems.at[p],
                device_id=sibling, device_id_type=MESH)

        def sum_copy(p):
            return pltpu.make_async_copy(sumbuf.at[2], sums_hbm.at[order_ref[2 * p] // 2], local_sem)

        @pl.when((js == 0) & (k == 0))
        def _():
            for cp in small_copies():
                cp.start()

        @pl.when(k == 0)
        def _():
            acc[...] = jnp.zeros_like(acc)

        acc[...] += _dot_tn(h_ref[...], d_ref[...])

        for p in range(4):
            @pl.when((js == 2 * p) & (k == nk - 1))
            def _():
                if p > 0:
                    slice_copy(p - 1).wait_send()
                sendbuf[...] = acc[...].astype(BF16)
                slice_copy(p).start()

            @pl.when((js == 2 * p + 1) & (k == nk - 1))
            def _():
                slice_copy(p).wait_recv()
                if p == 3:
                    sum_copy(2).wait()
                sumbuf[min(p, 2)] = (acc[...] + recvbuf[p].astype(F32)).astype(BF16)
                if p < 2:
                    ici_copy(p, my_chip).start()
                else:
                    sum_copy(p).start()

        @pl.when((js == N_DEV - 1) & (k == nk - 1))
        def _():
            slice_copy(3).wait_send()
            sum_copy(3).wait()
            for cp in small_copies():
                cp.wait()
            for p in range(2):
                ici_copy(p, 2 * near[p][0] + near[p][1]).wait_recv()
                ici_copy(p, my_chip).wait_send()

    any_spec = pl.BlockSpec(memory_space=pl.ANY)
    res = pl.pallas_call(
        body, name="gw_in_pair",
        grid_spec=pltpu.PrefetchScalarGridSpec(
            num_scalar_prefetch=1,
            grid=(N_DEV, nk),
            in_specs=[pl.BlockSpec((tk, m), lambda js, k, order_ref: (k, 0)),
                      pl.BlockSpec((tk, ncols), lambda js, k, order_ref: (k, order_ref[js]))] + [any_spec] * n,
            out_specs=(any_spec,) * (n + 2),
            scratch_shapes=[pltpu.VMEM((m, ncols), F32), pltpu.VMEM((m, ncols), BF16),
                            pltpu.VMEM((4, m, ncols), BF16), pltpu.VMEM((3, m, ncols), BF16),
                            pltpu.SemaphoreType.DMA((4,)), pltpu.SemaphoreType.DMA((4,)),
                            pltpu.SemaphoreType.DMA,
                            pltpu.SemaphoreType.DMA((4 * n,)), pltpu.SemaphoreType.DMA((4 * n,)),
                            pltpu.SemaphoreType.DMA((2,)), pltpu.SemaphoreType.DMA((2,))]),
        out_shape=(jax.ShapeDtypeStruct((4, m, ncols), BF16),) * 2 + tuple(small_shapes4),
        compiler_params=pltpu.CompilerParams(vmem_limit_bytes=VMEM_LIMIT),
    )(order, h, dproj, *smalls)
    return res[0], res[1], res[2:]


def _chip_copies(ins, outs, send_sems, recv_sems, local_sems, hops):
    n = len(ins)
    x, y, c = lax.axis_index("x"), lax.axis_index("y"), lax.axis_index("c")
    my_chip = 2 * x + y

    def peer_of(k):
        return ((1 - x) if (k >> 1) & 1 else x, (1 - y) if k & 1 else y, c)

    def copy(a, k, out_chip):
        peer = peer_of(k)
        return pltpu.make_async_remote_copy(
            src_ref=ins[a].at[2 * peer[0] + peer[1]], dst_ref=outs[a].at[out_chip],
            send_sem=send_sems.at[a * 3 + k - 1], recv_sem=recv_sems.at[a * 3 + k - 1],
            device_id=peer, device_id_type=MESH)

    sends = [copy(a, k, my_chip) for k in range(1, 4) for a in range(n) if k in hops[a]]
    arrivals = []
    for k in range(1, 4):
        peer = peer_of(k)
        arrivals += [copy(a, k, 2 * peer[0] + peer[1]) for a in range(n) if k in hops[a]]
    mine = [pltpu.make_async_copy(ins[a].at[my_chip], outs[a].at[my_chip], local_sems.at[a])
            for a in range(n)]
    return sends, arrivals, mine


def _pair_add(core, mine, theirs, row_tile, name):
    _, rows, cols = theirs.shape
    tr = rows if row_tile is None else row_tile

    def body(core_ref, a_ref, b_ref, o_ref):
        o_ref[...] = (a_ref[...].astype(F32) + b_ref[...].astype(F32)).astype(o_ref.dtype)

    blk = pl.BlockSpec((None, tr, cols), lambda q, i, core_ref: (q, i, 0))
    return pl.pallas_call(
        body, name=name,
        grid_spec=pltpu.PrefetchScalarGridSpec(
            num_scalar_prefetch=1,
            grid=(4, rows // tr),
            in_specs=[pl.BlockSpec((None, tr, cols), lambda q, i, core_ref: (2 * q + core_ref[0], i, 0)), blk],
            out_specs=blk),
        out_shape=jax.ShapeDtypeStruct(theirs.shape, theirs.dtype),
    )(core, mine, theirs)


def _mod_exchange(c8, w_ada, b_cols):
    cols = w_ada.shape[1]

    def body(c_ref, w_ref, b_ref, call_ref, mod_ref, msend, send1, recv1, send2, recv2):
        x, y, c = lax.axis_index("x"), lax.axis_index("y"), lax.axis_index("c")
        my_slot = 4 * x + 2 * y + c

        def peer_of(k):
            return ((1 - x) if (k >> 2) & 1 else x, (1 - y) if (k >> 1) & 1 else y, (1 - c) if k & 1 else c)

        def slot_of(dev):
            return 4 * dev[0] + 2 * dev[1] + dev[2]

        def exchange(src_of, dst_ref, send_sems, recv_sems):
            sends, arrivals = [], []
            for k in range(1, 8):
                peer = peer_of(k)
                sends.append(pltpu.make_async_remote_copy(
                    src_ref=src_of(slot_of(peer)), dst_ref=dst_ref.at[my_slot],
                    send_sem=send_sems.at[k - 1], recv_sem=recv_sems.at[k - 1],
                    device_id=peer, device_id_type=MESH))
                arrivals.append(pltpu.make_async_remote_copy(
                    src_ref=src_of(my_slot), dst_ref=dst_ref.at[slot_of(peer)],
                    send_sem=send_sems.at[k - 1], recv_sem=recv_sems.at[k - 1],
                    device_id=peer, device_id_type=MESH))
            for cp in sends:
                cp.start()
            for cp in arrivals:
                cp.wait_recv()
            for cp in sends:
                cp.wait_send()

        call_ref[my_slot] = c_ref[...]
        exchange(lambda s: c_ref, call_ref, send1, recv1)
        cv = call_ref[...].reshape(N_DEV * 8, c_ref.shape[1])
        act = cv * _sigmoid(cv)
        mod = jnp.dot(act, w_ref[...], preferred_element_type=F32,
                      precision=lax.Precision.HIGHEST) + b_ref[...]
        msend[...] = mod.reshape(N_DEV, 8, cols)
        mod_ref[my_slot] = msend[my_slot]
        exchange(lambda s: msend.at[s], mod_ref, send2, recv2)

    return pl.pallas_call(
        body, name="mod_exchange",
        out_shape=(jax.ShapeDtypeStruct((N_DEV, 8, c8.shape[1]), F32),
                   jax.ShapeDtypeStruct((N_DEV, 8, cols), F32)),
        scratch_shapes=[pltpu.VMEM((N_DEV, 8, cols), F32)] + [pltpu.SemaphoreType.DMA((7,))] * 4,
    )(c8, w_ada, b_cols)


def _ada_bwd(c_all_t, dmod_cols):
    def body(c_ref, d_ref, o_ref):
        cv = c_ref[...]
        sc = cv * _sigmoid(cv)
        o_ref[...] = jnp.dot(sc, d_ref[...], preferred_element_type=F32,
                             precision=lax.Precision.HIGHEST)

    return pl.pallas_call(
        body, name="ada_bwd",
        out_shape=jax.ShapeDtypeStruct((c_all_t.shape[0], dmod_cols.shape[1]), F32),
    )(c_all_t, dmod_cols)


def _prep_h(x2, mod3):
    ts = 512
    per_seq = S // ts

    def body(x_ref, mod_ref, h_ref):
        shift = mod_ref[0, 0:1, :]
        scale = mod_ref[0, 1:2, :]
        h_ref[...] = (x_ref[...] * (1.0 + scale) + shift).astype(BF16)

    return pl.pallas_call(
        body, name="prep_h",
        grid=(T // ts,),
        in_specs=[pl.BlockSpec((ts, D), lambda i: (i, 0)),
                  pl.BlockSpec((1, 3, D), lambda i: (i // per_seq, 0, 0))],
        out_specs=pl.BlockSpec((ts, D), lambda i: (i, 0)),
        out_shape=jax.ShapeDtypeStruct((T, D), BF16),
    )(x2, mod3)


def _shard_order():
    x, y, c = lax.axis_index("x"), lax.axis_index("y"), lax.axis_index("c")
    devs = [(x, y, c), (x, y, 1 - c)]
    for chip in [(1 - x, y), (x, 1 - y), (1 - x, 1 - y)]:
        devs += [(*chip, c), (*chip, 1 - c)]
    return jnp.stack([4 * d[0] + 2 * d[1] + d[2] for d in devs]).astype(jnp.int32)


def _gather_proj(order, h, w_shard, tm):
    rows, kdim = h.shape
    ncols = w_shard.shape[1]
    n_i = rows // tm

    def body(order_ref, h_ref, mine_hbm, o_ref, all_hbm, wv, send_sems, recv_sems, local_sems):
        j, i = pl.program_id(0), pl.program_id(1)
        x, y, c = lax.axis_index("x"), lax.axis_index("y"), lax.axis_index("c")
        me, sibling = (x, y, c), (x, y, 1 - c)
        chips = [(1 - x, y), (x, 1 - y), (1 - x, 1 - y)]

        def slot(dev):
            return 4 * dev[0] + 2 * dev[1] + dev[2]

        def copy(k, block, to):
            return pltpu.make_async_remote_copy(
                src_ref=wv.at[slot(block)], dst_ref=wv.at[slot(block)],
                send_sem=send_sems.at[k], recv_sem=recv_sems.at[k],
                device_id=to, device_id_type=MESH)

        def keep(step, block):
            return pltpu.make_async_copy(wv.at[slot(block)], all_hbm.at[slot(block)], local_sems.at[step])

        first = [copy(0, me, sibling)] + [copy(1 + q, me, (*chip, c)) for q, chip in enumerate(chips)]
        passed = [copy(4 + q, (*chip, c), sibling) for q, chip in enumerate(chips)]
        due = [(me, None, None), (sibling, copy(0, sibling, me), None)]
        for q, chip in enumerate(chips):
            due.append(((*chip, c), copy(1 + q, (*chip, c), me), passed[q]))
            due.append(((*chip, 1 - c), copy(4 + q, (*chip, 1 - c), me), None))

        @pl.when((j == 0) & (i == 0))
        def _():
            load = pltpu.make_async_copy(mine_hbm, wv.at[slot(me)], local_sems.at[N_DEV])
            load.start()
            load.wait()
            for cp in first:
                cp.start()
            keep(0, me).start()

        for step in range(1, N_DEV):
            block, arrival, forward = due[step]

            @pl.when((j == step) & (i == 0))
            def _():
                arrival.wait_recv()
                if forward is not None:
                    forward.start()
                keep(step, block).start()

        o_ref[...] = _dot(h_ref[...], wv[order_ref[j]]).astype(BF16)

        @pl.when((j == N_DEV - 1) & (i == n_i - 1))
        def _():
            for cp in first + passed:
                cp.wait_send()
            for step in range(N_DEV):
                keep(step, due[step][0]).wait()

    return pl.pallas_call(
        body, name="gather_proj",
        grid_spec=pltpu.PrefetchScalarGridSpec(
            num_scalar_prefetch=1,
            grid=(N_DEV, n_i),
            in_specs=[pl.BlockSpec((tm, kdim), lambda j, i, order_ref: (i, 0)),
                      pl.BlockSpec(memory_space=pl.ANY)],
            out_specs=(pl.BlockSpec((tm, ncols), lambda j, i, order_ref: (i, order_ref[j])),
                       pl.BlockSpec(memory_space=pl.ANY)),
            scratch_shapes=[pltpu.VMEM((N_DEV, kdim, ncols), BF16),
                            pltpu.SemaphoreType.DMA((7,)), pltpu.SemaphoreType.DMA((7,)),
                            pltpu.SemaphoreType.DMA((N_DEV + 1,))]),
        out_shape=(jax.ShapeDtypeStruct((rows, N_DEV * ncols), BF16),
                   jax.ShapeDtypeStruct((N_DEV, kdim, ncols), BF16)),
        compiler_params=pltpu.CompilerParams(vmem_limit_bytes=VMEM_LIMIT),
    )(order, h, w_shard)


def _bucket_maps():
    a = np.arange(QB)[:, None]
    b = np.arange(2 * QB)[None, :]
    steps = a + QB - b
    maps = []
    for dil in DILATIONS:
        dist = np.maximum(steps, 0) * dil
        nf = np.maximum(dist, 1).astype(np.float32)
        large = 16 + (np.log(nf / np.float32(16)) / np.float32(math.log(128.0))
                      * np.float32(16)).astype(np.int32)
        large = np.minimum(large, N_BUCKETS - 1)
        maps.append(np.where(dist < 16, dist, large).astype(np.int32))
    band = (steps >= 0) & (steps <= N_STEPS)
    first = band & (b >= QB)
    masks = np.stack([first, band]).astype(np.int32)
    return np.stack(maps), masks


def _bias_expand(rel_bias, buckets, masks):
    def body(tab_ref, bk_ref, mk_ref, o_ref):
        for g in range(3):
            bk = bk_ref[g]
            for h in range(4):
                col = 4 * g + h
                val = jnp.zeros((QB, 2 * QB), F32)
                for k in range(N_BUCKETS):
                    val = jnp.where(bk == k, tab_ref[k, col], val)
                o_ref[g, 0, h] = jnp.where(mk_ref[0] != 0, val, NEG_INF)
                o_ref[g, 1, h] = jnp.where(mk_ref[1] != 0, val, NEG_INF)

    return pl.pallas_call(
        body, name="bias_expand",
        in_specs=[pl.BlockSpec(memory_space=pltpu.SMEM),
                  pl.BlockSpec(memory_space=pltpu.VMEM),
                  pl.BlockSpec(memory_space=pltpu.VMEM)],
        out_shape=jax.ShapeDtypeStruct((3, 2, 4, QB, 2 * QB), F32),
    )(rel_bias, buckets, masks)


def _bias_grad(ds1, ds2, ds3, buckets):
    def body(d1_ref, d2_ref, d3_ref, bk_ref, o_ref):
        for g, d_ref in enumerate((d1_ref, d2_ref, d3_ref)):
            bk = bk_ref[g]
            for h in range(4):
                dv = d_ref[h]
                for k in range(N_BUCKETS):
                    o_ref[k, 4 * g + h] = jnp.sum(jnp.where(bk == k, dv, 0.0))

    return pl.pallas_call(
        body, name="bias_grad",
        in_specs=[pl.BlockSpec(memory_space=pltpu.VMEM)] * 4,
        out_specs=pl.BlockSpec(memory_space=pltpu.SMEM),
        out_shape=jax.ShapeDtypeStruct((N_BUCKETS, N_HEADS), F32),
    )(ds1, ds2, ds3, buckets)


def _scratch_sets(rows):
    return 4 if rows <= 512 else 1


def _unit_chunks(dil, size=16):
    units = [(h, r) for h in range(4) for r in range(dil)]
    return [units[i:i + size] for i in range(0, len(units), size)]


def _residue_rows(src_ref, copies, h, residue):
    sl = slice(h * HD, (h + 1) * HD)
    if copies is None:
        return lambda r: src_ref[:, sl]
    buf = copies[h % len(copies)]
    buf[...] = src_ref[:, sl].astype(F32)
    return lambda r: buf[residue(r), :].astype(BF16)


def _attn_fwd(proj, bias, g, ride=()):
    dil = DILATIONS[g]
    rows = QB * dil
    nsb = S // rows
    has_prev = nsb > 1

    def residue(r):
        return pl.ds(r, QB, stride=dil) if dil > 1 else pl.ds(0, QB)

    strided = dil > 1
    n_sets = _scratch_sets(rows)
    n_attn = 6 if has_prev else 4
    ride_arrs, ride_shapes = ride if ride else ((), ())
    n_ride = len(ride_arrs)
    n_in = n_attn + n_ride
    n_copied = (4 + (2 if has_prev else 0)) * (n_sets if strided else 0)

    def body(*refs):
        q_ref, kc_ref, vc_ref = refs[:3]
        kp_ref, vp_ref = refs[3:5] if has_prev else (None, None)
        b_ref = refs[n_attn - 1]
        o_ref, l_ref = refs[n_in:n_in + 2]
        scr = list(refs[n_in + 2 + n_ride:])
        ls = [scr.pop(0) for _ in range(4)]
        copies = {name: [scr.pop(0) for _ in range(n_sets)] if strided else None
                  for name in ("q", "kc", "vc", "o") + (("kp", "vp") if has_prev else ())}
        if n_ride:
            gather = _Gather(refs[n_attn:n_in], refs[n_in + 2:n_in + 2 + n_ride], scr[3:], *scr[:3])

            @pl.when((pl.program_id(0) == 0) & (pl.program_id(1) == 0))
            def _():
                gather.begin()
        lane = lax.broadcasted_iota(jnp.int32, (QB, 128), 1)
        refs_of = {"q": q_ref, "kc": kc_ref, "vc": vc_ref, "kp": kp_ref, "vp": vp_ref}
        for chunk in _unit_chunks(dil):
            rows_of = {h: {name: _residue_rows(refs_of[name], copies[name], h, residue)
                           for name in refs_of if refs_of[name] is not None}
                       for h in sorted({h for h, _ in chunk})}

            def batch(name):
                return jnp.stack([rows_of[h][name](r) for h, r in chunk])

            q, k, v = batch("q"), batch("kc"), batch("vc")
            if has_prev:
                k = jnp.concatenate([batch("kp"), k], axis=1)
                v = jnp.concatenate([batch("vp"), v], axis=1)
                bias_b = jnp.stack([b_ref[h] for h, _ in chunk])
            else:
                bias_b = jnp.stack([b_ref[h, :, QB:] for h, _ in chunk])
            s = jnp.einsum("uqd,ukd->uqk", q, k, preferred_element_type=F32) * SCALE + bias_b
            m = jnp.max(s, axis=-1, keepdims=True)
            p = jnp.exp(s - m)
            l = jnp.sum(p, axis=-1, keepdims=True)
            o = jnp.einsum("uqk,ukd->uqd", p.astype(BF16), v, preferred_element_type=F32) / l
            lse = m + jnp.log(l)
            for i, (h, r) in enumerate(chunk):
                if strided:
                    copies["o"][h % n_sets][residue(r), :] = o[i]
                else:
                    o_ref[:, h * HD:(h + 1) * HD] = o[i]
                ls[h][r * QB:(r + 1) * QB, :] = jnp.where(lane == h, lse[i], 0.0)
            if strided:
                for h in sorted({h for h, _ in chunk}):
                    o_ref[:, h * HD:(h + 1) * HD] = copies["o"][h % n_sets][...]
        for r in range(dil):
            blk = slice(r * QB, (r + 1) * QB)
            l_ref[residue(r), :] = (ls[0][blk, :] + ls[1][blk, :]) + (ls[2][blk, :] + ls[3][blk, :])
        if n_ride:
            @pl.when((pl.program_id(0) == BL - 1) & (pl.program_id(1) == nsb - 1))
            def _():
                gather.finish()

    def row(b, n):
        return b * nsb + n

    def prev(b, n):
        return b * nsb + jnp.maximum(n - 1, 0)

    in_specs = [
        pl.BlockSpec((rows, GW), lambda b, n: (row(b, n), CB_Q + g)),
        pl.BlockSpec((rows, GW), lambda b, n: (row(b, n), CB_K + g)),
        pl.BlockSpec((rows, GW), lambda b, n: (row(b, n), CB_V + g)),
    ]
    args = [proj, proj, proj]
    scratch = [pltpu.VMEM((rows, 128), F32)] * (4 + n_copied)
    if has_prev:
        in_specs += [pl.BlockSpec((rows, GW), lambda b, n: (prev(b, n), CB_K + g)),
                     pl.BlockSpec((rows, GW), lambda b, n: (prev(b, n), CB_V + g))]
        args += [proj, proj]
    in_specs.append(pl.BlockSpec((None, None, 4, QB, 2 * QB),
                                 lambda b, n: (g, jnp.minimum(n, 1), 0, 0, 0)))
    args.append(bias)
    any_spec = pl.BlockSpec(memory_space=pl.ANY)
    return pl.pallas_call(
        body, name=f"attn_fwd{g}",
        grid=(BL, nsb),
        in_specs=in_specs + [any_spec] * n_ride,
        out_specs=(pl.BlockSpec((rows, GW), lambda b, n: (row(b, n), 0)),
                   pl.BlockSpec((rows, 128), lambda b, n: (row(b, n), 0))) + (any_spec,) * n_ride,
        out_shape=(jax.ShapeDtypeStruct((T, GW), F32), jax.ShapeDtypeStruct((T, 128), F32)) + tuple(ride_shapes),
        scratch_shapes=scratch + (_Gather.scratch(ride_arrs) if n_ride else []),
        compiler_params=pltpu.CompilerParams(vmem_limit_bytes=VMEM_LIMIT),
    )(*args, *ride_arrs)


def _attn_bwd(proj, d_out, stats, bias, dproj, g):
    dil = DILATIONS[g]
    rows = QB * dil
    nsb = S // rows
    has_prev = nsb > 1
    n_steps = nsb + 1 if has_prev else 1
    n_in = 7 + (2 if has_prev else 0)

    def residue(r):
        return pl.ds(r, QB, stride=dil) if dil > 1 else pl.ds(0, QB)

    strided = dil > 1
    n_sets = _scratch_sets(rows)

    def body(*refs):
        q_ref, kc_ref, vc_ref, do_ref, st_ref, b_ref = refs[:6]
        kp_ref, vp_ref = refs[6:8] if has_prev else (None, None)
        out_ref, db_ref = refs[n_in], refs[n_in + 1]
        scr = list(refs[n_in + 2:])
        sq, sk, sv, sems = [scr.pop(0) for _ in range(4)]
        carry = scr.pop(0) if has_prev else None
        sts = scr.pop(0) if strided else st_ref
        copies = {name: [scr.pop(0) for _ in range(n_sets)] if strided else None
                  for name in ("q", "kc", "vc", "do", "dq", "dk", "dv") + (("kp", "vp") if has_prev else ())}
        b, n = pl.program_id(0), pl.program_id(1)

        @pl.when((b == 0) & (n == 0))
        def _():
            db_ref[...] = jnp.zeros_like(db_ref)

        def finish(h, r, dq, dk, dv):
            if strided:
                for name, val in (("dq", dq), ("dk", dk), ("dv", dv)):
                    copies[name][h % n_sets][residue(r), :] = val
            else:
                sl = slice(h * HD, (h + 1) * HD)
                sq[:, sl], sk[:, sl], sv[:, sl] = dq.astype(BF16), dk.astype(BF16), dv.astype(BF16)

        def finish_head(h):
            if strided:
                sl = slice(h * HD, (h + 1) * HD)
                sq[:, sl] = copies["dq"][h % n_sets][...].astype(BF16)
                sk[:, sl] = copies["dk"][h % n_sets][...].astype(BF16)
                sv[:, sl] = copies["dv"][h % n_sets][...].astype(BF16)

        def write_block(blk_idx):
            row0 = pl.multiple_of(blk_idx * rows, rows)
            _write_columns([(sq, CB * (CB_Q + g)), (sk, CB * (CB_K + g)), (sv, CB * (CB_V + g))],
                           out_ref, row0, sems)

        def carried(h, r):
            blk = slice(r * QB, (r + 1) * QB)
            return ((blk, slice(h * HD, (h + 1) * HD)), (blk, slice(GW + h * HD, GW + (h + 1) * HD)),
                    (blk, slice(2 * GW + h * HD, 2 * GW + (h + 1) * HD)))

        if has_prev:
            @pl.when(n == 0)
            def _():
                carry[...] = jnp.zeros_like(carry)

            @pl.when(n == nsb)
            def _():
                for h in range(4):
                    for r in range(dil):
                        cq, ck, cv = carried(h, r)
                        finish(h, r, carry[cq], carry[ck], carry[cv])
                    finish_head(h)
                write_block(b * nsb + nsb - 1)

        @pl.when(n < nsb)
        def _():
            if strided:
                for r in range(dil):
                    sts[r * QB:(r + 1) * QB, :] = st_ref[residue(r), :]
            refs_of = {"q": q_ref, "kc": kc_ref, "vc": vc_ref, "do": do_ref, "kp": kp_ref, "vp": vp_ref}
            for chunk in _unit_chunks(dil):
                heads = sorted({h for h, _ in chunk})
                rows_of = {h: {name: _residue_rows(refs_of[name], copies[name], h, residue)
                               for name in refs_of if refs_of[name] is not None}
                           for h in heads}

                def batch(name):
                    return jnp.stack([rows_of[h][name](r) for h, r in chunk])

                q, k, v, do = batch("q"), batch("kc"), batch("vc"), batch("do")
                if has_prev:
                    k = jnp.concatenate([batch("kp"), k], axis=1)
                    v = jnp.concatenate([batch("vp"), v], axis=1)
                    bias_b = jnp.stack([b_ref[h] for h, _ in chunk])
                else:
                    bias_b = jnp.stack([b_ref[h, :, QB:] for h, _ in chunk])
                lse = jnp.stack([sts[r * QB:(r + 1) * QB, h:h + 1] for h, r in chunk])
                delta = jnp.stack([sts[r * QB:(r + 1) * QB, 4 + h:5 + h] for h, r in chunk])
                s = jnp.einsum("uqd,ukd->uqk", q, k, preferred_element_type=F32) * SCALE + bias_b
                p = jnp.exp(s - lse)
                ds = p * (jnp.einsum("uqd,ukd->uqk", do, v, preferred_element_type=F32) - delta)
                for h in heads:
                    mine = [ds[i] for i, (hh, _) in enumerate(chunk) if hh == h]
                    tot = mine[0]
                    for extra in mine[1:]:
                        tot = tot + extra
                    if has_prev:
                        db_ref[h] += tot
                    else:
                        db_ref[h, :, QB:] += tot
                dsb, pb = ds.astype(BF16), p.astype(BF16)
                dq = jnp.einsum("uqk,ukd->uqd", dsb, k, preferred_element_type=F32) * SCALE
                dk = jnp.einsum("uqk,uqd->ukd", dsb, q, preferred_element_type=F32) * SCALE
                dv = jnp.einsum("uqk,uqd->ukd", pb, do, preferred_element_type=F32)
                for i, (h, r) in enumerate(chunk):
                    if has_prev:
                        cq, ck, cv = carried(h, r)
                        finish(h, r, carry[cq], carry[ck] + dk[i, :QB], carry[cv] + dv[i, :QB])
                        carry[cq] = dq[i]
                        carry[ck] = dk[i, QB:]
                        carry[cv] = dv[i, QB:]
                    else:
                        finish(h, r, dq[i], dk[i], dv[i])
                for h in heads:
                    finish_head(h)
            if has_prev:
                @pl.when(n > 0)
                def _():
                    write_block(b * nsb + n - 1)
            else:
                write_block(b)

    def row(b, n):
        return b * nsb + jnp.minimum(n, nsb - 1)

    def prev(b, n):
        return b * nsb + jnp.maximum(jnp.minimum(n, nsb - 1) - 1, 0)

    in_specs = [
        pl.BlockSpec((rows, GW), lambda b, n: (row(b, n), CB_Q + g)),
        pl.BlockSpec((rows, GW), lambda b, n: (row(b, n), CB_K + g)),
        pl.BlockSpec((rows, GW), lambda b, n: (row(b, n), CB_V + g)),
        pl.BlockSpec((rows, GW), lambda b, n: (row(b, n), 0)),
        pl.BlockSpec((rows, 128), lambda b, n: (row(b, n), 0)),
        pl.BlockSpec((None, None, 4, QB, 2 * QB),
                     lambda b, n: (g, jnp.minimum(jnp.minimum(n, nsb - 1), 1), 0, 0, 0)),
    ]
    args = [proj, proj, proj, d_out, stats, bias]
    scratch = [pltpu.VMEM((rows, GW), BF16)] * 3 + [pltpu.SemaphoreType.DMA((3,))]
    if has_prev:
        in_specs += [pl.BlockSpec((rows, GW), lambda b, n: (prev(b, n), CB_K + g)),
                     pl.BlockSpec((rows, GW), lambda b, n: (prev(b, n), CB_V + g))]
        args += [proj, proj]
        scratch.append(pltpu.VMEM((rows, 3 * GW), F32))
    if strided:
        n_copied = (7 + (2 if has_prev else 0)) * n_sets
        scratch += [pltpu.VMEM((rows, 128), F32)] * (1 + n_copied)
    in_specs.append(pl.BlockSpec(memory_space=pl.ANY))
    args.append(dproj)
    return pl.pallas_call(
        body, name=f"attn_bwd{g}",
        grid=(BL, n_steps),
        in_specs=in_specs,
        out_specs=(pl.BlockSpec(memory_space=pl.ANY),
                   pl.BlockSpec((4, QB, 2 * QB), lambda b, n: (0, 0, 0))),
        out_shape=(jax.ShapeDtypeStruct((T, NCOL), BF16),
                   jax.ShapeDtypeStruct((4, QB, 2 * QB), F32)),
        scratch_shapes=scratch,
        input_output_aliases={len(args) - 1: 0},
        compiler_params=pltpu.CompilerParams(vmem_limit_bytes=VMEM_LIMIT),
    )(*args)


def _tail(x2, tgt2, mod3, o_g, lse_g, proj, w_ao, w_co, w_o, conv_w, conv_b, ln_g, ln_b):
    tm = 256
    per_seq = S // tm
    halo = 16

    def body(x_ref, t_ref, mod_ref, o1_ref, o2_ref, o3_ref, l1_ref, l2_ref, l3_ref,
             ga_ref, u_ref, bg_ref, cg_ref, gc_ref, ma_ref, mc_ref, up_ref, cp_ref,
             wao_ref, wco_ref, wo_ref, cw_ref, cb_ref, lg_ref, lb_ref,
             dproj_ref, dyc_ref, do_ref, st_ref, dxd_ref,
             mg_ref, dy_ref, ain_ref, dao_ref, sin_ref, dso_ref, vec_ref,
             dga_s, dbg_s, dgm_s, sems):
        i = pl.program_id(0)
        bidx = i // per_seq
        first = (i % per_seq) == 0

        @pl.when(i == 0)
        def _():
            vec_ref[...] = jnp.zeros_like(vec_ref)

        l1, l2, l3 = l1_ref[...], l2_ref[...], l3_ref[...]
        mx = jnp.maximum(jnp.maximum(l1, l2), l3)
        e1, e2, e3 = jnp.exp(l1 - mx), jnp.exp(l2 - mx), jnp.exp(l3 - mx)
        esum = e1 + e2 + e3
        lse_tot = mx + jnp.log(esum)
        w1, w2, w3 = e1 / esum, e2 / esum, e3 / esum

        def per_head(wv):
            return jnp.concatenate([jnp.broadcast_to(wv[:, h:h + 1], (tm, HD)) for h in range(4)], axis=1)

        o = per_head(w1) * o1_ref[...] + per_head(w2) * o2_ref[...] + per_head(w3) * o3_ref[...]

        ga = ga_ref[...].astype(F32)
        sig_ga = _sigmoid(ga)
        silu_ga = ga * sig_ga
        a_in = (o * silu_ga).astype(BF16)
        a_out = _dot(a_in, wao_ref[...])

        u = u_ref[...].astype(F32)
        cg = cg_ref[...].astype(F32)
        z = cg * u
        zp = cp_ref[...].astype(F32) * up_ref[...].astype(F32)
        zp = jnp.where(first, 0.0, zp)
        zcat = jnp.concatenate([zp, z], axis=0)
        z1 = pltpu.roll(zcat, 1, 0)[halo:]
        z2 = pltpu.roll(zcat, 2, 0)[halo:]
        y_conv = cw_ref[0:1, :] * z2 + cw_ref[1:2, :] * z1 + cw_ref[2:3, :] * z + cb_ref[...]
        gc = gc_ref[...].astype(F32)
        sig_gc = _sigmoid(gc)
        silu_gc = gc * sig_gc
        bg = bg_ref[...].astype(F32)
        s_in = (bg * y_conv * silu_gc).astype(BF16)
        s_out = _dot(s_in, wco_ref[...])

        sa = _sigmoid(ma_ref[...].astype(F32))
        sc = _sigmoid(mc_ref[...].astype(F32))
        merged = (sa * a_out + sc * s_out).astype(BF16)
        y = _dot(merged, wo_ref[...])
        gate1 = 1.0 + mod_ref[0, 2:3, :]
        xv = x_ref[...]
        resid = ALPHA * xv + gate1 * y
        mu = jnp.mean(resid, axis=1, keepdims=True)
        xc = resid - mu
        var = jnp.mean(xc * xc, axis=1, keepdims=True)
        rstd = lax.rsqrt(var + LN_EPS)
        xhat = xc * rstd
        lg = lg_ref[...]
        err = xhat * lg + lb_ref[...] - t_ref[...]
        vec_ref[3:4, :] += (0.5 / D) * jnp.sum(err * err, axis=0, keepdims=True)

        dout = err * (1.0 / D)
        vec_ref[1:2, :] += jnp.sum(dout * xhat, axis=0, keepdims=True)
        vec_ref[2:3, :] += jnp.sum(dout, axis=0, keepdims=True)
        dxh = dout * lg
        dres = rstd * (dxh - jnp.mean(dxh, axis=1, keepdims=True)
                       - xhat * jnp.mean(dxh * xhat, axis=1, keepdims=True))
        dxd_ref[...] = ALPHA * dres
        dgate = jnp.sum(dres * y, axis=0, keepdims=True)
        vec_ref[4:5, :] += jnp.where(bidx == 0, dgate, 0.0)
        vec_ref[5:6, :] += jnp.where(bidx == 1, dgate, 0.0)
        dy = (dres * gate1).astype(BF16)

        dmerged = _dot_nt(dy, wo_ref[...])
        da_out = (dmerged * sa).astype(BF16)
        ds_out = (dmerged * sc).astype(BF16)
        dgm_s[:, 2 * D:3 * D] =(dmerged * s_out * sc * (1.0 - sc)).astype(BF16)
        dgm_s[:, D:2 * D] =(dmerged * a_out * sa * (1.0 - sa)).astype(BF16)
        da_in = _dot_nt(da_out, wao_ref[...])
        ds_in = _dot_nt(ds_out, wco_ref[...])

        d_o = da_in * silu_ga
        do_ref[...] = d_o.astype(BF16)
        dga_s[...] =(da_in * o * (sig_ga * (1.0 + ga * (1.0 - sig_ga)))).astype(BF16)
        lane = lax.broadcasted_iota(jnp.int32, (tm, 128), 1)
        stats = lse_tot
        od = o * d_o
        for h in range(4):
            delta = jnp.sum(od[:, h * HD:(h + 1) * HD], axis=1, keepdims=True)
            stats = jnp.where(lane == 4 + h, delta, stats)
        st_ref[...] = stats

        dbg_s[...] =(ds_in * y_conv * silu_gc).astype(BF16)
        dyc = ds_in * bg * silu_gc
        dyc_ref[...] = dyc
        vec_ref[0:1, :] += jnp.sum(dyc, axis=0, keepdims=True)
        dgm_s[:, 0:D] =(ds_in * bg * y_conv * (sig_gc * (1.0 + gc * (1.0 - sig_gc)))).astype(BF16)

        mg_ref[...] = merged
        dy_ref[...] = dy
        ain_ref[...] = a_in
        dao_ref[...] = da_out
        sin_ref[...] = s_in
        dso_ref[...] = ds_out
        _write_columns([(dga_s, CB * CB_GA), (dbg_s, D * KB_BG), (dgm_s, D * KB_GC)],
                       dproj_ref, pl.multiple_of(i * tm, tm), sems)

    def tile(width, cblk=0):
        return pl.BlockSpec((tm, width), lambda i: (i, cblk))

    def whole(shape):
        return pl.BlockSpec(shape, lambda i: tuple(0 for _ in shape))

    prev_rows = lambda i: (jnp.maximum(i * (tm // halo) - 1, 0),)
    in_specs = [
        tile(D), tile(D), pl.BlockSpec((1, 3, D), lambda i: (i // per_seq, 0, 0)),
        tile(GW), tile(GW), tile(GW), tile(128), tile(128), tile(128),
        tile(GW, CB_GA), tile(D, KB_U), tile(D, KB_BG), tile(D, KB_CG), tile(D, KB_GC),
        tile(D, KB_MA), tile(D, KB_MC),
        pl.BlockSpec((halo, D), lambda i: (*prev_rows(i), KB_U)),
        pl.BlockSpec((halo, D), lambda i: (*prev_rows(i), KB_CG)),
        whole((GW, D)), whole((D, D)), whole((D, D)),
        whole((3, D)), whole((1, D)), whole((1, D)), whole((1, D)),
    ]
    out_specs = (
        pl.BlockSpec(memory_space=pl.ANY), tile(D), tile(GW), tile(128), tile(D),
        tile(D), tile(D), tile(GW), tile(D), tile(D), tile(D),
        pl.BlockSpec((8, D), lambda i: (0, 0)),
    )
    out_shape = (
        jax.ShapeDtypeStruct((T, NCOL), BF16),
        jax.ShapeDtypeStruct((T, D), F32),
        jax.ShapeDtypeStruct((T, GW), BF16),
        jax.ShapeDtypeStruct((T, 128), F32),
        jax.ShapeDtypeStruct((T, D), F32),
        jax.ShapeDtypeStruct((T, D), BF16),
        jax.ShapeDtypeStruct((T, D), BF16),
        jax.ShapeDtypeStruct((T, GW), BF16),
        jax.ShapeDtypeStruct((T, D), BF16),
        jax.ShapeDtypeStruct((T, D), BF16),
        jax.ShapeDtypeStruct((T, D), BF16),
        jax.ShapeDtypeStruct((8, D), F32),
    )
    return pl.pallas_call(
        body, name="tail",
        grid=(T // tm,),
        in_specs=in_specs, out_specs=out_specs, out_shape=out_shape,
        scratch_shapes=[pltpu.VMEM((tm, GW), BF16), pltpu.VMEM((tm, D), BF16), pltpu.VMEM((tm, 3 * D), BF16),
                        pltpu.SemaphoreType.DMA((3,))],
        compiler_params=pltpu.CompilerParams(vmem_limit_bytes=VMEM_LIMIT),
    )(x2, tgt2, mod3, *o_g, *lse_g, proj, proj, proj, proj, proj, proj, proj, proj, proj,
      w_ao, w_co, w_o, conv_w, conv_b, ln_g, ln_b)


def _conv_bwd(dyc, proj, conv_w, dproj):
    tm = 512
    per_seq = S // tm
    halo = 16

    def body(d_ref, dn_ref, u_ref, c_ref, up_ref, cp_ref, cw_ref, _, dproj_ref, g_ref, du_s, dc_s, sems):
        i = pl.program_id(0)
        first = (i % per_seq) == 0
        last = (i % per_seq) == per_seq - 1

        @pl.when(i == 0)
        def _():
            g_ref[...] = jnp.zeros_like(g_ref)

        d = d_ref[...]
        dn = jnp.where(last, 0.0, dn_ref[...])
        dcat = jnp.concatenate([d, dn], axis=0)
        d1 = pltpu.roll(dcat, tm + 8 - 1, 0)[:tm]
        d2 = pltpu.roll(dcat, tm + 8 - 2, 0)[:tm]
        dz = cw_ref[2:3, :] * d + cw_ref[1:2, :] * d1 + cw_ref[0:1, :] * d2
        u = u_ref[...].astype(F32)
        cg = c_ref[...].astype(F32)
        du_s[...] = (dz * cg).astype(BF16)
        dc_s[...] = (dz * u).astype(BF16)
        _write_columns([(du_s, D * KB_U), (dc_s, D * KB_CG)], dproj_ref, pl.multiple_of(i * tm, tm), sems)

        z = cg * u
        zp = jnp.where(first, 0.0, cp_ref[...].astype(F32) * up_ref[...].astype(F32))
        zcat = jnp.concatenate([zp, z], axis=0)
        z1 = pltpu.roll(zcat, 1, 0)[halo:]
        z2 = pltpu.roll(zcat, 2, 0)[halo:]
        g_ref[0:1, :] += jnp.sum(d * z2, axis=0, keepdims=True)
        g_ref[1:2, :] += jnp.sum(d * z1, axis=0, keepdims=True)
        g_ref[2:3, :] += jnp.sum(d * z, axis=0, keepdims=True)

    n_tiles = T // tm
    prev_rows = lambda i: jnp.maximum(i * (tm // halo) - 1, 0)
    next_rows = lambda i: jnp.minimum((i + 1) * (tm // 8), T // 8 - 1)
    return pl.pallas_call(
        body, name="conv_bwd",
        grid=(n_tiles,),
        in_specs=[pl.BlockSpec((tm, D), lambda i: (i, 0)),
                  pl.BlockSpec((8, D), lambda i: (next_rows(i), 0)),
                  pl.BlockSpec((tm, D), lambda i: (i, KB_U)),
                  pl.BlockSpec((tm, D), lambda i: (i, KB_CG)),
                  pl.BlockSpec((halo, D), lambda i: (prev_rows(i), KB_U)),
                  pl.BlockSpec((halo, D), lambda i: (prev_rows(i), KB_CG)),
                  pl.BlockSpec((3, D), lambda i: (0, 0)),
                  pl.BlockSpec(memory_space=pl.ANY)],
        out_specs=(pl.BlockSpec(memory_space=pl.ANY),
                   pl.BlockSpec((8, D), lambda i: (0, 0))),
        out_shape=(jax.ShapeDtypeStruct((T, NCOL), BF16),
                   jax.ShapeDtypeStruct((8, D), F32)),
        scratch_shapes=[pltpu.VMEM((tm, D), BF16), pltpu.VMEM((tm, D), BF16), pltpu.SemaphoreType.DMA((2,))],
        input_output_aliases={7: 0},
        compiler_params=pltpu.CompilerParams(vmem_limit_bytes=VMEM_LIMIT),
    )(dyc, dyc, proj, proj, proj, proj, conv_w, dproj)


def _dh_dx(dproj, w_in_all, x2, dxd, mod3, chip_sums, hops=(), parts0=None):
    tm = 1024
    per_seq = S // tm
    n = len(chip_sums)
    n_in = 5 + n + (0 if parts0 is None else 1)

    def body(*refs):
        d_ref, w_ref, x_ref, dxd_ref, mod_ref = refs[:5]
        ins = refs[5:5 + n]
        gx_ref, vec_ref = refs[n_in:n_in + 2]
        outs = refs[n_in + 2:n_in + 2 + n]
        acc, send_sems, recv_sems, local_sems = refs[n_in + 2 + n:]
        i, jj = pl.program_id(0), pl.program_id(1)

        @pl.when((i == 0) & (jj == 0))
        def _():
            vec_ref[...] = jnp.zeros_like(vec_ref)
            if n:
                sends, _, mine = _chip_copies(ins, outs, send_sems, recv_sems, local_sems, hops)
                for cp in sends + mine:
                    cp.start()

        if n:
            @pl.when((i == T // tm - 1) & (jj == N_DEV - 1))
            def _():
                sends, arrivals, mine = _chip_copies(ins, outs, send_sems, recv_sems, local_sems, hops)
                for cp in arrivals:
                    cp.wait_recv()
                for cp in sends:
                    cp.wait_send()
                for cp in mine:
                    cp.wait()

        @pl.when(jj == 0)
        def _():
            acc[...] = jnp.zeros_like(acc)

        acc[...] += _dot_nt(d_ref[...], w_ref[...])

        @pl.when(jj == N_DEV - 1)
        def _():
            dh = acc[...]
            bidx = i // per_seq
            gx_ref[...] = dxd_ref[...] + dh * (1.0 + mod_ref[0, 1:2, :])
            dshift = jnp.sum(dh, axis=0, keepdims=True)
            dscale = jnp.sum(dh * x_ref[...], axis=0, keepdims=True)
            vec_ref[0:1, :] += jnp.where(bidx == 0, dshift, 0.0)
            vec_ref[1:2, :] += jnp.where(bidx == 1, dshift, 0.0)
            vec_ref[2:3, :] += jnp.where(bidx == 0, dscale, 0.0)
            vec_ref[3:4, :] += jnp.where(bidx == 1, dscale, 0.0)

    any_spec = pl.BlockSpec(memory_space=pl.ANY)
    res = pl.pallas_call(
        body, name="dh_dx",
        grid=(T // tm, N_DEV),
        in_specs=[
            pl.BlockSpec((tm, SHARD), lambda i, jj: (i, jj)),
            pl.BlockSpec((None, D, SHARD), lambda i, jj: (jj, 0, 0)),
            pl.BlockSpec((tm, D), lambda i, jj: (i, 0)),
            pl.BlockSpec((tm, D), lambda i, jj: (i, 0)),
            pl.BlockSpec((1, 3, D), lambda i, jj: (i // per_seq, 0, 0))] + [any_spec] * (n_in - 5),
        out_specs=(pl.BlockSpec((tm, D), lambda i, jj: (i, 0)),
                   pl.BlockSpec((8, D), lambda i, jj: (0, 0))) + (any_spec,) * n,
        out_shape=(jax.ShapeDtypeStruct((T, D), F32), jax.ShapeDtypeStruct((8, D), F32))
                  + tuple(jax.ShapeDtypeStruct(a.shape, a.dtype) for a in chip_sums),
        scratch_shapes=[pltpu.VMEM((tm, D), F32), pltpu.SemaphoreType.DMA((max(3 * n, 1),)),
                        pltpu.SemaphoreType.DMA((max(3 * n, 1),)), pltpu.SemaphoreType.DMA((max(n, 1),))],
        input_output_aliases={} if parts0 is None else {5 + n: 2},
        compiler_params=pltpu.CompilerParams(vmem_limit_bytes=VMEM_LIMIT),
    )(dproj, w_in_all, x2, dxd, mod3, *chip_sums, *([] if parts0 is None else [parts0]))
    return res[0], res[1], res[2:]


def _mm_tn(a, b, tn, blocks_leading, name):
    kk, m = a.shape
    n = b.shape[1]
    tk = 2048

    def body(a_ref, b_ref, o_ref, acc):
        @pl.when(pl.program_id(1) == 0)
        def _():
            acc[...] = jnp.zeros_like(acc)

        acc[...] += _dot_tn(a_ref[...], b_ref[...])

        @pl.when(pl.program_id(1) == kk // tk - 1)
        def _():
            o_ref[...] = acc[...].astype(BF16)

    if blocks_leading:
        out_spec = pl.BlockSpec((None, m, tn), lambda j, k: (j, 0, 0))
        out_shape = jax.ShapeDtypeStruct((n // tn, m, tn), BF16)
    else:
        out_spec = pl.BlockSpec((m, tn), lambda j, k: (0, j))
        out_shape = jax.ShapeDtypeStruct((m, n), BF16)
    return pl.pallas_call(
        body, name=name,
        grid=(n // tn, kk // tk),
        in_specs=[pl.BlockSpec((tk, m), lambda j, k: (k, 0)),
                  pl.BlockSpec((tk, tn), lambda j, k: (k, j))],
        out_specs=out_spec, out_shape=out_shape,
        scratch_shapes=[pltpu.VMEM((m, tn), F32)],
        compiler_params=pltpu.CompilerParams(vmem_limit_bytes=VMEM_LIMIT),
    )(a, b)


def _adamw(parts, w, m, v, name, row_tile=None):
    n_parts, rows, cols = parts.shape
    tr = rows if row_tile is None else row_tile
    c1 = 1.0 - ADAM_B1 ** ADAM_STEP
    c2 = 1.0 - ADAM_B2 ** ADAM_STEP

    def body(p_ref, w_ref, m_ref, v_ref, g_ref, d_ref, nm_ref, nv_ref):
        g = p_ref[0].astype(F32)
        for s in range(1, n_parts):
            g = g + p_ref[s].astype(F32)
        nm = ADAM_B1 * m_ref[...] + (1.0 - ADAM_B1) * g
        nv = ADAM_B2 * v_ref[...] + (1.0 - ADAM_B2) * (g * g)
        m_hat = nm / c1
        v_hat = nv / c2
        g_ref[...] = g
        d_ref[...] = -ADAM_LR * (m_hat / (jnp.sqrt(v_hat) + ADAM_EPS) + ADAM_WD * w_ref[...])
        nm_ref[...] = nm
        nv_ref[...] = nv

    blk = pl.BlockSpec((tr, cols), lambda i: (i, 0))
    shp = jax.ShapeDtypeStruct((rows, cols), F32)
    return pl.pallas_call(
        body, name=name,
        grid=(rows // tr,),
        in_specs=[pl.BlockSpec((n_parts, tr, cols), lambda i: (0, i, 0)), blk, blk, blk],
        out_specs=(blk, blk, blk, blk),
        out_shape=(shp, shp, shp, shp),
        compiler_params=pltpu.CompilerParams(vmem_limit_bytes=VMEM_LIMIT),
    )(parts, w, m, v)


def _loss_sum(rows):
    def body(r_ref, o_ref):
        o_ref[...] = jnp.sum(jnp.sum(r_ref[...], axis=0, keepdims=True), axis=1, keepdims=True)

    return pl.pallas_call(body, name="loss_sum", out_shape=jax.ShapeDtypeStruct((1, 1), F32))(rows)


def _attention_forward(proj, rel_bias, ride=()):
    buckets_np, masks_np = _bucket_maps()
    buckets, masks = jnp.asarray(buckets_np), jnp.asarray(masks_np)
    bias = _bias_expand(rel_bias, buckets, masks)
    fwd = [_attn_fwd(proj, bias, g, ride if g == 0 else ()) for g in range(3)]
    return bias, buckets, [f[0] for f in fwd], [f[1] for f in fwd], fwd[0][2:]


def _local_step(x2, tgt2, mod3, h, proj, attn, w_ao, w_co, w_o, conv_w, conv_b, ln_g, ln_b):
    bias, buckets, o_g, lse_g = attn

    (dproj, dyc, d_o, stats, dxd, merged, dy, a_in, da_out, s_in, ds_out, tail_vec) = _tail(
        x2, tgt2, mod3, o_g, lse_g, proj, w_ao, w_co, w_o, conv_w, conv_b, ln_g, ln_b)

    dbias = []
    for g in range(3):
        dproj, db = _attn_bwd(proj, d_o, stats, bias, dproj, g)
        dbias.append(db)
    g_rel_bias = _bias_grad(*dbias, buckets)
    dproj, conv_vec = _conv_bwd(dyc, proj, conv_w, dproj)

    gw_o = _mm_tn(merged, dy, D, False, "gw_o")
    gw_co = _mm_tn(s_in, ds_out, D, False, "gw_conv_out")
    gw_ao = _mm_tn(a_in, da_out, D, False, "gw_attn_out")
    gw_ao = jnp.transpose(gw_ao.reshape(GW, N_DEV, D // N_DEV), (1, 0, 2))
    return dproj, dxd, gw_ao, gw_co, gw_o, conv_vec, g_rel_bias, tail_vec


def kernel(x, c, w_ada, b_ada, w_in, conv_w, conv_b, rel_bias, w_attn_out, w_conv_out, w_o, ln_g, ln_b, loss_target, m_w_ada, m_b_ada, m_w_in, m_conv_w, m_conv_b, m_rel_bias, m_w_attn_out, m_w_conv_out, m_w_o, m_ln_g, m_ln_b, v_w_ada, v_b_ada, v_w_in, v_conv_w, v_conv_b, v_rel_bias, v_w_attn_out, v_w_conv_out, v_w_o, v_ln_g, v_ln_b):
    me = _my_index()
    x2 = x.reshape(T, D)
    tgt2 = loss_target.reshape(T, D)

    b_cols = lax.dynamic_slice(b_ada, (0, me * ADA_SHARD), (1, ADA_SHARD))
    c_g, mod_in = _mod_exchange(jnp.pad(c, ((0, 8 - BL), (0, 0))), w_ada[0], b_cols)
    c_all = c_g[:, 0:BL, :].reshape(N_DEV * BL, D)
    mod3 = jnp.transpose(mod_in[:, 0:BL, :], (1, 0, 2)).reshape(BL, 3, D)

    h = _prep_h(x2, mod3)
    proj, w_in_all = _gather_proj(_shard_order(), h, w_in[0].astype(BF16), 1024)

    *attn, (w_ao_g, w_co_g, w_o_g, conv_w_g) = _attention_forward(proj, rel_bias, (
        [w_attn_out[0].astype(BF16), w_conv_out[0].astype(BF16), w_o[0].astype(BF16), conv_w[0]],
        [jax.ShapeDtypeStruct((N_DEV, GW, D // N_DEV), BF16),
         jax.ShapeDtypeStruct((N_DEV, D // N_DEV, D), BF16),
         jax.ShapeDtypeStruct((N_DEV, D // N_DEV, D), BF16),
         jax.ShapeDtypeStruct((N_DEV, 3, D // N_DEV), F32)]))
    w_ao_full = jnp.transpose(w_ao_g, (1, 0, 2)).reshape(GW, D)
    w_co_full = w_co_g.reshape(D, D)
    w_o_full = w_o_g.reshape(D, D)
    conv_w_full = jnp.transpose(conv_w_g, (1, 0, 2)).reshape(3, D)

    (dproj, dxd, gw_ao, gw_co, gw_o, conv_vec, g_rel_bias, tail_vec) = _local_step(
        x2, tgt2, mod3, h, proj, attn, w_ao_full, w_co_full, w_o_full,
        conv_w_full, conv_b, ln_g, ln_b)

    g_conv_w_blocks = jnp.transpose(conv_vec[0:3].reshape(3, N_DEV, D // N_DEV), (1, 0, 2))
    partials = [gw_ao, gw_co.reshape(N_DEV, D // N_DEV, D), gw_o.reshape(N_DEV, D // N_DEV, D), g_conv_w_blocks]
    w_in_sums, w_in_parts, sib = _gw_in_pair(
        _slice_order(), h, dproj, partials,
        [jax.ShapeDtypeStruct((4, GW, D // N_DEV), BF16),
         jax.ShapeDtypeStruct((4, D // N_DEV, D), BF16),
         jax.ShapeDtypeStruct((4, D // N_DEV, D), BF16),
         jax.ShapeDtypeStruct((4, 3, D // N_DEV), F32)])
    names = ["w_attn_out", "w_conv_out", "w_o", "conv_w"]
    core = lax.axis_index("c").astype(jnp.int32).reshape(1)
    chip_sums = [w_in_sums] + [_pair_add(core, partials[a], sib[a], None, "pair_add_" + names[a])
                               for a in range(4)]
    hops = [(3,)] + [(1, 2, 3)] * 4
    grad_x, mod_vec, (r_in, r_ao, r_co, r_o, r_cw) = _dh_dx(
        dproj, w_in_all, x2, dxd, mod3, chip_sums, hops, w_in_parts)

    small = jnp.concatenate([
        tail_vec[0:4],
        jnp.pad(g_rel_bias.reshape(1, N_BUCKETS * N_HEADS), ((0, 0), (0, D - N_BUCKETS * N_HEADS))),
        jnp.zeros((3, D), F32)], axis=0)
    dmod = jnp.concatenate([mod_vec[0:2], mod_vec[2:4], tail_vec[4:6]], axis=1)
    small_g, dmod_g = _all_gather(
        [small, dmod],
        [jax.ShapeDtypeStruct((N_DEV, 8, D), F32), jax.ShapeDtypeStruct((N_DEV, BL, 3 * D), F32)],
        "gather_small")
    dmod_all = dmod_g.reshape(N_DEV * BL, 3 * D)
    loss = _loss_sum(small_g[:, 3, :]).reshape(())
    g_w_ada = _ada_bwd(jnp.transpose(c_all), lax.dynamic_slice(dmod_all, (0, me * ADA_SHARD),
                                                               (N_DEV * BL, ADA_SHARD)))

    def upd(parts, w, m, v, name, row_tile=None):
        shape = w.shape
        w2, m2, v2 = (t.reshape(parts.shape[1:]) for t in (w, m, v))
        return tuple(t.reshape(shape) for t in _adamw(parts, w2, m2, v2, name, row_tile))

    res = {
        "w_ada": upd(g_w_ada[None], w_ada, m_w_ada, v_w_ada, "adam_w_ada", 256),
        "b_ada": upd(dmod_all[:, None, :], b_ada, m_b_ada, v_b_ada, "adam_b_ada"),
        "w_in": upd(r_in, w_in, m_w_in, v_w_in, "adam_w_in", 128),
        "conv_w": upd(r_cw, conv_w, m_conv_w, v_conv_w, "adam_conv_w"),
        "conv_b": upd(small_g[:, 0:1, :], conv_b, m_conv_b, v_conv_b, "adam_conv_b"),
        "rel_bias": upd(small_g[:, 4, :N_BUCKETS * N_HEADS].reshape(N_DEV, N_BUCKETS, N_HEADS),
                        rel_bias, m_rel_bias, v_rel_bias, "adam_rel_bias"),
        "w_attn_out": upd(r_ao, w_attn_out, m_w_attn_out, v_w_attn_out, "adam_w_attn_out"),
        "w_conv_out": upd(r_co, w_conv_out, m_w_conv_out, v_w_conv_out, "adam_w_conv_out"),
        "w_o": upd(r_o, w_o, m_w_o, v_w_o, "adam_w_o"),
        "ln_g": upd(small_g[:, 1:2, :], ln_g, m_ln_g, v_ln_g, "adam_ln_g"),
        "ln_b": upd(small_g[:, 2:3, :], ln_b, m_ln_b, v_ln_b, "adam_ln_b"),
    }
    order = ["w_ada", "b_ada", "w_in", "conv_w", "conv_b", "rel_bias", "w_attn_out", "w_conv_out",
             "w_o", "ln_g", "ln_b"]
    outs = [loss, grad_x.reshape(BL, S, D)]
    for k in range(4):
        outs += [res[name][k] for name in order]
    return tuple(outs)
```

```python
import functools
import math

import numpy as np
import jax
import jax.numpy as jnp
from jax import lax
from jax.experimental import pallas as pl
from jax.experimental.pallas import tpu as pltpu

F32 = jnp.float32
BF16 = jnp.bfloat16
MESH = pl.DeviceIdType.MESH

N_DEV = 8
D = 1024
S = 2048
BL = 2
T = BL * S
NCOL = 11264
SHARD = NCOL // N_DEV
CB = 512
NCB = NCOL // CB
HD = 128
GW = 512
QB = 128
DILATIONS = (1, 4, 16)
N_STEPS = 128
N_BUCKETS = 32
N_HEADS = 12
ALPHA = 2.0 ** 0.25
LN_EPS = 1e-5
NEG_INF = -1e30
SCALE = HD ** -0.5
ADA_SHARD = 3 * D // N_DEV

CB_Q, CB_K, CB_V, CB_GA = 0, 3, 6, 9
KB_U, KB_BG, KB_CG, KB_GC, KB_MA, KB_MC = 5, 6, 7, 8, 9, 10

ADAM_LR, ADAM_B1, ADAM_B2, ADAM_EPS, ADAM_WD, ADAM_STEP = 0.001, 0.9, 0.999, 1e-08, 0.01, 10

VMEM_LIMIT = 56 * 1024 * 1024


def _dot(a, b):
    return jnp.dot(a, b, preferred_element_type=F32)


def _dot_nt(a, b):
    return lax.dot_general(a, b, (((1,), (1,)), ((), ())), preferred_element_type=F32)


def _dot_tn(a, b):
    return lax.dot_general(a, b, (((0,), (0,)), ((), ())), preferred_element_type=F32)


def _sigmoid(v):
    return 1.0 / (1.0 + jnp.exp(-v))


def _write_columns(pieces, dst_hbm, row0, sems):
    copies = []
    for k, (src, col0) in enumerate(pieces):
        rows, width = src.shape
        copies.append(pltpu.make_async_copy(
            src, dst_hbm.at[pl.ds(row0, rows), pl.ds(col0, width)], sems.at[k]))
    for cp in copies:
        cp.start()
    for cp in copies:
        cp.wait()


def _my_index():
    return 4 * lax.axis_index("x") + 2 * lax.axis_index("y") + lax.axis_index("c")


class _Gather:
    def __init__(self, ins, outs, stage, send_sems, recv_sems, local_sems):
        self.ins, self.outs, self.stage = ins, outs, stage
        self.send_sems, self.recv_sems, self.local_sems = send_sems, recv_sems, local_sems
        x, y, c = lax.axis_index("x"), lax.axis_index("y"), lax.axis_index("c")
        self.c = c
        self.me, self.sibling = (x, y, c), (x, y, 1 - c)
        self.chips = [(1 - x, y), (x, 1 - y), (1 - x, 1 - y)]

    @staticmethod
    def scratch(arrs):
        n = len(arrs)
        return ([pltpu.SemaphoreType.DMA((7 * n,)), pltpu.SemaphoreType.DMA((7 * n,)),
                 pltpu.SemaphoreType.DMA((n,))] + [pltpu.VMEM(a.shape, a.dtype) for a in arrs])

    def _copy(self, a, k, block, to, src=None):
        dst = self.outs[a].at[4 * block[0] + 2 * block[1] + block[2]]
        return pltpu.make_async_remote_copy(
            src_ref=dst if src is None else src, dst_ref=dst,
            send_sem=self.send_sems.at[a * 7 + k], recv_sem=self.recv_sems.at[a * 7 + k],
            device_id=to, device_id_type=MESH)

    def _first(self):
        first = []
        for a in range(len(self.ins)):
            first.append(self._copy(a, 0, self.me, self.sibling, src=self.ins[a]))
            first += [self._copy(a, 1 + j, self.me, (*chip, self.c), src=self.ins[a])
                      for j, chip in enumerate(self.chips)]
        return first

    def _mine(self):
        me = self.me
        return [pltpu.make_async_copy(self.stage[a], self.outs[a].at[4 * me[0] + 2 * me[1] + me[2]],
                                      self.local_sems.at[a]) for a in range(len(self.ins))]

    def begin(self):
        for cp in self._first():
            cp.start()
        loads = [pltpu.make_async_copy(self.ins[a], self.stage[a], self.local_sems.at[a])
                 for a in range(len(self.ins))]
        for cp in loads:
            cp.start()
        for cp in loads:
            cp.wait()
        for cp in self._mine():
            cp.start()

    def finish(self):
        n, c, me, sibling = len(self.ins), self.c, self.me, self.sibling
        passed = []
        for j, chip in enumerate(self.chips):
            for a in range(n):
                self._copy(a, 1 + j, (*chip, c), me).wait_recv()
                fwd = self._copy(a, 4 + j, (*chip, c), sibling)
                fwd.start()
                passed.append(fwd)
        for a in range(n):
            self._copy(a, 0, sibling, me).wait_recv()
        for j, chip in enumerate(self.chips):
            for a in range(n):
                self._copy(a, 4 + j, (*chip, 1 - c), me).wait_recv()
        for cp in self._first() + passed:
            cp.wait_send()
        for cp in self._mine():
            cp.wait()


def _all_gather(arrs, out_shapes, name):
    n = len(arrs)

    def body(*refs):
        g = _Gather(refs[:n], refs[n:2 * n], refs[2 * n + 3:], *refs[2 * n:2 * n + 3])
        g.begin()
        g.finish()

    any_spec = pl.BlockSpec(memory_space=pl.ANY)
    return pl.pallas_call(
        body, name=name,
        out_shape=tuple(out_shapes),
        in_specs=[any_spec] * n,
        out_specs=tuple([any_spec] * n),
        scratch_shapes=_Gather.scratch(arrs),
    )(*arrs)


def _slice_order():
    x, y, c = lax.axis_index("x"), lax.axis_index("y"), lax.axis_index("c")
    slots = []
    for q in (2 * (1 - x) + y, 2 * x + (1 - y), 2 * (1 - x) + (1 - y), 2 * x + y):
        slots += [2 * q + 1 - c, 2 * q + c]
    return jnp.stack(slots).astype(jnp.int32)


def _gw_in_pair(order, h, dproj, smalls, small_shapes4):
    kk, m = h.shape
    tk = min(kk, 2048)
    nk = kk // tk
    ncols = dproj.shape[1] // N_DEV
    n = len(smalls)

    def body(order_ref, h_ref, d_ref, *rest):
        ins = rest[:n]
        sums_hbm, parts_hbm = rest[n], rest[n + 1]
        sib = rest[n + 2:2 * n + 2]
        (acc, sendbuf, recvbuf, sumbuf, send_sems, recv_sems, local_sem, ssend, srecv,
         isend, irecv) = rest[2 * n + 2:]
        js, k = pl.program_id(0), pl.program_id(1)
        x, y, c = lax.axis_index("x"), lax.axis_index("y"), lax.axis_index("c")
        sibling = (x, y, 1 - c)
        my_chip = 2 * x + y
        near = [(1 - x, y, c), (x, 1 - y, c)]

        def ici_copy(p, out_chip):
            peer = near[p]
            return pltpu.make_async_remote_copy(
                src_ref=sumbuf.at[p], dst_ref=parts_hbm.at[out_chip],
                send_sem=isend.at[p], recv_sem=irecv.at[p], device_id=peer, device_id_type=MESH)

        def small_copies():
            return [pltpu.make_async_remote_copy(
                        src_ref=ins[a].at[2 * q + 1 - c], dst_ref=sib[a].at[q],
                        send_sem=ssend.at[a * 4 + q], recv_sem=srecv.at[a * 4 + q],
                        device_id=sibling, device_id_type=MESH)
                    for a in range(n) for q in range(4)]

        def slice_copy(p):
            return pltpu.make_async_remote_copy(
                src_ref=sendbuf, dst_ref=recvbuf.at[p], send_sem=send_sems.at[p], recv_sem=recv_sems.at[p],
                device_id=sibling, device_id_type=MESH)

        def sum_copy(p):
            return pltpu.make_async_copy(sumbuf.at[2], sums_hbm.at[order_ref[2 * p] // 2], local_sem)

        @pl.when((js == 0) & (k == 0))
        def _():
            for cp in small_copies():
                cp.start()

        @pl.when(k == 0)
        def _():
            acc[...] = jnp.zeros_like(acc)

        acc[...] += _dot_tn(h_ref[...], d_ref[...])

        for p in range(4):
            @pl.when((js == 2 * p) & (k == nk - 1))
            def _():
                if p > 0:
                    slice_copy(p - 1).wait_send()
                sendbuf[...] = acc[...].astype(BF16)
                slice_copy(p).start()

            @pl.when((js == 2 * p + 1) & (k == nk - 1))
            def _():
                slice_copy(p).wait_recv()
                if p == 3:
                    sum_copy(2).wait()
                sumbuf[min(p, 2)] = (acc[...] + recvbuf[p].astype(F32)).astype(BF16)
                if p < 2:
                    ici_copy(p, my_chip).start()
                else:
                    sum_copy(p).start()

        @pl.when((js == N_DEV - 1) & (k == nk - 1))
        def _():
            slice_copy(3).wait_send()
            sum_copy(3).wait()
            for cp in small_copies():
                cp.wait()
            for p in range(2):
                ici_copy(p, 2 * near[p][0] + near[p][1]).wait_recv()
                ici_copy(p, my_chip).wait_send()

    any_spec = pl.BlockSpec(memory_space=pl.ANY)
    res = pl.pallas_call(
        body, name="gw_in_pair",
        grid_spec=pltpu.PrefetchScalarGridSpec(
            num_scalar_prefetch=1,
            grid=(N_DEV, nk),
            in_specs=[pl.BlockSpec((tk, m), lambda js, k, order_ref: (k, 0)),
                      pl.BlockSpec((tk, ncols), lambda js, k, order_ref: (k, order_ref[js]))] + [any_spec] * n,
            out_specs=(any_spec,) * (n + 2),
            scratch_shapes=[pltpu.VMEM((m, ncols), F32), pltpu.VMEM((m, ncols), BF16),
                            pltpu.VMEM((4, m, ncols), BF16), pltpu.VMEM((3, m, ncols), BF16),
                            pltpu.SemaphoreType.DMA((4,)), pltpu.SemaphoreType.DMA((4,)),
                            pltpu.SemaphoreType.DMA,
                            pltpu.SemaphoreType.DMA((4 * n,)), pltpu.SemaphoreType.DMA((4 * n,)),
                            pltpu.SemaphoreType.DMA((2,)), pltpu.SemaphoreType.DMA((2,))]),
        out_shape=(jax.ShapeDtypeStruct((4, m, ncols), BF16),) * 2 + tuple(small_shapes4),
        compiler_params=pltpu.CompilerParams(vmem_limit_bytes=VMEM_LIMIT),
    )(order, h, dproj, *smalls)
    return res[0], res[1], res[2:]


def _chip_copies(ins, outs, send_sems, recv_sems, local_sems, hops):
    n = len(ins)
    x, y, c = lax.axis_index("x"), lax.axis_index("y"), lax.axis_index("c")
    my_chip = 2 * x + y

    def peer_of(k):
        return ((1 - x) if (k >> 1) & 1 else x, (1 - y) if k & 1 else y, c)

    def copy(a, k, out_chip):
        peer = peer_of(k)
        return pltpu.make_async_remote_copy(
            src_ref=ins[a].at[2 * peer[0] + peer[1]], dst_ref=outs[a].at[out_chip],
            send_sem=send_sems.at[a * 3 + k - 1], recv_sem=recv_sems.at[a * 3 + k - 1],
            device_id=peer, device_id_type=MESH)

    sends = [copy(a, k, my_chip) for k in range(1, 4) for a in range(n) if k in hops[a]]
    arrivals = []
    for k in range(1, 4):
        peer = peer_of(k)
        arrivals += [copy(a, k, 2 * peer[0] + peer[1]) for a in range(n) if k in hops[a]]
    mine = [pltpu.make_async_copy(ins[a].at[my_chip], outs[a].at[my_chip], local_sems.at[a])
            for a in range(n)]
    return sends, arrivals, mine


def _pair_add(core, mine, theirs, row_tile, name):
    _, rows, cols = theirs.shape
    tr = rows if row_tile is None else row_tile

    def body(core_ref, a_ref, b_ref, o_ref):
        o_ref[...] = (a_ref[...].astype(F32) + b_ref[...].astype(F32)).astype(o_ref.dtype)

    blk = pl.BlockSpec((None, tr, cols), lambda q, i, core_ref: (q, i, 0))
    return pl.pallas_call(
        body, name=name,
        grid_spec=pltpu.PrefetchScalarGridSpec(
            num_scalar_prefetch=1,
            grid=(4, rows // tr),
            in_specs=[pl.BlockSpec((None, tr, cols), lambda q, i, core_ref: (2 * q + core_ref[0], i, 0)), blk],
            out_specs=blk),
        out_shape=jax.ShapeDtypeStruct(theirs.shape, theirs.dtype),
    )(core, mine, theirs)


def _mod_exchange(c8, w_ada, b_cols):
    cols = w_ada.shape[1]

    def body(c_ref, w_ref, b_ref, call_ref, mod_ref, msend, send1, recv1, send2, recv2):
        x, y, c = lax.axis_index("x"), lax.axis_index("y"), lax.axis_index("c")
        my_slot = 4 * x + 2 * y + c

        def peer_of(k):
            return ((1 - x) if (k >> 2) & 1 else x, (1 - y) if (k >> 1) & 1 else y, (1 - c) if k & 1 else c)

        def slot_of(dev):
            return 4 * dev[0] + 2 * dev[1] + dev[2]

        def exchange(src_of, dst_ref, send_sems, recv_sems):
            sends, arrivals = [], []
            for k in range(1, 8):
                peer = peer_of(k)
                sends.append(pltpu.make_async_remote_copy(
                    src_ref=src_of(slot_of(peer)), dst_ref=dst_ref.at[my_slot],
                    send_sem=send_sems.at[k - 1], recv_sem=recv_sems.at[k - 1],
                    device_id=peer, device_id_type=MESH))
                arrivals.append(pltpu.make_async_remote_copy(
                    src_ref=src_of(my_slot), dst_ref=dst_ref.at[slot_of(peer)],
                    send_sem=send_sems.at[k - 1], recv_sem=recv_sems.at[k - 1],
                    device_id=peer, device_id_type=MESH))
            for cp in sends:
                cp.start()
            for cp in arrivals:
                cp.wait_recv()
            for cp in sends:
                cp.wait_send()

        call_ref[my_slot] = c_ref[...]
        exchange(lambda s: c_ref, call_ref, send1, recv1)
        cv = call_ref[...].reshape(N_DEV * 8, c_ref.shape[1])
        act = cv * _sigmoid(cv)
        mod = jnp.dot(act, w_ref[...], preferred_element_type=F32,
                      precision=lax.Precision.HIGHEST) + b_ref[...]
        msend[...] = mod.reshape(N_DEV, 8, cols)
        mod_ref[my_slot] = msend[my_slot]
        exchange(lambda s: msend.at[s], mod_ref, send2, recv2)

    return pl.pallas_call(
        body, name="mod_exchange",
        out_shape=(jax.ShapeDtypeStruct((N_DEV, 8, c8.shape[1]), F32),
                   jax.ShapeDtypeStruct((N_DEV, 8, cols), F32)),
        scratch_shapes=[pltpu.VMEM((N_DEV, 8, cols), F32)] + [pltpu.SemaphoreType.DMA((7,))] * 4,
    )(c8, w_ada, b_cols)


def _ada_bwd(c_all_t, dmod_cols):
    def body(c_ref, d_ref, o_ref):
        cv = c_ref[...]
        sc = cv * _sigmoid(cv)
        o_ref[...] = jnp.dot(sc, d_ref[...], preferred_element_type=F32,
                             precision=lax.Precision.HIGHEST)

    return pl.pallas_call(
        body, name="ada_bwd",
        out_shape=jax.ShapeDtypeStruct((c_all_t.shape[0], dmod_cols.shape[1]), F32),
    )(c_all_t, dmod_cols)


def _prep_h(x2, mod3):
    ts = 512
    per_seq = S // ts

    def body(x_ref, mod_ref, h_ref):
        shift = mod_ref[0, 0:1, :]
        scale = mod_ref[0, 1:2, :]
        h_ref[...] = (x_ref[...] * (1.0 + scale) + shift).astype(BF16)

    return pl.pallas_call(
        body, name="prep_h",
        grid=(T // ts,),
        in_specs=[pl.BlockSpec((ts, D), lambda i: (i, 0)),
                  pl.BlockSpec((1, 3, D), lambda i: (i // per_seq, 0, 0))],
        out_specs=pl.BlockSpec((ts, D), lambda i: (i, 0)),
        out_shape=jax.ShapeDtypeStruct((T, D), BF16),
    )(x2, mod3)


def _shard_order():
    x, y, c = lax.axis_index("x"), lax.axis_index("y"), lax.axis_index("c")
    devs = [(x, y, c), (x, y, 1 - c)]
    for chip in [(1 - x, y), (x, 1 - y), (1 - x, 1 - y)]:
        devs += [(*chip, c), (*chip, 1 - c)]
    return jnp.stack([4 * d[0] + 2 * d[1] + d[2] for d in devs]).astype(jnp.int32)


def _gather_proj(order, h, w_shard, tm):
    rows, kdim = h.shape
    ncols = w_shard.shape[1]
    n_i = rows // tm

    def body(order_ref, h_ref, mine_hbm, o_ref, all_hbm, wv, send_sems, recv_sems, local_sems):
        j, i = pl.program_id(0), pl.program_id(1)
        x, y, c = lax.axis_index("x"), lax.axis_index("y"), lax.axis_index("c")
        me, sibling = (x, y, c), (x, y, 1 - c)
        chips = [(1 - x, y), (x, 1 - y), (1 - x, 1 - y)]

        def slot(dev):
            return 4 * dev[0] + 2 * dev[1] + dev[2]

        def copy(k, block, to):
            return pltpu.make_async_remote_copy(
                src_ref=wv.at[slot(block)], dst_ref=wv.at[slot(block)],
                send_sem=send_sems.at[k], recv_sem=recv_sems.at[k],
                device_id=to, device_id_type=MESH)

        def keep(step, block):
            return pltpu.make_async_copy(wv.at[slot(block)], all_hbm.at[slot(block)], local_sems.at[step])

        first = [copy(0, me, sibling)] + [copy(1 + q, me, (*chip, c)) for q, chip in enumerate(chips)]
        passed = [copy(4 + q, (*chip, c), sibling) for q, chip in enumerate(chips)]
        due = [(me, None, None), (sibling, copy(0, sibling, me), None)]
        for q, chip in enumerate(chips):
            due.append(((*chip, c), copy(1 + q, (*chip, c), me), passed[q]))
            due.append(((*chip, 1 - c), copy(4 + q, (*chip, 1 - c), me), None))

        @pl.when((j == 0) & (i == 0))
        def _():
            load = pltpu.make_async_copy(mine_hbm, wv.at[slot(me)], local_sems.at[N_DEV])
            load.start()
            load.wait()
            for cp in first:
                cp.start()
            keep(0, me).start()

        for step in range(1, N_DEV):
            block, arrival, forward = due[step]

            @pl.when((j == step) & (i == 0))
            def _():
                arrival.wait_recv()
                if forward is not None:
                    forward.start()
                keep(step, block).start()

        o_ref[...] = _dot(h_ref[...], wv[order_ref[j]]).astype(BF16)

        @pl.when((j == N_DEV - 1) & (i == n_i - 1))
        def _():
            for cp in first + passed:
                cp.wait_send()
            for step in range(N_DEV):
                keep(step, due[step][0]).wait()

    return pl.pallas_call(
        body, name="gather_proj",
        grid_spec=pltpu.PrefetchScalarGridSpec(
            num_scalar_prefetch=1,
            grid=(N_DEV, n_i),
            in_specs=[pl.BlockSpec((tm, kdim), lambda j, i, order_ref: (i, 0)),
                      pl.BlockSpec(memory_space=pl.ANY)],
            out_specs=(pl.BlockSpec((tm, ncols), lambda j, i, order_ref: (i, order_ref[j])),
                       pl.BlockSpec(memory_space=pl.ANY)),
            scratch_shapes=[pltpu.VMEM((N_DEV, kdim, ncols), BF16),
                            pltpu.SemaphoreType.DMA((7,)), pltpu.SemaphoreType.DMA((7,)),
                            pltpu.SemaphoreType.DMA((N_DEV + 1,))]),
        out_shape=(jax.ShapeDtypeStruct((rows, N_DEV * ncols), BF16),
                   jax.ShapeDtypeStruct((N_DEV, kdim, ncols), BF16)),
        compiler_params=pltpu.CompilerParams(vmem_limit_bytes=VMEM_LIMIT),
    )(order, h, w_shard)


def _bucket_maps():
    a = np.arange(QB)[:, None]
    b = np.arange(2 * QB)[None, :]
    steps = a + QB - b
    maps = []
    for dil in DILATIONS:
        dist = np.maximum(steps, 0) * dil
        nf = np.maximum(dist, 1).astype(np.float32)
        large = 16 + (np.log(nf / np.float32(16)) / np.float32(math.log(128.0))
                      * np.float32(16)).astype(np.int32)
        large = np.minimum(large, N_BUCKETS - 1)
        maps.append(np.where(dist < 16, dist, large).astype(np.int32))
    band = (steps >= 0) & (steps <= N_STEPS)
    first = band & (b >= QB)
    masks = np.stack([first, band]).astype(np.int32)
    return np.stack(maps), masks


def _bias_expand(rel_bias, buckets, masks):
    def body(tab_ref, bk_ref, mk_ref, o_ref):
        for g in range(3):
            bk = bk_ref[g]
            for h in range(4):
                col = 4 * g + h
                val = jnp.zeros((QB, 2 * QB), F32)
                for k in range(N_BUCKETS):
                    val = jnp.where(bk == k, tab_ref[k, col], val)
                o_ref[g, 0, h] = jnp.where(mk_ref[0] != 0, val, NEG_INF)
                o_ref[g, 1, h] = jnp.where(mk_ref[1] != 0, val, NEG_INF)

    return pl.pallas_call(
        body, name="bias_expand",
        in_specs=[pl.BlockSpec(memory_space=pltpu.SMEM),
                  pl.BlockSpec(memory_space=pltpu.VMEM),
                  pl.BlockSpec(memory_space=pltpu.VMEM)],
        out_shape=jax.ShapeDtypeStruct((3, 2, 4, QB, 2 * QB), F32),
    )(rel_bias, buckets, masks)


def _bias_grad(ds1, ds2, ds3, buckets):
    def body(d1_ref, d2_ref, d3_ref, bk_ref, o_ref):
        for g, d_ref in enumerate((d1_ref, d2_ref, d3_ref)):
            bk = bk_ref[g]
            for h in range(4):
                dv = d_ref[h]
                for k in range(N_BUCKETS):
                    o_ref[k, 4 * g + h] = jnp.sum(jnp.where(bk == k, dv, 0.0))

    return pl.pallas_call(
        body, name="bias_grad",
        in_specs=[pl.BlockSpec(memory_space=pltpu.VMEM)] * 4,
        out_specs=pl.BlockSpec(memory_space=pltpu.SMEM),
        out_shape=jax.ShapeDtypeStruct((N_BUCKETS, N_HEADS), F32),
    )(ds1, ds2, ds3, buckets)


def _scratch_sets(rows):
    return 4 if rows <= 512 else 1


def _unit_chunks(dil, size=16):
    units = [(h, r) for h in range(4) for r in range(dil)]
    return [units[i:i + size] for i in range(0, len(units), size)]


def _residue_rows(src_ref, copies, h, residue):
    sl = slice(h * HD, (h + 1) * HD)
    if copies is None:
        return lambda r: src_ref[:, sl]
    buf = copies[h % len(copies)]
    buf[...] = src_ref[:, sl].astype(F32)
    return lambda r: buf[residue(r), :].astype(BF16)


def _attn_fwd(proj, bias, g, ride=()):
    dil = DILATIONS[g]
    rows = QB * dil
    nsb = S // rows
    has_prev = nsb > 1

    def residue(r):
        return pl.ds(r, QB, stride=dil) if dil > 1 else pl.ds(0, QB)

    strided = dil > 1
    n_sets = _scratch_sets(rows)
    n_attn = 6 if has_prev else 4
    ride_arrs, ride_shapes = ride if ride else ((), ())
    n_ride = len(ride_arrs)
    n_in = n_attn + n_ride
    n_copied = (4 + (2 if has_prev else 0)) * (n_sets if strided else 0)

    def body(*refs):
        q_ref, kc_ref, vc_ref = refs[:3]
        kp_ref, vp_ref = refs[3:5] if has_prev else (None, None)
        b_ref = refs[n_attn - 1]
        o_ref, l_ref = refs[n_in:n_in + 2]
        scr = list(refs[n_in + 2 + n_ride:])
        ls = [scr.pop(0) for _ in range(4)]
        copies = {name: [scr.pop(0) for _ in range(n_sets)] if strided else None
                  for name in ("q", "kc", "vc", "o") + (("kp", "vp") if has_prev else ())}
        if n_ride:
            gather = _Gather(refs[n_attn:n_in], refs[n_in + 2:n_in + 2 + n_ride], scr[3:], *scr[:3])

            @pl.when((pl.program_id(0) == 0) & (pl.program_id(1) == 0))
            def _():
                gather.begin()
        lane = lax.broadcasted_iota(jnp.int32, (QB, 128), 1)
        refs_of = {"q": q_ref, "kc": kc_ref, "vc": vc_ref, "kp": kp_ref, "vp": vp_ref}
        for chunk in _unit_chunks(dil):
            rows_of = {h: {name: _residue_rows(refs_of[name], copies[name], h, residue)
                           for name in refs_of if refs_of[name] is not None}
                       for h in sorted({h for h, _ in chunk})}

            def batch(name):
                return jnp.stack([rows_of[h][name](r) for h, r in chunk])

            q, k, v = batch("q"), batch("kc"), batch("vc")
            if has_prev:
                k = jnp.concatenate([batch("kp"), k], axis=1)
                v = jnp.concatenate([batch("vp"), v], axis=1)
                bias_b = jnp.stack([b_ref[h] for h, _ in chunk])
            else:
                bias_b = jnp.stack([b_ref[h, :, QB:] for h, _ in chunk])
            s = jnp.einsum("uqd,ukd->uqk", q, k, preferred_element_type=F32) * SCALE + bias_b
            m = jnp.max(s, axis=-1, keepdims=True)
            p = jnp.exp(s - m)
            l = jnp.sum(p, axis=-1, keepdims=True)
            o = jnp.einsum("uqk,ukd->uqd", p.astype(BF16), v, preferred_element_type=F32) / l
            lse = m + jnp.log(l)
            for i, (h, r) in enumerate(chunk):
                if strided:
                    copies["o"][h % n_sets][residue(r), :] = o[i]
                else:
                    o_ref[:, h * HD:(h + 1) * HD] = o[i]
                ls[h][r * QB:(r + 1) * QB, :] = jnp.where(lane == h, lse[i], 0.0)
            if strided:
                for h in sorted({h for h, _ in chunk}):
                    o_ref[:, h * HD:(h + 1) * HD] = copies["o"][h % n_sets][...]
        for r in range(dil):
            blk = slice(r * QB, (r + 1) * QB)
            l_ref[residue(r), :] = (ls[0][blk, :] + ls[1][blk, :]) + (ls[2][blk, :] + ls[3][blk, :])
        if n_ride:
            @pl.when((pl.program_id(0) == BL - 1) & (pl.program_id(1) == nsb - 1))
            def _():
                gather.finish()

    def row(b, n):
        return b * nsb + n

    def prev(b, n):
        return b * nsb + jnp.maximum(n - 1, 0)

    in_specs = [
        pl.BlockSpec((rows, GW), lambda b, n: (row(b, n), CB_Q + g)),
        pl.BlockSpec((rows, GW), lambda b, n: (row(b, n), CB_K + g)),
        pl.BlockSpec((rows, GW), lambda b, n: (row(b, n), CB_V + g)),
    ]
    args = [proj, proj, proj]
    scratch = [pltpu.VMEM((rows, 128), F32)] * (4 + n_copied)
    if has_prev:
        in_specs += [pl.BlockSpec((rows, GW), lambda b, n: (prev(b, n), CB_K + g)),
                     pl.BlockSpec((rows, GW), lambda b, n: (prev(b, n), CB_V + g))]
        args += [proj, proj]
    in_specs.append(pl.BlockSpec((None, None, 4, QB, 2 * QB),
                                 lambda b, n: (g, jnp.minimum(n, 1), 0, 0, 0)))
    args.append(bias)
    any_spec = pl.BlockSpec(memory_space=pl.ANY)
    return pl.pallas_call(
        body, name=f"attn_fwd{g}",
        grid=(BL, nsb),
        in_specs=in_specs + [any_spec] * n_ride,
        out_specs=(pl.BlockSpec((rows, GW), lambda b, n: (row(b, n), 0)),
                   pl.BlockSpec((rows, 128), lambda b, n: (row(b, n), 0))) + (any_spec,) * n_ride,
        out_shape=(jax.ShapeDtypeStruct((T, GW), F32), jax.ShapeDtypeStruct((T, 128), F32)) + tuple(ride_shapes),
        scratch_shapes=scratch + (_Gather.scratch(ride_arrs) if n_ride else []),
        compiler_params=pltpu.CompilerParams(vmem_limit_bytes=VMEM_LIMIT),
    )(*args, *ride_arrs)


def _attn_bwd(proj, d_out, stats, bias, dproj, g):
    dil = DILATIONS[g]
    rows = QB * dil
    nsb = S // rows
    has_prev = nsb > 1
    n_steps = nsb + 1 if has_prev else 1
    n_in = 7 + (2 if has_prev else 0)

    def residue(r):
        return pl.ds(r, QB, stride=dil) if dil > 1 else pl.ds(0, QB)

    strided = dil > 1
    n_sets = _scratch_sets(rows)

    def body(*refs):
        q_ref, kc_ref, vc_ref, do_ref, st_ref, b_ref = refs[:6]
        kp_ref, vp_ref = refs[6:8] if has_prev else (None, None)
        out_ref, db_ref = refs[n_in], refs[n_in + 1]
        scr = list(refs[n_in + 2:])
        sq, sk, sv, sems = [scr.pop(0) for _ in range(4)]
        carry = scr.pop(0) if has_prev else None
        sts = scr.pop(0) if strided else st_ref
        copies = {name: [scr.pop(0) for _ in range(n_sets)] if strided else None
                  for name in ("q", "kc", "vc", "do", "dq", "dk", "dv") + (("kp", "vp") if has_prev else ())}
        b, n = pl.program_id(0), pl.program_id(1)

        @pl.when((b == 0) & (n == 0))
        def _():
            db_ref[...] = jnp.zeros_like(db_ref)

        def finish(h, r, dq, dk, dv):
            if strided:
                for name, val in (("dq", dq), ("dk", dk), ("dv", dv)):
                    copies[name][h % n_sets][residue(r), :] = val
            else:
                sl = slice(h * HD, (h + 1) * HD)
                sq[:, sl], sk[:, sl], sv[:, sl] = dq.astype(BF16), dk.astype(BF16), dv.astype(BF16)

        def finish_head(h):
            if strided:
                sl = slice(h * HD, (h + 1) * HD)
                sq[:, sl] = copies["dq"][h % n_sets][...].astype(BF16)
                sk[:, sl] = copies["dk"][h % n_sets][...].astype(BF16)
                sv[:, sl] = copies["dv"][h % n_sets][...].astype(BF16)

        def write_block(blk_idx):
            row0 = pl.multiple_of(blk_idx * rows, rows)
            _write_columns([(sq, CB * (CB_Q + g)), (sk, CB * (CB_K + g)), (sv, CB * (CB_V + g))],
                           out_ref, row0, sems)

        def carried(h, r):
            blk = slice(r * QB, (r + 1) * QB)
            return ((blk, slice(h * HD, (h + 1) * HD)), (blk, slice(GW + h * HD, GW + (h + 1) * HD)),
                    (blk, slice(2 * GW + h * HD, 2 * GW + (h + 1) * HD)))

        if has_prev:
            @pl.when(n == 0)
            def _():
                carry[...] = jnp.zeros_like(carry)

            @pl.when(n == nsb)
            def _():
                for h in range(4):
                    for r in range(dil):
                        cq, ck, cv = carried(h, r)
                        finish(h, r, carry[cq], carry[ck], carry[cv])
                    finish_head(h)
                write_block(b * nsb + nsb - 1)

        @pl.when(n < nsb)
        def _():
            if strided:
                for r in range(dil):
                    sts[r * QB:(r + 1) * QB, :] = st_ref[residue(r), :]
            refs_of = {"q": q_ref, "kc": kc_ref, "vc": vc_ref, "do": do_ref, "kp": kp_ref, "vp": vp_ref}
            for chunk in _unit_chunks(dil):
                heads = sorted({h for h, _ in chunk})
                rows_of = {h: {name: _residue_rows(refs_of[name], copies[name], h, residue)
                               for name in refs_of if refs_of[name] is not None}
                           for h in heads}

                def batch(name):
                    return jnp.stack([rows_of[h][name](r) for h, r in chunk])

                q, k, v, do = batch("q"), batch("kc"), batch("vc"), batch("do")
                if has_prev:
                    k = jnp.concatenate([batch("kp"), k], axis=1)
                    v = jnp.concatenate([batch("vp"), v], axis=1)
                    bias_b = jnp.stack([b_ref[h] for h, _ in chunk])
                else:
                    bias_b = jnp.stack([b_ref[h, :, QB:] for h, _ in chunk])
                lse = jnp.stack([sts[r * QB:(r + 1) * QB, h:h + 1] for h, r in chunk])
                delta = jnp.stack([sts[r * QB:(r + 1) * QB, 4 + h:5 + h] for h, r in chunk])
                s = jnp.einsum("uqd,ukd->uqk", q, k, preferred_element_type=F32) * SCALE + bias_b
                p = jnp.exp(s - lse)
                ds = p * (jnp.einsum("uqd,ukd->uqk", do, v, preferred_element_type=F32) - delta)
                for h in heads:
                    mine = [ds[i] for i, (hh, _) in enumerate(chunk) if hh == h]
                    tot = mine[0]
                    for extra in mine[1:]:
                        tot = tot + extra
                    if has_prev:
                        db_ref[h] += tot
                    else:
                        db_ref[h, :, QB:] += tot
                dsb, pb = ds.astype(BF16), p.astype(BF16)
                dq = jnp.einsum("uqk,ukd->uqd", dsb, k, preferred_element_type=F32) * SCALE
                dk = jnp.einsum("uqk,uqd->ukd", dsb, q, preferred_element_type=F32) * SCALE
                dv = jnp.einsum("uqk,uqd->ukd", pb, do, preferred_element_type=F32)
                for i, (h, r) in enumerate(chunk):
                    if has_prev:
                        cq, ck, cv = carried(h, r)
                        finish(h, r, carry[cq], carry[ck] + dk[i, :QB], carry[cv] + dv[i, :QB])
                        carry[cq] = dq[i]
                        carry[ck] = dk[i, QB:]
                        carry[cv] = dv[i, QB:]
                    else:
                        finish(h, r, dq[i], dk[i], dv[i])
                for h in heads:
                    finish_head(h)
            if has_prev:
                @pl.when(n > 0)
                def _():
                    write_block(b * nsb + n - 1)
            else:
                write_block(b)

    def row(b, n):
        return b * nsb + jnp.minimum(n, nsb - 1)

    def prev(b, n):
        return b * nsb + jnp.maximum(jnp.minimum(n, nsb - 1) - 1, 0)

    in_specs = [
        pl.BlockSpec((rows, GW), lambda b, n: (row(b, n), CB_Q + g)),
        pl.BlockSpec((rows, GW), lambda b, n: (row(b, n), CB_K + g)),
        pl.BlockSpec((rows, GW), lambda b, n: (row(b, n), CB_V + g)),
        pl.BlockSpec((rows, GW), lambda b, n: (row(b, n), 0)),
        pl.BlockSpec((rows, 128), lambda b, n: (row(b, n), 0)),
        pl.BlockSpec((None, None, 4, QB, 2 * QB),
                     lambda b, n: (g, jnp.minimum(jnp.minimum(n, nsb - 1), 1), 0, 0, 0)),
    ]
    args = [proj, proj, proj, d_out, stats, bias]
    scratch = [pltpu.VMEM((rows, GW), BF16)] * 3 + [pltpu.SemaphoreType.DMA((3,))]
    if has_prev:
        in_specs += [pl.BlockSpec((rows, GW), lambda b, n: (prev(b, n), CB_K + g)),
                     pl.BlockSpec((rows, GW), lambda b, n: (prev(b, n), CB_V + g))]
        args += [proj, proj]
        scratch.append(pltpu.VMEM((rows, 3 * GW), F32))
    if strided:
        n_copied = (7 + (2 if has_prev else 0)) * n_sets
        scratch += [pltpu.VMEM((rows, 128), F32)] * (1 + n_copied)
    in_specs.append(pl.BlockSpec(memory_space=pl.ANY))
    args.append(dproj)
    return pl.pallas_call(
        body, name=f"attn_bwd{g}",
        grid=(BL, n_steps),
        in_specs=in_specs,
        out_specs=(pl.BlockSpec(memory_space=pl.ANY),
                   pl.BlockSpec((4, QB, 2 * QB), lambda b, n: (0, 0, 0))),
        out_shape=(jax.ShapeDtypeStruct((T, NCOL), BF16),
                   jax.ShapeDtypeStruct((4, QB, 2 * QB), F32)),
        scratch_shapes=scratch,
        input_output_aliases={len(args) - 1: 0},
        compiler_params=pltpu.CompilerParams(vmem_limit_bytes=VMEM_LIMIT),
    )(*args)


def _tail(x2, tgt2, mod3, o_g, lse_g, proj, w_ao, w_co, w_o, conv_w, conv_b, ln_g, ln_b):
    tm = 256
    per_seq = S // tm
    halo = 16

    def body(x_ref, t_ref, mod_ref, o1_ref, o2_ref, o3_ref, l1_ref, l2_ref, l3_ref,
             ga_ref, u_ref, bg_ref, cg_ref, gc_ref, ma_ref, mc_ref, up_ref, cp_ref,
             wao_ref, wco_ref, wo_ref, cw_ref, cb_ref, lg_ref, lb_ref,
             dproj_ref, dyc_ref, do_ref, st_ref, dxd_ref,
             mg_ref, dy_ref, ain_ref, dao_ref, sin_ref, dso_ref, vec_ref,
             dga_s, dbg_s, dgm_s, sems):
        i = pl.program_id(0)
        bidx = i // per_seq
        first = (i % per_seq) == 0

        @pl.when(i == 0)
        def _():
            vec_ref[...] = jnp.zeros_like(vec_ref)

        l1, l2, l3 = l1_ref[...], l2_ref[...], l3_ref[...]
        mx = jnp.maximum(jnp.maximum(l1, l2), l3)
        e1, e2, e3 = jnp.exp(l1 - mx), jnp.exp(l2 - mx), jnp.exp(l3 - mx)
        esum = e1 + e2 + e3
        lse_tot = mx + jnp.log(esum)
        w1, w2, w3 = e1 / esum, e2 / esum, e3 / esum

        def per_head(wv):
            return jnp.concatenate([jnp.broadcast_to(wv[:, h:h + 1], (tm, HD)) for h in range(4)], axis=1)

        o = per_head(w1) * o1_ref[...] + per_head(w2) * o2_ref[...] + per_head(w3) * o3_ref[...]

        ga = ga_ref[...].astype(F32)
        sig_ga = _sigmoid(ga)
        silu_ga = ga * sig_ga
        a_in = (o * silu_ga).astype(BF16)
        a_out = _dot(a_in, wao_ref[...])

        u = u_ref[...].astype(F32)
        cg = cg_ref[...].astype(F32)
        z = cg * u
        zp = cp_ref[...].astype(F32) * up_ref[...].astype(F32)
        zp = jnp.where(first, 0.0, zp)
        zcat = jnp.concatenate([zp, z], axis=0)
        z1 = pltpu.roll(zcat, 1, 0)[halo:]
        z2 = pltpu.roll(zcat, 2, 0)[halo:]
        y_conv = cw_ref[0:1, :] * z2 + cw_ref[1:2, :] * z1 + cw_ref[2:3, :] * z + cb_ref[...]
        gc = gc_ref[...].astype(F32)
        sig_gc = _sigmoid(gc)
        silu_gc = gc * sig_gc
        bg = bg_ref[...].astype(F32)
        s_in = (bg * y_conv * silu_gc).astype(BF16)
        s_out = _dot(s_in, wco_ref[...])

        sa = _sigmoid(ma_ref[...].astype(F32))
        sc = _sigmoid(mc_ref[...].astype(F32))
        merged = (sa * a_out + sc * s_out).astype(BF16)
        y = _dot(merged, wo_ref[...])
        gate1 = 1.0 + mod_ref[0, 2:3, :]
        xv = x_ref[...]
        resid = ALPHA * xv + gate1 * y
        mu = jnp.mean(resid, axis=1, keepdims=True)
        xc = resid - mu
        var = jnp.mean(xc * xc, axis=1, keepdims=True)
        rstd = lax.rsqrt(var + LN_EPS)
        xhat = xc * rstd
        lg = lg_ref[...]
        err = xhat * lg + lb_ref[...] - t_ref[...]
        vec_ref[3:4, :] += (0.5 / D) * jnp.sum(err * err, axis=0, keepdims=True)

        dout = err * (1.0 / D)
        vec_ref[1:2, :] += jnp.sum(dout * xhat, axis=0, keepdims=True)
        vec_ref[2:3, :] += jnp.sum(dout, axis=0, keepdims=True)
        dxh = dout * lg
        dres = rstd * (dxh - jnp.mean(dxh, axis=1, keepdims=True)
                       - xhat * jnp.mean(dxh * xhat, axis=1, keepdims=True))
        dxd_ref[...] = ALPHA * dres
        dgate = jnp.sum(dres * y, axis=0, keepdims=True)
        vec_ref[4:5, :] += jnp.where(bidx == 0, dgate, 0.0)
        vec_ref[5:6, :] += jnp.where(bidx == 1, dgate, 0.0)
        dy = (dres * gate1).astype(BF16)

        dmerged = _dot_nt(dy, wo_ref[...])
        da_out = (dmerged * sa).astype(BF16)
        ds_out = (dmerged * sc).astype(BF16)
        dgm_s[:, 2 * D:3 * D] =(dmerged * s_out * sc * (1.0 - sc)).astype(BF16)
        dgm_s[:, D:2 * D] =(dmerged * a_out * sa * (1.0 - sa)).astype(BF16)
        da_in = _dot_nt(da_out, wao_ref[...])
        ds_in = _dot_nt(ds_out, wco_ref[...])

        d_o = da_in * silu_ga
        do_ref[...] = d_o.astype(BF16)
        dga_s[...] =(da_in * o * (sig_ga * (1.0 + ga * (1.0 - sig_ga)))).astype(BF16)
        lane = lax.broadcasted_iota(jnp.int32, (tm, 128), 1)
        stats = lse_tot
        od = o * d_o
        for h in range(4):
            delta = jnp.sum(od[:, h * HD:(h + 1) * HD], axis=1, keepdims=True)
            stats = jnp.where(lane == 4 + h, delta, stats)
        st_ref[...] = stats

        dbg_s[...] =(ds_in * y_conv * silu_gc).astype(BF16)
        dyc = ds_in * bg * silu_gc
        dyc_ref[...] = dyc
        vec_ref[0:1, :] += jnp.sum(dyc, axis=0, keepdims=True)
        dgm_s[:, 0:D] =(ds_in * bg * y_conv * (sig_gc * (1.0 + gc * (1.0 - sig_gc)))).astype(BF16)

        mg_ref[...] = merged
        dy_ref[...] = dy
        ain_ref[...] = a_in
        dao_ref[...] = da_out
        sin_ref[...] = s_in
        dso_ref[...] = ds_out
        _write_columns([(dga_s, CB * CB_GA), (dbg_s, D * KB_BG), (dgm_s, D * KB_GC)],
                       dproj_ref, pl.multiple_of(i * tm, tm), sems)

    def tile(width, cblk=0):
        return pl.BlockSpec((tm, width), lambda i: (i, cblk))

    def whole(shape):
        return pl.BlockSpec(shape, lambda i: tuple(0 for _ in shape))

    prev_rows = lambda i: (jnp.maximum(i * (tm // halo) - 1, 0),)
    in_specs = [
        tile(D), tile(D), pl.BlockSpec((1, 3, D), lambda i: (i // per_seq, 0, 0)),
        tile(GW), tile(GW), tile(GW), tile(128), tile(128), tile(128),
        tile(GW, CB_GA), tile(D, KB_U), tile(D, KB_BG), tile(D, KB_CG), tile(D, KB_GC),
        tile(D, KB_MA), tile(D, KB_MC),
        pl.BlockSpec((halo, D), lambda i: (*prev_rows(i), KB_U)),
        pl.BlockSpec((halo, D), lambda i: (*prev_rows(i), KB_CG)),
        whole((GW, D)), whole((D, D)), whole((D, D)),
        whole((3, D)), whole((1, D)), whole((1, D)), whole((1, D)),
    ]
    out_specs = (
        pl.BlockSpec(memory_space=pl.ANY), tile(D), tile(GW), tile(128), tile(D),
        tile(D), tile(D), tile(GW), tile(D), tile(D), tile(D),
        pl.BlockSpec((8, D), lambda i: (0, 0)),
    )
    out_shape = (
        jax.ShapeDtypeStruct((T, NCOL), BF16),
        jax.ShapeDtypeStruct((T, D), F32),
        jax.ShapeDtypeStruct((T, GW), BF16),
        jax.ShapeDtypeStruct((T, 128), F32),
        jax.ShapeDtypeStruct((T, D), F32),
        jax.ShapeDtypeStruct((T, D), BF16),
        jax.ShapeDtypeStruct((T, D), BF16),
        jax.ShapeDtypeStruct((T, GW), BF16),
        jax.ShapeDtypeStruct((T, D), BF16),
        jax.ShapeDtypeStruct((T, D), BF16),
        jax.ShapeDtypeStruct((T, D), BF16),
        jax.ShapeDtypeStruct((8, D), F32),
    )
    return pl.pallas_call(
        body, name="tail",
        grid=(T // tm,),
        in_specs=in_specs, out_specs=out_specs, out_shape=out_shape,
        scratch_shapes=[pltpu.VMEM((tm, GW), BF16), pltpu.VMEM((tm, D), BF16), pltpu.VMEM((tm, 3 * D), BF16),
                        pltpu.SemaphoreType.DMA((3,))],
        compiler_params=pltpu.CompilerParams(vmem_limit_bytes=VMEM_LIMIT),
    )(x2, tgt2, mod3, *o_g, *lse_g, proj, proj, proj, proj, proj, proj, proj, proj, proj,
      w_ao, w_co, w_o, conv_w, conv_b, ln_g, ln_b)


def _conv_bwd(dyc, proj, conv_w, dproj):
    tm = 512
    per_seq = S // tm
    halo = 16

    def body(d_ref, dn_ref, u_ref, c_ref, up_ref, cp_ref, cw_ref, _, dproj_ref, g_ref, du_s, dc_s, sems):
        i = pl.program_id(0)
        first = (i % per_seq) == 0
        last = (i % per_seq) == per_seq - 1

        @pl.when(i == 0)
        def _():
            g_ref[...] = jnp.zeros_like(g_ref)

        d = d_ref[...]
        dn = jnp.where(last, 0.0, dn_ref[...])
        dcat = jnp.concatenate([d, dn], axis=0)
        d1 = pltpu.roll(dcat, tm + 8 - 1, 0)[:tm]
        d2 = pltpu.roll(dcat, tm + 8 - 2, 0)[:tm]
        dz = cw_ref[2:3, :] * d + cw_ref[1:2, :] * d1 + cw_ref[0:1, :] * d2
        u = u_ref[...].astype(F32)
        cg = c_ref[...].astype(F32)
        du_s[...] = (dz * cg).astype(BF16)
        dc_s[...] = (dz * u).astype(BF16)
        _write_columns([(du_s, D * KB_U), (dc_s, D * KB_CG)], dproj_ref, pl.multiple_of(i * tm, tm), sems)

        z = cg * u
        zp = jnp.where(first, 0.0, cp_ref[...].astype(F32) * up_ref[...].astype(F32))
        zcat = jnp.concatenate([zp, z], axis=0)
        z1 = pltpu.roll(zcat, 1, 0)[halo:]
        z2 = pltpu.roll(zcat, 2, 0)[halo:]
        g_ref[0:1, :] += jnp.sum(d * z2, axis=0, keepdims=True)
        g_ref[1:2, :] += jnp.sum(d * z1, axis=0, keepdims=True)
        g_ref[2:3, :] += jnp.sum(d * z, axis=0, keepdims=True)

    n_tiles = T // tm
    prev_rows = lambda i: jnp.maximum(i * (tm // halo) - 1, 0)
    next_rows = lambda i: jnp.minimum((i + 1) * (tm // 8), T // 8 - 1)
    return pl.pallas_call(
        body, name="conv_bwd",
        grid=(n_tiles,),
        in_specs=[pl.BlockSpec((tm, D), lambda i: (i, 0)),
                  pl.BlockSpec((8, D), lambda i: (next_rows(i), 0)),
                  pl.BlockSpec((tm, D), lambda i: (i, KB_U)),
                  pl.BlockSpec((tm, D), lambda i: (i, KB_CG)),
                  pl.BlockSpec((halo, D), lambda i: (prev_rows(i), KB_U)),
                  pl.BlockSpec((halo, D), lambda i: (prev_rows(i), KB_CG)),
                  pl.BlockSpec((3, D), lambda i: (0, 0)),
                  pl.BlockSpec(memory_space=pl.ANY)],
        out_specs=(pl.BlockSpec(memory_space=pl.ANY),
                   pl.BlockSpec((8, D), lambda i: (0, 0))),
        out_shape=(jax.ShapeDtypeStruct((T, NCOL), BF16),
                   jax.ShapeDtypeStruct((8, D), F32)),
        scratch_shapes=[pltpu.VMEM((tm, D), BF16), pltpu.VMEM((tm, D), BF16), pltpu.SemaphoreType.DMA((2,))],
        input_output_aliases={7: 0},
        compiler_params=pltpu.CompilerParams(vmem_limit_bytes=VMEM_LIMIT),
    )(dyc, dyc, proj, proj, proj, proj, conv_w, dproj)


def _dh_dx(dproj, w_in_all, x2, dxd, mod3, chip_sums, hops=(), parts0=None):
    tm = 1024
    per_seq = S // tm
    n = len(chip_sums)
    n_in = 5 + n + (0 if parts0 is None else 1)

    def body(*refs):
        d_ref, w_ref, x_ref, dxd_ref, mod_ref = refs[:5]
        ins = refs[5:5 + n]
        gx_ref, vec_ref = refs[n_in:n_in + 2]
        outs = refs[n_in + 2:n_in + 2 + n]
        acc, send_sems, recv_sems, local_sems = refs[n_in + 2 + n:]
        i, jj = pl.program_id(0), pl.program_id(1)

        @pl.when((i == 0) & (jj == 0))
        def _():
            vec_ref[...] = jnp.zeros_like(vec_ref)
            if n:
                sends, _, mine = _chip_copies(ins, outs, send_sems, recv_sems, local_sems, hops)
                for cp in sends + mine:
                    cp.start()

        if n:
            @pl.when((i == T // tm - 1) & (jj == N_DEV - 1))
            def _():
                sends, arrivals, mine = _chip_copies(ins, outs, send_sems, recv_sems, local_sems, hops)
                for cp in arrivals:
                    cp.wait_recv()
                for cp in sends:
                    cp.wait_send()
                for cp in mine:
                    cp.wait()

        @pl.when(jj == 0)
        def _():
            acc[...] = jnp.zeros_like(acc)

        acc[...] += _dot_nt(d_ref[...], w_ref[...])

        @pl.when(jj == N_DEV - 1)
        def _():
            dh = acc[...]
            bidx = i // per_seq
            gx_ref[...] = dxd_ref[...] + dh * (1.0 + mod_ref[0, 1:2, :])
            dshift = jnp.sum(dh, axis=0, keepdims=True)
            dscale = jnp.sum(dh * x_ref[...], axis=0, keepdims=True)
            vec_ref[0:1, :] += jnp.where(bidx == 0, dshift, 0.0)
            vec_ref[1:2, :] += jnp.where(bidx == 1, dshift, 0.0)
            vec_ref[2:3, :] += jnp.where(bidx == 0, dscale, 0.0)
            vec_ref[3:4, :] += jnp.where(bidx == 1, dscale, 0.0)

    any_spec = pl.BlockSpec(memory_space=pl.ANY)
    res = pl.pallas_call(
        body, name="dh_dx",
        grid=(T // tm, N_DEV),
        in_specs=[
            pl.BlockSpec((tm, SHARD), lambda i, jj: (i, jj)),
            pl.BlockSpec((None, D, SHARD), lambda i, jj: (jj, 0, 0)),
            pl.BlockSpec((tm, D), lambda i, jj: (i, 0)),
            pl.BlockSpec((tm, D), lambda i, jj: (i, 0)),
            pl.BlockSpec((1, 3, D), lambda i, jj: (i // per_seq, 0, 0))] + [any_spec] * (n_in - 5),
        out_specs=(pl.BlockSpec((tm, D), lambda i, jj: (i, 0)),
                   pl.BlockSpec((8, D), lambda i, jj: (0, 0))) + (any_spec,) * n,
        out_shape=(jax.ShapeDtypeStruct((T, D), F32), jax.ShapeDtypeStruct((8, D), F32))
                  + tuple(jax.ShapeDtypeStruct(a.shape, a.dtype) for a in chip_sums),
        scratch_shapes=[pltpu.VMEM((tm, D), F32), pltpu.SemaphoreType.DMA((max(3 * n, 1),)),
                        pltpu.SemaphoreType.DMA((max(3 * n, 1),)), pltpu.SemaphoreType.DMA((max(n, 1),))],
        input_output_aliases={} if parts0 is None else {5 + n: 2},
        compiler_params=pltpu.CompilerParams(vmem_limit_bytes=VMEM_LIMIT),
    )(dproj, w_in_all, x2, dxd, mod3, *chip_sums, *([] if parts0 is None else [parts0]))
    return res[0], res[1], res[2:]


def _mm_tn(a, b, tn, blocks_leading, name):
    kk, m = a.shape
    n = b.shape[1]
    tk = 2048

    def body(a_ref, b_ref, o_ref, acc):
        @pl.when(pl.program_id(1) == 0)
        def _():
            acc[...] = jnp.zeros_like(acc)

        acc[...] += _dot_tn(a_ref[...], b_ref[...])

        @pl.when(pl.program_id(1) == kk // tk - 1)
        def _():
            o_ref[...] = acc[...].astype(BF16)

    if blocks_leading:
        out_spec = pl.BlockSpec((None, m, tn), lambda j, k: (j, 0, 0))
        out_shape = jax.ShapeDtypeStruct((n // tn, m, tn), BF16)
    else:
        out_spec = pl.BlockSpec((m, tn), lambda j, k: (0, j))
        out_shape = jax.ShapeDtypeStruct((m, n), BF16)
    return pl.pallas_call(
        body, name=name,
        grid=(n // tn, kk // tk),
        in_specs=[pl.BlockSpec((tk, m), lambda j, k: (k, 0)),
                  pl.BlockSpec((tk, tn), lambda j, k: (k, j))],
        out_specs=out_spec, out_shape=out_shape,
        scratch_shapes=[pltpu.VMEM((m, tn), F32)],
        compiler_params=pltpu.CompilerParams(vmem_limit_bytes=VMEM_LIMIT),
    )(a, b)


def _adamw(parts, w, m, v, name, row_tile=None):
    n_parts, rows, cols = parts.shape
    tr = rows if row_tile is None else row_tile
    c1 = 1.0 - ADAM_B1 ** ADAM_STEP
    c2 = 1.0 - ADAM_B2 ** ADAM_STEP

    def body(p_ref, w_ref, m_ref, v_ref, g_ref, d_ref, nm_ref, nv_ref):
        g = p_ref[0].astype(F32)
        for s in range(1, n_parts):
            g = g + p_ref[s].astype(F32)
        nm = ADAM_B1 * m_ref[...] + (1.0 - ADAM_B1) * g
        nv = ADAM_B2 * v_ref[...] + (1.0 - ADAM_B2) * (g * g)
        m_hat = nm / c1
        v_hat = nv / c2
        g_ref[...] = g
        d_ref[...] = -ADAM_LR * (m_hat / (jnp.sqrt(v_hat) + ADAM_EPS) + ADAM_WD * w_ref[...])
        nm_ref[...] = nm
        nv_ref[...] = nv

    blk = pl.BlockSpec((tr, cols), lambda i: (i, 0))
    shp = jax.ShapeDtypeStruct((rows, cols), F32)
    return pl.pallas_call(
        body, name=name,
        grid=(rows // tr,),
        in_specs=[pl.BlockSpec((n_parts, tr, cols), lambda i: (0, i, 0)), blk, blk, blk],
        out_specs=(blk, blk, blk, blk),
        out_shape=(shp, shp, shp, shp),
        compiler_params=pltpu.CompilerParams(vmem_limit_bytes=VMEM_LIMIT),
    )(parts, w, m, v)


def _loss_sum(rows):
    def body(r_ref, o_ref):
        o_ref[...] = jnp.sum(jnp.sum(r_ref[...], axis=0, keepdims=True), axis=1, keepdims=True)

    return pl.pallas_call(body, name="loss_sum", out_shape=jax.ShapeDtypeStruct((1, 1), F32))(rows)


def _attention_forward(proj, rel_bias, rides=((), (), ())):
    buckets_np, masks_np = _bucket_maps()
    buckets, masks = jnp.asarray(buckets_np), jnp.asarray(masks_np)
    bias = _bias_expand(rel_bias, buckets, masks)
    fwd = [_attn_fwd(proj, bias, g, rides[g]) for g in range(3)]
    brought = [arr for f in fwd for arr in f[2:]]
    return bias, buckets, [f[0] for f in fwd], [f[1] for f in fwd], brought


def _local_step(x2, tgt2, mod3, h, proj, attn, w_ao, w_co, w_o, conv_w, conv_b, ln_g, ln_b):
    bias, buckets, o_g, lse_g = attn

    (dproj, dyc, d_o, stats, dxd, merged, dy, a_in, da_out, s_in, ds_out, tail_vec) = _tail(
        x2, tgt2, mod3, o_g, lse_g, proj, w_ao, w_co, w_o, conv_w, conv_b, ln_g, ln_b)

    dbias = []
    for g in range(3):
        dproj, db = _attn_bwd(proj, d_o, stats, bias, dproj, g)
        dbias.append(db)
    g_rel_bias = _bias_grad(*dbias, buckets)
    dproj, conv_vec = _conv_bwd(dyc, proj, conv_w, dproj)

    gw_o = _mm_tn(merged, dy, D, False, "gw_o")
    gw_co = _mm_tn(s_in, ds_out, D, False, "gw_conv_out")
    gw_ao = _mm_tn(a_in, da_out, D, False, "gw_attn_out")
    gw_ao = jnp.transpose(gw_ao.reshape(GW, N_DEV, D // N_DEV), (1, 0, 2))
    return dproj, dxd, gw_ao, gw_co, gw_o, conv_vec, g_rel_bias, tail_vec


def kernel(x, c, w_ada, b_ada, w_in, conv_w, conv_b, rel_bias, w_attn_out, w_conv_out, w_o, ln_g, ln_b, loss_target, m_w_ada, m_b_ada, m_w_in, m_conv_w, m_conv_b, m_rel_bias, m_w_attn_out, m_w_conv_out, m_w_o, m_ln_g, m_ln_b, v_w_ada, v_b_ada, v_w_in, v_conv_w, v_conv_b, v_rel_bias, v_w_attn_out, v_w_conv_out, v_w_o, v_ln_g, v_ln_b):
    me = _my_index()
    x2 = x.reshape(T, D)
    tgt2 = loss_target.reshape(T, D)

    b_cols = lax.dynamic_slice(b_ada, (0, me * ADA_SHARD), (1, ADA_SHARD))
    c_g, mod_in = _mod_exchange(jnp.pad(c, ((0, 8 - BL), (0, 0))), w_ada[0], b_cols)
    c_all = c_g[:, 0:BL, :].reshape(N_DEV * BL, D)
    mod3 = jnp.transpose(mod_in[:, 0:BL, :], (1, 0, 2)).reshape(BL, 3, D)

    h = _prep_h(x2, mod3)
    proj, w_in_all = _gather_proj(_shard_order(), h, w_in[0].astype(BF16), 1024)

    rows_shape = jax.ShapeDtypeStruct((N_DEV, D // N_DEV, D), BF16)
    *attn, (w_co_g, w_o_g, w_ao_g, conv_w_g) = _attention_forward(proj, rel_bias, (
        ([w_conv_out[0].astype(BF16)], [rows_shape]),
        ([w_o[0].astype(BF16)], [rows_shape]),
        ([w_attn_out[0].astype(BF16), conv_w[0]],
         [jax.ShapeDtypeStruct((N_DEV, GW, D // N_DEV), BF16), jax.ShapeDtypeStruct((N_DEV, 3, D // N_DEV), F32)])))
    w_ao_full = jnp.transpose(w_ao_g, (1, 0, 2)).reshape(GW, D)
    w_co_full = w_co_g.reshape(D, D)
    w_o_full = w_o_g.reshape(D, D)
    conv_w_full = jnp.transpose(conv_w_g, (1, 0, 2)).reshape(3, D)

    (dproj, dxd, gw_ao, gw_co, gw_o, conv_vec, g_rel_bias, tail_vec) = _local_step(
        x2, tgt2, mod3, h, proj, attn, w_ao_full, w_co_full, w_o_full,
        conv_w_full, conv_b, ln_g, ln_b)

    g_conv_w_blocks = jnp.transpose(conv_vec[0:3].reshape(3, N_DEV, D // N_DEV), (1, 0, 2))
    partials = [gw_ao, gw_co.reshape(N_DEV, D // N_DEV, D), gw_o.reshape(N_DEV, D // N_DEV, D), g_conv_w_blocks]
    w_in_sums, w_in_parts, sib = _gw_in_pair(
        _slice_order(), h, dproj, partials,
        [jax.ShapeDtypeStruct((4, GW, D // N_DEV), BF16),
         jax.ShapeDtypeStruct((4, D // N_DEV, D), BF16),
         jax.ShapeDtypeStruct((4, D // N_DEV, D), BF16),
         jax.ShapeDtypeStruct((4, 3, D // N_DEV), F32)])
    names = ["w_attn_out", "w_conv_out", "w_o", "conv_w"]
    core = lax.axis_index("c").astype(jnp.int32).reshape(1)
    chip_sums = [w_in_sums] + [_pair_add(core, partials[a], sib[a], None, "pair_add_" + names[a])
                               for a in range(4)]
    hops = [(3,)] + [(1, 2, 3)] * 4
    grad_x, mod_vec, (r_in, r_ao, r_co, r_o, r_cw) = _dh_dx(
        dproj, w_in_all, x2, dxd, mod3, chip_sums, hops, w_in_parts)

    small = jnp.concatenate([
        tail_vec[0:4],
        jnp.pad(g_rel_bias.reshape(1, N_BUCKETS * N_HEADS), ((0, 0), (0, D - N_BUCKETS * N_HEADS))),
        jnp.zeros((3, D), F32)], axis=0)
    dmod = jnp.concatenate([mod_vec[0:2], mod_vec[2:4], tail_vec[4:6]], axis=1)
    small_g, dmod_g = _all_gather(
        [small, dmod],
        [jax.ShapeDtypeStruct((N_DEV, 8, D), F32), jax.ShapeDtypeStruct((N_DEV, BL, 3 * D), F32)],
        "gather_small")
    dmod_all = dmod_g.reshape(N_DEV * BL, 3 * D)
    loss = _loss_sum(small_g[:, 3, :]).reshape(())
    g_w_ada = _ada_bwd(jnp.transpose(c_all), lax.dynamic_slice(dmod_all, (0, me * ADA_SHARD),
                                                               (N_DEV * BL, ADA_SHARD)))

    def upd(parts, w, m, v, name, row_tile=None):
        shape = w.shape
        w2, m2, v2 = (t.reshape(parts.shape[1:]) for t in (w, m, v))
        return tuple(t.reshape(shape) for t in _adamw(parts, w2, m2, v2, name, row_tile))

    res = {
        "w_ada": upd(g_w_ada[None], w_ada, m_w_ada, v_w_ada, "adam_w_ada", 256),
        "b_ada": upd(dmod_all[:, None, :], b_ada, m_b_ada, v_b_ada, "adam_b_ada"),
        "w_in": upd(r_in, w_in, m_w_in, v_w_in, "adam_w_in", 128),
        "conv_w": upd(r_cw, conv_w, m_conv_w, v_conv_w, "adam_conv_w"),
        "conv_b": upd(small_g[:, 0:1, :], conv_b, m_conv_b, v_conv_b, "adam_conv_b"),
        "rel_bias": upd(small_g[:, 4, :N_BUCKETS * N_HEADS].reshape(N_DEV, N_BUCKETS, N_HEADS),
                        rel_bias, m_rel_bias, v_rel_bias, "adam_rel_bias"),
        "w_attn_out": upd(r_ao, w_attn_out, m_w_attn_out, v_w_attn_out, "adam_w_attn_out"),
        "w_conv_out": upd(r_co, w_conv_out, m_w_conv_out, v_w_conv_out, "adam_w_conv_out"),
        "w_o": upd(r_o, w_o, m_w_o, v_w_o, "adam_w_o"),
        "ln_g": upd(small_g[:, 1:2, :], ln_g, m_ln_g, v_ln_g, "adam_ln_g"),
        "ln_b": upd(small_g[:, 2:3, :], ln_b, m_ln_b, v_ln_b, "adam_ln_b"),
    }
    order = ["w_ada", "b_ada", "w_in", "conv_w", "conv_b", "rel_bias", "w_attn_out", "w_conv_out",
             "w_o", "ln_g", "ln_b"]
    outs = [loss, grad_x.reshape(BL, S, D)]
    for k in range(4):
        outs += [res[name][k] for name in order]
    return tuple(outs)
```

```python
import functools
import math

import numpy as np
import jax
import jax.numpy as jnp
from jax import lax
from jax.experimental import pallas as pl
from jax.experimental.pallas import tpu as pltpu

F32 = jnp.float32
BF16 = jnp.bfloat16
MESH = pl.DeviceIdType.MESH

N_DEV = 8
D = 1024
S = 2048
BL = 2
T = BL * S
NCOL = 11264
SHARD = NCOL // N_DEV
CB = 512
NCB = NCOL // CB
HD = 128
GW = 512
QB = 128
DILATIONS = (1, 4, 16)
N_STEPS = 128
N_BUCKETS = 32
N_HEADS = 12
ALPHA = 2.0 ** 0.25
LN_EPS = 1e-5
NEG_INF = -1e30
SCALE = HD ** -0.5
ADA_SHARD = 3 * D // N_DEV

CB_Q, CB_K, CB_V, CB_GA = 0, 3, 6, 9
KB_U, KB_BG, KB_CG, KB_GC, KB_MA, KB_MC = 5, 6, 7, 8, 9, 10

ADAM_LR, ADAM_B1, ADAM_B2, ADAM_EPS, ADAM_WD, ADAM_STEP = 0.001, 0.9, 0.999, 1e-08, 0.01, 10

VMEM_LIMIT = 56 * 1024 * 1024


def _dot(a, b):
    return jnp.dot(a, b, preferred_element_type=F32)


def _dot_nt(a, b):
    return lax.dot_general(a, b, (((1,), (1,)), ((), ())), preferred_element_type=F32)


def _dot_tn(a, b):
    return lax.dot_general(a, b, (((0,), (0,)), ((), ())), preferred_element_type=F32)


def _sigmoid(v):
    return 1.0 / (1.0 + jnp.exp(-v))


def _write_columns(pieces, dst_hbm, row0, sems):
    copies = []
    for k, (src, col0) in enumerate(pieces):
        rows, width = src.shape
        copies.append(pltpu.make_async_copy(
            src, dst_hbm.at[pl.ds(row0, rows), pl.ds(col0, width)], sems.at[k]))
    for cp in copies:
        cp.start()
    for cp in copies:
        cp.wait()


def _my_index():
    return 4 * lax.axis_index("x") + 2 * lax.axis_index("y") + lax.axis_index("c")


class _Gather:
    def __init__(self, ins, outs, stage, send_sems, recv_sems, local_sems):
        self.ins, self.outs, self.stage = ins, outs, stage
        self.send_sems, self.recv_sems, self.local_sems = send_sems, recv_sems, local_sems
        x, y, c = lax.axis_index("x"), lax.axis_index("y"), lax.axis_index("c")
        self.c = c
        self.me, self.sibling = (x, y, c), (x, y, 1 - c)
        self.chips = [(1 - x, y), (x, 1 - y), (1 - x, 1 - y)]

    @staticmethod
    def scratch(arrs):
        n = len(arrs)
        return ([pltpu.SemaphoreType.DMA((7 * n,)), pltpu.SemaphoreType.DMA((7 * n,)),
                 pltpu.SemaphoreType.DMA((n,))] + [pltpu.VMEM(a.shape, a.dtype) for a in arrs])

    def _copy(self, a, k, block, to, src=None):
        dst = self.outs[a].at[4 * block[0] + 2 * block[1] + block[2]]
        return pltpu.make_async_remote_copy(
            src_ref=dst if src is None else src, dst_ref=dst,
            send_sem=self.send_sems.at[a * 7 + k], recv_sem=self.recv_sems.at[a * 7 + k],
            device_id=to, device_id_type=MESH)

    def _first(self):
        first = []
        for a in range(len(self.ins)):
            first.append(self._copy(a, 0, self.me, self.sibling, src=self.ins[a]))
            first += [self._copy(a, 1 + j, self.me, (*chip, self.c), src=self.ins[a])
                      for j, chip in enumerate(self.chips)]
        return first

    def _mine(self):
        me = self.me
        return [pltpu.make_async_copy(self.stage[a], self.outs[a].at[4 * me[0] + 2 * me[1] + me[2]],
                                      self.local_sems.at[a]) for a in range(len(self.ins))]

    def begin(self):
        for cp in self._first():
            cp.start()
        loads = [pltpu.make_async_copy(self.ins[a], self.stage[a], self.local_sems.at[a])
                 for a in range(len(self.ins))]
        for cp in loads:
            cp.start()
        for cp in loads:
            cp.wait()
        for cp in self._mine():
            cp.start()

    def finish(self):
        n, c, me, sibling = len(self.ins), self.c, self.me, self.sibling
        passed = []
        for j, chip in enumerate(self.chips):
            for a in range(n):
                self._copy(a, 1 + j, (*chip, c), me).wait_recv()
                fwd = self._copy(a, 4 + j, (*chip, c), sibling)
                fwd.start()
                passed.append(fwd)
        for a in range(n):
            self._copy(a, 0, sibling, me).wait_recv()
        for j, chip in enumerate(self.chips):
            for a in range(n):
                self._copy(a, 4 + j, (*chip, 1 - c), me).wait_recv()
        for cp in self._first() + passed:
            cp.wait_send()
        for cp in self._mine():
            cp.wait()


def _all_gather(arrs, out_shapes, name):
    n = len(arrs)

    def body(*refs):
        g = _Gather(refs[:n], refs[n:2 * n], refs[2 * n + 3:], *refs[2 * n:2 * n + 3])
        g.begin()
        g.finish()

    any_spec = pl.BlockSpec(memory_space=pl.ANY)
    return pl.pallas_call(
        body, name=name,
        out_shape=tuple(out_shapes),
        in_specs=[any_spec] * n,
        out_specs=tuple([any_spec] * n),
        scratch_shapes=_Gather.scratch(arrs),
    )(*arrs)


def _slice_order():
    x, y, c = lax.axis_index("x"), lax.axis_index("y"), lax.axis_index("c")
    slots = []
    for q in (2 * (1 - x) + y, 2 * x + (1 - y), 2 * (1 - x) + (1 - y), 2 * x + y):
        slots += [2 * q + 1 - c, 2 * q + c]
    return jnp.stack(slots).astype(jnp.int32)


def _gw_in_pair(order, h, dproj, smalls, small_shapes4):
    kk, m = h.shape
    tk = min(kk, 2048)
    nk = kk // tk
    ncols = dproj.shape[1] // N_DEV
    n = len(smalls)

    def body(order_ref, h_ref, d_ref, *rest):
        ins = rest[:n]
        sums_hbm, parts_hbm = rest[n], rest[n + 1]
        sib = rest[n + 2:2 * n + 2]
        (acc, sendbuf, recvbuf, sumbuf, send_sems, recv_sems, local_sem, ssend, srecv,
         isend, irecv) = rest[2 * n + 2:]
        js, k = pl.program_id(0), pl.program_id(1)
        x, y, c = lax.axis_index("x"), lax.axis_index("y"), lax.axis_index("c")
        sibling = (x, y, 1 - c)
        my_chip = 2 * x + y
        near = [(1 - x, y, c), (x, 1 - y, c)]

        def ici_copy(p, out_chip):
            peer = near[p]
            return pltpu.make_async_remote_copy(
                src_ref=sumbuf.at[p], dst_ref=parts_hbm.at[out_chip],
                send_sem=isend.at[p], recv_sem=irecv.at[p], device_id=peer, device_id_type=MESH)

        def small_copies():
            return [pltpu.make_async_remote_copy(
                        src_ref=ins[a].at[2 * q + 1 - c], dst_ref=sib[a].at[q],
                        send_sem=ssend.at[a * 4 + q], recv_sem=srecv.at[a * 4 + q],
                        device_id=sibling, device_id_type=MESH)
                    for a in range(n) for q in range(4)]

        def slice_copy(p):
            return pltpu.make_async_remote_copy(
                src_ref=sendbuf, dst_ref=recvbuf.at[p], send_sem=send_sems.at[p], recv_sem=recv_sems.at[p],
                device_id=sibling, device_id_type=MESH)

        def sum_copy(p):
            return pltpu.make_async_copy(sumbuf.at[2], sums_hbm.at[order_ref[2 * p] // 2], local_sem)

        @pl.when((js == 0) & (k == 0))
        def _():
            for cp in small_copies():
                cp.start()

        @pl.when(k == 0)
        def _():
            acc[...] = jnp.zeros_like(acc)

        acc[...] += _dot_tn(h_ref[...], d_ref[...])

        for p in range(4):
            @pl.when((js == 2 * p) & (k == nk - 1))
            def _():
                if p > 0:
                    slice_copy(p - 1).wait_send()
                sendbuf[...] = acc[...].astype(BF16)
                slice_copy(p).start()

            @pl.when((js == 2 * p + 1) & (k == nk - 1))
            def _():
                slice_copy(p).wait_recv()
                if p == 3:
                    sum_copy(2).wait()
                sumbuf[min(p, 2)] = (acc[...] + recvbuf[p].astype(F32)).astype(BF16)
                if p < 2:
                    ici_copy(p, my_chip).start()
                else:
                    sum_copy(p).start()

        @pl.when((js == N_DEV - 1) & (k == nk - 1))
        def _():
            slice_copy(3).wait_send()
            sum_copy(3).wait()
            for cp in small_copies():
                cp.wait()
            for p in range(2):
                ici_copy(p, 2 * near[p][0] + near[p][1]).wait_recv()
                ici_copy(p, my_chip).wait_send()

    any_spec = pl.BlockSpec(memory_space=pl.ANY)
    res = pl.pallas_call(
        body, name="gw_in_pair",
        grid_spec=pltpu.PrefetchScalarGridSpec(
            num_scalar_prefetch=1,
            grid=(N_DEV, nk),
            in_specs=[pl.BlockSpec((tk, m), lambda js, k, order_ref: (k, 0)),
                      pl.BlockSpec((tk, ncols), lambda js, k, order_ref: (k, order_ref[js]))] + [any_spec] * n,
            out_specs=(any_spec,) * (n + 2),
            scratch_shapes=[pltpu.VMEM((m, ncols), F32), pltpu.VMEM((m, ncols), BF16),
                            pltpu.VMEM((4, m, ncols), BF16), pltpu.VMEM((3, m, ncols), BF16),
                            pltpu.SemaphoreType.DMA((4,)), pltpu.SemaphoreType.DMA((4,)),
                            pltpu.SemaphoreType.DMA,
                            pltpu.SemaphoreType.DMA((4 * n,)), pltpu.SemaphoreType.DMA((4 * n,)),
                            pltpu.SemaphoreType.DMA((2,)), pltpu.SemaphoreType.DMA((2,))]),
        out_shape=(jax.ShapeDtypeStruct((4, m, ncols), BF16),) * 2 + tuple(small_shapes4),
        compiler_params=pltpu.CompilerParams(vmem_limit_bytes=VMEM_LIMIT),
    )(order, h, dproj, *smalls)
    return res[0], res[1], res[2:]


def _chip_copies(ins, outs, send_sems, recv_sems, local_sems, hops):
    n = len(ins)
    x, y, c = lax.axis_index("x"), lax.axis_index("y"), lax.axis_index("c")
    my_chip = 2 * x + y

    def peer_of(k):
        return ((1 - x) if (k >> 1) & 1 else x, (1 - y) if k & 1 else y, c)

    def copy(a, k, out_chip):
        peer = peer_of(k)
        return pltpu.make_async_remote_copy(
            src_ref=ins[a].at[2 * peer[0] + peer[1]], dst_ref=outs[a].at[out_chip],
            send_sem=send_sems.at[a * 3 + k - 1], recv_sem=recv_sems.at[a * 3 + k - 1],
            device_id=peer, device_id_type=MESH)

    sends = [copy(a, k, my_chip) for k in range(1, 4) for a in range(n) if k in hops[a]]
    arrivals = []
    for k in range(1, 4):
        peer = peer_of(k)
        arrivals += [copy(a, k, 2 * peer[0] + peer[1]) for a in range(n) if k in hops[a]]
    mine = [pltpu.make_async_copy(ins[a].at[my_chip], outs[a].at[my_chip], local_sems.at[a])
            for a in range(n)]
    return sends, arrivals, mine


def _pair_add(core, mine, theirs, row_tile, name):
    _, rows, cols = theirs.shape
    tr = rows if row_tile is None else row_tile

    def body(core_ref, a_ref, b_ref, o_ref):
        o_ref[...] = (a_ref[...].astype(F32) + b_ref[...].astype(F32)).astype(o_ref.dtype)

    blk = pl.BlockSpec((None, tr, cols), lambda q, i, core_ref: (q, i, 0))
    return pl.pallas_call(
        body, name=name,
        grid_spec=pltpu.PrefetchScalarGridSpec(
            num_scalar_prefetch=1,
            grid=(4, rows // tr),
            in_specs=[pl.BlockSpec((None, tr, cols), lambda q, i, core_ref: (2 * q + core_ref[0], i, 0)), blk],
            out_specs=blk),
        out_shape=jax.ShapeDtypeStruct(theirs.shape, theirs.dtype),
    )(core, mine, theirs)


def _mod_exchange(c8, w_ada, b_cols):
    cols = w_ada.shape[1]

    def body(c_ref, w_ref, b_ref, call_ref, mod_ref, msend, send1, recv1, send2, recv2):
        x, y, c = lax.axis_index("x"), lax.axis_index("y"), lax.axis_index("c")
        my_slot = 4 * x + 2 * y + c

        def peer_of(k):
            return ((1 - x) if (k >> 2) & 1 else x, (1 - y) if (k >> 1) & 1 else y, (1 - c) if k & 1 else c)

        def slot_of(dev):
            return 4 * dev[0] + 2 * dev[1] + dev[2]

        def exchange(src_of, dst_ref, send_sems, recv_sems):
            sends, arrivals = [], []
            for k in range(1, 8):
                peer = peer_of(k)
                sends.append(pltpu.make_async_remote_copy(
                    src_ref=src_of(slot_of(peer)), dst_ref=dst_ref.at[my_slot],
                    send_sem=send_sems.at[k - 1], recv_sem=recv_sems.at[k - 1],
                    device_id=peer, device_id_type=MESH))
                arrivals.append(pltpu.make_async_remote_copy(
                    src_ref=src_of(my_slot), dst_ref=dst_ref.at[slot_of(peer)],
                    send_sem=send_sems.at[k - 1], recv_sem=recv_sems.at[k - 1],
                    device_id=peer, device_id_type=MESH))
            for cp in sends:
                cp.start()
            for cp in arrivals:
                cp.wait_recv()
            for cp in sends:
                cp.wait_send()

        call_ref[my_slot] = c_ref[...]
        exchange(lambda s: c_ref, call_ref, send1, recv1)
        cv = call_ref[...].reshape(N_DEV * 8, c_ref.shape[1])
        act = cv * _sigmoid(cv)
        mod = jnp.dot(act, w_ref[...], preferred_element_type=F32,
                      precision=lax.Precision.HIGHEST) + b_ref[...]
        msend[...] = mod.reshape(N_DEV, 8, cols)
        mod_ref[my_slot] = msend[my_slot]
        exchange(lambda s: msend.at[s], mod_ref, send2, recv2)

    return pl.pallas_call(
        body, name="mod_exchange",
        out_shape=(jax.ShapeDtypeStruct((N_DEV, 8, c8.shape[1]), F32),
                   jax.ShapeDtypeStruct((N_DEV, 8, cols), F32)),
        scratch_shapes=[pltpu.VMEM((N_DEV, 8, cols), F32)] + [pltpu.SemaphoreType.DMA((7,))] * 4,
    )(c8, w_ada, b_cols)


def _ada_bwd(c_all_t, dmod_cols):
    def body(c_ref, d_ref, o_ref):
        cv = c_ref[...]
        sc = cv * _sigmoid(cv)
        o_ref[...] = jnp.dot(sc, d_ref[...], preferred_element_type=F32,
                             precision=lax.Precision.HIGHEST)

    return pl.pallas_call(
        body, name="ada_bwd",
        out_shape=jax.ShapeDtypeStruct((c_all_t.shape[0], dmod_cols.shape[1]), F32),
    )(c_all_t, dmod_cols)


def _prep_h(x2, mod3):
    ts = 512
    per_seq = S // ts

    def body(x_ref, mod_ref, h_ref):
        shift = mod_ref[0, 0:1, :]
        scale = mod_ref[0, 1:2, :]
        h_ref[...] = (x_ref[...] * (1.0 + scale) + shift).astype(BF16)

    return pl.pallas_call(
        body, name="prep_h",
        grid=(T // ts,),
        in_specs=[pl.BlockSpec((ts, D), lambda i: (i, 0)),
                  pl.BlockSpec((1, 3, D), lambda i: (i // per_seq, 0, 0))],
        out_specs=pl.BlockSpec((ts, D), lambda i: (i, 0)),
        out_shape=jax.ShapeDtypeStruct((T, D), BF16),
    )(x2, mod3)


def _shard_order():
    x, y, c = lax.axis_index("x"), lax.axis_index("y"), lax.axis_index("c")
    devs = [(x, y, c), (x, y, 1 - c)]
    for chip in [(1 - x, y), (x, 1 - y), (1 - x, 1 - y)]:
        devs += [(*chip, c), (*chip, 1 - c)]
    return jnp.stack([4 * d[0] + 2 * d[1] + d[2] for d in devs]).astype(jnp.int32)


def _gather_proj(order, h, w_shard, tm, ride=()):
    rows, kdim = h.shape
    ncols = w_shard.shape[1]
    n_i = rows // tm
    ride_arrs, ride_shapes = ride if ride else ((), ())
    n_ride = len(ride_arrs)

    def body(order_ref, h_ref, mine_hbm, *rest):
        ride_ins = rest[:n_ride]
        o_ref, all_hbm = rest[n_ride:n_ride + 2]
        ride_outs = rest[n_ride + 2:2 * n_ride + 2]
        wv, send_sems, recv_sems, local_sems = rest[2 * n_ride + 2:2 * n_ride + 6]
        ride_scr = rest[2 * n_ride + 6:]
        j, i = pl.program_id(0), pl.program_id(1)
        x, y, c = lax.axis_index("x"), lax.axis_index("y"), lax.axis_index("c")
        me, sibling = (x, y, c), (x, y, 1 - c)
        chips = [(1 - x, y), (x, 1 - y), (1 - x, 1 - y)]

        def slot(dev):
            return 4 * dev[0] + 2 * dev[1] + dev[2]

        def copy(k, block, to):
            return pltpu.make_async_remote_copy(
                src_ref=wv.at[slot(block)], dst_ref=wv.at[slot(block)],
                send_sem=send_sems.at[k], recv_sem=recv_sems.at[k],
                device_id=to, device_id_type=MESH)

        def keep(step, block):
            return pltpu.make_async_copy(wv.at[slot(block)], all_hbm.at[slot(block)], local_sems.at[step])

        if n_ride:
            gather = _Gather(ride_ins, ride_outs, ride_scr[3:], *ride_scr[:3])
        first = [copy(0, me, sibling)] + [copy(1 + q, me, (*chip, c)) for q, chip in enumerate(chips)]
        passed = [copy(4 + q, (*chip, c), sibling) for q, chip in enumerate(chips)]
        due = [(me, None, None), (sibling, copy(0, sibling, me), None)]
        for q, chip in enumerate(chips):
            due.append(((*chip, c), copy(1 + q, (*chip, c), me), passed[q]))
            due.append(((*chip, 1 - c), copy(4 + q, (*chip, 1 - c), me), None))

        @pl.when((j == 0) & (i == 0))
        def _():
            load = pltpu.make_async_copy(mine_hbm, wv.at[slot(me)], local_sems.at[N_DEV])
            load.start()
            load.wait()
            for cp in first:
                cp.start()
            keep(0, me).start()

        for step in range(1, N_DEV):
            block, arrival, forward = due[step]

            @pl.when((j == step) & (i == 0))
            def _():
                arrival.wait_recv()
                if forward is not None:
                    forward.start()
                keep(step, block).start()
                if n_ride and step == N_DEV - 2:
                    gather.begin()

        o_ref[...] = _dot(h_ref[...], wv[order_ref[j]]).astype(BF16)

        @pl.when((j == N_DEV - 1) & (i == n_i - 1))
        def _():
            for cp in first + passed:
                cp.wait_send()
            for step in range(N_DEV):
                keep(step, due[step][0]).wait()
            if n_ride:
                gather.finish()

    any_spec = pl.BlockSpec(memory_space=pl.ANY)
    res = pl.pallas_call(
        body, name="gather_proj",
        grid_spec=pltpu.PrefetchScalarGridSpec(
            num_scalar_prefetch=1,
            grid=(N_DEV, n_i),
            in_specs=[pl.BlockSpec((tm, kdim), lambda j, i, order_ref: (i, 0)), any_spec] + [any_spec] * n_ride,
            out_specs=(pl.BlockSpec((tm, ncols), lambda j, i, order_ref: (i, order_ref[j])), any_spec)
                      + (any_spec,) * n_ride,
            scratch_shapes=[pltpu.VMEM((N_DEV, kdim, ncols), BF16),
                            pltpu.SemaphoreType.DMA((7,)), pltpu.SemaphoreType.DMA((7,)),
                            pltpu.SemaphoreType.DMA((N_DEV + 1,))]
                           + (_Gather.scratch(ride_arrs) if n_ride else [])),
        out_shape=(jax.ShapeDtypeStruct((rows, N_DEV * ncols), BF16),
                   jax.ShapeDtypeStruct((N_DEV, kdim, ncols), BF16)) + tuple(ride_shapes),
        compiler_params=pltpu.CompilerParams(vmem_limit_bytes=VMEM_LIMIT),
    )(order, h, w_shard, *ride_arrs)
    return res[0], res[1], res[2:]


def _bucket_maps():
    a = np.arange(QB)[:, None]
    b = np.arange(2 * QB)[None, :]
    steps = a + QB - b
    maps = []
    for dil in DILATIONS:
        dist = np.maximum(steps, 0) * dil
        nf = np.maximum(dist, 1).astype(np.float32)
        large = 16 + (np.log(nf / np.float32(16)) / np.float32(math.log(128.0))
                      * np.float32(16)).astype(np.int32)
        large = np.minimum(large, N_BUCKETS - 1)
        maps.append(np.where(dist < 16, dist, large).astype(np.int32))
    band = (steps >= 0) & (steps <= N_STEPS)
    first = band & (b >= QB)
    masks = np.stack([first, band]).astype(np.int32)
    return np.stack(maps), masks


def _bias_expand(rel_bias, buckets, masks):
    def body(tab_ref, bk_ref, mk_ref, o_ref):
        for g in range(3):
            bk = bk_ref[g]
            for h in range(4):
                col = 4 * g + h
                val = jnp.zeros((QB, 2 * QB), F32)
                for k in range(N_BUCKETS):
                    val = jnp.where(bk == k, tab_ref[k, col], val)
                o_ref[g, 0, h] = jnp.where(mk_ref[0] != 0, val, NEG_INF)
                o_ref[g, 1, h] = jnp.where(mk_ref[1] != 0, val, NEG_INF)

    return pl.pallas_call(
        body, name="bias_expand",
        in_specs=[pl.BlockSpec(memory_space=pltpu.SMEM),
                  pl.BlockSpec(memory_space=pltpu.VMEM),
                  pl.BlockSpec(memory_space=pltpu.VMEM)],
        out_shape=jax.ShapeDtypeStruct((3, 2, 4, QB, 2 * QB), F32),
    )(rel_bias, buckets, masks)


def _bias_grad(ds1, ds2, ds3, buckets):
    def body(d1_ref, d2_ref, d3_ref, bk_ref, o_ref):
        for g, d_ref in enumerate((d1_ref, d2_ref, d3_ref)):
            bk = bk_ref[g]
            for h in range(4):
                dv = d_ref[h]
                for k in range(N_BUCKETS):
                    o_ref[k, 4 * g + h] = jnp.sum(jnp.where(bk == k, dv, 0.0))

    return pl.pallas_call(
        body, name="bias_grad",
        in_specs=[pl.BlockSpec(memory_space=pltpu.VMEM)] * 4,
        out_specs=pl.BlockSpec(memory_space=pltpu.SMEM),
        out_shape=jax.ShapeDtypeStruct((N_BUCKETS, N_HEADS), F32),
    )(ds1, ds2, ds3, buckets)


def _scratch_sets(rows):
    return 4 if rows <= 512 else 1


def _unit_chunks(dil, size=16):
    units = [(h, r) for h in range(4) for r in range(dil)]
    return [units[i:i + size] for i in range(0, len(units), size)]


def _residue_rows(src_ref, copies, h, residue):
    sl = slice(h * HD, (h + 1) * HD)
    if copies is None:
        return lambda r: src_ref[:, sl]
    buf = copies[h % len(copies)]
    buf[...] = src_ref[:, sl].astype(F32)
    return lambda r: buf[residue(r), :].astype(BF16)


def _attn_fwd(proj, bias, g, ride=()):
    dil = DILATIONS[g]
    rows = QB * dil
    nsb = S // rows
    has_prev = nsb > 1

    def residue(r):
        return pl.ds(r, QB, stride=dil) if dil > 1 else pl.ds(0, QB)

    strided = dil > 1
    n_sets = _scratch_sets(rows)
    n_attn = 6 if has_prev else 4
    ride_arrs, ride_shapes = ride if ride else ((), ())
    n_ride = len(ride_arrs)
    n_in = n_attn + n_ride
    n_copied = (4 + (2 if has_prev else 0)) * (n_sets if strided else 0)

    def body(*refs):
        q_ref, kc_ref, vc_ref = refs[:3]
        kp_ref, vp_ref = refs[3:5] if has_prev else (None, None)
        b_ref = refs[n_attn - 1]
        o_ref, l_ref = refs[n_in:n_in + 2]
        scr = list(refs[n_in + 2 + n_ride:])
        ls = [scr.pop(0) for _ in range(4)]
        copies = {name: [scr.pop(0) for _ in range(n_sets)] if strided else None
                  for name in ("q", "kc", "vc", "o") + (("kp", "vp") if has_prev else ())}
        if n_ride:
            gather = _Gather(refs[n_attn:n_in], refs[n_in + 2:n_in + 2 + n_ride], scr[3:], *scr[:3])

            @pl.when((pl.program_id(0) == 0) & (pl.program_id(1) == 0))
            def _():
                gather.begin()
        lane = lax.broadcasted_iota(jnp.int32, (QB, 128), 1)
        refs_of = {"q": q_ref, "kc": kc_ref, "vc": vc_ref, "kp": kp_ref, "vp": vp_ref}
        for chunk in _unit_chunks(dil):
            rows_of = {h: {name: _residue_rows(refs_of[name], copies[name], h, residue)
                           for name in refs_of if refs_of[name] is not None}
                       for h in sorted({h for h, _ in chunk})}

            def batch(name):
                return jnp.stack([rows_of[h][name](r) for h, r in chunk])

            q, k, v = batch("q"), batch("kc"), batch("vc")
            if has_prev:
                k = jnp.concatenate([batch("kp"), k], axis=1)
                v = jnp.concatenate([batch("vp"), v], axis=1)
                bias_b = jnp.stack([b_ref[h] for h, _ in chunk])
            else:
                bias_b = jnp.stack([b_ref[h, :, QB:] for h, _ in chunk])
            s = jnp.einsum("uqd,ukd->uqk", q, k, preferred_element_type=F32) * SCALE + bias_b
            m = jnp.max(s, axis=-1, keepdims=True)
            p = jnp.exp(s - m)
            l = jnp.sum(p, axis=-1, keepdims=True)
            o = jnp.einsum("uqk,ukd->uqd", p.astype(BF16), v, preferred_element_type=F32) / l
            lse = m + jnp.log(l)
            for i, (h, r) in enumerate(chunk):
                if strided:
                    copies["o"][h % n_sets][residue(r), :] = o[i]
                else:
                    o_ref[:, h * HD:(h + 1) * HD] = o[i]
                ls[h][r * QB:(r + 1) * QB, :] = jnp.where(lane == h, lse[i], 0.0)
            if strided:
                for h in sorted({h for h, _ in chunk}):
                    o_ref[:, h * HD:(h + 1) * HD] = copies["o"][h % n_sets][...]
        for r in range(dil):
            blk = slice(r * QB, (r + 1) * QB)
            l_ref[residue(r), :] = (ls[0][blk, :] + ls[1][blk, :]) + (ls[2][blk, :] + ls[3][blk, :])
        if n_ride:
            @pl.when((pl.program_id(0) == BL - 1) & (pl.program_id(1) == nsb - 1))
            def _():
                gather.finish()

    def row(b, n):
        return b * nsb + n

    def prev(b, n):
        return b * nsb + jnp.maximum(n - 1, 0)

    in_specs = [
        pl.BlockSpec((rows, GW), lambda b, n: (row(b, n), CB_Q + g)),
        pl.BlockSpec((rows, GW), lambda b, n: (row(b, n), CB_K + g)),
        pl.BlockSpec((rows, GW), lambda b, n: (row(b, n), CB_V + g)),
    ]
    args = [proj, proj, proj]
    scratch = [pltpu.VMEM((rows, 128), F32)] * (4 + n_copied)
    if has_prev:
        in_specs += [pl.BlockSpec((rows, GW), lambda b, n: (prev(b, n), CB_K + g)),
                     pl.BlockSpec((rows, GW), lambda b, n: (prev(b, n), CB_V + g))]
        args += [proj, proj]
    in_specs.append(pl.BlockSpec((None, None, 4, QB, 2 * QB),
                                 lambda b, n: (g, jnp.minimum(n, 1), 0, 0, 0)))
    args.append(bias)
    any_spec = pl.BlockSpec(memory_space=pl.ANY)
    return pl.pallas_call(
        body, name=f"attn_fwd{g}",
        grid=(BL, nsb),
        in_specs=in_specs + [any_spec] * n_ride,
        out_specs=(pl.BlockSpec((rows, GW), lambda b, n: (row(b, n), 0)),
                   pl.BlockSpec((rows, 128), lambda b, n: (row(b, n), 0))) + (any_spec,) * n_ride,
        out_shape=(jax.ShapeDtypeStruct((T, GW), F32), jax.ShapeDtypeStruct((T, 128), F32)) + tuple(ride_shapes),
        scratch_shapes=scratch + (_Gather.scratch(ride_arrs) if n_ride else []),
        compiler_params=pltpu.CompilerParams(vmem_limit_bytes=VMEM_LIMIT),
    )(*args, *ride_arrs)


def _attn_bwd(proj, d_out, stats, bias, dproj, g):
    dil = DILATIONS[g]
    rows = QB * dil
    nsb = S // rows
    has_prev = nsb > 1
    n_steps = nsb + 1 if has_prev else 1
    n_in = 7 + (2 if has_prev else 0)

    def residue(r):
        return pl.ds(r, QB, stride=dil) if dil > 1 else pl.ds(0, QB)

    strided = dil > 1
    n_sets = _scratch_sets(rows)

    def body(*refs):
        q_ref, kc_ref, vc_ref, do_ref, st_ref, b_ref = refs[:6]
        kp_ref, vp_ref = refs[6:8] if has_prev else (None, None)
        out_ref, db_ref = refs[n_in], refs[n_in + 1]
        scr = list(refs[n_in + 2:])
        sq, sk, sv, sems = [scr.pop(0) for _ in range(4)]
        carry = scr.pop(0) if has_prev else None
        sts = scr.pop(0) if strided else st_ref
        copies = {name: [scr.pop(0) for _ in range(n_sets)] if strided else None
                  for name in ("q", "kc", "vc", "do", "dq", "dk", "dv") + (("kp", "vp") if has_prev else ())}
        b, n = pl.program_id(0), pl.program_id(1)

        @pl.when((b == 0) & (n == 0))
        def _():
            db_ref[...] = jnp.zeros_like(db_ref)

        def finish(h, r, dq, dk, dv):
            if strided:
                for name, val in (("dq", dq), ("dk", dk), ("dv", dv)):
                    copies[name][h % n_sets][residue(r), :] = val
            else:
                sl = slice(h * HD, (h + 1) * HD)
                sq[:, sl], sk[:, sl], sv[:, sl] = dq.astype(BF16), dk.astype(BF16), dv.astype(BF16)

        def finish_head(h):
            if strided:
                sl = slice(h * HD, (h + 1) * HD)
                sq[:, sl] = copies["dq"][h % n_sets][...].astype(BF16)
                sk[:, sl] = copies["dk"][h % n_sets][...].astype(BF16)
                sv[:, sl] = copies["dv"][h % n_sets][...].astype(BF16)

        def write_block(blk_idx):
            row0 = pl.multiple_of(blk_idx * rows, rows)
            _write_columns([(sq, CB * (CB_Q + g)), (sk, CB * (CB_K + g)), (sv, CB * (CB_V + g))],
                           out_ref, row0, sems)

        def carried(h, r):
            blk = slice(r * QB, (r + 1) * QB)
            return ((blk, slice(h * HD, (h + 1) * HD)), (blk, slice(GW + h * HD, GW + (h + 1) * HD)),
                    (blk, slice(2 * GW + h * HD, 2 * GW + (h + 1) * HD)))

        if has_prev:
            @pl.when(n == 0)
            def _():
                carry[...] = jnp.zeros_like(carry)

            @pl.when(n == nsb)
            def _():
                for h in range(4):
                    for r in range(dil):
                        cq, ck, cv = carried(h, r)
                        finish(h, r, carry[cq], carry[ck], carry[cv])
                    finish_head(h)
                write_block(b * nsb + nsb - 1)

        @pl.when(n < nsb)
        def _():
            if strided:
                for r in range(dil):
                    sts[r * QB:(r + 1) * QB, :] = st_ref[residue(r), :]
            refs_of = {"q": q_ref, "kc": kc_ref, "vc": vc_ref, "do": do_ref, "kp": kp_ref, "vp": vp_ref}
            for chunk in _unit_chunks(dil):
                heads = sorted({h for h, _ in chunk})
                rows_of = {h: {name: _residue_rows(refs_of[name], copies[name], h, residue)
                               for name in refs_of if refs_of[name] is not None}
                           for h in heads}

                def batch(name):
                    return jnp.stack([rows_of[h][name](r) for h, r in chunk])

                q, k, v, do = batch("q"), batch("kc"), batch("vc"), batch("do")
                if has_prev:
                    k = jnp.concatenate([batch("kp"), k], axis=1)
                    v = jnp.concatenate([batch("vp"), v], axis=1)
                    bias_b = jnp.stack([b_ref[h] for h, _ in chunk])
                else:
                    bias_b = jnp.stack([b_ref[h, :, QB:] for h, _ in chunk])
                lse = jnp.stack([sts[r * QB:(r + 1) * QB, h:h + 1] for h, r in chunk])
                delta = jnp.stack([sts[r * QB:(r + 1) * QB, 4 + h:5 + h] for h, r in chunk])
                s = jnp.einsum("uqd,ukd->uqk", q, k, preferred_element_type=F32) * SCALE + bias_b
                p = jnp.exp(s - lse)
                ds = p * (jnp.einsum("uqd,ukd->uqk", do, v, preferred_element_type=F32) - delta)
                for h in heads:
                    mine = [ds[i] for i, (hh, _) in enumerate(chunk) if hh == h]
                    tot = mine[0]
                    for extra in mine[1:]:
                        tot = tot + extra
                    if has_prev:
                        db_ref[h] += tot
                    else:
                        db_ref[h, :, QB:] += tot
                dsb, pb = ds.astype(BF16), p.astype(BF16)
                dq = jnp.einsum("uqk,ukd->uqd", dsb, k, preferred_element_type=F32) * SCALE
                dk = jnp.einsum("uqk,uqd->ukd", dsb, q, preferred_element_type=F32) * SCALE
                dv = jnp.einsum("uqk,uqd->ukd", pb, do, preferred_element_type=F32)
                for i, (h, r) in enumerate(chunk):
                    if has_prev:
                        cq, ck, cv = carried(h, r)
                        finish(h, r, carry[cq], carry[ck] + dk[i, :QB], carry[cv] + dv[i, :QB])
                        carry[cq] = dq[i]
                        carry[ck] = dk[i, QB:]
                        carry[cv] = dv[i, QB:]
                    else:
                        finish(h, r, dq[i], dk[i], dv[i])
                for h in heads:
                    finish_head(h)
            if has_prev:
                @pl.when(n > 0)
                def _():
                    write_block(b * nsb + n - 1)
            else:
                write_block(b)

    def row(b, n):
        return b * nsb + jnp.minimum(n, nsb - 1)

    def prev(b, n):
        return b * nsb + jnp.maximum(jnp.minimum(n, nsb - 1) - 1, 0)

    in_specs = [
        pl.BlockSpec((rows, GW), lambda b, n: (row(b, n), CB_Q + g)),
        pl.BlockSpec((rows, GW), lambda b, n: (row(b, n), CB_K + g)),
        pl.BlockSpec((rows, GW), lambda b, n: (row(b, n), CB_V + g)),
        pl.BlockSpec((rows, GW), lambda b, n: (row(b, n), 0)),
        pl.BlockSpec((rows, 128), lambda b, n: (row(b, n), 0)),
        pl.BlockSpec((None, None, 4, QB, 2 * QB),
                     lambda b, n: (g, jnp.minimum(jnp.minimum(n, nsb - 1), 1), 0, 0, 0)),
    ]
    args = [proj, proj, proj, d_out, stats, bias]
    scratch = [pltpu.VMEM((rows, GW), BF16)] * 3 + [pltpu.SemaphoreType.DMA((3,))]
    if has_prev:
        in_specs += [pl.BlockSpec((rows, GW), lambda b, n: (prev(b, n), CB_K + g)),
                     pl.BlockSpec((rows, GW), lambda b, n: (prev(b, n), CB_V + g))]
        args += [proj, proj]
        scratch.append(pltpu.VMEM((rows, 3 * GW), F32))
    if strided:
        n_copied = (7 + (2 if has_prev else 0)) * n_sets
        scratch += [pltpu.VMEM((rows, 128), F32)] * (1 + n_copied)
    in_specs.append(pl.BlockSpec(memory_space=pl.ANY))
    args.append(dproj)
    return pl.pallas_call(
        body, name=f"attn_bwd{g}",
        grid=(BL, n_steps),
        in_specs=in_specs,
        out_specs=(pl.BlockSpec(memory_space=pl.ANY),
                   pl.BlockSpec((4, QB, 2 * QB), lambda b, n: (0, 0, 0))),
        out_shape=(jax.ShapeDtypeStruct((T, NCOL), BF16),
                   jax.ShapeDtypeStruct((4, QB, 2 * QB), F32)),
        scratch_shapes=scratch,
        input_output_aliases={len(args) - 1: 0},
        compiler_params=pltpu.CompilerParams(vmem_limit_bytes=VMEM_LIMIT),
    )(*args)


def _tail(x2, tgt2, mod3, o_g, lse_g, proj, w_ao, w_co, w_o, conv_w, conv_b, ln_g, ln_b):
    tm = 256
    per_seq = S // tm
    halo = 16

    def body(x_ref, t_ref, mod_ref, o1_ref, o2_ref, o3_ref, l1_ref, l2_ref, l3_ref,
             ga_ref, u_ref, bg_ref, cg_ref, gc_ref, ma_ref, mc_ref, up_ref, cp_ref,
             wao_ref, wco_ref, wo_ref, cw_ref, cb_ref, lg_ref, lb_ref,
             dproj_ref, dyc_ref, do_ref, st_ref, dxd_ref,
             mg_ref, dy_ref, ain_ref, dao_ref, sin_ref, dso_ref, vec_ref,
             dga_s, dbg_s, dgm_s, sems):
        i = pl.program_id(0)
        bidx = i // per_seq
        first = (i % per_seq) == 0

        @pl.when(i == 0)
        def _():
            vec_ref[...] = jnp.zeros_like(vec_ref)

        l1, l2, l3 = l1_ref[...], l2_ref[...], l3_ref[...]
        mx = jnp.maximum(jnp.maximum(l1, l2), l3)
        e1, e2, e3 = jnp.exp(l1 - mx), jnp.exp(l2 - mx), jnp.exp(l3 - mx)
        esum = e1 + e2 + e3
        lse_tot = mx + jnp.log(esum)
        w1, w2, w3 = e1 / esum, e2 / esum, e3 / esum

        def per_head(wv):
            return jnp.concatenate([jnp.broadcast_to(wv[:, h:h + 1], (tm, HD)) for h in range(4)], axis=1)

        o = per_head(w1) * o1_ref[...] + per_head(w2) * o2_ref[...] + per_head(w3) * o3_ref[...]

        ga = ga_ref[...].astype(F32)
        sig_ga = _sigmoid(ga)
        silu_ga = ga * sig_ga
        a_in = (o * silu_ga).astype(BF16)
        a_out = _dot(a_in, wao_ref[...])

        u = u_ref[...].astype(F32)
        cg = cg_ref[...].astype(F32)
        z = cg * u
        zp = cp_ref[...].astype(F32) * up_ref[...].astype(F32)
        zp = jnp.where(first, 0.0, zp)
        zcat = jnp.concatenate([zp, z], axis=0)
        z1 = pltpu.roll(zcat, 1, 0)[halo:]
        z2 = pltpu.roll(zcat, 2, 0)[halo:]
        y_conv = cw_ref[0:1, :] * z2 + cw_ref[1:2, :] * z1 + cw_ref[2:3, :] * z + cb_ref[...]
        gc = gc_ref[...].astype(F32)
        sig_gc = _sigmoid(gc)
        silu_gc = gc * sig_gc
        bg = bg_ref[...].astype(F32)
        s_in = (bg * y_conv * silu_gc).astype(BF16)
        s_out = _dot(s_in, wco_ref[...])

        sa = _sigmoid(ma_ref[...].astype(F32))
        sc = _sigmoid(mc_ref[...].astype(F32))
        merged = (sa * a_out + sc * s_out).astype(BF16)
        y = _dot(merged, wo_ref[...])
        gate1 = 1.0 + mod_ref[0, 2:3, :]
        xv = x_ref[...]
        resid = ALPHA * xv + gate1 * y
        mu = jnp.mean(resid, axis=1, keepdims=True)
        xc = resid - mu
        var = jnp.mean(xc * xc, axis=1, keepdims=True)
        rstd = lax.rsqrt(var + LN_EPS)
        xhat = xc * rstd
        lg = lg_ref[...]
        err = xhat * lg + lb_ref[...] - t_ref[...]
        vec_ref[3:4, :] += (0.5 / D) * jnp.sum(err * err, axis=0, keepdims=True)

        dout = err * (1.0 / D)
        vec_ref[1:2, :] += jnp.sum(dout * xhat, axis=0, keepdims=True)
        vec_ref[2:3, :] += jnp.sum(dout, axis=0, keepdims=True)
        dxh = dout * lg
        dres = rstd * (dxh - jnp.mean(dxh, axis=1, keepdims=True)
                       - xhat * jnp.mean(dxh * xhat, axis=1, keepdims=True))
        dxd_ref[...] = ALPHA * dres
        dgate = jnp.sum(dres * y, axis=0, keepdims=True)
        vec_ref[4:5, :] += jnp.where(bidx == 0, dgate, 0.0)
        vec_ref[5:6, :] += jnp.where(bidx == 1, dgate, 0.0)
        dy = (dres * gate1).astype(BF16)

        dmerged = _dot_nt(dy, wo_ref[...])
        da_out = (dmerged * sa).astype(BF16)
        ds_out = (dmerged * sc).astype(BF16)
        dgm_s[:, 2 * D:3 * D] =(dmerged * s_out * sc * (1.0 - sc)).astype(BF16)
        dgm_s[:, D:2 * D] =(dmerged * a_out * sa * (1.0 - sa)).astype(BF16)
        da_in = _dot_nt(da_out, wao_ref[...])
        ds_in = _dot_nt(ds_out, wco_ref[...])

        d_o = da_in * silu_ga
        do_ref[...] = d_o.astype(BF16)
        dga_s[...] =(da_in * o * (sig_ga * (1.0 + ga * (1.0 - sig_ga)))).astype(BF16)
        lane = lax.broadcasted_iota(jnp.int32, (tm, 128), 1)
        stats = lse_tot
        od = o * d_o
        for h in range(4):
            delta = jnp.sum(od[:, h * HD:(h + 1) * HD], axis=1, keepdims=True)
            stats = jnp.where(lane == 4 + h, delta, stats)
        st_ref[...] = stats

        dbg_s[...] =(ds_in * y_conv * silu_gc).astype(BF16)
        dyc = ds_in * bg * silu_gc
        dyc_ref[...] = dyc
        vec_ref[0:1, :] += jnp.sum(dyc, axis=0, keepdims=True)
        dgm_s[:, 0:D] =(ds_in * bg * y_conv * (sig_gc * (1.0 + gc * (1.0 - sig_gc)))).astype(BF16)

        mg_ref[...] = merged
        dy_ref[...] = dy
        ain_ref[...] = a_in
        dao_ref[...] = da_out
        sin_ref[...] = s_in
        dso_ref[...] = ds_out
        _write_columns([(dga_s, CB * CB_GA), (dbg_s, D * KB_BG), (dgm_s, D * KB_GC)],
                       dproj_ref, pl.multiple_of(i * tm, tm), sems)

    def tile(width, cblk=0):
        return pl.BlockSpec((tm, width), lambda i: (i, cblk))

    def whole(shape):
        return pl.BlockSpec(shape, lambda i: tuple(0 for _ in shape))

    prev_rows = lambda i: (jnp.maximum(i * (tm // halo) - 1, 0),)
    in_specs = [
        tile(D), tile(D), pl.BlockSpec((1, 3, D), lambda i: (i // per_seq, 0, 0)),
        tile(GW), tile(GW), tile(GW), tile(128), tile(128), tile(128),
        tile(GW, CB_GA), tile(D, KB_U), tile(D, KB_BG), tile(D, KB_CG), tile(D, KB_GC),
        tile(D, KB_MA), tile(D, KB_MC),
        pl.BlockSpec((halo, D), lambda i: (*prev_rows(i), KB_U)),
        pl.BlockSpec((halo, D), lambda i: (*prev_rows(i), KB_CG)),
        whole((GW, D)), whole((D, D)), whole((D, D)),
        whole((3, D)), whole((1, D)), whole((1, D)), whole((1, D)),
    ]
    out_specs = (
        pl.BlockSpec(memory_space=pl.ANY), tile(D), tile(GW), tile(128), tile(D),
        tile(D), tile(D), tile(GW), tile(D), tile(D), tile(D),
        pl.BlockSpec((8, D), lambda i: (0, 0)),
    )
    out_shape = (
        jax.ShapeDtypeStruct((T, NCOL), BF16),
        jax.ShapeDtypeStruct((T, D), F32),
        jax.ShapeDtypeStruct((T, GW), BF16),
        jax.ShapeDtypeStruct((T, 128), F32),
        jax.ShapeDtypeStruct((T, D), F32),
        jax.ShapeDtypeStruct((T, D), BF16),
        jax.ShapeDtypeStruct((T, D), BF16),
        jax.ShapeDtypeStruct((T, GW), BF16),
        jax.ShapeDtypeStruct((T, D), BF16),
        jax.ShapeDtypeStruct((T, D), BF16),
        jax.ShapeDtypeStruct((T, D), BF16),
        jax.ShapeDtypeStruct((8, D), F32),
    )
    return pl.pallas_call(
        body, name="tail",
        grid=(T // tm,),
        in_specs=in_specs, out_specs=out_specs, out_shape=out_shape,
        scratch_shapes=[pltpu.VMEM((tm, GW), BF16), pltpu.VMEM((tm, D), BF16), pltpu.VMEM((tm, 3 * D), BF16),
                        pltpu.SemaphoreType.DMA((3,))],
        compiler_params=pltpu.CompilerParams(vmem_limit_bytes=VMEM_LIMIT),
    )(x2, tgt2, mod3, *o_g, *lse_g, proj, proj, proj, proj, proj, proj, proj, proj, proj,
      w_ao, w_co, w_o, conv_w, conv_b, ln_g, ln_b)


def _conv_bwd(dyc, proj, conv_w, dproj):
    tm = 512
    per_seq = S // tm
    halo = 16

    def body(d_ref, dn_ref, u_ref, c_ref, up_ref, cp_ref, cw_ref, _, dproj_ref, g_ref, du_s, dc_s, sems):
        i = pl.program_id(0)
        first = (i % per_seq) == 0
        last = (i % per_seq) == per_seq - 1

        @pl.when(i == 0)
        def _():
            g_ref[...] = jnp.zeros_like(g_ref)

        d = d_ref[...]
        dn = jnp.where(last, 0.0, dn_ref[...])
        dcat = jnp.concatenate([d, dn], axis=0)
        d1 = pltpu.roll(dcat, tm + 8 - 1, 0)[:tm]
        d2 = pltpu.roll(dcat, tm + 8 - 2, 0)[:tm]
        dz = cw_ref[2:3, :] * d + cw_ref[1:2, :] * d1 + cw_ref[0:1, :] * d2
        u = u_ref[...].astype(F32)
        cg = c_ref[...].astype(F32)
        du_s[...] = (dz * cg).astype(BF16)
        dc_s[...] = (dz * u).astype(BF16)
        _write_columns([(du_s, D * KB_U), (dc_s, D * KB_CG)], dproj_ref, pl.multiple_of(i * tm, tm), sems)

        z = cg * u
        zp = jnp.where(first, 0.0, cp_ref[...].astype(F32) * up_ref[...].astype(F32))
        zcat = jnp.concatenate([zp, z], axis=0)
        z1 = pltpu.roll(zcat, 1, 0)[halo:]
        z2 = pltpu.roll(zcat, 2, 0)[halo:]
        g_ref[0:1, :] += jnp.sum(d * z2, axis=0, keepdims=True)
        g_ref[1:2, :] += jnp.sum(d * z1, axis=0, keepdims=True)
        g_ref[2:3, :] += jnp.sum(d * z, axis=0, keepdims=True)

    n_tiles = T // tm
    prev_rows = lambda i: jnp.maximum(i * (tm // halo) - 1, 0)
    next_rows = lambda i: jnp.minimum((i + 1) * (tm // 8), T // 8 - 1)
    return pl.pallas_call(
        body, name="conv_bwd",
        grid=(n_tiles,),
        in_specs=[pl.BlockSpec((tm, D), lambda i: (i, 0)),
                  pl.BlockSpec((8, D), lambda i: (next_rows(i), 0)),
                  pl.BlockSpec((tm, D), lambda i: (i, KB_U)),
                  pl.BlockSpec((tm, D), lambda i: (i, KB_CG)),
                  pl.BlockSpec((halo, D), lambda i: (prev_rows(i), KB_U)),
                  pl.BlockSpec((halo, D), lambda i: (prev_rows(i), KB_CG)),
                  pl.BlockSpec((3, D), lambda i: (0, 0)),
                  pl.BlockSpec(memory_space=pl.ANY)],
        out_specs=(pl.BlockSpec(memory_space=pl.ANY),
                   pl.BlockSpec((8, D), lambda i: (0, 0))),
        out_shape=(jax.ShapeDtypeStruct((T, NCOL), BF16),
                   jax.ShapeDtypeStruct((8, D), F32)),
        scratch_shapes=[pltpu.VMEM((tm, D), BF16), pltpu.VMEM((tm, D), BF16), pltpu.SemaphoreType.DMA((2,))],
        input_output_aliases={7: 0},
        compiler_params=pltpu.CompilerParams(vmem_limit_bytes=VMEM_LIMIT),
    )(dyc, dyc, proj, proj, proj, proj, conv_w, dproj)


def _dh_dx(dproj, w_in_all, x2, dxd, mod3, chip_sums, hops=(), parts0=None):
    tm = 1024
    per_seq = S // tm
    n = len(chip_sums)
    n_in = 5 + n + (0 if parts0 is None else 1)

    def body(*refs):
        d_ref, w_ref, x_ref, dxd_ref, mod_ref = refs[:5]
        ins = refs[5:5 + n]
        gx_ref, vec_ref = refs[n_in:n_in + 2]
        outs = refs[n_in + 2:n_in + 2 + n]
        acc, send_sems, recv_sems, local_sems = refs[n_in + 2 + n:]
        i, jj = pl.program_id(0), pl.program_id(1)

        @pl.when((i == 0) & (jj == 0))
        def _():
            vec_ref[...] = jnp.zeros_like(vec_ref)
            if n:
                sends, _, mine = _chip_copies(ins, outs, send_sems, recv_sems, local_sems, hops)
                for cp in sends + mine:
                    cp.start()

        if n:
            @pl.when((i == T // tm - 1) & (jj == N_DEV - 1))
            def _():
                sends, arrivals, mine = _chip_copies(ins, outs, send_sems, recv_sems, local_sems, hops)
                for cp in arrivals:
                    cp.wait_recv()
                for cp in sends:
                    cp.wait_send()
                for cp in mine:
                    cp.wait()

        @pl.when(jj == 0)
        def _():
            acc[...] = jnp.zeros_like(acc)

        acc[...] += _dot_nt(d_ref[...], w_ref[...])

        @pl.when(jj == N_DEV - 1)
        def _():
            dh = acc[...]
            bidx = i // per_seq
            gx_ref[...] = dxd_ref[...] + dh * (1.0 + mod_ref[0, 1:2, :])
            dshift = jnp.sum(dh, axis=0, keepdims=True)
            dscale = jnp.sum(dh * x_ref[...], axis=0, keepdims=True)
            vec_ref[0:1, :] += jnp.where(bidx == 0, dshift, 0.0)
            vec_ref[1:2, :] += jnp.where(bidx == 1, dshift, 0.0)
            vec_ref[2:3, :] += jnp.where(bidx == 0, dscale, 0.0)
            vec_ref[3:4, :] += jnp.where(bidx == 1, dscale, 0.0)

    any_spec = pl.BlockSpec(memory_space=pl.ANY)
    res = pl.pallas_call(
        body, name="dh_dx",
        grid=(T // tm, N_DEV),
        in_specs=[
            pl.BlockSpec((tm, SHARD), lambda i, jj: (i, jj)),
            pl.BlockSpec((None, D, SHARD), lambda i, jj: (jj, 0, 0)),
            pl.BlockSpec((tm, D), lambda i, jj: (i, 0)),
            pl.BlockSpec((tm, D), lambda i, jj: (i, 0)),
            pl.BlockSpec((1, 3, D), lambda i, jj: (i // per_seq, 0, 0))] + [any_spec] * (n_in - 5),
        out_specs=(pl.BlockSpec((tm, D), lambda i, jj: (i, 0)),
                   pl.BlockSpec((8, D), lambda i, jj: (0, 0))) + (any_spec,) * n,
        out_shape=(jax.ShapeDtypeStruct((T, D), F32), jax.ShapeDtypeStruct((8, D), F32))
                  + tuple(jax.ShapeDtypeStruct(a.shape, a.dtype) for a in chip_sums),
        scratch_shapes=[pltpu.VMEM((tm, D), F32), pltpu.SemaphoreType.DMA((max(3 * n, 1),)),
                        pltpu.SemaphoreType.DMA((max(3 * n, 1),)), pltpu.SemaphoreType.DMA((max(n, 1),))],
        input_output_aliases={} if parts0 is None else {5 + n: 2},
        compiler_params=pltpu.CompilerParams(vmem_limit_bytes=VMEM_LIMIT),
    )(dproj, w_in_all, x2, dxd, mod3, *chip_sums, *([] if parts0 is None else [parts0]))
    return res[0], res[1], res[2:]


def _mm_tn(a, b, tn, blocks_leading, name):
    kk, m = a.shape
    n = b.shape[1]
    tk = 2048

    def body(a_ref, b_ref, o_ref, acc):
        @pl.when(pl.program_id(1) == 0)
        def _():
            acc[...] = jnp.zeros_like(acc)

        acc[...] += _dot_tn(a_ref[...], b_ref[...])

        @pl.when(pl.program_id(1) == kk // tk - 1)
        def _():
            o_ref[...] = acc[...].astype(BF16)

    if blocks_leading:
        out_spec = pl.BlockSpec((None, m, tn), lambda j, k: (j, 0, 0))
        out_shape = jax.ShapeDtypeStruct((n // tn, m, tn), BF16)
    else:
        out_spec = pl.BlockSpec((m, tn), lambda j, k: (0, j))
        out_shape = jax.ShapeDtypeStruct((m, n), BF16)
    return pl.pallas_call(
        body, name=name,
        grid=(n // tn, kk // tk),
        in_specs=[pl.BlockSpec((tk, m), lambda j, k: (k, 0)),
                  pl.BlockSpec((tk, tn), lambda j, k: (k, j))],
        out_specs=out_spec, out_shape=out_shape,
        scratch_shapes=[pltpu.VMEM((m, tn), F32)],
        compiler_params=pltpu.CompilerParams(vmem_limit_bytes=VMEM_LIMIT),
    )(a, b)


def _adamw(parts, w, m, v, name, row_tile=None):
    n_parts, rows, cols = parts.shape
    tr = rows if row_tile is None else row_tile
    c1 = 1.0 - ADAM_B1 ** ADAM_STEP
    c2 = 1.0 - ADAM_B2 ** ADAM_STEP

    def body(p_ref, w_ref, m_ref, v_ref, g_ref, d_ref, nm_ref, nv_ref):
        g = p_ref[0].astype(F32)
        for s in range(1, n_parts):
            g = g + p_ref[s].astype(F32)
        nm = ADAM_B1 * m_ref[...] + (1.0 - ADAM_B1) * g
        nv = ADAM_B2 * v_ref[...] + (1.0 - ADAM_B2) * (g * g)
        m_hat = nm / c1
        v_hat = nv / c2
        g_ref[...] = g
        d_ref[...] = -ADAM_LR * (m_hat / (jnp.sqrt(v_hat) + ADAM_EPS) + ADAM_WD * w_ref[...])
        nm_ref[...] = nm
        nv_ref[...] = nv

    blk = pl.BlockSpec((tr, cols), lambda i: (i, 0))
    shp = jax.ShapeDtypeStruct((rows, cols), F32)
    return pl.pallas_call(
        body, name=name,
        grid=(rows // tr,),
        in_specs=[pl.BlockSpec((n_parts, tr, cols), lambda i: (0, i, 0)), blk, blk, blk],
        out_specs=(blk, blk, blk, blk),
        out_shape=(shp, shp, shp, shp),
        compiler_params=pltpu.CompilerParams(vmem_limit_bytes=VMEM_LIMIT),
    )(parts, w, m, v)


def _loss_sum(rows):
    def body(r_ref, o_ref):
        o_ref[...] = jnp.sum(jnp.sum(r_ref[...], axis=0, keepdims=True), axis=1, keepdims=True)

    return pl.pallas_call(body, name="loss_sum", out_shape=jax.ShapeDtypeStruct((1, 1), F32))(rows)


def _attention_forward(proj, rel_bias, rides=((), (), ())):
    buckets_np, masks_np = _bucket_maps()
    buckets, masks = jnp.asarray(buckets_np), jnp.asarray(masks_np)
    bias = _bias_expand(rel_bias, buckets, masks)
    fwd = [_attn_fwd(proj, bias, g, rides[g]) for g in range(3)]
    brought = [arr for f in fwd for arr in f[2:]]
    return bias, buckets, [f[0] for f in fwd], [f[1] for f in fwd], brought


def _local_step(x2, tgt2, mod3, h, proj, attn, w_ao, w_co, w_o, conv_w, conv_b, ln_g, ln_b):
    bias, buckets, o_g, lse_g = attn

    (dproj, dyc, d_o, stats, dxd, merged, dy, a_in, da_out, s_in, ds_out, tail_vec) = _tail(
        x2, tgt2, mod3, o_g, lse_g, proj, w_ao, w_co, w_o, conv_w, conv_b, ln_g, ln_b)

    dbias = []
    for g in range(3):
        dproj, db = _attn_bwd(proj, d_o, stats, bias, dproj, g)
        dbias.append(db)
    g_rel_bias = _bias_grad(*dbias, buckets)
    dproj, conv_vec = _conv_bwd(dyc, proj, conv_w, dproj)

    gw_o = _mm_tn(merged, dy, D, False, "gw_o")
    gw_co = _mm_tn(s_in, ds_out, D, False, "gw_conv_out")
    gw_ao = _mm_tn(a_in, da_out, D, False, "gw_attn_out")
    gw_ao = jnp.transpose(gw_ao.reshape(GW, N_DEV, D // N_DEV), (1, 0, 2))
    return dproj, dxd, gw_ao, gw_co, gw_o, conv_vec, g_rel_bias, tail_vec


def kernel(x, c, w_ada, b_ada, w_in, conv_w, conv_b, rel_bias, w_attn_out, w_conv_out, w_o, ln_g, ln_b, loss_target, m_w_ada, m_b_ada, m_w_in, m_conv_w, m_conv_b, m_rel_bias, m_w_attn_out, m_w_conv_out, m_w_o, m_ln_g, m_ln_b, v_w_ada, v_b_ada, v_w_in, v_conv_w, v_conv_b, v_rel_bias, v_w_attn_out, v_w_conv_out, v_w_o, v_ln_g, v_ln_b):
    me = _my_index()
    x2 = x.reshape(T, D)
    tgt2 = loss_target.reshape(T, D)

    b_cols = lax.dynamic_slice(b_ada, (0, me * ADA_SHARD), (1, ADA_SHARD))
    c_g, mod_in = _mod_exchange(jnp.pad(c, ((0, 8 - BL), (0, 0))), w_ada[0], b_cols)
    c_all = c_g[:, 0:BL, :].reshape(N_DEV * BL, D)
    mod3 = jnp.transpose(mod_in[:, 0:BL, :], (1, 0, 2)).reshape(BL, 3, D)

    h = _prep_h(x2, mod3)
    rows_shape = jax.ShapeDtypeStruct((N_DEV, D // N_DEV, D), BF16)
    proj, w_in_all, (w_ao_g, w_co_g, w_o_g, conv_w_g) = _gather_proj(
        _shard_order(), h, w_in[0].astype(BF16), 1024,
        ([w_attn_out[0].astype(BF16), w_conv_out[0].astype(BF16), w_o[0].astype(BF16), conv_w[0]],
         [jax.ShapeDtypeStruct((N_DEV, GW, D // N_DEV), BF16), rows_shape, rows_shape,
          jax.ShapeDtypeStruct((N_DEV, 3, D // N_DEV), F32)]))

    *attn, _ = _attention_forward(proj, rel_bias)
    w_ao_full = jnp.transpose(w_ao_g, (1, 0, 2)).reshape(GW, D)
    w_co_full = w_co_g.reshape(D, D)
    w_o_full = w_o_g.reshape(D, D)
    conv_w_full = jnp.transpose(conv_w_g, (1, 0, 2)).reshape(3, D)

    (dproj, dxd, gw_ao, gw_co, gw_o, conv_vec, g_rel_bias, tail_vec) = _local_step(
        x2, tgt2, mod3, h, proj, attn, w_ao_full, w_co_full, w_o_full,
        conv_w_full, conv_b, ln_g, ln_b)

    g_conv_w_blocks = jnp.transpose(conv_vec[0:3].reshape(3, N_DEV, D // N_DEV), (1, 0, 2))
    partials = [gw_ao, gw_co.reshape(N_DEV, D // N_DEV, D), gw_o.reshape(N_DEV, D // N_DEV, D), g_conv_w_blocks]
    w_in_sums, w_in_parts, sib = _gw_in_pair(
        _slice_order(), h, dproj, partials,
        [jax.ShapeDtypeStruct((4, GW, D // N_DEV), BF16),
         jax.ShapeDtypeStruct((4, D // N_DEV, D), BF16),
         jax.ShapeDtypeStruct((4, D // N_DEV, D), BF16),
         jax.ShapeDtypeStruct((4, 3, D // N_DEV), F32)])
    names = ["w_attn_out", "w_conv_out", "w_o", "conv_w"]
    core = lax.axis_index("c").astype(jnp.int32).reshape(1)
    chip_sums = [w_in_sums] + [_pair_add(core, partials[a], sib[a], None, "pair_add_" + names[a])
                               for a in range(4)]
    hops = [(3,)] + [(1, 2, 3)] * 4
    grad_x, mod_vec, (r_in, r_ao, r_co, r_o, r_cw) = _dh_dx(
        dproj, w_in_all, x2, dxd, mod3, chip_sums, hops, w_in_parts)

    small = jnp.concatenate([
        tail_vec[0:4],
        jnp.pad(g_rel_bias.reshape(1, N_BUCKETS * N_HEADS), ((0, 0), (0, D - N_BUCKETS * N_HEADS))),
        jnp.zeros((3, D), F32)], axis=0)
    dmod = jnp.concatenate([mod_vec[0:2], mod_vec[2:4], tail_vec[4:6]], axis=1)
    small_g, dmod_g = _all_gather(
        [small, dmod],
        [jax.ShapeDtypeStruct((N_DEV, 8, D), F32), jax.ShapeDtypeStruct((N_DEV, BL, 3 * D), F32)],
        "gather_small")
    dmod_all = dmod_g.reshape(N_DEV * BL, 3 * D)
    loss = _loss_sum(small_g[:, 3, :]).reshape(())
    g_w_ada = _ada_bwd(jnp.transpose(c_all), lax.dynamic_slice(dmod_all, (0, me * ADA_SHARD),
                                                               (N_DEV * BL, ADA_SHARD)))

    def upd(parts, w, m, v, name, row_tile=None):
        shape = w.shape
        w2, m2, v2 = (t.reshape(parts.shape[1:]) for t in (w, m, v))
        return tuple(t.reshape(shape) for t in _adamw(parts, w2, m2, v2, name, row_tile))

    res = {
        "w_ada": upd(g_w_ada[None], w_ada, m_w_ada, v_w_ada, "adam_w_ada", 256),
        "b_ada": upd(dmod_all[:, None, :], b_ada, m_b_ada, v_b_ada, "adam_b_ada"),
        "w_in": upd(r_in, w_in, m_w_in, v_w_in, "adam_w_in", 128),
        "conv_w": upd(r_cw, conv_w, m_conv_w, v_conv_w, "adam_conv_w"),
        "conv_b": upd(small_g[:, 0:1, :], conv_b, m_conv_b, v_conv_b, "adam_conv_b"),
        "rel_bias": upd(small_g[:, 4, :N_BUCKETS * N_HEADS].reshape(N_DEV, N_BUCKETS, N_HEADS),
                        rel_bias, m_rel_bias, v_rel_bias, "adam_rel_bias"),
        "w_attn_out": upd(r_ao, w_attn_out, m_w_attn_out, v_w_attn_out, "adam_w_attn_out"),
        "w_conv_out": upd(r_co, w_conv_out, m_w_conv_out, v_w_conv_out, "adam_w_conv_out"),
        "w_o": upd(r_o, w_o, m_w_o, v_w_o, "adam_w_o"),
        "ln_g": upd(small_g[:, 1:2, :], ln_g, m_ln_g, v_ln_g, "adam_ln_g"),
        "ln_b": upd(small_g[:, 2:3, :], ln_b, m_ln_b, v_ln_b, "adam_ln_b"),
    }
    order = ["w_ada", "b_ada", "w_in", "conv_w", "conv_b", "rel_bias", "w_attn_out", "w_conv_out",
             "w_o", "ln_g", "ln_b"]
    outs = [loss, grad_x.reshape(BL, S, D)]
    for k in range(4):
        outs += [res[name][k] for name in order]
    return tuple(outs)
```

```python
import functools
import math

import numpy as np
import jax
import jax.numpy as jnp
from jax import lax
from jax.experimental import pallas as pl
from jax.experimental.pallas import tpu as pltpu

F32 = jnp.float32
BF16 = jnp.bfloat16
MESH = pl.DeviceIdType.MESH

N_DEV = 8
D = 1024
S = 2048
BL = 2
T = BL * S
NCOL = 11264
SHARD = NCOL // N_DEV
CB = 512
NCB = NCOL // CB
HD = 128
GW = 512
QB = 128
DILATIONS = (1, 4, 16)
N_STEPS = 128
N_BUCKETS = 32
N_HEADS = 12
ALPHA = 2.0 ** 0.25
LN_EPS = 1e-5
NEG_INF = -1e30
SCALE = HD ** -0.5
ADA_SHARD = 3 * D // N_DEV

CB_Q, CB_K, CB_V, CB_GA = 0, 3, 6, 9
KB_U, KB_BG, KB_CG, KB_GC, KB_MA, KB_MC = 5, 6, 7, 8, 9, 10

ADAM_LR, ADAM_B1, ADAM_B2, ADAM_EPS, ADAM_WD, ADAM_STEP = 0.001, 0.9, 0.999, 1e-08, 0.01, 10

VMEM_LIMIT = 56 * 1024 * 1024


def _dot(a, b):
    return jnp.dot(a, b, preferred_element_type=F32)


def _dot_nt(a, b):
    return lax.dot_general(a, b, (((1,), (1,)), ((), ())), preferred_element_type=F32)


def _dot_tn(a, b):
    return lax.dot_general(a, b, (((0,), (0,)), ((), ())), preferred_element_type=F32)


def _sigmoid(v):
    return 1.0 / (1.0 + jnp.exp(-v))


def _write_columns(pieces, dst_hbm, row0, sems):
    copies = []
    for k, (src, col0) in enumerate(pieces):
        rows, width = src.shape
        copies.append(pltpu.make_async_copy(
            src, dst_hbm.at[pl.ds(row0, rows), pl.ds(col0, width)], sems.at[k]))
    for cp in copies:
        cp.start()
    for cp in copies:
        cp.wait()


def _my_index():
    return 4 * lax.axis_index("x") + 2 * lax.axis_index("y") + lax.axis_index("c")


class _Gather:
    def __init__(self, ins, outs, stage, send_sems, recv_sems, local_sems):
        self.ins, self.outs, self.stage = ins, outs, stage
        self.send_sems, self.recv_sems, self.local_sems = send_sems, recv_sems, local_sems
        x, y, c = lax.axis_index("x"), lax.axis_index("y"), lax.axis_index("c")
        self.c = c
        self.me, self.sibling = (x, y, c), (x, y, 1 - c)
        self.chips = [(1 - x, y), (x, 1 - y), (1 - x, 1 - y)]

    @staticmethod
    def scratch(arrs):
        n = len(arrs)
        return ([pltpu.SemaphoreType.DMA((7 * n,)), pltpu.SemaphoreType.DMA((7 * n,)),
                 pltpu.SemaphoreType.DMA((n,))] + [pltpu.VMEM(a.shape, a.dtype) for a in arrs])

    def _copy(self, a, k, block, to, src=None):
        dst = self.outs[a].at[4 * block[0] + 2 * block[1] + block[2]]
        return pltpu.make_async_remote_copy(
            src_ref=dst if src is None else src, dst_ref=dst,
            send_sem=self.send_sems.at[a * 7 + k], recv_sem=self.recv_sems.at[a * 7 + k],
            device_id=to, device_id_type=MESH)

    def _first(self):
        first = []
        for a in range(len(self.ins)):
            first.append(self._copy(a, 0, self.me, self.sibling, src=self.ins[a]))
            first += [self._copy(a, 1 + j, self.me, (*chip, self.c), src=self.ins[a])
                      for j, chip in enumerate(self.chips)]
        return first

    def _mine(self):
        me = self.me
        return [pltpu.make_async_copy(self.stage[a], self.outs[a].at[4 * me[0] + 2 * me[1] + me[2]],
                                      self.local_sems.at[a]) for a in range(len(self.ins))]

    def begin(self):
        for cp in self._first():
            cp.start()
        loads = [pltpu.make_async_copy(self.ins[a], self.stage[a], self.local_sems.at[a])
                 for a in range(len(self.ins))]
        for cp in loads:
            cp.start()
        for cp in loads:
            cp.wait()
        for cp in self._mine():
            cp.start()

    def finish(self):
        n, c, me, sibling = len(self.ins), self.c, self.me, self.sibling
        passed = []
        for j, chip in enumerate(self.chips):
            for a in range(n):
                self._copy(a, 1 + j, (*chip, c), me).wait_recv()
                fwd = self._copy(a, 4 + j, (*chip, c), sibling)
                fwd.start()
                passed.append(fwd)
        for a in range(n):
            self._copy(a, 0, sibling, me).wait_recv()
        for j, chip in enumerate(self.chips):
            for a in range(n):
                self._copy(a, 4 + j, (*chip, 1 - c), me).wait_recv()
        for cp in self._first() + passed:
            cp.wait_send()
        for cp in self._mine():
            cp.wait()


def _all_gather(arrs, out_shapes, name):
    n = len(arrs)

    def body(*refs):
        g = _Gather(refs[:n], refs[n:2 * n], refs[2 * n + 3:], *refs[2 * n:2 * n + 3])
        g.begin()
        g.finish()

    any_spec = pl.BlockSpec(memory_space=pl.ANY)
    return pl.pallas_call(
        body, name=name,
        out_shape=tuple(out_shapes),
        in_specs=[any_spec] * n,
        out_specs=tuple([any_spec] * n),
        scratch_shapes=_Gather.scratch(arrs),
    )(*arrs)


def _slice_order():
    x, y, c = lax.axis_index("x"), lax.axis_index("y"), lax.axis_index("c")
    slots = []
    for q in (2 * (1 - x) + y, 2 * x + (1 - y), 2 * (1 - x) + (1 - y), 2 * x + y):
        slots += [2 * q + 1 - c, 2 * q + c]
    return jnp.stack(slots).astype(jnp.int32)


def _gw_in_pair(order, h, dproj, smalls, small_shapes4):
    kk, m = h.shape
    tk = min(kk, 2048)
    nk = kk // tk
    ncols = dproj.shape[1] // N_DEV
    n = len(smalls)

    def body(order_ref, h_ref, d_ref, *rest):
        ins = rest[:n]
        sums_hbm, parts_hbm = rest[n], rest[n + 1]
        sib = rest[n + 2:2 * n + 2]
        (acc, sendbuf, recvbuf, sumbuf, send_sems, recv_sems, local_sem, ssend, srecv,
         isend, irecv) = rest[2 * n + 2:]
        js, k = pl.program_id(0), pl.program_id(1)
        x, y, c = lax.axis_index("x"), lax.axis_index("y"), lax.axis_index("c")
        sibling = (x, y, 1 - c)
        my_chip = 2 * x + y
        near = [(1 - x, y, c), (x, 1 - y, c)]

        def ici_copy(p, out_chip):
            peer = near[p]
            return pltpu.make_async_remote_copy(
                src_ref=sumbuf.at[p], dst_ref=parts_hbm.at[out_chip],
                send_sem=isend.at[p], recv_sem=irecv.at[p], device_id=peer, device_id_type=MESH)

        def small_copies():
            return [pltpu.make_async_remote_copy(
                        src_ref=ins[a].at[2 * q + 1 - c], dst_ref=sib[a].at[q],
                        send_sem=ssend.at[a * 4 + q], recv_sem=srecv.at[a * 4 + q],
                        device_id=sibling, device_id_type=MESH)
                    for a in range(n) for q in range(4)]

        def slice_copy(p):
            return pltpu.make_async_remote_copy(
                src_ref=sendbuf, dst_ref=recvbuf.at[p], send_sem=send_sems.at[p], recv_sem=recv_sems.at[p],
                device_id=sibling, device_id_type=MESH)

        def sum_copy(p):
            return pltpu.make_async_copy(sumbuf.at[2], sums_hbm.at[order_ref[2 * p] // 2], local_sem)

        @pl.when((js == 0) & (k == 0))
        def _():
            for cp in small_copies():
                cp.start()

        @pl.when(k == 0)
        def _():
            acc[...] = jnp.zeros_like(acc)

        acc[...] += _dot_tn(h_ref[...], d_ref[...])

        for p in range(4):
            @pl.when((js == 2 * p) & (k == nk - 1))
            def _():
                if p > 0:
                    slice_copy(p - 1).wait_send()
                sendbuf[...] = acc[...].astype(BF16)
                slice_copy(p).start()

            @pl.when((js == 2 * p + 1) & (k == nk - 1))
            def _():
                slice_copy(p).wait_recv()
                if p == 3:
                    sum_copy(2).wait()
                sumbuf[min(p, 2)] = (acc[...] + recvbuf[p].astype(F32)).astype(BF16)
                if p < 2:
                    ici_copy(p, my_chip).start()
                else:
                    sum_copy(p).start()

        @pl.when((js == N_DEV - 1) & (k == nk - 1))
        def _():
            slice_copy(3).wait_send()
            sum_copy(3).wait()
            for cp in small_copies():
                cp.wait()
            for p in range(2):
                ici_copy(p, 2 * near[p][0] + near[p][1]).wait_recv()
                ici_copy(p, my_chip).wait_send()

    any_spec = pl.BlockSpec(memory_space=pl.ANY)
    res = pl.pallas_call(
        body, name="gw_in_pair",
        grid_spec=pltpu.PrefetchScalarGridSpec(
            num_scalar_prefetch=1,
            grid=(N_DEV, nk),
            in_specs=[pl.BlockSpec((tk, m), lambda js, k, order_ref: (k, 0)),
                      pl.BlockSpec((tk, ncols), lambda js, k, order_ref: (k, order_ref[js]))] + [any_spec] * n,
            out_specs=(any_spec,) * (n + 2),
            scratch_shapes=[pltpu.VMEM((m, ncols), F32), pltpu.VMEM((m, ncols), BF16),
                            pltpu.VMEM((4, m, ncols), BF16), pltpu.VMEM((3, m, ncols), BF16),
                            pltpu.SemaphoreType.DMA((4,)), pltpu.SemaphoreType.DMA((4,)),
                            pltpu.SemaphoreType.DMA,
                            pltpu.SemaphoreType.DMA((4 * n,)), pltpu.SemaphoreType.DMA((4 * n,)),
                            pltpu.SemaphoreType.DMA((2,)), pltpu.SemaphoreType.DMA((2,))]),
        out_shape=(jax.ShapeDtypeStruct((4, m, ncols), BF16),) * 2 + tuple(small_shapes4),
        compiler_params=pltpu.CompilerParams(vmem_limit_bytes=VMEM_LIMIT),
    )(order, h, dproj, *smalls)
    return res[0], res[1], res[2:]


def _chip_copies(ins, outs, send_sems, recv_sems, local_sems, hops):
    n = len(ins)
    x, y, c = lax.axis_index("x"), lax.axis_index("y"), lax.axis_index("c")
    my_chip = 2 * x + y

    def peer_of(k):
        return ((1 - x) if (k >> 1) & 1 else x, (1 - y) if k & 1 else y, c)

    def copy(a, k, out_chip):
        peer = peer_of(k)
        return pltpu.make_async_remote_copy(
            src_ref=ins[a].at[2 * peer[0] + peer[1]], dst_ref=outs[a].at[out_chip],
            send_sem=send_sems.at[a * 3 + k - 1], recv_sem=recv_sems.at[a * 3 + k - 1],
            device_id=peer, device_id_type=MESH)

    sends = [copy(a, k, my_chip) for k in range(1, 4) for a in range(n) if k in hops[a]]
    arrivals = []
    for k in range(1, 4):
        peer = peer_of(k)
        arrivals += [copy(a, k, 2 * peer[0] + peer[1]) for a in range(n) if k in hops[a]]
    mine = [pltpu.make_async_copy(ins[a].at[my_chip], outs[a].at[my_chip], local_sems.at[a])
            for a in range(n)]
    return sends, arrivals, mine


def _pair_add(core, mine, theirs, row_tile, name):
    _, rows, cols = theirs.shape
    tr = rows if row_tile is None else row_tile

    def body(core_ref, a_ref, b_ref, o_ref):
        o_ref[...] = (a_ref[...].astype(F32) + b_ref[...].astype(F32)).astype(o_ref.dtype)

    blk = pl.BlockSpec((None, tr, cols), lambda q, i, core_ref: (q, i, 0))
    return pl.pallas_call(
        body, name=name,
        grid_spec=pltpu.PrefetchScalarGridSpec(
            num_scalar_prefetch=1,
            grid=(4, rows // tr),
            in_specs=[pl.BlockSpec((None, tr, cols), lambda q, i, core_ref: (2 * q + core_ref[0], i, 0)), blk],
            out_specs=blk),
        out_shape=jax.ShapeDtypeStruct(theirs.shape, theirs.dtype),
    )(core, mine, theirs)


def _mod_exchange(c8, w_ada, b_cols):
    cols = w_ada.shape[1]

    def body(c_ref, w_ref, b_ref, call_ref, mod_ref, msend, send1, recv1, send2, recv2):
        x, y, c = lax.axis_index("x"), lax.axis_index("y"), lax.axis_index("c")
        my_slot = 4 * x + 2 * y + c

        def peer_of(k):
            return ((1 - x) if (k >> 2) & 1 else x, (1 - y) if (k >> 1) & 1 else y, (1 - c) if k & 1 else c)

        def slot_of(dev):
            return 4 * dev[0] + 2 * dev[1] + dev[2]

        def exchange(src_of, dst_ref, send_sems, recv_sems):
            sends, arrivals = [], []
            for k in range(1, 8):
                peer = peer_of(k)
                sends.append(pltpu.make_async_remote_copy(
                    src_ref=src_of(slot_of(peer)), dst_ref=dst_ref.at[my_slot],
                    send_sem=send_sems.at[k - 1], recv_sem=recv_sems.at[k - 1],
                    device_id=peer, device_id_type=MESH))
                arrivals.append(pltpu.make_async_remote_copy(
                    src_ref=src_of(my_slot), dst_ref=dst_ref.at[slot_of(peer)],
                    send_sem=send_sems.at[k - 1], recv_sem=recv_sems.at[k - 1],
                    device_id=peer, device_id_type=MESH))
            for cp in sends:
                cp.start()
            for cp in arrivals:
                cp.wait_recv()
            for cp in sends:
                cp.wait_send()

        call_ref[my_slot] = c_ref[...]
        exchange(lambda s: c_ref, call_ref, send1, recv1)
        cv = call_ref[...].reshape(N_DEV * 8, c_ref.shape[1])
        act = cv * _sigmoid(cv)
        mod = jnp.dot(act, w_ref[...], preferred_element_type=F32,
                      precision=lax.Precision.HIGHEST) + b_ref[...]
        msend[...] = mod.reshape(N_DEV, 8, cols)
        mod_ref[my_slot] = msend[my_slot]
        exchange(lambda s: msend.at[s], mod_ref, send2, recv2)

    return pl.pallas_call(
        body, name="mod_exchange",
        out_shape=(jax.ShapeDtypeStruct((N_DEV, 8, c8.shape[1]), F32),
                   jax.ShapeDtypeStruct((N_DEV, 8, cols), F32)),
        scratch_shapes=[pltpu.VMEM((N_DEV, 8, cols), F32)] + [pltpu.SemaphoreType.DMA((7,))] * 4,
    )(c8, w_ada, b_cols)


def _ada_bwd(c_all_t, dmod_cols):
    def body(c_ref, d_ref, o_ref):
        cv = c_ref[...]
        sc = cv * _sigmoid(cv)
        o_ref[...] = jnp.dot(sc, d_ref[...], preferred_element_type=F32,
                             precision=lax.Precision.HIGHEST)

    return pl.pallas_call(
        body, name="ada_bwd",
        out_shape=jax.ShapeDtypeStruct((c_all_t.shape[0], dmod_cols.shape[1]), F32),
    )(c_all_t, dmod_cols)


def _shard_order():
    x, y, c = lax.axis_index("x"), lax.axis_index("y"), lax.axis_index("c")
    devs = [(x, y, c), (x, y, 1 - c)]
    for chip in [(1 - x, y), (x, 1 - y), (1 - x, 1 - y)]:
        devs += [(*chip, c), (*chip, 1 - c)]
    return jnp.stack([4 * d[0] + 2 * d[1] + d[2] for d in devs]).astype(jnp.int32)


def _gather_proj(order, x2, mod3, w_shard, tm, ride=()):
    rows, kdim = x2.shape
    ncols = w_shard.shape[1]
    n_i = rows // tm
    per_seq = n_i // mod3.shape[0]
    ride_arrs, ride_shapes = ride if ride else ((), ())
    n_ride = len(ride_arrs)

    def body(order_ref, x_ref, mod_ref, mine_hbm, *rest):
        ride_ins = rest[:n_ride]
        o_ref, h_ref, all_hbm = rest[n_ride:n_ride + 3]
        ride_outs = rest[n_ride + 3:2 * n_ride + 3]
        wv, send_sems, recv_sems, local_sems = rest[2 * n_ride + 3:2 * n_ride + 7]
        ride_scr = rest[2 * n_ride + 7:]
        j, i = pl.program_id(0), pl.program_id(1)
        x, y, c = lax.axis_index("x"), lax.axis_index("y"), lax.axis_index("c")
        me, sibling = (x, y, c), (x, y, 1 - c)
        chips = [(1 - x, y), (x, 1 - y), (1 - x, 1 - y)]

        def slot(dev):
            return 4 * dev[0] + 2 * dev[1] + dev[2]

        def copy(k, block, to):
            return pltpu.make_async_remote_copy(
                src_ref=wv.at[slot(block)], dst_ref=wv.at[slot(block)],
                send_sem=send_sems.at[k], recv_sem=recv_sems.at[k],
                device_id=to, device_id_type=MESH)

        def keep(step, block):
            return pltpu.make_async_copy(wv.at[slot(block)], all_hbm.at[slot(block)], local_sems.at[step])

        if n_ride:
            gather = _Gather(ride_ins, ride_outs, ride_scr[3:], *ride_scr[:3])
        first = [copy(0, me, sibling)] + [copy(1 + q, me, (*chip, c)) for q, chip in enumerate(chips)]
        passed = [copy(4 + q, (*chip, c), sibling) for q, chip in enumerate(chips)]
        due = [(me, None, None), (sibling, copy(0, sibling, me), None)]
        for q, chip in enumerate(chips):
            due.append(((*chip, c), copy(1 + q, (*chip, c), me), passed[q]))
            due.append(((*chip, 1 - c), copy(4 + q, (*chip, 1 - c), me), None))

        @pl.when((j == 0) & (i == 0))
        def _():
            load = pltpu.make_async_copy(mine_hbm, wv.at[slot(me)], local_sems.at[N_DEV])
            load.start()
            load.wait()
            for cp in first:
                cp.start()
            keep(0, me).start()

        for step in range(1, N_DEV):
            block, arrival, forward = due[step]

            @pl.when((j == step) & (i == 0))
            def _():
                arrival.wait_recv()
                if forward is not None:
                    forward.start()
                keep(step, block).start()
                if n_ride and step == N_DEV - 2:
                    gather.begin()

        hb = (x_ref[...] * (1.0 + mod_ref[0, 1:2, :]) + mod_ref[0, 0:1, :]).astype(BF16)

        @pl.when(j == 0)
        def _():
            h_ref[...] = hb

        o_ref[...] = _dot(hb, wv[order_ref[j]]).astype(BF16)

        @pl.when((j == N_DEV - 1) & (i == n_i - 1))
        def _():
            for cp in first + passed:
                cp.wait_send()
            for step in range(N_DEV):
                keep(step, due[step][0]).wait()
            if n_ride:
                gather.finish()

    any_spec = pl.BlockSpec(memory_space=pl.ANY)
    res = pl.pallas_call(
        body, name="gather_proj",
        grid_spec=pltpu.PrefetchScalarGridSpec(
            num_scalar_prefetch=1,
            grid=(N_DEV, n_i),
            in_specs=[pl.BlockSpec((tm, kdim), lambda j, i, order_ref: (i, 0)),
                      pl.BlockSpec((1, 3, kdim), lambda j, i, order_ref: (i // per_seq, 0, 0)),
                      any_spec] + [any_spec] * n_ride,
            out_specs=(pl.BlockSpec((tm, ncols), lambda j, i, order_ref: (i, order_ref[j])),
                       pl.BlockSpec((tm, kdim), lambda j, i, order_ref: (jnp.where(j == 0, i, n_i - 1), 0)),
                       any_spec)
                      + (any_spec,) * n_ride,
            scratch_shapes=[pltpu.VMEM((N_DEV, kdim, ncols), BF16),
                            pltpu.SemaphoreType.DMA((7,)), pltpu.SemaphoreType.DMA((7,)),
                            pltpu.SemaphoreType.DMA((N_DEV + 1,))]
                           + (_Gather.scratch(ride_arrs) if n_ride else [])),
        out_shape=(jax.ShapeDtypeStruct((rows, N_DEV * ncols), BF16),
                   jax.ShapeDtypeStruct((rows, kdim), BF16),
                   jax.ShapeDtypeStruct((N_DEV, kdim, ncols), BF16)) + tuple(ride_shapes),
        compiler_params=pltpu.CompilerParams(vmem_limit_bytes=VMEM_LIMIT),
    )(order, x2, mod3, w_shard, *ride_arrs)
    return res[0], res[1], res[2], res[3:]


def _bucket_maps():
    a = np.arange(QB)[:, None]
    b = np.arange(2 * QB)[None, :]
    steps = a + QB - b
    maps = []
    for dil in DILATIONS:
        dist = np.maximum(steps, 0) * dil
        nf = np.maximum(dist, 1).astype(np.float32)
        large = 16 + (np.log(nf / np.float32(16)) / np.float32(math.log(128.0))
                      * np.float32(16)).astype(np.int32)
        large = np.minimum(large, N_BUCKETS - 1)
        maps.append(np.where(dist < 16, dist, large).astype(np.int32))
    band = (steps >= 0) & (steps <= N_STEPS)
    first = band & (b >= QB)
    masks = np.stack([first, band]).astype(np.int32)
    return np.stack(maps), masks


def _bias_expand(rel_bias, buckets, masks):
    def body(tab_ref, bk_ref, mk_ref, o_ref):
        for g in range(3):
            bk = bk_ref[g]
            for h in range(4):
                col = 4 * g + h
                val = jnp.zeros((QB, 2 * QB), F32)
                for k in range(N_BUCKETS):
                    val = jnp.where(bk == k, tab_ref[k, col], val)
                o_ref[g, 0, h] = jnp.where(mk_ref[0] != 0, val, NEG_INF)
                o_ref[g, 1, h] = jnp.where(mk_ref[1] != 0, val, NEG_INF)

    return pl.pallas_call(
        body, name="bias_expand",
        in_specs=[pl.BlockSpec(memory_space=pltpu.SMEM),
                  pl.BlockSpec(memory_space=pltpu.VMEM),
                  pl.BlockSpec(memory_space=pltpu.VMEM)],
        out_shape=jax.ShapeDtypeStruct((3, 2, 4, QB, 2 * QB), F32),
    )(rel_bias, buckets, masks)


def _bias_grad(ds1, ds2, ds3, buckets):
    def body(d1_ref, d2_ref, d3_ref, bk_ref, o_ref):
        for g, d_ref in enumerate((d1_ref, d2_ref, d3_ref)):
            bk = bk_ref[g]
            for h in range(4):
                dv = d_ref[h]
                for k in range(N_BUCKETS):
                    o_ref[k, 4 * g + h] = jnp.sum(jnp.where(bk == k, dv, 0.0))

    return pl.pallas_call(
        body, name="bias_grad",
        in_specs=[pl.BlockSpec(memory_space=pltpu.VMEM)] * 4,
        out_specs=pl.BlockSpec(memory_space=pltpu.SMEM),
        out_shape=jax.ShapeDtypeStruct((N_BUCKETS, N_HEADS), F32),
    )(ds1, ds2, ds3, buckets)


def _scratch_sets(rows):
    return 4 if rows <= 512 else 1


def _unit_chunks(dil, size=16):
    units = [(h, r) for h in range(4) for r in range(dil)]
    return [units[i:i + size] for i in range(0, len(units), size)]


def _residue_rows(src_ref, copies, h, residue):
    sl = slice(h * HD, (h + 1) * HD)
    if copies is None:
        return lambda r: src_ref[:, sl]
    buf = copies[h % len(copies)]
    buf[...] = src_ref[:, sl].astype(F32)
    return lambda r: buf[residue(r), :].astype(BF16)


def _attn_fwd(proj, bias, g, ride=()):
    dil = DILATIONS[g]
    rows = QB * dil
    nsb = S // rows
    has_prev = nsb > 1

    def residue(r):
        return pl.ds(r, QB, stride=dil) if dil > 1 else pl.ds(0, QB)

    strided = dil > 1
    n_sets = _scratch_sets(rows)
    n_attn = 6 if has_prev else 4
    ride_arrs, ride_shapes = ride if ride else ((), ())
    n_ride = len(ride_arrs)
    n_in = n_attn + n_ride
    n_copied = (4 + (2 if has_prev else 0)) * (n_sets if strided else 0)

    def body(*refs):
        q_ref, kc_ref, vc_ref = refs[:3]
        kp_ref, vp_ref = refs[3:5] if has_prev else (None, None)
        b_ref = refs[n_attn - 1]
        o_ref, l_ref = refs[n_in:n_in + 2]
        scr = list(refs[n_in + 2 + n_ride:])
        ls = [scr.pop(0) for _ in range(4)]
        copies = {name: [scr.pop(0) for _ in range(n_sets)] if strided else None
                  for name in ("q", "kc", "vc", "o") + (("kp", "vp") if has_prev else ())}
        if n_ride:
            gather = _Gather(refs[n_attn:n_in], refs[n_in + 2:n_in + 2 + n_ride], scr[3:], *scr[:3])

            @pl.when((pl.program_id(0) == 0) & (pl.program_id(1) == 0))
            def _():
                gather.begin()
        lane = lax.broadcasted_iota(jnp.int32, (QB, 128), 1)
        refs_of = {"q": q_ref, "kc": kc_ref, "vc": vc_ref, "kp": kp_ref, "vp": vp_ref}
        for chunk in _unit_chunks(dil):
            rows_of = {h: {name: _residue_rows(refs_of[name], copies[name], h, residue)
                           for name in refs_of if refs_of[name] is not None}
                       for h in sorted({h for h, _ in chunk})}

            def batch(name):
                return jnp.stack([rows_of[h][name](r) for h, r in chunk])

            q, k, v = batch("q"), batch("kc"), batch("vc")
            if has_prev:
                k = jnp.concatenate([batch("kp"), k], axis=1)
                v = jnp.concatenate([batch("vp"), v], axis=1)
                bias_b = jnp.stack([b_ref[h] for h, _ in chunk])
            else:
                bias_b = jnp.stack([b_ref[h, :, QB:] for h, _ in chunk])
            s = jnp.einsum("uqd,ukd->uqk", q, k, preferred_element_type=F32) * SCALE + bias_b
            m = jnp.max(s, axis=-1, keepdims=True)
            p = jnp.exp(s - m)
            l = jnp.sum(p, axis=-1, keepdims=True)
            o = jnp.einsum("uqk,ukd->uqd", p.astype(BF16), v, preferred_element_type=F32) / l
            lse = m + jnp.log(l)
            for i, (h, r) in enumerate(chunk):
                if strided:
                    copies["o"][h % n_sets][residue(r), :] = o[i]
                else:
                    o_ref[:, h * HD:(h + 1) * HD] = o[i]
                ls[h][r * QB:(r + 1) * QB, :] = jnp.where(lane == h, lse[i], 0.0)
            if strided:
                for h in sorted({h for h, _ in chunk}):
                    o_ref[:, h * HD:(h + 1) * HD] = copies["o"][h % n_sets][...]
        for r in range(dil):
            blk = slice(r * QB, (r + 1) * QB)
            l_ref[residue(r), :] = (ls[0][blk, :] + ls[1][blk, :]) + (ls[2][blk, :] + ls[3][blk, :])
        if n_ride:
            @pl.when((pl.program_id(0) == BL - 1) & (pl.program_id(1) == nsb - 1))
            def _():
                gather.finish()

    def row(b, n):
        return b * nsb + n

    def prev(b, n):
        return b * nsb + jnp.maximum(n - 1, 0)

    in_specs = [
        pl.BlockSpec((rows, GW), lambda b, n: (row(b, n), CB_Q + g)),
        pl.BlockSpec((rows, GW), lambda b, n: (row(b, n), CB_K + g)),
        pl.BlockSpec((rows, GW), lambda b, n: (row(b, n), CB_V + g)),
    ]
    args = [proj, proj, proj]
    scratch = [pltpu.VMEM((rows, 128), F32)] * (4 + n_copied)
    if has_prev:
        in_specs += [pl.BlockSpec((rows, GW), lambda b, n: (prev(b, n), CB_K + g)),
                     pl.BlockSpec((rows, GW), lambda b, n: (prev(b, n), CB_V + g))]
        args += [proj, proj]
    in_specs.append(pl.BlockSpec((None, None, 4, QB, 2 * QB),
                                 lambda b, n: (g, jnp.minimum(n, 1), 0, 0, 0)))
    args.append(bias)
    any_spec = pl.BlockSpec(memory_space=pl.ANY)
    return pl.pallas_call(
        body, name=f"attn_fwd{g}",
        grid=(BL, nsb),
        in_specs=in_specs + [any_spec] * n_ride,
        out_specs=(pl.BlockSpec((rows, GW), lambda b, n: (row(b, n), 0)),
                   pl.BlockSpec((rows, 128), lambda b, n: (row(b, n), 0))) + (any_spec,) * n_ride,
        out_shape=(jax.ShapeDtypeStruct((T, GW), F32), jax.ShapeDtypeStruct((T, 128), F32)) + tuple(ride_shapes),
        scratch_shapes=scratch + (_Gather.scratch(ride_arrs) if n_ride else []),
        compiler_params=pltpu.CompilerParams(vmem_limit_bytes=VMEM_LIMIT),
    )(*args, *ride_arrs)


def _attn_bwd(proj, d_out, stats, bias, dproj, g):
    dil = DILATIONS[g]
    rows = QB * dil
    nsb = S // rows
    has_prev = nsb > 1
    n_steps = nsb + 1 if has_prev else 1
    n_in = 7 + (2 if has_prev else 0)

    def residue(r):
        return pl.ds(r, QB, stride=dil) if dil > 1 else pl.ds(0, QB)

    strided = dil > 1
    n_sets = _scratch_sets(rows)

    def body(*refs):
        q_ref, kc_ref, vc_ref, do_ref, st_ref, b_ref = refs[:6]
        kp_ref, vp_ref = refs[6:8] if has_prev else (None, None)
        out_ref, db_ref = refs[n_in], refs[n_in + 1]
        scr = list(refs[n_in + 2:])
        sq, sk, sv, sems = [scr.pop(0) for _ in range(4)]
        carry = scr.pop(0) if has_prev else None
        sts = scr.pop(0) if strided else st_ref
        copies = {name: [scr.pop(0) for _ in range(n_sets)] if strided else None
                  for name in ("q", "kc", "vc", "do", "dq", "dk", "dv") + (("kp", "vp") if has_prev else ())}
        b, n = pl.program_id(0), pl.program_id(1)

        @pl.when((b == 0) & (n == 0))
        def _():
            db_ref[...] = jnp.zeros_like(db_ref)

        def finish(h, r, dq, dk, dv):
            if strided:
                for name, val in (("dq", dq), ("dk", dk), ("dv", dv)):
                    copies[name][h % n_sets][residue(r), :] = val
            else:
                sl = slice(h * HD, (h + 1) * HD)
                sq[:, sl], sk[:, sl], sv[:, sl] = dq.astype(BF16), dk.astype(BF16), dv.astype(BF16)

        def finish_head(h):
            if strided:
                sl = slice(h * HD, (h + 1) * HD)
                sq[:, sl] = copies["dq"][h % n_sets][...].astype(BF16)
                sk[:, sl] = copies["dk"][h % n_sets][...].astype(BF16)
                sv[:, sl] = copies["dv"][h % n_sets][...].astype(BF16)

        def write_block(blk_idx):
            row0 = pl.multiple_of(blk_idx * rows, rows)
            _write_columns([(sq, CB * (CB_Q + g)), (sk, CB * (CB_K + g)), (sv, CB * (CB_V + g))],
                           out_ref, row0, sems)

        def carried(h, r):
            blk = slice(r * QB, (r + 1) * QB)
            return ((blk, slice(h * HD, (h + 1) * HD)), (blk, slice(GW + h * HD, GW + (h + 1) * HD)),
                    (blk, slice(2 * GW + h * HD, 2 * GW + (h + 1) * HD)))

        if has_prev:
            @pl.when(n == 0)
            def _():
                carry[...] = jnp.zeros_like(carry)

            @pl.when(n == nsb)
            def _():
                for h in range(4):
                    for r in range(dil):
                        cq, ck, cv = carried(h, r)
                        finish(h, r, carry[cq], carry[ck], carry[cv])
                    finish_head(h)
                write_block(b * nsb + nsb - 1)

        @pl.when(n < nsb)
        def _():
            if strided:
                for r in range(dil):
                    sts[r * QB:(r + 1) * QB, :] = st_ref[residue(r), :]
            refs_of = {"q": q_ref, "kc": kc_ref, "vc": vc_ref, "do": do_ref, "kp": kp_ref, "vp": vp_ref}
            for chunk in _unit_chunks(dil):
                heads = sorted({h for h, _ in chunk})
                rows_of = {h: {name: _residue_rows(refs_of[name], copies[name], h, residue)
                               for name in refs_of if refs_of[name] is not None}
                           for h in heads}

                def batch(name):
                    return jnp.stack([rows_of[h][name](r) for h, r in chunk])

                q, k, v, do = batch("q"), batch("kc"), batch("vc"), batch("do")
                if has_prev:
                    k = jnp.concatenate([batch("kp"), k], axis=1)
                    v = jnp.concatenate([batch("vp"), v], axis=1)
                    bias_b = jnp.stack([b_ref[h] for h, _ in chunk])
                else:
                    bias_b = jnp.stack([b_ref[h, :, QB:] for h, _ in chunk])
                lse = jnp.stack([sts[r * QB:(r + 1) * QB, h:h + 1] for h, r in chunk])
                delta = jnp.stack([sts[r * QB:(r + 1) * QB, 4 + h:5 + h] for h, r in chunk])
                s = jnp.einsum("uqd,ukd->uqk", q, k, preferred_element_type=F32) * SCALE + bias_b
                p = jnp.exp(s - lse)
                ds = p * (jnp.einsum("uqd,ukd->uqk", do, v, preferred_element_type=F32) - delta)
                for h in heads:
                    mine = [ds[i] for i, (hh, _) in enumerate(chunk) if hh == h]
                    tot = mine[0]
                    for extra in mine[1:]:
                        tot = tot + extra
                    if has_prev:
                        db_ref[h] += tot
                    else:
                        db_ref[h, :, QB:] += tot
                dsb, pb = ds.astype(BF16), p.astype(BF16)
                dq = jnp.einsum("uqk,ukd->uqd", dsb, k, preferred_element_type=F32) * SCALE
                dk = jnp.einsum("uqk,uqd->ukd", dsb, q, preferred_element_type=F32) * SCALE
                dv = jnp.einsum("uqk,uqd->ukd", pb, do, preferred_element_type=F32)
                for i, (h, r) in enumerate(chunk):
                    if has_prev:
                        cq, ck, cv = carried(h, r)
                        finish(h, r, carry[cq], carry[ck] + dk[i, :QB], carry[cv] + dv[i, :QB])
                        carry[cq] = dq[i]
                        carry[ck] = dk[i, QB:]
                        carry[cv] = dv[i, QB:]
                    else:
                        finish(h, r, dq[i], dk[i], dv[i])
                for h in heads:
                    finish_head(h)
            if has_prev:
                @pl.when(n > 0)
                def _():
                    write_block(b * nsb + n - 1)
            else:
                write_block(b)

    def row(b, n):
        return b * nsb + jnp.minimum(n, nsb - 1)

    def prev(b, n):
        return b * nsb + jnp.maximum(jnp.minimum(n, nsb - 1) - 1, 0)

    in_specs = [
        pl.BlockSpec((rows, GW), lambda b, n: (row(b, n), CB_Q + g)),
        pl.BlockSpec((rows, GW), lambda b, n: (row(b, n), CB_K + g)),
        pl.BlockSpec((rows, GW), lambda b, n: (row(b, n), CB_V + g)),
        pl.BlockSpec((rows, GW), lambda b, n: (row(b, n), 0)),
        pl.BlockSpec((rows, 128), lambda b, n: (row(b, n), 0)),
        pl.BlockSpec((None, None, 4, QB, 2 * QB),
                     lambda b, n: (g, jnp.minimum(jnp.minimum(n, nsb - 1), 1), 0, 0, 0)),
    ]
    args = [proj, proj, proj, d_out, stats, bias]
    scratch = [pltpu.VMEM((rows, GW), BF16)] * 3 + [pltpu.SemaphoreType.DMA((3,))]
    if has_prev:
        in_specs += [pl.BlockSpec((rows, GW), lambda b, n: (prev(b, n), CB_K + g)),
                     pl.BlockSpec((rows, GW), lambda b, n: (prev(b, n), CB_V + g))]
        args += [proj, proj]
        scratch.append(pltpu.VMEM((rows, 3 * GW), F32))
    if strided:
        n_copied = (7 + (2 if has_prev else 0)) * n_sets
        scratch += [pltpu.VMEM((rows, 128), F32)] * (1 + n_copied)
    in_specs.append(pl.BlockSpec(memory_space=pl.ANY))
    args.append(dproj)
    return pl.pallas_call(
        body, name=f"attn_bwd{g}",
        grid=(BL, n_steps),
        in_specs=in_specs,
        out_specs=(pl.BlockSpec(memory_space=pl.ANY),
                   pl.BlockSpec((4, QB, 2 * QB), lambda b, n: (0, 0, 0))),
        out_shape=(jax.ShapeDtypeStruct((T, NCOL), BF16),
                   jax.ShapeDtypeStruct((4, QB, 2 * QB), F32)),
        scratch_shapes=scratch,
        input_output_aliases={len(args) - 1: 0},
        compiler_params=pltpu.CompilerParams(vmem_limit_bytes=VMEM_LIMIT),
    )(*args)


def _tail(x2, tgt2, mod3, o_g, lse_g, proj, w_ao, w_co, w_o, conv_w, conv_b, ln_g, ln_b):
    tm = 256
    per_seq = S // tm
    halo = 16

    def body(x_ref, t_ref, mod_ref, o1_ref, o2_ref, o3_ref, l1_ref, l2_ref, l3_ref,
             ga_ref, u_ref, bg_ref, cg_ref, gc_ref, ma_ref, mc_ref, up_ref, cp_ref,
             wao_ref, wco_ref, wo_ref, cw_ref, cb_ref, lg_ref, lb_ref,
             dproj_ref, dyc_ref, do_ref, st_ref, dxd_ref,
             mg_ref, dy_ref, ain_ref, dao_ref, sin_ref, dso_ref, vec_ref,
             dga_s, dbg_s, dgm_s, sems):
        i = pl.program_id(0)
        bidx = i // per_seq
        first = (i % per_seq) == 0

        @pl.when(i == 0)
        def _():
            vec_ref[...] = jnp.zeros_like(vec_ref)

        l1, l2, l3 = l1_ref[...], l2_ref[...], l3_ref[...]
        mx = jnp.maximum(jnp.maximum(l1, l2), l3)
        e1, e2, e3 = jnp.exp(l1 - mx), jnp.exp(l2 - mx), jnp.exp(l3 - mx)
        esum = e1 + e2 + e3
        lse_tot = mx + jnp.log(esum)
        w1, w2, w3 = e1 / esum, e2 / esum, e3 / esum

        def per_head(wv):
            return jnp.concatenate([jnp.broadcast_to(wv[:, h:h + 1], (tm, HD)) for h in range(4)], axis=1)

        o = per_head(w1) * o1_ref[...] + per_head(w2) * o2_ref[...] + per_head(w3) * o3_ref[...]

        ga = ga_ref[...].astype(F32)
        sig_ga = _sigmoid(ga)
        silu_ga = ga * sig_ga
        a_in = (o * silu_ga).astype(BF16)
        a_out = _dot(a_in, wao_ref[...])

        u = u_ref[...].astype(F32)
        cg = cg_ref[...].astype(F32)
        z = cg * u
        zp = cp_ref[...].astype(F32) * up_ref[...].astype(F32)
        zp = jnp.where(first, 0.0, zp)
        zcat = jnp.concatenate([zp, z], axis=0)
        z1 = pltpu.roll(zcat, 1, 0)[halo:]
        z2 = pltpu.roll(zcat, 2, 0)[halo:]
        y_conv = cw_ref[0:1, :] * z2 + cw_ref[1:2, :] * z1 + cw_ref[2:3, :] * z + cb_ref[...]
        gc = gc_ref[...].astype(F32)
        sig_gc = _sigmoid(gc)
        silu_gc = gc * sig_gc
        bg = bg_ref[...].astype(F32)
        s_in = (bg * y_conv * silu_gc).astype(BF16)
        s_out = _dot(s_in, wco_ref[...])

        sa = _sigmoid(ma_ref[...].astype(F32))
        sc = _sigmoid(mc_ref[...].astype(F32))
        merged = (sa * a_out + sc * s_out).astype(BF16)
        y = _dot(merged, wo_ref[...])
        gate1 = 1.0 + mod_ref[0, 2:3, :]
        xv = x_ref[...]
        resid = ALPHA * xv + gate1 * y
        mu = jnp.mean(resid, axis=1, keepdims=True)
        xc = resid - mu
        var = jnp.mean(xc * xc, axis=1, keepdims=True)
        rstd = lax.rsqrt(var + LN_EPS)
        xhat = xc * rstd
        lg = lg_ref[...]
        err = xhat * lg + lb_ref[...] - t_ref[...]
        vec_ref[3:4, :] += (0.5 / D) * jnp.sum(err * err, axis=0, keepdims=True)

        dout = err * (1.0 / D)
        vec_ref[1:2, :] += jnp.sum(dout * xhat, axis=0, keepdims=True)
        vec_ref[2:3, :] += jnp.sum(dout, axis=0, keepdims=True)
        dxh = dout * lg
        dres = rstd * (dxh - jnp.mean(dxh, axis=1, keepdims=True)
                       - xhat * jnp.mean(dxh * xhat, axis=1, keepdims=True))
        dxd_ref[...] = ALPHA * dres
        dgate = jnp.sum(dres * y, axis=0, keepdims=True)
        vec_ref[4:5, :] += jnp.where(bidx == 0, dgate, 0.0)
        vec_ref[5:6, :] += jnp.where(bidx == 1, dgate, 0.0)
        dy = (dres * gate1).astype(BF16)

        dmerged = _dot_nt(dy, wo_ref[...])
        da_out = (dmerged * sa).astype(BF16)
        ds_out = (dmerged * sc).astype(BF16)
        dgm_s[:, 2 * D:3 * D] =(dmerged * s_out * sc * (1.0 - sc)).astype(BF16)
        dgm_s[:, D:2 * D] =(dmerged * a_out * sa * (1.0 - sa)).astype(BF16)
        da_in = _dot_nt(da_out, wao_ref[...])
        ds_in = _dot_nt(ds_out, wco_ref[...])

        d_o = da_in * silu_ga
        do_ref[...] = d_o.astype(BF16)
        dga_s[...] =(da_in * o * (sig_ga * (1.0 + ga * (1.0 - sig_ga)))).astype(BF16)
        lane = lax.broadcasted_iota(jnp.int32, (tm, 128), 1)
        stats = lse_tot
        od = o * d_o
        for h in range(4):
            delta = jnp.sum(od[:, h * HD:(h + 1) * HD], axis=1, keepdims=True)
            stats = jnp.where(lane == 4 + h, delta, stats)
        st_ref[...] = stats

        dbg_s[...] =(ds_in * y_conv * silu_gc).astype(BF16)
        dyc = ds_in * bg * silu_gc
        dyc_ref[...] = dyc
        vec_ref[0:1, :] += jnp.sum(dyc, axis=0, keepdims=True)
        dgm_s[:, 0:D] =(ds_in * bg * y_conv * (sig_gc * (1.0 + gc * (1.0 - sig_gc)))).astype(BF16)

        mg_ref[...] = merged
        dy_ref[...] = dy
        ain_ref[...] = a_in
        dao_ref[...] = da_out
        sin_ref[...] = s_in
        dso_ref[...] = ds_out
        _write_columns([(dga_s, CB * CB_GA), (dbg_s, D * KB_BG), (dgm_s, D * KB_GC)],
                       dproj_ref, pl.multiple_of(i * tm, tm), sems)

    def tile(width, cblk=0):
        return pl.BlockSpec((tm, width), lambda i: (i, cblk))

    def whole(shape):
        return pl.BlockSpec(shape, lambda i: tuple(0 for _ in shape))

    prev_rows = lambda i: (jnp.maximum(i * (tm // halo) - 1, 0),)
    in_specs = [
        tile(D), tile(D), pl.BlockSpec((1, 3, D), lambda i: (i // per_seq, 0, 0)),
        tile(GW), tile(GW), tile(GW), tile(128), tile(128), tile(128),
        tile(GW, CB_GA), tile(D, KB_U), tile(D, KB_BG), tile(D, KB_CG), tile(D, KB_GC),
        tile(D, KB_MA), tile(D, KB_MC),
        pl.BlockSpec((halo, D), lambda i: (*prev_rows(i), KB_U)),
        pl.BlockSpec((halo, D), lambda i: (*prev_rows(i), KB_CG)),
        whole((GW, D)), whole((D, D)), whole((D, D)),
        whole((3, D)), whole((1, D)), whole((1, D)), whole((1, D)),
    ]
    out_specs = (
        pl.BlockSpec(memory_space=pl.ANY), tile(D), tile(GW), tile(128), tile(D),
        tile(D), tile(D), tile(GW), tile(D), tile(D), tile(D),
        pl.BlockSpec((8, D), lambda i: (0, 0)),
    )
    out_shape = (
        jax.ShapeDtypeStruct((T, NCOL), BF16),
        jax.ShapeDtypeStruct((T, D), F32),
        jax.ShapeDtypeStruct((T, GW), BF16),
        jax.ShapeDtypeStruct((T, 128), F32),
        jax.ShapeDtypeStruct((T, D), F32),
        jax.ShapeDtypeStruct((T, D), BF16),
        jax.ShapeDtypeStruct((T, D), BF16),
        jax.ShapeDtypeStruct((T, GW), BF16),
        jax.ShapeDtypeStruct((T, D), BF16),
        jax.ShapeDtypeStruct((T, D), BF16),
        jax.ShapeDtypeStruct((T, D), BF16),
        jax.ShapeDtypeStruct((8, D), F32),
    )
    return pl.pallas_call(
        body, name="tail",
        grid=(T // tm,),
        in_specs=in_specs, out_specs=out_specs, out_shape=out_shape,
        scratch_shapes=[pltpu.VMEM((tm, GW), BF16), pltpu.VMEM((tm, D), BF16), pltpu.VMEM((tm, 3 * D), BF16),
                        pltpu.SemaphoreType.DMA((3,))],
        compiler_params=pltpu.CompilerParams(vmem_limit_bytes=VMEM_LIMIT),
    )(x2, tgt2, mod3, *o_g, *lse_g, proj, proj, proj, proj, proj, proj, proj, proj, proj,
      w_ao, w_co, w_o, conv_w, conv_b, ln_g, ln_b)


def _conv_bwd(dyc, proj, conv_w, dproj):
    tm = 512
    per_seq = S // tm
    halo = 16

    def body(d_ref, dn_ref, u_ref, c_ref, up_ref, cp_ref, cw_ref, _, dproj_ref, g_ref, du_s, dc_s, sems):
        i = pl.program_id(0)
        first = (i % per_seq) == 0
        last = (i % per_seq) == per_seq - 1

        @pl.when(i == 0)
        def _():
            g_ref[...] = jnp.zeros_like(g_ref)

        d = d_ref[...]
        dn = jnp.where(last, 0.0, dn_ref[...])
        dcat = jnp.concatenate([d, dn], axis=0)
        d1 = pltpu.roll(dcat, tm + 8 - 1, 0)[:tm]
        d2 = pltpu.roll(dcat, tm + 8 - 2, 0)[:tm]
        dz = cw_ref[2:3, :] * d + cw_ref[1:2, :] * d1 + cw_ref[0:1, :] * d2
        u = u_ref[...].astype(F32)
        cg = c_ref[...].astype(F32)
        du_s[...] = (dz * cg).astype(BF16)
        dc_s[...] = (dz * u).astype(BF16)
        _write_columns([(du_s, D * KB_U), (dc_s, D * KB_CG)], dproj_ref, pl.multiple_of(i * tm, tm), sems)

        z = cg * u
        zp = jnp.where(first, 0.0, cp_ref[...].astype(F32) * up_ref[...].astype(F32))
        zcat = jnp.concatenate([zp, z], axis=0)
        z1 = pltpu.roll(zcat, 1, 0)[halo:]
        z2 = pltpu.roll(zcat, 2, 0)[halo:]
        g_ref[0:1, :] += jnp.sum(d * z2, axis=0, keepdims=True)
        g_ref[1:2, :] += jnp.sum(d * z1, axis=0, keepdims=True)
        g_ref[2:3, :] += jnp.sum(d * z, axis=0, keepdims=True)

    n_tiles = T // tm
    prev_rows = lambda i: jnp.maximum(i * (tm // halo) - 1, 0)
    next_rows = lambda i: jnp.minimum((i + 1) * (tm // 8), T // 8 - 1)
    return pl.pallas_call(
        body, name="conv_bwd",
        grid=(n_tiles,),
        in_specs=[pl.BlockSpec((tm, D), lambda i: (i, 0)),
                  pl.BlockSpec((8, D), lambda i: (next_rows(i), 0)),
                  pl.BlockSpec((tm, D), lambda i: (i, KB_U)),
                  pl.BlockSpec((tm, D), lambda i: (i, KB_CG)),
                  pl.BlockSpec((halo, D), lambda i: (prev_rows(i), KB_U)),
                  pl.BlockSpec((halo, D), lambda i: (prev_rows(i), KB_CG)),
                  pl.BlockSpec((3, D), lambda i: (0, 0)),
                  pl.BlockSpec(memory_space=pl.ANY)],
        out_specs=(pl.BlockSpec(memory_space=pl.ANY),
                   pl.BlockSpec((8, D), lambda i: (0, 0))),
        out_shape=(jax.ShapeDtypeStruct((T, NCOL), BF16),
                   jax.ShapeDtypeStruct((8, D), F32)),
        scratch_shapes=[pltpu.VMEM((tm, D), BF16), pltpu.VMEM((tm, D), BF16), pltpu.SemaphoreType.DMA((2,))],
        input_output_aliases={7: 0},
        compiler_params=pltpu.CompilerParams(vmem_limit_bytes=VMEM_LIMIT),
    )(dyc, dyc, proj, proj, proj, proj, conv_w, dproj)


def _dh_dx(dproj, w_in_all, x2, dxd, mod3, chip_sums, hops=(), parts0=None):
    tm = 1024
    per_seq = S // tm
    n = len(chip_sums)
    n_in = 5 + n + (0 if parts0 is None else 1)

    def body(*refs):
        d_ref, w_ref, x_ref, dxd_ref, mod_ref = refs[:5]
        ins = refs[5:5 + n]
        gx_ref, vec_ref = refs[n_in:n_in + 2]
        outs = refs[n_in + 2:n_in + 2 + n]
        acc, send_sems, recv_sems, local_sems = refs[n_in + 2 + n:]
        i, jj = pl.program_id(0), pl.program_id(1)

        @pl.when((i == 0) & (jj == 0))
        def _():
            vec_ref[...] = jnp.zeros_like(vec_ref)
            if n:
                sends, _, mine = _chip_copies(ins, outs, send_sems, recv_sems, local_sems, hops)
                for cp in sends + mine:
                    cp.start()

        if n:
            @pl.when((i == T // tm - 1) & (jj == N_DEV - 1))
            def _():
                sends, arrivals, mine = _chip_copies(ins, outs, send_sems, recv_sems, local_sems, hops)
                for cp in arrivals:
                    cp.wait_recv()
                for cp in sends:
                    cp.wait_send()
                for cp in mine:
                    cp.wait()

        @pl.when(jj == 0)
        def _():
            acc[...] = jnp.zeros_like(acc)

        acc[...] += _dot_nt(d_ref[...], w_ref[...])

        @pl.when(jj == N_DEV - 1)
        def _():
            dh = acc[...]
            bidx = i // per_seq
            gx_ref[...] = dxd_ref[...] + dh * (1.0 + mod_ref[0, 1:2, :])
            dshift = jnp.sum(dh, axis=0, keepdims=True)
            dscale = jnp.sum(dh * x_ref[...], axis=0, keepdims=True)
            vec_ref[0:1, :] += jnp.where(bidx == 0, dshift, 0.0)
            vec_ref[1:2, :] += jnp.where(bidx == 1, dshift, 0.0)
            vec_ref[2:3, :] += jnp.where(bidx == 0, dscale, 0.0)
            vec_ref[3:4, :] += jnp.where(bidx == 1, dscale, 0.0)

    any_spec = pl.BlockSpec(memory_space=pl.ANY)
    res = pl.pallas_call(
        body, name="dh_dx",
        grid=(T // tm, N_DEV),
        in_specs=[
            pl.BlockSpec((tm, SHARD), lambda i, jj: (i, jj)),
            pl.BlockSpec((None, D, SHARD), lambda i, jj: (jj, 0, 0)),
            pl.BlockSpec((tm, D), lambda i, jj: (i, 0)),
            pl.BlockSpec((tm, D), lambda i, jj: (i, 0)),
            pl.BlockSpec((1, 3, D), lambda i, jj: (i // per_seq, 0, 0))] + [any_spec] * (n_in - 5),
        out_specs=(pl.BlockSpec((tm, D), lambda i, jj: (i, 0)),
                   pl.BlockSpec((8, D), lambda i, jj: (0, 0))) + (any_spec,) * n,
        out_shape=(jax.ShapeDtypeStruct((T, D), F32), jax.ShapeDtypeStruct((8, D), F32))
                  + tuple(jax.ShapeDtypeStruct(a.shape, a.dtype) for a in chip_sums),
        scratch_shapes=[pltpu.VMEM((tm, D), F32), pltpu.SemaphoreType.DMA((max(3 * n, 1),)),
                        pltpu.SemaphoreType.DMA((max(3 * n, 1),)), pltpu.SemaphoreType.DMA((max(n, 1),))],
        input_output_aliases={} if parts0 is None else {5 + n: 2},
        compiler_params=pltpu.CompilerParams(vmem_limit_bytes=VMEM_LIMIT),
    )(dproj, w_in_all, x2, dxd, mod3, *chip_sums, *([] if parts0 is None else [parts0]))
    return res[0], res[1], res[2:]


def _mm_tn(a, b, tn, blocks_leading, name):
    kk, m = a.shape
    n = b.shape[1]
    tk = 2048

    def body(a_ref, b_ref, o_ref, acc):
        @pl.when(pl.program_id(1) == 0)
        def _():
            acc[...] = jnp.zeros_like(acc)

        acc[...] += _dot_tn(a_ref[...], b_ref[...])

        @pl.when(pl.program_id(1) == kk // tk - 1)
        def _():
            o_ref[...] = acc[...].astype(BF16)

    if blocks_leading:
        out_spec = pl.BlockSpec((None, m, tn), lambda j, k: (j, 0, 0))
        out_shape = jax.ShapeDtypeStruct((n // tn, m, tn), BF16)
    else:
        out_spec = pl.BlockSpec((m, tn), lambda j, k: (0, j))
        out_shape = jax.ShapeDtypeStruct((m, n), BF16)
    return pl.pallas_call(
        body, name=name,
        grid=(n // tn, kk // tk),
        in_specs=[pl.BlockSpec((tk, m), lambda j, k: (k, 0)),
                  pl.BlockSpec((tk, tn), lambda j, k: (k, j))],
        out_specs=out_spec, out_shape=out_shape,
        scratch_shapes=[pltpu.VMEM((m, tn), F32)],
        compiler_params=pltpu.CompilerParams(vmem_limit_bytes=VMEM_LIMIT),
    )(a, b)


def _adamw(parts, w, m, v, name, row_tile=None):
    n_parts, rows, cols = parts.shape
    tr = rows if row_tile is None else row_tile
    c1 = 1.0 - ADAM_B1 ** ADAM_STEP
    c2 = 1.0 - ADAM_B2 ** ADAM_STEP

    def body(p_ref, w_ref, m_ref, v_ref, g_ref, d_ref, nm_ref, nv_ref):
        g = p_ref[0].astype(F32)
        for s in range(1, n_parts):
            g = g + p_ref[s].astype(F32)
        nm = ADAM_B1 * m_ref[...] + (1.0 - ADAM_B1) * g
        nv = ADAM_B2 * v_ref[...] + (1.0 - ADAM_B2) * (g * g)
        m_hat = nm / c1
        v_hat = nv / c2
        g_ref[...] = g
        d_ref[...] = -ADAM_LR * (m_hat / (jnp.sqrt(v_hat) + ADAM_EPS) + ADAM_WD * w_ref[...])
        nm_ref[...] = nm
        nv_ref[...] = nv

    blk = pl.BlockSpec((tr, cols), lambda i: (i, 0))
    shp = jax.ShapeDtypeStruct((rows, cols), F32)
    return pl.pallas_call(
        body, name=name,
        grid=(rows // tr,),
        in_specs=[pl.BlockSpec((n_parts, tr, cols), lambda i: (0, i, 0)), blk, blk, blk],
        out_specs=(blk, blk, blk, blk),
        out_shape=(shp, shp, shp, shp),
        compiler_params=pltpu.CompilerParams(vmem_limit_bytes=VMEM_LIMIT),
    )(parts, w, m, v)


def _loss_sum(rows):
    def body(r_ref, o_ref):
        o_ref[...] = jnp.sum(jnp.sum(r_ref[...], axis=0, keepdims=True), axis=1, keepdims=True)

    return pl.pallas_call(body, name="loss_sum", out_shape=jax.ShapeDtypeStruct((1, 1), F32))(rows)


def _attention_forward(proj, rel_bias, rides=((), (), ())):
    buckets_np, masks_np = _bucket_maps()
    buckets, masks = jnp.asarray(buckets_np), jnp.asarray(masks_np)
    bias = _bias_expand(rel_bias, buckets, masks)
    fwd = [_attn_fwd(proj, bias, g, rides[g]) for g in range(3)]
    brought = [arr for f in fwd for arr in f[2:]]
    return bias, buckets, [f[0] for f in fwd], [f[1] for f in fwd], brought


def _local_step(x2, tgt2, mod3, h, proj, attn, w_ao, w_co, w_o, conv_w, conv_b, ln_g, ln_b):
    bias, buckets, o_g, lse_g = attn

    (dproj, dyc, d_o, stats, dxd, merged, dy, a_in, da_out, s_in, ds_out, tail_vec) = _tail(
        x2, tgt2, mod3, o_g, lse_g, proj, w_ao, w_co, w_o, conv_w, conv_b, ln_g, ln_b)

    dbias = []
    for g in range(3):
        dproj, db = _attn_bwd(proj, d_o, stats, bias, dproj, g)
        dbias.append(db)
    g_rel_bias = _bias_grad(*dbias, buckets)
    dproj, conv_vec = _conv_bwd(dyc, proj, conv_w, dproj)

    gw_o = _mm_tn(merged, dy, D, False, "gw_o")
    gw_co = _mm_tn(s_in, ds_out, D, False, "gw_conv_out")
    gw_ao = _mm_tn(a_in, da_out, D, False, "gw_attn_out")
    gw_ao = jnp.transpose(gw_ao.reshape(GW, N_DEV, D // N_DEV), (1, 0, 2))
    return dproj, dxd, gw_ao, gw_co, gw_o, conv_vec, g_rel_bias, tail_vec


def kernel(x, c, w_ada, b_ada, w_in, conv_w, conv_b, rel_bias, w_attn_out, w_conv_out, w_o, ln_g, ln_b, loss_target, m_w_ada, m_b_ada, m_w_in, m_conv_w, m_conv_b, m_rel_bias, m_w_attn_out, m_w_conv_out, m_w_o, m_ln_g, m_ln_b, v_w_ada, v_b_ada, v_w_in, v_conv_w, v_conv_b, v_rel_bias, v_w_attn_out, v_w_conv_out, v_w_o, v_ln_g, v_ln_b):
    me = _my_index()
    x2 = x.reshape(T, D)
    tgt2 = loss_target.reshape(T, D)

    b_cols = lax.dynamic_slice(b_ada, (0, me * ADA_SHARD), (1, ADA_SHARD))
    c_g, mod_in = _mod_exchange(jnp.pad(c, ((0, 8 - BL), (0, 0))), w_ada[0], b_cols)
    c_all = c_g[:, 0:BL, :].reshape(N_DEV * BL, D)
    mod3 = jnp.transpose(mod_in[:, 0:BL, :], (1, 0, 2)).reshape(BL, 3, D)

    rows_shape = jax.ShapeDtypeStruct((N_DEV, D // N_DEV, D), BF16)
    proj, h, w_in_all, (w_ao_g, w_co_g, w_o_g, conv_w_g) = _gather_proj(
        _shard_order(), x2, mod3, w_in[0].astype(BF16), 1024,
        ([w_attn_out[0].astype(BF16), w_conv_out[0].astype(BF16), w_o[0].astype(BF16), conv_w[0]],
         [jax.ShapeDtypeStruct((N_DEV, GW, D // N_DEV), BF16), rows_shape, rows_shape,
          jax.ShapeDtypeStruct((N_DEV, 3, D // N_DEV), F32)]))

    *attn, _ = _attention_forward(proj, rel_bias)
    w_ao_full = jnp.transpose(w_ao_g, (1, 0, 2)).reshape(GW, D)
    w_co_full = w_co_g.reshape(D, D)
    w_o_full = w_o_g.reshape(D, D)
    conv_w_full = jnp.transpose(conv_w_g, (1, 0, 2)).reshape(3, D)

    (dproj, dxd, gw_ao, gw_co, gw_o, conv_vec, g_rel_bias, tail_vec) = _local_step(
        x2, tgt2, mod3, h, proj, attn, w_ao_full, w_co_full, w_o_full,
        conv_w_full, conv_b, ln_g, ln_b)

    g_conv_w_blocks = jnp.transpose(conv_vec[0:3].reshape(3, N_DEV, D // N_DEV), (1, 0, 2))
    partials = [gw_ao, gw_co.reshape(N_DEV, D // N_DEV, D), gw_o.reshape(N_DEV, D // N_DEV, D), g_conv_w_blocks]
    w_in_sums, w_in_parts, sib = _gw_in_pair(
        _slice_order(), h, dproj, partials,
        [jax.ShapeDtypeStruct((4, GW, D // N_DEV), BF16),
         jax.ShapeDtypeStruct((4, D // N_DEV, D), BF16),
         jax.ShapeDtypeStruct((4, D // N_DEV, D), BF16),
         jax.ShapeDtypeStruct((4, 3, D // N_DEV), F32)])
    names = ["w_attn_out", "w_conv_out", "w_o", "conv_w"]
    core = lax.axis_index("c").astype(jnp.int32).reshape(1)
    chip_sums = [w_in_sums] + [_pair_add(core, partials[a], sib[a], None, "pair_add_" + names[a])
                               for a in range(4)]
    hops = [(3,)] + [(1, 2, 3)] * 4
    grad_x, mod_vec, (r_in, r_ao, r_co, r_o, r_cw) = _dh_dx(
        dproj, w_in_all, x2, dxd, mod3, chip_sums, hops, w_in_parts)

    small = jnp.concatenate([
        tail_vec[0:4],
        jnp.pad(g_rel_bias.reshape(1, N_BUCKETS * N_HEADS), ((0, 0), (0, D - N_BUCKETS * N_HEADS))),
        jnp.zeros((3, D), F32)], axis=0)
    dmod = jnp.concatenate([mod_vec[0:2], mod_vec[2:4], tail_vec[4:6]], axis=1)
    small_g, dmod_g = _all_gather(
        [small, dmod],
        [jax.ShapeDtypeStruct((N_DEV, 8, D), F32), jax.ShapeDtypeStruct((N_DEV, BL, 3 * D), F32)],
        "gather_small")
    dmod_all = dmod_g.reshape(N_DEV * BL, 3 * D)
    loss = _loss_sum(small_g[:, 3, :]).reshape(())
    g_w_ada = _ada_bwd(jnp.transpose(c_all), lax.dynamic_slice(dmod_all, (0, me * ADA_SHARD),
                                                               (N_DEV * BL, ADA_SHARD)))

    def upd(parts, w, m, v, name, row_tile=None):
        shape = w.shape
        w2, m2, v2 = (t.reshape(parts.shape[1:]) for t in (w, m, v))
        return tuple(t.reshape(shape) for t in _adamw(parts, w2, m2, v2, name, row_tile))

    res = {
        "w_ada": upd(g_w_ada[None], w_ada, m_w_ada, v_w_ada, "adam_w_ada", 256),
        "b_ada": upd(dmod_all[:, None, :], b_ada, m_b_ada, v_b_ada, "adam_b_ada"),
        "w_in": upd(r_in, w_in, m_w_in, v_w_in, "adam_w_in", 128),
        "conv_w": upd(r_cw, conv_w, m_conv_w, v_conv_w, "adam_conv_w"),
        "conv_b": upd(small_g[:, 0:1, :], conv_b, m_conv_b, v_conv_b, "adam_conv_b"),
        "rel_bias": upd(small_g[:, 4, :N_BUCKETS * N_HEADS].reshape(N_DEV, N_BUCKETS, N_HEADS),
                        rel_bias, m_rel_bias, v_rel_bias, "adam_rel_bias"),
        "w_attn_out": upd(r_ao, w_attn_out, m_w_attn_out, v_w_attn_out, "adam_w_attn_out"),
        "w_conv_out": upd(r_co, w_conv_out, m_w_conv_out, v_w_conv_out, "adam_w_conv_out"),
        "w_o": upd(r_o, w_o, m_w_o, v_w_o, "adam_w_o"),
        "ln_g": upd(small_g[:, 1:2, :], ln_g, m_ln_g, v_ln_g, "adam_ln_g"),
        "ln_b": upd(small_g[:, 2:3, :], ln_b, m_ln_b, v_ln_b, "adam_ln_b"),
    }
    order = ["w_ada", "b_ada", "w_in", "conv_w", "conv_b", "rel_bias", "w_attn_out", "w_conv_out",
             "w_o", "ln_g", "ln_b"]
    outs = [loss, grad_x.reshape(BL, S, D)]
    for k in range(4):
        outs += [res[name][k] for name in order]
    return tuple(outs)
```

```python
import functools
import math

import numpy as np
import jax
import jax.numpy as jnp
from jax import lax
from jax.experimental import pallas as pl
from jax.experimental.pallas import tpu as pltpu

F32 = jnp.float32
BF16 = jnp.bfloat16
MESH = pl.DeviceIdType.MESH

N_DEV = 8
D = 1024
S = 2048
BL = 2
T = BL * S
NCOL = 11264
SHARD = NCOL // N_DEV
CB = 512
NCB = NCOL // CB
HD = 128
GW = 512
QB = 128
DILATIONS = (1, 4, 16)
N_STEPS = 128
N_BUCKETS = 32
N_HEADS = 12
ALPHA = 2.0 ** 0.25
LN_EPS = 1e-5
NEG_INF = -1e30
SCALE = HD ** -0.5
ADA_SHARD = 3 * D // N_DEV

CB_Q, CB_K, CB_V, CB_GA = 0, 3, 6, 9
KB_U, KB_BG, KB_CG, KB_GC, KB_MA, KB_MC = 5, 6, 7, 8, 9, 10

ADAM_LR, ADAM_B1, ADAM_B2, ADAM_EPS, ADAM_WD, ADAM_STEP = 0.001, 0.9, 0.999, 1e-08, 0.01, 10

VMEM_LIMIT = 56 * 1024 * 1024


def _dot(a, b):
    return jnp.dot(a, b, preferred_element_type=F32)


def _dot_nt(a, b):
    return lax.dot_general(a, b, (((1,), (1,)), ((), ())), preferred_element_type=F32)


def _dot_tn(a, b):
    return lax.dot_general(a, b, (((0,), (0,)), ((), ())), preferred_element_type=F32)


def _sigmoid(v):
    return 1.0 / (1.0 + jnp.exp(-v))


def _write_columns(pieces, dst_hbm, row0, sems):
    copies = []
    for k, (src, col0) in enumerate(pieces):
        rows, width = src.shape
        copies.append(pltpu.make_async_copy(
            src, dst_hbm.at[pl.ds(row0, rows), pl.ds(col0, width)], sems.at[k]))
    for cp in copies:
        cp.start()
    for cp in copies:
        cp.wait()


def _my_index():
    return 4 * lax.axis_index("x") + 2 * lax.axis_index("y") + lax.axis_index("c")


class _Gather:
    def __init__(self, ins, outs, stage, send_sems, recv_sems, local_sems):
        self.ins, self.outs, self.stage = ins, outs, stage
        self.send_sems, self.recv_sems, self.local_sems = send_sems, recv_sems, local_sems
        x, y, c = lax.axis_index("x"), lax.axis_index("y"), lax.axis_index("c")
        self.c = c
        self.me, self.sibling = (x, y, c), (x, y, 1 - c)
        self.chips = [(1 - x, y), (x, 1 - y), (1 - x, 1 - y)]

    @staticmethod
    def scratch(arrs):
        n = len(arrs)
        return ([pltpu.SemaphoreType.DMA((7 * n,)), pltpu.SemaphoreType.DMA((7 * n,)),
                 pltpu.SemaphoreType.DMA((n,))] + [pltpu.VMEM(a.shape, a.dtype) for a in arrs])

    def _copy(self, a, k, block, to, src=None):
        dst = self.outs[a].at[4 * block[0] + 2 * block[1] + block[2]]
        return pltpu.make_async_remote_copy(
            src_ref=dst if src is None else src, dst_ref=dst,
            send_sem=self.send_sems.at[a * 7 + k], recv_sem=self.recv_sems.at[a * 7 + k],
            device_id=to, device_id_type=MESH)

    def _first(self):
        first = []
        for a in range(len(self.ins)):
            first.append(self._copy(a, 0, self.me, self.sibling, src=self.ins[a]))
            first += [self._copy(a, 1 + j, self.me, (*chip, self.c), src=self.ins[a])
                      for j, chip in enumerate(self.chips)]
        return first

    def _mine(self):
        me = self.me
        return [pltpu.make_async_copy(self.stage[a], self.outs[a].at[4 * me[0] + 2 * me[1] + me[2]],
                                      self.local_sems.at[a]) for a in range(len(self.ins))]

    def begin(self):
        for cp in self._first():
            cp.start()
        loads = [pltpu.make_async_copy(self.ins[a], self.stage[a], self.local_sems.at[a])
                 for a in range(len(self.ins))]
        for cp in loads:
            cp.start()
        for cp in loads:
            cp.wait()
        for cp in self._mine():
            cp.start()

    def finish(self):
        n, c, me, sibling = len(self.ins), self.c, self.me, self.sibling
        passed = []
        for j, chip in enumerate(self.chips):
            for a in range(n):
                self._copy(a, 1 + j, (*chip, c), me).wait_recv()
                fwd = self._copy(a, 4 + j, (*chip, c), sibling)
                fwd.start()
                passed.append(fwd)
        for a in range(n):
            self._copy(a, 0, sibling, me).wait_recv()
        for j, chip in enumerate(self.chips):
            for a in range(n):
                self._copy(a, 4 + j, (*chip, 1 - c), me).wait_recv()
        for cp in self._first() + passed:
            cp.wait_send()
        for cp in self._mine():
            cp.wait()


def _all_gather(arrs, out_shapes, name):
    n = len(arrs)

    def body(*refs):
        g = _Gather(refs[:n], refs[n:2 * n], refs[2 * n + 3:], *refs[2 * n:2 * n + 3])
        g.begin()
        g.finish()

    any_spec = pl.BlockSpec(memory_space=pl.ANY)
    return pl.pallas_call(
        body, name=name,
        out_shape=tuple(out_shapes),
        in_specs=[any_spec] * n,
        out_specs=tuple([any_spec] * n),
        scratch_shapes=_Gather.scratch(arrs),
    )(*arrs)


def _slice_order():
    x, y, c = lax.axis_index("x"), lax.axis_index("y"), lax.axis_index("c")
    slots = []
    for q in (2 * (1 - x) + y, 2 * x + (1 - y), 2 * (1 - x) + (1 - y), 2 * x + y):
        slots += [2 * q + 1 - c, 2 * q + c]
    return jnp.stack(slots).astype(jnp.int32)


def _gw_in_pair(order, h, dproj, smalls, small_shapes4):
    kk, m = h.shape
    tk = min(kk, 2048)
    nk = kk // tk
    ncols = dproj.shape[1] // N_DEV
    n = len(smalls)

    def body(order_ref, h_ref, d_ref, *rest):
        ins = rest[:n]
        sums_hbm, parts_hbm = rest[n], rest[n + 1]
        sib = rest[n + 2:2 * n + 2]
        (acc, sendbuf, recvbuf, sumbuf, send_sems, recv_sems, local_sem, ssend, srecv,
         isend, irecv) = rest[2 * n + 2:]
        js, k = pl.program_id(0), pl.program_id(1)
        x, y, c = lax.axis_index("x"), lax.axis_index("y"), lax.axis_index("c")
        sibling = (x, y, 1 - c)
        my_chip = 2 * x + y
        near = [(1 - x, y, c), (x, 1 - y, c)]

        def ici_copy(p, out_chip):
            peer = near[p]
            return pltpu.make_async_remote_copy(
                src_ref=sumbuf.at[p], dst_ref=parts_hbm.at[out_chip],
                send_sem=isend.at[p], recv_sem=irecv.at[p], device_id=peer, device_id_type=MESH)

        def small_copies():
            return [pltpu.make_async_remote_copy(
                        src_ref=ins[a].at[2 * q + 1 - c], dst_ref=sib[a].at[q],
                        send_sem=ssend.at[a * 4 + q], recv_sem=srecv.at[a * 4 + q],
                        device_id=sibling, device_id_type=MESH)
                    for a in range(n) for q in range(4)]

        def slice_copy(p):
            return pltpu.make_async_remote_copy(
                src_ref=sendbuf, dst_ref=recvbuf.at[p], send_sem=send_sems.at[p], recv_sem=recv_sems.at[p],
                device_id=sibling, device_id_type=MESH)

        def sum_copy(p):
            return pltpu.make_async_copy(sumbuf.at[2], sums_hbm.at[order_ref[2 * p] // 2], local_sem)

        @pl.when((js == 0) & (k == 0))
        def _():
            for cp in small_copies():
                cp.start()

        @pl.when(k == 0)
        def _():
            acc[...] = jnp.zeros_like(acc)

        acc[...] += _dot_tn(h_ref[...], d_ref[...])

        for p in range(4):
            @pl.when((js == 2 * p) & (k == nk - 1))
            def _():
                if p > 0:
                    slice_copy(p - 1).wait_send()
                sendbuf[...] = acc[...].astype(BF16)
                slice_copy(p).start()

            @pl.when((js == 2 * p + 1) & (k == nk - 1))
            def _():
                slice_copy(p).wait_recv()
                if p == 3:
                    sum_copy(2).wait()
                sumbuf[min(p, 2)] = (acc[...] + recvbuf[p].astype(F32)).astype(BF16)
                if p < 2:
                    ici_copy(p, my_chip).start()
                else:
                    sum_copy(p).start()

        @pl.when((js == N_DEV - 1) & (k == nk - 1))
        def _():
            slice_copy(3).wait_send()
            sum_copy(3).wait()
            for cp in small_copies():
                cp.wait()
            for p in range(2):
                ici_copy(p, 2 * near[p][0] + near[p][1]).wait_recv()
                ici_copy(p, my_chip).wait_send()

    any_spec = pl.BlockSpec(memory_space=pl.ANY)
    res = pl.pallas_call(
        body, name="gw_in_pair",
        grid_spec=pltpu.PrefetchScalarGridSpec(
            num_scalar_prefetch=1,
            grid=(N_DEV, nk),
            in_specs=[pl.BlockSpec((tk, m), lambda js, k, order_ref: (k, 0)),
                      pl.BlockSpec((tk, ncols), lambda js, k, order_ref: (k, order_ref[js]))] + [any_spec] * n,
            out_specs=(any_spec,) * (n + 2),
            scratch_shapes=[pltpu.VMEM((m, ncols), F32), pltpu.VMEM((m, ncols), BF16),
                            pltpu.VMEM((4, m, ncols), BF16), pltpu.VMEM((3, m, ncols), BF16),
                            pltpu.SemaphoreType.DMA((4,)), pltpu.SemaphoreType.DMA((4,)),
                            pltpu.SemaphoreType.DMA,
                            pltpu.SemaphoreType.DMA((4 * n,)), pltpu.SemaphoreType.DMA((4 * n,)),
                            pltpu.SemaphoreType.DMA((2,)), pltpu.SemaphoreType.DMA((2,))]),
        out_shape=(jax.ShapeDtypeStruct((4, m, ncols), BF16),) * 2 + tuple(small_shapes4),
        compiler_params=pltpu.CompilerParams(vmem_limit_bytes=VMEM_LIMIT),
    )(order, h, dproj, *smalls)
    return res[0], res[1], res[2:]


def _chip_copies(ins, outs, send_sems, recv_sems, local_sems, hops):
    n = len(ins)
    x, y, c = lax.axis_index("x"), lax.axis_index("y"), lax.axis_index("c")
    my_chip = 2 * x + y

    def peer_of(k):
        return ((1 - x) if (k >> 1) & 1 else x, (1 - y) if k & 1 else y, c)

    def copy(a, k, out_chip):
        peer = peer_of(k)
        return pltpu.make_async_remote_copy(
            src_ref=ins[a].at[2 * peer[0] + peer[1]], dst_ref=outs[a].at[out_chip],
            send_sem=send_sems.at[a * 3 + k - 1], recv_sem=recv_sems.at[a * 3 + k - 1],
            device_id=peer, device_id_type=MESH)

    sends = [copy(a, k, my_chip) for k in range(1, 4) for a in range(n) if k in hops[a]]
    arrivals = []
    for k in range(1, 4):
        peer = peer_of(k)
        arrivals += [copy(a, k, 2 * peer[0] + peer[1]) for a in range(n) if k in hops[a]]
    mine = [pltpu.make_async_copy(ins[a].at[my_chip], outs[a].at[my_chip], local_sems.at[a])
            for a in range(n)]
    return sends, arrivals, mine


def _pair_add(core, mine, theirs, row_tile, name):
    _, rows, cols = theirs.shape
    tr = rows if row_tile is None else row_tile

    def body(core_ref, a_ref, b_ref, o_ref):
        o_ref[...] = (a_ref[...].astype(F32) + b_ref[...].astype(F32)).astype(o_ref.dtype)

    blk = pl.BlockSpec((None, tr, cols), lambda q, i, core_ref: (q, i, 0))
    return pl.pallas_call(
        body, name=name,
        grid_spec=pltpu.PrefetchScalarGridSpec(
            num_scalar_prefetch=1,
            grid=(4, rows // tr),
            in_specs=[pl.BlockSpec((None, tr, cols), lambda q, i, core_ref: (2 * q + core_ref[0], i, 0)), blk],
            out_specs=blk),
        out_shape=jax.ShapeDtypeStruct(theirs.shape, theirs.dtype),
    )(core, mine, theirs)


def _mod_exchange(c8, w_ada, b_cols):
    cols = w_ada.shape[1]

    def body(c_ref, w_ref, b_ref, call_ref, mod_ref, msend, send1, recv1, send2, recv2):
        x, y, c = lax.axis_index("x"), lax.axis_index("y"), lax.axis_index("c")
        my_slot = 4 * x + 2 * y + c

        def peer_of(k):
            return ((1 - x) if (k >> 2) & 1 else x, (1 - y) if (k >> 1) & 1 else y, (1 - c) if k & 1 else c)

        def slot_of(dev):
            return 4 * dev[0] + 2 * dev[1] + dev[2]

        def exchange(src_of, dst_ref, send_sems, recv_sems):
            sends, arrivals = [], []
            for k in range(1, 8):
                peer = peer_of(k)
                sends.append(pltpu.make_async_remote_copy(
                    src_ref=src_of(slot_of(peer)), dst_ref=dst_ref.at[my_slot],
                    send_sem=send_sems.at[k - 1], recv_sem=recv_sems.at[k - 1],
                    device_id=peer, device_id_type=MESH))
                arrivals.append(pltpu.make_async_remote_copy(
                    src_ref=src_of(my_slot), dst_ref=dst_ref.at[slot_of(peer)],
                    send_sem=send_sems.at[k - 1], recv_sem=recv_sems.at[k - 1],
                    device_id=peer, device_id_type=MESH))
            for cp in sends:
                cp.start()
            for cp in arrivals:
                cp.wait_recv()
            for cp in sends:
                cp.wait_send()

        call_ref[my_slot] = c_ref[...]
        exchange(lambda s: c_ref, call_ref, send1, recv1)
        cv = call_ref[...].reshape(N_DEV * 8, c_ref.shape[1])
        act = cv * _sigmoid(cv)
        mod = jnp.dot(act, w_ref[...], preferred_element_type=F32,
                      precision=lax.Precision.HIGHEST) + b_ref[...]
        msend[...] = mod.reshape(N_DEV, 8, cols)
        mod_ref[my_slot] = msend[my_slot]
        exchange(lambda s: msend.at[s], mod_ref, send2, recv2)

    return pl.pallas_call(
        body, name="mod_exchange",
        out_shape=(jax.ShapeDtypeStruct((N_DEV, 8, c8.shape[1]), F32),
                   jax.ShapeDtypeStruct((N_DEV, 8, cols), F32)),
        scratch_shapes=[pltpu.VMEM((N_DEV, 8, cols), F32)] + [pltpu.SemaphoreType.DMA((7,))] * 4,
    )(c8, w_ada, b_cols)


def _ada_bwd(c_all_t, dmod_cols):
    def body(c_ref, d_ref, o_ref):
        cv = c_ref[...]
        sc = cv * _sigmoid(cv)
        o_ref[...] = jnp.dot(sc, d_ref[...], preferred_element_type=F32,
                             precision=lax.Precision.HIGHEST)

    return pl.pallas_call(
        body, name="ada_bwd",
        out_shape=jax.ShapeDtypeStruct((c_all_t.shape[0], dmod_cols.shape[1]), F32),
    )(c_all_t, dmod_cols)


def _shard_order():
    x, y, c = lax.axis_index("x"), lax.axis_index("y"), lax.axis_index("c")
    devs = [(x, y, c), (x, y, 1 - c)]
    for chip in [(1 - x, y), (x, 1 - y), (1 - x, 1 - y)]:
        devs += [(*chip, c), (*chip, 1 - c)]
    return jnp.stack([4 * d[0] + 2 * d[1] + d[2] for d in devs]).astype(jnp.int32)


def _prep_h(x2, mod3):
    ts = 512
    per_seq = S // ts

    def body(x_ref, mod_ref, h_ref):
        shift = mod_ref[0, 0:1, :]
        scale = mod_ref[0, 1:2, :]
        h_ref[...] = (x_ref[...] * (1.0 + scale) + shift).astype(BF16)

    return pl.pallas_call(
        body, name="prep_h",
        grid=(T // ts,),
        in_specs=[pl.BlockSpec((ts, D), lambda i: (i, 0)),
                  pl.BlockSpec((1, 3, D), lambda i: (i // per_seq, 0, 0))],
        out_specs=pl.BlockSpec((ts, D), lambda i: (i, 0)),
        out_shape=jax.ShapeDtypeStruct((T, D), BF16),
    )(x2, mod3)


def _gather_proj(order, h, w_shard, tm, ride=()):
    rows, kdim = h.shape
    ncols = w_shard.shape[1]
    n_i = rows // tm
    ride_arrs, ride_shapes = ride if ride else ((), ())
    n_ride = len(ride_arrs)

    def body(order_ref, h_ref, mine_hbm, *rest):
        ride_ins = rest[:n_ride]
        o_ref, all_hbm = rest[n_ride:n_ride + 2]
        ride_outs = rest[n_ride + 2:2 * n_ride + 2]
        wv, send_sems, recv_sems, local_sems = rest[2 * n_ride + 2:2 * n_ride + 6]
        ride_scr = rest[2 * n_ride + 6:]
        j, i = pl.program_id(0), pl.program_id(1)
        x, y, c = lax.axis_index("x"), lax.axis_index("y"), lax.axis_index("c")
        me, sibling = (x, y, c), (x, y, 1 - c)
        chips = [(1 - x, y), (x, 1 - y), (1 - x, 1 - y)]

        def slot(dev):
            return 4 * dev[0] + 2 * dev[1] + dev[2]

        def copy(k, block, to):
            return pltpu.make_async_remote_copy(
                src_ref=wv.at[slot(block)], dst_ref=wv.at[slot(block)],
                send_sem=send_sems.at[k], recv_sem=recv_sems.at[k],
                device_id=to, device_id_type=MESH)

        def keep(step, block):
            return pltpu.make_async_copy(wv.at[slot(block)], all_hbm.at[slot(block)], local_sems.at[step])

        if n_ride:
            gather = _Gather(ride_ins, ride_outs, ride_scr[3:], *ride_scr[:3])
        first = [copy(0, me, sibling)] + [copy(1 + q, me, (*chip, c)) for q, chip in enumerate(chips)]
        passed = [copy(4 + q, (*chip, c), sibling) for q, chip in enumerate(chips)]
        due = [(me, None, None), (sibling, copy(0, sibling, me), None)]
        for q, chip in enumerate(chips):
            due.append(((*chip, c), copy(1 + q, (*chip, c), me), passed[q]))
            due.append(((*chip, 1 - c), copy(4 + q, (*chip, 1 - c), me), None))

        @pl.when((j == 0) & (i == 0))
        def _():
            load = pltpu.make_async_copy(mine_hbm, wv.at[slot(me)], local_sems.at[N_DEV])
            load.start()
            load.wait()
            for cp in first:
                cp.start()
            keep(0, me).start()

        for step in range(1, N_DEV):
            block, arrival, forward = due[step]

            @pl.when((j == step) & (i == 0))
            def _():
                arrival.wait_recv()
                if forward is not None:
                    forward.start()
                keep(step, block).start()
                if n_ride and step == N_DEV - 2:
                    gather.begin()

        o_ref[...] = _dot(h_ref[...], wv[order_ref[j]]).astype(BF16)

        @pl.when((j == N_DEV - 1) & (i == n_i - 1))
        def _():
            for cp in first + passed:
                cp.wait_send()
            for step in range(N_DEV):
                keep(step, due[step][0]).wait()
            if n_ride:
                gather.finish()

    any_spec = pl.BlockSpec(memory_space=pl.ANY)
    res = pl.pallas_call(
        body, name="gather_proj",
        grid_spec=pltpu.PrefetchScalarGridSpec(
            num_scalar_prefetch=1,
            grid=(N_DEV, n_i),
            in_specs=[pl.BlockSpec((tm, kdim), lambda j, i, order_ref: (i, 0)), any_spec] + [any_spec] * n_ride,
            out_specs=(pl.BlockSpec((tm, ncols), lambda j, i, order_ref: (i, order_ref[j])), any_spec)
                      + (any_spec,) * n_ride,
            scratch_shapes=[pltpu.VMEM((N_DEV, kdim, ncols), BF16),
                            pltpu.SemaphoreType.DMA((7,)), pltpu.SemaphoreType.DMA((7,)),
                            pltpu.SemaphoreType.DMA((N_DEV + 1,))]
                           + (_Gather.scratch(ride_arrs) if n_ride else [])),
        out_shape=(jax.ShapeDtypeStruct((rows, N_DEV * ncols), BF16),
                   jax.ShapeDtypeStruct((N_DEV, kdim, ncols), BF16)) + tuple(ride_shapes),
        compiler_params=pltpu.CompilerParams(vmem_limit_bytes=VMEM_LIMIT),
    )(order, h, w_shard, *ride_arrs)
    return res[0], res[1], res[2:]


def _bucket_maps():
    a = np.arange(QB)[:, None]
    b = np.arange(2 * QB)[None, :]
    steps = a + QB - b
    maps = []
    for dil in DILATIONS:
        dist = np.maximum(steps, 0) * dil
        nf = np.maximum(dist, 1).astype(np.float32)
        large = 16 + (np.log(nf / np.float32(16)) / np.float32(math.log(128.0))
                      * np.float32(16)).astype(np.int32)
        large = np.minimum(large, N_BUCKETS - 1)
        maps.append(np.where(dist < 16, dist, large).astype(np.int32))
    band = (steps >= 0) & (steps <= N_STEPS)
    first = band & (b >= QB)
    masks = np.stack([first, band]).astype(np.int32)
    return np.stack(maps), masks


def _bias_expand(rel_bias, buckets, masks):
    def body(tab_ref, bk_ref, mk_ref, o_ref):
        for g in range(3):
            bk = bk_ref[g]
            for h in range(4):
                col = 4 * g + h
                val = jnp.zeros((QB, 2 * QB), F32)
                for k in range(N_BUCKETS):
                    val = jnp.where(bk == k, tab_ref[k, col], val)
                o_ref[g, 0, h] = jnp.where(mk_ref[0] != 0, val, NEG_INF)
                o_ref[g, 1, h] = jnp.where(mk_ref[1] != 0, val, NEG_INF)

    return pl.pallas_call(
        body, name="bias_expand",
        in_specs=[pl.BlockSpec(memory_space=pltpu.SMEM),
                  pl.BlockSpec(memory_space=pltpu.VMEM),
                  pl.BlockSpec(memory_space=pltpu.VMEM)],
        out_shape=jax.ShapeDtypeStruct((3, 2, 4, QB, 2 * QB), F32),
    )(rel_bias, buckets, masks)


def _bias_grad(ds1, ds2, ds3, buckets):
    def body(d1_ref, d2_ref, d3_ref, bk_ref, o_ref):
        for g, d_ref in enumerate((d1_ref, d2_ref, d3_ref)):
            bk = bk_ref[g]
            for h in range(4):
                dv = d_ref[h]
                for k in range(N_BUCKETS):
                    o_ref[k, 4 * g + h] = jnp.sum(jnp.where(bk == k, dv, 0.0))

    return pl.pallas_call(
        body, name="bias_grad",
        in_specs=[pl.BlockSpec(memory_space=pltpu.VMEM)] * 4,
        out_specs=pl.BlockSpec(memory_space=pltpu.SMEM),
        out_shape=jax.ShapeDtypeStruct((N_BUCKETS, N_HEADS), F32),
    )(ds1, ds2, ds3, buckets)


def _scratch_sets(rows):
    return 4 if rows <= 512 else 1


def _unit_chunks(dil, size=16):
    units = [(h, r) for h in range(4) for r in range(dil)]
    return [units[i:i + size] for i in range(0, len(units), size)]


def _residue_rows(src_ref, copies, h, residue):
    sl = slice(h * HD, (h + 1) * HD)
    if copies is None:
        return lambda r: src_ref[:, sl]
    buf = copies[h % len(copies)]
    buf[...] = src_ref[:, sl].astype(F32)
    return lambda r: buf[residue(r), :].astype(BF16)


def _attn_fwd(proj, bias, g, ride=()):
    dil = DILATIONS[g]
    rows = QB * dil
    nsb = S // rows
    has_prev = nsb > 1

    def residue(r):
        return pl.ds(r, QB, stride=dil) if dil > 1 else pl.ds(0, QB)

    strided = dil > 1
    n_sets = _scratch_sets(rows)
    n_attn = 6 if has_prev else 4
    ride_arrs, ride_shapes = ride if ride else ((), ())
    n_ride = len(ride_arrs)
    n_in = n_attn + n_ride
    n_copied = (4 + (2 if has_prev else 0)) * (n_sets if strided else 0)

    def body(*refs):
        q_ref, kc_ref, vc_ref = refs[:3]
        kp_ref, vp_ref = refs[3:5] if has_prev else (None, None)
        b_ref = refs[n_attn - 1]
        o_ref, l_ref = refs[n_in:n_in + 2]
        scr = list(refs[n_in + 2 + n_ride:])
        ls = [scr.pop(0) for _ in range(4)]
        copies = {name: [scr.pop(0) for _ in range(n_sets)] if strided else None
                  for name in ("q", "kc", "vc", "o") + (("kp", "vp") if has_prev else ())}
        if n_ride:
            gather = _Gather(refs[n_attn:n_in], refs[n_in + 2:n_in + 2 + n_ride], scr[3:], *scr[:3])

            @pl.when((pl.program_id(0) == 0) & (pl.program_id(1) == 0))
            def _():
                gather.begin()
        lane = lax.broadcasted_iota(jnp.int32, (QB, 128), 1)
        refs_of = {"q": q_ref, "kc": kc_ref, "vc": vc_ref, "kp": kp_ref, "vp": vp_ref}
        for chunk in _unit_chunks(dil):
            rows_of = {h: {name: _residue_rows(refs_of[name], copies[name], h, residue)
                           for name in refs_of if refs_of[name] is not None}
                       for h in sorted({h for h, _ in chunk})}

            def batch(name):
                return jnp.stack([rows_of[h][name](r) for h, r in chunk])

            q, k, v = batch("q"), batch("kc"), batch("vc")
            if has_prev:
                k = jnp.concatenate([batch("kp"), k], axis=1)
                v = jnp.concatenate([batch("vp"), v], axis=1)
                bias_b = jnp.stack([b_ref[h] for h, _ in chunk])
            else:
                bias_b = jnp.stack([b_ref[h, :, QB:] for h, _ in chunk])
            s = jnp.einsum("uqd,ukd->uqk", q, k, preferred_element_type=F32) * SCALE + bias_b
            m = jnp.max(s, axis=-1, keepdims=True)
            p = jnp.exp(s - m)
            l = jnp.sum(p, axis=-1, keepdims=True)
            o = jnp.einsum("uqk,ukd->uqd", p.astype(BF16), v, preferred_element_type=F32) / l
            lse = m + jnp.log(l)
            for i, (h, r) in enumerate(chunk):
                if strided:
                    copies["o"][h % n_sets][residue(r), :] = o[i]
                else:
                    o_ref[:, h * HD:(h + 1) * HD] = o[i]
                ls[h][r * QB:(r + 1) * QB, :] = jnp.where(lane == h, lse[i], 0.0)
            if strided:
                for h in sorted({h for h, _ in chunk}):
                    o_ref[:, h * HD:(h + 1) * HD] = copies["o"][h % n_sets][...]
        for r in range(dil):
            blk = slice(r * QB, (r + 1) * QB)
            l_ref[residue(r), :] = (ls[0][blk, :] + ls[1][blk, :]) + (ls[2][blk, :] + ls[3][blk, :])
        if n_ride:
            @pl.when((pl.program_id(0) == BL - 1) & (pl.program_id(1) == nsb - 1))
            def _():
                gather.finish()

    def row(b, n):
        return b * nsb + n

    def prev(b, n):
        return b * nsb + jnp.maximum(n - 1, 0)

    in_specs = [
        pl.BlockSpec((rows, GW), lambda b, n: (row(b, n), CB_Q + g)),
        pl.BlockSpec((rows, GW), lambda b, n: (row(b, n), CB_K + g)),
        pl.BlockSpec((rows, GW), lambda b, n: (row(b, n), CB_V + g)),
    ]
    args = [proj, proj, proj]
    scratch = [pltpu.VMEM((rows, 128), F32)] * (4 + n_copied)
    if has_prev:
        in_specs += [pl.BlockSpec((rows, GW), lambda b, n: (prev(b, n), CB_K + g)),
                     pl.BlockSpec((rows, GW), lambda b, n: (prev(b, n), CB_V + g))]
        args += [proj, proj]
    in_specs.append(pl.BlockSpec((None, None, 4, QB, 2 * QB),
                                 lambda b, n: (g, jnp.minimum(n, 1), 0, 0, 0)))
    args.append(bias)
    any_spec = pl.BlockSpec(memory_space=pl.ANY)
    return pl.pallas_call(
        body, name=f"attn_fwd{g}",
        grid=(BL, nsb),
        in_specs=in_specs + [any_spec] * n_ride,
        out_specs=(pl.BlockSpec((rows, GW), lambda b, n: (row(b, n), 0)),
                   pl.BlockSpec((rows, 128), lambda b, n: (row(b, n), 0))) + (any_spec,) * n_ride,
        out_shape=(jax.ShapeDtypeStruct((T, GW), F32), jax.ShapeDtypeStruct((T, 128), F32)) + tuple(ride_shapes),
        scratch_shapes=scratch + (_Gather.scratch(ride_arrs) if n_ride else []),
        compiler_params=pltpu.CompilerParams(vmem_limit_bytes=VMEM_LIMIT),
    )(*args, *ride_arrs)


def _attn_bwd(proj, d_out, stats, bias, dproj, g):
    dil = DILATIONS[g]
    rows = QB * dil
    nsb = S // rows
    has_prev = nsb > 1
    n_steps = nsb + 1 if has_prev else 1
    n_in = 7 + (2 if has_prev else 0)

    def residue(r):
        return pl.ds(r, QB, stride=dil) if dil > 1 else pl.ds(0, QB)

    strided = dil > 1
    n_sets = _scratch_sets(rows)

    def body(*refs):
        q_ref, kc_ref, vc_ref, do_ref, st_ref, b_ref = refs[:6]
        kp_ref, vp_ref = refs[6:8] if has_prev else (None, None)
        out_ref, db_ref = refs[n_in], refs[n_in + 1]
        scr = list(refs[n_in + 2:])
        sq, sk, sv, sems = [scr.pop(0) for _ in range(4)]
        carry = scr.pop(0) if has_prev else None
        sts = scr.pop(0) if strided else st_ref
        copies = {name: [scr.pop(0) for _ in range(n_sets)] if strided else None
                  for name in ("q", "kc", "vc", "do", "dq", "dk", "dv") + (("kp", "vp") if has_prev else ())}
        b, n = pl.program_id(0), pl.program_id(1)

        @pl.when((b == 0) & (n == 0))
        def _():
            db_ref[...] = jnp.zeros_like(db_ref)

        def finish(h, r, dq, dk, dv):
            if strided:
                for name, val in (("dq", dq), ("dk", dk), ("dv", dv)):
                    copies[name][h % n_sets][residue(r), :] = val
            else:
                sl = slice(h * HD, (h + 1) * HD)
                sq[:, sl], sk[:, sl], sv[:, sl] = dq.astype(BF16), dk.astype(BF16), dv.astype(BF16)

        def finish_head(h):
            if strided:
                sl = slice(h * HD, (h + 1) * HD)
                sq[:, sl] = copies["dq"][h % n_sets][...].astype(BF16)
                sk[:, sl] = copies["dk"][h % n_sets][...].astype(BF16)
                sv[:, sl] = copies["dv"][h % n_sets][...].astype(BF16)

        def write_block(blk_idx):
            row0 = pl.multiple_of(blk_idx * rows, rows)
            _write_columns([(sq, CB * (CB_Q + g)), (sk, CB * (CB_K + g)), (sv, CB * (CB_V + g))],
                           out_ref, row0, sems)

        def carried(h, r):
            blk = slice(r * QB, (r + 1) * QB)
            return ((blk, slice(h * HD, (h + 1) * HD)), (blk, slice(GW + h * HD, GW + (h + 1) * HD)),
                    (blk, slice(2 * GW + h * HD, 2 * GW + (h + 1) * HD)))

        if has_prev:
            @pl.when(n == 0)
            def _():
                carry[...] = jnp.zeros_like(carry)

            @pl.when(n == nsb)
            def _():
                for h in range(4):
                    for r in range(dil):
                        cq, ck, cv = carried(h, r)
                        finish(h, r, carry[cq], carry[ck], carry[cv])
                    finish_head(h)
                write_block(b * nsb + nsb - 1)

        @pl.when(n < nsb)
        def _():
            if strided:
                for r in range(dil):
                    sts[r * QB:(r + 1) * QB, :] = st_ref[residue(r), :]
            refs_of = {"q": q_ref, "kc": kc_ref, "vc": vc_ref, "do": do_ref, "kp": kp_ref, "vp": vp_ref}
            for chunk in _unit_chunks(dil):
                heads = sorted({h for h, _ in chunk})
                rows_of = {h: {name: _residue_rows(refs_of[name], copies[name], h, residue)
                               for name in refs_of if refs_of[name] is not None}
                           for h in heads}

                def batch(name):
                    return jnp.stack([rows_of[h][name](r) for h, r in chunk])

                q, k, v, do = batch("q"), batch("kc"), batch("vc"), batch("do")
                if has_prev:
                    k = jnp.concatenate([batch("kp"), k], axis=1)
                    v = jnp.concatenate([batch("vp"), v], axis=1)
                    bias_b = jnp.stack([b_ref[h] for h, _ in chunk])
                else:
                    bias_b = jnp.stack([b_ref[h, :, QB:] for h, _ in chunk])
                lse = jnp.stack([sts[r * QB:(r + 1) * QB, h:h + 1] for h, r in chunk])
                delta = jnp.stack([sts[r * QB:(r + 1) * QB, 4 + h:5 + h] for h, r in chunk])
                s = jnp.einsum("uqd,ukd->uqk", q, k, preferred_element_type=F32) * SCALE + bias_b
                p = jnp.exp(s - lse)
                ds = p * (jnp.einsum("uqd,ukd->uqk", do, v, preferred_element_type=F32) - delta)
                for h in heads:
                    mine = [ds[i] for i, (hh, _) in enumerate(chunk) if hh == h]
                    tot = mine[0]
                    for extra in mine[1:]:
                        tot = tot + extra
                    if has_prev:
                        db_ref[h] += tot
                    else:
                        db_ref[h, :, QB:] += tot
                dsb, pb = ds.astype(BF16), p.astype(BF16)
                dq = jnp.einsum("uqk,ukd->uqd", dsb, k, preferred_element_type=F32) * SCALE
                dk = jnp.einsum("uqk,uqd->ukd", dsb, q, preferred_element_type=F32) * SCALE
                dv = jnp.einsum("uqk,uqd->ukd", pb, do, preferred_element_type=F32)
                for i, (h, r) in enumerate(chunk):
                    if has_prev:
                        cq, ck, cv = carried(h, r)
                        finish(h, r, carry[cq], carry[ck] + dk[i, :QB], carry[cv] + dv[i, :QB])
                        carry[cq] = dq[i]
                        carry[ck] = dk[i, QB:]
                        carry[cv] = dv[i, QB:]
                    else:
                        finish(h, r, dq[i], dk[i], dv[i])
                for h in heads:
                    finish_head(h)
            if has_prev:
                @pl.when(n > 0)
                def _():
                    write_block(b * nsb + n - 1)
            else:
                write_block(b)

    def row(b, n):
        return b * nsb + jnp.minimum(n, nsb - 1)

    def prev(b, n):
        return b * nsb + jnp.maximum(jnp.minimum(n, nsb - 1) - 1, 0)

    in_specs = [
        pl.BlockSpec((rows, GW), lambda b, n: (row(b, n), CB_Q + g)),
        pl.BlockSpec((rows, GW), lambda b, n: (row(b, n), CB_K + g)),
        pl.BlockSpec((rows, GW), lambda b, n: (row(b, n), CB_V + g)),
        pl.BlockSpec((rows, GW), lambda b, n: (row(b, n), 0)),
        pl.BlockSpec((rows, 128), lambda b, n: (row(b, n), 0)),
        pl.BlockSpec((None, None, 4, QB, 2 * QB),
                     lambda b, n: (g, jnp.minimum(jnp.minimum(n, nsb - 1), 1), 0, 0, 0)),
    ]
    args = [proj, proj, proj, d_out, stats, bias]
    scratch = [pltpu.VMEM((rows, GW), BF16)] * 3 + [pltpu.SemaphoreType.DMA((3,))]
    if has_prev:
        in_specs += [pl.BlockSpec((rows, GW), lambda b, n: (prev(b, n), CB_K + g)),
                     pl.BlockSpec((rows, GW), lambda b, n: (prev(b, n), CB_V + g))]
        args += [proj, proj]
        scratch.append(pltpu.VMEM((rows, 3 * GW), F32))
    if strided:
        n_copied = (7 + (2 if has_prev else 0)) * n_sets
        scratch += [pltpu.VMEM((rows, 128), F32)] * (1 + n_copied)
    in_specs.append(pl.BlockSpec(memory_space=pl.ANY))
    args.append(dproj)
    return pl.pallas_call(
        body, name=f"attn_bwd{g}",
        grid=(BL, n_steps),
        in_specs=in_specs,
        out_specs=(pl.BlockSpec(memory_space=pl.ANY),
                   pl.BlockSpec((4, QB, 2 * QB), lambda b, n: (0, 0, 0))),
        out_shape=(jax.ShapeDtypeStruct((T, NCOL), BF16),
                   jax.ShapeDtypeStruct((4, QB, 2 * QB), F32)),
        scratch_shapes=scratch,
        input_output_aliases={len(args) - 1: 0},
        compiler_params=pltpu.CompilerParams(vmem_limit_bytes=VMEM_LIMIT),
    )(*args)


def _tail(x2, tgt2, mod3, o_g, lse_g, proj, w_ao, w_co, w_o, conv_w, conv_b, ln_g, ln_b):
    tm = 256
    per_seq = S // tm
    halo = 16

    def body(x_ref, t_ref, mod_ref, o1_ref, o2_ref, o3_ref, l1_ref, l2_ref, l3_ref,
             ga_ref, u_ref, bg_ref, cg_ref, gc_ref, ma_ref, mc_ref, up_ref, cp_ref,
             wao_ref, wco_ref, wo_ref, cw_ref, cb_ref, lg_ref, lb_ref,
             dproj_ref, dyc_ref, do_ref, st_ref, dxd_ref,
             mg_ref, dy_ref, ain_ref, dao_ref, sin_ref, dso_ref, vec_ref,
             dga_s, dbg_s, dgm_s, sems):
        i = pl.program_id(0)
        bidx = i // per_seq
        first = (i % per_seq) == 0

        @pl.when(i == 0)
        def _():
            vec_ref[...] = jnp.zeros_like(vec_ref)

        l1, l2, l3 = l1_ref[...], l2_ref[...], l3_ref[...]
        mx = jnp.maximum(jnp.maximum(l1, l2), l3)
        e1, e2, e3 = jnp.exp(l1 - mx), jnp.exp(l2 - mx), jnp.exp(l3 - mx)
        esum = e1 + e2 + e3
        lse_tot = mx + jnp.log(esum)
        w1, w2, w3 = e1 / esum, e2 / esum, e3 / esum

        def per_head(wv):
            return jnp.concatenate([jnp.broadcast_to(wv[:, h:h + 1], (tm, HD)) for h in range(4)], axis=1)

        o = per_head(w1) * o1_ref[...] + per_head(w2) * o2_ref[...] + per_head(w3) * o3_ref[...]

        ga = ga_ref[...].astype(F32)
        sig_ga = _sigmoid(ga)
        silu_ga = ga * sig_ga
        a_in = (o * silu_ga).astype(BF16)
        a_out = _dot(a_in, wao_ref[...])

        u = u_ref[...].astype(F32)
        cg = cg_ref[...].astype(F32)
        z = cg * u
        zp = cp_ref[...].astype(F32) * up_ref[...].astype(F32)
        zp = jnp.where(first, 0.0, zp)
        zcat = jnp.concatenate([zp, z], axis=0)
        z1 = pltpu.roll(zcat, 1, 0)[halo:]
        z2 = pltpu.roll(zcat, 2, 0)[halo:]
        y_conv = cw_ref[0:1, :] * z2 + cw_ref[1:2, :] * z1 + cw_ref[2:3, :] * z + cb_ref[...]
        gc = gc_ref[...].astype(F32)
        sig_gc = _sigmoid(gc)
        silu_gc = gc * sig_gc
        bg = bg_ref[...].astype(F32)
        bg_yc = bg * y_conv
        s_in = (bg_yc * silu_gc).astype(BF16)
        s_out = _dot(s_in, wco_ref[...])

        sa = _sigmoid(ma_ref[...].astype(F32))
        sc = _sigmoid(mc_ref[...].astype(F32))
        merged = (sa * a_out + sc * s_out).astype(BF16)
        y = _dot(merged, wo_ref[...])
        gate1 = 1.0 + mod_ref[0, 2:3, :]
        xv = x_ref[...]
        resid = ALPHA * xv + gate1 * y
        mu = jnp.mean(resid, axis=1, keepdims=True)
        xc = resid - mu
        var = jnp.mean(xc * xc, axis=1, keepdims=True)
        rstd = lax.rsqrt(var + LN_EPS)
        xhat = xc * rstd
        lg = lg_ref[...]
        err = xhat * lg + lb_ref[...] - t_ref[...]
        vec_ref[3:4, :] += (0.5 / D) * jnp.sum(err * err, axis=0, keepdims=True)

        vec_ref[1:2, :] += (1.0 / D) * jnp.sum(err * xhat, axis=0, keepdims=True)
        vec_ref[2:3, :] += (1.0 / D) * jnp.sum(err, axis=0, keepdims=True)
        dxh = err * (lg * (1.0 / D))
        dres = rstd * (dxh - jnp.mean(dxh, axis=1, keepdims=True)
                       - xhat * jnp.mean(dxh * xhat, axis=1, keepdims=True))
        dxd_ref[...] = ALPHA * dres
        dgate = jnp.sum(dres * y, axis=0, keepdims=True)
        vec_ref[4:5, :] += jnp.where(bidx == 0, dgate, 0.0)
        vec_ref[5:6, :] += jnp.where(bidx == 1, dgate, 0.0)
        dy = (dres * gate1).astype(BF16)

        dmerged = _dot_nt(dy, wo_ref[...])
        da_out_f = dmerged * sa
        ds_out_f = dmerged * sc
        da_out = da_out_f.astype(BF16)
        ds_out = ds_out_f.astype(BF16)
        dgm_s[:, 2 * D:3 * D] = (ds_out_f * s_out * (1.0 - sc)).astype(BF16)
        dgm_s[:, D:2 * D] = (da_out_f * a_out * (1.0 - sa)).astype(BF16)
        da_in = _dot_nt(da_out, wao_ref[...])
        ds_in = _dot_nt(ds_out, wco_ref[...])

        d_o = da_in * silu_ga
        do_ref[...] = d_o.astype(BF16)
        dga_s[...] = (da_in * o * (sig_ga + silu_ga * (1.0 - sig_ga))).astype(BF16)
        lane = lax.broadcasted_iota(jnp.int32, (tm, 128), 1)
        stats = lse_tot
        od = o * d_o
        for h in range(4):
            delta = jnp.sum(od[:, h * HD:(h + 1) * HD], axis=1, keepdims=True)
            stats = jnp.where(lane == 4 + h, delta, stats)
        st_ref[...] = stats

        ds_silu = ds_in * silu_gc
        dbg_s[...] = (ds_silu * y_conv).astype(BF16)
        dyc = ds_silu * bg
        dyc_ref[...] = dyc
        vec_ref[0:1, :] += jnp.sum(dyc, axis=0, keepdims=True)
        dgm_s[:, 0:D] = (ds_in * bg_yc * (sig_gc + silu_gc * (1.0 - sig_gc))).astype(BF16)

        mg_ref[...] = merged
        dy_ref[...] = dy
        ain_ref[...] = a_in
        dao_ref[...] = da_out
        sin_ref[...] = s_in
        dso_ref[...] = ds_out
        _write_columns([(dga_s, CB * CB_GA), (dbg_s, D * KB_BG), (dgm_s, D * KB_GC)],
                       dproj_ref, pl.multiple_of(i * tm, tm), sems)

    def tile(width, cblk=0):
        return pl.BlockSpec((tm, width), lambda i: (i, cblk))

    def whole(shape):
        return pl.BlockSpec(shape, lambda i: tuple(0 for _ in shape))

    prev_rows = lambda i: (jnp.maximum(i * (tm // halo) - 1, 0),)
    in_specs = [
        tile(D), tile(D), pl.BlockSpec((1, 3, D), lambda i: (i // per_seq, 0, 0)),
        tile(GW), tile(GW), tile(GW), tile(128), tile(128), tile(128),
        tile(GW, CB_GA), tile(D, KB_U), tile(D, KB_BG), tile(D, KB_CG), tile(D, KB_GC),
        tile(D, KB_MA), tile(D, KB_MC),
        pl.BlockSpec((halo, D), lambda i: (*prev_rows(i), KB_U)),
        pl.BlockSpec((halo, D), lambda i: (*prev_rows(i), KB_CG)),
        whole((GW, D)), whole((D, D)), whole((D, D)),
        whole((3, D)), whole((1, D)), whole((1, D)), whole((1, D)),
    ]
    out_specs = (
        pl.BlockSpec(memory_space=pl.ANY), tile(D), tile(GW), tile(128), tile(D),
        tile(D), tile(D), tile(GW), tile(D), tile(D), tile(D),
        pl.BlockSpec((8, D), lambda i: (0, 0)),
    )
    out_shape = (
        jax.ShapeDtypeStruct((T, NCOL), BF16),
        jax.ShapeDtypeStruct((T, D), F32),
        jax.ShapeDtypeStruct((T, GW), BF16),
        jax.ShapeDtypeStruct((T, 128), F32),
        jax.ShapeDtypeStruct((T, D), F32),
        jax.ShapeDtypeStruct((T, D), BF16),
        jax.ShapeDtypeStruct((T, D), BF16),
        jax.ShapeDtypeStruct((T, GW), BF16),
        jax.ShapeDtypeStruct((T, D), BF16),
        jax.ShapeDtypeStruct((T, D), BF16),
        jax.ShapeDtypeStruct((T, D), BF16),
        jax.ShapeDtypeStruct((8, D), F32),
    )
    return pl.pallas_call(
        body, name="tail",
        grid=(T // tm,),
        in_specs=in_specs, out_specs=out_specs, out_shape=out_shape,
        scratch_shapes=[pltpu.VMEM((tm, GW), BF16), pltpu.VMEM((tm, D), BF16), pltpu.VMEM((tm, 3 * D), BF16),
                        pltpu.SemaphoreType.DMA((3,))],
        compiler_params=pltpu.CompilerParams(vmem_limit_bytes=VMEM_LIMIT),
    )(x2, tgt2, mod3, *o_g, *lse_g, proj, proj, proj, proj, proj, proj, proj, proj, proj,
      w_ao, w_co, w_o, conv_w, conv_b, ln_g, ln_b)


def _conv_bwd(dyc, proj, conv_w, dproj):
    tm = 512
    per_seq = S // tm
    halo = 16

    def body(d_ref, dn_ref, u_ref, c_ref, up_ref, cp_ref, cw_ref, _, dproj_ref, g_ref, du_s, dc_s, sems):
        i = pl.program_id(0)
        first = (i % per_seq) == 0
        last = (i % per_seq) == per_seq - 1

        @pl.when(i == 0)
        def _():
            g_ref[...] = jnp.zeros_like(g_ref)

        d = d_ref[...]
        dn = jnp.where(last, 0.0, dn_ref[...])
        dcat = jnp.concatenate([d, dn], axis=0)
        d1 = pltpu.roll(dcat, tm + 8 - 1, 0)[:tm]
        d2 = pltpu.roll(dcat, tm + 8 - 2, 0)[:tm]
        dz = cw_ref[2:3, :] * d + cw_ref[1:2, :] * d1 + cw_ref[0:1, :] * d2
        u = u_ref[...].astype(F32)
        cg = c_ref[...].astype(F32)
        du_s[...] = (dz * cg).astype(BF16)
        dc_s[...] = (dz * u).astype(BF16)
        _write_columns([(du_s, D * KB_U), (dc_s, D * KB_CG)], dproj_ref, pl.multiple_of(i * tm, tm), sems)

        z = cg * u
        zp = jnp.where(first, 0.0, cp_ref[...].astype(F32) * up_ref[...].astype(F32))
        zcat = jnp.concatenate([zp, z], axis=0)
        z1 = pltpu.roll(zcat, 1, 0)[halo:]
        z2 = pltpu.roll(zcat, 2, 0)[halo:]
        g_ref[0:1, :] += jnp.sum(d * z2, axis=0, keepdims=True)
        g_ref[1:2, :] += jnp.sum(d * z1, axis=0, keepdims=True)
        g_ref[2:3, :] += jnp.sum(d * z, axis=0, keepdims=True)

    n_tiles = T // tm
    prev_rows = lambda i: jnp.maximum(i * (tm // halo) - 1, 0)
    next_rows = lambda i: jnp.minimum((i + 1) * (tm // 8), T // 8 - 1)
    return pl.pallas_call(
        body, name="conv_bwd",
        grid=(n_tiles,),
        in_specs=[pl.BlockSpec((tm, D), lambda i: (i, 0)),
                  pl.BlockSpec((8, D), lambda i: (next_rows(i), 0)),
                  pl.BlockSpec((tm, D), lambda i: (i, KB_U)),
                  pl.BlockSpec((tm, D), lambda i: (i, KB_CG)),
                  pl.BlockSpec((halo, D), lambda i: (prev_rows(i), KB_U)),
                  pl.BlockSpec((halo, D), lambda i: (prev_rows(i), KB_CG)),
                  pl.BlockSpec((3, D), lambda i: (0, 0)),
                  pl.BlockSpec(memory_space=pl.ANY)],
        out_specs=(pl.BlockSpec(memory_space=pl.ANY),
                   pl.BlockSpec((8, D), lambda i: (0, 0))),
        out_shape=(jax.ShapeDtypeStruct((T, NCOL), BF16),
                   jax.ShapeDtypeStruct((8, D), F32)),
        scratch_shapes=[pltpu.VMEM((tm, D), BF16), pltpu.VMEM((tm, D), BF16), pltpu.SemaphoreType.DMA((2,))],
        input_output_aliases={7: 0},
        compiler_params=pltpu.CompilerParams(vmem_limit_bytes=VMEM_LIMIT),
    )(dyc, dyc, proj, proj, proj, proj, conv_w, dproj)


def _dh_dx(dproj, w_in_all, x2, dxd, mod3, chip_sums, hops=(), parts0=None):
    tm = 1024
    per_seq = S // tm
    n = len(chip_sums)
    n_in = 5 + n + (0 if parts0 is None else 1)

    def body(*refs):
        d_ref, w_ref, x_ref, dxd_ref, mod_ref = refs[:5]
        ins = refs[5:5 + n]
        gx_ref, vec_ref = refs[n_in:n_in + 2]
        outs = refs[n_in + 2:n_in + 2 + n]
        acc, send_sems, recv_sems, local_sems = refs[n_in + 2 + n:]
        i, jj = pl.program_id(0), pl.program_id(1)

        @pl.when((i == 0) & (jj == 0))
        def _():
            vec_ref[...] = jnp.zeros_like(vec_ref)
            if n:
                sends, _, mine = _chip_copies(ins, outs, send_sems, recv_sems, local_sems, hops)
                for cp in sends + mine:
                    cp.start()

        if n:
            @pl.when((i == T // tm - 1) & (jj == N_DEV - 1))
            def _():
                sends, arrivals, mine = _chip_copies(ins, outs, send_sems, recv_sems, local_sems, hops)
                for cp in arrivals:
                    cp.wait_recv()
                for cp in sends:
                    cp.wait_send()
                for cp in mine:
                    cp.wait()

        @pl.when(jj == 0)
        def _():
            acc[...] = jnp.zeros_like(acc)

        acc[...] += _dot_nt(d_ref[...], w_ref[...])

        @pl.when(jj == N_DEV - 1)
        def _():
            dh = acc[...]
            bidx = i // per_seq
            gx_ref[...] = dxd_ref[...] + dh * (1.0 + mod_ref[0, 1:2, :])
            dshift = jnp.sum(dh, axis=0, keepdims=True)
            dscale = jnp.sum(dh * x_ref[...], axis=0, keepdims=True)
            vec_ref[0:1, :] += jnp.where(bidx == 0, dshift, 0.0)
            vec_ref[1:2, :] += jnp.where(bidx == 1, dshift, 0.0)
            vec_ref[2:3, :] += jnp.where(bidx == 0, dscale, 0.0)
            vec_ref[3:4, :] += jnp.where(bidx == 1, dscale, 0.0)

    any_spec = pl.BlockSpec(memory_space=pl.ANY)
    res = pl.pallas_call(
        body, name="dh_dx",
        grid=(T // tm, N_DEV),
        in_specs=[
            pl.BlockSpec((tm, SHARD), lambda i, jj: (i, jj)),
            pl.BlockSpec((None, D, SHARD), lambda i, jj: (jj, 0, 0)),
            pl.BlockSpec((tm, D), lambda i, jj: (i, 0)),
            pl.BlockSpec((tm, D), lambda i, jj: (i, 0)),
            pl.BlockSpec((1, 3, D), lambda i, jj: (i // per_seq, 0, 0))] + [any_spec] * (n_in - 5),
        out_specs=(pl.BlockSpec((tm, D), lambda i, jj: (i, 0)),
                   pl.BlockSpec((8, D), lambda i, jj: (0, 0))) + (any_spec,) * n,
        out_shape=(jax.ShapeDtypeStruct((T, D), F32), jax.ShapeDtypeStruct((8, D), F32))
                  + tuple(jax.ShapeDtypeStruct(a.shape, a.dtype) for a in chip_sums),
        scratch_shapes=[pltpu.VMEM((tm, D), F32), pltpu.SemaphoreType.DMA((max(3 * n, 1),)),
                        pltpu.SemaphoreType.DMA((max(3 * n, 1),)), pltpu.SemaphoreType.DMA((max(n, 1),))],
        input_output_aliases={} if parts0 is None else {5 + n: 2},
        compiler_params=pltpu.CompilerParams(vmem_limit_bytes=VMEM_LIMIT),
    )(dproj, w_in_all, x2, dxd, mod3, *chip_sums, *([] if parts0 is None else [parts0]))
    return res[0], res[1], res[2:]


def _mm_tn(a, b, tn, blocks_leading, name):
    kk, m = a.shape
    n = b.shape[1]
    tk = 2048

    def body(a_ref, b_ref, o_ref, acc):
        @pl.when(pl.program_id(1) == 0)
        def _():
            acc[...] = jnp.zeros_like(acc)

        acc[...] += _dot_tn(a_ref[...], b_ref[...])

        @pl.when(pl.program_id(1) == kk // tk - 1)
        def _():
            o_ref[...] = acc[...].astype(BF16)

    if blocks_leading:
        out_spec = pl.BlockSpec((None, m, tn), lambda j, k: (j, 0, 0))
        out_shape = jax.ShapeDtypeStruct((n // tn, m, tn), BF16)
    else:
        out_spec = pl.BlockSpec((m, tn), lambda j, k: (0, j))
        out_shape = jax.ShapeDtypeStruct((m, n), BF16)
    return pl.pallas_call(
        body, name=name,
        grid=(n // tn, kk // tk),
        in_specs=[pl.BlockSpec((tk, m), lambda j, k: (k, 0)),
                  pl.BlockSpec((tk, tn), lambda j, k: (k, j))],
        out_specs=out_spec, out_shape=out_shape,
        scratch_shapes=[pltpu.VMEM((m, tn), F32)],
        compiler_params=pltpu.CompilerParams(vmem_limit_bytes=VMEM_LIMIT),
    )(a, b)


def _adam_step(g, w, m, v):
    nm = ADAM_B1 * m + (1.0 - ADAM_B1) * g
    nv = ADAM_B2 * v + (1.0 - ADAM_B2) * (g * g)
    m_hat = nm / (1.0 - ADAM_B1 ** ADAM_STEP)
    v_hat = nv / (1.0 - ADAM_B2 ** ADAM_STEP)
    return -ADAM_LR * (m_hat / (jnp.sqrt(v_hat) + ADAM_EPS) + ADAM_WD * w), nm, nv


def _adamw(parts, w, m, v, name, row_tile=None):
    n_parts, rows, cols = parts.shape
    tr = rows if row_tile is None else row_tile

    def body(p_ref, w_ref, m_ref, v_ref, g_ref, d_ref, nm_ref, nv_ref):
        g = p_ref[0].astype(F32)
        for s in range(1, n_parts):
            g = g + p_ref[s].astype(F32)
        g_ref[...] = g
        d_ref[...], nm_ref[...], nv_ref[...] = _adam_step(g, w_ref[...], m_ref[...], v_ref[...])

    blk = pl.BlockSpec((tr, cols), lambda i: (i, 0))
    shp = jax.ShapeDtypeStruct((rows, cols), F32)
    return pl.pallas_call(
        body, name=name,
        grid=(rows // tr,),
        in_specs=[pl.BlockSpec((n_parts, tr, cols), lambda i: (0, i, 0)), blk, blk, blk],
        out_specs=(blk, blk, blk, blk),
        out_shape=(shp, shp, shp, shp),
        compiler_params=pltpu.CompilerParams(vmem_limit_bytes=VMEM_LIMIT),
    )(parts, w, m, v)


def _small_updates(small_g, dmod_all, rel_parts, params):
    flat = [t for wmv in params for t in wmv]

    def body(sg_ref, dm_ref, rp_ref, *refs):
        ins, outs = refs[:len(flat)], refs[len(flat):]

        def over_devices(row):
            tot = sg_ref[0, row:row + 1, :]
            for s in range(1, N_DEV):
                tot = tot + sg_ref[s, row:row + 1, :]
            return tot

        g_b_ada = dm_ref[0:1, :]
        for r in range(1, N_DEV * BL):
            g_b_ada = g_b_ada + dm_ref[r:r + 1, :]
        g_rel = rp_ref[0]
        for s in range(1, N_DEV):
            g_rel = g_rel + rp_ref[s]
        grads = [g_b_ada, over_devices(0), g_rel, over_devices(1), over_devices(2)]
        outs[0][...] = jnp.sum(over_devices(3), axis=1, keepdims=True)
        for p, g in enumerate(grads):
            w_ref, m_ref, v_ref = ins[3 * p:3 * p + 3]
            g_ref, d_ref, nm_ref, nv_ref = outs[1 + 4 * p:5 + 4 * p]
            g_ref[...] = g
            d_ref[...], nm_ref[...], nv_ref[...] = _adam_step(g, w_ref[...], m_ref[...], v_ref[...])

    out_shape = [jax.ShapeDtypeStruct((1, 1), F32)]
    for w, _, _ in params:
        out_shape += [jax.ShapeDtypeStruct(w.shape, F32)] * 4
    res = pl.pallas_call(body, name="small_updates", out_shape=tuple(out_shape))(small_g, dmod_all, rel_parts, *flat)
    return res[0], [res[1 + 4 * p:5 + 4 * p] for p in range(len(params))]


def _attention_forward(proj, rel_bias, rides=((), (), ())):
    buckets_np, masks_np = _bucket_maps()
    buckets, masks = jnp.asarray(buckets_np), jnp.asarray(masks_np)
    bias = _bias_expand(rel_bias, buckets, masks)
    fwd = [_attn_fwd(proj, bias, g, rides[g]) for g in range(3)]
    brought = [arr for f in fwd for arr in f[2:]]
    return bias, buckets, [f[0] for f in fwd], [f[1] for f in fwd], brought


def _local_step(x2, tgt2, mod3, h, proj, attn, w_ao, w_co, w_o, conv_w, conv_b, ln_g, ln_b):
    bias, buckets, o_g, lse_g = attn

    (dproj, dyc, d_o, stats, dxd, merged, dy, a_in, da_out, s_in, ds_out, tail_vec) = _tail(
        x2, tgt2, mod3, o_g, lse_g, proj, w_ao, w_co, w_o, conv_w, conv_b, ln_g, ln_b)

    dbias = []
    for g in range(3):
        dproj, db = _attn_bwd(proj, d_o, stats, bias, dproj, g)
        dbias.append(db)
    g_rel_bias = _bias_grad(*dbias, buckets)
    dproj, conv_vec = _conv_bwd(dyc, proj, conv_w, dproj)

    gw_o = _mm_tn(merged, dy, D, False, "gw_o")
    gw_co = _mm_tn(s_in, ds_out, D, False, "gw_conv_out")
    gw_ao = _mm_tn(a_in, da_out, D, False, "gw_attn_out")
    gw_ao = jnp.transpose(gw_ao.reshape(GW, N_DEV, D // N_DEV), (1, 0, 2))
    return dproj, dxd, gw_ao, gw_co, gw_o, conv_vec, g_rel_bias, tail_vec


def kernel(x, c, w_ada, b_ada, w_in, conv_w, conv_b, rel_bias, w_attn_out, w_conv_out, w_o, ln_g, ln_b, loss_target, m_w_ada, m_b_ada, m_w_in, m_conv_w, m_conv_b, m_rel_bias, m_w_attn_out, m_w_conv_out, m_w_o, m_ln_g, m_ln_b, v_w_ada, v_b_ada, v_w_in, v_conv_w, v_conv_b, v_rel_bias, v_w_attn_out, v_w_conv_out, v_w_o, v_ln_g, v_ln_b):
    me = _my_index()
    x2 = x.reshape(T, D)
    tgt2 = loss_target.reshape(T, D)

    b_cols = lax.dynamic_slice(b_ada, (0, me * ADA_SHARD), (1, ADA_SHARD))
    c_g, mod_in = _mod_exchange(jnp.pad(c, ((0, 8 - BL), (0, 0))), w_ada[0], b_cols)
    c_all = c_g[:, 0:BL, :].reshape(N_DEV * BL, D)
    mod3 = jnp.transpose(mod_in[:, 0:BL, :], (1, 0, 2)).reshape(BL, 3, D)

    h = _prep_h(x2, mod3)
    rows_shape = jax.ShapeDtypeStruct((N_DEV, D // N_DEV, D), BF16)
    proj, w_in_all, (w_ao_g, w_co_g, w_o_g, conv_w_g) = _gather_proj(
        _shard_order(), h, w_in[0].astype(BF16), 1024,
        ([w_attn_out[0].astype(BF16), w_conv_out[0].astype(BF16), w_o[0].astype(BF16), conv_w[0]],
         [jax.ShapeDtypeStruct((N_DEV, GW, D // N_DEV), BF16), rows_shape, rows_shape,
          jax.ShapeDtypeStruct((N_DEV, 3, D // N_DEV), F32)]))

    *attn, _ = _attention_forward(proj, rel_bias)
    w_ao_full = jnp.transpose(w_ao_g, (1, 0, 2)).reshape(GW, D)
    w_co_full = w_co_g.reshape(D, D)
    w_o_full = w_o_g.reshape(D, D)
    conv_w_full = jnp.transpose(conv_w_g, (1, 0, 2)).reshape(3, D)

    (dproj, dxd, gw_ao, gw_co, gw_o, conv_vec, g_rel_bias, tail_vec) = _local_step(
        x2, tgt2, mod3, h, proj, attn, w_ao_full, w_co_full, w_o_full,
        conv_w_full, conv_b, ln_g, ln_b)

    g_conv_w_blocks = jnp.transpose(conv_vec[0:3].reshape(3, N_DEV, D // N_DEV), (1, 0, 2))
    partials = [gw_ao, gw_co.reshape(N_DEV, D // N_DEV, D), gw_o.reshape(N_DEV, D // N_DEV, D), g_conv_w_blocks]
    w_in_sums, w_in_parts, sib = _gw_in_pair(
        _slice_order(), h, dproj, partials,
        [jax.ShapeDtypeStruct((4, GW, D // N_DEV), BF16),
         jax.ShapeDtypeStruct((4, D // N_DEV, D), BF16),
         jax.ShapeDtypeStruct((4, D // N_DEV, D), BF16),
         jax.ShapeDtypeStruct((4, 3, D // N_DEV), F32)])
    names = ["w_attn_out", "w_conv_out", "w_o", "conv_w"]
    core = lax.axis_index("c").astype(jnp.int32).reshape(1)
    chip_sums = [w_in_sums] + [_pair_add(core, partials[a], sib[a], None, "pair_add_" + names[a])
                               for a in range(4)]
    hops = [(3,)] + [(1, 2, 3)] * 4
    grad_x, mod_vec, (r_in, r_ao, r_co, r_o, r_cw) = _dh_dx(
        dproj, w_in_all, x2, dxd, mod3, chip_sums, hops, w_in_parts)

    small = jnp.concatenate([
        tail_vec[0:4],
        jnp.pad(g_rel_bias.reshape(1, N_BUCKETS * N_HEADS), ((0, 0), (0, D - N_BUCKETS * N_HEADS))),
        jnp.zeros((3, D), F32)], axis=0)
    dmod = jnp.concatenate([mod_vec[0:2], mod_vec[2:4], tail_vec[4:6]], axis=1)
    small_g, dmod_g = _all_gather(
        [small, dmod],
        [jax.ShapeDtypeStruct((N_DEV, 8, D), F32), jax.ShapeDtypeStruct((N_DEV, BL, 3 * D), F32)],
        "gather_small")
    dmod_all = dmod_g.reshape(N_DEV * BL, 3 * D)
    small_names = ["b_ada", "conv_b", "rel_bias", "ln_g", "ln_b"]
    small_params = [(b_ada, m_b_ada, v_b_ada), (conv_b, m_conv_b, v_conv_b), (rel_bias, m_rel_bias, v_rel_bias),
                    (ln_g, m_ln_g, v_ln_g), (ln_b, m_ln_b, v_ln_b)]
    loss, small_res = _small_updates(
        small_g, dmod_all, small_g[:, 4, :N_BUCKETS * N_HEADS].reshape(N_DEV, N_BUCKETS, N_HEADS), small_params)
    loss = loss.reshape(())
    g_w_ada = _ada_bwd(jnp.transpose(c_all), lax.dynamic_slice(dmod_all, (0, me * ADA_SHARD),
                                                               (N_DEV * BL, ADA_SHARD)))

    def upd(parts, w, m, v, name, row_tile=None):
        shape = w.shape
        w2, m2, v2 = (t.reshape(parts.shape[1:]) for t in (w, m, v))
        return tuple(t.reshape(shape) for t in _adamw(parts, w2, m2, v2, name, row_tile))

    res = {
        "w_ada": upd(g_w_ada[None], w_ada, m_w_ada, v_w_ada, "adam_w_ada", 256),
        "w_in": upd(r_in, w_in, m_w_in, v_w_in, "adam_w_in", 128),
        "conv_w": upd(r_cw, conv_w, m_conv_w, v_conv_w, "adam_conv_w"),
        "w_attn_out": upd(r_ao, w_attn_out, m_w_attn_out, v_w_attn_out, "adam_w_attn_out"),
        "w_conv_out": upd(r_co, w_conv_out, m_w_conv_out, v_w_conv_out, "adam_w_conv_out"),
        "w_o": upd(r_o, w_o, m_w_o, v_w_o, "adam_w_o"),
    }
    res.update(dict(zip(small_names, small_res)))
    order = ["w_ada", "b_ada", "w_in", "conv_w", "conv_b", "rel_bias", "w_attn_out", "w_conv_out",
             "w_o", "ln_g", "ln_b"]
    outs = [loss, grad_x.reshape(BL, S, D)]
    for k in range(4):
        outs += [res[name][k] for name in order]
    return tuple(outs)
```

```python
import math

import numpy as np
import jax
import jax.numpy as jnp
from jax import lax
from jax.experimental import pallas as pl
from jax.experimental.pallas import tpu as pltpu

F32 = jnp.float32
BF16 = jnp.bfloat16
MESH = pl.DeviceIdType.MESH

N_DEV = 8
D = 1024
S = 2048
BL = 2
T = BL * S
NCOL = 11264
SHARD = NCOL // N_DEV
CB = 512
NCB = NCOL // CB
HD = 128
GW = 512
QB = 128
DILATIONS = (1, 4, 16)
N_STEPS = 128
N_BUCKETS = 32
N_HEADS = 12
ALPHA = 2.0 ** 0.25
LN_EPS = 1e-5
NEG_INF = -1e30
SCALE = HD ** -0.5
ADA_SHARD = 3 * D // N_DEV

CB_Q, CB_K, CB_V, CB_GA = 0, 3, 6, 9
KB_U, KB_BG, KB_CG, KB_GC, KB_MA, KB_MC = 5, 6, 7, 8, 9, 10

ADAM_LR, ADAM_B1, ADAM_B2, ADAM_EPS, ADAM_WD, ADAM_STEP = 0.001, 0.9, 0.999, 1e-08, 0.01, 10

VMEM_LIMIT = 56 * 1024 * 1024


def _dot(a, b):
    return jnp.dot(a, b, preferred_element_type=F32)


def _dot_nt(a, b):
    return lax.dot_general(a, b, (((1,), (1,)), ((), ())), preferred_element_type=F32)


def _dot_tn(a, b):
    return lax.dot_general(a, b, (((0,), (0,)), ((), ())), preferred_element_type=F32)


def _sigmoid(v):
    return 1.0 / (1.0 + jnp.exp(-v))


def _write_columns(pieces, dst_hbm, row0, sems):
    copies = []
    for k, (src, col0) in enumerate(pieces):
        rows, width = src.shape
        copies.append(pltpu.make_async_copy(
            src, dst_hbm.at[pl.ds(row0, rows), pl.ds(col0, width)], sems.at[k]))
    for cp in copies:
        cp.start()
    for cp in copies:
        cp.wait()


def _my_index():
    return 4 * lax.axis_index("x") + 2 * lax.axis_index("y") + lax.axis_index("c")


class _Gather:
    def __init__(self, ins, outs, stage, send_sems, recv_sems, local_sems):
        self.ins, self.outs, self.stage = ins, outs, stage
        self.send_sems, self.recv_sems, self.local_sems = send_sems, recv_sems, local_sems
        x, y, c = lax.axis_index("x"), lax.axis_index("y"), lax.axis_index("c")
        self.c = c
        self.me, self.sibling = (x, y, c), (x, y, 1 - c)
        self.chips = [(1 - x, y), (x, 1 - y), (1 - x, 1 - y)]

    @staticmethod
    def scratch(arrs):
        n = len(arrs)
        return ([pltpu.SemaphoreType.DMA((7 * n,)), pltpu.SemaphoreType.DMA((7 * n,)),
                 pltpu.SemaphoreType.DMA((n,))] + [pltpu.VMEM(a.shape, a.dtype) for a in arrs])

    def _copy(self, a, k, block, to, src=None):
        dst = self.outs[a].at[4 * block[0] + 2 * block[1] + block[2]]
        return pltpu.make_async_remote_copy(
            src_ref=dst if src is None else src, dst_ref=dst,
            send_sem=self.send_sems.at[a * 7 + k], recv_sem=self.recv_sems.at[a * 7 + k],
            device_id=to, device_id_type=MESH)

    def _first(self):
        first = []
        for a in range(len(self.ins)):
            first.append(self._copy(a, 0, self.me, self.sibling, src=self.ins[a]))
            first += [self._copy(a, 1 + j, self.me, (*chip, self.c), src=self.ins[a])
                      for j, chip in enumerate(self.chips)]
        return first

    def _mine(self):
        me = self.me
        return [pltpu.make_async_copy(self.stage[a], self.outs[a].at[4 * me[0] + 2 * me[1] + me[2]],
                                      self.local_sems.at[a]) for a in range(len(self.ins))]

    def begin(self):
        for cp in self._first():
            cp.start()
        loads = [pltpu.make_async_copy(self.ins[a], self.stage[a], self.local_sems.at[a])
                 for a in range(len(self.ins))]
        for cp in loads:
            cp.start()
        for cp in loads:
            cp.wait()
        for cp in self._mine():
            cp.start()

    def finish(self):
        n, c, me, sibling = len(self.ins), self.c, self.me, self.sibling
        passed = []
        for j, chip in enumerate(self.chips):
            for a in range(n):
                self._copy(a, 1 + j, (*chip, c), me).wait_recv()
                fwd = self._copy(a, 4 + j, (*chip, c), sibling)
                fwd.start()
                passed.append(fwd)
        for a in range(n):
            self._copy(a, 0, sibling, me).wait_recv()
        for j, chip in enumerate(self.chips):
            for a in range(n):
                self._copy(a, 4 + j, (*chip, 1 - c), me).wait_recv()
        for cp in self._first() + passed:
            cp.wait_send()
        for cp in self._mine():
            cp.wait()


def _all_gather(arrs, out_shapes, name):
    n = len(arrs)

    def body(*refs):
        g = _Gather(refs[:n], refs[n:2 * n], refs[2 * n + 3:], *refs[2 * n:2 * n + 3])
        g.begin()
        g.finish()

    any_spec = pl.BlockSpec(memory_space=pl.ANY)
    return pl.pallas_call(
        body, name=name,
        out_shape=tuple(out_shapes),
        in_specs=[any_spec] * n,
        out_specs=tuple([any_spec] * n),
        scratch_shapes=_Gather.scratch(arrs),
    )(*arrs)


def _slice_order():
    x, y, c = lax.axis_index("x"), lax.axis_index("y"), lax.axis_index("c")
    slots = []
    for q in (2 * (1 - x) + y, 2 * x + (1 - y), 2 * (1 - x) + (1 - y), 2 * x + y):
        slots += [2 * q + 1 - c, 2 * q + c]
    return jnp.stack(slots).astype(jnp.int32)


def _gw_in_pair(order, h, dproj, smalls, small_shapes4):
    kk, m = h.shape
    tk = min(kk, 2048)
    nk = kk // tk
    ncols = dproj.shape[1] // N_DEV
    n = len(smalls)

    def body(order_ref, h_ref, d_ref, *rest):
        ins = rest[:n]
        sums_hbm, parts_hbm = rest[n], rest[n + 1]
        sib = rest[n + 2:2 * n + 2]
        (acc, sendbuf, recvbuf, sumbuf, send_sems, recv_sems, local_sem, ssend, srecv,
         isend, irecv) = rest[2 * n + 2:]
        js, k = pl.program_id(0), pl.program_id(1)
        x, y, c = lax.axis_index("x"), lax.axis_index("y"), lax.axis_index("c")
        sibling = (x, y, 1 - c)
        my_chip = 2 * x + y
        near = [(1 - x, y, c), (x, 1 - y, c)]

        def ici_copy(p, out_chip):
            peer = near[p]
            return pltpu.make_async_remote_copy(
                src_ref=sumbuf.at[p], dst_ref=parts_hbm.at[out_chip],
                send_sem=isend.at[p], recv_sem=irecv.at[p], device_id=peer, device_id_type=MESH)

        def small_copies():
            return [pltpu.make_async_remote_copy(
                        src_ref=ins[a].at[2 * q + 1 - c], dst_ref=sib[a].at[q],
                        send_sem=ssend.at[a * 4 + q], recv_sem=srecv.at[a * 4 + q],
                        device_id=sibling, device_id_type=MESH)
                    for a in range(n) for q in range(4)]

        def slice_copy(p):
            return pltpu.make_async_remote_copy(
                src_ref=sendbuf, dst_ref=recvbuf.at[p], send_sem=send_sems.at[p], recv_sem=recv_sems.at[p],
                device_id=sibling, device_id_type=MESH)

        def sum_copy(p):
            return pltpu.make_async_copy(sumbuf.at[2], sums_hbm.at[order_ref[2 * p] // 2], local_sem)

        @pl.when((js == 0) & (k == 0))
        def _():
            for cp in small_copies():
                cp.start()

        @pl.when(k == 0)
        def _():
            acc[...] = jnp.zeros_like(acc)

        acc[...] += _dot_tn(h_ref[...], d_ref[...])

        for p in range(4):
            @pl.when((js == 2 * p) & (k == nk - 1))
            def _():
                if p > 0:
                    slice_copy(p - 1).wait_send()
                sendbuf[...] = acc[...].astype(BF16)
                slice_copy(p).start()

            @pl.when((js == 2 * p + 1) & (k == nk - 1))
            def _():
                slice_copy(p).wait_recv()
                if p == 3:
                    sum_copy(2).wait()
                sumbuf[min(p, 2)] = (acc[...] + recvbuf[p].astype(F32)).astype(BF16)
                if p < 2:
                    ici_copy(p, my_chip).start()
                else:
                    sum_copy(p).start()

        @pl.when((js == N_DEV - 1) & (k == nk - 1))
        def _():
            slice_copy(3).wait_send()
            sum_copy(3).wait()
            for cp in small_copies():
                cp.wait()
            for p in range(2):
                ici_copy(p, 2 * near[p][0] + near[p][1]).wait_recv()
                ici_copy(p, my_chip).wait_send()

    any_spec = pl.BlockSpec(memory_space=pl.ANY)
    res = pl.pallas_call(
        body, name="gw_in_pair",
        grid_spec=pltpu.PrefetchScalarGridSpec(
            num_scalar_prefetch=1,
            grid=(N_DEV, nk),
            in_specs=[pl.BlockSpec((tk, m), lambda js, k, order_ref: (k, 0)),
                      pl.BlockSpec((tk, ncols), lambda js, k, order_ref: (k, order_ref[js]))] + [any_spec] * n,
            out_specs=(any_spec,) * (n + 2),
            scratch_shapes=[pltpu.VMEM((m, ncols), F32), pltpu.VMEM((m, ncols), BF16),
                            pltpu.VMEM((4, m, ncols), BF16), pltpu.VMEM((3, m, ncols), BF16),
                            pltpu.SemaphoreType.DMA((4,)), pltpu.SemaphoreType.DMA((4,)),
                            pltpu.SemaphoreType.DMA,
                            pltpu.SemaphoreType.DMA((4 * n,)), pltpu.SemaphoreType.DMA((4 * n,)),
                            pltpu.SemaphoreType.DMA((2,)), pltpu.SemaphoreType.DMA((2,))]),
        out_shape=(jax.ShapeDtypeStruct((4, m, ncols), BF16),) * 2 + tuple(small_shapes4),
        compiler_params=pltpu.CompilerParams(vmem_limit_bytes=VMEM_LIMIT),
    )(order, h, dproj, *smalls)
    return res[0], res[1], res[2:]


def _chip_copies(ins, outs, send_sems, recv_sems, local_sems, hops):
    n = len(ins)
    x, y, c = lax.axis_index("x"), lax.axis_index("y"), lax.axis_index("c")
    my_chip = 2 * x + y

    def peer_of(k):
        return ((1 - x) if (k >> 1) & 1 else x, (1 - y) if k & 1 else y, c)

    def copy(a, k, out_chip):
        peer = peer_of(k)
        return pltpu.make_async_remote_copy(
            src_ref=ins[a].at[2 * peer[0] + peer[1]], dst_ref=outs[a].at[out_chip],
            send_sem=send_sems.at[a * 3 + k - 1], recv_sem=recv_sems.at[a * 3 + k - 1],
            device_id=peer, device_id_type=MESH)

    sends = [copy(a, k, my_chip) for k in range(1, 4) for a in range(n) if k in hops[a]]
    arrivals = []
    for k in range(1, 4):
        peer = peer_of(k)
        arrivals += [copy(a, k, 2 * peer[0] + peer[1]) for a in range(n) if k in hops[a]]
    mine = [pltpu.make_async_copy(ins[a].at[my_chip], outs[a].at[my_chip], local_sems.at[a])
            for a in range(n)]
    return sends, arrivals, mine


def _pair_add(core, mine, theirs, row_tile, name):
    _, rows, cols = theirs.shape
    tr = rows if row_tile is None else row_tile

    def body(core_ref, a_ref, b_ref, o_ref):
        o_ref[...] = (a_ref[...].astype(F32) + b_ref[...].astype(F32)).astype(o_ref.dtype)

    blk = pl.BlockSpec((None, tr, cols), lambda q, i, core_ref: (q, i, 0))
    return pl.pallas_call(
        body, name=name,
        grid_spec=pltpu.PrefetchScalarGridSpec(
            num_scalar_prefetch=1,
            grid=(4, rows // tr),
            in_specs=[pl.BlockSpec((None, tr, cols), lambda q, i, core_ref: (2 * q + core_ref[0], i, 0)), blk],
            out_specs=blk),
        out_shape=jax.ShapeDtypeStruct(theirs.shape, theirs.dtype),
    )(core, mine, theirs)


def _mod_exchange(c8, w_ada, b_cols):
    cols = w_ada.shape[1]

    def body(c_ref, w_ref, b_ref, call_ref, mod_ref, msend, send1, recv1, send2, recv2):
        x, y, c = lax.axis_index("x"), lax.axis_index("y"), lax.axis_index("c")
        my_slot = 4 * x + 2 * y + c

        def peer_of(k):
            return ((1 - x) if (k >> 2) & 1 else x, (1 - y) if (k >> 1) & 1 else y, (1 - c) if k & 1 else c)

        def slot_of(dev):
            return 4 * dev[0] + 2 * dev[1] + dev[2]

        def exchange(src_of, dst_ref, send_sems, recv_sems):
            sends, arrivals = [], []
            for k in range(1, 8):
                peer = peer_of(k)
                sends.append(pltpu.make_async_remote_copy(
                    src_ref=src_of(slot_of(peer)), dst_ref=dst_ref.at[my_slot],
                    send_sem=send_sems.at[k - 1], recv_sem=recv_sems.at[k - 1],
                    device_id=peer, device_id_type=MESH))
                arrivals.append(pltpu.make_async_remote_copy(
                    src_ref=src_of(my_slot), dst_ref=dst_ref.at[slot_of(peer)],
                    send_sem=send_sems.at[k - 1], recv_sem=recv_sems.at[k - 1],
                    device_id=peer, device_id_type=MESH))
            for cp in sends:
                cp.start()
            for cp in arrivals:
                cp.wait_recv()
            for cp in sends:
                cp.wait_send()

        call_ref[my_slot] = c_ref[...]
        exchange(lambda s: c_ref, call_ref, send1, recv1)
        cv = call_ref[...].reshape(N_DEV * 8, c_ref.shape[1])
        act = cv * _sigmoid(cv)
        mod = jnp.dot(act, w_ref[...], preferred_element_type=F32,
                      precision=lax.Precision.HIGHEST) + b_ref[...]
        msend[...] = mod.reshape(N_DEV, 8, cols)
        mod_ref[my_slot] = msend[my_slot]
        exchange(lambda s: msend.at[s], mod_ref, send2, recv2)

    return pl.pallas_call(
        body, name="mod_exchange",
        out_shape=(jax.ShapeDtypeStruct((N_DEV, 8, c8.shape[1]), F32),
                   jax.ShapeDtypeStruct((N_DEV, 8, cols), F32)),
        scratch_shapes=[pltpu.VMEM((N_DEV, 8, cols), F32)] + [pltpu.SemaphoreType.DMA((7,))] * 4,
    )(c8, w_ada, b_cols)


def _ada_bwd(c_all_t, dmod_cols):
    def body(c_ref, d_ref, o_ref):
        cv = c_ref[...]
        sc = cv * _sigmoid(cv)
        o_ref[...] = jnp.dot(sc, d_ref[...], preferred_element_type=F32,
                             precision=lax.Precision.HIGHEST)

    return pl.pallas_call(
        body, name="ada_bwd",
        out_shape=jax.ShapeDtypeStruct((c_all_t.shape[0], dmod_cols.shape[1]), F32),
    )(c_all_t, dmod_cols)


def _shard_order():
    x, y, c = lax.axis_index("x"), lax.axis_index("y"), lax.axis_index("c")
    devs = [(x, y, c), (x, y, 1 - c)]
    for chip in [(1 - x, y), (x, 1 - y), (1 - x, 1 - y)]:
        devs += [(*chip, c), (*chip, 1 - c)]
    return jnp.stack([4 * d[0] + 2 * d[1] + d[2] for d in devs]).astype(jnp.int32)


def _prep_h(x2, mod3):
    ts = 512
    per_seq = S // ts

    def body(x_ref, mod_ref, h_ref):
        shift = mod_ref[0, 0:1, :]
        scale = mod_ref[0, 1:2, :]
        h_ref[...] = (x_ref[...] * (1.0 + scale) + shift).astype(BF16)

    return pl.pallas_call(
        body, name="prep_h",
        grid=(T // ts,),
        in_specs=[pl.BlockSpec((ts, D), lambda i: (i, 0)),
                  pl.BlockSpec((1, 3, D), lambda i: (i // per_seq, 0, 0))],
        out_specs=pl.BlockSpec((ts, D), lambda i: (i, 0)),
        out_shape=jax.ShapeDtypeStruct((T, D), BF16),
    )(x2, mod3)


def _gather_proj(order, h, w_shard, tm, ride=()):
    rows, kdim = h.shape
    ncols = w_shard.shape[1]
    n_i = rows // tm
    ride_arrs, ride_shapes = ride if ride else ((), ())
    n_ride = len(ride_arrs)

    def body(order_ref, h_ref, mine_hbm, *rest):
        ride_ins = rest[:n_ride]
        o_ref, all_hbm = rest[n_ride:n_ride + 2]
        ride_outs = rest[n_ride + 2:2 * n_ride + 2]
        wv, send_sems, recv_sems, local_sems = rest[2 * n_ride + 2:2 * n_ride + 6]
        ride_scr = rest[2 * n_ride + 6:]
        j, i = pl.program_id(0), pl.program_id(1)
        x, y, c = lax.axis_index("x"), lax.axis_index("y"), lax.axis_index("c")
        me, sibling = (x, y, c), (x, y, 1 - c)
        chips = [(1 - x, y), (x, 1 - y), (1 - x, 1 - y)]

        def slot(dev):
            return 4 * dev[0] + 2 * dev[1] + dev[2]

        def copy(k, block, to):
            return pltpu.make_async_remote_copy(
                src_ref=wv.at[slot(block)], dst_ref=wv.at[slot(block)],
                send_sem=send_sems.at[k], recv_sem=recv_sems.at[k],
                device_id=to, device_id_type=MESH)

        def keep(step, block):
            return pltpu.make_async_copy(wv.at[slot(block)], all_hbm.at[slot(block)], local_sems.at[step])

        if n_ride:
            gather = _Gather(ride_ins, ride_outs, ride_scr[3:], *ride_scr[:3])
        first = [copy(0, me, sibling)] + [copy(1 + q, me, (*chip, c)) for q, chip in enumerate(chips)]
        passed = [copy(4 + q, (*chip, c), sibling) for q, chip in enumerate(chips)]
        due = [(me, None, None), (sibling, copy(0, sibling, me), None)]
        for q, chip in enumerate(chips):
            due.append(((*chip, c), copy(1 + q, (*chip, c), me), passed[q]))
            due.append(((*chip, 1 - c), copy(4 + q, (*chip, 1 - c), me), None))

        @pl.when((j == 0) & (i == 0))
        def _():
            load = pltpu.make_async_copy(mine_hbm, wv.at[slot(me)], local_sems.at[N_DEV])
            load.start()
            load.wait()
            for cp in first:
                cp.start()
            keep(0, me).start()

        for step in range(1, N_DEV):
            block, arrival, forward = due[step]

            @pl.when((j == step) & (i == 0))
            def _():
                arrival.wait_recv()
                if forward is not None:
                    forward.start()
                keep(step, block).start()
                if n_ride and step == N_DEV - 2:
                    gather.begin()

        o_ref[...] = _dot(h_ref[...], wv[order_ref[j]]).astype(BF16)

        @pl.when((j == N_DEV - 1) & (i == n_i - 1))
        def _():
            for cp in first + passed:
                cp.wait_send()
            for step in range(N_DEV):
                keep(step, due[step][0]).wait()
            if n_ride:
                gather.finish()

    any_spec = pl.BlockSpec(memory_space=pl.ANY)
    res = pl.pallas_call(
        body, name="gather_proj",
        grid_spec=pltpu.PrefetchScalarGridSpec(
            num_scalar_prefetch=1,
            grid=(N_DEV, n_i),
            in_specs=[pl.BlockSpec((tm, kdim), lambda j, i, order_ref: (i, 0)), any_spec] + [any_spec] * n_ride,
            out_specs=(pl.BlockSpec((tm, ncols), lambda j, i, order_ref: (i, order_ref[j])), any_spec)
                      + (any_spec,) * n_ride,
            scratch_shapes=[pltpu.VMEM((N_DEV, kdim, ncols), BF16),
                            pltpu.SemaphoreType.DMA((7,)), pltpu.SemaphoreType.DMA((7,)),
                            pltpu.SemaphoreType.DMA((N_DEV + 1,))]
                           + (_Gather.scratch(ride_arrs) if n_ride else [])),
        out_shape=(jax.ShapeDtypeStruct((rows, N_DEV * ncols), BF16),
                   jax.ShapeDtypeStruct((N_DEV, kdim, ncols), BF16)) + tuple(ride_shapes),
        compiler_params=pltpu.CompilerParams(vmem_limit_bytes=VMEM_LIMIT),
    )(order, h, w_shard, *ride_arrs)
    return res[0], res[1], res[2:]


def _bucket_maps():
    a = np.arange(QB)[:, None]
    b = np.arange(2 * QB)[None, :]
    steps = a + QB - b
    maps = []
    for dil in DILATIONS:
        dist = np.maximum(steps, 0) * dil
        nf = np.maximum(dist, 1).astype(np.float32)
        large = 16 + (np.log(nf / np.float32(16)) / np.float32(math.log(128.0))
                      * np.float32(16)).astype(np.int32)
        large = np.minimum(large, N_BUCKETS - 1)
        maps.append(np.where(dist < 16, dist, large).astype(np.int32))
    band = (steps >= 0) & (steps <= N_STEPS)
    first = band & (b >= QB)
    masks = np.stack([first, band]).astype(np.int32)
    return np.stack(maps), masks


def _bias_expand(rel_bias, buckets, masks):
    def body(tab_ref, bk_ref, mk_ref, o_ref):
        for g in range(3):
            bk = bk_ref[g]
            for h in range(4):
                col = 4 * g + h
                val = jnp.zeros((QB, 2 * QB), F32)
                for k in range(N_BUCKETS):
                    val = jnp.where(bk == k, tab_ref[k, col], val)
                o_ref[g, 0, h] = jnp.where(mk_ref[0] != 0, val, NEG_INF)
                o_ref[g, 1, h] = jnp.where(mk_ref[1] != 0, val, NEG_INF)

    return pl.pallas_call(
        body, name="bias_expand",
        in_specs=[pl.BlockSpec(memory_space=pltpu.SMEM),
                  pl.BlockSpec(memory_space=pltpu.VMEM),
                  pl.BlockSpec(memory_space=pltpu.VMEM)],
        out_shape=jax.ShapeDtypeStruct((3, 2, 4, QB, 2 * QB), F32),
    )(rel_bias, buckets, masks)


def _bias_grad(ds1, ds2, ds3, buckets):
    def body(d1_ref, d2_ref, d3_ref, bk_ref, o_ref):
        for g, d_ref in enumerate((d1_ref, d2_ref, d3_ref)):
            bk = bk_ref[g]
            for h in range(4):
                dv = d_ref[h]
                for k in range(N_BUCKETS):
                    o_ref[k, 4 * g + h] = jnp.sum(jnp.where(bk == k, dv, 0.0))

    return pl.pallas_call(
        body, name="bias_grad",
        in_specs=[pl.BlockSpec(memory_space=pltpu.VMEM)] * 4,
        out_specs=pl.BlockSpec(memory_space=pltpu.SMEM),
        out_shape=jax.ShapeDtypeStruct((N_BUCKETS, N_HEADS), F32),
    )(ds1, ds2, ds3, buckets)


def _scratch_sets(rows):
    return 4 if rows <= 512 else 1


def _unit_chunks(dil, size=16):
    units = [(h, r) for h in range(4) for r in range(dil)]
    return [units[i:i + size] for i in range(0, len(units), size)]


def _residue_rows(src_ref, copies, h, residue):
    sl = slice(h * HD, (h + 1) * HD)
    if copies is None:
        return lambda r: src_ref[:, sl]
    buf = copies[h % len(copies)]
    buf[...] = src_ref[:, sl].astype(F32)
    return lambda r: buf[residue(r), :].astype(BF16)


def _attn_fwd(proj, bias, g, ride=()):
    dil = DILATIONS[g]
    rows = QB * dil
    nsb = S // rows
    has_prev = nsb > 1

    def residue(r):
        return pl.ds(r, QB, stride=dil) if dil > 1 else pl.ds(0, QB)

    strided = dil > 1
    n_sets = _scratch_sets(rows)
    n_attn = 6 if has_prev else 4
    ride_arrs, ride_shapes = ride if ride else ((), ())
    n_ride = len(ride_arrs)
    n_in = n_attn + n_ride
    n_copied = (4 + (2 if has_prev else 0)) * (n_sets if strided else 0)

    def body(*refs):
        q_ref, kc_ref, vc_ref = refs[:3]
        kp_ref, vp_ref = refs[3:5] if has_prev else (None, None)
        b_ref = refs[n_attn - 1]
        o_ref, l_ref = refs[n_in:n_in + 2]
        scr = list(refs[n_in + 2 + n_ride:])
        ls = [scr.pop(0) for _ in range(4)]
        copies = {name: [scr.pop(0) for _ in range(n_sets)] if strided else None
                  for name in ("q", "kc", "vc", "o") + (("kp", "vp") if has_prev else ())}
        if n_ride:
            gather = _Gather(refs[n_attn:n_in], refs[n_in + 2:n_in + 2 + n_ride], scr[3:], *scr[:3])

            @pl.when((pl.program_id(0) == 0) & (pl.program_id(1) == 0))
            def _():
                gather.begin()
        lane = lax.broadcasted_iota(jnp.int32, (QB, 128), 1)
        refs_of = {"q": q_ref, "kc": kc_ref, "vc": vc_ref, "kp": kp_ref, "vp": vp_ref}
        for chunk in _unit_chunks(dil):
            rows_of = {h: {name: _residue_rows(refs_of[name], copies[name], h, residue)
                           for name in refs_of if refs_of[name] is not None}
                       for h in sorted({h for h, _ in chunk})}

            def batch(name):
                return jnp.stack([rows_of[h][name](r) for h, r in chunk])

            q, k, v = batch("q"), batch("kc"), batch("vc")
            if has_prev:
                k = jnp.concatenate([batch("kp"), k], axis=1)
                v = jnp.concatenate([batch("vp"), v], axis=1)
                bias_b = jnp.stack([b_ref[h] for h, _ in chunk])
            else:
                bias_b = jnp.stack([b_ref[h, :, QB:] for h, _ in chunk])
            s = jnp.einsum("uqd,ukd->uqk", q, k, preferred_element_type=F32) * SCALE + bias_b
            m = jnp.max(s, axis=-1, keepdims=True)
            p = jnp.exp(s - m)
            l = jnp.sum(p, axis=-1, keepdims=True)
            o = jnp.einsum("uqk,ukd->uqd", p.astype(BF16), v, preferred_element_type=F32) / l
            lse = m + jnp.log(l)
            for i, (h, r) in enumerate(chunk):
                if strided:
                    copies["o"][h % n_sets][residue(r), :] = o[i]
                else:
                    o_ref[:, h * HD:(h + 1) * HD] = o[i]
                ls[h][r * QB:(r + 1) * QB, :] = jnp.where(lane == h, lse[i], 0.0)
            if strided:
                for h in sorted({h for h, _ in chunk}):
                    o_ref[:, h * HD:(h + 1) * HD] = copies["o"][h % n_sets][...]
        for r in range(dil):
            blk = slice(r * QB, (r + 1) * QB)
            l_ref[residue(r), :] = (ls[0][blk, :] + ls[1][blk, :]) + (ls[2][blk, :] + ls[3][blk, :])
        if n_ride:
            @pl.when((pl.program_id(0) == BL - 1) & (pl.program_id(1) == nsb - 1))
            def _():
                gather.finish()

    def row(b, n):
        return b * nsb + n

    def prev(b, n):
        return b * nsb + jnp.maximum(n - 1, 0)

    in_specs = [
        pl.BlockSpec((rows, GW), lambda b, n: (row(b, n), CB_Q + g)),
        pl.BlockSpec((rows, GW), lambda b, n: (row(b, n), CB_K + g)),
        pl.BlockSpec((rows, GW), lambda b, n: (row(b, n), CB_V + g)),
    ]
    args = [proj, proj, proj]
    scratch = [pltpu.VMEM((rows, 128), F32)] * (4 + n_copied)
    if has_prev:
        in_specs += [pl.BlockSpec((rows, GW), lambda b, n: (prev(b, n), CB_K + g)),
                     pl.BlockSpec((rows, GW), lambda b, n: (prev(b, n), CB_V + g))]
        args += [proj, proj]
    in_specs.append(pl.BlockSpec((None, None, 4, QB, 2 * QB),
                                 lambda b, n: (g, jnp.minimum(n, 1), 0, 0, 0)))
    args.append(bias)
    any_spec = pl.BlockSpec(memory_space=pl.ANY)
    return pl.pallas_call(
        body, name=f"attn_fwd{g}",
        grid=(BL, nsb),
        in_specs=in_specs + [any_spec] * n_ride,
        out_specs=(pl.BlockSpec((rows, GW), lambda b, n: (row(b, n), 0)),
                   pl.BlockSpec((rows, 128), lambda b, n: (row(b, n), 0))) + (any_spec,) * n_ride,
        out_shape=(jax.ShapeDtypeStruct((T, GW), F32), jax.ShapeDtypeStruct((T, 128), F32)) + tuple(ride_shapes),
        scratch_shapes=scratch + (_Gather.scratch(ride_arrs) if n_ride else []),
        compiler_params=pltpu.CompilerParams(vmem_limit_bytes=VMEM_LIMIT),
    )(*args, *ride_arrs)


def _attn_bwd(proj, d_out, stats, bias, dproj, g):
    dil = DILATIONS[g]
    rows = QB * dil
    nsb = S // rows
    has_prev = nsb > 1
    n_steps = nsb + 1 if has_prev else 1
    n_in = 7 + (2 if has_prev else 0)

    def residue(r):
        return pl.ds(r, QB, stride=dil) if dil > 1 else pl.ds(0, QB)

    strided = dil > 1
    n_sets = _scratch_sets(rows)

    def body(*refs):
        q_ref, kc_ref, vc_ref, do_ref, st_ref, b_ref = refs[:6]
        kp_ref, vp_ref = refs[6:8] if has_prev else (None, None)
        out_ref, db_ref = refs[n_in], refs[n_in + 1]
        scr = list(refs[n_in + 2:])
        sq, sk, sv, sems = [scr.pop(0) for _ in range(4)]
        carry = scr.pop(0) if has_prev else None
        sts = scr.pop(0) if strided else st_ref
        copies = {name: [scr.pop(0) for _ in range(n_sets)] if strided else None
                  for name in ("q", "kc", "vc", "do", "dq", "dk", "dv") + (("kp", "vp") if has_prev else ())}
        b, n = pl.program_id(0), pl.program_id(1)

        @pl.when((b == 0) & (n == 0))
        def _():
            db_ref[...] = jnp.zeros_like(db_ref)

        def finish(h, r, dq, dk, dv):
            if strided:
                for name, val in (("dq", dq), ("dk", dk), ("dv", dv)):
                    copies[name][h % n_sets][residue(r), :] = val
            else:
                sl = slice(h * HD, (h + 1) * HD)
                sq[:, sl], sk[:, sl], sv[:, sl] = dq.astype(BF16), dk.astype(BF16), dv.astype(BF16)

        def finish_head(h):
            if strided:
                sl = slice(h * HD, (h + 1) * HD)
                sq[:, sl] = copies["dq"][h % n_sets][...].astype(BF16)
                sk[:, sl] = copies["dk"][h % n_sets][...].astype(BF16)
                sv[:, sl] = copies["dv"][h % n_sets][...].astype(BF16)

        def write_block(blk_idx):
            row0 = pl.multiple_of(blk_idx * rows, rows)
            _write_columns([(sq, CB * (CB_Q + g)), (sk, CB * (CB_K + g)), (sv, CB * (CB_V + g))],
                           out_ref, row0, sems)

        def carried(h, r):
            blk = slice(r * QB, (r + 1) * QB)
            return ((blk, slice(h * HD, (h + 1) * HD)), (blk, slice(GW + h * HD, GW + (h + 1) * HD)),
                    (blk, slice(2 * GW + h * HD, 2 * GW + (h + 1) * HD)))

        if has_prev:
            @pl.when(n == 0)
            def _():
                carry[...] = jnp.zeros_like(carry)

            @pl.when(n == nsb)
            def _():
                for h in range(4):
                    for r in range(dil):
                        cq, ck, cv = carried(h, r)
                        finish(h, r, carry[cq], carry[ck], carry[cv])
                    finish_head(h)
                write_block(b * nsb + nsb - 1)

        @pl.when(n < nsb)
        def _():
            if strided:
                for r in range(dil):
                    sts[r * QB:(r + 1) * QB, :] = st_ref[residue(r), :]
            refs_of = {"q": q_ref, "kc": kc_ref, "vc": vc_ref, "do": do_ref, "kp": kp_ref, "vp": vp_ref}
            for chunk in _unit_chunks(dil):
                heads = sorted({h for h, _ in chunk})
                rows_of = {h: {name: _residue_rows(refs_of[name], copies[name], h, residue)
                               for name in refs_of if refs_of[name] is not None}
                           for h in heads}

                def batch(name):
                    return jnp.stack([rows_of[h][name](r) for h, r in chunk])

                q, k, v, do = batch("q"), batch("kc"), batch("vc"), batch("do")
                if has_prev:
                    k = jnp.concatenate([batch("kp"), k], axis=1)
                    v = jnp.concatenate([batch("vp"), v], axis=1)
                    bias_b = jnp.stack([b_ref[h] for h, _ in chunk])
                else:
                    bias_b = jnp.stack([b_ref[h, :, QB:] for h, _ in chunk])
                lse = jnp.stack([sts[r * QB:(r + 1) * QB, h:h + 1] for h, r in chunk])
                delta = jnp.stack([sts[r * QB:(r + 1) * QB, 4 + h:5 + h] for h, r in chunk])
                s = jnp.einsum("uqd,ukd->uqk", q, k, preferred_element_type=F32) * SCALE + bias_b
                p = jnp.exp(s - lse)
                ds = p * (jnp.einsum("uqd,ukd->uqk", do, v, preferred_element_type=F32) - delta)
                for h in heads:
                    mine = [ds[i] for i, (hh, _) in enumerate(chunk) if hh == h]
                    tot = mine[0]
                    for extra in mine[1:]:
                        tot = tot + extra
                    if has_prev:
                        db_ref[h] += tot
                    else:
                        db_ref[h, :, QB:] += tot
                dsb, pb = ds.astype(BF16), p.astype(BF16)
                dq = jnp.einsum("uqk,ukd->uqd", dsb, k, preferred_element_type=F32) * SCALE
                dk = jnp.einsum("uqk,uqd->ukd", dsb, q, preferred_element_type=F32) * SCALE
                dv = jnp.einsum("uqk,uqd->ukd", pb, do, preferred_element_type=F32)
                for i, (h, r) in enumerate(chunk):
                    if has_prev:
                        cq, ck, cv = carried(h, r)
                        finish(h, r, carry[cq], carry[ck] + dk[i, :QB], carry[cv] + dv[i, :QB])
                        carry[cq] = dq[i]
                        carry[ck] = dk[i, QB:]
                        carry[cv] = dv[i, QB:]
                    else:
                        finish(h, r, dq[i], dk[i], dv[i])
                for h in heads:
                    finish_head(h)
            if has_prev:
                @pl.when(n > 0)
                def _():
                    write_block(b * nsb + n - 1)
            else:
                write_block(b)

    def row(b, n):
        return b * nsb + jnp.minimum(n, nsb - 1)

    def prev(b, n):
        return b * nsb + jnp.maximum(jnp.minimum(n, nsb - 1) - 1, 0)

    in_specs = [
        pl.BlockSpec((rows, GW), lambda b, n: (row(b, n), CB_Q + g)),
        pl.BlockSpec((rows, GW), lambda b, n: (row(b, n), CB_K + g)),
        pl.BlockSpec((rows, GW), lambda b, n: (row(b, n), CB_V + g)),
        pl.BlockSpec((rows, GW), lambda b, n: (row(b, n), 0)),
        pl.BlockSpec((rows, 128), lambda b, n: (row(b, n), 0)),
        pl.BlockSpec((None, None, 4, QB, 2 * QB),
                     lambda b, n: (g, jnp.minimum(jnp.minimum(n, nsb - 1), 1), 0, 0, 0)),
    ]
    args = [proj, proj, proj, d_out, stats, bias]
    scratch = [pltpu.VMEM((rows, GW), BF16)] * 3 + [pltpu.SemaphoreType.DMA((3,))]
    if has_prev:
        in_specs += [pl.BlockSpec((rows, GW), lambda b, n: (prev(b, n), CB_K + g)),
                     pl.BlockSpec((rows, GW), lambda b, n: (prev(b, n), CB_V + g))]
        args += [proj, proj]
        scratch.append(pltpu.VMEM((rows, 3 * GW), F32))
    if strided:
        n_copied = (7 + (2 if has_prev else 0)) * n_sets
        scratch += [pltpu.VMEM((rows, 128), F32)] * (1 + n_copied)
    in_specs.append(pl.BlockSpec(memory_space=pl.ANY))
    args.append(dproj)
    return pl.pallas_call(
        body, name=f"attn_bwd{g}",
        grid=(BL, n_steps),
        in_specs=in_specs,
        out_specs=(pl.BlockSpec(memory_space=pl.ANY),
                   pl.BlockSpec((4, QB, 2 * QB), lambda b, n: (0, 0, 0))),
        out_shape=(jax.ShapeDtypeStruct((T, NCOL), BF16),
                   jax.ShapeDtypeStruct((4, QB, 2 * QB), F32)),
        scratch_shapes=scratch,
        input_output_aliases={len(args) - 1: 0},
        compiler_params=pltpu.CompilerParams(vmem_limit_bytes=VMEM_LIMIT),
    )(*args)


def _tail(x2, tgt2, mod3, o_g, lse_g, proj, w_ao, w_co, w_o, conv_w, conv_b, ln_g, ln_b):
    tm = 256
    per_seq = S // tm
    halo = 16

    def body(x_ref, t_ref, mod_ref, o1_ref, o2_ref, o3_ref, l1_ref, l2_ref, l3_ref,
             ga_ref, u_ref, bg_ref, cg_ref, gc_ref, ma_ref, mc_ref, up_ref, cp_ref,
             wao_ref, wco_ref, wo_ref, cw_ref, cb_ref, lg_ref, lb_ref,
             dproj_ref, dyc_ref, do_ref, st_ref, dxd_ref,
             mg_ref, dy_ref, ain_ref, dao_ref, sin_ref, dso_ref, vec_ref,
             dga_s, dbg_s, dgm_s, sems):
        i = pl.program_id(0)
        bidx = i // per_seq
        first = (i % per_seq) == 0

        @pl.when(i == 0)
        def _():
            vec_ref[...] = jnp.zeros_like(vec_ref)

        l1, l2, l3 = l1_ref[...], l2_ref[...], l3_ref[...]
        mx = jnp.maximum(jnp.maximum(l1, l2), l3)
        e1, e2, e3 = jnp.exp(l1 - mx), jnp.exp(l2 - mx), jnp.exp(l3 - mx)
        esum = e1 + e2 + e3
        lse_tot = mx + jnp.log(esum)
        w1, w2, w3 = e1 / esum, e2 / esum, e3 / esum

        def per_head(wv):
            return jnp.concatenate([jnp.broadcast_to(wv[:, h:h + 1], (tm, HD)) for h in range(4)], axis=1)

        o = per_head(w1) * o1_ref[...] + per_head(w2) * o2_ref[...] + per_head(w3) * o3_ref[...]

        ga = ga_ref[...].astype(F32)
        sig_ga = _sigmoid(ga)
        silu_ga = ga * sig_ga
        a_in = (o * silu_ga).astype(BF16)
        a_out = _dot(a_in, wao_ref[...])

        u = u_ref[...].astype(F32)
        cg = cg_ref[...].astype(F32)
        z = cg * u
        zp = cp_ref[...].astype(F32) * up_ref[...].astype(F32)
        zp = jnp.where(first, 0.0, zp)
        zcat = jnp.concatenate([zp, z], axis=0)
        z1 = pltpu.roll(zcat, 1, 0)[halo:]
        z2 = pltpu.roll(zcat, 2, 0)[halo:]
        y_conv = cw_ref[0:1, :] * z2 + cw_ref[1:2, :] * z1 + cw_ref[2:3, :] * z + cb_ref[...]
        gc = gc_ref[...].astype(F32)
        sig_gc = _sigmoid(gc)
        silu_gc = gc * sig_gc
        bg = bg_ref[...].astype(F32)
        bg_yc = bg * y_conv
        s_in = (bg_yc * silu_gc).astype(BF16)
        s_out = _dot(s_in, wco_ref[...])

        sa = _sigmoid(ma_ref[...].astype(F32))
        sc = _sigmoid(mc_ref[...].astype(F32))
        merged = (sa * a_out + sc * s_out).astype(BF16)
        y = _dot(merged, wo_ref[...])
        gate1 = 1.0 + mod_ref[0, 2:3, :]
        xv = x_ref[...]
        resid = ALPHA * xv + gate1 * y
        mu = jnp.mean(resid, axis=1, keepdims=True)
        xc = resid - mu
        var = jnp.mean(xc * xc, axis=1, keepdims=True)
        rstd = lax.rsqrt(var + LN_EPS)
        xhat = xc * rstd
        lg = lg_ref[...]
        err = xhat * lg + lb_ref[...] - t_ref[...]
        vec_ref[3:4, :] += (0.5 / D) * jnp.sum(err * err, axis=0, keepdims=True)

        vec_ref[1:2, :] += (1.0 / D) * jnp.sum(err * xhat, axis=0, keepdims=True)
        vec_ref[2:3, :] += (1.0 / D) * jnp.sum(err, axis=0, keepdims=True)
        dxh = err * (lg * (1.0 / D))
        dres = rstd * (dxh - jnp.mean(dxh, axis=1, keepdims=True)
                       - xhat * jnp.mean(dxh * xhat, axis=1, keepdims=True))
        dxd_ref[...] = ALPHA * dres
        dgate = jnp.sum(dres * y, axis=0, keepdims=True)
        vec_ref[4:5, :] += jnp.where(bidx == 0, dgate, 0.0)
        vec_ref[5:6, :] += jnp.where(bidx == 1, dgate, 0.0)
        dy = (dres * gate1).astype(BF16)

        dmerged = _dot_nt(dy, wo_ref[...])
        da_out_f = dmerged * sa
        ds_out_f = dmerged * sc
        da_out = da_out_f.astype(BF16)
        ds_out = ds_out_f.astype(BF16)
        dgm_s[:, 2 * D:3 * D] = (ds_out_f * s_out * (1.0 - sc)).astype(BF16)
        dgm_s[:, D:2 * D] = (da_out_f * a_out * (1.0 - sa)).astype(BF16)
        da_in = _dot_nt(da_out, wao_ref[...])
        ds_in = _dot_nt(ds_out, wco_ref[...])

        d_o = da_in * silu_ga
        do_ref[...] = d_o.astype(BF16)
        dga_s[...] = (da_in * o * (sig_ga + silu_ga * (1.0 - sig_ga))).astype(BF16)
        lane = lax.broadcasted_iota(jnp.int32, (tm, 128), 1)
        stats = lse_tot
        od = o * d_o
        for h in range(4):
            delta = jnp.sum(od[:, h * HD:(h + 1) * HD], axis=1, keepdims=True)
            stats = jnp.where(lane == 4 + h, delta, stats)
        st_ref[...] = stats

        ds_silu = ds_in * silu_gc
        dbg_s[...] = (ds_silu * y_conv).astype(BF16)
        dyc = ds_silu * bg
        dyc_ref[...] = dyc
        vec_ref[0:1, :] += jnp.sum(dyc, axis=0, keepdims=True)
        dgm_s[:, 0:D] = (ds_in * bg_yc * (sig_gc + silu_gc * (1.0 - sig_gc))).astype(BF16)

        mg_ref[...] = merged
        dy_ref[...] = dy
        ain_ref[...] = a_in
        dao_ref[...] = da_out
        sin_ref[...] = s_in
        dso_ref[...] = ds_out
        _write_columns([(dga_s, CB * CB_GA), (dbg_s, D * KB_BG), (dgm_s, D * KB_GC)],
                       dproj_ref, pl.multiple_of(i * tm, tm), sems)

    def tile(width, cblk=0):
        return pl.BlockSpec((tm, width), lambda i: (i, cblk))

    def whole(shape):
        return pl.BlockSpec(shape, lambda i: tuple(0 for _ in shape))

    prev_rows = lambda i: (jnp.maximum(i * (tm // halo) - 1, 0),)
    in_specs = [
        tile(D), tile(D), pl.BlockSpec((1, 3, D), lambda i: (i // per_seq, 0, 0)),
        tile(GW), tile(GW), tile(GW), tile(128), tile(128), tile(128),
        tile(GW, CB_GA), tile(D, KB_U), tile(D, KB_BG), tile(D, KB_CG), tile(D, KB_GC),
        tile(D, KB_MA), tile(D, KB_MC),
        pl.BlockSpec((halo, D), lambda i: (*prev_rows(i), KB_U)),
        pl.BlockSpec((halo, D), lambda i: (*prev_rows(i), KB_CG)),
        whole((GW, D)), whole((D, D)), whole((D, D)),
        whole((3, D)), whole((1, D)), whole((1, D)), whole((1, D)),
    ]
    out_specs = (
        pl.BlockSpec(memory_space=pl.ANY), tile(D), tile(GW), tile(128), tile(D),
        tile(D), tile(D), tile(GW), tile(D), tile(D), tile(D),
        pl.BlockSpec((8, D), lambda i: (0, 0)),
    )
    out_shape = (
        jax.ShapeDtypeStruct((T, NCOL), BF16),
        jax.ShapeDtypeStruct((T, D), F32),
        jax.ShapeDtypeStruct((T, GW), BF16),
        jax.ShapeDtypeStruct((T, 128), F32),
        jax.ShapeDtypeStruct((T, D), F32),
        jax.ShapeDtypeStruct((T, D), BF16),
        jax.ShapeDtypeStruct((T, D), BF16),
        jax.ShapeDtypeStruct((T, GW), BF16),
        jax.ShapeDtypeStruct((T, D), BF16),
        jax.ShapeDtypeStruct((T, D), BF16),
        jax.ShapeDtypeStruct((T, D), BF16),
        jax.ShapeDtypeStruct((8, D), F32),
    )
    return pl.pallas_call(
        body, name="tail",
        grid=(T // tm,),
        in_specs=in_specs, out_specs=out_specs, out_shape=out_shape,
        scratch_shapes=[pltpu.VMEM((tm, GW), BF16), pltpu.VMEM((tm, D), BF16), pltpu.VMEM((tm, 3 * D), BF16),
                        pltpu.SemaphoreType.DMA((3,))],
        compiler_params=pltpu.CompilerParams(vmem_limit_bytes=VMEM_LIMIT),
    )(x2, tgt2, mod3, *o_g, *lse_g, proj, proj, proj, proj, proj, proj, proj, proj, proj,
      w_ao, w_co, w_o, conv_w, conv_b, ln_g, ln_b)


def _conv_bwd(dyc, proj, conv_w, dproj):
    tm = 512
    per_seq = S // tm
    halo = 16

    def body(d_ref, dn_ref, u_ref, c_ref, up_ref, cp_ref, cw_ref, _, dproj_ref, g_ref, du_s, dc_s, sems):
        i = pl.program_id(0)
        first = (i % per_seq) == 0
        last = (i % per_seq) == per_seq - 1

        @pl.when(i == 0)
        def _():
            g_ref[...] = jnp.zeros_like(g_ref)

        d = d_ref[...]
        dn = jnp.where(last, 0.0, dn_ref[...])
        dcat = jnp.concatenate([d, dn], axis=0)
        d1 = pltpu.roll(dcat, tm + 8 - 1, 0)[:tm]
        d2 = pltpu.roll(dcat, tm + 8 - 2, 0)[:tm]
        dz = cw_ref[2:3, :] * d + cw_ref[1:2, :] * d1 + cw_ref[0:1, :] * d2
        u = u_ref[...].astype(F32)
        cg = c_ref[...].astype(F32)
        du_s[...] = (dz * cg).astype(BF16)
        dc_s[...] = (dz * u).astype(BF16)
        _write_columns([(du_s, D * KB_U), (dc_s, D * KB_CG)], dproj_ref, pl.multiple_of(i * tm, tm), sems)

        z = cg * u
        zp = jnp.where(first, 0.0, cp_ref[...].astype(F32) * up_ref[...].astype(F32))
        zcat = jnp.concatenate([zp, z], axis=0)
        z1 = pltpu.roll(zcat, 1, 0)[halo:]
        z2 = pltpu.roll(zcat, 2, 0)[halo:]
        g_ref[0:1, :] += jnp.sum(d * z2, axis=0, keepdims=True)
        g_ref[1:2, :] += jnp.sum(d * z1, axis=0, keepdims=True)
        g_ref[2:3, :] += jnp.sum(d * z, axis=0, keepdims=True)

    n_tiles = T // tm
    prev_rows = lambda i: jnp.maximum(i * (tm // halo) - 1, 0)
    next_rows = lambda i: jnp.minimum((i + 1) * (tm // 8), T // 8 - 1)
    return pl.pallas_call(
        body, name="conv_bwd",
        grid=(n_tiles,),
        in_specs=[pl.BlockSpec((tm, D), lambda i: (i, 0)),
                  pl.BlockSpec((8, D), lambda i: (next_rows(i), 0)),
                  pl.BlockSpec((tm, D), lambda i: (i, KB_U)),
                  pl.BlockSpec((tm, D), lambda i: (i, KB_CG)),
                  pl.BlockSpec((halo, D), lambda i: (prev_rows(i), KB_U)),
                  pl.BlockSpec((halo, D), lambda i: (prev_rows(i), KB_CG)),
                  pl.BlockSpec((3, D), lambda i: (0, 0)),
                  pl.BlockSpec(memory_space=pl.ANY)],
        out_specs=(pl.BlockSpec(memory_space=pl.ANY),
                   pl.BlockSpec((8, D), lambda i: (0, 0))),
        out_shape=(jax.ShapeDtypeStruct((T, NCOL), BF16),
                   jax.ShapeDtypeStruct((8, D), F32)),
        scratch_shapes=[pltpu.VMEM((tm, D), BF16), pltpu.VMEM((tm, D), BF16), pltpu.SemaphoreType.DMA((2,))],
        input_output_aliases={7: 0},
        compiler_params=pltpu.CompilerParams(vmem_limit_bytes=VMEM_LIMIT),
    )(dyc, dyc, proj, proj, proj, proj, conv_w, dproj)


def _dh_dx(dproj, w_in_all, x2, dxd, mod3, chip_sums, hops=(), parts0=None):
    tm = 1024
    per_seq = S // tm
    n = len(chip_sums)
    n_in = 5 + n + (0 if parts0 is None else 1)

    def body(*refs):
        d_ref, w_ref, x_ref, dxd_ref, mod_ref = refs[:5]
        ins = refs[5:5 + n]
        gx_ref, vec_ref = refs[n_in:n_in + 2]
        outs = refs[n_in + 2:n_in + 2 + n]
        acc, send_sems, recv_sems, local_sems = refs[n_in + 2 + n:]
        i, jj = pl.program_id(0), pl.program_id(1)

        @pl.when((i == 0) & (jj == 0))
        def _():
            vec_ref[...] = jnp.zeros_like(vec_ref)
            if n:
                sends, _, mine = _chip_copies(ins, outs, send_sems, recv_sems, local_sems, hops)
                for cp in sends + mine:
                    cp.start()

        if n:
            @pl.when((i == T // tm - 1) & (jj == N_DEV - 1))
            def _():
                sends, arrivals, mine = _chip_copies(ins, outs, send_sems, recv_sems, local_sems, hops)
                for cp in arrivals:
                    cp.wait_recv()
                for cp in sends:
                    cp.wait_send()
                for cp in mine:
                    cp.wait()

        @pl.when(jj == 0)
        def _():
            acc[...] = jnp.zeros_like(acc)

        acc[...] += _dot_nt(d_ref[...], w_ref[...])

        @pl.when(jj == N_DEV - 1)
        def _():
            dh = acc[...]
            bidx = i // per_seq
            gx_ref[...] = dxd_ref[...] + dh * (1.0 + mod_ref[0, 1:2, :])
            dshift = jnp.sum(dh, axis=0, keepdims=True)
            dscale = jnp.sum(dh * x_ref[...], axis=0, keepdims=True)
            vec_ref[0:1, :] += jnp.where(bidx == 0, dshift, 0.0)
            vec_ref[1:2, :] += jnp.where(bidx == 1, dshift, 0.0)
            vec_ref[2:3, :] += jnp.where(bidx == 0, dscale, 0.0)
            vec_ref[3:4, :] += jnp.where(bidx == 1, dscale, 0.0)

    any_spec = pl.BlockSpec(memory_space=pl.ANY)
    res = pl.pallas_call(
        body, name="dh_dx",
        grid=(T // tm, N_DEV),
        in_specs=[
            pl.BlockSpec((tm, SHARD), lambda i, jj: (i, jj)),
            pl.BlockSpec((None, D, SHARD), lambda i, jj: (jj, 0, 0)),
            pl.BlockSpec((tm, D), lambda i, jj: (i, 0)),
            pl.BlockSpec((tm, D), lambda i, jj: (i, 0)),
            pl.BlockSpec((1, 3, D), lambda i, jj: (i // per_seq, 0, 0))] + [any_spec] * (n_in - 5),
        out_specs=(pl.BlockSpec((tm, D), lambda i, jj: (i, 0)),
                   pl.BlockSpec((8, D), lambda i, jj: (0, 0))) + (any_spec,) * n,
        out_shape=(jax.ShapeDtypeStruct((T, D), F32), jax.ShapeDtypeStruct((8, D), F32))
                  + tuple(jax.ShapeDtypeStruct(a.shape, a.dtype) for a in chip_sums),
        scratch_shapes=[pltpu.VMEM((tm, D), F32), pltpu.SemaphoreType.DMA((max(3 * n, 1),)),
                        pltpu.SemaphoreType.DMA((max(3 * n, 1),)), pltpu.SemaphoreType.DMA((max(n, 1),))],
        input_output_aliases={} if parts0 is None else {5 + n: 2},
        compiler_params=pltpu.CompilerParams(vmem_limit_bytes=VMEM_LIMIT),
    )(dproj, w_in_all, x2, dxd, mod3, *chip_sums, *([] if parts0 is None else [parts0]))
    return res[0], res[1], res[2:]


def _mm_tn(a, b, tn, blocks_leading, name):
    kk, m = a.shape
    n = b.shape[1]
    tk = 2048

    def body(a_ref, b_ref, o_ref, acc):
        @pl.when(pl.program_id(1) == 0)
        def _():
            acc[...] = jnp.zeros_like(acc)

        acc[...] += _dot_tn(a_ref[...], b_ref[...])

        @pl.when(pl.program_id(1) == kk // tk - 1)
        def _():
            o_ref[...] = acc[...].astype(BF16)

    if blocks_leading:
        out_spec = pl.BlockSpec((None, m, tn), lambda j, k: (j, 0, 0))
        out_shape = jax.ShapeDtypeStruct((n // tn, m, tn), BF16)
    else:
        out_spec = pl.BlockSpec((m, tn), lambda j, k: (0, j))
        out_shape = jax.ShapeDtypeStruct((m, n), BF16)
    return pl.pallas_call(
        body, name=name,
        grid=(n // tn, kk // tk),
        in_specs=[pl.BlockSpec((tk, m), lambda j, k: (k, 0)),
                  pl.BlockSpec((tk, tn), lambda j, k: (k, j))],
        out_specs=out_spec, out_shape=out_shape,
        scratch_shapes=[pltpu.VMEM((m, tn), F32)],
        compiler_params=pltpu.CompilerParams(vmem_limit_bytes=VMEM_LIMIT),
    )(a, b)


def _adam_step(g, w, m, v):
    nm = ADAM_B1 * m + (1.0 - ADAM_B1) * g
    nv = ADAM_B2 * v + (1.0 - ADAM_B2) * (g * g)
    m_hat = nm / (1.0 - ADAM_B1 ** ADAM_STEP)
    v_hat = nv / (1.0 - ADAM_B2 ** ADAM_STEP)
    return -ADAM_LR * (m_hat / (jnp.sqrt(v_hat) + ADAM_EPS) + ADAM_WD * w), nm, nv


def _adamw(parts, w, m, v, name, row_tile=None):
    n_parts, rows, cols = parts.shape
    tr = rows if row_tile is None else row_tile

    def body(p_ref, w_ref, m_ref, v_ref, g_ref, d_ref, nm_ref, nv_ref):
        g = p_ref[0].astype(F32)
        for s in range(1, n_parts):
            g = g + p_ref[s].astype(F32)
        g_ref[...] = g
        d_ref[...], nm_ref[...], nv_ref[...] = _adam_step(g, w_ref[...], m_ref[...], v_ref[...])

    blk = pl.BlockSpec((tr, cols), lambda i: (i, 0))
    shp = jax.ShapeDtypeStruct((rows, cols), F32)
    return pl.pallas_call(
        body, name=name,
        grid=(rows // tr,),
        in_specs=[pl.BlockSpec((n_parts, tr, cols), lambda i: (0, i, 0)), blk, blk, blk],
        out_specs=(blk, blk, blk, blk),
        out_shape=(shp, shp, shp, shp),
        compiler_params=pltpu.CompilerParams(vmem_limit_bytes=VMEM_LIMIT),
    )(parts, w, m, v)


def _multi_adamw(parts_list, params, name):
    n = len(params)
    flat = [t for wmv in params for t in wmv]

    def body(*refs):
        parts, ins, outs = refs[:n], refs[n:4 * n], refs[4 * n:]
        for p in range(n):
            g = parts[p][0].astype(F32)
            for s in range(1, parts[p].shape[0]):
                g = g + parts[p][s].astype(F32)
            w_ref, m_ref, v_ref = ins[3 * p:3 * p + 3]
            g_ref, d_ref, nm_ref, nv_ref = outs[4 * p:4 * p + 4]
            g_ref[...] = g
            d_ref[...], nm_ref[...], nv_ref[...] = _adam_step(g, w_ref[...], m_ref[...], v_ref[...])

    out_shape = []
    for w, _, _ in params:
        out_shape += [jax.ShapeDtypeStruct(w.shape, F32)] * 4
    res = pl.pallas_call(body, name=name, out_shape=tuple(out_shape))(*parts_list, *flat)
    return [res[4 * p:4 * p + 4] for p in range(n)]


def _small_updates(small_g, dmod_all, rel_parts, params):
    flat = [t for wmv in params for t in wmv]

    def body(sg_ref, dm_ref, rp_ref, *refs):
        ins, outs = refs[:len(flat)], refs[len(flat):]

        def over_devices(row):
            tot = sg_ref[0, row:row + 1, :]
            for s in range(1, N_DEV):
                tot = tot + sg_ref[s, row:row + 1, :]
            return tot

        g_b_ada = dm_ref[0:1, :]
        for r in range(1, N_DEV * BL):
            g_b_ada = g_b_ada + dm_ref[r:r + 1, :]
        g_rel = rp_ref[0]
        for s in range(1, N_DEV):
            g_rel = g_rel + rp_ref[s]
        grads = [g_b_ada, over_devices(0), g_rel, over_devices(1), over_devices(2)]
        outs[0][...] = jnp.sum(over_devices(3), axis=1, keepdims=True)
        for p, g in enumerate(grads):
            w_ref, m_ref, v_ref = ins[3 * p:3 * p + 3]
            g_ref, d_ref, nm_ref, nv_ref = outs[1 + 4 * p:5 + 4 * p]
            g_ref[...] = g
            d_ref[...], nm_ref[...], nv_ref[...] = _adam_step(g, w_ref[...], m_ref[...], v_ref[...])

    out_shape = [jax.ShapeDtypeStruct((1, 1), F32)]
    for w, _, _ in params:
        out_shape += [jax.ShapeDtypeStruct(w.shape, F32)] * 4
    res = pl.pallas_call(body, name="small_updates", out_shape=tuple(out_shape))(small_g, dmod_all, rel_parts, *flat)
    return res[0], [res[1 + 4 * p:5 + 4 * p] for p in range(len(params))]


def _attention_forward(proj, rel_bias, rides=((), (), ())):
    buckets_np, masks_np = _bucket_maps()
    buckets, masks = jnp.asarray(buckets_np), jnp.asarray(masks_np)
    bias = _bias_expand(rel_bias, buckets, masks)
    fwd = [_attn_fwd(proj, bias, g, rides[g]) for g in range(3)]
    brought = [arr for f in fwd for arr in f[2:]]
    return bias, buckets, [f[0] for f in fwd], [f[1] for f in fwd], brought


def _local_step(x2, tgt2, mod3, h, proj, attn, w_ao, w_co, w_o, conv_w, conv_b, ln_g, ln_b):
    bias, buckets, o_g, lse_g = attn

    (dproj, dyc, d_o, stats, dxd, merged, dy, a_in, da_out, s_in, ds_out, tail_vec) = _tail(
        x2, tgt2, mod3, o_g, lse_g, proj, w_ao, w_co, w_o, conv_w, conv_b, ln_g, ln_b)

    dbias = []
    for g in range(3):
        dproj, db = _attn_bwd(proj, d_o, stats, bias, dproj, g)
        dbias.append(db)
    g_rel_bias = _bias_grad(*dbias, buckets)
    dproj, conv_vec = _conv_bwd(dyc, proj, conv_w, dproj)

    gw_o = _mm_tn(merged, dy, D, False, "gw_o")
    gw_co = _mm_tn(s_in, ds_out, D, False, "gw_conv_out")
    gw_ao = _mm_tn(a_in, da_out, D, False, "gw_attn_out")
    gw_ao = jnp.transpose(gw_ao.reshape(GW, N_DEV, D // N_DEV), (1, 0, 2))
    return dproj, dxd, gw_ao, gw_co, gw_o, conv_vec, g_rel_bias, tail_vec


def kernel(x, c, w_ada, b_ada, w_in, conv_w, conv_b, rel_bias, w_attn_out, w_conv_out, w_o, ln_g, ln_b, loss_target, m_w_ada, m_b_ada, m_w_in, m_conv_w, m_conv_b, m_rel_bias, m_w_attn_out, m_w_conv_out, m_w_o, m_ln_g, m_ln_b, v_w_ada, v_b_ada, v_w_in, v_conv_w, v_conv_b, v_rel_bias, v_w_attn_out, v_w_conv_out, v_w_o, v_ln_g, v_ln_b):
    me = _my_index()
    x2 = x.reshape(T, D)
    tgt2 = loss_target.reshape(T, D)

    b_cols = lax.dynamic_slice(b_ada, (0, me * ADA_SHARD), (1, ADA_SHARD))
    c_g, mod_in = _mod_exchange(jnp.pad(c, ((0, 8 - BL), (0, 0))), w_ada[0], b_cols)
    c_all = c_g[:, 0:BL, :].reshape(N_DEV * BL, D)
    mod3 = jnp.transpose(mod_in[:, 0:BL, :], (1, 0, 2)).reshape(BL, 3, D)

    h = _prep_h(x2, mod3)
    rows_shape = jax.ShapeDtypeStruct((N_DEV, D // N_DEV, D), BF16)
    proj, w_in_all, (w_ao_g, w_co_g, w_o_g, conv_w_g) = _gather_proj(
        _shard_order(), h, w_in[0].astype(BF16), 1024,
        ([w_attn_out[0].astype(BF16), w_conv_out[0].astype(BF16), w_o[0].astype(BF16), conv_w[0]],
         [jax.ShapeDtypeStruct((N_DEV, GW, D // N_DEV), BF16), rows_shape, rows_shape,
          jax.ShapeDtypeStruct((N_DEV, 3, D // N_DEV), F32)]))

    *attn, _ = _attention_forward(proj, rel_bias)
    w_ao_full = jnp.transpose(w_ao_g, (1, 0, 2)).reshape(GW, D)
    w_co_full = w_co_g.reshape(D, D)
    w_o_full = w_o_g.reshape(D, D)
    conv_w_full = jnp.transpose(conv_w_g, (1, 0, 2)).reshape(3, D)

    (dproj, dxd, gw_ao, gw_co, gw_o, conv_vec, g_rel_bias, tail_vec) = _local_step(
        x2, tgt2, mod3, h, proj, attn, w_ao_full, w_co_full, w_o_full,
        conv_w_full, conv_b, ln_g, ln_b)

    g_conv_w_blocks = jnp.transpose(conv_vec[0:3].reshape(3, N_DEV, D // N_DEV), (1, 0, 2))
    partials = [gw_ao, gw_co.reshape(N_DEV, D // N_DEV, D), gw_o.reshape(N_DEV, D // N_DEV, D), g_conv_w_blocks]
    w_in_sums, w_in_parts, sib = _gw_in_pair(
        _slice_order(), h, dproj, partials,
        [jax.ShapeDtypeStruct((4, GW, D // N_DEV), BF16),
         jax.ShapeDtypeStruct((4, D // N_DEV, D), BF16),
         jax.ShapeDtypeStruct((4, D // N_DEV, D), BF16),
         jax.ShapeDtypeStruct((4, 3, D // N_DEV), F32)])
    names = ["w_attn_out", "w_conv_out", "w_o", "conv_w"]
    core = lax.axis_index("c").astype(jnp.int32).reshape(1)
    chip_sums = [w_in_sums] + [_pair_add(core, partials[a], sib[a], None, "pair_add_" + names[a])
                               for a in range(4)]
    hops = [(3,)] + [(1, 2, 3)] * 4
    grad_x, mod_vec, (r_in, r_ao, r_co, r_o, r_cw) = _dh_dx(
        dproj, w_in_all, x2, dxd, mod3, chip_sums, hops, w_in_parts)

    small = jnp.concatenate([
        tail_vec[0:4],
        jnp.pad(g_rel_bias.reshape(1, N_BUCKETS * N_HEADS), ((0, 0), (0, D - N_BUCKETS * N_HEADS))),
        jnp.zeros((3, D), F32)], axis=0)
    dmod = jnp.concatenate([mod_vec[0:2], mod_vec[2:4], tail_vec[4:6]], axis=1)
    small_g, dmod_g = _all_gather(
        [small, dmod],
        [jax.ShapeDtypeStruct((N_DEV, 8, D), F32), jax.ShapeDtypeStruct((N_DEV, BL, 3 * D), F32)],
        "gather_small")
    dmod_all = dmod_g.reshape(N_DEV * BL, 3 * D)
    small_names = ["b_ada", "conv_b", "rel_bias", "ln_g", "ln_b"]
    small_params = [(b_ada, m_b_ada, v_b_ada), (conv_b, m_conv_b, v_conv_b), (rel_bias, m_rel_bias, v_rel_bias),
                    (ln_g, m_ln_g, v_ln_g), (ln_b, m_ln_b, v_ln_b)]
    loss, small_res = _small_updates(
        small_g, dmod_all, small_g[:, 4, :N_BUCKETS * N_HEADS].reshape(N_DEV, N_BUCKETS, N_HEADS), small_params)
    loss = loss.reshape(())
    g_w_ada = _ada_bwd(jnp.transpose(c_all), lax.dynamic_slice(dmod_all, (0, me * ADA_SHARD),
                                                               (N_DEV * BL, ADA_SHARD)))

    def upd(parts, w, m, v, name, row_tile=None):
        shape = w.shape
        w2, m2, v2 = (t.reshape(parts.shape[1:]) for t in (w, m, v))
        return tuple(t.reshape(shape) for t in _adamw(parts, w2, m2, v2, name, row_tile))

    res = {
        "w_ada": upd(g_w_ada[None], w_ada, m_w_ada, v_w_ada, "adam_w_ada", 256),
        "w_in": upd(r_in, w_in, m_w_in, v_w_in, "adam_w_in", 128),
    }
    mid_names = ["conv_w", "w_attn_out", "w_conv_out", "w_o"]
    mid_parts = [r_cw, r_ao, r_co, r_o]
    mid_full = [(conv_w, m_conv_w, v_conv_w), (w_attn_out, m_w_attn_out, v_w_attn_out),
                (w_conv_out, m_w_conv_out, v_w_conv_out), (w_o, m_w_o, v_w_o)]
    mid_res = _multi_adamw(mid_parts, [tuple(t[0] for t in wmv) for wmv in mid_full], "adam_mid")
    for nm, wmv, outs4 in zip(mid_names, mid_full, mid_res):
        res[nm] = tuple(t[None] for t in outs4)
    res.update(dict(zip(small_names, small_res)))
    order = ["w_ada", "b_ada", "w_in", "conv_w", "conv_b", "rel_bias", "w_attn_out", "w_conv_out",
             "w_o", "ln_g", "ln_b"]
    outs = [loss, grad_x.reshape(BL, S, D)]
    for k in range(4):
        outs += [res[name][k] for name in order]
    return tuple(outs)
```

```python
import math

import numpy as np
import jax
import jax.numpy as jnp
from jax import lax
from jax.experimental import pallas as pl
from jax.experimental.pallas import tpu as pltpu

F32 = jnp.float32
BF16 = jnp.bfloat16
MESH = pl.DeviceIdType.MESH

N_DEV = 8
D = 1024
S = 2048
BL = 2
T = BL * S
NCOL = 11264
SHARD = NCOL // N_DEV
CB = 512
NCB = NCOL // CB
HD = 128
GW = 512
QB = 128
DILATIONS = (1, 4, 16)
N_STEPS = 128
N_BUCKETS = 32
N_HEADS = 12
ALPHA = 2.0 ** 0.25
LN_EPS = 1e-5
NEG_INF = -1e30
SCALE = HD ** -0.5
ADA_SHARD = 3 * D // N_DEV

CB_Q, CB_K, CB_V, CB_GA = 0, 3, 6, 9
KB_U, KB_BG, KB_CG, KB_GC, KB_MA, KB_MC = 5, 6, 7, 8, 9, 10

ADAM_LR, ADAM_B1, ADAM_B2, ADAM_EPS, ADAM_WD, ADAM_STEP = 0.001, 0.9, 0.999, 1e-08, 0.01, 10

VMEM_LIMIT = 56 * 1024 * 1024


def _dot(a, b):
    return jnp.dot(a, b, preferred_element_type=F32)


def _dot_nt(a, b):
    return lax.dot_general(a, b, (((1,), (1,)), ((), ())), preferred_element_type=F32)


def _dot_tn(a, b):
    return lax.dot_general(a, b, (((0,), (0,)), ((), ())), preferred_element_type=F32)


def _sigmoid(v):
    return 1.0 / (1.0 + jnp.exp(-v))


def _write_columns(pieces, dst_hbm, row0, sems):
    copies = []
    for k, (src, col0) in enumerate(pieces):
        rows, width = src.shape
        copies.append(pltpu.make_async_copy(
            src, dst_hbm.at[pl.ds(row0, rows), pl.ds(col0, width)], sems.at[k]))
    for cp in copies:
        cp.start()
    for cp in copies:
        cp.wait()


def _my_index():
    return 4 * lax.axis_index("x") + 2 * lax.axis_index("y") + lax.axis_index("c")


class _Gather:
    def __init__(self, ins, outs, stage, send_sems, recv_sems, local_sems):
        self.ins, self.outs, self.stage = ins, outs, stage
        self.send_sems, self.recv_sems, self.local_sems = send_sems, recv_sems, local_sems
        x, y, c = lax.axis_index("x"), lax.axis_index("y"), lax.axis_index("c")
        self.c = c
        self.me, self.sibling = (x, y, c), (x, y, 1 - c)
        self.chips = [(1 - x, y), (x, 1 - y), (1 - x, 1 - y)]

    @staticmethod
    def scratch(arrs):
        n = len(arrs)
        return ([pltpu.SemaphoreType.DMA((7 * n,)), pltpu.SemaphoreType.DMA((7 * n,)),
                 pltpu.SemaphoreType.DMA((n,))] + [pltpu.VMEM(a.shape, a.dtype) for a in arrs])

    def _copy(self, a, k, block, to, src=None):
        dst = self.outs[a].at[4 * block[0] + 2 * block[1] + block[2]]
        return pltpu.make_async_remote_copy(
            src_ref=dst if src is None else src, dst_ref=dst,
            send_sem=self.send_sems.at[a * 7 + k], recv_sem=self.recv_sems.at[a * 7 + k],
            device_id=to, device_id_type=MESH)

    def _first(self):
        first = []
        for a in range(len(self.ins)):
            first.append(self._copy(a, 0, self.me, self.sibling, src=self.ins[a]))
            first += [self._copy(a, 1 + j, self.me, (*chip, self.c), src=self.ins[a])
                      for j, chip in enumerate(self.chips)]
        return first

    def _mine(self):
        me = self.me
        return [pltpu.make_async_copy(self.stage[a], self.outs[a].at[4 * me[0] + 2 * me[1] + me[2]],
                                      self.local_sems.at[a]) for a in range(len(self.ins))]

    def begin(self):
        for cp in self._first():
            cp.start()
        loads = [pltpu.make_async_copy(self.ins[a], self.stage[a], self.local_sems.at[a])
                 for a in range(len(self.ins))]
        for cp in loads:
            cp.start()
        for cp in loads:
            cp.wait()
        for cp in self._mine():
            cp.start()

    def finish(self):
        n, c, me, sibling = len(self.ins), self.c, self.me, self.sibling
        passed = []
        for j, chip in enumerate(self.chips):
            for a in range(n):
                self._copy(a, 1 + j, (*chip, c), me).wait_recv()
                fwd = self._copy(a, 4 + j, (*chip, c), sibling)
                fwd.start()
                passed.append(fwd)
        for a in range(n):
            self._copy(a, 0, sibling, me).wait_recv()
        for j, chip in enumerate(self.chips):
            for a in range(n):
                self._copy(a, 4 + j, (*chip, 1 - c), me).wait_recv()
        for cp in self._first() + passed:
            cp.wait_send()
        for cp in self._mine():
            cp.wait()


def _all_gather(arrs, out_shapes, name):
    n = len(arrs)

    def body(*refs):
        g = _Gather(refs[:n], refs[n:2 * n], refs[2 * n + 3:], *refs[2 * n:2 * n + 3])
        g.begin()
        g.finish()

    any_spec = pl.BlockSpec(memory_space=pl.ANY)
    return pl.pallas_call(
        body, name=name,
        out_shape=tuple(out_shapes),
        in_specs=[any_spec] * n,
        out_specs=tuple([any_spec] * n),
        scratch_shapes=_Gather.scratch(arrs),
    )(*arrs)


def _slice_order():
    x, y, c = lax.axis_index("x"), lax.axis_index("y"), lax.axis_index("c")
    slots = []
    for q in (2 * (1 - x) + y, 2 * x + (1 - y), 2 * (1 - x) + (1 - y), 2 * x + y):
        slots += [2 * q + 1 - c, 2 * q + c]
    return jnp.stack(slots).astype(jnp.int32)


def _gw_in_pair(order, h, dproj, smalls, small_shapes4):
    kk, m = h.shape
    tk = min(kk, 2048)
    nk = kk // tk
    ncols = dproj.shape[1] // N_DEV
    n = len(smalls)

    def body(order_ref, h_ref, d_ref, *rest):
        ins = rest[:n]
        sums_hbm, parts_hbm = rest[n], rest[n + 1]
        sib = rest[n + 2:2 * n + 2]
        (acc, sendbuf, recvbuf, sumbuf, send_sems, recv_sems, local_sem, ssend, srecv,
         isend, irecv) = rest[2 * n + 2:]
        js, k = pl.program_id(0), pl.program_id(1)
        x, y, c = lax.axis_index("x"), lax.axis_index("y"), lax.axis_index("c")
        sibling = (x, y, 1 - c)
        my_chip = 2 * x + y
        near = [(1 - x, y, c), (x, 1 - y, c)]

        def ici_copy(p, out_chip):
            peer = near[p]
            return pltpu.make_async_remote_copy(
                src_ref=sumbuf.at[p], dst_ref=parts_hbm.at[out_chip],
                send_sem=isend.at[p], recv_sem=irecv.at[p], device_id=peer, device_id_type=MESH)

        def small_copies():
            return [pltpu.make_async_remote_copy(
                        src_ref=ins[a].at[2 * q + 1 - c], dst_ref=sib[a].at[q],
                        send_sem=ssend.at[a * 4 + q], recv_sem=srecv.at[a * 4 + q],
                        device_id=sibling, device_id_type=MESH)
                    for a in range(n) for q in range(4)]

        def slice_copy(p):
            return pltpu.make_async_remote_copy(
                src_ref=sendbuf, dst_ref=recvbuf.at[p], send_sem=send_sems.at[p], recv_sem=recv_sems.at[p],
                device_id=sibling, device_id_type=MESH)

        def sum_copy(p):
            return pltpu.make_async_copy(sumbuf.at[2], sums_hbm.at[order_ref[2 * p] // 2], local_sem)

        @pl.when((js == 0) & (k == 0))
        def _():
            for cp in small_copies():
                cp.start()

        @pl.when(k == 0)
        def _():
            acc[...] = jnp.zeros_like(acc)

        acc[...] += _dot_tn(h_ref[...], d_ref[...])

        for p in range(4):
            @pl.when((js == 2 * p) & (k == nk - 1))
            def _():
                if p > 0:
                    slice_copy(p - 1).wait_send()
                sendbuf[...] = acc[...].astype(BF16)
                slice_copy(p).start()

            @pl.when((js == 2 * p + 1) & (k == nk - 1))
            def _():
                slice_copy(p).wait_recv()
                if p == 3:
                    sum_copy(2).wait()
                sumbuf[min(p, 2)] = (acc[...] + recvbuf[p].astype(F32)).astype(BF16)
                if p < 2:
                    ici_copy(p, my_chip).start()
                else:
                    sum_copy(p).start()

        @pl.when((js == N_DEV - 1) & (k == nk - 1))
        def _():
            slice_copy(3).wait_send()
            sum_copy(3).wait()
            for cp in small_copies():
                cp.wait()
            for p in range(2):
                ici_copy(p, 2 * near[p][0] + near[p][1]).wait_recv()
                ici_copy(p, my_chip).wait_send()

    any_spec = pl.BlockSpec(memory_space=pl.ANY)
    res = pl.pallas_call(
        body, name="gw_in_pair",
        grid_spec=pltpu.PrefetchScalarGridSpec(
            num_scalar_prefetch=1,
            grid=(N_DEV, nk),
            in_specs=[pl.BlockSpec((tk, m), lambda js, k, order_ref: (k, 0)),
                      pl.BlockSpec((tk, ncols), lambda js, k, order_ref: (k, order_ref[js]))] + [any_spec] * n,
            out_specs=(any_spec,) * (n + 2),
            scratch_shapes=[pltpu.VMEM((m, ncols), F32), pltpu.VMEM((m, ncols), BF16),
                            pltpu.VMEM((4, m, ncols), BF16), pltpu.VMEM((3, m, ncols), BF16),
                            pltpu.SemaphoreType.DMA((4,)), pltpu.SemaphoreType.DMA((4,)),
                            pltpu.SemaphoreType.DMA,
                            pltpu.SemaphoreType.DMA((4 * n,)), pltpu.SemaphoreType.DMA((4 * n,)),
                            pltpu.SemaphoreType.DMA((2,)), pltpu.SemaphoreType.DMA((2,))]),
        out_shape=(jax.ShapeDtypeStruct((4, m, ncols), BF16),) * 2 + tuple(small_shapes4),
        compiler_params=pltpu.CompilerParams(vmem_limit_bytes=VMEM_LIMIT),
    )(order, h, dproj, *smalls)
    return res[0], res[1], res[2:]


def _chip_copies(ins, outs, send_sems, recv_sems, local_sems, hops):
    n = len(ins)
    x, y, c = lax.axis_index("x"), lax.axis_index("y"), lax.axis_index("c")
    my_chip = 2 * x + y

    def peer_of(k):
        return ((1 - x) if (k >> 1) & 1 else x, (1 - y) if k & 1 else y, c)

    def copy(a, k, out_chip):
        peer = peer_of(k)
        return pltpu.make_async_remote_copy(
            src_ref=ins[a].at[2 * peer[0] + peer[1]], dst_ref=outs[a].at[out_chip],
            send_sem=send_sems.at[a * 3 + k - 1], recv_sem=recv_sems.at[a * 3 + k - 1],
            device_id=peer, device_id_type=MESH)

    sends = [copy(a, k, my_chip) for k in range(1, 4) for a in range(n) if k in hops[a]]
    arrivals = []
    for k in range(1, 4):
        peer = peer_of(k)
        arrivals += [copy(a, k, 2 * peer[0] + peer[1]) for a in range(n) if k in hops[a]]
    mine = [pltpu.make_async_copy(ins[a].at[my_chip], outs[a].at[my_chip], local_sems.at[a])
            for a in range(n)]
    return sends, arrivals, mine


def _pair_add(core, mines, theirs):
    n = len(mines)

    def body(core_ref, *refs):
        mine, sib, outs = refs[:n], refs[n:2 * n], refs[2 * n:]
        for a in range(n):
            for q in range(4):
                outs[a][q] = (mine[a][2 * q + core_ref[0]].astype(F32)
                              + sib[a][q].astype(F32)).astype(outs[a].dtype)

    return pl.pallas_call(
        body, name="pair_add",
        in_specs=[pl.BlockSpec(memory_space=pltpu.SMEM)] + [pl.BlockSpec(memory_space=pltpu.VMEM)] * (2 * n),
        out_shape=tuple(jax.ShapeDtypeStruct(t.shape, t.dtype) for t in theirs),
    )(core, *mines, *theirs)


def _mod_exchange(c8, w_ada, b_cols):
    cols = w_ada.shape[1]

    def body(c_ref, w_ref, b_ref, call_ref, mod_ref, msend, send1, recv1, send2, recv2):
        x, y, c = lax.axis_index("x"), lax.axis_index("y"), lax.axis_index("c")
        my_slot = 4 * x + 2 * y + c

        def peer_of(k):
            return ((1 - x) if (k >> 2) & 1 else x, (1 - y) if (k >> 1) & 1 else y, (1 - c) if k & 1 else c)

        def slot_of(dev):
            return 4 * dev[0] + 2 * dev[1] + dev[2]

        def exchange(src_of, dst_ref, send_sems, recv_sems):
            sends, arrivals = [], []
            for k in range(1, 8):
                peer = peer_of(k)
                sends.append(pltpu.make_async_remote_copy(
                    src_ref=src_of(slot_of(peer)), dst_ref=dst_ref.at[my_slot],
                    send_sem=send_sems.at[k - 1], recv_sem=recv_sems.at[k - 1],
                    device_id=peer, device_id_type=MESH))
                arrivals.append(pltpu.make_async_remote_copy(
                    src_ref=src_of(my_slot), dst_ref=dst_ref.at[slot_of(peer)],
                    send_sem=send_sems.at[k - 1], recv_sem=recv_sems.at[k - 1],
                    device_id=peer, device_id_type=MESH))
            for cp in sends:
                cp.start()
            for cp in arrivals:
                cp.wait_recv()
            for cp in sends:
                cp.wait_send()

        call_ref[my_slot] = c_ref[...]
        exchange(lambda s: c_ref, call_ref, send1, recv1)
        cv = call_ref[...].reshape(N_DEV * 8, c_ref.shape[1])
        act = cv * _sigmoid(cv)
        mod = jnp.dot(act, w_ref[...], preferred_element_type=F32,
                      precision=lax.Precision.HIGHEST) + b_ref[...]
        msend[...] = mod.reshape(N_DEV, 8, cols)
        mod_ref[my_slot] = msend[my_slot]
        exchange(lambda s: msend.at[s], mod_ref, send2, recv2)

    return pl.pallas_call(
        body, name="mod_exchange",
        out_shape=(jax.ShapeDtypeStruct((N_DEV, 8, c8.shape[1]), F32),
                   jax.ShapeDtypeStruct((N_DEV, 8, cols), F32)),
        scratch_shapes=[pltpu.VMEM((N_DEV, 8, cols), F32)] + [pltpu.SemaphoreType.DMA((7,))] * 4,
    )(c8, w_ada, b_cols)


def _ada_bwd(c_all_t, dmod_cols):
    def body(c_ref, d_ref, o_ref):
        cv = c_ref[...]
        sc = cv * _sigmoid(cv)
        o_ref[...] = jnp.dot(sc, d_ref[...], preferred_element_type=F32,
                             precision=lax.Precision.HIGHEST)

    return pl.pallas_call(
        body, name="ada_bwd",
        out_shape=jax.ShapeDtypeStruct((c_all_t.shape[0], dmod_cols.shape[1]), F32),
    )(c_all_t, dmod_cols)


def _shard_order():
    x, y, c = lax.axis_index("x"), lax.axis_index("y"), lax.axis_index("c")
    devs = [(x, y, c), (x, y, 1 - c)]
    for chip in [(1 - x, y), (x, 1 - y), (1 - x, 1 - y)]:
        devs += [(*chip, c), (*chip, 1 - c)]
    return jnp.stack([4 * d[0] + 2 * d[1] + d[2] for d in devs]).astype(jnp.int32)


def _prep_h(x2, mod3):
    ts = 512
    per_seq = S // ts

    def body(x_ref, mod_ref, h_ref):
        shift = mod_ref[0, 0:1, :]
        scale = mod_ref[0, 1:2, :]
        h_ref[...] = (x_ref[...] * (1.0 + scale) + shift).astype(BF16)

    return pl.pallas_call(
        body, name="prep_h",
        grid=(T // ts,),
        in_specs=[pl.BlockSpec((ts, D), lambda i: (i, 0)),
                  pl.BlockSpec((1, 3, D), lambda i: (i // per_seq, 0, 0))],
        out_specs=pl.BlockSpec((ts, D), lambda i: (i, 0)),
        out_shape=jax.ShapeDtypeStruct((T, D), BF16),
    )(x2, mod3)


def _gather_proj(order, h, w_shard, tm, ride=()):
    rows, kdim = h.shape
    ncols = w_shard.shape[1]
    n_i = rows // tm
    ride_arrs, ride_shapes = ride if ride else ((), ())
    n_ride = len(ride_arrs)

    def body(order_ref, h_ref, mine_hbm, *rest):
        ride_ins = rest[:n_ride]
        o_ref, all_hbm = rest[n_ride:n_ride + 2]
        ride_outs = rest[n_ride + 2:2 * n_ride + 2]
        wv, send_sems, recv_sems, local_sems = rest[2 * n_ride + 2:2 * n_ride + 6]
        ride_scr = rest[2 * n_ride + 6:]
        j, i = pl.program_id(0), pl.program_id(1)
        x, y, c = lax.axis_index("x"), lax.axis_index("y"), lax.axis_index("c")
        me, sibling = (x, y, c), (x, y, 1 - c)
        chips = [(1 - x, y), (x, 1 - y), (1 - x, 1 - y)]

        def slot(dev):
            return 4 * dev[0] + 2 * dev[1] + dev[2]

        def copy(k, block, to):
            return pltpu.make_async_remote_copy(
                src_ref=wv.at[slot(block)], dst_ref=wv.at[slot(block)],
                send_sem=send_sems.at[k], recv_sem=recv_sems.at[k],
                device_id=to, device_id_type=MESH)

        def keep(step, block):
            return pltpu.make_async_copy(wv.at[slot(block)], all_hbm.at[slot(block)], local_sems.at[step])

        if n_ride:
            gather = _Gather(ride_ins, ride_outs, ride_scr[3:], *ride_scr[:3])
        first = [copy(0, me, sibling)] + [copy(1 + q, me, (*chip, c)) for q, chip in enumerate(chips)]
        passed = [copy(4 + q, (*chip, c), sibling) for q, chip in enumerate(chips)]
        due = [(me, None, None), (sibling, copy(0, sibling, me), None)]
        for q, chip in enumerate(chips):
            due.append(((*chip, c), copy(1 + q, (*chip, c), me), passed[q]))
            due.append(((*chip, 1 - c), copy(4 + q, (*chip, 1 - c), me), None))

        @pl.when((j == 0) & (i == 0))
        def _():
            load = pltpu.make_async_copy(mine_hbm, wv.at[slot(me)], local_sems.at[N_DEV])
            load.start()
            load.wait()
            for cp in first:
                cp.start()
            keep(0, me).start()

        for step in range(1, N_DEV):
            block, arrival, forward = due[step]

            @pl.when((j == step) & (i == 0))
            def _():
                arrival.wait_recv()
                if forward is not None:
                    forward.start()
                keep(step, block).start()
                if n_ride and step == N_DEV - 2:
                    gather.begin()

        o_ref[...] = _dot(h_ref[...], wv[order_ref[j]]).astype(BF16)

        @pl.when((j == N_DEV - 1) & (i == n_i - 1))
        def _():
            for cp in first + passed:
                cp.wait_send()
            for step in range(N_DEV):
                keep(step, due[step][0]).wait()
            if n_ride:
                gather.finish()

    any_spec = pl.BlockSpec(memory_space=pl.ANY)
    res = pl.pallas_call(
        body, name="gather_proj",
        grid_spec=pltpu.PrefetchScalarGridSpec(
            num_scalar_prefetch=1,
            grid=(N_DEV, n_i),
            in_specs=[pl.BlockSpec((tm, kdim), lambda j, i, order_ref: (i, 0)), any_spec] + [any_spec] * n_ride,
            out_specs=(pl.BlockSpec((tm, ncols), lambda j, i, order_ref: (i, order_ref[j])), any_spec)
                      + (any_spec,) * n_ride,
            scratch_shapes=[pltpu.VMEM((N_DEV, kdim, ncols), BF16),
                            pltpu.SemaphoreType.DMA((7,)), pltpu.SemaphoreType.DMA((7,)),
                            pltpu.SemaphoreType.DMA((N_DEV + 1,))]
                           + (_Gather.scratch(ride_arrs) if n_ride else [])),
        out_shape=(jax.ShapeDtypeStruct((rows, N_DEV * ncols), BF16),
                   jax.ShapeDtypeStruct((N_DEV, kdim, ncols), BF16)) + tuple(ride_shapes),
        compiler_params=pltpu.CompilerParams(vmem_limit_bytes=VMEM_LIMIT),
    )(order, h, w_shard, *ride_arrs)
    return res[0], res[1], res[2:]


def _bucket_maps():
    a = np.arange(QB)[:, None]
    b = np.arange(2 * QB)[None, :]
    steps = a + QB - b
    maps = []
    for dil in DILATIONS:
        dist = np.maximum(steps, 0) * dil
        nf = np.maximum(dist, 1).astype(np.float32)
        large = 16 + (np.log(nf / np.float32(16)) / np.float32(math.log(128.0))
                      * np.float32(16)).astype(np.int32)
        large = np.minimum(large, N_BUCKETS - 1)
        maps.append(np.where(dist < 16, dist, large).astype(np.int32))
    band = (steps >= 0) & (steps <= N_STEPS)
    first = band & (b >= QB)
    masks = np.stack([first, band]).astype(np.int32)
    return np.stack(maps), masks


def _bias_expand(rel_bias, buckets, masks):
    def body(tab_ref, bk_ref, mk_ref, o_ref):
        for g in range(3):
            bk = bk_ref[g]
            for h in range(4):
                col = 4 * g + h
                val = jnp.zeros((QB, 2 * QB), F32)
                for k in range(N_BUCKETS):
                    val = jnp.where(bk == k, tab_ref[k, col], val)
                o_ref[g, 0, h] = jnp.where(mk_ref[0] != 0, val, NEG_INF)
                o_ref[g, 1, h] = jnp.where(mk_ref[1] != 0, val, NEG_INF)

    return pl.pallas_call(
        body, name="bias_expand",
        in_specs=[pl.BlockSpec(memory_space=pltpu.SMEM),
                  pl.BlockSpec(memory_space=pltpu.VMEM),
                  pl.BlockSpec(memory_space=pltpu.VMEM)],
        out_shape=jax.ShapeDtypeStruct((3, 2, 4, QB, 2 * QB), F32),
    )(rel_bias, buckets, masks)


def _bias_grad(ds1, ds2, ds3, buckets):
    def body(d1_ref, d2_ref, d3_ref, bk_ref, o_ref):
        for g, d_ref in enumerate((d1_ref, d2_ref, d3_ref)):
            bk = bk_ref[g]
            for h in range(4):
                dv = d_ref[h]
                for k in range(N_BUCKETS):
                    o_ref[k, 4 * g + h] = jnp.sum(jnp.where(bk == k, dv, 0.0))

    return pl.pallas_call(
        body, name="bias_grad",
        in_specs=[pl.BlockSpec(memory_space=pltpu.VMEM)] * 4,
        out_specs=pl.BlockSpec(memory_space=pltpu.SMEM),
        out_shape=jax.ShapeDtypeStruct((N_BUCKETS, N_HEADS), F32),
    )(ds1, ds2, ds3, buckets)


def _scratch_sets(rows):
    return 4 if rows <= 512 else 1


def _unit_chunks(dil, size=16):
    units = [(h, r) for h in range(4) for r in range(dil)]
    return [units[i:i + size] for i in range(0, len(units), size)]


def _residue_rows(src_ref, copies, h, residue):
    sl = slice(h * HD, (h + 1) * HD)
    if copies is None:
        return lambda r: src_ref[:, sl]
    buf = copies[h % len(copies)]
    buf[...] = src_ref[:, sl].astype(F32)
    return lambda r: buf[residue(r), :].astype(BF16)


def _attn_fwd(proj, bias, g, ride=()):
    dil = DILATIONS[g]
    rows = QB * dil
    nsb = S // rows
    has_prev = nsb > 1

    def residue(r):
        return pl.ds(r, QB, stride=dil) if dil > 1 else pl.ds(0, QB)

    strided = dil > 1
    n_sets = _scratch_sets(rows)
    n_attn = 6 if has_prev else 4
    ride_arrs, ride_shapes = ride if ride else ((), ())
    n_ride = len(ride_arrs)
    n_in = n_attn + n_ride
    n_copied = (4 + (2 if has_prev else 0)) * (n_sets if strided else 0)

    def body(*refs):
        q_ref, kc_ref, vc_ref = refs[:3]
        kp_ref, vp_ref = refs[3:5] if has_prev else (None, None)
        b_ref = refs[n_attn - 1]
        o_ref, l_ref = refs[n_in:n_in + 2]
        scr = list(refs[n_in + 2 + n_ride:])
        ls = [scr.pop(0) for _ in range(4)]
        copies = {name: [scr.pop(0) for _ in range(n_sets)] if strided else None
                  for name in ("q", "kc", "vc", "o") + (("kp", "vp") if has_prev else ())}
        if n_ride:
            gather = _Gather(refs[n_attn:n_in], refs[n_in + 2:n_in + 2 + n_ride], scr[3:], *scr[:3])

            @pl.when((pl.program_id(0) == 0) & (pl.program_id(1) == 0))
            def _():
                gather.begin()
        lane = lax.broadcasted_iota(jnp.int32, (QB, 128), 1)
        refs_of = {"q": q_ref, "kc": kc_ref, "vc": vc_ref, "kp": kp_ref, "vp": vp_ref}
        for chunk in _unit_chunks(dil):
            rows_of = {h: {name: _residue_rows(refs_of[name], copies[name], h, residue)
                           for name in refs_of if refs_of[name] is not None}
                       for h in sorted({h for h, _ in chunk})}

            def batch(name):
                return jnp.stack([rows_of[h][name](r) for h, r in chunk])

            q, k, v = batch("q"), batch("kc"), batch("vc")
            if has_prev:
                k = jnp.concatenate([batch("kp"), k], axis=1)
                v = jnp.concatenate([batch("vp"), v], axis=1)
                bias_b = jnp.stack([b_ref[h] for h, _ in chunk])
            else:
                bias_b = jnp.stack([b_ref[h, :, QB:] for h, _ in chunk])
            s = jnp.einsum("uqd,ukd->uqk", q, k, preferred_element_type=F32) * SCALE + bias_b
            m = jnp.max(s, axis=-1, keepdims=True)
            p = jnp.exp(s - m)
            l = jnp.sum(p, axis=-1, keepdims=True)
            o = jnp.einsum("uqk,ukd->uqd", p.astype(BF16), v, preferred_element_type=F32) / l
            lse = m + jnp.log(l)
            for i, (h, r) in enumerate(chunk):
                if strided:
                    copies["o"][h % n_sets][residue(r), :] = o[i]
                else:
                    o_ref[:, h * HD:(h + 1) * HD] = o[i]
                ls[h][r * QB:(r + 1) * QB, :] = jnp.where(lane == h, lse[i], 0.0)
            if strided:
                for h in sorted({h for h, _ in chunk}):
                    o_ref[:, h * HD:(h + 1) * HD] = copies["o"][h % n_sets][...]
        for r in range(dil):
            blk = slice(r * QB, (r + 1) * QB)
            l_ref[residue(r), :] = (ls[0][blk, :] + ls[1][blk, :]) + (ls[2][blk, :] + ls[3][blk, :])
        if n_ride:
            @pl.when((pl.program_id(0) == BL - 1) & (pl.program_id(1) == nsb - 1))
            def _():
                gather.finish()

    def row(b, n):
        return b * nsb + n

    def prev(b, n):
        return b * nsb + jnp.maximum(n - 1, 0)

    in_specs = [
        pl.BlockSpec((rows, GW), lambda b, n: (row(b, n), CB_Q + g)),
        pl.BlockSpec((rows, GW), lambda b, n: (row(b, n), CB_K + g)),
        pl.BlockSpec((rows, GW), lambda b, n: (row(b, n), CB_V + g)),
    ]
    args = [proj, proj, proj]
    scratch = [pltpu.VMEM((rows, 128), F32)] * (4 + n_copied)
    if has_prev:
        in_specs += [pl.BlockSpec((rows, GW), lambda b, n: (prev(b, n), CB_K + g)),
                     pl.BlockSpec((rows, GW), lambda b, n: (prev(b, n), CB_V + g))]
        args += [proj, proj]
    in_specs.append(pl.BlockSpec((None, None, 4, QB, 2 * QB),
                                 lambda b, n: (g, jnp.minimum(n, 1), 0, 0, 0)))
    args.append(bias)
    any_spec = pl.BlockSpec(memory_space=pl.ANY)
    return pl.pallas_call(
        body, name=f"attn_fwd{g}",
        grid=(BL, nsb),
        in_specs=in_specs + [any_spec] * n_ride,
        out_specs=(pl.BlockSpec((rows, GW), lambda b, n: (row(b, n), 0)),
                   pl.BlockSpec((rows, 128), lambda b, n: (row(b, n), 0))) + (any_spec,) * n_ride,
        out_shape=(jax.ShapeDtypeStruct((T, GW), F32), jax.ShapeDtypeStruct((T, 128), F32)) + tuple(ride_shapes),
        scratch_shapes=scratch + (_Gather.scratch(ride_arrs) if n_ride else []),
        compiler_params=pltpu.CompilerParams(vmem_limit_bytes=VMEM_LIMIT),
    )(*args, *ride_arrs)


def _attn_bwd(proj, d_out, stats, bias, dproj, g):
    dil = DILATIONS[g]
    rows = QB * dil
    nsb = S // rows
    has_prev = nsb > 1
    n_steps = nsb + 1 if has_prev else 1
    n_in = 7 + (2 if has_prev else 0)

    def residue(r):
        return pl.ds(r, QB, stride=dil) if dil > 1 else pl.ds(0, QB)

    strided = dil > 1
    n_sets = _scratch_sets(rows)

    def body(*refs):
        q_ref, kc_ref, vc_ref, do_ref, st_ref, b_ref = refs[:6]
        kp_ref, vp_ref = refs[6:8] if has_prev else (None, None)
        out_ref, db_ref = refs[n_in], refs[n_in + 1]
        scr = list(refs[n_in + 2:])
        sq, sk, sv, sems = [scr.pop(0) for _ in range(4)]
        carry = scr.pop(0) if has_prev else None
        sts = scr.pop(0) if strided else st_ref
        copies = {name: [scr.pop(0) for _ in range(n_sets)] if strided else None
                  for name in ("q", "kc", "vc", "do", "dq", "dk", "dv") + (("kp", "vp") if has_prev else ())}
        b, n = pl.program_id(0), pl.program_id(1)

        @pl.when((b == 0) & (n == 0))
        def _():
            db_ref[...] = jnp.zeros_like(db_ref)

        def finish(h, r, dq, dk, dv):
            if strided:
                for name, val in (("dq", dq), ("dk", dk), ("dv", dv)):
                    copies[name][h % n_sets][residue(r), :] = val
            else:
                sl = slice(h * HD, (h + 1) * HD)
                sq[:, sl], sk[:, sl], sv[:, sl] = dq.astype(BF16), dk.astype(BF16), dv.astype(BF16)

        def finish_head(h):
            if strided:
                sl = slice(h * HD, (h + 1) * HD)
                sq[:, sl] = copies["dq"][h % n_sets][...].astype(BF16)
                sk[:, sl] = copies["dk"][h % n_sets][...].astype(BF16)
                sv[:, sl] = copies["dv"][h % n_sets][...].astype(BF16)

        def write_block(blk_idx):
            row0 = pl.multiple_of(blk_idx * rows, rows)
            _write_columns([(sq, CB * (CB_Q + g)), (sk, CB * (CB_K + g)), (sv, CB * (CB_V + g))],
                           out_ref, row0, sems)

        def carried(h, r):
            blk = slice(r * QB, (r + 1) * QB)
            return ((blk, slice(h * HD, (h + 1) * HD)), (blk, slice(GW + h * HD, GW + (h + 1) * HD)),
                    (blk, slice(2 * GW + h * HD, 2 * GW + (h + 1) * HD)))

        if has_prev:
            @pl.when(n == 0)
            def _():
                carry[...] = jnp.zeros_like(carry)

            @pl.when(n == nsb)
            def _():
                for h in range(4):
                    for r in range(dil):
                        cq, ck, cv = carried(h, r)
                        finish(h, r, carry[cq], carry[ck], carry[cv])
                    finish_head(h)
                write_block(b * nsb + nsb - 1)

        @pl.when(n < nsb)
        def _():
            if strided:
                for r in range(dil):
                    sts[r * QB:(r + 1) * QB, :] = st_ref[residue(r), :]
            refs_of = {"q": q_ref, "kc": kc_ref, "vc": vc_ref, "do": do_ref, "kp": kp_ref, "vp": vp_ref}
            for chunk in _unit_chunks(dil):
                heads = sorted({h for h, _ in chunk})
                rows_of = {h: {name: _residue_rows(refs_of[name], copies[name], h, residue)
                               for name in refs_of if refs_of[name] is not None}
                           for h in heads}

                def batch(name):
                    return jnp.stack([rows_of[h][name](r) for h, r in chunk])

                q, k, v, do = batch("q"), batch("kc"), batch("vc"), batch("do")
                if has_prev:
                    k = jnp.concatenate([batch("kp"), k], axis=1)
                    v = jnp.concatenate([batch("vp"), v], axis=1)
                    bias_b = jnp.stack([b_ref[h] for h, _ in chunk])
                else:
                    bias_b = jnp.stack([b_ref[h, :, QB:] for h, _ in chunk])
                lse = jnp.stack([sts[r * QB:(r + 1) * QB, h:h + 1] for h, r in chunk])
                delta = jnp.stack([sts[r * QB:(r + 1) * QB, 4 + h:5 + h] for h, r in chunk])
                s = jnp.einsum("uqd,ukd->uqk", q, k, preferred_element_type=F32) * SCALE + bias_b
                p = jnp.exp(s - lse)
                ds = p * (jnp.einsum("uqd,ukd->uqk", do, v, preferred_element_type=F32) - delta)
                for h in heads:
                    mine = [ds[i] for i, (hh, _) in enumerate(chunk) if hh == h]
                    tot = mine[0]
                    for extra in mine[1:]:
                        tot = tot + extra
                    if has_prev:
                        db_ref[h] += tot
                    else:
                        db_ref[h, :, QB:] += tot
                dsb, pb = ds.astype(BF16), p.astype(BF16)
                dq = jnp.einsum("uqk,ukd->uqd", dsb, k, preferred_element_type=F32) * SCALE
                dk = jnp.einsum("uqk,uqd->ukd", dsb, q, preferred_element_type=F32) * SCALE
                dv = jnp.einsum("uqk,uqd->ukd", pb, do, preferred_element_type=F32)
                for i, (h, r) in enumerate(chunk):
                    if has_prev:
                        cq, ck, cv = carried(h, r)
                        finish(h, r, carry[cq], carry[ck] + dk[i, :QB], carry[cv] + dv[i, :QB])
                        carry[cq] = dq[i]
                        carry[ck] = dk[i, QB:]
                        carry[cv] = dv[i, QB:]
                    else:
                        finish(h, r, dq[i], dk[i], dv[i])
                for h in heads:
                    finish_head(h)
            if has_prev:
                @pl.when(n > 0)
                def _():
                    write_block(b * nsb + n - 1)
            else:
                write_block(b)

    def row(b, n):
        return b * nsb + jnp.minimum(n, nsb - 1)

    def prev(b, n):
        return b * nsb + jnp.maximum(jnp.minimum(n, nsb - 1) - 1, 0)

    in_specs = [
        pl.BlockSpec((rows, GW), lambda b, n: (row(b, n), CB_Q + g)),
        pl.BlockSpec((rows, GW), lambda b, n: (row(b, n), CB_K + g)),
        pl.BlockSpec((rows, GW), lambda b, n: (row(b, n), CB_V + g)),
        pl.BlockSpec((rows, GW), lambda b, n: (row(b, n), 0)),
        pl.BlockSpec((rows, 128), lambda b, n: (row(b, n), 0)),
        pl.BlockSpec((None, None, 4, QB, 2 * QB),
                     lambda b, n: (g, jnp.minimum(jnp.minimum(n, nsb - 1), 1), 0, 0, 0)),
    ]
    args = [proj, proj, proj, d_out, stats, bias]
    scratch = [pltpu.VMEM((rows, GW), BF16)] * 3 + [pltpu.SemaphoreType.DMA((3,))]
    if has_prev:
        in_specs += [pl.BlockSpec((rows, GW), lambda b, n: (prev(b, n), CB_K + g)),
                     pl.BlockSpec((rows, GW), lambda b, n: (prev(b, n), CB_V + g))]
        args += [proj, proj]
        scratch.append(pltpu.VMEM((rows, 3 * GW), F32))
    if strided:
        n_copied = (7 + (2 if has_prev else 0)) * n_sets
        scratch += [pltpu.VMEM((rows, 128), F32)] * (1 + n_copied)
    in_specs.append(pl.BlockSpec(memory_space=pl.ANY))
    args.append(dproj)
    return pl.pallas_call(
        body, name=f"attn_bwd{g}",
        grid=(BL, n_steps),
        in_specs=in_specs,
        out_specs=(pl.BlockSpec(memory_space=pl.ANY),
                   pl.BlockSpec((4, QB, 2 * QB), lambda b, n: (0, 0, 0))),
        out_shape=(jax.ShapeDtypeStruct((T, NCOL), BF16),
                   jax.ShapeDtypeStruct((4, QB, 2 * QB), F32)),
        scratch_shapes=scratch,
        input_output_aliases={len(args) - 1: 0},
        compiler_params=pltpu.CompilerParams(vmem_limit_bytes=VMEM_LIMIT),
    )(*args)


def _tail(x2, tgt2, mod3, o_g, lse_g, proj, w_ao, w_co, w_o, conv_w, conv_b, ln_g, ln_b):
    tm = 256
    per_seq = S // tm
    halo = 16

    def body(x_ref, t_ref, mod_ref, o1_ref, o2_ref, o3_ref, l1_ref, l2_ref, l3_ref,
             ga_ref, u_ref, bg_ref, cg_ref, gc_ref, ma_ref, mc_ref, up_ref, cp_ref,
             wao_ref, wco_ref, wo_ref, cw_ref, cb_ref, lg_ref, lb_ref,
             dproj_ref, dyc_ref, do_ref, st_ref, dxd_ref,
             mg_ref, dy_ref, ain_ref, dao_ref, sin_ref, dso_ref, vec_ref,
             dga_s, dbg_s, dgm_s, sems):
        i = pl.program_id(0)
        bidx = i // per_seq
        first = (i % per_seq) == 0

        @pl.when(i == 0)
        def _():
            vec_ref[...] = jnp.zeros_like(vec_ref)

        l1, l2, l3 = l1_ref[...], l2_ref[...], l3_ref[...]
        mx = jnp.maximum(jnp.maximum(l1, l2), l3)
        e1, e2, e3 = jnp.exp(l1 - mx), jnp.exp(l2 - mx), jnp.exp(l3 - mx)
        esum = e1 + e2 + e3
        lse_tot = mx + jnp.log(esum)
        w1, w2, w3 = e1 / esum, e2 / esum, e3 / esum

        def per_head(wv):
            return jnp.concatenate([jnp.broadcast_to(wv[:, h:h + 1], (tm, HD)) for h in range(4)], axis=1)

        o = per_head(w1) * o1_ref[...] + per_head(w2) * o2_ref[...] + per_head(w3) * o3_ref[...]

        ga = ga_ref[...].astype(F32)
        sig_ga = _sigmoid(ga)
        silu_ga = ga * sig_ga
        a_in = (o * silu_ga).astype(BF16)
        a_out = _dot(a_in, wao_ref[...])

        u = u_ref[...].astype(F32)
        cg = cg_ref[...].astype(F32)
        z = cg * u
        zp = cp_ref[...].astype(F32) * up_ref[...].astype(F32)
        zp = jnp.where(first, 0.0, zp)
        zcat = jnp.concatenate([zp, z], axis=0)
        z1 = pltpu.roll(zcat, 1, 0)[halo:]
        z2 = pltpu.roll(zcat, 2, 0)[halo:]
        y_conv = cw_ref[0:1, :] * z2 + cw_ref[1:2, :] * z1 + cw_ref[2:3, :] * z + cb_ref[...]
        gc = gc_ref[...].astype(F32)
        sig_gc = _sigmoid(gc)
        silu_gc = gc * sig_gc
        bg = bg_ref[...].astype(F32)
        bg_yc = bg * y_conv
        s_in = (bg_yc * silu_gc).astype(BF16)
        s_out = _dot(s_in, wco_ref[...])

        sa = _sigmoid(ma_ref[...].astype(F32))
        sc = _sigmoid(mc_ref[...].astype(F32))
        merged = (sa * a_out + sc * s_out).astype(BF16)
        y = _dot(merged, wo_ref[...])
        gate1 = 1.0 + mod_ref[0, 2:3, :]
        xv = x_ref[...]
        resid = ALPHA * xv + gate1 * y
        mu = jnp.mean(resid, axis=1, keepdims=True)
        xc = resid - mu
        var = jnp.mean(xc * xc, axis=1, keepdims=True)
        rstd = lax.rsqrt(var + LN_EPS)
        xhat = xc * rstd
        lg = lg_ref[...]
        err = xhat * lg + lb_ref[...] - t_ref[...]
        vec_ref[3:4, :] += (0.5 / D) * jnp.sum(err * err, axis=0, keepdims=True)

        vec_ref[1:2, :] += (1.0 / D) * jnp.sum(err * xhat, axis=0, keepdims=True)
        vec_ref[2:3, :] += (1.0 / D) * jnp.sum(err, axis=0, keepdims=True)
        dxh = err * (lg * (1.0 / D))
        dres = rstd * (dxh - jnp.mean(dxh, axis=1, keepdims=True)
                       - xhat * jnp.mean(dxh * xhat, axis=1, keepdims=True))
        dxd_ref[...] = ALPHA * dres
        dgate = jnp.sum(dres * y, axis=0, keepdims=True)
        vec_ref[4:5, :] += jnp.where(bidx == 0, dgate, 0.0)
        vec_ref[5:6, :] += jnp.where(bidx == 1, dgate, 0.0)
        dy = (dres * gate1).astype(BF16)

        dmerged = _dot_nt(dy, wo_ref[...])
        da_out_f = dmerged * sa
        ds_out_f = dmerged * sc
        da_out = da_out_f.astype(BF16)
        ds_out = ds_out_f.astype(BF16)
        dgm_s[:, 2 * D:3 * D] = (ds_out_f * s_out * (1.0 - sc)).astype(BF16)
        dgm_s[:, D:2 * D] = (da_out_f * a_out * (1.0 - sa)).astype(BF16)
        da_in = _dot_nt(da_out, wao_ref[...])
        ds_in = _dot_nt(ds_out, wco_ref[...])

        d_o = da_in * silu_ga
        do_ref[...] = d_o.astype(BF16)
        dga_s[...] = (da_in * o * (sig_ga + silu_ga * (1.0 - sig_ga))).astype(BF16)
        lane = lax.broadcasted_iota(jnp.int32, (tm, 128), 1)
        stats = lse_tot
        od = o * d_o
        for h in range(4):
            delta = jnp.sum(od[:, h * HD:(h + 1) * HD], axis=1, keepdims=True)
            stats = jnp.where(lane == 4 + h, delta, stats)
        st_ref[...] = stats

        ds_silu = ds_in * silu_gc
        dbg_s[...] = (ds_silu * y_conv).astype(BF16)
        dyc = ds_silu * bg
        dyc_ref[...] = dyc
        vec_ref[0:1, :] += jnp.sum(dyc, axis=0, keepdims=True)
        dgm_s[:, 0:D] = (ds_in * bg_yc * (sig_gc + silu_gc * (1.0 - sig_gc))).astype(BF16)

        mg_ref[...] = merged
        dy_ref[...] = dy
        ain_ref[...] = a_in
        dao_ref[...] = da_out
        sin_ref[...] = s_in
        dso_ref[...] = ds_out
        _write_columns([(dga_s, CB * CB_GA), (dbg_s, D * KB_BG), (dgm_s, D * KB_GC)],
                       dproj_ref, pl.multiple_of(i * tm, tm), sems)

    def tile(width, cblk=0):
        return pl.BlockSpec((tm, width), lambda i: (i, cblk))

    def whole(shape):
        return pl.BlockSpec(shape, lambda i: tuple(0 for _ in shape))

    prev_rows = lambda i: (jnp.maximum(i * (tm // halo) - 1, 0),)
    in_specs = [
        tile(D), tile(D), pl.BlockSpec((1, 3, D), lambda i: (i // per_seq, 0, 0)),
        tile(GW), tile(GW), tile(GW), tile(128), tile(128), tile(128),
        tile(GW, CB_GA), tile(D, KB_U), tile(D, KB_BG), tile(D, KB_CG), tile(D, KB_GC),
        tile(D, KB_MA), tile(D, KB_MC),
        pl.BlockSpec((halo, D), lambda i: (*prev_rows(i), KB_U)),
        pl.BlockSpec((halo, D), lambda i: (*prev_rows(i), KB_CG)),
        whole((GW, D)), whole((D, D)), whole((D, D)),
        whole((3, D)), whole((1, D)), whole((1, D)), whole((1, D)),
    ]
    out_specs = (
        pl.BlockSpec(memory_space=pl.ANY), tile(D), tile(GW), tile(128), tile(D),
        tile(D), tile(D), tile(GW), tile(D), tile(D), tile(D),
        pl.BlockSpec((8, D), lambda i: (0, 0)),
    )
    out_shape = (
        jax.ShapeDtypeStruct((T, NCOL), BF16),
        jax.ShapeDtypeStruct((T, D), F32),
        jax.ShapeDtypeStruct((T, GW), BF16),
        jax.ShapeDtypeStruct((T, 128), F32),
        jax.ShapeDtypeStruct((T, D), F32),
        jax.ShapeDtypeStruct((T, D), BF16),
        jax.ShapeDtypeStruct((T, D), BF16),
        jax.ShapeDtypeStruct((T, GW), BF16),
        jax.ShapeDtypeStruct((T, D), BF16),
        jax.ShapeDtypeStruct((T, D), BF16),
        jax.ShapeDtypeStruct((T, D), BF16),
        jax.ShapeDtypeStruct((8, D), F32),
    )
    return pl.pallas_call(
        body, name="tail",
        grid=(T // tm,),
        in_specs=in_specs, out_specs=out_specs, out_shape=out_shape,
        scratch_shapes=[pltpu.VMEM((tm, GW), BF16), pltpu.VMEM((tm, D), BF16), pltpu.VMEM((tm, 3 * D), BF16),
                        pltpu.SemaphoreType.DMA((3,))],
        compiler_params=pltpu.CompilerParams(vmem_limit_bytes=VMEM_LIMIT),
    )(x2, tgt2, mod3, *o_g, *lse_g, proj, proj, proj, proj, proj, proj, proj, proj, proj,
      w_ao, w_co, w_o, conv_w, conv_b, ln_g, ln_b)


def _conv_bwd(dyc, proj, conv_w, dproj):
    tm = 512
    per_seq = S // tm
    halo = 16

    def body(d_ref, dn_ref, u_ref, c_ref, up_ref, cp_ref, cw_ref, _, dproj_ref, g_ref, du_s, dc_s, sems):
        i = pl.program_id(0)
        first = (i % per_seq) == 0
        last = (i % per_seq) == per_seq - 1

        @pl.when(i == 0)
        def _():
            g_ref[...] = jnp.zeros_like(g_ref)

        d = d_ref[...]
        dn = jnp.where(last, 0.0, dn_ref[...])
        dcat = jnp.concatenate([d, dn], axis=0)
        d1 = pltpu.roll(dcat, tm + 8 - 1, 0)[:tm]
        d2 = pltpu.roll(dcat, tm + 8 - 2, 0)[:tm]
        dz = cw_ref[2:3, :] * d + cw_ref[1:2, :] * d1 + cw_ref[0:1, :] * d2
        u = u_ref[...].astype(F32)
        cg = c_ref[...].astype(F32)
        du_s[...] = (dz * cg).astype(BF16)
        dc_s[...] = (dz * u).astype(BF16)
        _write_columns([(du_s, D * KB_U), (dc_s, D * KB_CG)], dproj_ref, pl.multiple_of(i * tm, tm), sems)

        z = cg * u
        zp = jnp.where(first, 0.0, cp_ref[...].astype(F32) * up_ref[...].astype(F32))
        zcat = jnp.concatenate([zp, z], axis=0)
        z1 = pltpu.roll(zcat, 1, 0)[halo:]
        z2 = pltpu.roll(zcat, 2, 0)[halo:]
        g_ref[0:1, :] += jnp.sum(d * z2, axis=0, keepdims=True)
        g_ref[1:2, :] += jnp.sum(d * z1, axis=0, keepdims=True)
        g_ref[2:3, :] += jnp.sum(d * z, axis=0, keepdims=True)

    n_tiles = T // tm
    prev_rows = lambda i: jnp.maximum(i * (tm // halo) - 1, 0)
    next_rows = lambda i: jnp.minimum((i + 1) * (tm // 8), T // 8 - 1)
    return pl.pallas_call(
        body, name="conv_bwd",
        grid=(n_tiles,),
        in_specs=[pl.BlockSpec((tm, D), lambda i: (i, 0)),
                  pl.BlockSpec((8, D), lambda i: (next_rows(i), 0)),
                  pl.BlockSpec((tm, D), lambda i: (i, KB_U)),
                  pl.BlockSpec((tm, D), lambda i: (i, KB_CG)),
                  pl.BlockSpec((halo, D), lambda i: (prev_rows(i), KB_U)),
                  pl.BlockSpec((halo, D), lambda i: (prev_rows(i), KB_CG)),
                  pl.BlockSpec((3, D), lambda i: (0, 0)),
                  pl.BlockSpec(memory_space=pl.ANY)],
        out_specs=(pl.BlockSpec(memory_space=pl.ANY),
                   pl.BlockSpec((8, D), lambda i: (0, 0))),
        out_shape=(jax.ShapeDtypeStruct((T, NCOL), BF16),
                   jax.ShapeDtypeStruct((8, D), F32)),
        scratch_shapes=[pltpu.VMEM((tm, D), BF16), pltpu.VMEM((tm, D), BF16), pltpu.SemaphoreType.DMA((2,))],
        input_output_aliases={7: 0},
        compiler_params=pltpu.CompilerParams(vmem_limit_bytes=VMEM_LIMIT),
    )(dyc, dyc, proj, proj, proj, proj, conv_w, dproj)


def _dh_dx(dproj, w_in_all, x2, dxd, mod3, chip_sums, hops=(), parts0=None):
    tm = 1024
    per_seq = S // tm
    n = len(chip_sums)
    n_in = 5 + n + (0 if parts0 is None else 1)

    def body(*refs):
        d_ref, w_ref, x_ref, dxd_ref, mod_ref = refs[:5]
        ins = refs[5:5 + n]
        gx_ref, vec_ref = refs[n_in:n_in + 2]
        outs = refs[n_in + 2:n_in + 2 + n]
        acc, send_sems, recv_sems, local_sems = refs[n_in + 2 + n:]
        i, jj = pl.program_id(0), pl.program_id(1)

        @pl.when((i == 0) & (jj == 0))
        def _():
            vec_ref[...] = jnp.zeros_like(vec_ref)
            if n:
                sends, _, mine = _chip_copies(ins, outs, send_sems, recv_sems, local_sems, hops)
                for cp in sends + mine:
                    cp.start()

        if n:
            @pl.when((i == T // tm - 1) & (jj == N_DEV - 1))
            def _():
                sends, arrivals, mine = _chip_copies(ins, outs, send_sems, recv_sems, local_sems, hops)
                for cp in arrivals:
                    cp.wait_recv()
                for cp in sends:
                    cp.wait_send()
                for cp in mine:
                    cp.wait()

        @pl.when(jj == 0)
        def _():
            acc[...] = jnp.zeros_like(acc)

        acc[...] += _dot_nt(d_ref[...], w_ref[...])

        @pl.when(jj == N_DEV - 1)
        def _():
            dh = acc[...]
            bidx = i // per_seq
            gx_ref[...] = dxd_ref[...] + dh * (1.0 + mod_ref[0, 1:2, :])
            dshift = jnp.sum(dh, axis=0, keepdims=True)
            dscale = jnp.sum(dh * x_ref[...], axis=0, keepdims=True)
            vec_ref[0:1, :] += jnp.where(bidx == 0, dshift, 0.0)
            vec_ref[1:2, :] += jnp.where(bidx == 1, dshift, 0.0)
            vec_ref[2:3, :] += jnp.where(bidx == 0, dscale, 0.0)
            vec_ref[3:4, :] += jnp.where(bidx == 1, dscale, 0.0)

    any_spec = pl.BlockSpec(memory_space=pl.ANY)
    res = pl.pallas_call(
        body, name="dh_dx",
        grid=(T // tm, N_DEV),
        in_specs=[
            pl.BlockSpec((tm, SHARD), lambda i, jj: (i, jj)),
            pl.BlockSpec((None, D, SHARD), lambda i, jj: (jj, 0, 0)),
            pl.BlockSpec((tm, D), lambda i, jj: (i, 0)),
            pl.BlockSpec((tm, D), lambda i, jj: (i, 0)),
            pl.BlockSpec((1, 3, D), lambda i, jj: (i // per_seq, 0, 0))] + [any_spec] * (n_in - 5),
        out_specs=(pl.BlockSpec((tm, D), lambda i, jj: (i, 0)),
                   pl.BlockSpec((8, D), lambda i, jj: (0, 0))) + (any_spec,) * n,
        out_shape=(jax.ShapeDtypeStruct((T, D), F32), jax.ShapeDtypeStruct((8, D), F32))
                  + tuple(jax.ShapeDtypeStruct(a.shape, a.dtype) for a in chip_sums),
        scratch_shapes=[pltpu.VMEM((tm, D), F32), pltpu.SemaphoreType.DMA((max(3 * n, 1),)),
                        pltpu.SemaphoreType.DMA((max(3 * n, 1),)), pltpu.SemaphoreType.DMA((max(n, 1),))],
        input_output_aliases={} if parts0 is None else {5 + n: 2},
        compiler_params=pltpu.CompilerParams(vmem_limit_bytes=VMEM_LIMIT),
    )(dproj, w_in_all, x2, dxd, mod3, *chip_sums, *([] if parts0 is None else [parts0]))
    return res[0], res[1], res[2:]


def _mm_tn(a, b, tn, blocks_leading, name):
    kk, m = a.shape
    n = b.shape[1]
    tk = 2048

    def body(a_ref, b_ref, o_ref, acc):
        @pl.when(pl.program_id(1) == 0)
        def _():
            acc[...] = jnp.zeros_like(acc)

        acc[...] += _dot_tn(a_ref[...], b_ref[...])

        @pl.when(pl.program_id(1) == kk // tk - 1)
        def _():
            o_ref[...] = acc[...].astype(BF16)

    if blocks_leading:
        out_spec = pl.BlockSpec((None, m, tn), lambda j, k: (j, 0, 0))
        out_shape = jax.ShapeDtypeStruct((n // tn, m, tn), BF16)
    else:
        out_spec = pl.BlockSpec((m, tn), lambda j, k: (0, j))
        out_shape = jax.ShapeDtypeStruct((m, n), BF16)
    return pl.pallas_call(
        body, name=name,
        grid=(n // tn, kk // tk),
        in_specs=[pl.BlockSpec((tk, m), lambda j, k: (k, 0)),
                  pl.BlockSpec((tk, tn), lambda j, k: (k, j))],
        out_specs=out_spec, out_shape=out_shape,
        scratch_shapes=[pltpu.VMEM((m, tn), F32)],
        compiler_params=pltpu.CompilerParams(vmem_limit_bytes=VMEM_LIMIT),
    )(a, b)


def _adam_step(g, w, m, v):
    nm = ADAM_B1 * m + (1.0 - ADAM_B1) * g
    nv = ADAM_B2 * v + (1.0 - ADAM_B2) * (g * g)
    m_hat = nm / (1.0 - ADAM_B1 ** ADAM_STEP)
    v_hat = nv / (1.0 - ADAM_B2 ** ADAM_STEP)
    return -ADAM_LR * (m_hat / (jnp.sqrt(v_hat) + ADAM_EPS) + ADAM_WD * w), nm, nv


def _adamw(parts, w, m, v, name, row_tile=None):
    n_parts, rows, cols = parts.shape
    tr = rows if row_tile is None else row_tile

    def body(p_ref, w_ref, m_ref, v_ref, g_ref, d_ref, nm_ref, nv_ref):
        g = p_ref[0].astype(F32)
        for s in range(1, n_parts):
            g = g + p_ref[s].astype(F32)
        g_ref[...] = g
        d_ref[...], nm_ref[...], nv_ref[...] = _adam_step(g, w_ref[...], m_ref[...], v_ref[...])

    blk = pl.BlockSpec((tr, cols), lambda i: (i, 0))
    shp = jax.ShapeDtypeStruct((rows, cols), F32)
    return pl.pallas_call(
        body, name=name,
        grid=(rows // tr,),
        in_specs=[pl.BlockSpec((n_parts, tr, cols), lambda i: (0, i, 0)), blk, blk, blk],
        out_specs=(blk, blk, blk, blk),
        out_shape=(shp, shp, shp, shp),
        compiler_params=pltpu.CompilerParams(vmem_limit_bytes=VMEM_LIMIT),
    )(parts, w, m, v)


def _multi_adamw(parts_list, params, name):
    n = len(params)
    flat = [t for wmv in params for t in wmv]

    def body(*refs):
        parts, ins, outs = refs[:n], refs[n:4 * n], refs[4 * n:]
        for p in range(n):
            g = parts[p][0].astype(F32)
            for s in range(1, parts[p].shape[0]):
                g = g + parts[p][s].astype(F32)
            w_ref, m_ref, v_ref = ins[3 * p:3 * p + 3]
            g_ref, d_ref, nm_ref, nv_ref = outs[4 * p:4 * p + 4]
            g_ref[...] = g
            d_ref[...], nm_ref[...], nv_ref[...] = _adam_step(g, w_ref[...], m_ref[...], v_ref[...])

    out_shape = []
    for w, _, _ in params:
        out_shape += [jax.ShapeDtypeStruct(w.shape, F32)] * 4
    res = pl.pallas_call(body, name=name, out_shape=tuple(out_shape))(*parts_list, *flat)
    return [res[4 * p:4 * p + 4] for p in range(n)]


def _small_updates(small_g, dmod_all, rel_parts, params):
    flat = [t for wmv in params for t in wmv]

    def body(sg_ref, dm_ref, rp_ref, *refs):
        ins, outs = refs[:len(flat)], refs[len(flat):]

        def over_devices(row):
            tot = sg_ref[0, row:row + 1, :]
            for s in range(1, N_DEV):
                tot = tot + sg_ref[s, row:row + 1, :]
            return tot

        g_b_ada = dm_ref[0:1, :]
        for r in range(1, N_DEV * BL):
            g_b_ada = g_b_ada + dm_ref[r:r + 1, :]
        g_rel = rp_ref[0]
        for s in range(1, N_DEV):
            g_rel = g_rel + rp_ref[s]
        grads = [g_b_ada, over_devices(0), g_rel, over_devices(1), over_devices(2)]
        outs[0][...] = jnp.sum(over_devices(3), axis=1, keepdims=True)
        for p, g in enumerate(grads):
            w_ref, m_ref, v_ref = ins[3 * p:3 * p + 3]
            g_ref, d_ref, nm_ref, nv_ref = outs[1 + 4 * p:5 + 4 * p]
            g_ref[...] = g
            d_ref[...], nm_ref[...], nv_ref[...] = _adam_step(g, w_ref[...], m_ref[...], v_ref[...])

    out_shape = [jax.ShapeDtypeStruct((1, 1), F32)]
    for w, _, _ in params:
        out_shape += [jax.ShapeDtypeStruct(w.shape, F32)] * 4
    res = pl.pallas_call(body, name="small_updates", out_shape=tuple(out_shape))(small_g, dmod_all, rel_parts, *flat)
    return res[0], [res[1 + 4 * p:5 + 4 * p] for p in range(len(params))]


def _attention_forward(proj, rel_bias, rides=((), (), ())):
    buckets_np, masks_np = _bucket_maps()
    buckets, masks = jnp.asarray(buckets_np), jnp.asarray(masks_np)
    bias = _bias_expand(rel_bias, buckets, masks)
    fwd = [_attn_fwd(proj, bias, g, rides[g]) for g in range(3)]
    brought = [arr for f in fwd for arr in f[2:]]
    return bias, buckets, [f[0] for f in fwd], [f[1] for f in fwd], brought


def _local_step(x2, tgt2, mod3, h, proj, attn, w_ao, w_co, w_o, conv_w, conv_b, ln_g, ln_b):
    bias, buckets, o_g, lse_g = attn

    (dproj, dyc, d_o, stats, dxd, merged, dy, a_in, da_out, s_in, ds_out, tail_vec) = _tail(
        x2, tgt2, mod3, o_g, lse_g, proj, w_ao, w_co, w_o, conv_w, conv_b, ln_g, ln_b)

    dbias = []
    for g in range(3):
        dproj, db = _attn_bwd(proj, d_o, stats, bias, dproj, g)
        dbias.append(db)
    g_rel_bias = _bias_grad(*dbias, buckets)
    dproj, conv_vec = _conv_bwd(dyc, proj, conv_w, dproj)

    gw_o = _mm_tn(merged, dy, D, False, "gw_o")
    gw_co = _mm_tn(s_in, ds_out, D, False, "gw_conv_out")
    gw_ao = _mm_tn(a_in, da_out, D, False, "gw_attn_out")
    gw_ao = jnp.transpose(gw_ao.reshape(GW, N_DEV, D // N_DEV), (1, 0, 2))
    return dproj, dxd, gw_ao, gw_co, gw_o, conv_vec, g_rel_bias, tail_vec


def kernel(x, c, w_ada, b_ada, w_in, conv_w, conv_b, rel_bias, w_attn_out, w_conv_out, w_o, ln_g, ln_b, loss_target, m_w_ada, m_b_ada, m_w_in, m_conv_w, m_conv_b, m_rel_bias, m_w_attn_out, m_w_conv_out, m_w_o, m_ln_g, m_ln_b, v_w_ada, v_b_ada, v_w_in, v_conv_w, v_conv_b, v_rel_bias, v_w_attn_out, v_w_conv_out, v_w_o, v_ln_g, v_ln_b):
    me = _my_index()
    x2 = x.reshape(T, D)
    tgt2 = loss_target.reshape(T, D)

    b_cols = lax.dynamic_slice(b_ada, (0, me * ADA_SHARD), (1, ADA_SHARD))
    c_g, mod_in = _mod_exchange(jnp.pad(c, ((0, 8 - BL), (0, 0))), w_ada[0], b_cols)
    c_all = c_g[:, 0:BL, :].reshape(N_DEV * BL, D)
    mod3 = jnp.transpose(mod_in[:, 0:BL, :], (1, 0, 2)).reshape(BL, 3, D)

    h = _prep_h(x2, mod3)
    rows_shape = jax.ShapeDtypeStruct((N_DEV, D // N_DEV, D), BF16)
    proj, w_in_all, (w_ao_g, w_co_g, w_o_g, conv_w_g) = _gather_proj(
        _shard_order(), h, w_in[0].astype(BF16), 1024,
        ([w_attn_out[0].astype(BF16), w_conv_out[0].astype(BF16), w_o[0].astype(BF16), conv_w[0]],
         [jax.ShapeDtypeStruct((N_DEV, GW, D // N_DEV), BF16), rows_shape, rows_shape,
          jax.ShapeDtypeStruct((N_DEV, 3, D // N_DEV), F32)]))

    *attn, _ = _attention_forward(proj, rel_bias)
    w_ao_full = jnp.transpose(w_ao_g, (1, 0, 2)).reshape(GW, D)
    w_co_full = w_co_g.reshape(D, D)
    w_o_full = w_o_g.reshape(D, D)
    conv_w_full = jnp.transpose(conv_w_g, (1, 0, 2)).reshape(3, D)

    (dproj, dxd, gw_ao, gw_co, gw_o, conv_vec, g_rel_bias, tail_vec) = _local_step(
        x2, tgt2, mod3, h, proj, attn, w_ao_full, w_co_full, w_o_full,
        conv_w_full, conv_b, ln_g, ln_b)

    g_conv_w_blocks = jnp.transpose(conv_vec[0:3].reshape(3, N_DEV, D // N_DEV), (1, 0, 2))
    partials = [gw_ao, gw_co.reshape(N_DEV, D // N_DEV, D), gw_o.reshape(N_DEV, D // N_DEV, D), g_conv_w_blocks]
    w_in_sums, w_in_parts, sib = _gw_in_pair(
        _slice_order(), h, dproj, partials,
        [jax.ShapeDtypeStruct((4, GW, D // N_DEV), BF16),
         jax.ShapeDtypeStruct((4, D // N_DEV, D), BF16),
         jax.ShapeDtypeStruct((4, D // N_DEV, D), BF16),
         jax.ShapeDtypeStruct((4, 3, D // N_DEV), F32)])
    core = lax.axis_index("c").astype(jnp.int32).reshape(1)
    chip_sums = [w_in_sums] + list(_pair_add(core, partials, sib))
    hops = [(3,)] + [(1, 2, 3)] * 4
    grad_x, mod_vec, (r_in, r_ao, r_co, r_o, r_cw) = _dh_dx(
        dproj, w_in_all, x2, dxd, mod3, chip_sums, hops, w_in_parts)

    small = jnp.concatenate([
        tail_vec[0:4],
        jnp.pad(g_rel_bias.reshape(1, N_BUCKETS * N_HEADS), ((0, 0), (0, D - N_BUCKETS * N_HEADS))),
        jnp.zeros((3, D), F32)], axis=0)
    dmod = jnp.concatenate([mod_vec[0:2], mod_vec[2:4], tail_vec[4:6]], axis=1)
    small_g, dmod_g = _all_gather(
        [small, dmod],
        [jax.ShapeDtypeStruct((N_DEV, 8, D), F32), jax.ShapeDtypeStruct((N_DEV, BL, 3 * D), F32)],
        "gather_small")
    dmod_all = dmod_g.reshape(N_DEV * BL, 3 * D)
    small_names = ["b_ada", "conv_b", "rel_bias", "ln_g", "ln_b"]
    small_params = [(b_ada, m_b_ada, v_b_ada), (conv_b, m_conv_b, v_conv_b), (rel_bias, m_rel_bias, v_rel_bias),
                    (ln_g, m_ln_g, v_ln_g), (ln_b, m_ln_b, v_ln_b)]
    loss, small_res = _small_updates(
        small_g, dmod_all, small_g[:, 4, :N_BUCKETS * N_HEADS].reshape(N_DEV, N_BUCKETS, N_HEADS), small_params)
    loss = loss.reshape(())
    g_w_ada = _ada_bwd(jnp.transpose(c_all), lax.dynamic_slice(dmod_all, (0, me * ADA_SHARD),
                                                               (N_DEV * BL, ADA_SHARD)))

    def upd(parts, w, m, v, name, row_tile=None):
        shape = w.shape
        w2, m2, v2 = (t.reshape(parts.shape[1:]) for t in (w, m, v))
        return tuple(t.reshape(shape) for t in _adamw(parts, w2, m2, v2, name, row_tile))

    res = {
        "w_ada": upd(g_w_ada[None], w_ada, m_w_ada, v_w_ada, "adam_w_ada", 256),
        "w_in": upd(r_in, w_in, m_w_in, v_w_in, "adam_w_in", 128),
    }
    mid_names = ["conv_w", "w_attn_out", "w_conv_out", "w_o"]
    mid_parts = [r_cw, r_ao, r_co, r_o]
    mid_full = [(conv_w, m_conv_w, v_conv_w), (w_attn_out, m_w_attn_out, v_w_attn_out),
                (w_conv_out, m_w_conv_out, v_w_conv_out), (w_o, m_w_o, v_w_o)]
    mid_res = _multi_adamw(mid_parts, [tuple(t[0] for t in wmv) for wmv in mid_full], "adam_mid")
    for nm, wmv, outs4 in zip(mid_names, mid_full, mid_res):
        res[nm] = tuple(t[None] for t in outs4)
    res.update(dict(zip(small_names, small_res)))
    order = ["w_ada", "b_ada", "w_in", "conv_w", "conv_b", "rel_bias", "w_attn_out", "w_conv_out",
             "w_o", "ln_g", "ln_b"]
    outs = [loss, grad_x.reshape(BL, S, D)]
    for k in range(4):
        outs += [res[name][k] for name in order]
    return tuple(outs)
```

```python
import math

import numpy as np
import jax
import jax.numpy as jnp
from jax import lax
from jax.experimental import pallas as pl
from jax.experimental.pallas import tpu as pltpu

F32 = jnp.float32
BF16 = jnp.bfloat16
MESH = pl.DeviceIdType.MESH

N_DEV = 8
D = 1024
S = 2048
BL = 2
T = BL * S
NCOL = 11264
SHARD = NCOL // N_DEV
CB = 512
NCB = NCOL // CB
HD = 128
GW = 512
QB = 128
DILATIONS = (1, 4, 16)
N_STEPS = 128
N_BUCKETS = 32
N_HEADS = 12
ALPHA = 2.0 ** 0.25
LN_EPS = 1e-5
NEG_INF = -1e30
SCALE = HD ** -0.5
ADA_SHARD = 3 * D // N_DEV

CB_Q, CB_K, CB_V, CB_GA = 0, 3, 6, 9
KB_U, KB_BG, KB_CG, KB_GC, KB_MA, KB_MC = 5, 6, 7, 8, 9, 10

ADAM_LR, ADAM_B1, ADAM_B2, ADAM_EPS, ADAM_WD, ADAM_STEP = 0.001, 0.9, 0.999, 1e-08, 0.01, 10

VMEM_LIMIT = 56 * 1024 * 1024


def _dot(a, b):
    return jnp.dot(a, b, preferred_element_type=F32)


def _dot_nt(a, b):
    return lax.dot_general(a, b, (((1,), (1,)), ((), ())), preferred_element_type=F32)


def _dot_tn(a, b):
    return lax.dot_general(a, b, (((0,), (0,)), ((), ())), preferred_element_type=F32)


def _sigmoid(v):
    return 1.0 / (1.0 + jnp.exp(-v))


def _write_columns(pieces, dst_hbm, row0, sems):
    copies = []
    for k, (src, col0) in enumerate(pieces):
        rows, width = src.shape
        copies.append(pltpu.make_async_copy(
            src, dst_hbm.at[pl.ds(row0, rows), pl.ds(col0, width)], sems.at[k]))
    for cp in copies:
        cp.start()
    for cp in copies:
        cp.wait()


def _my_index():
    return 4 * lax.axis_index("x") + 2 * lax.axis_index("y") + lax.axis_index("c")


class _Gather:
    def __init__(self, ins, outs, stage, send_sems, recv_sems, local_sems):
        self.ins, self.outs, self.stage = ins, outs, stage
        self.send_sems, self.recv_sems, self.local_sems = send_sems, recv_sems, local_sems
        x, y, c = lax.axis_index("x"), lax.axis_index("y"), lax.axis_index("c")
        self.c = c
        self.me, self.sibling = (x, y, c), (x, y, 1 - c)
        self.chips = [(1 - x, y), (x, 1 - y), (1 - x, 1 - y)]

    @staticmethod
    def scratch(arrs):
        n = len(arrs)
        return ([pltpu.SemaphoreType.DMA((7 * n,)), pltpu.SemaphoreType.DMA((7 * n,)),
                 pltpu.SemaphoreType.DMA((n,))] + [pltpu.VMEM(a.shape, a.dtype) for a in arrs])

    def _copy(self, a, k, block, to, src=None):
        dst = self.outs[a].at[4 * block[0] + 2 * block[1] + block[2]]
        return pltpu.make_async_remote_copy(
            src_ref=dst if src is None else src, dst_ref=dst,
            send_sem=self.send_sems.at[a * 7 + k], recv_sem=self.recv_sems.at[a * 7 + k],
            device_id=to, device_id_type=MESH)

    def _first(self):
        first = []
        for a in range(len(self.ins)):
            first.append(self._copy(a, 0, self.me, self.sibling, src=self.ins[a]))
            first += [self._copy(a, 1 + j, self.me, (*chip, self.c), src=self.ins[a])
                      for j, chip in enumerate(self.chips)]
        return first

    def _mine(self):
        me = self.me
        return [pltpu.make_async_copy(self.stage[a], self.outs[a].at[4 * me[0] + 2 * me[1] + me[2]],
                                      self.local_sems.at[a]) for a in range(len(self.ins))]

    def begin(self):
        for cp in self._first():
            cp.start()
        loads = [pltpu.make_async_copy(self.ins[a], self.stage[a], self.local_sems.at[a])
                 for a in range(len(self.ins))]
        for cp in loads:
            cp.start()
        for cp in loads:
            cp.wait()
        for cp in self._mine():
            cp.start()

    def finish(self):
        n, c, me, sibling = len(self.ins), self.c, self.me, self.sibling
        passed = []
        for j, chip in enumerate(self.chips):
            for a in range(n):
                self._copy(a, 1 + j, (*chip, c), me).wait_recv()
                fwd = self._copy(a, 4 + j, (*chip, c), sibling)
                fwd.start()
                passed.append(fwd)
        for a in range(n):
            self._copy(a, 0, sibling, me).wait_recv()
        for j, chip in enumerate(self.chips):
            for a in range(n):
                self._copy(a, 4 + j, (*chip, 1 - c), me).wait_recv()
        for cp in self._first() + passed:
            cp.wait_send()
        for cp in self._mine():
            cp.wait()


def _all_gather(arrs, out_shapes, name):
    n = len(arrs)

    def body(*refs):
        g = _Gather(refs[:n], refs[n:2 * n], refs[2 * n + 3:], *refs[2 * n:2 * n + 3])
        g.begin()
        g.finish()

    any_spec = pl.BlockSpec(memory_space=pl.ANY)
    return pl.pallas_call(
        body, name=name,
        out_shape=tuple(out_shapes),
        in_specs=[any_spec] * n,
        out_specs=tuple([any_spec] * n),
        scratch_shapes=_Gather.scratch(arrs),
    )(*arrs)


def _slice_order():
    x, y, c = lax.axis_index("x"), lax.axis_index("y"), lax.axis_index("c")
    slots = []
    for q in (2 * (1 - x) + y, 2 * x + (1 - y), 2 * (1 - x) + (1 - y), 2 * x + y):
        slots += [2 * q + 1 - c, 2 * q + c]
    return jnp.stack(slots).astype(jnp.int32)


def _gw_in_pair(order, h, dproj, smalls, small_shapes4):
    kk, m = h.shape
    tk = min(kk, 2048)
    nk = kk // tk
    ncols = dproj.shape[1] // N_DEV
    n = len(smalls)

    def body(order_ref, h_ref, d_ref, *rest):
        ins = rest[:n]
        sums_hbm, parts_hbm = rest[n], rest[n + 1]
        sib = rest[n + 2:2 * n + 2]
        (acc, sendbuf, recvbuf, sumbuf, send_sems, recv_sems, local_sem, ssend, srecv,
         isend, irecv) = rest[2 * n + 2:]
        js, k = pl.program_id(0), pl.program_id(1)
        x, y, c = lax.axis_index("x"), lax.axis_index("y"), lax.axis_index("c")
        sibling = (x, y, 1 - c)
        my_chip = 2 * x + y
        near = [(1 - x, y, c), (x, 1 - y, c)]

        def ici_copy(p, out_chip):
            peer = near[p]
            return pltpu.make_async_remote_copy(
                src_ref=sumbuf.at[p], dst_ref=parts_hbm.at[out_chip],
                send_sem=isend.at[p], recv_sem=irecv.at[p], device_id=peer, device_id_type=MESH)

        def small_copies():
            return [pltpu.make_async_remote_copy(
                        src_ref=ins[a].at[2 * q + 1 - c], dst_ref=sib[a].at[q],
                        send_sem=ssend.at[a * 4 + q], recv_sem=srecv.at[a * 4 + q],
                        device_id=sibling, device_id_type=MESH)
                    for a in range(n) for q in range(4)]

        def slice_copy(p):
            return pltpu.make_async_remote_copy(
                src_ref=sendbuf, dst_ref=recvbuf.at[p], send_sem=send_sems.at[p], recv_sem=recv_sems.at[p],
                device_id=sibling, device_id_type=MESH)

        def sum_copy(p):
            return pltpu.make_async_copy(sumbuf.at[2], sums_hbm.at[order_ref[2 * p] // 2], local_sem)

        @pl.when((js == 0) & (k == 0))
        def _():
            for cp in small_copies():
                cp.start()

        @pl.when(k == 0)
        def _():
            acc[...] = jnp.zeros_like(acc)

        acc[...] += _dot_tn(h_ref[...], d_ref[...])

        for p in range(4):
            @pl.when((js == 2 * p) & (k == nk - 1))
            def _():
                if p > 0:
                    slice_copy(p - 1).wait_send()
                sendbuf[...] = acc[...].astype(BF16)
                slice_copy(p).start()

            @pl.when((js == 2 * p + 1) & (k == nk - 1))
            def _():
                slice_copy(p).wait_recv()
                if p == 3:
                    sum_copy(2).wait()
                sumbuf[min(p, 2)] = (acc[...] + recvbuf[p].astype(F32)).astype(BF16)
                if p < 2:
                    ici_copy(p, my_chip).start()
                else:
                    sum_copy(p).start()

        @pl.when((js == N_DEV - 1) & (k == nk - 1))
        def _():
            slice_copy(3).wait_send()
            sum_copy(3).wait()
            for cp in small_copies():
                cp.wait()
            for p in range(2):
                ici_copy(p, 2 * near[p][0] + near[p][1]).wait_recv()
                ici_copy(p, my_chip).wait_send()

    any_spec = pl.BlockSpec(memory_space=pl.ANY)
    res = pl.pallas_call(
        body, name="gw_in_pair",
        grid_spec=pltpu.PrefetchScalarGridSpec(
            num_scalar_prefetch=1,
            grid=(N_DEV, nk),
            in_specs=[pl.BlockSpec((tk, m), lambda js, k, order_ref: (k, 0)),
                      pl.BlockSpec((tk, ncols), lambda js, k, order_ref: (k, order_ref[js]))] + [any_spec] * n,
            out_specs=(any_spec,) * (n + 2),
            scratch_shapes=[pltpu.VMEM((m, ncols), F32), pltpu.VMEM((m, ncols), BF16),
                            pltpu.VMEM((4, m, ncols), BF16), pltpu.VMEM((3, m, ncols), BF16),
                            pltpu.SemaphoreType.DMA((4,)), pltpu.SemaphoreType.DMA((4,)),
                            pltpu.SemaphoreType.DMA,
                            pltpu.SemaphoreType.DMA((4 * n,)), pltpu.SemaphoreType.DMA((4 * n,)),
                            pltpu.SemaphoreType.DMA((2,)), pltpu.SemaphoreType.DMA((2,))]),
        out_shape=(jax.ShapeDtypeStruct((4, m, ncols), BF16),) * 2 + tuple(small_shapes4),
        compiler_params=pltpu.CompilerParams(vmem_limit_bytes=VMEM_LIMIT),
    )(order, h, dproj, *smalls)
    return res[0], res[1], res[2:]


def _chip_copies(ins, outs, send_sems, recv_sems, local_sems, hops):
    n = len(ins)
    x, y, c = lax.axis_index("x"), lax.axis_index("y"), lax.axis_index("c")
    my_chip = 2 * x + y

    def peer_of(k):
        return ((1 - x) if (k >> 1) & 1 else x, (1 - y) if k & 1 else y, c)

    def copy(a, k, out_chip):
        peer = peer_of(k)
        return pltpu.make_async_remote_copy(
            src_ref=ins[a].at[2 * peer[0] + peer[1]], dst_ref=outs[a].at[out_chip],
            send_sem=send_sems.at[a * 3 + k - 1], recv_sem=recv_sems.at[a * 3 + k - 1],
            device_id=peer, device_id_type=MESH)

    sends = [copy(a, k, my_chip) for k in range(1, 4) for a in range(n) if k in hops[a]]
    arrivals = []
    for k in range(1, 4):
        peer = peer_of(k)
        arrivals += [copy(a, k, 2 * peer[0] + peer[1]) for a in range(n) if k in hops[a]]
    mine = [pltpu.make_async_copy(ins[a].at[my_chip], outs[a].at[my_chip], local_sems.at[a])
            for a in range(n)]
    return sends, arrivals, mine


def _pair_add(core, mines, theirs):
    n = len(mines)

    def body(core_ref, *refs):
        mine, sib, outs = refs[:n], refs[n:2 * n], refs[2 * n:]
        for a in range(n):
            for q in range(4):
                outs[a][q] = (mine[a][2 * q + core_ref[0]].astype(F32)
                              + sib[a][q].astype(F32)).astype(outs[a].dtype)

    return pl.pallas_call(
        body, name="pair_add",
        in_specs=[pl.BlockSpec(memory_space=pltpu.SMEM)] + [pl.BlockSpec(memory_space=pltpu.VMEM)] * (2 * n),
        out_shape=tuple(jax.ShapeDtypeStruct(t.shape, t.dtype) for t in theirs),
    )(core, *mines, *theirs)


def _mod_exchange(c8, w_ada, b_cols):
    cols = w_ada.shape[1]

    def body(c_ref, w_ref, b_ref, call_ref, mod_ref, msend, send1, recv1, send2, recv2):
        x, y, c = lax.axis_index("x"), lax.axis_index("y"), lax.axis_index("c")
        my_slot = 4 * x + 2 * y + c

        def peer_of(k):
            return ((1 - x) if (k >> 2) & 1 else x, (1 - y) if (k >> 1) & 1 else y, (1 - c) if k & 1 else c)

        def slot_of(dev):
            return 4 * dev[0] + 2 * dev[1] + dev[2]

        def exchange(src_of, dst_ref, send_sems, recv_sems):
            sends, arrivals = [], []
            for k in range(1, 8):
                peer = peer_of(k)
                sends.append(pltpu.make_async_remote_copy(
                    src_ref=src_of(slot_of(peer)), dst_ref=dst_ref.at[my_slot],
                    send_sem=send_sems.at[k - 1], recv_sem=recv_sems.at[k - 1],
                    device_id=peer, device_id_type=MESH))
                arrivals.append(pltpu.make_async_remote_copy(
                    src_ref=src_of(my_slot), dst_ref=dst_ref.at[slot_of(peer)],
                    send_sem=send_sems.at[k - 1], recv_sem=recv_sems.at[k - 1],
                    device_id=peer, device_id_type=MESH))
            for cp in sends:
                cp.start()
            for cp in arrivals:
                cp.wait_recv()
            for cp in sends:
                cp.wait_send()

        call_ref[my_slot] = c_ref[...]
        exchange(lambda s: c_ref, call_ref, send1, recv1)
        cv = call_ref[...].reshape(N_DEV * 8, c_ref.shape[1])
        act = cv * _sigmoid(cv)
        mod = jnp.dot(act, w_ref[...], preferred_element_type=F32,
                      precision=lax.Precision.HIGHEST) + b_ref[...]
        msend[...] = mod.reshape(N_DEV, 8, cols)
        mod_ref[my_slot] = msend[my_slot]
        exchange(lambda s: msend.at[s], mod_ref, send2, recv2)

    return pl.pallas_call(
        body, name="mod_exchange",
        out_shape=(jax.ShapeDtypeStruct((N_DEV, 8, c8.shape[1]), F32),
                   jax.ShapeDtypeStruct((N_DEV, 8, cols), F32)),
        scratch_shapes=[pltpu.VMEM((N_DEV, 8, cols), F32)] + [pltpu.SemaphoreType.DMA((7,))] * 4,
    )(c8, w_ada, b_cols)


def _ada_bwd(c_all_t, dmod_cols):
    def body(c_ref, d_ref, o_ref):
        cv = c_ref[...]
        sc = cv * _sigmoid(cv)
        o_ref[...] = jnp.dot(sc, d_ref[...], preferred_element_type=F32,
                             precision=lax.Precision.HIGHEST)

    return pl.pallas_call(
        body, name="ada_bwd",
        out_shape=jax.ShapeDtypeStruct((c_all_t.shape[0], dmod_cols.shape[1]), F32),
    )(c_all_t, dmod_cols)


def _shard_order():
    x, y, c = lax.axis_index("x"), lax.axis_index("y"), lax.axis_index("c")
    devs = [(x, y, c), (x, y, 1 - c)]
    for chip in [(1 - x, y), (x, 1 - y), (1 - x, 1 - y)]:
        devs += [(*chip, c), (*chip, 1 - c)]
    return jnp.stack([4 * d[0] + 2 * d[1] + d[2] for d in devs]).astype(jnp.int32)


def _prep_h(x2, mod3):
    ts = 512
    per_seq = S // ts

    def body(x_ref, mod_ref, h_ref):
        shift = mod_ref[0, 0:1, :]
        scale = mod_ref[0, 1:2, :]
        h_ref[...] = (x_ref[...] * (1.0 + scale) + shift).astype(BF16)

    return pl.pallas_call(
        body, name="prep_h",
        grid=(T // ts,),
        in_specs=[pl.BlockSpec((ts, D), lambda i: (i, 0)),
                  pl.BlockSpec((1, 3, D), lambda i: (i // per_seq, 0, 0))],
        out_specs=pl.BlockSpec((ts, D), lambda i: (i, 0)),
        out_shape=jax.ShapeDtypeStruct((T, D), BF16),
    )(x2, mod3)


def _gather_proj(order, h, w_shard, tm, ride=()):
    rows, kdim = h.shape
    ncols = w_shard.shape[1]
    n_i = rows // tm
    ride_arrs, ride_shapes = ride if ride else ((), ())
    n_ride = len(ride_arrs)

    def body(order_ref, h_ref, mine_hbm, *rest):
        ride_ins = rest[:n_ride]
        o_ref, all_hbm = rest[n_ride:n_ride + 2]
        ride_outs = rest[n_ride + 2:2 * n_ride + 2]
        wv, send_sems, recv_sems, local_sems = rest[2 * n_ride + 2:2 * n_ride + 6]
        ride_scr = rest[2 * n_ride + 6:]
        j, i = pl.program_id(0), pl.program_id(1)
        x, y, c = lax.axis_index("x"), lax.axis_index("y"), lax.axis_index("c")
        me, sibling = (x, y, c), (x, y, 1 - c)
        chips = [(1 - x, y), (x, 1 - y), (1 - x, 1 - y)]

        def slot(dev):
            return 4 * dev[0] + 2 * dev[1] + dev[2]

        def copy(k, block, to):
            return pltpu.make_async_remote_copy(
                src_ref=wv.at[slot(block)], dst_ref=wv.at[slot(block)],
                send_sem=send_sems.at[k], recv_sem=recv_sems.at[k],
                device_id=to, device_id_type=MESH)

        def keep(step, block):
            return pltpu.make_async_copy(wv.at[slot(block)], all_hbm.at[slot(block)], local_sems.at[step])

        if n_ride:
            gather = _Gather(ride_ins, ride_outs, ride_scr[3:], *ride_scr[:3])
        first = [copy(0, me, sibling)] + [copy(1 + q, me, (*chip, c)) for q, chip in enumerate(chips)]
        passed = [copy(4 + q, (*chip, c), sibling) for q, chip in enumerate(chips)]
        due = [(me, None, None), (sibling, copy(0, sibling, me), None)]
        for q, chip in enumerate(chips):
            due.append(((*chip, c), copy(1 + q, (*chip, c), me), passed[q]))
            due.append(((*chip, 1 - c), copy(4 + q, (*chip, 1 - c), me), None))

        @pl.when((j == 0) & (i == 0))
        def _():
            load = pltpu.make_async_copy(mine_hbm, wv.at[slot(me)], local_sems.at[N_DEV])
            load.start()
            load.wait()
            for cp in first:
                cp.start()
            keep(0, me).start()

        for step in range(1, N_DEV):
            block, arrival, forward = due[step]

            @pl.when((j == step) & (i == 0))
            def _():
                arrival.wait_recv()
                if forward is not None:
                    forward.start()
                keep(step, block).start()
                if n_ride and step == N_DEV - 2:
                    gather.begin()

        o_ref[...] = _dot(h_ref[...], wv[order_ref[j]]).astype(BF16)

        @pl.when((j == N_DEV - 1) & (i == n_i - 1))
        def _():
            for cp in first + passed:
                cp.wait_send()
            for step in range(N_DEV):
                keep(step, due[step][0]).wait()
            if n_ride:
                gather.finish()

    any_spec = pl.BlockSpec(memory_space=pl.ANY)
    res = pl.pallas_call(
        body, name="gather_proj",
        grid_spec=pltpu.PrefetchScalarGridSpec(
            num_scalar_prefetch=1,
            grid=(N_DEV, n_i),
            in_specs=[pl.BlockSpec((tm, kdim), lambda j, i, order_ref: (i, 0)), any_spec] + [any_spec] * n_ride,
            out_specs=(pl.BlockSpec((tm, ncols), lambda j, i, order_ref: (i, order_ref[j])), any_spec)
                      + (any_spec,) * n_ride,
            scratch_shapes=[pltpu.VMEM((N_DEV, kdim, ncols), BF16),
                            pltpu.SemaphoreType.DMA((7,)), pltpu.SemaphoreType.DMA((7,)),
                            pltpu.SemaphoreType.DMA((N_DEV + 1,))]
                           + (_Gather.scratch(ride_arrs) if n_ride else [])),
        out_shape=(jax.ShapeDtypeStruct((rows, N_DEV * ncols), BF16),
                   jax.ShapeDtypeStruct((N_DEV, kdim, ncols), BF16)) + tuple(ride_shapes),
        compiler_params=pltpu.CompilerParams(vmem_limit_bytes=VMEM_LIMIT),
    )(order, h, w_shard, *ride_arrs)
    return res[0], res[1], res[2:]


def _bucket_maps():
    a = np.arange(QB)[:, None]
    b = np.arange(2 * QB)[None, :]
    steps = a + QB - b
    maps = []
    for dil in DILATIONS:
        dist = np.maximum(steps, 0) * dil
        nf = np.maximum(dist, 1).astype(np.float32)
        large = 16 + (np.log(nf / np.float32(16)) / np.float32(math.log(128.0))
                      * np.float32(16)).astype(np.int32)
        large = np.minimum(large, N_BUCKETS - 1)
        maps.append(np.where(dist < 16, dist, large).astype(np.int32))
    band = (steps >= 0) & (steps <= N_STEPS)
    first = band & (b >= QB)
    masks = np.stack([first, band]).astype(np.int32)
    return np.stack(maps), masks


def _bias_expand(rel_bias, buckets, masks):
    def body(tab_ref, bk_ref, mk_ref, o_ref):
        for g in range(3):
            bk = bk_ref[g]
            for h in range(4):
                col = 4 * g + h
                val = jnp.zeros((QB, 2 * QB), F32)
                for k in range(N_BUCKETS):
                    val = jnp.where(bk == k, tab_ref[k, col], val)
                o_ref[g, 0, h] = jnp.where(mk_ref[0] != 0, val, NEG_INF)
                o_ref[g, 1, h] = jnp.where(mk_ref[1] != 0, val, NEG_INF)

    return pl.pallas_call(
        body, name="bias_expand",
        in_specs=[pl.BlockSpec(memory_space=pltpu.SMEM),
                  pl.BlockSpec(memory_space=pltpu.VMEM),
                  pl.BlockSpec(memory_space=pltpu.VMEM)],
        out_shape=jax.ShapeDtypeStruct((3, 2, 4, QB, 2 * QB), F32),
    )(rel_bias, buckets, masks)


def _bias_grad(ds1, ds2, ds3, buckets):
    def body(d1_ref, d2_ref, d3_ref, bk_ref, o_ref):
        for g, d_ref in enumerate((d1_ref, d2_ref, d3_ref)):
            bk = bk_ref[g]
            for h in range(4):
                dv = d_ref[h]
                for k in range(N_BUCKETS):
                    o_ref[k, 4 * g + h] = jnp.sum(jnp.where(bk == k, dv, 0.0))

    return pl.pallas_call(
        body, name="bias_grad",
        in_specs=[pl.BlockSpec(memory_space=pltpu.VMEM)] * 4,
        out_specs=pl.BlockSpec(memory_space=pltpu.SMEM),
        out_shape=jax.ShapeDtypeStruct((N_BUCKETS, N_HEADS), F32),
    )(ds1, ds2, ds3, buckets)


def _scratch_sets(rows):
    return 4 if rows <= 512 else 1


def _unit_chunks(dil, size=16):
    units = [(h, r) for h in range(4) for r in range(dil)]
    return [units[i:i + size] for i in range(0, len(units), size)]


def _residue_rows(src_ref, copies, h, residue):
    sl = slice(h * HD, (h + 1) * HD)
    if copies is None:
        return lambda r: src_ref[:, sl]
    buf = copies[h % len(copies)]
    buf[...] = src_ref[:, sl].astype(F32)
    return lambda r: buf[residue(r), :].astype(BF16)


def _attn_fwd(proj, bias, g):
    dil = DILATIONS[g]
    rows = QB * dil
    nsb = S // rows
    has_prev = nsb > 1

    def residue(r):
        return pl.ds(r, QB, stride=dil) if dil > 1 else pl.ds(0, QB)

    strided = dil > 1
    n_sets = _scratch_sets(rows)
    n_in = 6 if has_prev else 4
    n_copied = (4 + (2 if has_prev else 0)) * (n_sets if strided else 0)

    def body(*refs):
        q_ref, kc_ref, vc_ref = refs[:3]
        kp_ref, vp_ref = refs[3:5] if has_prev else (None, None)
        b_ref = refs[n_in - 1]
        o_ref, l_ref = refs[n_in:n_in + 2]
        scr = list(refs[n_in + 2:])
        ls = [scr.pop(0) for _ in range(4)]
        copies = {name: [scr.pop(0) for _ in range(n_sets)] if strided else None
                  for name in ("q", "kc", "vc", "o") + (("kp", "vp") if has_prev else ())}
        lane = lax.broadcasted_iota(jnp.int32, (QB, 128), 1)
        refs_of = {"q": q_ref, "kc": kc_ref, "vc": vc_ref, "kp": kp_ref, "vp": vp_ref}
        for chunk in _unit_chunks(dil):
            rows_of = {h: {name: _residue_rows(refs_of[name], copies[name], h, residue)
                           for name in refs_of if refs_of[name] is not None}
                       for h in sorted({h for h, _ in chunk})}

            def batch(name):
                return jnp.stack([rows_of[h][name](r) for h, r in chunk])

            q, k, v = batch("q"), batch("kc"), batch("vc")
            if has_prev:
                k = jnp.concatenate([batch("kp"), k], axis=1)
                v = jnp.concatenate([batch("vp"), v], axis=1)
                bias_b = jnp.stack([b_ref[h] for h, _ in chunk])
            else:
                bias_b = jnp.stack([b_ref[h, :, QB:] for h, _ in chunk])
            s = jnp.einsum("uqd,ukd->uqk", q, k, preferred_element_type=F32) * SCALE + bias_b
            m = jnp.max(s, axis=-1, keepdims=True)
            p = jnp.exp(s - m)
            l = jnp.sum(p, axis=-1, keepdims=True)
            o = jnp.einsum("uqk,ukd->uqd", p.astype(BF16), v, preferred_element_type=F32) / l
            lse = m + jnp.log(l)
            for i, (h, r) in enumerate(chunk):
                if strided:
                    copies["o"][h % n_sets][residue(r), :] = o[i]
                else:
                    o_ref[:, h * HD:(h + 1) * HD] = o[i]
                ls[h][r * QB:(r + 1) * QB, :] = jnp.where(lane == h, lse[i], 0.0)
            if strided:
                for h in sorted({h for h, _ in chunk}):
                    o_ref[:, h * HD:(h + 1) * HD] = copies["o"][h % n_sets][...]
        for r in range(dil):
            blk = slice(r * QB, (r + 1) * QB)
            l_ref[residue(r), :] = (ls[0][blk, :] + ls[1][blk, :]) + (ls[2][blk, :] + ls[3][blk, :])

    def row(b, n):
        return b * nsb + n

    def prev(b, n):
        return b * nsb + jnp.maximum(n - 1, 0)

    in_specs = [
        pl.BlockSpec((rows, GW), lambda b, n: (row(b, n), CB_Q + g)),
        pl.BlockSpec((rows, GW), lambda b, n: (row(b, n), CB_K + g)),
        pl.BlockSpec((rows, GW), lambda b, n: (row(b, n), CB_V + g)),
    ]
    args = [proj, proj, proj]
    scratch = [pltpu.VMEM((rows, 128), F32)] * (4 + n_copied)
    if has_prev:
        in_specs += [pl.BlockSpec((rows, GW), lambda b, n: (prev(b, n), CB_K + g)),
                     pl.BlockSpec((rows, GW), lambda b, n: (prev(b, n), CB_V + g))]
        args += [proj, proj]
    in_specs.append(pl.BlockSpec((None, None, 4, QB, 2 * QB),
                                 lambda b, n: (g, jnp.minimum(n, 1), 0, 0, 0)))
    args.append(bias)
    return pl.pallas_call(
        body, name=f"attn_fwd{g}",
        grid=(BL, nsb),
        in_specs=in_specs,
        out_specs=(pl.BlockSpec((rows, GW), lambda b, n: (row(b, n), 0)),
                   pl.BlockSpec((rows, 128), lambda b, n: (row(b, n), 0))),
        out_shape=(jax.ShapeDtypeStruct((T, GW), F32), jax.ShapeDtypeStruct((T, 128), F32)),
        scratch_shapes=scratch,
        compiler_params=pltpu.CompilerParams(vmem_limit_bytes=VMEM_LIMIT),
    )(*args)


def _attn_bwd(proj, d_out, stats, bias, dproj, g):
    dil = DILATIONS[g]
    rows = QB * dil
    nsb = S // rows
    has_prev = nsb > 1
    n_steps = nsb + 1 if has_prev else 1
    n_in = 7 + (2 if has_prev else 0)

    def residue(r):
        return pl.ds(r, QB, stride=dil) if dil > 1 else pl.ds(0, QB)

    strided = dil > 1
    n_sets = _scratch_sets(rows)

    def body(*refs):
        q_ref, kc_ref, vc_ref, do_ref, st_ref, b_ref = refs[:6]
        kp_ref, vp_ref = refs[6:8] if has_prev else (None, None)
        out_ref, db_ref = refs[n_in], refs[n_in + 1]
        scr = list(refs[n_in + 2:])
        sq, sk, sv, sems = [scr.pop(0) for _ in range(4)]
        carry = scr.pop(0) if has_prev else None
        sts = scr.pop(0) if strided else st_ref
        copies = {name: [scr.pop(0) for _ in range(n_sets)] if strided else None
                  for name in ("q", "kc", "vc", "do", "dq", "dk", "dv") + (("kp", "vp") if has_prev else ())}
        b, n = pl.program_id(0), pl.program_id(1)

        @pl.when((b == 0) & (n == 0))
        def _():
            db_ref[...] = jnp.zeros_like(db_ref)

        def finish(h, r, dq, dk, dv):
            if strided:
                for name, val in (("dq", dq), ("dk", dk), ("dv", dv)):
                    copies[name][h % n_sets][residue(r), :] = val
            else:
                sl = slice(h * HD, (h + 1) * HD)
                sq[:, sl], sk[:, sl], sv[:, sl] = dq.astype(BF16), dk.astype(BF16), dv.astype(BF16)

        def finish_head(h):
            if strided:
                sl = slice(h * HD, (h + 1) * HD)
                sq[:, sl] = copies["dq"][h % n_sets][...].astype(BF16)
                sk[:, sl] = copies["dk"][h % n_sets][...].astype(BF16)
                sv[:, sl] = copies["dv"][h % n_sets][...].astype(BF16)

        def write_block(blk_idx):
            row0 = pl.multiple_of(blk_idx * rows, rows)
            _write_columns([(sq, CB * (CB_Q + g)), (sk, CB * (CB_K + g)), (sv, CB * (CB_V + g))],
                           out_ref, row0, sems)

        def carried(h, r):
            blk = slice(r * QB, (r + 1) * QB)
            return ((blk, slice(h * HD, (h + 1) * HD)), (blk, slice(GW + h * HD, GW + (h + 1) * HD)),
                    (blk, slice(2 * GW + h * HD, 2 * GW + (h + 1) * HD)))

        if has_prev:
            @pl.when(n == 0)
            def _():
                carry[...] = jnp.zeros_like(carry)

            @pl.when(n == nsb)
            def _():
                for h in range(4):
                    for r in range(dil):
                        cq, ck, cv = carried(h, r)
                        finish(h, r, carry[cq], carry[ck], carry[cv])
                    finish_head(h)
                write_block(b * nsb + nsb - 1)

        @pl.when(n < nsb)
        def _():
            if strided:
                for r in range(dil):
                    sts[r * QB:(r + 1) * QB, :] = st_ref[residue(r), :]
            refs_of = {"q": q_ref, "kc": kc_ref, "vc": vc_ref, "do": do_ref, "kp": kp_ref, "vp": vp_ref}
            for chunk in _unit_chunks(dil):
                heads = sorted({h for h, _ in chunk})
                rows_of = {h: {name: _residue_rows(refs_of[name], copies[name], h, residue)
                               for name in refs_of if refs_of[name] is not None}
                           for h in heads}

                def batch(name):
                    return jnp.stack([rows_of[h][name](r) for h, r in chunk])

                q, k, v, do = batch("q"), batch("kc"), batch("vc"), batch("do")
                if has_prev:
                    k = jnp.concatenate([batch("kp"), k], axis=1)
                    v = jnp.concatenate([batch("vp"), v], axis=1)
                    bias_b = jnp.stack([b_ref[h] for h, _ in chunk])
                else:
                    bias_b = jnp.stack([b_ref[h, :, QB:] for h, _ in chunk])
                lse = jnp.stack([sts[r * QB:(r + 1) * QB, h:h + 1] for h, r in chunk])
                delta = jnp.stack([sts[r * QB:(r + 1) * QB, 4 + h:5 + h] for h, r in chunk])
                s = jnp.einsum("uqd,ukd->uqk", q, k, preferred_element_type=F32) * SCALE + bias_b
                p = jnp.exp(s - lse)
                ds = p * (jnp.einsum("uqd,ukd->uqk", do, v, preferred_element_type=F32) - delta)
                for h in heads:
                    mine = [ds[i] for i, (hh, _) in enumerate(chunk) if hh == h]
                    tot = mine[0]
                    for extra in mine[1:]:
                        tot = tot + extra
                    if has_prev:
                        db_ref[h] += tot
                    else:
                        db_ref[h, :, QB:] += tot
                dsb, pb = ds.astype(BF16), p.astype(BF16)
                dq = jnp.einsum("uqk,ukd->uqd", dsb, k, preferred_element_type=F32) * SCALE
                dk = jnp.einsum("uqk,uqd->ukd", dsb, q, preferred_element_type=F32) * SCALE
                dv = jnp.einsum("uqk,uqd->ukd", pb, do, preferred_element_type=F32)
                for i, (h, r) in enumerate(chunk):
                    if has_prev:
                        cq, ck, cv = carried(h, r)
                        finish(h, r, carry[cq], carry[ck] + dk[i, :QB], carry[cv] + dv[i, :QB])
                        carry[cq] = dq[i]
                        carry[ck] = dk[i, QB:]
                        carry[cv] = dv[i, QB:]
                    else:
                        finish(h, r, dq[i], dk[i], dv[i])
                for h in heads:
                    finish_head(h)
            if has_prev:
                @pl.when(n > 0)
                def _():
                    write_block(b * nsb + n - 1)
            else:
                write_block(b)

    def row(b, n):
        return b * nsb + jnp.minimum(n, nsb - 1)

    def prev(b, n):
        return b * nsb + jnp.maximum(jnp.minimum(n, nsb - 1) - 1, 0)

    in_specs = [
        pl.BlockSpec((rows, GW), lambda b, n: (row(b, n), CB_Q + g)),
        pl.BlockSpec((rows, GW), lambda b, n: (row(b, n), CB_K + g)),
        pl.BlockSpec((rows, GW), lambda b, n: (row(b, n), CB_V + g)),
        pl.BlockSpec((rows, GW), lambda b, n: (row(b, n), 0)),
        pl.BlockSpec((rows, 128), lambda b, n: (row(b, n), 0)),
        pl.BlockSpec((None, None, 4, QB, 2 * QB),
                     lambda b, n: (g, jnp.minimum(jnp.minimum(n, nsb - 1), 1), 0, 0, 0)),
    ]
    args = [proj, proj, proj, d_out, stats, bias]
    scratch = [pltpu.VMEM((rows, GW), BF16)] * 3 + [pltpu.SemaphoreType.DMA((3,))]
    if has_prev:
        in_specs += [pl.BlockSpec((rows, GW), lambda b, n: (prev(b, n), CB_K + g)),
                     pl.BlockSpec((rows, GW), lambda b, n: (prev(b, n), CB_V + g))]
        args += [proj, proj]
        scratch.append(pltpu.VMEM((rows, 3 * GW), F32))
    if strided:
        n_copied = (7 + (2 if has_prev else 0)) * n_sets
        scratch += [pltpu.VMEM((rows, 128), F32)] * (1 + n_copied)
    in_specs.append(pl.BlockSpec(memory_space=pl.ANY))
    args.append(dproj)
    return pl.pallas_call(
        body, name=f"attn_bwd{g}",
        grid=(BL, n_steps),
        in_specs=in_specs,
        out_specs=(pl.BlockSpec(memory_space=pl.ANY),
                   pl.BlockSpec((4, QB, 2 * QB), lambda b, n: (0, 0, 0))),
        out_shape=(jax.ShapeDtypeStruct((T, NCOL), BF16),
                   jax.ShapeDtypeStruct((4, QB, 2 * QB), F32)),
        scratch_shapes=scratch,
        input_output_aliases={len(args) - 1: 0},
        compiler_params=pltpu.CompilerParams(vmem_limit_bytes=VMEM_LIMIT),
    )(*args)


def _tail(x2, tgt2, mod3, o_g, lse_g, proj, w_ao, w_co, w_o, conv_w, conv_b, ln_g, ln_b):
    tm = 256
    per_seq = S // tm
    halo = 16

    def body(x_ref, t_ref, mod_ref, o1_ref, o2_ref, o3_ref, l1_ref, l2_ref, l3_ref,
             ga_ref, u_ref, bg_ref, cg_ref, gc_ref, ma_ref, mc_ref, up_ref, cp_ref,
             wao_ref, wco_ref, wo_ref, cw_ref, cb_ref, lg_ref, lb_ref,
             dproj_ref, dyc_ref, do_ref, st_ref, dxd_ref,
             mg_ref, dy_ref, ain_ref, dao_ref, sin_ref, dso_ref, vec_ref,
             dga_s, dbg_s, dgm_s, sems):
        i = pl.program_id(0)
        bidx = i // per_seq
        first = (i % per_seq) == 0

        @pl.when(i == 0)
        def _():
            vec_ref[...] = jnp.zeros_like(vec_ref)

        l1, l2, l3 = l1_ref[...], l2_ref[...], l3_ref[...]
        mx = jnp.maximum(jnp.maximum(l1, l2), l3)
        e1, e2, e3 = jnp.exp(l1 - mx), jnp.exp(l2 - mx), jnp.exp(l3 - mx)
        esum = e1 + e2 + e3
        lse_tot = mx + jnp.log(esum)
        w1, w2, w3 = e1 / esum, e2 / esum, e3 / esum

        def per_head(wv):
            return jnp.concatenate([jnp.broadcast_to(wv[:, h:h + 1], (tm, HD)) for h in range(4)], axis=1)

        o = per_head(w1) * o1_ref[...] + per_head(w2) * o2_ref[...] + per_head(w3) * o3_ref[...]

        ga = ga_ref[...].astype(F32)
        sig_ga = _sigmoid(ga)
        silu_ga = ga * sig_ga
        a_in = (o * silu_ga).astype(BF16)
        a_out = _dot(a_in, wao_ref[...])

        u = u_ref[...].astype(F32)
        cg = cg_ref[...].astype(F32)
        z = cg * u
        zp = cp_ref[...].astype(F32) * up_ref[...].astype(F32)
        zp = jnp.where(first, 0.0, zp)
        zcat = jnp.concatenate([zp, z], axis=0)
        z1 = pltpu.roll(zcat, 1, 0)[halo:]
        z2 = pltpu.roll(zcat, 2, 0)[halo:]
        y_conv = cw_ref[0:1, :] * z2 + cw_ref[1:2, :] * z1 + cw_ref[2:3, :] * z + cb_ref[...]
        gc = gc_ref[...].astype(F32)
        sig_gc = _sigmoid(gc)
        silu_gc = gc * sig_gc
        bg = bg_ref[...].astype(F32)
        bg_yc = bg * y_conv
        s_in = (bg_yc * silu_gc).astype(BF16)
        s_out = _dot(s_in, wco_ref[...])

        sa = _sigmoid(ma_ref[...].astype(F32))
        sc = _sigmoid(mc_ref[...].astype(F32))
        merged = (sa * a_out + sc * s_out).astype(BF16)
        y = _dot(merged, wo_ref[...])
        gate1 = 1.0 + mod_ref[0, 2:3, :]
        xv = x_ref[...]
        resid = ALPHA * xv + gate1 * y
        mu = jnp.mean(resid, axis=1, keepdims=True)
        xc = resid - mu
        var = jnp.mean(xc * xc, axis=1, keepdims=True)
        rstd = lax.rsqrt(var + LN_EPS)
        xhat = xc * rstd
        lg = lg_ref[...]
        err = xhat * lg + lb_ref[...] - t_ref[...]
        vec_ref[3:4, :] += (0.5 / D) * jnp.sum(err * err, axis=0, keepdims=True)

        vec_ref[1:2, :] += (1.0 / D) * jnp.sum(err * xhat, axis=0, keepdims=True)
        vec_ref[2:3, :] += (1.0 / D) * jnp.sum(err, axis=0, keepdims=True)
        dxh = err * (lg * (1.0 / D))
        dres = rstd * (dxh - jnp.mean(dxh, axis=1, keepdims=True)
                       - xhat * jnp.mean(dxh * xhat, axis=1, keepdims=True))
        dxd_ref[...] = ALPHA * dres
        dgate = jnp.sum(dres * y, axis=0, keepdims=True)
        vec_ref[4:5, :] += jnp.where(bidx == 0, dgate, 0.0)
        vec_ref[5:6, :] += jnp.where(bidx == 1, dgate, 0.0)
        dy = (dres * gate1).astype(BF16)

        dmerged = _dot_nt(dy, wo_ref[...])
        da_out_f = dmerged * sa
        ds_out_f = dmerged * sc
        da_out = da_out_f.astype(BF16)
        ds_out = ds_out_f.astype(BF16)
        dgm_s[:, 2 * D:3 * D] = (ds_out_f * s_out * (1.0 - sc)).astype(BF16)
        dgm_s[:, D:2 * D] = (da_out_f * a_out * (1.0 - sa)).astype(BF16)
        da_in = _dot_nt(da_out, wao_ref[...])
        ds_in = _dot_nt(ds_out, wco_ref[...])

        d_o = da_in * silu_ga
        do_ref[...] = d_o.astype(BF16)
        dga_s[...] = (da_in * o * (sig_ga + silu_ga * (1.0 - sig_ga))).astype(BF16)
        lane = lax.broadcasted_iota(jnp.int32, (tm, 128), 1)
        stats = lse_tot
        od = o * d_o
        for h in range(4):
            delta = jnp.sum(od[:, h * HD:(h + 1) * HD], axis=1, keepdims=True)
            stats = jnp.where(lane == 4 + h, delta, stats)
        st_ref[...] = stats

        ds_silu = ds_in * silu_gc
        dbg_s[...] = (ds_silu * y_conv).astype(BF16)
        dyc = ds_silu * bg
        dyc_ref[...] = dyc
        vec_ref[0:1, :] += jnp.sum(dyc, axis=0, keepdims=True)
        dgm_s[:, 0:D] = (ds_in * bg_yc * (sig_gc + silu_gc * (1.0 - sig_gc))).astype(BF16)

        mg_ref[...] = merged
        dy_ref[...] = dy
        ain_ref[...] = a_in
        dao_ref[...] = da_out
        sin_ref[...] = s_in
        dso_ref[...] = ds_out
        _write_columns([(dga_s, CB * CB_GA), (dbg_s, D * KB_BG), (dgm_s, D * KB_GC)],
                       dproj_ref, pl.multiple_of(i * tm, tm), sems)

    def tile(width, cblk=0):
        return pl.BlockSpec((tm, width), lambda i: (i, cblk))

    def whole(shape):
        return pl.BlockSpec(shape, lambda i: tuple(0 for _ in shape))

    prev_rows = lambda i: (jnp.maximum(i * (tm // halo) - 1, 0),)
    in_specs = [
        tile(D), tile(D), pl.BlockSpec((1, 3, D), lambda i: (i // per_seq, 0, 0)),
        tile(GW), tile(GW), tile(GW), tile(128), tile(128), tile(128),
        tile(GW, CB_GA), tile(D, KB_U), tile(D, KB_BG), tile(D, KB_CG), tile(D, KB_GC),
        tile(D, KB_MA), tile(D, KB_MC),
        pl.BlockSpec((halo, D), lambda i: (*prev_rows(i), KB_U)),
        pl.BlockSpec((halo, D), lambda i: (*prev_rows(i), KB_CG)),
        whole((GW, D)), whole((D, D)), whole((D, D)),
        whole((3, D)), whole((1, D)), whole((1, D)), whole((1, D)),
    ]
    out_specs = (
        pl.BlockSpec(memory_space=pl.ANY), tile(D), tile(GW), tile(128), tile(D),
        tile(D), tile(D), tile(GW), tile(D), tile(D), tile(D),
        pl.BlockSpec((8, D), lambda i: (0, 0)),
    )
    out_shape = (
        jax.ShapeDtypeStruct((T, NCOL), BF16),
        jax.ShapeDtypeStruct((T, D), F32),
        jax.ShapeDtypeStruct((T, GW), BF16),
        jax.ShapeDtypeStruct((T, 128), F32),
        jax.ShapeDtypeStruct((T, D), F32),
        jax.ShapeDtypeStruct((T, D), BF16),
        jax.ShapeDtypeStruct((T, D), BF16),
        jax.ShapeDtypeStruct((T, GW), BF16),
        jax.ShapeDtypeStruct((T, D), BF16),
        jax.ShapeDtypeStruct((T, D), BF16),
        jax.ShapeDtypeStruct((T, D), BF16),
        jax.ShapeDtypeStruct((8, D), F32),
    )
    return pl.pallas_call(
        body, name="tail",
        grid=(T // tm,),
        in_specs=in_specs, out_specs=out_specs, out_shape=out_shape,
        scratch_shapes=[pltpu.VMEM((tm, GW), BF16), pltpu.VMEM((tm, D), BF16), pltpu.VMEM((tm, 3 * D), BF16),
                        pltpu.SemaphoreType.DMA((3,))],
        compiler_params=pltpu.CompilerParams(vmem_limit_bytes=VMEM_LIMIT),
    )(x2, tgt2, mod3, *o_g, *lse_g, proj, proj, proj, proj, proj, proj, proj, proj, proj,
      w_ao, w_co, w_o, conv_w, conv_b, ln_g, ln_b)


def _conv_bwd(dyc, proj, conv_w, dproj):
    tm = 512
    per_seq = S // tm
    halo = 16

    def body(d_ref, dn_ref, u_ref, c_ref, up_ref, cp_ref, cw_ref, _, dproj_ref, g_ref, du_s, dc_s, sems):
        i = pl.program_id(0)
        first = (i % per_seq) == 0
        last = (i % per_seq) == per_seq - 1

        @pl.when(i == 0)
        def _():
            g_ref[...] = jnp.zeros_like(g_ref)

        d = d_ref[...]
        dn = jnp.where(last, 0.0, dn_ref[...])
        dcat = jnp.concatenate([d, dn], axis=0)
        d1 = pltpu.roll(dcat, tm + 8 - 1, 0)[:tm]
        d2 = pltpu.roll(dcat, tm + 8 - 2, 0)[:tm]
        dz = cw_ref[2:3, :] * d + cw_ref[1:2, :] * d1 + cw_ref[0:1, :] * d2
        u = u_ref[...].astype(F32)
        cg = c_ref[...].astype(F32)
        du_s[...] = (dz * cg).astype(BF16)
        dc_s[...] = (dz * u).astype(BF16)
        _write_columns([(du_s, D * KB_U), (dc_s, D * KB_CG)], dproj_ref, pl.multiple_of(i * tm, tm), sems)

        z = cg * u
        zp = jnp.where(first, 0.0, cp_ref[...].astype(F32) * up_ref[...].astype(F32))
        zcat = jnp.concatenate([zp, z], axis=0)
        z1 = pltpu.roll(zcat, 1, 0)[halo:]
        z2 = pltpu.roll(zcat, 2, 0)[halo:]
        g_ref[0:1, :] += jnp.sum(d * z2, axis=0, keepdims=True)
        g_ref[1:2, :] += jnp.sum(d * z1, axis=0, keepdims=True)
        g_ref[2:3, :] += jnp.sum(d * z, axis=0, keepdims=True)

    n_tiles = T // tm
    prev_rows = lambda i: jnp.maximum(i * (tm // halo) - 1, 0)
    next_rows = lambda i: jnp.minimum((i + 1) * (tm // 8), T // 8 - 1)
    return pl.pallas_call(
        body, name="conv_bwd",
        grid=(n_tiles,),
        in_specs=[pl.BlockSpec((tm, D), lambda i: (i, 0)),
                  pl.BlockSpec((8, D), lambda i: (next_rows(i), 0)),
                  pl.BlockSpec((tm, D), lambda i: (i, KB_U)),
                  pl.BlockSpec((tm, D), lambda i: (i, KB_CG)),
                  pl.BlockSpec((halo, D), lambda i: (prev_rows(i), KB_U)),
                  pl.BlockSpec((halo, D), lambda i: (prev_rows(i), KB_CG)),
                  pl.BlockSpec((3, D), lambda i: (0, 0)),
                  pl.BlockSpec(memory_space=pl.ANY)],
        out_specs=(pl.BlockSpec(memory_space=pl.ANY),
                   pl.BlockSpec((8, D), lambda i: (0, 0))),
        out_shape=(jax.ShapeDtypeStruct((T, NCOL), BF16),
                   jax.ShapeDtypeStruct((8, D), F32)),
        scratch_shapes=[pltpu.VMEM((tm, D), BF16), pltpu.VMEM((tm, D), BF16), pltpu.SemaphoreType.DMA((2,))],
        input_output_aliases={7: 0},
        compiler_params=pltpu.CompilerParams(vmem_limit_bytes=VMEM_LIMIT),
    )(dyc, dyc, proj, proj, proj, proj, conv_w, dproj)


def _dh_dx(dproj, w_in_all, x2, dxd, mod3, chip_sums, hops=(), parts0=None):
    tm = 1024
    per_seq = S // tm
    n = len(chip_sums)
    n_in = 5 + n + (0 if parts0 is None else 1)

    def body(*refs):
        d_ref, w_ref, x_ref, dxd_ref, mod_ref = refs[:5]
        ins = refs[5:5 + n]
        gx_ref, vec_ref = refs[n_in:n_in + 2]
        outs = refs[n_in + 2:n_in + 2 + n]
        acc, send_sems, recv_sems, local_sems = refs[n_in + 2 + n:]
        i, jj = pl.program_id(0), pl.program_id(1)

        @pl.when((i == 0) & (jj == 0))
        def _():
            vec_ref[...] = jnp.zeros_like(vec_ref)
            if n:
                sends, _, mine = _chip_copies(ins, outs, send_sems, recv_sems, local_sems, hops)
                for cp in sends + mine:
                    cp.start()

        if n:
            @pl.when((i == T // tm - 1) & (jj == N_DEV - 1))
            def _():
                sends, arrivals, mine = _chip_copies(ins, outs, send_sems, recv_sems, local_sems, hops)
                for cp in arrivals:
                    cp.wait_recv()
                for cp in sends:
                    cp.wait_send()
                for cp in mine:
                    cp.wait()

        @pl.when(jj == 0)
        def _():
            acc[...] = jnp.zeros_like(acc)

        acc[...] += _dot_nt(d_ref[...], w_ref[...])

        @pl.when(jj == N_DEV - 1)
        def _():
            dh = acc[...]
            bidx = i // per_seq
            gx_ref[...] = dxd_ref[...] + dh * (1.0 + mod_ref[0, 1:2, :])
            dshift = jnp.sum(dh, axis=0, keepdims=True)
            dscale = jnp.sum(dh * x_ref[...], axis=0, keepdims=True)
            vec_ref[0:1, :] += jnp.where(bidx == 0, dshift, 0.0)
            vec_ref[1:2, :] += jnp.where(bidx == 1, dshift, 0.0)
            vec_ref[2:3, :] += jnp.where(bidx == 0, dscale, 0.0)
            vec_ref[3:4, :] += jnp.where(bidx == 1, dscale, 0.0)

    any_spec = pl.BlockSpec(memory_space=pl.ANY)
    res = pl.pallas_call(
        body, name="dh_dx",
        grid=(T // tm, N_DEV),
        in_specs=[
            pl.BlockSpec((tm, SHARD), lambda i, jj: (i, jj)),
            pl.BlockSpec((None, D, SHARD), lambda i, jj: (jj, 0, 0)),
            pl.BlockSpec((tm, D), lambda i, jj: (i, 0)),
            pl.BlockSpec((tm, D), lambda i, jj: (i, 0)),
            pl.BlockSpec((1, 3, D), lambda i, jj: (i // per_seq, 0, 0))] + [any_spec] * (n_in - 5),
        out_specs=(pl.BlockSpec((tm, D), lambda i, jj: (i, 0)),
                   pl.BlockSpec((8, D), lambda i, jj: (0, 0))) + (any_spec,) * n,
        out_shape=(jax.ShapeDtypeStruct((T, D), F32), jax.ShapeDtypeStruct((8, D), F32))
                  + tuple(jax.ShapeDtypeStruct(a.shape, a.dtype) for a in chip_sums),
        scratch_shapes=[pltpu.VMEM((tm, D), F32), pltpu.SemaphoreType.DMA((max(3 * n, 1),)),
                        pltpu.SemaphoreType.DMA((max(3 * n, 1),)), pltpu.SemaphoreType.DMA((max(n, 1),))],
        input_output_aliases={} if parts0 is None else {5 + n: 2},
        compiler_params=pltpu.CompilerParams(vmem_limit_bytes=VMEM_LIMIT),
    )(dproj, w_in_all, x2, dxd, mod3, *chip_sums, *([] if parts0 is None else [parts0]))
    return res[0], res[1], res[2:]


def _mm_tn(a, b, tn, blocks_leading, name):
    kk, m = a.shape
    n = b.shape[1]
    tk = 2048

    def body(a_ref, b_ref, o_ref, acc):
        @pl.when(pl.program_id(1) == 0)
        def _():
            acc[...] = jnp.zeros_like(acc)

        acc[...] += _dot_tn(a_ref[...], b_ref[...])

        @pl.when(pl.program_id(1) == kk // tk - 1)
        def _():
            o_ref[...] = acc[...].astype(BF16)

    if blocks_leading:
        out_spec = pl.BlockSpec((None, m, tn), lambda j, k: (j, 0, 0))
        out_shape = jax.ShapeDtypeStruct((n // tn, m, tn), BF16)
    else:
        out_spec = pl.BlockSpec((m, tn), lambda j, k: (0, j))
        out_shape = jax.ShapeDtypeStruct((m, n), BF16)
    return pl.pallas_call(
        body, name=name,
        grid=(n // tn, kk // tk),
        in_specs=[pl.BlockSpec((tk, m), lambda j, k: (k, 0)),
                  pl.BlockSpec((tk, tn), lambda j, k: (k, j))],
        out_specs=out_spec, out_shape=out_shape,
        scratch_shapes=[pltpu.VMEM((m, tn), F32)],
        compiler_params=pltpu.CompilerParams(vmem_limit_bytes=VMEM_LIMIT),
    )(a, b)


def _adam_step(g, w, m, v):
    nm = ADAM_B1 * m + (1.0 - ADAM_B1) * g
    nv = ADAM_B2 * v + (1.0 - ADAM_B2) * (g * g)
    m_hat = nm / (1.0 - ADAM_B1 ** ADAM_STEP)
    v_hat = nv / (1.0 - ADAM_B2 ** ADAM_STEP)
    return -ADAM_LR * (m_hat / (jnp.sqrt(v_hat) + ADAM_EPS) + ADAM_WD * w), nm, nv


def _adamw(parts, w, m, v, name, row_tile=None):
    n_parts, rows, cols = parts.shape
    tr = rows if row_tile is None else row_tile

    def body(p_ref, w_ref, m_ref, v_ref, g_ref, d_ref, nm_ref, nv_ref):
        g = p_ref[0].astype(F32)
        for s in range(1, n_parts):
            g = g + p_ref[s].astype(F32)
        g_ref[...] = g
        d_ref[...], nm_ref[...], nv_ref[...] = _adam_step(g, w_ref[...], m_ref[...], v_ref[...])

    blk = pl.BlockSpec((tr, cols), lambda i: (i, 0))
    shp = jax.ShapeDtypeStruct((rows, cols), F32)
    return pl.pallas_call(
        body, name=name,
        grid=(rows // tr,),
        in_specs=[pl.BlockSpec((n_parts, tr, cols), lambda i: (0, i, 0)), blk, blk, blk],
        out_specs=(blk, blk, blk, blk),
        out_shape=(shp, shp, shp, shp),
        compiler_params=pltpu.CompilerParams(vmem_limit_bytes=VMEM_LIMIT),
    )(parts, w, m, v)


def _multi_adamw(parts_list, params, name):
    n = len(params)
    flat = [t for wmv in params for t in wmv]

    def body(*refs):
        parts, ins, outs = refs[:n], refs[n:4 * n], refs[4 * n:]
        for p in range(n):
            g = parts[p][0].astype(F32)
            for s in range(1, parts[p].shape[0]):
                g = g + parts[p][s].astype(F32)
            w_ref, m_ref, v_ref = ins[3 * p:3 * p + 3]
            g_ref, d_ref, nm_ref, nv_ref = outs[4 * p:4 * p + 4]
            g_ref[...] = g
            d_ref[...], nm_ref[...], nv_ref[...] = _adam_step(g, w_ref[...], m_ref[...], v_ref[...])

    out_shape = []
    for w, _, _ in params:
        out_shape += [jax.ShapeDtypeStruct(w.shape, F32)] * 4
    res = pl.pallas_call(body, name=name, out_shape=tuple(out_shape))(*parts_list, *flat)
    return [res[4 * p:4 * p + 4] for p in range(n)]


def _small_updates(small_g, dmod_all, rel_parts, params):
    flat = [t for wmv in params for t in wmv]

    def body(sg_ref, dm_ref, rp_ref, *refs):
        ins, outs = refs[:len(flat)], refs[len(flat):]

        def over_devices(row):
            tot = sg_ref[0, row:row + 1, :]
            for s in range(1, N_DEV):
                tot = tot + sg_ref[s, row:row + 1, :]
            return tot

        g_b_ada = dm_ref[0:1, :]
        for r in range(1, N_DEV * BL):
            g_b_ada = g_b_ada + dm_ref[r:r + 1, :]
        g_rel = rp_ref[0]
        for s in range(1, N_DEV):
            g_rel = g_rel + rp_ref[s]
        grads = [g_b_ada, over_devices(0), g_rel, over_devices(1), over_devices(2)]
        outs[0][...] = jnp.sum(over_devices(3), axis=1, keepdims=True)
        for p, g in enumerate(grads):
            w_ref, m_ref, v_ref = ins[3 * p:3 * p + 3]
            g_ref, d_ref, nm_ref, nv_ref = outs[1 + 4 * p:5 + 4 * p]
            g_ref[...] = g
            d_ref[...], nm_ref[...], nv_ref[...] = _adam_step(g, w_ref[...], m_ref[...], v_ref[...])

    out_shape = [jax.ShapeDtypeStruct((1, 1), F32)]
    for w, _, _ in params:
        out_shape += [jax.ShapeDtypeStruct(w.shape, F32)] * 4
    res = pl.pallas_call(body, name="small_updates", out_shape=tuple(out_shape))(small_g, dmod_all, rel_parts, *flat)
    return res[0], [res[1 + 4 * p:5 + 4 * p] for p in range(len(params))]


def _attn_fwd_dense(proj, bias):
    nq = 4
    rows = nq * QB
    nsb = S // rows

    def body(q_ref, k_ref, v_ref, kp_ref, vp_ref, b_ref, o_ref, l_ref, ls0, ls1, ls2, ls3):
        ls = [ls0, ls1, ls2, ls3]
        n = pl.program_id(1)
        lane = lax.broadcasted_iota(jnp.int32, (QB, 128), 1)
        units = [(h, j) for h in range(4) for j in range(nq)]

        def keys(cur_ref, prev_ref, h, j):
            sl = slice(h * HD, (h + 1) * HD)
            if j == 0:
                return jnp.concatenate([prev_ref[:, sl], cur_ref[0:QB, sl]], axis=0)
            return cur_ref[(j - 1) * QB:(j + 1) * QB, sl]

        q = jnp.stack([q_ref[j * QB:(j + 1) * QB, h * HD:(h + 1) * HD] for h, j in units])
        k = jnp.stack([keys(k_ref, kp_ref, h, j) for h, j in units])
        v = jnp.stack([keys(v_ref, vp_ref, h, j) for h, j in units])
        bias_b = jnp.stack([b_ref[jnp.minimum(n, 1), h] if j == 0 else b_ref[1, h] for h, j in units])
        s = jnp.einsum("uqd,ukd->uqk", q, k, preferred_element_type=F32) * SCALE + bias_b
        m = jnp.max(s, axis=-1, keepdims=True)
        p = jnp.exp(s - m)
        l = jnp.sum(p, axis=-1, keepdims=True)
        o = jnp.einsum("uqk,ukd->uqd", p.astype(BF16), v, preferred_element_type=F32) / l
        lse = m + jnp.log(l)
        for i, (h, j) in enumerate(units):
            o_ref[j * QB:(j + 1) * QB, h * HD:(h + 1) * HD] = o[i]
            ls[h][j * QB:(j + 1) * QB, :] = jnp.where(lane == h, lse[i], 0.0)
        l_ref[...] = (ls[0][...] + ls[1][...]) + (ls[2][...] + ls[3][...])

    def row(b, n):
        return b * nsb + n

    def prev(b, n):
        return jnp.maximum((b * nsb + n) * nq - 1, 0)

    in_specs = [
        pl.BlockSpec((rows, GW), lambda b, n: (row(b, n), CB_Q)),
        pl.BlockSpec((rows, GW), lambda b, n: (row(b, n), CB_K)),
        pl.BlockSpec((rows, GW), lambda b, n: (row(b, n), CB_V)),
        pl.BlockSpec((QB, GW), lambda b, n: (prev(b, n), CB_K)),
        pl.BlockSpec((QB, GW), lambda b, n: (prev(b, n), CB_V)),
        pl.BlockSpec((None, 2, 4, QB, 2 * QB), lambda b, n: (0, 0, 0, 0, 0)),
    ]
    return pl.pallas_call(
        body, name="attn_fwd0",
        grid=(BL, nsb),
        in_specs=in_specs,
        out_specs=(pl.BlockSpec((rows, GW), lambda b, n: (row(b, n), 0)),
                   pl.BlockSpec((rows, 128), lambda b, n: (row(b, n), 0))),
        out_shape=(jax.ShapeDtypeStruct((T, GW), F32), jax.ShapeDtypeStruct((T, 128), F32)),
        scratch_shapes=[pltpu.VMEM((rows, 128), F32)] * 4,
        compiler_params=pltpu.CompilerParams(vmem_limit_bytes=VMEM_LIMIT),
    )(proj, proj, proj, proj, proj, bias)


def _attention_forward(proj, rel_bias):
    buckets_np, masks_np = _bucket_maps()
    buckets, masks = jnp.asarray(buckets_np), jnp.asarray(masks_np)
    bias = _bias_expand(rel_bias, buckets, masks)
    fwd = [_attn_fwd_dense(proj, bias)] + [_attn_fwd(proj, bias, g) for g in (1, 2)]
    return bias, buckets, [f[0] for f in fwd], [f[1] for f in fwd]


def _local_step(x2, tgt2, mod3, h, proj, attn, w_ao, w_co, w_o, conv_w, conv_b, ln_g, ln_b):
    bias, buckets, o_g, lse_g = attn

    (dproj, dyc, d_o, stats, dxd, merged, dy, a_in, da_out, s_in, ds_out, tail_vec) = _tail(
        x2, tgt2, mod3, o_g, lse_g, proj, w_ao, w_co, w_o, conv_w, conv_b, ln_g, ln_b)

    dbias = []
    for g in range(3):
        dproj, db = _attn_bwd(proj, d_o, stats, bias, dproj, g)
        dbias.append(db)
    g_rel_bias = _bias_grad(*dbias, buckets)
    dproj, conv_vec = _conv_bwd(dyc, proj, conv_w, dproj)

    gw_o = _mm_tn(merged, dy, D, False, "gw_o")
    gw_co = _mm_tn(s_in, ds_out, D, False, "gw_conv_out")
    gw_ao = _mm_tn(a_in, da_out, D, False, "gw_attn_out")
    gw_ao = jnp.transpose(gw_ao.reshape(GW, N_DEV, D // N_DEV), (1, 0, 2))
    return dproj, dxd, gw_ao, gw_co, gw_o, conv_vec, g_rel_bias, tail_vec


def kernel(x, c, w_ada, b_ada, w_in, conv_w, conv_b, rel_bias, w_attn_out, w_conv_out, w_o, ln_g, ln_b, loss_target, m_w_ada, m_b_ada, m_w_in, m_conv_w, m_conv_b, m_rel_bias, m_w_attn_out, m_w_conv_out, m_w_o, m_ln_g, m_ln_b, v_w_ada, v_b_ada, v_w_in, v_conv_w, v_conv_b, v_rel_bias, v_w_attn_out, v_w_conv_out, v_w_o, v_ln_g, v_ln_b):
    me = _my_index()
    x2 = x.reshape(T, D)
    tgt2 = loss_target.reshape(T, D)

    b_cols = lax.dynamic_slice(b_ada, (0, me * ADA_SHARD), (1, ADA_SHARD))
    c_g, mod_in = _mod_exchange(jnp.pad(c, ((0, 8 - BL), (0, 0))), w_ada[0], b_cols)
    c_all = c_g[:, 0:BL, :].reshape(N_DEV * BL, D)
    mod3 = jnp.transpose(mod_in[:, 0:BL, :], (1, 0, 2)).reshape(BL, 3, D)

    h = _prep_h(x2, mod3)
    rows_shape = jax.ShapeDtypeStruct((N_DEV, D // N_DEV, D), BF16)
    proj, w_in_all, (w_ao_g, w_co_g, w_o_g, conv_w_g) = _gather_proj(
        _shard_order(), h, w_in[0].astype(BF16), 1024,
        ([w_attn_out[0].astype(BF16), w_conv_out[0].astype(BF16), w_o[0].astype(BF16), conv_w[0]],
         [jax.ShapeDtypeStruct((N_DEV, GW, D // N_DEV), BF16), rows_shape, rows_shape,
          jax.ShapeDtypeStruct((N_DEV, 3, D // N_DEV), F32)]))

    attn = _attention_forward(proj, rel_bias)
    w_ao_full = jnp.transpose(w_ao_g, (1, 0, 2)).reshape(GW, D)
    w_co_full = w_co_g.reshape(D, D)
    w_o_full = w_o_g.reshape(D, D)
    conv_w_full = jnp.transpose(conv_w_g, (1, 0, 2)).reshape(3, D)

    (dproj, dxd, gw_ao, gw_co, gw_o, conv_vec, g_rel_bias, tail_vec) = _local_step(
        x2, tgt2, mod3, h, proj, attn, w_ao_full, w_co_full, w_o_full,
        conv_w_full, conv_b, ln_g, ln_b)

    g_conv_w_blocks = jnp.transpose(conv_vec[0:3].reshape(3, N_DEV, D // N_DEV), (1, 0, 2))
    partials = [gw_ao, gw_co.reshape(N_DEV, D // N_DEV, D), gw_o.reshape(N_DEV, D // N_DEV, D), g_conv_w_blocks]
    w_in_sums, w_in_parts, sib = _gw_in_pair(
        _slice_order(), h, dproj, partials,
        [jax.ShapeDtypeStruct((4, GW, D // N_DEV), BF16),
         jax.ShapeDtypeStruct((4, D // N_DEV, D), BF16),
         jax.ShapeDtypeStruct((4, D // N_DEV, D), BF16),
         jax.ShapeDtypeStruct((4, 3, D // N_DEV), F32)])
    core = lax.axis_index("c").astype(jnp.int32).reshape(1)
    chip_sums = [w_in_sums] + list(_pair_add(core, partials, sib))
    hops = [(3,)] + [(1, 2, 3)] * 4
    grad_x, mod_vec, (r_in, r_ao, r_co, r_o, r_cw) = _dh_dx(
        dproj, w_in_all, x2, dxd, mod3, chip_sums, hops, w_in_parts)

    small = jnp.concatenate([
        tail_vec[0:4],
        jnp.pad(g_rel_bias.reshape(1, N_BUCKETS * N_HEADS), ((0, 0), (0, D - N_BUCKETS * N_HEADS))),
        jnp.zeros((3, D), F32)], axis=0)
    dmod = jnp.concatenate([mod_vec[0:2], mod_vec[2:4], tail_vec[4:6]], axis=1)
    small_g, dmod_g = _all_gather(
        [small, dmod],
        [jax.ShapeDtypeStruct((N_DEV, 8, D), F32), jax.ShapeDtypeStruct((N_DEV, BL, 3 * D), F32)],
        "gather_small")
    dmod_all = dmod_g.reshape(N_DEV * BL, 3 * D)
    small_names = ["b_ada", "conv_b", "rel_bias", "ln_g", "ln_b"]
    small_params = [(b_ada, m_b_ada, v_b_ada), (conv_b, m_conv_b, v_conv_b), (rel_bias, m_rel_bias, v_rel_bias),
                    (ln_g, m_ln_g, v_ln_g), (ln_b, m_ln_b, v_ln_b)]
    loss, small_res = _small_updates(
        small_g, dmod_all, small_g[:, 4, :N_BUCKETS * N_HEADS].reshape(N_DEV, N_BUCKETS, N_HEADS), small_params)
    loss = loss.reshape(())
    g_w_ada = _ada_bwd(jnp.transpose(c_all), lax.dynamic_slice(dmod_all, (0, me * ADA_SHARD),
                                                               (N_DEV * BL, ADA_SHARD)))

    def upd(parts, w, m, v, name, row_tile=None):
        shape = w.shape
        w2, m2, v2 = (t.reshape(parts.shape[1:]) for t in (w, m, v))
        return tuple(t.reshape(shape) for t in _adamw(parts, w2, m2, v2, name, row_tile))

    res = {
        "w_ada": upd(g_w_ada[None], w_ada, m_w_ada, v_w_ada, "adam_w_ada", 256),
        "w_in": upd(r_in, w_in, m_w_in, v_w_in, "adam_w_in", 128),
    }
    mid_names = ["conv_w", "w_attn_out", "w_conv_out", "w_o"]
    mid_parts = [r_cw, r_ao, r_co, r_o]
    mid_full = [(conv_w, m_conv_w, v_conv_w), (w_attn_out, m_w_attn_out, v_w_attn_out),
                (w_conv_out, m_w_conv_out, v_w_conv_out), (w_o, m_w_o, v_w_o)]
    mid_res = _multi_adamw(mid_parts, [tuple(t[0] for t in wmv) for wmv in mid_full], "adam_mid")
    for nm, wmv, outs4 in zip(mid_names, mid_full, mid_res):
        res[nm] = tuple(t[None] for t in outs4)
    res.update(dict(zip(small_names, small_res)))
    order = ["w_ada", "b_ada", "w_in", "conv_w", "conv_b", "rel_bias", "w_attn_out", "w_conv_out",
             "w_o", "ln_g", "ln_b"]
    outs = [loss, grad_x.reshape(BL, S, D)]
    for k in range(4):
        outs += [res[name][k] for name in order]
    return tuple(outs)
```

```python
import math

import numpy as np
import jax
import jax.numpy as jnp
from jax import lax
from jax.experimental import pallas as pl
from jax.experimental.pallas import tpu as pltpu

F32 = jnp.float32
BF16 = jnp.bfloat16
MESH = pl.DeviceIdType.MESH

N_DEV = 8
D = 1024
S = 2048
BL = 2
T = BL * S
NCOL = 11264
SHARD = NCOL // N_DEV
CB = 512
NCB = NCOL // CB
HD = 128
GW = 512
QB = 128
DILATIONS = (1, 4, 16)
N_STEPS = 128
N_BUCKETS = 32
N_HEADS = 12
ALPHA = 2.0 ** 0.25
LN_EPS = 1e-5
NEG_INF = -1e30
SCALE = HD ** -0.5
ADA_SHARD = 3 * D // N_DEV

CB_Q, CB_K, CB_V, CB_GA = 0, 3, 6, 9
KB_U, KB_BG, KB_CG, KB_GC, KB_MA, KB_MC = 5, 6, 7, 8, 9, 10

ADAM_LR, ADAM_B1, ADAM_B2, ADAM_EPS, ADAM_WD, ADAM_STEP = 0.001, 0.9, 0.999, 1e-08, 0.01, 10

VMEM_LIMIT = 56 * 1024 * 1024


def _dot(a, b):
    return jnp.dot(a, b, preferred_element_type=F32)


def _dot_nt(a, b):
    return lax.dot_general(a, b, (((1,), (1,)), ((), ())), preferred_element_type=F32)


def _dot_tn(a, b):
    return lax.dot_general(a, b, (((0,), (0,)), ((), ())), preferred_element_type=F32)


def _sigmoid(v):
    return 1.0 / (1.0 + jnp.exp(-v))


def _write_columns(pieces, dst_hbm, row0, sems):
    copies = []
    for k, (src, col0) in enumerate(pieces):
        rows, width = src.shape
        copies.append(pltpu.make_async_copy(
            src, dst_hbm.at[pl.ds(row0, rows), pl.ds(col0, width)], sems.at[k]))
    for cp in copies:
        cp.start()
    for cp in copies:
        cp.wait()


def _my_index():
    return 4 * lax.axis_index("x") + 2 * lax.axis_index("y") + lax.axis_index("c")


class _Gather:
    def __init__(self, ins, outs, stage, send_sems, recv_sems, local_sems):
        self.ins, self.outs, self.stage = ins, outs, stage
        self.send_sems, self.recv_sems, self.local_sems = send_sems, recv_sems, local_sems
        x, y, c = lax.axis_index("x"), lax.axis_index("y"), lax.axis_index("c")
        self.c = c
        self.me, self.sibling = (x, y, c), (x, y, 1 - c)
        self.chips = [(1 - x, y), (x, 1 - y), (1 - x, 1 - y)]

    @staticmethod
    def scratch(arrs):
        n = len(arrs)
        return ([pltpu.SemaphoreType.DMA((7 * n,)), pltpu.SemaphoreType.DMA((7 * n,)),
                 pltpu.SemaphoreType.DMA((n,))] + [pltpu.VMEM(a.shape, a.dtype) for a in arrs])

    def _copy(self, a, k, block, to, src=None):
        dst = self.outs[a].at[4 * block[0] + 2 * block[1] + block[2]]
        return pltpu.make_async_remote_copy(
            src_ref=dst if src is None else src, dst_ref=dst,
            send_sem=self.send_sems.at[a * 7 + k], recv_sem=self.recv_sems.at[a * 7 + k],
            device_id=to, device_id_type=MESH)

    def _first(self):
        first = []
        for a in range(len(self.ins)):
            first.append(self._copy(a, 0, self.me, self.sibling, src=self.ins[a]))
            first += [self._copy(a, 1 + j, self.me, (*chip, self.c), src=self.ins[a])
                      for j, chip in enumerate(self.chips)]
        return first

    def _mine(self):
        me = self.me
        return [pltpu.make_async_copy(self.stage[a], self.outs[a].at[4 * me[0] + 2 * me[1] + me[2]],
                                      self.local_sems.at[a]) for a in range(len(self.ins))]

    def begin(self):
        for cp in self._first():
            cp.start()
        loads = [pltpu.make_async_copy(self.ins[a], self.stage[a], self.local_sems.at[a])
                 for a in range(len(self.ins))]
        for cp in loads:
            cp.start()
        for cp in loads:
            cp.wait()
        for cp in self._mine():
            cp.start()

    def finish(self):
        n, c, me, sibling = len(self.ins), self.c, self.me, self.sibling
        passed = []
        for j, chip in enumerate(self.chips):
            for a in range(n):
                self._copy(a, 1 + j, (*chip, c), me).wait_recv()
                fwd = self._copy(a, 4 + j, (*chip, c), sibling)
                fwd.start()
                passed.append(fwd)
        for a in range(n):
            self._copy(a, 0, sibling, me).wait_recv()
        for j, chip in enumerate(self.chips):
            for a in range(n):
                self._copy(a, 4 + j, (*chip, 1 - c), me).wait_recv()
        for cp in self._first() + passed:
            cp.wait_send()
        for cp in self._mine():
            cp.wait()


def _all_gather(arrs, out_shapes, name):
    n = len(arrs)

    def body(*refs):
        g = _Gather(refs[:n], refs[n:2 * n], refs[2 * n + 3:], *refs[2 * n:2 * n + 3])
        g.begin()
        g.finish()

    any_spec = pl.BlockSpec(memory_space=pl.ANY)
    return pl.pallas_call(
        body, name=name,
        out_shape=tuple(out_shapes),
        in_specs=[any_spec] * n,
        out_specs=tuple([any_spec] * n),
        scratch_shapes=_Gather.scratch(arrs),
    )(*arrs)


def _slice_order():
    x, y, c = lax.axis_index("x"), lax.axis_index("y"), lax.axis_index("c")
    slots = []
    for q in (2 * (1 - x) + y, 2 * x + (1 - y), 2 * (1 - x) + (1 - y), 2 * x + y):
        slots += [2 * q + 1 - c, 2 * q + c]
    return jnp.stack(slots).astype(jnp.int32)


def _gw_in_pair(order, h, dproj, smalls, small_shapes4):
    kk, m = h.shape
    tk = min(kk, 2048)
    nk = kk // tk
    ncols = dproj.shape[1] // N_DEV
    n = len(smalls)

    def body(order_ref, h_ref, d_ref, *rest):
        ins = rest[:n]
        sums_hbm, parts_hbm = rest[n], rest[n + 1]
        sib = rest[n + 2:2 * n + 2]
        (acc, sendbuf, recvbuf, sumbuf, send_sems, recv_sems, local_sem, ssend, srecv,
         isend, irecv) = rest[2 * n + 2:]
        js, k = pl.program_id(0), pl.program_id(1)
        x, y, c = lax.axis_index("x"), lax.axis_index("y"), lax.axis_index("c")
        sibling = (x, y, 1 - c)
        my_chip = 2 * x + y
        near = [(1 - x, y, c), (x, 1 - y, c)]

        def ici_copy(p, out_chip):
            peer = near[p]
            return pltpu.make_async_remote_copy(
                src_ref=sumbuf.at[p], dst_ref=parts_hbm.at[out_chip],
                send_sem=isend.at[p], recv_sem=irecv.at[p], device_id=peer, device_id_type=MESH)

        def small_copies():
            return [pltpu.make_async_remote_copy(
                        src_ref=ins[a].at[2 * q + 1 - c], dst_ref=sib[a].at[q],
                        send_sem=ssend.at[a * 4 + q], recv_sem=srecv.at[a * 4 + q],
                        device_id=sibling, device_id_type=MESH)
                    for a in range(n) for q in range(4)]

        def slice_copy(p):
            return pltpu.make_async_remote_copy(
                src_ref=sendbuf, dst_ref=recvbuf.at[p], send_sem=send_sems.at[p], recv_sem=recv_sems.at[p],
                device_id=sibling, device_id_type=MESH)

        def sum_copy(p):
            return pltpu.make_async_copy(sumbuf.at[2], sums_hbm.at[order_ref[2 * p] // 2], local_sem)

        @pl.when((js == 0) & (k == 0))
        def _():
            for cp in small_copies():
                cp.start()

        @pl.when(k == 0)
        def _():
            acc[...] = jnp.zeros_like(acc)

        acc[...] += _dot_tn(h_ref[...], d_ref[...])

        for p in range(4):
            @pl.when((js == 2 * p) & (k == nk - 1))
            def _():
                if p > 0:
                    slice_copy(p - 1).wait_send()
                sendbuf[...] = acc[...].astype(BF16)
                slice_copy(p).start()

            @pl.when((js == 2 * p + 1) & (k == nk - 1))
            def _():
                slice_copy(p).wait_recv()
                if p == 3:
                    sum_copy(2).wait()
                sumbuf[min(p, 2)] = (acc[...] + recvbuf[p].astype(F32)).astype(BF16)
                if p < 2:
                    ici_copy(p, my_chip).start()
                else:
                    sum_copy(p).start()

        @pl.when((js == N_DEV - 1) & (k == nk - 1))
        def _():
            slice_copy(3).wait_send()
            sum_copy(3).wait()
            for cp in small_copies():
                cp.wait()
            for p in range(2):
                ici_copy(p, 2 * near[p][0] + near[p][1]).wait_recv()
                ici_copy(p, my_chip).wait_send()

    any_spec = pl.BlockSpec(memory_space=pl.ANY)
    res = pl.pallas_call(
        body, name="gw_in_pair",
        grid_spec=pltpu.PrefetchScalarGridSpec(
            num_scalar_prefetch=1,
            grid=(N_DEV, nk),
            in_specs=[pl.BlockSpec((tk, m), lambda js, k, order_ref: (k, 0)),
                      pl.BlockSpec((tk, ncols), lambda js, k, order_ref: (k, order_ref[js]))] + [any_spec] * n,
            out_specs=(any_spec,) * (n + 2),
            scratch_shapes=[pltpu.VMEM((m, ncols), F32), pltpu.VMEM((m, ncols), BF16),
                            pltpu.VMEM((4, m, ncols), BF16), pltpu.VMEM((3, m, ncols), BF16),
                            pltpu.SemaphoreType.DMA((4,)), pltpu.SemaphoreType.DMA((4,)),
                            pltpu.SemaphoreType.DMA,
                            pltpu.SemaphoreType.DMA((4 * n,)), pltpu.SemaphoreType.DMA((4 * n,)),
                            pltpu.SemaphoreType.DMA((2,)), pltpu.SemaphoreType.DMA((2,))]),
        out_shape=(jax.ShapeDtypeStruct((4, m, ncols), BF16),) * 2 + tuple(small_shapes4),
        compiler_params=pltpu.CompilerParams(vmem_limit_bytes=VMEM_LIMIT),
    )(order, h, dproj, *smalls)
    return res[0], res[1], res[2:]


def _chip_copies(ins, outs, send_sems, recv_sems, local_sems, hops):
    n = len(ins)
    x, y, c = lax.axis_index("x"), lax.axis_index("y"), lax.axis_index("c")
    my_chip = 2 * x + y

    def peer_of(k):
        return ((1 - x) if (k >> 1) & 1 else x, (1 - y) if k & 1 else y, c)

    def copy(a, k, out_chip):
        peer = peer_of(k)
        return pltpu.make_async_remote_copy(
            src_ref=ins[a].at[2 * peer[0] + peer[1]], dst_ref=outs[a].at[out_chip],
            send_sem=send_sems.at[a * 3 + k - 1], recv_sem=recv_sems.at[a * 3 + k - 1],
            device_id=peer, device_id_type=MESH)

    sends = [copy(a, k, my_chip) for k in range(1, 4) for a in range(n) if k in hops[a]]
    arrivals = []
    for k in range(1, 4):
        peer = peer_of(k)
        arrivals += [copy(a, k, 2 * peer[0] + peer[1]) for a in range(n) if k in hops[a]]
    mine = [pltpu.make_async_copy(ins[a].at[my_chip], outs[a].at[my_chip], local_sems.at[a])
            for a in range(n)]
    return sends, arrivals, mine


def _pair_add(core, mines, theirs):
    n = len(mines)

    def body(core_ref, *refs):
        mine, sib, outs = refs[:n], refs[n:2 * n], refs[2 * n:]
        for a in range(n):
            for q in range(4):
                outs[a][q] = (mine[a][2 * q + core_ref[0]].astype(F32)
                              + sib[a][q].astype(F32)).astype(outs[a].dtype)

    return pl.pallas_call(
        body, name="pair_add",
        in_specs=[pl.BlockSpec(memory_space=pltpu.SMEM)] + [pl.BlockSpec(memory_space=pltpu.VMEM)] * (2 * n),
        out_shape=tuple(jax.ShapeDtypeStruct(t.shape, t.dtype) for t in theirs),
    )(core, *mines, *theirs)


def _mod_exchange(c8, w_ada, b_cols):
    cols = w_ada.shape[1]

    def body(c_ref, w_ref, b_ref, call_ref, mod_ref, msend, send1, recv1, send2, recv2):
        x, y, c = lax.axis_index("x"), lax.axis_index("y"), lax.axis_index("c")
        my_slot = 4 * x + 2 * y + c

        def peer_of(k):
            return ((1 - x) if (k >> 2) & 1 else x, (1 - y) if (k >> 1) & 1 else y, (1 - c) if k & 1 else c)

        def slot_of(dev):
            return 4 * dev[0] + 2 * dev[1] + dev[2]

        def exchange(src_of, dst_ref, send_sems, recv_sems):
            sends, arrivals = [], []
            for k in range(1, 8):
                peer = peer_of(k)
                sends.append(pltpu.make_async_remote_copy(
                    src_ref=src_of(slot_of(peer)), dst_ref=dst_ref.at[my_slot],
                    send_sem=send_sems.at[k - 1], recv_sem=recv_sems.at[k - 1],
                    device_id=peer, device_id_type=MESH))
                arrivals.append(pltpu.make_async_remote_copy(
                    src_ref=src_of(my_slot), dst_ref=dst_ref.at[slot_of(peer)],
                    send_sem=send_sems.at[k - 1], recv_sem=recv_sems.at[k - 1],
                    device_id=peer, device_id_type=MESH))
            for cp in sends:
                cp.start()
            for cp in arrivals:
                cp.wait_recv()
            for cp in sends:
                cp.wait_send()

        call_ref[my_slot] = c_ref[...]
        exchange(lambda s: c_ref, call_ref, send1, recv1)
        cv = call_ref[...].reshape(N_DEV * 8, c_ref.shape[1])
        act = cv * _sigmoid(cv)
        mod = jnp.dot(act, w_ref[...], preferred_element_type=F32,
                      precision=lax.Precision.HIGHEST) + b_ref[...]
        msend[...] = mod.reshape(N_DEV, 8, cols)
        mod_ref[my_slot] = msend[my_slot]
        exchange(lambda s: msend.at[s], mod_ref, send2, recv2)

    return pl.pallas_call(
        body, name="mod_exchange",
        out_shape=(jax.ShapeDtypeStruct((N_DEV, 8, c8.shape[1]), F32),
                   jax.ShapeDtypeStruct((N_DEV, 8, cols), F32)),
        scratch_shapes=[pltpu.VMEM((N_DEV, 8, cols), F32)] + [pltpu.SemaphoreType.DMA((7,))] * 4,
    )(c8, w_ada, b_cols)


def _ada_bwd(c_all_t, dmod_cols):
    def body(c_ref, d_ref, o_ref):
        cv = c_ref[...]
        sc = cv * _sigmoid(cv)
        o_ref[...] = jnp.dot(sc, d_ref[...], preferred_element_type=F32,
                             precision=lax.Precision.HIGHEST)

    return pl.pallas_call(
        body, name="ada_bwd",
        out_shape=jax.ShapeDtypeStruct((c_all_t.shape[0], dmod_cols.shape[1]), F32),
    )(c_all_t, dmod_cols)


def _shard_order():
    x, y, c = lax.axis_index("x"), lax.axis_index("y"), lax.axis_index("c")
    devs = [(x, y, c), (x, y, 1 - c)]
    for chip in [(1 - x, y), (x, 1 - y), (1 - x, 1 - y)]:
        devs += [(*chip, c), (*chip, 1 - c)]
    return jnp.stack([4 * d[0] + 2 * d[1] + d[2] for d in devs]).astype(jnp.int32)


def _prep_h(x2, mod3):
    ts = 512
    per_seq = S // ts

    def body(x_ref, mod_ref, h_ref):
        shift = mod_ref[0, 0:1, :]
        scale = mod_ref[0, 1:2, :]
        h_ref[...] = (x_ref[...] * (1.0 + scale) + shift).astype(BF16)

    return pl.pallas_call(
        body, name="prep_h",
        grid=(T // ts,),
        in_specs=[pl.BlockSpec((ts, D), lambda i: (i, 0)),
                  pl.BlockSpec((1, 3, D), lambda i: (i // per_seq, 0, 0))],
        out_specs=pl.BlockSpec((ts, D), lambda i: (i, 0)),
        out_shape=jax.ShapeDtypeStruct((T, D), BF16),
    )(x2, mod3)


def _gather_proj(order, h, w_shard, tm, ride=()):
    rows, kdim = h.shape
    ncols = w_shard.shape[1]
    n_i = rows // tm
    ride_arrs, ride_shapes = ride if ride else ((), ())
    n_ride = len(ride_arrs)

    def body(order_ref, h_ref, mine_hbm, *rest):
        ride_ins = rest[:n_ride]
        o_ref, all_hbm = rest[n_ride:n_ride + 2]
        ride_outs = rest[n_ride + 2:2 * n_ride + 2]
        wv, send_sems, recv_sems, local_sems = rest[2 * n_ride + 2:2 * n_ride + 6]
        ride_scr = rest[2 * n_ride + 6:]
        j, i = pl.program_id(0), pl.program_id(1)
        x, y, c = lax.axis_index("x"), lax.axis_index("y"), lax.axis_index("c")
        me, sibling = (x, y, c), (x, y, 1 - c)
        chips = [(1 - x, y), (x, 1 - y), (1 - x, 1 - y)]

        def slot(dev):
            return 4 * dev[0] + 2 * dev[1] + dev[2]

        def copy(k, block, to):
            return pltpu.make_async_remote_copy(
                src_ref=wv.at[slot(block)], dst_ref=wv.at[slot(block)],
                send_sem=send_sems.at[k], recv_sem=recv_sems.at[k],
                device_id=to, device_id_type=MESH)

        def keep(step, block):
            return pltpu.make_async_copy(wv.at[slot(block)], all_hbm.at[slot(block)], local_sems.at[step])

        if n_ride:
            gather = _Gather(ride_ins, ride_outs, ride_scr[3:], *ride_scr[:3])
        first = [copy(0, me, sibling)] + [copy(1 + q, me, (*chip, c)) for q, chip in enumerate(chips)]
        passed = [copy(4 + q, (*chip, c), sibling) for q, chip in enumerate(chips)]
        due = [(me, None, None), (sibling, copy(0, sibling, me), None)]
        for q, chip in enumerate(chips):
            due.append(((*chip, c), copy(1 + q, (*chip, c), me), passed[q]))
            due.append(((*chip, 1 - c), copy(4 + q, (*chip, 1 - c), me), None))

        @pl.when((j == 0) & (i == 0))
        def _():
            load = pltpu.make_async_copy(mine_hbm, wv.at[slot(me)], local_sems.at[N_DEV])
            load.start()
            load.wait()
            for cp in first:
                cp.start()
            keep(0, me).start()

        for step in range(1, N_DEV):
            block, arrival, forward = due[step]

            @pl.when((j == step) & (i == 0))
            def _():
                arrival.wait_recv()
                if forward is not None:
                    forward.start()
                keep(step, block).start()
                if n_ride and step == N_DEV - 2:
                    gather.begin()

        o_ref[...] = _dot(h_ref[...], wv[order_ref[j]]).astype(BF16)

        @pl.when((j == N_DEV - 1) & (i == n_i - 1))
        def _():
            for cp in first + passed:
                cp.wait_send()
            for step in range(N_DEV):
                keep(step, due[step][0]).wait()
            if n_ride:
                gather.finish()

    any_spec = pl.BlockSpec(memory_space=pl.ANY)
    res = pl.pallas_call(
        body, name="gather_proj",
        grid_spec=pltpu.PrefetchScalarGridSpec(
            num_scalar_prefetch=1,
            grid=(N_DEV, n_i),
            in_specs=[pl.BlockSpec((tm, kdim), lambda j, i, order_ref: (i, 0)), any_spec] + [any_spec] * n_ride,
            out_specs=(pl.BlockSpec((tm, ncols), lambda j, i, order_ref: (i, order_ref[j])), any_spec)
                      + (any_spec,) * n_ride,
            scratch_shapes=[pltpu.VMEM((N_DEV, kdim, ncols), BF16),
                            pltpu.SemaphoreType.DMA((7,)), pltpu.SemaphoreType.DMA((7,)),
                            pltpu.SemaphoreType.DMA((N_DEV + 1,))]
                           + (_Gather.scratch(ride_arrs) if n_ride else [])),
        out_shape=(jax.ShapeDtypeStruct((rows, N_DEV * ncols), BF16),
                   jax.ShapeDtypeStruct((N_DEV, kdim, ncols), BF16)) + tuple(ride_shapes),
        compiler_params=pltpu.CompilerParams(vmem_limit_bytes=VMEM_LIMIT),
    )(order, h, w_shard, *ride_arrs)
    return res[0], res[1], res[2:]


def _bucket_maps():
    a = np.arange(QB)[:, None]
    b = np.arange(2 * QB)[None, :]
    steps = a + QB - b
    maps = []
    for dil in DILATIONS:
        dist = np.maximum(steps, 0) * dil
        nf = np.maximum(dist, 1).astype(np.float32)
        large = 16 + (np.log(nf / np.float32(16)) / np.float32(math.log(128.0))
                      * np.float32(16)).astype(np.int32)
        large = np.minimum(large, N_BUCKETS - 1)
        maps.append(np.where(dist < 16, dist, large).astype(np.int32))
    band = (steps >= 0) & (steps <= N_STEPS)
    first = band & (b >= QB)
    masks = np.stack([first, band]).astype(np.int32)
    return np.stack(maps), masks


def _bias_expand(rel_bias, buckets, masks):
    def body(tab_ref, bk_ref, mk_ref, o_ref):
        for g in range(3):
            bk = bk_ref[g]
            for h in range(4):
                col = 4 * g + h
                val = jnp.zeros((QB, 2 * QB), F32)
                for k in range(N_BUCKETS):
                    val = jnp.where(bk == k, tab_ref[k, col], val)
                o_ref[g, 0, h] = jnp.where(mk_ref[0] != 0, val, NEG_INF)
                o_ref[g, 1, h] = jnp.where(mk_ref[1] != 0, val, NEG_INF)

    return pl.pallas_call(
        body, name="bias_expand",
        in_specs=[pl.BlockSpec(memory_space=pltpu.SMEM),
                  pl.BlockSpec(memory_space=pltpu.VMEM),
                  pl.BlockSpec(memory_space=pltpu.VMEM)],
        out_shape=jax.ShapeDtypeStruct((3, 2, 4, QB, 2 * QB), F32),
    )(rel_bias, buckets, masks)


def _bias_grad(ds1, ds2, ds3, buckets):
    def body(d1_ref, d2_ref, d3_ref, bk_ref, o_ref):
        for g, d_ref in enumerate((d1_ref, d2_ref, d3_ref)):
            bk = bk_ref[g]
            for h in range(4):
                dv = d_ref[h]
                for k in range(N_BUCKETS):
                    o_ref[k, 4 * g + h] = jnp.sum(jnp.where(bk == k, dv, 0.0))

    return pl.pallas_call(
        body, name="bias_grad",
        in_specs=[pl.BlockSpec(memory_space=pltpu.VMEM)] * 4,
        out_specs=pl.BlockSpec(memory_space=pltpu.SMEM),
        out_shape=jax.ShapeDtypeStruct((N_BUCKETS, N_HEADS), F32),
    )(ds1, ds2, ds3, buckets)


def _scratch_sets(rows):
    return 4 if rows <= 512 else 1


def _unit_chunks(dil, size=16):
    units = [(h, r) for h in range(4) for r in range(dil)]
    return [units[i:i + size] for i in range(0, len(units), size)]


def _residue_rows(src_ref, copies, h, residue):
    sl = slice(h * HD, (h + 1) * HD)
    if copies is None:
        return lambda r: src_ref[:, sl]
    buf = copies[h % len(copies)]
    buf[...] = src_ref[:, sl].astype(F32)
    return lambda r: buf[residue(r), :].astype(BF16)


def _attn_fwd(proj, bias, g):
    dil = DILATIONS[g]
    rows = QB * dil
    nsb = S // rows
    has_prev = nsb > 1

    def residue(r):
        return pl.ds(r, QB, stride=dil) if dil > 1 else pl.ds(0, QB)

    strided = dil > 1
    n_sets = _scratch_sets(rows)
    n_in = 6 if has_prev else 4
    n_copied = (4 + (2 if has_prev else 0)) * (n_sets if strided else 0)

    def body(*refs):
        q_ref, kc_ref, vc_ref = refs[:3]
        kp_ref, vp_ref = refs[3:5] if has_prev else (None, None)
        b_ref = refs[n_in - 1]
        o_ref, l_ref = refs[n_in:n_in + 2]
        scr = list(refs[n_in + 2:])
        ls = [scr.pop(0) for _ in range(4)]
        copies = {name: [scr.pop(0) for _ in range(n_sets)] if strided else None
                  for name in ("q", "kc", "vc", "o") + (("kp", "vp") if has_prev else ())}
        lane = lax.broadcasted_iota(jnp.int32, (QB, 128), 1)
        refs_of = {"q": q_ref, "kc": kc_ref, "vc": vc_ref, "kp": kp_ref, "vp": vp_ref}
        for chunk in _unit_chunks(dil):
            rows_of = {h: {name: _residue_rows(refs_of[name], copies[name], h, residue)
                           for name in refs_of if refs_of[name] is not None}
                       for h in sorted({h for h, _ in chunk})}

            def batch(name):
                return jnp.stack([rows_of[h][name](r) for h, r in chunk])

            q, k, v = batch("q"), batch("kc"), batch("vc")
            if has_prev:
                k = jnp.concatenate([batch("kp"), k], axis=1)
                v = jnp.concatenate([batch("vp"), v], axis=1)
                bias_b = jnp.stack([b_ref[h] for h, _ in chunk])
            else:
                bias_b = jnp.stack([b_ref[h, :, QB:] for h, _ in chunk])
            s = jnp.einsum("uqd,ukd->uqk", q, k, preferred_element_type=F32) * SCALE + bias_b
            m = jnp.max(s, axis=-1, keepdims=True)
            p = jnp.exp(s - m)
            l = jnp.sum(p, axis=-1, keepdims=True)
            o = jnp.einsum("uqk,ukd->uqd", p.astype(BF16), v, preferred_element_type=F32) / l
            lse = m + jnp.log(l)
            for i, (h, r) in enumerate(chunk):
                if strided:
                    copies["o"][h % n_sets][residue(r), :] = o[i]
                else:
                    o_ref[:, h * HD:(h + 1) * HD] = o[i]
                ls[h][r * QB:(r + 1) * QB, :] = jnp.where(lane == h, lse[i], 0.0)
            if strided:
                for h in sorted({h for h, _ in chunk}):
                    o_ref[:, h * HD:(h + 1) * HD] = copies["o"][h % n_sets][...]
        for r in range(dil):
            blk = slice(r * QB, (r + 1) * QB)
            l_ref[residue(r), :] = (ls[0][blk, :] + ls[1][blk, :]) + (ls[2][blk, :] + ls[3][blk, :])

    def row(b, n):
        return b * nsb + n

    def prev(b, n):
        return b * nsb + jnp.maximum(n - 1, 0)

    in_specs = [
        pl.BlockSpec((rows, GW), lambda b, n: (row(b, n), CB_Q + g)),
        pl.BlockSpec((rows, GW), lambda b, n: (row(b, n), CB_K + g)),
        pl.BlockSpec((rows, GW), lambda b, n: (row(b, n), CB_V + g)),
    ]
    args = [proj, proj, proj]
    scratch = [pltpu.VMEM((rows, 128), F32)] * (4 + n_copied)
    if has_prev:
        in_specs += [pl.BlockSpec((rows, GW), lambda b, n: (prev(b, n), CB_K + g)),
                     pl.BlockSpec((rows, GW), lambda b, n: (prev(b, n), CB_V + g))]
        args += [proj, proj]
    in_specs.append(pl.BlockSpec((None, None, 4, QB, 2 * QB),
                                 lambda b, n: (g, jnp.minimum(n, 1), 0, 0, 0)))
    args.append(bias)
    return pl.pallas_call(
        body, name=f"attn_fwd{g}",
        grid=(BL, nsb),
        in_specs=in_specs,
        out_specs=(pl.BlockSpec((rows, GW), lambda b, n: (row(b, n), 0)),
                   pl.BlockSpec((rows, 128), lambda b, n: (row(b, n), 0))),
        out_shape=(jax.ShapeDtypeStruct((T, GW), F32), jax.ShapeDtypeStruct((T, 128), F32)),
        scratch_shapes=scratch,
        compiler_params=pltpu.CompilerParams(vmem_limit_bytes=VMEM_LIMIT),
    )(*args)


def _attn_bwd(proj, d_out, stats, bias, dproj, g):
    dil = DILATIONS[g]
    rows = QB * dil
    nsb = S // rows
    has_prev = nsb > 1
    n_steps = nsb + 1 if has_prev else 1
    n_in = 7 + (2 if has_prev else 0)

    def residue(r):
        return pl.ds(r, QB, stride=dil) if dil > 1 else pl.ds(0, QB)

    strided = dil > 1
    n_sets = _scratch_sets(rows)

    def body(*refs):
        q_ref, kc_ref, vc_ref, do_ref, st_ref, b_ref = refs[:6]
        kp_ref, vp_ref = refs[6:8] if has_prev else (None, None)
        out_ref, db_ref = refs[n_in], refs[n_in + 1]
        scr = list(refs[n_in + 2:])
        sq, sk, sv, sems = [scr.pop(0) for _ in range(4)]
        carry = scr.pop(0) if has_prev else None
        sts = scr.pop(0) if strided else st_ref
        copies = {name: [scr.pop(0) for _ in range(n_sets)] if strided else None
                  for name in ("q", "kc", "vc", "do", "dq", "dk", "dv") + (("kp", "vp") if has_prev else ())}
        b, n = pl.program_id(0), pl.program_id(1)

        @pl.when((b == 0) & (n == 0))
        def _():
            db_ref[...] = jnp.zeros_like(db_ref)

        def finish(h, r, dq, dk, dv):
            if strided:
                for name, val in (("dq", dq), ("dk", dk), ("dv", dv)):
                    copies[name][h % n_sets][residue(r), :] = val
            else:
                sl = slice(h * HD, (h + 1) * HD)
                sq[:, sl], sk[:, sl], sv[:, sl] = dq.astype(BF16), dk.astype(BF16), dv.astype(BF16)

        def finish_head(h):
            if strided:
                sl = slice(h * HD, (h + 1) * HD)
                sq[:, sl] = copies["dq"][h % n_sets][...].astype(BF16)
                sk[:, sl] = copies["dk"][h % n_sets][...].astype(BF16)
                sv[:, sl] = copies["dv"][h % n_sets][...].astype(BF16)

        def write_block(blk_idx):
            row0 = pl.multiple_of(blk_idx * rows, rows)
            _write_columns([(sq, CB * (CB_Q + g)), (sk, CB * (CB_K + g)), (sv, CB * (CB_V + g))],
                           out_ref, row0, sems)

        def carried(h, r):
            blk = slice(r * QB, (r + 1) * QB)
            return ((blk, slice(h * HD, (h + 1) * HD)), (blk, slice(GW + h * HD, GW + (h + 1) * HD)),
                    (blk, slice(2 * GW + h * HD, 2 * GW + (h + 1) * HD)))

        if has_prev:
            @pl.when(n == 0)
            def _():
                carry[...] = jnp.zeros_like(carry)

            @pl.when(n == nsb)
            def _():
                for h in range(4):
                    for r in range(dil):
                        cq, ck, cv = carried(h, r)
                        finish(h, r, carry[cq], carry[ck], carry[cv])
                    finish_head(h)
                write_block(b * nsb + nsb - 1)

        @pl.when(n < nsb)
        def _():
            if strided:
                for r in range(dil):
                    sts[r * QB:(r + 1) * QB, :] = st_ref[residue(r), :]
            refs_of = {"q": q_ref, "kc": kc_ref, "vc": vc_ref, "do": do_ref, "kp": kp_ref, "vp": vp_ref}
            for chunk in _unit_chunks(dil):
                heads = sorted({h for h, _ in chunk})
                rows_of = {h: {name: _residue_rows(refs_of[name], copies[name], h, residue)
                               for name in refs_of if refs_of[name] is not None}
                           for h in heads}

                def batch(name):
                    return jnp.stack([rows_of[h][name](r) for h, r in chunk])

                q, k, v, do = batch("q"), batch("kc"), batch("vc"), batch("do")
                if has_prev:
                    k = jnp.concatenate([batch("kp"), k], axis=1)
                    v = jnp.concatenate([batch("vp"), v], axis=1)
                    bias_b = jnp.stack([b_ref[h] for h, _ in chunk])
                else:
                    bias_b = jnp.stack([b_ref[h, :, QB:] for h, _ in chunk])
                lse = jnp.stack([sts[r * QB:(r + 1) * QB, h:h + 1] for h, r in chunk])
                delta = jnp.stack([sts[r * QB:(r + 1) * QB, 4 + h:5 + h] for h, r in chunk])
                s = jnp.einsum("uqd,ukd->uqk", q, k, preferred_element_type=F32) * SCALE + bias_b
                p = jnp.exp(s - lse)
                ds = p * (jnp.einsum("uqd,ukd->uqk", do, v, preferred_element_type=F32) - delta)
                for h in heads:
                    mine = [ds[i] for i, (hh, _) in enumerate(chunk) if hh == h]
                    tot = mine[0]
                    for extra in mine[1:]:
                        tot = tot + extra
                    if has_prev:
                        db_ref[h] += tot
                    else:
                        db_ref[h, :, QB:] += tot
                dsb, pb = ds.astype(BF16), p.astype(BF16)
                dq = jnp.einsum("uqk,ukd->uqd", dsb, k, preferred_element_type=F32) * SCALE
                dk = jnp.einsum("uqk,uqd->ukd", dsb, q, preferred_element_type=F32) * SCALE
                dv = jnp.einsum("uqk,uqd->ukd", pb, do, preferred_element_type=F32)
                for i, (h, r) in enumerate(chunk):
                    if has_prev:
                        cq, ck, cv = carried(h, r)
                        finish(h, r, carry[cq], carry[ck] + dk[i, :QB], carry[cv] + dv[i, :QB])
                        carry[cq] = dq[i]
                        carry[ck] = dk[i, QB:]
                        carry[cv] = dv[i, QB:]
                    else:
                        finish(h, r, dq[i], dk[i], dv[i])
                for h in heads:
                    finish_head(h)
            if has_prev:
                @pl.when(n > 0)
                def _():
                    write_block(b * nsb + n - 1)
            else:
                write_block(b)

    def row(b, n):
        return b * nsb + jnp.minimum(n, nsb - 1)

    def prev(b, n):
        return b * nsb + jnp.maximum(jnp.minimum(n, nsb - 1) - 1, 0)

    in_specs = [
        pl.BlockSpec((rows, GW), lambda b, n: (row(b, n), CB_Q + g)),
        pl.BlockSpec((rows, GW), lambda b, n: (row(b, n), CB_K + g)),
        pl.BlockSpec((rows, GW), lambda b, n: (row(b, n), CB_V + g)),
        pl.BlockSpec((rows, GW), lambda b, n: (row(b, n), 0)),
        pl.BlockSpec((rows, 128), lambda b, n: (row(b, n), 0)),
        pl.BlockSpec((None, None, 4, QB, 2 * QB),
                     lambda b, n: (g, jnp.minimum(jnp.minimum(n, nsb - 1), 1), 0, 0, 0)),
    ]
    args = [proj, proj, proj, d_out, stats, bias]
    scratch = [pltpu.VMEM((rows, GW), BF16)] * 3 + [pltpu.SemaphoreType.DMA((3,))]
    if has_prev:
        in_specs += [pl.BlockSpec((rows, GW), lambda b, n: (prev(b, n), CB_K + g)),
                     pl.BlockSpec((rows, GW), lambda b, n: (prev(b, n), CB_V + g))]
        args += [proj, proj]
        scratch.append(pltpu.VMEM((rows, 3 * GW), F32))
    if strided:
        n_copied = (7 + (2 if has_prev else 0)) * n_sets
        scratch += [pltpu.VMEM((rows, 128), F32)] * (1 + n_copied)
    in_specs.append(pl.BlockSpec(memory_space=pl.ANY))
    args.append(dproj)
    return pl.pallas_call(
        body, name=f"attn_bwd{g}",
        grid=(BL, n_steps),
        in_specs=in_specs,
        out_specs=(pl.BlockSpec(memory_space=pl.ANY),
                   pl.BlockSpec((4, QB, 2 * QB), lambda b, n: (0, 0, 0))),
        out_shape=(jax.ShapeDtypeStruct((T, NCOL), BF16),
                   jax.ShapeDtypeStruct((4, QB, 2 * QB), F32)),
        scratch_shapes=scratch,
        input_output_aliases={len(args) - 1: 0},
        compiler_params=pltpu.CompilerParams(vmem_limit_bytes=VMEM_LIMIT),
    )(*args)


def _tail(x2, tgt2, mod3, o_g, lse_g, proj, w_ao, w_co, w_o, conv_w, conv_b, ln_g, ln_b):
    tm = 256
    per_seq = S // tm
    halo = 16

    def body(x_ref, t_ref, mod_ref, o1_ref, o2_ref, o3_ref, l1_ref, l2_ref, l3_ref,
             ga_ref, u_ref, bg_ref, cg_ref, gc_ref, ma_ref, mc_ref, up_ref, cp_ref,
             wao_ref, wco_ref, wo_ref, cw_ref, cb_ref, lg_ref, lb_ref,
             dproj_ref, dyc_ref, do_ref, st_ref, dxd_ref,
             mg_ref, dy_ref, ain_ref, dao_ref, sin_ref, dso_ref, vec_ref,
             dga_s, dbg_s, dgm_s, sems):
        i = pl.program_id(0)
        bidx = i // per_seq
        first = (i % per_seq) == 0

        @pl.when(i == 0)
        def _():
            vec_ref[...] = jnp.zeros_like(vec_ref)

        l1, l2, l3 = l1_ref[...], l2_ref[...], l3_ref[...]
        mx = jnp.maximum(jnp.maximum(l1, l2), l3)
        e1, e2, e3 = jnp.exp(l1 - mx), jnp.exp(l2 - mx), jnp.exp(l3 - mx)
        esum = e1 + e2 + e3
        lse_tot = mx + jnp.log(esum)
        w1, w2, w3 = e1 / esum, e2 / esum, e3 / esum

        def per_head(wv):
            return jnp.concatenate([jnp.broadcast_to(wv[:, h:h + 1], (tm, HD)) for h in range(4)], axis=1)

        o = per_head(w1) * o1_ref[...] + per_head(w2) * o2_ref[...] + per_head(w3) * o3_ref[...]

        ga = ga_ref[...].astype(F32)
        sig_ga = _sigmoid(ga)
        silu_ga = ga * sig_ga
        a_in = (o * silu_ga).astype(BF16)
        a_out = _dot(a_in, wao_ref[...])

        u = u_ref[...].astype(F32)
        cg = cg_ref[...].astype(F32)
        z = cg * u
        zp = cp_ref[...].astype(F32) * up_ref[...].astype(F32)
        zp = jnp.where(first, 0.0, zp)
        zcat = jnp.concatenate([zp, z], axis=0)
        z1 = pltpu.roll(zcat, 1, 0)[halo:]
        z2 = pltpu.roll(zcat, 2, 0)[halo:]
        y_conv = cw_ref[0:1, :] * z2 + cw_ref[1:2, :] * z1 + cw_ref[2:3, :] * z + cb_ref[...]
        gc = gc_ref[...].astype(F32)
        sig_gc = _sigmoid(gc)
        silu_gc = gc * sig_gc
        bg = bg_ref[...].astype(F32)
        bg_yc = bg * y_conv
        s_in = (bg_yc * silu_gc).astype(BF16)
        s_out = _dot(s_in, wco_ref[...])

        sa = _sigmoid(ma_ref[...].astype(F32))
        sc = _sigmoid(mc_ref[...].astype(F32))
        merged = (sa * a_out + sc * s_out).astype(BF16)
        y = _dot(merged, wo_ref[...])
        gate1 = 1.0 + mod_ref[0, 2:3, :]
        xv = x_ref[...]
        resid = ALPHA * xv + gate1 * y
        mu = jnp.mean(resid, axis=1, keepdims=True)
        xc = resid - mu
        var = jnp.mean(xc * xc, axis=1, keepdims=True)
        rstd = lax.rsqrt(var + LN_EPS)
        xhat = xc * rstd
        lg = lg_ref[...]
        err = xhat * lg + lb_ref[...] - t_ref[...]
        vec_ref[3:4, :] += (0.5 / D) * jnp.sum(err * err, axis=0, keepdims=True)

        vec_ref[1:2, :] += (1.0 / D) * jnp.sum(err * xhat, axis=0, keepdims=True)
        vec_ref[2:3, :] += (1.0 / D) * jnp.sum(err, axis=0, keepdims=True)
        dxh = err * (lg * (1.0 / D))
        dres = rstd * (dxh - jnp.mean(dxh, axis=1, keepdims=True)
                       - xhat * jnp.mean(dxh * xhat, axis=1, keepdims=True))
        dxd_ref[...] = ALPHA * dres
        dgate = jnp.sum(dres * y, axis=0, keepdims=True)
        vec_ref[4:5, :] += jnp.where(bidx == 0, dgate, 0.0)
        vec_ref[5:6, :] += jnp.where(bidx == 1, dgate, 0.0)
        dy = (dres * gate1).astype(BF16)

        dmerged = _dot_nt(dy, wo_ref[...])
        da_out_f = dmerged * sa
        ds_out_f = dmerged * sc
        da_out = da_out_f.astype(BF16)
        ds_out = ds_out_f.astype(BF16)
        dgm_s[:, 2 * D:3 * D] = (ds_out_f * s_out * (1.0 - sc)).astype(BF16)
        dgm_s[:, D:2 * D] = (da_out_f * a_out * (1.0 - sa)).astype(BF16)
        da_in = _dot_nt(da_out, wao_ref[...])
        ds_in = _dot_nt(ds_out, wco_ref[...])

        d_o = da_in * silu_ga
        do_ref[...] = d_o.astype(BF16)
        dga_s[...] = (da_in * o * (sig_ga + silu_ga * (1.0 - sig_ga))).astype(BF16)
        lane = lax.broadcasted_iota(jnp.int32, (tm, 128), 1)
        stats = lse_tot
        od = o * d_o
        for h in range(4):
            delta = jnp.sum(od[:, h * HD:(h + 1) * HD], axis=1, keepdims=True)
            stats = jnp.where(lane == 4 + h, delta, stats)
        st_ref[...] = stats

        ds_silu = ds_in * silu_gc
        dbg_s[...] = (ds_silu * y_conv).astype(BF16)
        dyc = ds_silu * bg
        dyc_ref[...] = dyc
        vec_ref[0:1, :] += jnp.sum(dyc, axis=0, keepdims=True)
        dgm_s[:, 0:D] = (ds_in * bg_yc * (sig_gc + silu_gc * (1.0 - sig_gc))).astype(BF16)

        mg_ref[...] = merged
        dy_ref[...] = dy
        ain_ref[...] = a_in
        dao_ref[...] = da_out
        sin_ref[...] = s_in
        dso_ref[...] = ds_out
        _write_columns([(dga_s, CB * CB_GA), (dbg_s, D * KB_BG), (dgm_s, D * KB_GC)],
                       dproj_ref, pl.multiple_of(i * tm, tm), sems)

    def tile(width, cblk=0):
        return pl.BlockSpec((tm, width), lambda i: (i, cblk))

    def whole(shape):
        return pl.BlockSpec(shape, lambda i: tuple(0 for _ in shape))

    prev_rows = lambda i: (jnp.maximum(i * (tm // halo) - 1, 0),)
    in_specs = [
        tile(D), tile(D), pl.BlockSpec((1, 3, D), lambda i: (i // per_seq, 0, 0)),
        tile(GW), tile(GW), tile(GW), tile(128), tile(128), tile(128),
        tile(GW, CB_GA), tile(D, KB_U), tile(D, KB_BG), tile(D, KB_CG), tile(D, KB_GC),
        tile(D, KB_MA), tile(D, KB_MC),
        pl.BlockSpec((halo, D), lambda i: (*prev_rows(i), KB_U)),
        pl.BlockSpec((halo, D), lambda i: (*prev_rows(i), KB_CG)),
        whole((GW, D)), whole((D, D)), whole((D, D)),
        whole((3, D)), whole((1, D)), whole((1, D)), whole((1, D)),
    ]
    out_specs = (
        pl.BlockSpec(memory_space=pl.ANY), tile(D), tile(GW), tile(128), tile(D),
        tile(D), tile(D), tile(GW), tile(D), tile(D), tile(D),
        pl.BlockSpec((8, D), lambda i: (0, 0)),
    )
    out_shape = (
        jax.ShapeDtypeStruct((T, NCOL), BF16),
        jax.ShapeDtypeStruct((T, D), F32),
        jax.ShapeDtypeStruct((T, GW), BF16),
        jax.ShapeDtypeStruct((T, 128), F32),
        jax.ShapeDtypeStruct((T, D), F32),
        jax.ShapeDtypeStruct((T, D), BF16),
        jax.ShapeDtypeStruct((T, D), BF16),
        jax.ShapeDtypeStruct((T, GW), BF16),
        jax.ShapeDtypeStruct((T, D), BF16),
        jax.ShapeDtypeStruct((T, D), BF16),
        jax.ShapeDtypeStruct((T, D), BF16),
        jax.ShapeDtypeStruct((8, D), F32),
    )
    return pl.pallas_call(
        body, name="tail",
        grid=(T // tm,),
        in_specs=in_specs, out_specs=out_specs, out_shape=out_shape,
        scratch_shapes=[pltpu.VMEM((tm, GW), BF16), pltpu.VMEM((tm, D), BF16), pltpu.VMEM((tm, 3 * D), BF16),
                        pltpu.SemaphoreType.DMA((3,))],
        compiler_params=pltpu.CompilerParams(vmem_limit_bytes=VMEM_LIMIT),
    )(x2, tgt2, mod3, *o_g, *lse_g, proj, proj, proj, proj, proj, proj, proj, proj, proj,
      w_ao, w_co, w_o, conv_w, conv_b, ln_g, ln_b)


def _conv_bwd(dyc, proj, conv_w, dproj):
    tm = 512
    per_seq = S // tm
    halo = 16

    def body(d_ref, dn_ref, u_ref, c_ref, up_ref, cp_ref, cw_ref, _, dproj_ref, g_ref, du_s, dc_s, sems):
        i = pl.program_id(0)
        first = (i % per_seq) == 0
        last = (i % per_seq) == per_seq - 1

        @pl.when(i == 0)
        def _():
            g_ref[...] = jnp.zeros_like(g_ref)

        d = d_ref[...]
        dn = jnp.where(last, 0.0, dn_ref[...])
        dcat = jnp.concatenate([d, dn], axis=0)
        d1 = pltpu.roll(dcat, tm + 8 - 1, 0)[:tm]
        d2 = pltpu.roll(dcat, tm + 8 - 2, 0)[:tm]
        dz = cw_ref[2:3, :] * d + cw_ref[1:2, :] * d1 + cw_ref[0:1, :] * d2
        u = u_ref[...].astype(F32)
        cg = c_ref[...].astype(F32)
        du_s[...] = (dz * cg).astype(BF16)
        dc_s[...] = (dz * u).astype(BF16)
        _write_columns([(du_s, D * KB_U), (dc_s, D * KB_CG)], dproj_ref, pl.multiple_of(i * tm, tm), sems)

        z = cg * u
        zp = jnp.where(first, 0.0, cp_ref[...].astype(F32) * up_ref[...].astype(F32))
        zcat = jnp.concatenate([zp, z], axis=0)
        z1 = pltpu.roll(zcat, 1, 0)[halo:]
        z2 = pltpu.roll(zcat, 2, 0)[halo:]
        g_ref[0:1, :] += jnp.sum(d * z2, axis=0, keepdims=True)
        g_ref[1:2, :] += jnp.sum(d * z1, axis=0, keepdims=True)
        g_ref[2:3, :] += jnp.sum(d * z, axis=0, keepdims=True)

    n_tiles = T // tm
    prev_rows = lambda i: jnp.maximum(i * (tm // halo) - 1, 0)
    next_rows = lambda i: jnp.minimum((i + 1) * (tm // 8), T // 8 - 1)
    return pl.pallas_call(
        body, name="conv_bwd",
        grid=(n_tiles,),
        in_specs=[pl.BlockSpec((tm, D), lambda i: (i, 0)),
                  pl.BlockSpec((8, D), lambda i: (next_rows(i), 0)),
                  pl.BlockSpec((tm, D), lambda i: (i, KB_U)),
                  pl.BlockSpec((tm, D), lambda i: (i, KB_CG)),
                  pl.BlockSpec((halo, D), lambda i: (prev_rows(i), KB_U)),
                  pl.BlockSpec((halo, D), lambda i: (prev_rows(i), KB_CG)),
                  pl.BlockSpec((3, D), lambda i: (0, 0)),
                  pl.BlockSpec(memory_space=pl.ANY)],
        out_specs=(pl.BlockSpec(memory_space=pl.ANY),
                   pl.BlockSpec((8, D), lambda i: (0, 0))),
        out_shape=(jax.ShapeDtypeStruct((T, NCOL), BF16),
                   jax.ShapeDtypeStruct((8, D), F32)),
        scratch_shapes=[pltpu.VMEM((tm, D), BF16), pltpu.VMEM((tm, D), BF16), pltpu.SemaphoreType.DMA((2,))],
        input_output_aliases={7: 0},
        compiler_params=pltpu.CompilerParams(vmem_limit_bytes=VMEM_LIMIT),
    )(dyc, dyc, proj, proj, proj, proj, conv_w, dproj)


def _dh_dx(dproj, w_in_all, x2, dxd, mod3, chip_sums, hops=(), parts0=None):
    tm = 1024
    per_seq = S // tm
    n = len(chip_sums)
    n_in = 5 + n + (0 if parts0 is None else 1)

    def body(*refs):
        d_ref, w_ref, x_ref, dxd_ref, mod_ref = refs[:5]
        ins = refs[5:5 + n]
        gx_ref, vec_ref = refs[n_in:n_in + 2]
        outs = refs[n_in + 2:n_in + 2 + n]
        acc, send_sems, recv_sems, local_sems = refs[n_in + 2 + n:]
        i, jj = pl.program_id(0), pl.program_id(1)

        @pl.when((i == 0) & (jj == 0))
        def _():
            vec_ref[...] = jnp.zeros_like(vec_ref)
            if n:
                sends, _, mine = _chip_copies(ins, outs, send_sems, recv_sems, local_sems, hops)
                for cp in sends + mine:
                    cp.start()

        if n:
            @pl.when((i == T // tm - 1) & (jj == N_DEV - 1))
            def _():
                sends, arrivals, mine = _chip_copies(ins, outs, send_sems, recv_sems, local_sems, hops)
                for cp in arrivals:
                    cp.wait_recv()
                for cp in sends:
                    cp.wait_send()
                for cp in mine:
                    cp.wait()

        @pl.when(jj == 0)
        def _():
            acc[...] = jnp.zeros_like(acc)

        acc[...] += _dot_nt(d_ref[...], w_ref[...])

        @pl.when(jj == N_DEV - 1)
        def _():
            dh = acc[...]
            bidx = i // per_seq
            gx_ref[...] = dxd_ref[...] + dh * (1.0 + mod_ref[0, 1:2, :])
            dshift = jnp.sum(dh, axis=0, keepdims=True)
            dscale = jnp.sum(dh * x_ref[...], axis=0, keepdims=True)
            vec_ref[0:1, :] += jnp.where(bidx == 0, dshift, 0.0)
            vec_ref[1:2, :] += jnp.where(bidx == 1, dshift, 0.0)
            vec_ref[2:3, :] += jnp.where(bidx == 0, dscale, 0.0)
            vec_ref[3:4, :] += jnp.where(bidx == 1, dscale, 0.0)

    any_spec = pl.BlockSpec(memory_space=pl.ANY)
    res = pl.pallas_call(
        body, name="dh_dx",
        grid=(T // tm, N_DEV),
        in_specs=[
            pl.BlockSpec((tm, SHARD), lambda i, jj: (i, jj)),
            pl.BlockSpec((None, D, SHARD), lambda i, jj: (jj, 0, 0)),
            pl.BlockSpec((tm, D), lambda i, jj: (i, 0)),
            pl.BlockSpec((tm, D), lambda i, jj: (i, 0)),
            pl.BlockSpec((1, 3, D), lambda i, jj: (i // per_seq, 0, 0))] + [any_spec] * (n_in - 5),
        out_specs=(pl.BlockSpec((tm, D), lambda i, jj: (i, 0)),
                   pl.BlockSpec((8, D), lambda i, jj: (0, 0))) + (any_spec,) * n,
        out_shape=(jax.ShapeDtypeStruct((T, D), F32), jax.ShapeDtypeStruct((8, D), F32))
                  + tuple(jax.ShapeDtypeStruct(a.shape, a.dtype) for a in chip_sums),
        scratch_shapes=[pltpu.VMEM((tm, D), F32), pltpu.SemaphoreType.DMA((max(3 * n, 1),)),
                        pltpu.SemaphoreType.DMA((max(3 * n, 1),)), pltpu.SemaphoreType.DMA((max(n, 1),))],
        input_output_aliases={} if parts0 is None else {5 + n: 2},
        compiler_params=pltpu.CompilerParams(vmem_limit_bytes=VMEM_LIMIT),
    )(dproj, w_in_all, x2, dxd, mod3, *chip_sums, *([] if parts0 is None else [parts0]))
    return res[0], res[1], res[2:]


def _mm_tn(a, b, tn, blocks_leading, name):
    kk, m = a.shape
    n = b.shape[1]
    tk = 2048

    def body(a_ref, b_ref, o_ref, acc):
        @pl.when(pl.program_id(1) == 0)
        def _():
            acc[...] = jnp.zeros_like(acc)

        acc[...] += _dot_tn(a_ref[...], b_ref[...])

        @pl.when(pl.program_id(1) == kk // tk - 1)
        def _():
            o_ref[...] = acc[...].astype(BF16)

    if blocks_leading:
        out_spec = pl.BlockSpec((None, m, tn), lambda j, k: (j, 0, 0))
        out_shape = jax.ShapeDtypeStruct((n // tn, m, tn), BF16)
    else:
        out_spec = pl.BlockSpec((m, tn), lambda j, k: (0, j))
        out_shape = jax.ShapeDtypeStruct((m, n), BF16)
    return pl.pallas_call(
        body, name=name,
        grid=(n // tn, kk // tk),
        in_specs=[pl.BlockSpec((tk, m), lambda j, k: (k, 0)),
                  pl.BlockSpec((tk, tn), lambda j, k: (k, j))],
        out_specs=out_spec, out_shape=out_shape,
        scratch_shapes=[pltpu.VMEM((m, tn), F32)],
        compiler_params=pltpu.CompilerParams(vmem_limit_bytes=VMEM_LIMIT),
    )(a, b)


def _adam_step(g, w, m, v):
    nm = ADAM_B1 * m + (1.0 - ADAM_B1) * g
    nv = ADAM_B2 * v + (1.0 - ADAM_B2) * (g * g)
    m_hat = nm / (1.0 - ADAM_B1 ** ADAM_STEP)
    v_hat = nv / (1.0 - ADAM_B2 ** ADAM_STEP)
    return -ADAM_LR * (m_hat / (jnp.sqrt(v_hat) + ADAM_EPS) + ADAM_WD * w), nm, nv


def _adamw(parts, w, m, v, name, row_tile=None):
    n_parts, rows, cols = parts.shape
    tr = rows if row_tile is None else row_tile

    def body(p_ref, w_ref, m_ref, v_ref, g_ref, d_ref, nm_ref, nv_ref):
        g = p_ref[0].astype(F32)
        for s in range(1, n_parts):
            g = g + p_ref[s].astype(F32)
        g_ref[...] = g
        d_ref[...], nm_ref[...], nv_ref[...] = _adam_step(g, w_ref[...], m_ref[...], v_ref[...])

    blk = pl.BlockSpec((tr, cols), lambda i: (i, 0))
    shp = jax.ShapeDtypeStruct((rows, cols), F32)
    return pl.pallas_call(
        body, name=name,
        grid=(rows // tr,),
        in_specs=[pl.BlockSpec((n_parts, tr, cols), lambda i: (0, i, 0)), blk, blk, blk],
        out_specs=(blk, blk, blk, blk),
        out_shape=(shp, shp, shp, shp),
        compiler_params=pltpu.CompilerParams(vmem_limit_bytes=VMEM_LIMIT),
    )(parts, w, m, v)


def _multi_adamw(parts_list, params, name):
    n = len(params)
    flat = [t for wmv in params for t in wmv]

    def body(*refs):
        parts, ins, outs = refs[:n], refs[n:4 * n], refs[4 * n:]
        for p in range(n):
            g = parts[p][0].astype(F32)
            for s in range(1, parts[p].shape[0]):
                g = g + parts[p][s].astype(F32)
            w_ref, m_ref, v_ref = ins[3 * p:3 * p + 3]
            g_ref, d_ref, nm_ref, nv_ref = outs[4 * p:4 * p + 4]
            g_ref[...] = g
            d_ref[...], nm_ref[...], nv_ref[...] = _adam_step(g, w_ref[...], m_ref[...], v_ref[...])

    out_shape = []
    for w, _, _ in params:
        out_shape += [jax.ShapeDtypeStruct(w.shape, F32)] * 4
    res = pl.pallas_call(body, name=name, out_shape=tuple(out_shape))(*parts_list, *flat)
    return [res[4 * p:4 * p + 4] for p in range(n)]


def _small_updates(small_g, dmod_all, rel_parts, params):
    flat = [t for wmv in params for t in wmv]

    def body(sg_ref, dm_ref, rp_ref, *refs):
        ins, outs = refs[:len(flat)], refs[len(flat):]

        def over_devices(row):
            tot = sg_ref[0, row:row + 1, :]
            for s in range(1, N_DEV):
                tot = tot + sg_ref[s, row:row + 1, :]
            return tot

        g_b_ada = dm_ref[0:1, :]
        for r in range(1, N_DEV * BL):
            g_b_ada = g_b_ada + dm_ref[r:r + 1, :]
        g_rel = rp_ref[0]
        for s in range(1, N_DEV):
            g_rel = g_rel + rp_ref[s]
        grads = [g_b_ada, over_devices(0), g_rel, over_devices(1), over_devices(2)]
        outs[0][...] = jnp.sum(over_devices(3), axis=1, keepdims=True)
        for p, g in enumerate(grads):
            w_ref, m_ref, v_ref = ins[3 * p:3 * p + 3]
            g_ref, d_ref, nm_ref, nv_ref = outs[1 + 4 * p:5 + 4 * p]
            g_ref[...] = g
            d_ref[...], nm_ref[...], nv_ref[...] = _adam_step(g, w_ref[...], m_ref[...], v_ref[...])

    out_shape = [jax.ShapeDtypeStruct((1, 1), F32)]
    for w, _, _ in params:
        out_shape += [jax.ShapeDtypeStruct(w.shape, F32)] * 4
    res = pl.pallas_call(body, name="small_updates", out_shape=tuple(out_shape))(small_g, dmod_all, rel_parts, *flat)
    return res[0], [res[1 + 4 * p:5 + 4 * p] for p in range(len(params))]


def _attn_fwd_dense(proj, bias):
    nq = 4
    rows = nq * QB
    nsb = S // rows

    def body(q_ref, k_ref, v_ref, kp_ref, vp_ref, b_ref, o_ref, l_ref, ls0, ls1, ls2, ls3):
        ls = [ls0, ls1, ls2, ls3]
        n = pl.program_id(1)
        lane = lax.broadcasted_iota(jnp.int32, (QB, 128), 1)
        units = [(h, j) for h in range(4) for j in range(nq)]

        def keys(cur_ref, prev_ref, h, j):
            sl = slice(h * HD, (h + 1) * HD)
            if j == 0:
                return jnp.concatenate([prev_ref[:, sl], cur_ref[0:QB, sl]], axis=0)
            return cur_ref[(j - 1) * QB:(j + 1) * QB, sl]

        q = jnp.stack([q_ref[j * QB:(j + 1) * QB, h * HD:(h + 1) * HD] for h, j in units])
        k = jnp.stack([keys(k_ref, kp_ref, h, j) for h, j in units])
        v = jnp.stack([keys(v_ref, vp_ref, h, j) for h, j in units])
        bias_b = jnp.stack([b_ref[jnp.minimum(n, 1), h] if j == 0 else b_ref[1, h] for h, j in units])
        s = jnp.einsum("uqd,ukd->uqk", q, k, preferred_element_type=F32) * SCALE + bias_b
        m = jnp.max(s, axis=-1, keepdims=True)
        p = jnp.exp(s - m)
        l = jnp.sum(p, axis=-1, keepdims=True)
        o = jnp.einsum("uqk,ukd->uqd", p.astype(BF16), v, preferred_element_type=F32) / l
        lse = m + jnp.log(l)
        for i, (h, j) in enumerate(units):
            o_ref[j * QB:(j + 1) * QB, h * HD:(h + 1) * HD] = o[i]
            ls[h][j * QB:(j + 1) * QB, :] = jnp.where(lane == h, lse[i], 0.0)
        l_ref[...] = (ls[0][...] + ls[1][...]) + (ls[2][...] + ls[3][...])

    def row(b, n):
        return b * nsb + n

    def prev(b, n):
        return jnp.maximum((b * nsb + n) * nq - 1, 0)

    in_specs = [
        pl.BlockSpec((rows, GW), lambda b, n: (row(b, n), CB_Q)),
        pl.BlockSpec((rows, GW), lambda b, n: (row(b, n), CB_K)),
        pl.BlockSpec((rows, GW), lambda b, n: (row(b, n), CB_V)),
        pl.BlockSpec((QB, GW), lambda b, n: (prev(b, n), CB_K)),
        pl.BlockSpec((QB, GW), lambda b, n: (prev(b, n), CB_V)),
        pl.BlockSpec((None, 2, 4, QB, 2 * QB), lambda b, n: (0, 0, 0, 0, 0)),
    ]
    return pl.pallas_call(
        body, name="attn_fwd0",
        grid=(BL, nsb),
        in_specs=in_specs,
        out_specs=(pl.BlockSpec((rows, GW), lambda b, n: (row(b, n), 0)),
                   pl.BlockSpec((rows, 128), lambda b, n: (row(b, n), 0))),
        out_shape=(jax.ShapeDtypeStruct((T, GW), F32), jax.ShapeDtypeStruct((T, 128), F32)),
        scratch_shapes=[pltpu.VMEM((rows, 128), F32)] * 4,
        compiler_params=pltpu.CompilerParams(vmem_limit_bytes=VMEM_LIMIT),
    )(proj, proj, proj, proj, proj, bias)


def _attn_bwd_dense(proj, d_out, stats, bias, dproj):
    nq = 4
    rows = nq * QB
    nsb = S // rows
    cols_q, cols_k, cols_v = CB * CB_Q, CB * CB_K, CB * CB_V

    def body(q_ref, k_ref, v_ref, do_ref, st_ref, kp_ref, vp_ref, b_ref, _, out_ref, db_ref,
             sq, sk, sv, carry, sems):
        b, n = pl.program_id(0), pl.program_id(1)
        units = [(h, j) for h in range(4) for j in range(nq)]

        @pl.when((b == 0) & (n == 0))
        def _():
            db_ref[...] = jnp.zeros_like(db_ref)

        @pl.when(n == 0)
        def _():
            carry[...] = jnp.zeros_like(carry)

        def write(first_block, position, count):
            row0 = pl.multiple_of(first_block * QB, QB)
            part = pl.ds(position * QB, count * QB)
            _write_columns([(sq.at[part], cols_q), (sk.at[part], cols_k), (sv.at[part], cols_v)],
                           out_ref, row0, sems)

        @pl.when(n == nsb)
        def _():
            sq[0:QB, :] = carry[:, 0:GW].astype(BF16)
            sk[0:QB, :] = carry[:, GW:2 * GW].astype(BF16)
            sv[0:QB, :] = carry[:, 2 * GW:3 * GW].astype(BF16)
            write((b + 1) * nsb * nq - 1, 0, 1)

        @pl.when(n < nsb)
        def _():
            def keys(cur_ref, prev_ref, h, j):
                sl = slice(h * HD, (h + 1) * HD)
                if j == 0:
                    return jnp.concatenate([prev_ref[:, sl], cur_ref[0:QB, sl]], axis=0)
                return cur_ref[(j - 1) * QB:(j + 1) * QB, sl]

            def block(ref, h, j):
                return ref[j * QB:(j + 1) * QB, h * HD:(h + 1) * HD]

            q = jnp.stack([block(q_ref, h, j) for h, j in units])
            do = jnp.stack([block(do_ref, h, j) for h, j in units])
            k = jnp.stack([keys(k_ref, kp_ref, h, j) for h, j in units])
            v = jnp.stack([keys(v_ref, vp_ref, h, j) for h, j in units])
            bias_b = jnp.stack([b_ref[jnp.minimum(n, 1), h] if j == 0 else b_ref[1, h] for h, j in units])
            lse = jnp.stack([st_ref[j * QB:(j + 1) * QB, h:h + 1] for h, j in units])
            delta = jnp.stack([st_ref[j * QB:(j + 1) * QB, 4 + h:5 + h] for h, j in units])
            s = jnp.einsum("uqd,ukd->uqk", q, k, preferred_element_type=F32) * SCALE + bias_b
            p = jnp.exp(s - lse)
            ds = p * (jnp.einsum("uqd,ukd->uqk", do, v, preferred_element_type=F32) - delta)
            for h in range(4):
                tot = ds[h * nq]
                for j in range(1, nq):
                    tot = tot + ds[h * nq + j]
                db_ref[h] += tot
            dsb, pb = ds.astype(BF16), p.astype(BF16)
            dq = jnp.einsum("uqk,ukd->uqd", dsb, k, preferred_element_type=F32) * SCALE
            dk = jnp.einsum("uqk,uqd->ukd", dsb, q, preferred_element_type=F32) * SCALE
            dv = jnp.einsum("uqk,uqd->ukd", pb, do, preferred_element_type=F32)
            for h in range(4):
                sl = slice(h * HD, (h + 1) * HD)
                u0, last = h * nq, h * nq + nq - 1
                sq[0:QB, sl] = carry[:, sl].astype(BF16)
                sk[0:QB, sl] = (carry[:, GW + h * HD:GW + (h + 1) * HD] + dk[u0, :QB]).astype(BF16)
                sv[0:QB, sl] = (carry[:, 2 * GW + h * HD:2 * GW + (h + 1) * HD] + dv[u0, :QB]).astype(BF16)
                for j in range(nq - 1):
                    pos = slice((j + 1) * QB, (j + 2) * QB)
                    sq[pos, sl] = dq[u0 + j].astype(BF16)
                    sk[pos, sl] = (dk[u0 + j, QB:] + dk[u0 + j + 1, :QB]).astype(BF16)
                    sv[pos, sl] = (dv[u0 + j, QB:] + dv[u0 + j + 1, :QB]).astype(BF16)
                carry[:, sl] = dq[last]
                carry[:, GW + h * HD:GW + (h + 1) * HD] = dk[last, QB:]
                carry[:, 2 * GW + h * HD:2 * GW + (h + 1) * HD] = dv[last, QB:]

            @pl.when(n == 0)
            def _():
                write(b * nsb * nq, 1, nq - 1)

            @pl.when(n > 0)
            def _():
                write((b * nsb + n) * nq - 1, 0, nq)

    def row(b, n):
        return b * nsb + jnp.minimum(n, nsb - 1)

    def prev(b, n):
        return jnp.maximum(row(b, n) * nq - 1, 0)

    in_specs = [
        pl.BlockSpec((rows, GW), lambda b, n: (row(b, n), CB_Q)),
        pl.BlockSpec((rows, GW), lambda b, n: (row(b, n), CB_K)),
        pl.BlockSpec((rows, GW), lambda b, n: (row(b, n), CB_V)),
        pl.BlockSpec((rows, GW), lambda b, n: (row(b, n), 0)),
        pl.BlockSpec((rows, 128), lambda b, n: (row(b, n), 0)),
        pl.BlockSpec((QB, GW), lambda b, n: (prev(b, n), CB_K)),
        pl.BlockSpec((QB, GW), lambda b, n: (prev(b, n), CB_V)),
        pl.BlockSpec((None, 2, 4, QB, 2 * QB), lambda b, n: (0, 0, 0, 0, 0)),
        pl.BlockSpec(memory_space=pl.ANY),
    ]
    return pl.pallas_call(
        body, name="attn_bwd0",
        grid=(BL, nsb + 1),
        in_specs=in_specs,
        out_specs=(pl.BlockSpec(memory_space=pl.ANY),
                   pl.BlockSpec((4, QB, 2 * QB), lambda b, n: (0, 0, 0))),
        out_shape=(jax.ShapeDtypeStruct((T, NCOL), BF16),
                   jax.ShapeDtypeStruct((4, QB, 2 * QB), F32)),
        scratch_shapes=[pltpu.VMEM((rows, GW), BF16)] * 3
                       + [pltpu.VMEM((QB, 3 * GW), F32), pltpu.SemaphoreType.DMA((3,))],
        input_output_aliases={8: 0},
        compiler_params=pltpu.CompilerParams(vmem_limit_bytes=VMEM_LIMIT),
    )(proj, proj, proj, d_out, stats, proj, proj, bias, dproj)


def _attention_forward(proj, rel_bias):
    buckets_np, masks_np = _bucket_maps()
    buckets, masks = jnp.asarray(buckets_np), jnp.asarray(masks_np)
    bias = _bias_expand(rel_bias, buckets, masks)
    fwd = [_attn_fwd_dense(proj, bias)] + [_attn_fwd(proj, bias, g) for g in (1, 2)]
    return bias, buckets, [f[0] for f in fwd], [f[1] for f in fwd]


def _local_step(x2, tgt2, mod3, h, proj, attn, w_ao, w_co, w_o, conv_w, conv_b, ln_g, ln_b):
    bias, buckets, o_g, lse_g = attn

    (dproj, dyc, d_o, stats, dxd, merged, dy, a_in, da_out, s_in, ds_out, tail_vec) = _tail(
        x2, tgt2, mod3, o_g, lse_g, proj, w_ao, w_co, w_o, conv_w, conv_b, ln_g, ln_b)

    dproj, db = _attn_bwd_dense(proj, d_o, stats, bias, dproj)
    dbias = [db]
    for g in (1, 2):
        dproj, db = _attn_bwd(proj, d_o, stats, bias, dproj, g)
        dbias.append(db)
    g_rel_bias = _bias_grad(*dbias, buckets)
    dproj, conv_vec = _conv_bwd(dyc, proj, conv_w, dproj)

    gw_o = _mm_tn(merged, dy, D, False, "gw_o")
    gw_co = _mm_tn(s_in, ds_out, D, False, "gw_conv_out")
    gw_ao = _mm_tn(a_in, da_out, D, False, "gw_attn_out")
    gw_ao = jnp.transpose(gw_ao.reshape(GW, N_DEV, D // N_DEV), (1, 0, 2))
    return dproj, dxd, gw_ao, gw_co, gw_o, conv_vec, g_rel_bias, tail_vec


def kernel(x, c, w_ada, b_ada, w_in, conv_w, conv_b, rel_bias, w_attn_out, w_conv_out, w_o, ln_g, ln_b, loss_target, m_w_ada, m_b_ada, m_w_in, m_conv_w, m_conv_b, m_rel_bias, m_w_attn_out, m_w_conv_out, m_w_o, m_ln_g, m_ln_b, v_w_ada, v_b_ada, v_w_in, v_conv_w, v_conv_b, v_rel_bias, v_w_attn_out, v_w_conv_out, v_w_o, v_ln_g, v_ln_b):
    me = _my_index()
    x2 = x.reshape(T, D)
    tgt2 = loss_target.reshape(T, D)

    b_cols = lax.dynamic_slice(b_ada, (0, me * ADA_SHARD), (1, ADA_SHARD))
    c_g, mod_in = _mod_exchange(jnp.pad(c, ((0, 8 - BL), (0, 0))), w_ada[0], b_cols)
    c_all = c_g[:, 0:BL, :].reshape(N_DEV * BL, D)
    mod3 = jnp.transpose(mod_in[:, 0:BL, :], (1, 0, 2)).reshape(BL, 3, D)

    h = _prep_h(x2, mod3)
    rows_shape = jax.ShapeDtypeStruct((N_DEV, D // N_DEV, D), BF16)
    proj, w_in_all, (w_ao_g, w_co_g, w_o_g, conv_w_g) = _gather_proj(
        _shard_order(), h, w_in[0].astype(BF16), 1024,
        ([w_attn_out[0].astype(BF16), w_conv_out[0].astype(BF16), w_o[0].astype(BF16), conv_w[0]],
         [jax.ShapeDtypeStruct((N_DEV, GW, D // N_DEV), BF16), rows_shape, rows_shape,
          jax.ShapeDtypeStruct((N_DEV, 3, D // N_DEV), F32)]))

    attn = _attention_forward(proj, rel_bias)
    w_ao_full = jnp.transpose(w_ao_g, (1, 0, 2)).reshape(GW, D)
    w_co_full = w_co_g.reshape(D, D)
    w_o_full = w_o_g.reshape(D, D)
    conv_w_full = jnp.transpose(conv_w_g, (1, 0, 2)).reshape(3, D)

    (dproj, dxd, gw_ao, gw_co, gw_o, conv_vec, g_rel_bias, tail_vec) = _local_step(
        x2, tgt2, mod3, h, proj, attn, w_ao_full, w_co_full, w_o_full,
        conv_w_full, conv_b, ln_g, ln_b)

    g_conv_w_blocks = jnp.transpose(conv_vec[0:3].reshape(3, N_DEV, D // N_DEV), (1, 0, 2))
    partials = [gw_ao, gw_co.reshape(N_DEV, D // N_DEV, D), gw_o.reshape(N_DEV, D // N_DEV, D), g_conv_w_blocks]
    w_in_sums, w_in_parts, sib = _gw_in_pair(
        _slice_order(), h, dproj, partials,
        [jax.ShapeDtypeStruct((4, GW, D // N_DEV), BF16),
         jax.ShapeDtypeStruct((4, D // N_DEV, D), BF16),
         jax.ShapeDtypeStruct((4, D // N_DEV, D), BF16),
         jax.ShapeDtypeStruct((4, 3, D // N_DEV), F32)])
    core = lax.axis_index("c").astype(jnp.int32).reshape(1)
    chip_sums = [w_in_sums] + list(_pair_add(core, partials, sib))
    hops = [(3,)] + [(1, 2, 3)] * 4
    grad_x, mod_vec, (r_in, r_ao, r_co, r_o, r_cw) = _dh_dx(
        dproj, w_in_all, x2, dxd, mod3, chip_sums, hops, w_in_parts)

    small = jnp.concatenate([
        tail_vec[0:4],
        jnp.pad(g_rel_bias.reshape(1, N_BUCKETS * N_HEADS), ((0, 0), (0, D - N_BUCKETS * N_HEADS))),
        jnp.zeros((3, D), F32)], axis=0)
    dmod = jnp.concatenate([mod_vec[0:2], mod_vec[2:4], tail_vec[4:6]], axis=1)
    small_g, dmod_g = _all_gather(
        [small, dmod],
        [jax.ShapeDtypeStruct((N_DEV, 8, D), F32), jax.ShapeDtypeStruct((N_DEV, BL, 3 * D), F32)],
        "gather_small")
    dmod_all = dmod_g.reshape(N_DEV * BL, 3 * D)
    small_names = ["b_ada", "conv_b", "rel_bias", "ln_g", "ln_b"]
    small_params = [(b_ada, m_b_ada, v_b_ada), (conv_b, m_conv_b, v_conv_b), (rel_bias, m_rel_bias, v_rel_bias),
                    (ln_g, m_ln_g, v_ln_g), (ln_b, m_ln_b, v_ln_b)]
    loss, small_res = _small_updates(
        small_g, dmod_all, small_g[:, 4, :N_BUCKETS * N_HEADS].reshape(N_DEV, N_BUCKETS, N_HEADS), small_params)
    loss = loss.reshape(())
    g_w_ada = _ada_bwd(jnp.transpose(c_all), lax.dynamic_slice(dmod_all, (0, me * ADA_SHARD),
                                                               (N_DEV * BL, ADA_SHARD)))

    def upd(parts, w, m, v, name, row_tile=None):
        shape = w.shape
        w2, m2, v2 = (t.reshape(parts.shape[1:]) for t in (w, m, v))
        return tuple(t.reshape(shape) for t in _adamw(parts, w2, m2, v2, name, row_tile))

    res = {
        "w_ada": upd(g_w_ada[None], w_ada, m_w_ada, v_w_ada, "adam_w_ada", 256),
        "w_in": upd(r_in, w_in, m_w_in, v_w_in, "adam_w_in", 128),
    }
    mid_names = ["conv_w", "w_attn_out", "w_conv_out", "w_o"]
    mid_parts = [r_cw, r_ao, r_co, r_o]
    mid_full = [(conv_w, m_conv_w, v_conv_w), (w_attn_out, m_w_attn_out, v_w_attn_out),
                (w_conv_out, m_w_conv_out, v_w_conv_out), (w_o, m_w_o, v_w_o)]
    mid_res = _multi_adamw(mid_parts, [tuple(t[0] for t in wmv) for wmv in mid_full], "adam_mid")
    for nm, wmv, outs4 in zip(mid_names, mid_full, mid_res):
        res[nm] = tuple(t[None] for t in outs4)
    res.update(dict(zip(small_names, small_res)))
    order = ["w_ada", "b_ada", "w_in", "conv_w", "conv_b", "rel_bias", "w_attn_out", "w_conv_out",
             "w_o", "ln_g", "ln_b"]
    outs = [loss, grad_x.reshape(BL, S, D)]
    for k in range(4):
        outs += [res[name][k] for name in order]
    return tuple(outs)
```

```python
import math

import numpy as np
import jax
import jax.numpy as jnp
from jax import lax
from jax.experimental import pallas as pl
from jax.experimental.pallas import tpu as pltpu

F32 = jnp.float32
BF16 = jnp.bfloat16
MESH = pl.DeviceIdType.MESH

N_DEV = 8
D = 1024
S = 2048
BL = 2
T = BL * S
NCOL = 11264
SHARD = NCOL // N_DEV
CB = 512
NCB = NCOL // CB
HD = 128
GW = 512
QB = 128
DILATIONS = (1, 4, 16)
N_STEPS = 128
N_BUCKETS = 32
N_HEADS = 12
ALPHA = 2.0 ** 0.25
LN_EPS = 1e-5
NEG_INF = -1e30
SCALE = HD ** -0.5
ADA_SHARD = 3 * D // N_DEV

CB_Q, CB_K, CB_V, CB_GA = 0, 3, 6, 9
KB_U, KB_BG, KB_CG, KB_GC, KB_MA, KB_MC = 5, 6, 7, 8, 9, 10

ADAM_LR, ADAM_B1, ADAM_B2, ADAM_EPS, ADAM_WD, ADAM_STEP = 0.001, 0.9, 0.999, 1e-08, 0.01, 10

VMEM_LIMIT = 56 * 1024 * 1024


def _dot(a, b):
    return jnp.dot(a, b, preferred_element_type=F32)


def _dot_nt(a, b):
    return lax.dot_general(a, b, (((1,), (1,)), ((), ())), preferred_element_type=F32)


def _dot_tn(a, b):
    return lax.dot_general(a, b, (((0,), (0,)), ((), ())), preferred_element_type=F32)


def _sigmoid(v):
    return 1.0 / (1.0 + jnp.exp(-v))


def _write_columns(pieces, dst_hbm, row0, sems):
    copies = []
    for k, (src, col0) in enumerate(pieces):
        rows, width = src.shape
        copies.append(pltpu.make_async_copy(
            src, dst_hbm.at[pl.ds(row0, rows), pl.ds(col0, width)], sems.at[k]))
    for cp in copies:
        cp.start()
    for cp in copies:
        cp.wait()


def _my_index():
    return 4 * lax.axis_index("x") + 2 * lax.axis_index("y") + lax.axis_index("c")


class _Gather:
    def __init__(self, ins, outs, stage, send_sems, recv_sems, local_sems):
        self.ins, self.outs, self.stage = ins, outs, stage
        self.send_sems, self.recv_sems, self.local_sems = send_sems, recv_sems, local_sems
        x, y, c = lax.axis_index("x"), lax.axis_index("y"), lax.axis_index("c")
        self.c = c
        self.me, self.sibling = (x, y, c), (x, y, 1 - c)
        self.chips = [(1 - x, y), (x, 1 - y), (1 - x, 1 - y)]

    @staticmethod
    def scratch(arrs):
        n = len(arrs)
        return ([pltpu.SemaphoreType.DMA((7 * n,)), pltpu.SemaphoreType.DMA((7 * n,)),
                 pltpu.SemaphoreType.DMA((n,))] + [pltpu.VMEM(a.shape, a.dtype) for a in arrs])

    def _copy(self, a, k, block, to, src=None):
        dst = self.outs[a].at[4 * block[0] + 2 * block[1] + block[2]]
        return pltpu.make_async_remote_copy(
            src_ref=dst if src is None else src, dst_ref=dst,
            send_sem=self.send_sems.at[a * 7 + k], recv_sem=self.recv_sems.at[a * 7 + k],
            device_id=to, device_id_type=MESH)

    def _first(self):
        first = []
        for a in range(len(self.ins)):
            first.append(self._copy(a, 0, self.me, self.sibling, src=self.ins[a]))
            first += [self._copy(a, 1 + j, self.me, (*chip, self.c), src=self.ins[a])
                      for j, chip in enumerate(self.chips)]
        return first

    def _mine(self):
        me = self.me
        return [pltpu.make_async_copy(self.stage[a], self.outs[a].at[4 * me[0] + 2 * me[1] + me[2]],
                                      self.local_sems.at[a]) for a in range(len(self.ins))]

    def begin(self):
        for cp in self._first():
            cp.start()
        loads = [pltpu.make_async_copy(self.ins[a], self.stage[a], self.local_sems.at[a])
                 for a in range(len(self.ins))]
        for cp in loads:
            cp.start()
        for cp in loads:
            cp.wait()
        for cp in self._mine():
            cp.start()

    def finish(self):
        n, c, me, sibling = len(self.ins), self.c, self.me, self.sibling
        passed = []
        for j, chip in enumerate(self.chips):
            for a in range(n):
                self._copy(a, 1 + j, (*chip, c), me).wait_recv()
                fwd = self._copy(a, 4 + j, (*chip, c), sibling)
                fwd.start()
                passed.append(fwd)
        for a in range(n):
            self._copy(a, 0, sibling, me).wait_recv()
        for j, chip in enumerate(self.chips):
            for a in range(n):
                self._copy(a, 4 + j, (*chip, 1 - c), me).wait_recv()
        for cp in self._first() + passed:
            cp.wait_send()
        for cp in self._mine():
            cp.wait()


def _all_gather(arrs, out_shapes, name):
    n = len(arrs)

    def body(*refs):
        g = _Gather(refs[:n], refs[n:2 * n], refs[2 * n + 3:], *refs[2 * n:2 * n + 3])
        g.begin()
        g.finish()

    any_spec = pl.BlockSpec(memory_space=pl.ANY)
    return pl.pallas_call(
        body, name=name,
        out_shape=tuple(out_shapes),
        in_specs=[any_spec] * n,
        out_specs=tuple([any_spec] * n),
        scratch_shapes=_Gather.scratch(arrs),
    )(*arrs)


def _slice_order():
    x, y, c = lax.axis_index("x"), lax.axis_index("y"), lax.axis_index("c")
    slots = []
    for q in (2 * (1 - x) + y, 2 * x + (1 - y), 2 * (1 - x) + (1 - y), 2 * x + y):
        slots += [2 * q + 1 - c, 2 * q + c]
    return jnp.stack(slots).astype(jnp.int32)


def _gw_in_pair(order, h, dproj, smalls, small_shapes4):
    kk, m = h.shape
    tk = min(kk, 2048)
    nk = kk // tk
    ncols = dproj.shape[1] // N_DEV
    n = len(smalls)

    def body(order_ref, h_ref, d_ref, *rest):
        ins = rest[:n]
        sums_hbm, parts_hbm = rest[n], rest[n + 1]
        sib = rest[n + 2:2 * n + 2]
        (acc, sendbuf, recvbuf, sumbuf, send_sems, recv_sems, local_sem, ssend, srecv,
         isend, irecv) = rest[2 * n + 2:]
        js, k = pl.program_id(0), pl.program_id(1)
        x, y, c = lax.axis_index("x"), lax.axis_index("y"), lax.axis_index("c")
        sibling = (x, y, 1 - c)
        my_chip = 2 * x + y
        near = [(1 - x, y, c), (x, 1 - y, c)]

        def ici_copy(p, out_chip):
            peer = near[p]
            return pltpu.make_async_remote_copy(
                src_ref=sumbuf.at[p], dst_ref=parts_hbm.at[out_chip],
                send_sem=isend.at[p], recv_sem=irecv.at[p], device_id=peer, device_id_type=MESH)

        def small_copies():
            return [pltpu.make_async_remote_copy(
                        src_ref=ins[a].at[2 * q + 1 - c], dst_ref=sib[a].at[q],
                        send_sem=ssend.at[a * 4 + q], recv_sem=srecv.at[a * 4 + q],
                        device_id=sibling, device_id_type=MESH)
                    for a in range(n) for q in range(4)]

        def slice_copy(p):
            return pltpu.make_async_remote_copy(
                src_ref=sendbuf, dst_ref=recvbuf.at[p], send_sem=send_sems.at[p], recv_sem=recv_sems.at[p],
                device_id=sibling, device_id_type=MESH)

        def sum_copy(p):
            return pltpu.make_async_copy(sumbuf.at[2], sums_hbm.at[order_ref[2 * p] // 2], local_sem)

        @pl.when((js == 0) & (k == 0))
        def _():
            for cp in small_copies():
                cp.start()

        @pl.when(k == 0)
        def _():
            acc[...] = jnp.zeros_like(acc)

        acc[...] += _dot_tn(h_ref[...], d_ref[...])

        for p in range(4):
            @pl.when((js == 2 * p) & (k == nk - 1))
            def _():
                if p > 0:
                    slice_copy(p - 1).wait_send()
                sendbuf[...] = acc[...].astype(BF16)
                slice_copy(p).start()

            @pl.when((js == 2 * p + 1) & (k == nk - 1))
            def _():
                slice_copy(p).wait_recv()
                if p == 3:
                    sum_copy(2).wait()
                sumbuf[min(p, 2)] = (acc[...] + recvbuf[p].astype(F32)).astype(BF16)
                if p < 2:
                    ici_copy(p, my_chip).start()
                else:
                    sum_copy(p).start()

        @pl.when((js == N_DEV - 1) & (k == nk - 1))
        def _():
            slice_copy(3).wait_send()
            sum_copy(3).wait()
            for cp in small_copies():
                cp.wait()
            for p in range(2):
                ici_copy(p, 2 * near[p][0] + near[p][1]).wait_recv()
                ici_copy(p, my_chip).wait_send()

    any_spec = pl.BlockSpec(memory_space=pl.ANY)
    res = pl.pallas_call(
        body, name="gw_in_pair",
        grid_spec=pltpu.PrefetchScalarGridSpec(
            num_scalar_prefetch=1,
            grid=(N_DEV, nk),
            in_specs=[pl.BlockSpec((tk, m), lambda js, k, order_ref: (k, 0)),
                      pl.BlockSpec((tk, ncols), lambda js, k, order_ref: (k, order_ref[js]))] + [any_spec] * n,
            out_specs=(any_spec,) * (n + 2),
            scratch_shapes=[pltpu.VMEM((m, ncols), F32), pltpu.VMEM((m, ncols), BF16),
                            pltpu.VMEM((4, m, ncols), BF16), pltpu.VMEM((3, m, ncols), BF16),
                            pltpu.SemaphoreType.DMA((4,)), pltpu.SemaphoreType.DMA((4,)),
                            pltpu.SemaphoreType.DMA,
                            pltpu.SemaphoreType.DMA((4 * n,)), pltpu.SemaphoreType.DMA((4 * n,)),
                            pltpu.SemaphoreType.DMA((2,)), pltpu.SemaphoreType.DMA((2,))]),
        out_shape=(jax.ShapeDtypeStruct((4, m, ncols), BF16),) * 2 + tuple(small_shapes4),
        compiler_params=pltpu.CompilerParams(vmem_limit_bytes=VMEM_LIMIT),
    )(order, h, dproj, *smalls)
    return res[0], res[1], res[2:]


def _chip_copies(ins, outs, send_sems, recv_sems, local_sems, hops):
    n = len(ins)
    x, y, c = lax.axis_index("x"), lax.axis_index("y"), lax.axis_index("c")
    my_chip = 2 * x + y

    def peer_of(k):
        return ((1 - x) if (k >> 1) & 1 else x, (1 - y) if k & 1 else y, c)

    def copy(a, k, out_chip):
        peer = peer_of(k)
        return pltpu.make_async_remote_copy(
            src_ref=ins[a].at[2 * peer[0] + peer[1]], dst_ref=outs[a].at[out_chip],
            send_sem=send_sems.at[a * 3 + k - 1], recv_sem=recv_sems.at[a * 3 + k - 1],
            device_id=peer, device_id_type=MESH)

    sends = [copy(a, k, my_chip) for k in range(1, 4) for a in range(n) if k in hops[a]]
    arrivals = []
    for k in range(1, 4):
        peer = peer_of(k)
        arrivals += [copy(a, k, 2 * peer[0] + peer[1]) for a in range(n) if k in hops[a]]
    mine = [pltpu.make_async_copy(ins[a].at[my_chip], outs[a].at[my_chip], local_sems.at[a])
            for a in range(n)]
    return sends, arrivals, mine


def _pair_add(core, mines, theirs):
    n = len(mines)

    def body(core_ref, *refs):
        mine, sib, outs = refs[:n], refs[n:2 * n], refs[2 * n:]
        for a in range(n):
            for q in range(4):
                outs[a][q] = (mine[a][2 * q + core_ref[0]].astype(F32)
                              + sib[a][q].astype(F32)).astype(outs[a].dtype)

    return pl.pallas_call(
        body, name="pair_add",
        in_specs=[pl.BlockSpec(memory_space=pltpu.SMEM)] + [pl.BlockSpec(memory_space=pltpu.VMEM)] * (2 * n),
        out_shape=tuple(jax.ShapeDtypeStruct(t.shape, t.dtype) for t in theirs),
    )(core, *mines, *theirs)


def _mod_exchange(c8, w_ada, b_cols):
    cols = w_ada.shape[1]

    def body(c_ref, w_ref, b_ref, call_ref, mod_ref, msend, send1, recv1, send2, recv2):
        x, y, c = lax.axis_index("x"), lax.axis_index("y"), lax.axis_index("c")
        my_slot = 4 * x + 2 * y + c

        def peer_of(k):
            return ((1 - x) if (k >> 2) & 1 else x, (1 - y) if (k >> 1) & 1 else y, (1 - c) if k & 1 else c)

        def slot_of(dev):
            return 4 * dev[0] + 2 * dev[1] + dev[2]

        def exchange(src_of, dst_ref, send_sems, recv_sems):
            sends, arrivals = [], []
            for k in range(1, 8):
                peer = peer_of(k)
                sends.append(pltpu.make_async_remote_copy(
                    src_ref=src_of(slot_of(peer)), dst_ref=dst_ref.at[my_slot],
                    send_sem=send_sems.at[k - 1], recv_sem=recv_sems.at[k - 1],
                    device_id=peer, device_id_type=MESH))
                arrivals.append(pltpu.make_async_remote_copy(
                    src_ref=src_of(my_slot), dst_ref=dst_ref.at[slot_of(peer)],
                    send_sem=send_sems.at[k - 1], recv_sem=recv_sems.at[k - 1],
                    device_id=peer, device_id_type=MESH))
            for cp in sends:
                cp.start()
            for cp in arrivals:
                cp.wait_recv()
            for cp in sends:
                cp.wait_send()

        call_ref[my_slot] = c_ref[...]
        exchange(lambda s: c_ref, call_ref, send1, recv1)
        cv = call_ref[...].reshape(N_DEV * 8, c_ref.shape[1])
        act = cv * _sigmoid(cv)
        mod = jnp.dot(act, w_ref[...], preferred_element_type=F32,
                      precision=lax.Precision.HIGHEST) + b_ref[...]
        msend[...] = mod.reshape(N_DEV, 8, cols)
        mod_ref[my_slot] = msend[my_slot]
        exchange(lambda s: msend.at[s], mod_ref, send2, recv2)

    return pl.pallas_call(
        body, name="mod_exchange",
        out_shape=(jax.ShapeDtypeStruct((N_DEV, 8, c8.shape[1]), F32),
                   jax.ShapeDtypeStruct((N_DEV, 8, cols), F32)),
        scratch_shapes=[pltpu.VMEM((N_DEV, 8, cols), F32)] + [pltpu.SemaphoreType.DMA((7,))] * 4,
    )(c8, w_ada, b_cols)


def _ada_bwd(c_all_t, dmod_cols):
    def body(c_ref, d_ref, o_ref):
        cv = c_ref[...]
        sc = cv * _sigmoid(cv)
        o_ref[...] = jnp.dot(sc, d_ref[...], preferred_element_type=F32,
                             precision=lax.Precision.HIGHEST)

    return pl.pallas_call(
        body, name="ada_bwd",
        out_shape=jax.ShapeDtypeStruct((c_all_t.shape[0], dmod_cols.shape[1]), F32),
    )(c_all_t, dmod_cols)


def _shard_order():
    x, y, c = lax.axis_index("x"), lax.axis_index("y"), lax.axis_index("c")
    devs = [(x, y, c), (x, y, 1 - c)]
    for chip in [(1 - x, y), (x, 1 - y), (1 - x, 1 - y)]:
        devs += [(*chip, c), (*chip, 1 - c)]
    return jnp.stack([4 * d[0] + 2 * d[1] + d[2] for d in devs]).astype(jnp.int32)


def _prep_h(x2, mod3):
    ts = 512
    per_seq = S // ts

    def body(x_ref, mod_ref, h_ref):
        shift = mod_ref[0, 0:1, :]
        scale = mod_ref[0, 1:2, :]
        h_ref[...] = (x_ref[...] * (1.0 + scale) + shift).astype(BF16)

    return pl.pallas_call(
        body, name="prep_h",
        grid=(T // ts,),
        in_specs=[pl.BlockSpec((ts, D), lambda i: (i, 0)),
                  pl.BlockSpec((1, 3, D), lambda i: (i // per_seq, 0, 0))],
        out_specs=pl.BlockSpec((ts, D), lambda i: (i, 0)),
        out_shape=jax.ShapeDtypeStruct((T, D), BF16),
    )(x2, mod3)


def _gather_proj(order, h, w_shard, tm, ride=()):
    rows, kdim = h.shape
    ncols = w_shard.shape[1]
    n_i = rows // tm
    ride_arrs, ride_shapes = ride if ride else ((), ())
    n_ride = len(ride_arrs)

    def body(order_ref, h_ref, mine_hbm, *rest):
        ride_ins = rest[:n_ride]
        o_ref, all_hbm = rest[n_ride:n_ride + 2]
        ride_outs = rest[n_ride + 2:2 * n_ride + 2]
        wv, send_sems, recv_sems, local_sems = rest[2 * n_ride + 2:2 * n_ride + 6]
        ride_scr = rest[2 * n_ride + 6:]
        j, i = pl.program_id(0), pl.program_id(1)
        x, y, c = lax.axis_index("x"), lax.axis_index("y"), lax.axis_index("c")
        me, sibling = (x, y, c), (x, y, 1 - c)
        chips = [(1 - x, y), (x, 1 - y), (1 - x, 1 - y)]

        def slot(dev):
            return 4 * dev[0] + 2 * dev[1] + dev[2]

        def copy(k, block, to):
            return pltpu.make_async_remote_copy(
                src_ref=wv.at[slot(block)], dst_ref=wv.at[slot(block)],
                send_sem=send_sems.at[k], recv_sem=recv_sems.at[k],
                device_id=to, device_id_type=MESH)

        def keep(step, block):
            return pltpu.make_async_copy(wv.at[slot(block)], all_hbm.at[slot(block)], local_sems.at[step])

        if n_ride:
            gather = _Gather(ride_ins, ride_outs, ride_scr[3:], *ride_scr[:3])
        first = [copy(0, me, sibling)] + [copy(1 + q, me, (*chip, c)) for q, chip in enumerate(chips)]
        passed = [copy(4 + q, (*chip, c), sibling) for q, chip in enumerate(chips)]
        due = [(me, None, None), (sibling, copy(0, sibling, me), None)]
        for q, chip in enumerate(chips):
            due.append(((*chip, c), copy(1 + q, (*chip, c), me), passed[q]))
            due.append(((*chip, 1 - c), copy(4 + q, (*chip, 1 - c), me), None))

        @pl.when((j == 0) & (i == 0))
        def _():
            load = pltpu.make_async_copy(mine_hbm, wv.at[slot(me)], local_sems.at[N_DEV])
            load.start()
            load.wait()
            for cp in first:
                cp.start()
            keep(0, me).start()

        for step in range(1, N_DEV):
            block, arrival, forward = due[step]

            @pl.when((j == step) & (i == 0))
            def _():
                arrival.wait_recv()
                if forward is not None:
                    forward.start()
                keep(step, block).start()
                if n_ride and step == N_DEV - 2:
                    gather.begin()

        o_ref[...] = _dot(h_ref[...], wv[order_ref[j]]).astype(BF16)

        @pl.when((j == N_DEV - 1) & (i == n_i - 1))
        def _():
            for cp in first + passed:
                cp.wait_send()
            for step in range(N_DEV):
                keep(step, due[step][0]).wait()
            if n_ride:
                gather.finish()

    any_spec = pl.BlockSpec(memory_space=pl.ANY)
    res = pl.pallas_call(
        body, name="gather_proj",
        grid_spec=pltpu.PrefetchScalarGridSpec(
            num_scalar_prefetch=1,
            grid=(N_DEV, n_i),
            in_specs=[pl.BlockSpec((tm, kdim), lambda j, i, order_ref: (i, 0)), any_spec] + [any_spec] * n_ride,
            out_specs=(pl.BlockSpec((tm, ncols), lambda j, i, order_ref: (i, order_ref[j])), any_spec)
                      + (any_spec,) * n_ride,
            scratch_shapes=[pltpu.VMEM((N_DEV, kdim, ncols), BF16),
                            pltpu.SemaphoreType.DMA((7,)), pltpu.SemaphoreType.DMA((7,)),
                            pltpu.SemaphoreType.DMA((N_DEV + 1,))]
                           + (_Gather.scratch(ride_arrs) if n_ride else [])),
        out_shape=(jax.ShapeDtypeStruct((rows, N_DEV * ncols), BF16),
                   jax.ShapeDtypeStruct((N_DEV, kdim, ncols), BF16)) + tuple(ride_shapes),
        compiler_params=pltpu.CompilerParams(vmem_limit_bytes=VMEM_LIMIT),
    )(order, h, w_shard, *ride_arrs)
    return res[0], res[1], res[2:]


def _bucket_maps():
    a = np.arange(QB)[:, None]
    b = np.arange(2 * QB)[None, :]
    steps = a + QB - b
    maps = []
    for dil in DILATIONS:
        dist = np.maximum(steps, 0) * dil
        nf = np.maximum(dist, 1).astype(np.float32)
        large = 16 + (np.log(nf / np.float32(16)) / np.float32(math.log(128.0))
                      * np.float32(16)).astype(np.int32)
        large = np.minimum(large, N_BUCKETS - 1)
        maps.append(np.where(dist < 16, dist, large).astype(np.int32))
    band = (steps >= 0) & (steps <= N_STEPS)
    first = band & (b >= QB)
    masks = np.stack([first, band]).astype(np.int32)
    return np.stack(maps), masks


def _bias_expand(rel_bias, buckets, masks):
    def body(tab_ref, bk_ref, mk_ref, o_ref):
        for g in range(3):
            bk = bk_ref[g]
            for h in range(4):
                col = 4 * g + h
                val = jnp.zeros((QB, 2 * QB), F32)
                for k in range(N_BUCKETS):
                    val = jnp.where(bk == k, tab_ref[k, col], val)
                o_ref[g, 0, h] = jnp.where(mk_ref[0] != 0, val, NEG_INF)
                o_ref[g, 1, h] = jnp.where(mk_ref[1] != 0, val, NEG_INF)

    return pl.pallas_call(
        body, name="bias_expand",
        in_specs=[pl.BlockSpec(memory_space=pltpu.SMEM),
                  pl.BlockSpec(memory_space=pltpu.VMEM),
                  pl.BlockSpec(memory_space=pltpu.VMEM)],
        out_shape=jax.ShapeDtypeStruct((3, 2, 4, QB, 2 * QB), F32),
    )(rel_bias, buckets, masks)


def _bias_grad(ds1, ds2, ds3, buckets):
    def body(d1_ref, d2_ref, d3_ref, bk_ref, o_ref):
        for g, d_ref in enumerate((d1_ref, d2_ref, d3_ref)):
            bk = bk_ref[g]
            for h in range(4):
                dv = d_ref[h]
                for k in range(N_BUCKETS):
                    o_ref[k, 4 * g + h] = jnp.sum(jnp.where(bk == k, dv, 0.0))

    return pl.pallas_call(
        body, name="bias_grad",
        in_specs=[pl.BlockSpec(memory_space=pltpu.VMEM)] * 4,
        out_specs=pl.BlockSpec(memory_space=pltpu.SMEM),
        out_shape=jax.ShapeDtypeStruct((N_BUCKETS, N_HEADS), F32),
    )(ds1, ds2, ds3, buckets)


def _scratch_sets(rows):
    return 4 if rows <= 512 else 1


def _unit_chunks(dil, size=16):
    units = [(h, r) for h in range(4) for r in range(dil)]
    return [units[i:i + size] for i in range(0, len(units), size)]


def _residue_rows(src_ref, copies, h, residue):
    buf = copies[h % len(copies)]
    buf[...] = src_ref[:, h * HD:(h + 1) * HD].astype(F32)
    return lambda r: buf[residue(r), :].astype(BF16)


def _attn_fwd(proj, bias, g):
    dil = DILATIONS[g]
    rows = QB * dil
    nsb = S // rows
    has_prev = nsb > 1

    def residue(r):
        return pl.ds(r, QB, stride=dil)

    n_sets = _scratch_sets(rows)
    n_in = 6 if has_prev else 4
    n_copied = (4 + (2 if has_prev else 0)) * n_sets

    def body(*refs):
        q_ref, kc_ref, vc_ref = refs[:3]
        kp_ref, vp_ref = refs[3:5] if has_prev else (None, None)
        b_ref = refs[n_in - 1]
        o_ref, l_ref = refs[n_in:n_in + 2]
        scr = list(refs[n_in + 2:])
        ls = [scr.pop(0) for _ in range(4)]
        copies = {name: [scr.pop(0) for _ in range(n_sets)]
                  for name in ("q", "kc", "vc", "o") + (("kp", "vp") if has_prev else ())}
        lane = lax.broadcasted_iota(jnp.int32, (QB, 128), 1)
        refs_of = {"q": q_ref, "kc": kc_ref, "vc": vc_ref, "kp": kp_ref, "vp": vp_ref}
        for chunk in _unit_chunks(dil):
            rows_of = {h: {name: _residue_rows(refs_of[name], copies[name], h, residue)
                           for name in refs_of if refs_of[name] is not None}
                       for h in sorted({h for h, _ in chunk})}

            def batch(name):
                return jnp.stack([rows_of[h][name](r) for h, r in chunk])

            q, k, v = batch("q"), batch("kc"), batch("vc")
            if has_prev:
                k = jnp.concatenate([batch("kp"), k], axis=1)
                v = jnp.concatenate([batch("vp"), v], axis=1)
                bias_b = jnp.stack([b_ref[h] for h, _ in chunk])
            else:
                bias_b = jnp.stack([b_ref[h, :, QB:] for h, _ in chunk])
            s = jnp.einsum("uqd,ukd->uqk", q, k, preferred_element_type=F32) * SCALE + bias_b
            m = jnp.max(s, axis=-1, keepdims=True)
            p = jnp.exp(s - m)
            l = jnp.sum(p, axis=-1, keepdims=True)
            o = jnp.einsum("uqk,ukd->uqd", p.astype(BF16), v, preferred_element_type=F32) / l
            lse = m + jnp.log(l)
            for i, (h, r) in enumerate(chunk):
                copies["o"][h % n_sets][residue(r), :] = o[i]
                ls[h][r * QB:(r + 1) * QB, :] = jnp.where(lane == h, lse[i], 0.0)
            for h in sorted({h for h, _ in chunk}):
                o_ref[:, h * HD:(h + 1) * HD] = copies["o"][h % n_sets][...]
        for r in range(dil):
            blk = slice(r * QB, (r + 1) * QB)
            l_ref[residue(r), :] = (ls[0][blk, :] + ls[1][blk, :]) + (ls[2][blk, :] + ls[3][blk, :])

    def row(b, n):
        return b * nsb + n

    def prev(b, n):
        return b * nsb + jnp.maximum(n - 1, 0)

    in_specs = [
        pl.BlockSpec((rows, GW), lambda b, n: (row(b, n), CB_Q + g)),
        pl.BlockSpec((rows, GW), lambda b, n: (row(b, n), CB_K + g)),
        pl.BlockSpec((rows, GW), lambda b, n: (row(b, n), CB_V + g)),
    ]
    args = [proj, proj, proj]
    scratch = [pltpu.VMEM((rows, 128), F32)] * (4 + n_copied)
    if has_prev:
        in_specs += [pl.BlockSpec((rows, GW), lambda b, n: (prev(b, n), CB_K + g)),
                     pl.BlockSpec((rows, GW), lambda b, n: (prev(b, n), CB_V + g))]
        args += [proj, proj]
    in_specs.append(pl.BlockSpec((None, None, 4, QB, 2 * QB),
                                 lambda b, n: (g, jnp.minimum(n, 1), 0, 0, 0)))
    args.append(bias)
    return pl.pallas_call(
        body, name=f"attn_fwd{g}",
        grid=(BL, nsb),
        in_specs=in_specs,
        out_specs=(pl.BlockSpec((rows, GW), lambda b, n: (row(b, n), 0)),
                   pl.BlockSpec((rows, 128), lambda b, n: (row(b, n), 0))),
        out_shape=(jax.ShapeDtypeStruct((T, GW), F32), jax.ShapeDtypeStruct((T, 128), F32)),
        scratch_shapes=scratch,
        compiler_params=pltpu.CompilerParams(vmem_limit_bytes=VMEM_LIMIT),
    )(*args)


def _attn_bwd(proj, d_out, stats, bias, dproj, g):
    dil = DILATIONS[g]
    rows = QB * dil
    nsb = S // rows
    has_prev = nsb > 1
    n_steps = nsb + 1 if has_prev else 1
    n_in = 7 + (2 if has_prev else 0)

    def residue(r):
        return pl.ds(r, QB, stride=dil)

    n_sets = _scratch_sets(rows)

    def body(*refs):
        q_ref, kc_ref, vc_ref, do_ref, st_ref, b_ref = refs[:6]
        kp_ref, vp_ref = refs[6:8] if has_prev else (None, None)
        out_ref, db_ref = refs[n_in], refs[n_in + 1]
        scr = list(refs[n_in + 2:])
        sq, sk, sv, sems = [scr.pop(0) for _ in range(4)]
        carry = scr.pop(0) if has_prev else None
        sts = scr.pop(0)
        copies = {name: [scr.pop(0) for _ in range(n_sets)]
                  for name in ("q", "kc", "vc", "do", "dq", "dk", "dv") + (("kp", "vp") if has_prev else ())}
        b, n = pl.program_id(0), pl.program_id(1)

        @pl.when((b == 0) & (n == 0))
        def _():
            db_ref[...] = jnp.zeros_like(db_ref)

        def finish(h, r, dq, dk, dv):
            for name, val in (("dq", dq), ("dk", dk), ("dv", dv)):
                copies[name][h % n_sets][residue(r), :] = val

        def finish_head(h):
            sl = slice(h * HD, (h + 1) * HD)
            sq[:, sl] = copies["dq"][h % n_sets][...].astype(BF16)
            sk[:, sl] = copies["dk"][h % n_sets][...].astype(BF16)
            sv[:, sl] = copies["dv"][h % n_sets][...].astype(BF16)

        def write_block(blk_idx):
            row0 = pl.multiple_of(blk_idx * rows, rows)
            _write_columns([(sq, CB * (CB_Q + g)), (sk, CB * (CB_K + g)), (sv, CB * (CB_V + g))],
                           out_ref, row0, sems)

        def carried(h, r):
            blk = slice(r * QB, (r + 1) * QB)
            return ((blk, slice(h * HD, (h + 1) * HD)), (blk, slice(GW + h * HD, GW + (h + 1) * HD)),
                    (blk, slice(2 * GW + h * HD, 2 * GW + (h + 1) * HD)))

        if has_prev:
            @pl.when(n == 0)
            def _():
                carry[...] = jnp.zeros_like(carry)

            @pl.when(n == nsb)
            def _():
                for h in range(4):
                    for r in range(dil):
                        cq, ck, cv = carried(h, r)
                        finish(h, r, carry[cq], carry[ck], carry[cv])
                    finish_head(h)
                write_block(b * nsb + nsb - 1)

        @pl.when(n < nsb)
        def _():
            for r in range(dil):
                sts[r * QB:(r + 1) * QB, :] = st_ref[residue(r), :]
            refs_of = {"q": q_ref, "kc": kc_ref, "vc": vc_ref, "do": do_ref, "kp": kp_ref, "vp": vp_ref}
            for chunk in _unit_chunks(dil):
                heads = sorted({h for h, _ in chunk})
                rows_of = {h: {name: _residue_rows(refs_of[name], copies[name], h, residue)
                               for name in refs_of if refs_of[name] is not None}
                           for h in heads}

                def batch(name):
                    return jnp.stack([rows_of[h][name](r) for h, r in chunk])

                q, k, v, do = batch("q"), batch("kc"), batch("vc"), batch("do")
                if has_prev:
                    k = jnp.concatenate([batch("kp"), k], axis=1)
                    v = jnp.concatenate([batch("vp"), v], axis=1)
                    bias_b = jnp.stack([b_ref[h] for h, _ in chunk])
                else:
                    bias_b = jnp.stack([b_ref[h, :, QB:] for h, _ in chunk])
                lse = jnp.stack([sts[r * QB:(r + 1) * QB, h:h + 1] for h, r in chunk])
                delta = jnp.stack([sts[r * QB:(r + 1) * QB, 4 + h:5 + h] for h, r in chunk])
                s = jnp.einsum("uqd,ukd->uqk", q, k, preferred_element_type=F32) * SCALE + bias_b
                p = jnp.exp(s - lse)
                ds = p * (jnp.einsum("uqd,ukd->uqk", do, v, preferred_element_type=F32) - delta)
                for h in heads:
                    mine = [ds[i] for i, (hh, _) in enumerate(chunk) if hh == h]
                    tot = mine[0]
                    for extra in mine[1:]:
                        tot = tot + extra
                    if has_prev:
                        db_ref[h] += tot
                    else:
                        db_ref[h, :, QB:] += tot
                dsb, pb = ds.astype(BF16), p.astype(BF16)
                dq = jnp.einsum("uqk,ukd->uqd", dsb, k, preferred_element_type=F32) * SCALE
                dk = jnp.einsum("uqk,uqd->ukd", dsb, q, preferred_element_type=F32) * SCALE
                dv = jnp.einsum("uqk,uqd->ukd", pb, do, preferred_element_type=F32)
                for i, (h, r) in enumerate(chunk):
                    if has_prev:
                        cq, ck, cv = carried(h, r)
                        finish(h, r, carry[cq], carry[ck] + dk[i, :QB], carry[cv] + dv[i, :QB])
                        carry[cq] = dq[i]
                        carry[ck] = dk[i, QB:]
                        carry[cv] = dv[i, QB:]
                    else:
                        finish(h, r, dq[i], dk[i], dv[i])
                for h in heads:
                    finish_head(h)
            if has_prev:
                @pl.when(n > 0)
                def _():
                    write_block(b * nsb + n - 1)
            else:
                write_block(b)

    def row(b, n):
        return b * nsb + jnp.minimum(n, nsb - 1)

    def prev(b, n):
        return b * nsb + jnp.maximum(jnp.minimum(n, nsb - 1) - 1, 0)

    in_specs = [
        pl.BlockSpec((rows, GW), lambda b, n: (row(b, n), CB_Q + g)),
        pl.BlockSpec((rows, GW), lambda b, n: (row(b, n), CB_K + g)),
        pl.BlockSpec((rows, GW), lambda b, n: (row(b, n), CB_V + g)),
        pl.BlockSpec((rows, GW), lambda b, n: (row(b, n), 0)),
        pl.BlockSpec((rows, 128), lambda b, n: (row(b, n), 0)),
        pl.BlockSpec((None, None, 4, QB, 2 * QB),
                     lambda b, n: (g, jnp.minimum(jnp.minimum(n, nsb - 1), 1), 0, 0, 0)),
    ]
    args = [proj, proj, proj, d_out, stats, bias]
    scratch = [pltpu.VMEM((rows, GW), BF16)] * 3 + [pltpu.SemaphoreType.DMA((3,))]
    if has_prev:
        in_specs += [pl.BlockSpec((rows, GW), lambda b, n: (prev(b, n), CB_K + g)),
                     pl.BlockSpec((rows, GW), lambda b, n: (prev(b, n), CB_V + g))]
        args += [proj, proj]
        scratch.append(pltpu.VMEM((rows, 3 * GW), F32))
    n_copied = (7 + (2 if has_prev else 0)) * n_sets
    scratch += [pltpu.VMEM((rows, 128), F32)] * (1 + n_copied)
    in_specs.append(pl.BlockSpec(memory_space=pl.ANY))
    args.append(dproj)
    return pl.pallas_call(
        body, name=f"attn_bwd{g}",
        grid=(BL, n_steps),
        in_specs=in_specs,
        out_specs=(pl.BlockSpec(memory_space=pl.ANY),
                   pl.BlockSpec((4, QB, 2 * QB), lambda b, n: (0, 0, 0))),
        out_shape=(jax.ShapeDtypeStruct((T, NCOL), BF16),
                   jax.ShapeDtypeStruct((4, QB, 2 * QB), F32)),
        scratch_shapes=scratch,
        input_output_aliases={len(args) - 1: 0},
        compiler_params=pltpu.CompilerParams(vmem_limit_bytes=VMEM_LIMIT),
    )(*args)


def _tail(x2, tgt2, mod3, o_g, lse_g, proj, w_ao, w_co, w_o, conv_w, conv_b, ln_g, ln_b):
    tm = 256
    per_seq = S // tm
    halo = 16

    def body(x_ref, t_ref, mod_ref, o1_ref, o2_ref, o3_ref, l1_ref, l2_ref, l3_ref,
             ga_ref, u_ref, bg_ref, cg_ref, gc_ref, ma_ref, mc_ref, up_ref, cp_ref,
             wao_ref, wco_ref, wo_ref, cw_ref, cb_ref, lg_ref, lb_ref,
             dproj_ref, dyc_ref, do_ref, st_ref, dxd_ref,
             mg_ref, dy_ref, ain_ref, dao_ref, sin_ref, dso_ref, vec_ref,
             dga_s, dbg_s, dgm_s, sems):
        i = pl.program_id(0)
        bidx = i // per_seq
        first = (i % per_seq) == 0

        @pl.when(i == 0)
        def _():
            vec_ref[...] = jnp.zeros_like(vec_ref)

        l1, l2, l3 = l1_ref[...], l2_ref[...], l3_ref[...]
        mx = jnp.maximum(jnp.maximum(l1, l2), l3)
        e1, e2, e3 = jnp.exp(l1 - mx), jnp.exp(l2 - mx), jnp.exp(l3 - mx)
        esum = e1 + e2 + e3
        lse_tot = mx + jnp.log(esum)
        w1, w2, w3 = e1 / esum, e2 / esum, e3 / esum

        def per_head(wv):
            return jnp.concatenate([jnp.broadcast_to(wv[:, h:h + 1], (tm, HD)) for h in range(4)], axis=1)

        o = per_head(w1) * o1_ref[...] + per_head(w2) * o2_ref[...] + per_head(w3) * o3_ref[...]

        ga = ga_ref[...].astype(F32)
        sig_ga = _sigmoid(ga)
        silu_ga = ga * sig_ga
        a_in = (o * silu_ga).astype(BF16)
        a_out = _dot(a_in, wao_ref[...])

        u = u_ref[...].astype(F32)
        cg = cg_ref[...].astype(F32)
        z = cg * u
        zp = cp_ref[...].astype(F32) * up_ref[...].astype(F32)
        zp = jnp.where(first, 0.0, zp)
        zcat = jnp.concatenate([zp, z], axis=0)
        z1 = pltpu.roll(zcat, 1, 0)[halo:]
        z2 = pltpu.roll(zcat, 2, 0)[halo:]
        y_conv = cw_ref[0:1, :] * z2 + cw_ref[1:2, :] * z1 + cw_ref[2:3, :] * z + cb_ref[...]
        gc = gc_ref[...].astype(F32)
        sig_gc = _sigmoid(gc)
        silu_gc = gc * sig_gc
        bg = bg_ref[...].astype(F32)
        bg_yc = bg * y_conv
        s_in = (bg_yc * silu_gc).astype(BF16)
        s_out = _dot(s_in, wco_ref[...])

        sa = _sigmoid(ma_ref[...].astype(F32))
        sc = _sigmoid(mc_ref[...].astype(F32))
        merged = (sa * a_out + sc * s_out).astype(BF16)
        y = _dot(merged, wo_ref[...])
        gate1 = 1.0 + mod_ref[0, 2:3, :]
        xv = x_ref[...]
        resid = ALPHA * xv + gate1 * y
        mu = jnp.mean(resid, axis=1, keepdims=True)
        xc = resid - mu
        var = jnp.mean(xc * xc, axis=1, keepdims=True)
        rstd = lax.rsqrt(var + LN_EPS)
        xhat = xc * rstd
        lg = lg_ref[...]
        err = xhat * lg + lb_ref[...] - t_ref[...]
        vec_ref[3:4, :] += (0.5 / D) * jnp.sum(err * err, axis=0, keepdims=True)

        vec_ref[1:2, :] += (1.0 / D) * jnp.sum(err * xhat, axis=0, keepdims=True)
        vec_ref[2:3, :] += (1.0 / D) * jnp.sum(err, axis=0, keepdims=True)
        dxh = err * (lg * (1.0 / D))
        dres = rstd * (dxh - jnp.mean(dxh, axis=1, keepdims=True)
                       - xhat * jnp.mean(dxh * xhat, axis=1, keepdims=True))
        dxd_ref[...] = ALPHA * dres
        dgate = jnp.sum(dres * y, axis=0, keepdims=True)
        vec_ref[4:5, :] += jnp.where(bidx == 0, dgate, 0.0)
        vec_ref[5:6, :] += jnp.where(bidx == 1, dgate, 0.0)
        dy = (dres * gate1).astype(BF16)

        dmerged = _dot_nt(dy, wo_ref[...])
        da_out_f = dmerged * sa
        ds_out_f = dmerged * sc
        da_out = da_out_f.astype(BF16)
        ds_out = ds_out_f.astype(BF16)
        dgm_s[:, 2 * D:3 * D] = (ds_out_f * s_out * (1.0 - sc)).astype(BF16)
        dgm_s[:, D:2 * D] = (da_out_f * a_out * (1.0 - sa)).astype(BF16)
        da_in = _dot_nt(da_out, wao_ref[...])
        ds_in = _dot_nt(ds_out, wco_ref[...])

        d_o = da_in * silu_ga
        do_ref[...] = d_o.astype(BF16)
        dga_s[...] = (da_in * o * (sig_ga + silu_ga * (1.0 - sig_ga))).astype(BF16)
        lane = lax.broadcasted_iota(jnp.int32, (tm, 128), 1)
        stats = lse_tot
        od = o * d_o
        for h in range(4):
            delta = jnp.sum(od[:, h * HD:(h + 1) * HD], axis=1, keepdims=True)
            stats = jnp.where(lane == 4 + h, delta, stats)
        st_ref[...] = stats

        ds_silu = ds_in * silu_gc
        dbg_s[...] = (ds_silu * y_conv).astype(BF16)
        dyc = ds_silu * bg
        dyc_ref[...] = dyc
        vec_ref[0:1, :] += jnp.sum(dyc, axis=0, keepdims=True)
        dgm_s[:, 0:D] = (ds_in * bg_yc * (sig_gc + silu_gc * (1.0 - sig_gc))).astype(BF16)

        mg_ref[...] = merged
        dy_ref[...] = dy
        ain_ref[...] = a_in
        dao_ref[...] = da_out
        sin_ref[...] = s_in
        dso_ref[...] = ds_out
        _write_columns([(dga_s, CB * CB_GA), (dbg_s, D * KB_BG), (dgm_s, D * KB_GC)],
                       dproj_ref, pl.multiple_of(i * tm, tm), sems)

    def tile(width, cblk=0):
        return pl.BlockSpec((tm, width), lambda i: (i, cblk))

    def whole(shape):
        return pl.BlockSpec(shape, lambda i: tuple(0 for _ in shape))

    prev_rows = lambda i: (jnp.maximum(i * (tm // halo) - 1, 0),)
    in_specs = [
        tile(D), tile(D), pl.BlockSpec((1, 3, D), lambda i: (i // per_seq, 0, 0)),
        tile(GW), tile(GW), tile(GW), tile(128), tile(128), tile(128),
        tile(GW, CB_GA), tile(D, KB_U), tile(D, KB_BG), tile(D, KB_CG), tile(D, KB_GC),
        tile(D, KB_MA), tile(D, KB_MC),
        pl.BlockSpec((halo, D), lambda i: (*prev_rows(i), KB_U)),
        pl.BlockSpec((halo, D), lambda i: (*prev_rows(i), KB_CG)),
        whole((GW, D)), whole((D, D)), whole((D, D)),
        whole((3, D)), whole((1, D)), whole((1, D)), whole((1, D)),
    ]
    out_specs = (
        pl.BlockSpec(memory_space=pl.ANY), tile(D), tile(GW), tile(128), tile(D),
        tile(D), tile(D), tile(GW), tile(D), tile(D), tile(D),
        pl.BlockSpec((8, D), lambda i: (0, 0)),
    )
    out_shape = (
        jax.ShapeDtypeStruct((T, NCOL), BF16),
        jax.ShapeDtypeStruct((T, D), F32),
        jax.ShapeDtypeStruct((T, GW), BF16),
        jax.ShapeDtypeStruct((T, 128), F32),
        jax.ShapeDtypeStruct((T, D), F32),
        jax.ShapeDtypeStruct((T, D), BF16),
        jax.ShapeDtypeStruct((T, D), BF16),
        jax.ShapeDtypeStruct((T, GW), BF16),
        jax.ShapeDtypeStruct((T, D), BF16),
        jax.ShapeDtypeStruct((T, D), BF16),
        jax.ShapeDtypeStruct((T, D), BF16),
        jax.ShapeDtypeStruct((8, D), F32),
    )
    return pl.pallas_call(
        body, name="tail",
        grid=(T // tm,),
        in_specs=in_specs, out_specs=out_specs, out_shape=out_shape,
        scratch_shapes=[pltpu.VMEM((tm, GW), BF16), pltpu.VMEM((tm, D), BF16), pltpu.VMEM((tm, 3 * D), BF16),
                        pltpu.SemaphoreType.DMA((3,))],
        compiler_params=pltpu.CompilerParams(vmem_limit_bytes=VMEM_LIMIT),
    )(x2, tgt2, mod3, *o_g, *lse_g, proj, proj, proj, proj, proj, proj, proj, proj, proj,
      w_ao, w_co, w_o, conv_w, conv_b, ln_g, ln_b)


def _conv_bwd(dyc, proj, conv_w, dproj):
    tm = 512
    per_seq = S // tm

    def body(d_ref, dn_ref, u_ref, c_ref, cw_ref, _, dproj_ref, g_ref, du_s, dc_s, sems):
        i = pl.program_id(0)
        last = (i % per_seq) == per_seq - 1

        @pl.when(i == 0)
        def _():
            g_ref[...] = jnp.zeros_like(g_ref)

        d = d_ref[...]
        dn = jnp.where(last, 0.0, dn_ref[...])
        dcat = jnp.concatenate([d, dn], axis=0)
        d1 = pltpu.roll(dcat, tm + 8 - 1, 0)[:tm]
        d2 = pltpu.roll(dcat, tm + 8 - 2, 0)[:tm]
        dz = cw_ref[2:3, :] * d + cw_ref[1:2, :] * d1 + cw_ref[0:1, :] * d2
        u = u_ref[...].astype(F32)
        cg = c_ref[...].astype(F32)
        du_s[...] = (dz * cg).astype(BF16)
        dc_s[...] = (dz * u).astype(BF16)
        _write_columns([(du_s, D * KB_U), (dc_s, D * KB_CG)], dproj_ref, pl.multiple_of(i * tm, tm), sems)

        z = cg * u
        g_ref[0:1, :] += jnp.sum(d2 * z, axis=0, keepdims=True)
        g_ref[1:2, :] += jnp.sum(d1 * z, axis=0, keepdims=True)
        g_ref[2:3, :] += jnp.sum(d * z, axis=0, keepdims=True)

    n_tiles = T // tm
    next_rows = lambda i: jnp.minimum((i + 1) * (tm // 8), T // 8 - 1)
    return pl.pallas_call(
        body, name="conv_bwd",
        grid=(n_tiles,),
        in_specs=[pl.BlockSpec((tm, D), lambda i: (i, 0)),
                  pl.BlockSpec((8, D), lambda i: (next_rows(i), 0)),
                  pl.BlockSpec((tm, D), lambda i: (i, KB_U)),
                  pl.BlockSpec((tm, D), lambda i: (i, KB_CG)),
                  pl.BlockSpec((3, D), lambda i: (0, 0)),
                  pl.BlockSpec(memory_space=pl.ANY)],
        out_specs=(pl.BlockSpec(memory_space=pl.ANY),
                   pl.BlockSpec((8, D), lambda i: (0, 0))),
        out_shape=(jax.ShapeDtypeStruct((T, NCOL), BF16),
                   jax.ShapeDtypeStruct((8, D), F32)),
        scratch_shapes=[pltpu.VMEM((tm, D), BF16), pltpu.VMEM((tm, D), BF16), pltpu.SemaphoreType.DMA((2,))],
        input_output_aliases={5: 0},
        compiler_params=pltpu.CompilerParams(vmem_limit_bytes=VMEM_LIMIT),
    )(dyc, dyc, proj, proj, conv_w, dproj)


def _dh_dx(dproj, w_in_all, x2, dxd, mod3, chip_sums, hops=(), parts0=None):
    tm = 1024
    per_seq = S // tm
    n = len(chip_sums)
    n_in = 5 + n + (0 if parts0 is None else 1)

    def body(*refs):
        d_ref, w_ref, x_ref, dxd_ref, mod_ref = refs[:5]
        ins = refs[5:5 + n]
        gx_ref, vec_ref = refs[n_in:n_in + 2]
        outs = refs[n_in + 2:n_in + 2 + n]
        acc, send_sems, recv_sems, local_sems = refs[n_in + 2 + n:]
        i, jj = pl.program_id(0), pl.program_id(1)

        @pl.when((i == 0) & (jj == 0))
        def _():
            vec_ref[...] = jnp.zeros_like(vec_ref)
            if n:
                sends, _, mine = _chip_copies(ins, outs, send_sems, recv_sems, local_sems, hops)
                for cp in sends + mine:
                    cp.start()

        if n:
            @pl.when((i == T // tm - 1) & (jj == N_DEV - 1))
            def _():
                sends, arrivals, mine = _chip_copies(ins, outs, send_sems, recv_sems, local_sems, hops)
                for cp in arrivals:
                    cp.wait_recv()
                for cp in sends:
                    cp.wait_send()
                for cp in mine:
                    cp.wait()

        @pl.when(jj == 0)
        def _():
            acc[...] = jnp.zeros_like(acc)

        acc[...] += _dot_nt(d_ref[...], w_ref[...])

        @pl.when(jj == N_DEV - 1)
        def _():
            dh = acc[...]
            bidx = i // per_seq
            gx_ref[...] = dxd_ref[...] + dh * (1.0 + mod_ref[0, 1:2, :])
            dshift = jnp.sum(dh, axis=0, keepdims=True)
            dscale = jnp.sum(dh * x_ref[...], axis=0, keepdims=True)
            vec_ref[0:1, :] += jnp.where(bidx == 0, dshift, 0.0)
            vec_ref[1:2, :] += jnp.where(bidx == 1, dshift, 0.0)
            vec_ref[2:3, :] += jnp.where(bidx == 0, dscale, 0.0)
            vec_ref[3:4, :] += jnp.where(bidx == 1, dscale, 0.0)

    any_spec = pl.BlockSpec(memory_space=pl.ANY)
    res = pl.pallas_call(
        body, name="dh_dx",
        grid=(T // tm, N_DEV),
        in_specs=[
            pl.BlockSpec((tm, SHARD), lambda i, jj: (i, jj)),
            pl.BlockSpec((None, D, SHARD), lambda i, jj: (jj, 0, 0)),
            pl.BlockSpec((tm, D), lambda i, jj: (i, 0)),
            pl.BlockSpec((tm, D), lambda i, jj: (i, 0)),
            pl.BlockSpec((1, 3, D), lambda i, jj: (i // per_seq, 0, 0))] + [any_spec] * (n_in - 5),
        out_specs=(pl.BlockSpec((tm, D), lambda i, jj: (i, 0)),
                   pl.BlockSpec((8, D), lambda i, jj: (0, 0))) + (any_spec,) * n,
        out_shape=(jax.ShapeDtypeStruct((T, D), F32), jax.ShapeDtypeStruct((8, D), F32))
                  + tuple(jax.ShapeDtypeStruct(a.shape, a.dtype) for a in chip_sums),
        scratch_shapes=[pltpu.VMEM((tm, D), F32), pltpu.SemaphoreType.DMA((max(3 * n, 1),)),
                        pltpu.SemaphoreType.DMA((max(3 * n, 1),)), pltpu.SemaphoreType.DMA((max(n, 1),))],
        input_output_aliases={} if parts0 is None else {5 + n: 2},
        compiler_params=pltpu.CompilerParams(vmem_limit_bytes=VMEM_LIMIT),
    )(dproj, w_in_all, x2, dxd, mod3, *chip_sums, *([] if parts0 is None else [parts0]))
    return res[0], res[1], res[2:]


def _mm_tn(a, b, tn, blocks_leading, name):
    kk, m = a.shape
    n = b.shape[1]
    tk = 2048

    def body(a_ref, b_ref, o_ref, acc):
        @pl.when(pl.program_id(1) == 0)
        def _():
            acc[...] = jnp.zeros_like(acc)

        acc[...] += _dot_tn(a_ref[...], b_ref[...])

        @pl.when(pl.program_id(1) == kk // tk - 1)
        def _():
            o_ref[...] = acc[...].astype(BF16)

    if blocks_leading:
        out_spec = pl.BlockSpec((None, m, tn), lambda j, k: (j, 0, 0))
        out_shape = jax.ShapeDtypeStruct((n // tn, m, tn), BF16)
    else:
        out_spec = pl.BlockSpec((m, tn), lambda j, k: (0, j))
        out_shape = jax.ShapeDtypeStruct((m, n), BF16)
    return pl.pallas_call(
        body, name=name,
        grid=(n // tn, kk // tk),
        in_specs=[pl.BlockSpec((tk, m), lambda j, k: (k, 0)),
                  pl.BlockSpec((tk, tn), lambda j, k: (k, j))],
        out_specs=out_spec, out_shape=out_shape,
        scratch_shapes=[pltpu.VMEM((m, tn), F32)],
        compiler_params=pltpu.CompilerParams(vmem_limit_bytes=VMEM_LIMIT),
    )(a, b)


def _adam_step(g, w, m, v):
    nm = ADAM_B1 * m + (1.0 - ADAM_B1) * g
    nv = ADAM_B2 * v + (1.0 - ADAM_B2) * (g * g)
    m_hat = nm / (1.0 - ADAM_B1 ** ADAM_STEP)
    v_hat = nv / (1.0 - ADAM_B2 ** ADAM_STEP)
    return -ADAM_LR * (m_hat / (jnp.sqrt(v_hat) + ADAM_EPS) + ADAM_WD * w), nm, nv


def _adamw(parts, w, m, v, name, row_tile=None):
    n_parts, rows, cols = parts.shape
    tr = rows if row_tile is None else row_tile

    def body(p_ref, w_ref, m_ref, v_ref, g_ref, d_ref, nm_ref, nv_ref):
        g = p_ref[0].astype(F32)
        for s in range(1, n_parts):
            g = g + p_ref[s].astype(F32)
        g_ref[...] = g
        d_ref[...], nm_ref[...], nv_ref[...] = _adam_step(g, w_ref[...], m_ref[...], v_ref[...])

    blk = pl.BlockSpec((tr, cols), lambda i: (i, 0))
    shp = jax.ShapeDtypeStruct((rows, cols), F32)
    return pl.pallas_call(
        body, name=name,
        grid=(rows // tr,),
        in_specs=[pl.BlockSpec((n_parts, tr, cols), lambda i: (0, i, 0)), blk, blk, blk],
        out_specs=(blk, blk, blk, blk),
        out_shape=(shp, shp, shp, shp),
        compiler_params=pltpu.CompilerParams(vmem_limit_bytes=VMEM_LIMIT),
    )(parts, w, m, v)


def _multi_adamw(parts_list, params, name):
    n = len(params)
    flat = [t for wmv in params for t in wmv]

    def body(*refs):
        parts, ins, outs = refs[:n], refs[n:4 * n], refs[4 * n:]
        for p in range(n):
            g = parts[p][0].astype(F32)
            for s in range(1, parts[p].shape[0]):
                g = g + parts[p][s].astype(F32)
            w_ref, m_ref, v_ref = ins[3 * p:3 * p + 3]
            g_ref, d_ref, nm_ref, nv_ref = outs[4 * p:4 * p + 4]
            g_ref[...] = g
            d_ref[...], nm_ref[...], nv_ref[...] = _adam_step(g, w_ref[...], m_ref[...], v_ref[...])

    out_shape = []
    for w, _, _ in params:
        out_shape += [jax.ShapeDtypeStruct(w.shape, F32)] * 4
    res = pl.pallas_call(body, name=name, out_shape=tuple(out_shape))(*parts_list, *flat)
    return [res[4 * p:4 * p + 4] for p in range(n)]


def _small_updates(small_g, dmod_all, rel_parts, params):
    flat = [t for wmv in params for t in wmv]

    def body(sg_ref, dm_ref, rp_ref, *refs):
        ins, outs = refs[:len(flat)], refs[len(flat):]

        def over_devices(row):
            tot = sg_ref[0, row:row + 1, :]
            for s in range(1, N_DEV):
                tot = tot + sg_ref[s, row:row + 1, :]
            return tot

        g_b_ada = dm_ref[0:1, :]
        for r in range(1, N_DEV * BL):
            g_b_ada = g_b_ada + dm_ref[r:r + 1, :]
        g_rel = rp_ref[0]
        for s in range(1, N_DEV):
            g_rel = g_rel + rp_ref[s]
        grads = [g_b_ada, over_devices(0), g_rel, over_devices(1), over_devices(2)]
        outs[0][...] = jnp.sum(over_devices(3), axis=1, keepdims=True)
        for p, g in enumerate(grads):
            w_ref, m_ref, v_ref = ins[3 * p:3 * p + 3]
            g_ref, d_ref, nm_ref, nv_ref = outs[1 + 4 * p:5 + 4 * p]
            g_ref[...] = g
            d_ref[...], nm_ref[...], nv_ref[...] = _adam_step(g, w_ref[...], m_ref[...], v_ref[...])

    out_shape = [jax.ShapeDtypeStruct((1, 1), F32)]
    for w, _, _ in params:
        out_shape += [jax.ShapeDtypeStruct(w.shape, F32)] * 4
    res = pl.pallas_call(body, name="small_updates", out_shape=tuple(out_shape))(small_g, dmod_all, rel_parts, *flat)
    return res[0], [res[1 + 4 * p:5 + 4 * p] for p in range(len(params))]


def _attn_fwd_dense(proj, bias):
    nq = 4
    rows = nq * QB
    nsb = S // rows

    def body(q_ref, k_ref, v_ref, kp_ref, vp_ref, b_ref, o_ref, l_ref, ls0, ls1, ls2, ls3):
        ls = [ls0, ls1, ls2, ls3]
        n = pl.program_id(1)
        lane = lax.broadcasted_iota(jnp.int32, (QB, 128), 1)
        units = [(h, j) for h in range(4) for j in range(nq)]

        def keys(cur_ref, prev_ref, h, j):
            sl = slice(h * HD, (h + 1) * HD)
            if j == 0:
                return jnp.concatenate([prev_ref[:, sl], cur_ref[0:QB, sl]], axis=0)
            return cur_ref[(j - 1) * QB:(j + 1) * QB, sl]

        q = jnp.stack([q_ref[j * QB:(j + 1) * QB, h * HD:(h + 1) * HD] for h, j in units])
        k = jnp.stack([keys(k_ref, kp_ref, h, j) for h, j in units])
        v = jnp.stack([keys(v_ref, vp_ref, h, j) for h, j in units])
        bias_b = jnp.stack([b_ref[jnp.minimum(n, 1), h] if j == 0 else b_ref[1, h] for h, j in units])
        s = jnp.einsum("uqd,ukd->uqk", q, k, preferred_element_type=F32) * SCALE + bias_b
        m = jnp.max(s, axis=-1, keepdims=True)
        p = jnp.exp(s - m)
        l = jnp.sum(p, axis=-1, keepdims=True)
        o = jnp.einsum("uqk,ukd->uqd", p.astype(BF16), v, preferred_element_type=F32) / l
        lse = m + jnp.log(l)
        for i, (h, j) in enumerate(units):
            o_ref[j * QB:(j + 1) * QB, h * HD:(h + 1) * HD] = o[i]
            ls[h][j * QB:(j + 1) * QB, :] = jnp.where(lane == h, lse[i], 0.0)
        l_ref[...] = (ls[0][...] + ls[1][...]) + (ls[2][...] + ls[3][...])

    def row(b, n):
        return b * nsb + n

    def prev(b, n):
        return jnp.maximum((b * nsb + n) * nq - 1, 0)

    in_specs = [
        pl.BlockSpec((rows, GW), lambda b, n: (row(b, n), CB_Q)),
        pl.BlockSpec((rows, GW), lambda b, n: (row(b, n), CB_K)),
        pl.BlockSpec((rows, GW), lambda b, n: (row(b, n), CB_V)),
        pl.BlockSpec((QB, GW), lambda b, n: (prev(b, n), CB_K)),
        pl.BlockSpec((QB, GW), lambda b, n: (prev(b, n), CB_V)),
        pl.BlockSpec((None, 2, 4, QB, 2 * QB), lambda b, n: (0, 0, 0, 0, 0)),
    ]
    return pl.pallas_call(
        body, name="attn_fwd0",
        grid=(BL, nsb),
        in_specs=in_specs,
        out_specs=(pl.BlockSpec((rows, GW), lambda b, n: (row(b, n), 0)),
                   pl.BlockSpec((rows, 128), lambda b, n: (row(b, n), 0))),
        out_shape=(jax.ShapeDtypeStruct((T, GW), F32), jax.ShapeDtypeStruct((T, 128), F32)),
        scratch_shapes=[pltpu.VMEM((rows, 128), F32)] * 4,
        compiler_params=pltpu.CompilerParams(vmem_limit_bytes=VMEM_LIMIT),
    )(proj, proj, proj, proj, proj, bias)


def _attn_bwd_dense(proj, d_out, stats, bias, dproj):
    nq = 4
    rows = nq * QB
    nsb = S // rows
    cols_q, cols_k, cols_v = CB * CB_Q, CB * CB_K, CB * CB_V

    def body(q_ref, k_ref, v_ref, do_ref, st_ref, kp_ref, vp_ref, b_ref, _, out_ref, db_ref,
             sq, sk, sv, carry, sems):
        b, n = pl.program_id(0), pl.program_id(1)
        units = [(h, j) for h in range(4) for j in range(nq)]

        @pl.when((b == 0) & (n == 0))
        def _():
            db_ref[...] = jnp.zeros_like(db_ref)

        @pl.when(n == 0)
        def _():
            carry[...] = jnp.zeros_like(carry)

        def write(first_block, position, count):
            row0 = pl.multiple_of(first_block * QB, QB)
            part = pl.ds(position * QB, count * QB)
            _write_columns([(sq.at[part], cols_q), (sk.at[part], cols_k), (sv.at[part], cols_v)],
                           out_ref, row0, sems)

        @pl.when(n == nsb)
        def _():
            sq[0:QB, :] = carry[:, 0:GW].astype(BF16)
            sk[0:QB, :] = carry[:, GW:2 * GW].astype(BF16)
            sv[0:QB, :] = carry[:, 2 * GW:3 * GW].astype(BF16)
            write((b + 1) * nsb * nq - 1, 0, 1)

        @pl.when(n < nsb)
        def _():
            def keys(cur_ref, prev_ref, h, j):
                sl = slice(h * HD, (h + 1) * HD)
                if j == 0:
                    return jnp.concatenate([prev_ref[:, sl], cur_ref[0:QB, sl]], axis=0)
                return cur_ref[(j - 1) * QB:(j + 1) * QB, sl]

            def block(ref, h, j):
                return ref[j * QB:(j + 1) * QB, h * HD:(h + 1) * HD]

            q = jnp.stack([block(q_ref, h, j) for h, j in units])
            do = jnp.stack([block(do_ref, h, j) for h, j in units])
            k = jnp.stack([keys(k_ref, kp_ref, h, j) for h, j in units])
            v = jnp.stack([keys(v_ref, vp_ref, h, j) for h, j in units])
            bias_b = jnp.stack([b_ref[jnp.minimum(n, 1), h] if j == 0 else b_ref[1, h] for h, j in units])
            lse = jnp.stack([st_ref[j * QB:(j + 1) * QB, h:h + 1] for h, j in units])
            delta = jnp.stack([st_ref[j * QB:(j + 1) * QB, 4 + h:5 + h] for h, j in units])
            s = jnp.einsum("uqd,ukd->uqk", q, k, preferred_element_type=F32) * SCALE + bias_b
            p = jnp.exp(s - lse)
            ds = p * (jnp.einsum("uqd,ukd->uqk", do, v, preferred_element_type=F32) - delta)
            for h in range(4):
                tot = ds[h * nq]
                for j in range(1, nq):
                    tot = tot + ds[h * nq + j]
                db_ref[h] += tot
            dsb, pb = ds.astype(BF16), p.astype(BF16)
            dq = jnp.einsum("uqk,ukd->uqd", dsb, k, preferred_element_type=F32) * SCALE
            dk = jnp.einsum("uqk,uqd->ukd", dsb, q, preferred_element_type=F32) * SCALE
            dv = jnp.einsum("uqk,uqd->ukd", pb, do, preferred_element_type=F32)
            for h in range(4):
                sl = slice(h * HD, (h + 1) * HD)
                u0, last = h * nq, h * nq + nq - 1
                sq[0:QB, sl] = carry[:, sl].astype(BF16)
                sk[0:QB, sl] = (carry[:, GW + h * HD:GW + (h + 1) * HD] + dk[u0, :QB]).astype(BF16)
                sv[0:QB, sl] = (carry[:, 2 * GW + h * HD:2 * GW + (h + 1) * HD] + dv[u0, :QB]).astype(BF16)
                for j in range(nq - 1):
                    pos = slice((j + 1) * QB, (j + 2) * QB)
                    sq[pos, sl] = dq[u0 + j].astype(BF16)
                    sk[pos, sl] = (dk[u0 + j, QB:] + dk[u0 + j + 1, :QB]).astype(BF16)
                    sv[pos, sl] = (dv[u0 + j, QB:] + dv[u0 + j + 1, :QB]).astype(BF16)
                carry[:, sl] = dq[last]
                carry[:, GW + h * HD:GW + (h + 1) * HD] = dk[last, QB:]
                carry[:, 2 * GW + h * HD:2 * GW + (h + 1) * HD] = dv[last, QB:]

            @pl.when(n == 0)
            def _():
                write(b * nsb * nq, 1, nq - 1)

            @pl.when(n > 0)
            def _():
                write((b * nsb + n) * nq - 1, 0, nq)

    def row(b, n):
        return b * nsb + jnp.minimum(n, nsb - 1)

    def prev(b, n):
        return jnp.maximum(row(b, n) * nq - 1, 0)

    in_specs = [
        pl.BlockSpec((rows, GW), lambda b, n: (row(b, n), CB_Q)),
        pl.BlockSpec((rows, GW), lambda b, n: (row(b, n), CB_K)),
        pl.BlockSpec((rows, GW), lambda b, n: (row(b, n), CB_V)),
        pl.BlockSpec((rows, GW), lambda b, n: (row(b, n), 0)),
        pl.BlockSpec((rows, 128), lambda b, n: (row(b, n), 0)),
        pl.BlockSpec((QB, GW), lambda b, n: (prev(b, n), CB_K)),
        pl.BlockSpec((QB, GW), lambda b, n: (prev(b, n), CB_V)),
        pl.BlockSpec((None, 2, 4, QB, 2 * QB), lambda b, n: (0, 0, 0, 0, 0)),
        pl.BlockSpec(memory_space=pl.ANY),
    ]
    return pl.pallas_call(
        body, name="attn_bwd0",
        grid=(BL, nsb + 1),
        in_specs=in_specs,
        out_specs=(pl.BlockSpec(memory_space=pl.ANY),
                   pl.BlockSpec((4, QB, 2 * QB), lambda b, n: (0, 0, 0))),
        out_shape=(jax.ShapeDtypeStruct((T, NCOL), BF16),
                   jax.ShapeDtypeStruct((4, QB, 2 * QB), F32)),
        scratch_shapes=[pltpu.VMEM((rows, GW), BF16)] * 3
                       + [pltpu.VMEM((QB, 3 * GW), F32), pltpu.SemaphoreType.DMA((3,))],
        input_output_aliases={8: 0},
        compiler_params=pltpu.CompilerParams(vmem_limit_bytes=VMEM_LIMIT),
    )(proj, proj, proj, d_out, stats, proj, proj, bias, dproj)


def _attention_forward(proj, rel_bias):
    buckets_np, masks_np = _bucket_maps()
    buckets, masks = jnp.asarray(buckets_np), jnp.asarray(masks_np)
    bias = _bias_expand(rel_bias, buckets, masks)
    fwd = [_attn_fwd_dense(proj, bias)] + [_attn_fwd(proj, bias, g) for g in (1, 2)]
    return bias, buckets, [f[0] for f in fwd], [f[1] for f in fwd]


def _local_step(x2, tgt2, mod3, h, proj, attn, w_ao, w_co, w_o, conv_w, conv_b, ln_g, ln_b):
    bias, buckets, o_g, lse_g = attn

    (dproj, dyc, d_o, stats, dxd, merged, dy, a_in, da_out, s_in, ds_out, tail_vec) = _tail(
        x2, tgt2, mod3, o_g, lse_g, proj, w_ao, w_co, w_o, conv_w, conv_b, ln_g, ln_b)

    dproj, db = _attn_bwd_dense(proj, d_o, stats, bias, dproj)
    dbias = [db]
    for g in (1, 2):
        dproj, db = _attn_bwd(proj, d_o, stats, bias, dproj, g)
        dbias.append(db)
    g_rel_bias = _bias_grad(*dbias, buckets)
    dproj, conv_vec = _conv_bwd(dyc, proj, conv_w, dproj)

    gw_o = _mm_tn(merged, dy, D, False, "gw_o")
    gw_co = _mm_tn(s_in, ds_out, D, False, "gw_conv_out")
    gw_ao = _mm_tn(a_in, da_out, D, False, "gw_attn_out")
    gw_ao = jnp.transpose(gw_ao.reshape(GW, N_DEV, D // N_DEV), (1, 0, 2))
    return dproj, dxd, gw_ao, gw_co, gw_o, conv_vec, g_rel_bias, tail_vec


def kernel(x, c, w_ada, b_ada, w_in, conv_w, conv_b, rel_bias, w_attn_out, w_conv_out, w_o, ln_g, ln_b, loss_target, m_w_ada, m_b_ada, m_w_in, m_conv_w, m_conv_b, m_rel_bias, m_w_attn_out, m_w_conv_out, m_w_o, m_ln_g, m_ln_b, v_w_ada, v_b_ada, v_w_in, v_conv_w, v_conv_b, v_rel_bias, v_w_attn_out, v_w_conv_out, v_w_o, v_ln_g, v_ln_b):
    me = _my_index()
    x2 = x.reshape(T, D)
    tgt2 = loss_target.reshape(T, D)

    b_cols = lax.dynamic_slice(b_ada, (0, me * ADA_SHARD), (1, ADA_SHARD))
    c_g, mod_in = _mod_exchange(jnp.pad(c, ((0, 8 - BL), (0, 0))), w_ada[0], b_cols)
    c_all = c_g[:, 0:BL, :].reshape(N_DEV * BL, D)
    mod3 = jnp.transpose(mod_in[:, 0:BL, :], (1, 0, 2)).reshape(BL, 3, D)

    h = _prep_h(x2, mod3)
    rows_shape = jax.ShapeDtypeStruct((N_DEV, D // N_DEV, D), BF16)
    proj, w_in_all, (w_ao_g, w_co_g, w_o_g, conv_w_g) = _gather_proj(
        _shard_order(), h, w_in[0].astype(BF16), 1024,
        ([w_attn_out[0].astype(BF16), w_conv_out[0].astype(BF16), w_o[0].astype(BF16), conv_w[0]],
         [jax.ShapeDtypeStruct((N_DEV, GW, D // N_DEV), BF16), rows_shape, rows_shape,
          jax.ShapeDtypeStruct((N_DEV, 3, D // N_DEV), F32)]))

    attn = _attention_forward(proj, rel_bias)
    w_ao_full = jnp.transpose(w_ao_g, (1, 0, 2)).reshape(GW, D)
    w_co_full = w_co_g.reshape(D, D)
    w_o_full = w_o_g.reshape(D, D)
    conv_w_full = jnp.transpose(conv_w_g, (1, 0, 2)).reshape(3, D)

    (dproj, dxd, gw_ao, gw_co, gw_o, conv_vec, g_rel_bias, tail_vec) = _local_step(
        x2, tgt2, mod3, h, proj, attn, w_ao_full, w_co_full, w_o_full,
        conv_w_full, conv_b, ln_g, ln_b)

    g_conv_w_blocks = jnp.transpose(conv_vec[0:3].reshape(3, N_DEV, D // N_DEV), (1, 0, 2))
    partials = [gw_ao, gw_co.reshape(N_DEV, D // N_DEV, D), gw_o.reshape(N_DEV, D // N_DEV, D), g_conv_w_blocks]
    w_in_sums, w_in_parts, sib = _gw_in_pair(
        _slice_order(), h, dproj, partials,
        [jax.ShapeDtypeStruct((4, GW, D // N_DEV), BF16),
         jax.ShapeDtypeStruct((4, D // N_DEV, D), BF16),
         jax.ShapeDtypeStruct((4, D // N_DEV, D), BF16),
         jax.ShapeDtypeStruct((4, 3, D // N_DEV), F32)])
    core = lax.axis_index("c").astype(jnp.int32).reshape(1)
    chip_sums = [w_in_sums] + list(_pair_add(core, partials, sib))
    hops = [(3,)] + [(1, 2, 3)] * 4
    grad_x, mod_vec, (r_in, r_ao, r_co, r_o, r_cw) = _dh_dx(
        dproj, w_in_all, x2, dxd, mod3, chip_sums, hops, w_in_parts)

    small = jnp.concatenate([
        tail_vec[0:4],
        jnp.pad(g_rel_bias.reshape(1, N_BUCKETS * N_HEADS), ((0, 0), (0, D - N_BUCKETS * N_HEADS))),
        jnp.zeros((3, D), F32)], axis=0)
    dmod = jnp.concatenate([mod_vec[0:2], mod_vec[2:4], tail_vec[4:6]], axis=1)
    small_g, dmod_g = _all_gather(
        [small, dmod],
        [jax.ShapeDtypeStruct((N_DEV, 8, D), F32), jax.ShapeDtypeStruct((N_DEV, BL, 3 * D), F32)],
        "gather_small")
    dmod_all = dmod_g.reshape(N_DEV * BL, 3 * D)
    small_names = ["b_ada", "conv_b", "rel_bias", "ln_g", "ln_b"]
    small_params = [(b_ada, m_b_ada, v_b_ada), (conv_b, m_conv_b, v_conv_b), (rel_bias, m_rel_bias, v_rel_bias),
                    (ln_g, m_ln_g, v_ln_g), (ln_b, m_ln_b, v_ln_b)]
    loss, small_res = _small_updates(
        small_g, dmod_all, small_g[:, 4, :N_BUCKETS * N_HEADS].reshape(N_DEV, N_BUCKETS, N_HEADS), small_params)
    loss = loss.reshape(())
    g_w_ada = _ada_bwd(jnp.transpose(c_all), lax.dynamic_slice(dmod_all, (0, me * ADA_SHARD),
                                                               (N_DEV * BL, ADA_SHARD)))

    def upd(parts, w, m, v, name, row_tile=None):
        shape = w.shape
        w2, m2, v2 = (t.reshape(parts.shape[1:]) for t in (w, m, v))
        return tuple(t.reshape(shape) for t in _adamw(parts, w2, m2, v2, name, row_tile))

    res = {
        "w_ada": upd(g_w_ada[None], w_ada, m_w_ada, v_w_ada, "adam_w_ada", 256),
        "w_in": upd(r_in, w_in, m_w_in, v_w_in, "adam_w_in", 128),
    }
    mid_names = ["conv_w", "w_attn_out", "w_conv_out", "w_o"]
    mid_parts = [r_cw, r_ao, r_co, r_o]
    mid_full = [(conv_w, m_conv_w, v_conv_w), (w_attn_out, m_w_attn_out, v_w_attn_out),
                (w_conv_out, m_w_conv_out, v_w_conv_out), (w_o, m_w_o, v_w_o)]
    mid_res = _multi_adamw(mid_parts, [tuple(t[0] for t in wmv) for wmv in mid_full], "adam_mid")
    for nm, wmv, outs4 in zip(mid_names, mid_full, mid_res):
        res[nm] = tuple(t[None] for t in outs4)
    res.update(dict(zip(small_names, small_res)))
    order = ["w_ada", "b_ada", "w_in", "conv_w", "conv_b", "rel_bias", "w_attn_out", "w_conv_out",
             "w_o", "ln_g", "ln_b"]
    outs = [loss, grad_x.reshape(BL, S, D)]
    for k in range(4):
        outs += [res[name][k] for name in order]
    return tuple(outs)
```

```python
import math

import numpy as np
import jax
import jax.numpy as jnp
from jax import lax
from jax.experimental import pallas as pl
from jax.experimental.pallas import tpu as pltpu

F32 = jnp.float32
BF16 = jnp.bfloat16
MESH = pl.DeviceIdType.MESH

N_DEV = 8
D = 1024
S = 2048
BL = 2
T = BL * S
NCOL = 11264
SHARD = NCOL // N_DEV
CB = 512
NCB = NCOL // CB
HD = 128
GW = 512
QB = 128
DILATIONS = (1, 4, 16)
N_STEPS = 128
N_BUCKETS = 32
N_HEADS = 12
ALPHA = 2.0 ** 0.25
LN_EPS = 1e-5
NEG_INF = -1e30
SCALE = HD ** -0.5
ADA_SHARD = 3 * D // N_DEV

CB_Q, CB_K, CB_V, CB_GA = 0, 3, 6, 9
KB_U, KB_BG, KB_CG, KB_GC, KB_MA, KB_MC = 5, 6, 7, 8, 9, 10

ADAM_LR, ADAM_B1, ADAM_B2, ADAM_EPS, ADAM_WD, ADAM_STEP = 0.001, 0.9, 0.999, 1e-08, 0.01, 10

VMEM_LIMIT = 56 * 1024 * 1024
VMEM_LIMIT_TAIL = 62 * 1024 * 1024


def _dot(a, b):
    return jnp.dot(a, b, preferred_element_type=F32)


def _dot_nt(a, b):
    return lax.dot_general(a, b, (((1,), (1,)), ((), ())), preferred_element_type=F32)


def _dot_tn(a, b):
    return lax.dot_general(a, b, (((0,), (0,)), ((), ())), preferred_element_type=F32)


def _sigmoid(v):
    return 1.0 / (1.0 + jnp.exp(-v))


def _write_columns(pieces, dst_hbm, row0, sems):
    copies = []
    for k, (src, col0) in enumerate(pieces):
        rows, width = src.shape
        copies.append(pltpu.make_async_copy(
            src, dst_hbm.at[pl.ds(row0, rows), pl.ds(col0, width)], sems.at[k]))
    for cp in copies:
        cp.start()
    for cp in copies:
        cp.wait()


def _my_index():
    return 4 * lax.axis_index("x") + 2 * lax.axis_index("y") + lax.axis_index("c")


class _Gather:
    def __init__(self, ins, outs, stage, send_sems, recv_sems, local_sems):
        self.ins, self.outs, self.stage = ins, outs, stage
        self.send_sems, self.recv_sems, self.local_sems = send_sems, recv_sems, local_sems
        x, y, c = lax.axis_index("x"), lax.axis_index("y"), lax.axis_index("c")
        self.c = c
        self.me, self.sibling = (x, y, c), (x, y, 1 - c)
        self.chips = [(1 - x, y), (x, 1 - y), (1 - x, 1 - y)]

    @staticmethod
    def scratch(arrs):
        n = len(arrs)
        return ([pltpu.SemaphoreType.DMA((7 * n,)), pltpu.SemaphoreType.DMA((7 * n,)),
                 pltpu.SemaphoreType.DMA((n,))] + [pltpu.VMEM(a.shape, a.dtype) for a in arrs])

    def _copy(self, a, k, block, to, src=None):
        dst = self.outs[a].at[4 * block[0] + 2 * block[1] + block[2]]
        return pltpu.make_async_remote_copy(
            src_ref=dst if src is None else src, dst_ref=dst,
            send_sem=self.send_sems.at[a * 7 + k], recv_sem=self.recv_sems.at[a * 7 + k],
            device_id=to, device_id_type=MESH)

    def _first(self):
        first = []
        for a in range(len(self.ins)):
            first.append(self._copy(a, 0, self.me, self.sibling, src=self.ins[a]))
            first += [self._copy(a, 1 + j, self.me, (*chip, self.c), src=self.ins[a])
                      for j, chip in enumerate(self.chips)]
        return first

    def _mine(self):
        me = self.me
        return [pltpu.make_async_copy(self.stage[a], self.outs[a].at[4 * me[0] + 2 * me[1] + me[2]],
                                      self.local_sems.at[a]) for a in range(len(self.ins))]

    def begin(self):
        for cp in self._first():
            cp.start()
        loads = [pltpu.make_async_copy(self.ins[a], self.stage[a], self.local_sems.at[a])
                 for a in range(len(self.ins))]
        for cp in loads:
            cp.start()
        for cp in loads:
            cp.wait()
        for cp in self._mine():
            cp.start()

    def finish(self):
        n, c, me, sibling = len(self.ins), self.c, self.me, self.sibling
        passed = []
        for j, chip in enumerate(self.chips):
            for a in range(n):
                self._copy(a, 1 + j, (*chip, c), me).wait_recv()
                fwd = self._copy(a, 4 + j, (*chip, c), sibling)
                fwd.start()
                passed.append(fwd)
        for a in range(n):
            self._copy(a, 0, sibling, me).wait_recv()
        for j, chip in enumerate(self.chips):
            for a in range(n):
                self._copy(a, 4 + j, (*chip, 1 - c), me).wait_recv()
        for cp in self._first() + passed:
            cp.wait_send()
        for cp in self._mine():
            cp.wait()


def _all_gather(arrs, out_shapes, name):
    n = len(arrs)

    def body(*refs):
        g = _Gather(refs[:n], refs[n:2 * n], refs[2 * n + 3:], *refs[2 * n:2 * n + 3])
        g.begin()
        g.finish()

    any_spec = pl.BlockSpec(memory_space=pl.ANY)
    return pl.pallas_call(
        body, name=name,
        out_shape=tuple(out_shapes),
        in_specs=[any_spec] * n,
        out_specs=tuple([any_spec] * n),
        scratch_shapes=_Gather.scratch(arrs),
    )(*arrs)


def _slice_order():
    x, y, c = lax.axis_index("x"), lax.axis_index("y"), lax.axis_index("c")
    slots = []
    for q in (2 * (1 - x) + y, 2 * x + (1 - y), 2 * (1 - x) + (1 - y), 2 * x + y):
        slots += [2 * q + 1 - c, 2 * q + c]
    return jnp.stack(slots).astype(jnp.int32)


def _gw_in_pair(order, h, dproj, smalls, small_shapes4):
    kk, m = h.shape
    tk = min(kk, 2048)
    nk = kk // tk
    ncols = dproj.shape[1] // N_DEV
    n = len(smalls)

    def body(order_ref, h_ref, d_ref, *rest):
        ins = rest[:n]
        sums_hbm, parts_hbm = rest[n], rest[n + 1]
        sib = rest[n + 2:2 * n + 2]
        (acc, sendbuf, recvbuf, sumbuf, send_sems, recv_sems, local_sem, ssend, srecv,
         isend, irecv) = rest[2 * n + 2:]
        js, k = pl.program_id(0), pl.program_id(1)
        x, y, c = lax.axis_index("x"), lax.axis_index("y"), lax.axis_index("c")
        sibling = (x, y, 1 - c)
        my_chip = 2 * x + y
        near = [(1 - x, y, c), (x, 1 - y, c)]

        def ici_copy(p, out_chip):
            peer = near[p]
            return pltpu.make_async_remote_copy(
                src_ref=sumbuf.at[p], dst_ref=parts_hbm.at[out_chip],
                send_sem=isend.at[p], recv_sem=irecv.at[p], device_id=peer, device_id_type=MESH)

        def small_copies():
            return [pltpu.make_async_remote_copy(
                        src_ref=ins[a].at[2 * q + 1 - c], dst_ref=sib[a].at[q],
                        send_sem=ssend.at[a * 4 + q], recv_sem=srecv.at[a * 4 + q],
                        device_id=sibling, device_id_type=MESH)
                    for a in range(n) for q in range(4)]

        def slice_copy(p):
            return pltpu.make_async_remote_copy(
                src_ref=sendbuf, dst_ref=recvbuf.at[p], send_sem=send_sems.at[p], recv_sem=recv_sems.at[p],
                device_id=sibling, device_id_type=MESH)

        def sum_copy(p):
            return pltpu.make_async_copy(sumbuf.at[2], sums_hbm.at[order_ref[2 * p] // 2], local_sem)

        @pl.when((js == 0) & (k == 0))
        def _():
            for cp in small_copies():
                cp.start()

        @pl.when(k == 0)
        def _():
            acc[...] = jnp.zeros_like(acc)

        acc[...] += _dot_tn(h_ref[...], d_ref[...])

        for p in range(4):
            @pl.when((js == 2 * p) & (k == nk - 1))
            def _():
                if p > 0:
                    slice_copy(p - 1).wait_send()
                sendbuf[...] = acc[...].astype(BF16)
                slice_copy(p).start()

            @pl.when((js == 2 * p + 1) & (k == nk - 1))
            def _():
                slice_copy(p).wait_recv()
                if p == 3:
                    sum_copy(2).wait()
                sumbuf[min(p, 2)] = (acc[...] + recvbuf[p].astype(F32)).astype(BF16)
                if p < 2:
                    ici_copy(p, my_chip).start()
                else:
                    sum_copy(p).start()

        @pl.when((js == N_DEV - 1) & (k == nk - 1))
        def _():
            slice_copy(3).wait_send()
            sum_copy(3).wait()
            for cp in small_copies():
                cp.wait()
            for p in range(2):
                ici_copy(p, 2 * near[p][0] + near[p][1]).wait_recv()
                ici_copy(p, my_chip).wait_send()

    any_spec = pl.BlockSpec(memory_space=pl.ANY)
    res = pl.pallas_call(
        body, name="gw_in_pair",
        grid_spec=pltpu.PrefetchScalarGridSpec(
            num_scalar_prefetch=1,
            grid=(N_DEV, nk),
            in_specs=[pl.BlockSpec((tk, m), lambda js, k, order_ref: (k, 0)),
                      pl.BlockSpec((tk, ncols), lambda js, k, order_ref: (k, order_ref[js]))] + [any_spec] * n,
            out_specs=(any_spec,) * (n + 2),
            scratch_shapes=[pltpu.VMEM((m, ncols), F32), pltpu.VMEM((m, ncols), BF16),
                            pltpu.VMEM((4, m, ncols), BF16), pltpu.VMEM((3, m, ncols), BF16),
                            pltpu.SemaphoreType.DMA((4,)), pltpu.SemaphoreType.DMA((4,)),
                            pltpu.SemaphoreType.DMA,
                            pltpu.SemaphoreType.DMA((4 * n,)), pltpu.SemaphoreType.DMA((4 * n,)),
                            pltpu.SemaphoreType.DMA((2,)), pltpu.SemaphoreType.DMA((2,))]),
        out_shape=(jax.ShapeDtypeStruct((4, m, ncols), BF16),) * 2 + tuple(small_shapes4),
        compiler_params=pltpu.CompilerParams(vmem_limit_bytes=VMEM_LIMIT),
    )(order, h, dproj, *smalls)
    return res[0], res[1], res[2:]


def _chip_copies(ins, outs, send_sems, recv_sems, local_sems, hops):
    n = len(ins)
    x, y, c = lax.axis_index("x"), lax.axis_index("y"), lax.axis_index("c")
    my_chip = 2 * x + y

    def peer_of(k):
        return ((1 - x) if (k >> 1) & 1 else x, (1 - y) if k & 1 else y, c)

    def copy(a, k, out_chip):
        peer = peer_of(k)
        return pltpu.make_async_remote_copy(
            src_ref=ins[a].at[2 * peer[0] + peer[1]], dst_ref=outs[a].at[out_chip],
            send_sem=send_sems.at[a * 3 + k - 1], recv_sem=recv_sems.at[a * 3 + k - 1],
            device_id=peer, device_id_type=MESH)

    sends = [copy(a, k, my_chip) for k in range(1, 4) for a in range(n) if k in hops[a]]
    arrivals = []
    for k in range(1, 4):
        peer = peer_of(k)
        arrivals += [copy(a, k, 2 * peer[0] + peer[1]) for a in range(n) if k in hops[a]]
    mine = [pltpu.make_async_copy(ins[a].at[my_chip], outs[a].at[my_chip], local_sems.at[a])
            for a in range(n)]
    return sends, arrivals, mine


def _pair_add(core, mines, theirs):
    n = len(mines)

    def body(core_ref, *refs):
        mine, sib, outs = refs[:n], refs[n:2 * n], refs[2 * n:]
        for a in range(n):
            for q in range(4):
                outs[a][q] = (mine[a][2 * q + core_ref[0]].astype(F32)
                              + sib[a][q].astype(F32)).astype(outs[a].dtype)

    return pl.pallas_call(
        body, name="pair_add",
        in_specs=[pl.BlockSpec(memory_space=pltpu.SMEM)] + [pl.BlockSpec(memory_space=pltpu.VMEM)] * (2 * n),
        out_shape=tuple(jax.ShapeDtypeStruct(t.shape, t.dtype) for t in theirs),
    )(core, *mines, *theirs)


def _mod_exchange(c8, w_ada, b_cols):
    cols = w_ada.shape[1]

    def body(c_ref, w_ref, b_ref, call_ref, mod_ref, msend, send1, recv1, send2, recv2):
        x, y, c = lax.axis_index("x"), lax.axis_index("y"), lax.axis_index("c")
        my_slot = 4 * x + 2 * y + c

        def peer_of(k):
            return ((1 - x) if (k >> 2) & 1 else x, (1 - y) if (k >> 1) & 1 else y, (1 - c) if k & 1 else c)

        def slot_of(dev):
            return 4 * dev[0] + 2 * dev[1] + dev[2]

        def exchange(src_of, dst_ref, send_sems, recv_sems):
            sends, arrivals = [], []
            for k in range(1, 8):
                peer = peer_of(k)
                sends.append(pltpu.make_async_remote_copy(
                    src_ref=src_of(slot_of(peer)), dst_ref=dst_ref.at[my_slot],
                    send_sem=send_sems.at[k - 1], recv_sem=recv_sems.at[k - 1],
                    device_id=peer, device_id_type=MESH))
                arrivals.append(pltpu.make_async_remote_copy(
                    src_ref=src_of(my_slot), dst_ref=dst_ref.at[slot_of(peer)],
                    send_sem=send_sems.at[k - 1], recv_sem=recv_sems.at[k - 1],
                    device_id=peer, device_id_type=MESH))
            for cp in sends:
                cp.start()
            for cp in arrivals:
                cp.wait_recv()
            for cp in sends:
                cp.wait_send()

        call_ref[my_slot] = c_ref[...]
        exchange(lambda s: c_ref, call_ref, send1, recv1)
        cv = call_ref[...].reshape(N_DEV * 8, c_ref.shape[1])
        act = cv * _sigmoid(cv)
        mod = jnp.dot(act, w_ref[...], preferred_element_type=F32,
                      precision=lax.Precision.HIGHEST) + b_ref[...]
        msend[...] = mod.reshape(N_DEV, 8, cols)
        mod_ref[my_slot] = msend[my_slot]
        exchange(lambda s: msend.at[s], mod_ref, send2, recv2)

    return pl.pallas_call(
        body, name="mod_exchange",
        out_shape=(jax.ShapeDtypeStruct((N_DEV, 8, c8.shape[1]), F32),
                   jax.ShapeDtypeStruct((N_DEV, 8, cols), F32)),
        scratch_shapes=[pltpu.VMEM((N_DEV, 8, cols), F32)] + [pltpu.SemaphoreType.DMA((7,))] * 4,
    )(c8, w_ada, b_cols)


def _ada_bwd(c_all_t, dmod_cols):
    def body(c_ref, d_ref, o_ref):
        cv = c_ref[...]
        sc = cv * _sigmoid(cv)
        o_ref[...] = jnp.dot(sc, d_ref[...], preferred_element_type=F32,
                             precision=lax.Precision.HIGHEST)

    return pl.pallas_call(
        body, name="ada_bwd",
        out_shape=jax.ShapeDtypeStruct((c_all_t.shape[0], dmod_cols.shape[1]), F32),
    )(c_all_t, dmod_cols)


def _shard_order():
    x, y, c = lax.axis_index("x"), lax.axis_index("y"), lax.axis_index("c")
    devs = [(x, y, c), (x, y, 1 - c)]
    for chip in [(1 - x, y), (x, 1 - y), (1 - x, 1 - y)]:
        devs += [(*chip, c), (*chip, 1 - c)]
    return jnp.stack([4 * d[0] + 2 * d[1] + d[2] for d in devs]).astype(jnp.int32)


def _prep_h(x2, mod3):
    ts = 512
    per_seq = S // ts

    def body(x_ref, mod_ref, h_ref):
        shift = mod_ref[0, 0:1, :]
        scale = mod_ref[0, 1:2, :]
        h_ref[...] = (x_ref[...] * (1.0 + scale) + shift).astype(BF16)

    return pl.pallas_call(
        body, name="prep_h",
        grid=(T // ts,),
        in_specs=[pl.BlockSpec((ts, D), lambda i: (i, 0)),
                  pl.BlockSpec((1, 3, D), lambda i: (i // per_seq, 0, 0))],
        out_specs=pl.BlockSpec((ts, D), lambda i: (i, 0)),
        out_shape=jax.ShapeDtypeStruct((T, D), BF16),
    )(x2, mod3)


def _gather_proj(order, h, w_shard, tm, ride=()):
    rows, kdim = h.shape
    ncols = w_shard.shape[1]
    n_i = rows // tm
    ride_arrs, ride_shapes = ride if ride else ((), ())
    n_ride = len(ride_arrs)

    def body(order_ref, h_ref, mine_hbm, *rest):
        ride_ins = rest[:n_ride]
        o_ref, all_hbm = rest[n_ride:n_ride + 2]
        ride_outs = rest[n_ride + 2:2 * n_ride + 2]
        wv, send_sems, recv_sems, local_sems = rest[2 * n_ride + 2:2 * n_ride + 6]
        ride_scr = rest[2 * n_ride + 6:]
        j, i = pl.program_id(0), pl.program_id(1)
        x, y, c = lax.axis_index("x"), lax.axis_index("y"), lax.axis_index("c")
        me, sibling = (x, y, c), (x, y, 1 - c)
        chips = [(1 - x, y), (x, 1 - y), (1 - x, 1 - y)]

        def slot(dev):
            return 4 * dev[0] + 2 * dev[1] + dev[2]

        def copy(k, block, to):
            return pltpu.make_async_remote_copy(
                src_ref=wv.at[slot(block)], dst_ref=wv.at[slot(block)],
                send_sem=send_sems.at[k], recv_sem=recv_sems.at[k],
                device_id=to, device_id_type=MESH)

        def keep(step, block):
            return pltpu.make_async_copy(wv.at[slot(block)], all_hbm.at[slot(block)], local_sems.at[step])

        if n_ride:
            gather = _Gather(ride_ins, ride_outs, ride_scr[3:], *ride_scr[:3])
        first = [copy(0, me, sibling)] + [copy(1 + q, me, (*chip, c)) for q, chip in enumerate(chips)]
        passed = [copy(4 + q, (*chip, c), sibling) for q, chip in enumerate(chips)]
        due = [(me, None, None), (sibling, copy(0, sibling, me), None)]
        for q, chip in enumerate(chips):
            due.append(((*chip, c), copy(1 + q, (*chip, c), me), passed[q]))
            due.append(((*chip, 1 - c), copy(4 + q, (*chip, 1 - c), me), None))

        @pl.when((j == 0) & (i == 0))
        def _():
            load = pltpu.make_async_copy(mine_hbm, wv.at[slot(me)], local_sems.at[N_DEV])
            load.start()
            load.wait()
            for cp in first:
                cp.start()
            keep(0, me).start()

        for step in range(1, N_DEV):
            block, arrival, forward = due[step]

            @pl.when((j == step) & (i == 0))
            def _():
                arrival.wait_recv()
                if forward is not None:
                    forward.start()
                keep(step, block).start()
                if n_ride and step == N_DEV - 2:
                    gather.begin()

        o_ref[...] = _dot(h_ref[...], wv[order_ref[j]]).astype(BF16)

        @pl.when((j == N_DEV - 1) & (i == n_i - 1))
        def _():
            for cp in first + passed:
                cp.wait_send()
            for step in range(N_DEV):
                keep(step, due[step][0]).wait()
            if n_ride:
                gather.finish()

    any_spec = pl.BlockSpec(memory_space=pl.ANY)
    res = pl.pallas_call(
        body, name="gather_proj",
        grid_spec=pltpu.PrefetchScalarGridSpec(
            num_scalar_prefetch=1,
            grid=(N_DEV, n_i),
            in_specs=[pl.BlockSpec((tm, kdim), lambda j, i, order_ref: (i, 0)), any_spec] + [any_spec] * n_ride,
            out_specs=(pl.BlockSpec((tm, ncols), lambda j, i, order_ref: (i, order_ref[j])), any_spec)
                      + (any_spec,) * n_ride,
            scratch_shapes=[pltpu.VMEM((N_DEV, kdim, ncols), BF16),
                            pltpu.SemaphoreType.DMA((7,)), pltpu.SemaphoreType.DMA((7,)),
                            pltpu.SemaphoreType.DMA((N_DEV + 1,))]
                           + (_Gather.scratch(ride_arrs) if n_ride else [])),
        out_shape=(jax.ShapeDtypeStruct((rows, N_DEV * ncols), BF16),
                   jax.ShapeDtypeStruct((N_DEV, kdim, ncols), BF16)) + tuple(ride_shapes),
        compiler_params=pltpu.CompilerParams(vmem_limit_bytes=VMEM_LIMIT),
    )(order, h, w_shard, *ride_arrs)
    return res[0], res[1], res[2:]


def _bucket_maps():
    a = np.arange(QB)[:, None]
    b = np.arange(2 * QB)[None, :]
    steps = a + QB - b
    maps = []
    for dil in DILATIONS:
        dist = np.maximum(steps, 0) * dil
        nf = np.maximum(dist, 1).astype(np.float32)
        large = 16 + (np.log(nf / np.float32(16)) / np.float32(math.log(128.0))
                      * np.float32(16)).astype(np.int32)
        large = np.minimum(large, N_BUCKETS - 1)
        maps.append(np.where(dist < 16, dist, large).astype(np.int32))
    band = (steps >= 0) & (steps <= N_STEPS)
    first = band & (b >= QB)
    masks = np.stack([first, band]).astype(np.int32)
    return np.stack(maps), masks


def _bias_expand(rel_bias, buckets, masks):
    def body(tab_ref, bk_ref, mk_ref, o_ref):
        for g in range(3):
            bk = bk_ref[g]
            for h in range(4):
                col = 4 * g + h
                val = jnp.zeros((QB, 2 * QB), F32)
                for k in range(N_BUCKETS):
                    val = jnp.where(bk == k, tab_ref[k, col], val)
                o_ref[g, 0, h] = jnp.where(mk_ref[0] != 0, val, NEG_INF)
                o_ref[g, 1, h] = jnp.where(mk_ref[1] != 0, val, NEG_INF)

    return pl.pallas_call(
        body, name="bias_expand",
        in_specs=[pl.BlockSpec(memory_space=pltpu.SMEM),
                  pl.BlockSpec(memory_space=pltpu.VMEM),
                  pl.BlockSpec(memory_space=pltpu.VMEM)],
        out_shape=jax.ShapeDtypeStruct((3, 2, 4, QB, 2 * QB), F32),
    )(rel_bias, buckets, masks)


def _bias_grad(ds1, ds2, ds3, buckets):
    def body(d1_ref, d2_ref, d3_ref, bk_ref, o_ref):
        for g, d_ref in enumerate((d1_ref, d2_ref, d3_ref)):
            bk = bk_ref[g]
            for h in range(4):
                dv = d_ref[h]
                for k in range(N_BUCKETS):
                    o_ref[k, 4 * g + h] = jnp.sum(jnp.where(bk == k, dv, 0.0))

    return pl.pallas_call(
        body, name="bias_grad",
        in_specs=[pl.BlockSpec(memory_space=pltpu.VMEM)] * 4,
        out_specs=pl.BlockSpec(memory_space=pltpu.SMEM),
        out_shape=jax.ShapeDtypeStruct((N_BUCKETS, N_HEADS), F32),
    )(ds1, ds2, ds3, buckets)


def _scratch_sets(rows):
    return 4 if rows <= 512 else 1


def _unit_chunks(dil, size=16):
    units = [(h, r) for h in range(4) for r in range(dil)]
    return [units[i:i + size] for i in range(0, len(units), size)]


def _residue_rows(src_ref, copies, h, residue):
    buf = copies[h % len(copies)]
    buf[...] = src_ref[:, h * HD:(h + 1) * HD].astype(F32)
    return lambda r: buf[residue(r), :].astype(BF16)


def _attn_fwd(proj, bias, g):
    dil = DILATIONS[g]
    rows = QB * dil
    nsb = S // rows
    has_prev = nsb > 1

    def residue(r):
        return pl.ds(r, QB, stride=dil)

    n_sets = _scratch_sets(rows)
    n_in = 6 if has_prev else 4
    n_copied = (4 + (2 if has_prev else 0)) * n_sets

    def body(*refs):
        q_ref, kc_ref, vc_ref = refs[:3]
        kp_ref, vp_ref = refs[3:5] if has_prev else (None, None)
        b_ref = refs[n_in - 1]
        o_ref, l_ref = refs[n_in:n_in + 2]
        scr = list(refs[n_in + 2:])
        ls = [scr.pop(0) for _ in range(4)]
        copies = {name: [scr.pop(0) for _ in range(n_sets)]
                  for name in ("q", "kc", "vc", "o") + (("kp", "vp") if has_prev else ())}
        lane = lax.broadcasted_iota(jnp.int32, (QB, 128), 1)
        refs_of = {"q": q_ref, "kc": kc_ref, "vc": vc_ref, "kp": kp_ref, "vp": vp_ref}
        for chunk in _unit_chunks(dil):
            rows_of = {h: {name: _residue_rows(refs_of[name], copies[name], h, residue)
                           for name in refs_of if refs_of[name] is not None}
                       for h in sorted({h for h, _ in chunk})}

            def batch(name):
                return jnp.stack([rows_of[h][name](r) for h, r in chunk])

            q, k, v = batch("q"), batch("kc"), batch("vc")
            if has_prev:
                k = jnp.concatenate([batch("kp"), k], axis=1)
                v = jnp.concatenate([batch("vp"), v], axis=1)
                bias_b = jnp.stack([b_ref[h] for h, _ in chunk])
            else:
                bias_b = jnp.stack([b_ref[h, :, QB:] for h, _ in chunk])
            s = jnp.einsum("uqd,ukd->uqk", q, k, preferred_element_type=F32) * SCALE + bias_b
            m = jnp.max(s, axis=-1, keepdims=True)
            p = jnp.exp(s - m)
            l = jnp.sum(p, axis=-1, keepdims=True)
            o = jnp.einsum("uqk,ukd->uqd", p.astype(BF16), v, preferred_element_type=F32) / l
            lse = m + jnp.log(l)
            for i, (h, r) in enumerate(chunk):
                copies["o"][h % n_sets][residue(r), :] = o[i]
                ls[h][r * QB:(r + 1) * QB, :] = jnp.where(lane == h, lse[i], 0.0)
            for h in sorted({h for h, _ in chunk}):
                o_ref[:, h * HD:(h + 1) * HD] = copies["o"][h % n_sets][...]
        for r in range(dil):
            blk = slice(r * QB, (r + 1) * QB)
            l_ref[residue(r), :] = (ls[0][blk, :] + ls[1][blk, :]) + (ls[2][blk, :] + ls[3][blk, :])

    def row(b, n):
        return b * nsb + n

    def prev(b, n):
        return b * nsb + jnp.maximum(n - 1, 0)

    in_specs = [
        pl.BlockSpec((rows, GW), lambda b, n: (row(b, n), CB_Q + g)),
        pl.BlockSpec((rows, GW), lambda b, n: (row(b, n), CB_K + g)),
        pl.BlockSpec((rows, GW), lambda b, n: (row(b, n), CB_V + g)),
    ]
    args = [proj, proj, proj]
    scratch = [pltpu.VMEM((rows, 128), F32)] * (4 + n_copied)
    if has_prev:
        in_specs += [pl.BlockSpec((rows, GW), lambda b, n: (prev(b, n), CB_K + g)),
                     pl.BlockSpec((rows, GW), lambda b, n: (prev(b, n), CB_V + g))]
        args += [proj, proj]
    in_specs.append(pl.BlockSpec((None, None, 4, QB, 2 * QB),
                                 lambda b, n: (g, jnp.minimum(n, 1), 0, 0, 0)))
    args.append(bias)
    return pl.pallas_call(
        body, name=f"attn_fwd{g}",
        grid=(BL, nsb),
        in_specs=in_specs,
        out_specs=(pl.BlockSpec((rows, GW), lambda b, n: (row(b, n), 0)),
                   pl.BlockSpec((rows, 128), lambda b, n: (row(b, n), 0))),
        out_shape=(jax.ShapeDtypeStruct((T, GW), F32), jax.ShapeDtypeStruct((T, 128), F32)),
        scratch_shapes=scratch,
        compiler_params=pltpu.CompilerParams(vmem_limit_bytes=VMEM_LIMIT),
    )(*args)


def _attn_bwd(proj, d_out, stats, bias, dproj, g):
    dil = DILATIONS[g]
    rows = QB * dil
    nsb = S // rows
    has_prev = nsb > 1
    n_steps = nsb + 1 if has_prev else 1
    n_in = 7 + (2 if has_prev else 0)

    def residue(r):
        return pl.ds(r, QB, stride=dil)

    n_sets = _scratch_sets(rows)

    def body(*refs):
        q_ref, kc_ref, vc_ref, do_ref, st_ref, b_ref = refs[:6]
        kp_ref, vp_ref = refs[6:8] if has_prev else (None, None)
        out_ref, db_ref = refs[n_in], refs[n_in + 1]
        scr = list(refs[n_in + 2:])
        sq, sk, sv, sems = [scr.pop(0) for _ in range(4)]
        carry = scr.pop(0) if has_prev else None
        sts = scr.pop(0)
        copies = {name: [scr.pop(0) for _ in range(n_sets)]
                  for name in ("q", "kc", "vc", "do", "dq", "dk", "dv") + (("kp", "vp") if has_prev else ())}
        b, n = pl.program_id(0), pl.program_id(1)

        @pl.when((b == 0) & (n == 0))
        def _():
            db_ref[...] = jnp.zeros_like(db_ref)

        def finish(h, r, dq, dk, dv):
            for name, val in (("dq", dq), ("dk", dk), ("dv", dv)):
                copies[name][h % n_sets][residue(r), :] = val

        def finish_head(h):
            sl = slice(h * HD, (h + 1) * HD)
            sq[:, sl] = copies["dq"][h % n_sets][...].astype(BF16)
            sk[:, sl] = copies["dk"][h % n_sets][...].astype(BF16)
            sv[:, sl] = copies["dv"][h % n_sets][...].astype(BF16)

        def write_block(blk_idx):
            row0 = pl.multiple_of(blk_idx * rows, rows)
            _write_columns([(sq, CB * (CB_Q + g)), (sk, CB * (CB_K + g)), (sv, CB * (CB_V + g))],
                           out_ref, row0, sems)

        def carried(h, r):
            blk = slice(r * QB, (r + 1) * QB)
            return ((blk, slice(h * HD, (h + 1) * HD)), (blk, slice(GW + h * HD, GW + (h + 1) * HD)),
                    (blk, slice(2 * GW + h * HD, 2 * GW + (h + 1) * HD)))

        if has_prev:
            @pl.when(n == 0)
            def _():
                carry[...] = jnp.zeros_like(carry)

            @pl.when(n == nsb)
            def _():
                for h in range(4):
                    for r in range(dil):
                        cq, ck, cv = carried(h, r)
                        finish(h, r, carry[cq], carry[ck], carry[cv])
                    finish_head(h)
                write_block(b * nsb + nsb - 1)

        @pl.when(n < nsb)
        def _():
            for r in range(dil):
                sts[r * QB:(r + 1) * QB, :] = st_ref[residue(r), :]
            refs_of = {"q": q_ref, "kc": kc_ref, "vc": vc_ref, "do": do_ref, "kp": kp_ref, "vp": vp_ref}
            for chunk in _unit_chunks(dil):
                heads = sorted({h for h, _ in chunk})
                rows_of = {h: {name: _residue_rows(refs_of[name], copies[name], h, residue)
                               for name in refs_of if refs_of[name] is not None}
                           for h in heads}

                def batch(name):
                    return jnp.stack([rows_of[h][name](r) for h, r in chunk])

                q, k, v, do = batch("q"), batch("kc"), batch("vc"), batch("do")
                if has_prev:
                    k = jnp.concatenate([batch("kp"), k], axis=1)
                    v = jnp.concatenate([batch("vp"), v], axis=1)
                    bias_b = jnp.stack([b_ref[h] for h, _ in chunk])
                else:
                    bias_b = jnp.stack([b_ref[h, :, QB:] for h, _ in chunk])
                lse = jnp.stack([sts[r * QB:(r + 1) * QB, h:h + 1] for h, r in chunk])
                delta = jnp.stack([sts[r * QB:(r + 1) * QB, 4 + h:5 + h] for h, r in chunk])
                s = jnp.einsum("uqd,ukd->uqk", q, k, preferred_element_type=F32) * SCALE + bias_b
                p = jnp.exp(s - lse)
                ds = p * (jnp.einsum("uqd,ukd->uqk", do, v, preferred_element_type=F32) - delta)
                for h in heads:
                    mine = [ds[i] for i, (hh, _) in enumerate(chunk) if hh == h]
                    tot = mine[0]
                    for extra in mine[1:]:
                        tot = tot + extra
                    if has_prev:
                        db_ref[h] += tot
                    else:
                        db_ref[h, :, QB:] += tot
                dsb, pb = ds.astype(BF16), p.astype(BF16)
                dq = jnp.einsum("uqk,ukd->uqd", dsb, k, preferred_element_type=F32) * SCALE
                dk = jnp.einsum("uqk,uqd->ukd", dsb, q, preferred_element_type=F32) * SCALE
                dv = jnp.einsum("uqk,uqd->ukd", pb, do, preferred_element_type=F32)
                for i, (h, r) in enumerate(chunk):
                    if has_prev:
                        cq, ck, cv = carried(h, r)
                        finish(h, r, carry[cq], carry[ck] + dk[i, :QB], carry[cv] + dv[i, :QB])
                        carry[cq] = dq[i]
                        carry[ck] = dk[i, QB:]
                        carry[cv] = dv[i, QB:]
                    else:
                        finish(h, r, dq[i], dk[i], dv[i])
                for h in heads:
                    finish_head(h)
            if has_prev:
                @pl.when(n > 0)
                def _():
                    write_block(b * nsb + n - 1)
            else:
                write_block(b)

    def row(b, n):
        return b * nsb + jnp.minimum(n, nsb - 1)

    def prev(b, n):
        return b * nsb + jnp.maximum(jnp.minimum(n, nsb - 1) - 1, 0)

    in_specs = [
        pl.BlockSpec((rows, GW), lambda b, n: (row(b, n), CB_Q + g)),
        pl.BlockSpec((rows, GW), lambda b, n: (row(b, n), CB_K + g)),
        pl.BlockSpec((rows, GW), lambda b, n: (row(b, n), CB_V + g)),
        pl.BlockSpec((rows, GW), lambda b, n: (row(b, n), 0)),
        pl.BlockSpec((rows, 128), lambda b, n: (row(b, n), 0)),
        pl.BlockSpec((None, None, 4, QB, 2 * QB),
                     lambda b, n: (g, jnp.minimum(jnp.minimum(n, nsb - 1), 1), 0, 0, 0)),
    ]
    args = [proj, proj, proj, d_out, stats, bias]
    scratch = [pltpu.VMEM((rows, GW), BF16)] * 3 + [pltpu.SemaphoreType.DMA((3,))]
    if has_prev:
        in_specs += [pl.BlockSpec((rows, GW), lambda b, n: (prev(b, n), CB_K + g)),
                     pl.BlockSpec((rows, GW), lambda b, n: (prev(b, n), CB_V + g))]
        args += [proj, proj]
        scratch.append(pltpu.VMEM((rows, 3 * GW), F32))
    n_copied = (7 + (2 if has_prev else 0)) * n_sets
    scratch += [pltpu.VMEM((rows, 128), F32)] * (1 + n_copied)
    in_specs.append(pl.BlockSpec(memory_space=pl.ANY))
    args.append(dproj)
    return pl.pallas_call(
        body, name=f"attn_bwd{g}",
        grid=(BL, n_steps),
        in_specs=in_specs,
        out_specs=(pl.BlockSpec(memory_space=pl.ANY),
                   pl.BlockSpec((4, QB, 2 * QB), lambda b, n: (0, 0, 0))),
        out_shape=(jax.ShapeDtypeStruct((T, NCOL), BF16),
                   jax.ShapeDtypeStruct((4, QB, 2 * QB), F32)),
        scratch_shapes=scratch,
        input_output_aliases={len(args) - 1: 0},
        compiler_params=pltpu.CompilerParams(vmem_limit_bytes=VMEM_LIMIT),
    )(*args)


def _tail(x2, tgt2, mod3, o_g, lse_g, proj, w_ao, w_co, w_o, conv_w, conv_b, ln_g, ln_b):
    tm = 256
    per_seq = S // tm
    halo = 16

    def body(x_ref, t_ref, mod_ref, o1_ref, o2_ref, o3_ref, l1_ref, l2_ref, l3_ref,
             ga_ref, u_ref, bg_ref, cg_ref, gc_ref, ma_ref, mc_ref, up_ref, cp_ref,
             wao_ref, wco_ref, wo_ref, cw_ref, cb_ref, lg_ref, lb_ref,
             dproj_ref, dyc_ref, do_ref, st_ref, dxd_ref,
             gwo_ref, gwco_ref, gwao_ref, vec_ref,
             dga_s, dbg_s, dgm_s, sems, acc_o, acc_co, acc_ao):
        i = pl.program_id(0)
        bidx = i // per_seq
        first = (i % per_seq) == 0

        @pl.when(i == 0)
        def _():
            vec_ref[...] = jnp.zeros_like(vec_ref)

        l1, l2, l3 = l1_ref[...], l2_ref[...], l3_ref[...]
        mx = jnp.maximum(jnp.maximum(l1, l2), l3)
        e1, e2, e3 = jnp.exp(l1 - mx), jnp.exp(l2 - mx), jnp.exp(l3 - mx)
        esum = e1 + e2 + e3
        lse_tot = mx + jnp.log(esum)
        w1, w2, w3 = e1 / esum, e2 / esum, e3 / esum

        def per_head(wv):
            return jnp.concatenate([jnp.broadcast_to(wv[:, h:h + 1], (tm, HD)) for h in range(4)], axis=1)

        o = per_head(w1) * o1_ref[...] + per_head(w2) * o2_ref[...] + per_head(w3) * o3_ref[...]

        ga = ga_ref[...].astype(F32)
        sig_ga = _sigmoid(ga)
        silu_ga = ga * sig_ga
        a_in = (o * silu_ga).astype(BF16)
        a_out = _dot(a_in, wao_ref[...])

        u = u_ref[...].astype(F32)
        cg = cg_ref[...].astype(F32)
        z = cg * u
        zp = cp_ref[...].astype(F32) * up_ref[...].astype(F32)
        zp = jnp.where(first, 0.0, zp)
        zcat = jnp.concatenate([zp, z], axis=0)
        z1 = pltpu.roll(zcat, 1, 0)[halo:]
        z2 = pltpu.roll(zcat, 2, 0)[halo:]
        y_conv = cw_ref[0:1, :] * z2 + cw_ref[1:2, :] * z1 + cw_ref[2:3, :] * z + cb_ref[...]
        gc = gc_ref[...].astype(F32)
        sig_gc = _sigmoid(gc)
        silu_gc = gc * sig_gc
        bg = bg_ref[...].astype(F32)
        bg_yc = bg * y_conv
        s_in = (bg_yc * silu_gc).astype(BF16)
        s_out = _dot(s_in, wco_ref[...])

        sa = _sigmoid(ma_ref[...].astype(F32))
        sc = _sigmoid(mc_ref[...].astype(F32))
        merged = (sa * a_out + sc * s_out).astype(BF16)
        y = _dot(merged, wo_ref[...])
        gate1 = 1.0 + mod_ref[0, 2:3, :]
        xv = x_ref[...]
        resid = ALPHA * xv + gate1 * y
        mu = jnp.mean(resid, axis=1, keepdims=True)
        xc = resid - mu
        var = jnp.mean(xc * xc, axis=1, keepdims=True)
        rstd = lax.rsqrt(var + LN_EPS)
        xhat = xc * rstd
        lg = lg_ref[...]
        err = xhat * lg + lb_ref[...] - t_ref[...]
        vec_ref[3:4, :] += (0.5 / D) * jnp.sum(err * err, axis=0, keepdims=True)

        vec_ref[1:2, :] += (1.0 / D) * jnp.sum(err * xhat, axis=0, keepdims=True)
        vec_ref[2:3, :] += (1.0 / D) * jnp.sum(err, axis=0, keepdims=True)
        dxh = err * (lg * (1.0 / D))
        dres = rstd * (dxh - jnp.mean(dxh, axis=1, keepdims=True)
                       - xhat * jnp.mean(dxh * xhat, axis=1, keepdims=True))
        dxd_ref[...] = ALPHA * dres
        dgate = jnp.sum(dres * y, axis=0, keepdims=True)
        vec_ref[4:5, :] += jnp.where(bidx == 0, dgate, 0.0)
        vec_ref[5:6, :] += jnp.where(bidx == 1, dgate, 0.0)
        dy = (dres * gate1).astype(BF16)

        dmerged = _dot_nt(dy, wo_ref[...])
        da_out_f = dmerged * sa
        ds_out_f = dmerged * sc
        da_out = da_out_f.astype(BF16)
        ds_out = ds_out_f.astype(BF16)
        dgm_s[:, 2 * D:3 * D] = (ds_out_f * s_out * (1.0 - sc)).astype(BF16)
        dgm_s[:, D:2 * D] = (da_out_f * a_out * (1.0 - sa)).astype(BF16)
        da_in = _dot_nt(da_out, wao_ref[...])
        ds_in = _dot_nt(ds_out, wco_ref[...])

        d_o = da_in * silu_ga
        do_ref[...] = d_o.astype(BF16)
        dga_s[...] = (da_in * o * (sig_ga + silu_ga * (1.0 - sig_ga))).astype(BF16)
        lane = lax.broadcasted_iota(jnp.int32, (tm, 128), 1)
        stats = lse_tot
        od = o * d_o
        for h in range(4):
            delta = jnp.sum(od[:, h * HD:(h + 1) * HD], axis=1, keepdims=True)
            stats = jnp.where(lane == 4 + h, delta, stats)
        st_ref[...] = stats

        ds_silu = ds_in * silu_gc
        dbg_s[...] = (ds_silu * y_conv).astype(BF16)
        dyc = ds_silu * bg
        dyc_ref[...] = dyc
        vec_ref[0:1, :] += jnp.sum(dyc, axis=0, keepdims=True)
        dgm_s[:, 0:D] = (ds_in * bg_yc * (sig_gc + silu_gc * (1.0 - sig_gc))).astype(BF16)

        @pl.when(i == 0)
        def _():
            acc_o[...] = jnp.zeros_like(acc_o)
            acc_co[...] = jnp.zeros_like(acc_co)
            acc_ao[...] = jnp.zeros_like(acc_ao)

        acc_o[...] += _dot_tn(merged, dy)
        acc_co[...] += _dot_tn(s_in, ds_out)
        acc_ao[...] += _dot_tn(a_in, da_out)

        @pl.when(i == T // tm - 1)
        def _():
            gwo_ref[...] = acc_o[...].astype(BF16)
            gwco_ref[...] = acc_co[...].astype(BF16)
            gwao_ref[...] = acc_ao[...].astype(BF16)

        _write_columns([(dga_s, CB * CB_GA), (dbg_s, D * KB_BG), (dgm_s, D * KB_GC)],
                       dproj_ref, pl.multiple_of(i * tm, tm), sems)

    def tile(width, cblk=0):
        return pl.BlockSpec((tm, width), lambda i: (i, cblk))

    def whole(shape):
        return pl.BlockSpec(shape, lambda i: tuple(0 for _ in shape))

    def once(shape):
        return pl.BlockSpec(shape, lambda i: tuple(0 for _ in shape), pipeline_mode=pl.Buffered(1))

    prev_rows = lambda i: (jnp.maximum(i * (tm // halo) - 1, 0),)
    in_specs = [
        tile(D), tile(D), pl.BlockSpec((1, 3, D), lambda i: (i // per_seq, 0, 0)),
        tile(GW), tile(GW), tile(GW), tile(128), tile(128), tile(128),
        tile(GW, CB_GA), tile(D, KB_U), tile(D, KB_BG), tile(D, KB_CG), tile(D, KB_GC),
        tile(D, KB_MA), tile(D, KB_MC),
        pl.BlockSpec((halo, D), lambda i: (*prev_rows(i), KB_U)),
        pl.BlockSpec((halo, D), lambda i: (*prev_rows(i), KB_CG)),
        whole((GW, D)), whole((D, D)), whole((D, D)),
        whole((3, D)), whole((1, D)), whole((1, D)), whole((1, D)),
    ]
    out_specs = (
        pl.BlockSpec(memory_space=pl.ANY), tile(D), tile(GW), tile(128), tile(D),
        once((D, D)), once((D, D)), once((GW, D)),
        pl.BlockSpec((8, D), lambda i: (0, 0)),
    )
    out_shape = (
        jax.ShapeDtypeStruct((T, NCOL), BF16),
        jax.ShapeDtypeStruct((T, D), F32),
        jax.ShapeDtypeStruct((T, GW), BF16),
        jax.ShapeDtypeStruct((T, 128), F32),
        jax.ShapeDtypeStruct((T, D), F32),
        jax.ShapeDtypeStruct((D, D), BF16),
        jax.ShapeDtypeStruct((D, D), BF16),
        jax.ShapeDtypeStruct((GW, D), BF16),
        jax.ShapeDtypeStruct((8, D), F32),
    )
    return pl.pallas_call(
        body, name="tail",
        grid=(T // tm,),
        in_specs=in_specs, out_specs=out_specs, out_shape=out_shape,
        scratch_shapes=[pltpu.VMEM((tm, GW), BF16), pltpu.VMEM((tm, D), BF16), pltpu.VMEM((tm, 3 * D), BF16),
                        pltpu.SemaphoreType.DMA((3,)),
                        pltpu.VMEM((D, D), F32), pltpu.VMEM((D, D), F32), pltpu.VMEM((GW, D), F32)],
        compiler_params=pltpu.CompilerParams(vmem_limit_bytes=VMEM_LIMIT_TAIL),
    )(x2, tgt2, mod3, *o_g, *lse_g, proj, proj, proj, proj, proj, proj, proj, proj, proj,
      w_ao, w_co, w_o, conv_w, conv_b, ln_g, ln_b)


def _conv_bwd(dyc, proj, conv_w, dproj):
    tm = 512
    per_seq = S // tm

    def body(d_ref, dn_ref, u_ref, c_ref, cw_ref, _, dproj_ref, g_ref, du_s, dc_s, sems):
        i = pl.program_id(0)
        last = (i % per_seq) == per_seq - 1

        @pl.when(i == 0)
        def _():
            g_ref[...] = jnp.zeros_like(g_ref)

        d = d_ref[...]
        dn = jnp.where(last, 0.0, dn_ref[...])
        dcat = jnp.concatenate([d, dn], axis=0)
        d1 = pltpu.roll(dcat, tm + 8 - 1, 0)[:tm]
        d2 = pltpu.roll(dcat, tm + 8 - 2, 0)[:tm]
        dz = cw_ref[2:3, :] * d + cw_ref[1:2, :] * d1 + cw_ref[0:1, :] * d2
        u = u_ref[...].astype(F32)
        cg = c_ref[...].astype(F32)
        du_s[...] = (dz * cg).astype(BF16)
        dc_s[...] = (dz * u).astype(BF16)
        _write_columns([(du_s, D * KB_U), (dc_s, D * KB_CG)], dproj_ref, pl.multiple_of(i * tm, tm), sems)

        z = cg * u
        g_ref[0:1, :] += jnp.sum(d2 * z, axis=0, keepdims=True)
        g_ref[1:2, :] += jnp.sum(d1 * z, axis=0, keepdims=True)
        g_ref[2:3, :] += jnp.sum(d * z, axis=0, keepdims=True)

    n_tiles = T // tm
    next_rows = lambda i: jnp.minimum((i + 1) * (tm // 8), T // 8 - 1)
    return pl.pallas_call(
        body, name="conv_bwd",
        grid=(n_tiles,),
        in_specs=[pl.BlockSpec((tm, D), lambda i: (i, 0)),
                  pl.BlockSpec((8, D), lambda i: (next_rows(i), 0)),
                  pl.BlockSpec((tm, D), lambda i: (i, KB_U)),
                  pl.BlockSpec((tm, D), lambda i: (i, KB_CG)),
                  pl.BlockSpec((3, D), lambda i: (0, 0)),
                  pl.BlockSpec(memory_space=pl.ANY)],
        out_specs=(pl.BlockSpec(memory_space=pl.ANY),
                   pl.BlockSpec((8, D), lambda i: (0, 0))),
        out_shape=(jax.ShapeDtypeStruct((T, NCOL), BF16),
                   jax.ShapeDtypeStruct((8, D), F32)),
        scratch_shapes=[pltpu.VMEM((tm, D), BF16), pltpu.VMEM((tm, D), BF16), pltpu.SemaphoreType.DMA((2,))],
        input_output_aliases={5: 0},
        compiler_params=pltpu.CompilerParams(vmem_limit_bytes=VMEM_LIMIT),
    )(dyc, dyc, proj, proj, conv_w, dproj)


def _dh_dx(dproj, w_in_all, x2, dxd, mod3, chip_sums, hops=(), parts0=None):
    tm = 1024
    per_seq = S // tm
    n = len(chip_sums)
    n_in = 5 + n + (0 if parts0 is None else 1)

    def body(*refs):
        d_ref, w_ref, x_ref, dxd_ref, mod_ref = refs[:5]
        ins = refs[5:5 + n]
        gx_ref, vec_ref = refs[n_in:n_in + 2]
        outs = refs[n_in + 2:n_in + 2 + n]
        acc, send_sems, recv_sems, local_sems = refs[n_in + 2 + n:]
        i, jj = pl.program_id(0), pl.program_id(1)

        @pl.when((i == 0) & (jj == 0))
        def _():
            vec_ref[...] = jnp.zeros_like(vec_ref)
            if n:
                sends, _, mine = _chip_copies(ins, outs, send_sems, recv_sems, local_sems, hops)
                for cp in sends + mine:
                    cp.start()

        if n:
            @pl.when((i == T // tm - 1) & (jj == N_DEV - 1))
            def _():
                sends, arrivals, mine = _chip_copies(ins, outs, send_sems, recv_sems, local_sems, hops)
                for cp in arrivals:
                    cp.wait_recv()
                for cp in sends:
                    cp.wait_send()
                for cp in mine:
                    cp.wait()

        @pl.when(jj == 0)
        def _():
            acc[...] = jnp.zeros_like(acc)

        acc[...] += _dot_nt(d_ref[...], w_ref[...])

        @pl.when(jj == N_DEV - 1)
        def _():
            dh = acc[...]
            bidx = i // per_seq
            gx_ref[...] = dxd_ref[...] + dh * (1.0 + mod_ref[0, 1:2, :])
            dshift = jnp.sum(dh, axis=0, keepdims=True)
            dscale = jnp.sum(dh * x_ref[...], axis=0, keepdims=True)
            vec_ref[0:1, :] += jnp.where(bidx == 0, dshift, 0.0)
            vec_ref[1:2, :] += jnp.where(bidx == 1, dshift, 0.0)
            vec_ref[2:3, :] += jnp.where(bidx == 0, dscale, 0.0)
            vec_ref[3:4, :] += jnp.where(bidx == 1, dscale, 0.0)

    any_spec = pl.BlockSpec(memory_space=pl.ANY)
    res = pl.pallas_call(
        body, name="dh_dx",
        grid=(T // tm, N_DEV),
        in_specs=[
            pl.BlockSpec((tm, SHARD), lambda i, jj: (i, jj)),
            pl.BlockSpec((None, D, SHARD), lambda i, jj: (jj, 0, 0)),
            pl.BlockSpec((tm, D), lambda i, jj: (i, 0)),
            pl.BlockSpec((tm, D), lambda i, jj: (i, 0)),
            pl.BlockSpec((1, 3, D), lambda i, jj: (i // per_seq, 0, 0))] + [any_spec] * (n_in - 5),
        out_specs=(pl.BlockSpec((tm, D), lambda i, jj: (i, 0)),
                   pl.BlockSpec((8, D), lambda i, jj: (0, 0))) + (any_spec,) * n,
        out_shape=(jax.ShapeDtypeStruct((T, D), F32), jax.ShapeDtypeStruct((8, D), F32))
                  + tuple(jax.ShapeDtypeStruct(a.shape, a.dtype) for a in chip_sums),
        scratch_shapes=[pltpu.VMEM((tm, D), F32), pltpu.SemaphoreType.DMA((max(3 * n, 1),)),
                        pltpu.SemaphoreType.DMA((max(3 * n, 1),)), pltpu.SemaphoreType.DMA((max(n, 1),))],
        input_output_aliases={} if parts0 is None else {5 + n: 2},
        compiler_params=pltpu.CompilerParams(vmem_limit_bytes=VMEM_LIMIT),
    )(dproj, w_in_all, x2, dxd, mod3, *chip_sums, *([] if parts0 is None else [parts0]))
    return res[0], res[1], res[2:]


def _mm_tn(a, b, tn, blocks_leading, name):
    kk, m = a.shape
    n = b.shape[1]
    tk = 2048

    def body(a_ref, b_ref, o_ref, acc):
        @pl.when(pl.program_id(1) == 0)
        def _():
            acc[...] = jnp.zeros_like(acc)

        acc[...] += _dot_tn(a_ref[...], b_ref[...])

        @pl.when(pl.program_id(1) == kk // tk - 1)
        def _():
            o_ref[...] = acc[...].astype(BF16)

    if blocks_leading:
        out_spec = pl.BlockSpec((None, m, tn), lambda j, k: (j, 0, 0))
        out_shape = jax.ShapeDtypeStruct((n // tn, m, tn), BF16)
    else:
        out_spec = pl.BlockSpec((m, tn), lambda j, k: (0, j))
        out_shape = jax.ShapeDtypeStruct((m, n), BF16)
    return pl.pallas_call(
        body, name=name,
        grid=(n // tn, kk // tk),
        in_specs=[pl.BlockSpec((tk, m), lambda j, k: (k, 0)),
                  pl.BlockSpec((tk, tn), lambda j, k: (k, j))],
        out_specs=out_spec, out_shape=out_shape,
        scratch_shapes=[pltpu.VMEM((m, tn), F32)],
        compiler_params=pltpu.CompilerParams(vmem_limit_bytes=VMEM_LIMIT),
    )(a, b)


def _adam_step(g, w, m, v):
    nm = ADAM_B1 * m + (1.0 - ADAM_B1) * g
    nv = ADAM_B2 * v + (1.0 - ADAM_B2) * (g * g)
    m_hat = nm / (1.0 - ADAM_B1 ** ADAM_STEP)
    v_hat = nv / (1.0 - ADAM_B2 ** ADAM_STEP)
    return -ADAM_LR * (m_hat / (jnp.sqrt(v_hat) + ADAM_EPS) + ADAM_WD * w), nm, nv


def _adamw(parts, w, m, v, name, row_tile=None):
    n_parts, rows, cols = parts.shape
    tr = rows if row_tile is None else row_tile

    def body(p_ref, w_ref, m_ref, v_ref, g_ref, d_ref, nm_ref, nv_ref):
        g = p_ref[0].astype(F32)
        for s in range(1, n_parts):
            g = g + p_ref[s].astype(F32)
        g_ref[...] = g
        d_ref[...], nm_ref[...], nv_ref[...] = _adam_step(g, w_ref[...], m_ref[...], v_ref[...])

    blk = pl.BlockSpec((tr, cols), lambda i: (i, 0))
    shp = jax.ShapeDtypeStruct((rows, cols), F32)
    return pl.pallas_call(
        body, name=name,
        grid=(rows // tr,),
        in_specs=[pl.BlockSpec((n_parts, tr, cols), lambda i: (0, i, 0)), blk, blk, blk],
        out_specs=(blk, blk, blk, blk),
        out_shape=(shp, shp, shp, shp),
        compiler_params=pltpu.CompilerParams(vmem_limit_bytes=VMEM_LIMIT),
    )(parts, w, m, v)


def _multi_adamw(parts_list, params, name):
    n = len(params)
    flat = [t for wmv in params for t in wmv]

    def body(*refs):
        parts, ins, outs = refs[:n], refs[n:4 * n], refs[4 * n:]
        for p in range(n):
            g = parts[p][0].astype(F32)
            for s in range(1, parts[p].shape[0]):
                g = g + parts[p][s].astype(F32)
            w_ref, m_ref, v_ref = ins[3 * p:3 * p + 3]
            g_ref, d_ref, nm_ref, nv_ref = outs[4 * p:4 * p + 4]
            g_ref[...] = g
            d_ref[...], nm_ref[...], nv_ref[...] = _adam_step(g, w_ref[...], m_ref[...], v_ref[...])

    out_shape = []
    for w, _, _ in params:
        out_shape += [jax.ShapeDtypeStruct(w.shape, F32)] * 4
    res = pl.pallas_call(body, name=name, out_shape=tuple(out_shape))(*parts_list, *flat)
    return [res[4 * p:4 * p + 4] for p in range(n)]


def _small_updates(small_g, dmod_all, rel_parts, params):
    flat = [t for wmv in params for t in wmv]

    def body(sg_ref, dm_ref, rp_ref, *refs):
        ins, outs = refs[:len(flat)], refs[len(flat):]

        def over_devices(row):
            tot = sg_ref[0, row:row + 1, :]
            for s in range(1, N_DEV):
                tot = tot + sg_ref[s, row:row + 1, :]
            return tot

        g_b_ada = dm_ref[0:1, :]
        for r in range(1, N_DEV * BL):
            g_b_ada = g_b_ada + dm_ref[r:r + 1, :]
        g_rel = rp_ref[0]
        for s in range(1, N_DEV):
            g_rel = g_rel + rp_ref[s]
        grads = [g_b_ada, over_devices(0), g_rel, over_devices(1), over_devices(2)]
        outs[0][...] = jnp.sum(over_devices(3), axis=1, keepdims=True)
        for p, g in enumerate(grads):
            w_ref, m_ref, v_ref = ins[3 * p:3 * p + 3]
            g_ref, d_ref, nm_ref, nv_ref = outs[1 + 4 * p:5 + 4 * p]
            g_ref[...] = g
            d_ref[...], nm_ref[...], nv_ref[...] = _adam_step(g, w_ref[...], m_ref[...], v_ref[...])

    out_shape = [jax.ShapeDtypeStruct((1, 1), F32)]
    for w, _, _ in params:
        out_shape += [jax.ShapeDtypeStruct(w.shape, F32)] * 4
    res = pl.pallas_call(body, name="small_updates", out_shape=tuple(out_shape))(small_g, dmod_all, rel_parts, *flat)
    return res[0], [res[1 + 4 * p:5 + 4 * p] for p in range(len(params))]


def _attn_fwd_dense(proj, bias):
    nq = 4
    rows = nq * QB
    nsb = S // rows

    def body(q_ref, k_ref, v_ref, kp_ref, vp_ref, b_ref, o_ref, l_ref, ls0, ls1, ls2, ls3):
        ls = [ls0, ls1, ls2, ls3]
        n = pl.program_id(1)
        lane = lax.broadcasted_iota(jnp.int32, (QB, 128), 1)
        units = [(h, j) for h in range(4) for j in range(nq)]

        def keys(cur_ref, prev_ref, h, j):
            sl = slice(h * HD, (h + 1) * HD)
            if j == 0:
                return jnp.concatenate([prev_ref[:, sl], cur_ref[0:QB, sl]], axis=0)
            return cur_ref[(j - 1) * QB:(j + 1) * QB, sl]

        q = jnp.stack([q_ref[j * QB:(j + 1) * QB, h * HD:(h + 1) * HD] for h, j in units])
        k = jnp.stack([keys(k_ref, kp_ref, h, j) for h, j in units])
        v = jnp.stack([keys(v_ref, vp_ref, h, j) for h, j in units])
        bias_b = jnp.stack([b_ref[jnp.minimum(n, 1), h] if j == 0 else b_ref[1, h] for h, j in units])
        s = jnp.einsum("uqd,ukd->uqk", q, k, preferred_element_type=F32) * SCALE + bias_b
        m = jnp.max(s, axis=-1, keepdims=True)
        p = jnp.exp(s - m)
        l = jnp.sum(p, axis=-1, keepdims=True)
        o = jnp.einsum("uqk,ukd->uqd", p.astype(BF16), v, preferred_element_type=F32) / l
        lse = m + jnp.log(l)
        for i, (h, j) in enumerate(units):
            o_ref[j * QB:(j + 1) * QB, h * HD:(h + 1) * HD] = o[i]
            ls[h][j * QB:(j + 1) * QB, :] = jnp.where(lane == h, lse[i], 0.0)
        l_ref[...] = (ls[0][...] + ls[1][...]) + (ls[2][...] + ls[3][...])

    def row(b, n):
        return b * nsb + n

    def prev(b, n):
        return jnp.maximum((b * nsb + n) * nq - 1, 0)

    in_specs = [
        pl.BlockSpec((rows, GW), lambda b, n: (row(b, n), CB_Q)),
        pl.BlockSpec((rows, GW), lambda b, n: (row(b, n), CB_K)),
        pl.BlockSpec((rows, GW), lambda b, n: (row(b, n), CB_V)),
        pl.BlockSpec((QB, GW), lambda b, n: (prev(b, n), CB_K)),
        pl.BlockSpec((QB, GW), lambda b, n: (prev(b, n), CB_V)),
        pl.BlockSpec((None, 2, 4, QB, 2 * QB), lambda b, n: (0, 0, 0, 0, 0)),
    ]
    return pl.pallas_call(
        body, name="attn_fwd0",
        grid=(BL, nsb),
        in_specs=in_specs,
        out_specs=(pl.BlockSpec((rows, GW), lambda b, n: (row(b, n), 0)),
                   pl.BlockSpec((rows, 128), lambda b, n: (row(b, n), 0))),
        out_shape=(jax.ShapeDtypeStruct((T, GW), F32), jax.ShapeDtypeStruct((T, 128), F32)),
        scratch_shapes=[pltpu.VMEM((rows, 128), F32)] * 4,
        compiler_params=pltpu.CompilerParams(vmem_limit_bytes=VMEM_LIMIT),
    )(proj, proj, proj, proj, proj, bias)


def _attn_bwd_dense(proj, d_out, stats, bias, dproj):
    nq = 4
    rows = nq * QB
    nsb = S // rows
    cols_q, cols_k, cols_v = CB * CB_Q, CB * CB_K, CB * CB_V

    def body(q_ref, k_ref, v_ref, do_ref, st_ref, kp_ref, vp_ref, b_ref, _, out_ref, db_ref,
             sq, sk, sv, carry, sems):
        b, n = pl.program_id(0), pl.program_id(1)
        units = [(h, j) for h in range(4) for j in range(nq)]

        @pl.when((b == 0) & (n == 0))
        def _():
            db_ref[...] = jnp.zeros_like(db_ref)

        @pl.when(n == 0)
        def _():
            carry[...] = jnp.zeros_like(carry)

        def write(first_block, position, count):
            row0 = pl.multiple_of(first_block * QB, QB)
            part = pl.ds(position * QB, count * QB)
            _write_columns([(sq.at[part], cols_q), (sk.at[part], cols_k), (sv.at[part], cols_v)],
                           out_ref, row0, sems)

        @pl.when(n == nsb)
        def _():
            sq[0:QB, :] = carry[:, 0:GW].astype(BF16)
            sk[0:QB, :] = carry[:, GW:2 * GW].astype(BF16)
            sv[0:QB, :] = carry[:, 2 * GW:3 * GW].astype(BF16)
            write((b + 1) * nsb * nq - 1, 0, 1)

        @pl.when(n < nsb)
        def _():
            def keys(cur_ref, prev_ref, h, j):
                sl = slice(h * HD, (h + 1) * HD)
                if j == 0:
                    return jnp.concatenate([prev_ref[:, sl], cur_ref[0:QB, sl]], axis=0)
                return cur_ref[(j - 1) * QB:(j + 1) * QB, sl]

            def block(ref, h, j):
                return ref[j * QB:(j + 1) * QB, h * HD:(h + 1) * HD]

            q = jnp.stack([block(q_ref, h, j) for h, j in units])
            do = jnp.stack([block(do_ref, h, j) for h, j in units])
            k = jnp.stack([keys(k_ref, kp_ref, h, j) for h, j in units])
            v = jnp.stack([keys(v_ref, vp_ref, h, j) for h, j in units])
            bias_b = jnp.stack([b_ref[jnp.minimum(n, 1), h] if j == 0 else b_ref[1, h] for h, j in units])
            lse = jnp.stack([st_ref[j * QB:(j + 1) * QB, h:h + 1] for h, j in units])
            delta = jnp.stack([st_ref[j * QB:(j + 1) * QB, 4 + h:5 + h] for h, j in units])
            s = jnp.einsum("uqd,ukd->uqk", q, k, preferred_element_type=F32) * SCALE + bias_b
            p = jnp.exp(s - lse)
            ds = p * (jnp.einsum("uqd,ukd->uqk", do, v, preferred_element_type=F32) - delta)
            for h in range(4):
                tot = ds[h * nq]
                for j in range(1, nq):
                    tot = tot + ds[h * nq + j]
                db_ref[h] += tot
            dsb, pb = ds.astype(BF16), p.astype(BF16)
            dq = jnp.einsum("uqk,ukd->uqd", dsb, k, preferred_element_type=F32) * SCALE
            dk = jnp.einsum("uqk,uqd->ukd", dsb, q, preferred_element_type=F32) * SCALE
            dv = jnp.einsum("uqk,uqd->ukd", pb, do, preferred_element_type=F32)
            for h in range(4):
                sl = slice(h * HD, (h + 1) * HD)
                u0, last = h * nq, h * nq + nq - 1
                sq[0:QB, sl] = carry[:, sl].astype(BF16)
                sk[0:QB, sl] = (carry[:, GW + h * HD:GW + (h + 1) * HD] + dk[u0, :QB]).astype(BF16)
                sv[0:QB, sl] = (carry[:, 2 * GW + h * HD:2 * GW + (h + 1) * HD] + dv[u0, :QB]).astype(BF16)
                for j in range(nq - 1):
                    pos = slice((j + 1) * QB, (j + 2) * QB)
                    sq[pos, sl] = dq[u0 + j].astype(BF16)
                    sk[pos, sl] = (dk[u0 + j, QB:] + dk[u0 + j + 1, :QB]).astype(BF16)
                    sv[pos, sl] = (dv[u0 + j, QB:] + dv[u0 + j + 1, :QB]).astype(BF16)
                carry[:, sl] = dq[last]
                carry[:, GW + h * HD:GW + (h + 1) * HD] = dk[last, QB:]
                carry[:, 2 * GW + h * HD:2 * GW + (h + 1) * HD] = dv[last, QB:]

            @pl.when(n == 0)
            def _():
                write(b * nsb * nq, 1, nq - 1)

            @pl.when(n > 0)
            def _():
                write((b * nsb + n) * nq - 1, 0, nq)

    def row(b, n):
        return b * nsb + jnp.minimum(n, nsb - 1)

    def prev(b, n):
        return jnp.maximum(row(b, n) * nq - 1, 0)

    in_specs = [
        pl.BlockSpec((rows, GW), lambda b, n: (row(b, n), CB_Q)),
        pl.BlockSpec((rows, GW), lambda b, n: (row(b, n), CB_K)),
        pl.BlockSpec((rows, GW), lambda b, n: (row(b, n), CB_V)),
        pl.BlockSpec((rows, GW), lambda b, n: (row(b, n), 0)),
        pl.BlockSpec((rows, 128), lambda b, n: (row(b, n), 0)),
        pl.BlockSpec((QB, GW), lambda b, n: (prev(b, n), CB_K)),
        pl.BlockSpec((QB, GW), lambda b, n: (prev(b, n), CB_V)),
        pl.BlockSpec((None, 2, 4, QB, 2 * QB), lambda b, n: (0, 0, 0, 0, 0)),
        pl.BlockSpec(memory_space=pl.ANY),
    ]
    return pl.pallas_call(
        body, name="attn_bwd0",
        grid=(BL, nsb + 1),
        in_specs=in_specs,
        out_specs=(pl.BlockSpec(memory_space=pl.ANY),
                   pl.BlockSpec((4, QB, 2 * QB), lambda b, n: (0, 0, 0))),
        out_shape=(jax.ShapeDtypeStruct((T, NCOL), BF16),
                   jax.ShapeDtypeStruct((4, QB, 2 * QB), F32)),
        scratch_shapes=[pltpu.VMEM((rows, GW), BF16)] * 3
                       + [pltpu.VMEM((QB, 3 * GW), F32), pltpu.SemaphoreType.DMA((3,))],
        input_output_aliases={8: 0},
        compiler_params=pltpu.CompilerParams(vmem_limit_bytes=VMEM_LIMIT),
    )(proj, proj, proj, d_out, stats, proj, proj, bias, dproj)


def _attention_forward(proj, rel_bias):
    buckets_np, masks_np = _bucket_maps()
    buckets, masks = jnp.asarray(buckets_np), jnp.asarray(masks_np)
    bias = _bias_expand(rel_bias, buckets, masks)
    fwd = [_attn_fwd_dense(proj, bias)] + [_attn_fwd(proj, bias, g) for g in (1, 2)]
    return bias, buckets, [f[0] for f in fwd], [f[1] for f in fwd]


def _local_step(x2, tgt2, mod3, h, proj, attn, w_ao, w_co, w_o, conv_w, conv_b, ln_g, ln_b):
    bias, buckets, o_g, lse_g = attn

    (dproj, dyc, d_o, stats, dxd, gw_o, gw_co, gw_ao, tail_vec) = _tail(
        x2, tgt2, mod3, o_g, lse_g, proj, w_ao, w_co, w_o, conv_w, conv_b, ln_g, ln_b)

    dproj, db = _attn_bwd_dense(proj, d_o, stats, bias, dproj)
    dbias = [db]
    for g in (1, 2):
        dproj, db = _attn_bwd(proj, d_o, stats, bias, dproj, g)
        dbias.append(db)
    g_rel_bias = _bias_grad(*dbias, buckets)
    dproj, conv_vec = _conv_bwd(dyc, proj, conv_w, dproj)

    gw_ao = jnp.transpose(gw_ao.reshape(GW, N_DEV, D // N_DEV), (1, 0, 2))
    return dproj, dxd, gw_ao, gw_co, gw_o, conv_vec, g_rel_bias, tail_vec


def kernel(x, c, w_ada, b_ada, w_in, conv_w, conv_b, rel_bias, w_attn_out, w_conv_out, w_o, ln_g, ln_b, loss_target, m_w_ada, m_b_ada, m_w_in, m_conv_w, m_conv_b, m_rel_bias, m_w_attn_out, m_w_conv_out, m_w_o, m_ln_g, m_ln_b, v_w_ada, v_b_ada, v_w_in, v_conv_w, v_conv_b, v_rel_bias, v_w_attn_out, v_w_conv_out, v_w_o, v_ln_g, v_ln_b):
    me = _my_index()
    x2 = x.reshape(T, D)
    tgt2 = loss_target.reshape(T, D)

    b_cols = lax.dynamic_slice(b_ada, (0, me * ADA_SHARD), (1, ADA_SHARD))
    c_g, mod_in = _mod_exchange(jnp.pad(c, ((0, 8 - BL), (0, 0))), w_ada[0], b_cols)
    c_all = c_g[:, 0:BL, :].reshape(N_DEV * BL, D)
    mod3 = jnp.transpose(mod_in[:, 0:BL, :], (1, 0, 2)).reshape(BL, 3, D)

    h = _prep_h(x2, mod3)
    rows_shape = jax.ShapeDtypeStruct((N_DEV, D // N_DEV, D), BF16)
    proj, w_in_all, (w_ao_g, w_co_g, w_o_g, conv_w_g) = _gather_proj(
        _shard_order(), h, w_in[0].astype(BF16), 1024,
        ([w_attn_out[0].astype(BF16), w_conv_out[0].astype(BF16), w_o[0].astype(BF16), conv_w[0]],
         [jax.ShapeDtypeStruct((N_DEV, GW, D // N_DEV), BF16), rows_shape, rows_shape,
          jax.ShapeDtypeStruct((N_DEV, 3, D // N_DEV), F32)]))

    attn = _attention_forward(proj, rel_bias)
    w_ao_full = jnp.transpose(w_ao_g, (1, 0, 2)).reshape(GW, D)
    w_co_full = w_co_g.reshape(D, D)
    w_o_full = w_o_g.reshape(D, D)
    conv_w_full = jnp.transpose(conv_w_g, (1, 0, 2)).reshape(3, D)

    (dproj, dxd, gw_ao, gw_co, gw_o, conv_vec, g_rel_bias, tail_vec) = _local_step(
        x2, tgt2, mod3, h, proj, attn, w_ao_full, w_co_full, w_o_full,
        conv_w_full, conv_b, ln_g, ln_b)

    g_conv_w_blocks = jnp.transpose(conv_vec[0:3].reshape(3, N_DEV, D // N_DEV), (1, 0, 2))
    partials = [gw_ao, gw_co.reshape(N_DEV, D // N_DEV, D), gw_o.reshape(N_DEV, D // N_DEV, D), g_conv_w_blocks]
    w_in_sums, w_in_parts, sib = _gw_in_pair(
        _slice_order(), h, dproj, partials,
        [jax.ShapeDtypeStruct((4, GW, D // N_DEV), BF16),
         jax.ShapeDtypeStruct((4, D // N_DEV, D), BF16),
         jax.ShapeDtypeStruct((4, D // N_DEV, D), BF16),
         jax.ShapeDtypeStruct((4, 3, D // N_DEV), F32)])
    core = lax.axis_index("c").astype(jnp.int32).reshape(1)
    chip_sums = [w_in_sums] + list(_pair_add(core, partials, sib))
    hops = [(3,)] + [(1, 2, 3)] * 4
    grad_x, mod_vec, (r_in, r_ao, r_co, r_o, r_cw) = _dh_dx(
        dproj, w_in_all, x2, dxd, mod3, chip_sums, hops, w_in_parts)

    small = jnp.concatenate([
        tail_vec[0:4],
        jnp.pad(g_rel_bias.reshape(1, N_BUCKETS * N_HEADS), ((0, 0), (0, D - N_BUCKETS * N_HEADS))),
        jnp.zeros((3, D), F32)], axis=0)
    dmod = jnp.concatenate([mod_vec[0:2], mod_vec[2:4], tail_vec[4:6]], axis=1)
    small_g, dmod_g = _all_gather(
        [small, dmod],
        [jax.ShapeDtypeStruct((N_DEV, 8, D), F32), jax.ShapeDtypeStruct((N_DEV, BL, 3 * D), F32)],
        "gather_small")
    dmod_all = dmod_g.reshape(N_DEV * BL, 3 * D)
    small_names = ["b_ada", "conv_b", "rel_bias", "ln_g", "ln_b"]
    small_params = [(b_ada, m_b_ada, v_b_ada), (conv_b, m_conv_b, v_conv_b), (rel_bias, m_rel_bias, v_rel_bias),
                    (ln_g, m_ln_g, v_ln_g), (ln_b, m_ln_b, v_ln_b)]
    loss, small_res = _small_updates(
        small_g, dmod_all, small_g[:, 4, :N_BUCKETS * N_HEADS].reshape(N_DEV, N_BUCKETS, N_HEADS), small_params)
    loss = loss.reshape(())
    g_w_ada = _ada_bwd(jnp.transpose(c_all), lax.dynamic_slice(dmod_all, (0, me * ADA_SHARD),
                                                               (N_DEV * BL, ADA_SHARD)))

    def upd(parts, w, m, v, name, row_tile=None):
        shape = w.shape
        w2, m2, v2 = (t.reshape(parts.shape[1:]) for t in (w, m, v))
        return tuple(t.reshape(shape) for t in _adamw(parts, w2, m2, v2, name, row_tile))

    res = {
        "w_ada": upd(g_w_ada[None], w_ada, m_w_ada, v_w_ada, "adam_w_ada", 256),
        "w_in": upd(r_in, w_in, m_w_in, v_w_in, "adam_w_in", 128),
    }
    mid_names = ["conv_w", "w_attn_out", "w_conv_out", "w_o"]
    mid_parts = [r_cw, r_ao, r_co, r_o]
    mid_full = [(conv_w, m_conv_w, v_conv_w), (w_attn_out, m_w_attn_out, v_w_attn_out),
                (w_conv_out, m_w_conv_out, v_w_conv_out), (w_o, m_w_o, v_w_o)]
    mid_res = _multi_adamw(mid_parts, [tuple(t[0] for t in wmv) for wmv in mid_full], "adam_mid")
    for nm, wmv, outs4 in zip(mid_names, mid_full, mid_res):
        res[nm] = tuple(t[None] for t in outs4)
    res.update(dict(zip(small_names, small_res)))
    order = ["w_ada", "b_ada", "w_in", "conv_w", "conv_b", "rel_bias", "w_attn_out", "w_conv_out",
             "w_o", "ln_g", "ln_b"]
    outs = [loss, grad_x.reshape(BL, S, D)]
    for k in range(4):
        outs += [res[name][k] for name in order]
    return tuple(outs)
```

```python
import math

import numpy as np
import jax
import jax.numpy as jnp
from jax import lax
from jax.experimental import pallas as pl
from jax.experimental.pallas import tpu as pltpu

F32 = jnp.float32
BF16 = jnp.bfloat16
MESH = pl.DeviceIdType.MESH

N_DEV = 8
D = 1024
S = 2048
BL = 2
T = BL * S
NCOL = 11264
SHARD = NCOL // N_DEV
CB = 512
NCB = NCOL // CB
HD = 128
GW = 512
QB = 128
DILATIONS = (1, 4, 16)
N_STEPS = 128
N_BUCKETS = 32
N_HEADS = 12
ALPHA = 2.0 ** 0.25
LN_EPS = 1e-5
NEG_INF = -1e30
SCALE = HD ** -0.5
ADA_SHARD = 3 * D // N_DEV

CB_Q, CB_K, CB_V, CB_GA = 0, 3, 6, 9
KB_U, KB_BG, KB_CG, KB_GC, KB_MA, KB_MC = 5, 6, 7, 8, 9, 10

ADAM_LR, ADAM_B1, ADAM_B2, ADAM_EPS, ADAM_WD, ADAM_STEP = 0.001, 0.9, 0.999, 1e-08, 0.01, 10

VMEM_LIMIT = 56 * 1024 * 1024
VMEM_LIMIT_TAIL = 62 * 1024 * 1024


def _dot(a, b):
    return jnp.dot(a, b, preferred_element_type=F32)


def _dot_nt(a, b):
    return lax.dot_general(a, b, (((1,), (1,)), ((), ())), preferred_element_type=F32)


def _dot_tn(a, b):
    return lax.dot_general(a, b, (((0,), (0,)), ((), ())), preferred_element_type=F32)


def _sigmoid(v):
    return 1.0 / (1.0 + jnp.exp(-v))


def _write_columns(pieces, dst_hbm, row0, sems):
    copies = []
    for k, (src, col0) in enumerate(pieces):
        rows, width = src.shape
        copies.append(pltpu.make_async_copy(
            src, dst_hbm.at[pl.ds(row0, rows), pl.ds(col0, width)], sems.at[k]))
    for cp in copies:
        cp.start()
    for cp in copies:
        cp.wait()


def _my_index():
    return 4 * lax.axis_index("x") + 2 * lax.axis_index("y") + lax.axis_index("c")


class _Gather:
    def __init__(self, ins, outs, stage, send_sems, recv_sems, local_sems):
        self.ins, self.outs, self.stage = ins, outs, stage
        self.send_sems, self.recv_sems, self.local_sems = send_sems, recv_sems, local_sems
        x, y, c = lax.axis_index("x"), lax.axis_index("y"), lax.axis_index("c")
        self.c = c
        self.me, self.sibling = (x, y, c), (x, y, 1 - c)
        self.chips = [(1 - x, y), (x, 1 - y), (1 - x, 1 - y)]

    @staticmethod
    def scratch(arrs):
        n = len(arrs)
        return ([pltpu.SemaphoreType.DMA((7 * n,)), pltpu.SemaphoreType.DMA((7 * n,)),
                 pltpu.SemaphoreType.DMA((n,))] + [pltpu.VMEM(a.shape, a.dtype) for a in arrs])

    def _copy(self, a, k, block, to, src=None):
        dst = self.outs[a].at[4 * block[0] + 2 * block[1] + block[2]]
        return pltpu.make_async_remote_copy(
            src_ref=dst if src is None else src, dst_ref=dst,
            send_sem=self.send_sems.at[a * 7 + k], recv_sem=self.recv_sems.at[a * 7 + k],
            device_id=to, device_id_type=MESH)

    def _first(self):
        first = []
        for a in range(len(self.ins)):
            first.append(self._copy(a, 0, self.me, self.sibling, src=self.ins[a]))
            first += [self._copy(a, 1 + j, self.me, (*chip, self.c), src=self.ins[a])
                      for j, chip in enumerate(self.chips)]
        return first

    def _mine(self):
        me = self.me
        return [pltpu.make_async_copy(self.stage[a], self.outs[a].at[4 * me[0] + 2 * me[1] + me[2]],
                                      self.local_sems.at[a]) for a in range(len(self.ins))]

    def begin(self):
        for cp in self._first():
            cp.start()
        loads = [pltpu.make_async_copy(self.ins[a], self.stage[a], self.local_sems.at[a])
                 for a in range(len(self.ins))]
        for cp in loads:
            cp.start()
        for cp in loads:
            cp.wait()
        for cp in self._mine():
            cp.start()

    def finish(self):
        n, c, me, sibling = len(self.ins), self.c, self.me, self.sibling
        passed = []
        for j, chip in enumerate(self.chips):
            for a in range(n):
                self._copy(a, 1 + j, (*chip, c), me).wait_recv()
                fwd = self._copy(a, 4 + j, (*chip, c), sibling)
                fwd.start()
                passed.append(fwd)
        for a in range(n):
            self._copy(a, 0, sibling, me).wait_recv()
        for j, chip in enumerate(self.chips):
            for a in range(n):
                self._copy(a, 4 + j, (*chip, 1 - c), me).wait_recv()
        for cp in self._first() + passed:
            cp.wait_send()
        for cp in self._mine():
            cp.wait()


def _all_gather(arrs, out_shapes, name):
    n = len(arrs)

    def body(*refs):
        g = _Gather(refs[:n], refs[n:2 * n], refs[2 * n + 3:], *refs[2 * n:2 * n + 3])
        g.begin()
        g.finish()

    any_spec = pl.BlockSpec(memory_space=pl.ANY)
    return pl.pallas_call(
        body, name=name,
        out_shape=tuple(out_shapes),
        in_specs=[any_spec] * n,
        out_specs=tuple([any_spec] * n),
        scratch_shapes=_Gather.scratch(arrs),
    )(*arrs)


def _slice_order():
    x, y, c = lax.axis_index("x"), lax.axis_index("y"), lax.axis_index("c")
    slots = []
    for q in (2 * (1 - x) + y, 2 * x + (1 - y), 2 * (1 - x) + (1 - y), 2 * x + y):
        slots += [2 * q + 1 - c, 2 * q + c]
    return jnp.stack(slots).astype(jnp.int32)


def _gw_in_pair(order, h, dproj, smalls, small_shapes4):
    kk, m = h.shape
    tk = min(kk, 2048)
    nk = kk // tk
    ncols = dproj.shape[1] // N_DEV
    n = len(smalls)

    def body(order_ref, h_ref, d_ref, *rest):
        ins = rest[:n]
        sums_hbm, parts_hbm = rest[n], rest[n + 1]
        sib = rest[n + 2:2 * n + 2]
        (acc, sendbuf, recvbuf, sumbuf, send_sems, recv_sems, local_sem, ssend, srecv,
         isend, irecv) = rest[2 * n + 2:]
        js, k = pl.program_id(0), pl.program_id(1)
        x, y, c = lax.axis_index("x"), lax.axis_index("y"), lax.axis_index("c")
        sibling = (x, y, 1 - c)
        my_chip = 2 * x + y
        near = [(1 - x, y, c), (x, 1 - y, c)]

        def ici_copy(p, out_chip):
            peer = near[p]
            return pltpu.make_async_remote_copy(
                src_ref=sumbuf.at[p], dst_ref=parts_hbm.at[out_chip],
                send_sem=isend.at[p], recv_sem=irecv.at[p], device_id=peer, device_id_type=MESH)

        def small_copies():
            return [pltpu.make_async_remote_copy(
                        src_ref=ins[a].at[2 * q + 1 - c], dst_ref=sib[a].at[q],
                        send_sem=ssend.at[a * 4 + q], recv_sem=srecv.at[a * 4 + q],
                        device_id=sibling, device_id_type=MESH)
                    for a in range(n) for q in range(4)]

        def slice_copy(p):
            return pltpu.make_async_remote_copy(
                src_ref=sendbuf, dst_ref=recvbuf.at[p], send_sem=send_sems.at[p], recv_sem=recv_sems.at[p],
                device_id=sibling, device_id_type=MESH)

        def sum_copy(p):
            return pltpu.make_async_copy(sumbuf.at[2], sums_hbm.at[order_ref[2 * p] // 2], local_sem)

        @pl.when((js == 0) & (k == 0))
        def _():
            for cp in small_copies():
                cp.start()

        @pl.when(k == 0)
        def _():
            acc[...] = jnp.zeros_like(acc)

        acc[...] += _dot_tn(h_ref[...], d_ref[...])

        for p in range(4):
            @pl.when((js == 2 * p) & (k == nk - 1))
            def _():
                if p > 0:
                    slice_copy(p - 1).wait_send()
                sendbuf[...] = acc[...].astype(BF16)
                slice_copy(p).start()

            @pl.when((js == 2 * p + 1) & (k == nk - 1))
            def _():
                slice_copy(p).wait_recv()
                if p == 3:
                    sum_copy(2).wait()
                sumbuf[min(p, 2)] = (acc[...] + recvbuf[p].astype(F32)).astype(BF16)
                if p < 2:
                    ici_copy(p, my_chip).start()
                else:
                    sum_copy(p).start()

        @pl.when((js == N_DEV - 1) & (k == nk - 1))
        def _():
            slice_copy(3).wait_send()
            sum_copy(3).wait()
            for cp in small_copies():
                cp.wait()
            for p in range(2):
                ici_copy(p, 2 * near[p][0] + near[p][1]).wait_recv()
                ici_copy(p, my_chip).wait_send()

    any_spec = pl.BlockSpec(memory_space=pl.ANY)
    res = pl.pallas_call(
        body, name="gw_in_pair",
        grid_spec=pltpu.PrefetchScalarGridSpec(
            num_scalar_prefetch=1,
            grid=(N_DEV, nk),
            in_specs=[pl.BlockSpec((tk, m), lambda js, k, order_ref: (k, 0)),
                      pl.BlockSpec((tk, ncols), lambda js, k, order_ref: (k, order_ref[js]))] + [any_spec] * n,
            out_specs=(any_spec,) * (n + 2),
            scratch_shapes=[pltpu.VMEM((m, ncols), F32), pltpu.VMEM((m, ncols), BF16),
                            pltpu.VMEM((4, m, ncols), BF16), pltpu.VMEM((3, m, ncols), BF16),
                            pltpu.SemaphoreType.DMA((4,)), pltpu.SemaphoreType.DMA((4,)),
                            pltpu.SemaphoreType.DMA,
                            pltpu.SemaphoreType.DMA((4 * n,)), pltpu.SemaphoreType.DMA((4 * n,)),
                            pltpu.SemaphoreType.DMA((2,)), pltpu.SemaphoreType.DMA((2,))]),
        out_shape=(jax.ShapeDtypeStruct((4, m, ncols), BF16),) * 2 + tuple(small_shapes4),
        compiler_params=pltpu.CompilerParams(vmem_limit_bytes=VMEM_LIMIT),
    )(order, h, dproj, *smalls)
    return res[0], res[1], res[2:]


def _chip_copies(ins, outs, send_sems, recv_sems, local_sems, hops):
    n = len(ins)
    x, y, c = lax.axis_index("x"), lax.axis_index("y"), lax.axis_index("c")
    my_chip = 2 * x + y

    def peer_of(k):
        return ((1 - x) if (k >> 1) & 1 else x, (1 - y) if k & 1 else y, c)

    def copy(a, k, out_chip):
        peer = peer_of(k)
        return pltpu.make_async_remote_copy(
            src_ref=ins[a].at[2 * peer[0] + peer[1]], dst_ref=outs[a].at[out_chip],
            send_sem=send_sems.at[a * 3 + k - 1], recv_sem=recv_sems.at[a * 3 + k - 1],
            device_id=peer, device_id_type=MESH)

    sends = [copy(a, k, my_chip) for k in range(1, 4) for a in range(n) if k in hops[a]]
    arrivals = []
    for k in range(1, 4):
        peer = peer_of(k)
        arrivals += [copy(a, k, 2 * peer[0] + peer[1]) for a in range(n) if k in hops[a]]
    mine = [pltpu.make_async_copy(ins[a].at[my_chip], outs[a].at[my_chip], local_sems.at[a])
            for a in range(n)]
    return sends, arrivals, mine


def _pair_add(core, mines, theirs):
    n = len(mines)

    def body(core_ref, *refs):
        mine, sib, outs = refs[:n], refs[n:2 * n], refs[2 * n:]
        for a in range(n):
            for q in range(4):
                outs[a][q] = (mine[a][2 * q + core_ref[0]].astype(F32)
                              + sib[a][q].astype(F32)).astype(outs[a].dtype)

    return pl.pallas_call(
        body, name="pair_add",
        in_specs=[pl.BlockSpec(memory_space=pltpu.SMEM)] + [pl.BlockSpec(memory_space=pltpu.VMEM)] * (2 * n),
        out_shape=tuple(jax.ShapeDtypeStruct(t.shape, t.dtype) for t in theirs),
    )(core, *mines, *theirs)


def _mod_exchange(c8, w_ada, b_cols):
    cols = w_ada.shape[1]

    def body(c_ref, w_ref, b_ref, call_ref, mod_ref, msend, send1, recv1, send2, recv2):
        x, y, c = lax.axis_index("x"), lax.axis_index("y"), lax.axis_index("c")
        my_slot = 4 * x + 2 * y + c

        def peer_of(k):
            return ((1 - x) if (k >> 2) & 1 else x, (1 - y) if (k >> 1) & 1 else y, (1 - c) if k & 1 else c)

        def slot_of(dev):
            return 4 * dev[0] + 2 * dev[1] + dev[2]

        def exchange(src_of, dst_ref, send_sems, recv_sems):
            sends, arrivals = [], []
            for k in range(1, 8):
                peer = peer_of(k)
                sends.append(pltpu.make_async_remote_copy(
                    src_ref=src_of(slot_of(peer)), dst_ref=dst_ref.at[my_slot],
                    send_sem=send_sems.at[k - 1], recv_sem=recv_sems.at[k - 1],
                    device_id=peer, device_id_type=MESH))
                arrivals.append(pltpu.make_async_remote_copy(
                    src_ref=src_of(my_slot), dst_ref=dst_ref.at[slot_of(peer)],
                    send_sem=send_sems.at[k - 1], recv_sem=recv_sems.at[k - 1],
                    device_id=peer, device_id_type=MESH))
            for cp in sends:
                cp.start()
            for cp in arrivals:
                cp.wait_recv()
            for cp in sends:
                cp.wait_send()

        call_ref[my_slot] = c_ref[...]
        exchange(lambda s: c_ref, call_ref, send1, recv1)
        cv = call_ref[...].reshape(N_DEV * 8, c_ref.shape[1])
        act = cv * _sigmoid(cv)
        mod = jnp.dot(act, w_ref[...], preferred_element_type=F32,
                      precision=lax.Precision.HIGHEST) + b_ref[...]
        msend[...] = mod.reshape(N_DEV, 8, cols)
        mod_ref[my_slot] = msend[my_slot]
        exchange(lambda s: msend.at[s], mod_ref, send2, recv2)

    return pl.pallas_call(
        body, name="mod_exchange",
        out_shape=(jax.ShapeDtypeStruct((N_DEV, 8, c8.shape[1]), F32),
                   jax.ShapeDtypeStruct((N_DEV, 8, cols), F32)),
        scratch_shapes=[pltpu.VMEM((N_DEV, 8, cols), F32)] + [pltpu.SemaphoreType.DMA((7,))] * 4,
    )(c8, w_ada, b_cols)


def _ada_bwd(c_all_t, dmod_cols):
    def body(c_ref, d_ref, o_ref):
        cv = c_ref[...]
        sc = cv * _sigmoid(cv)
        o_ref[...] = jnp.dot(sc, d_ref[...], preferred_element_type=F32,
                             precision=lax.Precision.HIGHEST)

    return pl.pallas_call(
        body, name="ada_bwd",
        out_shape=jax.ShapeDtypeStruct((c_all_t.shape[0], dmod_cols.shape[1]), F32),
    )(c_all_t, dmod_cols)


def _shard_order():
    x, y, c = lax.axis_index("x"), lax.axis_index("y"), lax.axis_index("c")
    devs = [(x, y, c), (x, y, 1 - c)]
    for chip in [(1 - x, y), (x, 1 - y), (1 - x, 1 - y)]:
        devs += [(*chip, c), (*chip, 1 - c)]
    return jnp.stack([4 * d[0] + 2 * d[1] + d[2] for d in devs]).astype(jnp.int32)


def _prep_h(x2, mod3):
    ts = 512
    per_seq = S // ts

    def body(x_ref, mod_ref, h_ref):
        shift = mod_ref[0, 0:1, :]
        scale = mod_ref[0, 1:2, :]
        h_ref[...] = (x_ref[...] * (1.0 + scale) + shift).astype(BF16)

    return pl.pallas_call(
        body, name="prep_h",
        grid=(T // ts,),
        in_specs=[pl.BlockSpec((ts, D), lambda i: (i, 0)),
                  pl.BlockSpec((1, 3, D), lambda i: (i // per_seq, 0, 0))],
        out_specs=pl.BlockSpec((ts, D), lambda i: (i, 0)),
        out_shape=jax.ShapeDtypeStruct((T, D), BF16),
    )(x2, mod3)


def _gather_proj(order, h, w_shard, tm, ride=()):
    rows, kdim = h.shape
    ncols = w_shard.shape[1]
    n_i = rows // tm
    ride_arrs, ride_shapes = ride if ride else ((), ())
    n_ride = len(ride_arrs)

    def body(order_ref, h_ref, mine_hbm, *rest):
        ride_ins = rest[:n_ride]
        o_ref, all_hbm = rest[n_ride:n_ride + 2]
        ride_outs = rest[n_ride + 2:2 * n_ride + 2]
        wv, send_sems, recv_sems, local_sems = rest[2 * n_ride + 2:2 * n_ride + 6]
        ride_scr = rest[2 * n_ride + 6:]
        j, i = pl.program_id(0), pl.program_id(1)
        x, y, c = lax.axis_index("x"), lax.axis_index("y"), lax.axis_index("c")
        me, sibling = (x, y, c), (x, y, 1 - c)
        chips = [(1 - x, y), (x, 1 - y), (1 - x, 1 - y)]

        def slot(dev):
            return 4 * dev[0] + 2 * dev[1] + dev[2]

        def copy(k, block, to):
            return pltpu.make_async_remote_copy(
                src_ref=wv.at[slot(block)], dst_ref=wv.at[slot(block)],
                send_sem=send_sems.at[k], recv_sem=recv_sems.at[k],
                device_id=to, device_id_type=MESH)

        def keep(step, block):
            return pltpu.make_async_copy(wv.at[slot(block)], all_hbm.at[slot(block)], local_sems.at[step])

        if n_ride:
            gather = _Gather(ride_ins, ride_outs, ride_scr[3:], *ride_scr[:3])
        first = [copy(0, me, sibling)] + [copy(1 + q, me, (*chip, c)) for q, chip in enumerate(chips)]
        passed = [copy(4 + q, (*chip, c), sibling) for q, chip in enumerate(chips)]
        due = [(me, None, None), (sibling, copy(0, sibling, me), None)]
        for q, chip in enumerate(chips):
            due.append(((*chip, c), copy(1 + q, (*chip, c), me), passed[q]))
            due.append(((*chip, 1 - c), copy(4 + q, (*chip, 1 - c), me), None))

        @pl.when((j == 0) & (i == 0))
        def _():
            load = pltpu.make_async_copy(mine_hbm, wv.at[slot(me)], local_sems.at[N_DEV])
            load.start()
            load.wait()
            for cp in first:
                cp.start()
            keep(0, me).start()

        for step in range(1, N_DEV):
            block, arrival, forward = due[step]

            @pl.when((j == step) & (i == 0))
            def _():
                arrival.wait_recv()
                if forward is not None:
                    forward.start()
                keep(step, block).start()
                if n_ride and step == N_DEV - 2:
                    gather.begin()

        o_ref[...] = _dot(h_ref[...], wv[order_ref[j]]).astype(BF16)

        @pl.when((j == N_DEV - 1) & (i == n_i - 1))
        def _():
            for cp in first + passed:
                cp.wait_send()
            for step in range(N_DEV):
                keep(step, due[step][0]).wait()
            if n_ride:
                gather.finish()

    any_spec = pl.BlockSpec(memory_space=pl.ANY)
    res = pl.pallas_call(
        body, name="gather_proj",
        grid_spec=pltpu.PrefetchScalarGridSpec(
            num_scalar_prefetch=1,
            grid=(N_DEV, n_i),
            in_specs=[pl.BlockSpec((tm, kdim), lambda j, i, order_ref: (i, 0)), any_spec] + [any_spec] * n_ride,
            out_specs=(pl.BlockSpec((tm, ncols), lambda j, i, order_ref: (i, order_ref[j])), any_spec)
                      + (any_spec,) * n_ride,
            scratch_shapes=[pltpu.VMEM((N_DEV, kdim, ncols), BF16),
                            pltpu.SemaphoreType.DMA((7,)), pltpu.SemaphoreType.DMA((7,)),
                            pltpu.SemaphoreType.DMA((N_DEV + 1,))]
                           + (_Gather.scratch(ride_arrs) if n_ride else [])),
        out_shape=(jax.ShapeDtypeStruct((rows, N_DEV * ncols), BF16),
                   jax.ShapeDtypeStruct((N_DEV, kdim, ncols), BF16)) + tuple(ride_shapes),
        compiler_params=pltpu.CompilerParams(vmem_limit_bytes=VMEM_LIMIT),
    )(order, h, w_shard, *ride_arrs)
    return res[0], res[1], res[2:]


def _bucket_maps():
    a = np.arange(QB)[:, None]
    b = np.arange(2 * QB)[None, :]
    steps = a + QB - b
    maps = []
    for dil in DILATIONS:
        dist = np.maximum(steps, 0) * dil
        nf = np.maximum(dist, 1).astype(np.float32)
        large = 16 + (np.log(nf / np.float32(16)) / np.float32(math.log(128.0))
                      * np.float32(16)).astype(np.int32)
        large = np.minimum(large, N_BUCKETS - 1)
        maps.append(np.where(dist < 16, dist, large).astype(np.int32))
    band = (steps >= 0) & (steps <= N_STEPS)
    first = band & (b >= QB)
    masks = np.stack([first, band]).astype(np.int32)
    return np.stack(maps), masks


def _bias_expand(rel_bias, buckets, masks):
    def body(tab_ref, bk_ref, mk_ref, o_ref):
        for g in range(3):
            bk = bk_ref[g]
            for h in range(4):
                col = 4 * g + h
                val = jnp.zeros((QB, 2 * QB), F32)
                for k in range(N_BUCKETS):
                    val = jnp.where(bk == k, tab_ref[k, col], val)
                o_ref[g, 0, h] = jnp.where(mk_ref[0] != 0, val, NEG_INF)
                o_ref[g, 1, h] = jnp.where(mk_ref[1] != 0, val, NEG_INF)

    return pl.pallas_call(
        body, name="bias_expand",
        in_specs=[pl.BlockSpec(memory_space=pltpu.SMEM),
                  pl.BlockSpec(memory_space=pltpu.VMEM),
                  pl.BlockSpec(memory_space=pltpu.VMEM)],
        out_shape=jax.ShapeDtypeStruct((3, 2, 4, QB, 2 * QB), F32),
    )(rel_bias, buckets, masks)


def _bias_grad(ds1, ds2, ds3, buckets):
    def body(d1_ref, d2_ref, d3_ref, bk_ref, o_ref):
        for g, d_ref in enumerate((d1_ref, d2_ref, d3_ref)):
            bk = bk_ref[g]
            for h in range(4):
                dv = d_ref[h]
                for k in range(N_BUCKETS):
                    o_ref[k, 4 * g + h] = jnp.sum(jnp.where(bk == k, dv, 0.0))

    return pl.pallas_call(
        body, name="bias_grad",
        in_specs=[pl.BlockSpec(memory_space=pltpu.VMEM)] * 4,
        out_specs=pl.BlockSpec(memory_space=pltpu.SMEM),
        out_shape=jax.ShapeDtypeStruct((N_BUCKETS, N_HEADS), F32),
    )(ds1, ds2, ds3, buckets)


def _scratch_sets(rows):
    return 4 if rows <= 512 else 1


def _unit_chunks(dil, size=16):
    units = [(h, r) for h in range(4) for r in range(dil)]
    return [units[i:i + size] for i in range(0, len(units), size)]


def _residue_rows(src_ref, copies, h, residue):
    buf = copies[h % len(copies)]
    buf[...] = src_ref[:, h * HD:(h + 1) * HD].astype(F32)
    return lambda r: buf[residue(r), :].astype(BF16)


def _attn_fwd(proj, bias, g):
    dil = DILATIONS[g]
    rows = QB * dil
    nsb = S // rows
    has_prev = nsb > 1

    def residue(r):
        return pl.ds(r, QB, stride=dil)

    n_sets = _scratch_sets(rows)
    n_in = 6 if has_prev else 4
    n_copied = (4 + (2 if has_prev else 0)) * n_sets

    def body(*refs):
        q_ref, kc_ref, vc_ref = refs[:3]
        kp_ref, vp_ref = refs[3:5] if has_prev else (None, None)
        b_ref = refs[n_in - 1]
        o_ref, l_ref = refs[n_in:n_in + 2]
        scr = list(refs[n_in + 2:])
        ls = [scr.pop(0) for _ in range(4)]
        copies = {name: [scr.pop(0) for _ in range(n_sets)]
                  for name in ("q", "kc", "vc", "o") + (("kp", "vp") if has_prev else ())}
        lane = lax.broadcasted_iota(jnp.int32, (QB, 128), 1)
        refs_of = {"q": q_ref, "kc": kc_ref, "vc": vc_ref, "kp": kp_ref, "vp": vp_ref}
        for chunk in _unit_chunks(dil):
            rows_of = {h: {name: _residue_rows(refs_of[name], copies[name], h, residue)
                           for name in refs_of if refs_of[name] is not None}
                       for h in sorted({h for h, _ in chunk})}

            def batch(name):
                return jnp.stack([rows_of[h][name](r) for h, r in chunk])

            q, k, v = batch("q"), batch("kc"), batch("vc")
            if has_prev:
                k = jnp.concatenate([batch("kp"), k], axis=1)
                v = jnp.concatenate([batch("vp"), v], axis=1)
                bias_b = jnp.stack([b_ref[h] for h, _ in chunk])
            else:
                bias_b = jnp.stack([b_ref[h, :, QB:] for h, _ in chunk])
            s = jnp.einsum("uqd,ukd->uqk", q, k, preferred_element_type=F32) * SCALE + bias_b
            m = jnp.max(s, axis=-1, keepdims=True)
            p = jnp.exp(s - m)
            l = jnp.sum(p, axis=-1, keepdims=True)
            o = jnp.einsum("uqk,ukd->uqd", p.astype(BF16), v, preferred_element_type=F32) / l
            lse = m + jnp.log(l)
            for i, (h, r) in enumerate(chunk):
                copies["o"][h % n_sets][residue(r), :] = o[i]
                ls[h][r * QB:(r + 1) * QB, :] = jnp.where(lane == h, lse[i], 0.0)
            for h in sorted({h for h, _ in chunk}):
                o_ref[:, h * HD:(h + 1) * HD] = copies["o"][h % n_sets][...]
        for r in range(dil):
            blk = slice(r * QB, (r + 1) * QB)
            l_ref[residue(r), :] = (ls[0][blk, :] + ls[1][blk, :]) + (ls[2][blk, :] + ls[3][blk, :])

    def row(b, n):
        return b * nsb + n

    def prev(b, n):
        return b * nsb + jnp.maximum(n - 1, 0)

    in_specs = [
        pl.BlockSpec((rows, GW), lambda b, n: (row(b, n), CB_Q + g)),
        pl.BlockSpec((rows, GW), lambda b, n: (row(b, n), CB_K + g)),
        pl.BlockSpec((rows, GW), lambda b, n: (row(b, n), CB_V + g)),
    ]
    args = [proj, proj, proj]
    scratch = [pltpu.VMEM((rows, 128), F32)] * (4 + n_copied)
    if has_prev:
        in_specs += [pl.BlockSpec((rows, GW), lambda b, n: (prev(b, n), CB_K + g)),
                     pl.BlockSpec((rows, GW), lambda b, n: (prev(b, n), CB_V + g))]
        args += [proj, proj]
    in_specs.append(pl.BlockSpec((None, None, 4, QB, 2 * QB),
                                 lambda b, n: (g, jnp.minimum(n, 1), 0, 0, 0)))
    args.append(bias)
    return pl.pallas_call(
        body, name=f"attn_fwd{g}",
        grid=(BL, nsb),
        in_specs=in_specs,
        out_specs=(pl.BlockSpec((rows, GW), lambda b, n: (row(b, n), 0)),
                   pl.BlockSpec((rows, 128), lambda b, n: (row(b, n), 0))),
        out_shape=(jax.ShapeDtypeStruct((T, GW), F32), jax.ShapeDtypeStruct((T, 128), F32)),
        scratch_shapes=scratch,
        compiler_params=pltpu.CompilerParams(vmem_limit_bytes=VMEM_LIMIT),
    )(*args)


def _attn_bwd(proj, d_out, stats, bias, dproj, g):
    dil = DILATIONS[g]
    rows = QB * dil
    nsb = S // rows
    has_prev = nsb > 1
    n_steps = nsb + 1 if has_prev else 1
    n_in = 7 + (2 if has_prev else 0)

    def residue(r):
        return pl.ds(r, QB, stride=dil)

    n_sets = _scratch_sets(rows)

    def body(*refs):
        q_ref, kc_ref, vc_ref, do_ref, st_ref, b_ref = refs[:6]
        kp_ref, vp_ref = refs[6:8] if has_prev else (None, None)
        out_ref, db_ref = refs[n_in], refs[n_in + 1]
        scr = list(refs[n_in + 2:])
        sq, sk, sv, sems = [scr.pop(0) for _ in range(4)]
        carry = scr.pop(0) if has_prev else None
        sts = scr.pop(0)
        copies = {name: [scr.pop(0) for _ in range(n_sets)]
                  for name in ("q", "kc", "vc", "do", "dq", "dk", "dv") + (("kp", "vp") if has_prev else ())}
        b, n = pl.program_id(0), pl.program_id(1)

        @pl.when((b == 0) & (n == 0))
        def _():
            db_ref[...] = jnp.zeros_like(db_ref)

        def finish(h, r, dq, dk, dv):
            for name, val in (("dq", dq), ("dk", dk), ("dv", dv)):
                copies[name][h % n_sets][residue(r), :] = val

        def finish_head(h):
            sl = slice(h * HD, (h + 1) * HD)
            sq[:, sl] = copies["dq"][h % n_sets][...].astype(BF16)
            sk[:, sl] = copies["dk"][h % n_sets][...].astype(BF16)
            sv[:, sl] = copies["dv"][h % n_sets][...].astype(BF16)

        def write_block(blk_idx):
            row0 = pl.multiple_of(blk_idx * rows, rows)
            _write_columns([(sq, CB * (CB_Q + g)), (sk, CB * (CB_K + g)), (sv, CB * (CB_V + g))],
                           out_ref, row0, sems)

        def carried(h, r):
            blk = slice(r * QB, (r + 1) * QB)
            return ((blk, slice(h * HD, (h + 1) * HD)), (blk, slice(GW + h * HD, GW + (h + 1) * HD)),
                    (blk, slice(2 * GW + h * HD, 2 * GW + (h + 1) * HD)))

        if has_prev:
            @pl.when(n == 0)
            def _():
                carry[...] = jnp.zeros_like(carry)

            @pl.when(n == nsb)
            def _():
                for h in range(4):
                    for r in range(dil):
                        cq, ck, cv = carried(h, r)
                        finish(h, r, carry[cq], carry[ck], carry[cv])
                    finish_head(h)
                write_block(b * nsb + nsb - 1)

        @pl.when(n < nsb)
        def _():
            for r in range(dil):
                sts[r * QB:(r + 1) * QB, :] = st_ref[residue(r), :]
            refs_of = {"q": q_ref, "kc": kc_ref, "vc": vc_ref, "do": do_ref, "kp": kp_ref, "vp": vp_ref}
            for chunk in _unit_chunks(dil):
                heads = sorted({h for h, _ in chunk})
                rows_of = {h: {name: _residue_rows(refs_of[name], copies[name], h, residue)
                               for name in refs_of if refs_of[name] is not None}
                           for h in heads}

                def batch(name):
                    return jnp.stack([rows_of[h][name](r) for h, r in chunk])

                q, k, v, do = batch("q"), batch("kc"), batch("vc"), batch("do")
                if has_prev:
                    k = jnp.concatenate([batch("kp"), k], axis=1)
                    v = jnp.concatenate([batch("vp"), v], axis=1)
                    bias_b = jnp.stack([b_ref[h] for h, _ in chunk])
                else:
                    bias_b = jnp.stack([b_ref[h, :, QB:] for h, _ in chunk])
                lse = jnp.stack([sts[r * QB:(r + 1) * QB, h:h + 1] for h, r in chunk])
                delta = jnp.stack([sts[r * QB:(r + 1) * QB, 4 + h:5 + h] for h, r in chunk])
                s = jnp.einsum("uqd,ukd->uqk", q, k, preferred_element_type=F32) * SCALE + bias_b
                p = jnp.exp(s - lse)
                ds = p * (jnp.einsum("uqd,ukd->uqk", do, v, preferred_element_type=F32) - delta)
                for h in heads:
                    mine = [ds[i] for i, (hh, _) in enumerate(chunk) if hh == h]
                    tot = mine[0]
                    for extra in mine[1:]:
                        tot = tot + extra
                    if has_prev:
                        db_ref[h] += tot
                    else:
                        db_ref[h, :, QB:] += tot
                dsb, pb = ds.astype(BF16), p.astype(BF16)
                dq = jnp.einsum("uqk,ukd->uqd", dsb, k, preferred_element_type=F32) * SCALE
                dk = jnp.einsum("uqk,uqd->ukd", dsb, q, preferred_element_type=F32) * SCALE
                dv = jnp.einsum("uqk,uqd->ukd", pb, do, preferred_element_type=F32)
                for i, (h, r) in enumerate(chunk):
                    if has_prev:
                        cq, ck, cv = carried(h, r)
                        finish(h, r, carry[cq], carry[ck] + dk[i, :QB], carry[cv] + dv[i, :QB])
                        carry[cq] = dq[i]
                        carry[ck] = dk[i, QB:]
                        carry[cv] = dv[i, QB:]
                    else:
                        finish(h, r, dq[i], dk[i], dv[i])
                for h in heads:
                    finish_head(h)
            if has_prev:
                @pl.when(n > 0)
                def _():
                    write_block(b * nsb + n - 1)
            else:
                write_block(b)

    def row(b, n):
        return b * nsb + jnp.minimum(n, nsb - 1)

    def prev(b, n):
        return b * nsb + jnp.maximum(jnp.minimum(n, nsb - 1) - 1, 0)

    in_specs = [
        pl.BlockSpec((rows, GW), lambda b, n: (row(b, n), CB_Q + g)),
        pl.BlockSpec((rows, GW), lambda b, n: (row(b, n), CB_K + g)),
        pl.BlockSpec((rows, GW), lambda b, n: (row(b, n), CB_V + g)),
        pl.BlockSpec((rows, GW), lambda b, n: (row(b, n), 0)),
        pl.BlockSpec((rows, 128), lambda b, n: (row(b, n), 0)),
        pl.BlockSpec((None, None, 4, QB, 2 * QB),
                     lambda b, n: (g, jnp.minimum(jnp.minimum(n, nsb - 1), 1), 0, 0, 0)),
    ]
    args = [proj, proj, proj, d_out, stats, bias]
    scratch = [pltpu.VMEM((rows, GW), BF16)] * 3 + [pltpu.SemaphoreType.DMA((3,))]
    if has_prev:
        in_specs += [pl.BlockSpec((rows, GW), lambda b, n: (prev(b, n), CB_K + g)),
                     pl.BlockSpec((rows, GW), lambda b, n: (prev(b, n), CB_V + g))]
        args += [proj, proj]
        scratch.append(pltpu.VMEM((rows, 3 * GW), F32))
    n_copied = (7 + (2 if has_prev else 0)) * n_sets
    scratch += [pltpu.VMEM((rows, 128), F32)] * (1 + n_copied)
    in_specs.append(pl.BlockSpec(memory_space=pl.ANY))
    args.append(dproj)
    return pl.pallas_call(
        body, name=f"attn_bwd{g}",
        grid=(BL, n_steps),
        in_specs=in_specs,
        out_specs=(pl.BlockSpec(memory_space=pl.ANY),
                   pl.BlockSpec((4, QB, 2 * QB), lambda b, n: (0, 0, 0))),
        out_shape=(jax.ShapeDtypeStruct((T, NCOL), BF16),
                   jax.ShapeDtypeStruct((4, QB, 2 * QB), F32)),
        scratch_shapes=scratch,
        input_output_aliases={len(args) - 1: 0},
        compiler_params=pltpu.CompilerParams(vmem_limit_bytes=VMEM_LIMIT),
    )(*args)


def _tail(x2, tgt2, mod3, o_g, lse_g, proj, w_ao, w_co, w_o, conv_w, conv_b, ln_g, ln_b):
    tm = 256
    per_seq = S // tm
    halo = 16

    def body(x_ref, t_ref, mod_ref, o1_ref, o2_ref, o3_ref, l1_ref, l2_ref, l3_ref,
             ga_ref, u_ref, bg_ref, cg_ref, gc_ref, ma_ref, mc_ref, up_ref, cp_ref,
             wao_ref, wco_ref, wo_ref, cw_ref, cb_ref, lg_ref, lb_ref,
             dproj_ref, dyc_ref, do_ref, st_ref, dxd_ref,
             gwo_ref, gwco_ref, gwao_ref, vec_ref,
             dga_s, dbg_s, dgm_s, sems, acc_o, acc_co, acc_ao):
        i = pl.program_id(0)
        bidx = i // per_seq
        first = (i % per_seq) == 0

        @pl.when(i == 0)
        def _():
            vec_ref[...] = jnp.zeros_like(vec_ref)

        l1, l2, l3 = l1_ref[...], l2_ref[...], l3_ref[...]
        mx = jnp.maximum(jnp.maximum(l1, l2), l3)
        e1, e2, e3 = jnp.exp(l1 - mx), jnp.exp(l2 - mx), jnp.exp(l3 - mx)
        esum = e1 + e2 + e3
        lse_tot = mx + jnp.log(esum)
        w1, w2, w3 = e1 / esum, e2 / esum, e3 / esum

        def per_head(wv):
            return jnp.concatenate([jnp.broadcast_to(wv[:, h:h + 1], (tm, HD)) for h in range(4)], axis=1)

        o = per_head(w1) * o1_ref[...] + per_head(w2) * o2_ref[...] + per_head(w3) * o3_ref[...]

        ga = ga_ref[...].astype(F32)
        sig_ga = _sigmoid(ga)
        silu_ga = ga * sig_ga
        a_in = (o * silu_ga).astype(BF16)
        a_out = _dot(a_in, wao_ref[...])

        u = u_ref[...].astype(F32)
        cg = cg_ref[...].astype(F32)
        z = cg * u
        zp = cp_ref[...].astype(F32) * up_ref[...].astype(F32)
        zp = jnp.where(first, 0.0, zp)
        zcat = jnp.concatenate([zp, z], axis=0)
        z1 = pltpu.roll(zcat, 1, 0)[halo:]
        z2 = pltpu.roll(zcat, 2, 0)[halo:]
        y_conv = cw_ref[0:1, :] * z2 + cw_ref[1:2, :] * z1 + cw_ref[2:3, :] * z + cb_ref[...]
        gc = gc_ref[...].astype(F32)
        sig_gc = _sigmoid(gc)
        silu_gc = gc * sig_gc
        bg = bg_ref[...].astype(F32)
        bg_yc = bg * y_conv
        s_in = (bg_yc * silu_gc).astype(BF16)
        s_out = _dot(s_in, wco_ref[...])

        sa = _sigmoid(ma_ref[...].astype(F32))
        sc = _sigmoid(mc_ref[...].astype(F32))
        merged = (sa * a_out + sc * s_out).astype(BF16)
        y = _dot(merged, wo_ref[...])
        gate1 = 1.0 + mod_ref[0, 2:3, :]
        xv = x_ref[...]
        resid = ALPHA * xv + gate1 * y
        mu = jnp.mean(resid, axis=1, keepdims=True)
        xc = resid - mu
        var = jnp.mean(xc * xc, axis=1, keepdims=True)
        rstd = lax.rsqrt(var + LN_EPS)
        xhat = xc * rstd
        lg = lg_ref[...]
        err = xhat * lg + lb_ref[...] - t_ref[...]
        vec_ref[3:4, :] += (0.5 / D) * jnp.sum(err * err, axis=0, keepdims=True)

        vec_ref[1:2, :] += (1.0 / D) * jnp.sum(err * xhat, axis=0, keepdims=True)
        vec_ref[2:3, :] += (1.0 / D) * jnp.sum(err, axis=0, keepdims=True)
        dxh = err * (lg * (1.0 / D))
        dres = rstd * (dxh - jnp.mean(dxh, axis=1, keepdims=True)
                       - xhat * jnp.mean(dxh * xhat, axis=1, keepdims=True))
        dxd_ref[...] = ALPHA * dres
        dgate = jnp.sum(dres * y, axis=0, keepdims=True)
        vec_ref[4:5, :] += jnp.where(bidx == 0, dgate, 0.0)
        vec_ref[5:6, :] += jnp.where(bidx == 1, dgate, 0.0)
        dy = (dres * gate1).astype(BF16)

        dmerged = _dot_nt(dy, wo_ref[...])
        da_out_f = dmerged * sa
        ds_out_f = dmerged * sc
        da_out = da_out_f.astype(BF16)
        ds_out = ds_out_f.astype(BF16)
        dgm_s[:, 2 * D:3 * D] = (ds_out_f * s_out * (1.0 - sc)).astype(BF16)
        dgm_s[:, D:2 * D] = (da_out_f * a_out * (1.0 - sa)).astype(BF16)
        da_in = _dot_nt(da_out, wao_ref[...])
        ds_in = _dot_nt(ds_out, wco_ref[...])

        d_o = da_in * silu_ga
        do_ref[...] = d_o.astype(BF16)
        dga_s[...] = (da_in * o * (sig_ga + silu_ga * (1.0 - sig_ga))).astype(BF16)
        lane = lax.broadcasted_iota(jnp.int32, (tm, 128), 1)
        stats = lse_tot
        od = o * d_o
        for h in range(4):
            delta = jnp.sum(od[:, h * HD:(h + 1) * HD], axis=1, keepdims=True)
            stats = jnp.where(lane == 4 + h, delta, stats)
        st_ref[...] = stats

        ds_silu = ds_in * silu_gc
        dbg_s[...] = (ds_silu * y_conv).astype(BF16)
        dyc = ds_silu * bg
        dyc_ref[...] = dyc
        vec_ref[0:1, :] += jnp.sum(dyc, axis=0, keepdims=True)
        dgm_s[:, 0:D] = (ds_in * bg_yc * (sig_gc + silu_gc * (1.0 - sig_gc))).astype(BF16)

        @pl.when(i == 0)
        def _():
            acc_o[...] = jnp.zeros_like(acc_o)
            acc_co[...] = jnp.zeros_like(acc_co)
            acc_ao[...] = jnp.zeros_like(acc_ao)

        acc_o[...] += _dot_tn(merged, dy)
        acc_co[...] += _dot_tn(s_in, ds_out)
        acc_ao[...] += _dot_tn(a_in, da_out)

        @pl.when(i == T // tm - 1)
        def _():
            gwo_ref[...] = acc_o[...].astype(BF16)
            gwco_ref[...] = acc_co[...].astype(BF16)
            gwao_ref[...] = acc_ao[...].astype(BF16)

        _write_columns([(dga_s, CB * CB_GA), (dbg_s, D * KB_BG), (dgm_s, D * KB_GC)],
                       dproj_ref, pl.multiple_of(i * tm, tm), sems)

    def tile(width, cblk=0):
        return pl.BlockSpec((tm, width), lambda i: (i, cblk))

    def whole(shape):
        return pl.BlockSpec(shape, lambda i: tuple(0 for _ in shape))

    def once(shape):
        return pl.BlockSpec(shape, lambda i: tuple(0 for _ in shape), pipeline_mode=pl.Buffered(1))

    prev_rows = lambda i: (jnp.maximum(i * (tm // halo) - 1, 0),)
    in_specs = [
        tile(D), tile(D), pl.BlockSpec((1, 3, D), lambda i: (i // per_seq, 0, 0)),
        tile(GW), tile(GW), tile(GW), tile(128), tile(128), tile(128),
        tile(GW, CB_GA), tile(D, KB_U), tile(D, KB_BG), tile(D, KB_CG), tile(D, KB_GC),
        tile(D, KB_MA), tile(D, KB_MC),
        pl.BlockSpec((halo, D), lambda i: (*prev_rows(i), KB_U)),
        pl.BlockSpec((halo, D), lambda i: (*prev_rows(i), KB_CG)),
        whole((GW, D)), whole((D, D)), whole((D, D)),
        whole((3, D)), whole((1, D)), whole((1, D)), whole((1, D)),
    ]
    out_specs = (
        pl.BlockSpec(memory_space=pl.ANY), tile(D), tile(GW), tile(128), tile(D),
        once((D, D)), once((D, D)), once((GW, D)),
        pl.BlockSpec((8, D), lambda i: (0, 0)),
    )
    out_shape = (
        jax.ShapeDtypeStruct((T, NCOL), BF16),
        jax.ShapeDtypeStruct((T, D), F32),
        jax.ShapeDtypeStruct((T, GW), BF16),
        jax.ShapeDtypeStruct((T, 128), F32),
        jax.ShapeDtypeStruct((T, D), F32),
        jax.ShapeDtypeStruct((D, D), BF16),
        jax.ShapeDtypeStruct((D, D), BF16),
        jax.ShapeDtypeStruct((GW, D), BF16),
        jax.ShapeDtypeStruct((8, D), F32),
    )
    return pl.pallas_call(
        body, name="tail",
        grid=(T // tm,),
        in_specs=in_specs, out_specs=out_specs, out_shape=out_shape,
        scratch_shapes=[pltpu.VMEM((tm, GW), BF16), pltpu.VMEM((tm, D), BF16), pltpu.VMEM((tm, 3 * D), BF16),
                        pltpu.SemaphoreType.DMA((3,)),
                        pltpu.VMEM((D, D), F32), pltpu.VMEM((D, D), F32), pltpu.VMEM((GW, D), F32)],
        compiler_params=pltpu.CompilerParams(vmem_limit_bytes=VMEM_LIMIT_TAIL),
    )(x2, tgt2, mod3, *o_g, *lse_g, proj, proj, proj, proj, proj, proj, proj, proj, proj,
      w_ao, w_co, w_o, conv_w, conv_b, ln_g, ln_b)


def _conv_bwd(dyc, proj, conv_w, dproj):
    tm = 512
    per_seq = S // tm

    def body(d_ref, dn_ref, u_ref, c_ref, cw_ref, _, dproj_ref, g_ref, du_s, dc_s, sems):
        i = pl.program_id(0)
        last = (i % per_seq) == per_seq - 1

        @pl.when(i == 0)
        def _():
            g_ref[...] = jnp.zeros_like(g_ref)

        d = d_ref[...]
        dn = jnp.where(last, 0.0, dn_ref[...])
        dcat = jnp.concatenate([d, dn], axis=0)
        d1 = pltpu.roll(dcat, tm + 8 - 1, 0)[:tm]
        d2 = pltpu.roll(dcat, tm + 8 - 2, 0)[:tm]
        dz = cw_ref[2:3, :] * d + cw_ref[1:2, :] * d1 + cw_ref[0:1, :] * d2
        u = u_ref[...].astype(F32)
        cg = c_ref[...].astype(F32)
        du_s[...] = (dz * cg).astype(BF16)
        dc_s[...] = (dz * u).astype(BF16)
        _write_columns([(du_s, D * KB_U), (dc_s, D * KB_CG)], dproj_ref, pl.multiple_of(i * tm, tm), sems)

        z = cg * u
        g_ref[0:1, :] += jnp.sum(d2 * z, axis=0, keepdims=True)
        g_ref[1:2, :] += jnp.sum(d1 * z, axis=0, keepdims=True)
        g_ref[2:3, :] += jnp.sum(d * z, axis=0, keepdims=True)

    n_tiles = T // tm
    next_rows = lambda i: jnp.minimum((i + 1) * (tm // 8), T // 8 - 1)
    return pl.pallas_call(
        body, name="conv_bwd",
        grid=(n_tiles,),
        in_specs=[pl.BlockSpec((tm, D), lambda i: (i, 0)),
                  pl.BlockSpec((8, D), lambda i: (next_rows(i), 0)),
                  pl.BlockSpec((tm, D), lambda i: (i, KB_U)),
                  pl.BlockSpec((tm, D), lambda i: (i, KB_CG)),
                  pl.BlockSpec((3, D), lambda i: (0, 0)),
                  pl.BlockSpec(memory_space=pl.ANY)],
        out_specs=(pl.BlockSpec(memory_space=pl.ANY),
                   pl.BlockSpec((8, D), lambda i: (0, 0))),
        out_shape=(jax.ShapeDtypeStruct((T, NCOL), BF16),
                   jax.ShapeDtypeStruct((8, D), F32)),
        scratch_shapes=[pltpu.VMEM((tm, D), BF16), pltpu.VMEM((tm, D), BF16), pltpu.SemaphoreType.DMA((2,))],
        input_output_aliases={5: 0},
        compiler_params=pltpu.CompilerParams(vmem_limit_bytes=VMEM_LIMIT),
    )(dyc, dyc, proj, proj, conv_w, dproj)


def _dh_dx(dproj, w_in_all, x2, dxd, mod3, chip_sums, hops=(), parts0=None):
    tm = 512
    spj = 2
    n_jj = N_DEV // spj
    per_seq = S // tm
    n = len(chip_sums)
    n_in = 5 + n + (0 if parts0 is None else 1)

    def body(*refs):
        d_ref, w_ref, x_ref, dxd_ref, mod_ref = refs[:5]
        ins = refs[5:5 + n]
        gx_ref, vec_ref = refs[n_in:n_in + 2]
        outs = refs[n_in + 2:n_in + 2 + n]
        acc, send_sems, recv_sems, local_sems = refs[n_in + 2 + n:]
        i, jj = pl.program_id(0), pl.program_id(1)

        @pl.when((i == 0) & (jj == 0))
        def _():
            vec_ref[...] = jnp.zeros_like(vec_ref)
            if n:
                sends, _, mine = _chip_copies(ins, outs, send_sems, recv_sems, local_sems, hops)
                for cp in sends + mine:
                    cp.start()

        if n:
            @pl.when((i == T // tm - 1) & (jj == n_jj - 1))
            def _():
                sends, arrivals, mine = _chip_copies(ins, outs, send_sems, recv_sems, local_sems, hops)
                for cp in arrivals:
                    cp.wait_recv()
                for cp in sends:
                    cp.wait_send()
                for cp in mine:
                    cp.wait()

        @pl.when(jj == 0)
        def _():
            acc[...] = jnp.zeros_like(acc)

        part = _dot_nt(d_ref[:, 0:SHARD], w_ref[0])
        for s in range(1, spj):
            part = part + _dot_nt(d_ref[:, s * SHARD:(s + 1) * SHARD], w_ref[s])
        acc[...] += part

        @pl.when(jj == n_jj - 1)
        def _():
            dh = acc[...]
            bidx = i // per_seq
            gx_ref[...] = dxd_ref[...] + dh * (1.0 + mod_ref[0, 1:2, :])
            dshift = jnp.sum(dh, axis=0, keepdims=True)
            dscale = jnp.sum(dh * x_ref[...], axis=0, keepdims=True)
            vec_ref[0:1, :] += jnp.where(bidx == 0, dshift, 0.0)
            vec_ref[1:2, :] += jnp.where(bidx == 1, dshift, 0.0)
            vec_ref[2:3, :] += jnp.where(bidx == 0, dscale, 0.0)
            vec_ref[3:4, :] += jnp.where(bidx == 1, dscale, 0.0)

    any_spec = pl.BlockSpec(memory_space=pl.ANY)
    res = pl.pallas_call(
        body, name="dh_dx",
        grid=(T // tm, n_jj),
        in_specs=[
            pl.BlockSpec((tm, spj * SHARD), lambda i, jj: (i, jj)),
            pl.BlockSpec((spj, D, SHARD), lambda i, jj: (jj, 0, 0)),
            pl.BlockSpec((tm, D), lambda i, jj: (i, 0)),
            pl.BlockSpec((tm, D), lambda i, jj: (i, 0)),
            pl.BlockSpec((1, 3, D), lambda i, jj: (i // per_seq, 0, 0))] + [any_spec] * (n_in - 5),
        out_specs=(pl.BlockSpec((tm, D), lambda i, jj: (i, 0)),
                   pl.BlockSpec((8, D), lambda i, jj: (0, 0))) + (any_spec,) * n,
        out_shape=(jax.ShapeDtypeStruct((T, D), F32), jax.ShapeDtypeStruct((8, D), F32))
                  + tuple(jax.ShapeDtypeStruct(a.shape, a.dtype) for a in chip_sums),
        scratch_shapes=[pltpu.VMEM((tm, D), F32), pltpu.SemaphoreType.DMA((max(3 * n, 1),)),
                        pltpu.SemaphoreType.DMA((max(3 * n, 1),)), pltpu.SemaphoreType.DMA((max(n, 1),))],
        input_output_aliases={} if parts0 is None else {5 + n: 2},
        compiler_params=pltpu.CompilerParams(vmem_limit_bytes=VMEM_LIMIT),
    )(dproj, w_in_all, x2, dxd, mod3, *chip_sums, *([] if parts0 is None else [parts0]))
    return res[0], res[1], res[2:]


def _adam_step(g, w, m, v):
    nm = ADAM_B1 * m + (1.0 - ADAM_B1) * g
    nv = ADAM_B2 * v + (1.0 - ADAM_B2) * (g * g)
    m_hat = nm / (1.0 - ADAM_B1 ** ADAM_STEP)
    v_hat = nv / (1.0 - ADAM_B2 ** ADAM_STEP)
    return -ADAM_LR * (m_hat / (jnp.sqrt(v_hat) + ADAM_EPS) + ADAM_WD * w), nm, nv


def _adamw(parts, w, m, v, name, row_tile=None):
    n_parts, rows, cols = parts.shape
    tr = rows if row_tile is None else row_tile

    def body(p_ref, w_ref, m_ref, v_ref, g_ref, d_ref, nm_ref, nv_ref):
        g = p_ref[0].astype(F32)
        for s in range(1, n_parts):
            g = g + p_ref[s].astype(F32)
        g_ref[...] = g
        d_ref[...], nm_ref[...], nv_ref[...] = _adam_step(g, w_ref[...], m_ref[...], v_ref[...])

    blk = pl.BlockSpec((tr, cols), lambda i: (i, 0))
    shp = jax.ShapeDtypeStruct((rows, cols), F32)
    return pl.pallas_call(
        body, name=name,
        grid=(rows // tr,),
        in_specs=[pl.BlockSpec((n_parts, tr, cols), lambda i: (0, i, 0)), blk, blk, blk],
        out_specs=(blk, blk, blk, blk),
        out_shape=(shp, shp, shp, shp),
        compiler_params=pltpu.CompilerParams(vmem_limit_bytes=VMEM_LIMIT),
    )(parts, w, m, v)


def _multi_adamw(parts_list, params, name):
    n = len(params)
    flat = [t for wmv in params for t in wmv]

    def body(*refs):
        parts, ins, outs = refs[:n], refs[n:4 * n], refs[4 * n:]
        for p in range(n):
            g = parts[p][0].astype(F32)
            for s in range(1, parts[p].shape[0]):
                g = g + parts[p][s].astype(F32)
            w_ref, m_ref, v_ref = ins[3 * p:3 * p + 3]
            g_ref, d_ref, nm_ref, nv_ref = outs[4 * p:4 * p + 4]
            g_ref[...] = g
            d_ref[...], nm_ref[...], nv_ref[...] = _adam_step(g, w_ref[...], m_ref[...], v_ref[...])

    out_shape = []
    for w, _, _ in params:
        out_shape += [jax.ShapeDtypeStruct(w.shape, F32)] * 4
    res = pl.pallas_call(body, name=name, out_shape=tuple(out_shape))(*parts_list, *flat)
    return [res[4 * p:4 * p + 4] for p in range(n)]


def _small_updates(small_g, dmod_all, rel_parts, params):
    flat = [t for wmv in params for t in wmv]

    def body(sg_ref, dm_ref, rp_ref, *refs):
        ins, outs = refs[:len(flat)], refs[len(flat):]

        def over_devices(row):
            tot = sg_ref[0, row:row + 1, :]
            for s in range(1, N_DEV):
                tot = tot + sg_ref[s, row:row + 1, :]
            return tot

        g_b_ada = dm_ref[0:1, :]
        for r in range(1, N_DEV * BL):
            g_b_ada = g_b_ada + dm_ref[r:r + 1, :]
        g_rel = rp_ref[0]
        for s in range(1, N_DEV):
            g_rel = g_rel + rp_ref[s]
        grads = [g_b_ada, over_devices(0), g_rel, over_devices(1), over_devices(2)]
        outs[0][...] = jnp.sum(over_devices(3), axis=1, keepdims=True)
        for p, g in enumerate(grads):
            w_ref, m_ref, v_ref = ins[3 * p:3 * p + 3]
            g_ref, d_ref, nm_ref, nv_ref = outs[1 + 4 * p:5 + 4 * p]
            g_ref[...] = g
            d_ref[...], nm_ref[...], nv_ref[...] = _adam_step(g, w_ref[...], m_ref[...], v_ref[...])

    out_shape = [jax.ShapeDtypeStruct((1, 1), F32)]
    for w, _, _ in params:
        out_shape += [jax.ShapeDtypeStruct(w.shape, F32)] * 4
    res = pl.pallas_call(body, name="small_updates", out_shape=tuple(out_shape))(small_g, dmod_all, rel_parts, *flat)
    return res[0], [res[1 + 4 * p:5 + 4 * p] for p in range(len(params))]


def _attn_fwd_dense(proj, bias):
    nq = 4
    rows = nq * QB
    nsb = S // rows

    def body(q_ref, k_ref, v_ref, kp_ref, vp_ref, b_ref, o_ref, l_ref, ls0, ls1, ls2, ls3):
        ls = [ls0, ls1, ls2, ls3]
        n = pl.program_id(1)
        lane = lax.broadcasted_iota(jnp.int32, (QB, 128), 1)
        units = [(h, j) for h in range(4) for j in range(nq)]

        def keys(cur_ref, prev_ref, h, j):
            sl = slice(h * HD, (h + 1) * HD)
            if j == 0:
                return jnp.concatenate([prev_ref[:, sl], cur_ref[0:QB, sl]], axis=0)
            return cur_ref[(j - 1) * QB:(j + 1) * QB, sl]

        q = jnp.stack([q_ref[j * QB:(j + 1) * QB, h * HD:(h + 1) * HD] for h, j in units])
        k = jnp.stack([keys(k_ref, kp_ref, h, j) for h, j in units])
        v = jnp.stack([keys(v_ref, vp_ref, h, j) for h, j in units])
        bias_b = jnp.stack([b_ref[jnp.minimum(n, 1), h] if j == 0 else b_ref[1, h] for h, j in units])
        s = jnp.einsum("uqd,ukd->uqk", q, k, preferred_element_type=F32) * SCALE + bias_b
        m = jnp.max(s, axis=-1, keepdims=True)
        p = jnp.exp(s - m)
        l = jnp.sum(p, axis=-1, keepdims=True)
        o = jnp.einsum("uqk,ukd->uqd", p.astype(BF16), v, preferred_element_type=F32) / l
        lse = m + jnp.log(l)
        for i, (h, j) in enumerate(units):
            o_ref[j * QB:(j + 1) * QB, h * HD:(h + 1) * HD] = o[i]
            ls[h][j * QB:(j + 1) * QB, :] = jnp.where(lane == h, lse[i], 0.0)
        l_ref[...] = (ls[0][...] + ls[1][...]) + (ls[2][...] + ls[3][...])

    def row(b, n):
        return b * nsb + n

    def prev(b, n):
        return jnp.maximum((b * nsb + n) * nq - 1, 0)

    in_specs = [
        pl.BlockSpec((rows, GW), lambda b, n: (row(b, n), CB_Q)),
        pl.BlockSpec((rows, GW), lambda b, n: (row(b, n), CB_K)),
        pl.BlockSpec((rows, GW), lambda b, n: (row(b, n), CB_V)),
        pl.BlockSpec((QB, GW), lambda b, n: (prev(b, n), CB_K)),
        pl.BlockSpec((QB, GW), lambda b, n: (prev(b, n), CB_V)),
        pl.BlockSpec((None, 2, 4, QB, 2 * QB), lambda b, n: (0, 0, 0, 0, 0)),
    ]
    return pl.pallas_call(
        body, name="attn_fwd0",
        grid=(BL, nsb),
        in_specs=in_specs,
        out_specs=(pl.BlockSpec((rows, GW), lambda b, n: (row(b, n), 0)),
                   pl.BlockSpec((rows, 128), lambda b, n: (row(b, n), 0))),
        out_shape=(jax.ShapeDtypeStruct((T, GW), F32), jax.ShapeDtypeStruct((T, 128), F32)),
        scratch_shapes=[pltpu.VMEM((rows, 128), F32)] * 4,
        compiler_params=pltpu.CompilerParams(vmem_limit_bytes=VMEM_LIMIT),
    )(proj, proj, proj, proj, proj, bias)


def _attn_bwd_dense(proj, d_out, stats, bias, dproj):
    nq = 4
    rows = nq * QB
    nsb = S // rows
    cols_q, cols_k, cols_v = CB * CB_Q, CB * CB_K, CB * CB_V

    def body(q_ref, k_ref, v_ref, do_ref, st_ref, kp_ref, vp_ref, b_ref, _, out_ref, db_ref,
             sq, sk, sv, carry, sems):
        b, n = pl.program_id(0), pl.program_id(1)
        units = [(h, j) for h in range(4) for j in range(nq)]

        @pl.when((b == 0) & (n == 0))
        def _():
            db_ref[...] = jnp.zeros_like(db_ref)

        @pl.when(n == 0)
        def _():
            carry[...] = jnp.zeros_like(carry)

        def write(first_block, position, count):
            row0 = pl.multiple_of(first_block * QB, QB)
            part = pl.ds(position * QB, count * QB)
            _write_columns([(sq.at[part], cols_q), (sk.at[part], cols_k), (sv.at[part], cols_v)],
                           out_ref, row0, sems)

        @pl.when(n == nsb)
        def _():
            sq[0:QB, :] = carry[:, 0:GW].astype(BF16)
            sk[0:QB, :] = carry[:, GW:2 * GW].astype(BF16)
            sv[0:QB, :] = carry[:, 2 * GW:3 * GW].astype(BF16)
            write((b + 1) * nsb * nq - 1, 0, 1)

        @pl.when(n < nsb)
        def _():
            def keys(cur_ref, prev_ref, h, j):
                sl = slice(h * HD, (h + 1) * HD)
                if j == 0:
                    return jnp.concatenate([prev_ref[:, sl], cur_ref[0:QB, sl]], axis=0)
                return cur_ref[(j - 1) * QB:(j + 1) * QB, sl]

            def block(ref, h, j):
                return ref[j * QB:(j + 1) * QB, h * HD:(h + 1) * HD]

            q = jnp.stack([block(q_ref, h, j) for h, j in units])
            do = jnp.stack([block(do_ref, h, j) for h, j in units])
            k = jnp.stack([keys(k_ref, kp_ref, h, j) for h, j in units])
            v = jnp.stack([keys(v_ref, vp_ref, h, j) for h, j in units])
            bias_b = jnp.stack([b_ref[jnp.minimum(n, 1), h] if j == 0 else b_ref[1, h] for h, j in units])
            lse = jnp.stack([st_ref[j * QB:(j + 1) * QB, h:h + 1] for h, j in units])
            delta = jnp.stack([st_ref[j * QB:(j + 1) * QB, 4 + h:5 + h] for h, j in units])
            s = jnp.einsum("uqd,ukd->uqk", q, k, preferred_element_type=F32) * SCALE + bias_b
            p = jnp.exp(s - lse)
            ds = p * (jnp.einsum("uqd,ukd->uqk", do, v, preferred_element_type=F32) - delta)
            for h in range(4):
                tot = ds[h * nq]
                for j in range(1, nq):
                    tot = tot + ds[h * nq + j]
                db_ref[h] += tot
            dsb, pb = ds.astype(BF16), p.astype(BF16)
            dq = jnp.einsum("uqk,ukd->uqd", dsb, k, preferred_element_type=F32) * SCALE
            dk = jnp.einsum("uqk,uqd->ukd", dsb, q, preferred_element_type=F32) * SCALE
            dv = jnp.einsum("uqk,uqd->ukd", pb, do, preferred_element_type=F32)
            for h in range(4):
                sl = slice(h * HD, (h + 1) * HD)
                u0, last = h * nq, h * nq + nq - 1
                sq[0:QB, sl] = carry[:, sl].astype(BF16)
                sk[0:QB, sl] = (carry[:, GW + h * HD:GW + (h + 1) * HD] + dk[u0, :QB]).astype(BF16)
                sv[0:QB, sl] = (carry[:, 2 * GW + h * HD:2 * GW + (h + 1) * HD] + dv[u0, :QB]).astype(BF16)
                for j in range(nq - 1):
                    pos = slice((j + 1) * QB, (j + 2) * QB)
                    sq[pos, sl] = dq[u0 + j].astype(BF16)
                    sk[pos, sl] = (dk[u0 + j, QB:] + dk[u0 + j + 1, :QB]).astype(BF16)
                    sv[pos, sl] = (dv[u0 + j, QB:] + dv[u0 + j + 1, :QB]).astype(BF16)
                carry[:, sl] = dq[last]
                carry[:, GW + h * HD:GW + (h + 1) * HD] = dk[last, QB:]
                carry[:, 2 * GW + h * HD:2 * GW + (h + 1) * HD] = dv[last, QB:]

            @pl.when(n == 0)
            def _():
                write(b * nsb * nq, 1, nq - 1)

            @pl.when(n > 0)
            def _():
                write((b * nsb + n) * nq - 1, 0, nq)

    def row(b, n):
        return b * nsb + jnp.minimum(n, nsb - 1)

    def prev(b, n):
        return jnp.maximum(row(b, n) * nq - 1, 0)

    in_specs = [
        pl.BlockSpec((rows, GW), lambda b, n: (row(b, n), CB_Q)),
        pl.BlockSpec((rows, GW), lambda b, n: (row(b, n), CB_K)),
        pl.BlockSpec((rows, GW), lambda b, n: (row(b, n), CB_V)),
        pl.BlockSpec((rows, GW), lambda b, n: (row(b, n), 0)),
        pl.BlockSpec((rows, 128), lambda b, n: (row(b, n), 0)),
        pl.BlockSpec((QB, GW), lambda b, n: (prev(b, n), CB_K)),
        pl.BlockSpec((QB, GW), lambda b, n: (prev(b, n), CB_V)),
        pl.BlockSpec((None, 2, 4, QB, 2 * QB), lambda b, n: (0, 0, 0, 0, 0)),
        pl.BlockSpec(memory_space=pl.ANY),
    ]
    return pl.pallas_call(
        body, name="attn_bwd0",
        grid=(BL, nsb + 1),
        in_specs=in_specs,
        out_specs=(pl.BlockSpec(memory_space=pl.ANY),
                   pl.BlockSpec((4, QB, 2 * QB), lambda b, n: (0, 0, 0))),
        out_shape=(jax.ShapeDtypeStruct((T, NCOL), BF16),
                   jax.ShapeDtypeStruct((4, QB, 2 * QB), F32)),
        scratch_shapes=[pltpu.VMEM((rows, GW), BF16)] * 3
                       + [pltpu.VMEM((QB, 3 * GW), F32), pltpu.SemaphoreType.DMA((3,))],
        input_output_aliases={8: 0},
        compiler_params=pltpu.CompilerParams(vmem_limit_bytes=VMEM_LIMIT),
    )(proj, proj, proj, d_out, stats, proj, proj, bias, dproj)


def _attention_forward(proj, rel_bias):
    buckets_np, masks_np = _bucket_maps()
    buckets, masks = jnp.asarray(buckets_np), jnp.asarray(masks_np)
    bias = _bias_expand(rel_bias, buckets, masks)
    fwd = [_attn_fwd_dense(proj, bias)] + [_attn_fwd(proj, bias, g) for g in (1, 2)]
    return bias, buckets, [f[0] for f in fwd], [f[1] for f in fwd]


def _local_step(x2, tgt2, mod3, h, proj, attn, w_ao, w_co, w_o, conv_w, conv_b, ln_g, ln_b):
    bias, buckets, o_g, lse_g = attn

    (dproj, dyc, d_o, stats, dxd, gw_o, gw_co, gw_ao, tail_vec) = _tail(
        x2, tgt2, mod3, o_g, lse_g, proj, w_ao, w_co, w_o, conv_w, conv_b, ln_g, ln_b)

    dproj, db = _attn_bwd_dense(proj, d_o, stats, bias, dproj)
    dbias = [db]
    for g in (1, 2):
        dproj, db = _attn_bwd(proj, d_o, stats, bias, dproj, g)
        dbias.append(db)
    g_rel_bias = _bias_grad(*dbias, buckets)
    dproj, conv_vec = _conv_bwd(dyc, proj, conv_w, dproj)

    gw_ao = jnp.transpose(gw_ao.reshape(GW, N_DEV, D // N_DEV), (1, 0, 2))
    return dproj, dxd, gw_ao, gw_co, gw_o, conv_vec, g_rel_bias, tail_vec


def kernel(x, c, w_ada, b_ada, w_in, conv_w, conv_b, rel_bias, w_attn_out, w_conv_out, w_o, ln_g, ln_b, loss_target, m_w_ada, m_b_ada, m_w_in, m_conv_w, m_conv_b, m_rel_bias, m_w_attn_out, m_w_conv_out, m_w_o, m_ln_g, m_ln_b, v_w_ada, v_b_ada, v_w_in, v_conv_w, v_conv_b, v_rel_bias, v_w_attn_out, v_w_conv_out, v_w_o, v_ln_g, v_ln_b):
    me = _my_index()
    x2 = x.reshape(T, D)
    tgt2 = loss_target.reshape(T, D)

    b_cols = lax.dynamic_slice(b_ada, (0, me * ADA_SHARD), (1, ADA_SHARD))
    c_g, mod_in = _mod_exchange(jnp.pad(c, ((0, 8 - BL), (0, 0))), w_ada[0], b_cols)
    c_all = c_g[:, 0:BL, :].reshape(N_DEV * BL, D)
    mod3 = jnp.transpose(mod_in[:, 0:BL, :], (1, 0, 2)).reshape(BL, 3, D)

    h = _prep_h(x2, mod3)
    rows_shape = jax.ShapeDtypeStruct((N_DEV, D // N_DEV, D), BF16)
    proj, w_in_all, (w_ao_g, w_co_g, w_o_g, conv_w_g) = _gather_proj(
        _shard_order(), h, w_in[0].astype(BF16), 1024,
        ([w_attn_out[0].astype(BF16), w_conv_out[0].astype(BF16), w_o[0].astype(BF16), conv_w[0]],
         [jax.ShapeDtypeStruct((N_DEV, GW, D // N_DEV), BF16), rows_shape, rows_shape,
          jax.ShapeDtypeStruct((N_DEV, 3, D // N_DEV), F32)]))

    attn = _attention_forward(proj, rel_bias)
    w_ao_full = jnp.transpose(w_ao_g, (1, 0, 2)).reshape(GW, D)
    w_co_full = w_co_g.reshape(D, D)
    w_o_full = w_o_g.reshape(D, D)
    conv_w_full = jnp.transpose(conv_w_g, (1, 0, 2)).reshape(3, D)

    (dproj, dxd, gw_ao, gw_co, gw_o, conv_vec, g_rel_bias, tail_vec) = _local_step(
        x2, tgt2, mod3, h, proj, attn, w_ao_full, w_co_full, w_o_full,
        conv_w_full, conv_b, ln_g, ln_b)

    g_conv_w_blocks = jnp.transpose(conv_vec[0:3].reshape(3, N_DEV, D // N_DEV), (1, 0, 2))
    partials = [gw_ao, gw_co.reshape(N_DEV, D // N_DEV, D), gw_o.reshape(N_DEV, D // N_DEV, D), g_conv_w_blocks]
    w_in_sums, w_in_parts, sib = _gw_in_pair(
        _slice_order(), h, dproj, partials,
        [jax.ShapeDtypeStruct((4, GW, D // N_DEV), BF16),
         jax.ShapeDtypeStruct((4, D // N_DEV, D), BF16),
         jax.ShapeDtypeStruct((4, D // N_DEV, D), BF16),
         jax.ShapeDtypeStruct((4, 3, D // N_DEV), F32)])
    core = lax.axis_index("c").astype(jnp.int32).reshape(1)
    chip_sums = [w_in_sums] + list(_pair_add(core, partials, sib))
    hops = [(3,)] + [(1, 2, 3)] * 4
    grad_x, mod_vec, (r_in, r_ao, r_co, r_o, r_cw) = _dh_dx(
        dproj, w_in_all, x2, dxd, mod3, chip_sums, hops, w_in_parts)

    small = jnp.concatenate([
        tail_vec[0:4],
        jnp.pad(g_rel_bias.reshape(1, N_BUCKETS * N_HEADS), ((0, 0), (0, D - N_BUCKETS * N_HEADS))),
        jnp.zeros((3, D), F32)], axis=0)
    dmod = jnp.concatenate([mod_vec[0:2], mod_vec[2:4], tail_vec[4:6]], axis=1)
    small_g, dmod_g = _all_gather(
        [small, dmod],
        [jax.ShapeDtypeStruct((N_DEV, 8, D), F32), jax.ShapeDtypeStruct((N_DEV, BL, 3 * D), F32)],
        "gather_small")
    dmod_all = dmod_g.reshape(N_DEV * BL, 3 * D)
    small_names = ["b_ada", "conv_b", "rel_bias", "ln_g", "ln_b"]
    small_params = [(b_ada, m_b_ada, v_b_ada), (conv_b, m_conv_b, v_conv_b), (rel_bias, m_rel_bias, v_rel_bias),
                    (ln_g, m_ln_g, v_ln_g), (ln_b, m_ln_b, v_ln_b)]
    loss, small_res = _small_updates(
        small_g, dmod_all, small_g[:, 4, :N_BUCKETS * N_HEADS].reshape(N_DEV, N_BUCKETS, N_HEADS), small_params)
    loss = loss.reshape(())
    g_w_ada = _ada_bwd(jnp.transpose(c_all), lax.dynamic_slice(dmod_all, (0, me * ADA_SHARD),
                                                               (N_DEV * BL, ADA_SHARD)))

    def upd(parts, w, m, v, name, row_tile=None):
        shape = w.shape
        w2, m2, v2 = (t.reshape(parts.shape[1:]) for t in (w, m, v))
        return tuple(t.reshape(shape) for t in _adamw(parts, w2, m2, v2, name, row_tile))

    res = {
        "w_ada": upd(g_w_ada[None], w_ada, m_w_ada, v_w_ada, "adam_w_ada", 256),
        "w_in": upd(r_in, w_in, m_w_in, v_w_in, "adam_w_in", 128),
    }
    mid_names = ["conv_w", "w_attn_out", "w_conv_out", "w_o"]
    mid_parts = [r_cw, r_ao, r_co, r_o]
    mid_full = [(conv_w, m_conv_w, v_conv_w), (w_attn_out, m_w_attn_out, v_w_attn_out),
                (w_conv_out, m_w_conv_out, v_w_conv_out), (w_o, m_w_o, v_w_o)]
    mid_res = _multi_adamw(mid_parts, [tuple(t[0] for t in wmv) for wmv in mid_full], "adam_mid")
    for nm, wmv, outs4 in zip(mid_names, mid_full, mid_res):
        res[nm] = tuple(t[None] for t in outs4)
    res.update(dict(zip(small_names, small_res)))
    order = ["w_ada", "b_ada", "w_in", "conv_w", "conv_b", "rel_bias", "w_attn_out", "w_conv_out",
             "w_o", "ln_g", "ln_b"]
    outs = [loss, grad_x.reshape(BL, S, D)]
    for k in range(4):
        outs += [res[name][k] for name in order]
    return tuple(outs)
```

```python
import math

import numpy as np
import jax
import jax.numpy as jnp
from jax import lax
from jax.experimental import pallas as pl
from jax.experimental.pallas import tpu as pltpu

F32 = jnp.float32
BF16 = jnp.bfloat16
MESH = pl.DeviceIdType.MESH

N_DEV = 8
D = 1024
S = 2048
BL = 2
T = BL * S
NCOL = 11264
SHARD = NCOL // N_DEV
CB = 512
NCB = NCOL // CB
HD = 128
GW = 512
QB = 128
DILATIONS = (1, 4, 16)
N_STEPS = 128
N_BUCKETS = 32
N_HEADS = 12
ALPHA = 2.0 ** 0.25
LN_EPS = 1e-5
NEG_INF = -1e30
SCALE = HD ** -0.5
ADA_SHARD = 3 * D // N_DEV

CB_Q, CB_K, CB_V, CB_GA = 0, 3, 6, 9
KB_U, KB_BG, KB_CG, KB_GC, KB_MA, KB_MC = 5, 6, 7, 8, 9, 10

ADAM_LR, ADAM_B1, ADAM_B2, ADAM_EPS, ADAM_WD, ADAM_STEP = 0.001, 0.9, 0.999, 1e-08, 0.01, 10

VMEM_LIMIT = 56 * 1024 * 1024
VMEM_LIMIT_TAIL = 62 * 1024 * 1024


def _dot(a, b):
    return jnp.dot(a, b, preferred_element_type=F32)


def _dot_nt(a, b):
    return lax.dot_general(a, b, (((1,), (1,)), ((), ())), preferred_element_type=F32)


def _dot_tn(a, b):
    return lax.dot_general(a, b, (((0,), (0,)), ((), ())), preferred_element_type=F32)


def _sigmoid(v):
    return 1.0 / (1.0 + jnp.exp(-v))


def _write_columns(pieces, dst_hbm, row0, sems):
    copies = []
    for k, (src, col0) in enumerate(pieces):
        rows, width = src.shape
        copies.append(pltpu.make_async_copy(
            src, dst_hbm.at[pl.ds(row0, rows), pl.ds(col0, width)], sems.at[k]))
    for cp in copies:
        cp.start()
    for cp in copies:
        cp.wait()


def _my_index():
    return 4 * lax.axis_index("x") + 2 * lax.axis_index("y") + lax.axis_index("c")


class _Gather:
    def __init__(self, ins, outs, stage, send_sems, recv_sems, local_sems):
        self.ins, self.outs, self.stage = ins, outs, stage
        self.send_sems, self.recv_sems, self.local_sems = send_sems, recv_sems, local_sems
        x, y, c = lax.axis_index("x"), lax.axis_index("y"), lax.axis_index("c")
        self.c = c
        self.me, self.sibling = (x, y, c), (x, y, 1 - c)
        self.chips = [(1 - x, y), (x, 1 - y), (1 - x, 1 - y)]

    @staticmethod
    def scratch(arrs):
        n = len(arrs)
        return ([pltpu.SemaphoreType.DMA((7 * n,)), pltpu.SemaphoreType.DMA((7 * n,)),
                 pltpu.SemaphoreType.DMA((n,))] + [pltpu.VMEM(a.shape, a.dtype) for a in arrs])

    def _copy(self, a, k, block, to, src=None):
        dst = self.outs[a].at[4 * block[0] + 2 * block[1] + block[2]]
        return pltpu.make_async_remote_copy(
            src_ref=dst if src is None else src, dst_ref=dst,
            send_sem=self.send_sems.at[a * 7 + k], recv_sem=self.recv_sems.at[a * 7 + k],
            device_id=to, device_id_type=MESH)

    def _first(self):
        first = []
        for a in range(len(self.ins)):
            first.append(self._copy(a, 0, self.me, self.sibling, src=self.ins[a]))
            first += [self._copy(a, 1 + j, self.me, (*chip, self.c), src=self.ins[a])
                      for j, chip in enumerate(self.chips)]
        return first

    def _mine(self):
        me = self.me
        return [pltpu.make_async_copy(self.stage[a], self.outs[a].at[4 * me[0] + 2 * me[1] + me[2]],
                                      self.local_sems.at[a]) for a in range(len(self.ins))]

    def begin(self):
        for cp in self._first():
            cp.start()
        loads = [pltpu.make_async_copy(self.ins[a], self.stage[a], self.local_sems.at[a])
                 for a in range(len(self.ins))]
        for cp in loads:
            cp.start()
        for cp in loads:
            cp.wait()
        for cp in self._mine():
            cp.start()

    def finish(self):
        n, c, me, sibling = len(self.ins), self.c, self.me, self.sibling
        passed = []
        for j, chip in enumerate(self.chips):
            for a in range(n):
                self._copy(a, 1 + j, (*chip, c), me).wait_recv()
                fwd = self._copy(a, 4 + j, (*chip, c), sibling)
                fwd.start()
                passed.append(fwd)
        for a in range(n):
            self._copy(a, 0, sibling, me).wait_recv()
        for j, chip in enumerate(self.chips):
            for a in range(n):
                self._copy(a, 4 + j, (*chip, 1 - c), me).wait_recv()
        for cp in self._first() + passed:
            cp.wait_send()
        for cp in self._mine():
            cp.wait()


def _all_gather(arrs, out_shapes, name):
    n = len(arrs)

    def body(*refs):
        g = _Gather(refs[:n], refs[n:2 * n], refs[2 * n + 3:], *refs[2 * n:2 * n + 3])
        g.begin()
        g.finish()

    any_spec = pl.BlockSpec(memory_space=pl.ANY)
    return pl.pallas_call(
        body, name=name,
        out_shape=tuple(out_shapes),
        in_specs=[any_spec] * n,
        out_specs=tuple([any_spec] * n),
        scratch_shapes=_Gather.scratch(arrs),
    )(*arrs)


def _slice_order():
    x, y, c = lax.axis_index("x"), lax.axis_index("y"), lax.axis_index("c")
    slots = []
    for q in (2 * (1 - x) + y, 2 * x + (1 - y), 2 * (1 - x) + (1 - y), 2 * x + y):
        slots += [2 * q + 1 - c, 2 * q + c]
    return jnp.stack(slots).astype(jnp.int32)


def _gw_in_pair(order, h, dproj, smalls, small_shapes4):
    kk, m = h.shape
    tk = min(kk, 2048)
    nk = kk // tk
    ncols = dproj.shape[1] // N_DEV
    n = len(smalls)

    def body(order_ref, h_ref, d_ref, *rest):
        ins = rest[:n]
        sums_hbm, parts_hbm = rest[n], rest[n + 1]
        sib = rest[n + 2:2 * n + 2]
        (acc, sendbuf, recvbuf, sumbuf, send_sems, recv_sems, local_sem, ssend, srecv,
         isend, irecv) = rest[2 * n + 2:]
        js, k = pl.program_id(0), pl.program_id(1)
        x, y, c = lax.axis_index("x"), lax.axis_index("y"), lax.axis_index("c")
        sibling = (x, y, 1 - c)
        my_chip = 2 * x + y
        near = [(1 - x, y, c), (x, 1 - y, c)]

        def ici_copy(p, out_chip):
            peer = near[p]
            return pltpu.make_async_remote_copy(
                src_ref=sumbuf.at[p], dst_ref=parts_hbm.at[out_chip],
                send_sem=isend.at[p], recv_sem=irecv.at[p], device_id=peer, device_id_type=MESH)

        def small_copies():
            return [pltpu.make_async_remote_copy(
                        src_ref=ins[a].at[2 * q + 1 - c], dst_ref=sib[a].at[q],
                        send_sem=ssend.at[a * 4 + q], recv_sem=srecv.at[a * 4 + q],
                        device_id=sibling, device_id_type=MESH)
                    for a in range(n) for q in range(4)]

        def slice_copy(p):
            return pltpu.make_async_remote_copy(
                src_ref=sendbuf, dst_ref=recvbuf.at[p], send_sem=send_sems.at[p], recv_sem=recv_sems.at[p],
                device_id=sibling, device_id_type=MESH)

        def sum_copy(p):
            return pltpu.make_async_copy(sumbuf.at[2], sums_hbm.at[order_ref[2 * p] // 2], local_sem)

        @pl.when((js == 0) & (k == 0))
        def _():
            for cp in small_copies():
                cp.start()

        @pl.when(k == 0)
        def _():
            acc[...] = jnp.zeros_like(acc)

        acc[...] += _dot_tn(h_ref[...], d_ref[...])

        for p in range(4):
            @pl.when((js == 2 * p) & (k == nk - 1))
            def _():
                if p > 0:
                    slice_copy(p - 1).wait_send()
                sendbuf[...] = acc[...].astype(BF16)
                slice_copy(p).start()

            @pl.when((js == 2 * p + 1) & (k == nk - 1))
            def _():
                slice_copy(p).wait_recv()
                if p == 3:
                    sum_copy(2).wait()
                sumbuf[min(p, 2)] = (acc[...] + recvbuf[p].astype(F32)).astype(BF16)
                if p < 2:
                    ici_copy(p, my_chip).start()
                else:
                    sum_copy(p).start()

        @pl.when((js == N_DEV - 1) & (k == nk - 1))
        def _():
            slice_copy(3).wait_send()
            sum_copy(3).wait()
            for cp in small_copies():
                cp.wait()
            for p in range(2):
                ici_copy(p, 2 * near[p][0] + near[p][1]).wait_recv()
                ici_copy(p, my_chip).wait_send()

    any_spec = pl.BlockSpec(memory_space=pl.ANY)
    res = pl.pallas_call(
        body, name="gw_in_pair",
        grid_spec=pltpu.PrefetchScalarGridSpec(
            num_scalar_prefetch=1,
            grid=(N_DEV, nk),
            in_specs=[pl.BlockSpec((tk, m), lambda js, k, order_ref: (k, 0)),
                      pl.BlockSpec((tk, ncols), lambda js, k, order_ref: (k, order_ref[js]))] + [any_spec] * n,
            out_specs=(any_spec,) * (n + 2),
            scratch_shapes=[pltpu.VMEM((m, ncols), F32), pltpu.VMEM((m, ncols), BF16),
                            pltpu.VMEM((4, m, ncols), BF16), pltpu.VMEM((3, m, ncols), BF16),
                            pltpu.SemaphoreType.DMA((4,)), pltpu.SemaphoreType.DMA((4,)),
                            pltpu.SemaphoreType.DMA,
                            pltpu.SemaphoreType.DMA((4 * n,)), pltpu.SemaphoreType.DMA((4 * n,)),
                            pltpu.SemaphoreType.DMA((2,)), pltpu.SemaphoreType.DMA((2,))]),
        out_shape=(jax.ShapeDtypeStruct((4, m, ncols), BF16),) * 2 + tuple(small_shapes4),
        compiler_params=pltpu.CompilerParams(vmem_limit_bytes=VMEM_LIMIT),
    )(order, h, dproj, *smalls)
    return res[0], res[1], res[2:]


def _chip_copies(ins, outs, send_sems, recv_sems, local_sems, hops):
    n = len(ins)
    x, y, c = lax.axis_index("x"), lax.axis_index("y"), lax.axis_index("c")
    my_chip = 2 * x + y

    def peer_of(k):
        return ((1 - x) if (k >> 1) & 1 else x, (1 - y) if k & 1 else y, c)

    def copy(a, k, out_chip):
        peer = peer_of(k)
        return pltpu.make_async_remote_copy(
            src_ref=ins[a].at[2 * peer[0] + peer[1]], dst_ref=outs[a].at[out_chip],
            send_sem=send_sems.at[a * 3 + k - 1], recv_sem=recv_sems.at[a * 3 + k - 1],
            device_id=peer, device_id_type=MESH)

    sends = [copy(a, k, my_chip) for k in range(1, 4) for a in range(n) if k in hops[a]]
    arrivals = []
    for k in range(1, 4):
        peer = peer_of(k)
        arrivals += [copy(a, k, 2 * peer[0] + peer[1]) for a in range(n) if k in hops[a]]
    mine = [pltpu.make_async_copy(ins[a].at[my_chip], outs[a].at[my_chip], local_sems.at[a])
            for a in range(n)]
    return sends, arrivals, mine


def _pair_add(core, mines, theirs):
    n = len(mines)

    def body(core_ref, *refs):
        mine, sib, outs = refs[:n], refs[n:2 * n], refs[2 * n:]
        for a in range(n):
            for q in range(4):
                outs[a][q] = (mine[a][2 * q + core_ref[0]].astype(F32)
                              + sib[a][q].astype(F32)).astype(outs[a].dtype)

    return pl.pallas_call(
        body, name="pair_add",
        in_specs=[pl.BlockSpec(memory_space=pltpu.SMEM)] + [pl.BlockSpec(memory_space=pltpu.VMEM)] * (2 * n),
        out_shape=tuple(jax.ShapeDtypeStruct(t.shape, t.dtype) for t in theirs),
    )(core, *mines, *theirs)


def _mod_exchange(c8, w_ada, b_cols):
    cols = w_ada.shape[1]

    def body(c_ref, w_ref, b_ref, call_ref, mod_ref, msend, send1, recv1, send2, recv2):
        x, y, c = lax.axis_index("x"), lax.axis_index("y"), lax.axis_index("c")
        my_slot = 4 * x + 2 * y + c

        def peer_of(k):
            return ((1 - x) if (k >> 2) & 1 else x, (1 - y) if (k >> 1) & 1 else y, (1 - c) if k & 1 else c)

        def slot_of(dev):
            return 4 * dev[0] + 2 * dev[1] + dev[2]

        def exchange(src_of, dst_ref, send_sems, recv_sems):
            sends, arrivals = [], []
            for k in range(1, 8):
                peer = peer_of(k)
                sends.append(pltpu.make_async_remote_copy(
                    src_ref=src_of(slot_of(peer)), dst_ref=dst_ref.at[my_slot],
                    send_sem=send_sems.at[k - 1], recv_sem=recv_sems.at[k - 1],
                    device_id=peer, device_id_type=MESH))
                arrivals.append(pltpu.make_async_remote_copy(
                    src_ref=src_of(my_slot), dst_ref=dst_ref.at[slot_of(peer)],
                    send_sem=send_sems.at[k - 1], recv_sem=recv_sems.at[k - 1],
                    device_id=peer, device_id_type=MESH))
            for cp in sends:
                cp.start()
            for cp in arrivals:
                cp.wait_recv()
            for cp in sends:
                cp.wait_send()

        call_ref[my_slot] = c_ref[...]
        exchange(lambda s: c_ref, call_ref, send1, recv1)
        cv = call_ref[...].reshape(N_DEV * 8, c_ref.shape[1])
        act = cv * _sigmoid(cv)
        mod = jnp.dot(act, w_ref[...], preferred_element_type=F32,
                      precision=lax.Precision.HIGHEST) + b_ref[...]
        msend[...] = mod.reshape(N_DEV, 8, cols)
        mod_ref[my_slot] = msend[my_slot]
        exchange(lambda s: msend.at[s], mod_ref, send2, recv2)

    return pl.pallas_call(
        body, name="mod_exchange",
        out_shape=(jax.ShapeDtypeStruct((N_DEV, 8, c8.shape[1]), F32),
                   jax.ShapeDtypeStruct((N_DEV, 8, cols), F32)),
        scratch_shapes=[pltpu.VMEM((N_DEV, 8, cols), F32)] + [pltpu.SemaphoreType.DMA((7,))] * 4,
    )(c8, w_ada, b_cols)


def _ada_bwd(c_all_t, dmod_cols):
    def body(c_ref, d_ref, o_ref):
        cv = c_ref[...]
        sc = cv * _sigmoid(cv)
        o_ref[...] = jnp.dot(sc, d_ref[...], preferred_element_type=F32,
                             precision=lax.Precision.HIGHEST)

    return pl.pallas_call(
        body, name="ada_bwd",
        out_shape=jax.ShapeDtypeStruct((c_all_t.shape[0], dmod_cols.shape[1]), F32),
    )(c_all_t, dmod_cols)


def _shard_order():
    x, y, c = lax.axis_index("x"), lax.axis_index("y"), lax.axis_index("c")
    devs = [(x, y, c), (x, y, 1 - c)]
    for chip in [(1 - x, y), (x, 1 - y), (1 - x, 1 - y)]:
        devs += [(*chip, c), (*chip, 1 - c)]
    return jnp.stack([4 * d[0] + 2 * d[1] + d[2] for d in devs]).astype(jnp.int32)


def _prep_h(x2, mod3):
    ts = 512
    per_seq = S // ts

    def body(x_ref, mod_ref, h_ref):
        shift = mod_ref[0, 0:1, :]
        scale = mod_ref[0, 1:2, :]
        h_ref[...] = (x_ref[...] * (1.0 + scale) + shift).astype(BF16)

    return pl.pallas_call(
        body, name="prep_h",
        grid=(T // ts,),
        in_specs=[pl.BlockSpec((ts, D), lambda i: (i, 0)),
                  pl.BlockSpec((1, 3, D), lambda i: (i // per_seq, 0, 0))],
        out_specs=pl.BlockSpec((ts, D), lambda i: (i, 0)),
        out_shape=jax.ShapeDtypeStruct((T, D), BF16),
    )(x2, mod3)


def _gather_proj(order, h, w_shard, tm, ride=()):
    rows, kdim = h.shape
    ncols = w_shard.shape[1]
    n_i = rows // tm
    ride_arrs, ride_shapes = ride if ride else ((), ())
    n_ride = len(ride_arrs)

    def body(order_ref, h_ref, mine_hbm, *rest):
        ride_ins = rest[:n_ride]
        o_ref, all_hbm = rest[n_ride:n_ride + 2]
        ride_outs = rest[n_ride + 2:2 * n_ride + 2]
        wv, send_sems, recv_sems, local_sems = rest[2 * n_ride + 2:2 * n_ride + 6]
        ride_scr = rest[2 * n_ride + 6:]
        j, i = pl.program_id(0), pl.program_id(1)
        x, y, c = lax.axis_index("x"), lax.axis_index("y"), lax.axis_index("c")
        me, sibling = (x, y, c), (x, y, 1 - c)
        chips = [(1 - x, y), (x, 1 - y), (1 - x, 1 - y)]

        def slot(dev):
            return 4 * dev[0] + 2 * dev[1] + dev[2]

        def copy(k, block, to):
            return pltpu.make_async_remote_copy(
                src_ref=wv.at[slot(block)], dst_ref=wv.at[slot(block)],
                send_sem=send_sems.at[k], recv_sem=recv_sems.at[k],
                device_id=to, device_id_type=MESH)

        def keep(step, block):
            return pltpu.make_async_copy(wv.at[slot(block)], all_hbm.at[slot(block)], local_sems.at[step])

        if n_ride:
            gather = _Gather(ride_ins, ride_outs, ride_scr[3:], *ride_scr[:3])
        first = [copy(0, me, sibling)] + [copy(1 + q, me, (*chip, c)) for q, chip in enumerate(chips)]
        passed = [copy(4 + q, (*chip, c), sibling) for q, chip in enumerate(chips)]
        due = [(me, None, None), (sibling, copy(0, sibling, me), None)]
        for q, chip in enumerate(chips):
            due.append(((*chip, c), copy(1 + q, (*chip, c), me), passed[q]))
            due.append(((*chip, 1 - c), copy(4 + q, (*chip, 1 - c), me), None))

        @pl.when((j == 0) & (i == 0))
        def _():
            load = pltpu.make_async_copy(mine_hbm, wv.at[slot(me)], local_sems.at[N_DEV])
            load.start()
            load.wait()
            for cp in first:
                cp.start()
            keep(0, me).start()

        for step in range(1, N_DEV):
            block, arrival, forward = due[step]

            @pl.when((j == step) & (i == 0))
            def _():
                arrival.wait_recv()
                if forward is not None:
                    forward.start()
                keep(step, block).start()
                if n_ride and step == N_DEV - 2:
                    gather.begin()

        o_ref[...] = _dot(h_ref[...], wv[order_ref[j]]).astype(BF16)

        @pl.when((j == N_DEV - 1) & (i == n_i - 1))
        def _():
            for cp in first + passed:
                cp.wait_send()
            for step in range(N_DEV):
                keep(step, due[step][0]).wait()
            if n_ride:
                gather.finish()

    any_spec = pl.BlockSpec(memory_space=pl.ANY)
    res = pl.pallas_call(
        body, name="gather_proj",
        grid_spec=pltpu.PrefetchScalarGridSpec(
            num_scalar_prefetch=1,
            grid=(N_DEV, n_i),
            in_specs=[pl.BlockSpec((tm, kdim), lambda j, i, order_ref: (i, 0)), any_spec] + [any_spec] * n_ride,
            out_specs=(pl.BlockSpec((tm, ncols), lambda j, i, order_ref: (i, order_ref[j])), any_spec)
                      + (any_spec,) * n_ride,
            scratch_shapes=[pltpu.VMEM((N_DEV, kdim, ncols), BF16),
                            pltpu.SemaphoreType.DMA((7,)), pltpu.SemaphoreType.DMA((7,)),
                            pltpu.SemaphoreType.DMA((N_DEV + 1,))]
                           + (_Gather.scratch(ride_arrs) if n_ride else [])),
        out_shape=(jax.ShapeDtypeStruct((rows, N_DEV * ncols), BF16),
                   jax.ShapeDtypeStruct((N_DEV, kdim, ncols), BF16)) + tuple(ride_shapes),
        compiler_params=pltpu.CompilerParams(vmem_limit_bytes=VMEM_LIMIT),
    )(order, h, w_shard, *ride_arrs)
    return res[0], res[1], res[2:]


SKEW_W = 512


def _bucket_maps():
    lanes = np.arange(SKEW_W)

    def buckets_of(steps):
        rows = []
        for dil in DILATIONS:
            dist = np.maximum(steps, 0) * dil
            nf = np.maximum(dist, 1).astype(np.float32)
            large = 16 + (np.log(nf / np.float32(16)) / np.float32(math.log(128.0))
                          * np.float32(16)).astype(np.int32)
            large = np.minimum(large, N_BUCKETS - 1)
            bucket = np.where(dist < 16, dist, large)
            rows.append(np.where((steps >= 0) & (steps <= N_STEPS), bucket, -1).astype(np.int32))
        return np.stack(rows)[:, None, :]

    a = np.arange(QB)[:, None]
    b = np.arange(2 * QB)[None, :]
    steps = a + QB - b
    band = (steps >= 0) & (steps <= N_STEPS)
    first = band & (b >= QB)
    masks = np.stack([first, band]).astype(np.int32)
    return buckets_of(QB - lanes), buckets_of(2 * QB - 1 - lanes), masks


def _bias_expand(rel_bias, lane_buckets, masks):
    def body(tab_ref, bk_ref, mk_ref, o_ref):
        for g in range(3):
            bk = bk_ref[g]
            for h in range(4):
                col = 4 * g + h
                per_offset = jnp.zeros((1, SKEW_W), F32)
                for k in range(N_BUCKETS):
                    per_offset = jnp.where(bk == k, tab_ref[k, col], per_offset)
                tile = pltpu.roll(jnp.broadcast_to(per_offset, (QB, SKEW_W)), 0, 1, stride=1, stride_axis=0)
                tile = tile[:, :2 * QB]
                o_ref[g, 0, h] = jnp.where(mk_ref[0] != 0, tile, NEG_INF)
                o_ref[g, 1, h] = jnp.where(mk_ref[1] != 0, tile, NEG_INF)

    return pl.pallas_call(
        body, name="bias_expand",
        in_specs=[pl.BlockSpec(memory_space=pltpu.SMEM),
                  pl.BlockSpec(memory_space=pltpu.VMEM),
                  pl.BlockSpec(memory_space=pltpu.VMEM)],
        out_shape=jax.ShapeDtypeStruct((3, 2, 4, QB, 2 * QB), F32),
    )(rel_bias, lane_buckets, masks)


def _bias_grad(ds1, ds2, ds3, lane_buckets):
    exchange = jnp.asarray(np.eye(QB, dtype=np.float32)[::-1].copy())

    def body(d1_ref, d2_ref, d3_ref, bk_ref, ex_ref, o_ref):
        for g, d_ref in enumerate((d1_ref, d2_ref, d3_ref)):
            bk = bk_ref[g]
            for h in range(4):
                flipped = jnp.dot(ex_ref[...], d_ref[h], preferred_element_type=F32,
                                  precision=lax.Precision.HIGHEST)
                padded = jnp.concatenate([flipped, jnp.zeros((QB, SKEW_W - 2 * QB), F32)], axis=1)
                skewed = pltpu.roll(padded, 0, 1, stride=1, stride_axis=0)
                per_offset = jnp.sum(skewed, axis=0, keepdims=True)
                for k in range(N_BUCKETS):
                    o_ref[k, 4 * g + h] = jnp.sum(jnp.where(bk == k, per_offset, 0.0))

    return pl.pallas_call(
        body, name="bias_grad",
        in_specs=[pl.BlockSpec(memory_space=pltpu.VMEM)] * 5,
        out_specs=pl.BlockSpec(memory_space=pltpu.SMEM),
        out_shape=jax.ShapeDtypeStruct((N_BUCKETS, N_HEADS), F32),
    )(ds1, ds2, ds3, lane_buckets, exchange)


def _scratch_sets(rows):
    return 4 if rows <= 512 else 1


def _unit_chunks(dil, size=16):
    units = [(h, r) for h in range(4) for r in range(dil)]
    return [units[i:i + size] for i in range(0, len(units), size)]


def _residue_rows(src_ref, copies, h, residue):
    buf = copies[h % len(copies)]
    buf[...] = src_ref[:, h * HD:(h + 1) * HD].astype(F32)
    return lambda r: buf[residue(r), :].astype(BF16)


def _attn_fwd(proj, bias, g):
    dil = DILATIONS[g]
    rows = QB * dil
    nsb = S // rows
    has_prev = nsb > 1

    def residue(r):
        return pl.ds(r, QB, stride=dil)

    n_sets = _scratch_sets(rows)
    n_in = 6 if has_prev else 4
    n_copied = (4 + (2 if has_prev else 0)) * n_sets

    def body(*refs):
        q_ref, kc_ref, vc_ref = refs[:3]
        kp_ref, vp_ref = refs[3:5] if has_prev else (None, None)
        b_ref = refs[n_in - 1]
        o_ref, l_ref = refs[n_in:n_in + 2]
        scr = list(refs[n_in + 2:])
        ls = [scr.pop(0) for _ in range(4)]
        copies = {name: [scr.pop(0) for _ in range(n_sets)]
                  for name in ("q", "kc", "vc", "o") + (("kp", "vp") if has_prev else ())}
        lane = lax.broadcasted_iota(jnp.int32, (QB, 128), 1)
        refs_of = {"q": q_ref, "kc": kc_ref, "vc": vc_ref, "kp": kp_ref, "vp": vp_ref}
        for chunk in _unit_chunks(dil):
            rows_of = {h: {name: _residue_rows(refs_of[name], copies[name], h, residue)
                           for name in refs_of if refs_of[name] is not None}
                       for h in sorted({h for h, _ in chunk})}

            def batch(name):
                return jnp.stack([rows_of[h][name](r) for h, r in chunk])

            q, k, v = batch("q"), batch("kc"), batch("vc")
            if has_prev:
                k = jnp.concatenate([batch("kp"), k], axis=1)
                v = jnp.concatenate([batch("vp"), v], axis=1)
                bias_b = jnp.stack([b_ref[h] for h, _ in chunk])
            else:
                bias_b = jnp.stack([b_ref[h, :, QB:] for h, _ in chunk])
            s = jnp.einsum("uqd,ukd->uqk", q, k, preferred_element_type=F32) * SCALE + bias_b
            m = jnp.max(s, axis=-1, keepdims=True)
            p = jnp.exp(s - m)
            l = jnp.sum(p, axis=-1, keepdims=True)
            o = jnp.einsum("uqk,ukd->uqd", p.astype(BF16), v, preferred_element_type=F32) / l
            lse = m + jnp.log(l)
            for i, (h, r) in enumerate(chunk):
                copies["o"][h % n_sets][residue(r), :] = o[i]
                ls[h][r * QB:(r + 1) * QB, :] = jnp.where(lane == h, lse[i], 0.0)
            for h in sorted({h for h, _ in chunk}):
                o_ref[:, h * HD:(h + 1) * HD] = copies["o"][h % n_sets][...]
        for r in range(dil):
            blk = slice(r * QB, (r + 1) * QB)
            l_ref[residue(r), :] = (ls[0][blk, :] + ls[1][blk, :]) + (ls[2][blk, :] + ls[3][blk, :])

    def row(b, n):
        return b * nsb + n

    def prev(b, n):
        return b * nsb + jnp.maximum(n - 1, 0)

    in_specs = [
        pl.BlockSpec((rows, GW), lambda b, n: (row(b, n), CB_Q + g)),
        pl.BlockSpec((rows, GW), lambda b, n: (row(b, n), CB_K + g)),
        pl.BlockSpec((rows, GW), lambda b, n: (row(b, n), CB_V + g)),
    ]
    args = [proj, proj, proj]
    scratch = [pltpu.VMEM((rows, 128), F32)] * (4 + n_copied)
    if has_prev:
        in_specs += [pl.BlockSpec((rows, GW), lambda b, n: (prev(b, n), CB_K + g)),
                     pl.BlockSpec((rows, GW), lambda b, n: (prev(b, n), CB_V + g))]
        args += [proj, proj]
    in_specs.append(pl.BlockSpec((None, None, 4, QB, 2 * QB),
                                 lambda b, n: (g, jnp.minimum(n, 1), 0, 0, 0)))
    args.append(bias)
    return pl.pallas_call(
        body, name=f"attn_fwd{g}",
        grid=(BL, nsb),
        in_specs=in_specs,
        out_specs=(pl.BlockSpec((rows, GW), lambda b, n: (row(b, n), 0)),
                   pl.BlockSpec((rows, 128), lambda b, n: (row(b, n), 0))),
        out_shape=(jax.ShapeDtypeStruct((T, GW), F32), jax.ShapeDtypeStruct((T, 128), F32)),
        scratch_shapes=scratch,
        compiler_params=pltpu.CompilerParams(vmem_limit_bytes=VMEM_LIMIT),
    )(*args)


def _attn_bwd(proj, d_out, stats, bias, dproj, g):
    dil = DILATIONS[g]
    rows = QB * dil
    nsb = S // rows
    has_prev = nsb > 1
    n_steps = nsb + 1 if has_prev else 1
    n_in = 7 + (2 if has_prev else 0)

    def residue(r):
        return pl.ds(r, QB, stride=dil)

    n_sets = _scratch_sets(rows)

    def body(*refs):
        q_ref, kc_ref, vc_ref, do_ref, st_ref, b_ref = refs[:6]
        kp_ref, vp_ref = refs[6:8] if has_prev else (None, None)
        out_ref, db_ref = refs[n_in], refs[n_in + 1]
        scr = list(refs[n_in + 2:])
        sq, sk, sv, sems = [scr.pop(0) for _ in range(4)]
        carry = scr.pop(0) if has_prev else None
        sts = scr.pop(0)
        copies = {name: [scr.pop(0) for _ in range(n_sets)]
                  for name in ("q", "kc", "vc", "do", "dq", "dk", "dv") + (("kp", "vp") if has_prev else ())}
        b, n = pl.program_id(0), pl.program_id(1)

        @pl.when((b == 0) & (n == 0))
        def _():
            db_ref[...] = jnp.zeros_like(db_ref)

        def finish(h, r, dq, dk, dv):
            for name, val in (("dq", dq), ("dk", dk), ("dv", dv)):
                copies[name][h % n_sets][residue(r), :] = val

        def finish_head(h):
            sl = slice(h * HD, (h + 1) * HD)
            sq[:, sl] = copies["dq"][h % n_sets][...].astype(BF16)
            sk[:, sl] = copies["dk"][h % n_sets][...].astype(BF16)
            sv[:, sl] = copies["dv"][h % n_sets][...].astype(BF16)

        def write_block(blk_idx):
            row0 = pl.multiple_of(blk_idx * rows, rows)
            _write_columns([(sq, CB * (CB_Q + g)), (sk, CB * (CB_K + g)), (sv, CB * (CB_V + g))],
                           out_ref, row0, sems)

        def carried(h, r):
            blk = slice(r * QB, (r + 1) * QB)
            return ((blk, slice(h * HD, (h + 1) * HD)), (blk, slice(GW + h * HD, GW + (h + 1) * HD)),
                    (blk, slice(2 * GW + h * HD, 2 * GW + (h + 1) * HD)))

        if has_prev:
            @pl.when(n == 0)
            def _():
                carry[...] = jnp.zeros_like(carry)

            @pl.when(n == nsb)
            def _():
                for h in range(4):
                    for r in range(dil):
                        cq, ck, cv = carried(h, r)
                        finish(h, r, carry[cq], carry[ck], carry[cv])
                    finish_head(h)
                write_block(b * nsb + nsb - 1)

        @pl.when(n < nsb)
        def _():
            for r in range(dil):
                sts[r * QB:(r + 1) * QB, :] = st_ref[residue(r), :]
            refs_of = {"q": q_ref, "kc": kc_ref, "vc": vc_ref, "do": do_ref, "kp": kp_ref, "vp": vp_ref}
            for chunk in _unit_chunks(dil):
                heads = sorted({h for h, _ in chunk})
                rows_of = {h: {name: _residue_rows(refs_of[name], copies[name], h, residue)
                               for name in refs_of if refs_of[name] is not None}
                           for h in heads}

                def batch(name):
                    return jnp.stack([rows_of[h][name](r) for h, r in chunk])

                q, k, v, do = batch("q"), batch("kc"), batch("vc"), batch("do")
                if has_prev:
                    k = jnp.concatenate([batch("kp"), k], axis=1)
                    v = jnp.concatenate([batch("vp"), v], axis=1)
                    bias_b = jnp.stack([b_ref[h] for h, _ in chunk])
                else:
                    bias_b = jnp.stack([b_ref[h, :, QB:] for h, _ in chunk])
                lse = jnp.stack([sts[r * QB:(r + 1) * QB, h:h + 1] for h, r in chunk])
                delta = jnp.stack([sts[r * QB:(r + 1) * QB, 4 + h:5 + h] for h, r in chunk])
                s = jnp.einsum("uqd,ukd->uqk", q, k, preferred_element_type=F32) * SCALE + bias_b
                p = jnp.exp(s - lse)
                ds = p * (jnp.einsum("uqd,ukd->uqk", do, v, preferred_element_type=F32) - delta)
                for h in heads:
                    mine = [ds[i] for i, (hh, _) in enumerate(chunk) if hh == h]
                    tot = mine[0]
                    for extra in mine[1:]:
                        tot = tot + extra
                    if has_prev:
                        db_ref[h] += tot
                    else:
                        db_ref[h, :, QB:] += tot
                dsb, pb = ds.astype(BF16), p.astype(BF16)
                dq = jnp.einsum("uqk,ukd->uqd", dsb, k, preferred_element_type=F32) * SCALE
                dk = jnp.einsum("uqk,uqd->ukd", dsb, q, preferred_element_type=F32) * SCALE
                dv = jnp.einsum("uqk,uqd->ukd", pb, do, preferred_element_type=F32)
                for i, (h, r) in enumerate(chunk):
                    if has_prev:
                        cq, ck, cv = carried(h, r)
                        finish(h, r, carry[cq], carry[ck] + dk[i, :QB], carry[cv] + dv[i, :QB])
                        carry[cq] = dq[i]
                        carry[ck] = dk[i, QB:]
                        carry[cv] = dv[i, QB:]
                    else:
                        finish(h, r, dq[i], dk[i], dv[i])
                for h in heads:
                    finish_head(h)
            if has_prev:
                @pl.when(n > 0)
                def _():
                    write_block(b * nsb + n - 1)
            else:
                write_block(b)

    def row(b, n):
        return b * nsb + jnp.minimum(n, nsb - 1)

    def prev(b, n):
        return b * nsb + jnp.maximum(jnp.minimum(n, nsb - 1) - 1, 0)

    in_specs = [
        pl.BlockSpec((rows, GW), lambda b, n: (row(b, n), CB_Q + g)),
        pl.BlockSpec((rows, GW), lambda b, n: (row(b, n), CB_K + g)),
        pl.BlockSpec((rows, GW), lambda b, n: (row(b, n), CB_V + g)),
        pl.BlockSpec((rows, GW), lambda b, n: (row(b, n), 0)),
        pl.BlockSpec((rows, 128), lambda b, n: (row(b, n), 0)),
        pl.BlockSpec((None, None, 4, QB, 2 * QB),
                     lambda b, n: (g, jnp.minimum(jnp.minimum(n, nsb - 1), 1), 0, 0, 0)),
    ]
    args = [proj, proj, proj, d_out, stats, bias]
    scratch = [pltpu.VMEM((rows, GW), BF16)] * 3 + [pltpu.SemaphoreType.DMA((3,))]
    if has_prev:
        in_specs += [pl.BlockSpec((rows, GW), lambda b, n: (prev(b, n), CB_K + g)),
                     pl.BlockSpec((rows, GW), lambda b, n: (prev(b, n), CB_V + g))]
        args += [proj, proj]
        scratch.append(pltpu.VMEM((rows, 3 * GW), F32))
    n_copied = (7 + (2 if has_prev else 0)) * n_sets
    scratch += [pltpu.VMEM((rows, 128), F32)] * (1 + n_copied)
    in_specs.append(pl.BlockSpec(memory_space=pl.ANY))
    args.append(dproj)
    return pl.pallas_call(
        body, name=f"attn_bwd{g}",
        grid=(BL, n_steps),
        in_specs=in_specs,
        out_specs=(pl.BlockSpec(memory_space=pl.ANY),
                   pl.BlockSpec((4, QB, 2 * QB), lambda b, n: (0, 0, 0))),
        out_shape=(jax.ShapeDtypeStruct((T, NCOL), BF16),
                   jax.ShapeDtypeStruct((4, QB, 2 * QB), F32)),
        scratch_shapes=scratch,
        input_output_aliases={len(args) - 1: 0},
        compiler_params=pltpu.CompilerParams(vmem_limit_bytes=VMEM_LIMIT),
    )(*args)


def _tail(x2, tgt2, mod3, o_g, lse_g, proj, w_ao, w_co, w_o, conv_w, conv_b, ln_g, ln_b):
    tm = 256
    per_seq = S // tm
    halo = 16

    def body(x_ref, t_ref, mod_ref, o1_ref, o2_ref, o3_ref, l1_ref, l2_ref, l3_ref,
             ga_ref, u_ref, bg_ref, cg_ref, gc_ref, ma_ref, mc_ref, up_ref, cp_ref,
             wao_ref, wco_ref, wo_ref, cw_ref, cb_ref, lg_ref, lb_ref,
             dproj_ref, dyc_ref, do_ref, st_ref, dxd_ref,
             gwo_ref, gwco_ref, gwao_ref, vec_ref,
             dga_s, dbg_s, dgm_s, sems, acc_o, acc_co, acc_ao):
        i = pl.program_id(0)
        bidx = i // per_seq
        first = (i % per_seq) == 0

        @pl.when(i == 0)
        def _():
            vec_ref[...] = jnp.zeros_like(vec_ref)

        l1, l2, l3 = l1_ref[...], l2_ref[...], l3_ref[...]
        mx = jnp.maximum(jnp.maximum(l1, l2), l3)
        e1, e2, e3 = jnp.exp(l1 - mx), jnp.exp(l2 - mx), jnp.exp(l3 - mx)
        esum = e1 + e2 + e3
        lse_tot = mx + jnp.log(esum)
        w1, w2, w3 = e1 / esum, e2 / esum, e3 / esum

        def per_head(wv):
            return jnp.concatenate([jnp.broadcast_to(wv[:, h:h + 1], (tm, HD)) for h in range(4)], axis=1)

        o = per_head(w1) * o1_ref[...] + per_head(w2) * o2_ref[...] + per_head(w3) * o3_ref[...]

        ga = ga_ref[...].astype(F32)
        sig_ga = _sigmoid(ga)
        silu_ga = ga * sig_ga
        a_in = (o * silu_ga).astype(BF16)
        a_out = _dot(a_in, wao_ref[...])

        u = u_ref[...].astype(F32)
        cg = cg_ref[...].astype(F32)
        z = cg * u
        zp = cp_ref[...].astype(F32) * up_ref[...].astype(F32)
        zp = jnp.where(first, 0.0, zp)
        zcat = jnp.concatenate([zp, z], axis=0)
        z1 = pltpu.roll(zcat, 1, 0)[halo:]
        z2 = pltpu.roll(zcat, 2, 0)[halo:]
        y_conv = cw_ref[0:1, :] * z2 + cw_ref[1:2, :] * z1 + cw_ref[2:3, :] * z + cb_ref[...]
        gc = gc_ref[...].astype(F32)
        sig_gc = _sigmoid(gc)
        silu_gc = gc * sig_gc
        bg = bg_ref[...].astype(F32)
        bg_yc = bg * y_conv
        s_in = (bg_yc * silu_gc).astype(BF16)
        s_out = _dot(s_in, wco_ref[...])

        sa = _sigmoid(ma_ref[...].astype(F32))
        sc = _sigmoid(mc_ref[...].astype(F32))
        merged = (sa * a_out + sc * s_out).astype(BF16)
        y = _dot(merged, wo_ref[...])
        gate1 = 1.0 + mod_ref[0, 2:3, :]
        xv = x_ref[...]
        resid = ALPHA * xv + gate1 * y
        mu = jnp.mean(resid, axis=1, keepdims=True)
        xc = resid - mu
        var = jnp.mean(xc * xc, axis=1, keepdims=True)
        rstd = lax.rsqrt(var + LN_EPS)
        xhat = xc * rstd
        lg = lg_ref[...]
        err = xhat * lg + lb_ref[...] - t_ref[...]
        vec_ref[3:4, :] += (0.5 / D) * jnp.sum(err * err, axis=0, keepdims=True)

        vec_ref[1:2, :] += (1.0 / D) * jnp.sum(err * xhat, axis=0, keepdims=True)
        vec_ref[2:3, :] += (1.0 / D) * jnp.sum(err, axis=0, keepdims=True)
        dxh = err * (lg * (1.0 / D))
        dres = rstd * (dxh - jnp.mean(dxh, axis=1, keepdims=True)
                       - xhat * jnp.mean(dxh * xhat, axis=1, keepdims=True))
        dxd_ref[...] = ALPHA * dres
        dgate = jnp.sum(dres * y, axis=0, keepdims=True)
        vec_ref[4:5, :] += jnp.where(bidx == 0, dgate, 0.0)
        vec_ref[5:6, :] += jnp.where(bidx == 1, dgate, 0.0)
        dy = (dres * gate1).astype(BF16)

        dmerged = _dot_nt(dy, wo_ref[...])
        da_out_f = dmerged * sa
        ds_out_f = dmerged * sc
        da_out = da_out_f.astype(BF16)
        ds_out = ds_out_f.astype(BF16)
        dgm_s[:, 2 * D:3 * D] = (ds_out_f * s_out * (1.0 - sc)).astype(BF16)
        dgm_s[:, D:2 * D] = (da_out_f * a_out * (1.0 - sa)).astype(BF16)
        da_in = _dot_nt(da_out, wao_ref[...])
        ds_in = _dot_nt(ds_out, wco_ref[...])

        d_o = da_in * silu_ga
        do_ref[...] = d_o.astype(BF16)
        dga_s[...] = (da_in * o * (sig_ga + silu_ga * (1.0 - sig_ga))).astype(BF16)
        lane = lax.broadcasted_iota(jnp.int32, (tm, 128), 1)
        stats = lse_tot
        od = o * d_o
        for h in range(4):
            delta = jnp.sum(od[:, h * HD:(h + 1) * HD], axis=1, keepdims=True)
            stats = jnp.where(lane == 4 + h, delta, stats)
        st_ref[...] = stats

        ds_silu = ds_in * silu_gc
        dbg_s[...] = (ds_silu * y_conv).astype(BF16)
        dyc = ds_silu * bg
        dyc_ref[...] = dyc
        vec_ref[0:1, :] += jnp.sum(dyc, axis=0, keepdims=True)
        dgm_s[:, 0:D] = (ds_in * bg_yc * (sig_gc + silu_gc * (1.0 - sig_gc))).astype(BF16)

        @pl.when(i == 0)
        def _():
            acc_o[...] = jnp.zeros_like(acc_o)
            acc_co[...] = jnp.zeros_like(acc_co)
            acc_ao[...] = jnp.zeros_like(acc_ao)

        acc_o[...] += _dot_tn(merged, dy)
        acc_co[...] += _dot_tn(s_in, ds_out)
        acc_ao[...] += _dot_tn(a_in, da_out)

        @pl.when(i == T // tm - 1)
        def _():
            gwo_ref[...] = acc_o[...].astype(BF16)
            gwco_ref[...] = acc_co[...].astype(BF16)
            gwao_ref[...] = acc_ao[...].astype(BF16)

        _write_columns([(dga_s, CB * CB_GA), (dbg_s, D * KB_BG), (dgm_s, D * KB_GC)],
                       dproj_ref, pl.multiple_of(i * tm, tm), sems)

    def tile(width, cblk=0):
        return pl.BlockSpec((tm, width), lambda i: (i, cblk))

    def whole(shape):
        return pl.BlockSpec(shape, lambda i: tuple(0 for _ in shape))

    def once(shape):
        return pl.BlockSpec(shape, lambda i: tuple(0 for _ in shape), pipeline_mode=pl.Buffered(1))

    prev_rows = lambda i: (jnp.maximum(i * (tm // halo) - 1, 0),)
    in_specs = [
        tile(D), tile(D), pl.BlockSpec((1, 3, D), lambda i: (i // per_seq, 0, 0)),
        tile(GW), tile(GW), tile(GW), tile(128), tile(128), tile(128),
        tile(GW, CB_GA), tile(D, KB_U), tile(D, KB_BG), tile(D, KB_CG), tile(D, KB_GC),
        tile(D, KB_MA), tile(D, KB_MC),
        pl.BlockSpec((halo, D), lambda i: (*prev_rows(i), KB_U)),
        pl.BlockSpec((halo, D), lambda i: (*prev_rows(i), KB_CG)),
        whole((GW, D)), whole((D, D)), whole((D, D)),
        whole((3, D)), whole((1, D)), whole((1, D)), whole((1, D)),
    ]
    out_specs = (
        pl.BlockSpec(memory_space=pl.ANY), tile(D), tile(GW), tile(128), tile(D),
        once((D, D)), once((D, D)), once((GW, D)),
        pl.BlockSpec((8, D), lambda i: (0, 0)),
    )
    out_shape = (
        jax.ShapeDtypeStruct((T, NCOL), BF16),
        jax.ShapeDtypeStruct((T, D), F32),
        jax.ShapeDtypeStruct((T, GW), BF16),
        jax.ShapeDtypeStruct((T, 128), F32),
        jax.ShapeDtypeStruct((T, D), F32),
        jax.ShapeDtypeStruct((D, D), BF16),
        jax.ShapeDtypeStruct((D, D), BF16),
        jax.ShapeDtypeStruct((GW, D), BF16),
        jax.ShapeDtypeStruct((8, D), F32),
    )
    return pl.pallas_call(
        body, name="tail",
        grid=(T // tm,),
        in_specs=in_specs, out_specs=out_specs, out_shape=out_shape,
        scratch_shapes=[pltpu.VMEM((tm, GW), BF16), pltpu.VMEM((tm, D), BF16), pltpu.VMEM((tm, 3 * D), BF16),
                        pltpu.SemaphoreType.DMA((3,)),
                        pltpu.VMEM((D, D), F32), pltpu.VMEM((D, D), F32), pltpu.VMEM((GW, D), F32)],
        compiler_params=pltpu.CompilerParams(vmem_limit_bytes=VMEM_LIMIT_TAIL),
    )(x2, tgt2, mod3, *o_g, *lse_g, proj, proj, proj, proj, proj, proj, proj, proj, proj,
      w_ao, w_co, w_o, conv_w, conv_b, ln_g, ln_b)


def _conv_bwd(dyc, proj, conv_w, dproj):
    tm = 512
    per_seq = S // tm

    def body(d_ref, dn_ref, u_ref, c_ref, cw_ref, _, dproj_ref, g_ref, du_s, dc_s, sems):
        i = pl.program_id(0)
        last = (i % per_seq) == per_seq - 1

        @pl.when(i == 0)
        def _():
            g_ref[...] = jnp.zeros_like(g_ref)

        d = d_ref[...]
        dn = jnp.where(last, 0.0, dn_ref[...])
        dcat = jnp.concatenate([d, dn], axis=0)
        d1 = pltpu.roll(dcat, tm + 8 - 1, 0)[:tm]
        d2 = pltpu.roll(dcat, tm + 8 - 2, 0)[:tm]
        dz = cw_ref[2:3, :] * d + cw_ref[1:2, :] * d1 + cw_ref[0:1, :] * d2
        u = u_ref[...].astype(F32)
        cg = c_ref[...].astype(F32)
        du_s[...] = (dz * cg).astype(BF16)
        dc_s[...] = (dz * u).astype(BF16)
        _write_columns([(du_s, D * KB_U), (dc_s, D * KB_CG)], dproj_ref, pl.multiple_of(i * tm, tm), sems)

        z = cg * u
        g_ref[0:1, :] += jnp.sum(d2 * z, axis=0, keepdims=True)
        g_ref[1:2, :] += jnp.sum(d1 * z, axis=0, keepdims=True)
        g_ref[2:3, :] += jnp.sum(d * z, axis=0, keepdims=True)

    n_tiles = T // tm
    next_rows = lambda i: jnp.minimum((i + 1) * (tm // 8), T // 8 - 1)
    return pl.pallas_call(
        body, name="conv_bwd",
        grid=(n_tiles,),
        in_specs=[pl.BlockSpec((tm, D), lambda i: (i, 0)),
                  pl.BlockSpec((8, D), lambda i: (next_rows(i), 0)),
                  pl.BlockSpec((tm, D), lambda i: (i, KB_U)),
                  pl.BlockSpec((tm, D), lambda i: (i, KB_CG)),
                  pl.BlockSpec((3, D), lambda i: (0, 0)),
                  pl.BlockSpec(memory_space=pl.ANY)],
        out_specs=(pl.BlockSpec(memory_space=pl.ANY),
                   pl.BlockSpec((8, D), lambda i: (0, 0))),
        out_shape=(jax.ShapeDtypeStruct((T, NCOL), BF16),
                   jax.ShapeDtypeStruct((8, D), F32)),
        scratch_shapes=[pltpu.VMEM((tm, D), BF16), pltpu.VMEM((tm, D), BF16), pltpu.SemaphoreType.DMA((2,))],
        input_output_aliases={5: 0},
        compiler_params=pltpu.CompilerParams(vmem_limit_bytes=VMEM_LIMIT),
    )(dyc, dyc, proj, proj, conv_w, dproj)


def _dh_dx(dproj, w_in_all, x2, dxd, mod3, chip_sums, hops=(), parts0=None):
    tm = 1024
    per_seq = S // tm
    n = len(chip_sums)
    n_in = 5 + n + (0 if parts0 is None else 1)

    def body(*refs):
        d_ref, w_ref, x_ref, dxd_ref, mod_ref = refs[:5]
        ins = refs[5:5 + n]
        gx_ref, vec_ref = refs[n_in:n_in + 2]
        outs = refs[n_in + 2:n_in + 2 + n]
        acc, send_sems, recv_sems, local_sems = refs[n_in + 2 + n:]
        i, jj = pl.program_id(0), pl.program_id(1)

        @pl.when((i == 0) & (jj == 0))
        def _():
            vec_ref[...] = jnp.zeros_like(vec_ref)
            if n:
                sends, _, mine = _chip_copies(ins, outs, send_sems, recv_sems, local_sems, hops)
                for cp in sends + mine:
                    cp.start()

        if n:
            @pl.when((i == T // tm - 1) & (jj == N_DEV - 1))
            def _():
                sends, arrivals, mine = _chip_copies(ins, outs, send_sems, recv_sems, local_sems, hops)
                for cp in arrivals:
                    cp.wait_recv()
                for cp in sends:
                    cp.wait_send()
                for cp in mine:
                    cp.wait()

        @pl.when(jj == 0)
        def _():
            acc[...] = jnp.zeros_like(acc)

        acc[...] += _dot_nt(d_ref[...], w_ref[...])

        @pl.when(jj == N_DEV - 1)
        def _():
            dh = acc[...]
            bidx = i // per_seq
            gx_ref[...] = dxd_ref[...] + dh * (1.0 + mod_ref[0, 1:2, :])
            dshift = jnp.sum(dh, axis=0, keepdims=True)
            dscale = jnp.sum(dh * x_ref[...], axis=0, keepdims=True)
            vec_ref[0:1, :] += jnp.where(bidx == 0, dshift, 0.0)
            vec_ref[1:2, :] += jnp.where(bidx == 1, dshift, 0.0)
            vec_ref[2:3, :] += jnp.where(bidx == 0, dscale, 0.0)
            vec_ref[3:4, :] += jnp.where(bidx == 1, dscale, 0.0)

    any_spec = pl.BlockSpec(memory_space=pl.ANY)
    res = pl.pallas_call(
        body, name="dh_dx",
        grid=(T // tm, N_DEV),
        in_specs=[
            pl.BlockSpec((tm, SHARD), lambda i, jj: (i, jj)),
            pl.BlockSpec((None, D, SHARD), lambda i, jj: (jj, 0, 0)),
            pl.BlockSpec((tm, D), lambda i, jj: (i, 0)),
            pl.BlockSpec((tm, D), lambda i, jj: (i, 0)),
            pl.BlockSpec((1, 3, D), lambda i, jj: (i // per_seq, 0, 0))] + [any_spec] * (n_in - 5),
        out_specs=(pl.BlockSpec((tm, D), lambda i, jj: (i, 0)),
                   pl.BlockSpec((8, D), lambda i, jj: (0, 0))) + (any_spec,) * n,
        out_shape=(jax.ShapeDtypeStruct((T, D), F32), jax.ShapeDtypeStruct((8, D), F32))
                  + tuple(jax.ShapeDtypeStruct(a.shape, a.dtype) for a in chip_sums),
        scratch_shapes=[pltpu.VMEM((tm, D), F32), pltpu.SemaphoreType.DMA((max(3 * n, 1),)),
                        pltpu.SemaphoreType.DMA((max(3 * n, 1),)), pltpu.SemaphoreType.DMA((max(n, 1),))],
        input_output_aliases={} if parts0 is None else {5 + n: 2},
        compiler_params=pltpu.CompilerParams(vmem_limit_bytes=VMEM_LIMIT),
    )(dproj, w_in_all, x2, dxd, mod3, *chip_sums, *([] if parts0 is None else [parts0]))
    return res[0], res[1], res[2:]


def _adam_step(g, w, m, v):
    nm = ADAM_B1 * m + (1.0 - ADAM_B1) * g
    nv = ADAM_B2 * v + (1.0 - ADAM_B2) * (g * g)
    m_hat = nm / (1.0 - ADAM_B1 ** ADAM_STEP)
    v_hat = nv / (1.0 - ADAM_B2 ** ADAM_STEP)
    return -ADAM_LR * (m_hat / (jnp.sqrt(v_hat) + ADAM_EPS) + ADAM_WD * w), nm, nv


def _adamw(parts, w, m, v, name, row_tile=None):
    n_parts, rows, cols = parts.shape
    tr = rows if row_tile is None else row_tile

    def body(p_ref, w_ref, m_ref, v_ref, g_ref, d_ref, nm_ref, nv_ref):
        g = p_ref[0].astype(F32)
        for s in range(1, n_parts):
            g = g + p_ref[s].astype(F32)
        g_ref[...] = g
        d_ref[...], nm_ref[...], nv_ref[...] = _adam_step(g, w_ref[...], m_ref[...], v_ref[...])

    blk = pl.BlockSpec((tr, cols), lambda i: (i, 0))
    shp = jax.ShapeDtypeStruct((rows, cols), F32)
    return pl.pallas_call(
        body, name=name,
        grid=(rows // tr,),
        in_specs=[pl.BlockSpec((n_parts, tr, cols), lambda i: (0, i, 0)), blk, blk, blk],
        out_specs=(blk, blk, blk, blk),
        out_shape=(shp, shp, shp, shp),
        compiler_params=pltpu.CompilerParams(vmem_limit_bytes=VMEM_LIMIT),
    )(parts, w, m, v)


def _multi_adamw(parts_list, params, name):
    n = len(params)
    flat = [t for wmv in params for t in wmv]

    def body(*refs):
        parts, ins, outs = refs[:n], refs[n:4 * n], refs[4 * n:]
        for p in range(n):
            g = parts[p][0].astype(F32)
            for s in range(1, parts[p].shape[0]):
                g = g + parts[p][s].astype(F32)
            w_ref, m_ref, v_ref = ins[3 * p:3 * p + 3]
            g_ref, d_ref, nm_ref, nv_ref = outs[4 * p:4 * p + 4]
            g_ref[...] = g
            d_ref[...], nm_ref[...], nv_ref[...] = _adam_step(g, w_ref[...], m_ref[...], v_ref[...])

    out_shape = []
    for w, _, _ in params:
        out_shape += [jax.ShapeDtypeStruct(w.shape, F32)] * 4
    res = pl.pallas_call(body, name=name, out_shape=tuple(out_shape))(*parts_list, *flat)
    return [res[4 * p:4 * p + 4] for p in range(n)]


def _small_updates(small_g, dmod_all, rel_parts, params):
    flat = [t for wmv in params for t in wmv]

    def body(sg_ref, dm_ref, rp_ref, *refs):
        ins, outs = refs[:len(flat)], refs[len(flat):]

        def over_devices(row):
            tot = sg_ref[0, row:row + 1, :]
            for s in range(1, N_DEV):
                tot = tot + sg_ref[s, row:row + 1, :]
            return tot

        g_b_ada = dm_ref[0:1, :]
        for r in range(1, N_DEV * BL):
            g_b_ada = g_b_ada + dm_ref[r:r + 1, :]
        g_rel = rp_ref[0]
        for s in range(1, N_DEV):
            g_rel = g_rel + rp_ref[s]
        grads = [g_b_ada, over_devices(0), g_rel, over_devices(1), over_devices(2)]
        outs[0][...] = jnp.sum(over_devices(3), axis=1, keepdims=True)
        for p, g in enumerate(grads):
            w_ref, m_ref, v_ref = ins[3 * p:3 * p + 3]
            g_ref, d_ref, nm_ref, nv_ref = outs[1 + 4 * p:5 + 4 * p]
            g_ref[...] = g
            d_ref[...], nm_ref[...], nv_ref[...] = _adam_step(g, w_ref[...], m_ref[...], v_ref[...])

    out_shape = [jax.ShapeDtypeStruct((1, 1), F32)]
    for w, _, _ in params:
        out_shape += [jax.ShapeDtypeStruct(w.shape, F32)] * 4
    res = pl.pallas_call(body, name="small_updates", out_shape=tuple(out_shape))(small_g, dmod_all, rel_parts, *flat)
    return res[0], [res[1 + 4 * p:5 + 4 * p] for p in range(len(params))]


def _attn_fwd_dense(proj, bias):
    nq = 4
    rows = nq * QB
    nsb = S // rows

    def body(q_ref, k_ref, v_ref, kp_ref, vp_ref, b_ref, o_ref, l_ref, ls0, ls1, ls2, ls3):
        ls = [ls0, ls1, ls2, ls3]
        n = pl.program_id(1)
        lane = lax.broadcasted_iota(jnp.int32, (QB, 128), 1)
        units = [(h, j) for h in range(4) for j in range(nq)]

        def keys(cur_ref, prev_ref, h, j):
            sl = slice(h * HD, (h + 1) * HD)
            if j == 0:
                return jnp.concatenate([prev_ref[:, sl], cur_ref[0:QB, sl]], axis=0)
            return cur_ref[(j - 1) * QB:(j + 1) * QB, sl]

        q = jnp.stack([q_ref[j * QB:(j + 1) * QB, h * HD:(h + 1) * HD] for h, j in units])
        k = jnp.stack([keys(k_ref, kp_ref, h, j) for h, j in units])
        v = jnp.stack([keys(v_ref, vp_ref, h, j) for h, j in units])
        bias_b = jnp.stack([b_ref[jnp.minimum(n, 1), h] if j == 0 else b_ref[1, h] for h, j in units])
        s = jnp.einsum("uqd,ukd->uqk", q, k, preferred_element_type=F32) * SCALE + bias_b
        m = jnp.max(s, axis=-1, keepdims=True)
        p = jnp.exp(s - m)
        l = jnp.sum(p, axis=-1, keepdims=True)
        o = jnp.einsum("uqk,ukd->uqd", p.astype(BF16), v, preferred_element_type=F32) / l
        lse = m + jnp.log(l)
        for i, (h, j) in enumerate(units):
            o_ref[j * QB:(j + 1) * QB, h * HD:(h + 1) * HD] = o[i]
            ls[h][j * QB:(j + 1) * QB, :] = jnp.where(lane == h, lse[i], 0.0)
        l_ref[...] = (ls[0][...] + ls[1][...]) + (ls[2][...] + ls[3][...])

    def row(b, n):
        return b * nsb + n

    def prev(b, n):
        return jnp.maximum((b * nsb + n) * nq - 1, 0)

    in_specs = [
        pl.BlockSpec((rows, GW), lambda b, n: (row(b, n), CB_Q)),
        pl.BlockSpec((rows, GW), lambda b, n: (row(b, n), CB_K)),
        pl.BlockSpec((rows, GW), lambda b, n: (row(b, n), CB_V)),
        pl.BlockSpec((QB, GW), lambda b, n: (prev(b, n), CB_K)),
        pl.BlockSpec((QB, GW), lambda b, n: (prev(b, n), CB_V)),
        pl.BlockSpec((None, 2, 4, QB, 2 * QB), lambda b, n: (0, 0, 0, 0, 0)),
    ]
    return pl.pallas_call(
        body, name="attn_fwd0",
        grid=(BL, nsb),
        in_specs=in_specs,
        out_specs=(pl.BlockSpec((rows, GW), lambda b, n: (row(b, n), 0)),
                   pl.BlockSpec((rows, 128), lambda b, n: (row(b, n), 0))),
        out_shape=(jax.ShapeDtypeStruct((T, GW), F32), jax.ShapeDtypeStruct((T, 128), F32)),
        scratch_shapes=[pltpu.VMEM((rows, 128), F32)] * 4,
        compiler_params=pltpu.CompilerParams(vmem_limit_bytes=VMEM_LIMIT),
    )(proj, proj, proj, proj, proj, bias)


def _attn_bwd_dense(proj, d_out, stats, bias, dproj):
    nq = 4
    rows = nq * QB
    nsb = S // rows
    cols_q, cols_k, cols_v = CB * CB_Q, CB * CB_K, CB * CB_V

    def body(q_ref, k_ref, v_ref, do_ref, st_ref, kp_ref, vp_ref, b_ref, _, out_ref, db_ref,
             sq, sk, sv, carry, sems):
        b, n = pl.program_id(0), pl.program_id(1)
        units = [(h, j) for h in range(4) for j in range(nq)]

        @pl.when((b == 0) & (n == 0))
        def _():
            db_ref[...] = jnp.zeros_like(db_ref)

        @pl.when(n == 0)
        def _():
            carry[...] = jnp.zeros_like(carry)

        def write(first_block, position, count):
            row0 = pl.multiple_of(first_block * QB, QB)
            part = pl.ds(position * QB, count * QB)
            _write_columns([(sq.at[part], cols_q), (sk.at[part], cols_k), (sv.at[part], cols_v)],
                           out_ref, row0, sems)

        @pl.when(n == nsb)
        def _():
            sq[0:QB, :] = carry[:, 0:GW].astype(BF16)
            sk[0:QB, :] = carry[:, GW:2 * GW].astype(BF16)
            sv[0:QB, :] = carry[:, 2 * GW:3 * GW].astype(BF16)
            write((b + 1) * nsb * nq - 1, 0, 1)

        @pl.when(n < nsb)
        def _():
            def keys(cur_ref, prev_ref, h, j):
                sl = slice(h * HD, (h + 1) * HD)
                if j == 0:
                    return jnp.concatenate([prev_ref[:, sl], cur_ref[0:QB, sl]], axis=0)
                return cur_ref[(j - 1) * QB:(j + 1) * QB, sl]

            def block(ref, h, j):
                return ref[j * QB:(j + 1) * QB, h * HD:(h + 1) * HD]

            q = jnp.stack([block(q_ref, h, j) for h, j in units])
            do = jnp.stack([block(do_ref, h, j) for h, j in units])
            k = jnp.stack([keys(k_ref, kp_ref, h, j) for h, j in units])
            v = jnp.stack([keys(v_ref, vp_ref, h, j) for h, j in units])
            bias_b = jnp.stack([b_ref[jnp.minimum(n, 1), h] if j == 0 else b_ref[1, h] for h, j in units])
            lse = jnp.stack([st_ref[j * QB:(j + 1) * QB, h:h + 1] for h, j in units])
            delta = jnp.stack([st_ref[j * QB:(j + 1) * QB, 4 + h:5 + h] for h, j in units])
            s = jnp.einsum("uqd,ukd->uqk", q, k, preferred_element_type=F32) * SCALE + bias_b
            p = jnp.exp(s - lse)
            ds = p * (jnp.einsum("uqd,ukd->uqk", do, v, preferred_element_type=F32) - delta)
            for h in range(4):
                tot = ds[h * nq]
                for j in range(1, nq):
                    tot = tot + ds[h * nq + j]
                db_ref[h] += tot
            dsb, pb = ds.astype(BF16), p.astype(BF16)
            dq = jnp.einsum("uqk,ukd->uqd", dsb, k, preferred_element_type=F32) * SCALE
            dk = jnp.einsum("uqk,uqd->ukd", dsb, q, preferred_element_type=F32) * SCALE
            dv = jnp.einsum("uqk,uqd->ukd", pb, do, preferred_element_type=F32)
            for h in range(4):
                sl = slice(h * HD, (h + 1) * HD)
                u0, last = h * nq, h * nq + nq - 1
                sq[0:QB, sl] = carry[:, sl].astype(BF16)
                sk[0:QB, sl] = (carry[:, GW + h * HD:GW + (h + 1) * HD] + dk[u0, :QB]).astype(BF16)
                sv[0:QB, sl] = (carry[:, 2 * GW + h * HD:2 * GW + (h + 1) * HD] + dv[u0, :QB]).astype(BF16)
                for j in range(nq - 1):
                    pos = slice((j + 1) * QB, (j + 2) * QB)
                    sq[pos, sl] = dq[u0 + j].astype(BF16)
                    sk[pos, sl] = (dk[u0 + j, QB:] + dk[u0 + j + 1, :QB]).astype(BF16)
                    sv[pos, sl] = (dv[u0 + j, QB:] + dv[u0 + j + 1, :QB]).astype(BF16)
                carry[:, sl] = dq[last]
                carry[:, GW + h * HD:GW + (h + 1) * HD] = dk[last, QB:]
                carry[:, 2 * GW + h * HD:2 * GW + (h + 1) * HD] = dv[last, QB:]

            @pl.when(n == 0)
            def _():
                write(b * nsb * nq, 1, nq - 1)

            @pl.when(n > 0)
            def _():
                write((b * nsb + n) * nq - 1, 0, nq)

    def row(b, n):
        return b * nsb + jnp.minimum(n, nsb - 1)

    def prev(b, n):
        return jnp.maximum(row(b, n) * nq - 1, 0)

    in_specs = [
        pl.BlockSpec((rows, GW), lambda b, n: (row(b, n), CB_Q)),
        pl.BlockSpec((rows, GW), lambda b, n: (row(b, n), CB_K)),
        pl.BlockSpec((rows, GW), lambda b, n: (row(b, n), CB_V)),
        pl.BlockSpec((rows, GW), lambda b, n: (row(b, n), 0)),
        pl.BlockSpec((rows, 128), lambda b, n: (row(b, n), 0)),
        pl.BlockSpec((QB, GW), lambda b, n: (prev(b, n), CB_K)),
        pl.BlockSpec((QB, GW), lambda b, n: (prev(b, n), CB_V)),
        pl.BlockSpec((None, 2, 4, QB, 2 * QB), lambda b, n: (0, 0, 0, 0, 0)),
        pl.BlockSpec(memory_space=pl.ANY),
    ]
    return pl.pallas_call(
        body, name="attn_bwd0",
        grid=(BL, nsb + 1),
        in_specs=in_specs,
        out_specs=(pl.BlockSpec(memory_space=pl.ANY),
                   pl.BlockSpec((4, QB, 2 * QB), lambda b, n: (0, 0, 0))),
        out_shape=(jax.ShapeDtypeStruct((T, NCOL), BF16),
                   jax.ShapeDtypeStruct((4, QB, 2 * QB), F32)),
        scratch_shapes=[pltpu.VMEM((rows, GW), BF16)] * 3
                       + [pltpu.VMEM((QB, 3 * GW), F32), pltpu.SemaphoreType.DMA((3,))],
        input_output_aliases={8: 0},
        compiler_params=pltpu.CompilerParams(vmem_limit_bytes=VMEM_LIMIT),
    )(proj, proj, proj, d_out, stats, proj, proj, bias, dproj)


def _attention_forward(proj, rel_bias):
    expand_lanes, grad_lanes, masks = (jnp.asarray(t) for t in _bucket_maps())
    bias = _bias_expand(rel_bias, expand_lanes, masks)
    fwd = [_attn_fwd_dense(proj, bias)] + [_attn_fwd(proj, bias, g) for g in (1, 2)]
    return bias, grad_lanes, [f[0] for f in fwd], [f[1] for f in fwd]


def _local_step(x2, tgt2, mod3, h, proj, attn, w_ao, w_co, w_o, conv_w, conv_b, ln_g, ln_b):
    bias, buckets, o_g, lse_g = attn

    (dproj, dyc, d_o, stats, dxd, gw_o, gw_co, gw_ao, tail_vec) = _tail(
        x2, tgt2, mod3, o_g, lse_g, proj, w_ao, w_co, w_o, conv_w, conv_b, ln_g, ln_b)

    dproj, db = _attn_bwd_dense(proj, d_o, stats, bias, dproj)
    dbias = [db]
    for g in (1, 2):
        dproj, db = _attn_bwd(proj, d_o, stats, bias, dproj, g)
        dbias.append(db)
    g_rel_bias = _bias_grad(*dbias, buckets)
    dproj, conv_vec = _conv_bwd(dyc, proj, conv_w, dproj)

    gw_ao = jnp.transpose(gw_ao.reshape(GW, N_DEV, D // N_DEV), (1, 0, 2))
    return dproj, dxd, gw_ao, gw_co, gw_o, conv_vec, g_rel_bias, tail_vec


def kernel(x, c, w_ada, b_ada, w_in, conv_w, conv_b, rel_bias, w_attn_out, w_conv_out, w_o, ln_g, ln_b, loss_target, m_w_ada, m_b_ada, m_w_in, m_conv_w, m_conv_b, m_rel_bias, m_w_attn_out, m_w_conv_out, m_w_o, m_ln_g, m_ln_b, v_w_ada, v_b_ada, v_w_in, v_conv_w, v_conv_b, v_rel_bias, v_w_attn_out, v_w_conv_out, v_w_o, v_ln_g, v_ln_b):
    me = _my_index()
    x2 = x.reshape(T, D)
    tgt2 = loss_target.reshape(T, D)

    b_cols = lax.dynamic_slice(b_ada, (0, me * ADA_SHARD), (1, ADA_SHARD))
    c_g, mod_in = _mod_exchange(jnp.pad(c, ((0, 8 - BL), (0, 0))), w_ada[0], b_cols)
    c_all = c_g[:, 0:BL, :].reshape(N_DEV * BL, D)
    mod3 = jnp.transpose(mod_in[:, 0:BL, :], (1, 0, 2)).reshape(BL, 3, D)

    h = _prep_h(x2, mod3)
    rows_shape = jax.ShapeDtypeStruct((N_DEV, D // N_DEV, D), BF16)
    proj, w_in_all, (w_ao_g, w_co_g, w_o_g, conv_w_g) = _gather_proj(
        _shard_order(), h, w_in[0].astype(BF16), 1024,
        ([w_attn_out[0].astype(BF16), w_conv_out[0].astype(BF16), w_o[0].astype(BF16), conv_w[0]],
         [jax.ShapeDtypeStruct((N_DEV, GW, D // N_DEV), BF16), rows_shape, rows_shape,
          jax.ShapeDtypeStruct((N_DEV, 3, D // N_DEV), F32)]))

    attn = _attention_forward(proj, rel_bias)
    w_ao_full = jnp.transpose(w_ao_g, (1, 0, 2)).reshape(GW, D)
    w_co_full = w_co_g.reshape(D, D)
    w_o_full = w_o_g.reshape(D, D)
    conv_w_full = jnp.transpose(conv_w_g, (1, 0, 2)).reshape(3, D)

    (dproj, dxd, gw_ao, gw_co, gw_o, conv_vec, g_rel_bias, tail_vec) = _local_step(
        x2, tgt2, mod3, h, proj, attn, w_ao_full, w_co_full, w_o_full,
        conv_w_full, conv_b, ln_g, ln_b)

    g_conv_w_blocks = jnp.transpose(conv_vec[0:3].reshape(3, N_DEV, D // N_DEV), (1, 0, 2))
    partials = [gw_ao, gw_co.reshape(N_DEV, D // N_DEV, D), gw_o.reshape(N_DEV, D // N_DEV, D), g_conv_w_blocks]
    w_in_sums, w_in_parts, sib = _gw_in_pair(
        _slice_order(), h, dproj, partials,
        [jax.ShapeDtypeStruct((4, GW, D // N_DEV), BF16),
         jax.ShapeDtypeStruct((4, D // N_DEV, D), BF16),
         jax.ShapeDtypeStruct((4, D // N_DEV, D), BF16),
         jax.ShapeDtypeStruct((4, 3, D // N_DEV), F32)])
    core = lax.axis_index("c").astype(jnp.int32).reshape(1)
    chip_sums = [w_in_sums] + list(_pair_add(core, partials, sib))
    hops = [(3,)] + [(1, 2, 3)] * 4
    grad_x, mod_vec, (r_in, r_ao, r_co, r_o, r_cw) = _dh_dx(
        dproj, w_in_all, x2, dxd, mod3, chip_sums, hops, w_in_parts)

    small = jnp.concatenate([
        tail_vec[0:4],
        jnp.pad(g_rel_bias.reshape(1, N_BUCKETS * N_HEADS), ((0, 0), (0, D - N_BUCKETS * N_HEADS))),
        jnp.zeros((3, D), F32)], axis=0)
    dmod = jnp.concatenate([mod_vec[0:2], mod_vec[2:4], tail_vec[4:6]], axis=1)
    small_g, dmod_g = _all_gather(
        [small, dmod],
        [jax.ShapeDtypeStruct((N_DEV, 8, D), F32), jax.ShapeDtypeStruct((N_DEV, BL, 3 * D), F32)],
        "gather_small")
    dmod_all = dmod_g.reshape(N_DEV * BL, 3 * D)
    small_names = ["b_ada", "conv_b", "rel_bias", "ln_g", "ln_b"]
    small_params = [(b_ada, m_b_ada, v_b_ada), (conv_b, m_conv_b, v_conv_b), (rel_bias, m_rel_bias, v_rel_bias),
                    (ln_g, m_ln_g, v_ln_g), (ln_b, m_ln_b, v_ln_b)]
    loss, small_res = _small_updates(
        small_g, dmod_all, small_g[:, 4, :N_BUCKETS * N_HEADS].reshape(N_DEV, N_BUCKETS, N_HEADS), small_params)
    loss = loss.reshape(())
    g_w_ada = _ada_bwd(jnp.transpose(c_all), lax.dynamic_slice(dmod_all, (0, me * ADA_SHARD),
                                                               (N_DEV * BL, ADA_SHARD)))

    def upd(parts, w, m, v, name, row_tile=None):
        shape = w.shape
        w2, m2, v2 = (t.reshape(parts.shape[1:]) for t in (w, m, v))
        return tuple(t.reshape(shape) for t in _adamw(parts, w2, m2, v2, name, row_tile))

    res = {
        "w_ada": upd(g_w_ada[None], w_ada, m_w_ada, v_w_ada, "adam_w_ada", 256),
        "w_in": upd(r_in, w_in, m_w_in, v_w_in, "adam_w_in", 128),
    }
    mid_names = ["conv_w", "w_attn_out", "w_conv_out", "w_o"]
    mid_parts = [r_cw, r_ao, r_co, r_o]
    mid_full = [(conv_w, m_conv_w, v_conv_w), (w_attn_out, m_w_attn_out, v_w_attn_out),
                (w_conv_out, m_w_conv_out, v_w_conv_out), (w_o, m_w_o, v_w_o)]
    mid_res = _multi_adamw(mid_parts, [tuple(t[0] for t in wmv) for wmv in mid_full], "adam_mid")
    for nm, wmv, outs4 in zip(mid_names, mid_full, mid_res):
        res[nm] = tuple(t[None] for t in outs4)
    res.update(dict(zip(small_names, small_res)))
    order = ["w_ada", "b_ada", "w_in", "conv_w", "conv_b", "rel_bias", "w_attn_out", "w_conv_out",
             "w_o", "ln_g", "ln_b"]
    outs = [loss, grad_x.reshape(BL, S, D)]
    for k in range(4):
        outs += [res[name][k] for name in order]
    return tuple(outs)
```

```python
import math

import numpy as np
import jax
import jax.numpy as jnp
from jax import lax
from jax.experimental import pallas as pl
from jax.experimental.pallas import tpu as pltpu

F32 = jnp.float32
BF16 = jnp.bfloat16
MESH = pl.DeviceIdType.MESH

N_DEV = 8
D = 1024
S = 2048
BL = 2
T = BL * S
NCOL = 11264
SHARD = NCOL // N_DEV
CB = 512
NCB = NCOL // CB
HD = 128
GW = 512
QB = 128
DILATIONS = (1, 4, 16)
N_STEPS = 128
N_BUCKETS = 32
N_HEADS = 12
ALPHA = 2.0 ** 0.25
LN_EPS = 1e-5
NEG_INF = -1e30
SCALE = HD ** -0.5
ADA_SHARD = 3 * D // N_DEV

CB_Q, CB_K, CB_V, CB_GA = 0, 3, 6, 9
KB_U, KB_BG, KB_CG, KB_GC, KB_MA, KB_MC = 5, 6, 7, 8, 9, 10

ADAM_LR, ADAM_B1, ADAM_B2, ADAM_EPS, ADAM_WD, ADAM_STEP = 0.001, 0.9, 0.999, 1e-08, 0.01, 10

VMEM_LIMIT = 56 * 1024 * 1024
VMEM_LIMIT_TAIL = 62 * 1024 * 1024


def _dot(a, b):
    return jnp.dot(a, b, preferred_element_type=F32)


def _dot_nt(a, b):
    return lax.dot_general(a, b, (((1,), (1,)), ((), ())), preferred_element_type=F32)


def _dot_tn(a, b):
    return lax.dot_general(a, b, (((0,), (0,)), ((), ())), preferred_element_type=F32)


def _sigmoid(v):
    return 1.0 / (1.0 + jnp.exp(-v))


def _write_columns(pieces, dst_hbm, row0, sems):
    copies = []
    for k, (src, col0) in enumerate(pieces):
        rows, width = src.shape
        copies.append(pltpu.make_async_copy(
            src, dst_hbm.at[pl.ds(row0, rows), pl.ds(col0, width)], sems.at[k]))
    for cp in copies:
        cp.start()
    for cp in copies:
        cp.wait()


def _my_index():
    return 4 * lax.axis_index("x") + 2 * lax.axis_index("y") + lax.axis_index("c")


class _Gather:
    def __init__(self, ins, outs, stage, send_sems, recv_sems, local_sems):
        self.ins, self.outs, self.stage = ins, outs, stage
        self.send_sems, self.recv_sems, self.local_sems = send_sems, recv_sems, local_sems
        x, y, c = lax.axis_index("x"), lax.axis_index("y"), lax.axis_index("c")
        self.c = c
        self.me, self.sibling = (x, y, c), (x, y, 1 - c)
        self.chips = [(1 - x, y), (x, 1 - y), (1 - x, 1 - y)]

    @staticmethod
    def scratch(arrs):
        n = len(arrs)
        return ([pltpu.SemaphoreType.DMA((7 * n,)), pltpu.SemaphoreType.DMA((7 * n,)),
                 pltpu.SemaphoreType.DMA((n,))] + [pltpu.VMEM(a.shape, a.dtype) for a in arrs])

    def _copy(self, a, k, block, to, src=None):
        dst = self.outs[a].at[4 * block[0] + 2 * block[1] + block[2]]
        return pltpu.make_async_remote_copy(
            src_ref=dst if src is None else src, dst_ref=dst,
            send_sem=self.send_sems.at[a * 7 + k], recv_sem=self.recv_sems.at[a * 7 + k],
            device_id=to, device_id_type=MESH)

    def _first(self):
        first = []
        for a in range(len(self.ins)):
            first.append(self._copy(a, 0, self.me, self.sibling, src=self.ins[a]))
            first += [self._copy(a, 1 + j, self.me, (*chip, self.c), src=self.ins[a])
                      for j, chip in enumerate(self.chips)]
        return first

    def _mine(self):
        me = self.me
        return [pltpu.make_async_copy(self.stage[a], self.outs[a].at[4 * me[0] + 2 * me[1] + me[2]],
                                      self.local_sems.at[a]) for a in range(len(self.ins))]

    def begin(self):
        for cp in self._first():
            cp.start()
        loads = [pltpu.make_async_copy(self.ins[a], self.stage[a], self.local_sems.at[a])
                 for a in range(len(self.ins))]
        for cp in loads:
            cp.start()
        for cp in loads:
            cp.wait()
        for cp in self._mine():
            cp.start()

    def finish(self):
        n, c, me, sibling = len(self.ins), self.c, self.me, self.sibling
        passed = []
        for j, chip in enumerate(self.chips):
            for a in range(n):
                self._copy(a, 1 + j, (*chip, c), me).wait_recv()
                fwd = self._copy(a, 4 + j, (*chip, c), sibling)
                fwd.start()
                passed.append(fwd)
        for a in range(n):
            self._copy(a, 0, sibling, me).wait_recv()
        for j, chip in enumerate(self.chips):
            for a in range(n):
                self._copy(a, 4 + j, (*chip, 1 - c), me).wait_recv()
        for cp in self._first() + passed:
            cp.wait_send()
        for cp in self._mine():
            cp.wait()


def _all_gather(arrs, out_shapes, name):
    n = len(arrs)

    def body(*refs):
        g = _Gather(refs[:n], refs[n:2 * n], refs[2 * n + 3:], *refs[2 * n:2 * n + 3])
        g.begin()
        g.finish()

    any_spec = pl.BlockSpec(memory_space=pl.ANY)
    return pl.pallas_call(
        body, name=name,
        out_shape=tuple(out_shapes),
        in_specs=[any_spec] * n,
        out_specs=tuple([any_spec] * n),
        scratch_shapes=_Gather.scratch(arrs),
    )(*arrs)


def _slice_order():
    x, y, c = lax.axis_index("x"), lax.axis_index("y"), lax.axis_index("c")
    slots = []
    for q in (2 * (1 - x) + y, 2 * x + (1 - y), 2 * (1 - x) + (1 - y), 2 * x + y):
        slots += [2 * q + 1 - c, 2 * q + c]
    return jnp.stack(slots).astype(jnp.int32)


def _gw_in_pair(order, h, dproj, smalls, small_shapes4):
    kk, m = h.shape
    tk = min(kk, 2048)
    nk = kk // tk
    ncols = dproj.shape[1] // N_DEV
    n = len(smalls)

    def body(order_ref, h_ref, d_ref, *rest):
        ins = rest[:n]
        sums_hbm, parts_hbm = rest[n], rest[n + 1]
        sib = rest[n + 2:2 * n + 2]
        (acc, sendbuf, recvbuf, sumbuf, send_sems, recv_sems, local_sem, ssend, srecv,
         isend, irecv) = rest[2 * n + 2:]
        js, k = pl.program_id(0), pl.program_id(1)
        x, y, c = lax.axis_index("x"), lax.axis_index("y"), lax.axis_index("c")
        sibling = (x, y, 1 - c)
        my_chip = 2 * x + y
        near = [(1 - x, y, c), (x, 1 - y, c)]

        def ici_copy(p, out_chip):
            peer = near[p]
            return pltpu.make_async_remote_copy(
                src_ref=sumbuf.at[p], dst_ref=parts_hbm.at[out_chip],
                send_sem=isend.at[p], recv_sem=irecv.at[p], device_id=peer, device_id_type=MESH)

        def small_copies():
            return [pltpu.make_async_remote_copy(
                        src_ref=ins[a].at[2 * q + 1 - c], dst_ref=sib[a].at[q],
                        send_sem=ssend.at[a * 4 + q], recv_sem=srecv.at[a * 4 + q],
                        device_id=sibling, device_id_type=MESH)
                    for a in range(n) for q in range(4)]

        def slice_copy(p):
            return pltpu.make_async_remote_copy(
                src_ref=sendbuf, dst_ref=recvbuf.at[p], send_sem=send_sems.at[p], recv_sem=recv_sems.at[p],
                device_id=sibling, device_id_type=MESH)

        def sum_copy(p):
            return pltpu.make_async_copy(sumbuf.at[2], sums_hbm.at[order_ref[2 * p] // 2], local_sem)

        @pl.when((js == 0) & (k == 0))
        def _():
            for cp in small_copies():
                cp.start()

        @pl.when(k == 0)
        def _():
            acc[...] = jnp.zeros_like(acc)

        acc[...] += _dot_tn(h_ref[...], d_ref[...])

        for p in range(4):
            @pl.when((js == 2 * p) & (k == nk - 1))
            def _():
                if p > 0:
                    slice_copy(p - 1).wait_send()
                sendbuf[...] = acc[...].astype(BF16)
                slice_copy(p).start()

            @pl.when((js == 2 * p + 1) & (k == nk - 1))
            def _():
                slice_copy(p).wait_recv()
                if p == 3:
                    sum_copy(2).wait()
                sumbuf[min(p, 2)] = (acc[...] + recvbuf[p].astype(F32)).astype(BF16)
                if p < 2:
                    ici_copy(p, my_chip).start()
                else:
                    sum_copy(p).start()

        @pl.when((js == N_DEV - 1) & (k == nk - 1))
        def _():
            slice_copy(3).wait_send()
            sum_copy(3).wait()
            for cp in small_copies():
                cp.wait()
            for p in range(2):
                ici_copy(p, 2 * near[p][0] + near[p][1]).wait_recv()
                ici_copy(p, my_chip).wait_send()

    any_spec = pl.BlockSpec(memory_space=pl.ANY)
    res = pl.pallas_call(
        body, name="gw_in_pair",
        grid_spec=pltpu.PrefetchScalarGridSpec(
            num_scalar_prefetch=1,
            grid=(N_DEV, nk),
            in_specs=[pl.BlockSpec((tk, m), lambda js, k, order_ref: (k, 0)),
                      pl.BlockSpec((tk, ncols), lambda js, k, order_ref: (k, order_ref[js]))] + [any_spec] * n,
            out_specs=(any_spec,) * (n + 2),
            scratch_shapes=[pltpu.VMEM((m, ncols), F32), pltpu.VMEM((m, ncols), BF16),
                            pltpu.VMEM((4, m, ncols), BF16), pltpu.VMEM((3, m, ncols), BF16),
                            pltpu.SemaphoreType.DMA((4,)), pltpu.SemaphoreType.DMA((4,)),
                            pltpu.SemaphoreType.DMA,
                            pltpu.SemaphoreType.DMA((4 * n,)), pltpu.SemaphoreType.DMA((4 * n,)),
                            pltpu.SemaphoreType.DMA((2,)), pltpu.SemaphoreType.DMA((2,))]),
        out_shape=(jax.ShapeDtypeStruct((4, m, ncols), BF16),) * 2 + tuple(small_shapes4),
        compiler_params=pltpu.CompilerParams(vmem_limit_bytes=VMEM_LIMIT),
    )(order, h, dproj, *smalls)
    return res[0], res[1], res[2:]


def _chip_copies(ins, outs, send_sems, recv_sems, local_sems, hops):
    n = len(ins)
    x, y, c = lax.axis_index("x"), lax.axis_index("y"), lax.axis_index("c")
    my_chip = 2 * x + y

    def peer_of(k):
        return ((1 - x) if (k >> 1) & 1 else x, (1 - y) if k & 1 else y, c)

    def copy(a, k, out_chip):
        peer = peer_of(k)
        return pltpu.make_async_remote_copy(
            src_ref=ins[a].at[2 * peer[0] + peer[1]], dst_ref=outs[a].at[out_chip],
            send_sem=send_sems.at[a * 3 + k - 1], recv_sem=recv_sems.at[a * 3 + k - 1],
            device_id=peer, device_id_type=MESH)

    sends = [copy(a, k, my_chip) for k in range(1, 4) for a in range(n) if k in hops[a]]
    arrivals = []
    for k in range(1, 4):
        peer = peer_of(k)
        arrivals += [copy(a, k, 2 * peer[0] + peer[1]) for a in range(n) if k in hops[a]]
    mine = [pltpu.make_async_copy(ins[a].at[my_chip], outs[a].at[my_chip], local_sems.at[a])
            for a in range(n)]
    return sends, arrivals, mine


def _pair_add(core, mines, theirs):
    n = len(mines)

    def body(core_ref, *refs):
        mine, sib, outs = refs[:n], refs[n:2 * n], refs[2 * n:]
        for a in range(n):
            for q in range(4):
                outs[a][q] = (mine[a][2 * q + core_ref[0]].astype(F32)
                              + sib[a][q].astype(F32)).astype(outs[a].dtype)

    return pl.pallas_call(
        body, name="pair_add",
        in_specs=[pl.BlockSpec(memory_space=pltpu.SMEM)] + [pl.BlockSpec(memory_space=pltpu.VMEM)] * (2 * n),
        out_shape=tuple(jax.ShapeDtypeStruct(t.shape, t.dtype) for t in theirs),
    )(core, *mines, *theirs)


def _mod_exchange(c8, w_ada, b_cols):
    cols = w_ada.shape[1]

    def body(c_ref, w_ref, b_ref, call_ref, mod_ref, msend, send1, recv1, send2, recv2):
        x, y, c = lax.axis_index("x"), lax.axis_index("y"), lax.axis_index("c")
        my_slot = 4 * x + 2 * y + c

        def peer_of(k):
            return ((1 - x) if (k >> 2) & 1 else x, (1 - y) if (k >> 1) & 1 else y, (1 - c) if k & 1 else c)

        def slot_of(dev):
            return 4 * dev[0] + 2 * dev[1] + dev[2]

        def exchange(src_of, dst_ref, send_sems, recv_sems):
            sends, arrivals = [], []
            for k in range(1, 8):
                peer = peer_of(k)
                sends.append(pltpu.make_async_remote_copy(
                    src_ref=src_of(slot_of(peer)), dst_ref=dst_ref.at[my_slot],
                    send_sem=send_sems.at[k - 1], recv_sem=recv_sems.at[k - 1],
                    device_id=peer, device_id_type=MESH))
                arrivals.append(pltpu.make_async_remote_copy(
                    src_ref=src_of(my_slot), dst_ref=dst_ref.at[slot_of(peer)],
                    send_sem=send_sems.at[k - 1], recv_sem=recv_sems.at[k - 1],
                    device_id=peer, device_id_type=MESH))
            for cp in sends:
                cp.start()
            for cp in arrivals:
                cp.wait_recv()
            for cp in sends:
                cp.wait_send()

        call_ref[my_slot] = c_ref[...]
        exchange(lambda s: c_ref, call_ref, send1, recv1)
        cv = call_ref[...].reshape(N_DEV * 8, c_ref.shape[1])
        act = cv * _sigmoid(cv)
        mod = jnp.dot(act, w_ref[...], preferred_element_type=F32,
                      precision=lax.Precision.HIGHEST) + b_ref[...]
        msend[...] = mod.reshape(N_DEV, 8, cols)
        mod_ref[my_slot] = msend[my_slot]
        exchange(lambda s: msend.at[s], mod_ref, send2, recv2)

    return pl.pallas_call(
        body, name="mod_exchange",
        out_shape=(jax.ShapeDtypeStruct((N_DEV, 8, c8.shape[1]), F32),
                   jax.ShapeDtypeStruct((N_DEV, 8, cols), F32)),
        scratch_shapes=[pltpu.VMEM((N_DEV, 8, cols), F32)] + [pltpu.SemaphoreType.DMA((7,))] * 4,
    )(c8, w_ada, b_cols)


def _w_ada_update(c_all_t, dmod_cols, w, m, v):
    rows, cols = w.shape
    tr = 256

    def body(c_ref, d_ref, w_ref, m_ref, v_ref, g_ref, dl_ref, nm_ref, nv_ref):
        cv = c_ref[...]
        g = jnp.dot(cv * _sigmoid(cv), d_ref[...], preferred_element_type=F32,
                    precision=lax.Precision.HIGHEST)
        g_ref[...] = g
        dl_ref[...], nm_ref[...], nv_ref[...] = _adam_step(g, w_ref[...], m_ref[...], v_ref[...])

    blk = pl.BlockSpec((tr, cols), lambda i: (i, 0))
    shp = jax.ShapeDtypeStruct((rows, cols), F32)
    return pl.pallas_call(
        body, name="adam_w_ada",
        grid=(rows // tr,),
        in_specs=[pl.BlockSpec((tr, c_all_t.shape[1]), lambda i: (i, 0)),
                  pl.BlockSpec(dmod_cols.shape, lambda i: (0, 0)), blk, blk, blk],
        out_specs=(blk, blk, blk, blk),
        out_shape=(shp, shp, shp, shp),
    )(c_all_t, dmod_cols, w, m, v)


def _shard_order():
    x, y, c = lax.axis_index("x"), lax.axis_index("y"), lax.axis_index("c")
    devs = [(x, y, c), (x, y, 1 - c)]
    for chip in [(1 - x, y), (x, 1 - y), (1 - x, 1 - y)]:
        devs += [(*chip, c), (*chip, 1 - c)]
    return jnp.stack([4 * d[0] + 2 * d[1] + d[2] for d in devs]).astype(jnp.int32)


def _prep_h(x2, mod3):
    ts = 512
    per_seq = S // ts

    def body(x_ref, mod_ref, h_ref):
        shift = mod_ref[0, 0:1, :]
        scale = mod_ref[0, 1:2, :]
        h_ref[...] = (x_ref[...] * (1.0 + scale) + shift).astype(BF16)

    return pl.pallas_call(
        body, name="prep_h",
        grid=(T // ts,),
        in_specs=[pl.BlockSpec((ts, D), lambda i: (i, 0)),
                  pl.BlockSpec((1, 3, D), lambda i: (i // per_seq, 0, 0))],
        out_specs=pl.BlockSpec((ts, D), lambda i: (i, 0)),
        out_shape=jax.ShapeDtypeStruct((T, D), BF16),
    )(x2, mod3)


def _gather_proj(order, h, w_shard, tm, ride=()):
    rows, kdim = h.shape
    ncols = w_shard.shape[1]
    n_i = rows // tm
    ride_arrs, ride_shapes = ride if ride else ((), ())
    n_ride = len(ride_arrs)

    def body(order_ref, h_ref, mine_hbm, *rest):
        ride_ins = rest[:n_ride]
        o_ref, all_hbm = rest[n_ride:n_ride + 2]
        ride_outs = rest[n_ride + 2:2 * n_ride + 2]
        wv, send_sems, recv_sems, local_sems = rest[2 * n_ride + 2:2 * n_ride + 6]
        ride_scr = rest[2 * n_ride + 6:]
        j, i = pl.program_id(0), pl.program_id(1)
        x, y, c = lax.axis_index("x"), lax.axis_index("y"), lax.axis_index("c")
        me, sibling = (x, y, c), (x, y, 1 - c)
        chips = [(1 - x, y), (x, 1 - y), (1 - x, 1 - y)]

        def slot(dev):
            return 4 * dev[0] + 2 * dev[1] + dev[2]

        def copy(k, block, to):
            return pltpu.make_async_remote_copy(
                src_ref=wv.at[slot(block)], dst_ref=wv.at[slot(block)],
                send_sem=send_sems.at[k], recv_sem=recv_sems.at[k],
                device_id=to, device_id_type=MESH)

        def keep(step, block):
            return pltpu.make_async_copy(wv.at[slot(block)], all_hbm.at[slot(block)], local_sems.at[step])

        if n_ride:
            gather = _Gather(ride_ins, ride_outs, ride_scr[3:], *ride_scr[:3])
        first = [copy(0, me, sibling)] + [copy(1 + q, me, (*chip, c)) for q, chip in enumerate(chips)]
        passed = [copy(4 + q, (*chip, c), sibling) for q, chip in enumerate(chips)]
        due = [(me, None, None), (sibling, copy(0, sibling, me), None)]
        for q, chip in enumerate(chips):
            due.append(((*chip, c), copy(1 + q, (*chip, c), me), passed[q]))
            due.append(((*chip, 1 - c), copy(4 + q, (*chip, 1 - c), me), None))

        @pl.when((j == 0) & (i == 0))
        def _():
            load = pltpu.make_async_copy(mine_hbm, wv.at[slot(me)], local_sems.at[N_DEV])
            load.start()
            load.wait()
            for cp in first:
                cp.start()
            keep(0, me).start()

        for step in range(1, N_DEV):
            block, arrival, forward = due[step]

            @pl.when((j == step) & (i == 0))
            def _():
                arrival.wait_recv()
                if forward is not None:
                    forward.start()
                keep(step, block).start()
                if n_ride and step == N_DEV - 2:
                    gather.begin()

        o_ref[...] = _dot(h_ref[...], wv[order_ref[j]]).astype(BF16)

        @pl.when((j == N_DEV - 1) & (i == n_i - 1))
        def _():
            for cp in first + passed:
                cp.wait_send()
            for step in range(N_DEV):
                keep(step, due[step][0]).wait()
            if n_ride:
                gather.finish()

    any_spec = pl.BlockSpec(memory_space=pl.ANY)
    res = pl.pallas_call(
        body, name="gather_proj",
        grid_spec=pltpu.PrefetchScalarGridSpec(
            num_scalar_prefetch=1,
            grid=(N_DEV, n_i),
            in_specs=[pl.BlockSpec((tm, kdim), lambda j, i, order_ref: (i, 0)), any_spec] + [any_spec] * n_ride,
            out_specs=(pl.BlockSpec((tm, ncols), lambda j, i, order_ref: (i, order_ref[j])), any_spec)
                      + (any_spec,) * n_ride,
            scratch_shapes=[pltpu.VMEM((N_DEV, kdim, ncols), BF16),
                            pltpu.SemaphoreType.DMA((7,)), pltpu.SemaphoreType.DMA((7,)),
                            pltpu.SemaphoreType.DMA((N_DEV + 1,))]
                           + (_Gather.scratch(ride_arrs) if n_ride else [])),
        out_shape=(jax.ShapeDtypeStruct((rows, N_DEV * ncols), BF16),
                   jax.ShapeDtypeStruct((N_DEV, kdim, ncols), BF16)) + tuple(ride_shapes),
        compiler_params=pltpu.CompilerParams(vmem_limit_bytes=VMEM_LIMIT),
    )(order, h, w_shard, *ride_arrs)
    return res[0], res[1], res[2:]


SKEW_W = 512


def _bucket_maps():
    lanes = np.arange(SKEW_W)

    def buckets_of(steps):
        rows = []
        for dil in DILATIONS:
            dist = np.maximum(steps, 0) * dil
            nf = np.maximum(dist, 1).astype(np.float32)
            large = 16 + (np.log(nf / np.float32(16)) / np.float32(math.log(128.0))
                          * np.float32(16)).astype(np.int32)
            large = np.minimum(large, N_BUCKETS - 1)
            bucket = np.where(dist < 16, dist, large)
            rows.append(np.where((steps >= 0) & (steps <= N_STEPS), bucket, -1).astype(np.int32))
        return np.stack(rows)[:, None, :]

    a = np.arange(QB)[:, None]
    b = np.arange(2 * QB)[None, :]
    steps = a + QB - b
    band = (steps >= 0) & (steps <= N_STEPS)
    first = band & (b >= QB)
    masks = np.stack([first, band]).astype(np.int32)
    return buckets_of(QB - lanes), buckets_of(2 * QB - 1 - lanes), masks


def _bias_expand(rel_bias, lane_buckets, masks):
    def body(tab_ref, bk_ref, mk_ref, o_ref):
        for g in range(3):
            bk = bk_ref[g]
            for h in range(4):
                col = 4 * g + h
                per_offset = jnp.zeros((1, SKEW_W), F32)
                for k in range(N_BUCKETS):
                    per_offset = jnp.where(bk == k, tab_ref[k, col], per_offset)
                tile = pltpu.roll(jnp.broadcast_to(per_offset, (QB, SKEW_W)), 0, 1, stride=1, stride_axis=0)
                tile = tile[:, :2 * QB]
                o_ref[g, 0, h] = jnp.where(mk_ref[0] != 0, tile, NEG_INF)
                o_ref[g, 1, h] = jnp.where(mk_ref[1] != 0, tile, NEG_INF)

    return pl.pallas_call(
        body, name="bias_expand",
        in_specs=[pl.BlockSpec(memory_space=pltpu.SMEM),
                  pl.BlockSpec(memory_space=pltpu.VMEM),
                  pl.BlockSpec(memory_space=pltpu.VMEM)],
        out_shape=jax.ShapeDtypeStruct((3, 2, 4, QB, 2 * QB), F32),
    )(rel_bias, lane_buckets, masks)


def _bias_grad(ds1, ds2, ds3, lane_buckets):
    exchange = jnp.asarray(np.eye(QB, dtype=np.float32)[::-1].copy())

    def body(d1_ref, d2_ref, d3_ref, bk_ref, ex_ref, o_ref):
        for g, d_ref in enumerate((d1_ref, d2_ref, d3_ref)):
            bk = bk_ref[g]
            for h in range(4):
                flipped = jnp.dot(ex_ref[...], d_ref[h], preferred_element_type=F32,
                                  precision=lax.Precision.HIGHEST)
                padded = jnp.concatenate([flipped, jnp.zeros((QB, SKEW_W - 2 * QB), F32)], axis=1)
                skewed = pltpu.roll(padded, 0, 1, stride=1, stride_axis=0)
                per_offset = jnp.sum(skewed, axis=0, keepdims=True)
                for k in range(N_BUCKETS):
                    o_ref[k, 4 * g + h] = jnp.sum(jnp.where(bk == k, per_offset, 0.0))

    return pl.pallas_call(
        body, name="bias_grad",
        in_specs=[pl.BlockSpec(memory_space=pltpu.VMEM)] * 5,
        out_specs=pl.BlockSpec(memory_space=pltpu.SMEM),
        out_shape=jax.ShapeDtypeStruct((N_BUCKETS, N_HEADS), F32),
    )(ds1, ds2, ds3, lane_buckets, exchange)


def _scratch_sets(rows):
    return 4 if rows <= 512 else 1


def _unit_chunks(dil, size=16):
    units = [(h, r) for h in range(4) for r in range(dil)]
    return [units[i:i + size] for i in range(0, len(units), size)]


def _residue_rows(src_ref, copies, h, residue):
    buf = copies[h % len(copies)]
    buf[...] = src_ref[:, h * HD:(h + 1) * HD].astype(F32)
    return lambda r: buf[residue(r), :].astype(BF16)


def _attn_fwd(proj, bias, g):
    dil = DILATIONS[g]
    rows = QB * dil
    nsb = S // rows
    has_prev = nsb > 1

    def residue(r):
        return pl.ds(r, QB, stride=dil)

    n_sets = _scratch_sets(rows)
    n_in = 6 if has_prev else 4
    n_copied = (4 + (2 if has_prev else 0)) * n_sets

    def body(*refs):
        q_ref, kc_ref, vc_ref = refs[:3]
        kp_ref, vp_ref = refs[3:5] if has_prev else (None, None)
        b_ref = refs[n_in - 1]
        o_ref, l_ref = refs[n_in:n_in + 2]
        scr = list(refs[n_in + 2:])
        ls = [scr.pop(0) for _ in range(4)]
        copies = {name: [scr.pop(0) for _ in range(n_sets)]
                  for name in ("q", "kc", "vc", "o") + (("kp", "vp") if has_prev else ())}
        lane = lax.broadcasted_iota(jnp.int32, (QB, 128), 1)
        refs_of = {"q": q_ref, "kc": kc_ref, "vc": vc_ref, "kp": kp_ref, "vp": vp_ref}
        for chunk in _unit_chunks(dil):
            rows_of = {h: {name: _residue_rows(refs_of[name], copies[name], h, residue)
                           for name in refs_of if refs_of[name] is not None}
                       for h in sorted({h for h, _ in chunk})}

            def batch(name):
                return jnp.stack([rows_of[h][name](r) for h, r in chunk])

            q, k, v = batch("q"), batch("kc"), batch("vc")
            if has_prev:
                k = jnp.concatenate([batch("kp"), k], axis=1)
                v = jnp.concatenate([batch("vp"), v], axis=1)
                bias_b = jnp.stack([b_ref[h] for h, _ in chunk])
            else:
                bias_b = jnp.stack([b_ref[h, :, QB:] for h, _ in chunk])
            s = jnp.einsum("uqd,ukd->uqk", q, k, preferred_element_type=F32) * SCALE + bias_b
            m = jnp.max(s, axis=-1, keepdims=True)
            p = jnp.exp(s - m)
            l = jnp.sum(p, axis=-1, keepdims=True)
            o = jnp.einsum("uqk,ukd->uqd", p.astype(BF16), v, preferred_element_type=F32) / l
            lse = m + jnp.log(l)
            for i, (h, r) in enumerate(chunk):
                copies["o"][h % n_sets][residue(r), :] = o[i]
                ls[h][r * QB:(r + 1) * QB, :] = jnp.where(lane == h, lse[i], 0.0)
            for h in sorted({h for h, _ in chunk}):
                o_ref[:, h * HD:(h + 1) * HD] = copies["o"][h % n_sets][...]
        for r in range(dil):
            blk = slice(r * QB, (r + 1) * QB)
            l_ref[residue(r), :] = (ls[0][blk, :] + ls[1][blk, :]) + (ls[2][blk, :] + ls[3][blk, :])

    def row(b, n):
        return b * nsb + n

    def prev(b, n):
        return b * nsb + jnp.maximum(n - 1, 0)

    in_specs = [
        pl.BlockSpec((rows, GW), lambda b, n: (row(b, n), CB_Q + g)),
        pl.BlockSpec((rows, GW), lambda b, n: (row(b, n), CB_K + g)),
        pl.BlockSpec((rows, GW), lambda b, n: (row(b, n), CB_V + g)),
    ]
    args = [proj, proj, proj]
    scratch = [pltpu.VMEM((rows, 128), F32)] * (4 + n_copied)
    if has_prev:
        in_specs += [pl.BlockSpec((rows, GW), lambda b, n: (prev(b, n), CB_K + g)),
                     pl.BlockSpec((rows, GW), lambda b, n: (prev(b, n), CB_V + g))]
        args += [proj, proj]
    in_specs.append(pl.BlockSpec((None, None, 4, QB, 2 * QB),
                                 lambda b, n: (g, jnp.minimum(n, 1), 0, 0, 0)))
    args.append(bias)
    return pl.pallas_call(
        body, name=f"attn_fwd{g}",
        grid=(BL, nsb),
        in_specs=in_specs,
        out_specs=(pl.BlockSpec((rows, GW), lambda b, n: (row(b, n), 0)),
                   pl.BlockSpec((rows, 128), lambda b, n: (row(b, n), 0))),
        out_shape=(jax.ShapeDtypeStruct((T, GW), F32), jax.ShapeDtypeStruct((T, 128), F32)),
        scratch_shapes=scratch,
        compiler_params=pltpu.CompilerParams(vmem_limit_bytes=VMEM_LIMIT),
    )(*args)


def _attn_bwd(proj, d_out, stats, bias, dproj, g):
    dil = DILATIONS[g]
    rows = QB * dil
    nsb = S // rows
    has_prev = nsb > 1
    n_steps = nsb + 1 if has_prev else 1
    n_in = 7 + (2 if has_prev else 0)

    def residue(r):
        return pl.ds(r, QB, stride=dil)

    n_sets = _scratch_sets(rows)

    def body(*refs):
        q_ref, kc_ref, vc_ref, do_ref, st_ref, b_ref = refs[:6]
        kp_ref, vp_ref = refs[6:8] if has_prev else (None, None)
        out_ref, db_ref = refs[n_in], refs[n_in + 1]
        scr = list(refs[n_in + 2:])
        sq, sk, sv, sems = [scr.pop(0) for _ in range(4)]
        carry = scr.pop(0) if has_prev else None
        sts = scr.pop(0)
        copies = {name: [scr.pop(0) for _ in range(n_sets)]
                  for name in ("q", "kc", "vc", "do", "dq", "dk", "dv") + (("kp", "vp") if has_prev else ())}
        b, n = pl.program_id(0), pl.program_id(1)

        @pl.when((b == 0) & (n == 0))
        def _():
            db_ref[...] = jnp.zeros_like(db_ref)

        def finish(h, r, dq, dk, dv):
            for name, val in (("dq", dq), ("dk", dk), ("dv", dv)):
                copies[name][h % n_sets][residue(r), :] = val

        def finish_head(h):
            sl = slice(h * HD, (h + 1) * HD)
            sq[:, sl] = copies["dq"][h % n_sets][...].astype(BF16)
            sk[:, sl] = copies["dk"][h % n_sets][...].astype(BF16)
            sv[:, sl] = copies["dv"][h % n_sets][...].astype(BF16)

        def write_block(blk_idx):
            row0 = pl.multiple_of(blk_idx * rows, rows)
            _write_columns([(sq, CB * (CB_Q + g)), (sk, CB * (CB_K + g)), (sv, CB * (CB_V + g))],
                           out_ref, row0, sems)

        def carried(h, r):
            blk = slice(r * QB, (r + 1) * QB)
            return ((blk, slice(h * HD, (h + 1) * HD)), (blk, slice(GW + h * HD, GW + (h + 1) * HD)),
                    (blk, slice(2 * GW + h * HD, 2 * GW + (h + 1) * HD)))

        if has_prev:
            @pl.when(n == 0)
            def _():
                carry[...] = jnp.zeros_like(carry)

            @pl.when(n == nsb)
            def _():
                for h in range(4):
                    for r in range(dil):
                        cq, ck, cv = carried(h, r)
                        finish(h, r, carry[cq], carry[ck], carry[cv])
                    finish_head(h)
                write_block(b * nsb + nsb - 1)

        @pl.when(n < nsb)
        def _():
            for r in range(dil):
                sts[r * QB:(r + 1) * QB, :] = st_ref[residue(r), :]
            refs_of = {"q": q_ref, "kc": kc_ref, "vc": vc_ref, "do": do_ref, "kp": kp_ref, "vp": vp_ref}
            for chunk in _unit_chunks(dil):
                heads = sorted({h for h, _ in chunk})
                rows_of = {h: {name: _residue_rows(refs_of[name], copies[name], h, residue)
                               for name in refs_of if refs_of[name] is not None}
                           for h in heads}

                def batch(name):
                    return jnp.stack([rows_of[h][name](r) for h, r in chunk])

                q, k, v, do = batch("q"), batch("kc"), batch("vc"), batch("do")
                if has_prev:
                    k = jnp.concatenate([batch("kp"), k], axis=1)
                    v = jnp.concatenate([batch("vp"), v], axis=1)
                    bias_b = jnp.stack([b_ref[h] for h, _ in chunk])
                else:
                    bias_b = jnp.stack([b_ref[h, :, QB:] for h, _ in chunk])
                lse = jnp.stack([sts[r * QB:(r + 1) * QB, h:h + 1] for h, r in chunk])
                delta = jnp.stack([sts[r * QB:(r + 1) * QB, 4 + h:5 + h] for h, r in chunk])
                s = jnp.einsum("uqd,ukd->uqk", q, k, preferred_element_type=F32) * SCALE + bias_b
                p = jnp.exp(s - lse)
                ds = p * (jnp.einsum("uqd,ukd->uqk", do, v, preferred_element_type=F32) - delta)
                for h in heads:
                    mine = [ds[i] for i, (hh, _) in enumerate(chunk) if hh == h]
                    tot = mine[0]
                    for extra in mine[1:]:
                        tot = tot + extra
                    if has_prev:
                        db_ref[h] += tot
                    else:
                        db_ref[h, :, QB:] += tot
                dsb, pb = ds.astype(BF16), p.astype(BF16)
                dq = jnp.einsum("uqk,ukd->uqd", dsb, k, preferred_element_type=F32) * SCALE
                dk = jnp.einsum("uqk,uqd->ukd", dsb, q, preferred_element_type=F32) * SCALE
                dv = jnp.einsum("uqk,uqd->ukd", pb, do, preferred_element_type=F32)
                for i, (h, r) in enumerate(chunk):
                    if has_prev:
                        cq, ck, cv = carried(h, r)
                        finish(h, r, carry[cq], carry[ck] + dk[i, :QB], carry[cv] + dv[i, :QB])
                        carry[cq] = dq[i]
                        carry[ck] = dk[i, QB:]
                        carry[cv] = dv[i, QB:]
                    else:
                        finish(h, r, dq[i], dk[i], dv[i])
                for h in heads:
                    finish_head(h)
            if has_prev:
                @pl.when(n > 0)
                def _():
                    write_block(b * nsb + n - 1)
            else:
                write_block(b)

    def row(b, n):
        return b * nsb + jnp.minimum(n, nsb - 1)

    def prev(b, n):
        return b * nsb + jnp.maximum(jnp.minimum(n, nsb - 1) - 1, 0)

    in_specs = [
        pl.BlockSpec((rows, GW), lambda b, n: (row(b, n), CB_Q + g)),
        pl.BlockSpec((rows, GW), lambda b, n: (row(b, n), CB_K + g)),
        pl.BlockSpec((rows, GW), lambda b, n: (row(b, n), CB_V + g)),
        pl.BlockSpec((rows, GW), lambda b, n: (row(b, n), 0)),
        pl.BlockSpec((rows, 128), lambda b, n: (row(b, n), 0)),
        pl.BlockSpec((None, None, 4, QB, 2 * QB),
                     lambda b, n: (g, jnp.minimum(jnp.minimum(n, nsb - 1), 1), 0, 0, 0)),
    ]
    args = [proj, proj, proj, d_out, stats, bias]
    scratch = [pltpu.VMEM((rows, GW), BF16)] * 3 + [pltpu.SemaphoreType.DMA((3,))]
    if has_prev:
        in_specs += [pl.BlockSpec((rows, GW), lambda b, n: (prev(b, n), CB_K + g)),
                     pl.BlockSpec((rows, GW), lambda b, n: (prev(b, n), CB_V + g))]
        args += [proj, proj]
        scratch.append(pltpu.VMEM((rows, 3 * GW), F32))
    n_copied = (7 + (2 if has_prev else 0)) * n_sets
    scratch += [pltpu.VMEM((rows, 128), F32)] * (1 + n_copied)
    in_specs.append(pl.BlockSpec(memory_space=pl.ANY))
    args.append(dproj)
    return pl.pallas_call(
        body, name=f"attn_bwd{g}",
        grid=(BL, n_steps),
        in_specs=in_specs,
        out_specs=(pl.BlockSpec(memory_space=pl.ANY),
                   pl.BlockSpec((4, QB, 2 * QB), lambda b, n: (0, 0, 0))),
        out_shape=(jax.ShapeDtypeStruct((T, NCOL), BF16),
                   jax.ShapeDtypeStruct((4, QB, 2 * QB), F32)),
        scratch_shapes=scratch,
        input_output_aliases={len(args) - 1: 0},
        compiler_params=pltpu.CompilerParams(vmem_limit_bytes=VMEM_LIMIT),
    )(*args)


def _tail(x2, tgt2, mod3, o_g, lse_g, proj, w_ao, w_co, w_o, conv_w, conv_b, ln_g, ln_b):
    tm = 256
    per_seq = S // tm
    halo = 16

    def body(x_ref, t_ref, mod_ref, o1_ref, o2_ref, o3_ref, l1_ref, l2_ref, l3_ref,
             ga_ref, u_ref, bg_ref, cg_ref, gc_ref, ma_ref, mc_ref, up_ref, cp_ref,
             wao_ref, wco_ref, wo_ref, cw_ref, cb_ref, lg_ref, lb_ref,
             dproj_ref, dyc_ref, do_ref, st_ref, dxd_ref,
             gwo_ref, gwco_ref, gwao_ref, vec_ref,
             dga_s, dbg_s, dgm_s, sems, acc_o, acc_co, acc_ao):
        i = pl.program_id(0)
        bidx = i // per_seq
        first = (i % per_seq) == 0

        @pl.when(i == 0)
        def _():
            vec_ref[...] = jnp.zeros_like(vec_ref)

        l1, l2, l3 = l1_ref[...], l2_ref[...], l3_ref[...]
        mx = jnp.maximum(jnp.maximum(l1, l2), l3)
        e1, e2, e3 = jnp.exp(l1 - mx), jnp.exp(l2 - mx), jnp.exp(l3 - mx)
        esum = e1 + e2 + e3
        lse_tot = mx + jnp.log(esum)
        w1, w2, w3 = e1 / esum, e2 / esum, e3 / esum

        def per_head(wv):
            return jnp.concatenate([jnp.broadcast_to(wv[:, h:h + 1], (tm, HD)) for h in range(4)], axis=1)

        o = per_head(w1) * o1_ref[...] + per_head(w2) * o2_ref[...] + per_head(w3) * o3_ref[...]

        ga = ga_ref[...].astype(F32)
        sig_ga = _sigmoid(ga)
        silu_ga = ga * sig_ga
        a_in = (o * silu_ga).astype(BF16)
        a_out = _dot(a_in, wao_ref[...])

        u = u_ref[...].astype(F32)
        cg = cg_ref[...].astype(F32)
        z = cg * u
        zp = cp_ref[...].astype(F32) * up_ref[...].astype(F32)
        zp = jnp.where(first, 0.0, zp)
        zcat = jnp.concatenate([zp, z], axis=0)
        z1 = pltpu.roll(zcat, 1, 0)[halo:]
        z2 = pltpu.roll(zcat, 2, 0)[halo:]
        y_conv = cw_ref[0:1, :] * z2 + cw_ref[1:2, :] * z1 + cw_ref[2:3, :] * z + cb_ref[...]
        gc = gc_ref[...].astype(F32)
        sig_gc = _sigmoid(gc)
        silu_gc = gc * sig_gc
        bg = bg_ref[...].astype(F32)
        bg_yc = bg * y_conv
        s_in = (bg_yc * silu_gc).astype(BF16)
        s_out = _dot(s_in, wco_ref[...])

        sa = _sigmoid(ma_ref[...].astype(F32))
        sc = _sigmoid(mc_ref[...].astype(F32))
        merged = (sa * a_out + sc * s_out).astype(BF16)
        y = _dot(merged, wo_ref[...])
        gate1 = 1.0 + mod_ref[0, 2:3, :]
        xv = x_ref[...]
        resid = ALPHA * xv + gate1 * y
        mu = jnp.mean(resid, axis=1, keepdims=True)
        xc = resid - mu
        var = jnp.mean(xc * xc, axis=1, keepdims=True)
        rstd = lax.rsqrt(var + LN_EPS)
        xhat = xc * rstd
        lg = lg_ref[...]
        err = xhat * lg + lb_ref[...] - t_ref[...]
        vec_ref[3:4, :] += (0.5 / D) * jnp.sum(err * err, axis=0, keepdims=True)

        vec_ref[1:2, :] += (1.0 / D) * jnp.sum(err * xhat, axis=0, keepdims=True)
        vec_ref[2:3, :] += (1.0 / D) * jnp.sum(err, axis=0, keepdims=True)
        dxh = err * (lg * (1.0 / D))
        dres = rstd * (dxh - jnp.mean(dxh, axis=1, keepdims=True)
                       - xhat * jnp.mean(dxh * xhat, axis=1, keepdims=True))
        dxd_ref[...] = ALPHA * dres
        dgate = jnp.sum(dres * y, axis=0, keepdims=True)
        vec_ref[4:5, :] += jnp.where(bidx == 0, dgate, 0.0)
        vec_ref[5:6, :] += jnp.where(bidx == 1, dgate, 0.0)
        dy = (dres * gate1).astype(BF16)

        dmerged = _dot_nt(dy, wo_ref[...])
        da_out_f = dmerged * sa
        ds_out_f = dmerged * sc
        da_out = da_out_f.astype(BF16)
        ds_out = ds_out_f.astype(BF16)
        dgm_s[:, 2 * D:3 * D] = (ds_out_f * s_out * (1.0 - sc)).astype(BF16)
        dgm_s[:, D:2 * D] = (da_out_f * a_out * (1.0 - sa)).astype(BF16)
        da_in = _dot_nt(da_out, wao_ref[...])
        ds_in = _dot_nt(ds_out, wco_ref[...])

        d_o = da_in * silu_ga
        do_ref[...] = d_o.astype(BF16)
        dga_s[...] = (da_in * o * (sig_ga + silu_ga * (1.0 - sig_ga))).astype(BF16)
        lane = lax.broadcasted_iota(jnp.int32, (tm, 128), 1)
        stats = lse_tot
        od = o * d_o
        for h in range(4):
            delta = jnp.sum(od[:, h * HD:(h + 1) * HD], axis=1, keepdims=True)
            stats = jnp.where(lane == 4 + h, delta, stats)
        st_ref[...] = stats

        ds_silu = ds_in * silu_gc
        dbg_s[...] = (ds_silu * y_conv).astype(BF16)
        dyc = ds_silu * bg
        dyc_ref[...] = dyc
        vec_ref[0:1, :] += jnp.sum(dyc, axis=0, keepdims=True)
        dgm_s[:, 0:D] = (ds_in * bg_yc * (sig_gc + silu_gc * (1.0 - sig_gc))).astype(BF16)

        @pl.when(i == 0)
        def _():
            acc_o[...] = jnp.zeros_like(acc_o)
            acc_co[...] = jnp.zeros_like(acc_co)
            acc_ao[...] = jnp.zeros_like(acc_ao)

        acc_o[...] += _dot_tn(merged, dy)
        acc_co[...] += _dot_tn(s_in, ds_out)
        acc_ao[...] += _dot_tn(a_in, da_out)

        @pl.when(i == T // tm - 1)
        def _():
            gwo_ref[...] = acc_o[...].astype(BF16)
            gwco_ref[...] = acc_co[...].astype(BF16)
            gwao_ref[...] = acc_ao[...].astype(BF16)

        _write_columns([(dga_s, CB * CB_GA), (dbg_s, D * KB_BG), (dgm_s, D * KB_GC)],
                       dproj_ref, pl.multiple_of(i * tm, tm), sems)

    def tile(width, cblk=0):
        return pl.BlockSpec((tm, width), lambda i: (i, cblk))

    def whole(shape):
        return pl.BlockSpec(shape, lambda i: tuple(0 for _ in shape))

    def once(shape):
        return pl.BlockSpec(shape, lambda i: tuple(0 for _ in shape), pipeline_mode=pl.Buffered(1))

    prev_rows = lambda i: (jnp.maximum(i * (tm // halo) - 1, 0),)
    in_specs = [
        tile(D), tile(D), pl.BlockSpec((1, 3, D), lambda i: (i // per_seq, 0, 0)),
        tile(GW), tile(GW), tile(GW), tile(128), tile(128), tile(128),
        tile(GW, CB_GA), tile(D, KB_U), tile(D, KB_BG), tile(D, KB_CG), tile(D, KB_GC),
        tile(D, KB_MA), tile(D, KB_MC),
        pl.BlockSpec((halo, D), lambda i: (*prev_rows(i), KB_U)),
        pl.BlockSpec((halo, D), lambda i: (*prev_rows(i), KB_CG)),
        whole((GW, D)), whole((D, D)), whole((D, D)),
        whole((3, D)), whole((1, D)), whole((1, D)), whole((1, D)),
    ]
    out_specs = (
        pl.BlockSpec(memory_space=pl.ANY), tile(D), tile(GW), tile(128), tile(D),
        once((D, D)), once((D, D)), once((GW, D)),
        pl.BlockSpec((8, D), lambda i: (0, 0)),
    )
    out_shape = (
        jax.ShapeDtypeStruct((T, NCOL), BF16),
        jax.ShapeDtypeStruct((T, D), F32),
        jax.ShapeDtypeStruct((T, GW), BF16),
        jax.ShapeDtypeStruct((T, 128), F32),
        jax.ShapeDtypeStruct((T, D), F32),
        jax.ShapeDtypeStruct((D, D), BF16),
        jax.ShapeDtypeStruct((D, D), BF16),
        jax.ShapeDtypeStruct((GW, D), BF16),
        jax.ShapeDtypeStruct((8, D), F32),
    )
    return pl.pallas_call(
        body, name="tail",
        grid=(T // tm,),
        in_specs=in_specs, out_specs=out_specs, out_shape=out_shape,
        scratch_shapes=[pltpu.VMEM((tm, GW), BF16), pltpu.VMEM((tm, D), BF16), pltpu.VMEM((tm, 3 * D), BF16),
                        pltpu.SemaphoreType.DMA((3,)),
                        pltpu.VMEM((D, D), F32), pltpu.VMEM((D, D), F32), pltpu.VMEM((GW, D), F32)],
        compiler_params=pltpu.CompilerParams(vmem_limit_bytes=VMEM_LIMIT_TAIL),
    )(x2, tgt2, mod3, *o_g, *lse_g, proj, proj, proj, proj, proj, proj, proj, proj, proj,
      w_ao, w_co, w_o, conv_w, conv_b, ln_g, ln_b)


def _conv_bwd(dyc, proj, conv_w, dproj):
    tm = 512
    per_seq = S // tm

    def body(d_ref, dn_ref, u_ref, c_ref, cw_ref, _, dproj_ref, g_ref, du_s, dc_s, sems):
        i = pl.program_id(0)
        last = (i % per_seq) == per_seq - 1

        @pl.when(i == 0)
        def _():
            g_ref[...] = jnp.zeros_like(g_ref)

        d = d_ref[...]
        dn = jnp.where(last, 0.0, dn_ref[...])
        dcat = jnp.concatenate([d, dn], axis=0)
        d1 = pltpu.roll(dcat, tm + 8 - 1, 0)[:tm]
        d2 = pltpu.roll(dcat, tm + 8 - 2, 0)[:tm]
        dz = cw_ref[2:3, :] * d + cw_ref[1:2, :] * d1 + cw_ref[0:1, :] * d2
        u = u_ref[...].astype(F32)
        cg = c_ref[...].astype(F32)
        du_s[...] = (dz * cg).astype(BF16)
        dc_s[...] = (dz * u).astype(BF16)
        _write_columns([(du_s, D * KB_U), (dc_s, D * KB_CG)], dproj_ref, pl.multiple_of(i * tm, tm), sems)

        z = cg * u
        g_ref[0:1, :] += jnp.sum(d2 * z, axis=0, keepdims=True)
        g_ref[1:2, :] += jnp.sum(d1 * z, axis=0, keepdims=True)
        g_ref[2:3, :] += jnp.sum(d * z, axis=0, keepdims=True)

    n_tiles = T // tm
    next_rows = lambda i: jnp.minimum((i + 1) * (tm // 8), T // 8 - 1)
    return pl.pallas_call(
        body, name="conv_bwd",
        grid=(n_tiles,),
        in_specs=[pl.BlockSpec((tm, D), lambda i: (i, 0)),
                  pl.BlockSpec((8, D), lambda i: (next_rows(i), 0)),
                  pl.BlockSpec((tm, D), lambda i: (i, KB_U)),
                  pl.BlockSpec((tm, D), lambda i: (i, KB_CG)),
                  pl.BlockSpec((3, D), lambda i: (0, 0)),
                  pl.BlockSpec(memory_space=pl.ANY)],
        out_specs=(pl.BlockSpec(memory_space=pl.ANY),
                   pl.BlockSpec((8, D), lambda i: (0, 0))),
        out_shape=(jax.ShapeDtypeStruct((T, NCOL), BF16),
                   jax.ShapeDtypeStruct((8, D), F32)),
        scratch_shapes=[pltpu.VMEM((tm, D), BF16), pltpu.VMEM((tm, D), BF16), pltpu.SemaphoreType.DMA((2,))],
        input_output_aliases={5: 0},
        compiler_params=pltpu.CompilerParams(vmem_limit_bytes=VMEM_LIMIT),
    )(dyc, dyc, proj, proj, conv_w, dproj)


def _dh_dx(dproj, w_in_all, x2, dxd, mod3, chip_sums, hops=(), parts0=None):
    tm = 1024
    per_seq = S // tm
    n = len(chip_sums)
    n_in = 5 + n + (0 if parts0 is None else 1)

    def body(*refs):
        d_ref, w_ref, x_ref, dxd_ref, mod_ref = refs[:5]
        ins = refs[5:5 + n]
        gx_ref, vec_ref = refs[n_in:n_in + 2]
        outs = refs[n_in + 2:n_in + 2 + n]
        acc, send_sems, recv_sems, local_sems = refs[n_in + 2 + n:]
        i, jj = pl.program_id(0), pl.program_id(1)

        @pl.when((i == 0) & (jj == 0))
        def _():
            vec_ref[...] = jnp.zeros_like(vec_ref)
            if n:
                sends, _, mine = _chip_copies(ins, outs, send_sems, recv_sems, local_sems, hops)
                for cp in sends + mine:
                    cp.start()

        if n:
            @pl.when((i == T // tm - 1) & (jj == N_DEV - 1))
            def _():
                sends, arrivals, mine = _chip_copies(ins, outs, send_sems, recv_sems, local_sems, hops)
                for cp in arrivals:
                    cp.wait_recv()
                for cp in sends:
                    cp.wait_send()
                for cp in mine:
                    cp.wait()

        @pl.when(jj == 0)
        def _():
            acc[...] = jnp.zeros_like(acc)

        acc[...] += _dot_nt(d_ref[...], w_ref[...])

        @pl.when(jj == N_DEV - 1)
        def _():
            dh = acc[...]
            bidx = i // per_seq
            gx_ref[...] = dxd_ref[...] + dh * (1.0 + mod_ref[0, 1:2, :])
            dshift = jnp.sum(dh, axis=0, keepdims=True)
            dscale = jnp.sum(dh * x_ref[...], axis=0, keepdims=True)
            vec_ref[0:1, :] += jnp.where(bidx == 0, dshift, 0.0)
            vec_ref[1:2, :] += jnp.where(bidx == 1, dshift, 0.0)
            vec_ref[2:3, :] += jnp.where(bidx == 0, dscale, 0.0)
            vec_ref[3:4, :] += jnp.where(bidx == 1, dscale, 0.0)

    any_spec = pl.BlockSpec(memory_space=pl.ANY)
    res = pl.pallas_call(
        body, name="dh_dx",
        grid=(T // tm, N_DEV),
        in_specs=[
            pl.BlockSpec((tm, SHARD), lambda i, jj: (i, jj)),
            pl.BlockSpec((None, D, SHARD), lambda i, jj: (jj, 0, 0)),
            pl.BlockSpec((tm, D), lambda i, jj: (i, 0)),
            pl.BlockSpec((tm, D), lambda i, jj: (i, 0)),
            pl.BlockSpec((1, 3, D), lambda i, jj: (i // per_seq, 0, 0))] + [any_spec] * (n_in - 5),
        out_specs=(pl.BlockSpec((tm, D), lambda i, jj: (i, 0)),
                   pl.BlockSpec((8, D), lambda i, jj: (0, 0))) + (any_spec,) * n,
        out_shape=(jax.ShapeDtypeStruct((T, D), F32), jax.ShapeDtypeStruct((8, D), F32))
                  + tuple(jax.ShapeDtypeStruct(a.shape, a.dtype) for a in chip_sums),
        scratch_shapes=[pltpu.VMEM((tm, D), F32), pltpu.SemaphoreType.DMA((max(3 * n, 1),)),
                        pltpu.SemaphoreType.DMA((max(3 * n, 1),)), pltpu.SemaphoreType.DMA((max(n, 1),))],
        input_output_aliases={} if parts0 is None else {5 + n: 2},
        compiler_params=pltpu.CompilerParams(vmem_limit_bytes=VMEM_LIMIT),
    )(dproj, w_in_all, x2, dxd, mod3, *chip_sums, *([] if parts0 is None else [parts0]))
    return res[0], res[1], res[2:]


def _adam_step(g, w, m, v):
    nm = ADAM_B1 * m + (1.0 - ADAM_B1) * g
    nv = ADAM_B2 * v + (1.0 - ADAM_B2) * (g * g)
    m_hat = nm / (1.0 - ADAM_B1 ** ADAM_STEP)
    v_hat = nv / (1.0 - ADAM_B2 ** ADAM_STEP)
    return -ADAM_LR * (m_hat / (jnp.sqrt(v_hat) + ADAM_EPS) + ADAM_WD * w), nm, nv


def _adamw(parts, w, m, v, name, row_tile=None):
    n_parts, rows, cols = parts.shape
    tr = rows if row_tile is None else row_tile

    def body(p_ref, w_ref, m_ref, v_ref, g_ref, d_ref, nm_ref, nv_ref):
        g = p_ref[0].astype(F32)
        for s in range(1, n_parts):
            g = g + p_ref[s].astype(F32)
        g_ref[...] = g
        d_ref[...], nm_ref[...], nv_ref[...] = _adam_step(g, w_ref[...], m_ref[...], v_ref[...])

    blk = pl.BlockSpec((tr, cols), lambda i: (i, 0))
    shp = jax.ShapeDtypeStruct((rows, cols), F32)
    return pl.pallas_call(
        body, name=name,
        grid=(rows // tr,),
        in_specs=[pl.BlockSpec((n_parts, tr, cols), lambda i: (0, i, 0)), blk, blk, blk],
        out_specs=(blk, blk, blk, blk),
        out_shape=(shp, shp, shp, shp),
        compiler_params=pltpu.CompilerParams(vmem_limit_bytes=VMEM_LIMIT),
    )(parts, w, m, v)


def _multi_adamw(parts_list, params, name):
    n = len(params)
    flat = [t for wmv in params for t in wmv]

    def body(*refs):
        parts, ins, outs = refs[:n], refs[n:4 * n], refs[4 * n:]
        for p in range(n):
            g = parts[p][0].astype(F32)
            for s in range(1, parts[p].shape[0]):
                g = g + parts[p][s].astype(F32)
            w_ref, m_ref, v_ref = ins[3 * p:3 * p + 3]
            g_ref, d_ref, nm_ref, nv_ref = outs[4 * p:4 * p + 4]
            g_ref[...] = g
            d_ref[...], nm_ref[...], nv_ref[...] = _adam_step(g, w_ref[...], m_ref[...], v_ref[...])

    out_shape = []
    for w, _, _ in params:
        out_shape += [jax.ShapeDtypeStruct(w.shape, F32)] * 4
    res = pl.pallas_call(body, name=name, out_shape=tuple(out_shape))(*parts_list, *flat)
    return [res[4 * p:4 * p + 4] for p in range(n)]


def _small_updates(small_g, dmod_all, rel_parts, params):
    flat = [t for wmv in params for t in wmv]

    def body(sg_ref, dm_ref, rp_ref, *refs):
        ins, outs = refs[:len(flat)], refs[len(flat):]

        def over_devices(row):
            tot = sg_ref[0, row:row + 1, :]
            for s in range(1, N_DEV):
                tot = tot + sg_ref[s, row:row + 1, :]
            return tot

        g_b_ada = dm_ref[0:1, :]
        for r in range(1, N_DEV * BL):
            g_b_ada = g_b_ada + dm_ref[r:r + 1, :]
        g_rel = rp_ref[0]
        for s in range(1, N_DEV):
            g_rel = g_rel + rp_ref[s]
        grads = [g_b_ada, over_devices(0), g_rel, over_devices(1), over_devices(2)]
        outs[0][...] = jnp.sum(over_devices(3), axis=1, keepdims=True)
        for p, g in enumerate(grads):
            w_ref, m_ref, v_ref = ins[3 * p:3 * p + 3]
            g_ref, d_ref, nm_ref, nv_ref = outs[1 + 4 * p:5 + 4 * p]
            g_ref[...] = g
            d_ref[...], nm_ref[...], nv_ref[...] = _adam_step(g, w_ref[...], m_ref[...], v_ref[...])

    out_shape = [jax.ShapeDtypeStruct((1, 1), F32)]
    for w, _, _ in params:
        out_shape += [jax.ShapeDtypeStruct(w.shape, F32)] * 4
    res = pl.pallas_call(body, name="small_updates", out_shape=tuple(out_shape))(small_g, dmod_all, rel_parts, *flat)
    return res[0], [res[1 + 4 * p:5 + 4 * p] for p in range(len(params))]


def _attn_fwd_dense(proj, bias):
    nq = 4
    rows = nq * QB
    nsb = S // rows

    def body(q_ref, k_ref, v_ref, kp_ref, vp_ref, b_ref, o_ref, l_ref, ls0, ls1, ls2, ls3):
        ls = [ls0, ls1, ls2, ls3]
        n = pl.program_id(1)
        lane = lax.broadcasted_iota(jnp.int32, (QB, 128), 1)
        units = [(h, j) for h in range(4) for j in range(nq)]

        def keys(cur_ref, prev_ref, h, j):
            sl = slice(h * HD, (h + 1) * HD)
            if j == 0:
                return jnp.concatenate([prev_ref[:, sl], cur_ref[0:QB, sl]], axis=0)
            return cur_ref[(j - 1) * QB:(j + 1) * QB, sl]

        q = jnp.stack([q_ref[j * QB:(j + 1) * QB, h * HD:(h + 1) * HD] for h, j in units])
        k = jnp.stack([keys(k_ref, kp_ref, h, j) for h, j in units])
        v = jnp.stack([keys(v_ref, vp_ref, h, j) for h, j in units])
        bias_b = jnp.stack([b_ref[jnp.minimum(n, 1), h] if j == 0 else b_ref[1, h] for h, j in units])
        s = jnp.einsum("uqd,ukd->uqk", q, k, preferred_element_type=F32) * SCALE + bias_b
        m = jnp.max(s, axis=-1, keepdims=True)
        p = jnp.exp(s - m)
        l = jnp.sum(p, axis=-1, keepdims=True)
        o = jnp.einsum("uqk,ukd->uqd", p.astype(BF16), v, preferred_element_type=F32) / l
        lse = m + jnp.log(l)
        for i, (h, j) in enumerate(units):
            o_ref[j * QB:(j + 1) * QB, h * HD:(h + 1) * HD] = o[i]
            ls[h][j * QB:(j + 1) * QB, :] = jnp.where(lane == h, lse[i], 0.0)
        l_ref[...] = (ls[0][...] + ls[1][...]) + (ls[2][...] + ls[3][...])

    def row(b, n):
        return b * nsb + n

    def prev(b, n):
        return jnp.maximum((b * nsb + n) * nq - 1, 0)

    in_specs = [
        pl.BlockSpec((rows, GW), lambda b, n: (row(b, n), CB_Q)),
        pl.BlockSpec((rows, GW), lambda b, n: (row(b, n), CB_K)),
        pl.BlockSpec((rows, GW), lambda b, n: (row(b, n), CB_V)),
        pl.BlockSpec((QB, GW), lambda b, n: (prev(b, n), CB_K)),
        pl.BlockSpec((QB, GW), lambda b, n: (prev(b, n), CB_V)),
        pl.BlockSpec((None, 2, 4, QB, 2 * QB), lambda b, n: (0, 0, 0, 0, 0)),
    ]
    return pl.pallas_call(
        body, name="attn_fwd0",
        grid=(BL, nsb),
        in_specs=in_specs,
        out_specs=(pl.BlockSpec((rows, GW), lambda b, n: (row(b, n), 0)),
                   pl.BlockSpec((rows, 128), lambda b, n: (row(b, n), 0))),
        out_shape=(jax.ShapeDtypeStruct((T, GW), F32), jax.ShapeDtypeStruct((T, 128), F32)),
        scratch_shapes=[pltpu.VMEM((rows, 128), F32)] * 4,
        compiler_params=pltpu.CompilerParams(vmem_limit_bytes=VMEM_LIMIT),
    )(proj, proj, proj, proj, proj, bias)


def _attn_bwd_dense(proj, d_out, stats, bias, dproj):
    nq = 4
    rows = nq * QB
    nsb = S // rows
    cols_q, cols_k, cols_v = CB * CB_Q, CB * CB_K, CB * CB_V

    def body(q_ref, k_ref, v_ref, do_ref, st_ref, kp_ref, vp_ref, b_ref, _, out_ref, db_ref,
             sq, sk, sv, carry, sems):
        b, n = pl.program_id(0), pl.program_id(1)
        units = [(h, j) for h in range(4) for j in range(nq)]

        @pl.when((b == 0) & (n == 0))
        def _():
            db_ref[...] = jnp.zeros_like(db_ref)

        @pl.when(n == 0)
        def _():
            carry[...] = jnp.zeros_like(carry)

        def write(first_block, position, count):
            row0 = pl.multiple_of(first_block * QB, QB)
            part = pl.ds(position * QB, count * QB)
            _write_columns([(sq.at[part], cols_q), (sk.at[part], cols_k), (sv.at[part], cols_v)],
                           out_ref, row0, sems)

        @pl.when(n == nsb)
        def _():
            sq[0:QB, :] = carry[:, 0:GW].astype(BF16)
            sk[0:QB, :] = carry[:, GW:2 * GW].astype(BF16)
            sv[0:QB, :] = carry[:, 2 * GW:3 * GW].astype(BF16)
            write((b + 1) * nsb * nq - 1, 0, 1)

        @pl.when(n < nsb)
        def _():
            def keys(cur_ref, prev_ref, h, j):
                sl = slice(h * HD, (h + 1) * HD)
                if j == 0:
                    return jnp.concatenate([prev_ref[:, sl], cur_ref[0:QB, sl]], axis=0)
                return cur_ref[(j - 1) * QB:(j + 1) * QB, sl]

            def block(ref, h, j):
                return ref[j * QB:(j + 1) * QB, h * HD:(h + 1) * HD]

            q = jnp.stack([block(q_ref, h, j) for h, j in units])
            do = jnp.stack([block(do_ref, h, j) for h, j in units])
            k = jnp.stack([keys(k_ref, kp_ref, h, j) for h, j in units])
            v = jnp.stack([keys(v_ref, vp_ref, h, j) for h, j in units])
            bias_b = jnp.stack([b_ref[jnp.minimum(n, 1), h] if j == 0 else b_ref[1, h] for h, j in units])
            lse = jnp.stack([st_ref[j * QB:(j + 1) * QB, h:h + 1] for h, j in units])
            delta = jnp.stack([st_ref[j * QB:(j + 1) * QB, 4 + h:5 + h] for h, j in units])
            s = jnp.einsum("uqd,ukd->uqk", q, k, preferred_element_type=F32) * SCALE + bias_b
            p = jnp.exp(s - lse)
            ds = p * (jnp.einsum("uqd,ukd->uqk", do, v, preferred_element_type=F32) - delta)
            for h in range(4):
                tot = ds[h * nq]
                for j in range(1, nq):
                    tot = tot + ds[h * nq + j]
                db_ref[h] += tot
            dsb, pb = ds.astype(BF16), p.astype(BF16)
            dq = jnp.einsum("uqk,ukd->uqd", dsb, k, preferred_element_type=F32) * SCALE
            dk = jnp.einsum("uqk,uqd->ukd", dsb, q, preferred_element_type=F32) * SCALE
            dv = jnp.einsum("uqk,uqd->ukd", pb, do, preferred_element_type=F32)
            for h in range(4):
                sl = slice(h * HD, (h + 1) * HD)
                u0, last = h * nq, h * nq + nq - 1
                sq[0:QB, sl] = carry[:, sl].astype(BF16)
                sk[0:QB, sl] = (carry[:, GW + h * HD:GW + (h + 1) * HD] + dk[u0, :QB]).astype(BF16)
                sv[0:QB, sl] = (carry[:, 2 * GW + h * HD:2 * GW + (h + 1) * HD] + dv[u0, :QB]).astype(BF16)
                for j in range(nq - 1):
                    pos = slice((j + 1) * QB, (j + 2) * QB)
                    sq[pos, sl] = dq[u0 + j].astype(BF16)
                    sk[pos, sl] = (dk[u0 + j, QB:] + dk[u0 + j + 1, :QB]).astype(BF16)
                    sv[pos, sl] = (dv[u0 + j, QB:] + dv[u0 + j + 1, :QB]).astype(BF16)
                carry[:, sl] = dq[last]
                carry[:, GW + h * HD:GW + (h + 1) * HD] = dk[last, QB:]
                carry[:, 2 * GW + h * HD:2 * GW + (h + 1) * HD] = dv[last, QB:]

            @pl.when(n == 0)
            def _():
                write(b * nsb * nq, 1, nq - 1)

            @pl.when(n > 0)
            def _():
                write((b * nsb + n) * nq - 1, 0, nq)

    def row(b, n):
        return b * nsb + jnp.minimum(n, nsb - 1)

    def prev(b, n):
        return jnp.maximum(row(b, n) * nq - 1, 0)

    in_specs = [
        pl.BlockSpec((rows, GW), lambda b, n: (row(b, n), CB_Q)),
        pl.BlockSpec((rows, GW), lambda b, n: (row(b, n), CB_K)),
        pl.BlockSpec((rows, GW), lambda b, n: (row(b, n), CB_V)),
        pl.BlockSpec((rows, GW), lambda b, n: (row(b, n), 0)),
        pl.BlockSpec((rows, 128), lambda b, n: (row(b, n), 0)),
        pl.BlockSpec((QB, GW), lambda b, n: (prev(b, n), CB_K)),
        pl.BlockSpec((QB, GW), lambda b, n: (prev(b, n), CB_V)),
        pl.BlockSpec((None, 2, 4, QB, 2 * QB), lambda b, n: (0, 0, 0, 0, 0)),
        pl.BlockSpec(memory_space=pl.ANY),
    ]
    return pl.pallas_call(
        body, name="attn_bwd0",
        grid=(BL, nsb + 1),
        in_specs=in_specs,
        out_specs=(pl.BlockSpec(memory_space=pl.ANY),
                   pl.BlockSpec((4, QB, 2 * QB), lambda b, n: (0, 0, 0))),
        out_shape=(jax.ShapeDtypeStruct((T, NCOL), BF16),
                   jax.ShapeDtypeStruct((4, QB, 2 * QB), F32)),
        scratch_shapes=[pltpu.VMEM((rows, GW), BF16)] * 3
                       + [pltpu.VMEM((QB, 3 * GW), F32), pltpu.SemaphoreType.DMA((3,))],
        input_output_aliases={8: 0},
        compiler_params=pltpu.CompilerParams(vmem_limit_bytes=VMEM_LIMIT),
    )(proj, proj, proj, d_out, stats, proj, proj, bias, dproj)


def _attention_forward(proj, rel_bias):
    expand_lanes, grad_lanes, masks = (jnp.asarray(t) for t in _bucket_maps())
    bias = _bias_expand(rel_bias, expand_lanes, masks)
    fwd = [_attn_fwd_dense(proj, bias)] + [_attn_fwd(proj, bias, g) for g in (1, 2)]
    return bias, grad_lanes, [f[0] for f in fwd], [f[1] for f in fwd]


def _local_step(x2, tgt2, mod3, h, proj, attn, w_ao, w_co, w_o, conv_w, conv_b, ln_g, ln_b):
    bias, buckets, o_g, lse_g = attn

    (dproj, dyc, d_o, stats, dxd, gw_o, gw_co, gw_ao, tail_vec) = _tail(
        x2, tgt2, mod3, o_g, lse_g, proj, w_ao, w_co, w_o, conv_w, conv_b, ln_g, ln_b)

    dproj, db = _attn_bwd_dense(proj, d_o, stats, bias, dproj)
    dbias = [db]
    for g in (1, 2):
        dproj, db = _attn_bwd(proj, d_o, stats, bias, dproj, g)
        dbias.append(db)
    g_rel_bias = _bias_grad(*dbias, buckets)
    dproj, conv_vec = _conv_bwd(dyc, proj, conv_w, dproj)

    gw_ao = jnp.transpose(gw_ao.reshape(GW, N_DEV, D // N_DEV), (1, 0, 2))
    return dproj, dxd, gw_ao, gw_co, gw_o, conv_vec, g_rel_bias, tail_vec


def kernel(x, c, w_ada, b_ada, w_in, conv_w, conv_b, rel_bias, w_attn_out, w_conv_out, w_o, ln_g, ln_b, loss_target, m_w_ada, m_b_ada, m_w_in, m_conv_w, m_conv_b, m_rel_bias, m_w_attn_out, m_w_conv_out, m_w_o, m_ln_g, m_ln_b, v_w_ada, v_b_ada, v_w_in, v_conv_w, v_conv_b, v_rel_bias, v_w_attn_out, v_w_conv_out, v_w_o, v_ln_g, v_ln_b):
    me = _my_index()
    x2 = x.reshape(T, D)
    tgt2 = loss_target.reshape(T, D)

    b_cols = lax.dynamic_slice(b_ada, (0, me * ADA_SHARD), (1, ADA_SHARD))
    c_g, mod_in = _mod_exchange(jnp.pad(c, ((0, 8 - BL), (0, 0))), w_ada[0], b_cols)
    c_all = c_g[:, 0:BL, :].reshape(N_DEV * BL, D)
    mod3 = jnp.transpose(mod_in[:, 0:BL, :], (1, 0, 2)).reshape(BL, 3, D)

    h = _prep_h(x2, mod3)
    rows_shape = jax.ShapeDtypeStruct((N_DEV, D // N_DEV, D), BF16)
    proj, w_in_all, (w_ao_g, w_co_g, w_o_g, conv_w_g) = _gather_proj(
        _shard_order(), h, w_in[0].astype(BF16), 1024,
        ([w_attn_out[0].astype(BF16), w_conv_out[0].astype(BF16), w_o[0].astype(BF16), conv_w[0]],
         [jax.ShapeDtypeStruct((N_DEV, GW, D // N_DEV), BF16), rows_shape, rows_shape,
          jax.ShapeDtypeStruct((N_DEV, 3, D // N_DEV), F32)]))

    attn = _attention_forward(proj, rel_bias)
    w_ao_full = jnp.transpose(w_ao_g, (1, 0, 2)).reshape(GW, D)
    w_co_full = w_co_g.reshape(D, D)
    w_o_full = w_o_g.reshape(D, D)
    conv_w_full = jnp.transpose(conv_w_g, (1, 0, 2)).reshape(3, D)

    (dproj, dxd, gw_ao, gw_co, gw_o, conv_vec, g_rel_bias, tail_vec) = _local_step(
        x2, tgt2, mod3, h, proj, attn, w_ao_full, w_co_full, w_o_full,
        conv_w_full, conv_b, ln_g, ln_b)

    g_conv_w_blocks = jnp.transpose(conv_vec[0:3].reshape(3, N_DEV, D // N_DEV), (1, 0, 2))
    partials = [gw_ao, gw_co.reshape(N_DEV, D // N_DEV, D), gw_o.reshape(N_DEV, D // N_DEV, D), g_conv_w_blocks]
    w_in_sums, w_in_parts, sib = _gw_in_pair(
        _slice_order(), h, dproj, partials,
        [jax.ShapeDtypeStruct((4, GW, D // N_DEV), BF16),
         jax.ShapeDtypeStruct((4, D // N_DEV, D), BF16),
         jax.ShapeDtypeStruct((4, D // N_DEV, D), BF16),
         jax.ShapeDtypeStruct((4, 3, D // N_DEV), F32)])
    core = lax.axis_index("c").astype(jnp.int32).reshape(1)
    chip_sums = [w_in_sums] + list(_pair_add(core, partials, sib))
    hops = [(3,)] + [(1, 2, 3)] * 4
    grad_x, mod_vec, (r_in, r_ao, r_co, r_o, r_cw) = _dh_dx(
        dproj, w_in_all, x2, dxd, mod3, chip_sums, hops, w_in_parts)

    small = jnp.concatenate([
        tail_vec[0:4],
        jnp.pad(g_rel_bias.reshape(1, N_BUCKETS * N_HEADS), ((0, 0), (0, D - N_BUCKETS * N_HEADS))),
        jnp.zeros((3, D), F32)], axis=0)
    dmod = jnp.concatenate([mod_vec[0:2], mod_vec[2:4], tail_vec[4:6]], axis=1)
    small_g, dmod_g = _all_gather(
        [small, dmod],
        [jax.ShapeDtypeStruct((N_DEV, 8, D), F32), jax.ShapeDtypeStruct((N_DEV, BL, 3 * D), F32)],
        "gather_small")
    dmod_all = dmod_g.reshape(N_DEV * BL, 3 * D)
    small_names = ["b_ada", "conv_b", "rel_bias", "ln_g", "ln_b"]
    small_params = [(b_ada, m_b_ada, v_b_ada), (conv_b, m_conv_b, v_conv_b), (rel_bias, m_rel_bias, v_rel_bias),
                    (ln_g, m_ln_g, v_ln_g), (ln_b, m_ln_b, v_ln_b)]
    loss, small_res = _small_updates(
        small_g, dmod_all, small_g[:, 4, :N_BUCKETS * N_HEADS].reshape(N_DEV, N_BUCKETS, N_HEADS), small_params)
    loss = loss.reshape(())

    dmod_cols = lax.dynamic_slice(dmod_all, (0, me * ADA_SHARD), (N_DEV * BL, ADA_SHARD))
    res = {
        "w_ada": tuple(t[None] for t in _w_ada_update(jnp.transpose(c_all), dmod_cols,
                                                      w_ada[0], m_w_ada[0], v_w_ada[0])),
        "w_in": tuple(t[None] for t in _adamw(r_in, w_in[0], m_w_in[0], v_w_in[0], "adam_w_in", 128)),
    }
    mid_names = ["conv_w", "w_attn_out", "w_conv_out", "w_o"]
    mid_parts = [r_cw, r_ao, r_co, r_o]
    mid_full = [(conv_w, m_conv_w, v_conv_w), (w_attn_out, m_w_attn_out, v_w_attn_out),
                (w_conv_out, m_w_conv_out, v_w_conv_out), (w_o, m_w_o, v_w_o)]
    mid_res = _multi_adamw(mid_parts, [tuple(t[0] for t in wmv) for wmv in mid_full], "adam_mid")
    for nm, wmv, outs4 in zip(mid_names, mid_full, mid_res):
        res[nm] = tuple(t[None] for t in outs4)
    res.update(dict(zip(small_names, small_res)))
    order = ["w_ada", "b_ada", "w_in", "conv_w", "conv_b", "rel_bias", "w_attn_out", "w_conv_out",
             "w_o", "ln_g", "ln_b"]
    outs = [loss, grad_x.reshape(BL, S, D)]
    for k in range(4):
        outs += [res[name][k] for name in order]
    return tuple(outs)
```

```python
import math

import numpy as np
import jax
import jax.numpy as jnp
from jax import lax
from jax.experimental import pallas as pl
from jax.experimental.pallas import tpu as pltpu

F32 = jnp.float32
BF16 = jnp.bfloat16
MESH = pl.DeviceIdType.MESH

N_DEV = 8
D = 1024
S = 2048
BL = 2
T = BL * S
NCOL = 11264
SHARD = NCOL // N_DEV
CB = 512
NCB = NCOL // CB
HD = 128
GW = 512
QB = 128
DILATIONS = (1, 4, 16)
N_STEPS = 128
N_BUCKETS = 32
N_HEADS = 12
ALPHA = 2.0 ** 0.25
LN_EPS = 1e-5
NEG_INF = -1e30
SCALE = HD ** -0.5
ADA_SHARD = 3 * D // N_DEV

CB_Q, CB_K, CB_V, CB_GA = 0, 3, 6, 9
KB_U, KB_BG, KB_CG, KB_GC, KB_MA, KB_MC = 5, 6, 7, 8, 9, 10

ADAM_LR, ADAM_B1, ADAM_B2, ADAM_EPS, ADAM_WD, ADAM_STEP = 0.001, 0.9, 0.999, 1e-08, 0.01, 10

VMEM_LIMIT = 56 * 1024 * 1024
VMEM_LIMIT_TAIL = 62 * 1024 * 1024


def _dot(a, b):
    return jnp.dot(a, b, preferred_element_type=F32)


def _dot_nt(a, b):
    return lax.dot_general(a, b, (((1,), (1,)), ((), ())), preferred_element_type=F32)


def _dot_tn(a, b):
    return lax.dot_general(a, b, (((0,), (0,)), ((), ())), preferred_element_type=F32)


def _sigmoid(v):
    return 1.0 / (1.0 + jnp.exp(-v))


def _write_columns(pieces, dst_hbm, row0, sems):
    copies = []
    for k, (src, col0) in enumerate(pieces):
        rows, width = src.shape
        copies.append(pltpu.make_async_copy(
            src, dst_hbm.at[pl.ds(row0, rows), pl.ds(col0, width)], sems.at[k]))
    for cp in copies:
        cp.start()
    for cp in copies:
        cp.wait()


def _my_index():
    return 4 * lax.axis_index("x") + 2 * lax.axis_index("y") + lax.axis_index("c")


class _Gather:
    def __init__(self, ins, outs, stage, send_sems, recv_sems, local_sems):
        self.ins, self.outs, self.stage = ins, outs, stage
        self.send_sems, self.recv_sems, self.local_sems = send_sems, recv_sems, local_sems
        x, y, c = lax.axis_index("x"), lax.axis_index("y"), lax.axis_index("c")
        self.c = c
        self.me, self.sibling = (x, y, c), (x, y, 1 - c)
        self.chips = [(1 - x, y), (x, 1 - y), (1 - x, 1 - y)]

    @staticmethod
    def scratch(arrs):
        n = len(arrs)
        return ([pltpu.SemaphoreType.DMA((7 * n,)), pltpu.SemaphoreType.DMA((7 * n,)),
                 pltpu.SemaphoreType.DMA((n,))] + [pltpu.VMEM(a.shape, a.dtype) for a in arrs])

    def _copy(self, a, k, block, to, src=None):
        dst = self.outs[a].at[4 * block[0] + 2 * block[1] + block[2]]
        return pltpu.make_async_remote_copy(
            src_ref=dst if src is None else src, dst_ref=dst,
            send_sem=self.send_sems.at[a * 7 + k], recv_sem=self.recv_sems.at[a * 7 + k],
            device_id=to, device_id_type=MESH)

    def _first(self):
        first = []
        for a in range(len(self.ins)):
            first.append(self._copy(a, 0, self.me, self.sibling, src=self.ins[a]))
            first += [self._copy(a, 1 + j, self.me, (*chip, self.c), src=self.ins[a])
                      for j, chip in enumerate(self.chips)]
        return first

    def _mine(self):
        me = self.me
        return [pltpu.make_async_copy(self.stage[a], self.outs[a].at[4 * me[0] + 2 * me[1] + me[2]],
                                      self.local_sems.at[a]) for a in range(len(self.ins))]

    def begin(self):
        for cp in self._first():
            cp.start()
        loads = [pltpu.make_async_copy(self.ins[a], self.stage[a], self.local_sems.at[a])
                 for a in range(len(self.ins))]
        for cp in loads:
            cp.start()
        for cp in loads:
            cp.wait()
        for cp in self._mine():
            cp.start()

    def finish(self):
        n, c, me, sibling = len(self.ins), self.c, self.me, self.sibling
        passed = []
        for j, chip in enumerate(self.chips):
            for a in range(n):
                self._copy(a, 1 + j, (*chip, c), me).wait_recv()
                fwd = self._copy(a, 4 + j, (*chip, c), sibling)
                fwd.start()
                passed.append(fwd)
        for a in range(n):
            self._copy(a, 0, sibling, me).wait_recv()
        for j, chip in enumerate(self.chips):
            for a in range(n):
                self._copy(a, 4 + j, (*chip, 1 - c), me).wait_recv()
        for cp in self._first() + passed:
            cp.wait_send()
        for cp in self._mine():
            cp.wait()


def _all_gather(arrs, out_shapes, name):
    n = len(arrs)

    def body(*refs):
        g = _Gather(refs[:n], refs[n:2 * n], refs[2 * n + 3:], *refs[2 * n:2 * n + 3])
        g.begin()
        g.finish()

    any_spec = pl.BlockSpec(memory_space=pl.ANY)
    return pl.pallas_call(
        body, name=name,
        out_shape=tuple(out_shapes),
        in_specs=[any_spec] * n,
        out_specs=tuple([any_spec] * n),
        scratch_shapes=_Gather.scratch(arrs),
    )(*arrs)


def _slice_order():
    x, y, c = lax.axis_index("x"), lax.axis_index("y"), lax.axis_index("c")
    slots = []
    for q in (2 * (1 - x) + y, 2 * x + (1 - y), 2 * (1 - x) + (1 - y), 2 * x + y):
        slots += [2 * q + 1 - c, 2 * q + c]
    return jnp.stack(slots).astype(jnp.int32)


def _gw_in_pair(order, h, dproj, smalls, small_shapes4):
    kk, m = h.shape
    tk = min(kk, 2048)
    nk = kk // tk
    ncols = dproj.shape[1] // N_DEV
    n = len(smalls)

    def body(order_ref, h_ref, d_ref, *rest):
        ins = rest[:n]
        sums_hbm, parts_hbm = rest[n], rest[n + 1]
        sib = rest[n + 2:2 * n + 2]
        (acc, sendbuf, recvbuf, sumbuf, send_sems, recv_sems, local_sem, ssend, srecv,
         isend, irecv) = rest[2 * n + 2:]
        js, k = pl.program_id(0), pl.program_id(1)
        x, y, c = lax.axis_index("x"), lax.axis_index("y"), lax.axis_index("c")
        sibling = (x, y, 1 - c)
        my_chip = 2 * x + y
        near = [(1 - x, y, c), (x, 1 - y, c)]

        def ici_copy(p, out_chip):
            peer = near[p]
            return pltpu.make_async_remote_copy(
                src_ref=sumbuf.at[p], dst_ref=parts_hbm.at[out_chip],
                send_sem=isend.at[p], recv_sem=irecv.at[p], device_id=peer, device_id_type=MESH)

        def small_copies():
            return [pltpu.make_async_remote_copy(
                        src_ref=ins[a].at[2 * q + 1 - c], dst_ref=sib[a].at[q],
                        send_sem=ssend.at[a * 4 + q], recv_sem=srecv.at[a * 4 + q],
                        device_id=sibling, device_id_type=MESH)
                    for a in range(n) for q in range(4)]

        def slice_copy(p):
            return pltpu.make_async_remote_copy(
                src_ref=sendbuf, dst_ref=recvbuf.at[p], send_sem=send_sems.at[p], recv_sem=recv_sems.at[p],
                device_id=sibling, device_id_type=MESH)

        def sum_copy(p):
            return pltpu.make_async_copy(sumbuf.at[2], sums_hbm.at[order_ref[2 * p] // 2], local_sem)

        @pl.when((js == 0) & (k == 0))
        def _():
            for cp in small_copies():
                cp.start()

        @pl.when(k == 0)
        def _():
            acc[...] = jnp.zeros_like(acc)

        acc[...] += _dot_tn(h_ref[...], d_ref[...])

        for p in range(4):
            @pl.when((js == 2 * p) & (k == nk - 1))
            def _():
                if p > 0:
                    slice_copy(p - 1).wait_send()
                sendbuf[...] = acc[...].astype(BF16)
                slice_copy(p).start()

            @pl.when((js == 2 * p + 1) & (k == nk - 1))
            def _():
                slice_copy(p).wait_recv()
                if p == 3:
                    sum_copy(2).wait()
                sumbuf[min(p, 2)] = (acc[...] + recvbuf[p].astype(F32)).astype(BF16)
                if p < 2:
                    ici_copy(p, my_chip).start()
                else:
                    sum_copy(p).start()

        @pl.when((js == N_DEV - 1) & (k == nk - 1))
        def _():
            slice_copy(3).wait_send()
            sum_copy(3).wait()
            for cp in small_copies():
                cp.wait()
            for p in range(2):
                ici_copy(p, 2 * near[p][0] + near[p][1]).wait_recv()
                ici_copy(p, my_chip).wait_send()

    any_spec = pl.BlockSpec(memory_space=pl.ANY)
    res = pl.pallas_call(
        body, name="gw_in_pair",
        grid_spec=pltpu.PrefetchScalarGridSpec(
            num_scalar_prefetch=1,
            grid=(N_DEV, nk),
            in_specs=[pl.BlockSpec((tk, m), lambda js, k, order_ref: (k, 0)),
                      pl.BlockSpec((tk, ncols), lambda js, k, order_ref: (k, order_ref[js]))] + [any_spec] * n,
            out_specs=(any_spec,) * (n + 2),
            scratch_shapes=[pltpu.VMEM((m, ncols), F32), pltpu.VMEM((m, ncols), BF16),
                            pltpu.VMEM((4, m, ncols), BF16), pltpu.VMEM((3, m, ncols), BF16),
                            pltpu.SemaphoreType.DMA((4,)), pltpu.SemaphoreType.DMA((4,)),
                            pltpu.SemaphoreType.DMA,
                            pltpu.SemaphoreType.DMA((4 * n,)), pltpu.SemaphoreType.DMA((4 * n,)),
                            pltpu.SemaphoreType.DMA((2,)), pltpu.SemaphoreType.DMA((2,))]),
        out_shape=(jax.ShapeDtypeStruct((4, m, ncols), BF16),) * 2 + tuple(small_shapes4),
        compiler_params=pltpu.CompilerParams(vmem_limit_bytes=VMEM_LIMIT),
    )(order, h, dproj, *smalls)
    return res[0], res[1], res[2:]


def _chip_copies(ins, outs, send_sems, recv_sems, local_sems, hops):
    n = len(ins)
    x, y, c = lax.axis_index("x"), lax.axis_index("y"), lax.axis_index("c")
    my_chip = 2 * x + y

    def peer_of(k):
        return ((1 - x) if (k >> 1) & 1 else x, (1 - y) if k & 1 else y, c)

    def copy(a, k, out_chip):
        peer = peer_of(k)
        return pltpu.make_async_remote_copy(
            src_ref=ins[a].at[2 * peer[0] + peer[1]], dst_ref=outs[a].at[out_chip],
            send_sem=send_sems.at[a * 3 + k - 1], recv_sem=recv_sems.at[a * 3 + k - 1],
            device_id=peer, device_id_type=MESH)

    sends = [copy(a, k, my_chip) for k in range(1, 4) for a in range(n) if k in hops[a]]
    arrivals = []
    for k in range(1, 4):
        peer = peer_of(k)
        arrivals += [copy(a, k, 2 * peer[0] + peer[1]) for a in range(n) if k in hops[a]]
    mine = [pltpu.make_async_copy(ins[a].at[my_chip], outs[a].at[my_chip], local_sems.at[a])
            for a in range(n)]
    return sends, arrivals, mine


def _pair_add(core, mines, theirs):
    n = len(mines)

    def body(core_ref, *refs):
        mine, sib, outs = refs[:n], refs[n:2 * n], refs[2 * n:]
        for a in range(n):
            for q in range(4):
                outs[a][q] = (mine[a][2 * q + core_ref[0]].astype(F32)
                              + sib[a][q].astype(F32)).astype(outs[a].dtype)

    return pl.pallas_call(
        body, name="pair_add",
        in_specs=[pl.BlockSpec(memory_space=pltpu.SMEM)] + [pl.BlockSpec(memory_space=pltpu.VMEM)] * (2 * n),
        out_shape=tuple(jax.ShapeDtypeStruct(t.shape, t.dtype) for t in theirs),
    )(core, *mines, *theirs)


def _mod_exchange(c8, w_ada, b_cols):
    cols = w_ada.shape[1]

    def body(c_ref, w_ref, b_ref, call_ref, mod_ref, msend, send1, recv1, send2, recv2):
        x, y, c = lax.axis_index("x"), lax.axis_index("y"), lax.axis_index("c")
        my_slot = 4 * x + 2 * y + c

        def peer_of(k):
            return ((1 - x) if (k >> 2) & 1 else x, (1 - y) if (k >> 1) & 1 else y, (1 - c) if k & 1 else c)

        def slot_of(dev):
            return 4 * dev[0] + 2 * dev[1] + dev[2]

        def exchange(src_of, dst_ref, send_sems, recv_sems):
            sends, arrivals = [], []
            for k in range(1, 8):
                peer = peer_of(k)
                sends.append(pltpu.make_async_remote_copy(
                    src_ref=src_of(slot_of(peer)), dst_ref=dst_ref.at[my_slot],
                    send_sem=send_sems.at[k - 1], recv_sem=recv_sems.at[k - 1],
                    device_id=peer, device_id_type=MESH))
                arrivals.append(pltpu.make_async_remote_copy(
                    src_ref=src_of(my_slot), dst_ref=dst_ref.at[slot_of(peer)],
                    send_sem=send_sems.at[k - 1], recv_sem=recv_sems.at[k - 1],
                    device_id=peer, device_id_type=MESH))
            for cp in sends:
                cp.start()
            for cp in arrivals:
                cp.wait_recv()
            for cp in sends:
                cp.wait_send()

        call_ref[my_slot] = c_ref[...]
        exchange(lambda s: c_ref, call_ref, send1, recv1)
        cv = call_ref[...].reshape(N_DEV * 8, c_ref.shape[1])
        act = cv * _sigmoid(cv)
        mod = jnp.dot(act, w_ref[...], preferred_element_type=F32,
                      precision=lax.Precision.HIGHEST) + b_ref[...]
        msend[...] = mod.reshape(N_DEV, 8, cols)
        mod_ref[my_slot] = msend[my_slot]
        exchange(lambda s: msend.at[s], mod_ref, send2, recv2)

    return pl.pallas_call(
        body, name="mod_exchange",
        out_shape=(jax.ShapeDtypeStruct((N_DEV, 8, c8.shape[1]), F32),
                   jax.ShapeDtypeStruct((N_DEV, 8, cols), F32)),
        scratch_shapes=[pltpu.VMEM((N_DEV, 8, cols), F32)] + [pltpu.SemaphoreType.DMA((7,))] * 4,
    )(c8, w_ada, b_cols)


def _w_ada_update(c_all_t, dmod_cols, w, m, v):
    rows, cols = w.shape
    tr = 256

    def body(c_ref, d_ref, w_ref, m_ref, v_ref, g_ref, dl_ref, nm_ref, nv_ref):
        cv = c_ref[...]
        g = jnp.dot(cv * _sigmoid(cv), d_ref[...], preferred_element_type=F32,
                    precision=lax.Precision.HIGHEST)
        g_ref[...] = g
        dl_ref[...], nm_ref[...], nv_ref[...] = _adam_step(g, w_ref[...], m_ref[...], v_ref[...])

    blk = pl.BlockSpec((tr, cols), lambda i: (i, 0))
    shp = jax.ShapeDtypeStruct((rows, cols), F32)
    return pl.pallas_call(
        body, name="adam_w_ada",
        grid=(rows // tr,),
        in_specs=[pl.BlockSpec((tr, c_all_t.shape[1]), lambda i: (i, 0)),
                  pl.BlockSpec(dmod_cols.shape, lambda i: (0, 0)), blk, blk, blk],
        out_specs=(blk, blk, blk, blk),
        out_shape=(shp, shp, shp, shp),
    )(c_all_t, dmod_cols, w, m, v)


def _shard_order():
    x, y, c = lax.axis_index("x"), lax.axis_index("y"), lax.axis_index("c")
    devs = [(x, y, c), (x, y, 1 - c)]
    for chip in [(1 - x, y), (x, 1 - y), (1 - x, 1 - y)]:
        devs += [(*chip, c), (*chip, 1 - c)]
    return jnp.stack([4 * d[0] + 2 * d[1] + d[2] for d in devs]).astype(jnp.int32)


def _prep_h(x2, mod3):
    ts = 512
    per_seq = S // ts

    def body(x_ref, mod_ref, h_ref):
        shift = mod_ref[0, 0:1, :]
        scale = mod_ref[0, 1:2, :]
        h_ref[...] = (x_ref[...] * (1.0 + scale) + shift).astype(BF16)

    return pl.pallas_call(
        body, name="prep_h",
        grid=(T // ts,),
        in_specs=[pl.BlockSpec((ts, D), lambda i: (i, 0)),
                  pl.BlockSpec((1, 3, D), lambda i: (i // per_seq, 0, 0))],
        out_specs=pl.BlockSpec((ts, D), lambda i: (i, 0)),
        out_shape=jax.ShapeDtypeStruct((T, D), BF16),
    )(x2, mod3)


def _gather_proj(order, h, w_shard, tm, ride=()):
    rows, kdim = h.shape
    ncols = w_shard.shape[1]
    n_i = rows // tm
    ride_arrs, ride_shapes = ride if ride else ((), ())
    n_ride = len(ride_arrs)

    def body(order_ref, h_ref, mine_hbm, *rest):
        ride_ins = rest[:n_ride]
        o_ref, all_hbm = rest[n_ride:n_ride + 2]
        ride_outs = rest[n_ride + 2:2 * n_ride + 2]
        wv, send_sems, recv_sems, local_sems = rest[2 * n_ride + 2:2 * n_ride + 6]
        ride_scr = rest[2 * n_ride + 6:]
        j, i = pl.program_id(0), pl.program_id(1)
        x, y, c = lax.axis_index("x"), lax.axis_index("y"), lax.axis_index("c")
        me, sibling = (x, y, c), (x, y, 1 - c)
        chips = [(1 - x, y), (x, 1 - y), (1 - x, 1 - y)]

        def slot(dev):
            return 4 * dev[0] + 2 * dev[1] + dev[2]

        def copy(k, block, to, src=None):
            return pltpu.make_async_remote_copy(
                src_ref=wv.at[slot(block)] if src is None else src, dst_ref=wv.at[slot(block)],
                send_sem=send_sems.at[k], recv_sem=recv_sems.at[k],
                device_id=to, device_id_type=MESH)

        def keep(step, block):
            return pltpu.make_async_copy(wv.at[slot(block)], all_hbm.at[slot(block)], local_sems.at[step])

        if n_ride:
            gather = _Gather(ride_ins, ride_outs, ride_scr[3:], *ride_scr[:3])
        first = [copy(0, me, sibling, mine_hbm)] + [copy(1 + q, me, (*chip, c), mine_hbm)
                                                    for q, chip in enumerate(chips)]
        passed = [copy(4 + q, (*chip, c), sibling) for q, chip in enumerate(chips)]
        due = [(me, None, None), (sibling, copy(0, sibling, me), None)]
        for q, chip in enumerate(chips):
            due.append(((*chip, c), copy(1 + q, (*chip, c), me), passed[q]))
            due.append(((*chip, 1 - c), copy(4 + q, (*chip, 1 - c), me), None))

        @pl.when((j == 0) & (i == 0))
        def _():
            for cp in first:
                cp.start()
            load = pltpu.make_async_copy(mine_hbm, wv.at[slot(me)], local_sems.at[N_DEV])
            load.start()
            load.wait()
            keep(0, me).start()

        for step in range(1, N_DEV):
            block, arrival, forward = due[step]

            @pl.when((j == step) & (i == 0))
            def _():
                arrival.wait_recv()
                if forward is not None:
                    forward.start()
                keep(step, block).start()
                if n_ride and step == N_DEV - 2:
                    gather.begin()

        o_ref[...] = _dot(h_ref[...], wv[order_ref[j]]).astype(BF16)

        @pl.when((j == N_DEV - 1) & (i == n_i - 1))
        def _():
            for cp in first + passed:
                cp.wait_send()
            for step in range(N_DEV):
                keep(step, due[step][0]).wait()
            if n_ride:
                gather.finish()

    any_spec = pl.BlockSpec(memory_space=pl.ANY)
    res = pl.pallas_call(
        body, name="gather_proj",
        grid_spec=pltpu.PrefetchScalarGridSpec(
            num_scalar_prefetch=1,
            grid=(N_DEV, n_i),
            in_specs=[pl.BlockSpec((tm, kdim), lambda j, i, order_ref: (i, 0)), any_spec] + [any_spec] * n_ride,
            out_specs=(pl.BlockSpec((tm, ncols), lambda j, i, order_ref: (i, order_ref[j])), any_spec)
                      + (any_spec,) * n_ride,
            scratch_shapes=[pltpu.VMEM((N_DEV, kdim, ncols), BF16),
                            pltpu.SemaphoreType.DMA((7,)), pltpu.SemaphoreType.DMA((7,)),
                            pltpu.SemaphoreType.DMA((N_DEV + 1,))]
                           + (_Gather.scratch(ride_arrs) if n_ride else [])),
        out_shape=(jax.ShapeDtypeStruct((rows, N_DEV * ncols), BF16),
                   jax.ShapeDtypeStruct((N_DEV, kdim, ncols), BF16)) + tuple(ride_shapes),
        compiler_params=pltpu.CompilerParams(vmem_limit_bytes=VMEM_LIMIT),
    )(order, h, w_shard, *ride_arrs)
    return res[0], res[1], res[2:]


SKEW_W = 512


def _bucket_maps():
    lanes = np.arange(SKEW_W)

    def buckets_of(steps):
        rows = []
        for dil in DILATIONS:
            dist = np.maximum(steps, 0) * dil
            nf = np.maximum(dist, 1).astype(np.float32)
            large = 16 + (np.log(nf / np.float32(16)) / np.float32(math.log(128.0))
                          * np.float32(16)).astype(np.int32)
            large = np.minimum(large, N_BUCKETS - 1)
            bucket = np.where(dist < 16, dist, large)
            rows.append(np.where((steps >= 0) & (steps <= N_STEPS), bucket, -1).astype(np.int32))
        return np.stack(rows)[:, None, :]

    a = np.arange(QB)[:, None]
    b = np.arange(2 * QB)[None, :]
    steps = a + QB - b
    band = (steps >= 0) & (steps <= N_STEPS)
    first = band & (b >= QB)
    masks = np.stack([first, band]).astype(np.int32)
    return buckets_of(QB - lanes), buckets_of(2 * QB - 1 - lanes), masks


def _bias_expand(rel_bias, lane_buckets, masks):
    def body(tab_ref, bk_ref, mk_ref, o_ref):
        for g in range(3):
            bk = bk_ref[g]
            for h in range(4):
                col = 4 * g + h
                per_offset = jnp.zeros((1, SKEW_W), F32)
                for k in range(N_BUCKETS):
                    per_offset = jnp.where(bk == k, tab_ref[k, col], per_offset)
                tile = pltpu.roll(jnp.broadcast_to(per_offset, (QB, SKEW_W)), 0, 1, stride=1, stride_axis=0)
                tile = tile[:, :2 * QB]
                o_ref[g, 0, h] = jnp.where(mk_ref[0] != 0, tile, NEG_INF)
                o_ref[g, 1, h] = jnp.where(mk_ref[1] != 0, tile, NEG_INF)

    return pl.pallas_call(
        body, name="bias_expand",
        in_specs=[pl.BlockSpec(memory_space=pltpu.SMEM),
                  pl.BlockSpec(memory_space=pltpu.VMEM),
                  pl.BlockSpec(memory_space=pltpu.VMEM)],
        out_shape=jax.ShapeDtypeStruct((3, 2, 4, QB, 2 * QB), F32),
    )(rel_bias, lane_buckets, masks)


def _bias_grad(ds1, ds2, ds3, lane_buckets):
    exchange = jnp.asarray(np.eye(QB, dtype=np.float32)[::-1].copy())

    def body(d1_ref, d2_ref, d3_ref, bk_ref, ex_ref, o_ref):
        for g, d_ref in enumerate((d1_ref, d2_ref, d3_ref)):
            bk = bk_ref[g]
            for h in range(4):
                flipped = jnp.dot(ex_ref[...], d_ref[h], preferred_element_type=F32,
                                  precision=lax.Precision.HIGHEST)
                padded = jnp.concatenate([flipped, jnp.zeros((QB, SKEW_W - 2 * QB), F32)], axis=1)
                skewed = pltpu.roll(padded, 0, 1, stride=1, stride_axis=0)
                per_offset = jnp.sum(skewed, axis=0, keepdims=True)
                for k in range(N_BUCKETS):
                    o_ref[k, 4 * g + h] = jnp.sum(jnp.where(bk == k, per_offset, 0.0))

    return pl.pallas_call(
        body, name="bias_grad",
        in_specs=[pl.BlockSpec(memory_space=pltpu.VMEM)] * 5,
        out_specs=pl.BlockSpec(memory_space=pltpu.SMEM),
        out_shape=jax.ShapeDtypeStruct((N_BUCKETS, N_HEADS), F32),
    )(ds1, ds2, ds3, lane_buckets, exchange)


def _scratch_sets(rows):
    return 4 if rows <= 512 else 1


def _unit_chunks(dil, size=16):
    units = [(h, r) for h in range(4) for r in range(dil)]
    return [units[i:i + size] for i in range(0, len(units), size)]


def _residue_rows(src_ref, copies, h, residue):
    buf = copies[h % len(copies)]
    buf[...] = src_ref[:, h * HD:(h + 1) * HD].astype(F32)
    return lambda r: buf[residue(r), :].astype(BF16)


def _attn_fwd(proj, bias, g):
    dil = DILATIONS[g]
    rows = QB * dil
    nsb = S // rows
    has_prev = nsb > 1

    def residue(r):
        return pl.ds(r, QB, stride=dil)

    n_sets = _scratch_sets(rows)
    n_in = 6 if has_prev else 4
    n_copied = (4 + (2 if has_prev else 0)) * n_sets

    def body(*refs):
        q_ref, kc_ref, vc_ref = refs[:3]
        kp_ref, vp_ref = refs[3:5] if has_prev else (None, None)
        b_ref = refs[n_in - 1]
        o_ref, l_ref = refs[n_in:n_in + 2]
        scr = list(refs[n_in + 2:])
        ls = [scr.pop(0) for _ in range(4)]
        copies = {name: [scr.pop(0) for _ in range(n_sets)]
                  for name in ("q", "kc", "vc", "o") + (("kp", "vp") if has_prev else ())}
        lane = lax.broadcasted_iota(jnp.int32, (QB, 128), 1)
        refs_of = {"q": q_ref, "kc": kc_ref, "vc": vc_ref, "kp": kp_ref, "vp": vp_ref}
        for chunk in _unit_chunks(dil):
            rows_of = {h: {name: _residue_rows(refs_of[name], copies[name], h, residue)
                           for name in refs_of if refs_of[name] is not None}
                       for h in sorted({h for h, _ in chunk})}

            def batch(name):
                return jnp.stack([rows_of[h][name](r) for h, r in chunk])

            q, k, v = batch("q"), batch("kc"), batch("vc")
            if has_prev:
                k = jnp.concatenate([batch("kp"), k], axis=1)
                v = jnp.concatenate([batch("vp"), v], axis=1)
                bias_b = jnp.stack([b_ref[h] for h, _ in chunk])
            else:
                bias_b = jnp.stack([b_ref[h, :, QB:] for h, _ in chunk])
            s = jnp.einsum("uqd,ukd->uqk", q, k, preferred_element_type=F32) * SCALE + bias_b
            m = jnp.max(s, axis=-1, keepdims=True)
            p = jnp.exp(s - m)
            l = jnp.sum(p, axis=-1, keepdims=True)
            o = jnp.einsum("uqk,ukd->uqd", p.astype(BF16), v, preferred_element_type=F32) / l
            lse = m + jnp.log(l)
            for i, (h, r) in enumerate(chunk):
                copies["o"][h % n_sets][residue(r), :] = o[i]
                ls[h][r * QB:(r + 1) * QB, :] = jnp.where(lane == h, lse[i], 0.0)
            for h in sorted({h for h, _ in chunk}):
                o_ref[:, h * HD:(h + 1) * HD] = copies["o"][h % n_sets][...]
        for r in range(dil):
            blk = slice(r * QB, (r + 1) * QB)
            l_ref[residue(r), :] = (ls[0][blk, :] + ls[1][blk, :]) + (ls[2][blk, :] + ls[3][blk, :])

    def row(b, n):
        return b * nsb + n

    def prev(b, n):
        return b * nsb + jnp.maximum(n - 1, 0)

    in_specs = [
        pl.BlockSpec((rows, GW), lambda b, n: (row(b, n), CB_Q + g)),
        pl.BlockSpec((rows, GW), lambda b, n: (row(b, n), CB_K + g)),
        pl.BlockSpec((rows, GW), lambda b, n: (row(b, n), CB_V + g)),
    ]
    args = [proj, proj, proj]
    scratch = [pltpu.VMEM((rows, 128), F32)] * (4 + n_copied)
    if has_prev:
        in_specs += [pl.BlockSpec((rows, GW), lambda b, n: (prev(b, n), CB_K + g)),
                     pl.BlockSpec((rows, GW), lambda b, n: (prev(b, n), CB_V + g))]
        args += [proj, proj]
    in_specs.append(pl.BlockSpec((None, None, 4, QB, 2 * QB),
                                 lambda b, n: (g, jnp.minimum(n, 1), 0, 0, 0)))
    args.append(bias)
    return pl.pallas_call(
        body, name=f"attn_fwd{g}",
        grid=(BL, nsb),
        in_specs=in_specs,
        out_specs=(pl.BlockSpec((rows, GW), lambda b, n: (row(b, n), 0)),
                   pl.BlockSpec((rows, 128), lambda b, n: (row(b, n), 0))),
        out_shape=(jax.ShapeDtypeStruct((T, GW), F32), jax.ShapeDtypeStruct((T, 128), F32)),
        scratch_shapes=scratch,
        compiler_params=pltpu.CompilerParams(vmem_limit_bytes=VMEM_LIMIT),
    )(*args)


def _attn_bwd(proj, d_out, stats, bias, dproj, g):
    dil = DILATIONS[g]
    rows = QB * dil
    nsb = S // rows
    has_prev = nsb > 1
    n_steps = nsb + 1 if has_prev else 1
    n_in = 7 + (2 if has_prev else 0)

    def residue(r):
        return pl.ds(r, QB, stride=dil)

    n_sets = _scratch_sets(rows)

    def body(*refs):
        q_ref, kc_ref, vc_ref, do_ref, st_ref, b_ref = refs[:6]
        kp_ref, vp_ref = refs[6:8] if has_prev else (None, None)
        out_ref, db_ref = refs[n_in], refs[n_in + 1]
        scr = list(refs[n_in + 2:])
        sq, sk, sv, sems = [scr.pop(0) for _ in range(4)]
        carry = scr.pop(0) if has_prev else None
        sts = scr.pop(0)
        copies = {name: [scr.pop(0) for _ in range(n_sets)]
                  for name in ("q", "kc", "vc", "do", "dq", "dk", "dv") + (("kp", "vp") if has_prev else ())}
        b, n = pl.program_id(0), pl.program_id(1)

        @pl.when((b == 0) & (n == 0))
        def _():
            db_ref[...] = jnp.zeros_like(db_ref)

        def finish(h, r, dq, dk, dv):
            for name, val in (("dq", dq), ("dk", dk), ("dv", dv)):
                copies[name][h % n_sets][residue(r), :] = val

        def finish_head(h):
            sl = slice(h * HD, (h + 1) * HD)
            sq[:, sl] = copies["dq"][h % n_sets][...].astype(BF16)
            sk[:, sl] = copies["dk"][h % n_sets][...].astype(BF16)
            sv[:, sl] = copies["dv"][h % n_sets][...].astype(BF16)

        def write_block(blk_idx):
            row0 = pl.multiple_of(blk_idx * rows, rows)
            _write_columns([(sq, CB * (CB_Q + g)), (sk, CB * (CB_K + g)), (sv, CB * (CB_V + g))],
                           out_ref, row0, sems)

        def carried(h, r):
            blk = slice(r * QB, (r + 1) * QB)
            return ((blk, slice(h * HD, (h + 1) * HD)), (blk, slice(GW + h * HD, GW + (h + 1) * HD)),
                    (blk, slice(2 * GW + h * HD, 2 * GW + (h + 1) * HD)))

        if has_prev:
            @pl.when(n == 0)
            def _():
                carry[...] = jnp.zeros_like(carry)

            @pl.when(n == nsb)
            def _():
                for h in range(4):
                    for r in range(dil):
                        cq, ck, cv = carried(h, r)
                        finish(h, r, carry[cq], carry[ck], carry[cv])
                    finish_head(h)
                write_block(b * nsb + nsb - 1)

        @pl.when(n < nsb)
        def _():
            for r in range(dil):
                sts[r * QB:(r + 1) * QB, :] = st_ref[residue(r), :]
            refs_of = {"q": q_ref, "kc": kc_ref, "vc": vc_ref, "do": do_ref, "kp": kp_ref, "vp": vp_ref}
            for chunk in _unit_chunks(dil):
                heads = sorted({h for h, _ in chunk})
                rows_of = {h: {name: _residue_rows(refs_of[name], copies[name], h, residue)
                               for name in refs_of if refs_of[name] is not None}
                           for h in heads}

                def batch(name):
                    return jnp.stack([rows_of[h][name](r) for h, r in chunk])

                q, k, v, do = batch("q"), batch("kc"), batch("vc"), batch("do")
                if has_prev:
                    k = jnp.concatenate([batch("kp"), k], axis=1)
                    v = jnp.concatenate([batch("vp"), v], axis=1)
                    bias_b = jnp.stack([b_ref[h] for h, _ in chunk])
                else:
                    bias_b = jnp.stack([b_ref[h, :, QB:] for h, _ in chunk])
                lse = jnp.stack([sts[r * QB:(r + 1) * QB, h:h + 1] for h, r in chunk])
                delta = jnp.stack([sts[r * QB:(r + 1) * QB, 4 + h:5 + h] for h, r in chunk])
                s = jnp.einsum("uqd,ukd->uqk", q, k, preferred_element_type=F32) * SCALE + bias_b
                p = jnp.exp(s - lse)
                ds = p * (jnp.einsum("uqd,ukd->uqk", do, v, preferred_element_type=F32) - delta)
                for h in heads:
                    mine = [ds[i] for i, (hh, _) in enumerate(chunk) if hh == h]
                    tot = mine[0]
                    for extra in mine[1:]:
                        tot = tot + extra
                    if has_prev:
                        db_ref[h] += tot
                    else:
                        db_ref[h, :, QB:] += tot
                dsb, pb = ds.astype(BF16), p.astype(BF16)
                dq = jnp.einsum("uqk,ukd->uqd", dsb, k, preferred_element_type=F32) * SCALE
                dk = jnp.einsum("uqk,uqd->ukd", dsb, q, preferred_element_type=F32) * SCALE
                dv = jnp.einsum("uqk,uqd->ukd", pb, do, preferred_element_type=F32)
                for i, (h, r) in enumerate(chunk):
                    if has_prev:
                        cq, ck, cv = carried(h, r)
                        finish(h, r, carry[cq], carry[ck] + dk[i, :QB], carry[cv] + dv[i, :QB])
                        carry[cq] = dq[i]
                        carry[ck] = dk[i, QB:]
                        carry[cv] = dv[i, QB:]
                    else:
                        finish(h, r, dq[i], dk[i], dv[i])
                for h in heads:
                    finish_head(h)
            if has_prev:
                @pl.when(n > 0)
                def _():
                    write_block(b * nsb + n - 1)
            else:
                write_block(b)

    def row(b, n):
        return b * nsb + jnp.minimum(n, nsb - 1)

    def prev(b, n):
        return b * nsb + jnp.maximum(jnp.minimum(n, nsb - 1) - 1, 0)

    in_specs = [
        pl.BlockSpec((rows, GW), lambda b, n: (row(b, n), CB_Q + g)),
        pl.BlockSpec((rows, GW), lambda b, n: (row(b, n), CB_K + g)),
        pl.BlockSpec((rows, GW), lambda b, n: (row(b, n), CB_V + g)),
        pl.BlockSpec((rows, GW), lambda b, n: (row(b, n), 0)),
        pl.BlockSpec((rows, 128), lambda b, n: (row(b, n), 0)),
        pl.BlockSpec((None, None, 4, QB, 2 * QB),
                     lambda b, n: (g, jnp.minimum(jnp.minimum(n, nsb - 1), 1), 0, 0, 0)),
    ]
    args = [proj, proj, proj, d_out, stats, bias]
    scratch = [pltpu.VMEM((rows, GW), BF16)] * 3 + [pltpu.SemaphoreType.DMA((3,))]
    if has_prev:
        in_specs += [pl.BlockSpec((rows, GW), lambda b, n: (prev(b, n), CB_K + g)),
                     pl.BlockSpec((rows, GW), lambda b, n: (prev(b, n), CB_V + g))]
        args += [proj, proj]
        scratch.append(pltpu.VMEM((rows, 3 * GW), F32))
    n_copied = (7 + (2 if has_prev else 0)) * n_sets
    scratch += [pltpu.VMEM((rows, 128), F32)] * (1 + n_copied)
    in_specs.append(pl.BlockSpec(memory_space=pl.ANY))
    args.append(dproj)
    return pl.pallas_call(
        body, name=f"attn_bwd{g}",
        grid=(BL, n_steps),
        in_specs=in_specs,
        out_specs=(pl.BlockSpec(memory_space=pl.ANY),
                   pl.BlockSpec((4, QB, 2 * QB), lambda b, n: (0, 0, 0))),
        out_shape=(jax.ShapeDtypeStruct((T, NCOL), BF16),
                   jax.ShapeDtypeStruct((4, QB, 2 * QB), F32)),
        scratch_shapes=scratch,
        input_output_aliases={len(args) - 1: 0},
        compiler_params=pltpu.CompilerParams(vmem_limit_bytes=VMEM_LIMIT),
    )(*args)


def _tail(x2, tgt2, mod3, o_g, lse_g, proj, w_ao, w_co, w_o, conv_w, conv_b, ln_g, ln_b):
    tm = 256
    per_seq = S // tm
    halo = 16

    def body(x_ref, t_ref, mod_ref, o1_ref, o2_ref, o3_ref, l1_ref, l2_ref, l3_ref,
             ga_ref, u_ref, bg_ref, cg_ref, gc_ref, ma_ref, mc_ref, up_ref, cp_ref,
             wao_ref, wco_ref, wo_ref, cw_ref, cb_ref, lg_ref, lb_ref,
             dproj_ref, dyc_ref, do_ref, st_ref, dxd_ref,
             gwo_ref, gwco_ref, gwao_ref, vec_ref,
             dga_s, dbg_s, dgm_s, sems, acc_o, acc_co, acc_ao):
        i = pl.program_id(0)
        bidx = i // per_seq
        first = (i % per_seq) == 0

        @pl.when(i == 0)
        def _():
            vec_ref[...] = jnp.zeros_like(vec_ref)

        l1, l2, l3 = l1_ref[...], l2_ref[...], l3_ref[...]
        mx = jnp.maximum(jnp.maximum(l1, l2), l3)
        e1, e2, e3 = jnp.exp(l1 - mx), jnp.exp(l2 - mx), jnp.exp(l3 - mx)
        esum = e1 + e2 + e3
        lse_tot = mx + jnp.log(esum)
        w1, w2, w3 = e1 / esum, e2 / esum, e3 / esum

        def per_head(wv):
            return jnp.concatenate([jnp.broadcast_to(wv[:, h:h + 1], (tm, HD)) for h in range(4)], axis=1)

        o = per_head(w1) * o1_ref[...] + per_head(w2) * o2_ref[...] + per_head(w3) * o3_ref[...]

        ga = ga_ref[...].astype(F32)
        sig_ga = _sigmoid(ga)
        silu_ga = ga * sig_ga
        a_in = (o * silu_ga).astype(BF16)
        a_out = _dot(a_in, wao_ref[...])

        u = u_ref[...].astype(F32)
        cg = cg_ref[...].astype(F32)
        z = cg * u
        zp = cp_ref[...].astype(F32) * up_ref[...].astype(F32)
        zp = jnp.where(first, 0.0, zp)
        zcat = jnp.concatenate([zp, z], axis=0)
        z1 = pltpu.roll(zcat, 1, 0)[halo:]
        z2 = pltpu.roll(zcat, 2, 0)[halo:]
        y_conv = cw_ref[0:1, :] * z2 + cw_ref[1:2, :] * z1 + cw_ref[2:3, :] * z + cb_ref[...]
        gc = gc_ref[...].astype(F32)
        sig_gc = _sigmoid(gc)
        silu_gc = gc * sig_gc
        bg = bg_ref[...].astype(F32)
        bg_yc = bg * y_conv
        s_in = (bg_yc * silu_gc).astype(BF16)
        s_out = _dot(s_in, wco_ref[...])

        sa = _sigmoid(ma_ref[...].astype(F32))
        sc = _sigmoid(mc_ref[...].astype(F32))
        merged = (sa * a_out + sc * s_out).astype(BF16)
        y = _dot(merged, wo_ref[...])
        gate1 = 1.0 + mod_ref[0, 2:3, :]
        xv = x_ref[...]
        resid = ALPHA * xv + gate1 * y
        mu = jnp.mean(resid, axis=1, keepdims=True)
        xc = resid - mu
        var = jnp.mean(xc * xc, axis=1, keepdims=True)
        rstd = lax.rsqrt(var + LN_EPS)
        xhat = xc * rstd
        lg = lg_ref[...]
        err = xhat * lg + lb_ref[...] - t_ref[...]
        vec_ref[3:4, :] += (0.5 / D) * jnp.sum(err * err, axis=0, keepdims=True)

        vec_ref[1:2, :] += (1.0 / D) * jnp.sum(err * xhat, axis=0, keepdims=True)
        vec_ref[2:3, :] += (1.0 / D) * jnp.sum(err, axis=0, keepdims=True)
        dxh = err * (lg * (1.0 / D))
        dres = rstd * (dxh - jnp.mean(dxh, axis=1, keepdims=True)
                       - xhat * jnp.mean(dxh * xhat, axis=1, keepdims=True))
        dxd_ref[...] = ALPHA * dres
        dgate = jnp.sum(dres * y, axis=0, keepdims=True)
        vec_ref[4:5, :] += jnp.where(bidx == 0, dgate, 0.0)
        vec_ref[5:6, :] += jnp.where(bidx == 1, dgate, 0.0)
        dy = (dres * gate1).astype(BF16)

        dmerged = _dot_nt(dy, wo_ref[...])
        da_out_f = dmerged * sa
        ds_out_f = dmerged * sc
        da_out = da_out_f.astype(BF16)
        ds_out = ds_out_f.astype(BF16)
        dgm_s[:, 2 * D:3 * D] = (ds_out_f * s_out * (1.0 - sc)).astype(BF16)
        dgm_s[:, D:2 * D] = (da_out_f * a_out * (1.0 - sa)).astype(BF16)
        da_in = _dot_nt(da_out, wao_ref[...])
        ds_in = _dot_nt(ds_out, wco_ref[...])

        d_o = da_in * silu_ga
        do_ref[...] = d_o.astype(BF16)
        dga_s[...] = (da_in * o * (sig_ga + silu_ga * (1.0 - sig_ga))).astype(BF16)
        lane = lax.broadcasted_iota(jnp.int32, (tm, 128), 1)
        stats = lse_tot
        od = o * d_o
        for h in range(4):
            delta = jnp.sum(od[:, h * HD:(h + 1) * HD], axis=1, keepdims=True)
            stats = jnp.where(lane == 4 + h, delta, stats)
        st_ref[...] = stats

        ds_silu = ds_in * silu_gc
        dbg_s[...] = (ds_silu * y_conv).astype(BF16)
        dyc = ds_silu * bg
        dyc_ref[...] = dyc
        vec_ref[0:1, :] += jnp.sum(dyc, axis=0, keepdims=True)
        dgm_s[:, 0:D] = (ds_in * bg_yc * (sig_gc + silu_gc * (1.0 - sig_gc))).astype(BF16)

        @pl.when(i == 0)
        def _():
            acc_o[...] = jnp.zeros_like(acc_o)
            acc_co[...] = jnp.zeros_like(acc_co)
            acc_ao[...] = jnp.zeros_like(acc_ao)

        acc_o[...] += _dot_tn(merged, dy)
        acc_co[...] += _dot_tn(s_in, ds_out)
        acc_ao[...] += _dot_tn(a_in, da_out)

        @pl.when(i == T // tm - 1)
        def _():
            gwo_ref[...] = acc_o[...].astype(BF16)
            gwco_ref[...] = acc_co[...].astype(BF16)
            gwao_ref[...] = acc_ao[...].astype(BF16)

        _write_columns([(dga_s, CB * CB_GA), (dbg_s, D * KB_BG), (dgm_s, D * KB_GC)],
                       dproj_ref, pl.multiple_of(i * tm, tm), sems)

    def tile(width, cblk=0):
        return pl.BlockSpec((tm, width), lambda i: (i, cblk))

    def whole(shape):
        return pl.BlockSpec(shape, lambda i: tuple(0 for _ in shape))

    def once(shape):
        return pl.BlockSpec(shape, lambda i: tuple(0 for _ in shape), pipeline_mode=pl.Buffered(1))

    prev_rows = lambda i: (jnp.maximum(i * (tm // halo) - 1, 0),)
    in_specs = [
        tile(D), tile(D), pl.BlockSpec((1, 3, D), lambda i: (i // per_seq, 0, 0)),
        tile(GW), tile(GW), tile(GW), tile(128), tile(128), tile(128),
        tile(GW, CB_GA), tile(D, KB_U), tile(D, KB_BG), tile(D, KB_CG), tile(D, KB_GC),
        tile(D, KB_MA), tile(D, KB_MC),
        pl.BlockSpec((halo, D), lambda i: (*prev_rows(i), KB_U)),
        pl.BlockSpec((halo, D), lambda i: (*prev_rows(i), KB_CG)),
        whole((GW, D)), whole((D, D)), whole((D, D)),
        whole((3, D)), whole((1, D)), whole((1, D)), whole((1, D)),
    ]
    out_specs = (
        pl.BlockSpec(memory_space=pl.ANY), tile(D), tile(GW), tile(128), tile(D),
        once((D, D)), once((D, D)), once((GW, D)),
        pl.BlockSpec((8, D), lambda i: (0, 0)),
    )
    out_shape = (
        jax.ShapeDtypeStruct((T, NCOL), BF16),
        jax.ShapeDtypeStruct((T, D), F32),
        jax.ShapeDtypeStruct((T, GW), BF16),
        jax.ShapeDtypeStruct((T, 128), F32),
        jax.ShapeDtypeStruct((T, D), F32),
        jax.ShapeDtypeStruct((D, D), BF16),
        jax.ShapeDtypeStruct((D, D), BF16),
        jax.ShapeDtypeStruct((GW, D), BF16),
        jax.ShapeDtypeStruct((8, D), F32),
    )
    return pl.pallas_call(
        body, name="tail",
        grid=(T // tm,),
        in_specs=in_specs, out_specs=out_specs, out_shape=out_shape,
        scratch_shapes=[pltpu.VMEM((tm, GW), BF16), pltpu.VMEM((tm, D), BF16), pltpu.VMEM((tm, 3 * D), BF16),
                        pltpu.SemaphoreType.DMA((3,)),
                        pltpu.VMEM((D, D), F32), pltpu.VMEM((D, D), F32), pltpu.VMEM((GW, D), F32)],
        compiler_params=pltpu.CompilerParams(vmem_limit_bytes=VMEM_LIMIT_TAIL),
    )(x2, tgt2, mod3, *o_g, *lse_g, proj, proj, proj, proj, proj, proj, proj, proj, proj,
      w_ao, w_co, w_o, conv_w, conv_b, ln_g, ln_b)


def _conv_bwd(dyc, proj, conv_w, dproj):
    tm = 512
    per_seq = S // tm

    def body(d_ref, dn_ref, u_ref, c_ref, cw_ref, _, dproj_ref, g_ref, du_s, dc_s, sems):
        i = pl.program_id(0)
        last = (i % per_seq) == per_seq - 1

        @pl.when(i == 0)
        def _():
            g_ref[...] = jnp.zeros_like(g_ref)

        d = d_ref[...]
        dn = jnp.where(last, 0.0, dn_ref[...])
        dcat = jnp.concatenate([d, dn], axis=0)
        d1 = pltpu.roll(dcat, tm + 8 - 1, 0)[:tm]
        d2 = pltpu.roll(dcat, tm + 8 - 2, 0)[:tm]
        dz = cw_ref[2:3, :] * d + cw_ref[1:2, :] * d1 + cw_ref[0:1, :] * d2
        u = u_ref[...].astype(F32)
        cg = c_ref[...].astype(F32)
        du_s[...] = (dz * cg).astype(BF16)
        dc_s[...] = (dz * u).astype(BF16)
        _write_columns([(du_s, D * KB_U), (dc_s, D * KB_CG)], dproj_ref, pl.multiple_of(i * tm, tm), sems)

        z = cg * u
        g_ref[0:1, :] += jnp.sum(d2 * z, axis=0, keepdims=True)
        g_ref[1:2, :] += jnp.sum(d1 * z, axis=0, keepdims=True)
        g_ref[2:3, :] += jnp.sum(d * z, axis=0, keepdims=True)

    n_tiles = T // tm
    next_rows = lambda i: jnp.minimum((i + 1) * (tm // 8), T // 8 - 1)
    return pl.pallas_call(
        body, name="conv_bwd",
        grid=(n_tiles,),
        in_specs=[pl.BlockSpec((tm, D), lambda i: (i, 0)),
                  pl.BlockSpec((8, D), lambda i: (next_rows(i), 0)),
                  pl.BlockSpec((tm, D), lambda i: (i, KB_U)),
                  pl.BlockSpec((tm, D), lambda i: (i, KB_CG)),
                  pl.BlockSpec((3, D), lambda i: (0, 0)),
                  pl.BlockSpec(memory_space=pl.ANY)],
        out_specs=(pl.BlockSpec(memory_space=pl.ANY),
                   pl.BlockSpec((8, D), lambda i: (0, 0))),
        out_shape=(jax.ShapeDtypeStruct((T, NCOL), BF16),
                   jax.ShapeDtypeStruct((8, D), F32)),
        scratch_shapes=[pltpu.VMEM((tm, D), BF16), pltpu.VMEM((tm, D), BF16), pltpu.SemaphoreType.DMA((2,))],
        input_output_aliases={5: 0},
        compiler_params=pltpu.CompilerParams(vmem_limit_bytes=VMEM_LIMIT),
    )(dyc, dyc, proj, proj, conv_w, dproj)


def _dh_dx(dproj, w_in_all, x2, dxd, mod3, chip_sums, hops=(), parts0=None):
    tm = 1024
    per_seq = S // tm
    n = len(chip_sums)
    n_in = 5 + n + (0 if parts0 is None else 1)

    def body(*refs):
        d_ref, w_ref, x_ref, dxd_ref, mod_ref = refs[:5]
        ins = refs[5:5 + n]
        gx_ref, vec_ref = refs[n_in:n_in + 2]
        outs = refs[n_in + 2:n_in + 2 + n]
        acc, send_sems, recv_sems, local_sems = refs[n_in + 2 + n:]
        i, jj = pl.program_id(0), pl.program_id(1)

        @pl.when((i == 0) & (jj == 0))
        def _():
            vec_ref[...] = jnp.zeros_like(vec_ref)
            if n:
                sends, _, mine = _chip_copies(ins, outs, send_sems, recv_sems, local_sems, hops)
                for cp in sends + mine:
                    cp.start()

        if n:
            @pl.when((i == T // tm - 1) & (jj == N_DEV - 1))
            def _():
                sends, arrivals, mine = _chip_copies(ins, outs, send_sems, recv_sems, local_sems, hops)
                for cp in arrivals:
                    cp.wait_recv()
                for cp in sends:
                    cp.wait_send()
                for cp in mine:
                    cp.wait()

        @pl.when(jj == 0)
        def _():
            acc[...] = jnp.zeros_like(acc)

        acc[...] += _dot_nt(d_ref[...], w_ref[...])

        @pl.when(jj == N_DEV - 1)
        def _():
            dh = acc[...]
            bidx = i // per_seq
            gx_ref[...] = dxd_ref[...] + dh * (1.0 + mod_ref[0, 1:2, :])
            dshift = jnp.sum(dh, axis=0, keepdims=True)
            dscale = jnp.sum(dh * x_ref[...], axis=0, keepdims=True)
            vec_ref[0:1, :] += jnp.where(bidx == 0, dshift, 0.0)
            vec_ref[1:2, :] += jnp.where(bidx == 1, dshift, 0.0)
            vec_ref[2:3, :] += jnp.where(bidx == 0, dscale, 0.0)
            vec_ref[3:4, :] += jnp.where(bidx == 1, dscale, 0.0)

    any_spec = pl.BlockSpec(memory_space=pl.ANY)
    res = pl.pallas_call(
        body, name="dh_dx",
        grid=(T // tm, N_DEV),
        in_specs=[
            pl.BlockSpec((tm, SHARD), lambda i, jj: (i, jj)),
            pl.BlockSpec((None, D, SHARD), lambda i, jj: (jj, 0, 0)),
            pl.BlockSpec((tm, D), lambda i, jj: (i, 0)),
            pl.BlockSpec((tm, D), lambda i, jj: (i, 0)),
            pl.BlockSpec((1, 3, D), lambda i, jj: (i // per_seq, 0, 0))] + [any_spec] * (n_in - 5),
        out_specs=(pl.BlockSpec((tm, D), lambda i, jj: (i, 0)),
                   pl.BlockSpec((8, D), lambda i, jj: (0, 0))) + (any_spec,) * n,
        out_shape=(jax.ShapeDtypeStruct((T, D), F32), jax.ShapeDtypeStruct((8, D), F32))
                  + tuple(jax.ShapeDtypeStruct(a.shape, a.dtype) for a in chip_sums),
        scratch_shapes=[pltpu.VMEM((tm, D), F32), pltpu.SemaphoreType.DMA((max(3 * n, 1),)),
                        pltpu.SemaphoreType.DMA((max(3 * n, 1),)), pltpu.SemaphoreType.DMA((max(n, 1),))],
        input_output_aliases={} if parts0 is None else {5 + n: 2},
        compiler_params=pltpu.CompilerParams(vmem_limit_bytes=VMEM_LIMIT),
    )(dproj, w_in_all, x2, dxd, mod3, *chip_sums, *([] if parts0 is None else [parts0]))
    return res[0], res[1], res[2:]


def _adam_step(g, w, m, v):
    nm = ADAM_B1 * m + (1.0 - ADAM_B1) * g
    nv = ADAM_B2 * v + (1.0 - ADAM_B2) * (g * g)
    m_hat = nm / (1.0 - ADAM_B1 ** ADAM_STEP)
    v_hat = nv / (1.0 - ADAM_B2 ** ADAM_STEP)
    return -ADAM_LR * (m_hat / (jnp.sqrt(v_hat) + ADAM_EPS) + ADAM_WD * w), nm, nv


def _adamw(parts, w, m, v, name, row_tile=None):
    n_parts, rows, cols = parts.shape
    tr = rows if row_tile is None else row_tile

    def body(p_ref, w_ref, m_ref, v_ref, g_ref, d_ref, nm_ref, nv_ref):
        g = p_ref[0].astype(F32)
        for s in range(1, n_parts):
            g = g + p_ref[s].astype(F32)
        g_ref[...] = g
        d_ref[...], nm_ref[...], nv_ref[...] = _adam_step(g, w_ref[...], m_ref[...], v_ref[...])

    blk = pl.BlockSpec((tr, cols), lambda i: (i, 0))
    shp = jax.ShapeDtypeStruct((rows, cols), F32)
    return pl.pallas_call(
        body, name=name,
        grid=(rows // tr,),
        in_specs=[pl.BlockSpec((n_parts, tr, cols), lambda i: (0, i, 0)), blk, blk, blk],
        out_specs=(blk, blk, blk, blk),
        out_shape=(shp, shp, shp, shp),
        compiler_params=pltpu.CompilerParams(vmem_limit_bytes=VMEM_LIMIT),
    )(parts, w, m, v)


def _multi_adamw(parts_list, params, name):
    n = len(params)
    flat = [t for wmv in params for t in wmv]

    def body(*refs):
        parts, ins, outs = refs[:n], refs[n:4 * n], refs[4 * n:]
        for p in range(n):
            g = parts[p][0].astype(F32)
            for s in range(1, parts[p].shape[0]):
                g = g + parts[p][s].astype(F32)
            w_ref, m_ref, v_ref = ins[3 * p:3 * p + 3]
            g_ref, d_ref, nm_ref, nv_ref = outs[4 * p:4 * p + 4]
            g_ref[...] = g
            d_ref[...], nm_ref[...], nv_ref[...] = _adam_step(g, w_ref[...], m_ref[...], v_ref[...])

    out_shape = []
    for w, _, _ in params:
        out_shape += [jax.ShapeDtypeStruct(w.shape, F32)] * 4
    res = pl.pallas_call(body, name=name, out_shape=tuple(out_shape))(*parts_list, *flat)
    return [res[4 * p:4 * p + 4] for p in range(n)]


def _small_updates(small_g, dmod_all, rel_parts, params):
    flat = [t for wmv in params for t in wmv]

    def body(sg_ref, dm_ref, rp_ref, *refs):
        ins, outs = refs[:len(flat)], refs[len(flat):]

        def over_devices(row):
            tot = sg_ref[0, row:row + 1, :]
            for s in range(1, N_DEV):
                tot = tot + sg_ref[s, row:row + 1, :]
            return tot

        g_b_ada = dm_ref[0:1, :]
        for r in range(1, N_DEV * BL):
            g_b_ada = g_b_ada + dm_ref[r:r + 1, :]
        g_rel = rp_ref[0]
        for s in range(1, N_DEV):
            g_rel = g_rel + rp_ref[s]
        grads = [g_b_ada, over_devices(0), g_rel, over_devices(1), over_devices(2)]
        outs[0][...] = jnp.sum(over_devices(3), axis=1, keepdims=True)
        for p, g in enumerate(grads):
            w_ref, m_ref, v_ref = ins[3 * p:3 * p + 3]
            g_ref, d_ref, nm_ref, nv_ref = outs[1 + 4 * p:5 + 4 * p]
            g_ref[...] = g
            d_ref[...], nm_ref[...], nv_ref[...] = _adam_step(g, w_ref[...], m_ref[...], v_ref[...])

    out_shape = [jax.ShapeDtypeStruct((1, 1), F32)]
    for w, _, _ in params:
        out_shape += [jax.ShapeDtypeStruct(w.shape, F32)] * 4
    res = pl.pallas_call(body, name="small_updates", out_shape=tuple(out_shape))(small_g, dmod_all, rel_parts, *flat)
    return res[0], [res[1 + 4 * p:5 + 4 * p] for p in range(len(params))]


def _attn_fwd_dense(proj, bias):
    nq = 4
    rows = nq * QB
    nsb = S // rows

    def body(q_ref, k_ref, v_ref, kp_ref, vp_ref, b_ref, o_ref, l_ref, ls0, ls1, ls2, ls3):
        ls = [ls0, ls1, ls2, ls3]
        n = pl.program_id(1)
        lane = lax.broadcasted_iota(jnp.int32, (QB, 128), 1)
        units = [(h, j) for h in range(4) for j in range(nq)]

        def keys(cur_ref, prev_ref, h, j):
            sl = slice(h * HD, (h + 1) * HD)
            if j == 0:
                return jnp.concatenate([prev_ref[:, sl], cur_ref[0:QB, sl]], axis=0)
            return cur_ref[(j - 1) * QB:(j + 1) * QB, sl]

        q = jnp.stack([q_ref[j * QB:(j + 1) * QB, h * HD:(h + 1) * HD] for h, j in units])
        k = jnp.stack([keys(k_ref, kp_ref, h, j) for h, j in units])
        v = jnp.stack([keys(v_ref, vp_ref, h, j) for h, j in units])
        bias_b = jnp.stack([b_ref[jnp.minimum(n, 1), h] if j == 0 else b_ref[1, h] for h, j in units])
        s = jnp.einsum("uqd,ukd->uqk", q, k, preferred_element_type=F32) * SCALE + bias_b
        m = jnp.max(s, axis=-1, keepdims=True)
        p = jnp.exp(s - m)
        l = jnp.sum(p, axis=-1, keepdims=True)
        o = jnp.einsum("uqk,ukd->uqd", p.astype(BF16), v, preferred_element_type=F32) / l
        lse = m + jnp.log(l)
        for i, (h, j) in enumerate(units):
            o_ref[j * QB:(j + 1) * QB, h * HD:(h + 1) * HD] = o[i]
            ls[h][j * QB:(j + 1) * QB, :] = jnp.where(lane == h, lse[i], 0.0)
        l_ref[...] = (ls[0][...] + ls[1][...]) + (ls[2][...] + ls[3][...])

    def row(b, n):
        return b * nsb + n

    def prev(b, n):
        return jnp.maximum((b * nsb + n) * nq - 1, 0)

    in_specs = [
        pl.BlockSpec((rows, GW), lambda b, n: (row(b, n), CB_Q)),
        pl.BlockSpec((rows, GW), lambda b, n: (row(b, n), CB_K)),
        pl.BlockSpec((rows, GW), lambda b, n: (row(b, n), CB_V)),
        pl.BlockSpec((QB, GW), lambda b, n: (prev(b, n), CB_K)),
        pl.BlockSpec((QB, GW), lambda b, n: (prev(b, n), CB_V)),
        pl.BlockSpec((None, 2, 4, QB, 2 * QB), lambda b, n: (0, 0, 0, 0, 0)),
    ]
    return pl.pallas_call(
        body, name="attn_fwd0",
        grid=(BL, nsb),
        in_specs=in_specs,
        out_specs=(pl.BlockSpec((rows, GW), lambda b, n: (row(b, n), 0)),
                   pl.BlockSpec((rows, 128), lambda b, n: (row(b, n), 0))),
        out_shape=(jax.ShapeDtypeStruct((T, GW), F32), jax.ShapeDtypeStruct((T, 128), F32)),
        scratch_shapes=[pltpu.VMEM((rows, 128), F32)] * 4,
        compiler_params=pltpu.CompilerParams(vmem_limit_bytes=VMEM_LIMIT),
    )(proj, proj, proj, proj, proj, bias)


def _attn_bwd_dense(proj, d_out, stats, bias, dproj):
    nq = 4
    rows = nq * QB
    nsb = S // rows
    cols_q, cols_k, cols_v = CB * CB_Q, CB * CB_K, CB * CB_V

    def body(q_ref, k_ref, v_ref, do_ref, st_ref, kp_ref, vp_ref, b_ref, _, out_ref, db_ref,
             sq, sk, sv, carry, sems):
        b, n = pl.program_id(0), pl.program_id(1)
        units = [(h, j) for h in range(4) for j in range(nq)]

        @pl.when((b == 0) & (n == 0))
        def _():
            db_ref[...] = jnp.zeros_like(db_ref)

        @pl.when(n == 0)
        def _():
            carry[...] = jnp.zeros_like(carry)

        def write(first_block, position, count):
            row0 = pl.multiple_of(first_block * QB, QB)
            part = pl.ds(position * QB, count * QB)
            _write_columns([(sq.at[part], cols_q), (sk.at[part], cols_k), (sv.at[part], cols_v)],
                           out_ref, row0, sems)

        @pl.when(n == nsb)
        def _():
            sq[0:QB, :] = carry[:, 0:GW].astype(BF16)
            sk[0:QB, :] = carry[:, GW:2 * GW].astype(BF16)
            sv[0:QB, :] = carry[:, 2 * GW:3 * GW].astype(BF16)
            write((b + 1) * nsb * nq - 1, 0, 1)

        @pl.when(n < nsb)
        def _():
            def keys(cur_ref, prev_ref, h, j):
                sl = slice(h * HD, (h + 1) * HD)
                if j == 0:
                    return jnp.concatenate([prev_ref[:, sl], cur_ref[0:QB, sl]], axis=0)
                return cur_ref[(j - 1) * QB:(j + 1) * QB, sl]

            def block(ref, h, j):
                return ref[j * QB:(j + 1) * QB, h * HD:(h + 1) * HD]

            q = jnp.stack([block(q_ref, h, j) for h, j in units])
            do = jnp.stack([block(do_ref, h, j) for h, j in units])
            k = jnp.stack([keys(k_ref, kp_ref, h, j) for h, j in units])
            v = jnp.stack([keys(v_ref, vp_ref, h, j) for h, j in units])
            bias_b = jnp.stack([b_ref[jnp.minimum(n, 1), h] if j == 0 else b_ref[1, h] for h, j in units])
            lse = jnp.stack([st_ref[j * QB:(j + 1) * QB, h:h + 1] for h, j in units])
            delta = jnp.stack([st_ref[j * QB:(j + 1) * QB, 4 + h:5 + h] for h, j in units])
            s = jnp.einsum("uqd,ukd->uqk", q, k, preferred_element_type=F32) * SCALE + bias_b
            p = jnp.exp(s - lse)
            ds = p * (jnp.einsum("uqd,ukd->uqk", do, v, preferred_element_type=F32) - delta)
            for h in range(4):
                tot = ds[h * nq]
                for j in range(1, nq):
                    tot = tot + ds[h * nq + j]
                db_ref[h] += tot
            dsb, pb = ds.astype(BF16), p.astype(BF16)
            dq = jnp.einsum("uqk,ukd->uqd", dsb, k, preferred_element_type=F32) * SCALE
            dk = jnp.einsum("uqk,uqd->ukd", dsb, q, preferred_element_type=F32) * SCALE
            dv = jnp.einsum("uqk,uqd->ukd", pb, do, preferred_element_type=F32)
            for h in range(4):
                sl = slice(h * HD, (h + 1) * HD)
                u0, last = h * nq, h * nq + nq - 1
                sq[0:QB, sl] = carry[:, sl].astype(BF16)
                sk[0:QB, sl] = (carry[:, GW + h * HD:GW + (h + 1) * HD] + dk[u0, :QB]).astype(BF16)
                sv[0:QB, sl] = (carry[:, 2 * GW + h * HD:2 * GW + (h + 1) * HD] + dv[u0, :QB]).astype(BF16)
                for j in range(nq - 1):
                    pos = slice((j + 1) * QB, (j + 2) * QB)
                    sq[pos, sl] = dq[u0 + j].astype(BF16)
                    sk[pos, sl] = (dk[u0 + j, QB:] + dk[u0 + j + 1, :QB]).astype(BF16)
                    sv[pos, sl] = (dv[u0 + j, QB:] + dv[u0 + j + 1, :QB]).astype(BF16)
                carry[:, sl] = dq[last]
                carry[:, GW + h * HD:GW + (h + 1) * HD] = dk[last, QB:]
                carry[:, 2 * GW + h * HD:2 * GW + (h + 1) * HD] = dv[last, QB:]

            @pl.when(n == 0)
            def _():
                write(b * nsb * nq, 1, nq - 1)

            @pl.when(n > 0)
            def _():
                write((b * nsb + n) * nq - 1, 0, nq)

    def row(b, n):
        return b * nsb + jnp.minimum(n, nsb - 1)

    def prev(b, n):
        return jnp.maximum(row(b, n) * nq - 1, 0)

    in_specs = [
        pl.BlockSpec((rows, GW), lambda b, n: (row(b, n), CB_Q)),
        pl.BlockSpec((rows, GW), lambda b, n: (row(b, n), CB_K)),
        pl.BlockSpec((rows, GW), lambda b, n: (row(b, n), CB_V)),
        pl.BlockSpec((rows, GW), lambda b, n: (row(b, n), 0)),
        pl.BlockSpec((rows, 128), lambda b, n: (row(b, n), 0)),
        pl.BlockSpec((QB, GW), lambda b, n: (prev(b, n), CB_K)),
        pl.BlockSpec((QB, GW), lambda b, n: (prev(b, n), CB_V)),
        pl.BlockSpec((None, 2, 4, QB, 2 * QB), lambda b, n: (0, 0, 0, 0, 0)),
        pl.BlockSpec(memory_space=pl.ANY),
    ]
    return pl.pallas_call(
        body, name="attn_bwd0",
        grid=(BL, nsb + 1),
        in_specs=in_specs,
        out_specs=(pl.BlockSpec(memory_space=pl.ANY),
                   pl.BlockSpec((4, QB, 2 * QB), lambda b, n: (0, 0, 0))),
        out_shape=(jax.ShapeDtypeStruct((T, NCOL), BF16),
                   jax.ShapeDtypeStruct((4, QB, 2 * QB), F32)),
        scratch_shapes=[pltpu.VMEM((rows, GW), BF16)] * 3
                       + [pltpu.VMEM((QB, 3 * GW), F32), pltpu.SemaphoreType.DMA((3,))],
        input_output_aliases={8: 0},
        compiler_params=pltpu.CompilerParams(vmem_limit_bytes=VMEM_LIMIT),
    )(proj, proj, proj, d_out, stats, proj, proj, bias, dproj)


def _attention_forward(proj, rel_bias):
    expand_lanes, grad_lanes, masks = (jnp.asarray(t) for t in _bucket_maps())
    bias = _bias_expand(rel_bias, expand_lanes, masks)
    fwd = [_attn_fwd_dense(proj, bias)] + [_attn_fwd(proj, bias, g) for g in (1, 2)]
    return bias, grad_lanes, [f[0] for f in fwd], [f[1] for f in fwd]


def _local_step(x2, tgt2, mod3, h, proj, attn, w_ao, w_co, w_o, conv_w, conv_b, ln_g, ln_b):
    bias, buckets, o_g, lse_g = attn

    (dproj, dyc, d_o, stats, dxd, gw_o, gw_co, gw_ao, tail_vec) = _tail(
        x2, tgt2, mod3, o_g, lse_g, proj, w_ao, w_co, w_o, conv_w, conv_b, ln_g, ln_b)

    dproj, db = _attn_bwd_dense(proj, d_o, stats, bias, dproj)
    dbias = [db]
    for g in (1, 2):
        dproj, db = _attn_bwd(proj, d_o, stats, bias, dproj, g)
        dbias.append(db)
    g_rel_bias = _bias_grad(*dbias, buckets)
    dproj, conv_vec = _conv_bwd(dyc, proj, conv_w, dproj)

    gw_ao = jnp.transpose(gw_ao.reshape(GW, N_DEV, D // N_DEV), (1, 0, 2))
    return dproj, dxd, gw_ao, gw_co, gw_o, conv_vec, g_rel_bias, tail_vec


def kernel(x, c, w_ada, b_ada, w_in, conv_w, conv_b, rel_bias, w_attn_out, w_conv_out, w_o, ln_g, ln_b, loss_target, m_w_ada, m_b_ada, m_w_in, m_conv_w, m_conv_b, m_rel_bias, m_w_attn_out, m_w_conv_out, m_w_o, m_ln_g, m_ln_b, v_w_ada, v_b_ada, v_w_in, v_conv_w, v_conv_b, v_rel_bias, v_w_attn_out, v_w_conv_out, v_w_o, v_ln_g, v_ln_b):
    me = _my_index()
    x2 = x.reshape(T, D)
    tgt2 = loss_target.reshape(T, D)

    b_cols = lax.dynamic_slice(b_ada, (0, me * ADA_SHARD), (1, ADA_SHARD))
    c_g, mod_in = _mod_exchange(jnp.pad(c, ((0, 8 - BL), (0, 0))), w_ada[0], b_cols)
    c_all = c_g[:, 0:BL, :].reshape(N_DEV * BL, D)
    mod3 = jnp.transpose(mod_in[:, 0:BL, :], (1, 0, 2)).reshape(BL, 3, D)

    h = _prep_h(x2, mod3)
    rows_shape = jax.ShapeDtypeStruct((N_DEV, D // N_DEV, D), BF16)
    proj, w_in_all, (w_ao_g, w_co_g, w_o_g, conv_w_g) = _gather_proj(
        _shard_order(), h, w_in[0].astype(BF16), 1024,
        ([w_attn_out[0].astype(BF16), w_conv_out[0].astype(BF16), w_o[0].astype(BF16), conv_w[0]],
         [jax.ShapeDtypeStruct((N_DEV, GW, D // N_DEV), BF16), rows_shape, rows_shape,
          jax.ShapeDtypeStruct((N_DEV, 3, D // N_DEV), F32)]))

    attn = _attention_forward(proj, rel_bias)
    w_ao_full = jnp.transpose(w_ao_g, (1, 0, 2)).reshape(GW, D)
    w_co_full = w_co_g.reshape(D, D)
    w_o_full = w_o_g.reshape(D, D)
    conv_w_full = jnp.transpose(conv_w_g, (1, 0, 2)).reshape(3, D)

    (dproj, dxd, gw_ao, gw_co, gw_o, conv_vec, g_rel_bias, tail_vec) = _local_step(
        x2, tgt2, mod3, h, proj, attn, w_ao_full, w_co_full, w_o_full,
        conv_w_full, conv_b, ln_g, ln_b)

    g_conv_w_blocks = jnp.transpose(conv_vec[0:3].reshape(3, N_DEV, D // N_DEV), (1, 0, 2))
    partials = [gw_ao, gw_co.reshape(N_DEV, D // N_DEV, D), gw_o.reshape(N_DEV, D // N_DEV, D), g_conv_w_blocks]
    w_in_sums, w_in_parts, sib = _gw_in_pair(
        _slice_order(), h, dproj, partials,
        [jax.ShapeDtypeStruct((4, GW, D // N_DEV), BF16),
         jax.ShapeDtypeStruct((4, D // N_DEV, D), BF16),
         jax.ShapeDtypeStruct((4, D // N_DEV, D), BF16),
         jax.ShapeDtypeStruct((4, 3, D // N_DEV), F32)])
    core = lax.axis_index("c").astype(jnp.int32).reshape(1)
    chip_sums = [w_in_sums] + list(_pair_add(core, partials, sib))
    hops = [(3,)] + [(1, 2, 3)] * 4
    grad_x, mod_vec, (r_in, r_ao, r_co, r_o, r_cw) = _dh_dx(
        dproj, w_in_all, x2, dxd, mod3, chip_sums, hops, w_in_parts)

    small = jnp.concatenate([
        tail_vec[0:4],
        jnp.pad(g_rel_bias.reshape(1, N_BUCKETS * N_HEADS), ((0, 0), (0, D - N_BUCKETS * N_HEADS))),
        jnp.zeros((3, D), F32)], axis=0)
    dmod = jnp.concatenate([mod_vec[0:2], mod_vec[2:4], tail_vec[4:6]], axis=1)
    small_g, dmod_g = _all_gather(
        [small, dmod],
        [jax.ShapeDtypeStruct((N_DEV, 8, D), F32), jax.ShapeDtypeStruct((N_DEV, BL, 3 * D), F32)],
        "gather_small")
    dmod_all = dmod_g.reshape(N_DEV * BL, 3 * D)
    small_names = ["b_ada", "conv_b", "rel_bias", "ln_g", "ln_b"]
    small_params = [(b_ada, m_b_ada, v_b_ada), (conv_b, m_conv_b, v_conv_b), (rel_bias, m_rel_bias, v_rel_bias),
                    (ln_g, m_ln_g, v_ln_g), (ln_b, m_ln_b, v_ln_b)]
    loss, small_res = _small_updates(
        small_g, dmod_all, small_g[:, 4, :N_BUCKETS * N_HEADS].reshape(N_DEV, N_BUCKETS, N_HEADS), small_params)
    loss = loss.reshape(())

    dmod_cols = lax.dynamic_slice(dmod_all, (0, me * ADA_SHARD), (N_DEV * BL, ADA_SHARD))
    res = {
        "w_ada": tuple(t[None] for t in _w_ada_update(jnp.transpose(c_all), dmod_cols,
                                                      w_ada[0], m_w_ada[0], v_w_ada[0])),
        "w_in": tuple(t[None] for t in _adamw(r_in, w_in[0], m_w_in[0], v_w_in[0], "adam_w_in", 128)),
    }
    mid_names = ["conv_w", "w_attn_out", "w_conv_out", "w_o"]
    mid_parts = [r_cw, r_ao, r_co, r_o]
    mid_full = [(conv_w, m_conv_w, v_conv_w), (w_attn_out, m_w_attn_out, v_w_attn_out),
                (w_conv_out, m_w_conv_out, v_w_conv_out), (w_o, m_w_o, v_w_o)]
    mid_res = _multi_adamw(mid_parts, [tuple(t[0] for t in wmv) for wmv in mid_full], "adam_mid")
    for nm, wmv, outs4 in zip(mid_names, mid_full, mid_res):
        res[nm] = tuple(t[None] for t in outs4)
    res.update(dict(zip(small_names, small_res)))
    order = ["w_ada", "b_ada", "w_in", "conv_w", "conv_b", "rel_bias", "w_attn_out", "w_conv_out",
             "w_o", "ln_g", "ln_b"]
    outs = [loss, grad_x.reshape(BL, S, D)]
    for k in range(4):
        outs += [res[name][k] for name in order]
    return tuple(outs)
```

```python
import math

import numpy as np
import jax
import jax.numpy as jnp
from jax import lax
from jax.experimental import pallas as pl
from jax.experimental.pallas import tpu as pltpu

F32 = jnp.float32
BF16 = jnp.bfloat16
MESH = pl.DeviceIdType.MESH

N_DEV = 8
D = 1024
S = 2048
BL = 2
T = BL * S
NCOL = 11264
SHARD = NCOL // N_DEV
CB = 512
NCB = NCOL // CB
HD = 128
GW = 512
QB = 128
DILATIONS = (1, 4, 16)
N_STEPS = 128
N_BUCKETS = 32
N_HEADS = 12
ALPHA = 2.0 ** 0.25
LN_EPS = 1e-5
NEG_INF = -1e30
SCALE = HD ** -0.5
ADA_SHARD = 3 * D // N_DEV

CB_Q, CB_K, CB_V, CB_GA = 0, 3, 6, 9
KB_U, KB_BG, KB_CG, KB_GC, KB_MA, KB_MC = 5, 6, 7, 8, 9, 10

ADAM_LR, ADAM_B1, ADAM_B2, ADAM_EPS, ADAM_WD, ADAM_STEP = 0.001, 0.9, 0.999, 1e-08, 0.01, 10

VMEM_LIMIT = 56 * 1024 * 1024
VMEM_LIMIT_TAIL = 62 * 1024 * 1024


def _dot(a, b):
    return jnp.dot(a, b, preferred_element_type=F32)


def _dot_nt(a, b):
    return lax.dot_general(a, b, (((1,), (1,)), ((), ())), preferred_element_type=F32)


def _dot_tn(a, b):
    return lax.dot_general(a, b, (((0,), (0,)), ((), ())), preferred_element_type=F32)


def _sigmoid(v):
    return 1.0 / (1.0 + jnp.exp(-v))


def _write_columns(pieces, dst_hbm, row0, sems):
    copies = []
    for k, (src, col0) in enumerate(pieces):
        rows, width = src.shape
        copies.append(pltpu.make_async_copy(
            src, dst_hbm.at[pl.ds(row0, rows), pl.ds(col0, width)], sems.at[k]))
    for cp in copies:
        cp.start()
    for cp in copies:
        cp.wait()


def _my_index():
    return 4 * lax.axis_index("x") + 2 * lax.axis_index("y") + lax.axis_index("c")


class _Gather:
    def __init__(self, ins, outs, stage, send_sems, recv_sems, local_sems):
        self.ins, self.outs, self.stage = ins, outs, stage
        self.send_sems, self.recv_sems, self.local_sems = send_sems, recv_sems, local_sems
        x, y, c = lax.axis_index("x"), lax.axis_index("y"), lax.axis_index("c")
        self.c = c
        self.me, self.sibling = (x, y, c), (x, y, 1 - c)
        self.chips = [(1 - x, y), (x, 1 - y), (1 - x, 1 - y)]

    @staticmethod
    def scratch(arrs):
        n = len(arrs)
        return ([pltpu.SemaphoreType.DMA((7 * n,)), pltpu.SemaphoreType.DMA((7 * n,)),
                 pltpu.SemaphoreType.DMA((n,))] + [pltpu.VMEM(a.shape, a.dtype) for a in arrs])

    def _copy(self, a, k, block, to, src=None):
        dst = self.outs[a].at[4 * block[0] + 2 * block[1] + block[2]]
        return pltpu.make_async_remote_copy(
            src_ref=dst if src is None else src, dst_ref=dst,
            send_sem=self.send_sems.at[a * 7 + k], recv_sem=self.recv_sems.at[a * 7 + k],
            device_id=to, device_id_type=MESH)

    def _first(self):
        first = []
        for a in range(len(self.ins)):
            first.append(self._copy(a, 0, self.me, self.sibling, src=self.ins[a]))
            first += [self._copy(a, 1 + j, self.me, (*chip, self.c), src=self.ins[a])
                      for j, chip in enumerate(self.chips)]
        return first

    def _mine(self):
        me = self.me
        return [pltpu.make_async_copy(self.stage[a], self.outs[a].at[4 * me[0] + 2 * me[1] + me[2]],
                                      self.local_sems.at[a]) for a in range(len(self.ins))]

    def begin(self):
        for cp in self._first():
            cp.start()
        loads = [pltpu.make_async_copy(self.ins[a], self.stage[a], self.local_sems.at[a])
                 for a in range(len(self.ins))]
        for cp in loads:
            cp.start()
        for cp in loads:
            cp.wait()
        for cp in self._mine():
            cp.start()

    def finish(self):
        n, c, me, sibling = len(self.ins), self.c, self.me, self.sibling
        passed = []
        for j, chip in enumerate(self.chips):
            for a in range(n):
                self._copy(a, 1 + j, (*chip, c), me).wait_recv()
                fwd = self._copy(a, 4 + j, (*chip, c), sibling)
                fwd.start()
                passed.append(fwd)
        for a in range(n):
            self._copy(a, 0, sibling, me).wait_recv()
        for j, chip in enumerate(self.chips):
            for a in range(n):
                self._copy(a, 4 + j, (*chip, 1 - c), me).wait_recv()
        for cp in self._first() + passed:
            cp.wait_send()
        for cp in self._mine():
            cp.wait()


def _all_gather(arrs, out_shapes, name):
    n = len(arrs)

    def body(*refs):
        g = _Gather(refs[:n], refs[n:2 * n], refs[2 * n + 3:], *refs[2 * n:2 * n + 3])
        g.begin()
        g.finish()

    any_spec = pl.BlockSpec(memory_space=pl.ANY)
    return pl.pallas_call(
        body, name=name,
        out_shape=tuple(out_shapes),
        in_specs=[any_spec] * n,
        out_specs=tuple([any_spec] * n),
        scratch_shapes=_Gather.scratch(arrs),
    )(*arrs)


def _slice_order():
    x, y, c = lax.axis_index("x"), lax.axis_index("y"), lax.axis_index("c")
    slots = []
    for q in (2 * (1 - x) + y, 2 * x + (1 - y), 2 * (1 - x) + (1 - y), 2 * x + y):
        slots += [2 * q + 1 - c, 2 * q + c]
    return jnp.stack(slots).astype(jnp.int32)


def _gw_in_pair(order, h, dproj, smalls, small_shapes4):
    kk, m = h.shape
    tk = min(kk, 2048)
    nk = kk // tk
    ncols = dproj.shape[1] // N_DEV
    n = len(smalls)

    def body(order_ref, h_ref, d_ref, *rest):
        ins = rest[:n]
        sums_hbm, parts_hbm = rest[n], rest[n + 1]
        sib = rest[n + 2:2 * n + 2]
        (acc, sendbuf, recvbuf, sumbuf, send_sems, recv_sems, local_sem, ssend, srecv,
         isend, irecv) = rest[2 * n + 2:]
        js, k = pl.program_id(0), pl.program_id(1)
        x, y, c = lax.axis_index("x"), lax.axis_index("y"), lax.axis_index("c")
        sibling = (x, y, 1 - c)
        my_chip = 2 * x + y
        near = [(1 - x, y, c), (x, 1 - y, c)]

        def ici_copy(p, out_chip):
            peer = near[p]
            return pltpu.make_async_remote_copy(
                src_ref=sumbuf.at[p], dst_ref=parts_hbm.at[out_chip],
                send_sem=isend.at[p], recv_sem=irecv.at[p], device_id=peer, device_id_type=MESH)

        def small_copies():
            return [pltpu.make_async_remote_copy(
                        src_ref=ins[a].at[2 * q + 1 - c], dst_ref=sib[a].at[q],
                        send_sem=ssend.at[a * 4 + q], recv_sem=srecv.at[a * 4 + q],
                        device_id=sibling, device_id_type=MESH)
                    for a in range(n) for q in range(4)]

        def slice_copy(p):
            return pltpu.make_async_remote_copy(
                src_ref=sendbuf, dst_ref=recvbuf.at[p], send_sem=send_sems.at[p], recv_sem=recv_sems.at[p],
                device_id=sibling, device_id_type=MESH)

        def sum_copy(p):
            return pltpu.make_async_copy(sumbuf.at[2], sums_hbm.at[order_ref[2 * p] // 2], local_sem)

        @pl.when((js == 0) & (k == 0))
        def _():
            for cp in small_copies():
                cp.start()

        @pl.when(k == 0)
        def _():
            acc[...] = jnp.zeros_like(acc)

        acc[...] += _dot_tn(h_ref[...], d_ref[...])

        for p in range(4):
            @pl.when((js == 2 * p) & (k == nk - 1))
            def _():
                if p > 0:
                    slice_copy(p - 1).wait_send()
                sendbuf[...] = acc[...].astype(BF16)
                slice_copy(p).start()

            @pl.when((js == 2 * p + 1) & (k == nk - 1))
            def _():
                slice_copy(p).wait_recv()
                if p == 3:
                    sum_copy(2).wait()
                sumbuf[min(p, 2)] = (acc[...] + recvbuf[p].astype(F32)).astype(BF16)
                if p < 2:
                    ici_copy(p, my_chip).start()
                else:
                    sum_copy(p).start()

        @pl.when((js == N_DEV - 1) & (k == nk - 1))
        def _():
            slice_copy(3).wait_send()
            sum_copy(3).wait()
            for cp in small_copies():
                cp.wait()
            for p in range(2):
                ici_copy(p, 2 * near[p][0] + near[p][1]).wait_recv()
                ici_copy(p, my_chip).wait_send()

    any_spec = pl.BlockSpec(memory_space=pl.ANY)
    res = pl.pallas_call(
        body, name="gw_in_pair",
        grid_spec=pltpu.PrefetchScalarGridSpec(
            num_scalar_prefetch=1,
            grid=(N_DEV, nk),
            in_specs=[pl.BlockSpec((tk, m), lambda js, k, order_ref: (k, 0)),
                      pl.BlockSpec((tk, ncols), lambda js, k, order_ref: (k, order_ref[js]))] + [any_spec] * n,
            out_specs=(any_spec,) * (n + 2),
            scratch_shapes=[pltpu.VMEM((m, ncols), F32), pltpu.VMEM((m, ncols), BF16),
                            pltpu.VMEM((4, m, ncols), BF16), pltpu.VMEM((3, m, ncols), BF16),
                            pltpu.SemaphoreType.DMA((4,)), pltpu.SemaphoreType.DMA((4,)),
                            pltpu.SemaphoreType.DMA,
                            pltpu.SemaphoreType.DMA((4 * n,)), pltpu.SemaphoreType.DMA((4 * n,)),
                            pltpu.SemaphoreType.DMA((2,)), pltpu.SemaphoreType.DMA((2,))]),
        out_shape=(jax.ShapeDtypeStruct((4, m, ncols), BF16),) * 2 + tuple(small_shapes4),
        compiler_params=pltpu.CompilerParams(vmem_limit_bytes=VMEM_LIMIT),
    )(order, h, dproj, *smalls)
    return res[0], res[1], res[2:]


def _chip_copies(ins, outs, send_sems, recv_sems, local_sems, hops):
    n = len(ins)
    x, y, c = lax.axis_index("x"), lax.axis_index("y"), lax.axis_index("c")
    my_chip = 2 * x + y

    def peer_of(k):
        return ((1 - x) if (k >> 1) & 1 else x, (1 - y) if k & 1 else y, c)

    def copy(a, k, out_chip):
        peer = peer_of(k)
        return pltpu.make_async_remote_copy(
            src_ref=ins[a].at[2 * peer[0] + peer[1]], dst_ref=outs[a].at[out_chip],
            send_sem=send_sems.at[a * 3 + k - 1], recv_sem=recv_sems.at[a * 3 + k - 1],
            device_id=peer, device_id_type=MESH)

    sends = [copy(a, k, my_chip) for k in range(1, 4) for a in range(n) if k in hops[a]]
    arrivals = []
    for k in range(1, 4):
        peer = peer_of(k)
        arrivals += [copy(a, k, 2 * peer[0] + peer[1]) for a in range(n) if k in hops[a]]
    mine = [pltpu.make_async_copy(ins[a].at[my_chip], outs[a].at[my_chip], local_sems.at[a])
            for a in range(n)]
    return sends, arrivals, mine


def _pair_add(core, mines, theirs):
    n = len(mines)

    def body(core_ref, *refs):
        mine, sib, outs = refs[:n], refs[n:2 * n], refs[2 * n:]
        for a in range(n):
            for q in range(4):
                outs[a][q] = (mine[a][2 * q + core_ref[0]].astype(F32)
                              + sib[a][q].astype(F32)).astype(outs[a].dtype)

    return pl.pallas_call(
        body, name="pair_add",
        in_specs=[pl.BlockSpec(memory_space=pltpu.SMEM)] + [pl.BlockSpec(memory_space=pltpu.VMEM)] * (2 * n),
        out_shape=tuple(jax.ShapeDtypeStruct(t.shape, t.dtype) for t in theirs),
    )(core, *mines, *theirs)


def _mod_exchange(c8, w_ada, b_cols):
    cols = w_ada.shape[1]

    def body(c_ref, w_ref, b_ref, call_ref, mod_ref, msend, send1, recv1, send2, recv2):
        x, y, c = lax.axis_index("x"), lax.axis_index("y"), lax.axis_index("c")
        my_slot = 4 * x + 2 * y + c

        def peer_of(k):
            return ((1 - x) if (k >> 2) & 1 else x, (1 - y) if (k >> 1) & 1 else y, (1 - c) if k & 1 else c)

        def slot_of(dev):
            return 4 * dev[0] + 2 * dev[1] + dev[2]

        def exchange(src_of, dst_ref, send_sems, recv_sems):
            sends, arrivals = [], []
            for k in range(1, 8):
                peer = peer_of(k)
                sends.append(pltpu.make_async_remote_copy(
                    src_ref=src_of(slot_of(peer)), dst_ref=dst_ref.at[my_slot],
                    send_sem=send_sems.at[k - 1], recv_sem=recv_sems.at[k - 1],
                    device_id=peer, device_id_type=MESH))
                arrivals.append(pltpu.make_async_remote_copy(
                    src_ref=src_of(my_slot), dst_ref=dst_ref.at[slot_of(peer)],
                    send_sem=send_sems.at[k - 1], recv_sem=recv_sems.at[k - 1],
                    device_id=peer, device_id_type=MESH))
            for cp in sends:
                cp.start()
            for cp in arrivals:
                cp.wait_recv()
            for cp in sends:
                cp.wait_send()

        call_ref[my_slot] = c_ref[...]
        exchange(lambda s: c_ref, call_ref, send1, recv1)
        cv = call_ref[...].reshape(N_DEV * 8, c_ref.shape[1])
        act = cv * _sigmoid(cv)
        mod = jnp.dot(act, w_ref[...], preferred_element_type=F32,
                      precision=lax.Precision.HIGHEST) + b_ref[...]
        msend[...] = mod.reshape(N_DEV, 8, cols)
        mod_ref[my_slot] = msend[my_slot]
        exchange(lambda s: msend.at[s], mod_ref, send2, recv2)

    return pl.pallas_call(
        body, name="mod_exchange",
        out_shape=(jax.ShapeDtypeStruct((N_DEV, 8, c8.shape[1]), F32),
                   jax.ShapeDtypeStruct((N_DEV, 8, cols), F32)),
        scratch_shapes=[pltpu.VMEM((N_DEV, 8, cols), F32)] + [pltpu.SemaphoreType.DMA((7,))] * 4,
    )(c8, w_ada, b_cols)


def _w_ada_update(c_all_t, dmod_cols, w, m, v):
    rows, cols = w.shape
    tr = 256

    def body(c_ref, d_ref, w_ref, m_ref, v_ref, g_ref, dl_ref, nm_ref, nv_ref):
        cv = c_ref[...]
        g = jnp.dot(cv * _sigmoid(cv), d_ref[...], preferred_element_type=F32,
                    precision=lax.Precision.HIGHEST)
        g_ref[...] = g
        dl_ref[...], nm_ref[...], nv_ref[...] = _adam_step(g, w_ref[...], m_ref[...], v_ref[...])

    blk = pl.BlockSpec((tr, cols), lambda i: (i, 0))
    shp = jax.ShapeDtypeStruct((rows, cols), F32)
    return pl.pallas_call(
        body, name="adam_w_ada",
        grid=(rows // tr,),
        in_specs=[pl.BlockSpec((tr, c_all_t.shape[1]), lambda i: (i, 0)),
                  pl.BlockSpec(dmod_cols.shape, lambda i: (0, 0)), blk, blk, blk],
        out_specs=(blk, blk, blk, blk),
        out_shape=(shp, shp, shp, shp),
    )(c_all_t, dmod_cols, w, m, v)


def _neighbour_chips():
    x, y, c = lax.axis_index("x"), lax.axis_index("y"), lax.axis_index("c")
    first = (jnp.where(c == 0, 1 - x, x), jnp.where(c == 0, y, 1 - y))
    second = (jnp.where(c == 0, x, 1 - x), jnp.where(c == 0, 1 - y, y))
    return first, second, (1 - x, 1 - y)


def _shard_order():
    x, y, c = lax.axis_index("x"), lax.axis_index("y"), lax.axis_index("c")
    first, second, diag = _neighbour_chips()
    devs = [(x, y, c), (x, y, 1 - c), (*first, c), (*second, 1 - c), (*second, c), (*first, 1 - c),
            (*diag, c), (*diag, 1 - c)]
    return jnp.stack([4 * d[0] + 2 * d[1] + d[2] for d in devs]).astype(jnp.int32)


def _prep_h(x2, mod3):
    ts = 512
    per_seq = S // ts

    def body(x_ref, mod_ref, h_ref):
        shift = mod_ref[0, 0:1, :]
        scale = mod_ref[0, 1:2, :]
        h_ref[...] = (x_ref[...] * (1.0 + scale) + shift).astype(BF16)

    return pl.pallas_call(
        body, name="prep_h",
        grid=(T // ts,),
        in_specs=[pl.BlockSpec((ts, D), lambda i: (i, 0)),
                  pl.BlockSpec((1, 3, D), lambda i: (i // per_seq, 0, 0))],
        out_specs=pl.BlockSpec((ts, D), lambda i: (i, 0)),
        out_shape=jax.ShapeDtypeStruct((T, D), BF16),
    )(x2, mod3)


def _gather_proj(order, h, w_shard, tm, ride=()):
    rows, kdim = h.shape
    ncols = w_shard.shape[1]
    n_i = rows // tm
    ride_arrs, ride_shapes = ride if ride else ((), ())
    n_ride = len(ride_arrs)

    def body(order_ref, h_ref, mine_hbm, *rest):
        ride_ins = rest[:n_ride]
        o_ref, all_hbm = rest[n_ride:n_ride + 2]
        ride_outs = rest[n_ride + 2:2 * n_ride + 2]
        wv, send_sems, recv_sems, local_sems = rest[2 * n_ride + 2:2 * n_ride + 6]
        ride_scr = rest[2 * n_ride + 6:]
        j, i = pl.program_id(0), pl.program_id(1)
        c = lax.axis_index("c")
        me, sibling = (lax.axis_index("x"), lax.axis_index("y"), c), (lax.axis_index("x"), lax.axis_index("y"), 1 - c)
        nb1, nb2, diag = _neighbour_chips()

        def slot(dev):
            return 4 * dev[0] + 2 * dev[1] + dev[2]

        def copy(k, block, to, src=None, part=None):
            buf = wv.at[slot(block)]
            if part is not None:
                buf = buf.at[pl.ds(pl.multiple_of(part * (kdim // 2), kdim // 2), kdim // 2)]
            return pltpu.make_async_remote_copy(
                src_ref=buf if src is None else src, dst_ref=buf,
                send_sem=send_sems.at[k], recv_sem=recv_sems.at[k],
                device_id=to, device_id_type=MESH)

        def keep(step, block):
            return pltpu.make_async_copy(wv.at[slot(block)], all_hbm.at[slot(block)], local_sems.at[step])

        if n_ride:
            gather = _Gather(ride_ins, ride_outs, ride_scr[3:], *ride_scr[:3])
        to_sibling, to_nb1, to_nb2 = (copy(0, me, sibling, mine_hbm), copy(1, me, (*nb1, c), mine_hbm),
                                      copy(2, me, (*nb2, c), mine_hbm))
        relay1, relay2 = copy(3, (*nb2, c), (*nb1, c), part=c), copy(4, (*nb1, c), (*nb2, c), part=1 - c)
        pass_nb1, pass_nb2 = copy(5, (*nb1, c), sibling), copy(6, (*nb2, c), sibling)
        pass_d1, pass_d2 = copy(7, (*diag, c), sibling, part=c), copy(8, (*diag, c), sibling, part=1 - c)
        sends = [to_sibling, to_nb1, to_nb2, relay1, relay2, pass_nb1, pass_nb2, pass_d1, pass_d2]
        due = [
            (me, [], []),
            (sibling, [copy(0, sibling, me)], [[]]),
            ((*nb1, c), [copy(1, (*nb1, c), me)], [[pass_nb1, to_nb2]]),
            ((*nb2, 1 - c), [copy(5, (*nb2, 1 - c), me)], [[]]),
            ((*nb2, c), [copy(2, (*nb2, c), me)], [[pass_nb2, relay1, relay2]]),
            ((*nb1, 1 - c), [copy(6, (*nb1, 1 - c), me)], [[]]),
            ((*diag, c), [copy(3, (*diag, c), me, part=c), copy(4, (*diag, c), me, part=1 - c)],
             [[pass_d1], [pass_d2]]),
            ((*diag, 1 - c), [copy(7, (*diag, 1 - c), me, part=1 - c), copy(8, (*diag, 1 - c), me, part=c)],
             [[], []]),
        ]

        @pl.when((j == 0) & (i == 0))
        def _():
            to_sibling.start()
            to_nb1.start()
            load = pltpu.make_async_copy(mine_hbm, wv.at[slot(me)], local_sems.at[N_DEV])
            load.start()
            load.wait()
            keep(0, me).start()

        for step in range(1, N_DEV):
            block, arrivals, then = due[step]

            @pl.when((j == step) & (i == 0))
            def _():
                for arrival, follow in zip(arrivals, then):
                    arrival.wait_recv()
                    for cp in follow:
                        cp.start()
                keep(step, block).start()
                if n_ride and step == N_DEV - 2:
                    gather.begin()

        o_ref[...] = _dot(h_ref[...], wv[order_ref[j]]).astype(BF16)

        @pl.when((j == N_DEV - 1) & (i == n_i - 1))
        def _():
            for cp in sends:
                cp.wait_send()
            for step in range(N_DEV):
                keep(step, due[step][0]).wait()
            if n_ride:
                gather.finish()

    any_spec = pl.BlockSpec(memory_space=pl.ANY)
    res = pl.pallas_call(
        body, name="gather_proj",
        grid_spec=pltpu.PrefetchScalarGridSpec(
            num_scalar_prefetch=1,
            grid=(N_DEV, n_i),
            in_specs=[pl.BlockSpec((tm, kdim), lambda j, i, order_ref: (i, 0)), any_spec] + [any_spec] * n_ride,
            out_specs=(pl.BlockSpec((tm, ncols), lambda j, i, order_ref: (i, order_ref[j])), any_spec)
                      + (any_spec,) * n_ride,
            scratch_shapes=[pltpu.VMEM((N_DEV, kdim, ncols), BF16),
                            pltpu.SemaphoreType.DMA((9,)), pltpu.SemaphoreType.DMA((9,)),
                            pltpu.SemaphoreType.DMA((N_DEV + 1,))]
                           + (_Gather.scratch(ride_arrs) if n_ride else [])),
        out_shape=(jax.ShapeDtypeStruct((rows, N_DEV * ncols), BF16),
                   jax.ShapeDtypeStruct((N_DEV, kdim, ncols), BF16)) + tuple(ride_shapes),
        compiler_params=pltpu.CompilerParams(vmem_limit_bytes=VMEM_LIMIT),
    )(order, h, w_shard, *ride_arrs)
    return res[0], res[1], res[2:]


SKEW_W = 512


def _bucket_maps():
    lanes = np.arange(SKEW_W)

    def buckets_of(steps):
        rows = []
        for dil in DILATIONS:
            dist = np.maximum(steps, 0) * dil
            nf = np.maximum(dist, 1).astype(np.float32)
            large = 16 + (np.log(nf / np.float32(16)) / np.float32(math.log(128.0))
                          * np.float32(16)).astype(np.int32)
            large = np.minimum(large, N_BUCKETS - 1)
            bucket = np.where(dist < 16, dist, large)
            rows.append(np.where((steps >= 0) & (steps <= N_STEPS), bucket, -1).astype(np.int32))
        return np.stack(rows)[:, None, :]

    a = np.arange(QB)[:, None]
    b = np.arange(2 * QB)[None, :]
    steps = a + QB - b
    band = (steps >= 0) & (steps <= N_STEPS)
    first = band & (b >= QB)
    masks = np.stack([first, band]).astype(np.int32)
    return buckets_of(QB - lanes), buckets_of(2 * QB - 1 - lanes), masks


def _bias_expand(rel_bias, lane_buckets, masks):
    def body(tab_ref, bk_ref, mk_ref, o_ref):
        for g in range(3):
            bk = bk_ref[g]
            for h in range(4):
                col = 4 * g + h
                per_offset = jnp.zeros((1, SKEW_W), F32)
                for k in range(N_BUCKETS):
                    per_offset = jnp.where(bk == k, tab_ref[k, col], per_offset)
                tile = pltpu.roll(jnp.broadcast_to(per_offset, (QB, SKEW_W)), 0, 1, stride=1, stride_axis=0)
                tile = tile[:, :2 * QB]
                o_ref[g, 0, h] = jnp.where(mk_ref[0] != 0, tile, NEG_INF)
                o_ref[g, 1, h] = jnp.where(mk_ref[1] != 0, tile, NEG_INF)

    return pl.pallas_call(
        body, name="bias_expand",
        in_specs=[pl.BlockSpec(memory_space=pltpu.SMEM),
                  pl.BlockSpec(memory_space=pltpu.VMEM),
                  pl.BlockSpec(memory_space=pltpu.VMEM)],
        out_shape=jax.ShapeDtypeStruct((3, 2, 4, QB, 2 * QB), F32),
    )(rel_bias, lane_buckets, masks)


def _bias_grad(ds1, ds2, ds3, lane_buckets):
    exchange = jnp.asarray(np.eye(QB, dtype=np.float32)[::-1].copy())

    def body(d1_ref, d2_ref, d3_ref, bk_ref, ex_ref, o_ref):
        for g, d_ref in enumerate((d1_ref, d2_ref, d3_ref)):
            bk = bk_ref[g]
            for h in range(4):
                flipped = jnp.dot(ex_ref[...], d_ref[h], preferred_element_type=F32,
                                  precision=lax.Precision.HIGHEST)
                padded = jnp.concatenate([flipped, jnp.zeros((QB, SKEW_W - 2 * QB), F32)], axis=1)
                skewed = pltpu.roll(padded, 0, 1, stride=1, stride_axis=0)
                per_offset = jnp.sum(skewed, axis=0, keepdims=True)
                for k in range(N_BUCKETS):
                    o_ref[k, 4 * g + h] = jnp.sum(jnp.where(bk == k, per_offset, 0.0))

    return pl.pallas_call(
        body, name="bias_grad",
        in_specs=[pl.BlockSpec(memory_space=pltpu.VMEM)] * 5,
        out_specs=pl.BlockSpec(memory_space=pltpu.SMEM),
        out_shape=jax.ShapeDtypeStruct((N_BUCKETS, N_HEADS), F32),
    )(ds1, ds2, ds3, lane_buckets, exchange)


def _scratch_sets(rows):
    return 4 if rows <= 512 else 1


def _unit_chunks(dil, size=16):
    units = [(h, r) for h in range(4) for r in range(dil)]
    return [units[i:i + size] for i in range(0, len(units), size)]


def _residue_rows(src_ref, copies, h, residue):
    buf = copies[h % len(copies)]
    buf[...] = src_ref[:, h * HD:(h + 1) * HD].astype(F32)
    return lambda r: buf[residue(r), :].astype(BF16)


def _attn_fwd(proj, bias, g):
    dil = DILATIONS[g]
    rows = QB * dil
    nsb = S // rows
    has_prev = nsb > 1

    def residue(r):
        return pl.ds(r, QB, stride=dil)

    n_sets = _scratch_sets(rows)
    n_in = 6 if has_prev else 4
    n_copied = (4 + (2 if has_prev else 0)) * n_sets

    def body(*refs):
        q_ref, kc_ref, vc_ref = refs[:3]
        kp_ref, vp_ref = refs[3:5] if has_prev else (None, None)
        b_ref = refs[n_in - 1]
        o_ref, l_ref = refs[n_in:n_in + 2]
        scr = list(refs[n_in + 2:])
        ls = [scr.pop(0) for _ in range(4)]
        copies = {name: [scr.pop(0) for _ in range(n_sets)]
                  for name in ("q", "kc", "vc", "o") + (("kp", "vp") if has_prev else ())}
        lane = lax.broadcasted_iota(jnp.int32, (QB, 128), 1)
        refs_of = {"q": q_ref, "kc": kc_ref, "vc": vc_ref, "kp": kp_ref, "vp": vp_ref}
        for chunk in _unit_chunks(dil):
            rows_of = {h: {name: _residue_rows(refs_of[name], copies[name], h, residue)
                           for name in refs_of if refs_of[name] is not None}
                       for h in sorted({h for h, _ in chunk})}

            def batch(name):
                return jnp.stack([rows_of[h][name](r) for h, r in chunk])

            q, k, v = batch("q"), batch("kc"), batch("vc")
            if has_prev:
                k = jnp.concatenate([batch("kp"), k], axis=1)
                v = jnp.concatenate([batch("vp"), v], axis=1)
                bias_b = jnp.stack([b_ref[h] for h, _ in chunk])
            else:
                bias_b = jnp.stack([b_ref[h, :, QB:] for h, _ in chunk])
            s = jnp.einsum("uqd,ukd->uqk", q, k, preferred_element_type=F32) * SCALE + bias_b
            m = jnp.max(s, axis=-1, keepdims=True)
            p = jnp.exp(s - m)
            l = jnp.sum(p, axis=-1, keepdims=True)
            o = jnp.einsum("uqk,ukd->uqd", p.astype(BF16), v, preferred_element_type=F32) / l
            lse = m + jnp.log(l)
            for i, (h, r) in enumerate(chunk):
                copies["o"][h % n_sets][residue(r), :] = o[i]
                ls[h][r * QB:(r + 1) * QB, :] = jnp.where(lane == h, lse[i], 0.0)
            for h in sorted({h for h, _ in chunk}):
                o_ref[:, h * HD:(h + 1) * HD] = copies["o"][h % n_sets][...]
        for r in range(dil):
            blk = slice(r * QB, (r + 1) * QB)
            l_ref[residue(r), :] = (ls[0][blk, :] + ls[1][blk, :]) + (ls[2][blk, :] + ls[3][blk, :])

    def row(b, n):
        return b * nsb + n

    def prev(b, n):
        return b * nsb + jnp.maximum(n - 1, 0)

    in_specs = [
        pl.BlockSpec((rows, GW), lambda b, n: (row(b, n), CB_Q + g)),
        pl.BlockSpec((rows, GW), lambda b, n: (row(b, n), CB_K + g)),
        pl.BlockSpec((rows, GW), lambda b, n: (row(b, n), CB_V + g)),
    ]
    args = [proj, proj, proj]
    scratch = [pltpu.VMEM((rows, 128), F32)] * (4 + n_copied)
    if has_prev:
        in_specs += [pl.BlockSpec((rows, GW), lambda b, n: (prev(b, n), CB_K + g)),
                     pl.BlockSpec((rows, GW), lambda b, n: (prev(b, n), CB_V + g))]
        args += [proj, proj]
    in_specs.append(pl.BlockSpec((None, None, 4, QB, 2 * QB),
                                 lambda b, n: (g, jnp.minimum(n, 1), 0, 0, 0)))
    args.append(bias)
    return pl.pallas_call(
        body, name=f"attn_fwd{g}",
        grid=(BL, nsb),
        in_specs=in_specs,
        out_specs=(pl.BlockSpec((rows, GW), lambda b, n: (row(b, n), 0)),
                   pl.BlockSpec((rows, 128), lambda b, n: (row(b, n), 0))),
        out_shape=(jax.ShapeDtypeStruct((T, GW), F32), jax.ShapeDtypeStruct((T, 128), F32)),
        scratch_shapes=scratch,
        compiler_params=pltpu.CompilerParams(vmem_limit_bytes=VMEM_LIMIT),
    )(*args)


def _attn_bwd(proj, d_out, stats, bias, dproj, g):
    dil = DILATIONS[g]
    rows = QB * dil
    nsb = S // rows
    has_prev = nsb > 1
    n_steps = nsb + 1 if has_prev else 1
    n_in = 7 + (2 if has_prev else 0)

    def residue(r):
        return pl.ds(r, QB, stride=dil)

    n_sets = _scratch_sets(rows)

    def body(*refs):
        q_ref, kc_ref, vc_ref, do_ref, st_ref, b_ref = refs[:6]
        kp_ref, vp_ref = refs[6:8] if has_prev else (None, None)
        out_ref, db_ref = refs[n_in], refs[n_in + 1]
        scr = list(refs[n_in + 2:])
        sq, sk, sv, sems = [scr.pop(0) for _ in range(4)]
        carry = scr.pop(0) if has_prev else None
        sts = scr.pop(0)
        copies = {name: [scr.pop(0) for _ in range(n_sets)]
                  for name in ("q", "kc", "vc", "do", "dq", "dk", "dv") + (("kp", "vp") if has_prev else ())}
        b, n = pl.program_id(0), pl.program_id(1)

        @pl.when((b == 0) & (n == 0))
        def _():
            db_ref[...] = jnp.zeros_like(db_ref)

        def finish(h, r, dq, dk, dv):
            for name, val in (("dq", dq), ("dk", dk), ("dv", dv)):
                copies[name][h % n_sets][residue(r), :] = val

        def finish_head(h):
            sl = slice(h * HD, (h + 1) * HD)
            sq[:, sl] = copies["dq"][h % n_sets][...].astype(BF16)
            sk[:, sl] = copies["dk"][h % n_sets][...].astype(BF16)
            sv[:, sl] = copies["dv"][h % n_sets][...].astype(BF16)

        def write_block(blk_idx):
            row0 = pl.multiple_of(blk_idx * rows, rows)
            _write_columns([(sq, CB * (CB_Q + g)), (sk, CB * (CB_K + g)), (sv, CB * (CB_V + g))],
                           out_ref, row0, sems)

        def carried(h, r):
            blk = slice(r * QB, (r + 1) * QB)
            return ((blk, slice(h * HD, (h + 1) * HD)), (blk, slice(GW + h * HD, GW + (h + 1) * HD)),
                    (blk, slice(2 * GW + h * HD, 2 * GW + (h + 1) * HD)))

        if has_prev:
            @pl.when(n == 0)
            def _():
                carry[...] = jnp.zeros_like(carry)

            @pl.when(n == nsb)
            def _():
                for h in range(4):
                    for r in range(dil):
                        cq, ck, cv = carried(h, r)
                        finish(h, r, carry[cq], carry[ck], carry[cv])
                    finish_head(h)
                write_block(b * nsb + nsb - 1)

        @pl.when(n < nsb)
        def _():
            for r in range(dil):
                sts[r * QB:(r + 1) * QB, :] = st_ref[residue(r), :]
            refs_of = {"q": q_ref, "kc": kc_ref, "vc": vc_ref, "do": do_ref, "kp": kp_ref, "vp": vp_ref}
            for chunk in _unit_chunks(dil):
                heads = sorted({h for h, _ in chunk})
                rows_of = {h: {name: _residue_rows(refs_of[name], copies[name], h, residue)
                               for name in refs_of if refs_of[name] is not None}
                           for h in heads}

                def batch(name):
                    return jnp.stack([rows_of[h][name](r) for h, r in chunk])

                q, k, v, do = batch("q"), batch("kc"), batch("vc"), batch("do")
                if has_prev:
                    k = jnp.concatenate([batch("kp"), k], axis=1)
                    v = jnp.concatenate([batch("vp"), v], axis=1)
                    bias_b = jnp.stack([b_ref[h] for h, _ in chunk])
                else:
                    bias_b = jnp.stack([b_ref[h, :, QB:] for h, _ in chunk])
                lse = jnp.stack([sts[r * QB:(r + 1) * QB, h:h + 1] for h, r in chunk])
                delta = jnp.stack([sts[r * QB:(r + 1) * QB, 4 + h:5 + h] for h, r in chunk])
                s = jnp.einsum("uqd,ukd->uqk", q, k, preferred_element_type=F32) * SCALE + bias_b
                p = jnp.exp(s - lse)
                ds = p * (jnp.einsum("uqd,ukd->uqk", do, v, preferred_element_type=F32) - delta)
                for h in heads:
                    mine = [ds[i] for i, (hh, _) in enumerate(chunk) if hh == h]
                    tot = mine[0]
                    for extra in mine[1:]:
                        tot = tot + extra
                    if has_prev:
                        db_ref[h] += tot
                    else:
                        db_ref[h, :, QB:] += tot
                dsb, pb = ds.astype(BF16), p.astype(BF16)
                dq = jnp.einsum("uqk,ukd->uqd", dsb, k, preferred_element_type=F32) * SCALE
                dk = jnp.einsum("uqk,uqd->ukd", dsb, q, preferred_element_type=F32) * SCALE
                dv = jnp.einsum("uqk,uqd->ukd", pb, do, preferred_element_type=F32)
                for i, (h, r) in enumerate(chunk):
                    if has_prev:
                        cq, ck, cv = carried(h, r)
                        finish(h, r, carry[cq], carry[ck] + dk[i, :QB], carry[cv] + dv[i, :QB])
                        carry[cq] = dq[i]
                        carry[ck] = dk[i, QB:]
                        carry[cv] = dv[i, QB:]
                    else:
                        finish(h, r, dq[i], dk[i], dv[i])
                for h in heads:
                    finish_head(h)
            if has_prev:
                @pl.when(n > 0)
                def _():
                    write_block(b * nsb + n - 1)
            else:
                write_block(b)

    def row(b, n):
        return b * nsb + jnp.minimum(n, nsb - 1)

    def prev(b, n):
        return b * nsb + jnp.maximum(jnp.minimum(n, nsb - 1) - 1, 0)

    in_specs = [
        pl.BlockSpec((rows, GW), lambda b, n: (row(b, n), CB_Q + g)),
        pl.BlockSpec((rows, GW), lambda b, n: (row(b, n), CB_K + g)),
        pl.BlockSpec((rows, GW), lambda b, n: (row(b, n), CB_V + g)),
        pl.BlockSpec((rows, GW), lambda b, n: (row(b, n), 0)),
        pl.BlockSpec((rows, 128), lambda b, n: (row(b, n), 0)),
        pl.BlockSpec((None, None, 4, QB, 2 * QB),
                     lambda b, n: (g, jnp.minimum(jnp.minimum(n, nsb - 1), 1), 0, 0, 0)),
    ]
    args = [proj, proj, proj, d_out, stats, bias]
    scratch = [pltpu.VMEM((rows, GW), BF16)] * 3 + [pltpu.SemaphoreType.DMA((3,))]
    if has_prev:
        in_specs += [pl.BlockSpec((rows, GW), lambda b, n: (prev(b, n), CB_K + g)),
                     pl.BlockSpec((rows, GW), lambda b, n: (prev(b, n), CB_V + g))]
        args += [proj, proj]
        scratch.append(pltpu.VMEM((rows, 3 * GW), F32))
    n_copied = (7 + (2 if has_prev else 0)) * n_sets
    scratch += [pltpu.VMEM((rows, 128), F32)] * (1 + n_copied)
    in_specs.append(pl.BlockSpec(memory_space=pl.ANY))
    args.append(dproj)
    return pl.pallas_call(
        body, name=f"attn_bwd{g}",
        grid=(BL, n_steps),
        in_specs=in_specs,
        out_specs=(pl.BlockSpec(memory_space=pl.ANY),
                   pl.BlockSpec((4, QB, 2 * QB), lambda b, n: (0, 0, 0))),
        out_shape=(jax.ShapeDtypeStruct((T, NCOL), BF16),
                   jax.ShapeDtypeStruct((4, QB, 2 * QB), F32)),
        scratch_shapes=scratch,
        input_output_aliases={len(args) - 1: 0},
        compiler_params=pltpu.CompilerParams(vmem_limit_bytes=VMEM_LIMIT),
    )(*args)


def _tail(x2, tgt2, mod3, o_g, lse_g, proj, w_ao, w_co, w_o, conv_w, conv_b, ln_g, ln_b):
    tm = 256
    per_seq = S // tm
    halo = 16

    def body(x_ref, t_ref, mod_ref, o1_ref, o2_ref, o3_ref, l1_ref, l2_ref, l3_ref,
             ga_ref, u_ref, bg_ref, cg_ref, gc_ref, ma_ref, mc_ref, up_ref, cp_ref,
             wao_ref, wco_ref, wo_ref, cw_ref, cb_ref, lg_ref, lb_ref,
             dproj_ref, dyc_ref, do_ref, st_ref, dxd_ref,
             gwo_ref, gwco_ref, gwao_ref, vec_ref,
             dga_s, dbg_s, dgm_s, sems, acc_o, acc_co, acc_ao):
        i = pl.program_id(0)
        bidx = i // per_seq
        first = (i % per_seq) == 0

        @pl.when(i == 0)
        def _():
            vec_ref[...] = jnp.zeros_like(vec_ref)

        l1, l2, l3 = l1_ref[...], l2_ref[...], l3_ref[...]
        mx = jnp.maximum(jnp.maximum(l1, l2), l3)
        e1, e2, e3 = jnp.exp(l1 - mx), jnp.exp(l2 - mx), jnp.exp(l3 - mx)
        esum = e1 + e2 + e3
        lse_tot = mx + jnp.log(esum)
        w1, w2, w3 = e1 / esum, e2 / esum, e3 / esum

        def per_head(wv):
            return jnp.concatenate([jnp.broadcast_to(wv[:, h:h + 1], (tm, HD)) for h in range(4)], axis=1)

        o = per_head(w1) * o1_ref[...] + per_head(w2) * o2_ref[...] + per_head(w3) * o3_ref[...]

        ga = ga_ref[...].astype(F32)
        sig_ga = _sigmoid(ga)
        silu_ga = ga * sig_ga
        a_in = (o * silu_ga).astype(BF16)
        a_out = _dot(a_in, wao_ref[...])

        u = u_ref[...].astype(F32)
        cg = cg_ref[...].astype(F32)
        z = cg * u
        zp = cp_ref[...].astype(F32) * up_ref[...].astype(F32)
        zp = jnp.where(first, 0.0, zp)
        zcat = jnp.concatenate([zp, z], axis=0)
        z1 = pltpu.roll(zcat, 1, 0)[halo:]
        z2 = pltpu.roll(zcat, 2, 0)[halo:]
        y_conv = cw_ref[0:1, :] * z2 + cw_ref[1:2, :] * z1 + cw_ref[2:3, :] * z + cb_ref[...]
        gc = gc_ref[...].astype(F32)
        sig_gc = _sigmoid(gc)
        silu_gc = gc * sig_gc
        bg = bg_ref[...].astype(F32)
        bg_yc = bg * y_conv
        s_in = (bg_yc * silu_gc).astype(BF16)
        s_out = _dot(s_in, wco_ref[...])

        sa = _sigmoid(ma_ref[...].astype(F32))
        sc = _sigmoid(mc_ref[...].astype(F32))
        merged = (sa * a_out + sc * s_out).astype(BF16)
        y = _dot(merged, wo_ref[...])
        gate1 = 1.0 + mod_ref[0, 2:3, :]
        xv = x_ref[...]
        resid = ALPHA * xv + gate1 * y
        mu = jnp.mean(resid, axis=1, keepdims=True)
        xc = resid - mu
        var = jnp.mean(xc * xc, axis=1, keepdims=True)
        rstd = lax.rsqrt(var + LN_EPS)
        xhat = xc * rstd
        lg = lg_ref[...]
        err = xhat * lg + lb_ref[...] - t_ref[...]
        vec_ref[3:4, :] += (0.5 / D) * jnp.sum(err * err, axis=0, keepdims=True)

        vec_ref[1:2, :] += (1.0 / D) * jnp.sum(err * xhat, axis=0, keepdims=True)
        vec_ref[2:3, :] += (1.0 / D) * jnp.sum(err, axis=0, keepdims=True)
        dxh = err * (lg * (1.0 / D))
        dres = rstd * (dxh - jnp.mean(dxh, axis=1, keepdims=True)
                       - xhat * jnp.mean(dxh * xhat, axis=1, keepdims=True))
        dxd_ref[...] = ALPHA * dres
        dgate = jnp.sum(dres * y, axis=0, keepdims=True)
        vec_ref[4:5, :] += jnp.where(bidx == 0, dgate, 0.0)
        vec_ref[5:6, :] += jnp.where(bidx == 1, dgate, 0.0)
        dy = (dres * gate1).astype(BF16)

        dmerged = _dot_nt(dy, wo_ref[...])
        da_out_f = dmerged * sa
        ds_out_f = dmerged * sc
        da_out = da_out_f.astype(BF16)
        ds_out = ds_out_f.astype(BF16)
        dgm_s[:, 2 * D:3 * D] = (ds_out_f * s_out * (1.0 - sc)).astype(BF16)
        dgm_s[:, D:2 * D] = (da_out_f * a_out * (1.0 - sa)).astype(BF16)
        da_in = _dot_nt(da_out, wao_ref[...])
        ds_in = _dot_nt(ds_out, wco_ref[...])

        d_o = da_in * silu_ga
        do_ref[...] = d_o.astype(BF16)
        dga_s[...] = (da_in * o * (sig_ga + silu_ga * (1.0 - sig_ga))).astype(BF16)
        lane = lax.broadcasted_iota(jnp.int32, (tm, 128), 1)
        stats = lse_tot
        od = o * d_o
        for h in range(4):
            delta = jnp.sum(od[:, h * HD:(h + 1) * HD], axis=1, keepdims=True)
            stats = jnp.where(lane == 4 + h, delta, stats)
        st_ref[...] = stats

        ds_silu = ds_in * silu_gc
        dbg_s[...] = (ds_silu * y_conv).astype(BF16)
        dyc = ds_silu * bg
        dyc_ref[...] = dyc
        vec_ref[0:1, :] += jnp.sum(dyc, axis=0, keepdims=True)
        dgm_s[:, 0:D] = (ds_in * bg_yc * (sig_gc + silu_gc * (1.0 - sig_gc))).astype(BF16)

        @pl.when(i == 0)
        def _():
            acc_o[...] = jnp.zeros_like(acc_o)
            acc_co[...] = jnp.zeros_like(acc_co)
            acc_ao[...] = jnp.zeros_like(acc_ao)

        acc_o[...] += _dot_tn(merged, dy)
        acc_co[...] += _dot_tn(s_in, ds_out)
        acc_ao[...] += _dot_tn(a_in, da_out)

        @pl.when(i == T // tm - 1)
        def _():
            gwo_ref[...] = acc_o[...].astype(BF16)
            gwco_ref[...] = acc_co[...].astype(BF16)
            gwao_ref[...] = acc_ao[...].astype(BF16)

        _write_columns([(dga_s, CB * CB_GA), (dbg_s, D * KB_BG), (dgm_s, D * KB_GC)],
                       dproj_ref, pl.multiple_of(i * tm, tm), sems)

    def tile(width, cblk=0):
        return pl.BlockSpec((tm, width), lambda i: (i, cblk))

    def whole(shape):
        return pl.BlockSpec(shape, lambda i: tuple(0 for _ in shape))

    def once(shape):
        return pl.BlockSpec(shape, lambda i: tuple(0 for _ in shape), pipeline_mode=pl.Buffered(1))

    prev_rows = lambda i: (jnp.maximum(i * (tm // halo) - 1, 0),)
    in_specs = [
        tile(D), tile(D), pl.BlockSpec((1, 3, D), lambda i: (i // per_seq, 0, 0)),
        tile(GW), tile(GW), tile(GW), tile(128), tile(128), tile(128),
        tile(GW, CB_GA), tile(D, KB_U), tile(D, KB_BG), tile(D, KB_CG), tile(D, KB_GC),
        tile(D, KB_MA), tile(D, KB_MC),
        pl.BlockSpec((halo, D), lambda i: (*prev_rows(i), KB_U)),
        pl.BlockSpec((halo, D), lambda i: (*prev_rows(i), KB_CG)),
        whole((GW, D)), whole((D, D)), whole((D, D)),
        whole((3, D)), whole((1, D)), whole((1, D)), whole((1, D)),
    ]
    out_specs = (
        pl.BlockSpec(memory_space=pl.ANY), tile(D), tile(GW), tile(128), tile(D),
        once((D, D)), once((D, D)), once((GW, D)),
        pl.BlockSpec((8, D), lambda i: (0, 0)),
    )
    out_shape = (
        jax.ShapeDtypeStruct((T, NCOL), BF16),
        jax.ShapeDtypeStruct((T, D), F32),
        jax.ShapeDtypeStruct((T, GW), BF16),
        jax.ShapeDtypeStruct((T, 128), F32),
        jax.ShapeDtypeStruct((T, D), F32),
        jax.ShapeDtypeStruct((D, D), BF16),
        jax.ShapeDtypeStruct((D, D), BF16),
        jax.ShapeDtypeStruct((GW, D), BF16),
        jax.ShapeDtypeStruct((8, D), F32),
    )
    return pl.pallas_call(
        body, name="tail",
        grid=(T // tm,),
        in_specs=in_specs, out_specs=out_specs, out_shape=out_shape,
        scratch_shapes=[pltpu.VMEM((tm, GW), BF16), pltpu.VMEM((tm, D), BF16), pltpu.VMEM((tm, 3 * D), BF16),
                        pltpu.SemaphoreType.DMA((3,)),
                        pltpu.VMEM((D, D), F32), pltpu.VMEM((D, D), F32), pltpu.VMEM((GW, D), F32)],
        compiler_params=pltpu.CompilerParams(vmem_limit_bytes=VMEM_LIMIT_TAIL),
    )(x2, tgt2, mod3, *o_g, *lse_g, proj, proj, proj, proj, proj, proj, proj, proj, proj,
      w_ao, w_co, w_o, conv_w, conv_b, ln_g, ln_b)


def _conv_bwd(dyc, proj, conv_w, dproj):
    tm = 512
    per_seq = S // tm

    def body(d_ref, dn_ref, u_ref, c_ref, cw_ref, _, dproj_ref, g_ref, du_s, dc_s, sems):
        i = pl.program_id(0)
        last = (i % per_seq) == per_seq - 1

        @pl.when(i == 0)
        def _():
            g_ref[...] = jnp.zeros_like(g_ref)

        d = d_ref[...]
        dn = jnp.where(last, 0.0, dn_ref[...])
        dcat = jnp.concatenate([d, dn], axis=0)
        d1 = pltpu.roll(dcat, tm + 8 - 1, 0)[:tm]
        d2 = pltpu.roll(dcat, tm + 8 - 2, 0)[:tm]
        dz = cw_ref[2:3, :] * d + cw_ref[1:2, :] * d1 + cw_ref[0:1, :] * d2
        u = u_ref[...].astype(F32)
        cg = c_ref[...].astype(F32)
        du_s[...] = (dz * cg).astype(BF16)
        dc_s[...] = (dz * u).astype(BF16)
        _write_columns([(du_s, D * KB_U), (dc_s, D * KB_CG)], dproj_ref, pl.multiple_of(i * tm, tm), sems)

        z = cg * u
        g_ref[0:1, :] += jnp.sum(d2 * z, axis=0, keepdims=True)
        g_ref[1:2, :] += jnp.sum(d1 * z, axis=0, keepdims=True)
        g_ref[2:3, :] += jnp.sum(d * z, axis=0, keepdims=True)

    n_tiles = T // tm
    next_rows = lambda i: jnp.minimum((i + 1) * (tm // 8), T // 8 - 1)
    return pl.pallas_call(
        body, name="conv_bwd",
        grid=(n_tiles,),
        in_specs=[pl.BlockSpec((tm, D), lambda i: (i, 0)),
                  pl.BlockSpec((8, D), lambda i: (next_rows(i), 0)),
                  pl.BlockSpec((tm, D), lambda i: (i, KB_U)),
                  pl.BlockSpec((tm, D), lambda i: (i, KB_CG)),
                  pl.BlockSpec((3, D), lambda i: (0, 0)),
                  pl.BlockSpec(memory_space=pl.ANY)],
        out_specs=(pl.BlockSpec(memory_space=pl.ANY),
                   pl.BlockSpec((8, D), lambda i: (0, 0))),
        out_shape=(jax.ShapeDtypeStruct((T, NCOL), BF16),
                   jax.ShapeDtypeStruct((8, D), F32)),
        scratch_shapes=[pltpu.VMEM((tm, D), BF16), pltpu.VMEM((tm, D), BF16), pltpu.SemaphoreType.DMA((2,))],
        input_output_aliases={5: 0},
        compiler_params=pltpu.CompilerParams(vmem_limit_bytes=VMEM_LIMIT),
    )(dyc, dyc, proj, proj, conv_w, dproj)


def _dh_dx(dproj, w_in_all, x2, dxd, mod3, chip_sums, hops=(), parts0=None):
    tm = 1024
    per_seq = S // tm
    n = len(chip_sums)
    n_in = 5 + n + (0 if parts0 is None else 1)

    def body(*refs):
        d_ref, w_ref, x_ref, dxd_ref, mod_ref = refs[:5]
        ins = refs[5:5 + n]
        gx_ref, vec_ref = refs[n_in:n_in + 2]
        outs = refs[n_in + 2:n_in + 2 + n]
        acc, send_sems, recv_sems, local_sems = refs[n_in + 2 + n:]
        i, jj = pl.program_id(0), pl.program_id(1)

        @pl.when((i == 0) & (jj == 0))
        def _():
            vec_ref[...] = jnp.zeros_like(vec_ref)
            if n:
                sends, _, mine = _chip_copies(ins, outs, send_sems, recv_sems, local_sems, hops)
                for cp in sends + mine:
                    cp.start()

        if n:
            @pl.when((i == T // tm - 1) & (jj == N_DEV - 1))
            def _():
                sends, arrivals, mine = _chip_copies(ins, outs, send_sems, recv_sems, local_sems, hops)
                for cp in arrivals:
                    cp.wait_recv()
                for cp in sends:
                    cp.wait_send()
                for cp in mine:
                    cp.wait()

        @pl.when(jj == 0)
        def _():
            acc[...] = jnp.zeros_like(acc)

        acc[...] += _dot_nt(d_ref[...], w_ref[...])

        @pl.when(jj == N_DEV - 1)
        def _():
            dh = acc[...]
            bidx = i // per_seq
            gx_ref[...] = dxd_ref[...] + dh * (1.0 + mod_ref[0, 1:2, :])
            dshift = jnp.sum(dh, axis=0, keepdims=True)
            dscale = jnp.sum(dh * x_ref[...], axis=0, keepdims=True)
            vec_ref[0:1, :] += jnp.where(bidx == 0, dshift, 0.0)
            vec_ref[1:2, :] += jnp.where(bidx == 1, dshift, 0.0)
            vec_ref[2:3, :] += jnp.where(bidx == 0, dscale, 0.0)
            vec_ref[3:4, :] += jnp.where(bidx == 1, dscale, 0.0)

    any_spec = pl.BlockSpec(memory_space=pl.ANY)
    res = pl.pallas_call(
        body, name="dh_dx",
        grid=(T // tm, N_DEV),
        in_specs=[
            pl.BlockSpec((tm, SHARD), lambda i, jj: (i, jj)),
            pl.BlockSpec((None, D, SHARD), lambda i, jj: (jj, 0, 0)),
            pl.BlockSpec((tm, D), lambda i, jj: (i, 0)),
            pl.BlockSpec((tm, D), lambda i, jj: (i, 0)),
            pl.BlockSpec((1, 3, D), lambda i, jj: (i // per_seq, 0, 0))] + [any_spec] * (n_in - 5),
        out_specs=(pl.BlockSpec((tm, D), lambda i, jj: (i, 0)),
                   pl.BlockSpec((8, D), lambda i, jj: (0, 0))) + (any_spec,) * n,
        out_shape=(jax.ShapeDtypeStruct((T, D), F32), jax.ShapeDtypeStruct((8, D), F32))
                  + tuple(jax.ShapeDtypeStruct(a.shape, a.dtype) for a in chip_sums),
        scratch_shapes=[pltpu.VMEM((tm, D), F32), pltpu.SemaphoreType.DMA((max(3 * n, 1),)),
                        pltpu.SemaphoreType.DMA((max(3 * n, 1),)), pltpu.SemaphoreType.DMA((max(n, 1),))],
        input_output_aliases={} if parts0 is None else {5 + n: 2},
        compiler_params=pltpu.CompilerParams(vmem_limit_bytes=VMEM_LIMIT),
    )(dproj, w_in_all, x2, dxd, mod3, *chip_sums, *([] if parts0 is None else [parts0]))
    return res[0], res[1], res[2:]


def _adam_step(g, w, m, v):
    nm = ADAM_B1 * m + (1.0 - ADAM_B1) * g
    nv = ADAM_B2 * v + (1.0 - ADAM_B2) * (g * g)
    m_hat = nm / (1.0 - ADAM_B1 ** ADAM_STEP)
    v_hat = nv / (1.0 - ADAM_B2 ** ADAM_STEP)
    return -ADAM_LR * (m_hat / (jnp.sqrt(v_hat) + ADAM_EPS) + ADAM_WD * w), nm, nv


def _adamw(parts, w, m, v, name, row_tile=None):
    n_parts, rows, cols = parts.shape
    tr = rows if row_tile is None else row_tile

    def body(p_ref, w_ref, m_ref, v_ref, g_ref, d_ref, nm_ref, nv_ref):
        g = p_ref[0].astype(F32)
        for s in range(1, n_parts):
            g = g + p_ref[s].astype(F32)
        g_ref[...] = g
        d_ref[...], nm_ref[...], nv_ref[...] = _adam_step(g, w_ref[...], m_ref[...], v_ref[...])

    blk = pl.BlockSpec((tr, cols), lambda i: (i, 0))
    shp = jax.ShapeDtypeStruct((rows, cols), F32)
    return pl.pallas_call(
        body, name=name,
        grid=(rows // tr,),
        in_specs=[pl.BlockSpec((n_parts, tr, cols), lambda i: (0, i, 0)), blk, blk, blk],
        out_specs=(blk, blk, blk, blk),
        out_shape=(shp, shp, shp, shp),
        compiler_params=pltpu.CompilerParams(vmem_limit_bytes=VMEM_LIMIT),
    )(parts, w, m, v)


def _multi_adamw(parts_list, params, name):
    n = len(params)
    flat = [t for wmv in params for t in wmv]

    def body(*refs):
        parts, ins, outs = refs[:n], refs[n:4 * n], refs[4 * n:]
        for p in range(n):
            g = parts[p][0].astype(F32)
            for s in range(1, parts[p].shape[0]):
                g = g + parts[p][s].astype(F32)
            w_ref, m_ref, v_ref = ins[3 * p:3 * p + 3]
            g_ref, d_ref, nm_ref, nv_ref = outs[4 * p:4 * p + 4]
            g_ref[...] = g
            d_ref[...], nm_ref[...], nv_ref[...] = _adam_step(g, w_ref[...], m_ref[...], v_ref[...])

    out_shape = []
    for w, _, _ in params:
        out_shape += [jax.ShapeDtypeStruct(w.shape, F32)] * 4
    res = pl.pallas_call(body, name=name, out_shape=tuple(out_shape))(*parts_list, *flat)
    return [res[4 * p:4 * p + 4] for p in range(n)]


def _small_updates(small_g, dmod_all, rel_parts, params):
    flat = [t for wmv in params for t in wmv]

    def body(sg_ref, dm_ref, rp_ref, *refs):
        ins, outs = refs[:len(flat)], refs[len(flat):]

        def over_devices(row):
            tot = sg_ref[0, row:row + 1, :]
            for s in range(1, N_DEV):
                tot = tot + sg_ref[s, row:row + 1, :]
            return tot

        g_b_ada = dm_ref[0:1, :]
        for r in range(1, N_DEV * BL):
            g_b_ada = g_b_ada + dm_ref[r:r + 1, :]
        g_rel = rp_ref[0]
        for s in range(1, N_DEV):
            g_rel = g_rel + rp_ref[s]
        grads = [g_b_ada, over_devices(0), g_rel, over_devices(1), over_devices(2)]
        outs[0][...] = jnp.sum(over_devices(3), axis=1, keepdims=True)
        for p, g in enumerate(grads):
            w_ref, m_ref, v_ref = ins[3 * p:3 * p + 3]
            g_ref, d_ref, nm_ref, nv_ref = outs[1 + 4 * p:5 + 4 * p]
            g_ref[...] = g
            d_ref[...], nm_ref[...], nv_ref[...] = _adam_step(g, w_ref[...], m_ref[...], v_ref[...])

    out_shape = [jax.ShapeDtypeStruct((1, 1), F32)]
    for w, _, _ in params:
        out_shape += [jax.ShapeDtypeStruct(w.shape, F32)] * 4
    res = pl.pallas_call(body, name="small_updates", out_shape=tuple(out_shape))(small_g, dmod_all, rel_parts, *flat)
    return res[0], [res[1 + 4 * p:5 + 4 * p] for p in range(len(params))]


def _attn_fwd_dense(proj, bias):
    nq = 4
    rows = nq * QB
    nsb = S // rows

    def body(q_ref, k_ref, v_ref, kp_ref, vp_ref, b_ref, o_ref, l_ref, ls0, ls1, ls2, ls3):
        ls = [ls0, ls1, ls2, ls3]
        n = pl.program_id(1)
        lane = lax.broadcasted_iota(jnp.int32, (QB, 128), 1)
        units = [(h, j) for h in range(4) for j in range(nq)]

        def keys(cur_ref, prev_ref, h, j):
            sl = slice(h * HD, (h + 1) * HD)
            if j == 0:
                return jnp.concatenate([prev_ref[:, sl], cur_ref[0:QB, sl]], axis=0)
            return cur_ref[(j - 1) * QB:(j + 1) * QB, sl]

        q = jnp.stack([q_ref[j * QB:(j + 1) * QB, h * HD:(h + 1) * HD] for h, j in units])
        k = jnp.stack([keys(k_ref, kp_ref, h, j) for h, j in units])
        v = jnp.stack([keys(v_ref, vp_ref, h, j) for h, j in units])
        bias_b = jnp.stack([b_ref[jnp.minimum(n, 1), h] if j == 0 else b_ref[1, h] for h, j in units])
        s = jnp.einsum("uqd,ukd->uqk", q, k, preferred_element_type=F32) * SCALE + bias_b
        m = jnp.max(s, axis=-1, keepdims=True)
        p = jnp.exp(s - m)
        l = jnp.sum(p, axis=-1, keepdims=True)
        o = jnp.einsum("uqk,ukd->uqd", p.astype(BF16), v, preferred_element_type=F32) / l
        lse = m + jnp.log(l)
        for i, (h, j) in enumerate(units):
            o_ref[j * QB:(j + 1) * QB, h * HD:(h + 1) * HD] = o[i]
            ls[h][j * QB:(j + 1) * QB, :] = jnp.where(lane == h, lse[i], 0.0)
        l_ref[...] = (ls[0][...] + ls[1][...]) + (ls[2][...] + ls[3][...])

    def row(b, n):
        return b * nsb + n

    def prev(b, n):
        return jnp.maximum((b * nsb + n) * nq - 1, 0)

    in_specs = [
        pl.BlockSpec((rows, GW), lambda b, n: (row(b, n), CB_Q)),
        pl.BlockSpec((rows, GW), lambda b, n: (row(b, n), CB_K)),
        pl.BlockSpec((rows, GW), lambda b, n: (row(b, n), CB_V)),
        pl.BlockSpec((QB, GW), lambda b, n: (prev(b, n), CB_K)),
        pl.BlockSpec((QB, GW), lambda b, n: (prev(b, n), CB_V)),
        pl.BlockSpec((None, 2, 4, QB, 2 * QB), lambda b, n: (0, 0, 0, 0, 0)),
    ]
    return pl.pallas_call(
        body, name="attn_fwd0",
        grid=(BL, nsb),
        in_specs=in_specs,
        out_specs=(pl.BlockSpec((rows, GW), lambda b, n: (row(b, n), 0)),
                   pl.BlockSpec((rows, 128), lambda b, n: (row(b, n), 0))),
        out_shape=(jax.ShapeDtypeStruct((T, GW), F32), jax.ShapeDtypeStruct((T, 128), F32)),
        scratch_shapes=[pltpu.VMEM((rows, 128), F32)] * 4,
        compiler_params=pltpu.CompilerParams(vmem_limit_bytes=VMEM_LIMIT),
    )(proj, proj, proj, proj, proj, bias)


def _attn_bwd_dense(proj, d_out, stats, bias, dproj):
    nq = 4
    rows = nq * QB
    nsb = S // rows
    cols_q, cols_k, cols_v = CB * CB_Q, CB * CB_K, CB * CB_V

    def body(q_ref, k_ref, v_ref, do_ref, st_ref, kp_ref, vp_ref, b_ref, _, out_ref, db_ref,
             sq, sk, sv, carry, sems):
        b, n = pl.program_id(0), pl.program_id(1)
        units = [(h, j) for h in range(4) for j in range(nq)]

        @pl.when((b == 0) & (n == 0))
        def _():
            db_ref[...] = jnp.zeros_like(db_ref)

        @pl.when(n == 0)
        def _():
            carry[...] = jnp.zeros_like(carry)

        def write(first_block, position, count):
            row0 = pl.multiple_of(first_block * QB, QB)
            part = pl.ds(position * QB, count * QB)
            _write_columns([(sq.at[part], cols_q), (sk.at[part], cols_k), (sv.at[part], cols_v)],
                           out_ref, row0, sems)

        @pl.when(n == nsb)
        def _():
            sq[0:QB, :] = carry[:, 0:GW].astype(BF16)
            sk[0:QB, :] = carry[:, GW:2 * GW].astype(BF16)
            sv[0:QB, :] = carry[:, 2 * GW:3 * GW].astype(BF16)
            write((b + 1) * nsb * nq - 1, 0, 1)

        @pl.when(n < nsb)
        def _():
            def keys(cur_ref, prev_ref, h, j):
                sl = slice(h * HD, (h + 1) * HD)
                if j == 0:
                    return jnp.concatenate([prev_ref[:, sl], cur_ref[0:QB, sl]], axis=0)
                return cur_ref[(j - 1) * QB:(j + 1) * QB, sl]

            def block(ref, h, j):
                return ref[j * QB:(j + 1) * QB, h * HD:(h + 1) * HD]

            q = jnp.stack([block(q_ref, h, j) for h, j in units])
            do = jnp.stack([block(do_ref, h, j) for h, j in units])
            k = jnp.stack([keys(k_ref, kp_ref, h, j) for h, j in units])
            v = jnp.stack([keys(v_ref, vp_ref, h, j) for h, j in units])
            bias_b = jnp.stack([b_ref[jnp.minimum(n, 1), h] if j == 0 else b_ref[1, h] for h, j in units])
            lse = jnp.stack([st_ref[j * QB:(j + 1) * QB, h:h + 1] for h, j in units])
            delta = jnp.stack([st_ref[j * QB:(j + 1) * QB, 4 + h:5 + h] for h, j in units])
            s = jnp.einsum("uqd,ukd->uqk", q, k, preferred_element_type=F32) * SCALE + bias_b
            p = jnp.exp(s - lse)
            ds = p * (jnp.einsum("uqd,ukd->uqk", do, v, preferred_element_type=F32) - delta)
            for h in range(4):
                tot = ds[h * nq]
                for j in range(1, nq):
                    tot = tot + ds[h * nq + j]
                db_ref[h] += tot
            dsb, pb = ds.astype(BF16), p.astype(BF16)
            dq = jnp.einsum("uqk,ukd->uqd", dsb, k, preferred_element_type=F32) * SCALE
            dk = jnp.einsum("uqk,uqd->ukd", dsb, q, preferred_element_type=F32) * SCALE
            dv = jnp.einsum("uqk,uqd->ukd", pb, do, preferred_element_type=F32)
            for h in range(4):
                sl = slice(h * HD, (h + 1) * HD)
                u0, last = h * nq, h * nq + nq - 1
                sq[0:QB, sl] = carry[:, sl].astype(BF16)
                sk[0:QB, sl] = (carry[:, GW + h * HD:GW + (h + 1) * HD] + dk[u0, :QB]).astype(BF16)
                sv[0:QB, sl] = (carry[:, 2 * GW + h * HD:2 * GW + (h + 1) * HD] + dv[u0, :QB]).astype(BF16)
                for j in range(nq - 1):
                    pos = slice((j + 1) * QB, (j + 2) * QB)
                    sq[pos, sl] = dq[u0 + j].astype(BF16)
                    sk[pos, sl] = (dk[u0 + j, QB:] + dk[u0 + j + 1, :QB]).astype(BF16)
                    sv[pos, sl] = (dv[u0 + j, QB:] + dv[u0 + j + 1, :QB]).astype(BF16)
                carry[:, sl] = dq[last]
                carry[:, GW + h * HD:GW + (h + 1) * HD] = dk[last, QB:]
                carry[:, 2 * GW + h * HD:2 * GW + (h + 1) * HD] = dv[last, QB:]

            @pl.when(n == 0)
            def _():
                write(b * nsb * nq, 1, nq - 1)

            @pl.when(n > 0)
            def _():
                write((b * nsb + n) * nq - 1, 0, nq)

    def row(b, n):
        return b * nsb + jnp.minimum(n, nsb - 1)

    def prev(b, n):
        return jnp.maximum(row(b, n) * nq - 1, 0)

    in_specs = [
        pl.BlockSpec((rows, GW), lambda b, n: (row(b, n), CB_Q)),
        pl.BlockSpec((rows, GW), lambda b, n: (row(b, n), CB_K)),
        pl.BlockSpec((rows, GW), lambda b, n: (row(b, n), CB_V)),
        pl.BlockSpec((rows, GW), lambda b, n: (row(b, n), 0)),
        pl.BlockSpec((rows, 128), lambda b, n: (row(b, n), 0)),
        pl.BlockSpec((QB, GW), lambda b, n: (prev(b, n), CB_K)),
        pl.BlockSpec((QB, GW), lambda b, n: (prev(b, n), CB_V)),
        pl.BlockSpec((None, 2, 4, QB, 2 * QB), lambda b, n: (0, 0, 0, 0, 0)),
        pl.BlockSpec(memory_space=pl.ANY),
    ]
    return pl.pallas_call(
        body, name="attn_bwd0",
        grid=(BL, nsb + 1),
        in_specs=in_specs,
        out_specs=(pl.BlockSpec(memory_space=pl.ANY),
                   pl.BlockSpec((4, QB, 2 * QB), lambda b, n: (0, 0, 0))),
        out_shape=(jax.ShapeDtypeStruct((T, NCOL), BF16),
                   jax.ShapeDtypeStruct((4, QB, 2 * QB), F32)),
        scratch_shapes=[pltpu.VMEM((rows, GW), BF16)] * 3
                       + [pltpu.VMEM((QB, 3 * GW), F32), pltpu.SemaphoreType.DMA((3,))],
        input_output_aliases={8: 0},
        compiler_params=pltpu.CompilerParams(vmem_limit_bytes=VMEM_LIMIT),
    )(proj, proj, proj, d_out, stats, proj, proj, bias, dproj)


def _attention_forward(proj, rel_bias):
    expand_lanes, grad_lanes, masks = (jnp.asarray(t) for t in _bucket_maps())
    bias = _bias_expand(rel_bias, expand_lanes, masks)
    fwd = [_attn_fwd_dense(proj, bias)] + [_attn_fwd(proj, bias, g) for g in (1, 2)]
    return bias, grad_lanes, [f[0] for f in fwd], [f[1] for f in fwd]


def _local_step(x2, tgt2, mod3, h, proj, attn, w_ao, w_co, w_o, conv_w, conv_b, ln_g, ln_b):
    bias, buckets, o_g, lse_g = attn

    (dproj, dyc, d_o, stats, dxd, gw_o, gw_co, gw_ao, tail_vec) = _tail(
        x2, tgt2, mod3, o_g, lse_g, proj, w_ao, w_co, w_o, conv_w, conv_b, ln_g, ln_b)

    dproj, db = _attn_bwd_dense(proj, d_o, stats, bias, dproj)
    dbias = [db]
    for g in (1, 2):
        dproj, db = _attn_bwd(proj, d_o, stats, bias, dproj, g)
        dbias.append(db)
    g_rel_bias = _bias_grad(*dbias, buckets)
    dproj, conv_vec = _conv_bwd(dyc, proj, conv_w, dproj)

    gw_ao = jnp.transpose(gw_ao.reshape(GW, N_DEV, D // N_DEV), (1, 0, 2))
    return dproj, dxd, gw_ao, gw_co, gw_o, conv_vec, g_rel_bias, tail_vec


def kernel(x, c, w_ada, b_ada, w_in, conv_w, conv_b, rel_bias, w_attn_out, w_conv_out, w_o, ln_g, ln_b, loss_target, m_w_ada, m_b_ada, m_w_in, m_conv_w, m_conv_b, m_rel_bias, m_w_attn_out, m_w_conv_out, m_w_o, m_ln_g, m_ln_b, v_w_ada, v_b_ada, v_w_in, v_conv_w, v_conv_b, v_rel_bias, v_w_attn_out, v_w_conv_out, v_w_o, v_ln_g, v_ln_b):
    me = _my_index()
    x2 = x.reshape(T, D)
    tgt2 = loss_target.reshape(T, D)

    b_cols = lax.dynamic_slice(b_ada, (0, me * ADA_SHARD), (1, ADA_SHARD))
    c_g, mod_in = _mod_exchange(jnp.pad(c, ((0, 8 - BL), (0, 0))), w_ada[0], b_cols)
    c_all = c_g[:, 0:BL, :].reshape(N_DEV * BL, D)
    mod3 = jnp.transpose(mod_in[:, 0:BL, :], (1, 0, 2)).reshape(BL, 3, D)

    h = _prep_h(x2, mod3)
    rows_shape = jax.ShapeDtypeStruct((N_DEV, D // N_DEV, D), BF16)
    proj, w_in_all, (w_ao_g, w_co_g, w_o_g, conv_w_g) = _gather_proj(
        _shard_order(), h, w_in[0].astype(BF16), 1024,
        ([w_attn_out[0].astype(BF16), w_conv_out[0].astype(BF16), w_o[0].astype(BF16), conv_w[0]],
         [jax.ShapeDtypeStruct((N_DEV, GW, D // N_DEV), BF16), rows_shape, rows_shape,
          jax.ShapeDtypeStruct((N_DEV, 3, D // N_DEV), F32)]))

    attn = _attention_forward(proj, rel_bias)
    w_ao_full = jnp.transpose(w_ao_g, (1, 0, 2)).reshape(GW, D)
    w_co_full = w_co_g.reshape(D, D)
    w_o_full = w_o_g.reshape(D, D)
    conv_w_full = jnp.transpose(conv_w_g, (1, 0, 2)).reshape(3, D)

    (dproj, dxd, gw_ao, gw_co, gw_o, conv_vec, g_rel_bias, tail_vec) = _local_step(
        x2, tgt2, mod3, h, proj, attn, w_ao_full, w_co_full, w_o_full,
        conv_w_full, conv_b, ln_g, ln_b)

    g_conv_w_blocks = jnp.transpose(conv_vec[0:3].reshape(3, N_DEV, D // N_DEV), (1, 0, 2))
    partials = [gw_ao, gw_co.reshape(N_DEV, D // N_DEV, D), gw_o.reshape(N_DEV, D // N_DEV, D), g_conv_w_blocks]
    w_in_sums, w_in_parts, sib = _gw_in_pair(
        _slice_order(), h, dproj, partials,
        [jax.ShapeDtypeStruct((4, GW, D // N_DEV), BF16),
         jax.ShapeDtypeStruct((4, D // N_DEV, D), BF16),
         jax.ShapeDtypeStruct((4, D // N_DEV, D), BF16),
         jax.ShapeDtypeStruct((4, 3, D // N_DEV), F32)])
    core = lax.axis_index("c").astype(jnp.int32).reshape(1)
    chip_sums = [w_in_sums] + list(_pair_add(core, partials, sib))
    hops = [(3,)] + [(1, 2, 3)] * 4
    grad_x, mod_vec, (r_in, r_ao, r_co, r_o, r_cw) = _dh_dx(
        dproj, w_in_all, x2, dxd, mod3, chip_sums, hops, w_in_parts)

    small = jnp.concatenate([
        tail_vec[0:4],
        jnp.pad(g_rel_bias.reshape(1, N_BUCKETS * N_HEADS), ((0, 0), (0, D - N_BUCKETS * N_HEADS))),
        jnp.zeros((3, D), F32)], axis=0)
    dmod = jnp.concatenate([mod_vec[0:2], mod_vec[2:4], tail_vec[4:6]], axis=1)
    small_g, dmod_g = _all_gather(
        [small, dmod],
        [jax.ShapeDtypeStruct((N_DEV, 8, D), F32), jax.ShapeDtypeStruct((N_DEV, BL, 3 * D), F32)],
        "gather_small")
    dmod_all = dmod_g.reshape(N_DEV * BL, 3 * D)
    small_names = ["b_ada", "conv_b", "rel_bias", "ln_g", "ln_b"]
    small_params = [(b_ada, m_b_ada, v_b_ada), (conv_b, m_conv_b, v_conv_b), (rel_bias, m_rel_bias, v_rel_bias),
                    (ln_g, m_ln_g, v_ln_g), (ln_b, m_ln_b, v_ln_b)]
    loss, small_res = _small_updates(
        small_g, dmod_all, small_g[:, 4, :N_BUCKETS * N_HEADS].reshape(N_DEV, N_BUCKETS, N_HEADS), small_params)
    loss = loss.reshape(())

    dmod_cols = lax.dynamic_slice(dmod_all, (0, me * ADA_SHARD), (N_DEV * BL, ADA_SHARD))
    res = {
        "w_ada": tuple(t[None] for t in _w_ada_update(jnp.transpose(c_all), dmod_cols,
                                                      w_ada[0], m_w_ada[0], v_w_ada[0])),
        "w_in": tuple(t[None] for t in _adamw(r_in, w_in[0], m_w_in[0], v_w_in[0], "adam_w_in", 128)),
    }
    mid_names = ["conv_w", "w_attn_out", "w_conv_out", "w_o"]
    mid_parts = [r_cw, r_ao, r_co, r_o]
    mid_full = [(conv_w, m_conv_w, v_conv_w), (w_attn_out, m_w_attn_out, v_w_attn_out),
                (w_conv_out, m_w_conv_out, v_w_conv_out), (w_o, m_w_o, v_w_o)]
    mid_res = _multi_adamw(mid_parts, [tuple(t[0] for t in wmv) for wmv in mid_full], "adam_mid")
    for nm, wmv, outs4 in zip(mid_names, mid_full, mid_res):
        res[nm] = tuple(t[None] for t in outs4)
    res.update(dict(zip(small_names, small_res)))
    order = ["w_ada", "b_ada", "w_in", "conv_w", "conv_b", "rel_bias", "w_attn_out", "w_conv_out",
             "w_o", "ln_g", "ln_b"]
    outs = [loss, grad_x.reshape(BL, S, D)]
    for k in range(4):
        outs += [res[name][k] for name in order]
    return tuple(outs)
```

```python
import math

import numpy as np
import jax
import jax.numpy as jnp
from jax import lax
from jax.experimental import pallas as pl
from jax.experimental.pallas import tpu as pltpu

F32 = jnp.float32
BF16 = jnp.bfloat16
MESH = pl.DeviceIdType.MESH

N_DEV = 8
D = 1024
S = 2048
BL = 2
T = BL * S
NCOL = 11264
SHARD = NCOL // N_DEV
CB = 512
NCB = NCOL // CB
HD = 128
GW = 512
QB = 128
DILATIONS = (1, 4, 16)
N_STEPS = 128
N_BUCKETS = 32
N_HEADS = 12
ALPHA = 2.0 ** 0.25
LN_EPS = 1e-5
NEG_INF = -1e30
SCALE = HD ** -0.5
ADA_SHARD = 3 * D // N_DEV

CB_Q, CB_K, CB_V, CB_GA = 0, 3, 6, 9
KB_U, KB_BG, KB_CG, KB_GC, KB_MA, KB_MC = 5, 6, 7, 8, 9, 10

ADAM_LR, ADAM_B1, ADAM_B2, ADAM_EPS, ADAM_WD, ADAM_STEP = 0.001, 0.9, 0.999, 1e-08, 0.01, 10

VMEM_LIMIT = 56 * 1024 * 1024
VMEM_LIMIT_TAIL = 62 * 1024 * 1024


def _dot(a, b):
    return jnp.dot(a, b, preferred_element_type=F32)


def _dot_nt(a, b):
    return lax.dot_general(a, b, (((1,), (1,)), ((), ())), preferred_element_type=F32)


def _dot_tn(a, b):
    return lax.dot_general(a, b, (((0,), (0,)), ((), ())), preferred_element_type=F32)


def _sigmoid(v):
    return 1.0 / (1.0 + jnp.exp(-v))


def _write_columns(pieces, dst_hbm, row0, sems):
    copies = []
    for k, (src, col0) in enumerate(pieces):
        rows, width = src.shape
        copies.append(pltpu.make_async_copy(
            src, dst_hbm.at[pl.ds(row0, rows), pl.ds(col0, width)], sems.at[k]))
    for cp in copies:
        cp.start()
    for cp in copies:
        cp.wait()


def _my_index():
    return 4 * lax.axis_index("x") + 2 * lax.axis_index("y") + lax.axis_index("c")


class _Gather:
    def __init__(self, ins, outs, stage, send_sems, recv_sems, local_sems):
        self.ins, self.outs, self.stage = ins, outs, stage
        self.send_sems, self.recv_sems, self.local_sems = send_sems, recv_sems, local_sems
        x, y, c = lax.axis_index("x"), lax.axis_index("y"), lax.axis_index("c")
        self.c = c
        self.me, self.sibling = (x, y, c), (x, y, 1 - c)
        self.chips = [(1 - x, y), (x, 1 - y), (1 - x, 1 - y)]

    @staticmethod
    def scratch(arrs):
        n = len(arrs)
        return ([pltpu.SemaphoreType.DMA((7 * n,)), pltpu.SemaphoreType.DMA((7 * n,)),
                 pltpu.SemaphoreType.DMA((n,))] + [pltpu.VMEM(a.shape, a.dtype) for a in arrs])

    def _copy(self, a, k, block, to, src=None):
        dst = self.outs[a].at[4 * block[0] + 2 * block[1] + block[2]]
        return pltpu.make_async_remote_copy(
            src_ref=dst if src is None else src, dst_ref=dst,
            send_sem=self.send_sems.at[a * 7 + k], recv_sem=self.recv_sems.at[a * 7 + k],
            device_id=to, device_id_type=MESH)

    def _first(self):
        first = []
        for a in range(len(self.ins)):
            first.append(self._copy(a, 0, self.me, self.sibling, src=self.ins[a]))
            first += [self._copy(a, 1 + j, self.me, (*chip, self.c), src=self.ins[a])
                      for j, chip in enumerate(self.chips)]
        return first

    def _mine(self):
        me = self.me
        return [pltpu.make_async_copy(self.stage[a], self.outs[a].at[4 * me[0] + 2 * me[1] + me[2]],
                                      self.local_sems.at[a]) for a in range(len(self.ins))]

    def begin(self):
        for cp in self._first():
            cp.start()
        loads = [pltpu.make_async_copy(self.ins[a], self.stage[a], self.local_sems.at[a])
                 for a in range(len(self.ins))]
        for cp in loads:
            cp.start()
        for cp in loads:
            cp.wait()
        for cp in self._mine():
            cp.start()

    def finish(self):
        n, c, me, sibling = len(self.ins), self.c, self.me, self.sibling
        passed = []
        for j, chip in enumerate(self.chips):
            for a in range(n):
                self._copy(a, 1 + j, (*chip, c), me).wait_recv()
                fwd = self._copy(a, 4 + j, (*chip, c), sibling)
                fwd.start()
                passed.append(fwd)
        for a in range(n):
            self._copy(a, 0, sibling, me).wait_recv()
        for j, chip in enumerate(self.chips):
            for a in range(n):
                self._copy(a, 4 + j, (*chip, 1 - c), me).wait_recv()
        for cp in self._first() + passed:
            cp.wait_send()
        for cp in self._mine():
            cp.wait()


def _all_gather(arrs, out_shapes, name):
    n = len(arrs)

    def body(*refs):
        g = _Gather(refs[:n], refs[n:2 * n], refs[2 * n + 3:], *refs[2 * n:2 * n + 3])
        g.begin()
        g.finish()

    any_spec = pl.BlockSpec(memory_space=pl.ANY)
    return pl.pallas_call(
        body, name=name,
        out_shape=tuple(out_shapes),
        in_specs=[any_spec] * n,
        out_specs=tuple([any_spec] * n),
        scratch_shapes=_Gather.scratch(arrs),
    )(*arrs)


def _neighbour_chips():
    x, y, c = lax.axis_index("x"), lax.axis_index("y"), lax.axis_index("c")
    first = (jnp.where(c == 0, 1 - x, x), jnp.where(c == 0, y, 1 - y))
    second = (jnp.where(c == 0, x, 1 - x), jnp.where(c == 0, 1 - y, y))
    return first, second, (1 - x, 1 - y)


def _slice_order():
    x, y, c = lax.axis_index("x"), lax.axis_index("y"), lax.axis_index("c")
    nb1, nb2, diag = _neighbour_chips()
    slots = []
    for mine, theirs in ((nb1, nb2), (nb2, nb1), (diag, diag), ((x, y), (x, y))):
        slots += [2 * (2 * theirs[0] + theirs[1]) + 1 - c, 2 * (2 * mine[0] + mine[1]) + c]
    return jnp.stack(slots).astype(jnp.int32)


def _gw_in_pair(order, h, dproj, smalls, small_shapes4):
    kk, m = h.shape
    tk = min(kk, 2048)
    nk = kk // tk
    ncols = dproj.shape[1] // N_DEV
    n = len(smalls)

    def body(order_ref, h_ref, d_ref, *rest):
        ins = rest[:n]
        sums_hbm, parts_hbm = rest[n], rest[n + 1]
        sib = rest[n + 2:2 * n + 2]
        (acc, sendbuf, recvbuf, sumbuf, send_sems, recv_sems, local_sem, ssend, srecv,
         isend, irecv) = rest[2 * n + 2:]
        js, k = pl.program_id(0), pl.program_id(1)
        x, y, c = lax.axis_index("x"), lax.axis_index("y"), lax.axis_index("c")
        sibling = (x, y, 1 - c)
        my_chip = 2 * x + y
        nb1, nb2, _ = _neighbour_chips()
        near = [(*nb1, c), (*nb2, c)]

        def ici_copy(p, out_chip):
            peer = near[p]
            return pltpu.make_async_remote_copy(
                src_ref=sumbuf.at[p], dst_ref=parts_hbm.at[out_chip],
                send_sem=isend.at[p], recv_sem=irecv.at[p], device_id=peer, device_id_type=MESH)

        def small_copies():
            return [pltpu.make_async_remote_copy(
                        src_ref=ins[a].at[2 * q + 1 - c], dst_ref=sib[a].at[q],
                        send_sem=ssend.at[a * 4 + q], recv_sem=srecv.at[a * 4 + q],
                        device_id=sibling, device_id_type=MESH)
                    for a in range(n) for q in range(4)]

        def slice_copy(p):
            return pltpu.make_async_remote_copy(
                src_ref=sendbuf, dst_ref=recvbuf.at[p], send_sem=send_sems.at[p], recv_sem=recv_sems.at[p],
                device_id=sibling, device_id_type=MESH)

        def sum_copy(p):
            return pltpu.make_async_copy(sumbuf.at[2], sums_hbm.at[order_ref[2 * p] // 2], local_sem)

        @pl.when((js == 0) & (k == 0))
        def _():
            for cp in small_copies():
                cp.start()

        @pl.when(k == 0)
        def _():
            acc[...] = jnp.zeros_like(acc)

        acc[...] += _dot_tn(h_ref[...], d_ref[...])

        for p in range(4):
            @pl.when((js == 2 * p) & (k == nk - 1))
            def _():
                if p > 0:
                    slice_copy(p - 1).wait_send()
                sendbuf[...] = acc[...].astype(BF16)
                slice_copy(p).start()

            @pl.when((js == 2 * p + 1) & (k == nk - 1))
            def _():
                slice_copy(p).wait_recv()
                if p == 3:
                    sum_copy(2).wait()
                sumbuf[min(p, 2)] = (acc[...] + recvbuf[p].astype(F32)).astype(BF16)
                if p < 2:
                    ici_copy(p, my_chip).start()
                else:
                    sum_copy(p).start()

        @pl.when((js == N_DEV - 1) & (k == nk - 1))
        def _():
            slice_copy(3).wait_send()
            sum_copy(3).wait()
            for cp in small_copies():
                cp.wait()
            for p in range(2):
                ici_copy(p, 2 * near[p][0] + near[p][1]).wait_recv()
                ici_copy(p, my_chip).wait_send()

    any_spec = pl.BlockSpec(memory_space=pl.ANY)
    res = pl.pallas_call(
        body, name="gw_in_pair",
        grid_spec=pltpu.PrefetchScalarGridSpec(
            num_scalar_prefetch=1,
            grid=(N_DEV, nk),
            in_specs=[pl.BlockSpec((tk, m), lambda js, k, order_ref: (k, 0)),
                      pl.BlockSpec((tk, ncols), lambda js, k, order_ref: (k, order_ref[js]))] + [any_spec] * n,
            out_specs=(any_spec,) * (n + 2),
            scratch_shapes=[pltpu.VMEM((m, ncols), F32), pltpu.VMEM((m, ncols), BF16),
                            pltpu.VMEM((4, m, ncols), BF16), pltpu.VMEM((3, m, ncols), BF16),
                            pltpu.SemaphoreType.DMA((4,)), pltpu.SemaphoreType.DMA((4,)),
                            pltpu.SemaphoreType.DMA,
                            pltpu.SemaphoreType.DMA((4 * n,)), pltpu.SemaphoreType.DMA((4 * n,)),
                            pltpu.SemaphoreType.DMA((2,)), pltpu.SemaphoreType.DMA((2,))]),
        out_shape=(jax.ShapeDtypeStruct((4, m, ncols), BF16),) * 2 + tuple(small_shapes4),
        compiler_params=pltpu.CompilerParams(vmem_limit_bytes=VMEM_LIMIT),
    )(order, h, dproj, *smalls)
    return res[0], res[1], res[2:]


def _chip_copies(ins, outs, send_sems, recv_sems, local_sems, hops):
    n = len(ins)
    x, y, c = lax.axis_index("x"), lax.axis_index("y"), lax.axis_index("c")
    my_chip = 2 * x + y

    def peer_of(k):
        return ((1 - x) if (k >> 1) & 1 else x, (1 - y) if k & 1 else y, c)

    def copy(a, k, out_chip):
        peer = peer_of(k)
        return pltpu.make_async_remote_copy(
            src_ref=ins[a].at[2 * peer[0] + peer[1]], dst_ref=outs[a].at[out_chip],
            send_sem=send_sems.at[a * 3 + k - 1], recv_sem=recv_sems.at[a * 3 + k - 1],
            device_id=peer, device_id_type=MESH)

    sends = [copy(a, k, my_chip) for k in range(1, 4) for a in range(n) if k in hops[a]]
    arrivals = []
    for k in range(1, 4):
        peer = peer_of(k)
        arrivals += [copy(a, k, 2 * peer[0] + peer[1]) for a in range(n) if k in hops[a]]
    mine = [pltpu.make_async_copy(ins[a].at[my_chip], outs[a].at[my_chip], local_sems.at[a])
            for a in range(n)]
    return sends, arrivals, mine


def _pair_add(core, mines, theirs):
    n = len(mines)

    def body(core_ref, *refs):
        mine, sib, outs = refs[:n], refs[n:2 * n], refs[2 * n:]
        for a in range(n):
            for q in range(4):
                outs[a][q] = (mine[a][2 * q + core_ref[0]].astype(F32)
                              + sib[a][q].astype(F32)).astype(outs[a].dtype)

    return pl.pallas_call(
        body, name="pair_add",
        in_specs=[pl.BlockSpec(memory_space=pltpu.SMEM)] + [pl.BlockSpec(memory_space=pltpu.VMEM)] * (2 * n),
        out_shape=tuple(jax.ShapeDtypeStruct(t.shape, t.dtype) for t in theirs),
    )(core, *mines, *theirs)


def _mod_exchange(c8, w_ada, b_cols):
    cols = w_ada.shape[1]

    def body(c_ref, w_ref, b_ref, call_ref, mod_ref, msend, send1, recv1, send2, recv2):
        x, y, c = lax.axis_index("x"), lax.axis_index("y"), lax.axis_index("c")
        my_slot = 4 * x + 2 * y + c

        def peer_of(k):
            return ((1 - x) if (k >> 2) & 1 else x, (1 - y) if (k >> 1) & 1 else y, (1 - c) if k & 1 else c)

        def slot_of(dev):
            return 4 * dev[0] + 2 * dev[1] + dev[2]

        def exchange(src_of, dst_ref, send_sems, recv_sems):
            sends, arrivals = [], []
            for k in range(1, 8):
                peer = peer_of(k)
                sends.append(pltpu.make_async_remote_copy(
                    src_ref=src_of(slot_of(peer)), dst_ref=dst_ref.at[my_slot],
                    send_sem=send_sems.at[k - 1], recv_sem=recv_sems.at[k - 1],
                    device_id=peer, device_id_type=MESH))
                arrivals.append(pltpu.make_async_remote_copy(
                    src_ref=src_of(my_slot), dst_ref=dst_ref.at[slot_of(peer)],
                    send_sem=send_sems.at[k - 1], recv_sem=recv_sems.at[k - 1],
                    device_id=peer, device_id_type=MESH))
            for cp in sends:
                cp.start()
            for cp in arrivals:
                cp.wait_recv()
            for cp in sends:
                cp.wait_send()

        call_ref[my_slot] = c_ref[...]
        exchange(lambda s: c_ref, call_ref, send1, recv1)
        cv = call_ref[...].reshape(N_DEV * 8, c_ref.shape[1])
        act = cv * _sigmoid(cv)
        mod = jnp.dot(act, w_ref[...], preferred_element_type=F32,
                      precision=lax.Precision.HIGHEST) + b_ref[...]
        msend[...] = mod.reshape(N_DEV, 8, cols)
        mod_ref[my_slot] = msend[my_slot]
        exchange(lambda s: msend.at[s], mod_ref, send2, recv2)

    return pl.pallas_call(
        body, name="mod_exchange",
        out_shape=(jax.ShapeDtypeStruct((N_DEV, 8, c8.shape[1]), F32),
                   jax.ShapeDtypeStruct((N_DEV, 8, cols), F32)),
        scratch_shapes=[pltpu.VMEM((N_DEV, 8, cols), F32)] + [pltpu.SemaphoreType.DMA((7,))] * 4,
    )(c8, w_ada, b_cols)


def _w_ada_update(c_all_t, dmod_cols, w, m, v):
    rows, cols = w.shape
    tr = 256

    def body(c_ref, d_ref, w_ref, m_ref, v_ref, g_ref, dl_ref, nm_ref, nv_ref):
        cv = c_ref[...]
        g = jnp.dot(cv * _sigmoid(cv), d_ref[...], preferred_element_type=F32,
                    precision=lax.Precision.HIGHEST)
        g_ref[...] = g
        dl_ref[...], nm_ref[...], nv_ref[...] = _adam_step(g, w_ref[...], m_ref[...], v_ref[...])

    blk = pl.BlockSpec((tr, cols), lambda i: (i, 0))
    shp = jax.ShapeDtypeStruct((rows, cols), F32)
    return pl.pallas_call(
        body, name="adam_w_ada",
        grid=(rows // tr,),
        in_specs=[pl.BlockSpec((tr, c_all_t.shape[1]), lambda i: (i, 0)),
                  pl.BlockSpec(dmod_cols.shape, lambda i: (0, 0)), blk, blk, blk],
        out_specs=(blk, blk, blk, blk),
        out_shape=(shp, shp, shp, shp),
    )(c_all_t, dmod_cols, w, m, v)


def _shard_order():
    x, y, c = lax.axis_index("x"), lax.axis_index("y"), lax.axis_index("c")
    first, second, diag = _neighbour_chips()
    devs = [(x, y, c), (x, y, 1 - c), (*first, c), (*second, 1 - c), (*second, c), (*first, 1 - c),
            (*diag, c), (*diag, 1 - c)]
    return jnp.stack([4 * d[0] + 2 * d[1] + d[2] for d in devs]).astype(jnp.int32)


def _prep_h(x2, mod3):
    ts = 512
    per_seq = S // ts

    def body(x_ref, mod_ref, h_ref):
        shift = mod_ref[0, 0:1, :]
        scale = mod_ref[0, 1:2, :]
        h_ref[...] = (x_ref[...] * (1.0 + scale) + shift).astype(BF16)

    return pl.pallas_call(
        body, name="prep_h",
        grid=(T // ts,),
        in_specs=[pl.BlockSpec((ts, D), lambda i: (i, 0)),
                  pl.BlockSpec((1, 3, D), lambda i: (i // per_seq, 0, 0))],
        out_specs=pl.BlockSpec((ts, D), lambda i: (i, 0)),
        out_shape=jax.ShapeDtypeStruct((T, D), BF16),
    )(x2, mod3)


def _gather_proj(order, h, w_shard, tm, ride=()):
    rows, kdim = h.shape
    ncols = w_shard.shape[1]
    n_i = rows // tm
    ride_arrs, ride_shapes = ride if ride else ((), ())
    n_ride = len(ride_arrs)

    def body(order_ref, h_ref, mine_hbm, *rest):
        ride_ins = rest[:n_ride]
        o_ref, all_hbm = rest[n_ride:n_ride + 2]
        ride_outs = rest[n_ride + 2:2 * n_ride + 2]
        wv, send_sems, recv_sems, local_sems = rest[2 * n_ride + 2:2 * n_ride + 6]
        ride_scr = rest[2 * n_ride + 6:]
        j, i = pl.program_id(0), pl.program_id(1)
        c = lax.axis_index("c")
        me, sibling = (lax.axis_index("x"), lax.axis_index("y"), c), (lax.axis_index("x"), lax.axis_index("y"), 1 - c)
        nb1, nb2, diag = _neighbour_chips()

        def slot(dev):
            return 4 * dev[0] + 2 * dev[1] + dev[2]

        def copy(k, block, to, src=None, part=None):
            buf = wv.at[slot(block)]
            if part is not None:
                buf = buf.at[pl.ds(pl.multiple_of(part * (kdim // 2), kdim // 2), kdim // 2)]
            return pltpu.make_async_remote_copy(
                src_ref=buf if src is None else src, dst_ref=buf,
                send_sem=send_sems.at[k], recv_sem=recv_sems.at[k],
                device_id=to, device_id_type=MESH)

        def keep(step, block):
            return pltpu.make_async_copy(wv.at[slot(block)], all_hbm.at[slot(block)], local_sems.at[step])

        if n_ride:
            gather = _Gather(ride_ins, ride_outs, ride_scr[3:], *ride_scr[:3])
        to_sibling, to_nb1, to_nb2 = (copy(0, me, sibling, mine_hbm), copy(1, me, (*nb1, c), mine_hbm),
                                      copy(2, me, (*nb2, c), mine_hbm))
        relay1, relay2 = copy(3, (*nb2, c), (*nb1, c), part=c), copy(4, (*nb1, c), (*nb2, c), part=1 - c)
        pass_nb1, pass_nb2 = copy(5, (*nb1, c), sibling), copy(6, (*nb2, c), sibling)
        pass_d1, pass_d2 = copy(7, (*diag, c), sibling, part=c), copy(8, (*diag, c), sibling, part=1 - c)
        sends = [to_sibling, to_nb1, to_nb2, relay1, relay2, pass_nb1, pass_nb2, pass_d1, pass_d2]
        due = [
            (me, [], []),
            (sibling, [copy(0, sibling, me)], [[]]),
            ((*nb1, c), [copy(1, (*nb1, c), me)], [[pass_nb1, to_nb2]]),
            ((*nb2, 1 - c), [copy(5, (*nb2, 1 - c), me)], [[]]),
            ((*nb2, c), [copy(2, (*nb2, c), me)], [[pass_nb2, relay1, relay2]]),
            ((*nb1, 1 - c), [copy(6, (*nb1, 1 - c), me)], [[]]),
            ((*diag, c), [copy(3, (*diag, c), me, part=c), copy(4, (*diag, c), me, part=1 - c)],
             [[pass_d1], [pass_d2]]),
            ((*diag, 1 - c), [copy(7, (*diag, 1 - c), me, part=1 - c), copy(8, (*diag, 1 - c), me, part=c)],
             [[], []]),
        ]

        @pl.when((j == 0) & (i == 0))
        def _():
            to_sibling.start()
            to_nb1.start()
            load = pltpu.make_async_copy(mine_hbm, wv.at[slot(me)], local_sems.at[N_DEV])
            load.start()
            load.wait()
            keep(0, me).start()

        for step in range(1, N_DEV):
            block, arrivals, then = due[step]

            @pl.when((j == step) & (i == 0))
            def _():
                for arrival, follow in zip(arrivals, then):
                    arrival.wait_recv()
                    for cp in follow:
                        cp.start()
                keep(step, block).start()
                if n_ride and step == N_DEV - 2:
                    gather.begin()

        o_ref[...] = _dot(h_ref[...], wv[order_ref[j]]).astype(BF16)

        @pl.when((j == N_DEV - 1) & (i == n_i - 1))
        def _():
            for cp in sends:
                cp.wait_send()
            for step in range(N_DEV):
                keep(step, due[step][0]).wait()
            if n_ride:
                gather.finish()

    any_spec = pl.BlockSpec(memory_space=pl.ANY)
    res = pl.pallas_call(
        body, name="gather_proj",
        grid_spec=pltpu.PrefetchScalarGridSpec(
            num_scalar_prefetch=1,
            grid=(N_DEV, n_i),
            in_specs=[pl.BlockSpec((tm, kdim), lambda j, i, order_ref: (i, 0)), any_spec] + [any_spec] * n_ride,
            out_specs=(pl.BlockSpec((tm, ncols), lambda j, i, order_ref: (i, order_ref[j])), any_spec)
                      + (any_spec,) * n_ride,
            scratch_shapes=[pltpu.VMEM((N_DEV, kdim, ncols), BF16),
                            pltpu.SemaphoreType.DMA((9,)), pltpu.SemaphoreType.DMA((9,)),
                            pltpu.SemaphoreType.DMA((N_DEV + 1,))]
                           + (_Gather.scratch(ride_arrs) if n_ride else [])),
        out_shape=(jax.ShapeDtypeStruct((rows, N_DEV * ncols), BF16),
                   jax.ShapeDtypeStruct((N_DEV, kdim, ncols), BF16)) + tuple(ride_shapes),
        compiler_params=pltpu.CompilerParams(vmem_limit_bytes=VMEM_LIMIT),
    )(order, h, w_shard, *ride_arrs)
    return res[0], res[1], res[2:]


SKEW_W = 512


def _bucket_maps():
    lanes = np.arange(SKEW_W)

    def buckets_of(steps):
        rows = []
        for dil in DILATIONS:
            dist = np.maximum(steps, 0) * dil
            nf = np.maximum(dist, 1).astype(np.float32)
            large = 16 + (np.log(nf / np.float32(16)) / np.float32(math.log(128.0))
                          * np.float32(16)).astype(np.int32)
            large = np.minimum(large, N_BUCKETS - 1)
            bucket = np.where(dist < 16, dist, large)
            rows.append(np.where((steps >= 0) & (steps <= N_STEPS), bucket, -1).astype(np.int32))
        return np.stack(rows)[:, None, :]

    a = np.arange(QB)[:, None]
    b = np.arange(2 * QB)[None, :]
    steps = a + QB - b
    band = (steps >= 0) & (steps <= N_STEPS)
    first = band & (b >= QB)
    masks = np.stack([first, band]).astype(np.int32)
    return buckets_of(QB - lanes), buckets_of(2 * QB - 1 - lanes), masks


def _bias_expand(rel_bias, lane_buckets, masks):
    def body(tab_ref, bk_ref, mk_ref, o_ref):
        for g in range(3):
            bk = bk_ref[g]
            for h in range(4):
                col = 4 * g + h
                per_offset = jnp.zeros((1, SKEW_W), F32)
                for k in range(N_BUCKETS):
                    per_offset = jnp.where(bk == k, tab_ref[k, col], per_offset)
                tile = pltpu.roll(jnp.broadcast_to(per_offset, (QB, SKEW_W)), 0, 1, stride=1, stride_axis=0)
                tile = tile[:, :2 * QB]
                o_ref[g, 0, h] = jnp.where(mk_ref[0] != 0, tile, NEG_INF)
                o_ref[g, 1, h] = jnp.where(mk_ref[1] != 0, tile, NEG_INF)

    return pl.pallas_call(
        body, name="bias_expand",
        in_specs=[pl.BlockSpec(memory_space=pltpu.SMEM),
                  pl.BlockSpec(memory_space=pltpu.VMEM),
                  pl.BlockSpec(memory_space=pltpu.VMEM)],
        out_shape=jax.ShapeDtypeStruct((3, 2, 4, QB, 2 * QB), F32),
    )(rel_bias, lane_buckets, masks)


def _bias_grad(ds1, ds2, ds3, lane_buckets):
    exchange = jnp.asarray(np.eye(QB, dtype=np.float32)[::-1].copy())

    def body(d1_ref, d2_ref, d3_ref, bk_ref, ex_ref, o_ref):
        for g, d_ref in enumerate((d1_ref, d2_ref, d3_ref)):
            bk = bk_ref[g]
            for h in range(4):
                flipped = jnp.dot(ex_ref[...], d_ref[h], preferred_element_type=F32,
                                  precision=lax.Precision.HIGHEST)
                padded = jnp.concatenate([flipped, jnp.zeros((QB, SKEW_W - 2 * QB), F32)], axis=1)
                skewed = pltpu.roll(padded, 0, 1, stride=1, stride_axis=0)
                per_offset = jnp.sum(skewed, axis=0, keepdims=True)
                for k in range(N_BUCKETS):
                    o_ref[k, 4 * g + h] = jnp.sum(jnp.where(bk == k, per_offset, 0.0))

    return pl.pallas_call(
        body, name="bias_grad",
        in_specs=[pl.BlockSpec(memory_space=pltpu.VMEM)] * 5,
        out_specs=pl.BlockSpec(memory_space=pltpu.SMEM),
        out_shape=jax.ShapeDtypeStruct((N_BUCKETS, N_HEADS), F32),
    )(ds1, ds2, ds3, lane_buckets, exchange)


def _scratch_sets(rows):
    return 4 if rows <= 512 else 1


def _unit_chunks(dil, size=16):
    units = [(h, r) for h in range(4) for r in range(dil)]
    return [units[i:i + size] for i in range(0, len(units), size)]


def _residue_rows(src_ref, copies, h, residue):
    buf = copies[h % len(copies)]
    buf[...] = src_ref[:, h * HD:(h + 1) * HD].astype(F32)
    return lambda r: buf[residue(r), :].astype(BF16)


def _attn_fwd(proj, bias, g):
    dil = DILATIONS[g]
    rows = QB * dil
    nsb = S // rows
    has_prev = nsb > 1

    def residue(r):
        return pl.ds(r, QB, stride=dil)

    n_sets = _scratch_sets(rows)
    n_in = 6 if has_prev else 4
    n_copied = (4 + (2 if has_prev else 0)) * n_sets

    def body(*refs):
        q_ref, kc_ref, vc_ref = refs[:3]
        kp_ref, vp_ref = refs[3:5] if has_prev else (None, None)
        b_ref = refs[n_in - 1]
        o_ref, l_ref = refs[n_in:n_in + 2]
        scr = list(refs[n_in + 2:])
        ls = [scr.pop(0) for _ in range(4)]
        copies = {name: [scr.pop(0) for _ in range(n_sets)]
                  for name in ("q", "kc", "vc", "o") + (("kp", "vp") if has_prev else ())}
        lane = lax.broadcasted_iota(jnp.int32, (QB, 128), 1)
        refs_of = {"q": q_ref, "kc": kc_ref, "vc": vc_ref, "kp": kp_ref, "vp": vp_ref}
        for chunk in _unit_chunks(dil):
            rows_of = {h: {name: _residue_rows(refs_of[name], copies[name], h, residue)
                           for name in refs_of if refs_of[name] is not None}
                       for h in sorted({h for h, _ in chunk})}

            def batch(name):
                return jnp.stack([rows_of[h][name](r) for h, r in chunk])

            q, k, v = batch("q"), batch("kc"), batch("vc")
            if has_prev:
                k = jnp.concatenate([batch("kp"), k], axis=1)
                v = jnp.concatenate([batch("vp"), v], axis=1)
                bias_b = jnp.stack([b_ref[h] for h, _ in chunk])
            else:
                bias_b = jnp.stack([b_ref[h, :, QB:] for h, _ in chunk])
            s = jnp.einsum("uqd,ukd->uqk", q, k, preferred_element_type=F32) * SCALE + bias_b
            m = jnp.max(s, axis=-1, keepdims=True)
            p = jnp.exp(s - m)
            l = jnp.sum(p, axis=-1, keepdims=True)
            o = jnp.einsum("uqk,ukd->uqd", p.astype(BF16), v, preferred_element_type=F32) / l
            lse = m + jnp.log(l)
            for i, (h, r) in enumerate(chunk):
                copies["o"][h % n_sets][residue(r), :] = o[i]
                ls[h][r * QB:(r + 1) * QB, :] = jnp.where(lane == h, lse[i], 0.0)
            for h in sorted({h for h, _ in chunk}):
                o_ref[:, h * HD:(h + 1) * HD] = copies["o"][h % n_sets][...]
        for r in range(dil):
            blk = slice(r * QB, (r + 1) * QB)
            l_ref[residue(r), :] = (ls[0][blk, :] + ls[1][blk, :]) + (ls[2][blk, :] + ls[3][blk, :])

    def row(b, n):
        return b * nsb + n

    def prev(b, n):
        return b * nsb + jnp.maximum(n - 1, 0)

    in_specs = [
        pl.BlockSpec((rows, GW), lambda b, n: (row(b, n), CB_Q + g)),
        pl.BlockSpec((rows, GW), lambda b, n: (row(b, n), CB_K + g)),
        pl.BlockSpec((rows, GW), lambda b, n: (row(b, n), CB_V + g)),
    ]
    args = [proj, proj, proj]
    scratch = [pltpu.VMEM((rows, 128), F32)] * (4 + n_copied)
    if has_prev:
        in_specs += [pl.BlockSpec((rows, GW), lambda b, n: (prev(b, n), CB_K + g)),
                     pl.BlockSpec((rows, GW), lambda b, n: (prev(b, n), CB_V + g))]
        args += [proj, proj]
    in_specs.append(pl.BlockSpec((None, None, 4, QB, 2 * QB),
                                 lambda b, n: (g, jnp.minimum(n, 1), 0, 0, 0)))
    args.append(bias)
    return pl.pallas_call(
        body, name=f"attn_fwd{g}",
        grid=(BL, nsb),
        in_specs=in_specs,
        out_specs=(pl.BlockSpec((rows, GW), lambda b, n: (row(b, n), 0)),
                   pl.BlockSpec((rows, 128), lambda b, n: (row(b, n), 0))),
        out_shape=(jax.ShapeDtypeStruct((T, GW), F32), jax.ShapeDtypeStruct((T, 128), F32)),
        scratch_shapes=scratch,
        compiler_params=pltpu.CompilerParams(vmem_limit_bytes=VMEM_LIMIT),
    )(*args)


def _attn_bwd(proj, d_out, stats, bias, dproj, g):
    dil = DILATIONS[g]
    rows = QB * dil
    nsb = S // rows
    has_prev = nsb > 1
    n_steps = nsb + 1 if has_prev else 1
    n_in = 7 + (2 if has_prev else 0)

    def residue(r):
        return pl.ds(r, QB, stride=dil)

    n_sets = _scratch_sets(rows)

    def body(*refs):
        q_ref, kc_ref, vc_ref, do_ref, st_ref, b_ref = refs[:6]
        kp_ref, vp_ref = refs[6:8] if has_prev else (None, None)
        out_ref, db_ref = refs[n_in], refs[n_in + 1]
        scr = list(refs[n_in + 2:])
        sq, sk, sv, sems = [scr.pop(0) for _ in range(4)]
        carry = scr.pop(0) if has_prev else None
        sts = scr.pop(0)
        copies = {name: [scr.pop(0) for _ in range(n_sets)]
                  for name in ("q", "kc", "vc", "do", "dq", "dk", "dv") + (("kp", "vp") if has_prev else ())}
        b, n = pl.program_id(0), pl.program_id(1)

        @pl.when((b == 0) & (n == 0))
        def _():
            db_ref[...] = jnp.zeros_like(db_ref)

        def finish(h, r, dq, dk, dv):
            for name, val in (("dq", dq), ("dk", dk), ("dv", dv)):
                copies[name][h % n_sets][residue(r), :] = val

        def finish_head(h):
            sl = slice(h * HD, (h + 1) * HD)
            sq[:, sl] = copies["dq"][h % n_sets][...].astype(BF16)
            sk[:, sl] = copies["dk"][h % n_sets][...].astype(BF16)
            sv[:, sl] = copies["dv"][h % n_sets][...].astype(BF16)

        def write_block(blk_idx):
            row0 = pl.multiple_of(blk_idx * rows, rows)
            _write_columns([(sq, CB * (CB_Q + g)), (sk, CB * (CB_K + g)), (sv, CB * (CB_V + g))],
                           out_ref, row0, sems)

        def carried(h, r):
            blk = slice(r * QB, (r + 1) * QB)
            return ((blk, slice(h * HD, (h + 1) * HD)), (blk, slice(GW + h * HD, GW + (h + 1) * HD)),
                    (blk, slice(2 * GW + h * HD, 2 * GW + (h + 1) * HD)))

        if has_prev:
            @pl.when(n == 0)
            def _():
                carry[...] = jnp.zeros_like(carry)

            @pl.when(n == nsb)
            def _():
                for h in range(4):
                    for r in range(dil):
                        cq, ck, cv = carried(h, r)
                        finish(h, r, carry[cq], carry[ck], carry[cv])
                    finish_head(h)
                write_block(b * nsb + nsb - 1)

        @pl.when(n < nsb)
        def _():
            for r in range(dil):
                sts[r * QB:(r + 1) * QB, :] = st_ref[residue(r), :]
            refs_of = {"q": q_ref, "kc": kc_ref, "vc": vc_ref, "do": do_ref, "kp": kp_ref, "vp": vp_ref}
            for chunk in _unit_chunks(dil):
                heads = sorted({h for h, _ in chunk})
                rows_of = {h: {name: _residue_rows(refs_of[name], copies[name], h, residue)
                               for name in refs_of if refs_of[name] is not None}
                           for h in heads}

                def batch(name):
                    return jnp.stack([rows_of[h][name](r) for h, r in chunk])

                q, k, v, do = batch("q"), batch("kc"), batch("vc"), batch("do")
                if has_prev:
                    k = jnp.concatenate([batch("kp"), k], axis=1)
                    v = jnp.concatenate([batch("vp"), v], axis=1)
                    bias_b = jnp.stack([b_ref[h] for h, _ in chunk])
                else:
                    bias_b = jnp.stack([b_ref[h, :, QB:] for h, _ in chunk])
                lse = jnp.stack([sts[r * QB:(r + 1) * QB, h:h + 1] for h, r in chunk])
                delta = jnp.stack([sts[r * QB:(r + 1) * QB, 4 + h:5 + h] for h, r in chunk])
                s = jnp.einsum("uqd,ukd->uqk", q, k, preferred_element_type=F32) * SCALE + bias_b
                p = jnp.exp(s - lse)
                ds = p * (jnp.einsum("uqd,ukd->uqk", do, v, preferred_element_type=F32) - delta)
                for h in heads:
                    mine = [ds[i] for i, (hh, _) in enumerate(chunk) if hh == h]
                    tot = mine[0]
                    for extra in mine[1:]:
                        tot = tot + extra
                    if has_prev:
                        db_ref[h] += tot
                    else:
                        db_ref[h, :, QB:] += tot
                dsb, pb = ds.astype(BF16), p.astype(BF16)
                dq = jnp.einsum("uqk,ukd->uqd", dsb, k, preferred_element_type=F32) * SCALE
                dk = jnp.einsum("uqk,uqd->ukd", dsb, q, preferred_element_type=F32) * SCALE
                dv = jnp.einsum("uqk,uqd->ukd", pb, do, preferred_element_type=F32)
                for i, (h, r) in enumerate(chunk):
                    if has_prev:
                        cq, ck, cv = carried(h, r)
                        finish(h, r, carry[cq], carry[ck] + dk[i, :QB], carry[cv] + dv[i, :QB])
                        carry[cq] = dq[i]
                        carry[ck] = dk[i, QB:]
                        carry[cv] = dv[i, QB:]
                    else:
                        finish(h, r, dq[i], dk[i], dv[i])
                for h in heads:
                    finish_head(h)
            if has_prev:
                @pl.when(n > 0)
                def _():
                    write_block(b * nsb + n - 1)
            else:
                write_block(b)

    def row(b, n):
        return b * nsb + jnp.minimum(n, nsb - 1)

    def prev(b, n):
        return b * nsb + jnp.maximum(jnp.minimum(n, nsb - 1) - 1, 0)

    in_specs = [
        pl.BlockSpec((rows, GW), lambda b, n: (row(b, n), CB_Q + g)),
        pl.BlockSpec((rows, GW), lambda b, n: (row(b, n), CB_K + g)),
        pl.BlockSpec((rows, GW), lambda b, n: (row(b, n), CB_V + g)),
        pl.BlockSpec((rows, GW), lambda b, n: (row(b, n), 0)),
        pl.BlockSpec((rows, 128), lambda b, n: (row(b, n), 0)),
        pl.BlockSpec((None, None, 4, QB, 2 * QB),
                     lambda b, n: (g, jnp.minimum(jnp.minimum(n, nsb - 1), 1), 0, 0, 0)),
    ]
    args = [proj, proj, proj, d_out, stats, bias]
    scratch = [pltpu.VMEM((rows, GW), BF16)] * 3 + [pltpu.SemaphoreType.DMA((3,))]
    if has_prev:
        in_specs += [pl.BlockSpec((rows, GW), lambda b, n: (prev(b, n), CB_K + g)),
                     pl.BlockSpec((rows, GW), lambda b, n: (prev(b, n), CB_V + g))]
        args += [proj, proj]
        scratch.append(pltpu.VMEM((rows, 3 * GW), F32))
    n_copied = (7 + (2 if has_prev else 0)) * n_sets
    scratch += [pltpu.VMEM((rows, 128), F32)] * (1 + n_copied)
    in_specs.append(pl.BlockSpec(memory_space=pl.ANY))
    args.append(dproj)
    return pl.pallas_call(
        body, name=f"attn_bwd{g}",
        grid=(BL, n_steps),
        in_specs=in_specs,
        out_specs=(pl.BlockSpec(memory_space=pl.ANY),
                   pl.BlockSpec((4, QB, 2 * QB), lambda b, n: (0, 0, 0))),
        out_shape=(jax.ShapeDtypeStruct((T, NCOL), BF16),
                   jax.ShapeDtypeStruct((4, QB, 2 * QB), F32)),
        scratch_shapes=scratch,
        input_output_aliases={len(args) - 1: 0},
        compiler_params=pltpu.CompilerParams(vmem_limit_bytes=VMEM_LIMIT),
    )(*args)


def _tail(x2, tgt2, mod3, o_g, lse_g, proj, w_ao, w_co, w_o, conv_w, conv_b, ln_g, ln_b):
    tm = 256
    per_seq = S // tm
    halo = 16

    def body(x_ref, t_ref, mod_ref, o1_ref, o2_ref, o3_ref, l1_ref, l2_ref, l3_ref,
             ga_ref, u_ref, bg_ref, cg_ref, gc_ref, ma_ref, mc_ref, up_ref, cp_ref,
             wao_ref, wco_ref, wo_ref, cw_ref, cb_ref, lg_ref, lb_ref,
             dproj_ref, dyc_ref, do_ref, st_ref, dxd_ref,
             gwo_ref, gwco_ref, gwao_ref, vec_ref,
             dga_s, dbg_s, dgm_s, sems, acc_o, acc_co, acc_ao):
        i = pl.program_id(0)
        bidx = i // per_seq
        first = (i % per_seq) == 0

        @pl.when(i == 0)
        def _():
            vec_ref[...] = jnp.zeros_like(vec_ref)

        l1, l2, l3 = l1_ref[...], l2_ref[...], l3_ref[...]
        mx = jnp.maximum(jnp.maximum(l1, l2), l3)
        e1, e2, e3 = jnp.exp(l1 - mx), jnp.exp(l2 - mx), jnp.exp(l3 - mx)
        esum = e1 + e2 + e3
        lse_tot = mx + jnp.log(esum)
        w1, w2, w3 = e1 / esum, e2 / esum, e3 / esum

        def per_head(wv):
            return jnp.concatenate([jnp.broadcast_to(wv[:, h:h + 1], (tm, HD)) for h in range(4)], axis=1)

        o = per_head(w1) * o1_ref[...] + per_head(w2) * o2_ref[...] + per_head(w3) * o3_ref[...]

        ga = ga_ref[...].astype(F32)
        sig_ga = _sigmoid(ga)
        silu_ga = ga * sig_ga
        a_in = (o * silu_ga).astype(BF16)
        a_out = _dot(a_in, wao_ref[...])

        u = u_ref[...].astype(F32)
        cg = cg_ref[...].astype(F32)
        z = cg * u
        zp = cp_ref[...].astype(F32) * up_ref[...].astype(F32)
        zp = jnp.where(first, 0.0, zp)
        zcat = jnp.concatenate([zp, z], axis=0)
        z1 = pltpu.roll(zcat, 1, 0)[halo:]
        z2 = pltpu.roll(zcat, 2, 0)[halo:]
        y_conv = cw_ref[0:1, :] * z2 + cw_ref[1:2, :] * z1 + cw_ref[2:3, :] * z + cb_ref[...]
        gc = gc_ref[...].astype(F32)
        sig_gc = _sigmoid(gc)
        silu_gc = gc * sig_gc
        bg = bg_ref[...].astype(F32)
        bg_yc = bg * y_conv
        s_in = (bg_yc * silu_gc).astype(BF16)
        s_out = _dot(s_in, wco_ref[...])

        sa = _sigmoid(ma_ref[...].astype(F32))
        sc = _sigmoid(mc_ref[...].astype(F32))
        merged = (sa * a_out + sc * s_out).astype(BF16)
        y = _dot(merged, wo_ref[...])
        gate1 = 1.0 + mod_ref[0, 2:3, :]
        xv = x_ref[...]
        resid = ALPHA * xv + gate1 * y
        mu = jnp.mean(resid, axis=1, keepdims=True)
        xc = resid - mu
        var = jnp.mean(xc * xc, axis=1, keepdims=True)
        rstd = lax.rsqrt(var + LN_EPS)
        xhat = xc * rstd
        lg = lg_ref[...]
        err = xhat * lg + lb_ref[...] - t_ref[...]
        vec_ref[3:4, :] += (0.5 / D) * jnp.sum(err * err, axis=0, keepdims=True)

        vec_ref[1:2, :] += (1.0 / D) * jnp.sum(err * xhat, axis=0, keepdims=True)
        vec_ref[2:3, :] += (1.0 / D) * jnp.sum(err, axis=0, keepdims=True)
        dxh = err * (lg * (1.0 / D))
        dres = rstd * (dxh - jnp.mean(dxh, axis=1, keepdims=True)
                       - xhat * jnp.mean(dxh * xhat, axis=1, keepdims=True))
        dxd_ref[...] = ALPHA * dres
        dgate = jnp.sum(dres * y, axis=0, keepdims=True)
        vec_ref[4:5, :] += jnp.where(bidx == 0, dgate, 0.0)
        vec_ref[5:6, :] += jnp.where(bidx == 1, dgate, 0.0)
        dy = (dres * gate1).astype(BF16)

        dmerged = _dot_nt(dy, wo_ref[...])
        da_out_f = dmerged * sa
        ds_out_f = dmerged * sc
        da_out = da_out_f.astype(BF16)
        ds_out = ds_out_f.astype(BF16)
        dgm_s[:, 2 * D:3 * D] = (ds_out_f * s_out * (1.0 - sc)).astype(BF16)
        dgm_s[:, D:2 * D] = (da_out_f * a_out * (1.0 - sa)).astype(BF16)
        da_in = _dot_nt(da_out, wao_ref[...])
        ds_in = _dot_nt(ds_out, wco_ref[...])

        d_o = da_in * silu_ga
        do_ref[...] = d_o.astype(BF16)
        dga_s[...] = (da_in * o * (sig_ga + silu_ga * (1.0 - sig_ga))).astype(BF16)
        lane = lax.broadcasted_iota(jnp.int32, (tm, 128), 1)
        stats = lse_tot
        od = o * d_o
        for h in range(4):
            delta = jnp.sum(od[:, h * HD:(h + 1) * HD], axis=1, keepdims=True)
            stats = jnp.where(lane == 4 + h, delta, stats)
        st_ref[...] = stats

        ds_silu = ds_in * silu_gc
        dbg_s[...] = (ds_silu * y_conv).astype(BF16)
        dyc = ds_silu * bg
        dyc_ref[...] = dyc
        vec_ref[0:1, :] += jnp.sum(dyc, axis=0, keepdims=True)
        dgm_s[:, 0:D] = (ds_in * bg_yc * (sig_gc + silu_gc * (1.0 - sig_gc))).astype(BF16)

        @pl.when(i == 0)
        def _():
            acc_o[...] = jnp.zeros_like(acc_o)
            acc_co[...] = jnp.zeros_like(acc_co)
            acc_ao[...] = jnp.zeros_like(acc_ao)

        acc_o[...] += _dot_tn(merged, dy)
        acc_co[...] += _dot_tn(s_in, ds_out)
        acc_ao[...] += _dot_tn(a_in, da_out)

        @pl.when(i == T // tm - 1)
        def _():
            gwo_ref[...] = acc_o[...].astype(BF16)
            gwco_ref[...] = acc_co[...].astype(BF16)
            gwao_ref[...] = acc_ao[...].astype(BF16)

        _write_columns([(dga_s, CB * CB_GA), (dbg_s, D * KB_BG), (dgm_s, D * KB_GC)],
                       dproj_ref, pl.multiple_of(i * tm, tm), sems)

    def tile(width, cblk=0):
        return pl.BlockSpec((tm, width), lambda i: (i, cblk))

    def whole(shape):
        return pl.BlockSpec(shape, lambda i: tuple(0 for _ in shape))

    def once(shape):
        return pl.BlockSpec(shape, lambda i: tuple(0 for _ in shape), pipeline_mode=pl.Buffered(1))

    prev_rows = lambda i: (jnp.maximum(i * (tm // halo) - 1, 0),)
    in_specs = [
        tile(D), tile(D), pl.BlockSpec((1, 3, D), lambda i: (i // per_seq, 0, 0)),
        tile(GW), tile(GW), tile(GW), tile(128), tile(128), tile(128),
        tile(GW, CB_GA), tile(D, KB_U), tile(D, KB_BG), tile(D, KB_CG), tile(D, KB_GC),
        tile(D, KB_MA), tile(D, KB_MC),
        pl.BlockSpec((halo, D), lambda i: (*prev_rows(i), KB_U)),
        pl.BlockSpec((halo, D), lambda i: (*prev_rows(i), KB_CG)),
        whole((GW, D)), whole((D, D)), whole((D, D)),
        whole((3, D)), whole((1, D)), whole((1, D)), whole((1, D)),
    ]
    out_specs = (
        pl.BlockSpec(memory_space=pl.ANY), tile(D), tile(GW), tile(128), tile(D),
        once((D, D)), once((D, D)), once((GW, D)),
        pl.BlockSpec((8, D), lambda i: (0, 0)),
    )
    out_shape = (
        jax.ShapeDtypeStruct((T, NCOL), BF16),
        jax.ShapeDtypeStruct((T, D), F32),
        jax.ShapeDtypeStruct((T, GW), BF16),
        jax.ShapeDtypeStruct((T, 128), F32),
        jax.ShapeDtypeStruct((T, D), F32),
        jax.ShapeDtypeStruct((D, D), BF16),
        jax.ShapeDtypeStruct((D, D), BF16),
        jax.ShapeDtypeStruct((GW, D), BF16),
        jax.ShapeDtypeStruct((8, D), F32),
    )
    return pl.pallas_call(
        body, name="tail",
        grid=(T // tm,),
        in_specs=in_specs, out_specs=out_specs, out_shape=out_shape,
        scratch_shapes=[pltpu.VMEM((tm, GW), BF16), pltpu.VMEM((tm, D), BF16), pltpu.VMEM((tm, 3 * D), BF16),
                        pltpu.SemaphoreType.DMA((3,)),
                        pltpu.VMEM((D, D), F32), pltpu.VMEM((D, D), F32), pltpu.VMEM((GW, D), F32)],
        compiler_params=pltpu.CompilerParams(vmem_limit_bytes=VMEM_LIMIT_TAIL),
    )(x2, tgt2, mod3, *o_g, *lse_g, proj, proj, proj, proj, proj, proj, proj, proj, proj,
      w_ao, w_co, w_o, conv_w, conv_b, ln_g, ln_b)


def _conv_bwd(dyc, proj, conv_w, dproj):
    tm = 512
    per_seq = S // tm

    def body(d_ref, dn_ref, u_ref, c_ref, cw_ref, _, dproj_ref, g_ref, du_s, dc_s, sems):
        i = pl.program_id(0)
        last = (i % per_seq) == per_seq - 1

        @pl.when(i == 0)
        def _():
            g_ref[...] = jnp.zeros_like(g_ref)

        d = d_ref[...]
        dn = jnp.where(last, 0.0, dn_ref[...])
        dcat = jnp.concatenate([d, dn], axis=0)
        d1 = pltpu.roll(dcat, tm + 8 - 1, 0)[:tm]
        d2 = pltpu.roll(dcat, tm + 8 - 2, 0)[:tm]
        dz = cw_ref[2:3, :] * d + cw_ref[1:2, :] * d1 + cw_ref[0:1, :] * d2
        u = u_ref[...].astype(F32)
        cg = c_ref[...].astype(F32)
        du_s[...] = (dz * cg).astype(BF16)
        dc_s[...] = (dz * u).astype(BF16)
        _write_columns([(du_s, D * KB_U), (dc_s, D * KB_CG)], dproj_ref, pl.multiple_of(i * tm, tm), sems)

        z = cg * u
        g_ref[0:1, :] += jnp.sum(d2 * z, axis=0, keepdims=True)
        g_ref[1:2, :] += jnp.sum(d1 * z, axis=0, keepdims=True)
        g_ref[2:3, :] += jnp.sum(d * z, axis=0, keepdims=True)

    n_tiles = T // tm
    next_rows = lambda i: jnp.minimum((i + 1) * (tm // 8), T // 8 - 1)
    return pl.pallas_call(
        body, name="conv_bwd",
        grid=(n_tiles,),
        in_specs=[pl.BlockSpec((tm, D), lambda i: (i, 0)),
                  pl.BlockSpec((8, D), lambda i: (next_rows(i), 0)),
                  pl.BlockSpec((tm, D), lambda i: (i, KB_U)),
                  pl.BlockSpec((tm, D), lambda i: (i, KB_CG)),
                  pl.BlockSpec((3, D), lambda i: (0, 0)),
                  pl.BlockSpec(memory_space=pl.ANY)],
        out_specs=(pl.BlockSpec(memory_space=pl.ANY),
                   pl.BlockSpec((8, D), lambda i: (0, 0))),
        out_shape=(jax.ShapeDtypeStruct((T, NCOL), BF16),
                   jax.ShapeDtypeStruct((8, D), F32)),
        scratch_shapes=[pltpu.VMEM((tm, D), BF16), pltpu.VMEM((tm, D), BF16), pltpu.SemaphoreType.DMA((2,))],
        input_output_aliases={5: 0},
        compiler_params=pltpu.CompilerParams(vmem_limit_bytes=VMEM_LIMIT),
    )(dyc, dyc, proj, proj, conv_w, dproj)


def _dh_dx(dproj, w_in_all, x2, dxd, mod3, chip_sums, hops=(), parts0=None):
    tm = 1024
    per_seq = S // tm
    n = len(chip_sums)
    n_in = 5 + n + (0 if parts0 is None else 1)

    def body(*refs):
        d_ref, w_ref, x_ref, dxd_ref, mod_ref = refs[:5]
        ins = refs[5:5 + n]
        gx_ref, vec_ref = refs[n_in:n_in + 2]
        outs = refs[n_in + 2:n_in + 2 + n]
        acc, send_sems, recv_sems, local_sems = refs[n_in + 2 + n:]
        i, jj = pl.program_id(0), pl.program_id(1)

        @pl.when((i == 0) & (jj == 0))
        def _():
            vec_ref[...] = jnp.zeros_like(vec_ref)
            if n:
                sends, _, mine = _chip_copies(ins, outs, send_sems, recv_sems, local_sems, hops)
                for cp in sends + mine:
                    cp.start()

        if n:
            @pl.when((i == T // tm - 1) & (jj == N_DEV - 1))
            def _():
                sends, arrivals, mine = _chip_copies(ins, outs, send_sems, recv_sems, local_sems, hops)
                for cp in arrivals:
                    cp.wait_recv()
                for cp in sends:
                    cp.wait_send()
                for cp in mine:
                    cp.wait()

        @pl.when(jj == 0)
        def _():
            acc[...] = jnp.zeros_like(acc)

        acc[...] += _dot_nt(d_ref[...], w_ref[...])

        @pl.when(jj == N_DEV - 1)
        def _():
            dh = acc[...]
            bidx = i // per_seq
            gx_ref[...] = dxd_ref[...] + dh * (1.0 + mod_ref[0, 1:2, :])
            dshift = jnp.sum(dh, axis=0, keepdims=True)
            dscale = jnp.sum(dh * x_ref[...], axis=0, keepdims=True)
            vec_ref[0:1, :] += jnp.where(bidx == 0, dshift, 0.0)
            vec_ref[1:2, :] += jnp.where(bidx == 1, dshift, 0.0)
            vec_ref[2:3, :] += jnp.where(bidx == 0, dscale, 0.0)
            vec_ref[3:4, :] += jnp.where(bidx == 1, dscale, 0.0)

    any_spec = pl.BlockSpec(memory_space=pl.ANY)
    res = pl.pallas_call(
        body, name="dh_dx",
        grid=(T // tm, N_DEV),
        in_specs=[
            pl.BlockSpec((tm, SHARD), lambda i, jj: (i, jj)),
            pl.BlockSpec((None, D, SHARD), lambda i, jj: (jj, 0, 0)),
            pl.BlockSpec((tm, D), lambda i, jj: (i, 0)),
            pl.BlockSpec((tm, D), lambda i, jj: (i, 0)),
            pl.BlockSpec((1, 3, D), lambda i, jj: (i // per_seq, 0, 0))] + [any_spec] * (n_in - 5),
        out_specs=(pl.BlockSpec((tm, D), lambda i, jj: (i, 0)),
                   pl.BlockSpec((8, D), lambda i, jj: (0, 0))) + (any_spec,) * n,
        out_shape=(jax.ShapeDtypeStruct((T, D), F32), jax.ShapeDtypeStruct((8, D), F32))
                  + tuple(jax.ShapeDtypeStruct(a.shape, a.dtype) for a in chip_sums),
        scratch_shapes=[pltpu.VMEM((tm, D), F32), pltpu.SemaphoreType.DMA((max(3 * n, 1),)),
                        pltpu.SemaphoreType.DMA((max(3 * n, 1),)), pltpu.SemaphoreType.DMA((max(n, 1),))],
        input_output_aliases={} if parts0 is None else {5 + n: 2},
        compiler_params=pltpu.CompilerParams(vmem_limit_bytes=VMEM_LIMIT),
    )(dproj, w_in_all, x2, dxd, mod3, *chip_sums, *([] if parts0 is None else [parts0]))
    return res[0], res[1], res[2:]


def _adam_step(g, w, m, v):
    nm = ADAM_B1 * m + (1.0 - ADAM_B1) * g
    nv = ADAM_B2 * v + (1.0 - ADAM_B2) * (g * g)
    m_hat = nm / (1.0 - ADAM_B1 ** ADAM_STEP)
    v_hat = nv / (1.0 - ADAM_B2 ** ADAM_STEP)
    return -ADAM_LR * (m_hat / (jnp.sqrt(v_hat) + ADAM_EPS) + ADAM_WD * w), nm, nv


def _adamw(parts, w, m, v, name, row_tile=None):
    n_parts, rows, cols = parts.shape
    tr = rows if row_tile is None else row_tile

    def body(p_ref, w_ref, m_ref, v_ref, g_ref, d_ref, nm_ref, nv_ref):
        g = p_ref[0].astype(F32)
        for s in range(1, n_parts):
            g = g + p_ref[s].astype(F32)
        g_ref[...] = g
        d_ref[...], nm_ref[...], nv_ref[...] = _adam_step(g, w_ref[...], m_ref[...], v_ref[...])

    blk = pl.BlockSpec((tr, cols), lambda i: (i, 0))
    shp = jax.ShapeDtypeStruct((rows, cols), F32)
    return pl.pallas_call(
        body, name=name,
        grid=(rows // tr,),
        in_specs=[pl.BlockSpec((n_parts, tr, cols), lambda i: (0, i, 0)), blk, blk, blk],
        out_specs=(blk, blk, blk, blk),
        out_shape=(shp, shp, shp, shp),
        compiler_params=pltpu.CompilerParams(vmem_limit_bytes=VMEM_LIMIT),
    )(parts, w, m, v)


def _multi_adamw(parts_list, params, name):
    n = len(params)
    flat = [t for wmv in params for t in wmv]

    def body(*refs):
        parts, ins, outs = refs[:n], refs[n:4 * n], refs[4 * n:]
        for p in range(n):
            g = parts[p][0].astype(F32)
            for s in range(1, parts[p].shape[0]):
                g = g + parts[p][s].astype(F32)
            w_ref, m_ref, v_ref = ins[3 * p:3 * p + 3]
            g_ref, d_ref, nm_ref, nv_ref = outs[4 * p:4 * p + 4]
            g_ref[...] = g
            d_ref[...], nm_ref[...], nv_ref[...] = _adam_step(g, w_ref[...], m_ref[...], v_ref[...])

    out_shape = []
    for w, _, _ in params:
        out_shape += [jax.ShapeDtypeStruct(w.shape, F32)] * 4
    res = pl.pallas_call(body, name=name, out_shape=tuple(out_shape))(*parts_list, *flat)
    return [res[4 * p:4 * p + 4] for p in range(n)]


def _small_updates(small_g, dmod_all, rel_parts, params):
    flat = [t for wmv in params for t in wmv]

    def body(sg_ref, dm_ref, rp_ref, *refs):
        ins, outs = refs[:len(flat)], refs[len(flat):]

        def over_devices(row):
            tot = sg_ref[0, row:row + 1, :]
            for s in range(1, N_DEV):
                tot = tot + sg_ref[s, row:row + 1, :]
            return tot

        g_b_ada = dm_ref[0:1, :]
        for r in range(1, N_DEV * BL):
            g_b_ada = g_b_ada + dm_ref[r:r + 1, :]
        g_rel = rp_ref[0]
        for s in range(1, N_DEV):
            g_rel = g_rel + rp_ref[s]
        grads = [g_b_ada, over_devices(0), g_rel, over_devices(1), over_devices(2)]
        outs[0][...] = jnp.sum(over_devices(3), axis=1, keepdims=True)
        for p, g in enumerate(grads):
            w_ref, m_ref, v_ref = ins[3 * p:3 * p + 3]
            g_ref, d_ref, nm_ref, nv_ref = outs[1 + 4 * p:5 + 4 * p]
            g_ref[...] = g
            d_ref[...], nm_ref[...], nv_ref[...] = _adam_step(g, w_ref[...], m_ref[...], v_ref[...])

    out_shape = [jax.ShapeDtypeStruct((1, 1), F32)]
    for w, _, _ in params:
        out_shape += [jax.ShapeDtypeStruct(w.shape, F32)] * 4
    res = pl.pallas_call(body, name="small_updates", out_shape=tuple(out_shape))(small_g, dmod_all, rel_parts, *flat)
    return res[0], [res[1 + 4 * p:5 + 4 * p] for p in range(len(params))]


def _attn_fwd_dense(proj, bias):
    nq = 4
    rows = nq * QB
    nsb = S // rows

    def body(q_ref, k_ref, v_ref, kp_ref, vp_ref, b_ref, o_ref, l_ref, ls0, ls1, ls2, ls3):
        ls = [ls0, ls1, ls2, ls3]
        n = pl.program_id(1)
        lane = lax.broadcasted_iota(jnp.int32, (QB, 128), 1)
        units = [(h, j) for h in range(4) for j in range(nq)]

        def keys(cur_ref, prev_ref, h, j):
            sl = slice(h * HD, (h + 1) * HD)
            if j == 0:
                return jnp.concatenate([prev_ref[:, sl], cur_ref[0:QB, sl]], axis=0)
            return cur_ref[(j - 1) * QB:(j + 1) * QB, sl]

        q = jnp.stack([q_ref[j * QB:(j + 1) * QB, h * HD:(h + 1) * HD] for h, j in units])
        k = jnp.stack([keys(k_ref, kp_ref, h, j) for h, j in units])
        v = jnp.stack([keys(v_ref, vp_ref, h, j) for h, j in units])
        bias_b = jnp.stack([b_ref[jnp.minimum(n, 1), h] if j == 0 else b_ref[1, h] for h, j in units])
        s = jnp.einsum("uqd,ukd->uqk", q, k, preferred_element_type=F32) * SCALE + bias_b
        m = jnp.max(s, axis=-1, keepdims=True)
        p = jnp.exp(s - m)
        l = jnp.sum(p, axis=-1, keepdims=True)
        o = jnp.einsum("uqk,ukd->uqd", p.astype(BF16), v, preferred_element_type=F32) / l
        lse = m + jnp.log(l)
        for i, (h, j) in enumerate(units):
            o_ref[j * QB:(j + 1) * QB, h * HD:(h + 1) * HD] = o[i]
            ls[h][j * QB:(j + 1) * QB, :] = jnp.where(lane == h, lse[i], 0.0)
        l_ref[...] = (ls[0][...] + ls[1][...]) + (ls[2][...] + ls[3][...])

    def row(b, n):
        return b * nsb + n

    def prev(b, n):
        return jnp.maximum((b * nsb + n) * nq - 1, 0)

    in_specs = [
        pl.BlockSpec((rows, GW), lambda b, n: (row(b, n), CB_Q)),
        pl.BlockSpec((rows, GW), lambda b, n: (row(b, n), CB_K)),
        pl.BlockSpec((rows, GW), lambda b, n: (row(b, n), CB_V)),
        pl.BlockSpec((QB, GW), lambda b, n: (prev(b, n), CB_K)),
        pl.BlockSpec((QB, GW), lambda b, n: (prev(b, n), CB_V)),
        pl.BlockSpec((None, 2, 4, QB, 2 * QB), lambda b, n: (0, 0, 0, 0, 0)),
    ]
    return pl.pallas_call(
        body, name="attn_fwd0",
        grid=(BL, nsb),
        in_specs=in_specs,
        out_specs=(pl.BlockSpec((rows, GW), lambda b, n: (row(b, n), 0)),
                   pl.BlockSpec((rows, 128), lambda b, n: (row(b, n), 0))),
        out_shape=(jax.ShapeDtypeStruct((T, GW), F32), jax.ShapeDtypeStruct((T, 128), F32)),
        scratch_shapes=[pltpu.VMEM((rows, 128), F32)] * 4,
        compiler_params=pltpu.CompilerParams(vmem_limit_bytes=VMEM_LIMIT),
    )(proj, proj, proj, proj, proj, bias)


def _attn_bwd_dense(proj, d_out, stats, bias, dproj):
    nq = 4
    rows = nq * QB
    nsb = S // rows
    cols_q, cols_k, cols_v = CB * CB_Q, CB * CB_K, CB * CB_V

    def body(q_ref, k_ref, v_ref, do_ref, st_ref, kp_ref, vp_ref, b_ref, _, out_ref, db_ref,
             sq, sk, sv, carry, sems):
        b, n = pl.program_id(0), pl.program_id(1)
        units = [(h, j) for h in range(4) for j in range(nq)]

        @pl.when((b == 0) & (n == 0))
        def _():
            db_ref[...] = jnp.zeros_like(db_ref)

        @pl.when(n == 0)
        def _():
            carry[...] = jnp.zeros_like(carry)

        def write(first_block, position, count):
            row0 = pl.multiple_of(first_block * QB, QB)
            part = pl.ds(position * QB, count * QB)
            _write_columns([(sq.at[part], cols_q), (sk.at[part], cols_k), (sv.at[part], cols_v)],
                           out_ref, row0, sems)

        @pl.when(n == nsb)
        def _():
            sq[0:QB, :] = carry[:, 0:GW].astype(BF16)
            sk[0:QB, :] = carry[:, GW:2 * GW].astype(BF16)
            sv[0:QB, :] = carry[:, 2 * GW:3 * GW].astype(BF16)
            write((b + 1) * nsb * nq - 1, 0, 1)

        @pl.when(n < nsb)
        def _():
            def keys(cur_ref, prev_ref, h, j):
                sl = slice(h * HD, (h + 1) * HD)
                if j == 0:
                    return jnp.concatenate([prev_ref[:, sl], cur_ref[0:QB, sl]], axis=0)
                return cur_ref[(j - 1) * QB:(j + 1) * QB, sl]

            def block(ref, h, j):
                return ref[j * QB:(j + 1) * QB, h * HD:(h + 1) * HD]

            q = jnp.stack([block(q_ref, h, j) for h, j in units])
            do = jnp.stack([block(do_ref, h, j) for h, j in units])
            k = jnp.stack([keys(k_ref, kp_ref, h, j) for h, j in units])
            v = jnp.stack([keys(v_ref, vp_ref, h, j) for h, j in units])
            bias_b = jnp.stack([b_ref[jnp.minimum(n, 1), h] if j == 0 else b_ref[1, h] for h, j in units])
            lse = jnp.stack([st_ref[j * QB:(j + 1) * QB, h:h + 1] for h, j in units])
            delta = jnp.stack([st_ref[j * QB:(j + 1) * QB, 4 + h:5 + h] for h, j in units])
            s = jnp.einsum("uqd,ukd->uqk", q, k, preferred_element_type=F32) * SCALE + bias_b
            p = jnp.exp(s - lse)
            ds = p * (jnp.einsum("uqd,ukd->uqk", do, v, preferred_element_type=F32) - delta)
            for h in range(4):
                tot = ds[h * nq]
                for j in range(1, nq):
                    tot = tot + ds[h * nq + j]
                db_ref[h] += tot
            dsb, pb = ds.astype(BF16), p.astype(BF16)
            dq = jnp.einsum("uqk,ukd->uqd", dsb, k, preferred_element_type=F32) * SCALE
            dk = jnp.einsum("uqk,uqd->ukd", dsb, q, preferred_element_type=F32) * SCALE
            dv = jnp.einsum("uqk,uqd->ukd", pb, do, preferred_element_type=F32)
            for h in range(4):
                sl = slice(h * HD, (h + 1) * HD)
                u0, last = h * nq, h * nq + nq - 1
                sq[0:QB, sl] = carry[:, sl].astype(BF16)
                sk[0:QB, sl] = (carry[:, GW + h * HD:GW + (h + 1) * HD] + dk[u0, :QB]).astype(BF16)
                sv[0:QB, sl] = (carry[:, 2 * GW + h * HD:2 * GW + (h + 1) * HD] + dv[u0, :QB]).astype(BF16)
                for j in range(nq - 1):
                    pos = slice((j + 1) * QB, (j + 2) * QB)
                    sq[pos, sl] = dq[u0 + j].astype(BF16)
                    sk[pos, sl] = (dk[u0 + j, QB:] + dk[u0 + j + 1, :QB]).astype(BF16)
                    sv[pos, sl] = (dv[u0 + j, QB:] + dv[u0 + j + 1, :QB]).astype(BF16)
                carry[:, sl] = dq[last]
                carry[:, GW + h * HD:GW + (h + 1) * HD] = dk[last, QB:]
                carry[:, 2 * GW + h * HD:2 * GW + (h + 1) * HD] = dv[last, QB:]

            @pl.when(n == 0)
            def _():
                write(b * nsb * nq, 1, nq - 1)

            @pl.when(n > 0)
            def _():
                write((b * nsb + n) * nq - 1, 0, nq)

    def row(b, n):
        return b * nsb + jnp.minimum(n, nsb - 1)

    def prev(b, n):
        return jnp.maximum(row(b, n) * nq - 1, 0)

    in_specs = [
        pl.BlockSpec((rows, GW), lambda b, n: (row(b, n), CB_Q)),
        pl.BlockSpec((rows, GW), lambda b, n: (row(b, n), CB_K)),
        pl.BlockSpec((rows, GW), lambda b, n: (row(b, n), CB_V)),
        pl.BlockSpec((rows, GW), lambda b, n: (row(b, n), 0)),
        pl.BlockSpec((rows, 128), lambda b, n: (row(b, n), 0)),
        pl.BlockSpec((QB, GW), lambda b, n: (prev(b, n), CB_K)),
        pl.BlockSpec((QB, GW), lambda b, n: (prev(b, n), CB_V)),
        pl.BlockSpec((None, 2, 4, QB, 2 * QB), lambda b, n: (0, 0, 0, 0, 0)),
        pl.BlockSpec(memory_space=pl.ANY),
    ]
    return pl.pallas_call(
        body, name="attn_bwd0",
        grid=(BL, nsb + 1),
        in_specs=in_specs,
        out_specs=(pl.BlockSpec(memory_space=pl.ANY),
                   pl.BlockSpec((4, QB, 2 * QB), lambda b, n: (0, 0, 0))),
        out_shape=(jax.ShapeDtypeStruct((T, NCOL), BF16),
                   jax.ShapeDtypeStruct((4, QB, 2 * QB), F32)),
        scratch_shapes=[pltpu.VMEM((rows, GW), BF16)] * 3
                       + [pltpu.VMEM((QB, 3 * GW), F32), pltpu.SemaphoreType.DMA((3,))],
        input_output_aliases={8: 0},
        compiler_params=pltpu.CompilerParams(vmem_limit_bytes=VMEM_LIMIT),
    )(proj, proj, proj, d_out, stats, proj, proj, bias, dproj)


def _attention_forward(proj, rel_bias):
    expand_lanes, grad_lanes, masks = (jnp.asarray(t) for t in _bucket_maps())
    bias = _bias_expand(rel_bias, expand_lanes, masks)
    fwd = [_attn_fwd_dense(proj, bias)] + [_attn_fwd(proj, bias, g) for g in (1, 2)]
    return bias, grad_lanes, [f[0] for f in fwd], [f[1] for f in fwd]


def _local_step(x2, tgt2, mod3, h, proj, attn, w_ao, w_co, w_o, conv_w, conv_b, ln_g, ln_b):
    bias, buckets, o_g, lse_g = attn

    (dproj, dyc, d_o, stats, dxd, gw_o, gw_co, gw_ao, tail_vec) = _tail(
        x2, tgt2, mod3, o_g, lse_g, proj, w_ao, w_co, w_o, conv_w, conv_b, ln_g, ln_b)

    dproj, db = _attn_bwd_dense(proj, d_o, stats, bias, dproj)
    dbias = [db]
    for g in (1, 2):
        dproj, db = _attn_bwd(proj, d_o, stats, bias, dproj, g)
        dbias.append(db)
    g_rel_bias = _bias_grad(*dbias, buckets)
    dproj, conv_vec = _conv_bwd(dyc, proj, conv_w, dproj)

    gw_ao = jnp.transpose(gw_ao.reshape(GW, N_DEV, D // N_DEV), (1, 0, 2))
    return dproj, dxd, gw_ao, gw_co, gw_o, conv_vec, g_rel_bias, tail_vec


def kernel(x, c, w_ada, b_ada, w_in, conv_w, conv_b, rel_bias, w_attn_out, w_conv_out, w_o, ln_g, ln_b, loss_target, m_w_ada, m_b_ada, m_w_in, m_conv_w, m_conv_b, m_rel_bias, m_w_attn_out, m_w_conv_out, m_w_o, m_ln_g, m_ln_b, v_w_ada, v_b_ada, v_w_in, v_conv_w, v_conv_b, v_rel_bias, v_w_attn_out, v_w_conv_out, v_w_o, v_ln_g, v_ln_b):
    me = _my_index()
    x2 = x.reshape(T, D)
    tgt2 = loss_target.reshape(T, D)

    b_cols = lax.dynamic_slice(b_ada, (0, me * ADA_SHARD), (1, ADA_SHARD))
    c_g, mod_in = _mod_exchange(jnp.pad(c, ((0, 8 - BL), (0, 0))), w_ada[0], b_cols)
    c_all = c_g[:, 0:BL, :].reshape(N_DEV * BL, D)
    mod3 = jnp.transpose(mod_in[:, 0:BL, :], (1, 0, 2)).reshape(BL, 3, D)

    h = _prep_h(x2, mod3)
    rows_shape = jax.ShapeDtypeStruct((N_DEV, D // N_DEV, D), BF16)
    proj, w_in_all, (w_ao_g, w_co_g, w_o_g, conv_w_g) = _gather_proj(
        _shard_order(), h, w_in[0].astype(BF16), 1024,
        ([w_attn_out[0].astype(BF16), w_conv_out[0].astype(BF16), w_o[0].astype(BF16), conv_w[0]],
         [jax.ShapeDtypeStruct((N_DEV, GW, D // N_DEV), BF16), rows_shape, rows_shape,
          jax.ShapeDtypeStruct((N_DEV, 3, D // N_DEV), F32)]))

    attn = _attention_forward(proj, rel_bias)
    w_ao_full = jnp.transpose(w_ao_g, (1, 0, 2)).reshape(GW, D)
    w_co_full = w_co_g.reshape(D, D)
    w_o_full = w_o_g.reshape(D, D)
    conv_w_full = jnp.transpose(conv_w_g, (1, 0, 2)).reshape(3, D)

    (dproj, dxd, gw_ao, gw_co, gw_o, conv_vec, g_rel_bias, tail_vec) = _local_step(
        x2, tgt2, mod3, h, proj, attn, w_ao_full, w_co_full, w_o_full,
        conv_w_full, conv_b, ln_g, ln_b)

    g_conv_w_blocks = jnp.transpose(conv_vec[0:3].reshape(3, N_DEV, D // N_DEV), (1, 0, 2))
    partials = [gw_ao, gw_co.reshape(N_DEV, D // N_DEV, D), gw_o.reshape(N_DEV, D // N_DEV, D), g_conv_w_blocks]
    w_in_sums, w_in_parts, sib = _gw_in_pair(
        _slice_order(), h, dproj, partials,
        [jax.ShapeDtypeStruct((4, GW, D // N_DEV), BF16),
         jax.ShapeDtypeStruct((4, D // N_DEV, D), BF16),
         jax.ShapeDtypeStruct((4, D // N_DEV, D), BF16),
         jax.ShapeDtypeStruct((4, 3, D // N_DEV), F32)])
    core = lax.axis_index("c").astype(jnp.int32).reshape(1)
    chip_sums = [w_in_sums] + list(_pair_add(core, partials, sib))
    hops = [(3,)] + [(1, 2, 3)] * 4
    grad_x, mod_vec, (r_in, r_ao, r_co, r_o, r_cw) = _dh_dx(
        dproj, w_in_all, x2, dxd, mod3, chip_sums, hops, w_in_parts)

    small = jnp.concatenate([
        tail_vec[0:4],
        jnp.pad(g_rel_bias.reshape(1, N_BUCKETS * N_HEADS), ((0, 0), (0, D - N_BUCKETS * N_HEADS))),
        jnp.zeros((3, D), F32)], axis=0)
    dmod = jnp.concatenate([mod_vec[0:2], mod_vec[2:4], tail_vec[4:6]], axis=1)
    small_g, dmod_g = _all_gather(
        [small, dmod],
        [jax.ShapeDtypeStruct((N_DEV, 8, D), F32), jax.ShapeDtypeStruct((N_DEV, BL, 3 * D), F32)],
        "gather_small")
    dmod_all = dmod_g.reshape(N_DEV * BL, 3 * D)
    small_names = ["b_ada", "conv_b", "rel_bias", "ln_g", "ln_b"]
    small_params = [(b_ada, m_b_ada, v_b_ada), (conv_b, m_conv_b, v_conv_b), (rel_bias, m_rel_bias, v_rel_bias),
                    (ln_g, m_ln_g, v_ln_g), (ln_b, m_ln_b, v_ln_b)]
    loss, small_res = _small_updates(
        small_g, dmod_all, small_g[:, 4, :N_BUCKETS * N_HEADS].reshape(N_DEV, N_BUCKETS, N_HEADS), small_params)
    loss = loss.reshape(())

    dmod_cols = lax.dynamic_slice(dmod_all, (0, me * ADA_SHARD), (N_DEV * BL, ADA_SHARD))
    res = {
        "w_ada": tuple(t[None] for t in _w_ada_update(jnp.transpose(c_all), dmod_cols,
                                                      w_ada[0], m_w_ada[0], v_w_ada[0])),
        "w_in": tuple(t[None] for t in _adamw(r_in, w_in[0], m_w_in[0], v_w_in[0], "adam_w_in", 128)),
    }
    mid_names = ["conv_w", "w_attn_out", "w_conv_out", "w_o"]
    mid_parts = [r_cw, r_ao, r_co, r_o]
    mid_full = [(conv_w, m_conv_w, v_conv_w), (w_attn_out, m_w_attn_out, v_w_attn_out),
                (w_conv_out, m_w_conv_out, v_w_conv_out), (w_o, m_w_o, v_w_o)]
    mid_res = _multi_adamw(mid_parts, [tuple(t[0] for t in wmv) for wmv in mid_full], "adam_mid")
    for nm, wmv, outs4 in zip(mid_names, mid_full, mid_res):
        res[nm] = tuple(t[None] for t in outs4)
    res.update(dict(zip(small_names, small_res)))
    order = ["w_ada", "b_ada", "w_in", "conv_w", "conv_b", "rel_bias", "w_attn_out", "w_conv_out",
             "w_o", "ln_g", "ln_b"]
    outs = [loss, grad_x.reshape(BL, S, D)]
    for k in range(4):
        outs += [res[name][k] for name in order]
    return tuple(outs)
```

```python
import math

import numpy as np
import jax
import jax.numpy as jnp
from jax import lax
from jax.experimental import pallas as pl
from jax.experimental.pallas import tpu as pltpu

F32 = jnp.float32
BF16 = jnp.bfloat16
MESH = pl.DeviceIdType.MESH

N_DEV = 8
D = 1024
S = 2048
BL = 2
T = BL * S
NCOL = 11264
SHARD = NCOL // N_DEV
CB = 512
NCB = NCOL // CB
HD = 128
GW = 512
QB = 128
DILATIONS = (1, 4, 16)
N_STEPS = 128
N_BUCKETS = 32
N_HEADS = 12
ALPHA = 2.0 ** 0.25
LN_EPS = 1e-5
NEG_INF = -1e30
SCALE = HD ** -0.5
ADA_SHARD = 3 * D // N_DEV

CB_Q, CB_K, CB_V, CB_GA = 0, 3, 6, 9
KB_U, KB_BG, KB_CG, KB_GC, KB_MA, KB_MC = 5, 6, 7, 8, 9, 10

ADAM_LR, ADAM_B1, ADAM_B2, ADAM_EPS, ADAM_WD, ADAM_STEP = 0.001, 0.9, 0.999, 1e-08, 0.01, 10

VMEM_LIMIT = 56 * 1024 * 1024
VMEM_LIMIT_TAIL = 62 * 1024 * 1024


def _dot(a, b):
    return jnp.dot(a, b, preferred_element_type=F32)


def _dot_nt(a, b):
    return lax.dot_general(a, b, (((1,), (1,)), ((), ())), preferred_element_type=F32)


def _dot_tn(a, b):
    return lax.dot_general(a, b, (((0,), (0,)), ((), ())), preferred_element_type=F32)


def _sigmoid(v):
    return 1.0 / (1.0 + jnp.exp(-v))


def _write_columns(pieces, dst_hbm, row0, sems):
    copies = []
    for k, (src, col0) in enumerate(pieces):
        rows, width = src.shape
        copies.append(pltpu.make_async_copy(
            src, dst_hbm.at[pl.ds(row0, rows), pl.ds(col0, width)], sems.at[k]))
    for cp in copies:
        cp.start()
    for cp in copies:
        cp.wait()


def _my_index():
    return 4 * lax.axis_index("x") + 2 * lax.axis_index("y") + lax.axis_index("c")


class _Gather:
    def __init__(self, ins, outs, stage, send_sems, recv_sems, local_sems):
        self.ins, self.outs, self.stage = ins, outs, stage
        self.send_sems, self.recv_sems, self.local_sems = send_sems, recv_sems, local_sems
        x, y, c = lax.axis_index("x"), lax.axis_index("y"), lax.axis_index("c")
        self.c = c
        self.me, self.sibling = (x, y, c), (x, y, 1 - c)
        self.chips = [(1 - x, y), (x, 1 - y), (1 - x, 1 - y)]

    @staticmethod
    def scratch(arrs):
        n = len(arrs)
        return ([pltpu.SemaphoreType.DMA((7 * n,)), pltpu.SemaphoreType.DMA((7 * n,)),
                 pltpu.SemaphoreType.DMA((n,))] + [pltpu.VMEM(a.shape, a.dtype) for a in arrs])

    def _copy(self, a, k, block, to, src=None):
        dst = self.outs[a].at[4 * block[0] + 2 * block[1] + block[2]]
        return pltpu.make_async_remote_copy(
            src_ref=dst if src is None else src, dst_ref=dst,
            send_sem=self.send_sems.at[a * 7 + k], recv_sem=self.recv_sems.at[a * 7 + k],
            device_id=to, device_id_type=MESH)

    def _first(self):
        first = []
        for a in range(len(self.ins)):
            first.append(self._copy(a, 0, self.me, self.sibling, src=self.ins[a]))
            first += [self._copy(a, 1 + j, self.me, (*chip, self.c), src=self.ins[a])
                      for j, chip in enumerate(self.chips)]
        return first

    def _mine(self):
        me = self.me
        return [pltpu.make_async_copy(self.stage[a], self.outs[a].at[4 * me[0] + 2 * me[1] + me[2]],
                                      self.local_sems.at[a]) for a in range(len(self.ins))]

    def begin(self):
        for cp in self._first():
            cp.start()
        loads = [pltpu.make_async_copy(self.ins[a], self.stage[a], self.local_sems.at[a])
                 for a in range(len(self.ins))]
        for cp in loads:
            cp.start()
        for cp in loads:
            cp.wait()
        for cp in self._mine():
            cp.start()

    def finish(self):
        n, c, me, sibling = len(self.ins), self.c, self.me, self.sibling
        passed = []
        for j, chip in enumerate(self.chips):
            for a in range(n):
                self._copy(a, 1 + j, (*chip, c), me).wait_recv()
                fwd = self._copy(a, 4 + j, (*chip, c), sibling)
                fwd.start()
                passed.append(fwd)
        for a in range(n):
            self._copy(a, 0, sibling, me).wait_recv()
        for j, chip in enumerate(self.chips):
            for a in range(n):
                self._copy(a, 4 + j, (*chip, 1 - c), me).wait_recv()
        for cp in self._first() + passed:
            cp.wait_send()
        for cp in self._mine():
            cp.wait()


def _all_gather(arrs, out_shapes, name):
    n = len(arrs)

    def body(*refs):
        g = _Gather(refs[:n], refs[n:2 * n], refs[2 * n + 3:], *refs[2 * n:2 * n + 3])
        g.begin()
        g.finish()

    any_spec = pl.BlockSpec(memory_space=pl.ANY)
    return pl.pallas_call(
        body, name=name,
        out_shape=tuple(out_shapes),
        in_specs=[any_spec] * n,
        out_specs=tuple([any_spec] * n),
        scratch_shapes=_Gather.scratch(arrs),
    )(*arrs)


def _neighbour_chips():
    x, y, c = lax.axis_index("x"), lax.axis_index("y"), lax.axis_index("c")
    first = (jnp.where(c == 0, 1 - x, x), jnp.where(c == 0, y, 1 - y))
    second = (jnp.where(c == 0, x, 1 - x), jnp.where(c == 0, 1 - y, y))
    return first, second, (1 - x, 1 - y)


def _slice_order():
    x, y, c = lax.axis_index("x"), lax.axis_index("y"), lax.axis_index("c")
    nb1, nb2, diag = _neighbour_chips()
    slots = []
    for mine, theirs in ((nb1, nb2), (nb2, nb1), (diag, diag), ((x, y), (x, y))):
        slots += [2 * (2 * theirs[0] + theirs[1]) + 1 - c, 2 * (2 * mine[0] + mine[1]) + c]
    return jnp.stack(slots).astype(jnp.int32)


def _gw_in_pair(order, h, dproj, smalls, small_shapes4):
    kk, m = h.shape
    tk = min(kk, 2048)
    nk = kk // tk
    ncols = dproj.shape[1] // N_DEV
    n = len(smalls)

    def body(order_ref, h_ref, d_ref, *rest):
        ins = rest[:n]
        sums_hbm, parts_hbm = rest[n], rest[n + 1]
        sib = rest[n + 2:2 * n + 2]
        (acc, sendbuf, recvbuf, sumbuf, send_sems, recv_sems, local_sem, ssend, srecv,
         isend, irecv) = rest[2 * n + 2:]
        js, k = pl.program_id(0), pl.program_id(1)
        x, y, c = lax.axis_index("x"), lax.axis_index("y"), lax.axis_index("c")
        sibling = (x, y, 1 - c)
        my_chip = 2 * x + y
        nb1, nb2, _ = _neighbour_chips()
        near = [(*nb1, c), (*nb2, c)]

        def ici_copy(p, out_chip):
            peer = near[p] if isinstance(p, int) else tuple(jnp.where(p == 0, a, b) for a, b in zip(*near))
            return pltpu.make_async_remote_copy(
                src_ref=sumbuf.at[p], dst_ref=parts_hbm.at[out_chip],
                send_sem=isend.at[p], recv_sem=irecv.at[p], device_id=peer, device_id_type=MESH)

        def small_copies():
            return [pltpu.make_async_remote_copy(
                        src_ref=ins[a].at[2 * q + 1 - c], dst_ref=sib[a].at[q],
                        send_sem=ssend.at[a * 4 + q], recv_sem=srecv.at[a * 4 + q],
                        device_id=sibling, device_id_type=MESH)
                    for a in range(n) for q in range(4)]

        def slice_copy(p):
            return pltpu.make_async_remote_copy(
                src_ref=sendbuf, dst_ref=recvbuf.at[p], send_sem=send_sems.at[p], recv_sem=recv_sems.at[p],
                device_id=sibling, device_id_type=MESH)

        def sum_copy(p):
            return pltpu.make_async_copy(sumbuf.at[2], sums_hbm.at[order_ref[2 * p] // 2], local_sem)

        @pl.when((js == 0) & (k == 0))
        def _():
            for cp in small_copies():
                cp.start()

        def partial():
            return _dot_tn(h_ref[...], d_ref[...])

        if nk > 1:
            @pl.when(k == 0)
            def _():
                acc[...] = partial()
        if nk > 2:
            @pl.when((k > 0) & (k < nk - 1))
            def _():
                acc[...] += partial()

        def total():
            return partial() + acc[...] if nk > 1 else partial()

        p = js // 2

        @pl.when((js % 2 == 0) & (k == nk - 1))
        def _():
            @pl.when(p > 0)
            def _():
                slice_copy(p - 1).wait_send()
            sendbuf[...] = total().astype(BF16)
            slice_copy(p).start()

        @pl.when((js % 2 == 1) & (k == nk - 1))
        def _():
            slice_copy(p).wait_recv()

            @pl.when(p == 3)
            def _():
                sum_copy(2).wait()
            sumbuf[jnp.minimum(p, 2)] = (total() + recvbuf[p].astype(F32)).astype(BF16)

            @pl.when(p < 2)
            def _():
                ici_copy(p, my_chip).start()

            @pl.when(p >= 2)
            def _():
                sum_copy(p).start()

        @pl.when((js == N_DEV - 1) & (k == nk - 1))
        def _():
            slice_copy(3).wait_send()
            sum_copy(3).wait()
            for cp in small_copies():
                cp.wait()
            for p in range(2):
                ici_copy(p, 2 * near[p][0] + near[p][1]).wait_recv()
                ici_copy(p, my_chip).wait_send()

    any_spec = pl.BlockSpec(memory_space=pl.ANY)
    res = pl.pallas_call(
        body, name="gw_in_pair",
        grid_spec=pltpu.PrefetchScalarGridSpec(
            num_scalar_prefetch=1,
            grid=(N_DEV, nk),
            in_specs=[pl.BlockSpec((tk, m), lambda js, k, order_ref: (k, 0)),
                      pl.BlockSpec((tk, ncols), lambda js, k, order_ref: (k, order_ref[js]))] + [any_spec] * n,
            out_specs=(any_spec,) * (n + 2),
            scratch_shapes=[pltpu.VMEM((m, ncols), F32), pltpu.VMEM((m, ncols), BF16),
                            pltpu.VMEM((4, m, ncols), BF16), pltpu.VMEM((3, m, ncols), BF16),
                            pltpu.SemaphoreType.DMA((4,)), pltpu.SemaphoreType.DMA((4,)),
                            pltpu.SemaphoreType.DMA,
                            pltpu.SemaphoreType.DMA((4 * n,)), pltpu.SemaphoreType.DMA((4 * n,)),
                            pltpu.SemaphoreType.DMA((2,)), pltpu.SemaphoreType.DMA((2,))]),
        out_shape=(jax.ShapeDtypeStruct((4, m, ncols), BF16),) * 2 + tuple(small_shapes4),
        compiler_params=pltpu.CompilerParams(vmem_limit_bytes=VMEM_LIMIT),
    )(order, h, dproj, *smalls)
    return res[0], res[1], res[2:]


def _chip_copies(ins, outs, send_sems, recv_sems, local_sems, hops):
    n = len(ins)
    x, y, c = lax.axis_index("x"), lax.axis_index("y"), lax.axis_index("c")
    my_chip = 2 * x + y

    def peer_of(k):
        return ((1 - x) if (k >> 1) & 1 else x, (1 - y) if k & 1 else y, c)

    def copy(a, k, out_chip):
        peer = peer_of(k)
        return pltpu.make_async_remote_copy(
            src_ref=ins[a].at[2 * peer[0] + peer[1]], dst_ref=outs[a].at[out_chip],
            send_sem=send_sems.at[a * 3 + k - 1], recv_sem=recv_sems.at[a * 3 + k - 1],
            device_id=peer, device_id_type=MESH)

    sends = [copy(a, k, my_chip) for k in range(1, 4) for a in range(n) if k in hops[a]]
    arrivals = []
    for k in range(1, 4):
        peer = peer_of(k)
        arrivals += [copy(a, k, 2 * peer[0] + peer[1]) for a in range(n) if k in hops[a]]
    mine = [pltpu.make_async_copy(ins[a].at[my_chip], outs[a].at[my_chip], local_sems.at[a])
            for a in range(n)]
    return sends, arrivals, mine


def _pair_add(core, mines, theirs):
    n = len(mines)

    def body(core_ref, *refs):
        mine, sib, outs = refs[:n], refs[n:2 * n], refs[2 * n:]
        for a in range(n):
            for q in range(4):
                outs[a][q] = (mine[a][2 * q + core_ref[0]].astype(F32)
                              + sib[a][q].astype(F32)).astype(outs[a].dtype)

    return pl.pallas_call(
        body, name="pair_add",
        in_specs=[pl.BlockSpec(memory_space=pltpu.SMEM)] + [pl.BlockSpec(memory_space=pltpu.VMEM)] * (2 * n),
        out_shape=tuple(jax.ShapeDtypeStruct(t.shape, t.dtype) for t in theirs),
    )(core, *mines, *theirs)


def _mod_exchange(c8, w_ada, b_cols):
    cols = w_ada.shape[1]

    def body(c_ref, w_ref, b_ref, call_ref, mod_ref, msend, send1, recv1, send2, recv2):
        x, y, c = lax.axis_index("x"), lax.axis_index("y"), lax.axis_index("c")
        my_slot = 4 * x + 2 * y + c

        def peer_of(k):
            return ((1 - x) if (k >> 2) & 1 else x, (1 - y) if (k >> 1) & 1 else y, (1 - c) if k & 1 else c)

        def slot_of(dev):
            return 4 * dev[0] + 2 * dev[1] + dev[2]

        def exchange(src_of, dst_ref, send_sems, recv_sems):
            sends, arrivals = [], []
            for k in range(1, 8):
                peer = peer_of(k)
                sends.append(pltpu.make_async_remote_copy(
                    src_ref=src_of(slot_of(peer)), dst_ref=dst_ref.at[my_slot],
                    send_sem=send_sems.at[k - 1], recv_sem=recv_sems.at[k - 1],
                    device_id=peer, device_id_type=MESH))
                arrivals.append(pltpu.make_async_remote_copy(
                    src_ref=src_of(my_slot), dst_ref=dst_ref.at[slot_of(peer)],
                    send_sem=send_sems.at[k - 1], recv_sem=recv_sems.at[k - 1],
                    device_id=peer, device_id_type=MESH))
            for cp in sends:
                cp.start()
            for cp in arrivals:
                cp.wait_recv()
            for cp in sends:
                cp.wait_send()

        call_ref[my_slot] = c_ref[...]
        exchange(lambda s: c_ref, call_ref, send1, recv1)
        cv = call_ref[...].reshape(N_DEV * 8, c_ref.shape[1])
        act = cv * _sigmoid(cv)
        mod = jnp.dot(act, w_ref[...], preferred_element_type=F32,
                      precision=lax.Precision.HIGHEST) + b_ref[...]
        msend[...] = mod.reshape(N_DEV, 8, cols)
        mod_ref[my_slot] = msend[my_slot]
        exchange(lambda s: msend.at[s], mod_ref, send2, recv2)

    return pl.pallas_call(
        body, name="mod_exchange",
        out_shape=(jax.ShapeDtypeStruct((N_DEV, 8, c8.shape[1]), F32),
                   jax.ShapeDtypeStruct((N_DEV, 8, cols), F32)),
        scratch_shapes=[pltpu.VMEM((N_DEV, 8, cols), F32)] + [pltpu.SemaphoreType.DMA((7,))] * 4,
    )(c8, w_ada, b_cols)


def _w_ada_update(c_all_t, dmod_cols, w, m, v):
    rows, cols = w.shape
    tr = 256

    def body(c_ref, d_ref, w_ref, m_ref, v_ref, g_ref, dl_ref, nm_ref, nv_ref):
        cv = c_ref[...]
        g = jnp.dot(cv * _sigmoid(cv), d_ref[...], preferred_element_type=F32,
                    precision=lax.Precision.HIGHEST)
        g_ref[...] = g
        dl_ref[...], nm_ref[...], nv_ref[...] = _adam_step(g, w_ref[...], m_ref[...], v_ref[...])

    blk = pl.BlockSpec((tr, cols), lambda i: (i, 0))
    shp = jax.ShapeDtypeStruct((rows, cols), F32)
    return pl.pallas_call(
        body, name="adam_w_ada",
        grid=(rows // tr,),
        in_specs=[pl.BlockSpec((tr, c_all_t.shape[1]), lambda i: (i, 0)),
                  pl.BlockSpec(dmod_cols.shape, lambda i: (0, 0)), blk, blk, blk],
        out_specs=(blk, blk, blk, blk),
        out_shape=(shp, shp, shp, shp),
    )(c_all_t, dmod_cols, w, m, v)


def _shard_order():
    x, y, c = lax.axis_index("x"), lax.axis_index("y"), lax.axis_index("c")
    first, second, diag = _neighbour_chips()
    devs = [(x, y, c), (x, y, 1 - c), (*first, c), (*second, 1 - c), (*second, c), (*first, 1 - c),
            (*diag, c), (*diag, 1 - c)]
    return jnp.stack([4 * d[0] + 2 * d[1] + d[2] for d in devs]).astype(jnp.int32)


def _prep_h(x2, mod3):
    ts = 512
    per_seq = S // ts

    def body(x_ref, mod_ref, h_ref):
        shift = mod_ref[0, 0:1, :]
        scale = mod_ref[0, 1:2, :]
        h_ref[...] = (x_ref[...] * (1.0 + scale) + shift).astype(BF16)

    return pl.pallas_call(
        body, name="prep_h",
        grid=(T // ts,),
        in_specs=[pl.BlockSpec((ts, D), lambda i: (i, 0)),
                  pl.BlockSpec((1, 3, D), lambda i: (i // per_seq, 0, 0))],
        out_specs=pl.BlockSpec((ts, D), lambda i: (i, 0)),
        out_shape=jax.ShapeDtypeStruct((T, D), BF16),
    )(x2, mod3)


def _gather_proj(order, h, w_shard, tm, ride=()):
    rows, kdim = h.shape
    ncols = w_shard.shape[1]
    n_i = rows // tm
    ride_arrs, ride_shapes = ride if ride else ((), ())
    n_ride = len(ride_arrs)

    def body(order_ref, h_ref, mine_hbm, *rest):
        ride_ins = rest[:n_ride]
        o_ref, all_hbm = rest[n_ride:n_ride + 2]
        ride_outs = rest[n_ride + 2:2 * n_ride + 2]
        wv, send_sems, recv_sems, local_sems = rest[2 * n_ride + 2:2 * n_ride + 6]
        ride_scr = rest[2 * n_ride + 6:]
        j, i = pl.program_id(0), pl.program_id(1)
        c = lax.axis_index("c")
        me, sibling = (lax.axis_index("x"), lax.axis_index("y"), c), (lax.axis_index("x"), lax.axis_index("y"), 1 - c)
        nb1, nb2, diag = _neighbour_chips()

        def slot(dev):
            return 4 * dev[0] + 2 * dev[1] + dev[2]

        def copy(k, block, to, src=None, part=None):
            buf = wv.at[slot(block)]
            if part is not None:
                buf = buf.at[pl.ds(pl.multiple_of(part * (kdim // 2), kdim // 2), kdim // 2)]
            return pltpu.make_async_remote_copy(
                src_ref=buf if src is None else src, dst_ref=buf,
                send_sem=send_sems.at[k], recv_sem=recv_sems.at[k],
                device_id=to, device_id_type=MESH)

        def keep(step, block):
            return pltpu.make_async_copy(wv.at[slot(block)], all_hbm.at[slot(block)], local_sems.at[step])

        if n_ride:
            gather = _Gather(ride_ins, ride_outs, ride_scr[3:], *ride_scr[:3])
        to_sibling, to_nb1, to_nb2 = (copy(0, me, sibling, mine_hbm), copy(1, me, (*nb1, c), mine_hbm),
                                      copy(2, me, (*nb2, c), mine_hbm))
        relay1, relay2 = copy(3, (*nb2, c), (*nb1, c), part=c), copy(4, (*nb1, c), (*nb2, c), part=1 - c)
        pass_nb1, pass_nb2 = copy(5, (*nb1, c), sibling), copy(6, (*nb2, c), sibling)
        pass_d1, pass_d2 = copy(7, (*diag, c), sibling, part=c), copy(8, (*diag, c), sibling, part=1 - c)
        sends = [to_sibling, to_nb1, to_nb2, relay1, relay2, pass_nb1, pass_nb2, pass_d1, pass_d2]
        due = [
            (me, [], []),
            (sibling, [copy(0, sibling, me)], [[]]),
            ((*nb1, c), [copy(1, (*nb1, c), me)], [[pass_nb1, to_nb2]]),
            ((*nb2, 1 - c), [copy(5, (*nb2, 1 - c), me)], [[]]),
            ((*nb2, c), [copy(2, (*nb2, c), me)], [[pass_nb2, relay1, relay2]]),
            ((*nb1, 1 - c), [copy(6, (*nb1, 1 - c), me)], [[]]),
            ((*diag, c), [copy(3, (*diag, c), me, part=c), copy(4, (*diag, c), me, part=1 - c)],
             [[pass_d1], [pass_d2]]),
            ((*diag, 1 - c), [copy(7, (*diag, 1 - c), me, part=1 - c), copy(8, (*diag, 1 - c), me, part=c)],
             [[], []]),
        ]

        @pl.when((j == 0) & (i == 0))
        def _():
            to_sibling.start()
            to_nb1.start()
            load = pltpu.make_async_copy(mine_hbm, wv.at[slot(me)], local_sems.at[N_DEV])
            load.start()
            load.wait()
            keep(0, me).start()

        for step in range(1, N_DEV):
            block, arrivals, then = due[step]

            @pl.when((j == step) & (i == 0))
            def _():
                for arrival, follow in zip(arrivals, then):
                    arrival.wait_recv()
                    for cp in follow:
                        cp.start()
                keep(step, block).start()
                if n_ride and step == N_DEV - 2:
                    gather.begin()

        o_ref[...] = _dot(h_ref[...], wv[order_ref[j]]).astype(BF16)

        @pl.when((j == N_DEV - 1) & (i == n_i - 1))
        def _():
            for cp in sends:
                cp.wait_send()
            for step in range(N_DEV):
                keep(step, due[step][0]).wait()
            if n_ride:
                gather.finish()

    any_spec = pl.BlockSpec(memory_space=pl.ANY)
    res = pl.pallas_call(
        body, name="gather_proj",
        grid_spec=pltpu.PrefetchScalarGridSpec(
            num_scalar_prefetch=1,
            grid=(N_DEV, n_i),
            in_specs=[pl.BlockSpec((tm, kdim), lambda j, i, order_ref: (i, 0)), any_spec] + [any_spec] * n_ride,
            out_specs=(pl.BlockSpec((tm, ncols), lambda j, i, order_ref: (i, order_ref[j])), any_spec)
                      + (any_spec,) * n_ride,
            scratch_shapes=[pltpu.VMEM((N_DEV, kdim, ncols), BF16),
                            pltpu.SemaphoreType.DMA((9,)), pltpu.SemaphoreType.DMA((9,)),
                            pltpu.SemaphoreType.DMA((N_DEV + 1,))]
                           + (_Gather.scratch(ride_arrs) if n_ride else [])),
        out_shape=(jax.ShapeDtypeStruct((rows, N_DEV * ncols), BF16),
                   jax.ShapeDtypeStruct((N_DEV, kdim, ncols), BF16)) + tuple(ride_shapes),
        compiler_params=pltpu.CompilerParams(vmem_limit_bytes=VMEM_LIMIT),
    )(order, h, w_shard, *ride_arrs)
    return res[0], res[1], res[2:]


SKEW_W = 512


def _bucket_maps():
    lanes = np.arange(SKEW_W)

    def buckets_of(steps):
        rows = []
        for dil in DILATIONS:
            dist = np.maximum(steps, 0) * dil
            nf = np.maximum(dist, 1).astype(np.float32)
            large = 16 + (np.log(nf / np.float32(16)) / np.float32(math.log(128.0))
                          * np.float32(16)).astype(np.int32)
            large = np.minimum(large, N_BUCKETS - 1)
            bucket = np.where(dist < 16, dist, large)
            rows.append(np.where((steps >= 0) & (steps <= N_STEPS), bucket, -1).astype(np.int32))
        return np.stack(rows)[:, None, :]

    a = np.arange(QB)[:, None]
    b = np.arange(2 * QB)[None, :]
    steps = a + QB - b
    band = (steps >= 0) & (steps <= N_STEPS)
    first = band & (b >= QB)
    masks = np.stack([first, band]).astype(np.int32)
    return buckets_of(QB - lanes), buckets_of(2 * QB - 1 - lanes), masks


def _bias_expand(rel_bias, lane_buckets, masks):
    def body(tab_ref, bk_ref, mk_ref, o_ref):
        for g in range(3):
            bk = bk_ref[g]
            for h in range(4):
                col = 4 * g + h
                per_offset = jnp.zeros((1, SKEW_W), F32)
                for k in range(N_BUCKETS):
                    per_offset = jnp.where(bk == k, tab_ref[k, col], per_offset)
                tile = pltpu.roll(jnp.broadcast_to(per_offset, (QB, SKEW_W)), 0, 1, stride=1, stride_axis=0)
                tile = tile[:, :2 * QB]
                o_ref[g, 0, h] = jnp.where(mk_ref[0] != 0, tile, NEG_INF)
                o_ref[g, 1, h] = jnp.where(mk_ref[1] != 0, tile, NEG_INF)

    return pl.pallas_call(
        body, name="bias_expand",
        in_specs=[pl.BlockSpec(memory_space=pltpu.SMEM),
                  pl.BlockSpec(memory_space=pltpu.VMEM),
                  pl.BlockSpec(memory_space=pltpu.VMEM)],
        out_shape=jax.ShapeDtypeStruct((3, 2, 4, QB, 2 * QB), F32),
    )(rel_bias, lane_buckets, masks)


def _bias_grad(ds1, ds2, ds3, lane_buckets):
    exchange = jnp.asarray(np.eye(QB, dtype=np.float32)[::-1].copy())

    def body(d1_ref, d2_ref, d3_ref, bk_ref, ex_ref, o_ref):
        for g, d_ref in enumerate((d1_ref, d2_ref, d3_ref)):
            bk = bk_ref[g]
            for h in range(4):
                flipped = jnp.dot(ex_ref[...], d_ref[h], preferred_element_type=F32,
                                  precision=lax.Precision.HIGHEST)
                padded = jnp.concatenate([flipped, jnp.zeros((QB, SKEW_W - 2 * QB), F32)], axis=1)
                skewed = pltpu.roll(padded, 0, 1, stride=1, stride_axis=0)
                per_offset = jnp.sum(skewed, axis=0, keepdims=True)
                for k in range(N_BUCKETS):
                    o_ref[k, 4 * g + h] = jnp.sum(jnp.where(bk == k, per_offset, 0.0))

    return pl.pallas_call(
        body, name="bias_grad",
        in_specs=[pl.BlockSpec(memory_space=pltpu.VMEM)] * 5,
        out_specs=pl.BlockSpec(memory_space=pltpu.SMEM),
        out_shape=jax.ShapeDtypeStruct((N_BUCKETS, N_HEADS), F32),
    )(ds1, ds2, ds3, lane_buckets, exchange)


def _scratch_sets(rows):
    return 4 if rows <= 512 else 1


def _unit_chunks(dil, size=16):
    units = [(h, r) for h in range(4) for r in range(dil)]
    return [units[i:i + size] for i in range(0, len(units), size)]


def _residue_rows(src_ref, copies, h, residue):
    buf = copies[h % len(copies)]
    buf[...] = src_ref[:, h * HD:(h + 1) * HD].astype(F32)
    return lambda r: buf[residue(r), :].astype(BF16)


def _attn_fwd(proj, bias, g):
    dil = DILATIONS[g]
    rows = QB * dil
    nsb = S // rows
    has_prev = nsb > 1

    def residue(r):
        return pl.ds(r, QB, stride=dil)

    n_sets = _scratch_sets(rows)
    n_in = 6 if has_prev else 4
    n_copied = (4 + (2 if has_prev else 0)) * n_sets

    def body(*refs):
        q_ref, kc_ref, vc_ref = refs[:3]
        kp_ref, vp_ref = refs[3:5] if has_prev else (None, None)
        b_ref = refs[n_in - 1]
        o_ref, l_ref = refs[n_in:n_in + 2]
        scr = list(refs[n_in + 2:])
        ls = [scr.pop(0) for _ in range(4)]
        copies = {name: [scr.pop(0) for _ in range(n_sets)]
                  for name in ("q", "kc", "vc", "o") + (("kp", "vp") if has_prev else ())}
        lane = lax.broadcasted_iota(jnp.int32, (QB, 128), 1)
        refs_of = {"q": q_ref, "kc": kc_ref, "vc": vc_ref, "kp": kp_ref, "vp": vp_ref}
        for chunk in _unit_chunks(dil):
            rows_of = {h: {name: _residue_rows(refs_of[name], copies[name], h, residue)
                           for name in refs_of if refs_of[name] is not None}
                       for h in sorted({h for h, _ in chunk})}

            def batch(name):
                return jnp.stack([rows_of[h][name](r) for h, r in chunk])

            q, k, v = batch("q"), batch("kc"), batch("vc")
            if has_prev:
                k = jnp.concatenate([batch("kp"), k], axis=1)
                v = jnp.concatenate([batch("vp"), v], axis=1)
                bias_b = jnp.stack([b_ref[h] for h, _ in chunk])
            else:
                bias_b = jnp.stack([b_ref[h, :, QB:] for h, _ in chunk])
            s = jnp.einsum("uqd,ukd->uqk", q, k, preferred_element_type=F32) * SCALE + bias_b
            m = jnp.max(s, axis=-1, keepdims=True)
            p = jnp.exp(s - m)
            l = jnp.sum(p, axis=-1, keepdims=True)
            o = jnp.einsum("uqk,ukd->uqd", p.astype(BF16), v, preferred_element_type=F32) / l
            lse = m + jnp.log(l)
            for i, (h, r) in enumerate(chunk):
                copies["o"][h % n_sets][residue(r), :] = o[i]
                ls[h][r * QB:(r + 1) * QB, :] = jnp.where(lane == h, lse[i], 0.0)
            for h in sorted({h for h, _ in chunk}):
                o_ref[:, h * HD:(h + 1) * HD] = copies["o"][h % n_sets][...]
        for r in range(dil):
            blk = slice(r * QB, (r + 1) * QB)
            l_ref[residue(r), :] = (ls[0][blk, :] + ls[1][blk, :]) + (ls[2][blk, :] + ls[3][blk, :])

    def row(b, n):
        return b * nsb + n

    def prev(b, n):
        return b * nsb + jnp.maximum(n - 1, 0)

    in_specs = [
        pl.BlockSpec((rows, GW), lambda b, n: (row(b, n), CB_Q + g)),
        pl.BlockSpec((rows, GW), lambda b, n: (row(b, n), CB_K + g)),
        pl.BlockSpec((rows, GW), lambda b, n: (row(b, n), CB_V + g)),
    ]
    args = [proj, proj, proj]
    scratch = [pltpu.VMEM((rows, 128), F32)] * (4 + n_copied)
    if has_prev:
        in_specs += [pl.BlockSpec((rows, GW), lambda b, n: (prev(b, n), CB_K + g)),
                     pl.BlockSpec((rows, GW), lambda b, n: (prev(b, n), CB_V + g))]
        args += [proj, proj]
    in_specs.append(pl.BlockSpec((None, None, 4, QB, 2 * QB),
                                 lambda b, n: (g, jnp.minimum(n, 1), 0, 0, 0)))
    args.append(bias)
    return pl.pallas_call(
        body, name=f"attn_fwd{g}",
        grid=(BL, nsb),
        in_specs=in_specs,
        out_specs=(pl.BlockSpec((rows, GW), lambda b, n: (row(b, n), 0)),
                   pl.BlockSpec((rows, 128), lambda b, n: (row(b, n), 0))),
        out_shape=(jax.ShapeDtypeStruct((T, GW), F32), jax.ShapeDtypeStruct((T, 128), F32)),
        scratch_shapes=scratch,
        compiler_params=pltpu.CompilerParams(vmem_limit_bytes=VMEM_LIMIT),
    )(*args)


def _attn_bwd(proj, d_out, stats, bias, dproj, g):
    dil = DILATIONS[g]
    rows = QB * dil
    nsb = S // rows
    has_prev = nsb > 1
    n_steps = nsb + 1 if has_prev else 1
    n_in = 7 + (2 if has_prev else 0)

    def residue(r):
        return pl.ds(r, QB, stride=dil)

    n_sets = _scratch_sets(rows)

    def body(*refs):
        q_ref, kc_ref, vc_ref, do_ref, st_ref, b_ref = refs[:6]
        kp_ref, vp_ref = refs[6:8] if has_prev else (None, None)
        out_ref, db_ref = refs[n_in], refs[n_in + 1]
        scr = list(refs[n_in + 2:])
        sq, sk, sv, sems = [scr.pop(0) for _ in range(4)]
        carry = scr.pop(0) if has_prev else None
        sts = scr.pop(0)
        copies = {name: [scr.pop(0) for _ in range(n_sets)]
                  for name in ("q", "kc", "vc", "do", "dq", "dk", "dv") + (("kp", "vp") if has_prev else ())}
        b, n = pl.program_id(0), pl.program_id(1)

        @pl.when((b == 0) & (n == 0))
        def _():
            db_ref[...] = jnp.zeros_like(db_ref)

        def finish(h, r, dq, dk, dv):
            for name, val in (("dq", dq), ("dk", dk), ("dv", dv)):
                copies[name][h % n_sets][residue(r), :] = val

        def finish_head(h):
            sl = slice(h * HD, (h + 1) * HD)
            sq[:, sl] = copies["dq"][h % n_sets][...].astype(BF16)
            sk[:, sl] = copies["dk"][h % n_sets][...].astype(BF16)
            sv[:, sl] = copies["dv"][h % n_sets][...].astype(BF16)

        def write_block(blk_idx):
            row0 = pl.multiple_of(blk_idx * rows, rows)
            _write_columns([(sq, CB * (CB_Q + g)), (sk, CB * (CB_K + g)), (sv, CB * (CB_V + g))],
                           out_ref, row0, sems)

        def carried(h, r):
            blk = slice(r * QB, (r + 1) * QB)
            return ((blk, slice(h * HD, (h + 1) * HD)), (blk, slice(GW + h * HD, GW + (h + 1) * HD)),
                    (blk, slice(2 * GW + h * HD, 2 * GW + (h + 1) * HD)))

        if has_prev:
            @pl.when(n == 0)
            def _():
                carry[...] = jnp.zeros_like(carry)

            @pl.when(n == nsb)
            def _():
                for h in range(4):
                    for r in range(dil):
                        cq, ck, cv = carried(h, r)
                        finish(h, r, carry[cq], carry[ck], carry[cv])
                    finish_head(h)
                write_block(b * nsb + nsb - 1)

        @pl.when(n < nsb)
        def _():
            for r in range(dil):
                sts[r * QB:(r + 1) * QB, :] = st_ref[residue(r), :]
            refs_of = {"q": q_ref, "kc": kc_ref, "vc": vc_ref, "do": do_ref, "kp": kp_ref, "vp": vp_ref}
            for chunk in _unit_chunks(dil):
                heads = sorted({h for h, _ in chunk})
                rows_of = {h: {name: _residue_rows(refs_of[name], copies[name], h, residue)
                               for name in refs_of if refs_of[name] is not None}
                           for h in heads}

                def batch(name):
                    return jnp.stack([rows_of[h][name](r) for h, r in chunk])

                q, k, v, do = batch("q"), batch("kc"), batch("vc"), batch("do")
                if has_prev:
                    k = jnp.concatenate([batch("kp"), k], axis=1)
                    v = jnp.concatenate([batch("vp"), v], axis=1)
                    bias_b = jnp.stack([b_ref[h] for h, _ in chunk])
                else:
                    bias_b = jnp.stack([b_ref[h, :, QB:] for h, _ in chunk])
                lse = jnp.stack([sts[r * QB:(r + 1) * QB, h:h + 1] for h, r in chunk])
                delta = jnp.stack([sts[r * QB:(r + 1) * QB, 4 + h:5 + h] for h, r in chunk])
                s = jnp.einsum("uqd,ukd->uqk", q, k, preferred_element_type=F32) * SCALE + bias_b
                p = jnp.exp(s - lse)
                ds = p * (jnp.einsum("uqd,ukd->uqk", do, v, preferred_element_type=F32) - delta)
                for h in heads:
                    mine = [ds[i] for i, (hh, _) in enumerate(chunk) if hh == h]
                    tot = mine[0]
                    for extra in mine[1:]:
                        tot = tot + extra
                    if has_prev:
                        db_ref[h] += tot
                    else:
                        db_ref[h, :, QB:] += tot
                dsb, pb = ds.astype(BF16), p.astype(BF16)
                dq = jnp.einsum("uqk,ukd->uqd", dsb, k, preferred_element_type=F32) * SCALE
                dk = jnp.einsum("uqk,uqd->ukd", dsb, q, preferred_element_type=F32) * SCALE
                dv = jnp.einsum("uqk,uqd->ukd", pb, do, preferred_element_type=F32)
                for i, (h, r) in enumerate(chunk):
                    if has_prev:
                        cq, ck, cv = carried(h, r)
                        finish(h, r, carry[cq], carry[ck] + dk[i, :QB], carry[cv] + dv[i, :QB])
                        carry[cq] = dq[i]
                        carry[ck] = dk[i, QB:]
                        carry[cv] = dv[i, QB:]
                    else:
                        finish(h, r, dq[i], dk[i], dv[i])
                for h in heads:
                    finish_head(h)
            if has_prev:
                @pl.when(n > 0)
                def _():
                    write_block(b * nsb + n - 1)
            else:
                write_block(b)

    def row(b, n):
        return b * nsb + jnp.minimum(n, nsb - 1)

    def prev(b, n):
        return b * nsb + jnp.maximum(jnp.minimum(n, nsb - 1) - 1, 0)

    in_specs = [
        pl.BlockSpec((rows, GW), lambda b, n: (row(b, n), CB_Q + g)),
        pl.BlockSpec((rows, GW), lambda b, n: (row(b, n), CB_K + g)),
        pl.BlockSpec((rows, GW), lambda b, n: (row(b, n), CB_V + g)),
        pl.BlockSpec((rows, GW), lambda b, n: (row(b, n), 0)),
        pl.BlockSpec((rows, 128), lambda b, n: (row(b, n), 0)),
        pl.BlockSpec((None, None, 4, QB, 2 * QB),
                     lambda b, n: (g, jnp.minimum(jnp.minimum(n, nsb - 1), 1), 0, 0, 0)),
    ]
    args = [proj, proj, proj, d_out, stats, bias]
    scratch = [pltpu.VMEM((rows, GW), BF16)] * 3 + [pltpu.SemaphoreType.DMA((3,))]
    if has_prev:
        in_specs += [pl.BlockSpec((rows, GW), lambda b, n: (prev(b, n), CB_K + g)),
                     pl.BlockSpec((rows, GW), lambda b, n: (prev(b, n), CB_V + g))]
        args += [proj, proj]
        scratch.append(pltpu.VMEM((rows, 3 * GW), F32))
    n_copied = (7 + (2 if has_prev else 0)) * n_sets
    scratch += [pltpu.VMEM((rows, 128), F32)] * (1 + n_copied)
    in_specs.append(pl.BlockSpec(memory_space=pl.ANY))
    args.append(dproj)
    return pl.pallas_call(
        body, name=f"attn_bwd{g}",
        grid=(BL, n_steps),
        in_specs=in_specs,
        out_specs=(pl.BlockSpec(memory_space=pl.ANY),
                   pl.BlockSpec((4, QB, 2 * QB), lambda b, n: (0, 0, 0))),
        out_shape=(jax.ShapeDtypeStruct((T, NCOL), BF16),
                   jax.ShapeDtypeStruct((4, QB, 2 * QB), F32)),
        scratch_shapes=scratch,
        input_output_aliases={len(args) - 1: 0},
        compiler_params=pltpu.CompilerParams(vmem_limit_bytes=VMEM_LIMIT),
    )(*args)


def _tail(x2, tgt2, mod3, o_g, lse_g, proj, w_ao, w_co, w_o, conv_w, conv_b, ln_g, ln_b):
    tm = 256
    per_seq = S // tm
    halo = 16

    def body(x_ref, t_ref, mod_ref, o1_ref, o2_ref, o3_ref, l1_ref, l2_ref, l3_ref,
             ga_ref, u_ref, bg_ref, cg_ref, gc_ref, ma_ref, mc_ref, up_ref, cp_ref,
             wao_ref, wco_ref, wo_ref, cw_ref, cb_ref, lg_ref, lb_ref,
             dproj_ref, dyc_ref, do_ref, st_ref, dxd_ref,
             gwo_ref, gwco_ref, gwao_ref, vec_ref,
             dga_s, dbg_s, dgm_s, sems, acc_o, acc_co, acc_ao):
        i = pl.program_id(0)
        bidx = i // per_seq
        first = (i % per_seq) == 0

        @pl.when(i == 0)
        def _():
            vec_ref[...] = jnp.zeros_like(vec_ref)

        l1, l2, l3 = l1_ref[...], l2_ref[...], l3_ref[...]
        mx = jnp.maximum(jnp.maximum(l1, l2), l3)
        e1, e2, e3 = jnp.exp(l1 - mx), jnp.exp(l2 - mx), jnp.exp(l3 - mx)
        esum = e1 + e2 + e3
        lse_tot = mx + jnp.log(esum)
        w1, w2, w3 = e1 / esum, e2 / esum, e3 / esum

        def per_head(wv):
            return jnp.concatenate([jnp.broadcast_to(wv[:, h:h + 1], (tm, HD)) for h in range(4)], axis=1)

        o = per_head(w1) * o1_ref[...] + per_head(w2) * o2_ref[...] + per_head(w3) * o3_ref[...]

        ga = ga_ref[...].astype(F32)
        sig_ga = _sigmoid(ga)
        silu_ga = ga * sig_ga
        a_in = (o * silu_ga).astype(BF16)
        a_out = _dot(a_in, wao_ref[...])

        u = u_ref[...].astype(F32)
        cg = cg_ref[...].astype(F32)
        z = cg * u
        zp = cp_ref[...].astype(F32) * up_ref[...].astype(F32)
        zp = jnp.where(first, 0.0, zp)
        zcat = jnp.concatenate([zp, z], axis=0)
        z1 = pltpu.roll(zcat, 1, 0)[halo:]
        z2 = pltpu.roll(zcat, 2, 0)[halo:]
        y_conv = cw_ref[0:1, :] * z2 + cw_ref[1:2, :] * z1 + cw_ref[2:3, :] * z + cb_ref[...]
        gc = gc_ref[...].astype(F32)
        sig_gc = _sigmoid(gc)
        silu_gc = gc * sig_gc
        bg = bg_ref[...].astype(F32)
        bg_yc = bg * y_conv
        s_in = (bg_yc * silu_gc).astype(BF16)
        s_out = _dot(s_in, wco_ref[...])

        sa = _sigmoid(ma_ref[...].astype(F32))
        sc = _sigmoid(mc_ref[...].astype(F32))
        merged = (sa * a_out + sc * s_out).astype(BF16)
        y = _dot(merged, wo_ref[...])
        gate1 = 1.0 + mod_ref[0, 2:3, :]
        xv = x_ref[...]
        resid = ALPHA * xv + gate1 * y
        mu = jnp.mean(resid, axis=1, keepdims=True)
        xc = resid - mu
        var = jnp.mean(xc * xc, axis=1, keepdims=True)
        rstd = lax.rsqrt(var + LN_EPS)
        xhat = xc * rstd
        lg = lg_ref[...]
        err = xhat * lg + lb_ref[...] - t_ref[...]
        vec_ref[3:4, :] += (0.5 / D) * jnp.sum(err * err, axis=0, keepdims=True)

        vec_ref[1:2, :] += (1.0 / D) * jnp.sum(err * xhat, axis=0, keepdims=True)
        vec_ref[2:3, :] += (1.0 / D) * jnp.sum(err, axis=0, keepdims=True)
        dxh = err * (lg * (1.0 / D))
        dres = rstd * (dxh - jnp.mean(dxh, axis=1, keepdims=True)
                       - xhat * jnp.mean(dxh * xhat, axis=1, keepdims=True))
        dxd_ref[...] = ALPHA * dres
        dgate = jnp.sum(dres * y, axis=0, keepdims=True)
        vec_ref[4:5, :] += jnp.where(bidx == 0, dgate, 0.0)
        vec_ref[5:6, :] += jnp.where(bidx == 1, dgate, 0.0)
        dy = (dres * gate1).astype(BF16)

        dmerged = _dot_nt(dy, wo_ref[...])
        da_out_f = dmerged * sa
        ds_out_f = dmerged * sc
        da_out = da_out_f.astype(BF16)
        ds_out = ds_out_f.astype(BF16)
        dgm_s[:, 2 * D:3 * D] = (ds_out_f * s_out * (1.0 - sc)).astype(BF16)
        dgm_s[:, D:2 * D] = (da_out_f * a_out * (1.0 - sa)).astype(BF16)
        da_in = _dot_nt(da_out, wao_ref[...])
        ds_in = _dot_nt(ds_out, wco_ref[...])

        d_o = da_in * silu_ga
        do_ref[...] = d_o.astype(BF16)
        dga_s[...] = (da_in * o * (sig_ga + silu_ga * (1.0 - sig_ga))).astype(BF16)
        lane = lax.broadcasted_iota(jnp.int32, (tm, 128), 1)
        stats = lse_tot
        od = o * d_o
        for h in range(4):
            delta = jnp.sum(od[:, h * HD:(h + 1) * HD], axis=1, keepdims=True)
            stats = jnp.where(lane == 4 + h, delta, stats)
        st_ref[...] = stats

        ds_silu = ds_in * silu_gc
        dbg_s[...] = (ds_silu * y_conv).astype(BF16)
        dyc = ds_silu * bg
        dyc_ref[...] = dyc
        vec_ref[0:1, :] += jnp.sum(dyc, axis=0, keepdims=True)
        dgm_s[:, 0:D] = (ds_in * bg_yc * (sig_gc + silu_gc * (1.0 - sig_gc))).astype(BF16)

        @pl.when(i == 0)
        def _():
            acc_o[...] = jnp.zeros_like(acc_o)
            acc_co[...] = jnp.zeros_like(acc_co)
            acc_ao[...] = jnp.zeros_like(acc_ao)

        acc_o[...] += _dot_tn(merged, dy)
        acc_co[...] += _dot_tn(s_in, ds_out)
        acc_ao[...] += _dot_tn(a_in, da_out)

        @pl.when(i == T // tm - 1)
        def _():
            gwo_ref[...] = acc_o[...].astype(BF16)
            gwco_ref[...] = acc_co[...].astype(BF16)
            gwao_ref[...] = acc_ao[...].astype(BF16)

        _write_columns([(dga_s, CB * CB_GA), (dbg_s, D * KB_BG), (dgm_s, D * KB_GC)],
                       dproj_ref, pl.multiple_of(i * tm, tm), sems)

    def tile(width, cblk=0):
        return pl.BlockSpec((tm, width), lambda i: (i, cblk))

    def whole(shape):
        return pl.BlockSpec(shape, lambda i: tuple(0 for _ in shape))

    def once(shape):
        return pl.BlockSpec(shape, lambda i: tuple(0 for _ in shape), pipeline_mode=pl.Buffered(1))

    prev_rows = lambda i: (jnp.maximum(i * (tm // halo) - 1, 0),)
    in_specs = [
        tile(D), tile(D), pl.BlockSpec((1, 3, D), lambda i: (i // per_seq, 0, 0)),
        tile(GW), tile(GW), tile(GW), tile(128), tile(128), tile(128),
        tile(GW, CB_GA), tile(D, KB_U), tile(D, KB_BG), tile(D, KB_CG), tile(D, KB_GC),
        tile(D, KB_MA), tile(D, KB_MC),
        pl.BlockSpec((halo, D), lambda i: (*prev_rows(i), KB_U)),
        pl.BlockSpec((halo, D), lambda i: (*prev_rows(i), KB_CG)),
        whole((GW, D)), whole((D, D)), whole((D, D)),
        whole((3, D)), whole((1, D)), whole((1, D)), whole((1, D)),
    ]
    out_specs = (
        pl.BlockSpec(memory_space=pl.ANY), tile(D), tile(GW), tile(128), tile(D),
        once((D, D)), once((D, D)), once((GW, D)),
        pl.BlockSpec((8, D), lambda i: (0, 0)),
    )
    out_shape = (
        jax.ShapeDtypeStruct((T, NCOL), BF16),
        jax.ShapeDtypeStruct((T, D), F32),
        jax.ShapeDtypeStruct((T, GW), BF16),
        jax.ShapeDtypeStruct((T, 128), F32),
        jax.ShapeDtypeStruct((T, D), F32),
        jax.ShapeDtypeStruct((D, D), BF16),
        jax.ShapeDtypeStruct((D, D), BF16),
        jax.ShapeDtypeStruct((GW, D), BF16),
        jax.ShapeDtypeStruct((8, D), F32),
    )
    return pl.pallas_call(
        body, name="tail",
        grid=(T // tm,),
        in_specs=in_specs, out_specs=out_specs, out_shape=out_shape,
        scratch_shapes=[pltpu.VMEM((tm, GW), BF16), pltpu.VMEM((tm, D), BF16), pltpu.VMEM((tm, 3 * D), BF16),
                        pltpu.SemaphoreType.DMA((3,)),
                        pltpu.VMEM((D, D), F32), pltpu.VMEM((D, D), F32), pltpu.VMEM((GW, D), F32)],
        compiler_params=pltpu.CompilerParams(vmem_limit_bytes=VMEM_LIMIT_TAIL),
    )(x2, tgt2, mod3, *o_g, *lse_g, proj, proj, proj, proj, proj, proj, proj, proj, proj,
      w_ao, w_co, w_o, conv_w, conv_b, ln_g, ln_b)


def _conv_bwd(dyc, proj, conv_w, dproj):
    tm = 512
    per_seq = S // tm

    def body(d_ref, dn_ref, u_ref, c_ref, cw_ref, _, dproj_ref, g_ref, du_s, dc_s, sems):
        i = pl.program_id(0)
        last = (i % per_seq) == per_seq - 1

        @pl.when(i == 0)
        def _():
            g_ref[...] = jnp.zeros_like(g_ref)

        d = d_ref[...]
        dn = jnp.where(last, 0.0, dn_ref[...])
        dcat = jnp.concatenate([d, dn], axis=0)
        d1 = pltpu.roll(dcat, tm + 8 - 1, 0)[:tm]
        d2 = pltpu.roll(dcat, tm + 8 - 2, 0)[:tm]
        dz = cw_ref[2:3, :] * d + cw_ref[1:2, :] * d1 + cw_ref[0:1, :] * d2
        u = u_ref[...].astype(F32)
        cg = c_ref[...].astype(F32)
        du_s[...] = (dz * cg).astype(BF16)
        dc_s[...] = (dz * u).astype(BF16)
        _write_columns([(du_s, D * KB_U), (dc_s, D * KB_CG)], dproj_ref, pl.multiple_of(i * tm, tm), sems)

        z = cg * u
        g_ref[0:1, :] += jnp.sum(d2 * z, axis=0, keepdims=True)
        g_ref[1:2, :] += jnp.sum(d1 * z, axis=0, keepdims=True)
        g_ref[2:3, :] += jnp.sum(d * z, axis=0, keepdims=True)

    n_tiles = T // tm
    next_rows = lambda i: jnp.minimum((i + 1) * (tm // 8), T // 8 - 1)
    return pl.pallas_call(
        body, name="conv_bwd",
        grid=(n_tiles,),
        in_specs=[pl.BlockSpec((tm, D), lambda i: (i, 0)),
                  pl.BlockSpec((8, D), lambda i: (next_rows(i), 0)),
                  pl.BlockSpec((tm, D), lambda i: (i, KB_U)),
                  pl.BlockSpec((tm, D), lambda i: (i, KB_CG)),
                  pl.BlockSpec((3, D), lambda i: (0, 0)),
                  pl.BlockSpec(memory_space=pl.ANY)],
        out_specs=(pl.BlockSpec(memory_space=pl.ANY),
                   pl.BlockSpec((8, D), lambda i: (0, 0))),
        out_shape=(jax.ShapeDtypeStruct((T, NCOL), BF16),
                   jax.ShapeDtypeStruct((8, D), F32)),
        scratch_shapes=[pltpu.VMEM((tm, D), BF16), pltpu.VMEM((tm, D), BF16), pltpu.SemaphoreType.DMA((2,))],
        input_output_aliases={5: 0},
        compiler_params=pltpu.CompilerParams(vmem_limit_bytes=VMEM_LIMIT),
    )(dyc, dyc, proj, proj, conv_w, dproj)


def _dh_dx(dproj, w_in_all, x2, dxd, mod3, chip_sums, hops=(), parts0=None):
    tm = 1024
    per_seq = S // tm
    n = len(chip_sums)
    n_in = 5 + n + (0 if parts0 is None else 1)

    def body(*refs):
        d_ref, w_ref, x_ref, dxd_ref, mod_ref = refs[:5]
        ins = refs[5:5 + n]
        gx_ref, vec_ref = refs[n_in:n_in + 2]
        outs = refs[n_in + 2:n_in + 2 + n]
        acc, send_sems, recv_sems, local_sems = refs[n_in + 2 + n:]
        i, jj = pl.program_id(0), pl.program_id(1)

        @pl.when((i == 0) & (jj == 0))
        def _():
            vec_ref[...] = jnp.zeros_like(vec_ref)
            if n:
                sends, _, mine = _chip_copies(ins, outs, send_sems, recv_sems, local_sems, hops)
                for cp in sends + mine:
                    cp.start()

        if n:
            @pl.when((i == T // tm - 1) & (jj == N_DEV - 1))
            def _():
                sends, arrivals, mine = _chip_copies(ins, outs, send_sems, recv_sems, local_sems, hops)
                for cp in arrivals:
                    cp.wait_recv()
                for cp in sends:
                    cp.wait_send()
                for cp in mine:
                    cp.wait()

        @pl.when(jj == 0)
        def _():
            acc[...] = jnp.zeros_like(acc)

        acc[...] += _dot_nt(d_ref[...], w_ref[...])

        @pl.when(jj == N_DEV - 1)
        def _():
            dh = acc[...]
            bidx = i // per_seq
            gx_ref[...] = dxd_ref[...] + dh * (1.0 + mod_ref[0, 1:2, :])
            dshift = jnp.sum(dh, axis=0, keepdims=True)
            dscale = jnp.sum(dh * x_ref[...], axis=0, keepdims=True)
            vec_ref[0:1, :] += jnp.where(bidx == 0, dshift, 0.0)
            vec_ref[1:2, :] += jnp.where(bidx == 1, dshift, 0.0)
            vec_ref[2:3, :] += jnp.where(bidx == 0, dscale, 0.0)
            vec_ref[3:4, :] += jnp.where(bidx == 1, dscale, 0.0)

    any_spec = pl.BlockSpec(memory_space=pl.ANY)
    res = pl.pallas_call(
        body, name="dh_dx",
        grid=(T // tm, N_DEV),
        in_specs=[
            pl.BlockSpec((tm, SHARD), lambda i, jj: (i, jj)),
            pl.BlockSpec((None, D, SHARD), lambda i, jj: (jj, 0, 0)),
            pl.BlockSpec((tm, D), lambda i, jj: (i, 0)),
            pl.BlockSpec((tm, D), lambda i, jj: (i, 0)),
            pl.BlockSpec((1, 3, D), lambda i, jj: (i // per_seq, 0, 0))] + [any_spec] * (n_in - 5),
        out_specs=(pl.BlockSpec((tm, D), lambda i, jj: (i, 0)),
                   pl.BlockSpec((8, D), lambda i, jj: (0, 0))) + (any_spec,) * n,
        out_shape=(jax.ShapeDtypeStruct((T, D), F32), jax.ShapeDtypeStruct((8, D), F32))
                  + tuple(jax.ShapeDtypeStruct(a.shape, a.dtype) for a in chip_sums),
        scratch_shapes=[pltpu.VMEM((tm, D), F32), pltpu.SemaphoreType.DMA((max(3 * n, 1),)),
                        pltpu.SemaphoreType.DMA((max(3 * n, 1),)), pltpu.SemaphoreType.DMA((max(n, 1),))],
        input_output_aliases={} if parts0 is None else {5 + n: 2},
        compiler_params=pltpu.CompilerParams(vmem_limit_bytes=VMEM_LIMIT),
    )(dproj, w_in_all, x2, dxd, mod3, *chip_sums, *([] if parts0 is None else [parts0]))
    return res[0], res[1], res[2:]


def _adam_step(g, w, m, v):
    nm = ADAM_B1 * m + (1.0 - ADAM_B1) * g
    nv = ADAM_B2 * v + (1.0 - ADAM_B2) * (g * g)
    m_hat = nm / (1.0 - ADAM_B1 ** ADAM_STEP)
    v_hat = nv / (1.0 - ADAM_B2 ** ADAM_STEP)
    return -ADAM_LR * (m_hat / (jnp.sqrt(v_hat) + ADAM_EPS) + ADAM_WD * w), nm, nv


def _adamw(parts, w, m, v, name, row_tile=None):
    n_parts, rows, cols = parts.shape
    tr = rows if row_tile is None else row_tile

    def body(p_ref, w_ref, m_ref, v_ref, g_ref, d_ref, nm_ref, nv_ref):
        g = p_ref[0].astype(F32)
        for s in range(1, n_parts):
            g = g + p_ref[s].astype(F32)
        g_ref[...] = g
        d_ref[...], nm_ref[...], nv_ref[...] = _adam_step(g, w_ref[...], m_ref[...], v_ref[...])

    blk = pl.BlockSpec((tr, cols), lambda i: (i, 0))
    shp = jax.ShapeDtypeStruct((rows, cols), F32)
    return pl.pallas_call(
        body, name=name,
        grid=(rows // tr,),
        in_specs=[pl.BlockSpec((n_parts, tr, cols), lambda i: (0, i, 0)), blk, blk, blk],
        out_specs=(blk, blk, blk, blk),
        out_shape=(shp, shp, shp, shp),
        compiler_params=pltpu.CompilerParams(vmem_limit_bytes=VMEM_LIMIT),
    )(parts, w, m, v)


def _multi_adamw(parts_list, params, name):
    n = len(params)
    flat = [t for wmv in params for t in wmv]

    def body(*refs):
        parts, ins, outs = refs[:n], refs[n:4 * n], refs[4 * n:]
        for p in range(n):
            g = parts[p][0].astype(F32)
            for s in range(1, parts[p].shape[0]):
                g = g + parts[p][s].astype(F32)
            w_ref, m_ref, v_ref = ins[3 * p:3 * p + 3]
            g_ref, d_ref, nm_ref, nv_ref = outs[4 * p:4 * p + 4]
            g_ref[...] = g
            d_ref[...], nm_ref[...], nv_ref[...] = _adam_step(g, w_ref[...], m_ref[...], v_ref[...])

    out_shape = []
    for w, _, _ in params:
        out_shape += [jax.ShapeDtypeStruct(w.shape, F32)] * 4
    res = pl.pallas_call(body, name=name, out_shape=tuple(out_shape))(*parts_list, *flat)
    return [res[4 * p:4 * p + 4] for p in range(n)]


def _small_updates(small_g, dmod_all, rel_parts, params):
    flat = [t for wmv in params for t in wmv]

    def body(sg_ref, dm_ref, rp_ref, *refs):
        ins, outs = refs[:len(flat)], refs[len(flat):]

        def over_devices(row):
            tot = sg_ref[0, row:row + 1, :]
            for s in range(1, N_DEV):
                tot = tot + sg_ref[s, row:row + 1, :]
            return tot

        g_b_ada = dm_ref[0:1, :]
        for r in range(1, N_DEV * BL):
            g_b_ada = g_b_ada + dm_ref[r:r + 1, :]
        g_rel = rp_ref[0]
        for s in range(1, N_DEV):
            g_rel = g_rel + rp_ref[s]
        grads = [g_b_ada, over_devices(0), g_rel, over_devices(1), over_devices(2)]
        outs[0][...] = jnp.sum(over_devices(3), axis=1, keepdims=True)
        for p, g in enumerate(grads):
            w_ref, m_ref, v_ref = ins[3 * p:3 * p + 3]
            g_ref, d_ref, nm_ref, nv_ref = outs[1 + 4 * p:5 + 4 * p]
            g_ref[...] = g
            d_ref[...], nm_ref[...], nv_ref[...] = _adam_step(g, w_ref[...], m_ref[...], v_ref[...])

    out_shape = [jax.ShapeDtypeStruct((1, 1), F32)]
    for w, _, _ in params:
        out_shape += [jax.ShapeDtypeStruct(w.shape, F32)] * 4
    res = pl.pallas_call(body, name="small_updates", out_shape=tuple(out_shape))(small_g, dmod_all, rel_parts, *flat)
    return res[0], [res[1 + 4 * p:5 + 4 * p] for p in range(len(params))]


def _attn_fwd_dense(proj, bias):
    nq = 4
    rows = nq * QB
    nsb = S // rows

    def body(q_ref, k_ref, v_ref, kp_ref, vp_ref, b_ref, o_ref, l_ref, ls0, ls1, ls2, ls3):
        ls = [ls0, ls1, ls2, ls3]
        n = pl.program_id(1)
        lane = lax.broadcasted_iota(jnp.int32, (QB, 128), 1)
        units = [(h, j) for h in range(4) for j in range(nq)]

        def keys(cur_ref, prev_ref, h, j):
            sl = slice(h * HD, (h + 1) * HD)
            if j == 0:
                return jnp.concatenate([prev_ref[:, sl], cur_ref[0:QB, sl]], axis=0)
            return cur_ref[(j - 1) * QB:(j + 1) * QB, sl]

        q = jnp.stack([q_ref[j * QB:(j + 1) * QB, h * HD:(h + 1) * HD] for h, j in units])
        k = jnp.stack([keys(k_ref, kp_ref, h, j) for h, j in units])
        v = jnp.stack([keys(v_ref, vp_ref, h, j) for h, j in units])
        bias_b = jnp.stack([b_ref[jnp.minimum(n, 1), h] if j == 0 else b_ref[1, h] for h, j in units])
        s = jnp.einsum("uqd,ukd->uqk", q, k, preferred_element_type=F32) * SCALE + bias_b
        m = jnp.max(s, axis=-1, keepdims=True)
        p = jnp.exp(s - m)
        l = jnp.sum(p, axis=-1, keepdims=True)
        o = jnp.einsum("uqk,ukd->uqd", p.astype(BF16), v, preferred_element_type=F32) / l
        lse = m + jnp.log(l)
        for i, (h, j) in enumerate(units):
            o_ref[j * QB:(j + 1) * QB, h * HD:(h + 1) * HD] = o[i]
            ls[h][j * QB:(j + 1) * QB, :] = jnp.where(lane == h, lse[i], 0.0)
        l_ref[...] = (ls[0][...] + ls[1][...]) + (ls[2][...] + ls[3][...])

    def row(b, n):
        return b * nsb + n

    def prev(b, n):
        return jnp.maximum((b * nsb + n) * nq - 1, 0)

    in_specs = [
        pl.BlockSpec((rows, GW), lambda b, n: (row(b, n), CB_Q)),
        pl.BlockSpec((rows, GW), lambda b, n: (row(b, n), CB_K)),
        pl.BlockSpec((rows, GW), lambda b, n: (row(b, n), CB_V)),
        pl.BlockSpec((QB, GW), lambda b, n: (prev(b, n), CB_K)),
        pl.BlockSpec((QB, GW), lambda b, n: (prev(b, n), CB_V)),
        pl.BlockSpec((None, 2, 4, QB, 2 * QB), lambda b, n: (0, 0, 0, 0, 0)),
    ]
    return pl.pallas_call(
        body, name="attn_fwd0",
        grid=(BL, nsb),
        in_specs=in_specs,
        out_specs=(pl.BlockSpec((rows, GW), lambda b, n: (row(b, n), 0)),
                   pl.BlockSpec((rows, 128), lambda b, n: (row(b, n), 0))),
        out_shape=(jax.ShapeDtypeStruct((T, GW), F32), jax.ShapeDtypeStruct((T, 128), F32)),
        scratch_shapes=[pltpu.VMEM((rows, 128), F32)] * 4,
        compiler_params=pltpu.CompilerParams(vmem_limit_bytes=VMEM_LIMIT),
    )(proj, proj, proj, proj, proj, bias)


def _attn_bwd_dense(proj, d_out, stats, bias, dproj):
    nq = 4
    rows = nq * QB
    nsb = S // rows
    cols_q, cols_k, cols_v = CB * CB_Q, CB * CB_K, CB * CB_V

    def body(q_ref, k_ref, v_ref, do_ref, st_ref, kp_ref, vp_ref, b_ref, _, out_ref, db_ref,
             sq, sk, sv, carry, sems):
        b, n = pl.program_id(0), pl.program_id(1)
        units = [(h, j) for h in range(4) for j in range(nq)]

        @pl.when((b == 0) & (n == 0))
        def _():
            db_ref[...] = jnp.zeros_like(db_ref)

        @pl.when(n == 0)
        def _():
            carry[...] = jnp.zeros_like(carry)

        def write(first_block, position, count):
            row0 = pl.multiple_of(first_block * QB, QB)
            part = pl.ds(position * QB, count * QB)
            _write_columns([(sq.at[part], cols_q), (sk.at[part], cols_k), (sv.at[part], cols_v)],
                           out_ref, row0, sems)

        @pl.when(n == nsb)
        def _():
            sq[0:QB, :] = carry[:, 0:GW].astype(BF16)
            sk[0:QB, :] = carry[:, GW:2 * GW].astype(BF16)
            sv[0:QB, :] = carry[:, 2 * GW:3 * GW].astype(BF16)
            write((b + 1) * nsb * nq - 1, 0, 1)

        @pl.when(n < nsb)
        def _():
            def keys(cur_ref, prev_ref, h, j):
                sl = slice(h * HD, (h + 1) * HD)
                if j == 0:
                    return jnp.concatenate([prev_ref[:, sl], cur_ref[0:QB, sl]], axis=0)
                return cur_ref[(j - 1) * QB:(j + 1) * QB, sl]

            def block(ref, h, j):
                return ref[j * QB:(j + 1) * QB, h * HD:(h + 1) * HD]

            q = jnp.stack([block(q_ref, h, j) for h, j in units])
            do = jnp.stack([block(do_ref, h, j) for h, j in units])
            k = jnp.stack([keys(k_ref, kp_ref, h, j) for h, j in units])
            v = jnp.stack([keys(v_ref, vp_ref, h, j) for h, j in units])
            bias_b = jnp.stack([b_ref[jnp.minimum(n, 1), h] if j == 0 else b_ref[1, h] for h, j in units])
            lse = jnp.stack([st_ref[j * QB:(j + 1) * QB, h:h + 1] for h, j in units])
            delta = jnp.stack([st_ref[j * QB:(j + 1) * QB, 4 + h:5 + h] for h, j in units])
            s = jnp.einsum("uqd,ukd->uqk", q, k, preferred_element_type=F32) * SCALE + bias_b
            p = jnp.exp(s - lse)
            ds = p * (jnp.einsum("uqd,ukd->uqk", do, v, preferred_element_type=F32) - delta)
            for h in range(4):
                tot = ds[h * nq]
                for j in range(1, nq):
                    tot = tot + ds[h * nq + j]
                db_ref[h] += tot
            dsb, pb = ds.astype(BF16), p.astype(BF16)
            dq = jnp.einsum("uqk,ukd->uqd", dsb, k, preferred_element_type=F32) * SCALE
            dk = jnp.einsum("uqk,uqd->ukd", dsb, q, preferred_element_type=F32) * SCALE
            dv = jnp.einsum("uqk,uqd->ukd", pb, do, preferred_element_type=F32)
            for h in range(4):
                sl = slice(h * HD, (h + 1) * HD)
                u0, last = h * nq, h * nq + nq - 1
                sq[0:QB, sl] = carry[:, sl].astype(BF16)
                sk[0:QB, sl] = (carry[:, GW + h * HD:GW + (h + 1) * HD] + dk[u0, :QB]).astype(BF16)
                sv[0:QB, sl] = (carry[:, 2 * GW + h * HD:2 * GW + (h + 1) * HD] + dv[u0, :QB]).astype(BF16)
                for j in range(nq - 1):
                    pos = slice((j + 1) * QB, (j + 2) * QB)
                    sq[pos, sl] = dq[u0 + j].astype(BF16)
                    sk[pos, sl] = (dk[u0 + j, QB:] + dk[u0 + j + 1, :QB]).astype(BF16)
                    sv[pos, sl] = (dv[u0 + j, QB:] + dv[u0 + j + 1, :QB]).astype(BF16)
                carry[:, sl] = dq[last]
                carry[:, GW + h * HD:GW + (h + 1) * HD] = dk[last, QB:]
                carry[:, 2 * GW + h * HD:2 * GW + (h + 1) * HD] = dv[last, QB:]

            @pl.when(n == 0)
            def _():
                write(b * nsb * nq, 1, nq - 1)

            @pl.when(n > 0)
            def _():
                write((b * nsb + n) * nq - 1, 0, nq)

    def row(b, n):
        return b * nsb + jnp.minimum(n, nsb - 1)

    def prev(b, n):
        return jnp.maximum(row(b, n) * nq - 1, 0)

    in_specs = [
        pl.BlockSpec((rows, GW), lambda b, n: (row(b, n), CB_Q)),
        pl.BlockSpec((rows, GW), lambda b, n: (row(b, n), CB_K)),
        pl.BlockSpec((rows, GW), lambda b, n: (row(b, n), CB_V)),
        pl.BlockSpec((rows, GW), lambda b, n: (row(b, n), 0)),
        pl.BlockSpec((rows, 128), lambda b, n: (row(b, n), 0)),
        pl.BlockSpec((QB, GW), lambda b, n: (prev(b, n), CB_K)),
        pl.BlockSpec((QB, GW), lambda b, n: (prev(b, n), CB_V)),
        pl.BlockSpec((None, 2, 4, QB, 2 * QB), lambda b, n: (0, 0, 0, 0, 0)),
        pl.BlockSpec(memory_space=pl.ANY),
    ]
    return pl.pallas_call(
        body, name="attn_bwd0",
        grid=(BL, nsb + 1),
        in_specs=in_specs,
        out_specs=(pl.BlockSpec(memory_space=pl.ANY),
                   pl.BlockSpec((4, QB, 2 * QB), lambda b, n: (0, 0, 0))),
        out_shape=(jax.ShapeDtypeStruct((T, NCOL), BF16),
                   jax.ShapeDtypeStruct((4, QB, 2 * QB), F32)),
        scratch_shapes=[pltpu.VMEM((rows, GW), BF16)] * 3
                       + [pltpu.VMEM((QB, 3 * GW), F32), pltpu.SemaphoreType.DMA((3,))],
        input_output_aliases={8: 0},
        compiler_params=pltpu.CompilerParams(vmem_limit_bytes=VMEM_LIMIT),
    )(proj, proj, proj, d_out, stats, proj, proj, bias, dproj)


def _attention_forward(proj, rel_bias):
    expand_lanes, grad_lanes, masks = (jnp.asarray(t) for t in _bucket_maps())
    bias = _bias_expand(rel_bias, expand_lanes, masks)
    fwd = [_attn_fwd_dense(proj, bias)] + [_attn_fwd(proj, bias, g) for g in (1, 2)]
    return bias, grad_lanes, [f[0] for f in fwd], [f[1] for f in fwd]


def _local_step(x2, tgt2, mod3, h, proj, attn, w_ao, w_co, w_o, conv_w, conv_b, ln_g, ln_b):
    bias, buckets, o_g, lse_g = attn

    (dproj, dyc, d_o, stats, dxd, gw_o, gw_co, gw_ao, tail_vec) = _tail(
        x2, tgt2, mod3, o_g, lse_g, proj, w_ao, w_co, w_o, conv_w, conv_b, ln_g, ln_b)

    dproj, db = _attn_bwd_dense(proj, d_o, stats, bias, dproj)
    dbias = [db]
    for g in (1, 2):
        dproj, db = _attn_bwd(proj, d_o, stats, bias, dproj, g)
        dbias.append(db)
    g_rel_bias = _bias_grad(*dbias, buckets)
    dproj, conv_vec = _conv_bwd(dyc, proj, conv_w, dproj)

    gw_ao = jnp.transpose(gw_ao.reshape(GW, N_DEV, D // N_DEV), (1, 0, 2))
    return dproj, dxd, gw_ao, gw_co, gw_o, conv_vec, g_rel_bias, tail_vec


def kernel(x, c, w_ada, b_ada, w_in, conv_w, conv_b, rel_bias, w_attn_out, w_conv_out, w_o, ln_g, ln_b, loss_target, m_w_ada, m_b_ada, m_w_in, m_conv_w, m_conv_b, m_rel_bias, m_w_attn_out, m_w_conv_out, m_w_o, m_ln_g, m_ln_b, v_w_ada, v_b_ada, v_w_in, v_conv_w, v_conv_b, v_rel_bias, v_w_attn_out, v_w_conv_out, v_w_o, v_ln_g, v_ln_b):
    me = _my_index()
    x2 = x.reshape(T, D)
    tgt2 = loss_target.reshape(T, D)

    b_cols = lax.dynamic_slice(b_ada, (0, me * ADA_SHARD), (1, ADA_SHARD))
    c_g, mod_in = _mod_exchange(jnp.pad(c, ((0, 8 - BL), (0, 0))), w_ada[0], b_cols)
    c_all = c_g[:, 0:BL, :].reshape(N_DEV * BL, D)
    mod3 = jnp.transpose(mod_in[:, 0:BL, :], (1, 0, 2)).reshape(BL, 3, D)

    h = _prep_h(x2, mod3)
    rows_shape = jax.ShapeDtypeStruct((N_DEV, D // N_DEV, D), BF16)
    proj, w_in_all, (w_ao_g, w_co_g, w_o_g, conv_w_g) = _gather_proj(
        _shard_order(), h, w_in[0].astype(BF16), 1024,
        ([w_attn_out[0].astype(BF16), w_conv_out[0].astype(BF16), w_o[0].astype(BF16), conv_w[0]],
         [jax.ShapeDtypeStruct((N_DEV, GW, D // N_DEV), BF16), rows_shape, rows_shape,
          jax.ShapeDtypeStruct((N_DEV, 3, D // N_DEV), F32)]))

    attn = _attention_forward(proj, rel_bias)
    w_ao_full = jnp.transpose(w_ao_g, (1, 0, 2)).reshape(GW, D)
    w_co_full = w_co_g.reshape(D, D)
    w_o_full = w_o_g.reshape(D, D)
    conv_w_full = jnp.transpose(conv_w_g, (1, 0, 2)).reshape(3, D)

    (dproj, dxd, gw_ao, gw_co, gw_o, conv_vec, g_rel_bias, tail_vec) = _local_step(
        x2, tgt2, mod3, h, proj, attn, w_ao_full, w_co_full, w_o_full,
        conv_w_full, conv_b, ln_g, ln_b)

    g_conv_w_blocks = jnp.transpose(conv_vec[0:3].reshape(3, N_DEV, D // N_DEV), (1, 0, 2))
    partials = [gw_ao, gw_co.reshape(N_DEV, D // N_DEV, D), gw_o.reshape(N_DEV, D // N_DEV, D), g_conv_w_blocks]
    w_in_sums, w_in_parts, sib = _gw_in_pair(
        _slice_order(), h, dproj, partials,
        [jax.ShapeDtypeStruct((4, GW, D // N_DEV), BF16),
         jax.ShapeDtypeStruct((4, D // N_DEV, D), BF16),
         jax.ShapeDtypeStruct((4, D // N_DEV, D), BF16),
         jax.ShapeDtypeStruct((4, 3, D // N_DEV), F32)])
    core = lax.axis_index("c").astype(jnp.int32).reshape(1)
    chip_sums = [w_in_sums] + list(_pair_add(core, partials, sib))
    hops = [(3,)] + [(1, 2, 3)] * 4
    grad_x, mod_vec, (r_in, r_ao, r_co, r_o, r_cw) = _dh_dx(
        dproj, w_in_all, x2, dxd, mod3, chip_sums, hops, w_in_parts)

    small = jnp.concatenate([
        tail_vec[0:4],
        jnp.pad(g_rel_bias.reshape(1, N_BUCKETS * N_HEADS), ((0, 0), (0, D - N_BUCKETS * N_HEADS))),
        jnp.zeros((3, D), F32)], axis=0)
    dmod = jnp.concatenate([mod_vec[0:2], mod_vec[2:4], tail_vec[4:6]], axis=1)
    small_g, dmod_g = _all_gather(
        [small, dmod],
        [jax.ShapeDtypeStruct((N_DEV, 8, D), F32), jax.ShapeDtypeStruct((N_DEV, BL, 3 * D), F32)],
        "gather_small")
    dmod_all = dmod_g.reshape(N_DEV * BL, 3 * D)
    small_names = ["b_ada", "conv_b", "rel_bias", "ln_g", "ln_b"]
    small_params = [(b_ada, m_b_ada, v_b_ada), (conv_b, m_conv_b, v_conv_b), (rel_bias, m_rel_bias, v_rel_bias),
                    (ln_g, m_ln_g, v_ln_g), (ln_b, m_ln_b, v_ln_b)]
    loss, small_res = _small_updates(
        small_g, dmod_all, small_g[:, 4, :N_BUCKETS * N_HEADS].reshape(N_DEV, N_BUCKETS, N_HEADS), small_params)
    loss = loss.reshape(())

    dmod_cols = lax.dynamic_slice(dmod_all, (0, me * ADA_SHARD), (N_DEV * BL, ADA_SHARD))
    res = {
        "w_ada": tuple(t[None] for t in _w_ada_update(jnp.transpose(c_all), dmod_cols,
                                                      w_ada[0], m_w_ada[0], v_w_ada[0])),
        "w_in": tuple(t[None] for t in _adamw(r_in, w_in[0], m_w_in[0], v_w_in[0], "adam_w_in", 128)),
    }
    mid_names = ["conv_w", "w_attn_out", "w_conv_out", "w_o"]
    mid_parts = [r_cw, r_ao, r_co, r_o]
    mid_full = [(conv_w, m_conv_w, v_conv_w), (w_attn_out, m_w_attn_out, v_w_attn_out),
                (w_conv_out, m_w_conv_out, v_w_conv_out), (w_o, m_w_o, v_w_o)]
    mid_res = _multi_adamw(mid_parts, [tuple(t[0] for t in wmv) for wmv in mid_full], "adam_mid")
    for nm, wmv, outs4 in zip(mid_names, mid_full, mid_res):
        res[nm] = tuple(t[None] for t in outs4)
    res.update(dict(zip(small_names, small_res)))
    order = ["w_ada", "b_ada", "w_in", "conv_w", "conv_b", "rel_bias", "w_attn_out", "w_conv_out",
             "w_o", "ln_g", "ln_b"]
    outs = [loss, grad_x.reshape(BL, S, D)]
    for k in range(4):
        outs += [res[name][k] for name in order]
    return tuple(outs)
```

```python
import math

import numpy as np
import jax
import jax.numpy as jnp
from jax import lax
from jax.experimental import pallas as pl
from jax.experimental.pallas import tpu as pltpu

F32 = jnp.float32
BF16 = jnp.bfloat16
MESH = pl.DeviceIdType.MESH

N_DEV = 8
D = 1024
S = 2048
BL = 2
T = BL * S
NCOL = 11264
SHARD = NCOL // N_DEV
CB = 512
NCB = NCOL // CB
HD = 128
GW = 512
QB = 128
DILATIONS = (1, 4, 16)
N_STEPS = 128
N_BUCKETS = 32
N_HEADS = 12
ALPHA = 2.0 ** 0.25
LN_EPS = 1e-5
NEG_INF = -1e30
SCALE = HD ** -0.5
ADA_SHARD = 3 * D // N_DEV

CB_Q, CB_K, CB_V, CB_GA = 0, 3, 6, 9
KB_U, KB_BG, KB_CG, KB_GC, KB_MA, KB_MC = 5, 6, 7, 8, 9, 10

ADAM_LR, ADAM_B1, ADAM_B2, ADAM_EPS, ADAM_WD, ADAM_STEP = 0.001, 0.9, 0.999, 1e-08, 0.01, 10

VMEM_LIMIT = 56 * 1024 * 1024
VMEM_LIMIT_TAIL = 62 * 1024 * 1024


def _dot(a, b):
    return jnp.dot(a, b, preferred_element_type=F32)


def _dot_nt(a, b):
    return lax.dot_general(a, b, (((1,), (1,)), ((), ())), preferred_element_type=F32)


def _dot_tn(a, b):
    return lax.dot_general(a, b, (((0,), (0,)), ((), ())), preferred_element_type=F32)


def _sigmoid(v):
    return 1.0 / (1.0 + jnp.exp(-v))


def _write_columns(pieces, dst_hbm, row0, sems):
    copies = []
    for k, (src, col0) in enumerate(pieces):
        rows, width = src.shape
        copies.append(pltpu.make_async_copy(
            src, dst_hbm.at[pl.ds(row0, rows), pl.ds(col0, width)], sems.at[k]))
    for cp in copies:
        cp.start()
    for cp in copies:
        cp.wait()


def _my_index():
    return 4 * lax.axis_index("x") + 2 * lax.axis_index("y") + lax.axis_index("c")


class _Gather:
    def __init__(self, ins, outs, stage, send_sems, recv_sems, local_sems):
        self.ins, self.outs, self.stage = ins, outs, stage
        self.send_sems, self.recv_sems, self.local_sems = send_sems, recv_sems, local_sems
        x, y, c = lax.axis_index("x"), lax.axis_index("y"), lax.axis_index("c")
        self.c = c
        self.me, self.sibling = (x, y, c), (x, y, 1 - c)
        self.chips = [(1 - x, y), (x, 1 - y), (1 - x, 1 - y)]

    @staticmethod
    def scratch(arrs):
        n = len(arrs)
        return ([pltpu.SemaphoreType.DMA((7 * n,)), pltpu.SemaphoreType.DMA((7 * n,)),
                 pltpu.SemaphoreType.DMA((n,))] + [pltpu.VMEM(a.shape, a.dtype) for a in arrs])

    def _copy(self, a, k, block, to, src=None):
        dst = self.outs[a].at[4 * block[0] + 2 * block[1] + block[2]]
        return pltpu.make_async_remote_copy(
            src_ref=dst if src is None else src, dst_ref=dst,
            send_sem=self.send_sems.at[a * 7 + k], recv_sem=self.recv_sems.at[a * 7 + k],
            device_id=to, device_id_type=MESH)

    def _first(self):
        first = []
        for a in range(len(self.ins)):
            first.append(self._copy(a, 0, self.me, self.sibling, src=self.ins[a]))
            first += [self._copy(a, 1 + j, self.me, (*chip, self.c), src=self.ins[a])
                      for j, chip in enumerate(self.chips)]
        return first

    def _mine(self):
        me = self.me
        return [pltpu.make_async_copy(self.stage[a], self.outs[a].at[4 * me[0] + 2 * me[1] + me[2]],
                                      self.local_sems.at[a]) for a in range(len(self.ins))]

    def begin(self):
        for cp in self._first():
            cp.start()
        loads = [pltpu.make_async_copy(self.ins[a], self.stage[a], self.local_sems.at[a])
                 for a in range(len(self.ins))]
        for cp in loads:
            cp.start()
        for cp in loads:
            cp.wait()
        for cp in self._mine():
            cp.start()

    def finish(self):
        n, c, me, sibling = len(self.ins), self.c, self.me, self.sibling
        passed = []
        for j, chip in enumerate(self.chips):
            for a in range(n):
                self._copy(a, 1 + j, (*chip, c), me).wait_recv()
                fwd = self._copy(a, 4 + j, (*chip, c), sibling)
                fwd.start()
                passed.append(fwd)
        for a in range(n):
            self._copy(a, 0, sibling, me).wait_recv()
        for j, chip in enumerate(self.chips):
            for a in range(n):
                self._copy(a, 4 + j, (*chip, 1 - c), me).wait_recv()
        for cp in self._first() + passed:
            cp.wait_send()
        for cp in self._mine():
            cp.wait()


def _all_gather(arrs, out_shapes, name):
    n = len(arrs)

    def body(*refs):
        g = _Gather(refs[:n], refs[n:2 * n], refs[2 * n + 3:], *refs[2 * n:2 * n + 3])
        g.begin()
        g.finish()

    any_spec = pl.BlockSpec(memory_space=pl.ANY)
    return pl.pallas_call(
        body, name=name,
        out_shape=tuple(out_shapes),
        in_specs=[any_spec] * n,
        out_specs=tuple([any_spec] * n),
        scratch_shapes=_Gather.scratch(arrs),
    )(*arrs)


def _neighbour_chips():
    x, y, c = lax.axis_index("x"), lax.axis_index("y"), lax.axis_index("c")
    first = (jnp.where(c == 0, 1 - x, x), jnp.where(c == 0, y, 1 - y))
    second = (jnp.where(c == 0, x, 1 - x), jnp.where(c == 0, 1 - y, y))
    return first, second, (1 - x, 1 - y)


def _slice_order():
    x, y, c = lax.axis_index("x"), lax.axis_index("y"), lax.axis_index("c")
    nb1, nb2, diag = _neighbour_chips()
    slots = []
    for mine, theirs in ((nb1, nb2), (nb2, nb1), (diag, diag), ((x, y), (x, y))):
        slots += [2 * (2 * theirs[0] + theirs[1]) + 1 - c, 2 * (2 * mine[0] + mine[1]) + c]
    return jnp.stack(slots).astype(jnp.int32)


def _gw_in_pair(order, h, dproj, smalls, small_shapes4):
    kk, m = h.shape
    tk = min(kk, 2048)
    nk = kk // tk
    ncols = dproj.shape[1] // N_DEV
    n = len(smalls)

    def body(order_ref, h_ref, d_ref, *rest):
        ins = rest[:n]
        sums_hbm, parts_hbm = rest[n], rest[n + 1]
        sib = rest[n + 2:2 * n + 2]
        (acc, sendbuf, recvbuf, sumbuf, send_sems, recv_sems, local_sem, ssend, srecv,
         isend, irecv) = rest[2 * n + 2:]
        js, k = pl.program_id(0), pl.program_id(1)
        x, y, c = lax.axis_index("x"), lax.axis_index("y"), lax.axis_index("c")
        sibling = (x, y, 1 - c)
        my_chip = 2 * x + y
        nb1, nb2, _ = _neighbour_chips()
        near = [(*nb1, c), (*nb2, c)]

        def ici_copy(p, out_chip):
            peer = near[p] if isinstance(p, int) else tuple(jnp.where(p == 0, a, b) for a, b in zip(*near))
            return pltpu.make_async_remote_copy(
                src_ref=sumbuf.at[p], dst_ref=parts_hbm.at[out_chip],
                send_sem=isend.at[p], recv_sem=irecv.at[p], device_id=peer, device_id_type=MESH)

        def small_copies():
            return [pltpu.make_async_remote_copy(
                        src_ref=ins[a].at[2 * q + 1 - c], dst_ref=sib[a].at[q],
                        send_sem=ssend.at[a * 4 + q], recv_sem=srecv.at[a * 4 + q],
                        device_id=sibling, device_id_type=MESH)
                    for a in range(n) for q in range(4)]

        def slice_copy(p):
            return pltpu.make_async_remote_copy(
                src_ref=sendbuf, dst_ref=recvbuf.at[p], send_sem=send_sems.at[p], recv_sem=recv_sems.at[p],
                device_id=sibling, device_id_type=MESH)

        def sum_copy(p):
            return pltpu.make_async_copy(sumbuf.at[2], sums_hbm.at[order_ref[2 * p] // 2], local_sem)

        @pl.when((js == 0) & (k == 0))
        def _():
            for cp in small_copies():
                cp.start()

        def partial():
            return _dot_tn(h_ref[...], d_ref[...])

        if nk > 1:
            @pl.when(k == 0)
            def _():
                acc[...] = partial()
        if nk > 2:
            @pl.when((k > 0) & (k < nk - 1))
            def _():
                acc[...] += partial()

        def total():
            return partial() + acc[...] if nk > 1 else partial()

        p = js // 2

        @pl.when((js % 2 == 0) & (k == nk - 1))
        def _():
            @pl.when(p > 0)
            def _():
                slice_copy(p - 1).wait_send()
            sendbuf[...] = total().astype(BF16)
            slice_copy(p).start()

        @pl.when((js % 2 == 1) & (k == nk - 1))
        def _():
            slice_copy(p).wait_recv()

            @pl.when(p == 3)
            def _():
                sum_copy(2).wait()
            sumbuf[jnp.minimum(p, 2)] = (total() + recvbuf[p].astype(F32)).astype(BF16)

            @pl.when(p < 2)
            def _():
                ici_copy(p, my_chip).start()

            @pl.when(p >= 2)
            def _():
                sum_copy(p).start()

        @pl.when((js == N_DEV - 1) & (k == nk - 1))
        def _():
            slice_copy(3).wait_send()
            sum_copy(3).wait()
            for cp in small_copies():
                cp.wait()
            for p in range(2):
                ici_copy(p, 2 * near[p][0] + near[p][1]).wait_recv()
                ici_copy(p, my_chip).wait_send()

    any_spec = pl.BlockSpec(memory_space=pl.ANY)
    res = pl.pallas_call(
        body, name="gw_in_pair",
        grid_spec=pltpu.PrefetchScalarGridSpec(
            num_scalar_prefetch=1,
            grid=(N_DEV, nk),
            in_specs=[pl.BlockSpec((tk, m), lambda js, k, order_ref: (k, 0)),
                      pl.BlockSpec((tk, ncols), lambda js, k, order_ref: (k, order_ref[js]))] + [any_spec] * n,
            out_specs=(any_spec,) * (n + 2),
            scratch_shapes=[pltpu.VMEM((m, ncols), F32), pltpu.VMEM((m, ncols), BF16),
                            pltpu.VMEM((4, m, ncols), BF16), pltpu.VMEM((3, m, ncols), BF16),
                            pltpu.SemaphoreType.DMA((4,)), pltpu.SemaphoreType.DMA((4,)),
                            pltpu.SemaphoreType.DMA,
                            pltpu.SemaphoreType.DMA((4 * n,)), pltpu.SemaphoreType.DMA((4 * n,)),
                            pltpu.SemaphoreType.DMA((2,)), pltpu.SemaphoreType.DMA((2,))]),
        out_shape=(jax.ShapeDtypeStruct((4, m, ncols), BF16),) * 2 + tuple(small_shapes4),
        compiler_params=pltpu.CompilerParams(vmem_limit_bytes=VMEM_LIMIT),
    )(order, h, dproj, *smalls)
    return res[0], res[1], res[2:]


def _chip_copies(ins, outs, send_sems, recv_sems, local_sems, hops):
    n = len(ins)
    x, y, c = lax.axis_index("x"), lax.axis_index("y"), lax.axis_index("c")
    my_chip = 2 * x + y

    def peer_of(k):
        return ((1 - x) if (k >> 1) & 1 else x, (1 - y) if k & 1 else y, c)

    def copy(a, k, out_chip):
        peer = peer_of(k)
        return pltpu.make_async_remote_copy(
            src_ref=ins[a].at[2 * peer[0] + peer[1]], dst_ref=outs[a].at[out_chip],
            send_sem=send_sems.at[a * 3 + k - 1], recv_sem=recv_sems.at[a * 3 + k - 1],
            device_id=peer, device_id_type=MESH)

    sends = [copy(a, k, my_chip) for k in range(1, 4) for a in range(n) if k in hops[a]]
    arrivals = []
    for k in range(1, 4):
        peer = peer_of(k)
        arrivals += [copy(a, k, 2 * peer[0] + peer[1]) for a in range(n) if k in hops[a]]
    mine = [pltpu.make_async_copy(ins[a].at[my_chip], outs[a].at[my_chip], local_sems.at[a])
            for a in range(n)]
    return sends, arrivals, mine


def _pair_add(core, mines, theirs):
    n = len(mines)

    def body(core_ref, *refs):
        mine, sib, outs = refs[:n], refs[n:2 * n], refs[2 * n:]
        for a in range(n):
            for q in range(4):
                outs[a][q] = (mine[a][2 * q + core_ref[0]].astype(F32)
                              + sib[a][q].astype(F32)).astype(outs[a].dtype)

    return pl.pallas_call(
        body, name="pair_add",
        in_specs=[pl.BlockSpec(memory_space=pltpu.SMEM)] + [pl.BlockSpec(memory_space=pltpu.VMEM)] * (2 * n),
        out_shape=tuple(jax.ShapeDtypeStruct(t.shape, t.dtype) for t in theirs),
    )(core, *mines, *theirs)


def _mod_exchange(c8, w_ada, b_cols):
    cols = w_ada.shape[1]

    def body(c_ref, w_ref, b_ref, call_ref, mod_ref, msend, send1, recv1, send2, recv2):
        x, y, c = lax.axis_index("x"), lax.axis_index("y"), lax.axis_index("c")
        my_slot = 4 * x + 2 * y + c

        def peer_of(k):
            return ((1 - x) if (k >> 2) & 1 else x, (1 - y) if (k >> 1) & 1 else y, (1 - c) if k & 1 else c)

        def slot_of(dev):
            return 4 * dev[0] + 2 * dev[1] + dev[2]

        def exchange(src_of, dst_ref, send_sems, recv_sems):
            sends, arrivals = [], []
            for k in range(1, 8):
                peer = peer_of(k)
                sends.append(pltpu.make_async_remote_copy(
                    src_ref=src_of(slot_of(peer)), dst_ref=dst_ref.at[my_slot],
                    send_sem=send_sems.at[k - 1], recv_sem=recv_sems.at[k - 1],
                    device_id=peer, device_id_type=MESH))
                arrivals.append(pltpu.make_async_remote_copy(
                    src_ref=src_of(my_slot), dst_ref=dst_ref.at[slot_of(peer)],
                    send_sem=send_sems.at[k - 1], recv_sem=recv_sems.at[k - 1],
                    device_id=peer, device_id_type=MESH))
            for cp in sends:
                cp.start()
            for cp in arrivals:
                cp.wait_recv()
            for cp in sends:
                cp.wait_send()

        call_ref[my_slot] = c_ref[...]
        exchange(lambda s: c_ref, call_ref, send1, recv1)
        cv = call_ref[...].reshape(N_DEV * 8, c_ref.shape[1])
        act = cv * _sigmoid(cv)
        mod = jnp.dot(act, w_ref[...], preferred_element_type=F32,
                      precision=lax.Precision.HIGHEST) + b_ref[...]
        msend[...] = mod.reshape(N_DEV, 8, cols)
        mod_ref[my_slot] = msend[my_slot]
        exchange(lambda s: msend.at[s], mod_ref, send2, recv2)

    return pl.pallas_call(
        body, name="mod_exchange",
        out_shape=(jax.ShapeDtypeStruct((N_DEV, 8, c8.shape[1]), F32),
                   jax.ShapeDtypeStruct((N_DEV, 8, cols), F32)),
        scratch_shapes=[pltpu.VMEM((N_DEV, 8, cols), F32)] + [pltpu.SemaphoreType.DMA((7,))] * 4,
    )(c8, w_ada, b_cols)


def _w_ada_update(c_all_t, dmod_cols, w, m, v):
    rows, cols = w.shape
    tr = 256

    def body(c_ref, d_ref, w_ref, m_ref, v_ref, g_ref, dl_ref, nm_ref, nv_ref):
        cv = c_ref[...]
        g = jnp.dot(cv * _sigmoid(cv), d_ref[...], preferred_element_type=F32,
                    precision=lax.Precision.HIGHEST)
        g_ref[...] = g
        dl_ref[...], nm_ref[...], nv_ref[...] = _adam_step(g, w_ref[...], m_ref[...], v_ref[...])

    blk = pl.BlockSpec((tr, cols), lambda i: (i, 0))
    shp = jax.ShapeDtypeStruct((rows, cols), F32)
    return pl.pallas_call(
        body, name="adam_w_ada",
        grid=(rows // tr,),
        in_specs=[pl.BlockSpec((tr, c_all_t.shape[1]), lambda i: (i, 0)),
                  pl.BlockSpec(dmod_cols.shape, lambda i: (0, 0)), blk, blk, blk],
        out_specs=(blk, blk, blk, blk),
        out_shape=(shp, shp, shp, shp),
    )(c_all_t, dmod_cols, w, m, v)


def _shard_order():
    x, y, c = lax.axis_index("x"), lax.axis_index("y"), lax.axis_index("c")
    first, second, diag = _neighbour_chips()
    devs = [(x, y, c), (x, y, 1 - c), (*first, c), (*second, 1 - c), (*second, c), (*first, 1 - c),
            (*diag, c), (*diag, 1 - c)]
    return jnp.stack([4 * d[0] + 2 * d[1] + d[2] for d in devs]).astype(jnp.int32)


def _prep_h(x2, mod3):
    ts = 512
    per_seq = S // ts

    def body(x_ref, mod_ref, h_ref):
        shift = mod_ref[0, 0:1, :]
        scale = mod_ref[0, 1:2, :]
        h_ref[...] = (x_ref[...] * (1.0 + scale) + shift).astype(BF16)

    return pl.pallas_call(
        body, name="prep_h",
        grid=(T // ts,),
        in_specs=[pl.BlockSpec((ts, D), lambda i: (i, 0)),
                  pl.BlockSpec((1, 3, D), lambda i: (i // per_seq, 0, 0))],
        out_specs=pl.BlockSpec((ts, D), lambda i: (i, 0)),
        out_shape=jax.ShapeDtypeStruct((T, D), BF16),
    )(x2, mod3)


def _gather_proj(order, h, w_shard, tm, ride=()):
    rows, kdim = h.shape
    ncols = w_shard.shape[1]
    n_i = rows // tm
    ride_arrs, ride_shapes = ride if ride else ((), ())
    n_ride = len(ride_arrs)

    def body(order_ref, h_ref, mine_hbm, *rest):
        ride_ins = rest[:n_ride]
        o_ref, all_hbm = rest[n_ride:n_ride + 2]
        ride_outs = rest[n_ride + 2:2 * n_ride + 2]
        wv, send_sems, recv_sems, local_sems = rest[2 * n_ride + 2:2 * n_ride + 6]
        ride_scr = rest[2 * n_ride + 6:]
        j, i = pl.program_id(0), pl.program_id(1)
        c = lax.axis_index("c")
        me, sibling = (lax.axis_index("x"), lax.axis_index("y"), c), (lax.axis_index("x"), lax.axis_index("y"), 1 - c)
        nb1, nb2, diag = _neighbour_chips()

        def slot(dev):
            return 4 * dev[0] + 2 * dev[1] + dev[2]

        def copy(k, block, to, src=None, part=None):
            buf = wv.at[slot(block)]
            if part is not None:
                buf = buf.at[pl.ds(pl.multiple_of(part * (kdim // 2), kdim // 2), kdim // 2)]
            return pltpu.make_async_remote_copy(
                src_ref=buf if src is None else src, dst_ref=buf,
                send_sem=send_sems.at[k], recv_sem=recv_sems.at[k],
                device_id=to, device_id_type=MESH)

        def keep(step, block):
            return pltpu.make_async_copy(wv.at[slot(block)], all_hbm.at[slot(block)], local_sems.at[step])

        if n_ride:
            gather = _Gather(ride_ins, ride_outs, ride_scr[3:], *ride_scr[:3])
        to_sibling, to_nb1, to_nb2 = (copy(0, me, sibling, mine_hbm), copy(1, me, (*nb1, c), mine_hbm),
                                      copy(2, me, (*nb2, c), mine_hbm))
        relay1, relay2 = copy(3, (*nb2, c), (*nb1, c), part=c), copy(4, (*nb1, c), (*nb2, c), part=1 - c)
        pass_nb1, pass_nb2 = copy(5, (*nb1, c), sibling), copy(6, (*nb2, c), sibling)
        pass_d1, pass_d2 = copy(7, (*diag, c), sibling, part=c), copy(8, (*diag, c), sibling, part=1 - c)
        sends = [to_sibling, to_nb1, to_nb2, relay1, relay2, pass_nb1, pass_nb2, pass_d1, pass_d2]
        due = [
            (me, [], []),
            (sibling, [copy(0, sibling, me)], [[]]),
            ((*nb1, c), [copy(1, (*nb1, c), me)], [[pass_nb1, to_nb2]]),
            ((*nb2, 1 - c), [copy(5, (*nb2, 1 - c), me)], [[]]),
            ((*nb2, c), [copy(2, (*nb2, c), me)], [[pass_nb2, relay1, relay2]]),
            ((*nb1, 1 - c), [copy(6, (*nb1, 1 - c), me)], [[]]),
            ((*diag, c), [copy(3, (*diag, c), me, part=c), copy(4, (*diag, c), me, part=1 - c)],
             [[pass_d1], [pass_d2]]),
            ((*diag, 1 - c), [copy(7, (*diag, 1 - c), me, part=1 - c), copy(8, (*diag, 1 - c), me, part=c)],
             [[], []]),
        ]

        @pl.when((j == 0) & (i == 0))
        def _():
            to_sibling.start()
            to_nb1.start()
            load = pltpu.make_async_copy(mine_hbm, wv.at[slot(me)], local_sems.at[N_DEV])
            load.start()
            load.wait()
            keep(0, me).start()

        for step in range(1, N_DEV):
            block, arrivals, then = due[step]

            @pl.when((j == step) & (i == 0))
            def _():
                for arrival, follow in zip(arrivals, then):
                    arrival.wait_recv()
                    for cp in follow:
                        cp.start()
                keep(step, block).start()
                if n_ride and step == N_DEV - 2:
                    gather.begin()

        o_ref[...] = _dot(h_ref[...], wv[order_ref[j]]).astype(BF16)

        @pl.when((j == N_DEV - 1) & (i == n_i - 1))
        def _():
            for cp in sends:
                cp.wait_send()
            for step in range(N_DEV):
                keep(step, due[step][0]).wait()
            if n_ride:
                gather.finish()

    any_spec = pl.BlockSpec(memory_space=pl.ANY)
    res = pl.pallas_call(
        body, name="gather_proj",
        grid_spec=pltpu.PrefetchScalarGridSpec(
            num_scalar_prefetch=1,
            grid=(N_DEV, n_i),
            in_specs=[pl.BlockSpec((tm, kdim), lambda j, i, order_ref: (i, 0)), any_spec] + [any_spec] * n_ride,
            out_specs=(pl.BlockSpec((tm, ncols), lambda j, i, order_ref: (i, order_ref[j])), any_spec)
                      + (any_spec,) * n_ride,
            scratch_shapes=[pltpu.VMEM((N_DEV, kdim, ncols), BF16),
                            pltpu.SemaphoreType.DMA((9,)), pltpu.SemaphoreType.DMA((9,)),
                            pltpu.SemaphoreType.DMA((N_DEV + 1,))]
                           + (_Gather.scratch(ride_arrs) if n_ride else [])),
        out_shape=(jax.ShapeDtypeStruct((rows, N_DEV * ncols), BF16),
                   jax.ShapeDtypeStruct((N_DEV, kdim, ncols), BF16)) + tuple(ride_shapes),
        compiler_params=pltpu.CompilerParams(vmem_limit_bytes=VMEM_LIMIT),
    )(order, h, w_shard, *ride_arrs)
    return res[0], res[1], res[2:]


SKEW_W = 512


def _bucket_maps():
    lanes = np.arange(SKEW_W)

    def buckets_of(steps):
        rows = []
        for dil in DILATIONS:
            dist = np.maximum(steps, 0) * dil
            nf = np.maximum(dist, 1).astype(np.float32)
            large = 16 + (np.log(nf / np.float32(16)) / np.float32(math.log(128.0))
                          * np.float32(16)).astype(np.int32)
            large = np.minimum(large, N_BUCKETS - 1)
            bucket = np.where(dist < 16, dist, large)
            rows.append(np.where((steps >= 0) & (steps <= N_STEPS), bucket, -1).astype(np.int32))
        return np.stack(rows)[:, None, :]

    a = np.arange(QB)[:, None]
    b = np.arange(2 * QB)[None, :]
    steps = a + QB - b
    band = (steps >= 0) & (steps <= N_STEPS)
    first = band & (b >= QB)
    masks = np.stack([first, band]).astype(np.int32)
    return buckets_of(QB - lanes), buckets_of(2 * QB - 1 - lanes), masks


def _bias_expand(rel_bias, lane_buckets, masks):
    def body(tab_ref, bk_ref, mk_ref, o_ref):
        for g in range(3):
            bk = bk_ref[g]
            for h in range(4):
                col = 4 * g + h
                per_offset = jnp.zeros((1, SKEW_W), F32)
                for k in range(N_BUCKETS):
                    per_offset = jnp.where(bk == k, tab_ref[k, col], per_offset)
                tile = pltpu.roll(jnp.broadcast_to(per_offset, (QB, SKEW_W)), 0, 1, stride=1, stride_axis=0)
                tile = tile[:, :2 * QB]
                o_ref[g, 0, h] = jnp.where(mk_ref[0] != 0, tile, NEG_INF)
                o_ref[g, 1, h] = jnp.where(mk_ref[1] != 0, tile, NEG_INF)

    return pl.pallas_call(
        body, name="bias_expand",
        in_specs=[pl.BlockSpec(memory_space=pltpu.SMEM),
                  pl.BlockSpec(memory_space=pltpu.VMEM),
                  pl.BlockSpec(memory_space=pltpu.VMEM)],
        out_shape=jax.ShapeDtypeStruct((3, 2, 4, QB, 2 * QB), F32),
    )(rel_bias, lane_buckets, masks)


def _bias_grad(ds1, ds2, ds3, lane_buckets):
    exchange = jnp.asarray(np.eye(QB, dtype=np.float32)[::-1].copy())

    def body(d1_ref, d2_ref, d3_ref, bk_ref, ex_ref, o_ref):
        for g, d_ref in enumerate((d1_ref, d2_ref, d3_ref)):
            bk = bk_ref[g]
            for h in range(4):
                flipped = jnp.dot(ex_ref[...], d_ref[h], preferred_element_type=F32,
                                  precision=lax.Precision.HIGHEST)
                padded = jnp.concatenate([flipped, jnp.zeros((QB, SKEW_W - 2 * QB), F32)], axis=1)
                skewed = pltpu.roll(padded, 0, 1, stride=1, stride_axis=0)
                per_offset = jnp.sum(skewed, axis=0, keepdims=True)
                for k in range(N_BUCKETS):
                    o_ref[k, 4 * g + h] = jnp.sum(jnp.where(bk == k, per_offset, 0.0))

    return pl.pallas_call(
        body, name="bias_grad",
        in_specs=[pl.BlockSpec(memory_space=pltpu.VMEM)] * 5,
        out_specs=pl.BlockSpec(memory_space=pltpu.SMEM),
        out_shape=jax.ShapeDtypeStruct((N_BUCKETS, N_HEADS), F32),
    )(ds1, ds2, ds3, lane_buckets, exchange)


def _scratch_sets(rows):
    return 4 if rows <= 512 else 1


def _unit_chunks(dil, size=16):
    units = [(h, r) for h in range(4) for r in range(dil)]
    return [units[i:i + size] for i in range(0, len(units), size)]


def _residue_rows(src_ref, copies, h, residue):
    buf = copies[h % len(copies)]
    buf[...] = src_ref[:, h * HD:(h + 1) * HD].astype(F32)
    return lambda r: buf[residue(r), :].astype(BF16)


def _attn_fwd(proj, bias, g):
    dil = DILATIONS[g]
    rows = QB * dil
    nsb = S // rows
    has_prev = nsb > 1

    def residue(r):
        return pl.ds(r, QB, stride=dil)

    n_sets = _scratch_sets(rows)
    n_in = 6 if has_prev else 4
    n_copied = (4 + (2 if has_prev else 0)) * n_sets

    def body(*refs):
        q_ref, kc_ref, vc_ref = refs[:3]
        kp_ref, vp_ref = refs[3:5] if has_prev else (None, None)
        b_ref = refs[n_in - 1]
        o_ref, l_ref = refs[n_in:n_in + 2]
        scr = list(refs[n_in + 2:])
        ls = [scr.pop(0) for _ in range(4)]
        copies = {name: [scr.pop(0) for _ in range(n_sets)]
                  for name in ("q", "kc", "vc", "o") + (("kp", "vp") if has_prev else ())}
        lane = lax.broadcasted_iota(jnp.int32, (QB, 128), 1)
        refs_of = {"q": q_ref, "kc": kc_ref, "vc": vc_ref, "kp": kp_ref, "vp": vp_ref}
        for chunk in _unit_chunks(dil):
            rows_of = {h: {name: _residue_rows(refs_of[name], copies[name], h, residue)
                           for name in refs_of if refs_of[name] is not None}
                       for h in sorted({h for h, _ in chunk})}

            def batch(name):
                return jnp.stack([rows_of[h][name](r) for h, r in chunk])

            q, k, v = batch("q"), batch("kc"), batch("vc")
            if has_prev:
                k = jnp.concatenate([batch("kp"), k], axis=1)
                v = jnp.concatenate([batch("vp"), v], axis=1)
                bias_b = jnp.stack([b_ref[h] for h, _ in chunk])
            else:
                bias_b = jnp.stack([b_ref[h, :, QB:] for h, _ in chunk])
            s = jnp.einsum("uqd,ukd->uqk", q, k, preferred_element_type=F32) * SCALE + bias_b
            m = jnp.max(s, axis=-1, keepdims=True)
            p = jnp.exp(s - m)
            l = jnp.sum(p, axis=-1, keepdims=True)
            o = jnp.einsum("uqk,ukd->uqd", p.astype(BF16), v, preferred_element_type=F32) / l
            lse = m + jnp.log(l)
            for i, (h, r) in enumerate(chunk):
                copies["o"][h % n_sets][residue(r), :] = o[i]
                ls[h][r * QB:(r + 1) * QB, :] = jnp.where(lane == h, lse[i], 0.0)
            for h in sorted({h for h, _ in chunk}):
                o_ref[:, h * HD:(h + 1) * HD] = copies["o"][h % n_sets][...]
        for r in range(dil):
            blk = slice(r * QB, (r + 1) * QB)
            l_ref[residue(r), :] = (ls[0][blk, :] + ls[1][blk, :]) + (ls[2][blk, :] + ls[3][blk, :])

    def row(b, n):
        return b * nsb + n

    def prev(b, n):
        return b * nsb + jnp.maximum(n - 1, 0)

    in_specs = [
        pl.BlockSpec((rows, GW), lambda b, n: (row(b, n), CB_Q + g)),
        pl.BlockSpec((rows, GW), lambda b, n: (row(b, n), CB_K + g)),
        pl.BlockSpec((rows, GW), lambda b, n: (row(b, n), CB_V + g)),
    ]
    args = [proj, proj, proj]
    scratch = [pltpu.VMEM((rows, 128), F32)] * (4 + n_copied)
    if has_prev:
        in_specs += [pl.BlockSpec((rows, GW), lambda b, n: (prev(b, n), CB_K + g)),
                     pl.BlockSpec((rows, GW), lambda b, n: (prev(b, n), CB_V + g))]
        args += [proj, proj]
    in_specs.append(pl.BlockSpec((None, None, 4, QB, 2 * QB),
                                 lambda b, n: (g, jnp.minimum(n, 1), 0, 0, 0)))
    args.append(bias)
    return pl.pallas_call(
        body, name=f"attn_fwd{g}",
        grid=(BL, nsb),
        in_specs=in_specs,
        out_specs=(pl.BlockSpec((rows, GW), lambda b, n: (row(b, n), 0)),
                   pl.BlockSpec((rows, 128), lambda b, n: (row(b, n), 0))),
        out_shape=(jax.ShapeDtypeStruct((T, GW), F32), jax.ShapeDtypeStruct((T, 128), F32)),
        scratch_shapes=scratch,
        compiler_params=pltpu.CompilerParams(vmem_limit_bytes=VMEM_LIMIT),
    )(*args)


def _attn_bwd(proj, d_out, stats, bias, dproj, g):
    dil = DILATIONS[g]
    rows = QB * dil
    nsb = S // rows
    has_prev = nsb > 1
    n_steps = nsb + 1 if has_prev else 1
    n_in = 7 + (2 if has_prev else 0)

    def residue(r):
        return pl.ds(r, QB, stride=dil)

    n_sets = _scratch_sets(rows)

    def body(*refs):
        q_ref, kc_ref, vc_ref, do_ref, st_ref, b_ref = refs[:6]
        kp_ref, vp_ref = refs[6:8] if has_prev else (None, None)
        out_ref, db_ref = refs[n_in], refs[n_in + 1]
        scr = list(refs[n_in + 2:])
        sq, sk, sv, sems = [scr.pop(0) for _ in range(4)]
        carry = scr.pop(0) if has_prev else None
        sts = scr.pop(0)
        copies = {name: [scr.pop(0) for _ in range(n_sets)]
                  for name in ("q", "kc", "vc", "do", "dq", "dk", "dv") + (("kp", "vp") if has_prev else ())}
        b, n = pl.program_id(0), pl.program_id(1)

        @pl.when((b == 0) & (n == 0))
        def _():
            db_ref[...] = jnp.zeros_like(db_ref)

        def finish(h, r, dq, dk, dv):
            for name, val in (("dq", dq), ("dk", dk), ("dv", dv)):
                copies[name][h % n_sets][residue(r), :] = val

        def finish_head(h):
            sl = slice(h * HD, (h + 1) * HD)
            sq[:, sl] = copies["dq"][h % n_sets][...].astype(BF16)
            sk[:, sl] = copies["dk"][h % n_sets][...].astype(BF16)
            sv[:, sl] = copies["dv"][h % n_sets][...].astype(BF16)

        def write_block(blk_idx):
            row0 = pl.multiple_of(blk_idx * rows, rows)
            _write_columns([(sq, CB * (CB_Q + g)), (sk, CB * (CB_K + g)), (sv, CB * (CB_V + g))],
                           out_ref, row0, sems)

        def carried(h, r):
            blk = slice(r * QB, (r + 1) * QB)
            return ((blk, slice(h * HD, (h + 1) * HD)), (blk, slice(GW + h * HD, GW + (h + 1) * HD)),
                    (blk, slice(2 * GW + h * HD, 2 * GW + (h + 1) * HD)))

        if has_prev:
            @pl.when(n == 0)
            def _():
                carry[...] = jnp.zeros_like(carry)

            @pl.when(n == nsb)
            def _():
                for h in range(4):
                    for r in range(dil):
                        cq, ck, cv = carried(h, r)
                        finish(h, r, carry[cq], carry[ck], carry[cv])
                    finish_head(h)
                write_block(b * nsb + nsb - 1)

        @pl.when(n < nsb)
        def _():
            for r in range(dil):
                sts[r * QB:(r + 1) * QB, :] = st_ref[residue(r), :]
            refs_of = {"q": q_ref, "kc": kc_ref, "vc": vc_ref, "do": do_ref, "kp": kp_ref, "vp": vp_ref}
            for chunk in _unit_chunks(dil):
                heads = sorted({h for h, _ in chunk})
                rows_of = {h: {name: _residue_rows(refs_of[name], copies[name], h, residue)
                               for name in refs_of if refs_of[name] is not None}
                           for h in heads}

                def batch(name):
                    return jnp.stack([rows_of[h][name](r) for h, r in chunk])

                q, k, v, do = batch("q"), batch("kc"), batch("vc"), batch("do")
                if has_prev:
                    k = jnp.concatenate([batch("kp"), k], axis=1)
                    v = jnp.concatenate([batch("vp"), v], axis=1)
                    bias_b = jnp.stack([b_ref[h] for h, _ in chunk])
                else:
                    bias_b = jnp.stack([b_ref[h, :, QB:] for h, _ in chunk])
                lse = jnp.stack([sts[r * QB:(r + 1) * QB, h:h + 1] for h, r in chunk])
                delta = jnp.stack([sts[r * QB:(r + 1) * QB, 4 + h:5 + h] for h, r in chunk])
                s = jnp.einsum("uqd,ukd->uqk", q, k, preferred_element_type=F32) * SCALE + bias_b
                p = jnp.exp(s - lse)
                ds = p * (jnp.einsum("uqd,ukd->uqk", do, v, preferred_element_type=F32) - delta)
                for h in heads:
                    mine = [ds[i] for i, (hh, _) in enumerate(chunk) if hh == h]
                    tot = mine[0]
                    for extra in mine[1:]:
                        tot = tot + extra
                    if has_prev:
                        db_ref[h] += tot
                    else:
                        db_ref[h, :, QB:] += tot
                dsb, pb = ds.astype(BF16), p.astype(BF16)
                dq = jnp.einsum("uqk,ukd->uqd", dsb, k, preferred_element_type=F32) * SCALE
                dk = jnp.einsum("uqk,uqd->ukd", dsb, q, preferred_element_type=F32) * SCALE
                dv = jnp.einsum("uqk,uqd->ukd", pb, do, preferred_element_type=F32)
                for i, (h, r) in enumerate(chunk):
                    if has_prev:
                        cq, ck, cv = carried(h, r)
                        finish(h, r, carry[cq], carry[ck] + dk[i, :QB], carry[cv] + dv[i, :QB])
                        carry[cq] = dq[i]
                        carry[ck] = dk[i, QB:]
                        carry[cv] = dv[i, QB:]
                    else:
                        finish(h, r, dq[i], dk[i], dv[i])
                for h in heads:
                    finish_head(h)
            if has_prev:
                @pl.when(n > 0)
                def _():
                    write_block(b * nsb + n - 1)
            else:
                write_block(b)

    def row(b, n):
        return b * nsb + jnp.minimum(n, nsb - 1)

    def prev(b, n):
        return b * nsb + jnp.maximum(jnp.minimum(n, nsb - 1) - 1, 0)

    in_specs = [
        pl.BlockSpec((rows, GW), lambda b, n: (row(b, n), CB_Q + g)),
        pl.BlockSpec((rows, GW), lambda b, n: (row(b, n), CB_K + g)),
        pl.BlockSpec((rows, GW), lambda b, n: (row(b, n), CB_V + g)),
        pl.BlockSpec((rows, GW), lambda b, n: (row(b, n), 0)),
        pl.BlockSpec((rows, 128), lambda b, n: (row(b, n), 0)),
        pl.BlockSpec((None, None, 4, QB, 2 * QB),
                     lambda b, n: (g, jnp.minimum(jnp.minimum(n, nsb - 1), 1), 0, 0, 0)),
    ]
    args = [proj, proj, proj, d_out, stats, bias]
    scratch = [pltpu.VMEM((rows, GW), BF16)] * 3 + [pltpu.SemaphoreType.DMA((3,))]
    if has_prev:
        in_specs += [pl.BlockSpec((rows, GW), lambda b, n: (prev(b, n), CB_K + g)),
                     pl.BlockSpec((rows, GW), lambda b, n: (prev(b, n), CB_V + g))]
        args += [proj, proj]
        scratch.append(pltpu.VMEM((rows, 3 * GW), F32))
    n_copied = (7 + (2 if has_prev else 0)) * n_sets
    scratch += [pltpu.VMEM((rows, 128), F32)] * (1 + n_copied)
    in_specs.append(pl.BlockSpec(memory_space=pl.ANY))
    args.append(dproj)
    return pl.pallas_call(
        body, name=f"attn_bwd{g}",
        grid=(BL, n_steps),
        in_specs=in_specs,
        out_specs=(pl.BlockSpec(memory_space=pl.ANY),
                   pl.BlockSpec((4, QB, 2 * QB), lambda b, n: (0, 0, 0))),
        out_shape=(jax.ShapeDtypeStruct((T, NCOL), BF16),
                   jax.ShapeDtypeStruct((4, QB, 2 * QB), F32)),
        scratch_shapes=scratch,
        input_output_aliases={len(args) - 1: 0},
        compiler_params=pltpu.CompilerParams(vmem_limit_bytes=VMEM_LIMIT),
    )(*args)


def _tail(x2, tgt2, mod3, o_g, lse_g, proj, w_ao, w_co, w_o, conv_w, conv_b, ln_g, ln_b):
    tm = 256
    per_seq = S // tm
    halo = 16

    def body(x_ref, t_ref, mod_ref, o1_ref, o2_ref, o3_ref, l1_ref, l2_ref, l3_ref,
             ga_ref, u_ref, bg_ref, cg_ref, gc_ref, ma_ref, mc_ref, up_ref, cp_ref,
             wao_ref, wco_ref, wo_ref, cw_ref, cb_ref, lg_ref, lb_ref,
             dproj_ref, dyc_ref, do_ref, st_ref, dxd_ref,
             gwo_ref, gwco_ref, gwao_ref, vec_ref,
             dga_s, dbg_s, dgm_s, sems, acc_o, acc_co, acc_ao):
        i = pl.program_id(0)
        bidx = i // per_seq
        first = (i % per_seq) == 0

        @pl.when(i == 0)
        def _():
            vec_ref[...] = jnp.zeros_like(vec_ref)

        l1, l2, l3 = l1_ref[...], l2_ref[...], l3_ref[...]
        mx = jnp.maximum(jnp.maximum(l1, l2), l3)
        e1, e2, e3 = jnp.exp(l1 - mx), jnp.exp(l2 - mx), jnp.exp(l3 - mx)
        esum = e1 + e2 + e3
        lse_tot = mx + jnp.log(esum)
        w1, w2, w3 = e1 / esum, e2 / esum, e3 / esum

        def per_head(wv):
            return jnp.concatenate([jnp.broadcast_to(wv[:, h:h + 1], (tm, HD)) for h in range(4)], axis=1)

        o = per_head(w1) * o1_ref[...] + per_head(w2) * o2_ref[...] + per_head(w3) * o3_ref[...]

        ga = ga_ref[...].astype(F32)
        sig_ga = _sigmoid(ga)
        silu_ga = ga * sig_ga
        a_in = (o * silu_ga).astype(BF16)
        a_out = _dot(a_in, wao_ref[...])

        u = u_ref[...].astype(F32)
        cg = cg_ref[...].astype(F32)
        z = cg * u
        zp = cp_ref[...].astype(F32) * up_ref[...].astype(F32)
        zp = jnp.where(first, 0.0, zp)
        zcat = jnp.concatenate([zp, z], axis=0)
        z1 = pltpu.roll(zcat, 1, 0)[halo:]
        z2 = pltpu.roll(zcat, 2, 0)[halo:]
        y_conv = cw_ref[0:1, :] * z2 + cw_ref[1:2, :] * z1 + cw_ref[2:3, :] * z + cb_ref[...]
        gc = gc_ref[...].astype(F32)
        sig_gc = _sigmoid(gc)
        silu_gc = gc * sig_gc
        bg = bg_ref[...].astype(F32)
        bg_yc = bg * y_conv
        s_in = (bg_yc * silu_gc).astype(BF16)
        s_out = _dot(s_in, wco_ref[...])

        sa = _sigmoid(ma_ref[...].astype(F32))
        sc = _sigmoid(mc_ref[...].astype(F32))
        merged = (sa * a_out + sc * s_out).astype(BF16)
        y = _dot(merged, wo_ref[...])
        gate1 = 1.0 + mod_ref[0, 2:3, :]
        xv = x_ref[...]
        resid = ALPHA * xv + gate1 * y
        mu = jnp.mean(resid, axis=1, keepdims=True)
        xc = resid - mu
        var = jnp.mean(xc * xc, axis=1, keepdims=True)
        rstd = lax.rsqrt(var + LN_EPS)
        xhat = xc * rstd
        lg = lg_ref[...]
        err = xhat * lg + lb_ref[...] - t_ref[...]
        vec_ref[3:4, :] += (0.5 / D) * jnp.sum(err * err, axis=0, keepdims=True)

        vec_ref[1:2, :] += (1.0 / D) * jnp.sum(err * xhat, axis=0, keepdims=True)
        vec_ref[2:3, :] += (1.0 / D) * jnp.sum(err, axis=0, keepdims=True)
        dxh = err * (lg * (1.0 / D))
        dres = rstd * (dxh - jnp.mean(dxh, axis=1, keepdims=True)
                       - xhat * jnp.mean(dxh * xhat, axis=1, keepdims=True))
        dxd_ref[...] = ALPHA * dres
        dgate = jnp.sum(dres * y, axis=0, keepdims=True)
        vec_ref[4:5, :] += jnp.where(bidx == 0, dgate, 0.0)
        vec_ref[5:6, :] += jnp.where(bidx == 1, dgate, 0.0)
        dy = (dres * gate1).astype(BF16)

        dmerged = _dot_nt(dy, wo_ref[...])
        da_out_f = dmerged * sa
        ds_out_f = dmerged * sc
        da_out = da_out_f.astype(BF16)
        ds_out = ds_out_f.astype(BF16)
        dgm_s[:, 2 * D:3 * D] = (ds_out_f * s_out * (1.0 - sc)).astype(BF16)
        dgm_s[:, D:2 * D] = (da_out_f * a_out * (1.0 - sa)).astype(BF16)
        da_in = _dot_nt(da_out, wao_ref[...])
        ds_in = _dot_nt(ds_out, wco_ref[...])

        d_o = da_in * silu_ga
        do_ref[...] = d_o.astype(BF16)
        dga_s[...] = (da_in * o * (sig_ga + silu_ga * (1.0 - sig_ga))).astype(BF16)
        lane = lax.broadcasted_iota(jnp.int32, (tm, 128), 1)
        stats = lse_tot
        od = o * d_o
        for h in range(4):
            delta = jnp.sum(od[:, h * HD:(h + 1) * HD], axis=1, keepdims=True)
            stats = jnp.where(lane == 4 + h, delta, stats)
        st_ref[...] = stats

        ds_silu = ds_in * silu_gc
        dbg_s[...] = (ds_silu * y_conv).astype(BF16)
        dyc = ds_silu * bg
        dyc_ref[...] = dyc
        vec_ref[0:1, :] += jnp.sum(dyc, axis=0, keepdims=True)
        dgm_s[:, 0:D] = (ds_in * bg_yc * (sig_gc + silu_gc * (1.0 - sig_gc))).astype(BF16)

        @pl.when(i == 0)
        def _():
            acc_o[...] = jnp.zeros_like(acc_o)
            acc_co[...] = jnp.zeros_like(acc_co)
            acc_ao[...] = jnp.zeros_like(acc_ao)

        acc_o[...] += _dot_tn(merged, dy)
        acc_co[...] += _dot_tn(s_in, ds_out)
        acc_ao[...] += _dot_tn(a_in, da_out)

        @pl.when(i == T // tm - 1)
        def _():
            gwo_ref[...] = acc_o[...].astype(BF16)
            gwco_ref[...] = acc_co[...].astype(BF16)
            gwao_ref[...] = acc_ao[...].astype(BF16)

        _write_columns([(dga_s, CB * CB_GA), (dbg_s, D * KB_BG), (dgm_s, D * KB_GC)],
                       dproj_ref, pl.multiple_of(i * tm, tm), sems)

    def tile(width, cblk=0):
        return pl.BlockSpec((tm, width), lambda i: (i, cblk))

    def whole(shape):
        return pl.BlockSpec(shape, lambda i: tuple(0 for _ in shape))

    def once(shape):
        return pl.BlockSpec(shape, lambda i: tuple(0 for _ in shape), pipeline_mode=pl.Buffered(1))

    prev_rows = lambda i: (jnp.maximum(i * (tm // halo) - 1, 0),)
    in_specs = [
        tile(D), tile(D), pl.BlockSpec((1, 3, D), lambda i: (i // per_seq, 0, 0)),
        tile(GW), tile(GW), tile(GW), tile(128), tile(128), tile(128),
        tile(GW, CB_GA), tile(D, KB_U), tile(D, KB_BG), tile(D, KB_CG), tile(D, KB_GC),
        tile(D, KB_MA), tile(D, KB_MC),
        pl.BlockSpec((halo, D), lambda i: (*prev_rows(i), KB_U)),
        pl.BlockSpec((halo, D), lambda i: (*prev_rows(i), KB_CG)),
        whole((GW, D)), whole((D, D)), whole((D, D)),
        whole((3, D)), whole((1, D)), whole((1, D)), whole((1, D)),
    ]
    out_specs = (
        pl.BlockSpec(memory_space=pl.ANY), tile(D), tile(GW), tile(128), tile(D),
        once((D, D)), once((D, D)), once((GW, D)),
        pl.BlockSpec((8, D), lambda i: (0, 0)),
    )
    out_shape = (
        jax.ShapeDtypeStruct((T, NCOL), BF16),
        jax.ShapeDtypeStruct((T, D), F32),
        jax.ShapeDtypeStruct((T, GW), BF16),
        jax.ShapeDtypeStruct((T, 128), F32),
        jax.ShapeDtypeStruct((T, D), F32),
        jax.ShapeDtypeStruct((D, D), BF16),
        jax.ShapeDtypeStruct((D, D), BF16),
        jax.ShapeDtypeStruct((GW, D), BF16),
        jax.ShapeDtypeStruct((8, D), F32),
    )
    return pl.pallas_call(
        body, name="tail",
        grid=(T // tm,),
        in_specs=in_specs, out_specs=out_specs, out_shape=out_shape,
        scratch_shapes=[pltpu.VMEM((tm, GW), BF16), pltpu.VMEM((tm, D), BF16), pltpu.VMEM((tm, 3 * D), BF16),
                        pltpu.SemaphoreType.DMA((3,)),
                        pltpu.VMEM((D, D), F32), pltpu.VMEM((D, D), F32), pltpu.VMEM((GW, D), F32)],
        compiler_params=pltpu.CompilerParams(vmem_limit_bytes=VMEM_LIMIT_TAIL),
    )(x2, tgt2, mod3, *o_g, *lse_g, proj, proj, proj, proj, proj, proj, proj, proj, proj,
      w_ao, w_co, w_o, conv_w, conv_b, ln_g, ln_b)


def _conv_bwd(dyc, proj, conv_w, dproj):
    tm = 512
    per_seq = S // tm

    def body(d_ref, dn_ref, u_ref, c_ref, cw_ref, _, dproj_ref, g_ref, du_s, dc_s, sems):
        i = pl.program_id(0)
        last = (i % per_seq) == per_seq - 1

        @pl.when(i == 0)
        def _():
            g_ref[...] = jnp.zeros_like(g_ref)

        d = d_ref[...]
        dn = jnp.where(last, 0.0, dn_ref[...])
        dcat = jnp.concatenate([d, dn], axis=0)
        d1 = pltpu.roll(dcat, tm + 8 - 1, 0)[:tm]
        d2 = pltpu.roll(dcat, tm + 8 - 2, 0)[:tm]
        dz = cw_ref[2:3, :] * d + cw_ref[1:2, :] * d1 + cw_ref[0:1, :] * d2
        u = u_ref[...].astype(F32)
        cg = c_ref[...].astype(F32)
        du_s[...] = (dz * cg).astype(BF16)
        dc_s[...] = (dz * u).astype(BF16)
        _write_columns([(du_s, D * KB_U), (dc_s, D * KB_CG)], dproj_ref, pl.multiple_of(i * tm, tm), sems)

        z = cg * u
        g_ref[0:1, :] += jnp.sum(d2 * z, axis=0, keepdims=True)
        g_ref[1:2, :] += jnp.sum(d1 * z, axis=0, keepdims=True)
        g_ref[2:3, :] += jnp.sum(d * z, axis=0, keepdims=True)

    n_tiles = T // tm
    next_rows = lambda i: jnp.minimum((i + 1) * (tm // 8), T // 8 - 1)
    return pl.pallas_call(
        body, name="conv_bwd",
        grid=(n_tiles,),
        in_specs=[pl.BlockSpec((tm, D), lambda i: (i, 0)),
                  pl.BlockSpec((8, D), lambda i: (next_rows(i), 0)),
                  pl.BlockSpec((tm, D), lambda i: (i, KB_U)),
                  pl.BlockSpec((tm, D), lambda i: (i, KB_CG)),
                  pl.BlockSpec((3, D), lambda i: (0, 0)),
                  pl.BlockSpec(memory_space=pl.ANY)],
        out_specs=(pl.BlockSpec(memory_space=pl.ANY),
                   pl.BlockSpec((8, D), lambda i: (0, 0))),
        out_shape=(jax.ShapeDtypeStruct((T, NCOL), BF16),
                   jax.ShapeDtypeStruct((8, D), F32)),
        scratch_shapes=[pltpu.VMEM((tm, D), BF16), pltpu.VMEM((tm, D), BF16), pltpu.SemaphoreType.DMA((2,))],
        input_output_aliases={5: 0},
        compiler_params=pltpu.CompilerParams(vmem_limit_bytes=VMEM_LIMIT),
    )(dyc, dyc, proj, proj, conv_w, dproj)


def _dh_dx(dproj, w_in_all, x2, dxd, mod3, chip_sums, hops=(), parts0=None):
    tm = 1024
    per_seq = S // tm
    n = len(chip_sums)
    n_in = 5 + n + (0 if parts0 is None else 1)

    def body(*refs):
        d_ref, w_ref, x_ref, dxd_ref, mod_ref = refs[:5]
        ins = refs[5:5 + n]
        gx_ref, vec_ref = refs[n_in:n_in + 2]
        outs = refs[n_in + 2:n_in + 2 + n]
        acc, send_sems, recv_sems, local_sems = refs[n_in + 2 + n:]
        jj, i = pl.program_id(0), pl.program_id(1)

        @pl.when((i == 0) & (jj == 0))
        def _():
            vec_ref[...] = jnp.zeros_like(vec_ref)
            if n:
                sends, _, mine = _chip_copies(ins, outs, send_sems, recv_sems, local_sems, hops)
                for cp in sends + mine:
                    cp.start()

        if n:
            @pl.when((i == T // tm - 1) & (jj == N_DEV - 1))
            def _():
                sends, arrivals, mine = _chip_copies(ins, outs, send_sems, recv_sems, local_sems, hops)
                for cp in arrivals:
                    cp.wait_recv()
                for cp in sends:
                    cp.wait_send()
                for cp in mine:
                    cp.wait()

        def partial():
            return _dot_nt(d_ref[...], w_ref[...])

        @pl.when(jj == 0)
        def _():
            acc[i] = partial()

        @pl.when((jj > 0) & (jj < N_DEV - 1))
        def _():
            acc[i] += partial()

        @pl.when(jj == N_DEV - 1)
        def _():
            dh = acc[i] + partial()
            bidx = i // per_seq
            gx_ref[...] = dxd_ref[...] + dh * (1.0 + mod_ref[0, 1:2, :])
            dshift = jnp.sum(dh, axis=0, keepdims=True)
            dscale = jnp.sum(dh * x_ref[...], axis=0, keepdims=True)
            vec_ref[0:1, :] += jnp.where(bidx == 0, dshift, 0.0)
            vec_ref[1:2, :] += jnp.where(bidx == 1, dshift, 0.0)
            vec_ref[2:3, :] += jnp.where(bidx == 0, dscale, 0.0)
            vec_ref[3:4, :] += jnp.where(bidx == 1, dscale, 0.0)

    def last_pass(jj, i):
        return jnp.where(jj == N_DEV - 1, i, 0)

    any_spec = pl.BlockSpec(memory_space=pl.ANY)
    res = pl.pallas_call(
        body, name="dh_dx",
        grid=(N_DEV, T // tm),
        in_specs=[
            pl.BlockSpec((tm, SHARD), lambda jj, i: (i, jj)),
            pl.BlockSpec((None, D, SHARD), lambda jj, i: (jj, 0, 0)),
            pl.BlockSpec((tm, D), lambda jj, i: (last_pass(jj, i), 0)),
            pl.BlockSpec((tm, D), lambda jj, i: (last_pass(jj, i), 0)),
            pl.BlockSpec((1, 3, D), lambda jj, i: (last_pass(jj, i) // per_seq, 0, 0))]
                 + [any_spec] * (n_in - 5),
        out_specs=(pl.BlockSpec((tm, D), lambda jj, i: (last_pass(jj, i), 0)),
                   pl.BlockSpec((8, D), lambda jj, i: (0, 0))) + (any_spec,) * n,
        out_shape=(jax.ShapeDtypeStruct((T, D), F32), jax.ShapeDtypeStruct((8, D), F32))
                  + tuple(jax.ShapeDtypeStruct(a.shape, a.dtype) for a in chip_sums),
        scratch_shapes=[pltpu.VMEM((T // tm, tm, D), F32), pltpu.SemaphoreType.DMA((max(3 * n, 1),)),
                        pltpu.SemaphoreType.DMA((max(3 * n, 1),)), pltpu.SemaphoreType.DMA((max(n, 1),))],
        input_output_aliases={} if parts0 is None else {5 + n: 2},
        compiler_params=pltpu.CompilerParams(vmem_limit_bytes=VMEM_LIMIT),
    )(dproj, w_in_all, x2, dxd, mod3, *chip_sums, *([] if parts0 is None else [parts0]))
    return res[0], res[1], res[2:]


def _adam_step(g, w, m, v):
    nm = ADAM_B1 * m + (1.0 - ADAM_B1) * g
    nv = ADAM_B2 * v + (1.0 - ADAM_B2) * (g * g)
    m_hat = nm / (1.0 - ADAM_B1 ** ADAM_STEP)
    v_hat = nv / (1.0 - ADAM_B2 ** ADAM_STEP)
    return -ADAM_LR * (m_hat / (jnp.sqrt(v_hat) + ADAM_EPS) + ADAM_WD * w), nm, nv


def _adamw(parts, w, m, v, name, row_tile=None):
    n_parts, rows, cols = parts.shape
    tr = rows if row_tile is None else row_tile

    def body(p_ref, w_ref, m_ref, v_ref, g_ref, d_ref, nm_ref, nv_ref):
        g = p_ref[0].astype(F32)
        for s in range(1, n_parts):
            g = g + p_ref[s].astype(F32)
        g_ref[...] = g
        d_ref[...], nm_ref[...], nv_ref[...] = _adam_step(g, w_ref[...], m_ref[...], v_ref[...])

    blk = pl.BlockSpec((tr, cols), lambda i: (i, 0))
    shp = jax.ShapeDtypeStruct((rows, cols), F32)
    return pl.pallas_call(
        body, name=name,
        grid=(rows // tr,),
        in_specs=[pl.BlockSpec((n_parts, tr, cols), lambda i: (0, i, 0)), blk, blk, blk],
        out_specs=(blk, blk, blk, blk),
        out_shape=(shp, shp, shp, shp),
        compiler_params=pltpu.CompilerParams(vmem_limit_bytes=VMEM_LIMIT),
    )(parts, w, m, v)


def _multi_adamw(parts_list, params, name):
    n = len(params)
    flat = [t for wmv in params for t in wmv]

    def body(*refs):
        parts, ins, outs = refs[:n], refs[n:4 * n], refs[4 * n:]
        for p in range(n):
            g = parts[p][0].astype(F32)
            for s in range(1, parts[p].shape[0]):
                g = g + parts[p][s].astype(F32)
            w_ref, m_ref, v_ref = ins[3 * p:3 * p + 3]
            g_ref, d_ref, nm_ref, nv_ref = outs[4 * p:4 * p + 4]
            g_ref[...] = g
            d_ref[...], nm_ref[...], nv_ref[...] = _adam_step(g, w_ref[...], m_ref[...], v_ref[...])

    out_shape = []
    for w, _, _ in params:
        out_shape += [jax.ShapeDtypeStruct(w.shape, F32)] * 4
    res = pl.pallas_call(body, name=name, out_shape=tuple(out_shape))(*parts_list, *flat)
    return [res[4 * p:4 * p + 4] for p in range(n)]


def _small_updates(small_g, dmod_all, rel_parts, params):
    flat = [t for wmv in params for t in wmv]

    def body(sg_ref, dm_ref, rp_ref, *refs):
        ins, outs = refs[:len(flat)], refs[len(flat):]

        def over_devices(row):
            tot = sg_ref[0, row:row + 1, :]
            for s in range(1, N_DEV):
                tot = tot + sg_ref[s, row:row + 1, :]
            return tot

        g_b_ada = dm_ref[0:1, :]
        for r in range(1, N_DEV * BL):
            g_b_ada = g_b_ada + dm_ref[r:r + 1, :]
        g_rel = rp_ref[0]
        for s in range(1, N_DEV):
            g_rel = g_rel + rp_ref[s]
        grads = [g_b_ada, over_devices(0), g_rel, over_devices(1), over_devices(2)]
        outs[0][...] = jnp.sum(over_devices(3), axis=1, keepdims=True)
        for p, g in enumerate(grads):
            w_ref, m_ref, v_ref = ins[3 * p:3 * p + 3]
            g_ref, d_ref, nm_ref, nv_ref = outs[1 + 4 * p:5 + 4 * p]
            g_ref[...] = g
            d_ref[...], nm_ref[...], nv_ref[...] = _adam_step(g, w_ref[...], m_ref[...], v_ref[...])

    out_shape = [jax.ShapeDtypeStruct((1, 1), F32)]
    for w, _, _ in params:
        out_shape += [jax.ShapeDtypeStruct(w.shape, F32)] * 4
    res = pl.pallas_call(body, name="small_updates", out_shape=tuple(out_shape))(small_g, dmod_all, rel_parts, *flat)
    return res[0], [res[1 + 4 * p:5 + 4 * p] for p in range(len(params))]


def _attn_fwd_dense(proj, bias):
    nq = 4
    rows = nq * QB
    nsb = S // rows

    def body(q_ref, k_ref, v_ref, kp_ref, vp_ref, b_ref, o_ref, l_ref, ls0, ls1, ls2, ls3):
        ls = [ls0, ls1, ls2, ls3]
        n = pl.program_id(1)
        lane = lax.broadcasted_iota(jnp.int32, (QB, 128), 1)
        units = [(h, j) for h in range(4) for j in range(nq)]

        def keys(cur_ref, prev_ref, h, j):
            sl = slice(h * HD, (h + 1) * HD)
            if j == 0:
                return jnp.concatenate([prev_ref[:, sl], cur_ref[0:QB, sl]], axis=0)
            return cur_ref[(j - 1) * QB:(j + 1) * QB, sl]

        q = jnp.stack([q_ref[j * QB:(j + 1) * QB, h * HD:(h + 1) * HD] for h, j in units])
        k = jnp.stack([keys(k_ref, kp_ref, h, j) for h, j in units])
        v = jnp.stack([keys(v_ref, vp_ref, h, j) for h, j in units])
        bias_b = jnp.stack([b_ref[jnp.minimum(n, 1), h] if j == 0 else b_ref[1, h] for h, j in units])
        s = jnp.einsum("uqd,ukd->uqk", q, k, preferred_element_type=F32) * SCALE + bias_b
        m = jnp.max(s, axis=-1, keepdims=True)
        p = jnp.exp(s - m)
        l = jnp.sum(p, axis=-1, keepdims=True)
        o = jnp.einsum("uqk,ukd->uqd", p.astype(BF16), v, preferred_element_type=F32) / l
        lse = m + jnp.log(l)
        for i, (h, j) in enumerate(units):
            o_ref[j * QB:(j + 1) * QB, h * HD:(h + 1) * HD] = o[i]
            ls[h][j * QB:(j + 1) * QB, :] = jnp.where(lane == h, lse[i], 0.0)
        l_ref[...] = (ls[0][...] + ls[1][...]) + (ls[2][...] + ls[3][...])

    def row(b, n):
        return b * nsb + n

    def prev(b, n):
        return jnp.maximum((b * nsb + n) * nq - 1, 0)

    in_specs = [
        pl.BlockSpec((rows, GW), lambda b, n: (row(b, n), CB_Q)),
        pl.BlockSpec((rows, GW), lambda b, n: (row(b, n), CB_K)),
        pl.BlockSpec((rows, GW), lambda b, n: (row(b, n), CB_V)),
        pl.BlockSpec((QB, GW), lambda b, n: (prev(b, n), CB_K)),
        pl.BlockSpec((QB, GW), lambda b, n: (prev(b, n), CB_V)),
        pl.BlockSpec((None, 2, 4, QB, 2 * QB), lambda b, n: (0, 0, 0, 0, 0)),
    ]
    return pl.pallas_call(
        body, name="attn_fwd0",
        grid=(BL, nsb),
        in_specs=in_specs,
        out_specs=(pl.BlockSpec((rows, GW), lambda b, n: (row(b, n), 0)),
                   pl.BlockSpec((rows, 128), lambda b, n: (row(b, n), 0))),
        out_shape=(jax.ShapeDtypeStruct((T, GW), F32), jax.ShapeDtypeStruct((T, 128), F32)),
        scratch_shapes=[pltpu.VMEM((rows, 128), F32)] * 4,
        compiler_params=pltpu.CompilerParams(vmem_limit_bytes=VMEM_LIMIT),
    )(proj, proj, proj, proj, proj, bias)


def _attn_bwd_dense(proj, d_out, stats, bias, dproj):
    nq = 4
    rows = nq * QB
    nsb = S // rows
    cols_q, cols_k, cols_v = CB * CB_Q, CB * CB_K, CB * CB_V

    def body(q_ref, k_ref, v_ref, do_ref, st_ref, kp_ref, vp_ref, b_ref, _, out_ref, db_ref,
             sq, sk, sv, carry, sems):
        b, n = pl.program_id(0), pl.program_id(1)
        units = [(h, j) for h in range(4) for j in range(nq)]

        @pl.when((b == 0) & (n == 0))
        def _():
            db_ref[...] = jnp.zeros_like(db_ref)

        @pl.when(n == 0)
        def _():
            carry[...] = jnp.zeros_like(carry)

        def write(first_block, position, count):
            row0 = pl.multiple_of(first_block * QB, QB)
            part = pl.ds(position * QB, count * QB)
            _write_columns([(sq.at[part], cols_q), (sk.at[part], cols_k), (sv.at[part], cols_v)],
                           out_ref, row0, sems)

        @pl.when(n == nsb)
        def _():
            sq[0:QB, :] = carry[:, 0:GW].astype(BF16)
            sk[0:QB, :] = carry[:, GW:2 * GW].astype(BF16)
            sv[0:QB, :] = carry[:, 2 * GW:3 * GW].astype(BF16)
            write((b + 1) * nsb * nq - 1, 0, 1)

        @pl.when(n < nsb)
        def _():
            def keys(cur_ref, prev_ref, h, j):
                sl = slice(h * HD, (h + 1) * HD)
                if j == 0:
                    return jnp.concatenate([prev_ref[:, sl], cur_ref[0:QB, sl]], axis=0)
                return cur_ref[(j - 1) * QB:(j + 1) * QB, sl]

            def block(ref, h, j):
                return ref[j * QB:(j + 1) * QB, h * HD:(h + 1) * HD]

            q = jnp.stack([block(q_ref, h, j) for h, j in units])
            do = jnp.stack([block(do_ref, h, j) for h, j in units])
            k = jnp.stack([keys(k_ref, kp_ref, h, j) for h, j in units])
            v = jnp.stack([keys(v_ref, vp_ref, h, j) for h, j in units])
            bias_b = jnp.stack([b_ref[jnp.minimum(n, 1), h] if j == 0 else b_ref[1, h] for h, j in units])
            lse = jnp.stack([st_ref[j * QB:(j + 1) * QB, h:h + 1] for h, j in units])
            delta = jnp.stack([st_ref[j * QB:(j + 1) * QB, 4 + h:5 + h] for h, j in units])
            s = jnp.einsum("uqd,ukd->uqk", q, k, preferred_element_type=F32) * SCALE + bias_b
            p = jnp.exp(s - lse)
            ds = p * (jnp.einsum("uqd,ukd->uqk", do, v, preferred_element_type=F32) - delta)
            for h in range(4):
                tot = ds[h * nq]
                for j in range(1, nq):
                    tot = tot + ds[h * nq + j]
                db_ref[h] += tot
            dsb, pb = ds.astype(BF16), p.astype(BF16)
            dq = jnp.einsum("uqk,ukd->uqd", dsb, k, preferred_element_type=F32) * SCALE
            dk = jnp.einsum("uqk,uqd->ukd", dsb, q, preferred_element_type=F32) * SCALE
            dv = jnp.einsum("uqk,uqd->ukd", pb, do, preferred_element_type=F32)
            for h in range(4):
                sl = slice(h * HD, (h + 1) * HD)
                u0, last = h * nq, h * nq + nq - 1
                sq[0:QB, sl] = carry[:, sl].astype(BF16)
                sk[0:QB, sl] = (carry[:, GW + h * HD:GW + (h + 1) * HD] + dk[u0, :QB]).astype(BF16)
                sv[0:QB, sl] = (carry[:, 2 * GW + h * HD:2 * GW + (h + 1) * HD] + dv[u0, :QB]).astype(BF16)
                for j in range(nq - 1):
                    pos = slice((j + 1) * QB, (j + 2) * QB)
                    sq[pos, sl] = dq[u0 + j].astype(BF16)
                    sk[pos, sl] = (dk[u0 + j, QB:] + dk[u0 + j + 1, :QB]).astype(BF16)
                    sv[pos, sl] = (dv[u0 + j, QB:] + dv[u0 + j + 1, :QB]).astype(BF16)
                carry[:, sl] = dq[last]
                carry[:, GW + h * HD:GW + (h + 1) * HD] = dk[last, QB:]
                carry[:, 2 * GW + h * HD:2 * GW + (h + 1) * HD] = dv[last, QB:]

            @pl.when(n == 0)
            def _():
                write(b * nsb * nq, 1, nq - 1)

            @pl.when(n > 0)
            def _():
                write((b * nsb + n) * nq - 1, 0, nq)

    def row(b, n):
        return b * nsb + jnp.minimum(n, nsb - 1)

    def prev(b, n):
        return jnp.maximum(row(b, n) * nq - 1, 0)

    in_specs = [
        pl.BlockSpec((rows, GW), lambda b, n: (row(b, n), CB_Q)),
        pl.BlockSpec((rows, GW), lambda b, n: (row(b, n), CB_K)),
        pl.BlockSpec((rows, GW), lambda b, n: (row(b, n), CB_V)),
        pl.BlockSpec((rows, GW), lambda b, n: (row(b, n), 0)),
        pl.BlockSpec((rows, 128), lambda b, n: (row(b, n), 0)),
        pl.BlockSpec((QB, GW), lambda b, n: (prev(b, n), CB_K)),
        pl.BlockSpec((QB, GW), lambda b, n: (prev(b, n), CB_V)),
        pl.BlockSpec((None, 2, 4, QB, 2 * QB), lambda b, n: (0, 0, 0, 0, 0)),
        pl.BlockSpec(memory_space=pl.ANY),
    ]
    return pl.pallas_call(
        body, name="attn_bwd0",
        grid=(BL, nsb + 1),
        in_specs=in_specs,
        out_specs=(pl.BlockSpec(memory_space=pl.ANY),
                   pl.BlockSpec((4, QB, 2 * QB), lambda b, n: (0, 0, 0))),
        out_shape=(jax.ShapeDtypeStruct((T, NCOL), BF16),
                   jax.ShapeDtypeStruct((4, QB, 2 * QB), F32)),
        scratch_shapes=[pltpu.VMEM((rows, GW), BF16)] * 3
                       + [pltpu.VMEM((QB, 3 * GW), F32), pltpu.SemaphoreType.DMA((3,))],
        input_output_aliases={8: 0},
        compiler_params=pltpu.CompilerParams(vmem_limit_bytes=VMEM_LIMIT),
    )(proj, proj, proj, d_out, stats, proj, proj, bias, dproj)


def _attention_forward(proj, rel_bias):
    expand_lanes, grad_lanes, masks = (jnp.asarray(t) for t in _bucket_maps())
    bias = _bias_expand(rel_bias, expand_lanes, masks)
    fwd = [_attn_fwd_dense(proj, bias)] + [_attn_fwd(proj, bias, g) for g in (1, 2)]
    return bias, grad_lanes, [f[0] for f in fwd], [f[1] for f in fwd]


def _local_step(x2, tgt2, mod3, h, proj, attn, w_ao, w_co, w_o, conv_w, conv_b, ln_g, ln_b):
    bias, buckets, o_g, lse_g = attn

    (dproj, dyc, d_o, stats, dxd, gw_o, gw_co, gw_ao, tail_vec) = _tail(
        x2, tgt2, mod3, o_g, lse_g, proj, w_ao, w_co, w_o, conv_w, conv_b, ln_g, ln_b)

    dproj, db = _attn_bwd_dense(proj, d_o, stats, bias, dproj)
    dbias = [db]
    for g in (1, 2):
        dproj, db = _attn_bwd(proj, d_o, stats, bias, dproj, g)
        dbias.append(db)
    g_rel_bias = _bias_grad(*dbias, buckets)
    dproj, conv_vec = _conv_bwd(dyc, proj, conv_w, dproj)

    gw_ao = jnp.transpose(gw_ao.reshape(GW, N_DEV, D // N_DEV), (1, 0, 2))
    return dproj, dxd, gw_ao, gw_co, gw_o, conv_vec, g_rel_bias, tail_vec


def kernel(x, c, w_ada, b_ada, w_in, conv_w, conv_b, rel_bias, w_attn_out, w_conv_out, w_o, ln_g, ln_b, loss_target, m_w_ada, m_b_ada, m_w_in, m_conv_w, m_conv_b, m_rel_bias, m_w_attn_out, m_w_conv_out, m_w_o, m_ln_g, m_ln_b, v_w_ada, v_b_ada, v_w_in, v_conv_w, v_conv_b, v_rel_bias, v_w_attn_out, v_w_conv_out, v_w_o, v_ln_g, v_ln_b):
    me = _my_index()
    x2 = x.reshape(T, D)
    tgt2 = loss_target.reshape(T, D)

    b_cols = lax.dynamic_slice(b_ada, (0, me * ADA_SHARD), (1, ADA_SHARD))
    c_g, mod_in = _mod_exchange(jnp.pad(c, ((0, 8 - BL), (0, 0))), w_ada[0], b_cols)
    c_all = c_g[:, 0:BL, :].reshape(N_DEV * BL, D)
    mod3 = jnp.transpose(mod_in[:, 0:BL, :], (1, 0, 2)).reshape(BL, 3, D)

    h = _prep_h(x2, mod3)
    rows_shape = jax.ShapeDtypeStruct((N_DEV, D // N_DEV, D), BF16)
    proj, w_in_all, (w_ao_g, w_co_g, w_o_g, conv_w_g) = _gather_proj(
        _shard_order(), h, w_in[0].astype(BF16), 1024,
        ([w_attn_out[0].astype(BF16), w_conv_out[0].astype(BF16), w_o[0].astype(BF16), conv_w[0]],
         [jax.ShapeDtypeStruct((N_DEV, GW, D // N_DEV), BF16), rows_shape, rows_shape,
          jax.ShapeDtypeStruct((N_DEV, 3, D // N_DEV), F32)]))

    attn = _attention_forward(proj, rel_bias)
    w_ao_full = jnp.transpose(w_ao_g, (1, 0, 2)).reshape(GW, D)
    w_co_full = w_co_g.reshape(D, D)
    w_o_full = w_o_g.reshape(D, D)
    conv_w_full = jnp.transpose(conv_w_g, (1, 0, 2)).reshape(3, D)

    (dproj, dxd, gw_ao, gw_co, gw_o, conv_vec, g_rel_bias, tail_vec) = _local_step(
        x2, tgt2, mod3, h, proj, attn, w_ao_full, w_co_full, w_o_full,
        conv_w_full, conv_b, ln_g, ln_b)

    g_conv_w_blocks = jnp.transpose(conv_vec[0:3].reshape(3, N_DEV, D // N_DEV), (1, 0, 2))
    partials = [gw_ao, gw_co.reshape(N_DEV, D // N_DEV, D), gw_o.reshape(N_DEV, D // N_DEV, D), g_conv_w_blocks]
    w_in_sums, w_in_parts, sib = _gw_in_pair(
        _slice_order(), h, dproj, partials,
        [jax.ShapeDtypeStruct((4, GW, D // N_DEV), BF16),
         jax.ShapeDtypeStruct((4, D // N_DEV, D), BF16),
         jax.ShapeDtypeStruct((4, D // N_DEV, D), BF16),
         jax.ShapeDtypeStruct((4, 3, D // N_DEV), F32)])
    core = lax.axis_index("c").astype(jnp.int32).reshape(1)
    chip_sums = [w_in_sums] + list(_pair_add(core, partials, sib))
    hops = [(3,)] + [(1, 2, 3)] * 4
    grad_x, mod_vec, (r_in, r_ao, r_co, r_o, r_cw) = _dh_dx(
        dproj, w_in_all, x2, dxd, mod3, chip_sums, hops, w_in_parts)

    small = jnp.concatenate([
        tail_vec[0:4],
        jnp.pad(g_rel_bias.reshape(1, N_BUCKETS * N_HEADS), ((0, 0), (0, D - N_BUCKETS * N_HEADS))),
        jnp.zeros((3, D), F32)], axis=0)
    dmod = jnp.concatenate([mod_vec[0:2], mod_vec[2:4], tail_vec[4:6]], axis=1)
    small_g, dmod_g = _all_gather(
        [small, dmod],
        [jax.ShapeDtypeStruct((N_DEV, 8, D), F32), jax.ShapeDtypeStruct((N_DEV, BL, 3 * D), F32)],
        "gather_small")
    dmod_all = dmod_g.reshape(N_DEV * BL, 3 * D)
    small_names = ["b_ada", "conv_b", "rel_bias", "ln_g", "ln_b"]
    small_params = [(b_ada, m_b_ada, v_b_ada), (conv_b, m_conv_b, v_conv_b), (rel_bias, m_rel_bias, v_rel_bias),
                    (ln_g, m_ln_g, v_ln_g), (ln_b, m_ln_b, v_ln_b)]
    loss, small_res = _small_updates(
        small_g, dmod_all, small_g[:, 4, :N_BUCKETS * N_HEADS].reshape(N_DEV, N_BUCKETS, N_HEADS), small_params)
    loss = loss.reshape(())

    dmod_cols = lax.dynamic_slice(dmod_all, (0, me * ADA_SHARD), (N_DEV * BL, ADA_SHARD))
    res = {
        "w_ada": tuple(t[None] for t in _w_ada_update(jnp.transpose(c_all), dmod_cols,
                                                      w_ada[0], m_w_ada[0], v_w_ada[0])),
        "w_in": tuple(t[None] for t in _adamw(r_in, w_in[0], m_w_in[0], v_w_in[0], "adam_w_in", 128)),
    }
    mid_names = ["conv_w", "w_attn_out", "w_conv_out", "w_o"]
    mid_parts = [r_cw, r_ao, r_co, r_o]
    mid_full = [(conv_w, m_conv_w, v_conv_w), (w_attn_out, m_w_attn_out, v_w_attn_out),
                (w_conv_out, m_w_conv_out, v_w_conv_out), (w_o, m_w_o, v_w_o)]
    mid_res = _multi_adamw(mid_parts, [tuple(t[0] for t in wmv) for wmv in mid_full], "adam_mid")
    for nm, wmv, outs4 in zip(mid_names, mid_full, mid_res):
        res[nm] = tuple(t[None] for t in outs4)
    res.update(dict(zip(small_names, small_res)))
    order = ["w_ada", "b_ada", "w_in", "conv_w", "conv_b", "rel_bias", "w_attn_out", "w_conv_out",
             "w_o", "ln_g", "ln_b"]
    outs = [loss, grad_x.reshape(BL, S, D)]
    for k in range(4):
        outs += [res[name][k] for name in order]
    return tuple(outs)
```

```python
import math

import numpy as np
import jax
import jax.numpy as jnp
from jax import lax
from jax.experimental import pallas as pl
from jax.experimental.pallas import tpu as pltpu

F32 = jnp.float32
BF16 = jnp.bfloat16
MESH = pl.DeviceIdType.MESH

N_DEV = 8
D = 1024
S = 2048
BL = 2
T = BL * S
NCOL = 11264
SHARD = NCOL // N_DEV
CB = 512
NCB = NCOL // CB
HD = 128
GW = 512
QB = 128
DILATIONS = (1, 4, 16)
N_STEPS = 128
N_BUCKETS = 32
N_HEADS = 12
ALPHA = 2.0 ** 0.25
LN_EPS = 1e-5
NEG_INF = -1e30
SCALE = HD ** -0.5
ADA_SHARD = 3 * D // N_DEV

CB_Q, CB_K, CB_V, CB_GA = 0, 3, 6, 9
KB_U, KB_BG, KB_CG, KB_GC, KB_MA, KB_MC = 5, 6, 7, 8, 9, 10

ADAM_LR, ADAM_B1, ADAM_B2, ADAM_EPS, ADAM_WD, ADAM_STEP = 0.001, 0.9, 0.999, 1e-08, 0.01, 10

VMEM_LIMIT = 56 * 1024 * 1024
VMEM_LIMIT_TAIL = 62 * 1024 * 1024


def _dot(a, b):
    return jnp.dot(a, b, preferred_element_type=F32)


def _dot_nt(a, b):
    return lax.dot_general(a, b, (((1,), (1,)), ((), ())), preferred_element_type=F32)


def _dot_tn(a, b):
    return lax.dot_general(a, b, (((0,), (0,)), ((), ())), preferred_element_type=F32)


def _sigmoid(v):
    return 1.0 / (1.0 + jnp.exp(-v))


def _column_copies(pieces, dst_hbm, row0, sems):
    copies = []
    for k, (src, col0) in enumerate(pieces):
        rows, width = src.shape
        copies.append(pltpu.make_async_copy(
            src, dst_hbm.at[pl.ds(row0, rows), pl.ds(col0, width)], sems.at[k]))
    return copies


def _write_columns(pieces, dst_hbm, row0, sems):
    copies = _column_copies(pieces, dst_hbm, row0, sems)
    for cp in copies:
        cp.start()
    for cp in copies:
        cp.wait()


def _my_index():
    return 4 * lax.axis_index("x") + 2 * lax.axis_index("y") + lax.axis_index("c")


class _Gather:
    def __init__(self, ins, outs, stage, send_sems, recv_sems, local_sems):
        self.ins, self.outs, self.stage = ins, outs, stage
        self.send_sems, self.recv_sems, self.local_sems = send_sems, recv_sems, local_sems
        x, y, c = lax.axis_index("x"), lax.axis_index("y"), lax.axis_index("c")
        self.c = c
        self.me, self.sibling = (x, y, c), (x, y, 1 - c)
        self.chips = [(1 - x, y), (x, 1 - y), (1 - x, 1 - y)]

    @staticmethod
    def scratch(arrs):
        n = len(arrs)
        return ([pltpu.SemaphoreType.DMA((7 * n,)), pltpu.SemaphoreType.DMA((7 * n,)),
                 pltpu.SemaphoreType.DMA((n,))] + [pltpu.VMEM(a.shape, a.dtype) for a in arrs])

    def _copy(self, a, k, block, to, src=None):
        dst = self.outs[a].at[4 * block[0] + 2 * block[1] + block[2]]
        return pltpu.make_async_remote_copy(
            src_ref=dst if src is None else src, dst_ref=dst,
            send_sem=self.send_sems.at[a * 7 + k], recv_sem=self.recv_sems.at[a * 7 + k],
            device_id=to, device_id_type=MESH)

    def _first(self):
        first = []
        for a in range(len(self.ins)):
            first.append(self._copy(a, 0, self.me, self.sibling, src=self.ins[a]))
            first += [self._copy(a, 1 + j, self.me, (*chip, self.c), src=self.ins[a])
                      for j, chip in enumerate(self.chips)]
        return first

    def _mine(self):
        me = self.me
        return [pltpu.make_async_copy(self.stage[a], self.outs[a].at[4 * me[0] + 2 * me[1] + me[2]],
                                      self.local_sems.at[a]) for a in range(len(self.ins))]

    def begin(self):
        for cp in self._first():
            cp.start()
        loads = [pltpu.make_async_copy(self.ins[a], self.stage[a], self.local_sems.at[a])
                 for a in range(len(self.ins))]
        for cp in loads:
            cp.start()
        for cp in loads:
            cp.wait()
        for cp in self._mine():
            cp.start()

    def finish(self):
        n, c, me, sibling = len(self.ins), self.c, self.me, self.sibling
        passed = []
        for j, chip in enumerate(self.chips):
            for a in range(n):
                self._copy(a, 1 + j, (*chip, c), me).wait_recv()
                fwd = self._copy(a, 4 + j, (*chip, c), sibling)
                fwd.start()
                passed.append(fwd)
        for a in range(n):
            self._copy(a, 0, sibling, me).wait_recv()
        for j, chip in enumerate(self.chips):
            for a in range(n):
                self._copy(a, 4 + j, (*chip, 1 - c), me).wait_recv()
        for cp in self._first() + passed:
            cp.wait_send()
        for cp in self._mine():
            cp.wait()


def _all_gather(arrs, out_shapes, name):
    n = len(arrs)

    def body(*refs):
        g = _Gather(refs[:n], refs[n:2 * n], refs[2 * n + 3:], *refs[2 * n:2 * n + 3])
        g.begin()
        g.finish()

    any_spec = pl.BlockSpec(memory_space=pl.ANY)
    return pl.pallas_call(
        body, name=name,
        out_shape=tuple(out_shapes),
        in_specs=[any_spec] * n,
        out_specs=tuple([any_spec] * n),
        scratch_shapes=_Gather.scratch(arrs),
    )(*arrs)


def _neighbour_chips():
    x, y, c = lax.axis_index("x"), lax.axis_index("y"), lax.axis_index("c")
    first = (jnp.where(c == 0, 1 - x, x), jnp.where(c == 0, y, 1 - y))
    second = (jnp.where(c == 0, x, 1 - x), jnp.where(c == 0, 1 - y, y))
    return first, second, (1 - x, 1 - y)


def _slice_order():
    x, y, c = lax.axis_index("x"), lax.axis_index("y"), lax.axis_index("c")
    nb1, nb2, diag = _neighbour_chips()
    slots = []
    for mine, theirs in ((nb1, nb2), (nb2, nb1), (diag, diag), ((x, y), (x, y))):
        slots += [2 * (2 * theirs[0] + theirs[1]) + 1 - c, 2 * (2 * mine[0] + mine[1]) + c]
    return jnp.stack(slots).astype(jnp.int32)


def _gw_in_pair(order, h, dproj, smalls, small_shapes4):
    kk, m = h.shape
    tk = min(kk, 2048)
    nk = kk // tk
    ncols = dproj.shape[1] // N_DEV
    n = len(smalls)

    def body(order_ref, h_ref, d_ref, *rest):
        ins = rest[:n]
        sums_hbm, parts_hbm = rest[n], rest[n + 1]
        sib = rest[n + 2:2 * n + 2]
        (acc, sendbuf, recvbuf, sumbuf, send_sems, recv_sems, local_sem, ssend, srecv,
         isend, irecv) = rest[2 * n + 2:]
        js, k = pl.program_id(0), pl.program_id(1)
        x, y, c = lax.axis_index("x"), lax.axis_index("y"), lax.axis_index("c")
        sibling = (x, y, 1 - c)
        my_chip = 2 * x + y
        nb1, nb2, _ = _neighbour_chips()
        near = [(*nb1, c), (*nb2, c)]

        def ici_copy(p, out_chip):
            peer = near[p] if isinstance(p, int) else tuple(jnp.where(p == 0, a, b) for a, b in zip(*near))
            return pltpu.make_async_remote_copy(
                src_ref=sumbuf.at[p], dst_ref=parts_hbm.at[out_chip],
                send_sem=isend.at[p], recv_sem=irecv.at[p], device_id=peer, device_id_type=MESH)

        def small_copies():
            return [pltpu.make_async_remote_copy(
                        src_ref=ins[a].at[2 * q + 1 - c], dst_ref=sib[a].at[q],
                        send_sem=ssend.at[a * 4 + q], recv_sem=srecv.at[a * 4 + q],
                        device_id=sibling, device_id_type=MESH)
                    for a in range(n) for q in range(4)]

        def slice_copy(p):
            return pltpu.make_async_remote_copy(
                src_ref=sendbuf, dst_ref=recvbuf.at[p], send_sem=send_sems.at[p], recv_sem=recv_sems.at[p],
                device_id=sibling, device_id_type=MESH)

        def sum_copy(p):
            return pltpu.make_async_copy(sumbuf.at[2], sums_hbm.at[order_ref[2 * p] // 2], local_sem)

        @pl.when((js == 0) & (k == 0))
        def _():
            for cp in small_copies():
                cp.start()

        def partial():
            return _dot_tn(h_ref[...], d_ref[...])

        if nk > 1:
            @pl.when(k == 0)
            def _():
                acc[...] = partial()
        if nk > 2:
            @pl.when((k > 0) & (k < nk - 1))
            def _():
                acc[...] += partial()

        def total():
            return partial() + acc[...] if nk > 1 else partial()

        p = js // 2

        @pl.when((js % 2 == 0) & (k == nk - 1))
        def _():
            @pl.when(p > 0)
            def _():
                slice_copy(p - 1).wait_send()
            sendbuf[...] = total().astype(BF16)
            slice_copy(p).start()

        @pl.when((js % 2 == 1) & (k == nk - 1))
        def _():
            slice_copy(p).wait_recv()

            @pl.when(p == 3)
            def _():
                sum_copy(2).wait()
            sumbuf[jnp.minimum(p, 2)] = (total() + recvbuf[p].astype(F32)).astype(BF16)

            @pl.when(p < 2)
            def _():
                ici_copy(p, my_chip).start()

            @pl.when(p >= 2)
            def _():
                sum_copy(p).start()

        @pl.when((js == N_DEV - 1) & (k == nk - 1))
        def _():
            slice_copy(3).wait_send()
            sum_copy(3).wait()
            for cp in small_copies():
                cp.wait()
            for p in range(2):
                ici_copy(p, 2 * near[p][0] + near[p][1]).wait_recv()
                ici_copy(p, my_chip).wait_send()

    any_spec = pl.BlockSpec(memory_space=pl.ANY)
    res = pl.pallas_call(
        body, name="gw_in_pair",
        grid_spec=pltpu.PrefetchScalarGridSpec(
            num_scalar_prefetch=1,
            grid=(N_DEV, nk),
            in_specs=[pl.BlockSpec((tk, m), lambda js, k, order_ref: (k, 0)),
                      pl.BlockSpec((tk, ncols), lambda js, k, order_ref: (k, order_ref[js]))] + [any_spec] * n,
            out_specs=(any_spec,) * (n + 2),
            scratch_shapes=[pltpu.VMEM((m, ncols), F32), pltpu.VMEM((m, ncols), BF16),
                            pltpu.VMEM((4, m, ncols), BF16), pltpu.VMEM((3, m, ncols), BF16),
                            pltpu.SemaphoreType.DMA((4,)), pltpu.SemaphoreType.DMA((4,)),
                            pltpu.SemaphoreType.DMA,
                            pltpu.SemaphoreType.DMA((4 * n,)), pltpu.SemaphoreType.DMA((4 * n,)),
                            pltpu.SemaphoreType.DMA((2,)), pltpu.SemaphoreType.DMA((2,))]),
        out_shape=(jax.ShapeDtypeStruct((4, m, ncols), BF16),) * 2 + tuple(small_shapes4),
        compiler_params=pltpu.CompilerParams(vmem_limit_bytes=VMEM_LIMIT),
    )(order, h, dproj, *smalls)
    return res[0], res[1], res[2:]


def _chip_copies(ins, outs, send_sems, recv_sems, local_sems, hops):
    n = len(ins)
    x, y, c = lax.axis_index("x"), lax.axis_index("y"), lax.axis_index("c")
    my_chip = 2 * x + y

    def peer_of(k):
        return ((1 - x) if (k >> 1) & 1 else x, (1 - y) if k & 1 else y, c)

    def copy(a, k, out_chip):
        peer = peer_of(k)
        return pltpu.make_async_remote_copy(
            src_ref=ins[a].at[2 * peer[0] + peer[1]], dst_ref=outs[a].at[out_chip],
            send_sem=send_sems.at[a * 3 + k - 1], recv_sem=recv_sems.at[a * 3 + k - 1],
            device_id=peer, device_id_type=MESH)

    sends = [copy(a, k, my_chip) for k in range(1, 4) for a in range(n) if k in hops[a]]
    arrivals = []
    for k in range(1, 4):
        peer = peer_of(k)
        arrivals += [copy(a, k, 2 * peer[0] + peer[1]) for a in range(n) if k in hops[a]]
    mine = [pltpu.make_async_copy(ins[a].at[my_chip], outs[a].at[my_chip], local_sems.at[a])
            for a in range(n)]
    return sends, arrivals, mine


def _pair_add(core, mines, theirs):
    n = len(mines)

    def body(core_ref, *refs):
        mine, sib, outs = refs[:n], refs[n:2 * n], refs[2 * n:]
        for a in range(n):
            for q in range(4):
                outs[a][q] = (mine[a][2 * q + core_ref[0]].astype(F32)
                              + sib[a][q].astype(F32)).astype(outs[a].dtype)

    return pl.pallas_call(
        body, name="pair_add",
        in_specs=[pl.BlockSpec(memory_space=pltpu.SMEM)] + [pl.BlockSpec(memory_space=pltpu.VMEM)] * (2 * n),
        out_shape=tuple(jax.ShapeDtypeStruct(t.shape, t.dtype) for t in theirs),
    )(core, *mines, *theirs)


def _mod_exchange(c8, w_ada, b_cols):
    cols = w_ada.shape[1]

    def body(c_ref, w_ref, b_ref, call_ref, mod_ref, msend, send1, recv1, send2, recv2):
        x, y, c = lax.axis_index("x"), lax.axis_index("y"), lax.axis_index("c")
        my_slot = 4 * x + 2 * y + c

        def peer_of(k):
            return ((1 - x) if (k >> 2) & 1 else x, (1 - y) if (k >> 1) & 1 else y, (1 - c) if k & 1 else c)

        def slot_of(dev):
            return 4 * dev[0] + 2 * dev[1] + dev[2]

        def exchange(src_of, dst_ref, send_sems, recv_sems):
            sends, arrivals = [], []
            for k in range(1, 8):
                peer = peer_of(k)
                sends.append(pltpu.make_async_remote_copy(
                    src_ref=src_of(slot_of(peer)), dst_ref=dst_ref.at[my_slot],
                    send_sem=send_sems.at[k - 1], recv_sem=recv_sems.at[k - 1],
                    device_id=peer, device_id_type=MESH))
                arrivals.append(pltpu.make_async_remote_copy(
                    src_ref=src_of(my_slot), dst_ref=dst_ref.at[slot_of(peer)],
                    send_sem=send_sems.at[k - 1], recv_sem=recv_sems.at[k - 1],
                    device_id=peer, device_id_type=MESH))
            for cp in sends:
                cp.start()
            for cp in arrivals:
                cp.wait_recv()
            for cp in sends:
                cp.wait_send()

        call_ref[my_slot] = c_ref[...]
        exchange(lambda s: c_ref, call_ref, send1, recv1)
        cv = call_ref[...].reshape(N_DEV * 8, c_ref.shape[1])
        act = cv * _sigmoid(cv)
        mod = jnp.dot(act, w_ref[...], preferred_element_type=F32,
                      precision=lax.Precision.HIGHEST) + b_ref[...]
        msend[...] = mod.reshape(N_DEV, 8, cols)
        mod_ref[my_slot] = msend[my_slot]
        exchange(lambda s: msend.at[s], mod_ref, send2, recv2)

    return pl.pallas_call(
        body, name="mod_exchange",
        out_shape=(jax.ShapeDtypeStruct((N_DEV, 8, c8.shape[1]), F32),
                   jax.ShapeDtypeStruct((N_DEV, 8, cols), F32)),
        scratch_shapes=[pltpu.VMEM((N_DEV, 8, cols), F32)] + [pltpu.SemaphoreType.DMA((7,))] * 4,
    )(c8, w_ada, b_cols)


def _w_ada_update(c_all_t, dmod_cols, w, m, v):
    rows, cols = w.shape
    tr = 256

    def body(c_ref, d_ref, w_ref, m_ref, v_ref, g_ref, dl_ref, nm_ref, nv_ref):
        cv = c_ref[...]
        g = jnp.dot(cv * _sigmoid(cv), d_ref[...], preferred_element_type=F32,
                    precision=lax.Precision.HIGHEST)
        g_ref[...] = g
        dl_ref[...], nm_ref[...], nv_ref[...] = _adam_step(g, w_ref[...], m_ref[...], v_ref[...])

    blk = pl.BlockSpec((tr, cols), lambda i: (i, 0))
    shp = jax.ShapeDtypeStruct((rows, cols), F32)
    return pl.pallas_call(
        body, name="adam_w_ada",
        grid=(rows // tr,),
        in_specs=[pl.BlockSpec((tr, c_all_t.shape[1]), lambda i: (i, 0)),
                  pl.BlockSpec(dmod_cols.shape, lambda i: (0, 0)), blk, blk, blk],
        out_specs=(blk, blk, blk, blk),
        out_shape=(shp, shp, shp, shp),
    )(c_all_t, dmod_cols, w, m, v)


def _shard_order():
    x, y, c = lax.axis_index("x"), lax.axis_index("y"), lax.axis_index("c")
    first, second, diag = _neighbour_chips()
    devs = [(x, y, c), (x, y, 1 - c), (*first, c), (*second, 1 - c), (*second, c), (*first, 1 - c),
            (*diag, c), (*diag, 1 - c)]
    return jnp.stack([4 * d[0] + 2 * d[1] + d[2] for d in devs]).astype(jnp.int32)


def _prep_h(x2, mod3):
    ts = 512
    per_seq = S // ts

    def body(x_ref, mod_ref, h_ref):
        shift = mod_ref[0, 0:1, :]
        scale = mod_ref[0, 1:2, :]
        h_ref[...] = (x_ref[...] * (1.0 + scale) + shift).astype(BF16)

    return pl.pallas_call(
        body, name="prep_h",
        grid=(T // ts,),
        in_specs=[pl.BlockSpec((ts, D), lambda i: (i, 0)),
                  pl.BlockSpec((1, 3, D), lambda i: (i // per_seq, 0, 0))],
        out_specs=pl.BlockSpec((ts, D), lambda i: (i, 0)),
        out_shape=jax.ShapeDtypeStruct((T, D), BF16),
    )(x2, mod3)


def _gather_proj(order, h, w_shard, tm, ride=()):
    rows, kdim = h.shape
    ncols = w_shard.shape[1]
    n_i = rows // tm
    ride_arrs, ride_shapes = ride if ride else ((), ())
    n_ride = len(ride_arrs)

    def body(order_ref, h_ref, mine_hbm, *rest):
        ride_ins = rest[:n_ride]
        o_ref, all_hbm = rest[n_ride:n_ride + 2]
        ride_outs = rest[n_ride + 2:2 * n_ride + 2]
        wv, send_sems, recv_sems, local_sems = rest[2 * n_ride + 2:2 * n_ride + 6]
        ride_scr = rest[2 * n_ride + 6:]
        j, i = pl.program_id(0), pl.program_id(1)
        c = lax.axis_index("c")
        me, sibling = (lax.axis_index("x"), lax.axis_index("y"), c), (lax.axis_index("x"), lax.axis_index("y"), 1 - c)
        nb1, nb2, diag = _neighbour_chips()

        def slot(dev):
            return 4 * dev[0] + 2 * dev[1] + dev[2]

        def copy(k, block, to, src=None, part=None):
            buf = wv.at[slot(block)]
            if part is not None:
                buf = buf.at[pl.ds(pl.multiple_of(part * (kdim // 2), kdim // 2), kdim // 2)]
            return pltpu.make_async_remote_copy(
                src_ref=buf if src is None else src, dst_ref=buf,
                send_sem=send_sems.at[k], recv_sem=recv_sems.at[k],
                device_id=to, device_id_type=MESH)

        def keep(step, block):
            return pltpu.make_async_copy(wv.at[slot(block)], all_hbm.at[slot(block)], local_sems.at[step])

        if n_ride:
            gather = _Gather(ride_ins, ride_outs, ride_scr[3:], *ride_scr[:3])
        to_sibling, to_nb1, to_nb2 = (copy(0, me, sibling, mine_hbm), copy(1, me, (*nb1, c), mine_hbm),
                                      copy(2, me, (*nb2, c), mine_hbm))
        relay1, relay2 = copy(3, (*nb2, c), (*nb1, c), part=c), copy(4, (*nb1, c), (*nb2, c), part=1 - c)
        pass_nb1, pass_nb2 = copy(5, (*nb1, c), sibling), copy(6, (*nb2, c), sibling)
        pass_d1, pass_d2 = copy(7, (*diag, c), sibling, part=c), copy(8, (*diag, c), sibling, part=1 - c)
        sends = [to_sibling, to_nb1, to_nb2, relay1, relay2, pass_nb1, pass_nb2, pass_d1, pass_d2]
        due = [
            (me, [], []),
            (sibling, [copy(0, sibling, me)], [[]]),
            ((*nb1, c), [copy(1, (*nb1, c), me)], [[pass_nb1, to_nb2]]),
            ((*nb2, 1 - c), [copy(5, (*nb2, 1 - c), me)], [[]]),
            ((*nb2, c), [copy(2, (*nb2, c), me)], [[pass_nb2, relay1, relay2]]),
            ((*nb1, 1 - c), [copy(6, (*nb1, 1 - c), me)], [[]]),
            ((*diag, c), [copy(3, (*diag, c), me, part=c), copy(4, (*diag, c), me, part=1 - c)],
             [[pass_d1], [pass_d2]]),
            ((*diag, 1 - c), [copy(7, (*diag, 1 - c), me, part=1 - c), copy(8, (*diag, 1 - c), me, part=c)],
             [[], []]),
        ]

        @pl.when((j == 0) & (i == 0))
        def _():
            to_sibling.start()
            to_nb1.start()
            load = pltpu.make_async_copy(mine_hbm, wv.at[slot(me)], local_sems.at[N_DEV])
            load.start()
            load.wait()
            keep(0, me).start()

        for step in range(1, N_DEV):
            block, arrivals, then = due[step]

            @pl.when((j == step) & (i == 0))
            def _():
                for arrival, follow in zip(arrivals, then):
                    arrival.wait_recv()
                    for cp in follow:
                        cp.start()
                keep(step, block).start()
                if n_ride and step == N_DEV - 2:
                    gather.begin()

        o_ref[...] = _dot(h_ref[...], wv[order_ref[j]]).astype(BF16)

        @pl.when((j == N_DEV - 1) & (i == n_i - 1))
        def _():
            for cp in sends:
                cp.wait_send()
            for step in range(N_DEV):
                keep(step, due[step][0]).wait()
            if n_ride:
                gather.finish()

    any_spec = pl.BlockSpec(memory_space=pl.ANY)
    res = pl.pallas_call(
        body, name="gather_proj",
        grid_spec=pltpu.PrefetchScalarGridSpec(
            num_scalar_prefetch=1,
            grid=(N_DEV, n_i),
            in_specs=[pl.BlockSpec((tm, kdim), lambda j, i, order_ref: (i, 0)), any_spec] + [any_spec] * n_ride,
            out_specs=(pl.BlockSpec((tm, ncols), lambda j, i, order_ref: (i, order_ref[j])), any_spec)
                      + (any_spec,) * n_ride,
            scratch_shapes=[pltpu.VMEM((N_DEV, kdim, ncols), BF16),
                            pltpu.SemaphoreType.DMA((9,)), pltpu.SemaphoreType.DMA((9,)),
                            pltpu.SemaphoreType.DMA((N_DEV + 1,))]
                           + (_Gather.scratch(ride_arrs) if n_ride else [])),
        out_shape=(jax.ShapeDtypeStruct((rows, N_DEV * ncols), BF16),
                   jax.ShapeDtypeStruct((N_DEV, kdim, ncols), BF16)) + tuple(ride_shapes),
        compiler_params=pltpu.CompilerParams(vmem_limit_bytes=VMEM_LIMIT),
    )(order, h, w_shard, *ride_arrs)
    return res[0], res[1], res[2:]


SKEW_W = 512


def _bucket_maps():
    lanes = np.arange(SKEW_W)

    def buckets_of(steps):
        rows = []
        for dil in DILATIONS:
            dist = np.maximum(steps, 0) * dil
            nf = np.maximum(dist, 1).astype(np.float32)
            large = 16 + (np.log(nf / np.float32(16)) / np.float32(math.log(128.0))
                          * np.float32(16)).astype(np.int32)
            large = np.minimum(large, N_BUCKETS - 1)
            bucket = np.where(dist < 16, dist, large)
            rows.append(np.where((steps >= 0) & (steps <= N_STEPS), bucket, -1).astype(np.int32))
        return np.stack(rows)[:, None, :]

    a = np.arange(QB)[:, None]
    b = np.arange(2 * QB)[None, :]
    steps = a + QB - b
    band = (steps >= 0) & (steps <= N_STEPS)
    first = band & (b >= QB)
    masks = np.stack([first, band]).astype(np.int32)
    return buckets_of(QB - lanes), buckets_of(2 * QB - 1 - lanes), masks


def _bias_expand(rel_bias, lane_buckets, masks):
    def body(tab_ref, bk_ref, mk_ref, o_ref):
        for g in range(3):
            bk = bk_ref[g]
            for h in range(4):
                col = 4 * g + h
                per_offset = jnp.zeros((1, SKEW_W), F32)
                for k in range(N_BUCKETS):
                    per_offset = jnp.where(bk == k, tab_ref[k, col], per_offset)
                tile = pltpu.roll(jnp.broadcast_to(per_offset, (QB, SKEW_W)), 0, 1, stride=1, stride_axis=0)
                tile = tile[:, :2 * QB]
                o_ref[g, 0, h] = jnp.where(mk_ref[0] != 0, tile, NEG_INF)
                o_ref[g, 1, h] = jnp.where(mk_ref[1] != 0, tile, NEG_INF)

    return pl.pallas_call(
        body, name="bias_expand",
        in_specs=[pl.BlockSpec(memory_space=pltpu.SMEM),
                  pl.BlockSpec(memory_space=pltpu.VMEM),
                  pl.BlockSpec(memory_space=pltpu.VMEM)],
        out_shape=jax.ShapeDtypeStruct((3, 2, 4, QB, 2 * QB), F32),
    )(rel_bias, lane_buckets, masks)


def _bias_grad(ds1, ds2, ds3, lane_buckets):
    exchange = jnp.asarray(np.eye(QB, dtype=np.float32)[::-1].copy())

    def body(d1_ref, d2_ref, d3_ref, bk_ref, ex_ref, o_ref):
        for g, d_ref in enumerate((d1_ref, d2_ref, d3_ref)):
            bk = bk_ref[g]
            for h in range(4):
                flipped = jnp.dot(ex_ref[...], d_ref[h], preferred_element_type=F32,
                                  precision=lax.Precision.HIGHEST)
                padded = jnp.concatenate([flipped, jnp.zeros((QB, SKEW_W - 2 * QB), F32)], axis=1)
                skewed = pltpu.roll(padded, 0, 1, stride=1, stride_axis=0)
                per_offset = jnp.sum(skewed, axis=0, keepdims=True)
                for k in range(N_BUCKETS):
                    o_ref[k, 4 * g + h] = jnp.sum(jnp.where(bk == k, per_offset, 0.0))

    return pl.pallas_call(
        body, name="bias_grad",
        in_specs=[pl.BlockSpec(memory_space=pltpu.VMEM)] * 5,
        out_specs=pl.BlockSpec(memory_space=pltpu.SMEM),
        out_shape=jax.ShapeDtypeStruct((N_BUCKETS, N_HEADS), F32),
    )(ds1, ds2, ds3, lane_buckets, exchange)


def _scratch_sets(rows):
    return 4 if rows <= 512 else 1


def _unit_chunks(dil, size=16):
    units = [(h, r) for h in range(4) for r in range(dil)]
    return [units[i:i + size] for i in range(0, len(units), size)]


def _residue_rows(src_ref, copies, h, residue):
    buf = copies[h % len(copies)]
    buf[...] = src_ref[:, h * HD:(h + 1) * HD].astype(F32)
    return lambda r: buf[residue(r), :].astype(BF16)


def _attn_fwd(proj, bias, g):
    dil = DILATIONS[g]
    rows = QB * dil
    nsb = S // rows
    has_prev = nsb > 1

    def residue(r):
        return pl.ds(r, QB, stride=dil)

    n_sets = _scratch_sets(rows)
    n_in = 6 if has_prev else 4
    n_copied = (4 + (2 if has_prev else 0)) * n_sets

    def body(*refs):
        q_ref, kc_ref, vc_ref = refs[:3]
        kp_ref, vp_ref = refs[3:5] if has_prev else (None, None)
        b_ref = refs[n_in - 1]
        o_ref, l_ref = refs[n_in:n_in + 2]
        scr = list(refs[n_in + 2:])
        ls = [scr.pop(0) for _ in range(4)]
        copies = {name: [scr.pop(0) for _ in range(n_sets)]
                  for name in ("q", "kc", "vc", "o") + (("kp", "vp") if has_prev else ())}
        lane = lax.broadcasted_iota(jnp.int32, (QB, 128), 1)
        refs_of = {"q": q_ref, "kc": kc_ref, "vc": vc_ref, "kp": kp_ref, "vp": vp_ref}
        for chunk in _unit_chunks(dil):
            rows_of = {h: {name: _residue_rows(refs_of[name], copies[name], h, residue)
                           for name in refs_of if refs_of[name] is not None}
                       for h in sorted({h for h, _ in chunk})}

            def batch(name):
                return jnp.stack([rows_of[h][name](r) for h, r in chunk])

            q, k, v = batch("q"), batch("kc"), batch("vc")
            if has_prev:
                k = jnp.concatenate([batch("kp"), k], axis=1)
                v = jnp.concatenate([batch("vp"), v], axis=1)
                bias_b = jnp.stack([b_ref[h] for h, _ in chunk])
            else:
                bias_b = jnp.stack([b_ref[h, :, QB:] for h, _ in chunk])
            s = jnp.einsum("uqd,ukd->uqk", q, k, preferred_element_type=F32) * SCALE + bias_b
            m = jnp.max(s, axis=-1, keepdims=True)
            p = jnp.exp(s - m)
            l = jnp.sum(p, axis=-1, keepdims=True)
            o = jnp.einsum("uqk,ukd->uqd", p.astype(BF16), v, preferred_element_type=F32) / l
            lse = m + jnp.log(l)
            for i, (h, r) in enumerate(chunk):
                copies["o"][h % n_sets][residue(r), :] = o[i]
                ls[h][r * QB:(r + 1) * QB, :] = jnp.where(lane == h, lse[i], 0.0)
            for h in sorted({h for h, _ in chunk}):
                o_ref[:, h * HD:(h + 1) * HD] = copies["o"][h % n_sets][...]
        for r in range(dil):
            blk = slice(r * QB, (r + 1) * QB)
            l_ref[residue(r), :] = (ls[0][blk, :] + ls[1][blk, :]) + (ls[2][blk, :] + ls[3][blk, :])

    def row(b, n):
        return b * nsb + n

    def prev(b, n):
        return b * nsb + jnp.maximum(n - 1, 0)

    in_specs = [
        pl.BlockSpec((rows, GW), lambda b, n: (row(b, n), CB_Q + g)),
        pl.BlockSpec((rows, GW), lambda b, n: (row(b, n), CB_K + g)),
        pl.BlockSpec((rows, GW), lambda b, n: (row(b, n), CB_V + g)),
    ]
    args = [proj, proj, proj]
    scratch = [pltpu.VMEM((rows, 128), F32)] * (4 + n_copied)
    if has_prev:
        in_specs += [pl.BlockSpec((rows, GW), lambda b, n: (prev(b, n), CB_K + g)),
                     pl.BlockSpec((rows, GW), lambda b, n: (prev(b, n), CB_V + g))]
        args += [proj, proj]
    in_specs.append(pl.BlockSpec((None, None, 4, QB, 2 * QB),
                                 lambda b, n: (g, jnp.minimum(n, 1), 0, 0, 0)))
    args.append(bias)
    return pl.pallas_call(
        body, name=f"attn_fwd{g}",
        grid=(BL, nsb),
        in_specs=in_specs,
        out_specs=(pl.BlockSpec((rows, GW), lambda b, n: (row(b, n), 0)),
                   pl.BlockSpec((rows, 128), lambda b, n: (row(b, n), 0))),
        out_shape=(jax.ShapeDtypeStruct((T, GW), F32), jax.ShapeDtypeStruct((T, 128), F32)),
        scratch_shapes=scratch,
        compiler_params=pltpu.CompilerParams(vmem_limit_bytes=VMEM_LIMIT),
    )(*args)


def _attn_bwd(proj, d_out, stats, bias, dproj, g):
    dil = DILATIONS[g]
    rows = QB * dil
    nsb = S // rows
    has_prev = nsb > 1
    n_steps = nsb + 1 if has_prev else 1
    n_in = 7 + (2 if has_prev else 0)

    def residue(r):
        return pl.ds(r, QB, stride=dil)

    n_sets = _scratch_sets(rows)

    def body(*refs):
        q_ref, kc_ref, vc_ref, do_ref, st_ref, b_ref = refs[:6]
        kp_ref, vp_ref = refs[6:8] if has_prev else (None, None)
        out_ref, db_ref = refs[n_in], refs[n_in + 1]
        scr = list(refs[n_in + 2:])
        sq, sk, sv, sems = [scr.pop(0) for _ in range(4)]
        carry = scr.pop(0) if has_prev else None
        sts = scr.pop(0)
        copies = {name: [scr.pop(0) for _ in range(n_sets)]
                  for name in ("q", "kc", "vc", "do", "dq", "dk", "dv") + (("kp", "vp") if has_prev else ())}
        b, n = pl.program_id(0), pl.program_id(1)

        @pl.when((b == 0) & (n == 0))
        def _():
            db_ref[...] = jnp.zeros_like(db_ref)

        def finish(h, r, dq, dk, dv):
            for name, val in (("dq", dq), ("dk", dk), ("dv", dv)):
                copies[name][h % n_sets][residue(r), :] = val

        def finish_head(h):
            sl = slice(h * HD, (h + 1) * HD)
            sq[:, sl] = copies["dq"][h % n_sets][...].astype(BF16)
            sk[:, sl] = copies["dk"][h % n_sets][...].astype(BF16)
            sv[:, sl] = copies["dv"][h % n_sets][...].astype(BF16)

        def write_block(blk_idx):
            row0 = pl.multiple_of(blk_idx * rows, rows)
            _write_columns([(sq, CB * (CB_Q + g)), (sk, CB * (CB_K + g)), (sv, CB * (CB_V + g))],
                           out_ref, row0, sems)

        def carried(h, r):
            blk = slice(r * QB, (r + 1) * QB)
            return ((blk, slice(h * HD, (h + 1) * HD)), (blk, slice(GW + h * HD, GW + (h + 1) * HD)),
                    (blk, slice(2 * GW + h * HD, 2 * GW + (h + 1) * HD)))

        if has_prev:
            @pl.when(n == 0)
            def _():
                carry[...] = jnp.zeros_like(carry)

            @pl.when(n == nsb)
            def _():
                for h in range(4):
                    for r in range(dil):
                        cq, ck, cv = carried(h, r)
                        finish(h, r, carry[cq], carry[ck], carry[cv])
                    finish_head(h)
                write_block(b * nsb + nsb - 1)

        @pl.when(n < nsb)
        def _():
            for r in range(dil):
                sts[r * QB:(r + 1) * QB, :] = st_ref[residue(r), :]
            refs_of = {"q": q_ref, "kc": kc_ref, "vc": vc_ref, "do": do_ref, "kp": kp_ref, "vp": vp_ref}
            for chunk in _unit_chunks(dil):
                heads = sorted({h for h, _ in chunk})
                rows_of = {h: {name: _residue_rows(refs_of[name], copies[name], h, residue)
                               for name in refs_of if refs_of[name] is not None}
                           for h in heads}

                def batch(name):
                    return jnp.stack([rows_of[h][name](r) for h, r in chunk])

                q, k, v, do = batch("q"), batch("kc"), batch("vc"), batch("do")
                if has_prev:
                    k = jnp.concatenate([batch("kp"), k], axis=1)
                    v = jnp.concatenate([batch("vp"), v], axis=1)
                    bias_b = jnp.stack([b_ref[h] for h, _ in chunk])
                else:
                    bias_b = jnp.stack([b_ref[h, :, QB:] for h, _ in chunk])
                lse = jnp.stack([sts[r * QB:(r + 1) * QB, h:h + 1] for h, r in chunk])
                delta = jnp.stack([sts[r * QB:(r + 1) * QB, 4 + h:5 + h] for h, r in chunk])
                s = jnp.einsum("uqd,ukd->uqk", q, k, preferred_element_type=F32) * SCALE + bias_b
                p = jnp.exp(s - lse)
                ds = p * (jnp.einsum("uqd,ukd->uqk", do, v, preferred_element_type=F32) - delta)
                for h in heads:
                    mine = [ds[i] for i, (hh, _) in enumerate(chunk) if hh == h]
                    tot = mine[0]
                    for extra in mine[1:]:
                        tot = tot + extra
                    if has_prev:
                        db_ref[h] += tot
                    else:
                        db_ref[h, :, QB:] += tot
                dsb, pb = ds.astype(BF16), p.astype(BF16)
                dq = jnp.einsum("uqk,ukd->uqd", dsb, k, preferred_element_type=F32) * SCALE
                dk = jnp.einsum("uqk,uqd->ukd", dsb, q, preferred_element_type=F32) * SCALE
                dv = jnp.einsum("uqk,uqd->ukd", pb, do, preferred_element_type=F32)
                for i, (h, r) in enumerate(chunk):
                    if has_prev:
                        cq, ck, cv = carried(h, r)
                        finish(h, r, carry[cq], carry[ck] + dk[i, :QB], carry[cv] + dv[i, :QB])
                        carry[cq] = dq[i]
                        carry[ck] = dk[i, QB:]
                        carry[cv] = dv[i, QB:]
                    else:
                        finish(h, r, dq[i], dk[i], dv[i])
                for h in heads:
                    finish_head(h)
            if has_prev:
                @pl.when(n > 0)
                def _():
                    write_block(b * nsb + n - 1)
            else:
                write_block(b)

    def row(b, n):
        return b * nsb + jnp.minimum(n, nsb - 1)

    def prev(b, n):
        return b * nsb + jnp.maximum(jnp.minimum(n, nsb - 1) - 1, 0)

    in_specs = [
        pl.BlockSpec((rows, GW), lambda b, n: (row(b, n), CB_Q + g)),
        pl.BlockSpec((rows, GW), lambda b, n: (row(b, n), CB_K + g)),
        pl.BlockSpec((rows, GW), lambda b, n: (row(b, n), CB_V + g)),
        pl.BlockSpec((rows, GW), lambda b, n: (row(b, n), 0)),
        pl.BlockSpec((rows, 128), lambda b, n: (row(b, n), 0)),
        pl.BlockSpec((None, None, 4, QB, 2 * QB),
                     lambda b, n: (g, jnp.minimum(jnp.minimum(n, nsb - 1), 1), 0, 0, 0)),
    ]
    args = [proj, proj, proj, d_out, stats, bias]
    scratch = [pltpu.VMEM((rows, GW), BF16)] * 3 + [pltpu.SemaphoreType.DMA((3,))]
    if has_prev:
        in_specs += [pl.BlockSpec((rows, GW), lambda b, n: (prev(b, n), CB_K + g)),
                     pl.BlockSpec((rows, GW), lambda b, n: (prev(b, n), CB_V + g))]
        args += [proj, proj]
        scratch.append(pltpu.VMEM((rows, 3 * GW), F32))
    n_copied = (7 + (2 if has_prev else 0)) * n_sets
    scratch += [pltpu.VMEM((rows, 128), F32)] * (1 + n_copied)
    in_specs.append(pl.BlockSpec(memory_space=pl.ANY))
    args.append(dproj)
    return pl.pallas_call(
        body, name=f"attn_bwd{g}",
        grid=(BL, n_steps),
        in_specs=in_specs,
        out_specs=(pl.BlockSpec(memory_space=pl.ANY),
                   pl.BlockSpec((4, QB, 2 * QB), lambda b, n: (0, 0, 0))),
        out_shape=(jax.ShapeDtypeStruct((T, NCOL), BF16),
                   jax.ShapeDtypeStruct((4, QB, 2 * QB), F32)),
        scratch_shapes=scratch,
        input_output_aliases={len(args) - 1: 0},
        compiler_params=pltpu.CompilerParams(vmem_limit_bytes=VMEM_LIMIT),
    )(*args)


def _tail(x2, tgt2, mod3, o_g, lse_g, proj, w_ao, w_co, w_o, conv_w, conv_b, ln_g, ln_b):
    tm = 256
    per_seq = S // tm
    halo = 16

    def body(x_ref, t_ref, mod_ref, o1_ref, o2_ref, o3_ref, l1_ref, l2_ref, l3_ref,
             ga_ref, u_ref, bg_ref, cg_ref, gc_ref, ma_ref, mc_ref, up_ref, cp_ref,
             wao_ref, wco_ref, wo_ref, cw_ref, cb_ref, lg_ref, lb_ref,
             dproj_ref, dyc_ref, do_ref, st_ref, dxd_ref,
             gwo_ref, gwco_ref, gwao_ref, vec_ref,
             dga_s, dbg_s, dgm_s, sems, acc_o, acc_co, acc_ao):
        i = pl.program_id(0)
        bidx = i // per_seq
        first = (i % per_seq) == 0

        @pl.when(i == 0)
        def _():
            vec_ref[...] = jnp.zeros_like(vec_ref)

        slot = i % 2

        def column_copies(s, row0):
            return _column_copies([(dga_s.at[s], CB * CB_GA), (dbg_s.at[s], D * KB_BG), (dgm_s.at[s], D * KB_GC)],
                                  dproj_ref, row0, sems.at[s])

        @pl.when(i >= 2)
        def _():
            for cp in column_copies(slot, 0):
                cp.wait()

        l1, l2, l3 = l1_ref[...], l2_ref[...], l3_ref[...]
        mx = jnp.maximum(jnp.maximum(l1, l2), l3)
        e1, e2, e3 = jnp.exp(l1 - mx), jnp.exp(l2 - mx), jnp.exp(l3 - mx)
        esum = e1 + e2 + e3
        lse_tot = mx + jnp.log(esum)
        w1, w2, w3 = e1 / esum, e2 / esum, e3 / esum

        def per_head(wv):
            return jnp.concatenate([jnp.broadcast_to(wv[:, h:h + 1], (tm, HD)) for h in range(4)], axis=1)

        o = per_head(w1) * o1_ref[...] + per_head(w2) * o2_ref[...] + per_head(w3) * o3_ref[...]

        ga = ga_ref[...].astype(F32)
        sig_ga = _sigmoid(ga)
        silu_ga = ga * sig_ga
        a_in = (o * silu_ga).astype(BF16)
        a_out = _dot(a_in, wao_ref[...])

        u = u_ref[...].astype(F32)
        cg = cg_ref[...].astype(F32)
        z = cg * u
        zp = cp_ref[...].astype(F32) * up_ref[...].astype(F32)
        zp = jnp.where(first, 0.0, zp)
        zcat = jnp.concatenate([zp, z], axis=0)
        z1 = pltpu.roll(zcat, 1, 0)[halo:]
        z2 = pltpu.roll(zcat, 2, 0)[halo:]
        y_conv = cw_ref[0:1, :] * z2 + cw_ref[1:2, :] * z1 + cw_ref[2:3, :] * z + cb_ref[...]
        gc = gc_ref[...].astype(F32)
        sig_gc = _sigmoid(gc)
        silu_gc = gc * sig_gc
        bg = bg_ref[...].astype(F32)
        bg_yc = bg * y_conv
        s_in = (bg_yc * silu_gc).astype(BF16)
        s_out = _dot(s_in, wco_ref[...])

        sa = _sigmoid(ma_ref[...].astype(F32))
        sc = _sigmoid(mc_ref[...].astype(F32))
        merged = (sa * a_out + sc * s_out).astype(BF16)
        y = _dot(merged, wo_ref[...])
        gate1 = 1.0 + mod_ref[0, 2:3, :]
        xv = x_ref[...]
        resid = ALPHA * xv + gate1 * y
        mu = jnp.mean(resid, axis=1, keepdims=True)
        xc = resid - mu
        var = jnp.mean(xc * xc, axis=1, keepdims=True)
        rstd = lax.rsqrt(var + LN_EPS)
        xhat = xc * rstd
        lg = lg_ref[...]
        err = xhat * lg + lb_ref[...] - t_ref[...]
        vec_ref[3:4, :] += (0.5 / D) * jnp.sum(err * err, axis=0, keepdims=True)

        vec_ref[1:2, :] += (1.0 / D) * jnp.sum(err * xhat, axis=0, keepdims=True)
        vec_ref[2:3, :] += (1.0 / D) * jnp.sum(err, axis=0, keepdims=True)
        dxh = err * (lg * (1.0 / D))
        dres = rstd * (dxh - jnp.mean(dxh, axis=1, keepdims=True)
                       - xhat * jnp.mean(dxh * xhat, axis=1, keepdims=True))
        dxd_ref[...] = ALPHA * dres
        dgate = jnp.sum(dres * y, axis=0, keepdims=True)
        vec_ref[4:5, :] += jnp.where(bidx == 0, dgate, 0.0)
        vec_ref[5:6, :] += jnp.where(bidx == 1, dgate, 0.0)
        dy = (dres * gate1).astype(BF16)

        dmerged = _dot_nt(dy, wo_ref[...])
        da_out_f = dmerged * sa
        ds_out_f = dmerged * sc
        da_out = da_out_f.astype(BF16)
        ds_out = ds_out_f.astype(BF16)
        dgm_s[slot, :, 2 * D:3 * D] = (ds_out_f * s_out * (1.0 - sc)).astype(BF16)
        dgm_s[slot, :, D:2 * D] = (da_out_f * a_out * (1.0 - sa)).astype(BF16)
        da_in = _dot_nt(da_out, wao_ref[...])
        ds_in = _dot_nt(ds_out, wco_ref[...])

        d_o = da_in * silu_ga
        do_ref[...] = d_o.astype(BF16)
        dga_s[slot] = (da_in * o * (sig_ga + silu_ga * (1.0 - sig_ga))).astype(BF16)
        lane = lax.broadcasted_iota(jnp.int32, (tm, 128), 1)
        stats = lse_tot
        od = o * d_o
        for h in range(4):
            delta = jnp.sum(od[:, h * HD:(h + 1) * HD], axis=1, keepdims=True)
            stats = jnp.where(lane == 4 + h, delta, stats)
        st_ref[...] = stats

        ds_silu = ds_in * silu_gc
        dbg_s[slot] = (ds_silu * y_conv).astype(BF16)
        dyc = ds_silu * bg
        dyc_ref[...] = dyc
        vec_ref[0:1, :] += jnp.sum(dyc, axis=0, keepdims=True)
        dgm_s[slot, :, 0:D] = (ds_in * bg_yc * (sig_gc + silu_gc * (1.0 - sig_gc))).astype(BF16)

        @pl.when(i == 0)
        def _():
            acc_o[...] = jnp.zeros_like(acc_o)
            acc_co[...] = jnp.zeros_like(acc_co)
            acc_ao[...] = jnp.zeros_like(acc_ao)

        acc_o[...] += _dot_tn(merged, dy)
        acc_co[...] += _dot_tn(s_in, ds_out)
        acc_ao[...] += _dot_tn(a_in, da_out)

        for cp in column_copies(slot, pl.multiple_of(i * tm, tm)):
            cp.start()

        @pl.when(i == T // tm - 1)
        def _():
            gwo_ref[...] = acc_o[...].astype(BF16)
            gwco_ref[...] = acc_co[...].astype(BF16)
            gwao_ref[...] = acc_ao[...].astype(BF16)
            for s in range(2):
                for cp in column_copies(s, 0):
                    cp.wait()

    def tile(width, cblk=0):
        return pl.BlockSpec((tm, width), lambda i: (i, cblk))

    def whole(shape):
        return pl.BlockSpec(shape, lambda i: tuple(0 for _ in shape))

    def once(shape):
        return pl.BlockSpec(shape, lambda i: tuple(0 for _ in shape), pipeline_mode=pl.Buffered(1))

    prev_rows = lambda i: (jnp.maximum(i * (tm // halo) - 1, 0),)
    in_specs = [
        tile(D), tile(D), pl.BlockSpec((1, 3, D), lambda i: (i // per_seq, 0, 0)),
        tile(GW), tile(GW), tile(GW), tile(128), tile(128), tile(128),
        tile(GW, CB_GA), tile(D, KB_U), tile(D, KB_BG), tile(D, KB_CG), tile(D, KB_GC),
        tile(D, KB_MA), tile(D, KB_MC),
        pl.BlockSpec((halo, D), lambda i: (*prev_rows(i), KB_U)),
        pl.BlockSpec((halo, D), lambda i: (*prev_rows(i), KB_CG)),
        whole((GW, D)), whole((D, D)), whole((D, D)),
        whole((3, D)), whole((1, D)), whole((1, D)), whole((1, D)),
    ]
    out_specs = (
        pl.BlockSpec(memory_space=pl.ANY), tile(D), tile(GW), tile(128), tile(D),
        once((D, D)), once((D, D)), once((GW, D)),
        pl.BlockSpec((8, D), lambda i: (0, 0)),
    )
    out_shape = (
        jax.ShapeDtypeStruct((T, NCOL), BF16),
        jax.ShapeDtypeStruct((T, D), F32),
        jax.ShapeDtypeStruct((T, GW), BF16),
        jax.ShapeDtypeStruct((T, 128), F32),
        jax.ShapeDtypeStruct((T, D), F32),
        jax.ShapeDtypeStruct((D, D), BF16),
        jax.ShapeDtypeStruct((D, D), BF16),
        jax.ShapeDtypeStruct((GW, D), BF16),
        jax.ShapeDtypeStruct((8, D), F32),
    )
    return pl.pallas_call(
        body, name="tail",
        grid=(T // tm,),
        in_specs=in_specs, out_specs=out_specs, out_shape=out_shape,
        scratch_shapes=[pltpu.VMEM((2, tm, GW), BF16), pltpu.VMEM((2, tm, D), BF16), pltpu.VMEM((2, tm, 3 * D), BF16),
                        pltpu.SemaphoreType.DMA((2, 3)),
                        pltpu.VMEM((D, D), F32), pltpu.VMEM((D, D), F32), pltpu.VMEM((GW, D), F32)],
        compiler_params=pltpu.CompilerParams(vmem_limit_bytes=VMEM_LIMIT_TAIL),
    )(x2, tgt2, mod3, *o_g, *lse_g, proj, proj, proj, proj, proj, proj, proj, proj, proj,
      w_ao, w_co, w_o, conv_w, conv_b, ln_g, ln_b)


def _conv_bwd(dyc, proj, conv_w, dproj):
    tm = 512
    per_seq = S // tm

    def body(d_ref, dn_ref, u_ref, c_ref, cw_ref, _, dproj_ref, g_ref, du_s, dc_s, sems):
        i = pl.program_id(0)
        last = (i % per_seq) == per_seq - 1

        @pl.when(i == 0)
        def _():
            g_ref[...] = jnp.zeros_like(g_ref)

        d = d_ref[...]
        dn = jnp.where(last, 0.0, dn_ref[...])
        dcat = jnp.concatenate([d, dn], axis=0)
        d1 = pltpu.roll(dcat, tm + 8 - 1, 0)[:tm]
        d2 = pltpu.roll(dcat, tm + 8 - 2, 0)[:tm]
        dz = cw_ref[2:3, :] * d + cw_ref[1:2, :] * d1 + cw_ref[0:1, :] * d2
        u = u_ref[...].astype(F32)
        cg = c_ref[...].astype(F32)
        du_s[...] = (dz * cg).astype(BF16)
        dc_s[...] = (dz * u).astype(BF16)
        _write_columns([(du_s, D * KB_U), (dc_s, D * KB_CG)], dproj_ref, pl.multiple_of(i * tm, tm), sems)

        z = cg * u
        g_ref[0:1, :] += jnp.sum(d2 * z, axis=0, keepdims=True)
        g_ref[1:2, :] += jnp.sum(d1 * z, axis=0, keepdims=True)
        g_ref[2:3, :] += jnp.sum(d * z, axis=0, keepdims=True)

    n_tiles = T // tm
    next_rows = lambda i: jnp.minimum((i + 1) * (tm // 8), T // 8 - 1)
    return pl.pallas_call(
        body, name="conv_bwd",
        grid=(n_tiles,),
        in_specs=[pl.BlockSpec((tm, D), lambda i: (i, 0)),
                  pl.BlockSpec((8, D), lambda i: (next_rows(i), 0)),
                  pl.BlockSpec((tm, D), lambda i: (i, KB_U)),
                  pl.BlockSpec((tm, D), lambda i: (i, KB_CG)),
                  pl.BlockSpec((3, D), lambda i: (0, 0)),
                  pl.BlockSpec(memory_space=pl.ANY)],
        out_specs=(pl.BlockSpec(memory_space=pl.ANY),
                   pl.BlockSpec((8, D), lambda i: (0, 0))),
        out_shape=(jax.ShapeDtypeStruct((T, NCOL), BF16),
                   jax.ShapeDtypeStruct((8, D), F32)),
        scratch_shapes=[pltpu.VMEM((tm, D), BF16), pltpu.VMEM((tm, D), BF16), pltpu.SemaphoreType.DMA((2,))],
        input_output_aliases={5: 0},
        compiler_params=pltpu.CompilerParams(vmem_limit_bytes=VMEM_LIMIT),
    )(dyc, dyc, proj, proj, conv_w, dproj)


def _dh_dx(dproj, w_in_all, x2, dxd, mod3, chip_sums, hops=(), parts0=None):
    tm = 1024
    per_seq = S // tm
    n = len(chip_sums)
    n_in = 5 + n + (0 if parts0 is None else 1)

    def body(*refs):
        d_ref, w_ref, x_ref, dxd_ref, mod_ref = refs[:5]
        ins = refs[5:5 + n]
        gx_ref, vec_ref = refs[n_in:n_in + 2]
        outs = refs[n_in + 2:n_in + 2 + n]
        acc, send_sems, recv_sems, local_sems = refs[n_in + 2 + n:]
        jj, i = pl.program_id(0), pl.program_id(1)

        @pl.when((i == 0) & (jj == 0))
        def _():
            vec_ref[...] = jnp.zeros_like(vec_ref)
            if n:
                sends, _, mine = _chip_copies(ins, outs, send_sems, recv_sems, local_sems, hops)
                for cp in sends + mine:
                    cp.start()

        if n:
            @pl.when((i == T // tm - 1) & (jj == N_DEV - 1))
            def _():
                sends, arrivals, mine = _chip_copies(ins, outs, send_sems, recv_sems, local_sems, hops)
                for cp in arrivals:
                    cp.wait_recv()
                for cp in sends:
                    cp.wait_send()
                for cp in mine:
                    cp.wait()

        def partial():
            return _dot_nt(d_ref[...], w_ref[...])

        @pl.when(jj == 0)
        def _():
            acc[i] = partial()

        @pl.when((jj > 0) & (jj < N_DEV - 1))
        def _():
            acc[i] += partial()

        @pl.when(jj == N_DEV - 1)
        def _():
            dh = acc[i] + partial()
            bidx = i // per_seq
            gx_ref[...] = dxd_ref[...] + dh * (1.0 + mod_ref[0, 1:2, :])
            dshift = jnp.sum(dh, axis=0, keepdims=True)
            dscale = jnp.sum(dh * x_ref[...], axis=0, keepdims=True)
            vec_ref[0:1, :] += jnp.where(bidx == 0, dshift, 0.0)
            vec_ref[1:2, :] += jnp.where(bidx == 1, dshift, 0.0)
            vec_ref[2:3, :] += jnp.where(bidx == 0, dscale, 0.0)
            vec_ref[3:4, :] += jnp.where(bidx == 1, dscale, 0.0)

    def last_pass(jj, i):
        return jnp.where(jj == N_DEV - 1, i, 0)

    any_spec = pl.BlockSpec(memory_space=pl.ANY)
    res = pl.pallas_call(
        body, name="dh_dx",
        grid=(N_DEV, T // tm),
        in_specs=[
            pl.BlockSpec((tm, SHARD), lambda jj, i: (i, jj)),
            pl.BlockSpec((None, D, SHARD), lambda jj, i: (jj, 0, 0)),
            pl.BlockSpec((tm, D), lambda jj, i: (last_pass(jj, i), 0)),
            pl.BlockSpec((tm, D), lambda jj, i: (last_pass(jj, i), 0)),
            pl.BlockSpec((1, 3, D), lambda jj, i: (last_pass(jj, i) // per_seq, 0, 0))]
                 + [any_spec] * (n_in - 5),
        out_specs=(pl.BlockSpec((tm, D), lambda jj, i: (last_pass(jj, i), 0)),
                   pl.BlockSpec((8, D), lambda jj, i: (0, 0))) + (any_spec,) * n,
        out_shape=(jax.ShapeDtypeStruct((T, D), F32), jax.ShapeDtypeStruct((8, D), F32))
                  + tuple(jax.ShapeDtypeStruct(a.shape, a.dtype) for a in chip_sums),
        scratch_shapes=[pltpu.VMEM((T // tm, tm, D), F32), pltpu.SemaphoreType.DMA((max(3 * n, 1),)),
                        pltpu.SemaphoreType.DMA((max(3 * n, 1),)), pltpu.SemaphoreType.DMA((max(n, 1),))],
        input_output_aliases={} if parts0 is None else {5 + n: 2},
        compiler_params=pltpu.CompilerParams(vmem_limit_bytes=VMEM_LIMIT),
    )(dproj, w_in_all, x2, dxd, mod3, *chip_sums, *([] if parts0 is None else [parts0]))
    return res[0], res[1], res[2:]


def _adam_step(g, w, m, v):
    nm = ADAM_B1 * m + (1.0 - ADAM_B1) * g
    nv = ADAM_B2 * v + (1.0 - ADAM_B2) * (g * g)
    m_hat = nm / (1.0 - ADAM_B1 ** ADAM_STEP)
    v_hat = nv / (1.0 - ADAM_B2 ** ADAM_STEP)
    return -ADAM_LR * (m_hat / (jnp.sqrt(v_hat) + ADAM_EPS) + ADAM_WD * w), nm, nv


def _adamw(parts, w, m, v, name, row_tile=None):
    n_parts, rows, cols = parts.shape
    tr = rows if row_tile is None else row_tile

    def body(p_ref, w_ref, m_ref, v_ref, g_ref, d_ref, nm_ref, nv_ref):
        g = p_ref[0].astype(F32)
        for s in range(1, n_parts):
            g = g + p_ref[s].astype(F32)
        g_ref[...] = g
        d_ref[...], nm_ref[...], nv_ref[...] = _adam_step(g, w_ref[...], m_ref[...], v_ref[...])

    blk = pl.BlockSpec((tr, cols), lambda i: (i, 0))
    shp = jax.ShapeDtypeStruct((rows, cols), F32)
    return pl.pallas_call(
        body, name=name,
        grid=(rows // tr,),
        in_specs=[pl.BlockSpec((n_parts, tr, cols), lambda i: (0, i, 0)), blk, blk, blk],
        out_specs=(blk, blk, blk, blk),
        out_shape=(shp, shp, shp, shp),
        compiler_params=pltpu.CompilerParams(vmem_limit_bytes=VMEM_LIMIT),
    )(parts, w, m, v)


def _multi_adamw(parts_list, params, name):
    n = len(params)
    flat = [t for wmv in params for t in wmv]

    def body(*refs):
        parts, ins, outs = refs[:n], refs[n:4 * n], refs[4 * n:]
        for p in range(n):
            g = parts[p][0].astype(F32)
            for s in range(1, parts[p].shape[0]):
                g = g + parts[p][s].astype(F32)
            w_ref, m_ref, v_ref = ins[3 * p:3 * p + 3]
            g_ref, d_ref, nm_ref, nv_ref = outs[4 * p:4 * p + 4]
            g_ref[...] = g
            d_ref[...], nm_ref[...], nv_ref[...] = _adam_step(g, w_ref[...], m_ref[...], v_ref[...])

    out_shape = []
    for w, _, _ in params:
        out_shape += [jax.ShapeDtypeStruct(w.shape, F32)] * 4
    res = pl.pallas_call(body, name=name, out_shape=tuple(out_shape))(*parts_list, *flat)
    return [res[4 * p:4 * p + 4] for p in range(n)]


def _small_updates(small_g, dmod_all, rel_parts, params):
    flat = [t for wmv in params for t in wmv]

    def body(sg_ref, dm_ref, rp_ref, *refs):
        ins, outs = refs[:len(flat)], refs[len(flat):]

        def over_devices(row):
            tot = sg_ref[0, row:row + 1, :]
            for s in range(1, N_DEV):
                tot = tot + sg_ref[s, row:row + 1, :]
            return tot

        g_b_ada = dm_ref[0:1, :]
        for r in range(1, N_DEV * BL):
            g_b_ada = g_b_ada + dm_ref[r:r + 1, :]
        g_rel = rp_ref[0]
        for s in range(1, N_DEV):
            g_rel = g_rel + rp_ref[s]
        grads = [g_b_ada, over_devices(0), g_rel, over_devices(1), over_devices(2)]
        outs[0][...] = jnp.sum(over_devices(3), axis=1, keepdims=True)
        for p, g in enumerate(grads):
            w_ref, m_ref, v_ref = ins[3 * p:3 * p + 3]
            g_ref, d_ref, nm_ref, nv_ref = outs[1 + 4 * p:5 + 4 * p]
            g_ref[...] = g
            d_ref[...], nm_ref[...], nv_ref[...] = _adam_step(g, w_ref[...], m_ref[...], v_ref[...])

    out_shape = [jax.ShapeDtypeStruct((1, 1), F32)]
    for w, _, _ in params:
        out_shape += [jax.ShapeDtypeStruct(w.shape, F32)] * 4
    res = pl.pallas_call(body, name="small_updates", out_shape=tuple(out_shape))(small_g, dmod_all, rel_parts, *flat)
    return res[0], [res[1 + 4 * p:5 + 4 * p] for p in range(len(params))]


def _attn_fwd_dense(proj, bias):
    nq = 4
    rows = nq * QB
    nsb = S // rows

    def body(q_ref, k_ref, v_ref, kp_ref, vp_ref, b_ref, o_ref, l_ref, ls0, ls1, ls2, ls3):
        ls = [ls0, ls1, ls2, ls3]
        n = pl.program_id(1)
        lane = lax.broadcasted_iota(jnp.int32, (QB, 128), 1)
        units = [(h, j) for h in range(4) for j in range(nq)]

        def keys(cur_ref, prev_ref, h, j):
            sl = slice(h * HD, (h + 1) * HD)
            if j == 0:
                return jnp.concatenate([prev_ref[:, sl], cur_ref[0:QB, sl]], axis=0)
            return cur_ref[(j - 1) * QB:(j + 1) * QB, sl]

        q = jnp.stack([q_ref[j * QB:(j + 1) * QB, h * HD:(h + 1) * HD] for h, j in units])
        k = jnp.stack([keys(k_ref, kp_ref, h, j) for h, j in units])
        v = jnp.stack([keys(v_ref, vp_ref, h, j) for h, j in units])
        bias_b = jnp.stack([b_ref[jnp.minimum(n, 1), h] if j == 0 else b_ref[1, h] for h, j in units])
        s = jnp.einsum("uqd,ukd->uqk", q, k, preferred_element_type=F32) * SCALE + bias_b
        m = jnp.max(s, axis=-1, keepdims=True)
        p = jnp.exp(s - m)
        l = jnp.sum(p, axis=-1, keepdims=True)
        o = jnp.einsum("uqk,ukd->uqd", p.astype(BF16), v, preferred_element_type=F32) / l
        lse = m + jnp.log(l)
        for i, (h, j) in enumerate(units):
            o_ref[j * QB:(j + 1) * QB, h * HD:(h + 1) * HD] = o[i]
            ls[h][j * QB:(j + 1) * QB, :] = jnp.where(lane == h, lse[i], 0.0)
        l_ref[...] = (ls[0][...] + ls[1][...]) + (ls[2][...] + ls[3][...])

    def row(b, n):
        return b * nsb + n

    def prev(b, n):
        return jnp.maximum((b * nsb + n) * nq - 1, 0)

    in_specs = [
        pl.BlockSpec((rows, GW), lambda b, n: (row(b, n), CB_Q)),
        pl.BlockSpec((rows, GW), lambda b, n: (row(b, n), CB_K)),
        pl.BlockSpec((rows, GW), lambda b, n: (row(b, n), CB_V)),
        pl.BlockSpec((QB, GW), lambda b, n: (prev(b, n), CB_K)),
        pl.BlockSpec((QB, GW), lambda b, n: (prev(b, n), CB_V)),
        pl.BlockSpec((None, 2, 4, QB, 2 * QB), lambda b, n: (0, 0, 0, 0, 0)),
    ]
    return pl.pallas_call(
        body, name="attn_fwd0",
        grid=(BL, nsb),
        in_specs=in_specs,
        out_specs=(pl.BlockSpec((rows, GW), lambda b, n: (row(b, n), 0)),
                   pl.BlockSpec((rows, 128), lambda b, n: (row(b, n), 0))),
        out_shape=(jax.ShapeDtypeStruct((T, GW), F32), jax.ShapeDtypeStruct((T, 128), F32)),
        scratch_shapes=[pltpu.VMEM((rows, 128), F32)] * 4,
        compiler_params=pltpu.CompilerParams(vmem_limit_bytes=VMEM_LIMIT),
    )(proj, proj, proj, proj, proj, bias)


def _attn_bwd_dense(proj, d_out, stats, bias, dproj):
    nq = 4
    rows = nq * QB
    nsb = S // rows
    cols_q, cols_k, cols_v = CB * CB_Q, CB * CB_K, CB * CB_V

    def body(q_ref, k_ref, v_ref, do_ref, st_ref, kp_ref, vp_ref, b_ref, _, out_ref, db_ref,
             sq, sk, sv, carry, sems):
        b, n = pl.program_id(0), pl.program_id(1)
        units = [(h, j) for h in range(4) for j in range(nq)]

        @pl.when((b == 0) & (n == 0))
        def _():
            db_ref[...] = jnp.zeros_like(db_ref)

        @pl.when(n == 0)
        def _():
            carry[...] = jnp.zeros_like(carry)

        def write(first_block, position, count):
            row0 = pl.multiple_of(first_block * QB, QB)
            part = pl.ds(position * QB, count * QB)
            _write_columns([(sq.at[part], cols_q), (sk.at[part], cols_k), (sv.at[part], cols_v)],
                           out_ref, row0, sems)

        @pl.when(n == nsb)
        def _():
            sq[0:QB, :] = carry[:, 0:GW].astype(BF16)
            sk[0:QB, :] = carry[:, GW:2 * GW].astype(BF16)
            sv[0:QB, :] = carry[:, 2 * GW:3 * GW].astype(BF16)
            write((b + 1) * nsb * nq - 1, 0, 1)

        @pl.when(n < nsb)
        def _():
            def keys(cur_ref, prev_ref, h, j):
                sl = slice(h * HD, (h + 1) * HD)
                if j == 0:
                    return jnp.concatenate([prev_ref[:, sl], cur_ref[0:QB, sl]], axis=0)
                return cur_ref[(j - 1) * QB:(j + 1) * QB, sl]

            def block(ref, h, j):
                return ref[j * QB:(j + 1) * QB, h * HD:(h + 1) * HD]

            q = jnp.stack([block(q_ref, h, j) for h, j in units])
            do = jnp.stack([block(do_ref, h, j) for h, j in units])
            k = jnp.stack([keys(k_ref, kp_ref, h, j) for h, j in units])
            v = jnp.stack([keys(v_ref, vp_ref, h, j) for h, j in units])
            bias_b = jnp.stack([b_ref[jnp.minimum(n, 1), h] if j == 0 else b_ref[1, h] for h, j in units])
            lse = jnp.stack([st_ref[j * QB:(j + 1) * QB, h:h + 1] for h, j in units])
            delta = jnp.stack([st_ref[j * QB:(j + 1) * QB, 4 + h:5 + h] for h, j in units])
            s = jnp.einsum("uqd,ukd->uqk", q, k, preferred_element_type=F32) * SCALE + bias_b
            p = jnp.exp(s - lse)
            ds = p * (jnp.einsum("uqd,ukd->uqk", do, v, preferred_element_type=F32) - delta)
            for h in range(4):
                tot = ds[h * nq]
                for j in range(1, nq):
                    tot = tot + ds[h * nq + j]
                db_ref[h] += tot
            dsb, pb = ds.astype(BF16), p.astype(BF16)
            dq = jnp.einsum("uqk,ukd->uqd", dsb, k, preferred_element_type=F32) * SCALE
            dk = jnp.einsum("uqk,uqd->ukd", dsb, q, preferred_element_type=F32) * SCALE
            dv = jnp.einsum("uqk,uqd->ukd", pb, do, preferred_element_type=F32)
            for h in range(4):
                sl = slice(h * HD, (h + 1) * HD)
                u0, last = h * nq, h * nq + nq - 1
                sq[0:QB, sl] = carry[:, sl].astype(BF16)
                sk[0:QB, sl] = (carry[:, GW + h * HD:GW + (h + 1) * HD] + dk[u0, :QB]).astype(BF16)
                sv[0:QB, sl] = (carry[:, 2 * GW + h * HD:2 * GW + (h + 1) * HD] + dv[u0, :QB]).astype(BF16)
                for j in range(nq - 1):
                    pos = slice((j + 1) * QB, (j + 2) * QB)
                    sq[pos, sl] = dq[u0 + j].astype(BF16)
                    sk[pos, sl] = (dk[u0 + j, QB:] + dk[u0 + j + 1, :QB]).astype(BF16)
                    sv[pos, sl] = (dv[u0 + j, QB:] + dv[u0 + j + 1, :QB]).astype(BF16)
                carry[:, sl] = dq[last]
                carry[:, GW + h * HD:GW + (h + 1) * HD] = dk[last, QB:]
                carry[:, 2 * GW + h * HD:2 * GW + (h + 1) * HD] = dv[last, QB:]

            @pl.when(n == 0)
            def _():
                write(b * nsb * nq, 1, nq - 1)

            @pl.when(n > 0)
            def _():
                write((b * nsb + n) * nq - 1, 0, nq)

    def row(b, n):
        return b * nsb + jnp.minimum(n, nsb - 1)

    def prev(b, n):
        return jnp.maximum(row(b, n) * nq - 1, 0)

    in_specs = [
        pl.BlockSpec((rows, GW), lambda b, n: (row(b, n), CB_Q)),
        pl.BlockSpec((rows, GW), lambda b, n: (row(b, n), CB_K)),
        pl.BlockSpec((rows, GW), lambda b, n: (row(b, n), CB_V)),
        pl.BlockSpec((rows, GW), lambda b, n: (row(b, n), 0)),
        pl.BlockSpec((rows, 128), lambda b, n: (row(b, n), 0)),
        pl.BlockSpec((QB, GW), lambda b, n: (prev(b, n), CB_K)),
        pl.BlockSpec((QB, GW), lambda b, n: (prev(b, n), CB_V)),
        pl.BlockSpec((None, 2, 4, QB, 2 * QB), lambda b, n: (0, 0, 0, 0, 0)),
        pl.BlockSpec(memory_space=pl.ANY),
    ]
    return pl.pallas_call(
        body, name="attn_bwd0",
        grid=(BL, nsb + 1),
        in_specs=in_specs,
        out_specs=(pl.BlockSpec(memory_space=pl.ANY),
                   pl.BlockSpec((4, QB, 2 * QB), lambda b, n: (0, 0, 0))),
        out_shape=(jax.ShapeDtypeStruct((T, NCOL), BF16),
                   jax.ShapeDtypeStruct((4, QB, 2 * QB), F32)),
        scratch_shapes=[pltpu.VMEM((rows, GW), BF16)] * 3
                       + [pltpu.VMEM((QB, 3 * GW), F32), pltpu.SemaphoreType.DMA((3,))],
        input_output_aliases={8: 0},
        compiler_params=pltpu.CompilerParams(vmem_limit_bytes=VMEM_LIMIT),
    )(proj, proj, proj, d_out, stats, proj, proj, bias, dproj)


def _attention_forward(proj, rel_bias):
    expand_lanes, grad_lanes, masks = (jnp.asarray(t) for t in _bucket_maps())
    bias = _bias_expand(rel_bias, expand_lanes, masks)
    fwd = [_attn_fwd_dense(proj, bias)] + [_attn_fwd(proj, bias, g) for g in (1, 2)]
    return bias, grad_lanes, [f[0] for f in fwd], [f[1] for f in fwd]


def _local_step(x2, tgt2, mod3, h, proj, attn, w_ao, w_co, w_o, conv_w, conv_b, ln_g, ln_b):
    bias, buckets, o_g, lse_g = attn

    (dproj, dyc, d_o, stats, dxd, gw_o, gw_co, gw_ao, tail_vec) = _tail(
        x2, tgt2, mod3, o_g, lse_g, proj, w_ao, w_co, w_o, conv_w, conv_b, ln_g, ln_b)

    dproj, db = _attn_bwd_dense(proj, d_o, stats, bias, dproj)
    dbias = [db]
    for g in (1, 2):
        dproj, db = _attn_bwd(proj, d_o, stats, bias, dproj, g)
        dbias.append(db)
    g_rel_bias = _bias_grad(*dbias, buckets)
    dproj, conv_vec = _conv_bwd(dyc, proj, conv_w, dproj)

    gw_ao = jnp.transpose(gw_ao.reshape(GW, N_DEV, D // N_DEV), (1, 0, 2))
    return dproj, dxd, gw_ao, gw_co, gw_o, conv_vec, g_rel_bias, tail_vec


def kernel(x, c, w_ada, b_ada, w_in, conv_w, conv_b, rel_bias, w_attn_out, w_conv_out, w_o, ln_g, ln_b, loss_target, m_w_ada, m_b_ada, m_w_in, m_conv_w, m_conv_b, m_rel_bias, m_w_attn_out, m_w_conv_out, m_w_o, m_ln_g, m_ln_b, v_w_ada, v_b_ada, v_w_in, v_conv_w, v_conv_b, v_rel_bias, v_w_attn_out, v_w_conv_out, v_w_o, v_ln_g, v_ln_b):
    me = _my_index()
    x2 = x.reshape(T, D)
    tgt2 = loss_target.reshape(T, D)

    b_cols = lax.dynamic_slice(b_ada, (0, me * ADA_SHARD), (1, ADA_SHARD))
    c_g, mod_in = _mod_exchange(jnp.pad(c, ((0, 8 - BL), (0, 0))), w_ada[0], b_cols)
    c_all = c_g[:, 0:BL, :].reshape(N_DEV * BL, D)
    mod3 = jnp.transpose(mod_in[:, 0:BL, :], (1, 0, 2)).reshape(BL, 3, D)

    h = _prep_h(x2, mod3)
    rows_shape = jax.ShapeDtypeStruct((N_DEV, D // N_DEV, D), BF16)
    proj, w_in_all, (w_ao_g, w_co_g, w_o_g, conv_w_g) = _gather_proj(
        _shard_order(), h, w_in[0].astype(BF16), 1024,
        ([w_attn_out[0].astype(BF16), w_conv_out[0].astype(BF16), w_o[0].astype(BF16), conv_w[0]],
         [jax.ShapeDtypeStruct((N_DEV, GW, D // N_DEV), BF16), rows_shape, rows_shape,
          jax.ShapeDtypeStruct((N_DEV, 3, D // N_DEV), F32)]))

    attn = _attention_forward(proj, rel_bias)
    w_ao_full = jnp.transpose(w_ao_g, (1, 0, 2)).reshape(GW, D)
    w_co_full = w_co_g.reshape(D, D)
    w_o_full = w_o_g.reshape(D, D)
    conv_w_full = jnp.transpose(conv_w_g, (1, 0, 2)).reshape(3, D)

    (dproj, dxd, gw_ao, gw_co, gw_o, conv_vec, g_rel_bias, tail_vec) = _local_step(
        x2, tgt2, mod3, h, proj, attn, w_ao_full, w_co_full, w_o_full,
        conv_w_full, conv_b, ln_g, ln_b)

    g_conv_w_blocks = jnp.transpose(conv_vec[0:3].reshape(3, N_DEV, D // N_DEV), (1, 0, 2))
    partials = [gw_ao, gw_co.reshape(N_DEV, D // N_DEV, D), gw_o.reshape(N_DEV, D // N_DEV, D), g_conv_w_blocks]
    w_in_sums, w_in_parts, sib = _gw_in_pair(
        _slice_order(), h, dproj, partials,
        [jax.ShapeDtypeStruct((4, GW, D // N_DEV), BF16),
         jax.ShapeDtypeStruct((4, D // N_DEV, D), BF16),
         jax.ShapeDtypeStruct((4, D // N_DEV, D), BF16),
         jax.ShapeDtypeStruct((4, 3, D // N_DEV), F32)])
    core = lax.axis_index("c").astype(jnp.int32).reshape(1)
    chip_sums = [w_in_sums] + list(_pair_add(core, partials, sib))
    hops = [(3,)] + [(1, 2, 3)] * 4
    grad_x, mod_vec, (r_in, r_ao, r_co, r_o, r_cw) = _dh_dx(
        dproj, w_in_all, x2, dxd, mod3, chip_sums, hops, w_in_parts)

    small = jnp.concatenate([
        tail_vec[0:4],
        jnp.pad(g_rel_bias.reshape(1, N_BUCKETS * N_HEADS), ((0, 0), (0, D - N_BUCKETS * N_HEADS))),
        jnp.zeros((3, D), F32)], axis=0)
    dmod = jnp.concatenate([mod_vec[0:2], mod_vec[2:4], tail_vec[4:6]], axis=1)
    small_g, dmod_g = _all_gather(
        [small, dmod],
        [jax.ShapeDtypeStruct((N_DEV, 8, D), F32), jax.ShapeDtypeStruct((N_DEV, BL, 3 * D), F32)],
        "gather_small")
    dmod_all = dmod_g.reshape(N_DEV * BL, 3 * D)
    small_names = ["b_ada", "conv_b", "rel_bias", "ln_g", "ln_b"]
    small_params = [(b_ada, m_b_ada, v_b_ada), (conv_b, m_conv_b, v_conv_b), (rel_bias, m_rel_bias, v_rel_bias),
                    (ln_g, m_ln_g, v_ln_g), (ln_b, m_ln_b, v_ln_b)]
    loss, small_res = _small_updates(
        small_g, dmod_all, small_g[:, 4, :N_BUCKETS * N_HEADS].reshape(N_DEV, N_BUCKETS, N_HEADS), small_params)
    loss = loss.reshape(())

    dmod_cols = lax.dynamic_slice(dmod_all, (0, me * ADA_SHARD), (N_DEV * BL, ADA_SHARD))
    res = {
        "w_ada": tuple(t[None] for t in _w_ada_update(jnp.transpose(c_all), dmod_cols,
                                                      w_ada[0], m_w_ada[0], v_w_ada[0])),
        "w_in": tuple(t[None] for t in _adamw(r_in, w_in[0], m_w_in[0], v_w_in[0], "adam_w_in", 128)),
    }
    mid_names = ["conv_w", "w_attn_out", "w_conv_out", "w_o"]
    mid_parts = [r_cw, r_ao, r_co, r_o]
    mid_full = [(conv_w, m_conv_w, v_conv_w), (w_attn_out, m_w_attn_out, v_w_attn_out),
                (w_conv_out, m_w_conv_out, v_w_conv_out), (w_o, m_w_o, v_w_o)]
    mid_res = _multi_adamw(mid_parts, [tuple(t[0] for t in wmv) for wmv in mid_full], "adam_mid")
    for nm, wmv, outs4 in zip(mid_names, mid_full, mid_res):
        res[nm] = tuple(t[None] for t in outs4)
    res.update(dict(zip(small_names, small_res)))
    order = ["w_ada", "b_ada", "w_in", "conv_w", "conv_b", "rel_bias", "w_attn_out", "w_conv_out",
             "w_o", "ln_g", "ln_b"]
    outs = [loss, grad_x.reshape(BL, S, D)]
    for k in range(4):
        outs += [res[name][k] for name in order]
    return tuple(outs)
```

```python
import math

import numpy as np
import jax
import jax.numpy as jnp
from jax import lax
from jax.experimental import pallas as pl
from jax.experimental.pallas import tpu as pltpu

F32 = jnp.float32
BF16 = jnp.bfloat16
MESH = pl.DeviceIdType.MESH

N_DEV = 8
D = 1024
S = 2048
BL = 2
T = BL * S
NCOL = 11264
SHARD = NCOL // N_DEV
CB = 512
NCB = NCOL // CB
HD = 128
GW = 512
QB = 128
DILATIONS = (1, 4, 16)
N_STEPS = 128
N_BUCKETS = 32
N_HEADS = 12
ALPHA = 2.0 ** 0.25
LN_EPS = 1e-5
NEG_INF = -1e30
SCALE = HD ** -0.5
ADA_SHARD = 3 * D // N_DEV

CB_Q, CB_K, CB_V, CB_GA = 0, 3, 6, 9
KB_U, KB_BG, KB_CG, KB_GC, KB_MA, KB_MC = 5, 6, 7, 8, 9, 10

ADAM_LR, ADAM_B1, ADAM_B2, ADAM_EPS, ADAM_WD, ADAM_STEP = 0.001, 0.9, 0.999, 1e-08, 0.01, 10

VMEM_LIMIT = 56 * 1024 * 1024
VMEM_LIMIT_TAIL = 62 * 1024 * 1024


def _dot(a, b):
    return jnp.dot(a, b, preferred_element_type=F32)


def _dot_nt(a, b):
    return lax.dot_general(a, b, (((1,), (1,)), ((), ())), preferred_element_type=F32)


def _dot_tn(a, b):
    return lax.dot_general(a, b, (((0,), (0,)), ((), ())), preferred_element_type=F32)


def _sigmoid(v):
    return 1.0 / (1.0 + jnp.exp(-v))


def _column_copies(pieces, dst_hbm, row0, sems):
    copies = []
    for k, (src, col0) in enumerate(pieces):
        rows, width = src.shape
        copies.append(pltpu.make_async_copy(
            src, dst_hbm.at[pl.ds(row0, rows), pl.ds(col0, width)], sems.at[k]))
    return copies


def _my_index():
    return 4 * lax.axis_index("x") + 2 * lax.axis_index("y") + lax.axis_index("c")


class _Gather:
    def __init__(self, ins, outs, stage, send_sems, recv_sems, local_sems):
        self.ins, self.outs, self.stage = ins, outs, stage
        self.send_sems, self.recv_sems, self.local_sems = send_sems, recv_sems, local_sems
        x, y, c = lax.axis_index("x"), lax.axis_index("y"), lax.axis_index("c")
        self.c = c
        self.me, self.sibling = (x, y, c), (x, y, 1 - c)
        self.chips = [(1 - x, y), (x, 1 - y), (1 - x, 1 - y)]

    @staticmethod
    def scratch(arrs):
        n = len(arrs)
        return ([pltpu.SemaphoreType.DMA((7 * n,)), pltpu.SemaphoreType.DMA((7 * n,)),
                 pltpu.SemaphoreType.DMA((n,))] + [pltpu.VMEM(a.shape, a.dtype) for a in arrs])

    def _copy(self, a, k, block, to, src=None):
        dst = self.outs[a].at[4 * block[0] + 2 * block[1] + block[2]]
        return pltpu.make_async_remote_copy(
            src_ref=dst if src is None else src, dst_ref=dst,
            send_sem=self.send_sems.at[a * 7 + k], recv_sem=self.recv_sems.at[a * 7 + k],
            device_id=to, device_id_type=MESH)

    def _first(self):
        first = []
        for a in range(len(self.ins)):
            first.append(self._copy(a, 0, self.me, self.sibling, src=self.ins[a]))
            first += [self._copy(a, 1 + j, self.me, (*chip, self.c), src=self.ins[a])
                      for j, chip in enumerate(self.chips)]
        return first

    def _mine(self):
        me = self.me
        return [pltpu.make_async_copy(self.stage[a], self.outs[a].at[4 * me[0] + 2 * me[1] + me[2]],
                                      self.local_sems.at[a]) for a in range(len(self.ins))]

    def begin(self):
        for cp in self._first():
            cp.start()
        loads = [pltpu.make_async_copy(self.ins[a], self.stage[a], self.local_sems.at[a])
                 for a in range(len(self.ins))]
        for cp in loads:
            cp.start()
        for cp in loads:
            cp.wait()
        for cp in self._mine():
            cp.start()

    def finish(self):
        n, c, me, sibling = len(self.ins), self.c, self.me, self.sibling
        passed = []
        for j, chip in enumerate(self.chips):
            for a in range(n):
                self._copy(a, 1 + j, (*chip, c), me).wait_recv()
                fwd = self._copy(a, 4 + j, (*chip, c), sibling)
                fwd.start()
                passed.append(fwd)
        for a in range(n):
            self._copy(a, 0, sibling, me).wait_recv()
        for j, chip in enumerate(self.chips):
            for a in range(n):
                self._copy(a, 4 + j, (*chip, 1 - c), me).wait_recv()
        for cp in self._first() + passed:
            cp.wait_send()
        for cp in self._mine():
            cp.wait()


def _all_gather(arrs, out_shapes, name):
    n = len(arrs)

    def body(*refs):
        g = _Gather(refs[:n], refs[n:2 * n], refs[2 * n + 3:], *refs[2 * n:2 * n + 3])
        g.begin()
        g.finish()

    any_spec = pl.BlockSpec(memory_space=pl.ANY)
    return pl.pallas_call(
        body, name=name,
        out_shape=tuple(out_shapes),
        in_specs=[any_spec] * n,
        out_specs=tuple([any_spec] * n),
        scratch_shapes=_Gather.scratch(arrs),
    )(*arrs)


def _neighbour_chips():
    x, y, c = lax.axis_index("x"), lax.axis_index("y"), lax.axis_index("c")
    first = (jnp.where(c == 0, 1 - x, x), jnp.where(c == 0, y, 1 - y))
    second = (jnp.where(c == 0, x, 1 - x), jnp.where(c == 0, 1 - y, y))
    return first, second, (1 - x, 1 - y)


def _slice_order():
    x, y, c = lax.axis_index("x"), lax.axis_index("y"), lax.axis_index("c")
    nb1, nb2, diag = _neighbour_chips()
    slots = []
    for mine, theirs in ((nb1, nb2), (nb2, nb1), (diag, diag), ((x, y), (x, y))):
        slots += [2 * (2 * theirs[0] + theirs[1]) + 1 - c, 2 * (2 * mine[0] + mine[1]) + c]
    return jnp.stack(slots).astype(jnp.int32)


def _gw_in_pair(order, h, dproj, smalls, small_shapes4):
    kk, m = h.shape
    tk = min(kk, 2048)
    nk = kk // tk
    ncols = dproj.shape[1] // N_DEV
    n = len(smalls)

    def body(order_ref, h_ref, d_ref, *rest):
        ins = rest[:n]
        sums_hbm, parts_hbm = rest[n], rest[n + 1]
        sib = rest[n + 2:2 * n + 2]
        (acc, sendbuf, recvbuf, sumbuf, send_sems, recv_sems, local_sem, ssend, srecv,
         isend, irecv) = rest[2 * n + 2:]
        js, k = pl.program_id(0), pl.program_id(1)
        x, y, c = lax.axis_index("x"), lax.axis_index("y"), lax.axis_index("c")
        sibling = (x, y, 1 - c)
        my_chip = 2 * x + y
        nb1, nb2, _ = _neighbour_chips()
        near = [(*nb1, c), (*nb2, c)]

        def ici_copy(p, out_chip):
            peer = near[p] if isinstance(p, int) else tuple(jnp.where(p == 0, a, b) for a, b in zip(*near))
            return pltpu.make_async_remote_copy(
                src_ref=sumbuf.at[p], dst_ref=parts_hbm.at[out_chip],
                send_sem=isend.at[p], recv_sem=irecv.at[p], device_id=peer, device_id_type=MESH)

        def small_copies():
            return [pltpu.make_async_remote_copy(
                        src_ref=ins[a].at[2 * q + 1 - c], dst_ref=sib[a].at[q],
                        send_sem=ssend.at[a * 4 + q], recv_sem=srecv.at[a * 4 + q],
                        device_id=sibling, device_id_type=MESH)
                    for a in range(n) for q in range(4)]

        def slice_copy(p):
            return pltpu.make_async_remote_copy(
                src_ref=sendbuf, dst_ref=recvbuf.at[p], send_sem=send_sems.at[p], recv_sem=recv_sems.at[p],
                device_id=sibling, device_id_type=MESH)

        def sum_copy(p):
            return pltpu.make_async_copy(sumbuf.at[2], sums_hbm.at[order_ref[2 * p] // 2], local_sem)

        @pl.when((js == 0) & (k == 0))
        def _():
            for cp in small_copies():
                cp.start()

        def partial():
            return _dot_tn(h_ref[...], d_ref[...])

        if nk > 1:
            @pl.when(k == 0)
            def _():
                acc[...] = partial()
        if nk > 2:
            @pl.when((k > 0) & (k < nk - 1))
            def _():
                acc[...] += partial()

        def total():
            return partial() + acc[...] if nk > 1 else partial()

        p = js // 2

        @pl.when((js % 2 == 0) & (k == nk - 1))
        def _():
            @pl.when(p > 0)
            def _():
                slice_copy(p - 1).wait_send()
            sendbuf[...] = total().astype(BF16)
            slice_copy(p).start()

        @pl.when((js % 2 == 1) & (k == nk - 1))
        def _():
            slice_copy(p).wait_recv()

            @pl.when(p == 3)
            def _():
                sum_copy(2).wait()
            sumbuf[jnp.minimum(p, 2)] = (total() + recvbuf[p].astype(F32)).astype(BF16)

            @pl.when(p < 2)
            def _():
                ici_copy(p, my_chip).start()

            @pl.when(p >= 2)
            def _():
                sum_copy(p).start()

        @pl.when((js == N_DEV - 1) & (k == nk - 1))
        def _():
            slice_copy(3).wait_send()
            sum_copy(3).wait()
            for cp in small_copies():
                cp.wait()
            for p in range(2):
                ici_copy(p, 2 * near[p][0] + near[p][1]).wait_recv()
                ici_copy(p, my_chip).wait_send()

    any_spec = pl.BlockSpec(memory_space=pl.ANY)
    res = pl.pallas_call(
        body, name="gw_in_pair",
        grid_spec=pltpu.PrefetchScalarGridSpec(
            num_scalar_prefetch=1,
            grid=(N_DEV, nk),
            in_specs=[pl.BlockSpec((tk, m), lambda js, k, order_ref: (k, 0)),
                      pl.BlockSpec((tk, ncols), lambda js, k, order_ref: (k, order_ref[js]))] + [any_spec] * n,
            out_specs=(any_spec,) * (n + 2),
            scratch_shapes=[pltpu.VMEM((m, ncols), F32), pltpu.VMEM((m, ncols), BF16),
                            pltpu.VMEM((4, m, ncols), BF16), pltpu.VMEM((3, m, ncols), BF16),
                            pltpu.SemaphoreType.DMA((4,)), pltpu.SemaphoreType.DMA((4,)),
                            pltpu.SemaphoreType.DMA,
                            pltpu.SemaphoreType.DMA((4 * n,)), pltpu.SemaphoreType.DMA((4 * n,)),
                            pltpu.SemaphoreType.DMA((2,)), pltpu.SemaphoreType.DMA((2,))]),
        out_shape=(jax.ShapeDtypeStruct((4, m, ncols), BF16),) * 2 + tuple(small_shapes4),
        compiler_params=pltpu.CompilerParams(vmem_limit_bytes=VMEM_LIMIT),
    )(order, h, dproj, *smalls)
    return res[0], res[1], res[2:]


def _chip_copies(ins, outs, send_sems, recv_sems, local_sems, hops):
    n = len(ins)
    x, y, c = lax.axis_index("x"), lax.axis_index("y"), lax.axis_index("c")
    my_chip = 2 * x + y

    def peer_of(k):
        return ((1 - x) if (k >> 1) & 1 else x, (1 - y) if k & 1 else y, c)

    def copy(a, k, out_chip):
        peer = peer_of(k)
        return pltpu.make_async_remote_copy(
            src_ref=ins[a].at[2 * peer[0] + peer[1]], dst_ref=outs[a].at[out_chip],
            send_sem=send_sems.at[a * 3 + k - 1], recv_sem=recv_sems.at[a * 3 + k - 1],
            device_id=peer, device_id_type=MESH)

    sends = [copy(a, k, my_chip) for k in range(1, 4) for a in range(n) if k in hops[a]]
    arrivals = []
    for k in range(1, 4):
        peer = peer_of(k)
        arrivals += [copy(a, k, 2 * peer[0] + peer[1]) for a in range(n) if k in hops[a]]
    mine = [pltpu.make_async_copy(ins[a].at[my_chip], outs[a].at[my_chip], local_sems.at[a])
            for a in range(n)]
    return sends, arrivals, mine


def _pair_add(core, mines, theirs):
    n = len(mines)

    def body(core_ref, *refs):
        mine, sib, outs = refs[:n], refs[n:2 * n], refs[2 * n:]
        for a in range(n):
            for q in range(4):
                outs[a][q] = (mine[a][2 * q + core_ref[0]].astype(F32)
                              + sib[a][q].astype(F32)).astype(outs[a].dtype)

    return pl.pallas_call(
        body, name="pair_add",
        in_specs=[pl.BlockSpec(memory_space=pltpu.SMEM)] + [pl.BlockSpec(memory_space=pltpu.VMEM)] * (2 * n),
        out_shape=tuple(jax.ShapeDtypeStruct(t.shape, t.dtype) for t in theirs),
    )(core, *mines, *theirs)


def _mod_exchange(c8, w_ada, b_cols):
    cols = w_ada.shape[1]

    def body(c_ref, w_ref, b_ref, call_ref, mod_ref, msend, send1, recv1, send2, recv2):
        x, y, c = lax.axis_index("x"), lax.axis_index("y"), lax.axis_index("c")
        my_slot = 4 * x + 2 * y + c

        def peer_of(k):
            return ((1 - x) if (k >> 2) & 1 else x, (1 - y) if (k >> 1) & 1 else y, (1 - c) if k & 1 else c)

        def slot_of(dev):
            return 4 * dev[0] + 2 * dev[1] + dev[2]

        def exchange(src_of, dst_ref, send_sems, recv_sems):
            sends, arrivals = [], []
            for k in range(1, 8):
                peer = peer_of(k)
                sends.append(pltpu.make_async_remote_copy(
                    src_ref=src_of(slot_of(peer)), dst_ref=dst_ref.at[my_slot],
                    send_sem=send_sems.at[k - 1], recv_sem=recv_sems.at[k - 1],
                    device_id=peer, device_id_type=MESH))
                arrivals.append(pltpu.make_async_remote_copy(
                    src_ref=src_of(my_slot), dst_ref=dst_ref.at[slot_of(peer)],
                    send_sem=send_sems.at[k - 1], recv_sem=recv_sems.at[k - 1],
                    device_id=peer, device_id_type=MESH))
            for cp in sends:
                cp.start()
            for cp in arrivals:
                cp.wait_recv()
            for cp in sends:
                cp.wait_send()

        call_ref[my_slot] = c_ref[...]
        exchange(lambda s: c_ref, call_ref, send1, recv1)
        cv = call_ref[...].reshape(N_DEV * 8, c_ref.shape[1])
        act = cv * _sigmoid(cv)
        mod = jnp.dot(act, w_ref[...], preferred_element_type=F32,
                      precision=lax.Precision.HIGHEST) + b_ref[...]
        msend[...] = mod.reshape(N_DEV, 8, cols)
        mod_ref[my_slot] = msend[my_slot]
        exchange(lambda s: msend.at[s], mod_ref, send2, recv2)

    return pl.pallas_call(
        body, name="mod_exchange",
        out_shape=(jax.ShapeDtypeStruct((N_DEV, 8, c8.shape[1]), F32),
                   jax.ShapeDtypeStruct((N_DEV, 8, cols), F32)),
        scratch_shapes=[pltpu.VMEM((N_DEV, 8, cols), F32)] + [pltpu.SemaphoreType.DMA((7,))] * 4,
    )(c8, w_ada, b_cols)


def _w_ada_update(c_all_t, dmod_cols, w, m, v):
    rows, cols = w.shape
    tr = 256

    def body(c_ref, d_ref, w_ref, m_ref, v_ref, g_ref, dl_ref, nm_ref, nv_ref):
        cv = c_ref[...]
        g = jnp.dot(cv * _sigmoid(cv), d_ref[...], preferred_element_type=F32,
                    precision=lax.Precision.HIGHEST)
        g_ref[...] = g
        dl_ref[...], nm_ref[...], nv_ref[...] = _adam_step(g, w_ref[...], m_ref[...], v_ref[...])

    blk = pl.BlockSpec((tr, cols), lambda i: (i, 0))
    shp = jax.ShapeDtypeStruct((rows, cols), F32)
    return pl.pallas_call(
        body, name="adam_w_ada",
        grid=(rows // tr,),
        in_specs=[pl.BlockSpec((tr, c_all_t.shape[1]), lambda i: (i, 0)),
                  pl.BlockSpec(dmod_cols.shape, lambda i: (0, 0)), blk, blk, blk],
        out_specs=(blk, blk, blk, blk),
        out_shape=(shp, shp, shp, shp),
    )(c_all_t, dmod_cols, w, m, v)


def _shard_order():
    x, y, c = lax.axis_index("x"), lax.axis_index("y"), lax.axis_index("c")
    first, second, diag = _neighbour_chips()
    devs = [(x, y, c), (x, y, 1 - c), (*first, c), (*second, 1 - c), (*second, c), (*first, 1 - c),
            (*diag, c), (*diag, 1 - c)]
    return jnp.stack([4 * d[0] + 2 * d[1] + d[2] for d in devs]).astype(jnp.int32)


def _prep_h(x2, mod3):
    ts = 512
    per_seq = S // ts

    def body(x_ref, mod_ref, h_ref):
        shift = mod_ref[0, 0:1, :]
        scale = mod_ref[0, 1:2, :]
        h_ref[...] = (x_ref[...] * (1.0 + scale) + shift).astype(BF16)

    return pl.pallas_call(
        body, name="prep_h",
        grid=(T // ts,),
        in_specs=[pl.BlockSpec((ts, D), lambda i: (i, 0)),
                  pl.BlockSpec((1, 3, D), lambda i: (i // per_seq, 0, 0))],
        out_specs=pl.BlockSpec((ts, D), lambda i: (i, 0)),
        out_shape=jax.ShapeDtypeStruct((T, D), BF16),
    )(x2, mod3)


def _gather_proj(order, h, w_shard, tm, ride=()):
    rows, kdim = h.shape
    ncols = w_shard.shape[1]
    n_i = rows // tm
    ride_arrs, ride_shapes = ride if ride else ((), ())
    n_ride = len(ride_arrs)

    def body(order_ref, h_ref, mine_hbm, *rest):
        ride_ins = rest[:n_ride]
        o_ref, all_hbm = rest[n_ride:n_ride + 2]
        ride_outs = rest[n_ride + 2:2 * n_ride + 2]
        wv, send_sems, recv_sems, local_sems = rest[2 * n_ride + 2:2 * n_ride + 6]
        ride_scr = rest[2 * n_ride + 6:]
        j, i = pl.program_id(0), pl.program_id(1)
        c = lax.axis_index("c")
        me, sibling = (lax.axis_index("x"), lax.axis_index("y"), c), (lax.axis_index("x"), lax.axis_index("y"), 1 - c)
        nb1, nb2, diag = _neighbour_chips()

        def slot(dev):
            return 4 * dev[0] + 2 * dev[1] + dev[2]

        def copy(k, block, to, src=None, part=None):
            buf = wv.at[slot(block)]
            if part is not None:
                buf = buf.at[pl.ds(pl.multiple_of(part * (kdim // 2), kdim // 2), kdim // 2)]
            return pltpu.make_async_remote_copy(
                src_ref=buf if src is None else src, dst_ref=buf,
                send_sem=send_sems.at[k], recv_sem=recv_sems.at[k],
                device_id=to, device_id_type=MESH)

        def keep(step, block):
            return pltpu.make_async_copy(wv.at[slot(block)], all_hbm.at[slot(block)], local_sems.at[step])

        if n_ride:
            gather = _Gather(ride_ins, ride_outs, ride_scr[3:], *ride_scr[:3])
        to_sibling, to_nb1, to_nb2 = (copy(0, me, sibling, mine_hbm), copy(1, me, (*nb1, c), mine_hbm),
                                      copy(2, me, (*nb2, c), mine_hbm))
        relay1, relay2 = copy(3, (*nb2, c), (*nb1, c), part=c), copy(4, (*nb1, c), (*nb2, c), part=1 - c)
        pass_nb1, pass_nb2 = copy(5, (*nb1, c), sibling), copy(6, (*nb2, c), sibling)
        pass_d1, pass_d2 = copy(7, (*diag, c), sibling, part=c), copy(8, (*diag, c), sibling, part=1 - c)
        sends = [to_sibling, to_nb1, to_nb2, relay1, relay2, pass_nb1, pass_nb2, pass_d1, pass_d2]
        due = [
            (me, [], []),
            (sibling, [copy(0, sibling, me)], [[]]),
            ((*nb1, c), [copy(1, (*nb1, c), me)], [[pass_nb1, to_nb2]]),
            ((*nb2, 1 - c), [copy(5, (*nb2, 1 - c), me)], [[]]),
            ((*nb2, c), [copy(2, (*nb2, c), me)], [[pass_nb2, relay1, relay2]]),
            ((*nb1, 1 - c), [copy(6, (*nb1, 1 - c), me)], [[]]),
            ((*diag, c), [copy(3, (*diag, c), me, part=c), copy(4, (*diag, c), me, part=1 - c)],
             [[pass_d1], [pass_d2]]),
            ((*diag, 1 - c), [copy(7, (*diag, 1 - c), me, part=1 - c), copy(8, (*diag, 1 - c), me, part=c)],
             [[], []]),
        ]

        @pl.when((j == 0) & (i == 0))
        def _():
            to_sibling.start()
            to_nb1.start()
            load = pltpu.make_async_copy(mine_hbm, wv.at[slot(me)], local_sems.at[N_DEV])
            load.start()
            load.wait()
            keep(0, me).start()

        for step in range(1, N_DEV):
            block, arrivals, then = due[step]

            @pl.when((j == step) & (i == 0))
            def _():
                for arrival, follow in zip(arrivals, then):
                    arrival.wait_recv()
                    for cp in follow:
                        cp.start()
                keep(step, block).start()
                if n_ride and step == N_DEV - 2:
                    gather.begin()

        o_ref[...] = _dot(h_ref[...], wv[order_ref[j]]).astype(BF16)

        @pl.when((j == N_DEV - 1) & (i == n_i - 1))
        def _():
            for cp in sends:
                cp.wait_send()
            for step in range(N_DEV):
                keep(step, due[step][0]).wait()
            if n_ride:
                gather.finish()

    any_spec = pl.BlockSpec(memory_space=pl.ANY)
    res = pl.pallas_call(
        body, name="gather_proj",
        grid_spec=pltpu.PrefetchScalarGridSpec(
            num_scalar_prefetch=1,
            grid=(N_DEV, n_i),
            in_specs=[pl.BlockSpec((tm, kdim), lambda j, i, order_ref: (i, 0)), any_spec] + [any_spec] * n_ride,
            out_specs=(pl.BlockSpec((tm, ncols), lambda j, i, order_ref: (i, order_ref[j])), any_spec)
                      + (any_spec,) * n_ride,
            scratch_shapes=[pltpu.VMEM((N_DEV, kdim, ncols), BF16),
                            pltpu.SemaphoreType.DMA((9,)), pltpu.SemaphoreType.DMA((9,)),
                            pltpu.SemaphoreType.DMA((N_DEV + 1,))]
                           + (_Gather.scratch(ride_arrs) if n_ride else [])),
        out_shape=(jax.ShapeDtypeStruct((rows, N_DEV * ncols), BF16),
                   jax.ShapeDtypeStruct((N_DEV, kdim, ncols), BF16)) + tuple(ride_shapes),
        compiler_params=pltpu.CompilerParams(vmem_limit_bytes=VMEM_LIMIT),
    )(order, h, w_shard, *ride_arrs)
    return res[0], res[1], res[2:]


SKEW_W = 512


def _bucket_maps():
    lanes = np.arange(SKEW_W)

    def buckets_of(steps):
        rows = []
        for dil in DILATIONS:
            dist = np.maximum(steps, 0) * dil
            nf = np.maximum(dist, 1).astype(np.float32)
            large = 16 + (np.log(nf / np.float32(16)) / np.float32(math.log(128.0))
                          * np.float32(16)).astype(np.int32)
            large = np.minimum(large, N_BUCKETS - 1)
            bucket = np.where(dist < 16, dist, large)
            rows.append(np.where((steps >= 0) & (steps <= N_STEPS), bucket, -1).astype(np.int32))
        return np.stack(rows)[:, None, :]

    a = np.arange(QB)[:, None]
    b = np.arange(2 * QB)[None, :]
    steps = a + QB - b
    band = (steps >= 0) & (steps <= N_STEPS)
    first = band & (b >= QB)
    masks = np.stack([first, band]).astype(np.int32)
    return buckets_of(QB - lanes), buckets_of(2 * QB - 1 - lanes), masks


def _bias_expand(rel_bias, lane_buckets, masks):
    def body(tab_ref, bk_ref, mk_ref, o_ref):
        for g in range(3):
            bk = bk_ref[g]
            for h in range(4):
                col = 4 * g + h
                per_offset = jnp.zeros((1, SKEW_W), F32)
                for k in range(N_BUCKETS):
                    per_offset = jnp.where(bk == k, tab_ref[k, col], per_offset)
                tile = pltpu.roll(jnp.broadcast_to(per_offset, (QB, SKEW_W)), 0, 1, stride=1, stride_axis=0)
                tile = tile[:, :2 * QB]
                o_ref[g, 0, h] = jnp.where(mk_ref[0] != 0, tile, NEG_INF)
                o_ref[g, 1, h] = jnp.where(mk_ref[1] != 0, tile, NEG_INF)

    return pl.pallas_call(
        body, name="bias_expand",
        in_specs=[pl.BlockSpec(memory_space=pltpu.SMEM),
                  pl.BlockSpec(memory_space=pltpu.VMEM),
                  pl.BlockSpec(memory_space=pltpu.VMEM)],
        out_shape=jax.ShapeDtypeStruct((3, 2, 4, QB, 2 * QB), F32),
    )(rel_bias, lane_buckets, masks)


def _bias_grad(ds1, ds2, ds3, lane_buckets):
    exchange = jnp.asarray(np.eye(QB, dtype=np.float32)[::-1].copy())

    def body(d1_ref, d2_ref, d3_ref, bk_ref, ex_ref, o_ref):
        for g, d_ref in enumerate((d1_ref, d2_ref, d3_ref)):
            bk = bk_ref[g]
            for h in range(4):
                flipped = jnp.dot(ex_ref[...], d_ref[h], preferred_element_type=F32,
                                  precision=lax.Precision.HIGHEST)
                padded = jnp.concatenate([flipped, jnp.zeros((QB, SKEW_W - 2 * QB), F32)], axis=1)
                skewed = pltpu.roll(padded, 0, 1, stride=1, stride_axis=0)
                per_offset = jnp.sum(skewed, axis=0, keepdims=True)
                for k in range(N_BUCKETS):
                    o_ref[k, 4 * g + h] = jnp.sum(jnp.where(bk == k, per_offset, 0.0))

    return pl.pallas_call(
        body, name="bias_grad",
        in_specs=[pl.BlockSpec(memory_space=pltpu.VMEM)] * 5,
        out_specs=pl.BlockSpec(memory_space=pltpu.SMEM),
        out_shape=jax.ShapeDtypeStruct((N_BUCKETS, N_HEADS), F32),
    )(ds1, ds2, ds3, lane_buckets, exchange)


def _scratch_sets(rows):
    return 4 if rows <= 512 else 1


def _unit_chunks(dil, size=16):
    units = [(h, r) for h in range(4) for r in range(dil)]
    return [units[i:i + size] for i in range(0, len(units), size)]


def _residue_rows(src_ref, copies, h, residue):
    buf = copies[h % len(copies)]
    buf[...] = src_ref[:, h * HD:(h + 1) * HD].astype(F32)
    return lambda r: buf[residue(r), :].astype(BF16)


def _attn_fwd(proj, bias, g):
    dil = DILATIONS[g]
    rows = QB * dil
    nsb = S // rows
    has_prev = nsb > 1

    def residue(r):
        return pl.ds(r, QB, stride=dil)

    n_sets = _scratch_sets(rows)
    n_in = 6 if has_prev else 4
    n_copied = (4 + (2 if has_prev else 0)) * n_sets

    def body(*refs):
        q_ref, kc_ref, vc_ref = refs[:3]
        kp_ref, vp_ref = refs[3:5] if has_prev else (None, None)
        b_ref = refs[n_in - 1]
        o_ref, l_ref = refs[n_in:n_in + 2]
        scr = list(refs[n_in + 2:])
        ls = [scr.pop(0) for _ in range(4)]
        copies = {name: [scr.pop(0) for _ in range(n_sets)]
                  for name in ("q", "kc", "vc", "o") + (("kp", "vp") if has_prev else ())}
        lane = lax.broadcasted_iota(jnp.int32, (QB, 128), 1)
        refs_of = {"q": q_ref, "kc": kc_ref, "vc": vc_ref, "kp": kp_ref, "vp": vp_ref}
        for chunk in _unit_chunks(dil):
            rows_of = {h: {name: _residue_rows(refs_of[name], copies[name], h, residue)
                           for name in refs_of if refs_of[name] is not None}
                       for h in sorted({h for h, _ in chunk})}

            def batch(name):
                return jnp.stack([rows_of[h][name](r) for h, r in chunk])

            q, k, v = batch("q"), batch("kc"), batch("vc")
            if has_prev:
                k = jnp.concatenate([batch("kp"), k], axis=1)
                v = jnp.concatenate([batch("vp"), v], axis=1)
                bias_b = jnp.stack([b_ref[h] for h, _ in chunk])
            else:
                bias_b = jnp.stack([b_ref[h, :, QB:] for h, _ in chunk])
            s = jnp.einsum("uqd,ukd->uqk", q, k, preferred_element_type=F32) * SCALE + bias_b
            m = jnp.max(s, axis=-1, keepdims=True)
            p = jnp.exp(s - m)
            l = jnp.sum(p, axis=-1, keepdims=True)
            o = jnp.einsum("uqk,ukd->uqd", p.astype(BF16), v, preferred_element_type=F32) / l
            lse = m + jnp.log(l)
            for i, (h, r) in enumerate(chunk):
                copies["o"][h % n_sets][residue(r), :] = o[i]
                ls[h][r * QB:(r + 1) * QB, :] = jnp.where(lane == h, lse[i], 0.0)
            for h in sorted({h for h, _ in chunk}):
                o_ref[:, h * HD:(h + 1) * HD] = copies["o"][h % n_sets][...]
        for r in range(dil):
            blk = slice(r * QB, (r + 1) * QB)
            l_ref[residue(r), :] = (ls[0][blk, :] + ls[1][blk, :]) + (ls[2][blk, :] + ls[3][blk, :])

    def row(b, n):
        return b * nsb + n

    def prev(b, n):
        return b * nsb + jnp.maximum(n - 1, 0)

    in_specs = [
        pl.BlockSpec((rows, GW), lambda b, n: (row(b, n), CB_Q + g)),
        pl.BlockSpec((rows, GW), lambda b, n: (row(b, n), CB_K + g)),
        pl.BlockSpec((rows, GW), lambda b, n: (row(b, n), CB_V + g)),
    ]
    args = [proj, proj, proj]
    scratch = [pltpu.VMEM((rows, 128), F32)] * (4 + n_copied)
    if has_prev:
        in_specs += [pl.BlockSpec((rows, GW), lambda b, n: (prev(b, n), CB_K + g)),
                     pl.BlockSpec((rows, GW), lambda b, n: (prev(b, n), CB_V + g))]
        args += [proj, proj]
    in_specs.append(pl.BlockSpec((None, None, 4, QB, 2 * QB),
                                 lambda b, n: (g, jnp.minimum(n, 1), 0, 0, 0)))
    args.append(bias)
    return pl.pallas_call(
        body, name=f"attn_fwd{g}",
        grid=(BL, nsb),
        in_specs=in_specs,
        out_specs=(pl.BlockSpec((rows, GW), lambda b, n: (row(b, n), 0)),
                   pl.BlockSpec((rows, 128), lambda b, n: (row(b, n), 0))),
        out_shape=(jax.ShapeDtypeStruct((T, GW), F32), jax.ShapeDtypeStruct((T, 128), F32)),
        scratch_shapes=scratch,
        compiler_params=pltpu.CompilerParams(vmem_limit_bytes=VMEM_LIMIT),
    )(*args)


def _attn_bwd(proj, d_out, stats, bias, dproj, g):
    dil = DILATIONS[g]
    rows = QB * dil
    nsb = S // rows
    has_prev = nsb > 1
    n_steps = nsb + 1 if has_prev else 1
    n_in = 7 + (2 if has_prev else 0)

    def residue(r):
        return pl.ds(r, QB, stride=dil)

    n_sets = _scratch_sets(rows)

    def body(*refs):
        q_ref, kc_ref, vc_ref, do_ref, st_ref, b_ref = refs[:6]
        kp_ref, vp_ref = refs[6:8] if has_prev else (None, None)
        out_ref, db_ref = refs[n_in], refs[n_in + 1]
        scr = list(refs[n_in + 2:])
        sq, sk, sv, sems = [scr.pop(0) for _ in range(4)]
        carry = scr.pop(0) if has_prev else None
        sts = scr.pop(0)
        copies = {name: [scr.pop(0) for _ in range(n_sets)]
                  for name in ("q", "kc", "vc", "do", "dq", "dk", "dv") + (("kp", "vp") if has_prev else ())}
        b, n = pl.program_id(0), pl.program_id(1)

        @pl.when((b == 0) & (n == 0))
        def _():
            db_ref[...] = jnp.zeros_like(db_ref)

        def finish(h, r, dq, dk, dv):
            for name, val in (("dq", dq), ("dk", dk), ("dv", dv)):
                copies[name][h % n_sets][residue(r), :] = val

        step = b * n_steps + n
        slot = step % 2

        def stage_copies(s, row0):
            return _column_copies([(sq.at[s], CB * (CB_Q + g)), (sk.at[s], CB * (CB_K + g)),
                                   (sv.at[s], CB * (CB_V + g))], out_ref, row0, sems.at[s])

        @pl.when((step >= 2) & ((step - 2) % n_steps >= (1 if has_prev else 0)))
        def _():
            for cp in stage_copies(slot, 0):
                cp.wait()

        def finish_head(h):
            sl = slice(h * HD, (h + 1) * HD)
            sq[slot, :, sl] = copies["dq"][h % n_sets][...].astype(BF16)
            sk[slot, :, sl] = copies["dk"][h % n_sets][...].astype(BF16)
            sv[slot, :, sl] = copies["dv"][h % n_sets][...].astype(BF16)

        def write_block(blk_idx):
            for cp in stage_copies(slot, pl.multiple_of(blk_idx * rows, rows)):
                cp.start()

            @pl.when(step == BL * n_steps - 1)
            def _():
                for s in range(2):
                    for cp in stage_copies(s, 0):
                        cp.wait()

        def carried(h, r):
            blk = slice(r * QB, (r + 1) * QB)
            return ((blk, slice(h * HD, (h + 1) * HD)), (blk, slice(GW + h * HD, GW + (h + 1) * HD)),
                    (blk, slice(2 * GW + h * HD, 2 * GW + (h + 1) * HD)))

        if has_prev:
            @pl.when(n == 0)
            def _():
                carry[...] = jnp.zeros_like(carry)

            @pl.when(n == nsb)
            def _():
                for h in range(4):
                    for r in range(dil):
                        cq, ck, cv = carried(h, r)
                        finish(h, r, carry[cq], carry[ck], carry[cv])
                    finish_head(h)
                write_block(b * nsb + nsb - 1)

        @pl.when(n < nsb)
        def _():
            for r in range(dil):
                sts[r * QB:(r + 1) * QB, :] = st_ref[residue(r), :]
            refs_of = {"q": q_ref, "kc": kc_ref, "vc": vc_ref, "do": do_ref, "kp": kp_ref, "vp": vp_ref}
            for chunk in _unit_chunks(dil):
                heads = sorted({h for h, _ in chunk})
                rows_of = {h: {name: _residue_rows(refs_of[name], copies[name], h, residue)
                               for name in refs_of if refs_of[name] is not None}
                           for h in heads}

                def batch(name):
                    return jnp.stack([rows_of[h][name](r) for h, r in chunk])

                q, k, v, do = batch("q"), batch("kc"), batch("vc"), batch("do")
                if has_prev:
                    k = jnp.concatenate([batch("kp"), k], axis=1)
                    v = jnp.concatenate([batch("vp"), v], axis=1)
                    bias_b = jnp.stack([b_ref[h] for h, _ in chunk])
                else:
                    bias_b = jnp.stack([b_ref[h, :, QB:] for h, _ in chunk])
                lse = jnp.stack([sts[r * QB:(r + 1) * QB, h:h + 1] for h, r in chunk])
                delta = jnp.stack([sts[r * QB:(r + 1) * QB, 4 + h:5 + h] for h, r in chunk])
                s = jnp.einsum("uqd,ukd->uqk", q, k, preferred_element_type=F32) * SCALE + bias_b
                p = jnp.exp(s - lse)
                ds = p * (jnp.einsum("uqd,ukd->uqk", do, v, preferred_element_type=F32) - delta)
                for h in heads:
                    mine = [ds[i] for i, (hh, _) in enumerate(chunk) if hh == h]
                    tot = mine[0]
                    for extra in mine[1:]:
                        tot = tot + extra
                    if has_prev:
                        db_ref[h] += tot
                    else:
                        db_ref[h, :, QB:] += tot
                dsb, pb = ds.astype(BF16), p.astype(BF16)
                dq = jnp.einsum("uqk,ukd->uqd", dsb, k, preferred_element_type=F32) * SCALE
                dk = jnp.einsum("uqk,uqd->ukd", dsb, q, preferred_element_type=F32) * SCALE
                dv = jnp.einsum("uqk,uqd->ukd", pb, do, preferred_element_type=F32)
                for i, (h, r) in enumerate(chunk):
                    if has_prev:
                        cq, ck, cv = carried(h, r)
                        finish(h, r, carry[cq], carry[ck] + dk[i, :QB], carry[cv] + dv[i, :QB])
                        carry[cq] = dq[i]
                        carry[ck] = dk[i, QB:]
                        carry[cv] = dv[i, QB:]
                    else:
                        finish(h, r, dq[i], dk[i], dv[i])
                for h in heads:
                    finish_head(h)
            if has_prev:
                @pl.when(n > 0)
                def _():
                    write_block(b * nsb + n - 1)
            else:
                write_block(b)

    def row(b, n):
        return b * nsb + jnp.minimum(n, nsb - 1)

    def prev(b, n):
        return b * nsb + jnp.maximum(jnp.minimum(n, nsb - 1) - 1, 0)

    in_specs = [
        pl.BlockSpec((rows, GW), lambda b, n: (row(b, n), CB_Q + g)),
        pl.BlockSpec((rows, GW), lambda b, n: (row(b, n), CB_K + g)),
        pl.BlockSpec((rows, GW), lambda b, n: (row(b, n), CB_V + g)),
        pl.BlockSpec((rows, GW), lambda b, n: (row(b, n), 0)),
        pl.BlockSpec((rows, 128), lambda b, n: (row(b, n), 0)),
        pl.BlockSpec((None, None, 4, QB, 2 * QB),
                     lambda b, n: (g, jnp.minimum(jnp.minimum(n, nsb - 1), 1), 0, 0, 0)),
    ]
    args = [proj, proj, proj, d_out, stats, bias]
    scratch = [pltpu.VMEM((2, rows, GW), BF16)] * 3 + [pltpu.SemaphoreType.DMA((2, 3))]
    if has_prev:
        in_specs += [pl.BlockSpec((rows, GW), lambda b, n: (prev(b, n), CB_K + g)),
                     pl.BlockSpec((rows, GW), lambda b, n: (prev(b, n), CB_V + g))]
        args += [proj, proj]
        scratch.append(pltpu.VMEM((rows, 3 * GW), F32))
    n_copied = (7 + (2 if has_prev else 0)) * n_sets
    scratch += [pltpu.VMEM((rows, 128), F32)] * (1 + n_copied)
    in_specs.append(pl.BlockSpec(memory_space=pl.ANY))
    args.append(dproj)
    return pl.pallas_call(
        body, name=f"attn_bwd{g}",
        grid=(BL, n_steps),
        in_specs=in_specs,
        out_specs=(pl.BlockSpec(memory_space=pl.ANY),
                   pl.BlockSpec((4, QB, 2 * QB), lambda b, n: (0, 0, 0))),
        out_shape=(jax.ShapeDtypeStruct((T, NCOL), BF16),
                   jax.ShapeDtypeStruct((4, QB, 2 * QB), F32)),
        scratch_shapes=scratch,
        input_output_aliases={len(args) - 1: 0},
        compiler_params=pltpu.CompilerParams(vmem_limit_bytes=VMEM_LIMIT),
    )(*args)


def _tail(x2, tgt2, mod3, o_g, lse_g, proj, w_ao, w_co, w_o, conv_w, conv_b, ln_g, ln_b):
    tm = 256
    per_seq = S // tm
    halo = 16

    def body(x_ref, t_ref, mod_ref, o1_ref, o2_ref, o3_ref, l1_ref, l2_ref, l3_ref,
             ga_ref, u_ref, bg_ref, cg_ref, gc_ref, ma_ref, mc_ref, up_ref, cp_ref,
             wao_ref, wco_ref, wo_ref, cw_ref, cb_ref, lg_ref, lb_ref,
             dproj_ref, dyc_ref, do_ref, st_ref, dxd_ref,
             gwo_ref, gwco_ref, gwao_ref, vec_ref,
             dga_s, dbg_s, dgm_s, sems, acc_o, acc_co, acc_ao):
        i = pl.program_id(0)
        bidx = i // per_seq
        first = (i % per_seq) == 0

        @pl.when(i == 0)
        def _():
            vec_ref[...] = jnp.zeros_like(vec_ref)

        slot = i % 2

        def column_copies(s, row0):
            return _column_copies([(dga_s.at[s], CB * CB_GA), (dbg_s.at[s], D * KB_BG), (dgm_s.at[s], D * KB_GC)],
                                  dproj_ref, row0, sems.at[s])

        @pl.when(i >= 2)
        def _():
            for cp in column_copies(slot, 0):
                cp.wait()

        l1, l2, l3 = l1_ref[...], l2_ref[...], l3_ref[...]
        mx = jnp.maximum(jnp.maximum(l1, l2), l3)
        e1, e2, e3 = jnp.exp(l1 - mx), jnp.exp(l2 - mx), jnp.exp(l3 - mx)
        esum = e1 + e2 + e3
        lse_tot = mx + jnp.log(esum)
        w1, w2, w3 = e1 / esum, e2 / esum, e3 / esum

        def per_head(wv):
            return jnp.concatenate([jnp.broadcast_to(wv[:, h:h + 1], (tm, HD)) for h in range(4)], axis=1)

        o = per_head(w1) * o1_ref[...] + per_head(w2) * o2_ref[...] + per_head(w3) * o3_ref[...]

        ga = ga_ref[...].astype(F32)
        sig_ga = _sigmoid(ga)
        silu_ga = ga * sig_ga
        a_in = (o * silu_ga).astype(BF16)
        a_out = _dot(a_in, wao_ref[...])

        u = u_ref[...].astype(F32)
        cg = cg_ref[...].astype(F32)
        z = cg * u
        zp = cp_ref[...].astype(F32) * up_ref[...].astype(F32)
        zp = jnp.where(first, 0.0, zp)
        zcat = jnp.concatenate([zp, z], axis=0)
        z1 = pltpu.roll(zcat, 1, 0)[halo:]
        z2 = pltpu.roll(zcat, 2, 0)[halo:]
        y_conv = cw_ref[0:1, :] * z2 + cw_ref[1:2, :] * z1 + cw_ref[2:3, :] * z + cb_ref[...]
        gc = gc_ref[...].astype(F32)
        sig_gc = _sigmoid(gc)
        silu_gc = gc * sig_gc
        bg = bg_ref[...].astype(F32)
        bg_yc = bg * y_conv
        s_in = (bg_yc * silu_gc).astype(BF16)
        s_out = _dot(s_in, wco_ref[...])

        sa = _sigmoid(ma_ref[...].astype(F32))
        sc = _sigmoid(mc_ref[...].astype(F32))
        merged = (sa * a_out + sc * s_out).astype(BF16)
        y = _dot(merged, wo_ref[...])
        gate1 = 1.0 + mod_ref[0, 2:3, :]
        xv = x_ref[...]
        resid = ALPHA * xv + gate1 * y
        mu = jnp.mean(resid, axis=1, keepdims=True)
        xc = resid - mu
        var = jnp.mean(xc * xc, axis=1, keepdims=True)
        rstd = lax.rsqrt(var + LN_EPS)
        xhat = xc * rstd
        lg = lg_ref[...]
        err = xhat * lg + lb_ref[...] - t_ref[...]
        vec_ref[3:4, :] += (0.5 / D) * jnp.sum(err * err, axis=0, keepdims=True)

        vec_ref[1:2, :] += (1.0 / D) * jnp.sum(err * xhat, axis=0, keepdims=True)
        vec_ref[2:3, :] += (1.0 / D) * jnp.sum(err, axis=0, keepdims=True)
        dxh = err * (lg * (1.0 / D))
        dres = rstd * (dxh - jnp.mean(dxh, axis=1, keepdims=True)
                       - xhat * jnp.mean(dxh * xhat, axis=1, keepdims=True))
        dxd_ref[...] = ALPHA * dres
        dgate = jnp.sum(dres * y, axis=0, keepdims=True)
        vec_ref[4:5, :] += jnp.where(bidx == 0, dgate, 0.0)
        vec_ref[5:6, :] += jnp.where(bidx == 1, dgate, 0.0)
        dy = (dres * gate1).astype(BF16)

        dmerged = _dot_nt(dy, wo_ref[...])
        da_out_f = dmerged * sa
        ds_out_f = dmerged * sc
        da_out = da_out_f.astype(BF16)
        ds_out = ds_out_f.astype(BF16)
        dgm_s[slot, :, 2 * D:3 * D] = (ds_out_f * s_out * (1.0 - sc)).astype(BF16)
        dgm_s[slot, :, D:2 * D] = (da_out_f * a_out * (1.0 - sa)).astype(BF16)
        da_in = _dot_nt(da_out, wao_ref[...])
        ds_in = _dot_nt(ds_out, wco_ref[...])

        d_o = da_in * silu_ga
        do_ref[...] = d_o.astype(BF16)
        dga_s[slot] = (da_in * o * (sig_ga + silu_ga * (1.0 - sig_ga))).astype(BF16)
        lane = lax.broadcasted_iota(jnp.int32, (tm, 128), 1)
        stats = lse_tot
        od = o * d_o
        for h in range(4):
            delta = jnp.sum(od[:, h * HD:(h + 1) * HD], axis=1, keepdims=True)
            stats = jnp.where(lane == 4 + h, delta, stats)
        st_ref[...] = stats

        ds_silu = ds_in * silu_gc
        dbg_s[slot] = (ds_silu * y_conv).astype(BF16)
        dyc = ds_silu * bg
        dyc_ref[...] = dyc
        vec_ref[0:1, :] += jnp.sum(dyc, axis=0, keepdims=True)
        dgm_s[slot, :, 0:D] = (ds_in * bg_yc * (sig_gc + silu_gc * (1.0 - sig_gc))).astype(BF16)

        @pl.when(i == 0)
        def _():
            acc_o[...] = jnp.zeros_like(acc_o)
            acc_co[...] = jnp.zeros_like(acc_co)
            acc_ao[...] = jnp.zeros_like(acc_ao)

        acc_o[...] += _dot_tn(merged, dy)
        acc_co[...] += _dot_tn(s_in, ds_out)
        acc_ao[...] += _dot_tn(a_in, da_out)

        for cp in column_copies(slot, pl.multiple_of(i * tm, tm)):
            cp.start()

        @pl.when(i == T // tm - 1)
        def _():
            gwo_ref[...] = acc_o[...].astype(BF16)
            gwco_ref[...] = acc_co[...].astype(BF16)
            gwao_ref[...] = acc_ao[...].astype(BF16)
            for s in range(2):
                for cp in column_copies(s, 0):
                    cp.wait()

    def tile(width, cblk=0):
        return pl.BlockSpec((tm, width), lambda i: (i, cblk))

    def whole(shape):
        return pl.BlockSpec(shape, lambda i: tuple(0 for _ in shape))

    def once(shape):
        return pl.BlockSpec(shape, lambda i: tuple(0 for _ in shape), pipeline_mode=pl.Buffered(1))

    prev_rows = lambda i: (jnp.maximum(i * (tm // halo) - 1, 0),)
    in_specs = [
        tile(D), tile(D), pl.BlockSpec((1, 3, D), lambda i: (i // per_seq, 0, 0)),
        tile(GW), tile(GW), tile(GW), tile(128), tile(128), tile(128),
        tile(GW, CB_GA), tile(D, KB_U), tile(D, KB_BG), tile(D, KB_CG), tile(D, KB_GC),
        tile(D, KB_MA), tile(D, KB_MC),
        pl.BlockSpec((halo, D), lambda i: (*prev_rows(i), KB_U)),
        pl.BlockSpec((halo, D), lambda i: (*prev_rows(i), KB_CG)),
        whole((GW, D)), whole((D, D)), whole((D, D)),
        whole((3, D)), whole((1, D)), whole((1, D)), whole((1, D)),
    ]
    out_specs = (
        pl.BlockSpec(memory_space=pl.ANY), tile(D), tile(GW), tile(128), tile(D),
        once((D, D)), once((D, D)), once((GW, D)),
        pl.BlockSpec((8, D), lambda i: (0, 0)),
    )
    out_shape = (
        jax.ShapeDtypeStruct((T, NCOL), BF16),
        jax.ShapeDtypeStruct((T, D), F32),
        jax.ShapeDtypeStruct((T, GW), BF16),
        jax.ShapeDtypeStruct((T, 128), F32),
        jax.ShapeDtypeStruct((T, D), F32),
        jax.ShapeDtypeStruct((D, D), BF16),
        jax.ShapeDtypeStruct((D, D), BF16),
        jax.ShapeDtypeStruct((GW, D), BF16),
        jax.ShapeDtypeStruct((8, D), F32),
    )
    return pl.pallas_call(
        body, name="tail",
        grid=(T // tm,),
        in_specs=in_specs, out_specs=out_specs, out_shape=out_shape,
        scratch_shapes=[pltpu.VMEM((2, tm, GW), BF16), pltpu.VMEM((2, tm, D), BF16), pltpu.VMEM((2, tm, 3 * D), BF16),
                        pltpu.SemaphoreType.DMA((2, 3)),
                        pltpu.VMEM((D, D), F32), pltpu.VMEM((D, D), F32), pltpu.VMEM((GW, D), F32)],
        compiler_params=pltpu.CompilerParams(vmem_limit_bytes=VMEM_LIMIT_TAIL),
    )(x2, tgt2, mod3, *o_g, *lse_g, proj, proj, proj, proj, proj, proj, proj, proj, proj,
      w_ao, w_co, w_o, conv_w, conv_b, ln_g, ln_b)


def _conv_bwd(dyc, proj, conv_w, dproj):
    tm = 512
    per_seq = S // tm

    def body(d_ref, dn_ref, u_ref, c_ref, cw_ref, _, dproj_ref, g_ref, du_s, dc_s, sems):
        i = pl.program_id(0)
        last = (i % per_seq) == per_seq - 1

        @pl.when(i == 0)
        def _():
            g_ref[...] = jnp.zeros_like(g_ref)

        slot = i % 2

        def column_copies(s, row0):
            return _column_copies([(du_s.at[s], D * KB_U), (dc_s.at[s], D * KB_CG)], dproj_ref, row0, sems.at[s])

        @pl.when(i >= 2)
        def _():
            for cp in column_copies(slot, 0):
                cp.wait()

        d = d_ref[...]
        dn = jnp.where(last, 0.0, dn_ref[...])
        dcat = jnp.concatenate([d, dn], axis=0)
        d1 = pltpu.roll(dcat, tm + 8 - 1, 0)[:tm]
        d2 = pltpu.roll(dcat, tm + 8 - 2, 0)[:tm]
        dz = cw_ref[2:3, :] * d + cw_ref[1:2, :] * d1 + cw_ref[0:1, :] * d2
        u = u_ref[...].astype(F32)
        cg = c_ref[...].astype(F32)
        du_s[slot] = (dz * cg).astype(BF16)
        dc_s[slot] = (dz * u).astype(BF16)
        for cp in column_copies(slot, pl.multiple_of(i * tm, tm)):
            cp.start()

        z = cg * u
        g_ref[0:1, :] += jnp.sum(d2 * z, axis=0, keepdims=True)
        g_ref[1:2, :] += jnp.sum(d1 * z, axis=0, keepdims=True)
        g_ref[2:3, :] += jnp.sum(d * z, axis=0, keepdims=True)

        @pl.when(i == T // tm - 1)
        def _():
            for s in range(2):
                for cp in column_copies(s, 0):
                    cp.wait()

    n_tiles = T // tm
    next_rows = lambda i: jnp.minimum((i + 1) * (tm // 8), T // 8 - 1)
    return pl.pallas_call(
        body, name="conv_bwd",
        grid=(n_tiles,),
        in_specs=[pl.BlockSpec((tm, D), lambda i: (i, 0)),
                  pl.BlockSpec((8, D), lambda i: (next_rows(i), 0)),
                  pl.BlockSpec((tm, D), lambda i: (i, KB_U)),
                  pl.BlockSpec((tm, D), lambda i: (i, KB_CG)),
                  pl.BlockSpec((3, D), lambda i: (0, 0)),
                  pl.BlockSpec(memory_space=pl.ANY)],
        out_specs=(pl.BlockSpec(memory_space=pl.ANY),
                   pl.BlockSpec((8, D), lambda i: (0, 0))),
        out_shape=(jax.ShapeDtypeStruct((T, NCOL), BF16),
                   jax.ShapeDtypeStruct((8, D), F32)),
        scratch_shapes=[pltpu.VMEM((2, tm, D), BF16), pltpu.VMEM((2, tm, D), BF16), pltpu.SemaphoreType.DMA((2, 2))],
        input_output_aliases={5: 0},
        compiler_params=pltpu.CompilerParams(vmem_limit_bytes=VMEM_LIMIT),
    )(dyc, dyc, proj, proj, conv_w, dproj)


def _dh_dx(dproj, w_in_all, x2, dxd, mod3, chip_sums, hops=(), parts0=None):
    tm = 1024
    per_seq = S // tm
    n = len(chip_sums)
    n_in = 5 + n + (0 if parts0 is None else 1)

    def body(*refs):
        d_ref, w_ref, x_ref, dxd_ref, mod_ref = refs[:5]
        ins = refs[5:5 + n]
        gx_ref, vec_ref = refs[n_in:n_in + 2]
        outs = refs[n_in + 2:n_in + 2 + n]
        acc, send_sems, recv_sems, local_sems = refs[n_in + 2 + n:]
        jj, i = pl.program_id(0), pl.program_id(1)

        @pl.when((i == 0) & (jj == 0))
        def _():
            vec_ref[...] = jnp.zeros_like(vec_ref)
            if n:
                sends, _, mine = _chip_copies(ins, outs, send_sems, recv_sems, local_sems, hops)
                for cp in sends + mine:
                    cp.start()

        if n:
            @pl.when((i == T // tm - 1) & (jj == N_DEV - 1))
            def _():
                sends, arrivals, mine = _chip_copies(ins, outs, send_sems, recv_sems, local_sems, hops)
                for cp in arrivals:
                    cp.wait_recv()
                for cp in sends:
                    cp.wait_send()
                for cp in mine:
                    cp.wait()

        def partial():
            return _dot_nt(d_ref[...], w_ref[...])

        @pl.when(jj == 0)
        def _():
            acc[i] = partial()

        @pl.when((jj > 0) & (jj < N_DEV - 1))
        def _():
            acc[i] += partial()

        @pl.when(jj == N_DEV - 1)
        def _():
            dh = acc[i] + partial()
            bidx = i // per_seq
            gx_ref[...] = dxd_ref[...] + dh * (1.0 + mod_ref[0, 1:2, :])
            dshift = jnp.sum(dh, axis=0, keepdims=True)
            dscale = jnp.sum(dh * x_ref[...], axis=0, keepdims=True)
            vec_ref[0:1, :] += jnp.where(bidx == 0, dshift, 0.0)
            vec_ref[1:2, :] += jnp.where(bidx == 1, dshift, 0.0)
            vec_ref[2:3, :] += jnp.where(bidx == 0, dscale, 0.0)
            vec_ref[3:4, :] += jnp.where(bidx == 1, dscale, 0.0)

    def last_pass(jj, i):
        return jnp.where(jj == N_DEV - 1, i, 0)

    any_spec = pl.BlockSpec(memory_space=pl.ANY)
    res = pl.pallas_call(
        body, name="dh_dx",
        grid=(N_DEV, T // tm),
        in_specs=[
            pl.BlockSpec((tm, SHARD), lambda jj, i: (i, jj)),
            pl.BlockSpec((None, D, SHARD), lambda jj, i: (jj, 0, 0)),
            pl.BlockSpec((tm, D), lambda jj, i: (last_pass(jj, i), 0)),
            pl.BlockSpec((tm, D), lambda jj, i: (last_pass(jj, i), 0)),
            pl.BlockSpec((1, 3, D), lambda jj, i: (last_pass(jj, i) // per_seq, 0, 0))]
                 + [any_spec] * (n_in - 5),
        out_specs=(pl.BlockSpec((tm, D), lambda jj, i: (last_pass(jj, i), 0)),
                   pl.BlockSpec((8, D), lambda jj, i: (0, 0))) + (any_spec,) * n,
        out_shape=(jax.ShapeDtypeStruct((T, D), F32), jax.ShapeDtypeStruct((8, D), F32))
                  + tuple(jax.ShapeDtypeStruct(a.shape, a.dtype) for a in chip_sums),
        scratch_shapes=[pltpu.VMEM((T // tm, tm, D), F32), pltpu.SemaphoreType.DMA((max(3 * n, 1),)),
                        pltpu.SemaphoreType.DMA((max(3 * n, 1),)), pltpu.SemaphoreType.DMA((max(n, 1),))],
        input_output_aliases={} if parts0 is None else {5 + n: 2},
        compiler_params=pltpu.CompilerParams(vmem_limit_bytes=VMEM_LIMIT),
    )(dproj, w_in_all, x2, dxd, mod3, *chip_sums, *([] if parts0 is None else [parts0]))
    return res[0], res[1], res[2:]


def _adam_step(g, w, m, v):
    nm = ADAM_B1 * m + (1.0 - ADAM_B1) * g
    nv = ADAM_B2 * v + (1.0 - ADAM_B2) * (g * g)
    m_hat = nm / (1.0 - ADAM_B1 ** ADAM_STEP)
    v_hat = nv / (1.0 - ADAM_B2 ** ADAM_STEP)
    return -ADAM_LR * (m_hat / (jnp.sqrt(v_hat) + ADAM_EPS) + ADAM_WD * w), nm, nv


def _adamw(parts, w, m, v, name, row_tile=None):
    n_parts, rows, cols = parts.shape
    tr = rows if row_tile is None else row_tile

    def body(p_ref, w_ref, m_ref, v_ref, g_ref, d_ref, nm_ref, nv_ref):
        g = p_ref[0].astype(F32)
        for s in range(1, n_parts):
            g = g + p_ref[s].astype(F32)
        g_ref[...] = g
        d_ref[...], nm_ref[...], nv_ref[...] = _adam_step(g, w_ref[...], m_ref[...], v_ref[...])

    blk = pl.BlockSpec((tr, cols), lambda i: (i, 0))
    shp = jax.ShapeDtypeStruct((rows, cols), F32)
    return pl.pallas_call(
        body, name=name,
        grid=(rows // tr,),
        in_specs=[pl.BlockSpec((n_parts, tr, cols), lambda i: (0, i, 0)), blk, blk, blk],
        out_specs=(blk, blk, blk, blk),
        out_shape=(shp, shp, shp, shp),
        compiler_params=pltpu.CompilerParams(vmem_limit_bytes=VMEM_LIMIT),
    )(parts, w, m, v)


def _multi_adamw(parts_list, params, name):
    n = len(params)
    flat = [t for wmv in params for t in wmv]

    def body(*refs):
        parts, ins, outs = refs[:n], refs[n:4 * n], refs[4 * n:]
        for p in range(n):
            g = parts[p][0].astype(F32)
            for s in range(1, parts[p].shape[0]):
                g = g + parts[p][s].astype(F32)
            w_ref, m_ref, v_ref = ins[3 * p:3 * p + 3]
            g_ref, d_ref, nm_ref, nv_ref = outs[4 * p:4 * p + 4]
            g_ref[...] = g
            d_ref[...], nm_ref[...], nv_ref[...] = _adam_step(g, w_ref[...], m_ref[...], v_ref[...])

    out_shape = []
    for w, _, _ in params:
        out_shape += [jax.ShapeDtypeStruct(w.shape, F32)] * 4
    res = pl.pallas_call(body, name=name, out_shape=tuple(out_shape))(*parts_list, *flat)
    return [res[4 * p:4 * p + 4] for p in range(n)]


def _small_updates(small_g, dmod_all, rel_parts, params):
    flat = [t for wmv in params for t in wmv]

    def body(sg_ref, dm_ref, rp_ref, *refs):
        ins, outs = refs[:len(flat)], refs[len(flat):]

        def over_devices(row):
            tot = sg_ref[0, row:row + 1, :]
            for s in range(1, N_DEV):
                tot = tot + sg_ref[s, row:row + 1, :]
            return tot

        g_b_ada = dm_ref[0:1, :]
        for r in range(1, N_DEV * BL):
            g_b_ada = g_b_ada + dm_ref[r:r + 1, :]
        g_rel = rp_ref[0]
        for s in range(1, N_DEV):
            g_rel = g_rel + rp_ref[s]
        grads = [g_b_ada, over_devices(0), g_rel, over_devices(1), over_devices(2)]
        outs[0][...] = jnp.sum(over_devices(3), axis=1, keepdims=True)
        for p, g in enumerate(grads):
            w_ref, m_ref, v_ref = ins[3 * p:3 * p + 3]
            g_ref, d_ref, nm_ref, nv_ref = outs[1 + 4 * p:5 + 4 * p]
            g_ref[...] = g
            d_ref[...], nm_ref[...], nv_ref[...] = _adam_step(g, w_ref[...], m_ref[...], v_ref[...])

    out_shape = [jax.ShapeDtypeStruct((1, 1), F32)]
    for w, _, _ in params:
        out_shape += [jax.ShapeDtypeStruct(w.shape, F32)] * 4
    res = pl.pallas_call(body, name="small_updates", out_shape=tuple(out_shape))(small_g, dmod_all, rel_parts, *flat)
    return res[0], [res[1 + 4 * p:5 + 4 * p] for p in range(len(params))]


def _attn_fwd_dense(proj, bias):
    nq = 4
    rows = nq * QB
    nsb = S // rows

    def body(q_ref, k_ref, v_ref, kp_ref, vp_ref, b_ref, o_ref, l_ref, ls0, ls1, ls2, ls3):
        ls = [ls0, ls1, ls2, ls3]
        n = pl.program_id(1)
        lane = lax.broadcasted_iota(jnp.int32, (QB, 128), 1)
        units = [(h, j) for h in range(4) for j in range(nq)]

        def keys(cur_ref, prev_ref, h, j):
            sl = slice(h * HD, (h + 1) * HD)
            if j == 0:
                return jnp.concatenate([prev_ref[:, sl], cur_ref[0:QB, sl]], axis=0)
            return cur_ref[(j - 1) * QB:(j + 1) * QB, sl]

        q = jnp.stack([q_ref[j * QB:(j + 1) * QB, h * HD:(h + 1) * HD] for h, j in units])
        k = jnp.stack([keys(k_ref, kp_ref, h, j) for h, j in units])
        v = jnp.stack([keys(v_ref, vp_ref, h, j) for h, j in units])
        bias_b = jnp.stack([b_ref[jnp.minimum(n, 1), h] if j == 0 else b_ref[1, h] for h, j in units])
        s = jnp.einsum("uqd,ukd->uqk", q, k, preferred_element_type=F32) * SCALE + bias_b
        m = jnp.max(s, axis=-1, keepdims=True)
        p = jnp.exp(s - m)
        l = jnp.sum(p, axis=-1, keepdims=True)
        o = jnp.einsum("uqk,ukd->uqd", p.astype(BF16), v, preferred_element_type=F32) / l
        lse = m + jnp.log(l)
        for i, (h, j) in enumerate(units):
            o_ref[j * QB:(j + 1) * QB, h * HD:(h + 1) * HD] = o[i]
            ls[h][j * QB:(j + 1) * QB, :] = jnp.where(lane == h, lse[i], 0.0)
        l_ref[...] = (ls[0][...] + ls[1][...]) + (ls[2][...] + ls[3][...])

    def row(b, n):
        return b * nsb + n

    def prev(b, n):
        return jnp.maximum((b * nsb + n) * nq - 1, 0)

    in_specs = [
        pl.BlockSpec((rows, GW), lambda b, n: (row(b, n), CB_Q)),
        pl.BlockSpec((rows, GW), lambda b, n: (row(b, n), CB_K)),
        pl.BlockSpec((rows, GW), lambda b, n: (row(b, n), CB_V)),
        pl.BlockSpec((QB, GW), lambda b, n: (prev(b, n), CB_K)),
        pl.BlockSpec((QB, GW), lambda b, n: (prev(b, n), CB_V)),
        pl.BlockSpec((None, 2, 4, QB, 2 * QB), lambda b, n: (0, 0, 0, 0, 0)),
    ]
    return pl.pallas_call(
        body, name="attn_fwd0",
        grid=(BL, nsb),
        in_specs=in_specs,
        out_specs=(pl.BlockSpec((rows, GW), lambda b, n: (row(b, n), 0)),
                   pl.BlockSpec((rows, 128), lambda b, n: (row(b, n), 0))),
        out_shape=(jax.ShapeDtypeStruct((T, GW), F32), jax.ShapeDtypeStruct((T, 128), F32)),
        scratch_shapes=[pltpu.VMEM((rows, 128), F32)] * 4,
        compiler_params=pltpu.CompilerParams(vmem_limit_bytes=VMEM_LIMIT),
    )(proj, proj, proj, proj, proj, bias)


def _attn_bwd_dense(proj, d_out, stats, bias, dproj):
    nq = 4
    rows = nq * QB
    nsb = S // rows
    cols_q, cols_k, cols_v = CB * CB_Q, CB * CB_K, CB * CB_V

    def body(q_ref, k_ref, v_ref, do_ref, st_ref, kp_ref, vp_ref, b_ref, _, out_ref, db_ref,
             sq, sk, sv, carry, sems):
        b, n = pl.program_id(0), pl.program_id(1)
        units = [(h, j) for h in range(4) for j in range(nq)]

        @pl.when((b == 0) & (n == 0))
        def _():
            db_ref[...] = jnp.zeros_like(db_ref)

        @pl.when(n == 0)
        def _():
            carry[...] = jnp.zeros_like(carry)

        step = b * (nsb + 1) + n
        slot = step % 2

        def block_copies(s, position, block):
            part = pl.ds(position * QB, QB)
            return _column_copies([(sq.at[s, part], cols_q), (sk.at[s, part], cols_k), (sv.at[s, part], cols_v)],
                                  out_ref, pl.multiple_of(block * QB, QB), sems.at[s, position])

        def wait_blocks(s, positions):
            for position in positions:
                for cp in block_copies(s, position, 0):
                    cp.wait()

        before = (step - 2) % (nsb + 1)

        @pl.when((step >= 2) & (before > 0))
        def _():
            wait_blocks(slot, [0])

        @pl.when((step >= 2) & (before < nsb))
        def _():
            wait_blocks(slot, range(1, nq))

        def write(first_block, position, count):
            for j in range(count):
                for cp in block_copies(slot, position + j, first_block + j):
                    cp.start()

        @pl.when(n == nsb)
        def _():
            sq[slot, 0:QB, :] = carry[:, 0:GW].astype(BF16)
            sk[slot, 0:QB, :] = carry[:, GW:2 * GW].astype(BF16)
            sv[slot, 0:QB, :] = carry[:, 2 * GW:3 * GW].astype(BF16)
            write((b + 1) * nsb * nq - 1, 0, 1)

            @pl.when(b == BL - 1)
            def _():
                wait_blocks(slot, [0])
                wait_blocks(1 - slot, range(nq))

        @pl.when(n < nsb)
        def _():
            def keys(cur_ref, prev_ref, h, j):
                sl = slice(h * HD, (h + 1) * HD)
                if j == 0:
                    return jnp.concatenate([prev_ref[:, sl], cur_ref[0:QB, sl]], axis=0)
                return cur_ref[(j - 1) * QB:(j + 1) * QB, sl]

            def block(ref, h, j):
                return ref[j * QB:(j + 1) * QB, h * HD:(h + 1) * HD]

            q = jnp.stack([block(q_ref, h, j) for h, j in units])
            do = jnp.stack([block(do_ref, h, j) for h, j in units])
            k = jnp.stack([keys(k_ref, kp_ref, h, j) for h, j in units])
            v = jnp.stack([keys(v_ref, vp_ref, h, j) for h, j in units])
            bias_b = jnp.stack([b_ref[jnp.minimum(n, 1), h] if j == 0 else b_ref[1, h] for h, j in units])
            lse = jnp.stack([st_ref[j * QB:(j + 1) * QB, h:h + 1] for h, j in units])
            delta = jnp.stack([st_ref[j * QB:(j + 1) * QB, 4 + h:5 + h] for h, j in units])
            s = jnp.einsum("uqd,ukd->uqk", q, k, preferred_element_type=F32) * SCALE + bias_b
            p = jnp.exp(s - lse)
            ds = p * (jnp.einsum("uqd,ukd->uqk", do, v, preferred_element_type=F32) - delta)
            for h in range(4):
                tot = ds[h * nq]
                for j in range(1, nq):
                    tot = tot + ds[h * nq + j]
                db_ref[h] += tot
            dsb, pb = ds.astype(BF16), p.astype(BF16)
            dq = jnp.einsum("uqk,ukd->uqd", dsb, k, preferred_element_type=F32) * SCALE
            dk = jnp.einsum("uqk,uqd->ukd", dsb, q, preferred_element_type=F32) * SCALE
            dv = jnp.einsum("uqk,uqd->ukd", pb, do, preferred_element_type=F32)
            for h in range(4):
                sl = slice(h * HD, (h + 1) * HD)
                u0, last = h * nq, h * nq + nq - 1
                sq[slot, 0:QB, sl] = carry[:, sl].astype(BF16)
                sk[slot, 0:QB, sl] = (carry[:, GW + h * HD:GW + (h + 1) * HD] + dk[u0, :QB]).astype(BF16)
                sv[slot, 0:QB, sl] = (carry[:, 2 * GW + h * HD:2 * GW + (h + 1) * HD] + dv[u0, :QB]).astype(BF16)
                for j in range(nq - 1):
                    pos = slice((j + 1) * QB, (j + 2) * QB)
                    sq[slot, pos, sl] = dq[u0 + j].astype(BF16)
                    sk[slot, pos, sl] = (dk[u0 + j, QB:] + dk[u0 + j + 1, :QB]).astype(BF16)
                    sv[slot, pos, sl] = (dv[u0 + j, QB:] + dv[u0 + j + 1, :QB]).astype(BF16)
                carry[:, sl] = dq[last]
                carry[:, GW + h * HD:GW + (h + 1) * HD] = dk[last, QB:]
                carry[:, 2 * GW + h * HD:2 * GW + (h + 1) * HD] = dv[last, QB:]

            @pl.when(n == 0)
            def _():
                write(b * nsb * nq, 1, nq - 1)

            @pl.when(n > 0)
            def _():
                write((b * nsb + n) * nq - 1, 0, nq)

    def row(b, n):
        return b * nsb + jnp.minimum(n, nsb - 1)

    def prev(b, n):
        return jnp.maximum(row(b, n) * nq - 1, 0)

    in_specs = [
        pl.BlockSpec((rows, GW), lambda b, n: (row(b, n), CB_Q)),
        pl.BlockSpec((rows, GW), lambda b, n: (row(b, n), CB_K)),
        pl.BlockSpec((rows, GW), lambda b, n: (row(b, n), CB_V)),
        pl.BlockSpec((rows, GW), lambda b, n: (row(b, n), 0)),
        pl.BlockSpec((rows, 128), lambda b, n: (row(b, n), 0)),
        pl.BlockSpec((QB, GW), lambda b, n: (prev(b, n), CB_K)),
        pl.BlockSpec((QB, GW), lambda b, n: (prev(b, n), CB_V)),
        pl.BlockSpec((None, 2, 4, QB, 2 * QB), lambda b, n: (0, 0, 0, 0, 0)),
        pl.BlockSpec(memory_space=pl.ANY),
    ]
    return pl.pallas_call(
        body, name="attn_bwd0",
        grid=(BL, nsb + 1),
        in_specs=in_specs,
        out_specs=(pl.BlockSpec(memory_space=pl.ANY),
                   pl.BlockSpec((4, QB, 2 * QB), lambda b, n: (0, 0, 0))),
        out_shape=(jax.ShapeDtypeStruct((T, NCOL), BF16),
                   jax.ShapeDtypeStruct((4, QB, 2 * QB), F32)),
        scratch_shapes=[pltpu.VMEM((2, rows, GW), BF16)] * 3
                       + [pltpu.VMEM((QB, 3 * GW), F32), pltpu.SemaphoreType.DMA((2, nq, 3))],
        input_output_aliases={8: 0},
        compiler_params=pltpu.CompilerParams(vmem_limit_bytes=VMEM_LIMIT),
    )(proj, proj, proj, d_out, stats, proj, proj, bias, dproj)


def _attention_forward(proj, rel_bias):
    expand_lanes, grad_lanes, masks = (jnp.asarray(t) for t in _bucket_maps())
    bias = _bias_expand(rel_bias, expand_lanes, masks)
    fwd = [_attn_fwd_dense(proj, bias)] + [_attn_fwd(proj, bias, g) for g in (1, 2)]
    return bias, grad_lanes, [f[0] for f in fwd], [f[1] for f in fwd]


def _local_step(x2, tgt2, mod3, h, proj, attn, w_ao, w_co, w_o, conv_w, conv_b, ln_g, ln_b):
    bias, buckets, o_g, lse_g = attn

    (dproj, dyc, d_o, stats, dxd, gw_o, gw_co, gw_ao, tail_vec) = _tail(
        x2, tgt2, mod3, o_g, lse_g, proj, w_ao, w_co, w_o, conv_w, conv_b, ln_g, ln_b)

    dproj, db = _attn_bwd_dense(proj, d_o, stats, bias, dproj)
    dbias = [db]
    for g in (1, 2):
        dproj, db = _attn_bwd(proj, d_o, stats, bias, dproj, g)
        dbias.append(db)
    g_rel_bias = _bias_grad(*dbias, buckets)
    dproj, conv_vec = _conv_bwd(dyc, proj, conv_w, dproj)

    gw_ao = jnp.transpose(gw_ao.reshape(GW, N_DEV, D // N_DEV), (1, 0, 2))
    return dproj, dxd, gw_ao, gw_co, gw_o, conv_vec, g_rel_bias, tail_vec


def kernel(x, c, w_ada, b_ada, w_in, conv_w, conv_b, rel_bias, w_attn_out, w_conv_out, w_o, ln_g, ln_b, loss_target, m_w_ada, m_b_ada, m_w_in, m_conv_w, m_conv_b, m_rel_bias, m_w_attn_out, m_w_conv_out, m_w_o, m_ln_g, m_ln_b, v_w_ada, v_b_ada, v_w_in, v_conv_w, v_conv_b, v_rel_bias, v_w_attn_out, v_w_conv_out, v_w_o, v_ln_g, v_ln_b):
    me = _my_index()
    x2 = x.reshape(T, D)
    tgt2 = loss_target.reshape(T, D)

    b_cols = lax.dynamic_slice(b_ada, (0, me * ADA_SHARD), (1, ADA_SHARD))
    c_g, mod_in = _mod_exchange(jnp.pad(c, ((0, 8 - BL), (0, 0))), w_ada[0], b_cols)
    c_all = c_g[:, 0:BL, :].reshape(N_DEV * BL, D)
    mod3 = jnp.transpose(mod_in[:, 0:BL, :], (1, 0, 2)).reshape(BL, 3, D)

    h = _prep_h(x2, mod3)
    rows_shape = jax.ShapeDtypeStruct((N_DEV, D // N_DEV, D), BF16)
    proj, w_in_all, (w_ao_g, w_co_g, w_o_g, conv_w_g) = _gather_proj(
        _shard_order(), h, w_in[0].astype(BF16), 1024,
        ([w_attn_out[0].astype(BF16), w_conv_out[0].astype(BF16), w_o[0].astype(BF16), conv_w[0]],
         [jax.ShapeDtypeStruct((N_DEV, GW, D // N_DEV), BF16), rows_shape, rows_shape,
          jax.ShapeDtypeStruct((N_DEV, 3, D // N_DEV), F32)]))

    attn = _attention_forward(proj, rel_bias)
    w_ao_full = jnp.transpose(w_ao_g, (1, 0, 2)).reshape(GW, D)
    w_co_full = w_co_g.reshape(D, D)
    w_o_full = w_o_g.reshape(D, D)
    conv_w_full = jnp.transpose(conv_w_g, (1, 0, 2)).reshape(3, D)

    (dproj, dxd, gw_ao, gw_co, gw_o, conv_vec, g_rel_bias, tail_vec) = _local_step(
        x2, tgt2, mod3, h, proj, attn, w_ao_full, w_co_full, w_o_full,
        conv_w_full, conv_b, ln_g, ln_b)

    g_conv_w_blocks = jnp.transpose(conv_vec[0:3].reshape(3, N_DEV, D // N_DEV), (1, 0, 2))
    partials = [gw_ao, gw_co.reshape(N_DEV, D // N_DEV, D), gw_o.reshape(N_DEV, D // N_DEV, D), g_conv_w_blocks]
    w_in_sums, w_in_parts, sib = _gw_in_pair(
        _slice_order(), h, dproj, partials,
        [jax.ShapeDtypeStruct((4, GW, D // N_DEV), BF16),
         jax.ShapeDtypeStruct((4, D // N_DEV, D), BF16),
         jax.ShapeDtypeStruct((4, D // N_DEV, D), BF16),
         jax.ShapeDtypeStruct((4, 3, D // N_DEV), F32)])
    core = lax.axis_index("c").astype(jnp.int32).reshape(1)
    chip_sums = [w_in_sums] + list(_pair_add(core, partials, sib))
    hops = [(3,)] + [(1, 2, 3)] * 4
    grad_x, mod_vec, (r_in, r_ao, r_co, r_o, r_cw) = _dh_dx(
        dproj, w_in_all, x2, dxd, mod3, chip_sums, hops, w_in_parts)

    small = jnp.concatenate([
        tail_vec[0:4],
        jnp.pad(g_rel_bias.reshape(1, N_BUCKETS * N_HEADS), ((0, 0), (0, D - N_BUCKETS * N_HEADS))),
        jnp.zeros((3, D), F32)], axis=0)
    dmod = jnp.concatenate([mod_vec[0:2], mod_vec[2:4], tail_vec[4:6]], axis=1)
    small_g, dmod_g = _all_gather(
        [small, dmod],
        [jax.ShapeDtypeStruct((N_DEV, 8, D), F32), jax.ShapeDtypeStruct((N_DEV, BL, 3 * D), F32)],
        "gather_small")
    dmod_all = dmod_g.reshape(N_DEV * BL, 3 * D)
    small_names = ["b_ada", "conv_b", "rel_bias", "ln_g", "ln_b"]
    small_params = [(b_ada, m_b_ada, v_b_ada), (conv_b, m_conv_b, v_conv_b), (rel_bias, m_rel_bias, v_rel_bias),
                    (ln_g, m_ln_g, v_ln_g), (ln_b, m_ln_b, v_ln_b)]
    loss, small_res = _small_updates(
        small_g, dmod_all, small_g[:, 4, :N_BUCKETS * N_HEADS].reshape(N_DEV, N_BUCKETS, N_HEADS), small_params)
    loss = loss.reshape(())

    dmod_cols = lax.dynamic_slice(dmod_all, (0, me * ADA_SHARD), (N_DEV * BL, ADA_SHARD))
    res = {
        "w_ada": tuple(t[None] for t in _w_ada_update(jnp.transpose(c_all), dmod_cols,
                                                      w_ada[0], m_w_ada[0], v_w_ada[0])),
        "w_in": tuple(t[None] for t in _adamw(r_in, w_in[0], m_w_in[0], v_w_in[0], "adam_w_in", 128)),
    }
    mid_names = ["conv_w", "w_attn_out", "w_conv_out", "w_o"]
    mid_parts = [r_cw, r_ao, r_co, r_o]
    mid_full = [(conv_w, m_conv_w, v_conv_w), (w_attn_out, m_w_attn_out, v_w_attn_out),
                (w_conv_out, m_w_conv_out, v_w_conv_out), (w_o, m_w_o, v_w_o)]
    mid_res = _multi_adamw(mid_parts, [tuple(t[0] for t in wmv) for wmv in mid_full], "adam_mid")
    for nm, wmv, outs4 in zip(mid_names, mid_full, mid_res):
        res[nm] = tuple(t[None] for t in outs4)
    res.update(dict(zip(small_names, small_res)))
    order = ["w_ada", "b_ada", "w_in", "conv_w", "conv_b", "rel_bias", "w_attn_out", "w_conv_out",
             "w_o", "ln_g", "ln_b"]
    outs = [loss, grad_x.reshape(BL, S, D)]
    for k in range(4):
        outs += [res[name][k] for name in order]
    return tuple(outs)
```

```python
import math

import numpy as np
import jax
import jax.numpy as jnp
from jax import lax
from jax.experimental import pallas as pl
from jax.experimental.pallas import tpu as pltpu

F32 = jnp.float32
BF16 = jnp.bfloat16
MESH = pl.DeviceIdType.MESH

N_DEV = 8
D = 1024
S = 2048
BL = 2
T = BL * S
NCOL = 11264
SHARD = NCOL // N_DEV
CB = 512
NCB = NCOL // CB
HD = 128
GW = 512
QB = 128
DILATIONS = (1, 4, 16)
N_STEPS = 128
N_BUCKETS = 32
N_HEADS = 12
ALPHA = 2.0 ** 0.25
LN_EPS = 1e-5
NEG_INF = -1e30
SCALE = HD ** -0.5
ADA_SHARD = 3 * D // N_DEV

CB_Q, CB_K, CB_V, CB_GA = 0, 3, 6, 9
KB_U, KB_BG, KB_CG, KB_GC, KB_MA, KB_MC = 5, 6, 7, 8, 9, 10

ADAM_LR, ADAM_B1, ADAM_B2, ADAM_EPS, ADAM_WD, ADAM_STEP = 0.001, 0.9, 0.999, 1e-08, 0.01, 10

VMEM_LIMIT = 56 * 1024 * 1024
VMEM_LIMIT_TAIL = 62 * 1024 * 1024


def _dot(a, b):
    return jnp.dot(a, b, preferred_element_type=F32)


def _dot_nt(a, b):
    return lax.dot_general(a, b, (((1,), (1,)), ((), ())), preferred_element_type=F32)


def _dot_tn(a, b):
    return lax.dot_general(a, b, (((0,), (0,)), ((), ())), preferred_element_type=F32)


def _sigmoid(v):
    return 1.0 / (1.0 + jnp.exp(-v))


def _column_copies(pieces, dst_hbm, row0, sems):
    copies = []
    for k, (src, col0) in enumerate(pieces):
        rows, width = src.shape
        copies.append(pltpu.make_async_copy(
            src, dst_hbm.at[pl.ds(row0, rows), pl.ds(col0, width)], sems.at[k]))
    return copies


def _my_index():
    return 4 * lax.axis_index("x") + 2 * lax.axis_index("y") + lax.axis_index("c")


class _Gather:
    def __init__(self, ins, outs, stage, send_sems, recv_sems, local_sems):
        self.ins, self.outs, self.stage = ins, outs, stage
        self.send_sems, self.recv_sems, self.local_sems = send_sems, recv_sems, local_sems
        x, y, c = lax.axis_index("x"), lax.axis_index("y"), lax.axis_index("c")
        self.c = c
        self.me, self.sibling = (x, y, c), (x, y, 1 - c)
        self.chips = [(1 - x, y), (x, 1 - y), (1 - x, 1 - y)]

    @staticmethod
    def scratch(arrs):
        n = len(arrs)
        return ([pltpu.SemaphoreType.DMA((7 * n,)), pltpu.SemaphoreType.DMA((7 * n,)),
                 pltpu.SemaphoreType.DMA((n,))] + [pltpu.VMEM(a.shape, a.dtype) for a in arrs])

    def _copy(self, a, k, block, to, src=None):
        dst = self.outs[a].at[4 * block[0] + 2 * block[1] + block[2]]
        return pltpu.make_async_remote_copy(
            src_ref=dst if src is None else src, dst_ref=dst,
            send_sem=self.send_sems.at[a * 7 + k], recv_sem=self.recv_sems.at[a * 7 + k],
            device_id=to, device_id_type=MESH)

    def _first(self):
        first = []
        for a in range(len(self.ins)):
            first.append(self._copy(a, 0, self.me, self.sibling, src=self.ins[a]))
            first += [self._copy(a, 1 + j, self.me, (*chip, self.c), src=self.ins[a])
                      for j, chip in enumerate(self.chips)]
        return first

    def _mine(self):
        me = self.me
        return [pltpu.make_async_copy(self.stage[a], self.outs[a].at[4 * me[0] + 2 * me[1] + me[2]],
                                      self.local_sems.at[a]) for a in range(len(self.ins))]

    def begin(self):
        for cp in self._first():
            cp.start()
        loads = [pltpu.make_async_copy(self.ins[a], self.stage[a], self.local_sems.at[a])
                 for a in range(len(self.ins))]
        for cp in loads:
            cp.start()
        for cp in loads:
            cp.wait()
        for cp in self._mine():
            cp.start()

    def finish(self):
        n, c, me, sibling = len(self.ins), self.c, self.me, self.sibling
        passed = []
        for j, chip in enumerate(self.chips):
            for a in range(n):
                self._copy(a, 1 + j, (*chip, c), me).wait_recv()
                fwd = self._copy(a, 4 + j, (*chip, c), sibling)
                fwd.start()
                passed.append(fwd)
        for a in range(n):
            self._copy(a, 0, sibling, me).wait_recv()
        for j, chip in enumerate(self.chips):
            for a in range(n):
                self._copy(a, 4 + j, (*chip, 1 - c), me).wait_recv()
        for cp in self._first() + passed:
            cp.wait_send()
        for cp in self._mine():
            cp.wait()


def _all_gather(arrs, out_shapes, name):
    n = len(arrs)

    def body(*refs):
        g = _Gather(refs[:n], refs[n:2 * n], refs[2 * n + 3:], *refs[2 * n:2 * n + 3])
        g.begin()
        g.finish()

    any_spec = pl.BlockSpec(memory_space=pl.ANY)
    return pl.pallas_call(
        body, name=name,
        out_shape=tuple(out_shapes),
        in_specs=[any_spec] * n,
        out_specs=tuple([any_spec] * n),
        scratch_shapes=_Gather.scratch(arrs),
    )(*arrs)


def _neighbour_chips():
    x, y, c = lax.axis_index("x"), lax.axis_index("y"), lax.axis_index("c")
    first = (jnp.where(c == 0, 1 - x, x), jnp.where(c == 0, y, 1 - y))
    second = (jnp.where(c == 0, x, 1 - x), jnp.where(c == 0, 1 - y, y))
    return first, second, (1 - x, 1 - y)


def _slice_order():
    x, y, c = lax.axis_index("x"), lax.axis_index("y"), lax.axis_index("c")
    nb1, nb2, diag = _neighbour_chips()
    slots = []
    for mine, theirs in ((nb1, nb2), (nb2, nb1), (diag, diag), ((x, y), (x, y))):
        slots += [2 * (2 * theirs[0] + theirs[1]) + 1 - c, 2 * (2 * mine[0] + mine[1]) + c]
    return jnp.stack(slots).astype(jnp.int32)


def _gw_in_pair(order, h, dproj, smalls, small_shapes4):
    kk, m = h.shape
    tk = min(kk, 2048)
    nk = kk // tk
    ncols = dproj.shape[1] // N_DEV
    n = len(smalls)

    def body(order_ref, h_ref, d_ref, *rest):
        ins = rest[:n]
        sums_hbm, parts_hbm = rest[n], rest[n + 1]
        sib = rest[n + 2:2 * n + 2]
        (acc, sendbuf, recvbuf, sumbuf, send_sems, recv_sems, local_sem, ssend, srecv,
         isend, irecv) = rest[2 * n + 2:]
        js, k = pl.program_id(0), pl.program_id(1)
        x, y, c = lax.axis_index("x"), lax.axis_index("y"), lax.axis_index("c")
        sibling = (x, y, 1 - c)
        my_chip = 2 * x + y
        nb1, nb2, _ = _neighbour_chips()
        near = [(*nb1, c), (*nb2, c)]

        def ici_copy(p, out_chip):
            peer = near[p] if isinstance(p, int) else tuple(jnp.where(p == 0, a, b) for a, b in zip(*near))
            return pltpu.make_async_remote_copy(
                src_ref=sumbuf.at[p], dst_ref=parts_hbm.at[out_chip],
                send_sem=isend.at[p], recv_sem=irecv.at[p], device_id=peer, device_id_type=MESH)

        def small_copies():
            return [pltpu.make_async_remote_copy(
                        src_ref=ins[a].at[2 * q + 1 - c], dst_ref=sib[a].at[q],
                        send_sem=ssend.at[a * 4 + q], recv_sem=srecv.at[a * 4 + q],
                        device_id=sibling, device_id_type=MESH)
                    for a in range(n) for q in range(4)]

        def slice_copy(p):
            return pltpu.make_async_remote_copy(
                src_ref=sendbuf, dst_ref=recvbuf.at[p], send_sem=send_sems.at[p], recv_sem=recv_sems.at[p],
                device_id=sibling, device_id_type=MESH)

        def sum_copy(p):
            return pltpu.make_async_copy(sumbuf.at[2], sums_hbm.at[order_ref[2 * p] // 2], local_sem)

        @pl.when((js == 0) & (k == 0))
        def _():
            for cp in small_copies():
                cp.start()

        def partial():
            return _dot_tn(h_ref[...], d_ref[...])

        if nk > 1:
            @pl.when(k == 0)
            def _():
                acc[...] = partial()
        if nk > 2:
            @pl.when((k > 0) & (k < nk - 1))
            def _():
                acc[...] += partial()

        def total():
            return partial() + acc[...] if nk > 1 else partial()

        p = js // 2

        @pl.when((js % 2 == 0) & (k == nk - 1))
        def _():
            @pl.when(p > 0)
            def _():
                slice_copy(p - 1).wait_send()
            sendbuf[...] = total().astype(BF16)
            slice_copy(p).start()

        @pl.when((js % 2 == 1) & (k == nk - 1))
        def _():
            slice_copy(p).wait_recv()

            @pl.when(p == 3)
            def _():
                sum_copy(2).wait()
            sumbuf[jnp.minimum(p, 2)] = (total() + recvbuf[p].astype(F32)).astype(BF16)

            @pl.when(p < 2)
            def _():
                ici_copy(p, my_chip).start()

            @pl.when(p >= 2)
            def _():
                sum_copy(p).start()

        @pl.when((js == N_DEV - 1) & (k == nk - 1))
        def _():
            slice_copy(3).wait_send()
            sum_copy(3).wait()
            for cp in small_copies():
                cp.wait()
            for p in range(2):
                ici_copy(p, 2 * near[p][0] + near[p][1]).wait_recv()
                ici_copy(p, my_chip).wait_send()

    any_spec = pl.BlockSpec(memory_space=pl.ANY)
    res = pl.pallas_call(
        body, name="gw_in_pair",
        grid_spec=pltpu.PrefetchScalarGridSpec(
            num_scalar_prefetch=1,
            grid=(N_DEV, nk),
            in_specs=[pl.BlockSpec((tk, m), lambda js, k, order_ref: (k, 0)),
                      pl.BlockSpec((tk, ncols), lambda js, k, order_ref: (k, order_ref[js]))] + [any_spec] * n,
            out_specs=(any_spec,) * (n + 2),
            scratch_shapes=[pltpu.VMEM((m, ncols), F32), pltpu.VMEM((m, ncols), BF16),
                            pltpu.VMEM((4, m, ncols), BF16), pltpu.VMEM((3, m, ncols), BF16),
                            pltpu.SemaphoreType.DMA((4,)), pltpu.SemaphoreType.DMA((4,)),
                            pltpu.SemaphoreType.DMA,
                            pltpu.SemaphoreType.DMA((4 * n,)), pltpu.SemaphoreType.DMA((4 * n,)),
                            pltpu.SemaphoreType.DMA((2,)), pltpu.SemaphoreType.DMA((2,))]),
        out_shape=(jax.ShapeDtypeStruct((4, m, ncols), BF16),) * 2 + tuple(small_shapes4),
        compiler_params=pltpu.CompilerParams(vmem_limit_bytes=VMEM_LIMIT),
    )(order, h, dproj, *smalls)
    return res[0], res[1], res[2:]


def _chip_copies(ins, outs, send_sems, recv_sems, local_sems, hops):
    n = len(ins)
    x, y, c = lax.axis_index("x"), lax.axis_index("y"), lax.axis_index("c")
    my_chip = 2 * x + y

    def peer_of(k):
        return ((1 - x) if (k >> 1) & 1 else x, (1 - y) if k & 1 else y, c)

    def copy(a, k, out_chip):
        peer = peer_of(k)
        return pltpu.make_async_remote_copy(
            src_ref=ins[a].at[2 * peer[0] + peer[1]], dst_ref=outs[a].at[out_chip],
            send_sem=send_sems.at[a * 3 + k - 1], recv_sem=recv_sems.at[a * 3 + k - 1],
            device_id=peer, device_id_type=MESH)

    sends = [copy(a, k, my_chip) for k in range(1, 4) for a in range(n) if k in hops[a]]
    arrivals = []
    for k in range(1, 4):
        peer = peer_of(k)
        arrivals += [copy(a, k, 2 * peer[0] + peer[1]) for a in range(n) if k in hops[a]]
    mine = [pltpu.make_async_copy(ins[a].at[my_chip], outs[a].at[my_chip], local_sems.at[a])
            for a in range(n)]
    return sends, arrivals, mine


def _pair_add(core, mines, theirs):
    n = len(mines)

    def body(core_ref, *refs):
        mine, sib, outs = refs[:n], refs[n:2 * n], refs[2 * n:]
        for a in range(n):
            for q in range(4):
                outs[a][q] = (mine[a][2 * q + core_ref[0]].astype(F32)
                              + sib[a][q].astype(F32)).astype(outs[a].dtype)

    return pl.pallas_call(
        body, name="pair_add",
        in_specs=[pl.BlockSpec(memory_space=pltpu.SMEM)] + [pl.BlockSpec(memory_space=pltpu.VMEM)] * (2 * n),
        out_shape=tuple(jax.ShapeDtypeStruct(t.shape, t.dtype) for t in theirs),
    )(core, *mines, *theirs)


def _mod_exchange(c8, w_ada, b_cols):
    cols = w_ada.shape[1]

    def body(c_ref, w_ref, b_ref, call_ref, mod_ref, msend, send1, recv1, send2, recv2):
        x, y, c = lax.axis_index("x"), lax.axis_index("y"), lax.axis_index("c")
        my_slot = 4 * x + 2 * y + c

        def peer_of(k):
            return ((1 - x) if (k >> 2) & 1 else x, (1 - y) if (k >> 1) & 1 else y, (1 - c) if k & 1 else c)

        def slot_of(dev):
            return 4 * dev[0] + 2 * dev[1] + dev[2]

        def exchange(src_of, dst_ref, send_sems, recv_sems):
            sends, arrivals = [], []
            for k in range(1, 8):
                peer = peer_of(k)
                sends.append(pltpu.make_async_remote_copy(
                    src_ref=src_of(slot_of(peer)), dst_ref=dst_ref.at[my_slot],
                    send_sem=send_sems.at[k - 1], recv_sem=recv_sems.at[k - 1],
                    device_id=peer, device_id_type=MESH))
                arrivals.append(pltpu.make_async_remote_copy(
                    src_ref=src_of(my_slot), dst_ref=dst_ref.at[slot_of(peer)],
                    send_sem=send_sems.at[k - 1], recv_sem=recv_sems.at[k - 1],
                    device_id=peer, device_id_type=MESH))
            for cp in sends:
                cp.start()
            for cp in arrivals:
                cp.wait_recv()
            for cp in sends:
                cp.wait_send()

        call_ref[my_slot] = c_ref[...]
        exchange(lambda s: c_ref, call_ref, send1, recv1)
        cv = call_ref[...].reshape(N_DEV * 8, c_ref.shape[1])
        act = cv * _sigmoid(cv)
        mod = jnp.dot(act, w_ref[...], preferred_element_type=F32,
                      precision=lax.Precision.HIGHEST) + b_ref[...]
        msend[...] = mod.reshape(N_DEV, 8, cols)
        mod_ref[my_slot] = msend[my_slot]
        exchange(lambda s: msend.at[s], mod_ref, send2, recv2)

    return pl.pallas_call(
        body, name="mod_exchange",
        out_shape=(jax.ShapeDtypeStruct((N_DEV, 8, c8.shape[1]), F32),
                   jax.ShapeDtypeStruct((N_DEV, 8, cols), F32)),
        scratch_shapes=[pltpu.VMEM((N_DEV, 8, cols), F32)] + [pltpu.SemaphoreType.DMA((7,))] * 4,
    )(c8, w_ada, b_cols)


def _w_ada_update(c_all_t, dmod_cols, w, m, v):
    rows, cols = w.shape
    tr = 256

    def body(c_ref, d_ref, w_ref, m_ref, v_ref, g_ref, dl_ref, nm_ref, nv_ref):
        cv = c_ref[...]
        g = jnp.dot(cv * _sigmoid(cv), d_ref[...], preferred_element_type=F32,
                    precision=lax.Precision.HIGHEST)
        g_ref[...] = g
        dl_ref[...], nm_ref[...], nv_ref[...] = _adam_step(g, w_ref[...], m_ref[...], v_ref[...])

    blk = pl.BlockSpec((tr, cols), lambda i: (i, 0))
    shp = jax.ShapeDtypeStruct((rows, cols), F32)
    return pl.pallas_call(
        body, name="adam_w_ada",
        grid=(rows // tr,),
        in_specs=[pl.BlockSpec((tr, c_all_t.shape[1]), lambda i: (i, 0)),
                  pl.BlockSpec(dmod_cols.shape, lambda i: (0, 0)), blk, blk, blk],
        out_specs=(blk, blk, blk, blk),
        out_shape=(shp, shp, shp, shp),
    )(c_all_t, dmod_cols, w, m, v)


def _shard_order():
    x, y, c = lax.axis_index("x"), lax.axis_index("y"), lax.axis_index("c")
    first, second, diag = _neighbour_chips()
    devs = [(x, y, c), (x, y, 1 - c), (*first, c), (*second, 1 - c), (*second, c), (*first, 1 - c),
            (*diag, c), (*diag, 1 - c)]
    return jnp.stack([4 * d[0] + 2 * d[1] + d[2] for d in devs]).astype(jnp.int32)


def _prep_h(x2, mod3):
    ts = 512
    per_seq = S // ts

    def body(x_ref, mod_ref, h_ref):
        shift = mod_ref[0, 0:1, :]
        scale = mod_ref[0, 1:2, :]
        h_ref[...] = (x_ref[...] * (1.0 + scale) + shift).astype(BF16)

    return pl.pallas_call(
        body, name="prep_h",
        grid=(T // ts,),
        in_specs=[pl.BlockSpec((ts, D), lambda i: (i, 0)),
                  pl.BlockSpec((1, 3, D), lambda i: (i // per_seq, 0, 0))],
        out_specs=pl.BlockSpec((ts, D), lambda i: (i, 0)),
        out_shape=jax.ShapeDtypeStruct((T, D), BF16),
    )(x2, mod3)


def _gather_proj(order, h, w_shard, tm, ride=()):
    rows, kdim = h.shape
    ncols = w_shard.shape[1]
    n_i = rows // tm
    ride_arrs, ride_shapes = ride if ride else ((), ())
    n_ride = len(ride_arrs)

    def body(order_ref, h_ref, mine_hbm, *rest):
        ride_ins = rest[:n_ride]
        o_ref, all_hbm = rest[n_ride:n_ride + 2]
        ride_outs = rest[n_ride + 2:2 * n_ride + 2]
        wv, send_sems, recv_sems, local_sems = rest[2 * n_ride + 2:2 * n_ride + 6]
        ride_scr = rest[2 * n_ride + 6:]
        j, i = pl.program_id(0), pl.program_id(1)
        c = lax.axis_index("c")
        me, sibling = (lax.axis_index("x"), lax.axis_index("y"), c), (lax.axis_index("x"), lax.axis_index("y"), 1 - c)
        nb1, nb2, diag = _neighbour_chips()

        def slot(dev):
            return 4 * dev[0] + 2 * dev[1] + dev[2]

        def copy(k, block, to, src=None, part=None):
            buf = wv.at[slot(block)]
            if part is not None:
                buf = buf.at[pl.ds(pl.multiple_of(part * (kdim // 2), kdim // 2), kdim // 2)]
            return pltpu.make_async_remote_copy(
                src_ref=buf if src is None else src, dst_ref=buf,
                send_sem=send_sems.at[k], recv_sem=recv_sems.at[k],
                device_id=to, device_id_type=MESH)

        def keep(step, block):
            s = slot(block)
            cols = pl.ds(pl.multiple_of((s % 2) * ncols, 128), ncols)
            return pltpu.make_async_copy(wv.at[s], all_hbm.at[s // 2, :, cols], local_sems.at[step])

        if n_ride:
            gather = _Gather(ride_ins, ride_outs, ride_scr[3:], *ride_scr[:3])
        to_sibling, to_nb1, to_nb2 = (copy(0, me, sibling, mine_hbm), copy(1, me, (*nb1, c), mine_hbm),
                                      copy(2, me, (*nb2, c), mine_hbm))
        relay1, relay2 = copy(3, (*nb2, c), (*nb1, c), part=c), copy(4, (*nb1, c), (*nb2, c), part=1 - c)
        pass_nb1, pass_nb2 = copy(5, (*nb1, c), sibling), copy(6, (*nb2, c), sibling)
        pass_d1, pass_d2 = copy(7, (*diag, c), sibling, part=c), copy(8, (*diag, c), sibling, part=1 - c)
        sends = [to_sibling, to_nb1, to_nb2, relay1, relay2, pass_nb1, pass_nb2, pass_d1, pass_d2]
        due = [
            (me, [], []),
            (sibling, [copy(0, sibling, me)], [[]]),
            ((*nb1, c), [copy(1, (*nb1, c), me)], [[pass_nb1, to_nb2]]),
            ((*nb2, 1 - c), [copy(5, (*nb2, 1 - c), me)], [[]]),
            ((*nb2, c), [copy(2, (*nb2, c), me)], [[pass_nb2, relay1, relay2]]),
            ((*nb1, 1 - c), [copy(6, (*nb1, 1 - c), me)], [[]]),
            ((*diag, c), [copy(3, (*diag, c), me, part=c), copy(4, (*diag, c), me, part=1 - c)],
             [[pass_d1], [pass_d2]]),
            ((*diag, 1 - c), [copy(7, (*diag, 1 - c), me, part=1 - c), copy(8, (*diag, 1 - c), me, part=c)],
             [[], []]),
        ]

        @pl.when((j == 0) & (i == 0))
        def _():
            to_sibling.start()
            to_nb1.start()
            load = pltpu.make_async_copy(mine_hbm, wv.at[slot(me)], local_sems.at[N_DEV])
            load.start()
            load.wait()
            keep(0, me).start()

        for step in range(1, N_DEV):
            block, arrivals, then = due[step]

            @pl.when((j == step) & (i == 0))
            def _():
                for arrival, follow in zip(arrivals, then):
                    arrival.wait_recv()
                    for cp in follow:
                        cp.start()
                keep(step, block).start()
                if n_ride and step == N_DEV - 2:
                    gather.begin()

        o_ref[...] = _dot(h_ref[...], wv[order_ref[j]]).astype(BF16)

        @pl.when((j == N_DEV - 1) & (i == n_i - 1))
        def _():
            for cp in sends:
                cp.wait_send()
            for step in range(N_DEV):
                keep(step, due[step][0]).wait()
            if n_ride:
                gather.finish()

    any_spec = pl.BlockSpec(memory_space=pl.ANY)
    res = pl.pallas_call(
        body, name="gather_proj",
        grid_spec=pltpu.PrefetchScalarGridSpec(
            num_scalar_prefetch=1,
            grid=(N_DEV, n_i),
            in_specs=[pl.BlockSpec((tm, kdim), lambda j, i, order_ref: (i, 0)), any_spec] + [any_spec] * n_ride,
            out_specs=(pl.BlockSpec((tm, ncols), lambda j, i, order_ref: (i, order_ref[j])), any_spec)
                      + (any_spec,) * n_ride,
            scratch_shapes=[pltpu.VMEM((N_DEV, kdim, ncols), BF16),
                            pltpu.SemaphoreType.DMA((9,)), pltpu.SemaphoreType.DMA((9,)),
                            pltpu.SemaphoreType.DMA((N_DEV + 1,))]
                           + (_Gather.scratch(ride_arrs) if n_ride else [])),
        out_shape=(jax.ShapeDtypeStruct((rows, N_DEV * ncols), BF16),
                   jax.ShapeDtypeStruct((N_DEV // 2, kdim, 2 * ncols), BF16)) + tuple(ride_shapes),
        compiler_params=pltpu.CompilerParams(vmem_limit_bytes=VMEM_LIMIT),
    )(order, h, w_shard, *ride_arrs)
    return res[0], res[1], res[2:]


SKEW_W = 512


def _bucket_maps():
    lanes = np.arange(SKEW_W)

    def buckets_of(steps):
        rows = []
        for dil in DILATIONS:
            dist = np.maximum(steps, 0) * dil
            nf = np.maximum(dist, 1).astype(np.float32)
            large = 16 + (np.log(nf / np.float32(16)) / np.float32(math.log(128.0))
                          * np.float32(16)).astype(np.int32)
            large = np.minimum(large, N_BUCKETS - 1)
            bucket = np.where(dist < 16, dist, large)
            rows.append(np.where((steps >= 0) & (steps <= N_STEPS), bucket, -1).astype(np.int32))
        return np.stack(rows)[:, None, :]

    a = np.arange(QB)[:, None]
    b = np.arange(2 * QB)[None, :]
    steps = a + QB - b
    band = (steps >= 0) & (steps <= N_STEPS)
    first = band & (b >= QB)
    masks = np.stack([first, band]).astype(np.int32)
    return buckets_of(QB - lanes), buckets_of(2 * QB - 1 - lanes), masks


def _bias_expand(rel_bias, lane_buckets, masks):
    def body(tab_ref, bk_ref, mk_ref, o_ref):
        for g in range(3):
            bk = bk_ref[g]
            for h in range(4):
                col = 4 * g + h
                per_offset = jnp.zeros((1, SKEW_W), F32)
                for k in range(N_BUCKETS):
                    per_offset = jnp.where(bk == k, tab_ref[k, col], per_offset)
                tile = pltpu.roll(jnp.broadcast_to(per_offset, (QB, SKEW_W)), 0, 1, stride=1, stride_axis=0)
                tile = tile[:, :2 * QB]
                o_ref[g, 0, h] = jnp.where(mk_ref[0] != 0, tile, NEG_INF)
                o_ref[g, 1, h] = jnp.where(mk_ref[1] != 0, tile, NEG_INF)

    return pl.pallas_call(
        body, name="bias_expand",
        in_specs=[pl.BlockSpec(memory_space=pltpu.SMEM),
                  pl.BlockSpec(memory_space=pltpu.VMEM),
                  pl.BlockSpec(memory_space=pltpu.VMEM)],
        out_shape=jax.ShapeDtypeStruct((3, 2, 4, QB, 2 * QB), F32),
    )(rel_bias, lane_buckets, masks)


def _bias_grad(ds1, ds2, ds3, lane_buckets):
    exchange = jnp.asarray(np.eye(QB, dtype=np.float32)[::-1].copy())

    def body(d1_ref, d2_ref, d3_ref, bk_ref, ex_ref, o_ref):
        for g, d_ref in enumerate((d1_ref, d2_ref, d3_ref)):
            bk = bk_ref[g]
            for h in range(4):
                flipped = jnp.dot(ex_ref[...], d_ref[h], preferred_element_type=F32,
                                  precision=lax.Precision.HIGHEST)
                padded = jnp.concatenate([flipped, jnp.zeros((QB, SKEW_W - 2 * QB), F32)], axis=1)
                skewed = pltpu.roll(padded, 0, 1, stride=1, stride_axis=0)
                per_offset = jnp.sum(skewed, axis=0, keepdims=True)
                for k in range(N_BUCKETS):
                    o_ref[k, 4 * g + h] = jnp.sum(jnp.where(bk == k, per_offset, 0.0))

    return pl.pallas_call(
        body, name="bias_grad",
        in_specs=[pl.BlockSpec(memory_space=pltpu.VMEM)] * 5,
        out_specs=pl.BlockSpec(memory_space=pltpu.SMEM),
        out_shape=jax.ShapeDtypeStruct((N_BUCKETS, N_HEADS), F32),
    )(ds1, ds2, ds3, lane_buckets, exchange)


def _scratch_sets(rows):
    return 4 if rows <= 512 else 1


def _unit_chunks(dil, size=16):
    units = [(h, r) for h in range(4) for r in range(dil)]
    return [units[i:i + size] for i in range(0, len(units), size)]


def _residue_rows(src_ref, copies, h, residue):
    buf = copies[h % len(copies)]
    buf[...] = src_ref[:, h * HD:(h + 1) * HD].astype(F32)
    return lambda r: buf[residue(r), :].astype(BF16)


def _attn_fwd(proj, bias, g):
    dil = DILATIONS[g]
    rows = QB * dil
    nsb = S // rows
    has_prev = nsb > 1

    def residue(r):
        return pl.ds(r, QB, stride=dil)

    n_sets = _scratch_sets(rows)
    n_in = 6 if has_prev else 4
    n_copied = (4 + (2 if has_prev else 0)) * n_sets

    def body(*refs):
        q_ref, kc_ref, vc_ref = refs[:3]
        kp_ref, vp_ref = refs[3:5] if has_prev else (None, None)
        b_ref = refs[n_in - 1]
        o_ref, l_ref = refs[n_in:n_in + 2]
        scr = list(refs[n_in + 2:])
        ls = [scr.pop(0) for _ in range(4)]
        copies = {name: [scr.pop(0) for _ in range(n_sets)]
                  for name in ("q", "kc", "vc", "o") + (("kp", "vp") if has_prev else ())}
        lane = lax.broadcasted_iota(jnp.int32, (QB, 128), 1)
        refs_of = {"q": q_ref, "kc": kc_ref, "vc": vc_ref, "kp": kp_ref, "vp": vp_ref}
        for chunk in _unit_chunks(dil):
            rows_of = {h: {name: _residue_rows(refs_of[name], copies[name], h, residue)
                           for name in refs_of if refs_of[name] is not None}
                       for h in sorted({h for h, _ in chunk})}

            def batch(name):
                return jnp.stack([rows_of[h][name](r) for h, r in chunk])

            q, k, v = batch("q"), batch("kc"), batch("vc")
            if has_prev:
                k = jnp.concatenate([batch("kp"), k], axis=1)
                v = jnp.concatenate([batch("vp"), v], axis=1)
                bias_b = jnp.stack([b_ref[h] for h, _ in chunk])
            else:
                bias_b = jnp.stack([b_ref[h, :, QB:] for h, _ in chunk])
            s = jnp.einsum("uqd,ukd->uqk", q, k, preferred_element_type=F32) * SCALE + bias_b
            m = jnp.max(s, axis=-1, keepdims=True)
            p = jnp.exp(s - m)
            l = jnp.sum(p, axis=-1, keepdims=True)
            o = jnp.einsum("uqk,ukd->uqd", p.astype(BF16), v, preferred_element_type=F32) / l
            lse = m + jnp.log(l)
            for i, (h, r) in enumerate(chunk):
                copies["o"][h % n_sets][residue(r), :] = o[i]
                ls[h][r * QB:(r + 1) * QB, :] = jnp.where(lane == h, lse[i], 0.0)
            for h in sorted({h for h, _ in chunk}):
                o_ref[:, h * HD:(h + 1) * HD] = copies["o"][h % n_sets][...]
        for r in range(dil):
            blk = slice(r * QB, (r + 1) * QB)
            l_ref[residue(r), :] = (ls[0][blk, :] + ls[1][blk, :]) + (ls[2][blk, :] + ls[3][blk, :])

    def row(b, n):
        return b * nsb + n

    def prev(b, n):
        return b * nsb + jnp.maximum(n - 1, 0)

    in_specs = [
        pl.BlockSpec((rows, GW), lambda b, n: (row(b, n), CB_Q + g)),
        pl.BlockSpec((rows, GW), lambda b, n: (row(b, n), CB_K + g)),
        pl.BlockSpec((rows, GW), lambda b, n: (row(b, n), CB_V + g)),
    ]
    args = [proj, proj, proj]
    scratch = [pltpu.VMEM((rows, 128), F32)] * (4 + n_copied)
    if has_prev:
        in_specs += [pl.BlockSpec((rows, GW), lambda b, n: (prev(b, n), CB_K + g)),
                     pl.BlockSpec((rows, GW), lambda b, n: (prev(b, n), CB_V + g))]
        args += [proj, proj]
    in_specs.append(pl.BlockSpec((None, None, 4, QB, 2 * QB),
                                 lambda b, n: (g, jnp.minimum(n, 1), 0, 0, 0)))
    args.append(bias)
    return pl.pallas_call(
        body, name=f"attn_fwd{g}",
        grid=(BL, nsb),
        in_specs=in_specs,
        out_specs=(pl.BlockSpec((rows, GW), lambda b, n: (row(b, n), 0)),
                   pl.BlockSpec((rows, 128), lambda b, n: (row(b, n), 0))),
        out_shape=(jax.ShapeDtypeStruct((T, GW), F32), jax.ShapeDtypeStruct((T, 128), F32)),
        scratch_shapes=scratch,
        compiler_params=pltpu.CompilerParams(vmem_limit_bytes=VMEM_LIMIT),
    )(*args)


def _attn_bwd(proj, d_out, stats, bias, dproj, g):
    dil = DILATIONS[g]
    rows = QB * dil
    nsb = S // rows
    has_prev = nsb > 1
    n_steps = nsb + 1 if has_prev else 1
    n_in = 7 + (2 if has_prev else 0)

    def residue(r):
        return pl.ds(r, QB, stride=dil)

    n_sets = _scratch_sets(rows)

    def body(*refs):
        q_ref, kc_ref, vc_ref, do_ref, st_ref, b_ref = refs[:6]
        kp_ref, vp_ref = refs[6:8] if has_prev else (None, None)
        out_ref, db_ref = refs[n_in], refs[n_in + 1]
        scr = list(refs[n_in + 2:])
        sq, sk, sv, sems = [scr.pop(0) for _ in range(4)]
        carry = scr.pop(0) if has_prev else None
        sts = scr.pop(0)
        copies = {name: [scr.pop(0) for _ in range(n_sets)]
                  for name in ("q", "kc", "vc", "do", "dq", "dk", "dv") + (("kp", "vp") if has_prev else ())}
        b, n = pl.program_id(0), pl.program_id(1)

        @pl.when((b == 0) & (n == 0))
        def _():
            db_ref[...] = jnp.zeros_like(db_ref)

        def finish(h, r, dq, dk, dv):
            for name, val in (("dq", dq), ("dk", dk), ("dv", dv)):
                copies[name][h % n_sets][residue(r), :] = val

        step = b * n_steps + n
        slot = step % 2

        def stage_copies(s, row0):
            return _column_copies([(sq.at[s], CB * (CB_Q + g)), (sk.at[s], CB * (CB_K + g)),
                                   (sv.at[s], CB * (CB_V + g))], out_ref, row0, sems.at[s])

        @pl.when((step >= 2) & ((step - 2) % n_steps >= (1 if has_prev else 0)))
        def _():
            for cp in stage_copies(slot, 0):
                cp.wait()

        def finish_head(h):
            sl = slice(h * HD, (h + 1) * HD)
            sq[slot, :, sl] = copies["dq"][h % n_sets][...].astype(BF16)
            sk[slot, :, sl] = copies["dk"][h % n_sets][...].astype(BF16)
            sv[slot, :, sl] = copies["dv"][h % n_sets][...].astype(BF16)

        def write_block(blk_idx):
            for cp in stage_copies(slot, pl.multiple_of(blk_idx * rows, rows)):
                cp.start()

            @pl.when(step == BL * n_steps - 1)
            def _():
                for s in range(2):
                    for cp in stage_copies(s, 0):
                        cp.wait()

        def carried(h, r):
            blk = slice(r * QB, (r + 1) * QB)
            return ((blk, slice(h * HD, (h + 1) * HD)), (blk, slice(GW + h * HD, GW + (h + 1) * HD)),
                    (blk, slice(2 * GW + h * HD, 2 * GW + (h + 1) * HD)))

        if has_prev:
            @pl.when(n == 0)
            def _():
                carry[...] = jnp.zeros_like(carry)

            @pl.when(n == nsb)
            def _():
                for h in range(4):
                    for r in range(dil):
                        cq, ck, cv = carried(h, r)
                        finish(h, r, carry[cq], carry[ck], carry[cv])
                    finish_head(h)
                write_block(b * nsb + nsb - 1)

        @pl.when(n < nsb)
        def _():
            for r in range(dil):
                sts[r * QB:(r + 1) * QB, :] = st_ref[residue(r), :]
            refs_of = {"q": q_ref, "kc": kc_ref, "vc": vc_ref, "do": do_ref, "kp": kp_ref, "vp": vp_ref}
            for chunk in _unit_chunks(dil):
                heads = sorted({h for h, _ in chunk})
                rows_of = {h: {name: _residue_rows(refs_of[name], copies[name], h, residue)
                               for name in refs_of if refs_of[name] is not None}
                           for h in heads}

                def batch(name):
                    return jnp.stack([rows_of[h][name](r) for h, r in chunk])

                q, k, v, do = batch("q"), batch("kc"), batch("vc"), batch("do")
                if has_prev:
                    k = jnp.concatenate([batch("kp"), k], axis=1)
                    v = jnp.concatenate([batch("vp"), v], axis=1)
                    bias_b = jnp.stack([b_ref[h] for h, _ in chunk])
                else:
                    bias_b = jnp.stack([b_ref[h, :, QB:] for h, _ in chunk])
                lse = jnp.stack([sts[r * QB:(r + 1) * QB, h:h + 1] for h, r in chunk])
                delta = jnp.stack([sts[r * QB:(r + 1) * QB, 4 + h:5 + h] for h, r in chunk])
                s = jnp.einsum("uqd,ukd->uqk", q, k, preferred_element_type=F32) * SCALE + bias_b
                p = jnp.exp(s - lse)
                ds = p * (jnp.einsum("uqd,ukd->uqk", do, v, preferred_element_type=F32) - delta)
                for h in heads:
                    mine = [ds[i] for i, (hh, _) in enumerate(chunk) if hh == h]
                    tot = mine[0]
                    for extra in mine[1:]:
                        tot = tot + extra
                    if has_prev:
                        db_ref[h] += tot
                    else:
                        db_ref[h, :, QB:] += tot
                dsb, pb = ds.astype(BF16), p.astype(BF16)
                dq = jnp.einsum("uqk,ukd->uqd", dsb, k, preferred_element_type=F32) * SCALE
                dk = jnp.einsum("uqk,uqd->ukd", dsb, q, preferred_element_type=F32) * SCALE
                dv = jnp.einsum("uqk,uqd->ukd", pb, do, preferred_element_type=F32)
                for i, (h, r) in enumerate(chunk):
                    if has_prev:
                        cq, ck, cv = carried(h, r)
                        finish(h, r, carry[cq], carry[ck] + dk[i, :QB], carry[cv] + dv[i, :QB])
                        carry[cq] = dq[i]
                        carry[ck] = dk[i, QB:]
                        carry[cv] = dv[i, QB:]
                    else:
                        finish(h, r, dq[i], dk[i], dv[i])
                for h in heads:
                    finish_head(h)
            if has_prev:
                @pl.when(n > 0)
                def _():
                    write_block(b * nsb + n - 1)
            else:
                write_block(b)

    def row(b, n):
        return b * nsb + jnp.minimum(n, nsb - 1)

    def prev(b, n):
        return b * nsb + jnp.maximum(jnp.minimum(n, nsb - 1) - 1, 0)

    in_specs = [
        pl.BlockSpec((rows, GW), lambda b, n: (row(b, n), CB_Q + g)),
        pl.BlockSpec((rows, GW), lambda b, n: (row(b, n), CB_K + g)),
        pl.BlockSpec((rows, GW), lambda b, n: (row(b, n), CB_V + g)),
        pl.BlockSpec((rows, GW), lambda b, n: (row(b, n), 0)),
        pl.BlockSpec((rows, 128), lambda b, n: (row(b, n), 0)),
        pl.BlockSpec((None, None, 4, QB, 2 * QB),
                     lambda b, n: (g, jnp.minimum(jnp.minimum(n, nsb - 1), 1), 0, 0, 0)),
    ]
    args = [proj, proj, proj, d_out, stats, bias]
    scratch = [pltpu.VMEM((2, rows, GW), BF16)] * 3 + [pltpu.SemaphoreType.DMA((2, 3))]
    if has_prev:
        in_specs += [pl.BlockSpec((rows, GW), lambda b, n: (prev(b, n), CB_K + g)),
                     pl.BlockSpec((rows, GW), lambda b, n: (prev(b, n), CB_V + g))]
        args += [proj, proj]
        scratch.append(pltpu.VMEM((rows, 3 * GW), F32))
    n_copied = (7 + (2 if has_prev else 0)) * n_sets
    scratch += [pltpu.VMEM((rows, 128), F32)] * (1 + n_copied)
    in_specs.append(pl.BlockSpec(memory_space=pl.ANY))
    args.append(dproj)
    return pl.pallas_call(
        body, name=f"attn_bwd{g}",
        grid=(BL, n_steps),
        in_specs=in_specs,
        out_specs=(pl.BlockSpec(memory_space=pl.ANY),
                   pl.BlockSpec((4, QB, 2 * QB), lambda b, n: (0, 0, 0))),
        out_shape=(jax.ShapeDtypeStruct((T, NCOL), BF16),
                   jax.ShapeDtypeStruct((4, QB, 2 * QB), F32)),
        scratch_shapes=scratch,
        input_output_aliases={len(args) - 1: 0},
        compiler_params=pltpu.CompilerParams(vmem_limit_bytes=VMEM_LIMIT),
    )(*args)


def _tail(x2, tgt2, mod3, o_g, lse_g, proj, w_ao, w_co, w_o, conv_w, conv_b, ln_g, ln_b):
    tm = 256
    per_seq = S // tm
    halo = 16

    def body(x_ref, t_ref, mod_ref, o1_ref, o2_ref, o3_ref, l1_ref, l2_ref, l3_ref,
             ga_ref, u_ref, bg_ref, cg_ref, gc_ref, ma_ref, mc_ref, up_ref, cp_ref,
             wao_ref, wco_ref, wo_ref, cw_ref, cb_ref, lg_ref, lb_ref,
             dproj_ref, dyc_ref, do_ref, st_ref, dxd_ref,
             gwo_ref, gwco_ref, gwao_ref, vec_ref,
             dga_s, dbg_s, dgm_s, sems, acc_o, acc_co, acc_ao):
        i = pl.program_id(0)
        bidx = i // per_seq
        first = (i % per_seq) == 0

        @pl.when(i == 0)
        def _():
            vec_ref[...] = jnp.zeros_like(vec_ref)

        slot = i % 2

        def column_copies(s, row0):
            return _column_copies([(dga_s.at[s], CB * CB_GA), (dbg_s.at[s], D * KB_BG), (dgm_s.at[s], D * KB_GC)],
                                  dproj_ref, row0, sems.at[s])

        @pl.when(i >= 2)
        def _():
            for cp in column_copies(slot, 0):
                cp.wait()

        l1, l2, l3 = l1_ref[...], l2_ref[...], l3_ref[...]
        mx = jnp.maximum(jnp.maximum(l1, l2), l3)
        e1, e2, e3 = jnp.exp(l1 - mx), jnp.exp(l2 - mx), jnp.exp(l3 - mx)
        esum = e1 + e2 + e3
        lse_tot = mx + jnp.log(esum)
        w1, w2, w3 = e1 / esum, e2 / esum, e3 / esum

        def per_head(wv):
            return jnp.concatenate([jnp.broadcast_to(wv[:, h:h + 1], (tm, HD)) for h in range(4)], axis=1)

        o = per_head(w1) * o1_ref[...] + per_head(w2) * o2_ref[...] + per_head(w3) * o3_ref[...]

        ga = ga_ref[...].astype(F32)
        sig_ga = _sigmoid(ga)
        silu_ga = ga * sig_ga
        a_in = (o * silu_ga).astype(BF16)
        a_out = _dot(a_in, wao_ref[...])

        u = u_ref[...].astype(F32)
        cg = cg_ref[...].astype(F32)
        z = cg * u
        zp = cp_ref[...].astype(F32) * up_ref[...].astype(F32)
        zp = jnp.where(first, 0.0, zp)
        zcat = jnp.concatenate([zp, z], axis=0)
        z1 = pltpu.roll(zcat, 1, 0)[halo:]
        z2 = pltpu.roll(zcat, 2, 0)[halo:]
        y_conv = cw_ref[0:1, :] * z2 + cw_ref[1:2, :] * z1 + cw_ref[2:3, :] * z + cb_ref[...]
        gc = gc_ref[...].astype(F32)
        sig_gc = _sigmoid(gc)
        silu_gc = gc * sig_gc
        bg = bg_ref[...].astype(F32)
        bg_yc = bg * y_conv
        s_in = (bg_yc * silu_gc).astype(BF16)
        s_out = _dot(s_in, wco_ref[...])

        sa = _sigmoid(ma_ref[...].astype(F32))
        sc = _sigmoid(mc_ref[...].astype(F32))
        merged = (sa * a_out + sc * s_out).astype(BF16)
        y = _dot(merged, wo_ref[...])
        gate1 = 1.0 + mod_ref[0, 2:3, :]
        xv = x_ref[...]
        resid = ALPHA * xv + gate1 * y
        mu = jnp.mean(resid, axis=1, keepdims=True)
        xc = resid - mu
        var = jnp.mean(xc * xc, axis=1, keepdims=True)
        rstd = lax.rsqrt(var + LN_EPS)
        xhat = xc * rstd
        lg = lg_ref[...]
        err = xhat * lg + lb_ref[...] - t_ref[...]
        vec_ref[3:4, :] += (0.5 / D) * jnp.sum(err * err, axis=0, keepdims=True)

        vec_ref[1:2, :] += (1.0 / D) * jnp.sum(err * xhat, axis=0, keepdims=True)
        vec_ref[2:3, :] += (1.0 / D) * jnp.sum(err, axis=0, keepdims=True)
        dxh = err * (lg * (1.0 / D))
        dres = rstd * (dxh - jnp.mean(dxh, axis=1, keepdims=True)
                       - xhat * jnp.mean(dxh * xhat, axis=1, keepdims=True))
        dxd_ref[...] = ALPHA * dres
        dgate = jnp.sum(dres * y, axis=0, keepdims=True)
        vec_ref[4:5, :] += jnp.where(bidx == 0, dgate, 0.0)
        vec_ref[5:6, :] += jnp.where(bidx == 1, dgate, 0.0)
        dy = (dres * gate1).astype(BF16)

        dmerged = _dot_nt(dy, wo_ref[...])
        da_out_f = dmerged * sa
        ds_out_f = dmerged * sc
        da_out = da_out_f.astype(BF16)
        ds_out = ds_out_f.astype(BF16)
        dgm_s[slot, :, 2 * D:3 * D] = (ds_out_f * s_out * (1.0 - sc)).astype(BF16)
        dgm_s[slot, :, D:2 * D] = (da_out_f * a_out * (1.0 - sa)).astype(BF16)
        da_in = _dot_nt(da_out, wao_ref[...])
        ds_in = _dot_nt(ds_out, wco_ref[...])

        d_o = da_in * silu_ga
        do_ref[...] = d_o.astype(BF16)
        dga_s[slot] = (da_in * o * (sig_ga + silu_ga * (1.0 - sig_ga))).astype(BF16)
        lane = lax.broadcasted_iota(jnp.int32, (tm, 128), 1)
        stats = lse_tot
        od = o * d_o
        for h in range(4):
            delta = jnp.sum(od[:, h * HD:(h + 1) * HD], axis=1, keepdims=True)
            stats = jnp.where(lane == 4 + h, delta, stats)
        st_ref[...] = stats

        ds_silu = ds_in * silu_gc
        dbg_s[slot] = (ds_silu * y_conv).astype(BF16)
        dyc = ds_silu * bg
        dyc_ref[...] = dyc
        vec_ref[0:1, :] += jnp.sum(dyc, axis=0, keepdims=True)
        dgm_s[slot, :, 0:D] = (ds_in * bg_yc * (sig_gc + silu_gc * (1.0 - sig_gc))).astype(BF16)

        @pl.when(i == 0)
        def _():
            acc_o[...] = jnp.zeros_like(acc_o)
            acc_co[...] = jnp.zeros_like(acc_co)
            acc_ao[...] = jnp.zeros_like(acc_ao)

        acc_o[...] += _dot_tn(merged, dy)
        acc_co[...] += _dot_tn(s_in, ds_out)
        acc_ao[...] += _dot_tn(a_in, da_out)

        for cp in column_copies(slot, pl.multiple_of(i * tm, tm)):
            cp.start()

        @pl.when(i == T // tm - 1)
        def _():
            gwo_ref[...] = acc_o[...].astype(BF16)
            gwco_ref[...] = acc_co[...].astype(BF16)
            gwao_ref[...] = acc_ao[...].astype(BF16)
            for s in range(2):
                for cp in column_copies(s, 0):
                    cp.wait()

    def tile(width, cblk=0):
        return pl.BlockSpec((tm, width), lambda i: (i, cblk))

    def whole(shape):
        return pl.BlockSpec(shape, lambda i: tuple(0 for _ in shape))

    def once(shape):
        return pl.BlockSpec(shape, lambda i: tuple(0 for _ in shape), pipeline_mode=pl.Buffered(1))

    prev_rows = lambda i: (jnp.maximum(i * (tm // halo) - 1, 0),)
    in_specs = [
        tile(D), tile(D), pl.BlockSpec((1, 3, D), lambda i: (i // per_seq, 0, 0)),
        tile(GW), tile(GW), tile(GW), tile(128), tile(128), tile(128),
        tile(GW, CB_GA), tile(D, KB_U), tile(D, KB_BG), tile(D, KB_CG), tile(D, KB_GC),
        tile(D, KB_MA), tile(D, KB_MC),
        pl.BlockSpec((halo, D), lambda i: (*prev_rows(i), KB_U)),
        pl.BlockSpec((halo, D), lambda i: (*prev_rows(i), KB_CG)),
        whole((GW, D)), whole((D, D)), whole((D, D)),
        whole((3, D)), whole((1, D)), whole((1, D)), whole((1, D)),
    ]
    out_specs = (
        pl.BlockSpec(memory_space=pl.ANY), tile(D), tile(GW), tile(128), tile(D),
        once((D, D)), once((D, D)), once((GW, D)),
        pl.BlockSpec((8, D), lambda i: (0, 0)),
    )
    out_shape = (
        jax.ShapeDtypeStruct((T, NCOL), BF16),
        jax.ShapeDtypeStruct((T, D), F32),
        jax.ShapeDtypeStruct((T, GW), BF16),
        jax.ShapeDtypeStruct((T, 128), F32),
        jax.ShapeDtypeStruct((T, D), F32),
        jax.ShapeDtypeStruct((D, D), BF16),
        jax.ShapeDtypeStruct((D, D), BF16),
        jax.ShapeDtypeStruct((GW, D), BF16),
        jax.ShapeDtypeStruct((8, D), F32),
    )
    return pl.pallas_call(
        body, name="tail",
        grid=(T // tm,),
        in_specs=in_specs, out_specs=out_specs, out_shape=out_shape,
        scratch_shapes=[pltpu.VMEM((2, tm, GW), BF16), pltpu.VMEM((2, tm, D), BF16), pltpu.VMEM((2, tm, 3 * D), BF16),
                        pltpu.SemaphoreType.DMA((2, 3)),
                        pltpu.VMEM((D, D), F32), pltpu.VMEM((D, D), F32), pltpu.VMEM((GW, D), F32)],
        compiler_params=pltpu.CompilerParams(vmem_limit_bytes=VMEM_LIMIT_TAIL),
    )(x2, tgt2, mod3, *o_g, *lse_g, proj, proj, proj, proj, proj, proj, proj, proj, proj,
      w_ao, w_co, w_o, conv_w, conv_b, ln_g, ln_b)


def _conv_bwd(dyc, proj, conv_w, dproj):
    tm = 512
    per_seq = S // tm

    def body(d_ref, dn_ref, u_ref, c_ref, cw_ref, _, dproj_ref, g_ref, du_s, dc_s, sems):
        i = pl.program_id(0)
        last = (i % per_seq) == per_seq - 1

        @pl.when(i == 0)
        def _():
            g_ref[...] = jnp.zeros_like(g_ref)

        slot = i % 2

        def column_copies(s, row0):
            return _column_copies([(du_s.at[s], D * KB_U), (dc_s.at[s], D * KB_CG)], dproj_ref, row0, sems.at[s])

        @pl.when(i >= 2)
        def _():
            for cp in column_copies(slot, 0):
                cp.wait()

        d = d_ref[...]
        dn = jnp.where(last, 0.0, dn_ref[...])
        dcat = jnp.concatenate([d, dn], axis=0)
        d1 = pltpu.roll(dcat, tm + 8 - 1, 0)[:tm]
        d2 = pltpu.roll(dcat, tm + 8 - 2, 0)[:tm]
        dz = cw_ref[2:3, :] * d + cw_ref[1:2, :] * d1 + cw_ref[0:1, :] * d2
        u = u_ref[...].astype(F32)
        cg = c_ref[...].astype(F32)
        du_s[slot] = (dz * cg).astype(BF16)
        dc_s[slot] = (dz * u).astype(BF16)
        for cp in column_copies(slot, pl.multiple_of(i * tm, tm)):
            cp.start()

        z = cg * u
        g_ref[0:1, :] += jnp.sum(d2 * z, axis=0, keepdims=True)
        g_ref[1:2, :] += jnp.sum(d1 * z, axis=0, keepdims=True)
        g_ref[2:3, :] += jnp.sum(d * z, axis=0, keepdims=True)

        @pl.when(i == T // tm - 1)
        def _():
            for s in range(2):
                for cp in column_copies(s, 0):
                    cp.wait()

    n_tiles = T // tm
    next_rows = lambda i: jnp.minimum((i + 1) * (tm // 8), T // 8 - 1)
    return pl.pallas_call(
        body, name="conv_bwd",
        grid=(n_tiles,),
        in_specs=[pl.BlockSpec((tm, D), lambda i: (i, 0)),
                  pl.BlockSpec((8, D), lambda i: (next_rows(i), 0)),
                  pl.BlockSpec((tm, D), lambda i: (i, KB_U)),
                  pl.BlockSpec((tm, D), lambda i: (i, KB_CG)),
                  pl.BlockSpec((3, D), lambda i: (0, 0)),
                  pl.BlockSpec(memory_space=pl.ANY)],
        out_specs=(pl.BlockSpec(memory_space=pl.ANY),
                   pl.BlockSpec((8, D), lambda i: (0, 0))),
        out_shape=(jax.ShapeDtypeStruct((T, NCOL), BF16),
                   jax.ShapeDtypeStruct((8, D), F32)),
        scratch_shapes=[pltpu.VMEM((2, tm, D), BF16), pltpu.VMEM((2, tm, D), BF16), pltpu.SemaphoreType.DMA((2, 2))],
        input_output_aliases={5: 0},
        compiler_params=pltpu.CompilerParams(vmem_limit_bytes=VMEM_LIMIT),
    )(dyc, dyc, proj, proj, conv_w, dproj)


def _dh_dx(dproj, w_in_all, x2, dxd, mod3, chip_sums, hops=(), parts0=None):
    tm = 512
    per_seq = S // tm
    n_pass, _, width = w_in_all.shape
    n = len(chip_sums)
    n_in = 5 + n + (0 if parts0 is None else 1)

    def body(*refs):
        d_ref, w_ref, x_ref, dxd_ref, mod_ref = refs[:5]
        ins = refs[5:5 + n]
        gx_ref, vec_ref = refs[n_in:n_in + 2]
        outs = refs[n_in + 2:n_in + 2 + n]
        acc, send_sems, recv_sems, local_sems = refs[n_in + 2 + n:]
        jj, i = pl.program_id(0), pl.program_id(1)

        @pl.when((i == 0) & (jj == 0))
        def _():
            vec_ref[...] = jnp.zeros_like(vec_ref)
            if n:
                sends, _, mine = _chip_copies(ins, outs, send_sems, recv_sems, local_sems, hops)
                for cp in sends + mine:
                    cp.start()

        if n:
            @pl.when((i == T // tm - 1) & (jj == n_pass - 1))
            def _():
                sends, arrivals, mine = _chip_copies(ins, outs, send_sems, recv_sems, local_sems, hops)
                for cp in arrivals:
                    cp.wait_recv()
                for cp in sends:
                    cp.wait_send()
                for cp in mine:
                    cp.wait()

        def partial():
            return _dot_nt(d_ref[...], w_ref[...])

        @pl.when(jj == 0)
        def _():
            acc[i] = partial()

        @pl.when((jj > 0) & (jj < n_pass - 1))
        def _():
            acc[i] += partial()

        @pl.when(jj == n_pass - 1)
        def _():
            dh = acc[i] + partial()
            bidx = i // per_seq
            gx_ref[...] = dxd_ref[...] + dh * (1.0 + mod_ref[0, 1:2, :])
            dshift = jnp.sum(dh, axis=0, keepdims=True)
            dscale = jnp.sum(dh * x_ref[...], axis=0, keepdims=True)
            vec_ref[0:1, :] += jnp.where(bidx == 0, dshift, 0.0)
            vec_ref[1:2, :] += jnp.where(bidx == 1, dshift, 0.0)
            vec_ref[2:3, :] += jnp.where(bidx == 0, dscale, 0.0)
            vec_ref[3:4, :] += jnp.where(bidx == 1, dscale, 0.0)

    def last_pass(jj, i):
        return jnp.where(jj == n_pass - 1, i, 0)

    any_spec = pl.BlockSpec(memory_space=pl.ANY)
    res = pl.pallas_call(
        body, name="dh_dx",
        grid=(n_pass, T // tm),
        in_specs=[
            pl.BlockSpec((tm, width), lambda jj, i: (i, jj)),
            pl.BlockSpec((None, D, width), lambda jj, i: (jj, 0, 0)),
            pl.BlockSpec((tm, D), lambda jj, i: (last_pass(jj, i), 0)),
            pl.BlockSpec((tm, D), lambda jj, i: (last_pass(jj, i), 0)),
            pl.BlockSpec((1, 3, D), lambda jj, i: (last_pass(jj, i) // per_seq, 0, 0))]
                 + [any_spec] * (n_in - 5),
        out_specs=(pl.BlockSpec((tm, D), lambda jj, i: (last_pass(jj, i), 0)),
                   pl.BlockSpec((8, D), lambda jj, i: (0, 0))) + (any_spec,) * n,
        out_shape=(jax.ShapeDtypeStruct((T, D), F32), jax.ShapeDtypeStruct((8, D), F32))
                  + tuple(jax.ShapeDtypeStruct(a.shape, a.dtype) for a in chip_sums),
        scratch_shapes=[pltpu.VMEM((T // tm, tm, D), F32), pltpu.SemaphoreType.DMA((max(3 * n, 1),)),
                        pltpu.SemaphoreType.DMA((max(3 * n, 1),)), pltpu.SemaphoreType.DMA((max(n, 1),))],
        input_output_aliases={} if parts0 is None else {5 + n: 2},
        compiler_params=pltpu.CompilerParams(vmem_limit_bytes=VMEM_LIMIT),
    )(dproj, w_in_all, x2, dxd, mod3, *chip_sums, *([] if parts0 is None else [parts0]))
    return res[0], res[1], res[2:]


def _adam_step(g, w, m, v):
    nm = ADAM_B1 * m + (1.0 - ADAM_B1) * g
    nv = ADAM_B2 * v + (1.0 - ADAM_B2) * (g * g)
    m_hat = nm / (1.0 - ADAM_B1 ** ADAM_STEP)
    v_hat = nv / (1.0 - ADAM_B2 ** ADAM_STEP)
    return -ADAM_LR * (m_hat / (jnp.sqrt(v_hat) + ADAM_EPS) + ADAM_WD * w), nm, nv


def _adamw(parts, w, m, v, name, row_tile=None):
    n_parts, rows, cols = parts.shape
    tr = rows if row_tile is None else row_tile

    def body(p_ref, w_ref, m_ref, v_ref, g_ref, d_ref, nm_ref, nv_ref):
        g = p_ref[0].astype(F32)
        for s in range(1, n_parts):
            g = g + p_ref[s].astype(F32)
        g_ref[...] = g
        d_ref[...], nm_ref[...], nv_ref[...] = _adam_step(g, w_ref[...], m_ref[...], v_ref[...])

    blk = pl.BlockSpec((tr, cols), lambda i: (i, 0))
    shp = jax.ShapeDtypeStruct((rows, cols), F32)
    return pl.pallas_call(
        body, name=name,
        grid=(rows // tr,),
        in_specs=[pl.BlockSpec((n_parts, tr, cols), lambda i: (0, i, 0)), blk, blk, blk],
        out_specs=(blk, blk, blk, blk),
        out_shape=(shp, shp, shp, shp),
        compiler_params=pltpu.CompilerParams(vmem_limit_bytes=VMEM_LIMIT),
    )(parts, w, m, v)


def _multi_adamw(parts_list, params, name):
    n = len(params)
    flat = [t for wmv in params for t in wmv]

    def body(*refs):
        parts, ins, outs = refs[:n], refs[n:4 * n], refs[4 * n:]
        for p in range(n):
            g = parts[p][0].astype(F32)
            for s in range(1, parts[p].shape[0]):
                g = g + parts[p][s].astype(F32)
            w_ref, m_ref, v_ref = ins[3 * p:3 * p + 3]
            g_ref, d_ref, nm_ref, nv_ref = outs[4 * p:4 * p + 4]
            g_ref[...] = g
            d_ref[...], nm_ref[...], nv_ref[...] = _adam_step(g, w_ref[...], m_ref[...], v_ref[...])

    out_shape = []
    for w, _, _ in params:
        out_shape += [jax.ShapeDtypeStruct(w.shape, F32)] * 4
    res = pl.pallas_call(body, name=name, out_shape=tuple(out_shape))(*parts_list, *flat)
    return [res[4 * p:4 * p + 4] for p in range(n)]


def _small_updates(small_g, dmod_all, rel_parts, params):
    flat = [t for wmv in params for t in wmv]

    def body(sg_ref, dm_ref, rp_ref, *refs):
        ins, outs = refs[:len(flat)], refs[len(flat):]

        def over_devices(row):
            tot = sg_ref[0, row:row + 1, :]
            for s in range(1, N_DEV):
                tot = tot + sg_ref[s, row:row + 1, :]
            return tot

        g_b_ada = dm_ref[0:1, :]
        for r in range(1, N_DEV * BL):
            g_b_ada = g_b_ada + dm_ref[r:r + 1, :]
        g_rel = rp_ref[0]
        for s in range(1, N_DEV):
            g_rel = g_rel + rp_ref[s]
        grads = [g_b_ada, over_devices(0), g_rel, over_devices(1), over_devices(2)]
        outs[0][...] = jnp.sum(over_devices(3), axis=1, keepdims=True)
        for p, g in enumerate(grads):
            w_ref, m_ref, v_ref = ins[3 * p:3 * p + 3]
            g_ref, d_ref, nm_ref, nv_ref = outs[1 + 4 * p:5 + 4 * p]
            g_ref[...] = g
            d_ref[...], nm_ref[...], nv_ref[...] = _adam_step(g, w_ref[...], m_ref[...], v_ref[...])

    out_shape = [jax.ShapeDtypeStruct((1, 1), F32)]
    for w, _, _ in params:
        out_shape += [jax.ShapeDtypeStruct(w.shape, F32)] * 4
    res = pl.pallas_call(body, name="small_updates", out_shape=tuple(out_shape))(small_g, dmod_all, rel_parts, *flat)
    return res[0], [res[1 + 4 * p:5 + 4 * p] for p in range(len(params))]


def _attn_fwd_dense(proj, bias):
    nq = 4
    rows = nq * QB
    nsb = S // rows

    def body(q_ref, k_ref, v_ref, kp_ref, vp_ref, b_ref, o_ref, l_ref, ls0, ls1, ls2, ls3):
        ls = [ls0, ls1, ls2, ls3]
        n = pl.program_id(1)
        lane = lax.broadcasted_iota(jnp.int32, (QB, 128), 1)
        units = [(h, j) for h in range(4) for j in range(nq)]

        def keys(cur_ref, prev_ref, h, j):
            sl = slice(h * HD, (h + 1) * HD)
            if j == 0:
                return jnp.concatenate([prev_ref[:, sl], cur_ref[0:QB, sl]], axis=0)
            return cur_ref[(j - 1) * QB:(j + 1) * QB, sl]

        q = jnp.stack([q_ref[j * QB:(j + 1) * QB, h * HD:(h + 1) * HD] for h, j in units])
        k = jnp.stack([keys(k_ref, kp_ref, h, j) for h, j in units])
        v = jnp.stack([keys(v_ref, vp_ref, h, j) for h, j in units])
        bias_b = jnp.stack([b_ref[jnp.minimum(n, 1), h] if j == 0 else b_ref[1, h] for h, j in units])
        s = jnp.einsum("uqd,ukd->uqk", q, k, preferred_element_type=F32) * SCALE + bias_b
        m = jnp.max(s, axis=-1, keepdims=True)
        p = jnp.exp(s - m)
        l = jnp.sum(p, axis=-1, keepdims=True)
        o = jnp.einsum("uqk,ukd->uqd", p.astype(BF16), v, preferred_element_type=F32) / l
        lse = m + jnp.log(l)
        for i, (h, j) in enumerate(units):
            o_ref[j * QB:(j + 1) * QB, h * HD:(h + 1) * HD] = o[i]
            ls[h][j * QB:(j + 1) * QB, :] = jnp.where(lane == h, lse[i], 0.0)
        l_ref[...] = (ls[0][...] + ls[1][...]) + (ls[2][...] + ls[3][...])

    def row(b, n):
        return b * nsb + n

    def prev(b, n):
        return jnp.maximum((b * nsb + n) * nq - 1, 0)

    in_specs = [
        pl.BlockSpec((rows, GW), lambda b, n: (row(b, n), CB_Q)),
        pl.BlockSpec((rows, GW), lambda b, n: (row(b, n), CB_K)),
        pl.BlockSpec((rows, GW), lambda b, n: (row(b, n), CB_V)),
        pl.BlockSpec((QB, GW), lambda b, n: (prev(b, n), CB_K)),
        pl.BlockSpec((QB, GW), lambda b, n: (prev(b, n), CB_V)),
        pl.BlockSpec((None, 2, 4, QB, 2 * QB), lambda b, n: (0, 0, 0, 0, 0)),
    ]
    return pl.pallas_call(
        body, name="attn_fwd0",
        grid=(BL, nsb),
        in_specs=in_specs,
        out_specs=(pl.BlockSpec((rows, GW), lambda b, n: (row(b, n), 0)),
                   pl.BlockSpec((rows, 128), lambda b, n: (row(b, n), 0))),
        out_shape=(jax.ShapeDtypeStruct((T, GW), F32), jax.ShapeDtypeStruct((T, 128), F32)),
        scratch_shapes=[pltpu.VMEM((rows, 128), F32)] * 4,
        compiler_params=pltpu.CompilerParams(vmem_limit_bytes=VMEM_LIMIT),
    )(proj, proj, proj, proj, proj, bias)


def _attn_bwd_dense(proj, d_out, stats, bias, dproj):
    nq = 4
    rows = nq * QB
    nsb = S // rows
    cols_q, cols_k, cols_v = CB * CB_Q, CB * CB_K, CB * CB_V

    def body(q_ref, k_ref, v_ref, do_ref, st_ref, kp_ref, vp_ref, b_ref, _, out_ref, db_ref,
             sq, sk, sv, carry, sems):
        b, n = pl.program_id(0), pl.program_id(1)
        units = [(h, j) for h in range(4) for j in range(nq)]

        @pl.when((b == 0) & (n == 0))
        def _():
            db_ref[...] = jnp.zeros_like(db_ref)

        @pl.when(n == 0)
        def _():
            carry[...] = jnp.zeros_like(carry)

        step = b * (nsb + 1) + n
        slot = step % 2

        def block_copies(s, position, block):
            part = pl.ds(position * QB, QB)
            return _column_copies([(sq.at[s, part], cols_q), (sk.at[s, part], cols_k), (sv.at[s, part], cols_v)],
                                  out_ref, pl.multiple_of(block * QB, QB), sems.at[s, position])

        def wait_blocks(s, positions):
            for position in positions:
                for cp in block_copies(s, position, 0):
                    cp.wait()

        before = (step - 2) % (nsb + 1)

        @pl.when((step >= 2) & (before > 0))
        def _():
            wait_blocks(slot, [0])

        @pl.when((step >= 2) & (before < nsb))
        def _():
            wait_blocks(slot, range(1, nq))

        def write(first_block, position, count):
            for j in range(count):
                for cp in block_copies(slot, position + j, first_block + j):
                    cp.start()

        @pl.when(n == nsb)
        def _():
            sq[slot, 0:QB, :] = carry[:, 0:GW].astype(BF16)
            sk[slot, 0:QB, :] = carry[:, GW:2 * GW].astype(BF16)
            sv[slot, 0:QB, :] = carry[:, 2 * GW:3 * GW].astype(BF16)
            write((b + 1) * nsb * nq - 1, 0, 1)

            @pl.when(b == BL - 1)
            def _():
                wait_blocks(slot, [0])
                wait_blocks(1 - slot, range(nq))

        @pl.when(n < nsb)
        def _():
            def keys(cur_ref, prev_ref, h, j):
                sl = slice(h * HD, (h + 1) * HD)
                if j == 0:
                    return jnp.concatenate([prev_ref[:, sl], cur_ref[0:QB, sl]], axis=0)
                return cur_ref[(j - 1) * QB:(j + 1) * QB, sl]

            def block(ref, h, j):
                return ref[j * QB:(j + 1) * QB, h * HD:(h + 1) * HD]

            q = jnp.stack([block(q_ref, h, j) for h, j in units])
            do = jnp.stack([block(do_ref, h, j) for h, j in units])
            k = jnp.stack([keys(k_ref, kp_ref, h, j) for h, j in units])
            v = jnp.stack([keys(v_ref, vp_ref, h, j) for h, j in units])
            bias_b = jnp.stack([b_ref[jnp.minimum(n, 1), h] if j == 0 else b_ref[1, h] for h, j in units])
            lse = jnp.stack([st_ref[j * QB:(j + 1) * QB, h:h + 1] for h, j in units])
            delta = jnp.stack([st_ref[j * QB:(j + 1) * QB, 4 + h:5 + h] for h, j in units])
            s = jnp.einsum("uqd,ukd->uqk", q, k, preferred_element_type=F32) * SCALE + bias_b
            p = jnp.exp(s - lse)
            ds = p * (jnp.einsum("uqd,ukd->uqk", do, v, preferred_element_type=F32) - delta)
            for h in range(4):
                tot = ds[h * nq]
                for j in range(1, nq):
                    tot = tot + ds[h * nq + j]
                db_ref[h] += tot
            dsb, pb = ds.astype(BF16), p.astype(BF16)
            dq = jnp.einsum("uqk,ukd->uqd", dsb, k, preferred_element_type=F32) * SCALE
            dk = jnp.einsum("uqk,uqd->ukd", dsb, q, preferred_element_type=F32) * SCALE
            dv = jnp.einsum("uqk,uqd->ukd", pb, do, preferred_element_type=F32)
            for h in range(4):
                sl = slice(h * HD, (h + 1) * HD)
                u0, last = h * nq, h * nq + nq - 1
                sq[slot, 0:QB, sl] = carry[:, sl].astype(BF16)
                sk[slot, 0:QB, sl] = (carry[:, GW + h * HD:GW + (h + 1) * HD] + dk[u0, :QB]).astype(BF16)
                sv[slot, 0:QB, sl] = (carry[:, 2 * GW + h * HD:2 * GW + (h + 1) * HD] + dv[u0, :QB]).astype(BF16)
                for j in range(nq - 1):
                    pos = slice((j + 1) * QB, (j + 2) * QB)
                    sq[slot, pos, sl] = dq[u0 + j].astype(BF16)
                    sk[slot, pos, sl] = (dk[u0 + j, QB:] + dk[u0 + j + 1, :QB]).astype(BF16)
                    sv[slot, pos, sl] = (dv[u0 + j, QB:] + dv[u0 + j + 1, :QB]).astype(BF16)
                carry[:, sl] = dq[last]
                carry[:, GW + h * HD:GW + (h + 1) * HD] = dk[last, QB:]
                carry[:, 2 * GW + h * HD:2 * GW + (h + 1) * HD] = dv[last, QB:]

            @pl.when(n == 0)
            def _():
                write(b * nsb * nq, 1, nq - 1)

            @pl.when(n > 0)
            def _():
                write((b * nsb + n) * nq - 1, 0, nq)

    def row(b, n):
        return b * nsb + jnp.minimum(n, nsb - 1)

    def prev(b, n):
        return jnp.maximum(row(b, n) * nq - 1, 0)

    in_specs = [
        pl.BlockSpec((rows, GW), lambda b, n: (row(b, n), CB_Q)),
        pl.BlockSpec((rows, GW), lambda b, n: (row(b, n), CB_K)),
        pl.BlockSpec((rows, GW), lambda b, n: (row(b, n), CB_V)),
        pl.BlockSpec((rows, GW), lambda b, n: (row(b, n), 0)),
        pl.BlockSpec((rows, 128), lambda b, n: (row(b, n), 0)),
        pl.BlockSpec((QB, GW), lambda b, n: (prev(b, n), CB_K)),
        pl.BlockSpec((QB, GW), lambda b, n: (prev(b, n), CB_V)),
        pl.BlockSpec((None, 2, 4, QB, 2 * QB), lambda b, n: (0, 0, 0, 0, 0)),
        pl.BlockSpec(memory_space=pl.ANY),
    ]
    return pl.pallas_call(
        body, name="attn_bwd0",
        grid=(BL, nsb + 1),
        in_specs=in_specs,
        out_specs=(pl.BlockSpec(memory_space=pl.ANY),
                   pl.BlockSpec((4, QB, 2 * QB), lambda b, n: (0, 0, 0))),
        out_shape=(jax.ShapeDtypeStruct((T, NCOL), BF16),
                   jax.ShapeDtypeStruct((4, QB, 2 * QB), F32)),
        scratch_shapes=[pltpu.VMEM((2, rows, GW), BF16)] * 3
                       + [pltpu.VMEM((QB, 3 * GW), F32), pltpu.SemaphoreType.DMA((2, nq, 3))],
        input_output_aliases={8: 0},
        compiler_params=pltpu.CompilerParams(vmem_limit_bytes=VMEM_LIMIT),
    )(proj, proj, proj, d_out, stats, proj, proj, bias, dproj)


def _attention_forward(proj, rel_bias):
    expand_lanes, grad_lanes, masks = (jnp.asarray(t) for t in _bucket_maps())
    bias = _bias_expand(rel_bias, expand_lanes, masks)
    fwd = [_attn_fwd_dense(proj, bias)] + [_attn_fwd(proj, bias, g) for g in (1, 2)]
    return bias, grad_lanes, [f[0] for f in fwd], [f[1] for f in fwd]


def _local_step(x2, tgt2, mod3, h, proj, attn, w_ao, w_co, w_o, conv_w, conv_b, ln_g, ln_b):
    bias, buckets, o_g, lse_g = attn

    (dproj, dyc, d_o, stats, dxd, gw_o, gw_co, gw_ao, tail_vec) = _tail(
        x2, tgt2, mod3, o_g, lse_g, proj, w_ao, w_co, w_o, conv_w, conv_b, ln_g, ln_b)

    dproj, db = _attn_bwd_dense(proj, d_o, stats, bias, dproj)
    dbias = [db]
    for g in (1, 2):
        dproj, db = _attn_bwd(proj, d_o, stats, bias, dproj, g)
        dbias.append(db)
    g_rel_bias = _bias_grad(*dbias, buckets)
    dproj, conv_vec = _conv_bwd(dyc, proj, conv_w, dproj)

    gw_ao = jnp.transpose(gw_ao.reshape(GW, N_DEV, D // N_DEV), (1, 0, 2))
    return dproj, dxd, gw_ao, gw_co, gw_o, conv_vec, g_rel_bias, tail_vec


def kernel(x, c, w_ada, b_ada, w_in, conv_w, conv_b, rel_bias, w_attn_out, w_conv_out, w_o, ln_g, ln_b, loss_target, m_w_ada, m_b_ada, m_w_in, m_conv_w, m_conv_b, m_rel_bias, m_w_attn_out, m_w_conv_out, m_w_o, m_ln_g, m_ln_b, v_w_ada, v_b_ada, v_w_in, v_conv_w, v_conv_b, v_rel_bias, v_w_attn_out, v_w_conv_out, v_w_o, v_ln_g, v_ln_b):
    me = _my_index()
    x2 = x.reshape(T, D)
    tgt2 = loss_target.reshape(T, D)

    b_cols = lax.dynamic_slice(b_ada, (0, me * ADA_SHARD), (1, ADA_SHARD))
    c_g, mod_in = _mod_exchange(jnp.pad(c, ((0, 8 - BL), (0, 0))), w_ada[0], b_cols)
    c_all = c_g[:, 0:BL, :].reshape(N_DEV * BL, D)
    mod3 = jnp.transpose(mod_in[:, 0:BL, :], (1, 0, 2)).reshape(BL, 3, D)

    h = _prep_h(x2, mod3)
    rows_shape = jax.ShapeDtypeStruct((N_DEV, D // N_DEV, D), BF16)
    proj, w_in_all, (w_ao_g, w_co_g, w_o_g, conv_w_g) = _gather_proj(
        _shard_order(), h, w_in[0].astype(BF16), 1024,
        ([w_attn_out[0].astype(BF16), w_conv_out[0].astype(BF16), w_o[0].astype(BF16), conv_w[0]],
         [jax.ShapeDtypeStruct((N_DEV, GW, D // N_DEV), BF16), rows_shape, rows_shape,
          jax.ShapeDtypeStruct((N_DEV, 3, D // N_DEV), F32)]))

    attn = _attention_forward(proj, rel_bias)
    w_ao_full = jnp.transpose(w_ao_g, (1, 0, 2)).reshape(GW, D)
    w_co_full = w_co_g.reshape(D, D)
    w_o_full = w_o_g.reshape(D, D)
    conv_w_full = jnp.transpose(conv_w_g, (1, 0, 2)).reshape(3, D)

    (dproj, dxd, gw_ao, gw_co, gw_o, conv_vec, g_rel_bias, tail_vec) = _local_step(
        x2, tgt2, mod3, h, proj, attn, w_ao_full, w_co_full, w_o_full,
        conv_w_full, conv_b, ln_g, ln_b)

    g_conv_w_blocks = jnp.transpose(conv_vec[0:3].reshape(3, N_DEV, D // N_DEV), (1, 0, 2))
    partials = [gw_ao, gw_co.reshape(N_DEV, D // N_DEV, D), gw_o.reshape(N_DEV, D // N_DEV, D), g_conv_w_blocks]
    w_in_sums, w_in_parts, sib = _gw_in_pair(
        _slice_order(), h, dproj, partials,
        [jax.ShapeDtypeStruct((4, GW, D // N_DEV), BF16),
         jax.ShapeDtypeStruct((4, D // N_DEV, D), BF16),
         jax.ShapeDtypeStruct((4, D // N_DEV, D), BF16),
         jax.ShapeDtypeStruct((4, 3, D // N_DEV), F32)])
    core = lax.axis_index("c").astype(jnp.int32).reshape(1)
    chip_sums = [w_in_sums] + list(_pair_add(core, partials, sib))
    hops = [(3,)] + [(1, 2, 3)] * 4
    grad_x, mod_vec, (r_in, r_ao, r_co, r_o, r_cw) = _dh_dx(
        dproj, w_in_all, x2, dxd, mod3, chip_sums, hops, w_in_parts)

    small = jnp.concatenate([
        tail_vec[0:4],
        jnp.pad(g_rel_bias.reshape(1, N_BUCKETS * N_HEADS), ((0, 0), (0, D - N_BUCKETS * N_HEADS))),
        jnp.zeros((3, D), F32)], axis=0)
    dmod = jnp.concatenate([mod_vec[0:2], mod_vec[2:4], tail_vec[4:6]], axis=1)
    small_g, dmod_g = _all_gather(
        [small, dmod],
        [jax.ShapeDtypeStruct((N_DEV, 8, D), F32), jax.ShapeDtypeStruct((N_DEV, BL, 3 * D), F32)],
        "gather_small")
    dmod_all = dmod_g.reshape(N_DEV * BL, 3 * D)
    small_names = ["b_ada", "conv_b", "rel_bias", "ln_g", "ln_b"]
    small_params = [(b_ada, m_b_ada, v_b_ada), (conv_b, m_conv_b, v_conv_b), (rel_bias, m_rel_bias, v_rel_bias),
                    (ln_g, m_ln_g, v_ln_g), (ln_b, m_ln_b, v_ln_b)]
    loss, small_res = _small_updates(
        small_g, dmod_all, small_g[:, 4, :N_BUCKETS * N_HEADS].reshape(N_DEV, N_BUCKETS, N_HEADS), small_params)
    loss = loss.reshape(())

    dmod_cols = lax.dynamic_slice(dmod_all, (0, me * ADA_SHARD), (N_DEV * BL, ADA_SHARD))
    res = {
        "w_ada": tuple(t[None] for t in _w_ada_update(jnp.transpose(c_all), dmod_cols,
                                                      w_ada[0], m_w_ada[0], v_w_ada[0])),
        "w_in": tuple(t[None] for t in _adamw(r_in, w_in[0], m_w_in[0], v_w_in[0], "adam_w_in", 128)),
    }
    mid_names = ["conv_w", "w_attn_out", "w_conv_out", "w_o"]
    mid_parts = [r_cw, r_ao, r_co, r_o]
    mid_full = [(conv_w, m_conv_w, v_conv_w), (w_attn_out, m_w_attn_out, v_w_attn_out),
                (w_conv_out, m_w_conv_out, v_w_conv_out), (w_o, m_w_o, v_w_o)]
    mid_res = _multi_adamw(mid_parts, [tuple(t[0] for t in wmv) for wmv in mid_full], "adam_mid")
    for nm, wmv, outs4 in zip(mid_names, mid_full, mid_res):
        res[nm] = tuple(t[None] for t in outs4)
    res.update(dict(zip(small_names, small_res)))
    order = ["w_ada", "b_ada", "w_in", "conv_w", "conv_b", "rel_bias", "w_attn_out", "w_conv_out",
             "w_o", "ln_g", "ln_b"]
    outs = [loss, grad_x.reshape(BL, S, D)]
    for k in range(4):
        outs += [res[name][k] for name in order]
    return tuple(outs)
```

```python
import math

import numpy as np
import jax
import jax.numpy as jnp
from jax import lax
from jax.experimental import pallas as pl
from jax.experimental.pallas import tpu as pltpu

F32 = jnp.float32
BF16 = jnp.bfloat16
MESH = pl.DeviceIdType.MESH

N_DEV = 8
D = 1024
S = 2048
BL = 2
T = BL * S
NCOL = 11264
SHARD = NCOL // N_DEV
CB = 512
NCB = NCOL // CB
HD = 128
GW = 512
QB = 128
DILATIONS = (1, 4, 16)
N_STEPS = 128
N_BUCKETS = 32
N_HEADS = 12
ALPHA = 2.0 ** 0.25
LN_EPS = 1e-5
NEG_INF = -1e30
SCALE = HD ** -0.5
ADA_SHARD = 3 * D // N_DEV

CB_Q, CB_K, CB_V, CB_GA = 0, 3, 6, 9
KB_U, KB_BG, KB_CG, KB_GC, KB_MA, KB_MC = 5, 6, 7, 8, 9, 10

ADAM_LR, ADAM_B1, ADAM_B2, ADAM_EPS, ADAM_WD, ADAM_STEP = 0.001, 0.9, 0.999, 1e-08, 0.01, 10

VMEM_LIMIT = 56 * 1024 * 1024
VMEM_LIMIT_HIGH = 62 * 1024 * 1024


def _dot(a, b):
    return jnp.dot(a, b, preferred_element_type=F32)


def _dot_nt(a, b):
    return lax.dot_general(a, b, (((1,), (1,)), ((), ())), preferred_element_type=F32)


def _dot_tn(a, b):
    return lax.dot_general(a, b, (((0,), (0,)), ((), ())), preferred_element_type=F32)


def _sigmoid(v):
    return 1.0 / (1.0 + jnp.exp(-v))


def _column_copies(pieces, dst_hbm, row0, sems):
    copies = []
    for k, (src, col0) in enumerate(pieces):
        rows, width = src.shape
        copies.append(pltpu.make_async_copy(
            src, dst_hbm.at[pl.ds(row0, rows), pl.ds(col0, width)], sems.at[k]))
    return copies


def _my_index():
    return 4 * lax.axis_index("x") + 2 * lax.axis_index("y") + lax.axis_index("c")


class _Gather:
    def __init__(self, ins, outs, stage, send_sems, recv_sems, local_sems):
        self.ins, self.outs, self.stage = ins, outs, stage
        self.send_sems, self.recv_sems, self.local_sems = send_sems, recv_sems, local_sems
        x, y, c = lax.axis_index("x"), lax.axis_index("y"), lax.axis_index("c")
        self.c = c
        self.me, self.sibling = (x, y, c), (x, y, 1 - c)
        self.chips = [(1 - x, y), (x, 1 - y), (1 - x, 1 - y)]

    @staticmethod
    def scratch(arrs):
        n = len(arrs)
        return ([pltpu.SemaphoreType.DMA((7 * n,)), pltpu.SemaphoreType.DMA((7 * n,)),
                 pltpu.SemaphoreType.DMA((n,))] + [pltpu.VMEM(a.shape, a.dtype) for a in arrs])

    def _copy(self, a, k, block, to, src=None):
        dst = self.outs[a].at[4 * block[0] + 2 * block[1] + block[2]]
        return pltpu.make_async_remote_copy(
            src_ref=dst if src is None else src, dst_ref=dst,
            send_sem=self.send_sems.at[a * 7 + k], recv_sem=self.recv_sems.at[a * 7 + k],
            device_id=to, device_id_type=MESH)

    def _first(self):
        first = []
        for a in range(len(self.ins)):
            first.append(self._copy(a, 0, self.me, self.sibling, src=self.ins[a]))
            first += [self._copy(a, 1 + j, self.me, (*chip, self.c), src=self.ins[a])
                      for j, chip in enumerate(self.chips)]
        return first

    def _mine(self):
        me = self.me
        return [pltpu.make_async_copy(self.stage[a], self.outs[a].at[4 * me[0] + 2 * me[1] + me[2]],
                                      self.local_sems.at[a]) for a in range(len(self.ins))]

    def begin(self):
        for cp in self._first():
            cp.start()
        loads = [pltpu.make_async_copy(self.ins[a], self.stage[a], self.local_sems.at[a])
                 for a in range(len(self.ins))]
        for cp in loads:
            cp.start()
        for cp in loads:
            cp.wait()
        for cp in self._mine():
            cp.start()

    def finish(self):
        n, c, me, sibling = len(self.ins), self.c, self.me, self.sibling
        passed = []
        for j, chip in enumerate(self.chips):
            for a in range(n):
                self._copy(a, 1 + j, (*chip, c), me).wait_recv()
                fwd = self._copy(a, 4 + j, (*chip, c), sibling)
                fwd.start()
                passed.append(fwd)
        for a in range(n):
            self._copy(a, 0, sibling, me).wait_recv()
        for j, chip in enumerate(self.chips):
            for a in range(n):
                self._copy(a, 4 + j, (*chip, 1 - c), me).wait_recv()
        for cp in self._first() + passed:
            cp.wait_send()
        for cp in self._mine():
            cp.wait()


def _all_gather(arrs, out_shapes, name):
    n = len(arrs)

    def body(*refs):
        g = _Gather(refs[:n], refs[n:2 * n], refs[2 * n + 3:], *refs[2 * n:2 * n + 3])
        g.begin()
        g.finish()

    any_spec = pl.BlockSpec(memory_space=pl.ANY)
    return pl.pallas_call(
        body, name=name,
        out_shape=tuple(out_shapes),
        in_specs=[any_spec] * n,
        out_specs=tuple([any_spec] * n),
        scratch_shapes=_Gather.scratch(arrs),
    )(*arrs)


def _neighbour_chips():
    x, y, c = lax.axis_index("x"), lax.axis_index("y"), lax.axis_index("c")
    first = (jnp.where(c == 0, 1 - x, x), jnp.where(c == 0, y, 1 - y))
    second = (jnp.where(c == 0, x, 1 - x), jnp.where(c == 0, 1 - y, y))
    return first, second, (1 - x, 1 - y)


def _slice_order():
    x, y, c = lax.axis_index("x"), lax.axis_index("y"), lax.axis_index("c")
    nb1, nb2, diag = _neighbour_chips()
    slots = []
    for mine, theirs in ((nb1, nb2), (nb2, nb1), (diag, diag), ((x, y), (x, y))):
        slots += [2 * (2 * theirs[0] + theirs[1]) + 1 - c, 2 * (2 * mine[0] + mine[1]) + c]
    return jnp.stack(slots).astype(jnp.int32)


def _gw_in_pair(order, h, dproj, smalls, small_shapes4):
    kk, m = h.shape
    tk = kk
    nk = kk // tk
    ncols = dproj.shape[1] // N_DEV
    n = len(smalls)

    def body(order_ref, h_ref, d_ref, *rest):
        ins = rest[:n]
        sums_hbm, parts_hbm = rest[n], rest[n + 1]
        sib = rest[n + 2:2 * n + 2]
        (acc, sendbuf, recvbuf, sumbuf, send_sems, recv_sems, local_sem, ssend, srecv,
         isend, irecv) = rest[2 * n + 2:]
        js, k = pl.program_id(0), pl.program_id(1)
        x, y, c = lax.axis_index("x"), lax.axis_index("y"), lax.axis_index("c")
        sibling = (x, y, 1 - c)
        my_chip = 2 * x + y
        nb1, nb2, _ = _neighbour_chips()
        near = [(*nb1, c), (*nb2, c)]

        def ici_copy(p, out_chip):
            peer = near[p] if isinstance(p, int) else tuple(jnp.where(p == 0, a, b) for a, b in zip(*near))
            return pltpu.make_async_remote_copy(
                src_ref=sumbuf.at[p], dst_ref=parts_hbm.at[out_chip],
                send_sem=isend.at[p], recv_sem=irecv.at[p], device_id=peer, device_id_type=MESH)

        def small_copies():
            return [pltpu.make_async_remote_copy(
                        src_ref=ins[a].at[2 * q + 1 - c], dst_ref=sib[a].at[q],
                        send_sem=ssend.at[a * 4 + q], recv_sem=srecv.at[a * 4 + q],
                        device_id=sibling, device_id_type=MESH)
                    for a in range(n) for q in range(4)]

        def slice_copy(p):
            return pltpu.make_async_remote_copy(
                src_ref=sendbuf, dst_ref=recvbuf.at[p], send_sem=send_sems.at[p], recv_sem=recv_sems.at[p],
                device_id=sibling, device_id_type=MESH)

        def sum_copy(p):
            return pltpu.make_async_copy(sumbuf.at[2], sums_hbm.at[order_ref[2 * p] // 2], local_sem)

        @pl.when((js == 0) & (k == 0))
        def _():
            for cp in small_copies():
                cp.start()

        def partial():
            return _dot_tn(h_ref[...], d_ref[...])

        if nk > 1:
            @pl.when(k == 0)
            def _():
                acc[...] = partial()
        if nk > 2:
            @pl.when((k > 0) & (k < nk - 1))
            def _():
                acc[...] += partial()

        def total():
            return partial() + acc[...] if nk > 1 else partial()

        p = js // 2

        @pl.when((js % 2 == 0) & (k == nk - 1))
        def _():
            @pl.when(p > 0)
            def _():
                slice_copy(p - 1).wait_send()
            sendbuf[...] = total().astype(BF16)
            slice_copy(p).start()

        @pl.when((js % 2 == 1) & (k == nk - 1))
        def _():
            slice_copy(p).wait_recv()

            @pl.when(p == 3)
            def _():
                sum_copy(2).wait()
            sumbuf[jnp.minimum(p, 2)] = (total() + recvbuf[p].astype(F32)).astype(BF16)

            @pl.when(p < 2)
            def _():
                ici_copy(p, my_chip).start()

            @pl.when(p >= 2)
            def _():
                sum_copy(p).start()

        @pl.when((js == N_DEV - 1) & (k == nk - 1))
        def _():
            slice_copy(3).wait_send()
            sum_copy(3).wait()
            for cp in small_copies():
                cp.wait()
            for p in range(2):
                ici_copy(p, 2 * near[p][0] + near[p][1]).wait_recv()
                ici_copy(p, my_chip).wait_send()

    any_spec = pl.BlockSpec(memory_space=pl.ANY)
    res = pl.pallas_call(
        body, name="gw_in_pair",
        grid_spec=pltpu.PrefetchScalarGridSpec(
            num_scalar_prefetch=1,
            grid=(N_DEV, nk),
            in_specs=[pl.BlockSpec((tk, m), lambda js, k, order_ref: (k, 0),
                                   **({"pipeline_mode": pl.Buffered(1)} if nk == 1 else {})),
                      pl.BlockSpec((tk, ncols), lambda js, k, order_ref: (k, order_ref[js]))] + [any_spec] * n,
            out_specs=(any_spec,) * (n + 2),
            scratch_shapes=[pltpu.VMEM((m, ncols) if nk > 1 else (8, 128), F32), pltpu.VMEM((m, ncols), BF16),
                            pltpu.VMEM((4, m, ncols), BF16), pltpu.VMEM((3, m, ncols), BF16),
                            pltpu.SemaphoreType.DMA((4,)), pltpu.SemaphoreType.DMA((4,)),
                            pltpu.SemaphoreType.DMA,
                            pltpu.SemaphoreType.DMA((4 * n,)), pltpu.SemaphoreType.DMA((4 * n,)),
                            pltpu.SemaphoreType.DMA((2,)), pltpu.SemaphoreType.DMA((2,))]),
        out_shape=(jax.ShapeDtypeStruct((4, m, ncols), BF16),) * 2 + tuple(small_shapes4),
        compiler_params=pltpu.CompilerParams(vmem_limit_bytes=VMEM_LIMIT_HIGH),
    )(order, h, dproj, *smalls)
    return res[0], res[1], res[2:]


def _chip_copies(ins, outs, send_sems, recv_sems, local_sems, hops):
    n = len(ins)
    x, y, c = lax.axis_index("x"), lax.axis_index("y"), lax.axis_index("c")
    my_chip = 2 * x + y

    def peer_of(k):
        return ((1 - x) if (k >> 1) & 1 else x, (1 - y) if k & 1 else y, c)

    def copy(a, k, out_chip):
        peer = peer_of(k)
        return pltpu.make_async_remote_copy(
            src_ref=ins[a].at[2 * peer[0] + peer[1]], dst_ref=outs[a].at[out_chip],
            send_sem=send_sems.at[a * 3 + k - 1], recv_sem=recv_sems.at[a * 3 + k - 1],
            device_id=peer, device_id_type=MESH)

    sends = [copy(a, k, my_chip) for k in range(1, 4) for a in range(n) if k in hops[a]]
    arrivals = []
    for k in range(1, 4):
        peer = peer_of(k)
        arrivals += [copy(a, k, 2 * peer[0] + peer[1]) for a in range(n) if k in hops[a]]
    mine = [pltpu.make_async_copy(ins[a].at[my_chip], outs[a].at[my_chip], local_sems.at[a])
            for a in range(n)]
    return sends, arrivals, mine


def _pair_add(core, mines, theirs):
    n = len(mines)

    def body(core_ref, *refs):
        mine, sib, outs = refs[:n], refs[n:2 * n], refs[2 * n:]
        for a in range(n):
            for q in range(4):
                outs[a][q] = (mine[a][2 * q + core_ref[0]].astype(F32)
                              + sib[a][q].astype(F32)).astype(outs[a].dtype)

    return pl.pallas_call(
        body, name="pair_add",
        in_specs=[pl.BlockSpec(memory_space=pltpu.SMEM)] + [pl.BlockSpec(memory_space=pltpu.VMEM)] * (2 * n),
        out_shape=tuple(jax.ShapeDtypeStruct(t.shape, t.dtype) for t in theirs),
    )(core, *mines, *theirs)


def _mod_exchange(c8, w_ada, b_cols):
    cols = w_ada.shape[1]

    def body(c_ref, w_ref, b_ref, call_ref, mod_ref, msend, send1, recv1, send2, recv2):
        x, y, c = lax.axis_index("x"), lax.axis_index("y"), lax.axis_index("c")
        my_slot = 4 * x + 2 * y + c

        def peer_of(k):
            return ((1 - x) if (k >> 2) & 1 else x, (1 - y) if (k >> 1) & 1 else y, (1 - c) if k & 1 else c)

        def slot_of(dev):
            return 4 * dev[0] + 2 * dev[1] + dev[2]

        def exchange(src_of, dst_ref, send_sems, recv_sems):
            sends, arrivals = [], []
            for k in range(1, 8):
                peer = peer_of(k)
                sends.append(pltpu.make_async_remote_copy(
                    src_ref=src_of(slot_of(peer)), dst_ref=dst_ref.at[my_slot],
                    send_sem=send_sems.at[k - 1], recv_sem=recv_sems.at[k - 1],
                    device_id=peer, device_id_type=MESH))
                arrivals.append(pltpu.make_async_remote_copy(
                    src_ref=src_of(my_slot), dst_ref=dst_ref.at[slot_of(peer)],
                    send_sem=send_sems.at[k - 1], recv_sem=recv_sems.at[k - 1],
                    device_id=peer, device_id_type=MESH))
            for cp in sends:
                cp.start()
            for cp in arrivals:
                cp.wait_recv()
            for cp in sends:
                cp.wait_send()

        call_ref[my_slot] = c_ref[...]
        exchange(lambda s: c_ref, call_ref, send1, recv1)
        cv = call_ref[...].reshape(N_DEV * 8, c_ref.shape[1])
        act = cv * _sigmoid(cv)
        mod = jnp.dot(act, w_ref[...], preferred_element_type=F32,
                      precision=lax.Precision.HIGHEST) + b_ref[...]
        msend[...] = mod.reshape(N_DEV, 8, cols)
        mod_ref[my_slot] = msend[my_slot]
        exchange(lambda s: msend.at[s], mod_ref, send2, recv2)

    return pl.pallas_call(
        body, name="mod_exchange",
        out_shape=(jax.ShapeDtypeStruct((N_DEV, 8, c8.shape[1]), F32),
                   jax.ShapeDtypeStruct((N_DEV, 8, cols), F32)),
        scratch_shapes=[pltpu.VMEM((N_DEV, 8, cols), F32)] + [pltpu.SemaphoreType.DMA((7,))] * 4,
    )(c8, w_ada, b_cols)


def _w_ada_update(c_all_t, dmod_cols, w, m, v):
    rows, cols = w.shape
    tr = 256

    def body(c_ref, d_ref, w_ref, m_ref, v_ref, g_ref, dl_ref, nm_ref, nv_ref):
        cv = c_ref[...]
        g = jnp.dot(cv * _sigmoid(cv), d_ref[...], preferred_element_type=F32,
                    precision=lax.Precision.HIGHEST)
        g_ref[...] = g
        dl_ref[...], nm_ref[...], nv_ref[...] = _adam_step(g, w_ref[...], m_ref[...], v_ref[...])

    blk = pl.BlockSpec((tr, cols), lambda i: (i, 0))
    shp = jax.ShapeDtypeStruct((rows, cols), F32)
    return pl.pallas_call(
        body, name="adam_w_ada",
        grid=(rows // tr,),
        in_specs=[pl.BlockSpec((tr, c_all_t.shape[1]), lambda i: (i, 0)),
                  pl.BlockSpec(dmod_cols.shape, lambda i: (0, 0)), blk, blk, blk],
        out_specs=(blk, blk, blk, blk),
        out_shape=(shp, shp, shp, shp),
    )(c_all_t, dmod_cols, w, m, v)


def _shard_order():
    x, y, c = lax.axis_index("x"), lax.axis_index("y"), lax.axis_index("c")
    first, second, diag = _neighbour_chips()
    devs = [(x, y, c), (x, y, 1 - c), (*first, c), (*second, 1 - c), (*second, c), (*first, 1 - c),
            (*diag, c), (*diag, 1 - c)]
    return jnp.stack([4 * d[0] + 2 * d[1] + d[2] for d in devs]).astype(jnp.int32)


def _prep_h(x2, mod3):
    ts = 512
    per_seq = S // ts

    def body(x_ref, mod_ref, h_ref):
        shift = mod_ref[0, 0:1, :]
        scale = mod_ref[0, 1:2, :]
        h_ref[...] = (x_ref[...] * (1.0 + scale) + shift).astype(BF16)

    return pl.pallas_call(
        body, name="prep_h",
        grid=(T // ts,),
        in_specs=[pl.BlockSpec((ts, D), lambda i: (i, 0)),
                  pl.BlockSpec((1, 3, D), lambda i: (i // per_seq, 0, 0))],
        out_specs=pl.BlockSpec((ts, D), lambda i: (i, 0)),
        out_shape=jax.ShapeDtypeStruct((T, D), BF16),
    )(x2, mod3)


def _gather_proj(order, h, w_shard, tm, ride=()):
    rows, kdim = h.shape
    ncols = w_shard.shape[1]
    n_i = rows // tm
    ride_arrs, ride_shapes = ride if ride else ((), ())
    n_ride = len(ride_arrs)

    def body(order_ref, h_ref, mine_hbm, *rest):
        ride_ins = rest[:n_ride]
        o_ref, all_hbm = rest[n_ride:n_ride + 2]
        ride_outs = rest[n_ride + 2:2 * n_ride + 2]
        wv, send_sems, recv_sems, local_sems = rest[2 * n_ride + 2:2 * n_ride + 6]
        ride_scr = rest[2 * n_ride + 6:]
        j, i = pl.program_id(0), pl.program_id(1)
        c = lax.axis_index("c")
        me, sibling = (lax.axis_index("x"), lax.axis_index("y"), c), (lax.axis_index("x"), lax.axis_index("y"), 1 - c)
        nb1, nb2, diag = _neighbour_chips()

        def slot(dev):
            return 4 * dev[0] + 2 * dev[1] + dev[2]

        def copy(k, block, to, src=None, part=None):
            buf = wv.at[slot(block)]
            if part is not None:
                buf = buf.at[pl.ds(pl.multiple_of(part * (kdim // 2), kdim // 2), kdim // 2)]
            return pltpu.make_async_remote_copy(
                src_ref=buf if src is None else src, dst_ref=buf,
                send_sem=send_sems.at[k], recv_sem=recv_sems.at[k],
                device_id=to, device_id_type=MESH)

        def keep(step, block):
            s = slot(block)
            cols = pl.ds(pl.multiple_of((s % 2) * ncols, 128), ncols)
            return pltpu.make_async_copy(wv.at[s], all_hbm.at[s // 2, :, cols], local_sems.at[step])

        if n_ride:
            gather = _Gather(ride_ins, ride_outs, ride_scr[3:], *ride_scr[:3])
        to_sibling, to_nb1, to_nb2 = (copy(0, me, sibling, mine_hbm), copy(1, me, (*nb1, c), mine_hbm),
                                      copy(2, me, (*nb2, c), mine_hbm))
        relay1, relay2 = copy(3, (*nb2, c), (*nb1, c), part=c), copy(4, (*nb1, c), (*nb2, c), part=1 - c)
        pass_nb1, pass_nb2 = copy(5, (*nb1, c), sibling), copy(6, (*nb2, c), sibling)
        pass_d1, pass_d2 = copy(7, (*diag, c), sibling, part=c), copy(8, (*diag, c), sibling, part=1 - c)
        sends = [to_sibling, to_nb1, to_nb2, relay1, relay2, pass_nb1, pass_nb2, pass_d1, pass_d2]
        due = [
            (me, [], []),
            (sibling, [copy(0, sibling, me)], [[]]),
            ((*nb1, c), [copy(1, (*nb1, c), me)], [[pass_nb1, to_nb2]]),
            ((*nb2, 1 - c), [copy(5, (*nb2, 1 - c), me)], [[]]),
            ((*nb2, c), [copy(2, (*nb2, c), me)], [[pass_nb2, relay1, relay2]]),
            ((*nb1, 1 - c), [copy(6, (*nb1, 1 - c), me)], [[]]),
            ((*diag, c), [copy(3, (*diag, c), me, part=c), copy(4, (*diag, c), me, part=1 - c)],
             [[pass_d1], [pass_d2]]),
            ((*diag, 1 - c), [copy(7, (*diag, 1 - c), me, part=1 - c), copy(8, (*diag, 1 - c), me, part=c)],
             [[], []]),
        ]

        @pl.when((j == 0) & (i == 0))
        def _():
            to_sibling.start()
            to_nb1.start()
            load = pltpu.make_async_copy(mine_hbm, wv.at[slot(me)], local_sems.at[N_DEV])
            load.start()
            load.wait()
            keep(0, me).start()

        for step in range(1, N_DEV):
            block, arrivals, then = due[step]

            @pl.when((j == step) & (i == 0))
            def _():
                for arrival, follow in zip(arrivals, then):
                    arrival.wait_recv()
                    for cp in follow:
                        cp.start()
                keep(step, block).start()
                if n_ride and step == N_DEV - 2:
                    gather.begin()

        o_ref[...] = _dot(h_ref[...], wv[order_ref[j]]).astype(BF16)

        @pl.when((j == N_DEV - 1) & (i == n_i - 1))
        def _():
            for cp in sends:
                cp.wait_send()
            for step in range(N_DEV):
                keep(step, due[step][0]).wait()
            if n_ride:
                gather.finish()

    any_spec = pl.BlockSpec(memory_space=pl.ANY)
    res = pl.pallas_call(
        body, name="gather_proj",
        grid_spec=pltpu.PrefetchScalarGridSpec(
            num_scalar_prefetch=1,
            grid=(N_DEV, n_i),
            in_specs=[pl.BlockSpec((tm, kdim), lambda j, i, order_ref: (i, 0)), any_spec] + [any_spec] * n_ride,
            out_specs=(pl.BlockSpec((tm, ncols), lambda j, i, order_ref: (i, order_ref[j])), any_spec)
                      + (any_spec,) * n_ride,
            scratch_shapes=[pltpu.VMEM((N_DEV, kdim, ncols), BF16),
                            pltpu.SemaphoreType.DMA((9,)), pltpu.SemaphoreType.DMA((9,)),
                            pltpu.SemaphoreType.DMA((N_DEV + 1,))]
                           + (_Gather.scratch(ride_arrs) if n_ride else [])),
        out_shape=(jax.ShapeDtypeStruct((rows, N_DEV * ncols), BF16),
                   jax.ShapeDtypeStruct((N_DEV // 2, kdim, 2 * ncols), BF16)) + tuple(ride_shapes),
        compiler_params=pltpu.CompilerParams(vmem_limit_bytes=VMEM_LIMIT),
    )(order, h, w_shard, *ride_arrs)
    return res[0], res[1], res[2:]


SKEW_W = 512


def _bucket_maps():
    lanes = np.arange(SKEW_W)

    def buckets_of(steps):
        rows = []
        for dil in DILATIONS:
            dist = np.maximum(steps, 0) * dil
            nf = np.maximum(dist, 1).astype(np.float32)
            large = 16 + (np.log(nf / np.float32(16)) / np.float32(math.log(128.0))
                          * np.float32(16)).astype(np.int32)
            large = np.minimum(large, N_BUCKETS - 1)
            bucket = np.where(dist < 16, dist, large)
            rows.append(np.where((steps >= 0) & (steps <= N_STEPS), bucket, -1).astype(np.int32))
        return np.stack(rows)[:, None, :]

    a = np.arange(QB)[:, None]
    b = np.arange(2 * QB)[None, :]
    steps = a + QB - b
    band = (steps >= 0) & (steps <= N_STEPS)
    first = band & (b >= QB)
    masks = np.stack([first, band]).astype(np.int32)
    return buckets_of(QB - lanes), buckets_of(2 * QB - 1 - lanes), masks


def _bias_expand(rel_bias, lane_buckets, masks):
    def body(tab_ref, bk_ref, mk_ref, o_ref):
        for g in range(3):
            bk = bk_ref[g]
            for h in range(4):
                col = 4 * g + h
                per_offset = jnp.zeros((1, SKEW_W), F32)
                for k in range(N_BUCKETS):
                    per_offset = jnp.where(bk == k, tab_ref[k, col], per_offset)
                tile = pltpu.roll(jnp.broadcast_to(per_offset, (QB, SKEW_W)), 0, 1, stride=1, stride_axis=0)
                tile = tile[:, :2 * QB]
                o_ref[g, 0, h] = jnp.where(mk_ref[0] != 0, tile, NEG_INF)
                o_ref[g, 1, h] = jnp.where(mk_ref[1] != 0, tile, NEG_INF)

    return pl.pallas_call(
        body, name="bias_expand",
        in_specs=[pl.BlockSpec(memory_space=pltpu.SMEM),
                  pl.BlockSpec(memory_space=pltpu.VMEM),
                  pl.BlockSpec(memory_space=pltpu.VMEM)],
        out_shape=jax.ShapeDtypeStruct((3, 2, 4, QB, 2 * QB), F32),
    )(rel_bias, lane_buckets, masks)


def _bias_grad(ds1, ds2, ds3, lane_buckets):
    exchange = jnp.asarray(np.eye(QB, dtype=np.float32)[::-1].copy())

    def body(d1_ref, d2_ref, d3_ref, bk_ref, ex_ref, o_ref):
        for g, d_ref in enumerate((d1_ref, d2_ref, d3_ref)):
            bk = bk_ref[g]
            for h in range(4):
                flipped = jnp.dot(ex_ref[...], d_ref[h], preferred_element_type=F32,
                                  precision=lax.Precision.HIGHEST)
                padded = jnp.concatenate([flipped, jnp.zeros((QB, SKEW_W - 2 * QB), F32)], axis=1)
                skewed = pltpu.roll(padded, 0, 1, stride=1, stride_axis=0)
                per_offset = jnp.sum(skewed, axis=0, keepdims=True)
                for k in range(N_BUCKETS):
                    o_ref[k, 4 * g + h] = jnp.sum(jnp.where(bk == k, per_offset, 0.0))

    return pl.pallas_call(
        body, name="bias_grad",
        in_specs=[pl.BlockSpec(memory_space=pltpu.VMEM)] * 5,
        out_specs=pl.BlockSpec(memory_space=pltpu.SMEM),
        out_shape=jax.ShapeDtypeStruct((N_BUCKETS, N_HEADS), F32),
    )(ds1, ds2, ds3, lane_buckets, exchange)


def _scratch_sets(rows):
    return 4 if rows <= 512 else 1


def _unit_chunks(dil, size=16):
    units = [(h, r) for h in range(4) for r in range(dil)]
    return [units[i:i + size] for i in range(0, len(units), size)]


def _residue_rows(src_ref, copies, h, residue):
    buf = copies[h % len(copies)]
    buf[...] = src_ref[:, h * HD:(h + 1) * HD].astype(F32)
    return lambda r: buf[residue(r), :].astype(BF16)


def _attn_fwd(proj, bias, g):
    dil = DILATIONS[g]
    rows = QB * dil
    nsb = S // rows
    has_prev = nsb > 1

    def residue(r):
        return pl.ds(r, QB, stride=dil)

    n_sets = _scratch_sets(rows)
    n_in = 6 if has_prev else 4
    n_copied = (4 + (2 if has_prev else 0)) * n_sets

    def body(*refs):
        q_ref, kc_ref, vc_ref = refs[:3]
        kp_ref, vp_ref = refs[3:5] if has_prev else (None, None)
        b_ref = refs[n_in - 1]
        o_ref, l_ref = refs[n_in:n_in + 2]
        scr = list(refs[n_in + 2:])
        ls = [scr.pop(0) for _ in range(4)]
        copies = {name: [scr.pop(0) for _ in range(n_sets)]
                  for name in ("q", "kc", "vc", "o") + (("kp", "vp") if has_prev else ())}
        lane = lax.broadcasted_iota(jnp.int32, (QB, 128), 1)
        refs_of = {"q": q_ref, "kc": kc_ref, "vc": vc_ref, "kp": kp_ref, "vp": vp_ref}
        for chunk in _unit_chunks(dil):
            rows_of = {h: {name: _residue_rows(refs_of[name], copies[name], h, residue)
                           for name in refs_of if refs_of[name] is not None}
                       for h in sorted({h for h, _ in chunk})}

            def batch(name):
                return jnp.stack([rows_of[h][name](r) for h, r in chunk])

            q, k, v = batch("q"), batch("kc"), batch("vc")
            if has_prev:
                k = jnp.concatenate([batch("kp"), k], axis=1)
                v = jnp.concatenate([batch("vp"), v], axis=1)
                bias_b = jnp.stack([b_ref[h] for h, _ in chunk])
            else:
                bias_b = jnp.stack([b_ref[h, :, QB:] for h, _ in chunk])
            s = jnp.einsum("uqd,ukd->uqk", q, k, preferred_element_type=F32) * SCALE + bias_b
            m = jnp.max(s, axis=-1, keepdims=True)
            p = jnp.exp(s - m)
            l = jnp.sum(p, axis=-1, keepdims=True)
            o = jnp.einsum("uqk,ukd->uqd", p.astype(BF16), v, preferred_element_type=F32) / l
            lse = m + jnp.log(l)
            for i, (h, r) in enumerate(chunk):
                copies["o"][h % n_sets][residue(r), :] = o[i]
                ls[h][r * QB:(r + 1) * QB, :] = jnp.where(lane == h, lse[i], 0.0)
            for h in sorted({h for h, _ in chunk}):
                o_ref[:, h * HD:(h + 1) * HD] = copies["o"][h % n_sets][...]
        for r in range(dil):
            blk = slice(r * QB, (r + 1) * QB)
            l_ref[residue(r), :] = (ls[0][blk, :] + ls[1][blk, :]) + (ls[2][blk, :] + ls[3][blk, :])

    def row(b, n):
        return b * nsb + n

    def prev(b, n):
        return b * nsb + jnp.maximum(n - 1, 0)

    in_specs = [
        pl.BlockSpec((rows, GW), lambda b, n: (row(b, n), CB_Q + g)),
        pl.BlockSpec((rows, GW), lambda b, n: (row(b, n), CB_K + g)),
        pl.BlockSpec((rows, GW), lambda b, n: (row(b, n), CB_V + g)),
    ]
    args = [proj, proj, proj]
    scratch = [pltpu.VMEM((rows, 128), F32)] * (4 + n_copied)
    if has_prev:
        in_specs += [pl.BlockSpec((rows, GW), lambda b, n: (prev(b, n), CB_K + g)),
                     pl.BlockSpec((rows, GW), lambda b, n: (prev(b, n), CB_V + g))]
        args += [proj, proj]
    in_specs.append(pl.BlockSpec((None, None, 4, QB, 2 * QB),
                                 lambda b, n: (g, jnp.minimum(n, 1), 0, 0, 0)))
    args.append(bias)
    return pl.pallas_call(
        body, name=f"attn_fwd{g}",
        grid=(BL, nsb),
        in_specs=in_specs,
        out_specs=(pl.BlockSpec((rows, GW), lambda b, n: (row(b, n), 0)),
                   pl.BlockSpec((rows, 128), lambda b, n: (row(b, n), 0))),
        out_shape=(jax.ShapeDtypeStruct((T, GW), F32), jax.ShapeDtypeStruct((T, 128), F32)),
        scratch_shapes=scratch,
        compiler_params=pltpu.CompilerParams(vmem_limit_bytes=VMEM_LIMIT),
    )(*args)


def _attn_bwd(proj, d_out, stats, bias, dproj, g):
    dil = DILATIONS[g]
    rows = QB * dil
    nsb = S // rows
    has_prev = nsb > 1
    n_steps = nsb + 1 if has_prev else 1
    n_in = 7 + (2 if has_prev else 0)

    def residue(r):
        return pl.ds(r, QB, stride=dil)

    n_sets = _scratch_sets(rows)

    def body(*refs):
        q_ref, kc_ref, vc_ref, do_ref, st_ref, b_ref = refs[:6]
        kp_ref, vp_ref = refs[6:8] if has_prev else (None, None)
        out_ref, db_ref = refs[n_in], refs[n_in + 1]
        scr = list(refs[n_in + 2:])
        sq, sk, sv, sems = [scr.pop(0) for _ in range(4)]
        carry = scr.pop(0) if has_prev else None
        sts = scr.pop(0)
        copies = {name: [scr.pop(0) for _ in range(n_sets)]
                  for name in ("q", "kc", "vc", "do", "dq", "dk", "dv") + (("kp", "vp") if has_prev else ())}
        b, n = pl.program_id(0), pl.program_id(1)

        @pl.when((b == 0) & (n == 0))
        def _():
            db_ref[...] = jnp.zeros_like(db_ref)

        def finish(h, r, dq, dk, dv):
            for name, val in (("dq", dq), ("dk", dk), ("dv", dv)):
                copies[name][h % n_sets][residue(r), :] = val

        step = b * n_steps + n
        slot = step % 2

        def stage_copies(s, row0):
            return _column_copies([(sq.at[s], CB * (CB_Q + g)), (sk.at[s], CB * (CB_K + g)),
                                   (sv.at[s], CB * (CB_V + g))], out_ref, row0, sems.at[s])

        @pl.when((step >= 2) & ((step - 2) % n_steps >= (1 if has_prev else 0)))
        def _():
            for cp in stage_copies(slot, 0):
                cp.wait()

        def finish_head(h):
            sl = slice(h * HD, (h + 1) * HD)
            sq[slot, :, sl] = copies["dq"][h % n_sets][...].astype(BF16)
            sk[slot, :, sl] = copies["dk"][h % n_sets][...].astype(BF16)
            sv[slot, :, sl] = copies["dv"][h % n_sets][...].astype(BF16)

        def write_block(blk_idx):
            for cp in stage_copies(slot, pl.multiple_of(blk_idx * rows, rows)):
                cp.start()

            @pl.when(step == BL * n_steps - 1)
            def _():
                for s in range(2):
                    for cp in stage_copies(s, 0):
                        cp.wait()

        def carried(h, r):
            blk = slice(r * QB, (r + 1) * QB)
            return ((blk, slice(h * HD, (h + 1) * HD)), (blk, slice(GW + h * HD, GW + (h + 1) * HD)),
                    (blk, slice(2 * GW + h * HD, 2 * GW + (h + 1) * HD)))

        if has_prev:
            @pl.when(n == 0)
            def _():
                carry[...] = jnp.zeros_like(carry)

            @pl.when(n == nsb)
            def _():
                for h in range(4):
                    for r in range(dil):
                        cq, ck, cv = carried(h, r)
                        finish(h, r, carry[cq], carry[ck], carry[cv])
                    finish_head(h)
                write_block(b * nsb + nsb - 1)

        @pl.when(n < nsb)
        def _():
            for r in range(dil):
                sts[r * QB:(r + 1) * QB, :] = st_ref[residue(r), :]
            refs_of = {"q": q_ref, "kc": kc_ref, "vc": vc_ref, "do": do_ref, "kp": kp_ref, "vp": vp_ref}
            for chunk in _unit_chunks(dil):
                heads = sorted({h for h, _ in chunk})
                rows_of = {h: {name: _residue_rows(refs_of[name], copies[name], h, residue)
                               for name in refs_of if refs_of[name] is not None}
                           for h in heads}

                def batch(name):
                    return jnp.stack([rows_of[h][name](r) for h, r in chunk])

                q, k, v, do = batch("q"), batch("kc"), batch("vc"), batch("do")
                if has_prev:
                    k = jnp.concatenate([batch("kp"), k], axis=1)
                    v = jnp.concatenate([batch("vp"), v], axis=1)
                    bias_b = jnp.stack([b_ref[h] for h, _ in chunk])
                else:
                    bias_b = jnp.stack([b_ref[h, :, QB:] for h, _ in chunk])
                lse = jnp.stack([sts[r * QB:(r + 1) * QB, h:h + 1] for h, r in chunk])
                delta = jnp.stack([sts[r * QB:(r + 1) * QB, 4 + h:5 + h] for h, r in chunk])
                s = jnp.einsum("uqd,ukd->uqk", q, k, preferred_element_type=F32) * SCALE + bias_b
                p = jnp.exp(s - lse)
                ds = p * (jnp.einsum("uqd,ukd->uqk", do, v, preferred_element_type=F32) - delta)
                for h in heads:
                    mine = [ds[i] for i, (hh, _) in enumerate(chunk) if hh == h]
                    tot = mine[0]
                    for extra in mine[1:]:
                        tot = tot + extra
                    if has_prev:
                        db_ref[h] += tot
                    else:
                        db_ref[h, :, QB:] += tot
                dsb, pb = ds.astype(BF16), p.astype(BF16)
                dq = jnp.einsum("uqk,ukd->uqd", dsb, k, preferred_element_type=F32) * SCALE
                dk = jnp.einsum("uqk,uqd->ukd", dsb, q, preferred_element_type=F32) * SCALE
                dv = jnp.einsum("uqk,uqd->ukd", pb, do, preferred_element_type=F32)
                for i, (h, r) in enumerate(chunk):
                    if has_prev:
                        cq, ck, cv = carried(h, r)
                        finish(h, r, carry[cq], carry[ck] + dk[i, :QB], carry[cv] + dv[i, :QB])
                        carry[cq] = dq[i]
                        carry[ck] = dk[i, QB:]
                        carry[cv] = dv[i, QB:]
                    else:
                        finish(h, r, dq[i], dk[i], dv[i])
                for h in heads:
                    finish_head(h)
            if has_prev:
                @pl.when(n > 0)
                def _():
                    write_block(b * nsb + n - 1)
            else:
                write_block(b)

    def row(b, n):
        return b * nsb + jnp.minimum(n, nsb - 1)

    def prev(b, n):
        return b * nsb + jnp.maximum(jnp.minimum(n, nsb - 1) - 1, 0)

    in_specs = [
        pl.BlockSpec((rows, GW), lambda b, n: (row(b, n), CB_Q + g)),
        pl.BlockSpec((rows, GW), lambda b, n: (row(b, n), CB_K + g)),
        pl.BlockSpec((rows, GW), lambda b, n: (row(b, n), CB_V + g)),
        pl.BlockSpec((rows, GW), lambda b, n: (row(b, n), 0)),
        pl.BlockSpec((rows, 128), lambda b, n: (row(b, n), 0)),
        pl.BlockSpec((None, None, 4, QB, 2 * QB),
                     lambda b, n: (g, jnp.minimum(jnp.minimum(n, nsb - 1), 1), 0, 0, 0)),
    ]
    args = [proj, proj, proj, d_out, stats, bias]
    scratch = [pltpu.VMEM((2, rows, GW), BF16)] * 3 + [pltpu.SemaphoreType.DMA((2, 3))]
    if has_prev:
        in_specs += [pl.BlockSpec((rows, GW), lambda b, n: (prev(b, n), CB_K + g)),
                     pl.BlockSpec((rows, GW), lambda b, n: (prev(b, n), CB_V + g))]
        args += [proj, proj]
        scratch.append(pltpu.VMEM((rows, 3 * GW), F32))
    n_copied = (7 + (2 if has_prev else 0)) * n_sets
    scratch += [pltpu.VMEM((rows, 128), F32)] * (1 + n_copied)
    in_specs.append(pl.BlockSpec(memory_space=pl.ANY))
    args.append(dproj)
    return pl.pallas_call(
        body, name=f"attn_bwd{g}",
        grid=(BL, n_steps),
        in_specs=in_specs,
        out_specs=(pl.BlockSpec(memory_space=pl.ANY),
                   pl.BlockSpec((4, QB, 2 * QB), lambda b, n: (0, 0, 0))),
        out_shape=(jax.ShapeDtypeStruct((T, NCOL), BF16),
                   jax.ShapeDtypeStruct((4, QB, 2 * QB), F32)),
        scratch_shapes=scratch,
        input_output_aliases={len(args) - 1: 0},
        compiler_params=pltpu.CompilerParams(vmem_limit_bytes=VMEM_LIMIT),
    )(*args)


def _tail(x2, tgt2, mod3, o_g, lse_g, proj, w_ao, w_co, w_o, conv_w, conv_b, ln_g, ln_b):
    tm = 256
    per_seq = S // tm
    halo = 16

    def body(x_ref, t_ref, mod_ref, o1_ref, o2_ref, o3_ref, l1_ref, l2_ref, l3_ref,
             ga_ref, u_ref, bg_ref, cg_ref, gc_ref, ma_ref, mc_ref, up_ref, cp_ref,
             wao_ref, wco_ref, wo_ref, cw_ref, cb_ref, lg_ref, lb_ref,
             dproj_ref, dyc_ref, do_ref, st_ref, dxd_ref,
             gwo_ref, gwco_ref, gwao_ref, vec_ref,
             dga_s, dbg_s, dgm_s, sems, acc_o, acc_co, acc_ao):
        i = pl.program_id(0)
        bidx = i // per_seq
        first = (i % per_seq) == 0

        @pl.when(i == 0)
        def _():
            vec_ref[...] = jnp.zeros_like(vec_ref)

        slot = i % 2

        def column_copies(s, row0):
            return _column_copies([(dga_s.at[s], CB * CB_GA), (dbg_s.at[s], D * KB_BG), (dgm_s.at[s], D * KB_GC)],
                                  dproj_ref, row0, sems.at[s])

        @pl.when(i >= 2)
        def _():
            for cp in column_copies(slot, 0):
                cp.wait()

        l1, l2, l3 = l1_ref[...], l2_ref[...], l3_ref[...]
        mx = jnp.maximum(jnp.maximum(l1, l2), l3)
        e1, e2, e3 = jnp.exp(l1 - mx), jnp.exp(l2 - mx), jnp.exp(l3 - mx)
        esum = e1 + e2 + e3
        lse_tot = mx + jnp.log(esum)
        w1, w2, w3 = e1 / esum, e2 / esum, e3 / esum

        def per_head(wv):
            return jnp.concatenate([jnp.broadcast_to(wv[:, h:h + 1], (tm, HD)) for h in range(4)], axis=1)

        o = per_head(w1) * o1_ref[...] + per_head(w2) * o2_ref[...] + per_head(w3) * o3_ref[...]

        ga = ga_ref[...].astype(F32)
        sig_ga = _sigmoid(ga)
        silu_ga = ga * sig_ga
        a_in = (o * silu_ga).astype(BF16)
        a_out = _dot(a_in, wao_ref[...])

        u = u_ref[...].astype(F32)
        cg = cg_ref[...].astype(F32)
        z = cg * u
        zp = cp_ref[...].astype(F32) * up_ref[...].astype(F32)
        zp = jnp.where(first, 0.0, zp)
        zcat = jnp.concatenate([zp, z], axis=0)
        z1 = pltpu.roll(zcat, 1, 0)[halo:]
        z2 = pltpu.roll(zcat, 2, 0)[halo:]
        y_conv = cw_ref[0:1, :] * z2 + cw_ref[1:2, :] * z1 + cw_ref[2:3, :] * z + cb_ref[...]
        gc = gc_ref[...].astype(F32)
        sig_gc = _sigmoid(gc)
        silu_gc = gc * sig_gc
        bg = bg_ref[...].astype(F32)
        bg_yc = bg * y_conv
        s_in = (bg_yc * silu_gc).astype(BF16)
        s_out = _dot(s_in, wco_ref[...])

        sa = _sigmoid(ma_ref[...].astype(F32))
        sc = _sigmoid(mc_ref[...].astype(F32))
        merged = (sa * a_out + sc * s_out).astype(BF16)
        y = _dot(merged, wo_ref[...])
        gate1 = 1.0 + mod_ref[0, 2:3, :]
        xv = x_ref[...]
        resid = ALPHA * xv + gate1 * y
        mu = jnp.mean(resid, axis=1, keepdims=True)
        xc = resid - mu
        var = jnp.mean(xc * xc, axis=1, keepdims=True)
        rstd = lax.rsqrt(var + LN_EPS)
        xhat = xc * rstd
        lg = lg_ref[...]
        err = xhat * lg + lb_ref[...] - t_ref[...]
        vec_ref[3:4, :] += (0.5 / D) * jnp.sum(err * err, axis=0, keepdims=True)

        vec_ref[1:2, :] += (1.0 / D) * jnp.sum(err * xhat, axis=0, keepdims=True)
        vec_ref[2:3, :] += (1.0 / D) * jnp.sum(err, axis=0, keepdims=True)
        dxh = err * (lg * (1.0 / D))
        dres = rstd * (dxh - jnp.mean(dxh, axis=1, keepdims=True)
                       - xhat * jnp.mean(dxh * xhat, axis=1, keepdims=True))
        dxd_ref[...] = ALPHA * dres
        dgate = jnp.sum(dres * y, axis=0, keepdims=True)
        vec_ref[4:5, :] += jnp.where(bidx == 0, dgate, 0.0)
        vec_ref[5:6, :] += jnp.where(bidx == 1, dgate, 0.0)
        dy = (dres * gate1).astype(BF16)

        dmerged = _dot_nt(dy, wo_ref[...])
        da_out_f = dmerged * sa
        ds_out_f = dmerged * sc
        da_out = da_out_f.astype(BF16)
        ds_out = ds_out_f.astype(BF16)
        dgm_s[slot, :, 2 * D:3 * D] = (ds_out_f * s_out * (1.0 - sc)).astype(BF16)
        dgm_s[slot, :, D:2 * D] = (da_out_f * a_out * (1.0 - sa)).astype(BF16)
        da_in = _dot_nt(da_out, wao_ref[...])
        ds_in = _dot_nt(ds_out, wco_ref[...])

        d_o = da_in * silu_ga
        do_ref[...] = d_o.astype(BF16)
        dga_s[slot] = (da_in * o * (sig_ga + silu_ga * (1.0 - sig_ga))).astype(BF16)
        lane = lax.broadcasted_iota(jnp.int32, (tm, 128), 1)
        stats = lse_tot
        od = o * d_o
        for h in range(4):
            delta = jnp.sum(od[:, h * HD:(h + 1) * HD], axis=1, keepdims=True)
            stats = jnp.where(lane == 4 + h, delta, stats)
        st_ref[...] = stats

        ds_silu = ds_in * silu_gc
        dbg_s[slot] = (ds_silu * y_conv).astype(BF16)
        dyc = ds_silu * bg
        dyc_ref[...] = dyc
        vec_ref[0:1, :] += jnp.sum(dyc, axis=0, keepdims=True)
        dgm_s[slot, :, 0:D] = (ds_in * bg_yc * (sig_gc + silu_gc * (1.0 - sig_gc))).astype(BF16)

        @pl.when(i == 0)
        def _():
            acc_o[...] = jnp.zeros_like(acc_o)
            acc_co[...] = jnp.zeros_like(acc_co)
            acc_ao[...] = jnp.zeros_like(acc_ao)

        acc_o[...] += _dot_tn(merged, dy)
        acc_co[...] += _dot_tn(s_in, ds_out)
        acc_ao[...] += _dot_tn(a_in, da_out)

        for cp in column_copies(slot, pl.multiple_of(i * tm, tm)):
            cp.start()

        @pl.when(i == T // tm - 1)
        def _():
            gwo_ref[...] = acc_o[...].astype(BF16)
            gwco_ref[...] = acc_co[...].astype(BF16)
            gwao_ref[...] = acc_ao[...].astype(BF16)
            for s in range(2):
                for cp in column_copies(s, 0):
                    cp.wait()

    def tile(width, cblk=0):
        return pl.BlockSpec((tm, width), lambda i: (i, cblk))

    def whole(shape):
        return pl.BlockSpec(shape, lambda i: tuple(0 for _ in shape))

    def once(shape):
        return pl.BlockSpec(shape, lambda i: tuple(0 for _ in shape), pipeline_mode=pl.Buffered(1))

    prev_rows = lambda i: (jnp.maximum(i * (tm // halo) - 1, 0),)
    in_specs = [
        tile(D), tile(D), pl.BlockSpec((1, 3, D), lambda i: (i // per_seq, 0, 0)),
        tile(GW), tile(GW), tile(GW), tile(128), tile(128), tile(128),
        tile(GW, CB_GA), tile(D, KB_U), tile(D, KB_BG), tile(D, KB_CG), tile(D, KB_GC),
        tile(D, KB_MA), tile(D, KB_MC),
        pl.BlockSpec((halo, D), lambda i: (*prev_rows(i), KB_U)),
        pl.BlockSpec((halo, D), lambda i: (*prev_rows(i), KB_CG)),
        whole((GW, D)), whole((D, D)), whole((D, D)),
        whole((3, D)), whole((1, D)), whole((1, D)), whole((1, D)),
    ]
    out_specs = (
        pl.BlockSpec(memory_space=pl.ANY), tile(D), tile(GW), tile(128), tile(D),
        once((D, D)), once((D, D)), once((GW, D)),
        pl.BlockSpec((8, D), lambda i: (0, 0)),
    )
    out_shape = (
        jax.ShapeDtypeStruct((T, NCOL), BF16),
        jax.ShapeDtypeStruct((T, D), F32),
        jax.ShapeDtypeStruct((T, GW), BF16),
        jax.ShapeDtypeStruct((T, 128), F32),
        jax.ShapeDtypeStruct((T, D), F32),
        jax.ShapeDtypeStruct((D, D), BF16),
        jax.ShapeDtypeStruct((D, D), BF16),
        jax.ShapeDtypeStruct((GW, D), BF16),
        jax.ShapeDtypeStruct((8, D), F32),
    )
    return pl.pallas_call(
        body, name="tail",
        grid=(T // tm,),
        in_specs=in_specs, out_specs=out_specs, out_shape=out_shape,
        scratch_shapes=[pltpu.VMEM((2, tm, GW), BF16), pltpu.VMEM((2, tm, D), BF16), pltpu.VMEM((2, tm, 3 * D), BF16),
                        pltpu.SemaphoreType.DMA((2, 3)),
                        pltpu.VMEM((D, D), F32), pltpu.VMEM((D, D), F32), pltpu.VMEM((GW, D), F32)],
        compiler_params=pltpu.CompilerParams(vmem_limit_bytes=VMEM_LIMIT_HIGH),
    )(x2, tgt2, mod3, *o_g, *lse_g, proj, proj, proj, proj, proj, proj, proj, proj, proj,
      w_ao, w_co, w_o, conv_w, conv_b, ln_g, ln_b)


def _conv_bwd(dyc, proj, conv_w, dproj):
    tm = 512
    per_seq = S // tm

    def body(d_ref, dn_ref, u_ref, c_ref, cw_ref, _, dproj_ref, g_ref, du_s, dc_s, sems):
        i = pl.program_id(0)
        last = (i % per_seq) == per_seq - 1

        @pl.when(i == 0)
        def _():
            g_ref[...] = jnp.zeros_like(g_ref)

        slot = i % 2

        def column_copies(s, row0):
            return _column_copies([(du_s.at[s], D * KB_U), (dc_s.at[s], D * KB_CG)], dproj_ref, row0, sems.at[s])

        @pl.when(i >= 2)
        def _():
            for cp in column_copies(slot, 0):
                cp.wait()

        d = d_ref[...]
        dn = jnp.where(last, 0.0, dn_ref[...])
        dcat = jnp.concatenate([d, dn], axis=0)
        d1 = pltpu.roll(dcat, tm + 8 - 1, 0)[:tm]
        d2 = pltpu.roll(dcat, tm + 8 - 2, 0)[:tm]
        dz = cw_ref[2:3, :] * d + cw_ref[1:2, :] * d1 + cw_ref[0:1, :] * d2
        u = u_ref[...].astype(F32)
        cg = c_ref[...].astype(F32)
        du_s[slot] = (dz * cg).astype(BF16)
        dc_s[slot] = (dz * u).astype(BF16)
        for cp in column_copies(slot, pl.multiple_of(i * tm, tm)):
            cp.start()

        z = cg * u
        g_ref[0:1, :] += jnp.sum(d2 * z, axis=0, keepdims=True)
        g_ref[1:2, :] += jnp.sum(d1 * z, axis=0, keepdims=True)
        g_ref[2:3, :] += jnp.sum(d * z, axis=0, keepdims=True)

        @pl.when(i == T // tm - 1)
        def _():
            for s in range(2):
                for cp in column_copies(s, 0):
                    cp.wait()

    n_tiles = T // tm
    next_rows = lambda i: jnp.minimum((i + 1) * (tm // 8), T // 8 - 1)
    return pl.pallas_call(
        body, name="conv_bwd",
        grid=(n_tiles,),
        in_specs=[pl.BlockSpec((tm, D), lambda i: (i, 0)),
                  pl.BlockSpec((8, D), lambda i: (next_rows(i), 0)),
                  pl.BlockSpec((tm, D), lambda i: (i, KB_U)),
                  pl.BlockSpec((tm, D), lambda i: (i, KB_CG)),
                  pl.BlockSpec((3, D), lambda i: (0, 0)),
                  pl.BlockSpec(memory_space=pl.ANY)],
        out_specs=(pl.BlockSpec(memory_space=pl.ANY),
                   pl.BlockSpec((8, D), lambda i: (0, 0))),
        out_shape=(jax.ShapeDtypeStruct((T, NCOL), BF16),
                   jax.ShapeDtypeStruct((8, D), F32)),
        scratch_shapes=[pltpu.VMEM((2, tm, D), BF16), pltpu.VMEM((2, tm, D), BF16), pltpu.SemaphoreType.DMA((2, 2))],
        input_output_aliases={5: 0},
        compiler_params=pltpu.CompilerParams(vmem_limit_bytes=VMEM_LIMIT),
    )(dyc, dyc, proj, proj, conv_w, dproj)


def _dh_dx(dproj, w_in_all, x2, dxd, mod3, chip_sums, hops=(), parts0=None):
    tm = 512
    per_seq = S // tm
    n_pass, _, width = w_in_all.shape
    n = len(chip_sums)
    n_in = 5 + n + (0 if parts0 is None else 1)

    def body(*refs):
        d_ref, w_ref, x_ref, dxd_ref, mod_ref = refs[:5]
        ins = refs[5:5 + n]
        gx_ref, vec_ref = refs[n_in:n_in + 2]
        outs = refs[n_in + 2:n_in + 2 + n]
        acc, send_sems, recv_sems, local_sems = refs[n_in + 2 + n:]
        jj, i = pl.program_id(0), pl.program_id(1)

        @pl.when((i == 0) & (jj == 0))
        def _():
            vec_ref[...] = jnp.zeros_like(vec_ref)
            if n:
                sends, _, mine = _chip_copies(ins, outs, send_sems, recv_sems, local_sems, hops)
                for cp in sends + mine:
                    cp.start()

        if n:
            @pl.when((i == T // tm - 1) & (jj == n_pass - 1))
            def _():
                sends, arrivals, mine = _chip_copies(ins, outs, send_sems, recv_sems, local_sems, hops)
                for cp in arrivals:
                    cp.wait_recv()
                for cp in sends:
                    cp.wait_send()
                for cp in mine:
                    cp.wait()

        def partial():
            return _dot_nt(d_ref[...], w_ref[...])

        @pl.when(jj == 0)
        def _():
            acc[i] = partial()

        @pl.when((jj > 0) & (jj < n_pass - 1))
        def _():
            acc[i] += partial()

        @pl.when(jj == n_pass - 1)
        def _():
            dh = acc[i] + partial()
            bidx = i // per_seq
            gx_ref[...] = dxd_ref[...] + dh * (1.0 + mod_ref[0, 1:2, :])
            dshift = jnp.sum(dh, axis=0, keepdims=True)
            dscale = jnp.sum(dh * x_ref[...], axis=0, keepdims=True)
            vec_ref[0:1, :] += jnp.where(bidx == 0, dshift, 0.0)
            vec_ref[1:2, :] += jnp.where(bidx == 1, dshift, 0.0)
            vec_ref[2:3, :] += jnp.where(bidx == 0, dscale, 0.0)
            vec_ref[3:4, :] += jnp.where(bidx == 1, dscale, 0.0)

    def last_pass(jj, i):
        return jnp.where(jj == n_pass - 1, i, 0)

    any_spec = pl.BlockSpec(memory_space=pl.ANY)
    res = pl.pallas_call(
        body, name="dh_dx",
        grid=(n_pass, T // tm),
        in_specs=[
            pl.BlockSpec((tm, width), lambda jj, i: (i, jj)),
            pl.BlockSpec((None, D, width), lambda jj, i: (jj, 0, 0)),
            pl.BlockSpec((tm, D), lambda jj, i: (last_pass(jj, i), 0)),
            pl.BlockSpec((tm, D), lambda jj, i: (last_pass(jj, i), 0)),
            pl.BlockSpec((1, 3, D), lambda jj, i: (last_pass(jj, i) // per_seq, 0, 0))]
                 + [any_spec] * (n_in - 5),
        out_specs=(pl.BlockSpec((tm, D), lambda jj, i: (last_pass(jj, i), 0)),
                   pl.BlockSpec((8, D), lambda jj, i: (0, 0))) + (any_spec,) * n,
        out_shape=(jax.ShapeDtypeStruct((T, D), F32), jax.ShapeDtypeStruct((8, D), F32))
                  + tuple(jax.ShapeDtypeStruct(a.shape, a.dtype) for a in chip_sums),
        scratch_shapes=[pltpu.VMEM((T // tm, tm, D), F32), pltpu.SemaphoreType.DMA((max(3 * n, 1),)),
                        pltpu.SemaphoreType.DMA((max(3 * n, 1),)), pltpu.SemaphoreType.DMA((max(n, 1),))],
        input_output_aliases={} if parts0 is None else {5 + n: 2},
        compiler_params=pltpu.CompilerParams(vmem_limit_bytes=VMEM_LIMIT),
    )(dproj, w_in_all, x2, dxd, mod3, *chip_sums, *([] if parts0 is None else [parts0]))
    return res[0], res[1], res[2:]


def _adam_step(g, w, m, v):
    nm = ADAM_B1 * m + (1.0 - ADAM_B1) * g
    nv = ADAM_B2 * v + (1.0 - ADAM_B2) * (g * g)
    m_hat = nm / (1.0 - ADAM_B1 ** ADAM_STEP)
    v_hat = nv / (1.0 - ADAM_B2 ** ADAM_STEP)
    return -ADAM_LR * (m_hat / (jnp.sqrt(v_hat) + ADAM_EPS) + ADAM_WD * w), nm, nv


def _adamw(parts, w, m, v, name, row_tile=None):
    n_parts, rows, cols = parts.shape
    tr = rows if row_tile is None else row_tile

    def body(p_ref, w_ref, m_ref, v_ref, g_ref, d_ref, nm_ref, nv_ref):
        g = p_ref[0].astype(F32)
        for s in range(1, n_parts):
            g = g + p_ref[s].astype(F32)
        g_ref[...] = g
        d_ref[...], nm_ref[...], nv_ref[...] = _adam_step(g, w_ref[...], m_ref[...], v_ref[...])

    blk = pl.BlockSpec((tr, cols), lambda i: (i, 0))
    shp = jax.ShapeDtypeStruct((rows, cols), F32)
    return pl.pallas_call(
        body, name=name,
        grid=(rows // tr,),
        in_specs=[pl.BlockSpec((n_parts, tr, cols), lambda i: (0, i, 0)), blk, blk, blk],
        out_specs=(blk, blk, blk, blk),
        out_shape=(shp, shp, shp, shp),
        compiler_params=pltpu.CompilerParams(vmem_limit_bytes=VMEM_LIMIT),
    )(parts, w, m, v)


def _multi_adamw(parts_list, params, name):
    n = len(params)
    flat = [t for wmv in params for t in wmv]

    def body(*refs):
        parts, ins, outs = refs[:n], refs[n:4 * n], refs[4 * n:]
        for p in range(n):
            g = parts[p][0].astype(F32)
            for s in range(1, parts[p].shape[0]):
                g = g + parts[p][s].astype(F32)
            w_ref, m_ref, v_ref = ins[3 * p:3 * p + 3]
            g_ref, d_ref, nm_ref, nv_ref = outs[4 * p:4 * p + 4]
            g_ref[...] = g
            d_ref[...], nm_ref[...], nv_ref[...] = _adam_step(g, w_ref[...], m_ref[...], v_ref[...])

    out_shape = []
    for w, _, _ in params:
        out_shape += [jax.ShapeDtypeStruct(w.shape, F32)] * 4
    res = pl.pallas_call(body, name=name, out_shape=tuple(out_shape))(*parts_list, *flat)
    return [res[4 * p:4 * p + 4] for p in range(n)]


def _small_updates(small_g, dmod_all, rel_parts, params):
    flat = [t for wmv in params for t in wmv]

    def body(sg_ref, dm_ref, rp_ref, *refs):
        ins, outs = refs[:len(flat)], refs[len(flat):]

        def over_devices(row):
            tot = sg_ref[0, row:row + 1, :]
            for s in range(1, N_DEV):
                tot = tot + sg_ref[s, row:row + 1, :]
            return tot

        g_b_ada = dm_ref[0:1, :]
        for r in range(1, N_DEV * BL):
            g_b_ada = g_b_ada + dm_ref[r:r + 1, :]
        g_rel = rp_ref[0]
        for s in range(1, N_DEV):
            g_rel = g_rel + rp_ref[s]
        grads = [g_b_ada, over_devices(0), g_rel, over_devices(1), over_devices(2)]
        outs[0][...] = jnp.sum(over_devices(3), axis=1, keepdims=True)
        for p, g in enumerate(grads):
            w_ref, m_ref, v_ref = ins[3 * p:3 * p + 3]
            g_ref, d_ref, nm_ref, nv_ref = outs[1 + 4 * p:5 + 4 * p]
            g_ref[...] = g
            d_ref[...], nm_ref[...], nv_ref[...] = _adam_step(g, w_ref[...], m_ref[...], v_ref[...])

    out_shape = [jax.ShapeDtypeStruct((1, 1), F32)]
    for w, _, _ in params:
        out_shape += [jax.ShapeDtypeStruct(w.shape, F32)] * 4
    res = pl.pallas_call(body, name="small_updates", out_shape=tuple(out_shape))(small_g, dmod_all, rel_parts, *flat)
    return res[0], [res[1 + 4 * p:5 + 4 * p] for p in range(len(params))]


def _attn_fwd_dense(proj, bias):
    nq = 4
    rows = nq * QB
    nsb = S // rows

    def body(q_ref, k_ref, v_ref, kp_ref, vp_ref, b_ref, o_ref, l_ref, ls0, ls1, ls2, ls3):
        ls = [ls0, ls1, ls2, ls3]
        n = pl.program_id(1)
        lane = lax.broadcasted_iota(jnp.int32, (QB, 128), 1)
        units = [(h, j) for h in range(4) for j in range(nq)]

        def keys(cur_ref, prev_ref, h, j):
            sl = slice(h * HD, (h + 1) * HD)
            if j == 0:
                return jnp.concatenate([prev_ref[:, sl], cur_ref[0:QB, sl]], axis=0)
            return cur_ref[(j - 1) * QB:(j + 1) * QB, sl]

        q = jnp.stack([q_ref[j * QB:(j + 1) * QB, h * HD:(h + 1) * HD] for h, j in units])
        k = jnp.stack([keys(k_ref, kp_ref, h, j) for h, j in units])
        v = jnp.stack([keys(v_ref, vp_ref, h, j) for h, j in units])
        bias_b = jnp.stack([b_ref[jnp.minimum(n, 1), h] if j == 0 else b_ref[1, h] for h, j in units])
        s = jnp.einsum("uqd,ukd->uqk", q, k, preferred_element_type=F32) * SCALE + bias_b
        m = jnp.max(s, axis=-1, keepdims=True)
        p = jnp.exp(s - m)
        l = jnp.sum(p, axis=-1, keepdims=True)
        o = jnp.einsum("uqk,ukd->uqd", p.astype(BF16), v, preferred_element_type=F32) / l
        lse = m + jnp.log(l)
        for i, (h, j) in enumerate(units):
            o_ref[j * QB:(j + 1) * QB, h * HD:(h + 1) * HD] = o[i]
            ls[h][j * QB:(j + 1) * QB, :] = jnp.where(lane == h, lse[i], 0.0)
        l_ref[...] = (ls[0][...] + ls[1][...]) + (ls[2][...] + ls[3][...])

    def row(b, n):
        return b * nsb + n

    def prev(b, n):
        return jnp.maximum((b * nsb + n) * nq - 1, 0)

    in_specs = [
        pl.BlockSpec((rows, GW), lambda b, n: (row(b, n), CB_Q)),
        pl.BlockSpec((rows, GW), lambda b, n: (row(b, n), CB_K)),
        pl.BlockSpec((rows, GW), lambda b, n: (row(b, n), CB_V)),
        pl.BlockSpec((QB, GW), lambda b, n: (prev(b, n), CB_K)),
        pl.BlockSpec((QB, GW), lambda b, n: (prev(b, n), CB_V)),
        pl.BlockSpec((None, 2, 4, QB, 2 * QB), lambda b, n: (0, 0, 0, 0, 0)),
    ]
    return pl.pallas_call(
        body, name="attn_fwd0",
        grid=(BL, nsb),
        in_specs=in_specs,
        out_specs=(pl.BlockSpec((rows, GW), lambda b, n: (row(b, n), 0)),
                   pl.BlockSpec((rows, 128), lambda b, n: (row(b, n), 0))),
        out_shape=(jax.ShapeDtypeStruct((T, GW), F32), jax.ShapeDtypeStruct((T, 128), F32)),
        scratch_shapes=[pltpu.VMEM((rows, 128), F32)] * 4,
        compiler_params=pltpu.CompilerParams(vmem_limit_bytes=VMEM_LIMIT),
    )(proj, proj, proj, proj, proj, bias)


def _attn_bwd_dense(proj, d_out, stats, bias, dproj):
    nq = 4
    rows = nq * QB
    nsb = S // rows
    cols_q, cols_k, cols_v = CB * CB_Q, CB * CB_K, CB * CB_V

    def body(q_ref, k_ref, v_ref, do_ref, st_ref, kp_ref, vp_ref, b_ref, _, out_ref, db_ref,
             sq, sk, sv, carry, sems):
        b, n = pl.program_id(0), pl.program_id(1)
        units = [(h, j) for h in range(4) for j in range(nq)]

        @pl.when((b == 0) & (n == 0))
        def _():
            db_ref[...] = jnp.zeros_like(db_ref)

        @pl.when(n == 0)
        def _():
            carry[...] = jnp.zeros_like(carry)

        step = b * (nsb + 1) + n
        slot = step % 2

        def block_copies(s, position, block):
            part = pl.ds(position * QB, QB)
            return _column_copies([(sq.at[s, part], cols_q), (sk.at[s, part], cols_k), (sv.at[s, part], cols_v)],
                                  out_ref, pl.multiple_of(block * QB, QB), sems.at[s, position])

        def wait_blocks(s, positions):
            for position in positions:
                for cp in block_copies(s, position, 0):
                    cp.wait()

        before = (step - 2) % (nsb + 1)

        @pl.when((step >= 2) & (before > 0))
        def _():
            wait_blocks(slot, [0])

        @pl.when((step >= 2) & (before < nsb))
        def _():
            wait_blocks(slot, range(1, nq))

        def write(first_block, position, count):
            for j in range(count):
                for cp in block_copies(slot, position + j, first_block + j):
                    cp.start()

        @pl.when(n == nsb)
        def _():
            sq[slot, 0:QB, :] = carry[:, 0:GW].astype(BF16)
            sk[slot, 0:QB, :] = carry[:, GW:2 * GW].astype(BF16)
            sv[slot, 0:QB, :] = carry[:, 2 * GW:3 * GW].astype(BF16)
            write((b + 1) * nsb * nq - 1, 0, 1)

            @pl.when(b == BL - 1)
            def _():
                wait_blocks(slot, [0])
                wait_blocks(1 - slot, range(nq))

        @pl.when(n < nsb)
        def _():
            def keys(cur_ref, prev_ref, h, j):
                sl = slice(h * HD, (h + 1) * HD)
                if j == 0:
                    return jnp.concatenate([prev_ref[:, sl], cur_ref[0:QB, sl]], axis=0)
                return cur_ref[(j - 1) * QB:(j + 1) * QB, sl]

            def block(ref, h, j):
                return ref[j * QB:(j + 1) * QB, h * HD:(h + 1) * HD]

            q = jnp.stack([block(q_ref, h, j) for h, j in units])
            do = jnp.stack([block(do_ref, h, j) for h, j in units])
            k = jnp.stack([keys(k_ref, kp_ref, h, j) for h, j in units])
            v = jnp.stack([keys(v_ref, vp_ref, h, j) for h, j in units])
            bias_b = jnp.stack([b_ref[jnp.minimum(n, 1), h] if j == 0 else b_ref[1, h] for h, j in units])
            lse = jnp.stack([st_ref[j * QB:(j + 1) * QB, h:h + 1] for h, j in units])
            delta = jnp.stack([st_ref[j * QB:(j + 1) * QB, 4 + h:5 + h] for h, j in units])
            s = jnp.einsum("uqd,ukd->uqk", q, k, preferred_element_type=F32) * SCALE + bias_b
            p = jnp.exp(s - lse)
            ds = p * (jnp.einsum("uqd,ukd->uqk", do, v, preferred_element_type=F32) - delta)
            for h in range(4):
                tot = ds[h * nq]
                for j in range(1, nq):
                    tot = tot + ds[h * nq + j]
                db_ref[h] += tot
            dsb, pb = ds.astype(BF16), p.astype(BF16)
            dq = jnp.einsum("uqk,ukd->uqd", dsb, k, preferred_element_type=F32) * SCALE
            dk = jnp.einsum("uqk,uqd->ukd", dsb, q, preferred_element_type=F32) * SCALE
            dv = jnp.einsum("uqk,uqd->ukd", pb, do, preferred_element_type=F32)
            for h in range(4):
                sl = slice(h * HD, (h + 1) * HD)
                u0, last = h * nq, h * nq + nq - 1
                sq[slot, 0:QB, sl] = carry[:, sl].astype(BF16)
                sk[slot, 0:QB, sl] = (carry[:, GW + h * HD:GW + (h + 1) * HD] + dk[u0, :QB]).astype(BF16)
                sv[slot, 0:QB, sl] = (carry[:, 2 * GW + h * HD:2 * GW + (h + 1) * HD] + dv[u0, :QB]).astype(BF16)
                for j in range(nq - 1):
                    pos = slice((j + 1) * QB, (j + 2) * QB)
                    sq[slot, pos, sl] = dq[u0 + j].astype(BF16)
                    sk[slot, pos, sl] = (dk[u0 + j, QB:] + dk[u0 + j + 1, :QB]).astype(BF16)
                    sv[slot, pos, sl] = (dv[u0 + j, QB:] + dv[u0 + j + 1, :QB]).astype(BF16)
                carry[:, sl] = dq[last]
                carry[:, GW + h * HD:GW + (h + 1) * HD] = dk[last, QB:]
                carry[:, 2 * GW + h * HD:2 * GW + (h + 1) * HD] = dv[last, QB:]

            @pl.when(n == 0)
            def _():
                write(b * nsb * nq, 1, nq - 1)

            @pl.when(n > 0)
            def _():
                write((b * nsb + n) * nq - 1, 0, nq)

    def row(b, n):
        return b * nsb + jnp.minimum(n, nsb - 1)

    def prev(b, n):
        return jnp.maximum(row(b, n) * nq - 1, 0)

    in_specs = [
        pl.BlockSpec((rows, GW), lambda b, n: (row(b, n), CB_Q)),
        pl.BlockSpec((rows, GW), lambda b, n: (row(b, n), CB_K)),
        pl.BlockSpec((rows, GW), lambda b, n: (row(b, n), CB_V)),
        pl.BlockSpec((rows, GW), lambda b, n: (row(b, n), 0)),
        pl.BlockSpec((rows, 128), lambda b, n: (row(b, n), 0)),
        pl.BlockSpec((QB, GW), lambda b, n: (prev(b, n), CB_K)),
        pl.BlockSpec((QB, GW), lambda b, n: (prev(b, n), CB_V)),
        pl.BlockSpec((None, 2, 4, QB, 2 * QB), lambda b, n: (0, 0, 0, 0, 0)),
        pl.BlockSpec(memory_space=pl.ANY),
    ]
    return pl.pallas_call(
        body, name="attn_bwd0",
        grid=(BL, nsb + 1),
        in_specs=in_specs,
        out_specs=(pl.BlockSpec(memory_space=pl.ANY),
                   pl.BlockSpec((4, QB, 2 * QB), lambda b, n: (0, 0, 0))),
        out_shape=(jax.ShapeDtypeStruct((T, NCOL), BF16),
                   jax.ShapeDtypeStruct((4, QB, 2 * QB), F32)),
        scratch_shapes=[pltpu.VMEM((2, rows, GW), BF16)] * 3
                       + [pltpu.VMEM((QB, 3 * GW), F32), pltpu.SemaphoreType.DMA((2, nq, 3))],
        input_output_aliases={8: 0},
        compiler_params=pltpu.CompilerParams(vmem_limit_bytes=VMEM_LIMIT),
    )(proj, proj, proj, d_out, stats, proj, proj, bias, dproj)


def _attention_forward(proj, rel_bias):
    expand_lanes, grad_lanes, masks = (jnp.asarray(t) for t in _bucket_maps())
    bias = _bias_expand(rel_bias, expand_lanes, masks)
    fwd = [_attn_fwd_dense(proj, bias)] + [_attn_fwd(proj, bias, g) for g in (1, 2)]
    return bias, grad_lanes, [f[0] for f in fwd], [f[1] for f in fwd]


def _local_step(x2, tgt2, mod3, h, proj, attn, w_ao, w_co, w_o, conv_w, conv_b, ln_g, ln_b):
    bias, buckets, o_g, lse_g = attn

    (dproj, dyc, d_o, stats, dxd, gw_o, gw_co, gw_ao, tail_vec) = _tail(
        x2, tgt2, mod3, o_g, lse_g, proj, w_ao, w_co, w_o, conv_w, conv_b, ln_g, ln_b)

    dproj, db = _attn_bwd_dense(proj, d_o, stats, bias, dproj)
    dbias = [db]
    for g in (1, 2):
        dproj, db = _attn_bwd(proj, d_o, stats, bias, dproj, g)
        dbias.append(db)
    g_rel_bias = _bias_grad(*dbias, buckets)
    dproj, conv_vec = _conv_bwd(dyc, proj, conv_w, dproj)

    gw_ao = jnp.transpose(gw_ao.reshape(GW, N_DEV, D // N_DEV), (1, 0, 2))
    return dproj, dxd, gw_ao, gw_co, gw_o, conv_vec, g_rel_bias, tail_vec


def kernel(x, c, w_ada, b_ada, w_in, conv_w, conv_b, rel_bias, w_attn_out, w_conv_out, w_o, ln_g, ln_b, loss_target, m_w_ada, m_b_ada, m_w_in, m_conv_w, m_conv_b, m_rel_bias, m_w_attn_out, m_w_conv_out, m_w_o, m_ln_g, m_ln_b, v_w_ada, v_b_ada, v_w_in, v_conv_w, v_conv_b, v_rel_bias, v_w_attn_out, v_w_conv_out, v_w_o, v_ln_g, v_ln_b):
    me = _my_index()
    x2 = x.reshape(T, D)
    tgt2 = loss_target.reshape(T, D)

    b_cols = lax.dynamic_slice(b_ada, (0, me * ADA_SHARD), (1, ADA_SHARD))
    c_g, mod_in = _mod_exchange(jnp.pad(c, ((0, 8 - BL), (0, 0))), w_ada[0], b_cols)
    c_all = c_g[:, 0:BL, :].reshape(N_DEV * BL, D)
    mod3 = jnp.transpose(mod_in[:, 0:BL, :], (1, 0, 2)).reshape(BL, 3, D)

    h = _prep_h(x2, mod3)
    rows_shape = jax.ShapeDtypeStruct((N_DEV, D // N_DEV, D), BF16)
    proj, w_in_all, (w_ao_g, w_co_g, w_o_g, conv_w_g) = _gather_proj(
        _shard_order(), h, w_in[0].astype(BF16), 1024,
        ([w_attn_out[0].astype(BF16), w_conv_out[0].astype(BF16), w_o[0].astype(BF16), conv_w[0]],
         [jax.ShapeDtypeStruct((N_DEV, GW, D // N_DEV), BF16), rows_shape, rows_shape,
          jax.ShapeDtypeStruct((N_DEV, 3, D // N_DEV), F32)]))

    attn = _attention_forward(proj, rel_bias)
    w_ao_full = jnp.transpose(w_ao_g, (1, 0, 2)).reshape(GW, D)
    w_co_full = w_co_g.reshape(D, D)
    w_o_full = w_o_g.reshape(D, D)
    conv_w_full = jnp.transpose(conv_w_g, (1, 0, 2)).reshape(3, D)

    (dproj, dxd, gw_ao, gw_co, gw_o, conv_vec, g_rel_bias, tail_vec) = _local_step(
        x2, tgt2, mod3, h, proj, attn, w_ao_full, w_co_full, w_o_full,
        conv_w_full, conv_b, ln_g, ln_b)

    g_conv_w_blocks = jnp.transpose(conv_vec[0:3].reshape(3, N_DEV, D // N_DEV), (1, 0, 2))
    partials = [gw_ao, gw_co.reshape(N_DEV, D // N_DEV, D), gw_o.reshape(N_DEV, D // N_DEV, D), g_conv_w_blocks]
    w_in_sums, w_in_parts, sib = _gw_in_pair(
        _slice_order(), h, dproj, partials,
        [jax.ShapeDtypeStruct((4, GW, D // N_DEV), BF16),
         jax.ShapeDtypeStruct((4, D // N_DEV, D), BF16),
         jax.ShapeDtypeStruct((4, D // N_DEV, D), BF16),
         jax.ShapeDtypeStruct((4, 3, D // N_DEV), F32)])
    core = lax.axis_index("c").astype(jnp.int32).reshape(1)
    chip_sums = [w_in_sums] + list(_pair_add(core, partials, sib))
    hops = [(3,)] + [(1, 2, 3)] * 4
    grad_x, mod_vec, (r_in, r_ao, r_co, r_o, r_cw) = _dh_dx(
        dproj, w_in_all, x2, dxd, mod3, chip_sums, hops, w_in_parts)

    small = jnp.concatenate([
        tail_vec[0:4],
        jnp.pad(g_rel_bias.reshape(1, N_BUCKETS * N_HEADS), ((0, 0), (0, D - N_BUCKETS * N_HEADS))),
        jnp.zeros((3, D), F32)], axis=0)
    dmod = jnp.concatenate([mod_vec[0:2], mod_vec[2:4], tail_vec[4:6]], axis=1)
    small_g, dmod_g = _all_gather(
        [small, dmod],
        [jax.ShapeDtypeStruct((N_DEV, 8, D), F32), jax.ShapeDtypeStruct((N_DEV, BL, 3 * D), F32)],
        "gather_small")
    dmod_all = dmod_g.reshape(N_DEV * BL, 3 * D)
    small_names = ["b_ada", "conv_b", "rel_bias", "ln_g", "ln_b"]
    small_params = [(b_ada, m_b_ada, v_b_ada), (conv_b, m_conv_b, v_conv_b), (rel_bias, m_rel_bias, v_rel_bias),
                    (ln_g, m_ln_g, v_ln_g), (ln_b, m_ln_b, v_ln_b)]
    loss, small_res = _small_updates(
        small_g, dmod_all, small_g[:, 4, :N_BUCKETS * N_HEADS].reshape(N_DEV, N_BUCKETS, N_HEADS), small_params)
    loss = loss.reshape(())

    dmod_cols = lax.dynamic_slice(dmod_all, (0, me * ADA_SHARD), (N_DEV * BL, ADA_SHARD))
    res = {
        "w_ada": tuple(t[None] for t in _w_ada_update(jnp.transpose(c_all), dmod_cols,
                                                      w_ada[0], m_w_ada[0], v_w_ada[0])),
        "w_in": tuple(t[None] for t in _adamw(r_in, w_in[0], m_w_in[0], v_w_in[0], "adam_w_in", 128)),
    }
    mid_names = ["conv_w", "w_attn_out", "w_conv_out", "w_o"]
    mid_parts = [r_cw, r_ao, r_co, r_o]
    mid_full = [(conv_w, m_conv_w, v_conv_w), (w_attn_out, m_w_attn_out, v_w_attn_out),
                (w_conv_out, m_w_conv_out, v_w_conv_out), (w_o, m_w_o, v_w_o)]
    mid_res = _multi_adamw(mid_parts, [tuple(t[0] for t in wmv) for wmv in mid_full], "adam_mid")
    for nm, wmv, outs4 in zip(mid_names, mid_full, mid_res):
        res[nm] = tuple(t[None] for t in outs4)
    res.update(dict(zip(small_names, small_res)))
    order = ["w_ada", "b_ada", "w_in", "conv_w", "conv_b", "rel_bias", "w_attn_out", "w_conv_out",
             "w_o", "ln_g", "ln_b"]
    outs = [loss, grad_x.reshape(BL, S, D)]
    for k in range(4):
        outs += [res[name][k] for name in order]
    return tuple(outs)
```

```python
import math

import numpy as np
import jax
import jax.numpy as jnp
from jax import lax
from jax.experimental import pallas as pl
from jax.experimental.pallas import tpu as pltpu

F32 = jnp.float32
BF16 = jnp.bfloat16
MESH = pl.DeviceIdType.MESH

N_DEV = 8
D = 1024
S = 2048
BL = 2
T = BL * S
NCOL = 11264
SHARD = NCOL // N_DEV
CB = 512
NCB = NCOL // CB
HD = 128
GW = 512
QB = 128
DILATIONS = (1, 4, 16)
N_STEPS = 128
N_BUCKETS = 32
N_HEADS = 12
ALPHA = 2.0 ** 0.25
LN_EPS = 1e-5
NEG_INF = -1e30
SCALE = HD ** -0.5
ADA_SHARD = 3 * D // N_DEV

CB_Q, CB_K, CB_V, CB_GA = 0, 3, 6, 9
KB_U, KB_BG, KB_CG, KB_GC, KB_MA, KB_MC = 5, 6, 7, 8, 9, 10

ADAM_LR, ADAM_B1, ADAM_B2, ADAM_EPS, ADAM_WD, ADAM_STEP = 0.001, 0.9, 0.999, 1e-08, 0.01, 10

VMEM_LIMIT = 56 * 1024 * 1024
VMEM_LIMIT_TAIL = 62 * 1024 * 1024


def _dot(a, b):
    return jnp.dot(a, b, preferred_element_type=F32)


def _dot_nt(a, b):
    return lax.dot_general(a, b, (((1,), (1,)), ((), ())), preferred_element_type=F32)


def _dot_tn(a, b):
    return lax.dot_general(a, b, (((0,), (0,)), ((), ())), preferred_element_type=F32)


def _sigmoid(v):
    return 1.0 / (1.0 + jnp.exp(-v))


def _column_copies(pieces, dst_hbm, row0, sems):
    copies = []
    for k, (src, col0) in enumerate(pieces):
        rows, width = src.shape
        copies.append(pltpu.make_async_copy(
            src, dst_hbm.at[pl.ds(row0, rows), pl.ds(col0, width)], sems.at[k]))
    return copies


def _my_index():
    return 4 * lax.axis_index("x") + 2 * lax.axis_index("y") + lax.axis_index("c")


class _Gather:
    def __init__(self, ins, outs, stage, send_sems, recv_sems, local_sems, direct=False):
        self.direct = direct
        self.ins, self.outs, self.stage = ins, outs, stage
        self.send_sems, self.recv_sems, self.local_sems = send_sems, recv_sems, local_sems
        x, y, c = lax.axis_index("x"), lax.axis_index("y"), lax.axis_index("c")
        self.c = c
        self.me, self.sibling = (x, y, c), (x, y, 1 - c)
        self.chips = [(1 - x, y), (x, 1 - y), (1 - x, 1 - y)]

    @staticmethod
    def scratch(arrs):
        n = len(arrs)
        return ([pltpu.SemaphoreType.DMA((7 * n,)), pltpu.SemaphoreType.DMA((7 * n,)),
                 pltpu.SemaphoreType.DMA((n,))] + [pltpu.VMEM(a.shape, a.dtype) for a in arrs])

    def _copy(self, a, k, block, to, src=None):
        dst = self.outs[a].at[4 * block[0] + 2 * block[1] + block[2]]
        return pltpu.make_async_remote_copy(
            src_ref=dst if src is None else src, dst_ref=dst,
            send_sem=self.send_sems.at[a * 7 + k], recv_sem=self.recv_sems.at[a * 7 + k],
            device_id=to, device_id_type=MESH)

    def _first(self):
        first = []
        for a in range(len(self.ins)):
            first.append(self._copy(a, 0, self.me, self.sibling, src=self.ins[a]))
            first += [self._copy(a, 1 + j, self.me, (*chip, self.c), src=self.ins[a])
                      for j, chip in enumerate(self.chips)]
            if self.direct:
                first += [self._copy(a, 4 + j, self.me, (*chip, 1 - self.c), src=self.ins[a])
                          for j, chip in enumerate(self.chips)]
        return first

    def _mine(self):
        me = self.me
        return [pltpu.make_async_copy(self.stage[a], self.outs[a].at[4 * me[0] + 2 * me[1] + me[2]],
                                      self.local_sems.at[a]) for a in range(len(self.ins))]

    def begin(self):
        for cp in self._first():
            cp.start()
        loads = [pltpu.make_async_copy(self.ins[a], self.stage[a], self.local_sems.at[a])
                 for a in range(len(self.ins))]
        for cp in loads:
            cp.start()
        for cp in loads:
            cp.wait()
        for cp in self._mine():
            cp.start()

    def finish(self):
        n, c, me, sibling = len(self.ins), self.c, self.me, self.sibling
        passed = []
        for j, chip in enumerate(self.chips):
            for a in range(n):
                self._copy(a, 1 + j, (*chip, c), me).wait_recv()
                if not self.direct:
                    fwd = self._copy(a, 4 + j, (*chip, c), sibling)
                    fwd.start()
                    passed.append(fwd)
        for a in range(n):
            self._copy(a, 0, sibling, me).wait_recv()
        for j, chip in enumerate(self.chips):
            for a in range(n):
                self._copy(a, 4 + j, (*chip, 1 - c), me).wait_recv()
        for cp in self._first() + passed:
            cp.wait_send()
        for cp in self._mine():
            cp.wait()


def _all_gather(arrs, out_shapes, name):
    n = len(arrs)

    def body(*refs):
        g = _Gather(refs[:n], refs[n:2 * n], refs[2 * n + 3:], *refs[2 * n:2 * n + 3], direct=True)
        g.begin()
        g.finish()

    any_spec = pl.BlockSpec(memory_space=pl.ANY)
    return pl.pallas_call(
        body, name=name,
        out_shape=tuple(out_shapes),
        in_specs=[any_spec] * n,
        out_specs=tuple([any_spec] * n),
        scratch_shapes=_Gather.scratch(arrs),
    )(*arrs)


def _neighbour_chips():
    x, y, c = lax.axis_index("x"), lax.axis_index("y"), lax.axis_index("c")
    first = (jnp.where(c == 0, 1 - x, x), jnp.where(c == 0, y, 1 - y))
    second = (jnp.where(c == 0, x, 1 - x), jnp.where(c == 0, 1 - y, y))
    return first, second, (1 - x, 1 - y)


def _slice_order():
    x, y, c = lax.axis_index("x"), lax.axis_index("y"), lax.axis_index("c")
    nb1, nb2, diag = _neighbour_chips()
    slots = []
    for mine, theirs in ((nb1, nb2), (nb2, nb1), (diag, diag), ((x, y), (x, y))):
        slots += [2 * (2 * theirs[0] + theirs[1]) + 1 - c, 2 * (2 * mine[0] + mine[1]) + c]
    return jnp.stack(slots).astype(jnp.int32)


def _gw_in_pair(order, h, dproj, smalls, small_shapes4):
    kk, m = h.shape
    tk = min(kk, 2048)
    nk = kk // tk
    ncols = dproj.shape[1] // N_DEV
    n = len(smalls)

    def body(order_ref, h_ref, d_ref, *rest):
        ins = rest[:n]
        sums_hbm, parts_hbm = rest[n], rest[n + 1]
        sib = rest[n + 2:2 * n + 2]
        (acc, sendbuf, recvbuf, sumbuf, send_sems, recv_sems, local_sem, ssend, srecv,
         isend, irecv) = rest[2 * n + 2:]
        js, k = pl.program_id(0), pl.program_id(1)
        x, y, c = lax.axis_index("x"), lax.axis_index("y"), lax.axis_index("c")
        sibling = (x, y, 1 - c)
        my_chip = 2 * x + y
        nb1, nb2, _ = _neighbour_chips()
        near = [(*nb1, c), (*nb2, c)]

        def ici_copy(p, out_chip):
            peer = near[p] if isinstance(p, int) else tuple(jnp.where(p == 0, a, b) for a, b in zip(*near))
            return pltpu.make_async_remote_copy(
                src_ref=sumbuf.at[p], dst_ref=parts_hbm.at[out_chip],
                send_sem=isend.at[p], recv_sem=irecv.at[p], device_id=peer, device_id_type=MESH)

        def small_copies():
            return [pltpu.make_async_remote_copy(
                        src_ref=ins[a].at[2 * q + 1 - c], dst_ref=sib[a].at[q],
                        send_sem=ssend.at[a * 4 + q], recv_sem=srecv.at[a * 4 + q],
                        device_id=sibling, device_id_type=MESH)
                    for a in range(n) for q in range(4)]

        def slice_copy(p):
            return pltpu.make_async_remote_copy(
                src_ref=sendbuf, dst_ref=recvbuf.at[p], send_sem=send_sems.at[p], recv_sem=recv_sems.at[p],
                device_id=sibling, device_id_type=MESH)

        def sum_copy(p):
            return pltpu.make_async_copy(sumbuf.at[2], sums_hbm.at[order_ref[2 * p] // 2], local_sem)

        @pl.when((js == 0) & (k == 0))
        def _():
            for cp in small_copies():
                cp.start()

        def partial():
            return _dot_tn(h_ref[...], d_ref[...])

        if nk > 1:
            @pl.when(k == 0)
            def _():
                acc[...] = partial()
        if nk > 2:
            @pl.when((k > 0) & (k < nk - 1))
            def _():
                acc[...] += partial()

        def total():
            return partial() + acc[...] if nk > 1 else partial()

        p = js // 2

        @pl.when((js % 2 == 0) & (k == nk - 1))
        def _():
            @pl.when(p > 0)
            def _():
                slice_copy(p - 1).wait_send()
            sendbuf[...] = total().astype(BF16)
            slice_copy(p).start()

        @pl.when((js % 2 == 1) & (k == nk - 1))
        def _():
            slice_copy(p).wait_recv()

            @pl.when(p == 3)
            def _():
                sum_copy(2).wait()
            sumbuf[jnp.minimum(p, 2)] = (total() + recvbuf[p].astype(F32)).astype(BF16)

            @pl.when(p < 2)
            def _():
                ici_copy(p, my_chip).start()

            @pl.when(p >= 2)
            def _():
                sum_copy(p).start()

        @pl.when((js == N_DEV - 1) & (k == nk - 1))
        def _():
            slice_copy(3).wait_send()
            sum_copy(3).wait()
            for cp in small_copies():
                cp.wait()
            for p in range(2):
                ici_copy(p, 2 * near[p][0] + near[p][1]).wait_recv()
                ici_copy(p, my_chip).wait_send()

    any_spec = pl.BlockSpec(memory_space=pl.ANY)
    res = pl.pallas_call(
        body, name="gw_in_pair",
        grid_spec=pltpu.PrefetchScalarGridSpec(
            num_scalar_prefetch=1,
            grid=(N_DEV, nk),
            in_specs=[pl.BlockSpec((tk, m), lambda js, k, order_ref: (k, 0)),
                      pl.BlockSpec((tk, ncols), lambda js, k, order_ref: (k, order_ref[js]))] + [any_spec] * n,
            out_specs=(any_spec,) * (n + 2),
            scratch_shapes=[pltpu.VMEM((m, ncols), F32), pltpu.VMEM((m, ncols), BF16),
                            pltpu.VMEM((4, m, ncols), BF16), pltpu.VMEM((3, m, ncols), BF16),
                            pltpu.SemaphoreType.DMA((4,)), pltpu.SemaphoreType.DMA((4,)),
                            pltpu.SemaphoreType.DMA,
                            pltpu.SemaphoreType.DMA((4 * n,)), pltpu.SemaphoreType.DMA((4 * n,)),
                            pltpu.SemaphoreType.DMA((2,)), pltpu.SemaphoreType.DMA((2,))]),
        out_shape=(jax.ShapeDtypeStruct((4, m, ncols), BF16),) * 2 + tuple(small_shapes4),
        compiler_params=pltpu.CompilerParams(vmem_limit_bytes=VMEM_LIMIT),
    )(order, h, dproj, *smalls)
    return res[0], res[1], res[2:]


def _chip_copies(ins, outs, send_sems, recv_sems, local_sems, hops):
    n = len(ins)
    x, y, c = lax.axis_index("x"), lax.axis_index("y"), lax.axis_index("c")
    my_chip = 2 * x + y

    def peer_of(k):
        return ((1 - x) if (k >> 1) & 1 else x, (1 - y) if k & 1 else y, c)

    def copy(a, k, out_chip):
        peer = peer_of(k)
        return pltpu.make_async_remote_copy(
            src_ref=ins[a].at[2 * peer[0] + peer[1]], dst_ref=outs[a].at[out_chip],
            send_sem=send_sems.at[a * 3 + k - 1], recv_sem=recv_sems.at[a * 3 + k - 1],
            device_id=peer, device_id_type=MESH)

    sends = [copy(a, k, my_chip) for k in range(1, 4) for a in range(n) if k in hops[a]]
    arrivals = []
    for k in range(1, 4):
        peer = peer_of(k)
        arrivals += [copy(a, k, 2 * peer[0] + peer[1]) for a in range(n) if k in hops[a]]
    mine = [pltpu.make_async_copy(ins[a].at[my_chip], outs[a].at[my_chip], local_sems.at[a])
            for a in range(n)]
    return sends, arrivals, mine


def _pair_add(core, mines, theirs):
    n = len(mines)

    def body(core_ref, *refs):
        mine, sib, outs = refs[:n], refs[n:2 * n], refs[2 * n:]
        for a in range(n):
            for q in range(4):
                outs[a][q] = (mine[a][2 * q + core_ref[0]].astype(F32)
                              + sib[a][q].astype(F32)).astype(outs[a].dtype)

    return pl.pallas_call(
        body, name="pair_add",
        in_specs=[pl.BlockSpec(memory_space=pltpu.SMEM)] + [pl.BlockSpec(memory_space=pltpu.VMEM)] * (2 * n),
        out_shape=tuple(jax.ShapeDtypeStruct(t.shape, t.dtype) for t in theirs),
    )(core, *mines, *theirs)


def _mod_exchange(c8, w_ada, b_cols):
    cols = w_ada.shape[1]

    def body(c_ref, w_ref, b_ref, call_ref, mod_ref, msend, send1, recv1, send2, recv2):
        x, y, c = lax.axis_index("x"), lax.axis_index("y"), lax.axis_index("c")
        my_slot = 4 * x + 2 * y + c

        def peer_of(k):
            return ((1 - x) if (k >> 2) & 1 else x, (1 - y) if (k >> 1) & 1 else y, (1 - c) if k & 1 else c)

        def slot_of(dev):
            return 4 * dev[0] + 2 * dev[1] + dev[2]

        def exchange(src_of, dst_ref, send_sems, recv_sems):
            sends, arrivals = [], []
            for k in range(1, 8):
                peer = peer_of(k)
                sends.append(pltpu.make_async_remote_copy(
                    src_ref=src_of(slot_of(peer)), dst_ref=dst_ref.at[my_slot],
                    send_sem=send_sems.at[k - 1], recv_sem=recv_sems.at[k - 1],
                    device_id=peer, device_id_type=MESH))
                arrivals.append(pltpu.make_async_remote_copy(
                    src_ref=src_of(my_slot), dst_ref=dst_ref.at[slot_of(peer)],
                    send_sem=send_sems.at[k - 1], recv_sem=recv_sems.at[k - 1],
                    device_id=peer, device_id_type=MESH))
            for cp in sends:
                cp.start()
            for cp in arrivals:
                cp.wait_recv()
            for cp in sends:
                cp.wait_send()

        call_ref[my_slot] = c_ref[...]
        exchange(lambda s: c_ref, call_ref, send1, recv1)
        cv = call_ref[...].reshape(N_DEV * 8, c_ref.shape[1])
        act = cv * _sigmoid(cv)
        mod = jnp.dot(act, w_ref[...], preferred_element_type=F32,
                      precision=lax.Precision.HIGHEST) + b_ref[...]
        msend[...] = mod.reshape(N_DEV, 8, cols)
        mod_ref[my_slot] = msend[my_slot]
        exchange(lambda s: msend.at[s], mod_ref, send2, recv2)

    return pl.pallas_call(
        body, name="mod_exchange",
        out_shape=(jax.ShapeDtypeStruct((N_DEV, 8, c8.shape[1]), F32),
                   jax.ShapeDtypeStruct((N_DEV, 8, cols), F32)),
        scratch_shapes=[pltpu.VMEM((N_DEV, 8, cols), F32)] + [pltpu.SemaphoreType.DMA((7,))] * 4,
    )(c8, w_ada, b_cols)


def _w_ada_update(c_all_t, dmod_cols, w, m, v):
    rows, cols = w.shape
    tr = 256

    def body(c_ref, d_ref, w_ref, m_ref, v_ref, g_ref, dl_ref, nm_ref, nv_ref):
        cv = c_ref[...]
        g = jnp.dot(cv * _sigmoid(cv), d_ref[...], preferred_element_type=F32,
                    precision=lax.Precision.HIGHEST)
        g_ref[...] = g
        dl_ref[...], nm_ref[...], nv_ref[...] = _adam_step(g, w_ref[...], m_ref[...], v_ref[...])

    blk = pl.BlockSpec((tr, cols), lambda i: (i, 0))
    shp = jax.ShapeDtypeStruct((rows, cols), F32)
    return pl.pallas_call(
        body, name="adam_w_ada",
        grid=(rows // tr,),
        in_specs=[pl.BlockSpec((tr, c_all_t.shape[1]), lambda i: (i, 0)),
                  pl.BlockSpec(dmod_cols.shape, lambda i: (0, 0)), blk, blk, blk],
        out_specs=(blk, blk, blk, blk),
        out_shape=(shp, shp, shp, shp),
    )(c_all_t, dmod_cols, w, m, v)


def _shard_order():
    x, y, c = lax.axis_index("x"), lax.axis_index("y"), lax.axis_index("c")
    first, second, diag = _neighbour_chips()
    devs = [(x, y, c), (x, y, 1 - c), (*first, c), (*second, 1 - c), (*second, c), (*first, 1 - c),
            (*diag, c), (*diag, 1 - c)]
    return jnp.stack([4 * d[0] + 2 * d[1] + d[2] for d in devs]).astype(jnp.int32)


def _prep_h(x2, mod3):
    ts = 512
    per_seq = S // ts

    def body(x_ref, mod_ref, h_ref):
        shift = mod_ref[0, 0:1, :]
        scale = mod_ref[0, 1:2, :]
        h_ref[...] = (x_ref[...] * (1.0 + scale) + shift).astype(BF16)

    return pl.pallas_call(
        body, name="prep_h",
        grid=(T // ts,),
        in_specs=[pl.BlockSpec((ts, D), lambda i: (i, 0)),
                  pl.BlockSpec((1, 3, D), lambda i: (i // per_seq, 0, 0))],
        out_specs=pl.BlockSpec((ts, D), lambda i: (i, 0)),
        out_shape=jax.ShapeDtypeStruct((T, D), BF16),
    )(x2, mod3)


def _gather_proj(order, h, w_shard, tm, ride=()):
    rows, kdim = h.shape
    ncols = w_shard.shape[1]
    n_i = rows // tm
    ride_arrs, ride_shapes = ride if ride else ((), ())
    n_ride = len(ride_arrs)

    def body(order_ref, h_ref, mine_hbm, *rest):
        ride_ins = rest[:n_ride]
        o_ref, all_hbm = rest[n_ride:n_ride + 2]
        ride_outs = rest[n_ride + 2:2 * n_ride + 2]
        wv, send_sems, recv_sems, local_sems = rest[2 * n_ride + 2:2 * n_ride + 6]
        ride_scr = rest[2 * n_ride + 6:]
        j, i = pl.program_id(0), pl.program_id(1)
        c = lax.axis_index("c")
        me, sibling = (lax.axis_index("x"), lax.axis_index("y"), c), (lax.axis_index("x"), lax.axis_index("y"), 1 - c)
        nb1, nb2, diag = _neighbour_chips()

        def slot(dev):
            return 4 * dev[0] + 2 * dev[1] + dev[2]

        def copy(k, block, to, src=None, part=None):
            buf = wv.at[slot(block)]
            if part is not None:
                buf = buf.at[pl.ds(pl.multiple_of(part * (kdim // 2), kdim // 2), kdim // 2)]
            return pltpu.make_async_remote_copy(
                src_ref=buf if src is None else src, dst_ref=buf,
                send_sem=send_sems.at[k], recv_sem=recv_sems.at[k],
                device_id=to, device_id_type=MESH)

        def keep(step, block):
            s = slot(block)
            cols = pl.ds(pl.multiple_of((s % 2) * ncols, 128), ncols)
            return pltpu.make_async_copy(wv.at[s], all_hbm.at[s // 2, :, cols], local_sems.at[step])

        if n_ride:
            gather = _Gather(ride_ins, ride_outs, ride_scr[3:], *ride_scr[:3])
        to_sibling, to_nb1, to_nb2 = (copy(0, me, sibling, mine_hbm), copy(1, me, (*nb1, c), mine_hbm),
                                      copy(2, me, (*nb2, c), mine_hbm))
        relay1, relay2 = copy(3, (*nb2, c), (*nb1, c), part=c), copy(4, (*nb1, c), (*nb2, c), part=1 - c)
        pass_nb1, pass_nb2 = copy(5, (*nb1, c), sibling), copy(6, (*nb2, c), sibling)
        pass_d1, pass_d2 = copy(7, (*diag, c), sibling, part=c), copy(8, (*diag, c), sibling, part=1 - c)
        sends = [to_sibling, to_nb1, to_nb2, relay1, relay2, pass_nb1, pass_nb2, pass_d1, pass_d2]
        due = [
            (me, [], []),
            (sibling, [copy(0, sibling, me)], [[]]),
            ((*nb1, c), [copy(1, (*nb1, c), me)], [[pass_nb1, to_nb2]]),
            ((*nb2, 1 - c), [copy(5, (*nb2, 1 - c), me)], [[]]),
            ((*nb2, c), [copy(2, (*nb2, c), me)], [[pass_nb2, relay1, relay2]]),
            ((*nb1, 1 - c), [copy(6, (*nb1, 1 - c), me)], [[]]),
            ((*diag, c), [copy(3, (*diag, c), me, part=c), copy(4, (*diag, c), me, part=1 - c)],
             [[pass_d1], [pass_d2]]),
            ((*diag, 1 - c), [copy(7, (*diag, 1 - c), me, part=1 - c), copy(8, (*diag, 1 - c), me, part=c)],
             [[], []]),
        ]

        @pl.when((j == 0) & (i == 0))
        def _():
            to_sibling.start()
            to_nb1.start()
            load = pltpu.make_async_copy(mine_hbm, wv.at[slot(me)], local_sems.at[N_DEV])
            load.start()
            load.wait()
            keep(0, me).start()

        for step in range(1, N_DEV):
            block, arrivals, then = due[step]

            @pl.when((j == step) & (i == 0))
            def _():
                for arrival, follow in zip(arrivals, then):
                    arrival.wait_recv()
                    for cp in follow:
                        cp.start()
                keep(step, block).start()
                if n_ride and step == N_DEV - 2:
                    gather.begin()

        o_ref[...] = _dot(h_ref[...], wv[order_ref[j]]).astype(BF16)

        @pl.when((j == N_DEV - 1) & (i == n_i - 1))
        def _():
            for cp in sends:
                cp.wait_send()
            for step in range(N_DEV):
                keep(step, due[step][0]).wait()
            if n_ride:
                gather.finish()

    any_spec = pl.BlockSpec(memory_space=pl.ANY)
    res = pl.pallas_call(
        body, name="gather_proj",
        grid_spec=pltpu.PrefetchScalarGridSpec(
            num_scalar_prefetch=1,
            grid=(N_DEV, n_i),
            in_specs=[pl.BlockSpec((tm, kdim), lambda j, i, order_ref: (i, 0)), any_spec] + [any_spec] * n_ride,
            out_specs=(pl.BlockSpec((tm, ncols), lambda j, i, order_ref: (i, order_ref[j])), any_spec)
                      + (any_spec,) * n_ride,
            scratch_shapes=[pltpu.VMEM((N_DEV, kdim, ncols), BF16),
                            pltpu.SemaphoreType.DMA((9,)), pltpu.SemaphoreType.DMA((9,)),
                            pltpu.SemaphoreType.DMA((N_DEV + 1,))]
                           + (_Gather.scratch(ride_arrs) if n_ride else [])),
        out_shape=(jax.ShapeDtypeStruct((rows, N_DEV * ncols), BF16),
                   jax.ShapeDtypeStruct((N_DEV // 2, kdim, 2 * ncols), BF16)) + tuple(ride_shapes),
        compiler_params=pltpu.CompilerParams(vmem_limit_bytes=VMEM_LIMIT),
    )(order, h, w_shard, *ride_arrs)
    return res[0], res[1], res[2:]


SKEW_W = 512


def _bucket_maps():
    lanes = np.arange(SKEW_W)

    def buckets_of(steps):
        rows = []
        for dil in DILATIONS:
            dist = np.maximum(steps, 0) * dil
            nf = np.maximum(dist, 1).astype(np.float32)
            large = 16 + (np.log(nf / np.float32(16)) / np.float32(math.log(128.0))
                          * np.float32(16)).astype(np.int32)
            large = np.minimum(large, N_BUCKETS - 1)
            bucket = np.where(dist < 16, dist, large)
            rows.append(np.where((steps >= 0) & (steps <= N_STEPS), bucket, -1).astype(np.int32))
        return np.stack(rows)[:, None, :]

    a = np.arange(QB)[:, None]
    b = np.arange(2 * QB)[None, :]
    steps = a + QB - b
    band = (steps >= 0) & (steps <= N_STEPS)
    first = band & (b >= QB)
    masks = np.stack([first, band]).astype(np.int32)
    return buckets_of(QB - lanes), buckets_of(2 * QB - 1 - lanes), masks


def _bias_expand(rel_bias, lane_buckets, masks):
    def body(tab_ref, bk_ref, mk_ref, o_ref):
        for g in range(3):
            bk = bk_ref[g]
            for h in range(4):
                col = 4 * g + h
                per_offset = jnp.zeros((1, SKEW_W), F32)
                for k in range(N_BUCKETS):
                    per_offset = jnp.where(bk == k, tab_ref[k, col], per_offset)
                tile = pltpu.roll(jnp.broadcast_to(per_offset, (QB, SKEW_W)), 0, 1, stride=1, stride_axis=0)
                tile = tile[:, :2 * QB]
                o_ref[g, 0, h] = jnp.where(mk_ref[0] != 0, tile, NEG_INF)
                o_ref[g, 1, h] = jnp.where(mk_ref[1] != 0, tile, NEG_INF)

    return pl.pallas_call(
        body, name="bias_expand",
        in_specs=[pl.BlockSpec(memory_space=pltpu.SMEM),
                  pl.BlockSpec(memory_space=pltpu.VMEM),
                  pl.BlockSpec(memory_space=pltpu.VMEM)],
        out_shape=jax.ShapeDtypeStruct((3, 2, 4, QB, 2 * QB), F32),
    )(rel_bias, lane_buckets, masks)


def _bias_grad(ds1, ds2, ds3, lane_buckets):
    exchange = jnp.asarray(np.eye(QB, dtype=np.float32)[::-1].copy())

    def body(d1_ref, d2_ref, d3_ref, bk_ref, ex_ref, o_ref):
        for g, d_ref in enumerate((d1_ref, d2_ref, d3_ref)):
            bk = bk_ref[g]
            for h in range(4):
                flipped = jnp.dot(ex_ref[...], d_ref[h], preferred_element_type=F32,
                                  precision=lax.Precision.HIGHEST)
                padded = jnp.concatenate([flipped, jnp.zeros((QB, SKEW_W - 2 * QB), F32)], axis=1)
                skewed = pltpu.roll(padded, 0, 1, stride=1, stride_axis=0)
                per_offset = jnp.sum(skewed, axis=0, keepdims=True)
                for k in range(N_BUCKETS):
                    o_ref[k, 4 * g + h] = jnp.sum(jnp.where(bk == k, per_offset, 0.0))

    return pl.pallas_call(
        body, name="bias_grad",
        in_specs=[pl.BlockSpec(memory_space=pltpu.VMEM)] * 5,
        out_specs=pl.BlockSpec(memory_space=pltpu.SMEM),
        out_shape=jax.ShapeDtypeStruct((N_BUCKETS, N_HEADS), F32),
    )(ds1, ds2, ds3, lane_buckets, exchange)


def _scratch_sets(rows):
    return 4 if rows <= 512 else 1


def _unit_chunks(dil, size=16):
    units = [(h, r) for h in range(4) for r in range(dil)]
    return [units[i:i + size] for i in range(0, len(units), size)]


def _residue_rows(src_ref, copies, h, residue):
    buf = copies[h % len(copies)]
    buf[...] = src_ref[:, h * HD:(h + 1) * HD].astype(F32)
    return lambda r: buf[residue(r), :].astype(BF16)


def _attn_fwd(proj, bias, g):
    dil = DILATIONS[g]
    rows = QB * dil
    nsb = S // rows
    has_prev = nsb > 1

    def residue(r):
        return pl.ds(r, QB, stride=dil)

    n_sets = _scratch_sets(rows)
    n_in = 6 if has_prev else 4
    n_copied = (4 + (2 if has_prev else 0)) * n_sets

    def body(*refs):
        q_ref, kc_ref, vc_ref = refs[:3]
        kp_ref, vp_ref = refs[3:5] if has_prev else (None, None)
        b_ref = refs[n_in - 1]
        o_ref, l_ref = refs[n_in:n_in + 2]
        scr = list(refs[n_in + 2:])
        ls = [scr.pop(0) for _ in range(4)]
        copies = {name: [scr.pop(0) for _ in range(n_sets)]
                  for name in ("q", "kc", "vc", "o") + (("kp", "vp") if has_prev else ())}
        lane = lax.broadcasted_iota(jnp.int32, (QB, 128), 1)
        refs_of = {"q": q_ref, "kc": kc_ref, "vc": vc_ref, "kp": kp_ref, "vp": vp_ref}
        for chunk in _unit_chunks(dil):
            rows_of = {h: {name: _residue_rows(refs_of[name], copies[name], h, residue)
                           for name in refs_of if refs_of[name] is not None}
                       for h in sorted({h for h, _ in chunk})}

            def batch(name):
                return jnp.stack([rows_of[h][name](r) for h, r in chunk])

            q, k, v = batch("q"), batch("kc"), batch("vc")
            if has_prev:
                k = jnp.concatenate([batch("kp"), k], axis=1)
                v = jnp.concatenate([batch("vp"), v], axis=1)
                bias_b = jnp.stack([b_ref[h] for h, _ in chunk])
            else:
                bias_b = jnp.stack([b_ref[h, :, QB:] for h, _ in chunk])
            s = jnp.einsum("uqd,ukd->uqk", q, k, preferred_element_type=F32) * SCALE + bias_b
            m = jnp.max(s, axis=-1, keepdims=True)
            p = jnp.exp(s - m)
            l = jnp.sum(p, axis=-1, keepdims=True)
            o = jnp.einsum("uqk,ukd->uqd", p.astype(BF16), v, preferred_element_type=F32) / l
            lse = m + jnp.log(l)
            for i, (h, r) in enumerate(chunk):
                copies["o"][h % n_sets][residue(r), :] = o[i]
                ls[h][r * QB:(r + 1) * QB, :] = jnp.where(lane == h, lse[i], 0.0)
            for h in sorted({h for h, _ in chunk}):
                o_ref[:, h * HD:(h + 1) * HD] = copies["o"][h % n_sets][...]
        for r in range(dil):
            blk = slice(r * QB, (r + 1) * QB)
            l_ref[residue(r), :] = (ls[0][blk, :] + ls[1][blk, :]) + (ls[2][blk, :] + ls[3][blk, :])

    def row(b, n):
        return b * nsb + n

    def prev(b, n):
        return b * nsb + jnp.maximum(n - 1, 0)

    in_specs = [
        pl.BlockSpec((rows, GW), lambda b, n: (row(b, n), CB_Q + g)),
        pl.BlockSpec((rows, GW), lambda b, n: (row(b, n), CB_K + g)),
        pl.BlockSpec((rows, GW), lambda b, n: (row(b, n), CB_V + g)),
    ]
    args = [proj, proj, proj]
    scratch = [pltpu.VMEM((rows, 128), F32)] * (4 + n_copied)
    if has_prev:
        in_specs += [pl.BlockSpec((rows, GW), lambda b, n: (prev(b, n), CB_K + g)),
                     pl.BlockSpec((rows, GW), lambda b, n: (prev(b, n), CB_V + g))]
        args += [proj, proj]
    in_specs.append(pl.BlockSpec((None, None, 4, QB, 2 * QB),
                                 lambda b, n: (g, jnp.minimum(n, 1), 0, 0, 0)))
    args.append(bias)
    return pl.pallas_call(
        body, name=f"attn_fwd{g}",
        grid=(BL, nsb),
        in_specs=in_specs,
        out_specs=(pl.BlockSpec((rows, GW), lambda b, n: (row(b, n), 0)),
                   pl.BlockSpec((rows, 128), lambda b, n: (row(b, n), 0))),
        out_shape=(jax.ShapeDtypeStruct((T, GW), F32), jax.ShapeDtypeStruct((T, 128), F32)),
        scratch_shapes=scratch,
        compiler_params=pltpu.CompilerParams(vmem_limit_bytes=VMEM_LIMIT),
    )(*args)


def _attn_bwd(proj, d_out, stats, bias, dproj, g):
    dil = DILATIONS[g]
    rows = QB * dil
    nsb = S // rows
    has_prev = nsb > 1
    n_steps = nsb + 1 if has_prev else 1
    n_in = 7 + (2 if has_prev else 0)

    def residue(r):
        return pl.ds(r, QB, stride=dil)

    n_sets = _scratch_sets(rows)

    def body(*refs):
        q_ref, kc_ref, vc_ref, do_ref, st_ref, b_ref = refs[:6]
        kp_ref, vp_ref = refs[6:8] if has_prev else (None, None)
        out_ref, db_ref = refs[n_in], refs[n_in + 1]
        scr = list(refs[n_in + 2:])
        sq, sk, sv, sems = [scr.pop(0) for _ in range(4)]
        carry = scr.pop(0) if has_prev else None
        sts = scr.pop(0)
        copies = {name: [scr.pop(0) for _ in range(n_sets)]
                  for name in ("q", "kc", "vc", "do", "dq", "dk", "dv") + (("kp", "vp") if has_prev else ())}
        b, n = pl.program_id(0), pl.program_id(1)

        @pl.when((b == 0) & (n == 0))
        def _():
            db_ref[...] = jnp.zeros_like(db_ref)

        def finish(h, r, dq, dk, dv):
            for name, val in (("dq", dq), ("dk", dk), ("dv", dv)):
                copies[name][h % n_sets][residue(r), :] = val

        step = b * n_steps + n
        slot = step % 2

        def stage_copies(s, row0):
            return _column_copies([(sq.at[s], CB * (CB_Q + g)), (sk.at[s], CB * (CB_K + g)),
                                   (sv.at[s], CB * (CB_V + g))], out_ref, row0, sems.at[s])

        @pl.when((step >= 2) & ((step - 2) % n_steps >= (1 if has_prev else 0)))
        def _():
            for cp in stage_copies(slot, 0):
                cp.wait()

        def finish_head(h):
            sl = slice(h * HD, (h + 1) * HD)
            sq[slot, :, sl] = copies["dq"][h % n_sets][...].astype(BF16)
            sk[slot, :, sl] = copies["dk"][h % n_sets][...].astype(BF16)
            sv[slot, :, sl] = copies["dv"][h % n_sets][...].astype(BF16)

        def write_block(blk_idx):
            for cp in stage_copies(slot, pl.multiple_of(blk_idx * rows, rows)):
                cp.start()

            @pl.when(step == BL * n_steps - 1)
            def _():
                for s in range(2):
                    for cp in stage_copies(s, 0):
                        cp.wait()

        def carried(h, r):
            blk = slice(r * QB, (r + 1) * QB)
            return ((blk, slice(h * HD, (h + 1) * HD)), (blk, slice(GW + h * HD, GW + (h + 1) * HD)),
                    (blk, slice(2 * GW + h * HD, 2 * GW + (h + 1) * HD)))

        if has_prev:
            @pl.when(n == 0)
            def _():
                carry[...] = jnp.zeros_like(carry)

            @pl.when(n == nsb)
            def _():
                for h in range(4):
                    for r in range(dil):
                        cq, ck, cv = carried(h, r)
                        finish(h, r, carry[cq], carry[ck], carry[cv])
                    finish_head(h)
                write_block(b * nsb + nsb - 1)

        @pl.when(n < nsb)
        def _():
            for r in range(dil):
                sts[r * QB:(r + 1) * QB, :] = st_ref[residue(r), :]
            refs_of = {"q": q_ref, "kc": kc_ref, "vc": vc_ref, "do": do_ref, "kp": kp_ref, "vp": vp_ref}
            for chunk in _unit_chunks(dil):
                heads = sorted({h for h, _ in chunk})
                rows_of = {h: {name: _residue_rows(refs_of[name], copies[name], h, residue)
                               for name in refs_of if refs_of[name] is not None}
                           for h in heads}

                def batch(name):
                    return jnp.stack([rows_of[h][name](r) for h, r in chunk])

                q, k, v, do = batch("q"), batch("kc"), batch("vc"), batch("do")
                if has_prev:
                    k = jnp.concatenate([batch("kp"), k], axis=1)
                    v = jnp.concatenate([batch("vp"), v], axis=1)
                    bias_b = jnp.stack([b_ref[h] for h, _ in chunk])
                else:
                    bias_b = jnp.stack([b_ref[h, :, QB:] for h, _ in chunk])
                lse = jnp.stack([sts[r * QB:(r + 1) * QB, h:h + 1] for h, r in chunk])
                delta = jnp.stack([sts[r * QB:(r + 1) * QB, 4 + h:5 + h] for h, r in chunk])
                s = jnp.einsum("uqd,ukd->uqk", q, k, preferred_element_type=F32) * SCALE + bias_b
                p = jnp.exp(s - lse)
                ds = p * (jnp.einsum("uqd,ukd->uqk", do, v, preferred_element_type=F32) - delta)
                for h in heads:
                    mine = [ds[i] for i, (hh, _) in enumerate(chunk) if hh == h]
                    tot = mine[0]
                    for extra in mine[1:]:
                        tot = tot + extra
                    if has_prev:
                        db_ref[h] += tot
                    else:
                        db_ref[h, :, QB:] += tot
                dsb, pb = ds.astype(BF16), p.astype(BF16)
                dq = jnp.einsum("uqk,ukd->uqd", dsb, k, preferred_element_type=F32) * SCALE
                dk = jnp.einsum("uqk,uqd->ukd", dsb, q, preferred_element_type=F32) * SCALE
                dv = jnp.einsum("uqk,uqd->ukd", pb, do, preferred_element_type=F32)
                for i, (h, r) in enumerate(chunk):
                    if has_prev:
                        cq, ck, cv = carried(h, r)
                        finish(h, r, carry[cq], carry[ck] + dk[i, :QB], carry[cv] + dv[i, :QB])
                        carry[cq] = dq[i]
                        carry[ck] = dk[i, QB:]
                        carry[cv] = dv[i, QB:]
                    else:
                        finish(h, r, dq[i], dk[i], dv[i])
                for h in heads:
                    finish_head(h)
            if has_prev:
                @pl.when(n > 0)
                def _():
                    write_block(b * nsb + n - 1)
            else:
                write_block(b)

    def row(b, n):
        return b * nsb + jnp.minimum(n, nsb - 1)

    def prev(b, n):
        return b * nsb + jnp.maximum(jnp.minimum(n, nsb - 1) - 1, 0)

    in_specs = [
        pl.BlockSpec((rows, GW), lambda b, n: (row(b, n), CB_Q + g)),
        pl.BlockSpec((rows, GW), lambda b, n: (row(b, n), CB_K + g)),
        pl.BlockSpec((rows, GW), lambda b, n: (row(b, n), CB_V + g)),
        pl.BlockSpec((rows, GW), lambda b, n: (row(b, n), 0)),
        pl.BlockSpec((rows, 128), lambda b, n: (row(b, n), 0)),
        pl.BlockSpec((None, None, 4, QB, 2 * QB),
                     lambda b, n: (g, jnp.minimum(jnp.minimum(n, nsb - 1), 1), 0, 0, 0)),
    ]
    args = [proj, proj, proj, d_out, stats, bias]
    scratch = [pltpu.VMEM((2, rows, GW), BF16)] * 3 + [pltpu.SemaphoreType.DMA((2, 3))]
    if has_prev:
        in_specs += [pl.BlockSpec((rows, GW), lambda b, n: (prev(b, n), CB_K + g)),
                     pl.BlockSpec((rows, GW), lambda b, n: (prev(b, n), CB_V + g))]
        args += [proj, proj]
        scratch.append(pltpu.VMEM((rows, 3 * GW), F32))
    n_copied = (7 + (2 if has_prev else 0)) * n_sets
    scratch += [pltpu.VMEM((rows, 128), F32)] * (1 + n_copied)
    in_specs.append(pl.BlockSpec(memory_space=pl.ANY))
    args.append(dproj)
    return pl.pallas_call(
        body, name=f"attn_bwd{g}",
        grid=(BL, n_steps),
        in_specs=in_specs,
        out_specs=(pl.BlockSpec(memory_space=pl.ANY),
                   pl.BlockSpec((4, QB, 2 * QB), lambda b, n: (0, 0, 0))),
        out_shape=(jax.ShapeDtypeStruct((T, NCOL), BF16),
                   jax.ShapeDtypeStruct((4, QB, 2 * QB), F32)),
        scratch_shapes=scratch,
        input_output_aliases={len(args) - 1: 0},
        compiler_params=pltpu.CompilerParams(vmem_limit_bytes=VMEM_LIMIT),
    )(*args)


def _tail(x2, tgt2, mod3, o_g, lse_g, proj, w_ao, w_co, w_o, conv_w, conv_b, ln_g, ln_b):
    tm = 256
    per_seq = S // tm
    halo = 16

    def body(x_ref, t_ref, mod_ref, o1_ref, o2_ref, o3_ref, l1_ref, l2_ref, l3_ref,
             ga_ref, u_ref, bg_ref, cg_ref, gc_ref, ma_ref, mc_ref, up_ref, cp_ref,
             wao_ref, wco_ref, wo_ref, cw_ref, cb_ref, lg_ref, lb_ref,
             dproj_ref, dyc_ref, do_ref, st_ref, dxd_ref,
             gwo_ref, gwco_ref, gwao_ref, vec_ref,
             dga_s, dbg_s, dgm_s, sems, acc_o, acc_co, acc_ao):
        i = pl.program_id(0)
        bidx = i // per_seq
        first = (i % per_seq) == 0

        @pl.when(i == 0)
        def _():
            vec_ref[...] = jnp.zeros_like(vec_ref)

        slot = i % 2

        def column_copies(s, row0):
            return _column_copies([(dga_s.at[s], CB * CB_GA), (dbg_s.at[s], D * KB_BG), (dgm_s.at[s], D * KB_GC)],
                                  dproj_ref, row0, sems.at[s])

        @pl.when(i >= 2)
        def _():
            for cp in column_copies(slot, 0):
                cp.wait()

        l1, l2, l3 = l1_ref[...], l2_ref[...], l3_ref[...]
        mx = jnp.maximum(jnp.maximum(l1, l2), l3)
        e1, e2, e3 = jnp.exp(l1 - mx), jnp.exp(l2 - mx), jnp.exp(l3 - mx)
        esum = e1 + e2 + e3
        lse_tot = mx + jnp.log(esum)
        w1, w2, w3 = e1 / esum, e2 / esum, e3 / esum

        def per_head(wv):
            return jnp.concatenate([jnp.broadcast_to(wv[:, h:h + 1], (tm, HD)) for h in range(4)], axis=1)

        o = per_head(w1) * o1_ref[...] + per_head(w2) * o2_ref[...] + per_head(w3) * o3_ref[...]

        ga = ga_ref[...].astype(F32)
        sig_ga = _sigmoid(ga)
        silu_ga = ga * sig_ga
        a_in = (o * silu_ga).astype(BF16)
        a_out = _dot(a_in, wao_ref[...])

        u = u_ref[...].astype(F32)
        cg = cg_ref[...].astype(F32)
        z = cg * u
        zp = cp_ref[...].astype(F32) * up_ref[...].astype(F32)
        zp = jnp.where(first, 0.0, zp)
        zcat = jnp.concatenate([zp, z], axis=0)
        z1 = pltpu.roll(zcat, 1, 0)[halo:]
        z2 = pltpu.roll(zcat, 2, 0)[halo:]
        y_conv = cw_ref[0:1, :] * z2 + cw_ref[1:2, :] * z1 + cw_ref[2:3, :] * z + cb_ref[...]
        gc = gc_ref[...].astype(F32)
        sig_gc = _sigmoid(gc)
        silu_gc = gc * sig_gc
        bg = bg_ref[...].astype(F32)
        bg_yc = bg * y_conv
        s_in = (bg_yc * silu_gc).astype(BF16)
        s_out = _dot(s_in, wco_ref[...])

        sa = _sigmoid(ma_ref[...].astype(F32))
        sc = _sigmoid(mc_ref[...].astype(F32))
        merged = (sa * a_out + sc * s_out).astype(BF16)
        y = _dot(merged, wo_ref[...])
        gate1 = 1.0 + mod_ref[0, 2:3, :]
        xv = x_ref[...]
        resid = ALPHA * xv + gate1 * y
        mu = jnp.mean(resid, axis=1, keepdims=True)
        xc = resid - mu
        var = jnp.mean(xc * xc, axis=1, keepdims=True)
        rstd = lax.rsqrt(var + LN_EPS)
        xhat = xc * rstd
        lg = lg_ref[...]
        err = xhat * lg + lb_ref[...] - t_ref[...]
        vec_ref[3:4, :] += (0.5 / D) * jnp.sum(err * err, axis=0, keepdims=True)

        vec_ref[1:2, :] += (1.0 / D) * jnp.sum(err * xhat, axis=0, keepdims=True)
        vec_ref[2:3, :] += (1.0 / D) * jnp.sum(err, axis=0, keepdims=True)
        dxh = err * (lg * (1.0 / D))
        dres = rstd * (dxh - jnp.mean(dxh, axis=1, keepdims=True)
                       - xhat * jnp.mean(dxh * xhat, axis=1, keepdims=True))
        dxd_ref[...] = ALPHA * dres
        dgate = jnp.sum(dres * y, axis=0, keepdims=True)
        vec_ref[4:5, :] += jnp.where(bidx == 0, dgate, 0.0)
        vec_ref[5:6, :] += jnp.where(bidx == 1, dgate, 0.0)
        dy = (dres * gate1).astype(BF16)

        dmerged = _dot_nt(dy, wo_ref[...])
        da_out_f = dmerged * sa
        ds_out_f = dmerged * sc
        da_out = da_out_f.astype(BF16)
        ds_out = ds_out_f.astype(BF16)
        dgm_s[slot, :, 2 * D:3 * D] = (ds_out_f * s_out * (1.0 - sc)).astype(BF16)
        dgm_s[slot, :, D:2 * D] = (da_out_f * a_out * (1.0 - sa)).astype(BF16)
        da_in = _dot_nt(da_out, wao_ref[...])
        ds_in = _dot_nt(ds_out, wco_ref[...])

        d_o = da_in * silu_ga
        do_ref[...] = d_o.astype(BF16)
        dga_s[slot] = (da_in * o * (sig_ga + silu_ga * (1.0 - sig_ga))).astype(BF16)
        lane = lax.broadcasted_iota(jnp.int32, (tm, 128), 1)
        stats = lse_tot
        od = o * d_o
        for h in range(4):
            delta = jnp.sum(od[:, h * HD:(h + 1) * HD], axis=1, keepdims=True)
            stats = jnp.where(lane == 4 + h, delta, stats)
        st_ref[...] = stats

        ds_silu = ds_in * silu_gc
        dbg_s[slot] = (ds_silu * y_conv).astype(BF16)
        dyc = ds_silu * bg
        dyc_ref[...] = dyc
        vec_ref[0:1, :] += jnp.sum(dyc, axis=0, keepdims=True)
        dgm_s[slot, :, 0:D] = (ds_in * bg_yc * (sig_gc + silu_gc * (1.0 - sig_gc))).astype(BF16)

        @pl.when(i == 0)
        def _():
            acc_o[...] = jnp.zeros_like(acc_o)
            acc_co[...] = jnp.zeros_like(acc_co)
            acc_ao[...] = jnp.zeros_like(acc_ao)

        acc_o[...] += _dot_tn(merged, dy)
        acc_co[...] += _dot_tn(s_in, ds_out)
        acc_ao[...] += _dot_tn(a_in, da_out)

        for cp in column_copies(slot, pl.multiple_of(i * tm, tm)):
            cp.start()

        @pl.when(i == T // tm - 1)
        def _():
            gwo_ref[...] = acc_o[...].astype(BF16)
            gwco_ref[...] = acc_co[...].astype(BF16)
            gwao_ref[...] = acc_ao[...].astype(BF16)
            for s in range(2):
                for cp in column_copies(s, 0):
                    cp.wait()

    def tile(width, cblk=0):
        return pl.BlockSpec((tm, width), lambda i: (i, cblk))

    def whole(shape):
        return pl.BlockSpec(shape, lambda i: tuple(0 for _ in shape))

    def once(shape):
        return pl.BlockSpec(shape, lambda i: tuple(0 for _ in shape), pipeline_mode=pl.Buffered(1))

    prev_rows = lambda i: (jnp.maximum(i * (tm // halo) - 1, 0),)
    in_specs = [
        tile(D), tile(D), pl.BlockSpec((1, 3, D), lambda i: (i // per_seq, 0, 0)),
        tile(GW), tile(GW), tile(GW), tile(128), tile(128), tile(128),
        tile(GW, CB_GA), tile(D, KB_U), tile(D, KB_BG), tile(D, KB_CG), tile(D, KB_GC),
        tile(D, KB_MA), tile(D, KB_MC),
        pl.BlockSpec((halo, D), lambda i: (*prev_rows(i), KB_U)),
        pl.BlockSpec((halo, D), lambda i: (*prev_rows(i), KB_CG)),
        whole((GW, D)), whole((D, D)), whole((D, D)),
        whole((3, D)), whole((1, D)), whole((1, D)), whole((1, D)),
    ]
    out_specs = (
        pl.BlockSpec(memory_space=pl.ANY), tile(D), tile(GW), tile(128), tile(D),
        once((D, D)), once((D, D)), once((GW, D)),
        pl.BlockSpec((8, D), lambda i: (0, 0)),
    )
    out_shape = (
        jax.ShapeDtypeStruct((T, NCOL), BF16),
        jax.ShapeDtypeStruct((T, D), F32),
        jax.ShapeDtypeStruct((T, GW), BF16),
        jax.ShapeDtypeStruct((T, 128), F32),
        jax.ShapeDtypeStruct((T, D), F32),
        jax.ShapeDtypeStruct((D, D), BF16),
        jax.ShapeDtypeStruct((D, D), BF16),
        jax.ShapeDtypeStruct((GW, D), BF16),
        jax.ShapeDtypeStruct((8, D), F32),
    )
    return pl.pallas_call(
        body, name="tail",
        grid=(T // tm,),
        in_specs=in_specs, out_specs=out_specs, out_shape=out_shape,
        scratch_shapes=[pltpu.VMEM((2, tm, GW), BF16), pltpu.VMEM((2, tm, D), BF16), pltpu.VMEM((2, tm, 3 * D), BF16),
                        pltpu.SemaphoreType.DMA((2, 3)),
                        pltpu.VMEM((D, D), F32), pltpu.VMEM((D, D), F32), pltpu.VMEM((GW, D), F32)],
        compiler_params=pltpu.CompilerParams(vmem_limit_bytes=VMEM_LIMIT_TAIL),
    )(x2, tgt2, mod3, *o_g, *lse_g, proj, proj, proj, proj, proj, proj, proj, proj, proj,
      w_ao, w_co, w_o, conv_w, conv_b, ln_g, ln_b)


def _conv_bwd(dyc, proj, conv_w, dproj):
    tm = 512
    per_seq = S // tm

    def body(d_ref, dn_ref, u_ref, c_ref, cw_ref, _, dproj_ref, g_ref, du_s, dc_s, sems):
        i = pl.program_id(0)
        last = (i % per_seq) == per_seq - 1

        @pl.when(i == 0)
        def _():
            g_ref[...] = jnp.zeros_like(g_ref)

        slot = i % 2

        def column_copies(s, row0):
            return _column_copies([(du_s.at[s], D * KB_U), (dc_s.at[s], D * KB_CG)], dproj_ref, row0, sems.at[s])

        @pl.when(i >= 2)
        def _():
            for cp in column_copies(slot, 0):
                cp.wait()

        d = d_ref[...]
        dn = jnp.where(last, 0.0, dn_ref[...])
        dcat = jnp.concatenate([d, dn], axis=0)
        d1 = pltpu.roll(dcat, tm + 8 - 1, 0)[:tm]
        d2 = pltpu.roll(dcat, tm + 8 - 2, 0)[:tm]
        dz = cw_ref[2:3, :] * d + cw_ref[1:2, :] * d1 + cw_ref[0:1, :] * d2
        u = u_ref[...].astype(F32)
        cg = c_ref[...].astype(F32)
        du_s[slot] = (dz * cg).astype(BF16)
        dc_s[slot] = (dz * u).astype(BF16)
        for cp in column_copies(slot, pl.multiple_of(i * tm, tm)):
            cp.start()

        z = cg * u
        g_ref[0:1, :] += jnp.sum(d2 * z, axis=0, keepdims=True)
        g_ref[1:2, :] += jnp.sum(d1 * z, axis=0, keepdims=True)
        g_ref[2:3, :] += jnp.sum(d * z, axis=0, keepdims=True)

        @pl.when(i == T // tm - 1)
        def _():
            for s in range(2):
                for cp in column_copies(s, 0):
                    cp.wait()

    n_tiles = T // tm
    next_rows = lambda i: jnp.minimum((i + 1) * (tm // 8), T // 8 - 1)
    return pl.pallas_call(
        body, name="conv_bwd",
        grid=(n_tiles,),
        in_specs=[pl.BlockSpec((tm, D), lambda i: (i, 0)),
                  pl.BlockSpec((8, D), lambda i: (next_rows(i), 0)),
                  pl.BlockSpec((tm, D), lambda i: (i, KB_U)),
                  pl.BlockSpec((tm, D), lambda i: (i, KB_CG)),
                  pl.BlockSpec((3, D), lambda i: (0, 0)),
                  pl.BlockSpec(memory_space=pl.ANY)],
        out_specs=(pl.BlockSpec(memory_space=pl.ANY),
                   pl.BlockSpec((8, D), lambda i: (0, 0))),
        out_shape=(jax.ShapeDtypeStruct((T, NCOL), BF16),
                   jax.ShapeDtypeStruct((8, D), F32)),
        scratch_shapes=[pltpu.VMEM((2, tm, D), BF16), pltpu.VMEM((2, tm, D), BF16), pltpu.SemaphoreType.DMA((2, 2))],
        input_output_aliases={5: 0},
        compiler_params=pltpu.CompilerParams(vmem_limit_bytes=VMEM_LIMIT),
    )(dyc, dyc, proj, proj, conv_w, dproj)


def _dh_dx(dproj, w_in_all, x2, dxd, mod3, chip_sums, hops=(), parts0=None):
    tm = 512
    per_seq = S // tm
    n_pass, _, width = w_in_all.shape
    n = len(chip_sums)
    n_in = 5 + n + (0 if parts0 is None else 1)

    def body(*refs):
        d_ref, w_ref, x_ref, dxd_ref, mod_ref = refs[:5]
        ins = refs[5:5 + n]
        gx_ref, vec_ref = refs[n_in:n_in + 2]
        outs = refs[n_in + 2:n_in + 2 + n]
        acc, send_sems, recv_sems, local_sems = refs[n_in + 2 + n:]
        jj, i = pl.program_id(0), pl.program_id(1)

        @pl.when((i == 0) & (jj == 0))
        def _():
            vec_ref[...] = jnp.zeros_like(vec_ref)
            if n:
                sends, _, mine = _chip_copies(ins, outs, send_sems, recv_sems, local_sems, hops)
                for cp in sends + mine:
                    cp.start()

        if n:
            @pl.when((i == T // tm - 1) & (jj == n_pass - 1))
            def _():
                sends, arrivals, mine = _chip_copies(ins, outs, send_sems, recv_sems, local_sems, hops)
                for cp in arrivals:
                    cp.wait_recv()
                for cp in sends:
                    cp.wait_send()
                for cp in mine:
                    cp.wait()

        def partial():
            return _dot_nt(d_ref[...], w_ref[...])

        @pl.when(jj == 0)
        def _():
            acc[i] = partial()

        @pl.when((jj > 0) & (jj < n_pass - 1))
        def _():
            acc[i] += partial()

        @pl.when(jj == n_pass - 1)
        def _():
            dh = acc[i] + partial()
            bidx = i // per_seq
            gx_ref[...] = dxd_ref[...] + dh * (1.0 + mod_ref[0, 1:2, :])
            dshift = jnp.sum(dh, axis=0, keepdims=True)
            dscale = jnp.sum(dh * x_ref[...], axis=0, keepdims=True)
            vec_ref[0:1, :] += jnp.where(bidx == 0, dshift, 0.0)
            vec_ref[1:2, :] += jnp.where(bidx == 1, dshift, 0.0)
            vec_ref[2:3, :] += jnp.where(bidx == 0, dscale, 0.0)
            vec_ref[3:4, :] += jnp.where(bidx == 1, dscale, 0.0)

    def last_pass(jj, i):
        return jnp.where(jj == n_pass - 1, i, 0)

    any_spec = pl.BlockSpec(memory_space=pl.ANY)
    res = pl.pallas_call(
        body, name="dh_dx",
        grid=(n_pass, T // tm),
        in_specs=[
            pl.BlockSpec((tm, width), lambda jj, i: (i, jj)),
            pl.BlockSpec((None, D, width), lambda jj, i: (jj, 0, 0)),
            pl.BlockSpec((tm, D), lambda jj, i: (last_pass(jj, i), 0)),
            pl.BlockSpec((tm, D), lambda jj, i: (last_pass(jj, i), 0)),
            pl.BlockSpec((1, 3, D), lambda jj, i: (last_pass(jj, i) // per_seq, 0, 0))]
                 + [any_spec] * (n_in - 5),
        out_specs=(pl.BlockSpec((tm, D), lambda jj, i: (last_pass(jj, i), 0)),
                   pl.BlockSpec((8, D), lambda jj, i: (0, 0))) + (any_spec,) * n,
        out_shape=(jax.ShapeDtypeStruct((T, D), F32), jax.ShapeDtypeStruct((8, D), F32))
                  + tuple(jax.ShapeDtypeStruct(a.shape, a.dtype) for a in chip_sums),
        scratch_shapes=[pltpu.VMEM((T // tm, tm, D), F32), pltpu.SemaphoreType.DMA((max(3 * n, 1),)),
                        pltpu.SemaphoreType.DMA((max(3 * n, 1),)), pltpu.SemaphoreType.DMA((max(n, 1),))],
        input_output_aliases={} if parts0 is None else {5 + n: 2},
        compiler_params=pltpu.CompilerParams(vmem_limit_bytes=VMEM_LIMIT),
    )(dproj, w_in_all, x2, dxd, mod3, *chip_sums, *([] if parts0 is None else [parts0]))
    return res[0], res[1], res[2:]


def _adam_step(g, w, m, v):
    nm = ADAM_B1 * m + (1.0 - ADAM_B1) * g
    nv = ADAM_B2 * v + (1.0 - ADAM_B2) * (g * g)
    m_hat = nm / (1.0 - ADAM_B1 ** ADAM_STEP)
    v_hat = nv / (1.0 - ADAM_B2 ** ADAM_STEP)
    return -ADAM_LR * (m_hat / (jnp.sqrt(v_hat) + ADAM_EPS) + ADAM_WD * w), nm, nv


def _adamw(parts, w, m, v, name, row_tile=None):
    n_parts, rows, cols = parts.shape
    tr = rows if row_tile is None else row_tile

    def body(p_ref, w_ref, m_ref, v_ref, g_ref, d_ref, nm_ref, nv_ref):
        g = p_ref[0].astype(F32)
        for s in range(1, n_parts):
            g = g + p_ref[s].astype(F32)
        g_ref[...] = g
        d_ref[...], nm_ref[...], nv_ref[...] = _adam_step(g, w_ref[...], m_ref[...], v_ref[...])

    blk = pl.BlockSpec((tr, cols), lambda i: (i, 0))
    shp = jax.ShapeDtypeStruct((rows, cols), F32)
    return pl.pallas_call(
        body, name=name,
        grid=(rows // tr,),
        in_specs=[pl.BlockSpec((n_parts, tr, cols), lambda i: (0, i, 0)), blk, blk, blk],
        out_specs=(blk, blk, blk, blk),
        out_shape=(shp, shp, shp, shp),
        compiler_params=pltpu.CompilerParams(vmem_limit_bytes=VMEM_LIMIT),
    )(parts, w, m, v)


def _multi_adamw(parts_list, params, name):
    n = len(params)
    flat = [t for wmv in params for t in wmv]

    def body(*refs):
        parts, ins, outs = refs[:n], refs[n:4 * n], refs[4 * n:]
        for p in range(n):
            g = parts[p][0].astype(F32)
            for s in range(1, parts[p].shape[0]):
                g = g + parts[p][s].astype(F32)
            w_ref, m_ref, v_ref = ins[3 * p:3 * p + 3]
            g_ref, d_ref, nm_ref, nv_ref = outs[4 * p:4 * p + 4]
            g_ref[...] = g
            d_ref[...], nm_ref[...], nv_ref[...] = _adam_step(g, w_ref[...], m_ref[...], v_ref[...])

    out_shape = []
    for w, _, _ in params:
        out_shape += [jax.ShapeDtypeStruct(w.shape, F32)] * 4
    res = pl.pallas_call(body, name=name, out_shape=tuple(out_shape))(*parts_list, *flat)
    return [res[4 * p:4 * p + 4] for p in range(n)]


def _small_updates(small_g, dmod_all, rel_parts, params):
    flat = [t for wmv in params for t in wmv]

    def body(sg_ref, dm_ref, rp_ref, *refs):
        ins, outs = refs[:len(flat)], refs[len(flat):]

        def over_devices(row):
            tot = sg_ref[0, row:row + 1, :]
            for s in range(1, N_DEV):
                tot = tot + sg_ref[s, row:row + 1, :]
            return tot

        g_b_ada = dm_ref[0:1, :]
        for r in range(1, N_DEV * BL):
            g_b_ada = g_b_ada + dm_ref[r:r + 1, :]
        g_rel = rp_ref[0]
        for s in range(1, N_DEV):
            g_rel = g_rel + rp_ref[s]
        grads = [g_b_ada, over_devices(0), g_rel, over_devices(1), over_devices(2)]
        outs[0][...] = jnp.sum(over_devices(3), axis=1, keepdims=True)
        for p, g in enumerate(grads):
            w_ref, m_ref, v_ref = ins[3 * p:3 * p + 3]
            g_ref, d_ref, nm_ref, nv_ref = outs[1 + 4 * p:5 + 4 * p]
            g_ref[...] = g
            d_ref[...], nm_ref[...], nv_ref[...] = _adam_step(g, w_ref[...], m_ref[...], v_ref[...])

    out_shape = [jax.ShapeDtypeStruct((1, 1), F32)]
    for w, _, _ in params:
        out_shape += [jax.ShapeDtypeStruct(w.shape, F32)] * 4
    res = pl.pallas_call(body, name="small_updates", out_shape=tuple(out_shape))(small_g, dmod_all, rel_parts, *flat)
    return res[0], [res[1 + 4 * p:5 + 4 * p] for p in range(len(params))]


def _attn_fwd_dense(proj, bias):
    nq = 4
    rows = nq * QB
    nsb = S // rows

    def body(q_ref, k_ref, v_ref, kp_ref, vp_ref, b_ref, o_ref, l_ref, ls0, ls1, ls2, ls3):
        ls = [ls0, ls1, ls2, ls3]
        n = pl.program_id(1)
        lane = lax.broadcasted_iota(jnp.int32, (QB, 128), 1)
        units = [(h, j) for h in range(4) for j in range(nq)]

        def keys(cur_ref, prev_ref, h, j):
            sl = slice(h * HD, (h + 1) * HD)
            if j == 0:
                return jnp.concatenate([prev_ref[:, sl], cur_ref[0:QB, sl]], axis=0)
            return cur_ref[(j - 1) * QB:(j + 1) * QB, sl]

        q = jnp.stack([q_ref[j * QB:(j + 1) * QB, h * HD:(h + 1) * HD] for h, j in units])
        k = jnp.stack([keys(k_ref, kp_ref, h, j) for h, j in units])
        v = jnp.stack([keys(v_ref, vp_ref, h, j) for h, j in units])
        bias_b = jnp.stack([b_ref[jnp.minimum(n, 1), h] if j == 0 else b_ref[1, h] for h, j in units])
        s = jnp.einsum("uqd,ukd->uqk", q, k, preferred_element_type=F32) * SCALE + bias_b
        m = jnp.max(s, axis=-1, keepdims=True)
        p = jnp.exp(s - m)
        l = jnp.sum(p, axis=-1, keepdims=True)
        o = jnp.einsum("uqk,ukd->uqd", p.astype(BF16), v, preferred_element_type=F32) / l
        lse = m + jnp.log(l)
        for i, (h, j) in enumerate(units):
            o_ref[j * QB:(j + 1) * QB, h * HD:(h + 1) * HD] = o[i]
            ls[h][j * QB:(j + 1) * QB, :] = jnp.where(lane == h, lse[i], 0.0)
        l_ref[...] = (ls[0][...] + ls[1][...]) + (ls[2][...] + ls[3][...])

    def row(b, n):
        return b * nsb + n

    def prev(b, n):
        return jnp.maximum((b * nsb + n) * nq - 1, 0)

    in_specs = [
        pl.BlockSpec((rows, GW), lambda b, n: (row(b, n), CB_Q)),
        pl.BlockSpec((rows, GW), lambda b, n: (row(b, n), CB_K)),
        pl.BlockSpec((rows, GW), lambda b, n: (row(b, n), CB_V)),
        pl.BlockSpec((QB, GW), lambda b, n: (prev(b, n), CB_K)),
        pl.BlockSpec((QB, GW), lambda b, n: (prev(b, n), CB_V)),
        pl.BlockSpec((None, 2, 4, QB, 2 * QB), lambda b, n: (0, 0, 0, 0, 0)),
    ]
    return pl.pallas_call(
        body, name="attn_fwd0",
        grid=(BL, nsb),
        in_specs=in_specs,
        out_specs=(pl.BlockSpec((rows, GW), lambda b, n: (row(b, n), 0)),
                   pl.BlockSpec((rows, 128), lambda b, n: (row(b, n), 0))),
        out_shape=(jax.ShapeDtypeStruct((T, GW), F32), jax.ShapeDtypeStruct((T, 128), F32)),
        scratch_shapes=[pltpu.VMEM((rows, 128), F32)] * 4,
        compiler_params=pltpu.CompilerParams(vmem_limit_bytes=VMEM_LIMIT),
    )(proj, proj, proj, proj, proj, bias)


def _attn_bwd_dense(proj, d_out, stats, bias, dproj):
    nq = 4
    rows = nq * QB
    nsb = S // rows
    cols_q, cols_k, cols_v = CB * CB_Q, CB * CB_K, CB * CB_V

    def body(q_ref, k_ref, v_ref, do_ref, st_ref, kp_ref, vp_ref, b_ref, _, out_ref, db_ref,
             sq, sk, sv, carry, sems):
        b, n = pl.program_id(0), pl.program_id(1)
        units = [(h, j) for h in range(4) for j in range(nq)]

        @pl.when((b == 0) & (n == 0))
        def _():
            db_ref[...] = jnp.zeros_like(db_ref)

        @pl.when(n == 0)
        def _():
            carry[...] = jnp.zeros_like(carry)

        step = b * (nsb + 1) + n
        slot = step % 2

        def block_copies(s, position, block):
            part = pl.ds(position * QB, QB)
            return _column_copies([(sq.at[s, part], cols_q), (sk.at[s, part], cols_k), (sv.at[s, part], cols_v)],
                                  out_ref, pl.multiple_of(block * QB, QB), sems.at[s, position])

        def wait_blocks(s, positions):
            for position in positions:
                for cp in block_copies(s, position, 0):
                    cp.wait()

        before = (step - 2) % (nsb + 1)

        @pl.when((step >= 2) & (before > 0))
        def _():
            wait_blocks(slot, [0])

        @pl.when((step >= 2) & (before < nsb))
        def _():
            wait_blocks(slot, range(1, nq))

        def write(first_block, position, count):
            for j in range(count):
                for cp in block_copies(slot, position + j, first_block + j):
                    cp.start()

        @pl.when(n == nsb)
        def _():
            sq[slot, 0:QB, :] = carry[:, 0:GW].astype(BF16)
            sk[slot, 0:QB, :] = carry[:, GW:2 * GW].astype(BF16)
            sv[slot, 0:QB, :] = carry[:, 2 * GW:3 * GW].astype(BF16)
            write((b + 1) * nsb * nq - 1, 0, 1)

            @pl.when(b == BL - 1)
            def _():
                wait_blocks(slot, [0])
                wait_blocks(1 - slot, range(nq))

        @pl.when(n < nsb)
        def _():
            def keys(cur_ref, prev_ref, h, j):
                sl = slice(h * HD, (h + 1) * HD)
                if j == 0:
                    return jnp.concatenate([prev_ref[:, sl], cur_ref[0:QB, sl]], axis=0)
                return cur_ref[(j - 1) * QB:(j + 1) * QB, sl]

            def block(ref, h, j):
                return ref[j * QB:(j + 1) * QB, h * HD:(h + 1) * HD]

            q = jnp.stack([block(q_ref, h, j) for h, j in units])
            do = jnp.stack([block(do_ref, h, j) for h, j in units])
            k = jnp.stack([keys(k_ref, kp_ref, h, j) for h, j in units])
            v = jnp.stack([keys(v_ref, vp_ref, h, j) for h, j in units])
            bias_b = jnp.stack([b_ref[jnp.minimum(n, 1), h] if j == 0 else b_ref[1, h] for h, j in units])
            lse = jnp.stack([st_ref[j * QB:(j + 1) * QB, h:h + 1] for h, j in units])
            delta = jnp.stack([st_ref[j * QB:(j + 1) * QB, 4 + h:5 + h] for h, j in units])
            s = jnp.einsum("uqd,ukd->uqk", q, k, preferred_element_type=F32) * SCALE + bias_b
            p = jnp.exp(s - lse)
            ds = p * (jnp.einsum("uqd,ukd->uqk", do, v, preferred_element_type=F32) - delta)
            for h in range(4):
                tot = ds[h * nq]
                for j in range(1, nq):
                    tot = tot + ds[h * nq + j]
                db_ref[h] += tot
            dsb, pb = ds.astype(BF16), p.astype(BF16)
            dq = jnp.einsum("uqk,ukd->uqd", dsb, k, preferred_element_type=F32) * SCALE
            dk = jnp.einsum("uqk,uqd->ukd", dsb, q, preferred_element_type=F32) * SCALE
            dv = jnp.einsum("uqk,uqd->ukd", pb, do, preferred_element_type=F32)
            for h in range(4):
                sl = slice(h * HD, (h + 1) * HD)
                u0, last = h * nq, h * nq + nq - 1
                sq[slot, 0:QB, sl] = carry[:, sl].astype(BF16)
                sk[slot, 0:QB, sl] = (carry[:, GW + h * HD:GW + (h + 1) * HD] + dk[u0, :QB]).astype(BF16)
                sv[slot, 0:QB, sl] = (carry[:, 2 * GW + h * HD:2 * GW + (h + 1) * HD] + dv[u0, :QB]).astype(BF16)
                for j in range(nq - 1):
                    pos = slice((j + 1) * QB, (j + 2) * QB)
                    sq[slot, pos, sl] = dq[u0 + j].astype(BF16)
                    sk[slot, pos, sl] = (dk[u0 + j, QB:] + dk[u0 + j + 1, :QB]).astype(BF16)
                    sv[slot, pos, sl] = (dv[u0 + j, QB:] + dv[u0 + j + 1, :QB]).astype(BF16)
                carry[:, sl] = dq[last]
                carry[:, GW + h * HD:GW + (h + 1) * HD] = dk[last, QB:]
                carry[:, 2 * GW + h * HD:2 * GW + (h + 1) * HD] = dv[last, QB:]

            @pl.when(n == 0)
            def _():
                write(b * nsb * nq, 1, nq - 1)

            @pl.when(n > 0)
            def _():
                write((b * nsb + n) * nq - 1, 0, nq)

    def row(b, n):
        return b * nsb + jnp.minimum(n, nsb - 1)

    def prev(b, n):
        return jnp.maximum(row(b, n) * nq - 1, 0)

    in_specs = [
        pl.BlockSpec((rows, GW), lambda b, n: (row(b, n), CB_Q)),
        pl.BlockSpec((rows, GW), lambda b, n: (row(b, n), CB_K)),
        pl.BlockSpec((rows, GW), lambda b, n: (row(b, n), CB_V)),
        pl.BlockSpec((rows, GW), lambda b, n: (row(b, n), 0)),
        pl.BlockSpec((rows, 128), lambda b, n: (row(b, n), 0)),
        pl.BlockSpec((QB, GW), lambda b, n: (prev(b, n), CB_K)),
        pl.BlockSpec((QB, GW), lambda b, n: (prev(b, n), CB_V)),
        pl.BlockSpec((None, 2, 4, QB, 2 * QB), lambda b, n: (0, 0, 0, 0, 0)),
        pl.BlockSpec(memory_space=pl.ANY),
    ]
    return pl.pallas_call(
        body, name="attn_bwd0",
        grid=(BL, nsb + 1),
        in_specs=in_specs,
        out_specs=(pl.BlockSpec(memory_space=pl.ANY),
                   pl.BlockSpec((4, QB, 2 * QB), lambda b, n: (0, 0, 0))),
        out_shape=(jax.ShapeDtypeStruct((T, NCOL), BF16),
                   jax.ShapeDtypeStruct((4, QB, 2 * QB), F32)),
        scratch_shapes=[pltpu.VMEM((2, rows, GW), BF16)] * 3
                       + [pltpu.VMEM((QB, 3 * GW), F32), pltpu.SemaphoreType.DMA((2, nq, 3))],
        input_output_aliases={8: 0},
        compiler_params=pltpu.CompilerParams(vmem_limit_bytes=VMEM_LIMIT),
    )(proj, proj, proj, d_out, stats, proj, proj, bias, dproj)


def _attention_forward(proj, rel_bias):
    expand_lanes, grad_lanes, masks = (jnp.asarray(t) for t in _bucket_maps())
    bias = _bias_expand(rel_bias, expand_lanes, masks)
    fwd = [_attn_fwd_dense(proj, bias)] + [_attn_fwd(proj, bias, g) for g in (1, 2)]
    return bias, grad_lanes, [f[0] for f in fwd], [f[1] for f in fwd]


def _local_step(x2, tgt2, mod3, h, proj, attn, w_ao, w_co, w_o, conv_w, conv_b, ln_g, ln_b):
    bias, buckets, o_g, lse_g = attn

    (dproj, dyc, d_o, stats, dxd, gw_o, gw_co, gw_ao, tail_vec) = _tail(
        x2, tgt2, mod3, o_g, lse_g, proj, w_ao, w_co, w_o, conv_w, conv_b, ln_g, ln_b)

    dproj, db = _attn_bwd_dense(proj, d_o, stats, bias, dproj)
    dbias = [db]
    for g in (1, 2):
        dproj, db = _attn_bwd(proj, d_o, stats, bias, dproj, g)
        dbias.append(db)
    g_rel_bias = _bias_grad(*dbias, buckets)
    dproj, conv_vec = _conv_bwd(dyc, proj, conv_w, dproj)

    gw_ao = jnp.transpose(gw_ao.reshape(GW, N_DEV, D // N_DEV), (1, 0, 2))
    return dproj, dxd, gw_ao, gw_co, gw_o, conv_vec, g_rel_bias, tail_vec


def kernel(x, c, w_ada, b_ada, w_in, conv_w, conv_b, rel_bias, w_attn_out, w_conv_out, w_o, ln_g, ln_b, loss_target, m_w_ada, m_b_ada, m_w_in, m_conv_w, m_conv_b, m_rel_bias, m_w_attn_out, m_w_conv_out, m_w_o, m_ln_g, m_ln_b, v_w_ada, v_b_ada, v_w_in, v_conv_w, v_conv_b, v_rel_bias, v_w_attn_out, v_w_conv_out, v_w_o, v_ln_g, v_ln_b):
    me = _my_index()
    x2 = x.reshape(T, D)
    tgt2 = loss_target.reshape(T, D)

    b_cols = lax.dynamic_slice(b_ada, (0, me * ADA_SHARD), (1, ADA_SHARD))
    c_g, mod_in = _mod_exchange(jnp.pad(c, ((0, 8 - BL), (0, 0))), w_ada[0], b_cols)
    c_all = c_g[:, 0:BL, :].reshape(N_DEV * BL, D)
    mod3 = jnp.transpose(mod_in[:, 0:BL, :], (1, 0, 2)).reshape(BL, 3, D)

    h = _prep_h(x2, mod3)
    rows_shape = jax.ShapeDtypeStruct((N_DEV, D // N_DEV, D), BF16)
    proj, w_in_all, (w_ao_g, w_co_g, w_o_g, conv_w_g) = _gather_proj(
        _shard_order(), h, w_in[0].astype(BF16), 1024,
        ([w_attn_out[0].astype(BF16), w_conv_out[0].astype(BF16), w_o[0].astype(BF16), conv_w[0]],
         [jax.ShapeDtypeStruct((N_DEV, GW, D // N_DEV), BF16), rows_shape, rows_shape,
          jax.ShapeDtypeStruct((N_DEV, 3, D // N_DEV), F32)]))

    attn = _attention_forward(proj, rel_bias)
    w_ao_full = jnp.transpose(w_ao_g, (1, 0, 2)).reshape(GW, D)
    w_co_full = w_co_g.reshape(D, D)
    w_o_full = w_o_g.reshape(D, D)
    conv_w_full = jnp.transpose(conv_w_g, (1, 0, 2)).reshape(3, D)

    (dproj, dxd, gw_ao, gw_co, gw_o, conv_vec, g_rel_bias, tail_vec) = _local_step(
        x2, tgt2, mod3, h, proj, attn, w_ao_full, w_co_full, w_o_full,
        conv_w_full, conv_b, ln_g, ln_b)

    g_conv_w_blocks = jnp.transpose(conv_vec[0:3].reshape(3, N_DEV, D // N_DEV), (1, 0, 2))
    partials = [gw_ao, gw_co.reshape(N_DEV, D // N_DEV, D), gw_o.reshape(N_DEV, D // N_DEV, D), g_conv_w_blocks]
    w_in_sums, w_in_parts, sib = _gw_in_pair(
        _slice_order(), h, dproj, partials,
        [jax.ShapeDtypeStruct((4, GW, D // N_DEV), BF16),
         jax.ShapeDtypeStruct((4, D // N_DEV, D), BF16),
         jax.ShapeDtypeStruct((4, D // N_DEV, D), BF16),
         jax.ShapeDtypeStruct((4, 3, D // N_DEV), F32)])
    core = lax.axis_index("c").astype(jnp.int32).reshape(1)
    chip_sums = [w_in_sums] + list(_pair_add(core, partials, sib))
    hops = [(3,)] + [(1, 2, 3)] * 4
    grad_x, mod_vec, (r_in, r_ao, r_co, r_o, r_cw) = _dh_dx(
        dproj, w_in_all, x2, dxd, mod3, chip_sums, hops, w_in_parts)

    small = jnp.concatenate([
        tail_vec[0:4],
        jnp.pad(g_rel_bias.reshape(1, N_BUCKETS * N_HEADS), ((0, 0), (0, D - N_BUCKETS * N_HEADS))),
        jnp.zeros((3, D), F32)], axis=0)
    dmod = jnp.concatenate([mod_vec[0:2], mod_vec[2:4], tail_vec[4:6]], axis=1)
    small_g, dmod_g = _all_gather(
        [small, dmod],
        [jax.ShapeDtypeStruct((N_DEV, 8, D), F32), jax.ShapeDtypeStruct((N_DEV, BL, 3 * D), F32)],
        "gather_small")
    dmod_all = dmod_g.reshape(N_DEV * BL, 3 * D)
    small_names = ["b_ada", "conv_b", "rel_bias", "ln_g", "ln_b"]
    small_params = [(b_ada, m_b_ada, v_b_ada), (conv_b, m_conv_b, v_conv_b), (rel_bias, m_rel_bias, v_rel_bias),
                    (ln_g, m_ln_g, v_ln_g), (ln_b, m_ln_b, v_ln_b)]
    loss, small_res = _small_updates(
        small_g, dmod_all, small_g[:, 4, :N_BUCKETS * N_HEADS].reshape(N_DEV, N_BUCKETS, N_HEADS), small_params)
    loss = loss.reshape(())

    dmod_cols = lax.dynamic_slice(dmod_all, (0, me * ADA_SHARD), (N_DEV * BL, ADA_SHARD))
    res = {
        "w_ada": tuple(t[None] for t in _w_ada_update(jnp.transpose(c_all), dmod_cols,
                                                      w_ada[0], m_w_ada[0], v_w_ada[0])),
        "w_in": tuple(t[None] for t in _adamw(r_in, w_in[0], m_w_in[0], v_w_in[0], "adam_w_in", 128)),
    }
    mid_names = ["conv_w", "w_attn_out", "w_conv_out", "w_o"]
    mid_parts = [r_cw, r_ao, r_co, r_o]
    mid_full = [(conv_w, m_conv_w, v_conv_w), (w_attn_out, m_w_attn_out, v_w_attn_out),
                (w_conv_out, m_w_conv_out, v_w_conv_out), (w_o, m_w_o, v_w_o)]
    mid_res = _multi_adamw(mid_parts, [tuple(t[0] for t in wmv) for wmv in mid_full], "adam_mid")
    for nm, wmv, outs4 in zip(mid_names, mid_full, mid_res):
        res[nm] = tuple(t[None] for t in outs4)
    res.update(dict(zip(small_names, small_res)))
    order = ["w_ada", "b_ada", "w_in", "conv_w", "conv_b", "rel_bias", "w_attn_out", "w_conv_out",
             "w_o", "ln_g", "ln_b"]
    outs = [loss, grad_x.reshape(BL, S, D)]
    for k in range(4):
        outs += [res[name][k] for name in order]
    return tuple(outs)
```

```python
import math

import numpy as np
import jax
import jax.numpy as jnp
from jax import lax
from jax.experimental import pallas as pl
from jax.experimental.pallas import tpu as pltpu

F32 = jnp.float32
BF16 = jnp.bfloat16
MESH = pl.DeviceIdType.MESH

N_DEV = 8
D = 1024
S = 2048
BL = 2
T = BL * S
NCOL = 11264
SHARD = NCOL // N_DEV
CB = 512
NCB = NCOL // CB
HD = 128
GW = 512
QB = 128
DILATIONS = (1, 4, 16)
N_STEPS = 128
N_BUCKETS = 32
N_HEADS = 12
ALPHA = 2.0 ** 0.25
LN_EPS = 1e-5
NEG_INF = -1e30
SCALE = HD ** -0.5
ADA_SHARD = 3 * D // N_DEV

CB_Q, CB_K, CB_V, CB_GA = 0, 3, 6, 9
KB_U, KB_BG, KB_CG, KB_GC, KB_MA, KB_MC = 5, 6, 7, 8, 9, 10

ADAM_LR, ADAM_B1, ADAM_B2, ADAM_EPS, ADAM_WD, ADAM_STEP = 0.001, 0.9, 0.999, 1e-08, 0.01, 10

VMEM_LIMIT = 56 * 1024 * 1024
VMEM_LIMIT_TAIL = 62 * 1024 * 1024


def _dot(a, b):
    return jnp.dot(a, b, preferred_element_type=F32)


def _dot_nt(a, b):
    return lax.dot_general(a, b, (((1,), (1,)), ((), ())), preferred_element_type=F32)


def _dot_tn(a, b):
    return lax.dot_general(a, b, (((0,), (0,)), ((), ())), preferred_element_type=F32)


def _sigmoid(v):
    return 1.0 / (1.0 + jnp.exp(-v))


def _column_copies(pieces, dst_hbm, row0, sems):
    copies = []
    for k, (src, col0) in enumerate(pieces):
        rows, width = src.shape
        copies.append(pltpu.make_async_copy(
            src, dst_hbm.at[pl.ds(row0, rows), pl.ds(col0, width)], sems.at[k]))
    return copies


def _my_index():
    return 4 * lax.axis_index("x") + 2 * lax.axis_index("y") + lax.axis_index("c")


class _Gather:
    def __init__(self, ins, outs, stage, send_sems, recv_sems, local_sems):
        self.ins, self.outs, self.stage = ins, outs, stage
        self.send_sems, self.recv_sems, self.local_sems = send_sems, recv_sems, local_sems
        x, y, c = lax.axis_index("x"), lax.axis_index("y"), lax.axis_index("c")
        self.c = c
        self.me, self.sibling = (x, y, c), (x, y, 1 - c)
        self.chips = [(1 - x, y), (x, 1 - y), (1 - x, 1 - y)]

    @staticmethod
    def scratch(arrs):
        n = len(arrs)
        return ([pltpu.SemaphoreType.DMA((7 * n,)), pltpu.SemaphoreType.DMA((7 * n,)),
                 pltpu.SemaphoreType.DMA((n,))] + [pltpu.VMEM(a.shape, a.dtype) for a in arrs])

    def _copy(self, a, k, block, to, src=None):
        dst = self.outs[a].at[4 * block[0] + 2 * block[1] + block[2]]
        return pltpu.make_async_remote_copy(
            src_ref=dst if src is None else src, dst_ref=dst,
            send_sem=self.send_sems.at[a * 7 + k], recv_sem=self.recv_sems.at[a * 7 + k],
            device_id=to, device_id_type=MESH)

    def _first(self):
        first = []
        for a in range(len(self.ins)):
            first.append(self._copy(a, 0, self.me, self.sibling, src=self.ins[a]))
            first += [self._copy(a, 1 + j, self.me, (*chip, self.c), src=self.ins[a])
                      for j, chip in enumerate(self.chips)]
        return first

    def _mine(self):
        me = self.me
        return [pltpu.make_async_copy(self.stage[a], self.outs[a].at[4 * me[0] + 2 * me[1] + me[2]],
                                      self.local_sems.at[a]) for a in range(len(self.ins))]

    def begin(self):
        for cp in self._first():
            cp.start()
        loads = [pltpu.make_async_copy(self.ins[a], self.stage[a], self.local_sems.at[a])
                 for a in range(len(self.ins))]
        for cp in loads:
            cp.start()
        for cp in loads:
            cp.wait()
        for cp in self._mine():
            cp.start()

    def finish(self):
        n, c, me, sibling = len(self.ins), self.c, self.me, self.sibling
        passed = []
        for j, chip in enumerate(self.chips):
            for a in range(n):
                self._copy(a, 1 + j, (*chip, c), me).wait_recv()
                fwd = self._copy(a, 4 + j, (*chip, c), sibling)
                fwd.start()
                passed.append(fwd)
        for a in range(n):
            self._copy(a, 0, sibling, me).wait_recv()
        for j, chip in enumerate(self.chips):
            for a in range(n):
                self._copy(a, 4 + j, (*chip, 1 - c), me).wait_recv()
        for cp in self._first() + passed:
            cp.wait_send()
        for cp in self._mine():
            cp.wait()


def _all_gather(arrs, out_shapes, name):
    n = len(arrs)

    def body(*refs):
        g = _Gather(refs[:n], refs[n:2 * n], refs[2 * n + 3:], *refs[2 * n:2 * n + 3])
        g.begin()
        g.finish()

    any_spec = pl.BlockSpec(memory_space=pl.ANY)
    return pl.pallas_call(
        body, name=name,
        out_shape=tuple(out_shapes),
        in_specs=[any_spec] * n,
        out_specs=tuple([any_spec] * n),
        scratch_shapes=_Gather.scratch(arrs),
    )(*arrs)


def _neighbour_chips():
    x, y, c = lax.axis_index("x"), lax.axis_index("y"), lax.axis_index("c")
    first = (jnp.where(c == 0, 1 - x, x), jnp.where(c == 0, y, 1 - y))
    second = (jnp.where(c == 0, x, 1 - x), jnp.where(c == 0, 1 - y, y))
    return first, second, (1 - x, 1 - y)


def _slice_order():
    x, y, c = lax.axis_index("x"), lax.axis_index("y"), lax.axis_index("c")
    nb1, nb2, diag = _neighbour_chips()
    slots = []
    for mine, theirs in ((nb1, nb2), (nb2, nb1), (diag, diag), ((x, y), (x, y))):
        slots += [2 * (2 * theirs[0] + theirs[1]) + 1 - c, 2 * (2 * mine[0] + mine[1]) + c]
    return jnp.stack(slots).astype(jnp.int32)


def _gw_in_pair(order, h, dproj, smalls, small_shapes4):
    kk, m = h.shape
    tk = min(kk, 2048)
    nk = kk // tk
    ncols = dproj.shape[1] // N_DEV
    n = len(smalls)

    def body(order_ref, h_ref, d_ref, *rest):
        ins = rest[:n]
        sums_hbm, parts_hbm = rest[n], rest[n + 1]
        sib = rest[n + 2:2 * n + 2]
        (acc, sendbuf, recvbuf, sumbuf, send_sems, recv_sems, local_sem, ssend, srecv,
         isend, irecv) = rest[2 * n + 2:]
        js, k = pl.program_id(0), pl.program_id(1)
        x, y, c = lax.axis_index("x"), lax.axis_index("y"), lax.axis_index("c")
        sibling = (x, y, 1 - c)
        my_chip = 2 * x + y
        nb1, nb2, _ = _neighbour_chips()
        near = [(*nb1, c), (*nb2, c)]

        def ici_copy(p, out_chip):
            peer = near[p] if isinstance(p, int) else tuple(jnp.where(p == 0, a, b) for a, b in zip(*near))
            return pltpu.make_async_remote_copy(
                src_ref=sumbuf.at[p], dst_ref=parts_hbm.at[out_chip],
                send_sem=isend.at[p], recv_sem=irecv.at[p], device_id=peer, device_id_type=MESH)

        def small_copies():
            return [pltpu.make_async_remote_copy(
                        src_ref=ins[a].at[2 * q + 1 - c], dst_ref=sib[a].at[q],
                        send_sem=ssend.at[a * 4 + q], recv_sem=srecv.at[a * 4 + q],
                        device_id=sibling, device_id_type=MESH)
                    for a in range(n) for q in range(4)]

        def slice_copy(p):
            return pltpu.make_async_remote_copy(
                src_ref=sendbuf, dst_ref=recvbuf.at[p], send_sem=send_sems.at[p], recv_sem=recv_sems.at[p],
                device_id=sibling, device_id_type=MESH)

        def sum_copy(p):
            return pltpu.make_async_copy(sumbuf.at[2], sums_hbm.at[order_ref[2 * p] // 2], local_sem)

        @pl.when((js == 0) & (k == 0))
        def _():
            for cp in small_copies():
                cp.start()

        def partial():
            return _dot_tn(h_ref[...], d_ref[...])

        if nk > 1:
            @pl.when(k == 0)
            def _():
                acc[...] = partial()
        if nk > 2:
            @pl.when((k > 0) & (k < nk - 1))
            def _():
                acc[...] += partial()

        def total():
            return partial() + acc[...] if nk > 1 else partial()

        p = js // 2

        @pl.when((js % 2 == 0) & (k == nk - 1))
        def _():
            @pl.when(p > 0)
            def _():
                slice_copy(p - 1).wait_send()
            sendbuf[...] = total().astype(BF16)
            slice_copy(p).start()

        @pl.when((js % 2 == 1) & (k == nk - 1))
        def _():
            slice_copy(p).wait_recv()

            @pl.when(p == 3)
            def _():
                sum_copy(2).wait()
            sumbuf[jnp.minimum(p, 2)] = (total() + recvbuf[p].astype(F32)).astype(BF16)

            @pl.when(p < 2)
            def _():
                ici_copy(p, my_chip).start()

            @pl.when(p >= 2)
            def _():
                sum_copy(p).start()

        @pl.when((js == N_DEV - 1) & (k == nk - 1))
        def _():
            slice_copy(3).wait_send()
            sum_copy(3).wait()
            for cp in small_copies():
                cp.wait()
            for p in range(2):
                ici_copy(p, 2 * near[p][0] + near[p][1]).wait_recv()
                ici_copy(p, my_chip).wait_send()

    any_spec = pl.BlockSpec(memory_space=pl.ANY)
    res = pl.pallas_call(
        body, name="gw_in_pair",
        grid_spec=pltpu.PrefetchScalarGridSpec(
            num_scalar_prefetch=1,
            grid=(N_DEV, nk),
            in_specs=[pl.BlockSpec((tk, m), lambda js, k, order_ref: (k, 0)),
                      pl.BlockSpec((tk, ncols), lambda js, k, order_ref: (k, order_ref[js]))] + [any_spec] * n,
            out_specs=(any_spec,) * (n + 2),
            scratch_shapes=[pltpu.VMEM((m, ncols), F32), pltpu.VMEM((m, ncols), BF16),
                            pltpu.VMEM((4, m, ncols), BF16), pltpu.VMEM((3, m, ncols), BF16),
                            pltpu.SemaphoreType.DMA((4,)), pltpu.SemaphoreType.DMA((4,)),
                            pltpu.SemaphoreType.DMA,
                            pltpu.SemaphoreType.DMA((4 * n,)), pltpu.SemaphoreType.DMA((4 * n,)),
                            pltpu.SemaphoreType.DMA((2,)), pltpu.SemaphoreType.DMA((2,))]),
        out_shape=(jax.ShapeDtypeStruct((4, m, ncols), BF16),) * 2 + tuple(small_shapes4),
        compiler_params=pltpu.CompilerParams(vmem_limit_bytes=VMEM_LIMIT),
    )(order, h, dproj, *smalls)
    return res[0], res[1], res[2:]


def _chip_copies(ins, outs, send_sems, recv_sems, local_sems, hops):
    n = len(ins)
    x, y, c = lax.axis_index("x"), lax.axis_index("y"), lax.axis_index("c")
    my_chip = 2 * x + y

    def peer_of(k):
        return ((1 - x) if (k >> 1) & 1 else x, (1 - y) if k & 1 else y, c)

    def copy(a, k, out_chip):
        peer = peer_of(k)
        return pltpu.make_async_remote_copy(
            src_ref=ins[a].at[2 * peer[0] + peer[1]], dst_ref=outs[a].at[out_chip],
            send_sem=send_sems.at[a * 3 + k - 1], recv_sem=recv_sems.at[a * 3 + k - 1],
            device_id=peer, device_id_type=MESH)

    sends = [copy(a, k, my_chip) for k in range(1, 4) for a in range(n) if k in hops[a]]
    arrivals = []
    for k in range(1, 4):
        peer = peer_of(k)
        arrivals += [copy(a, k, 2 * peer[0] + peer[1]) for a in range(n) if k in hops[a]]
    mine = [pltpu.make_async_copy(ins[a].at[my_chip], outs[a].at[my_chip], local_sems.at[a])
            for a in range(n)]
    return sends, arrivals, mine


def _pair_add(core, mines, theirs):
    n = len(mines)

    def body(core_ref, *refs):
        mine, sib, outs = refs[:n], refs[n:2 * n], refs[2 * n:]
        for a in range(n):
            for q in range(4):
                outs[a][q] = (mine[a][2 * q + core_ref[0]].astype(F32)
                              + sib[a][q].astype(F32)).astype(outs[a].dtype)

    return pl.pallas_call(
        body, name="pair_add",
        in_specs=[pl.BlockSpec(memory_space=pltpu.SMEM)] + [pl.BlockSpec(memory_space=pltpu.VMEM)] * (2 * n),
        out_shape=tuple(jax.ShapeDtypeStruct(t.shape, t.dtype) for t in theirs),
    )(core, *mines, *theirs)


def _mod_exchange(c8, w_ada, b_cols):
    cols = w_ada.shape[1]

    def body(c_ref, w_ref, b_ref, call_ref, mod_ref, msend, send1, recv1, send2, recv2):
        x, y, c = lax.axis_index("x"), lax.axis_index("y"), lax.axis_index("c")
        my_slot = 4 * x + 2 * y + c

        def peer_of(k):
            return ((1 - x) if (k >> 2) & 1 else x, (1 - y) if (k >> 1) & 1 else y, (1 - c) if k & 1 else c)

        def slot_of(dev):
            return 4 * dev[0] + 2 * dev[1] + dev[2]

        def exchange(src_of, dst_ref, send_sems, recv_sems):
            sends, arrivals = [], []
            for k in range(1, 8):
                peer = peer_of(k)
                sends.append(pltpu.make_async_remote_copy(
                    src_ref=src_of(slot_of(peer)), dst_ref=dst_ref.at[my_slot],
                    send_sem=send_sems.at[k - 1], recv_sem=recv_sems.at[k - 1],
                    device_id=peer, device_id_type=MESH))
                arrivals.append(pltpu.make_async_remote_copy(
                    src_ref=src_of(my_slot), dst_ref=dst_ref.at[slot_of(peer)],
                    send_sem=send_sems.at[k - 1], recv_sem=recv_sems.at[k - 1],
                    device_id=peer, device_id_type=MESH))
            for cp in sends:
                cp.start()
            for cp in arrivals:
                cp.wait_recv()
            for cp in sends:
                cp.wait_send()

        call_ref[my_slot] = c_ref[...]
        exchange(lambda s: c_ref, call_ref, send1, recv1)
        cv = call_ref[...].reshape(N_DEV * 8, c_ref.shape[1])
        act = cv * _sigmoid(cv)
        mod = jnp.dot(act, w_ref[...], preferred_element_type=F32,
                      precision=lax.Precision.HIGHEST) + b_ref[...]
        msend[...] = mod.reshape(N_DEV, 8, cols)
        mod_ref[my_slot] = msend[my_slot]
        exchange(lambda s: msend.at[s], mod_ref, send2, recv2)

    return pl.pallas_call(
        body, name="mod_exchange",
        out_shape=(jax.ShapeDtypeStruct((N_DEV, 8, c8.shape[1]), F32),
                   jax.ShapeDtypeStruct((N_DEV, 8, cols), F32)),
        scratch_shapes=[pltpu.VMEM((N_DEV, 8, cols), F32)] + [pltpu.SemaphoreType.DMA((7,))] * 4,
    )(c8, w_ada, b_cols)


def _w_ada_update(c_all_t, dmod_cols, w, m, v):
    rows, cols = w.shape
    tr = 256

    def body(c_ref, d_ref, w_ref, m_ref, v_ref, g_ref, dl_ref, nm_ref, nv_ref):
        cv = c_ref[...]
        g = jnp.dot(cv * _sigmoid(cv), d_ref[...], preferred_element_type=F32,
                    precision=lax.Precision.HIGHEST)
        g_ref[...] = g
        dl_ref[...], nm_ref[...], nv_ref[...] = _adam_step(g, w_ref[...], m_ref[...], v_ref[...])

    blk = pl.BlockSpec((tr, cols), lambda i: (i, 0))
    shp = jax.ShapeDtypeStruct((rows, cols), F32)
    return pl.pallas_call(
        body, name="adam_w_ada",
        grid=(rows // tr,),
        in_specs=[pl.BlockSpec((tr, c_all_t.shape[1]), lambda i: (i, 0)),
                  pl.BlockSpec(dmod_cols.shape, lambda i: (0, 0)), blk, blk, blk],
        out_specs=(blk, blk, blk, blk),
        out_shape=(shp, shp, shp, shp),
    )(c_all_t, dmod_cols, w, m, v)


def _shard_order():
    x, y, c = lax.axis_index("x"), lax.axis_index("y"), lax.axis_index("c")
    first, second, diag = _neighbour_chips()
    devs = [(x, y, c), (x, y, 1 - c), (*first, c), (*second, 1 - c), (*second, c), (*first, 1 - c),
            (*diag, c), (*diag, 1 - c)]
    return jnp.stack([4 * d[0] + 2 * d[1] + d[2] for d in devs]).astype(jnp.int32)


def _prep_h(x2, mod3):
    ts = 512
    per_seq = S // ts

    def body(x_ref, mod_ref, h_ref):
        shift = mod_ref[0, 0:1, :]
        scale = mod_ref[0, 1:2, :]
        h_ref[...] = (x_ref[...] * (1.0 + scale) + shift).astype(BF16)

    return pl.pallas_call(
        body, name="prep_h",
        grid=(T // ts,),
        in_specs=[pl.BlockSpec((ts, D), lambda i: (i, 0)),
                  pl.BlockSpec((1, 3, D), lambda i: (i // per_seq, 0, 0))],
        out_specs=pl.BlockSpec((ts, D), lambda i: (i, 0)),
        out_shape=jax.ShapeDtypeStruct((T, D), BF16),
    )(x2, mod3)


def _gather_proj(order, h, w_shard, tm, ride=()):
    rows, kdim = h.shape
    ncols = w_shard.shape[1]
    n_i = rows // tm
    ride_arrs, ride_shapes = ride if ride else ((), ())
    n_ride = len(ride_arrs)

    def body(order_ref, h_ref, mine_hbm, *rest):
        ride_ins = rest[:n_ride]
        o_ref, all_hbm = rest[n_ride:n_ride + 2]
        ride_outs = rest[n_ride + 2:2 * n_ride + 2]
        wv, send_sems, recv_sems, local_sems = rest[2 * n_ride + 2:2 * n_ride + 6]
        ride_scr = rest[2 * n_ride + 6:]
        j, i = pl.program_id(0), pl.program_id(1)
        c = lax.axis_index("c")
        me, sibling = (lax.axis_index("x"), lax.axis_index("y"), c), (lax.axis_index("x"), lax.axis_index("y"), 1 - c)
        nb1, nb2, diag = _neighbour_chips()

        def slot(dev):
            return 4 * dev[0] + 2 * dev[1] + dev[2]

        def copy(k, block, to, src=None, part=None):
            buf = wv.at[slot(block)]
            if part is not None:
                buf = buf.at[pl.ds(pl.multiple_of(part * (kdim // 2), kdim // 2), kdim // 2)]
            return pltpu.make_async_remote_copy(
                src_ref=buf if src is None else src, dst_ref=buf,
                send_sem=send_sems.at[k], recv_sem=recv_sems.at[k],
                device_id=to, device_id_type=MESH)

        def keep(step, block):
            s = slot(block)
            cols = pl.ds(pl.multiple_of((s % 2) * ncols, 128), ncols)
            return pltpu.make_async_copy(wv.at[s], all_hbm.at[s // 2, :, cols], local_sems.at[step])

        if n_ride:
            gather = _Gather(ride_ins, ride_outs, ride_scr[3:], *ride_scr[:3])
        to_sibling, to_nb1, to_nb2 = (copy(0, me, sibling, mine_hbm), copy(1, me, (*nb1, c), mine_hbm),
                                      copy(2, me, (*nb2, c), mine_hbm))
        relay1, relay2 = copy(3, (*nb2, c), (*nb1, c), part=c), copy(4, (*nb1, c), (*nb2, c), part=1 - c)
        pass_nb1, pass_nb2 = copy(5, (*nb1, c), sibling), copy(6, (*nb2, c), sibling)
        pass_d1, pass_d2 = copy(7, (*diag, c), sibling, part=c), copy(8, (*diag, c), sibling, part=1 - c)
        sends = [to_sibling, to_nb1, to_nb2, relay1, relay2, pass_nb1, pass_nb2, pass_d1, pass_d2]
        due = [
            (me, [], []),
            (sibling, [copy(0, sibling, me)], [[]]),
            ((*nb1, c), [copy(1, (*nb1, c), me)], [[pass_nb1, to_nb2]]),
            ((*nb2, 1 - c), [copy(5, (*nb2, 1 - c), me)], [[]]),
            ((*nb2, c), [copy(2, (*nb2, c), me)], [[pass_nb2, relay1, relay2]]),
            ((*nb1, 1 - c), [copy(6, (*nb1, 1 - c), me)], [[]]),
            ((*diag, c), [copy(3, (*diag, c), me, part=c), copy(4, (*diag, c), me, part=1 - c)],
             [[pass_d1], [pass_d2]]),
            ((*diag, 1 - c), [copy(7, (*diag, 1 - c), me, part=1 - c), copy(8, (*diag, 1 - c), me, part=c)],
             [[], []]),
        ]

        @pl.when((j == 0) & (i == 0))
        def _():
            to_sibling.start()
            to_nb1.start()
            load = pltpu.make_async_copy(mine_hbm, wv.at[slot(me)], local_sems.at[N_DEV])
            load.start()
            load.wait()
            keep(0, me).start()

        for step in range(1, N_DEV):
            block, arrivals, then = due[step]

            @pl.when((j == step) & (i == 0))
            def _():
                for arrival, follow in zip(arrivals, then):
                    arrival.wait_recv()
                    for cp in follow:
                        cp.start()
                keep(step, block).start()
                if n_ride and step == N_DEV - 2:
                    gather.begin()

        o_ref[...] = _dot(h_ref[...], wv[order_ref[j]]).astype(BF16)

        @pl.when((j == N_DEV - 1) & (i == n_i - 1))
        def _():
            for cp in sends:
                cp.wait_send()
            for step in range(N_DEV):
                keep(step, due[step][0]).wait()
            if n_ride:
                gather.finish()

    any_spec = pl.BlockSpec(memory_space=pl.ANY)
    res = pl.pallas_call(
        body, name="gather_proj",
        grid_spec=pltpu.PrefetchScalarGridSpec(
            num_scalar_prefetch=1,
            grid=(N_DEV, n_i),
            in_specs=[pl.BlockSpec((tm, kdim), lambda j, i, order_ref: (i, 0)), any_spec] + [any_spec] * n_ride,
            out_specs=(pl.BlockSpec((tm, ncols), lambda j, i, order_ref: (i, order_ref[j])), any_spec)
                      + (any_spec,) * n_ride,
            scratch_shapes=[pltpu.VMEM((N_DEV, kdim, ncols), BF16),
                            pltpu.SemaphoreType.DMA((9,)), pltpu.SemaphoreType.DMA((9,)),
                            pltpu.SemaphoreType.DMA((N_DEV + 1,))]
                           + (_Gather.scratch(ride_arrs) if n_ride else [])),
        out_shape=(jax.ShapeDtypeStruct((rows, N_DEV * ncols), BF16),
                   jax.ShapeDtypeStruct((N_DEV // 2, kdim, 2 * ncols), BF16)) + tuple(ride_shapes),
        compiler_params=pltpu.CompilerParams(vmem_limit_bytes=VMEM_LIMIT),
    )(order, h, w_shard, *ride_arrs)
    return res[0], res[1], res[2:]


SKEW_W = 512


def _bucket_maps():
    lanes = np.arange(SKEW_W)

    def buckets_of(steps):
        rows = []
        for dil in DILATIONS:
            dist = np.maximum(steps, 0) * dil
            nf = np.maximum(dist, 1).astype(np.float32)
            large = 16 + (np.log(nf / np.float32(16)) / np.float32(math.log(128.0))
                          * np.float32(16)).astype(np.int32)
            large = np.minimum(large, N_BUCKETS - 1)
            bucket = np.where(dist < 16, dist, large)
            rows.append(np.where((steps >= 0) & (steps <= N_STEPS), bucket, -1).astype(np.int32))
        return np.stack(rows)[:, None, :]

    a = np.arange(QB)[:, None]
    b = np.arange(2 * QB)[None, :]
    steps = a + QB - b
    band = (steps >= 0) & (steps <= N_STEPS)
    first = band & (b >= QB)
    masks = np.stack([first, band]).astype(np.int32)
    return buckets_of(QB - lanes), buckets_of(2 * QB - 1 - lanes), masks


def _bias_expand(rel_bias, lane_buckets, masks):
    def body(tab_ref, bk_ref, mk_ref, o_ref):
        for g in range(3):
            bk = bk_ref[g]
            for h in range(4):
                col = 4 * g + h
                per_offset = jnp.zeros((1, SKEW_W), F32)
                for k in range(N_BUCKETS):
                    per_offset = jnp.where(bk == k, tab_ref[k, col], per_offset)
                tile = pltpu.roll(jnp.broadcast_to(per_offset, (QB, SKEW_W)), 0, 1, stride=1, stride_axis=0)
                tile = tile[:, :2 * QB]
                o_ref[g, 0, h] = jnp.where(mk_ref[0] != 0, tile, NEG_INF)
                o_ref[g, 1, h] = jnp.where(mk_ref[1] != 0, tile, NEG_INF)

    return pl.pallas_call(
        body, name="bias_expand",
        in_specs=[pl.BlockSpec(memory_space=pltpu.SMEM),
                  pl.BlockSpec(memory_space=pltpu.VMEM),
                  pl.BlockSpec(memory_space=pltpu.VMEM)],
        out_shape=jax.ShapeDtypeStruct((3, 2, 4, QB, 2 * QB), F32),
    )(rel_bias, lane_buckets, masks)


def _bias_grad(ds1, ds2, ds3, lane_buckets):
    exchange = jnp.asarray(np.eye(QB, dtype=np.float32)[::-1].copy())

    def body(d1_ref, d2_ref, d3_ref, bk_ref, ex_ref, o_ref):
        for g, d_ref in enumerate((d1_ref, d2_ref, d3_ref)):
            bk = bk_ref[g]
            for h in range(4):
                flipped = jnp.dot(ex_ref[...], d_ref[h], preferred_element_type=F32,
                                  precision=lax.Precision.HIGHEST)
                padded = jnp.concatenate([flipped, jnp.zeros((QB, SKEW_W - 2 * QB), F32)], axis=1)
                skewed = pltpu.roll(padded, 0, 1, stride=1, stride_axis=0)
                per_offset = jnp.sum(skewed, axis=0, keepdims=True)
                for k in range(N_BUCKETS):
                    o_ref[k, 4 * g + h] = jnp.sum(jnp.where(bk == k, per_offset, 0.0))

    return pl.pallas_call(
        body, name="bias_grad",
        in_specs=[pl.BlockSpec(memory_space=pltpu.VMEM)] * 5,
        out_specs=pl.BlockSpec(memory_space=pltpu.SMEM),
        out_shape=jax.ShapeDtypeStruct((N_BUCKETS, N_HEADS), F32),
    )(ds1, ds2, ds3, lane_buckets, exchange)


def _scratch_sets(rows):
    return 4 if rows <= 512 else 1


def _unit_chunks(dil, size=16):
    units = [(h, r) for h in range(4) for r in range(dil)]
    return [units[i:i + size] for i in range(0, len(units), size)]


def _residue_rows(src_ref, copies, h, residue):
    buf = copies[h % len(copies)]
    buf[...] = src_ref[:, h * HD:(h + 1) * HD].astype(F32)
    return lambda r: buf[residue(r), :].astype(BF16)


def _attn_fwd(proj, bias, g):
    dil = DILATIONS[g]
    rows = QB * dil
    nsb = S // rows
    has_prev = nsb > 1

    def residue(r):
        return pl.ds(r, QB, stride=dil)

    n_sets = _scratch_sets(rows)
    n_in = 6 if has_prev else 4
    n_copied = (4 + (2 if has_prev else 0)) * n_sets

    def body(*refs):
        q_ref, kc_ref, vc_ref = refs[:3]
        kp_ref, vp_ref = refs[3:5] if has_prev else (None, None)
        b_ref = refs[n_in - 1]
        o_ref, l_ref = refs[n_in:n_in + 2]
        scr = list(refs[n_in + 2:])
        ls = [scr.pop(0) for _ in range(4)]
        copies = {name: [scr.pop(0) for _ in range(n_sets)]
                  for name in ("q", "kc", "vc", "o") + (("kp", "vp") if has_prev else ())}
        lane = lax.broadcasted_iota(jnp.int32, (QB, 128), 1)
        refs_of = {"q": q_ref, "kc": kc_ref, "vc": vc_ref, "kp": kp_ref, "vp": vp_ref}
        for chunk in _unit_chunks(dil):
            rows_of = {h: {name: _residue_rows(refs_of[name], copies[name], h, residue)
                           for name in refs_of if refs_of[name] is not None}
                       for h in sorted({h for h, _ in chunk})}

            def batch(name):
                return jnp.stack([rows_of[h][name](r) for h, r in chunk])

            q, k, v = batch("q"), batch("kc"), batch("vc")
            if has_prev:
                k = jnp.concatenate([batch("kp"), k], axis=1)
                v = jnp.concatenate([batch("vp"), v], axis=1)
                bias_b = jnp.stack([b_ref[h] for h, _ in chunk])
            else:
                bias_b = jnp.stack([b_ref[h, :, QB:] for h, _ in chunk])
            s = jnp.einsum("uqd,ukd->uqk", q, k, preferred_element_type=F32) * SCALE + bias_b
            m = jnp.max(s, axis=-1, keepdims=True)
            p = jnp.exp(s - m)
            l = jnp.sum(p, axis=-1, keepdims=True)
            o = jnp.einsum("uqk,ukd->uqd", p.astype(BF16), v, preferred_element_type=F32) / l
            lse = m + jnp.log(l)
            for i, (h, r) in enumerate(chunk):
                copies["o"][h % n_sets][residue(r), :] = o[i]
                ls[h][r * QB:(r + 1) * QB, :] = jnp.where(lane == h, lse[i], 0.0)
            for h in sorted({h for h, _ in chunk}):
                o_ref[:, h * HD:(h + 1) * HD] = copies["o"][h % n_sets][...]
        for r in range(dil):
            blk = slice(r * QB, (r + 1) * QB)
            l_ref[residue(r), :] = (ls[0][blk, :] + ls[1][blk, :]) + (ls[2][blk, :] + ls[3][blk, :])

    def row(b, n):
        return b * nsb + n

    def prev(b, n):
        return b * nsb + jnp.maximum(n - 1, 0)

    in_specs = [
        pl.BlockSpec((rows, GW), lambda b, n: (row(b, n), CB_Q + g)),
        pl.BlockSpec((rows, GW), lambda b, n: (row(b, n), CB_K + g)),
        pl.BlockSpec((rows, GW), lambda b, n: (row(b, n), CB_V + g)),
    ]
    args = [proj, proj, proj]
    scratch = [pltpu.VMEM((rows, 128), F32)] * (4 + n_copied)
    if has_prev:
        in_specs += [pl.BlockSpec((rows, GW), lambda b, n: (prev(b, n), CB_K + g)),
                     pl.BlockSpec((rows, GW), lambda b, n: (prev(b, n), CB_V + g))]
        args += [proj, proj]
    in_specs.append(pl.BlockSpec((None, None, 4, QB, 2 * QB),
                                 lambda b, n: (g, jnp.minimum(n, 1), 0, 0, 0)))
    args.append(bias)
    return pl.pallas_call(
        body, name=f"attn_fwd{g}",
        grid=(BL, nsb),
        in_specs=in_specs,
        out_specs=(pl.BlockSpec((rows, GW), lambda b, n: (row(b, n), 0)),
                   pl.BlockSpec((rows, 128), lambda b, n: (row(b, n), 0))),
        out_shape=(jax.ShapeDtypeStruct((T, GW), F32), jax.ShapeDtypeStruct((T, 128), F32)),
        scratch_shapes=scratch,
        compiler_params=pltpu.CompilerParams(vmem_limit_bytes=VMEM_LIMIT),
    )(*args)


def _attn_bwd(proj, d_out, stats, bias, dproj, g):
    dil = DILATIONS[g]
    rows = QB * dil
    nsb = S // rows
    has_prev = nsb > 1
    n_steps = nsb + 1 if has_prev else 1
    n_in = 7 + (2 if has_prev else 0)

    def residue(r):
        return pl.ds(r, QB, stride=dil)

    n_sets = _scratch_sets(rows)

    def body(*refs):
        q_ref, kc_ref, vc_ref, do_ref, st_ref, b_ref = refs[:6]
        kp_ref, vp_ref = refs[6:8] if has_prev else (None, None)
        out_ref, db_ref = refs[n_in], refs[n_in + 1]
        scr = list(refs[n_in + 2:])
        sq, sk, sv, sems = [scr.pop(0) for _ in range(4)]
        carry = scr.pop(0) if has_prev else None
        sts = scr.pop(0)
        copies = {name: [scr.pop(0) for _ in range(n_sets)]
                  for name in ("q", "kc", "vc", "do", "dq", "dk", "dv") + (("kp", "vp") if has_prev else ())}
        b, n = pl.program_id(0), pl.program_id(1)

        @pl.when((b == 0) & (n == 0))
        def _():
            db_ref[...] = jnp.zeros_like(db_ref)

        def finish(h, r, dq, dk, dv):
            for name, val in (("dq", dq), ("dk", dk), ("dv", dv)):
                copies[name][h % n_sets][residue(r), :] = val

        step = b * n_steps + n
        slot = step % 2

        def stage_copies(s, row0):
            return _column_copies([(sq.at[s], CB * (CB_Q + g)), (sk.at[s], CB * (CB_K + g)),
                                   (sv.at[s], CB * (CB_V + g))], out_ref, row0, sems.at[s])

        @pl.when((step >= 2) & ((step - 2) % n_steps >= (1 if has_prev else 0)))
        def _():
            for cp in stage_copies(slot, 0):
                cp.wait()

        def finish_head(h):
            sl = slice(h * HD, (h + 1) * HD)
            sq[slot, :, sl] = copies["dq"][h % n_sets][...].astype(BF16)
            sk[slot, :, sl] = copies["dk"][h % n_sets][...].astype(BF16)
            sv[slot, :, sl] = copies["dv"][h % n_sets][...].astype(BF16)

        def write_block(blk_idx):
            for cp in stage_copies(slot, pl.multiple_of(blk_idx * rows, rows)):
                cp.start()

            @pl.when(step == BL * n_steps - 1)
            def _():
                for s in range(2):
                    for cp in stage_copies(s, 0):
                        cp.wait()

        def carried(h, r):
            blk = slice(r * QB, (r + 1) * QB)
            return ((blk, slice(h * HD, (h + 1) * HD)), (blk, slice(GW + h * HD, GW + (h + 1) * HD)),
                    (blk, slice(2 * GW + h * HD, 2 * GW + (h + 1) * HD)))

        if has_prev:
            @pl.when(n == 0)
            def _():
                carry[...] = jnp.zeros_like(carry)

            @pl.when(n == nsb)
            def _():
                for h in range(4):
                    for r in range(dil):
                        cq, ck, cv = carried(h, r)
                        finish(h, r, carry[cq], carry[ck], carry[cv])
                    finish_head(h)
                write_block(b * nsb + nsb - 1)

        @pl.when(n < nsb)
        def _():
            for r in range(dil):
                sts[r * QB:(r + 1) * QB, :] = st_ref[residue(r), :]
            refs_of = {"q": q_ref, "kc": kc_ref, "vc": vc_ref, "do": do_ref, "kp": kp_ref, "vp": vp_ref}
            for chunk in _unit_chunks(dil):
                heads = sorted({h for h, _ in chunk})
                rows_of = {h: {name: _residue_rows(refs_of[name], copies[name], h, residue)
                               for name in refs_of if refs_of[name] is not None}
                           for h in heads}

                def batch(name):
                    return jnp.stack([rows_of[h][name](r) for h, r in chunk])

                q, k, v, do = batch("q"), batch("kc"), batch("vc"), batch("do")
                if has_prev:
                    k = jnp.concatenate([batch("kp"), k], axis=1)
                    v = jnp.concatenate([batch("vp"), v], axis=1)
                    bias_b = jnp.stack([b_ref[h] for h, _ in chunk])
                else:
                    bias_b = jnp.stack([b_ref[h, :, QB:] for h, _ in chunk])
                lse = jnp.stack([sts[r * QB:(r + 1) * QB, h:h + 1] for h, r in chunk])
                delta = jnp.stack([sts[r * QB:(r + 1) * QB, 4 + h:5 + h] for h, r in chunk])
                s = jnp.einsum("uqd,ukd->uqk", q, k, preferred_element_type=F32) * SCALE + bias_b
                p = jnp.exp(s - lse)
                ds = p * (jnp.einsum("uqd,ukd->uqk", do, v, preferred_element_type=F32) - delta)
                for h in heads:
                    mine = [ds[i] for i, (hh, _) in enumerate(chunk) if hh == h]
                    tot = mine[0]
                    for extra in mine[1:]:
                        tot = tot + extra
                    if has_prev:
                        db_ref[h] += tot
                    else:
                        db_ref[h, :, QB:] += tot
                dsb, pb = ds.astype(BF16), p.astype(BF16)
                dq = jnp.einsum("uqk,ukd->uqd", dsb, k, preferred_element_type=F32) * SCALE
                dk = jnp.einsum("uqk,uqd->ukd", dsb, q, preferred_element_type=F32) * SCALE
                dv = jnp.einsum("uqk,uqd->ukd", pb, do, preferred_element_type=F32)
                for i, (h, r) in enumerate(chunk):
                    if has_prev:
                        cq, ck, cv = carried(h, r)
                        finish(h, r, carry[cq], carry[ck] + dk[i, :QB], carry[cv] + dv[i, :QB])
                        carry[cq] = dq[i]
                        carry[ck] = dk[i, QB:]
                        carry[cv] = dv[i, QB:]
                    else:
                        finish(h, r, dq[i], dk[i], dv[i])
                for h in heads:
                    finish_head(h)
            if has_prev:
                @pl.when(n > 0)
                def _():
                    write_block(b * nsb + n - 1)
            else:
                write_block(b)

    def row(b, n):
        return b * nsb + jnp.minimum(n, nsb - 1)

    def prev(b, n):
        return b * nsb + jnp.maximum(jnp.minimum(n, nsb - 1) - 1, 0)

    in_specs = [
        pl.BlockSpec((rows, GW), lambda b, n: (row(b, n), CB_Q + g)),
        pl.BlockSpec((rows, GW), lambda b, n: (row(b, n), CB_K + g)),
        pl.BlockSpec((rows, GW), lambda b, n: (row(b, n), CB_V + g)),
        pl.BlockSpec((rows, GW), lambda b, n: (row(b, n), 0)),
        pl.BlockSpec((rows, 128), lambda b, n: (row(b, n), 0)),
        pl.BlockSpec((None, None, 4, QB, 2 * QB),
                     lambda b, n: (g, jnp.minimum(jnp.minimum(n, nsb - 1), 1), 0, 0, 0)),
    ]
    args = [proj, proj, proj, d_out, stats, bias]
    scratch = [pltpu.VMEM((2, rows, GW), BF16)] * 3 + [pltpu.SemaphoreType.DMA((2, 3))]
    if has_prev:
        in_specs += [pl.BlockSpec((rows, GW), lambda b, n: (prev(b, n), CB_K + g)),
                     pl.BlockSpec((rows, GW), lambda b, n: (prev(b, n), CB_V + g))]
        args += [proj, proj]
        scratch.append(pltpu.VMEM((rows, 3 * GW), F32))
    n_copied = (7 + (2 if has_prev else 0)) * n_sets
    scratch += [pltpu.VMEM((rows, 128), F32)] * (1 + n_copied)
    in_specs.append(pl.BlockSpec(memory_space=pl.ANY))
    args.append(dproj)
    return pl.pallas_call(
        body, name=f"attn_bwd{g}",
        grid=(BL, n_steps),
        in_specs=in_specs,
        out_specs=(pl.BlockSpec(memory_space=pl.ANY),
                   pl.BlockSpec((4, QB, 2 * QB), lambda b, n: (0, 0, 0))),
        out_shape=(jax.ShapeDtypeStruct((T, NCOL), BF16),
                   jax.ShapeDtypeStruct((4, QB, 2 * QB), F32)),
        scratch_shapes=scratch,
        input_output_aliases={len(args) - 1: 0},
        compiler_params=pltpu.CompilerParams(vmem_limit_bytes=VMEM_LIMIT),
    )(*args)


def _tail(x2, tgt2, mod3, o_g, lse_g, proj, w_ao, w_co, w_o, conv_w, conv_b, ln_g, ln_b):
    tm = 256
    per_seq = S // tm
    halo = 16

    def body(x_ref, t_ref, mod_ref, o1_ref, o2_ref, o3_ref, l1_ref, l2_ref, l3_ref,
             ga_ref, u_ref, bg_ref, cg_ref, gc_ref, ma_ref, mc_ref, up_ref, cp_ref,
             wao_ref, wco_ref, wo_ref, cw_ref, cb_ref, lg_ref, lb_ref,
             dproj_ref, dyc_ref, do_ref, st_ref, dxd_ref,
             gwo_ref, gwco_ref, gwao_ref, vec_ref,
             dga_s, dbg_s, dgm_s, sems, acc_o, acc_co, acc_ao):
        i = pl.program_id(0)
        bidx = i // per_seq
        first = (i % per_seq) == 0

        @pl.when(i == 0)
        def _():
            vec_ref[...] = jnp.zeros_like(vec_ref)

        slot = i % 2

        def column_copies(s, row0):
            return _column_copies([(dga_s.at[s], CB * CB_GA), (dbg_s.at[s], D * KB_BG), (dgm_s.at[s], D * KB_GC)],
                                  dproj_ref, row0, sems.at[s])

        @pl.when(i >= 2)
        def _():
            for cp in column_copies(slot, 0):
                cp.wait()

        l1, l2, l3 = l1_ref[...], l2_ref[...], l3_ref[...]
        mx = jnp.maximum(jnp.maximum(l1, l2), l3)
        e1, e2, e3 = jnp.exp(l1 - mx), jnp.exp(l2 - mx), jnp.exp(l3 - mx)
        esum = e1 + e2 + e3
        lse_tot = mx + jnp.log(esum)
        w1, w2, w3 = e1 / esum, e2 / esum, e3 / esum

        def per_head(wv):
            return jnp.concatenate([jnp.broadcast_to(wv[:, h:h + 1], (tm, HD)) for h in range(4)], axis=1)

        o = per_head(w1) * o1_ref[...] + per_head(w2) * o2_ref[...] + per_head(w3) * o3_ref[...]

        ga = ga_ref[...].astype(F32)
        sig_ga = _sigmoid(ga)
        silu_ga = ga * sig_ga
        a_in = (o * silu_ga).astype(BF16)
        a_out = _dot(a_in, wao_ref[...])

        u = u_ref[...].astype(F32)
        cg = cg_ref[...].astype(F32)
        z = cg * u
        zp = cp_ref[...].astype(F32) * up_ref[...].astype(F32)
        zp = jnp.where(first, 0.0, zp)
        zcat = jnp.concatenate([zp, z], axis=0)
        z1 = pltpu.roll(zcat, 1, 0)[halo:]
        z2 = pltpu.roll(zcat, 2, 0)[halo:]
        y_conv = cw_ref[0:1, :] * z2 + cw_ref[1:2, :] * z1 + cw_ref[2:3, :] * z + cb_ref[...]
        gc = gc_ref[...].astype(F32)
        sig_gc = _sigmoid(gc)
        silu_gc = gc * sig_gc
        bg = bg_ref[...].astype(F32)
        bg_yc = bg * y_conv
        s_in = (bg_yc * silu_gc).astype(BF16)
        s_out = _dot(s_in, wco_ref[...])

        sa = _sigmoid(ma_ref[...].astype(F32))
        sc = _sigmoid(mc_ref[...].astype(F32))
        merged = (sa * a_out + sc * s_out).astype(BF16)
        y = _dot(merged, wo_ref[...])
        gate1 = 1.0 + mod_ref[0, 2:3, :]
        xv = x_ref[...]
        resid = ALPHA * xv + gate1 * y
        mu = jnp.mean(resid, axis=1, keepdims=True)
        xc = resid - mu
        var = jnp.mean(xc * xc, axis=1, keepdims=True)
        rstd = lax.rsqrt(var + LN_EPS)
        xhat = xc * rstd
        lg = lg_ref[...]
        err = xhat * lg + lb_ref[...] - t_ref[...]
        vec_ref[3:4, :] += (0.5 / D) * jnp.sum(err * err, axis=0, keepdims=True)

        vec_ref[1:2, :] += (1.0 / D) * jnp.sum(err * xhat, axis=0, keepdims=True)
        vec_ref[2:3, :] += (1.0 / D) * jnp.sum(err, axis=0, keepdims=True)
        dxh = err * (lg * (1.0 / D))
        dres = rstd * (dxh - jnp.mean(dxh, axis=1, keepdims=True)
                       - xhat * jnp.mean(dxh * xhat, axis=1, keepdims=True))
        dxd_ref[...] = ALPHA * dres
        dgate = jnp.sum(dres * y, axis=0, keepdims=True)
        vec_ref[4:5, :] += jnp.where(bidx == 0, dgate, 0.0)
        vec_ref[5:6, :] += jnp.where(bidx == 1, dgate, 0.0)
        dy = (dres * gate1).astype(BF16)

        dmerged = _dot_nt(dy, wo_ref[...])
        da_out_f = dmerged * sa
        ds_out_f = dmerged * sc
        da_out = da_out_f.astype(BF16)
        ds_out = ds_out_f.astype(BF16)
        dgm_s[slot, :, 2 * D:3 * D] = (ds_out_f * s_out * (1.0 - sc)).astype(BF16)
        dgm_s[slot, :, D:2 * D] = (da_out_f * a_out * (1.0 - sa)).astype(BF16)
        da_in = _dot_nt(da_out, wao_ref[...])
        ds_in = _dot_nt(ds_out, wco_ref[...])

        d_o = da_in * silu_ga
        do_ref[...] = d_o.astype(BF16)
        dga_s[slot] = (da_in * o * (sig_ga + silu_ga * (1.0 - sig_ga))).astype(BF16)
        lane = lax.broadcasted_iota(jnp.int32, (tm, 128), 1)
        stats = lse_tot
        od = o * d_o
        for h in range(4):
            delta = jnp.sum(od[:, h * HD:(h + 1) * HD], axis=1, keepdims=True)
            stats = jnp.where(lane == 4 + h, delta, stats)
        st_ref[...] = stats

        ds_silu = ds_in * silu_gc
        dbg_s[slot] = (ds_silu * y_conv).astype(BF16)
        dyc = ds_silu * bg
        dyc_ref[...] = dyc
        vec_ref[0:1, :] += jnp.sum(dyc, axis=0, keepdims=True)
        dgm_s[slot, :, 0:D] = (ds_in * bg_yc * (sig_gc + silu_gc * (1.0 - sig_gc))).astype(BF16)

        @pl.when(i == 0)
        def _():
            acc_o[...] = jnp.zeros_like(acc_o)
            acc_co[...] = jnp.zeros_like(acc_co)
            acc_ao[...] = jnp.zeros_like(acc_ao)

        acc_o[...] += _dot_tn(merged, dy)
        acc_co[...] += _dot_tn(s_in, ds_out)
        acc_ao[...] += _dot_tn(a_in, da_out)

        for cp in column_copies(slot, pl.multiple_of(i * tm, tm)):
            cp.start()

        @pl.when(i == T // tm - 1)
        def _():
            gwo_ref[...] = acc_o[...].astype(BF16)
            gwco_ref[...] = acc_co[...].astype(BF16)
            gwao_ref[...] = acc_ao[...].astype(BF16)
            for s in range(2):
                for cp in column_copies(s, 0):
                    cp.wait()

    def tile(width, cblk=0):
        return pl.BlockSpec((tm, width), lambda i: (i, cblk))

    def whole(shape):
        return pl.BlockSpec(shape, lambda i: tuple(0 for _ in shape))

    def once(shape):
        return pl.BlockSpec(shape, lambda i: tuple(0 for _ in shape), pipeline_mode=pl.Buffered(1))

    prev_rows = lambda i: (jnp.maximum(i * (tm // halo) - 1, 0),)
    in_specs = [
        tile(D), tile(D), pl.BlockSpec((1, 3, D), lambda i: (i // per_seq, 0, 0)),
        tile(GW), tile(GW), tile(GW), tile(128), tile(128), tile(128),
        tile(GW, CB_GA), tile(D, KB_U), tile(D, KB_BG), tile(D, KB_CG), tile(D, KB_GC),
        tile(D, KB_MA), tile(D, KB_MC),
        pl.BlockSpec((halo, D), lambda i: (*prev_rows(i), KB_U)),
        pl.BlockSpec((halo, D), lambda i: (*prev_rows(i), KB_CG)),
        whole((GW, D)), whole((D, D)), whole((D, D)),
        whole((3, D)), whole((1, D)), whole((1, D)), whole((1, D)),
    ]
    out_specs = (
        pl.BlockSpec(memory_space=pl.ANY), tile(D), tile(GW), tile(128), tile(D),
        once((D, D)), once((D, D)), once((GW, D)),
        pl.BlockSpec((8, D), lambda i: (0, 0)),
    )
    out_shape = (
        jax.ShapeDtypeStruct((T, NCOL), BF16),
        jax.ShapeDtypeStruct((T, D), F32),
        jax.ShapeDtypeStruct((T, GW), BF16),
        jax.ShapeDtypeStruct((T, 128), F32),
        jax.ShapeDtypeStruct((T, D), F32),
        jax.ShapeDtypeStruct((D, D), BF16),
        jax.ShapeDtypeStruct((D, D), BF16),
        jax.ShapeDtypeStruct((GW, D), BF16),
        jax.ShapeDtypeStruct((8, D), F32),
    )
    return pl.pallas_call(
        body, name="tail",
        grid=(T // tm,),
        in_specs=in_specs, out_specs=out_specs, out_shape=out_shape,
        scratch_shapes=[pltpu.VMEM((2, tm, GW), BF16), pltpu.VMEM((2, tm, D), BF16), pltpu.VMEM((2, tm, 3 * D), BF16),
                        pltpu.SemaphoreType.DMA((2, 3)),
                        pltpu.VMEM((D, D), F32), pltpu.VMEM((D, D), F32), pltpu.VMEM((GW, D), F32)],
        compiler_params=pltpu.CompilerParams(vmem_limit_bytes=VMEM_LIMIT_TAIL),
    )(x2, tgt2, mod3, *o_g, *lse_g, proj, proj, proj, proj, proj, proj, proj, proj, proj,
      w_ao, w_co, w_o, conv_w, conv_b, ln_g, ln_b)


def _conv_bwd(dyc, proj, conv_w, dproj):
    tm = 512
    per_seq = S // tm

    def body(d_ref, dn_ref, u_ref, c_ref, cw_ref, _, dproj_ref, g_ref, du_s, dc_s, sems):
        i = pl.program_id(0)
        last = (i % per_seq) == per_seq - 1

        @pl.when(i == 0)
        def _():
            g_ref[...] = jnp.zeros_like(g_ref)

        slot = i % 2

        def column_copies(s, row0):
            return _column_copies([(du_s.at[s], D * KB_U), (dc_s.at[s], D * KB_CG)], dproj_ref, row0, sems.at[s])

        @pl.when(i >= 2)
        def _():
            for cp in column_copies(slot, 0):
                cp.wait()

        d = d_ref[...]
        dn = jnp.where(last, 0.0, dn_ref[...])
        dcat = jnp.concatenate([d, dn], axis=0)
        d1 = pltpu.roll(dcat, tm + 8 - 1, 0)[:tm]
        d2 = pltpu.roll(dcat, tm + 8 - 2, 0)[:tm]
        dz = cw_ref[2:3, :] * d + cw_ref[1:2, :] * d1 + cw_ref[0:1, :] * d2
        u = u_ref[...].astype(F32)
        cg = c_ref[...].astype(F32)
        du_s[slot] = (dz * cg).astype(BF16)
        dc_s[slot] = (dz * u).astype(BF16)
        for cp in column_copies(slot, pl.multiple_of(i * tm, tm)):
            cp.start()

        z = cg * u
        g_ref[0:1, :] += jnp.sum(d2 * z, axis=0, keepdims=True)
        g_ref[1:2, :] += jnp.sum(d1 * z, axis=0, keepdims=True)
        g_ref[2:3, :] += jnp.sum(d * z, axis=0, keepdims=True)

        @pl.when(i == T // tm - 1)
        def _():
            for s in range(2):
                for cp in column_copies(s, 0):
                    cp.wait()

    n_tiles = T // tm
    next_rows = lambda i: jnp.minimum((i + 1) * (tm // 8), T // 8 - 1)
    return pl.pallas_call(
        body, name="conv_bwd",
        grid=(n_tiles,),
        in_specs=[pl.BlockSpec((tm, D), lambda i: (i, 0)),
                  pl.BlockSpec((8, D), lambda i: (next_rows(i), 0)),
                  pl.BlockSpec((tm, D), lambda i: (i, KB_U)),
                  pl.BlockSpec((tm, D), lambda i: (i, KB_CG)),
                  pl.BlockSpec((3, D), lambda i: (0, 0)),
                  pl.BlockSpec(memory_space=pl.ANY)],
        out_specs=(pl.BlockSpec(memory_space=pl.ANY),
                   pl.BlockSpec((8, D), lambda i: (0, 0))),
        out_shape=(jax.ShapeDtypeStruct((T, NCOL), BF16),
                   jax.ShapeDtypeStruct((8, D), F32)),
        scratch_shapes=[pltpu.VMEM((2, tm, D), BF16), pltpu.VMEM((2, tm, D), BF16), pltpu.SemaphoreType.DMA((2, 2))],
        input_output_aliases={5: 0},
        compiler_params=pltpu.CompilerParams(vmem_limit_bytes=VMEM_LIMIT),
    )(dyc, dyc, proj, proj, conv_w, dproj)


def _dh_dx(dproj, w_in_all, x2, dxd, mod3, chip_sums, hops=(), parts0=None):
    tm = 512
    per_seq = S // tm
    n_pass, _, width = w_in_all.shape
    n = len(chip_sums)
    n_in = 5 + n + (0 if parts0 is None else 1)

    def body(*refs):
        d_ref, w_ref, x_ref, dxd_ref, mod_ref = refs[:5]
        ins = refs[5:5 + n]
        gx_ref, vec_ref = refs[n_in:n_in + 2]
        outs = refs[n_in + 2:n_in + 2 + n]
        acc, send_sems, recv_sems, local_sems = refs[n_in + 2 + n:]
        jj, i = pl.program_id(0), pl.program_id(1)

        @pl.when((i == 0) & (jj == 0))
        def _():
            vec_ref[...] = jnp.zeros_like(vec_ref)
            if n:
                sends, _, mine = _chip_copies(ins, outs, send_sems, recv_sems, local_sems, hops)
                for cp in sends + mine:
                    cp.start()

        if n:
            @pl.when((i == T // tm - 1) & (jj == n_pass - 1))
            def _():
                sends, arrivals, mine = _chip_copies(ins, outs, send_sems, recv_sems, local_sems, hops)
                for cp in arrivals:
                    cp.wait_recv()
                for cp in sends:
                    cp.wait_send()
                for cp in mine:
                    cp.wait()

        def partial():
            return _dot_nt(d_ref[...], w_ref[...])

        @pl.when(jj == 0)
        def _():
            acc[i] = partial()

        @pl.when((jj > 0) & (jj < n_pass - 1))
        def _():
            acc[i] += partial()

        @pl.when(jj == n_pass - 1)
        def _():
            dh = acc[i] + partial()
            bidx = i // per_seq
            gx_ref[...] = dxd_ref[...] + dh * (1.0 + mod_ref[0, 1:2, :])
            dshift = jnp.sum(dh, axis=0, keepdims=True)
            dscale = jnp.sum(dh * x_ref[...], axis=0, keepdims=True)
            vec_ref[0:1, :] += jnp.where(bidx == 0, dshift, 0.0)
            vec_ref[1:2, :] += jnp.where(bidx == 1, dshift, 0.0)
            vec_ref[2:3, :] += jnp.where(bidx == 0, dscale, 0.0)
            vec_ref[3:4, :] += jnp.where(bidx == 1, dscale, 0.0)

    def last_pass(jj, i):
        return jnp.where(jj == n_pass - 1, i, 0)

    any_spec = pl.BlockSpec(memory_space=pl.ANY)
    res = pl.pallas_call(
        body, name="dh_dx",
        grid=(n_pass, T // tm),
        in_specs=[
            pl.BlockSpec((tm, width), lambda jj, i: (i, jj)),
            pl.BlockSpec((None, D, width), lambda jj, i: (jj, 0, 0)),
            pl.BlockSpec((tm, D), lambda jj, i: (last_pass(jj, i), 0)),
            pl.BlockSpec((tm, D), lambda jj, i: (last_pass(jj, i), 0)),
            pl.BlockSpec((1, 3, D), lambda jj, i: (last_pass(jj, i) // per_seq, 0, 0))]
                 + [any_spec] * (n_in - 5),
        out_specs=(pl.BlockSpec((tm, D), lambda jj, i: (last_pass(jj, i), 0)),
                   pl.BlockSpec((8, D), lambda jj, i: (0, 0))) + (any_spec,) * n,
        out_shape=(jax.ShapeDtypeStruct((T, D), F32), jax.ShapeDtypeStruct((8, D), F32))
                  + tuple(jax.ShapeDtypeStruct(a.shape, a.dtype) for a in chip_sums),
        scratch_shapes=[pltpu.VMEM((T // tm, tm, D), F32), pltpu.SemaphoreType.DMA((max(3 * n, 1),)),
                        pltpu.SemaphoreType.DMA((max(3 * n, 1),)), pltpu.SemaphoreType.DMA((max(n, 1),))],
        input_output_aliases={} if parts0 is None else {5 + n: 2},
        compiler_params=pltpu.CompilerParams(vmem_limit_bytes=VMEM_LIMIT),
    )(dproj, w_in_all, x2, dxd, mod3, *chip_sums, *([] if parts0 is None else [parts0]))
    return res[0], res[1], res[2:]


def _adam_step(g, w, m, v):
    nm = ADAM_B1 * m + (1.0 - ADAM_B1) * g
    nv = ADAM_B2 * v + (1.0 - ADAM_B2) * (g * g)
    m_hat = nm / (1.0 - ADAM_B1 ** ADAM_STEP)
    v_hat = nv / (1.0 - ADAM_B2 ** ADAM_STEP)
    return -ADAM_LR * (m_hat / (jnp.sqrt(v_hat) + ADAM_EPS) + ADAM_WD * w), nm, nv


def _adamw(parts, w, m, v, name, row_tile=None):
    n_parts, rows, cols = parts.shape
    tr = rows if row_tile is None else row_tile

    def body(p_ref, w_ref, m_ref, v_ref, g_ref, d_ref, nm_ref, nv_ref):
        g = p_ref[0].astype(F32)
        for s in range(1, n_parts):
            g = g + p_ref[s].astype(F32)
        g_ref[...] = g
        d_ref[...], nm_ref[...], nv_ref[...] = _adam_step(g, w_ref[...], m_ref[...], v_ref[...])

    blk = pl.BlockSpec((tr, cols), lambda i: (i, 0))
    shp = jax.ShapeDtypeStruct((rows, cols), F32)
    return pl.pallas_call(
        body, name=name,
        grid=(rows // tr,),
        in_specs=[pl.BlockSpec((n_parts, tr, cols), lambda i: (0, i, 0)), blk, blk, blk],
        out_specs=(blk, blk, blk, blk),
        out_shape=(shp, shp, shp, shp),
        compiler_params=pltpu.CompilerParams(vmem_limit_bytes=VMEM_LIMIT),
    )(parts, w, m, v)


def _multi_adamw(parts_list, params, name):
    n = len(params)
    flat = [t for wmv in params for t in wmv]

    def body(*refs):
        parts, ins, outs = refs[:n], refs[n:4 * n], refs[4 * n:]
        for p in range(n):
            g = parts[p][0].astype(F32)
            for s in range(1, parts[p].shape[0]):
                g = g + parts[p][s].astype(F32)
            w_ref, m_ref, v_ref = ins[3 * p:3 * p + 3]
            g_ref, d_ref, nm_ref, nv_ref = outs[4 * p:4 * p + 4]
            g_ref[...] = g
            d_ref[...], nm_ref[...], nv_ref[...] = _adam_step(g, w_ref[...], m_ref[...], v_ref[...])

    out_shape = []
    for w, _, _ in params:
        out_shape += [jax.ShapeDtypeStruct(w.shape, F32)] * 4
    res = pl.pallas_call(body, name=name, out_shape=tuple(out_shape))(*parts_list, *flat)
    return [res[4 * p:4 * p + 4] for p in range(n)]


def _small_updates(small_g, dmod_all, rel_parts, params):
    flat = [t for wmv in params for t in wmv]

    def body(sg_ref, dm_ref, rp_ref, *refs):
        ins, outs = refs[:len(flat)], refs[len(flat):]

        def over_devices(row):
            tot = sg_ref[0, row:row + 1, :]
            for s in range(1, N_DEV):
                tot = tot + sg_ref[s, row:row + 1, :]
            return tot

        g_b_ada = dm_ref[0:1, :]
        for r in range(1, N_DEV * BL):
            g_b_ada = g_b_ada + dm_ref[r:r + 1, :]
        g_rel = rp_ref[0]
        for s in range(1, N_DEV):
            g_rel = g_rel + rp_ref[s]
        grads = [g_b_ada, over_devices(0), g_rel, over_devices(1), over_devices(2)]
        outs[0][...] = jnp.sum(over_devices(3), axis=1, keepdims=True)
        for p, g in enumerate(grads):
            w_ref, m_ref, v_ref = ins[3 * p:3 * p + 3]
            g_ref, d_ref, nm_ref, nv_ref = outs[1 + 4 * p:5 + 4 * p]
            g_ref[...] = g
            d_ref[...], nm_ref[...], nv_ref[...] = _adam_step(g, w_ref[...], m_ref[...], v_ref[...])

    out_shape = [jax.ShapeDtypeStruct((1, 1), F32)]
    for w, _, _ in params:
        out_shape += [jax.ShapeDtypeStruct(w.shape, F32)] * 4
    res = pl.pallas_call(body, name="small_updates", out_shape=tuple(out_shape))(small_g, dmod_all, rel_parts, *flat)
    return res[0], [res[1 + 4 * p:5 + 4 * p] for p in range(len(params))]


def _attn_fwd_dense(proj, bias):
    nq = 4
    rows = nq * QB
    nsb = S // rows

    def body(q_ref, k_ref, v_ref, kp_ref, vp_ref, b_ref, o_ref, l_ref, ls0, ls1, ls2, ls3):
        ls = [ls0, ls1, ls2, ls3]
        n = pl.program_id(1)
        lane = lax.broadcasted_iota(jnp.int32, (QB, 128), 1)
        units = [(h, j) for h in range(4) for j in range(nq)]

        def keys(cur_ref, prev_ref, h, j):
            sl = slice(h * HD, (h + 1) * HD)
            if j == 0:
                return jnp.concatenate([prev_ref[:, sl], cur_ref[0:QB, sl]], axis=0)
            return cur_ref[(j - 1) * QB:(j + 1) * QB, sl]

        q = jnp.stack([q_ref[j * QB:(j + 1) * QB, h * HD:(h + 1) * HD] for h, j in units])
        k = jnp.stack([keys(k_ref, kp_ref, h, j) for h, j in units])
        v = jnp.stack([keys(v_ref, vp_ref, h, j) for h, j in units])
        bias_b = jnp.stack([b_ref[jnp.minimum(n, 1), h] if j == 0 else b_ref[1, h] for h, j in units])
        s = jnp.einsum("uqd,ukd->uqk", q, k, preferred_element_type=F32) * SCALE + bias_b
        m = jnp.max(s, axis=-1, keepdims=True)
        p = jnp.exp(s - m)
        l = jnp.sum(p, axis=-1, keepdims=True)
        o = jnp.einsum("uqk,ukd->uqd", p.astype(BF16), v, preferred_element_type=F32) / l
        lse = m + jnp.log(l)
        for i, (h, j) in enumerate(units):
            o_ref[j * QB:(j + 1) * QB, h * HD:(h + 1) * HD] = o[i]
            ls[h][j * QB:(j + 1) * QB, :] = jnp.where(lane == h, lse[i], 0.0)
        l_ref[...] = (ls[0][...] + ls[1][...]) + (ls[2][...] + ls[3][...])

    def row(b, n):
        return b * nsb + n

    def prev(b, n):
        return jnp.maximum((b * nsb + n) * nq - 1, 0)

    in_specs = [
        pl.BlockSpec((rows, GW), lambda b, n: (row(b, n), CB_Q)),
        pl.BlockSpec((rows, GW), lambda b, n: (row(b, n), CB_K)),
        pl.BlockSpec((rows, GW), lambda b, n: (row(b, n), CB_V)),
        pl.BlockSpec((QB, GW), lambda b, n: (prev(b, n), CB_K)),
        pl.BlockSpec((QB, GW), lambda b, n: (prev(b, n), CB_V)),
        pl.BlockSpec((None, 2, 4, QB, 2 * QB), lambda b, n: (0, 0, 0, 0, 0)),
    ]
    return pl.pallas_call(
        body, name="attn_fwd0",
        grid=(BL, nsb),
        in_specs=in_specs,
        out_specs=(pl.BlockSpec((rows, GW), lambda b, n: (row(b, n), 0)),
                   pl.BlockSpec((rows, 128), lambda b, n: (row(b, n), 0))),
        out_shape=(jax.ShapeDtypeStruct((T, GW), F32), jax.ShapeDtypeStruct((T, 128), F32)),
        scratch_shapes=[pltpu.VMEM((rows, 128), F32)] * 4,
        compiler_params=pltpu.CompilerParams(vmem_limit_bytes=VMEM_LIMIT),
    )(proj, proj, proj, proj, proj, bias)


def _attn_bwd_dense(proj, d_out, stats, bias, dproj):
    nq = 4
    rows = nq * QB
    nsb = S // rows
    cols_q, cols_k, cols_v = CB * CB_Q, CB * CB_K, CB * CB_V

    def body(q_ref, k_ref, v_ref, do_ref, st_ref, kp_ref, vp_ref, b_ref, _, out_ref, db_ref,
             sq, sk, sv, carry, sems):
        b, n = pl.program_id(0), pl.program_id(1)
        units = [(h, j) for h in range(4) for j in range(nq)]

        @pl.when((b == 0) & (n == 0))
        def _():
            db_ref[...] = jnp.zeros_like(db_ref)

        @pl.when(n == 0)
        def _():
            carry[...] = jnp.zeros_like(carry)

        step = b * (nsb + 1) + n
        slot = step % 2

        def block_copies(s, position, block):
            part = pl.ds(position * QB, QB)
            return _column_copies([(sq.at[s, part], cols_q), (sk.at[s, part], cols_k), (sv.at[s, part], cols_v)],
                                  out_ref, pl.multiple_of(block * QB, QB), sems.at[s, position])

        def wait_blocks(s, positions):
            for position in positions:
                for cp in block_copies(s, position, 0):
                    cp.wait()

        before = (step - 2) % (nsb + 1)

        @pl.when((step >= 2) & (before > 0))
        def _():
            wait_blocks(slot, [0])

        @pl.when((step >= 2) & (before < nsb))
        def _():
            wait_blocks(slot, range(1, nq))

        def write(first_block, position, count):
            for j in range(count):
                for cp in block_copies(slot, position + j, first_block + j):
                    cp.start()

        @pl.when(n == nsb)
        def _():
            sq[slot, 0:QB, :] = carry[:, 0:GW].astype(BF16)
            sk[slot, 0:QB, :] = carry[:, GW:2 * GW].astype(BF16)
            sv[slot, 0:QB, :] = carry[:, 2 * GW:3 * GW].astype(BF16)
            write((b + 1) * nsb * nq - 1, 0, 1)

            @pl.when(b == BL - 1)
            def _():
                wait_blocks(slot, [0])
                wait_blocks(1 - slot, range(nq))

        @pl.when(n < nsb)
        def _():
            def keys(cur_ref, prev_ref, h, j):
                sl = slice(h * HD, (h + 1) * HD)
                if j == 0:
                    return jnp.concatenate([prev_ref[:, sl], cur_ref[0:QB, sl]], axis=0)
                return cur_ref[(j - 1) * QB:(j + 1) * QB, sl]

            def block(ref, h, j):
                return ref[j * QB:(j + 1) * QB, h * HD:(h + 1) * HD]

            q = jnp.stack([block(q_ref, h, j) for h, j in units])
            do = jnp.stack([block(do_ref, h, j) for h, j in units])
            k = jnp.stack([keys(k_ref, kp_ref, h, j) for h, j in units])
            v = jnp.stack([keys(v_ref, vp_ref, h, j) for h, j in units])
            bias_b = jnp.stack([b_ref[jnp.minimum(n, 1), h] if j == 0 else b_ref[1, h] for h, j in units])
            lse = jnp.stack([st_ref[j * QB:(j + 1) * QB, h:h + 1] for h, j in units])
            delta = jnp.stack([st_ref[j * QB:(j + 1) * QB, 4 + h:5 + h] for h, j in units])
            s = jnp.einsum("uqd,ukd->uqk", q, k, preferred_element_type=F32) * SCALE + bias_b
            p = jnp.exp(s - lse)
            ds = p * (jnp.einsum("uqd,ukd->uqk", do, v, preferred_element_type=F32) - delta)
            for h in range(4):
                tot = ds[h * nq]
                for j in range(1, nq):
                    tot = tot + ds[h * nq + j]
                db_ref[h] += tot
            dsb, pb = ds.astype(BF16), p.astype(BF16)
            dq = jnp.einsum("uqk,ukd->uqd", dsb, k, preferred_element_type=F32) * SCALE
            dk = jnp.einsum("uqk,uqd->ukd", dsb, q, preferred_element_type=F32) * SCALE
            dv = jnp.einsum("uqk,uqd->ukd", pb, do, preferred_element_type=F32)
            for h in range(4):
                sl = slice(h * HD, (h + 1) * HD)
                u0, last = h * nq, h * nq + nq - 1
                sq[slot, 0:QB, sl] = carry[:, sl].astype(BF16)
                sk[slot, 0:QB, sl] = (carry[:, GW + h * HD:GW + (h + 1) * HD] + dk[u0, :QB]).astype(BF16)
                sv[slot, 0:QB, sl] = (carry[:, 2 * GW + h * HD:2 * GW + (h + 1) * HD] + dv[u0, :QB]).astype(BF16)
                for j in range(nq - 1):
                    pos = slice((j + 1) * QB, (j + 2) * QB)
                    sq[slot, pos, sl] = dq[u0 + j].astype(BF16)
                    sk[slot, pos, sl] = (dk[u0 + j, QB:] + dk[u0 + j + 1, :QB]).astype(BF16)
                    sv[slot, pos, sl] = (dv[u0 + j, QB:] + dv[u0 + j + 1, :QB]).astype(BF16)
                carry[:, sl] = dq[last]
                carry[:, GW + h * HD:GW + (h + 1) * HD] = dk[last, QB:]
                carry[:, 2 * GW + h * HD:2 * GW + (h + 1) * HD] = dv[last, QB:]

            @pl.when(n == 0)
            def _():
                write(b * nsb * nq, 1, nq - 1)

            @pl.when(n > 0)
            def _():
                write((b * nsb + n) * nq - 1, 0, nq)

    def row(b, n):
        return b * nsb + jnp.minimum(n, nsb - 1)

    def prev(b, n):
        return jnp.maximum(row(b, n) * nq - 1, 0)

    in_specs = [
        pl.BlockSpec((rows, GW), lambda b, n: (row(b, n), CB_Q)),
        pl.BlockSpec((rows, GW), lambda b, n: (row(b, n), CB_K)),
        pl.BlockSpec((rows, GW), lambda b, n: (row(b, n), CB_V)),
        pl.BlockSpec((rows, GW), lambda b, n: (row(b, n), 0)),
        pl.BlockSpec((rows, 128), lambda b, n: (row(b, n), 0)),
        pl.BlockSpec((QB, GW), lambda b, n: (prev(b, n), CB_K)),
        pl.BlockSpec((QB, GW), lambda b, n: (prev(b, n), CB_V)),
        pl.BlockSpec((None, 2, 4, QB, 2 * QB), lambda b, n: (0, 0, 0, 0, 0)),
        pl.BlockSpec(memory_space=pl.ANY),
    ]
    return pl.pallas_call(
        body, name="attn_bwd0",
        grid=(BL, nsb + 1),
        in_specs=in_specs,
        out_specs=(pl.BlockSpec(memory_space=pl.ANY),
                   pl.BlockSpec((4, QB, 2 * QB), lambda b, n: (0, 0, 0))),
        out_shape=(jax.ShapeDtypeStruct((T, NCOL), BF16),
                   jax.ShapeDtypeStruct((4, QB, 2 * QB), F32)),
        scratch_shapes=[pltpu.VMEM((2, rows, GW), BF16)] * 3
                       + [pltpu.VMEM((QB, 3 * GW), F32), pltpu.SemaphoreType.DMA((2, nq, 3))],
        input_output_aliases={8: 0},
        compiler_params=pltpu.CompilerParams(vmem_limit_bytes=VMEM_LIMIT),
    )(proj, proj, proj, d_out, stats, proj, proj, bias, dproj)


def _attention_forward(proj, rel_bias):
    expand_lanes, grad_lanes, masks = (jnp.asarray(t) for t in _bucket_maps())
    bias = _bias_expand(rel_bias, expand_lanes, masks)
    fwd = [_attn_fwd_dense(proj, bias)] + [_attn_fwd(proj, bias, g) for g in (1, 2)]
    return bias, grad_lanes, [f[0] for f in fwd], [f[1] for f in fwd]


def _local_step(x2, tgt2, mod3, h, proj, attn, w_ao, w_co, w_o, conv_w, conv_b, ln_g, ln_b):
    bias, buckets, o_g, lse_g = attn

    (dproj, dyc, d_o, stats, dxd, gw_o, gw_co, gw_ao, tail_vec) = _tail(
        x2, tgt2, mod3, o_g, lse_g, proj, w_ao, w_co, w_o, conv_w, conv_b, ln_g, ln_b)

    dproj, db = _attn_bwd_dense(proj, d_o, stats, bias, dproj)
    dbias = [db]
    for g in (1, 2):
        dproj, db = _attn_bwd(proj, d_o, stats, bias, dproj, g)
        dbias.append(db)
    g_rel_bias = _bias_grad(*dbias, buckets)
    dproj, conv_vec = _conv_bwd(dyc, proj, conv_w, dproj)

    gw_ao = jnp.transpose(gw_ao.reshape(GW, N_DEV, D // N_DEV), (1, 0, 2))
    return dproj, dxd, gw_ao, gw_co, gw_o, conv_vec, g_rel_bias, tail_vec


def kernel(x, c, w_ada, b_ada, w_in, conv_w, conv_b, rel_bias, w_attn_out, w_conv_out, w_o, ln_g, ln_b, loss_target, m_w_ada, m_b_ada, m_w_in, m_conv_w, m_conv_b, m_rel_bias, m_w_attn_out, m_w_conv_out, m_w_o, m_ln_g, m_ln_b, v_w_ada, v_b_ada, v_w_in, v_conv_w, v_conv_b, v_rel_bias, v_w_attn_out, v_w_conv_out, v_w_o, v_ln_g, v_ln_b):
    me = _my_index()
    x2 = x.reshape(T, D)
    tgt2 = loss_target.reshape(T, D)

    b_cols = lax.dynamic_slice(b_ada, (0, me * ADA_SHARD), (1, ADA_SHARD))
    c_g, mod_in = _mod_exchange(jnp.pad(c, ((0, 8 - BL), (0, 0))), w_ada[0], b_cols)
    c_all = c_g[:, 0:BL, :].reshape(N_DEV * BL, D)
    mod3 = jnp.transpose(mod_in[:, 0:BL, :], (1, 0, 2)).reshape(BL, 3, D)

    h = _prep_h(x2, mod3)
    rows_shape = jax.ShapeDtypeStruct((N_DEV, D // N_DEV, D), BF16)
    proj, w_in_all, (w_ao_g, w_co_g, w_o_g, conv_w_g) = _gather_proj(
        _shard_order(), h, w_in[0].astype(BF16), 2048,
        ([w_attn_out[0].astype(BF16), w_conv_out[0].astype(BF16), w_o[0].astype(BF16), conv_w[0]],
         [jax.ShapeDtypeStruct((N_DEV, GW, D // N_DEV), BF16), rows_shape, rows_shape,
          jax.ShapeDtypeStruct((N_DEV, 3, D // N_DEV), F32)]))

    attn = _attention_forward(proj, rel_bias)
    w_ao_full = jnp.transpose(w_ao_g, (1, 0, 2)).reshape(GW, D)
    w_co_full = w_co_g.reshape(D, D)
    w_o_full = w_o_g.reshape(D, D)
    conv_w_full = jnp.transpose(conv_w_g, (1, 0, 2)).reshape(3, D)

    (dproj, dxd, gw_ao, gw_co, gw_o, conv_vec, g_rel_bias, tail_vec) = _local_step(
        x2, tgt2, mod3, h, proj, attn, w_ao_full, w_co_full, w_o_full,
        conv_w_full, conv_b, ln_g, ln_b)

    g_conv_w_blocks = jnp.transpose(conv_vec[0:3].reshape(3, N_DEV, D // N_DEV), (1, 0, 2))
    partials = [gw_ao, gw_co.reshape(N_DEV, D // N_DEV, D), gw_o.reshape(N_DEV, D // N_DEV, D), g_conv_w_blocks]
    w_in_sums, w_in_parts, sib = _gw_in_pair(
        _slice_order(), h, dproj, partials,
        [jax.ShapeDtypeStruct((4, GW, D // N_DEV), BF16),
         jax.ShapeDtypeStruct((4, D // N_DEV, D), BF16),
         jax.ShapeDtypeStruct((4, D // N_DEV, D), BF16),
         jax.ShapeDtypeStruct((4, 3, D // N_DEV), F32)])
    core = lax.axis_index("c").astype(jnp.int32).reshape(1)
    chip_sums = [w_in_sums] + list(_pair_add(core, partials, sib))
    hops = [(3,)] + [(1, 2, 3)] * 4
    grad_x, mod_vec, (r_in, r_ao, r_co, r_o, r_cw) = _dh_dx(
        dproj, w_in_all, x2, dxd, mod3, chip_sums, hops, w_in_parts)

    small = jnp.concatenate([
        tail_vec[0:4],
        jnp.pad(g_rel_bias.reshape(1, N_BUCKETS * N_HEADS), ((0, 0), (0, D - N_BUCKETS * N_HEADS))),
        jnp.zeros((3, D), F32)], axis=0)
    dmod = jnp.concatenate([mod_vec[0:2], mod_vec[2:4], tail_vec[4:6]], axis=1)
    small_g, dmod_g = _all_gather(
        [small, dmod],
        [jax.ShapeDtypeStruct((N_DEV, 8, D), F32), jax.ShapeDtypeStruct((N_DEV, BL, 3 * D), F32)],
        "gather_small")
    dmod_all = dmod_g.reshape(N_DEV * BL, 3 * D)
    small_names = ["b_ada", "conv_b", "rel_bias", "ln_g", "ln_b"]
    small_params = [(b_ada, m_b_ada, v_b_ada), (conv_b, m_conv_b, v_conv_b), (rel_bias, m_rel_bias, v_rel_bias),
                    (ln_g, m_ln_g, v_ln_g), (ln_b, m_ln_b, v_ln_b)]
    loss, small_res = _small_updates(
        small_g, dmod_all, small_g[:, 4, :N_BUCKETS * N_HEADS].reshape(N_DEV, N_BUCKETS, N_HEADS), small_params)
    loss = loss.reshape(())

    dmod_cols = lax.dynamic_slice(dmod_all, (0, me * ADA_SHARD), (N_DEV * BL, ADA_SHARD))
    res = {
        "w_ada": tuple(t[None] for t in _w_ada_update(jnp.transpose(c_all), dmod_cols,
                                                      w_ada[0], m_w_ada[0], v_w_ada[0])),
        "w_in": tuple(t[None] for t in _adamw(r_in, w_in[0], m_w_in[0], v_w_in[0], "adam_w_in", 128)),
    }
    mid_names = ["conv_w", "w_attn_out", "w_conv_out", "w_o"]
    mid_parts = [r_cw, r_ao, r_co, r_o]
    mid_full = [(conv_w, m_conv_w, v_conv_w), (w_attn_out, m_w_attn_out, v_w_attn_out),
                (w_conv_out, m_w_conv_out, v_w_conv_out), (w_o, m_w_o, v_w_o)]
    mid_res = _multi_adamw(mid_parts, [tuple(t[0] for t in wmv) for wmv in mid_full], "adam_mid")
    for nm, wmv, outs4 in zip(mid_names, mid_full, mid_res):
        res[nm] = tuple(t[None] for t in outs4)
    res.update(dict(zip(small_names, small_res)))
    order = ["w_ada", "b_ada", "w_in", "conv_w", "conv_b", "rel_bias", "w_attn_out", "w_conv_out",
             "w_o", "ln_g", "ln_b"]
    outs = [loss, grad_x.reshape(BL, S, D)]
    for k in range(4):
        outs += [res[name][k] for name in order]
    return tuple(outs)
```

```python
import math

import numpy as np
import jax
import jax.numpy as jnp
from jax import lax
from jax.experimental import pallas as pl
from jax.experimental.pallas import tpu as pltpu

F32 = jnp.float32
BF16 = jnp.bfloat16
MESH = pl.DeviceIdType.MESH

N_DEV = 8
D = 1024
S = 2048
BL = 2
T = BL * S
NCOL = 11264
SHARD = NCOL // N_DEV
CB = 512
NCB = NCOL // CB
HD = 128
GW = 512
QB = 128
DILATIONS = (1, 4, 16)
N_STEPS = 128
N_BUCKETS = 32
N_HEADS = 12
ALPHA = 2.0 ** 0.25
LN_EPS = 1e-5
NEG_INF = -1e30
SCALE = HD ** -0.5
ADA_SHARD = 3 * D // N_DEV

CB_Q, CB_K, CB_V, CB_GA = 0, 3, 6, 9
KB_U, KB_BG, KB_CG, KB_GC, KB_MA, KB_MC = 5, 6, 7, 8, 9, 10

ADAM_LR, ADAM_B1, ADAM_B2, ADAM_EPS, ADAM_WD, ADAM_STEP = 0.001, 0.9, 0.999, 1e-08, 0.01, 10

VMEM_LIMIT = 56 * 1024 * 1024
VMEM_LIMIT_TAIL = 62 * 1024 * 1024


def _dot(a, b):
    return jnp.dot(a, b, preferred_element_type=F32)


def _dot_nt(a, b):
    return lax.dot_general(a, b, (((1,), (1,)), ((), ())), preferred_element_type=F32)


def _dot_tn(a, b):
    return lax.dot_general(a, b, (((0,), (0,)), ((), ())), preferred_element_type=F32)


def _sigmoid(v):
    return 1.0 / (1.0 + jnp.exp(-v))


def _column_copies(pieces, dst_hbm, row0, sems):
    copies = []
    for k, (src, col0) in enumerate(pieces):
        rows, width = src.shape
        copies.append(pltpu.make_async_copy(
            src, dst_hbm.at[pl.ds(row0, rows), pl.ds(col0, width)], sems.at[k]))
    return copies


def _my_index():
    return 4 * lax.axis_index("x") + 2 * lax.axis_index("y") + lax.axis_index("c")


class _Gather:
    def __init__(self, ins, outs, stage, send_sems, recv_sems, local_sems):
        self.ins, self.outs, self.stage = ins, outs, stage
        self.send_sems, self.recv_sems, self.local_sems = send_sems, recv_sems, local_sems
        x, y, c = lax.axis_index("x"), lax.axis_index("y"), lax.axis_index("c")
        self.c = c
        self.me, self.sibling = (x, y, c), (x, y, 1 - c)
        self.chips = [(1 - x, y), (x, 1 - y), (1 - x, 1 - y)]

    @staticmethod
    def scratch(arrs):
        n = len(arrs)
        return ([pltpu.SemaphoreType.DMA((7 * n,)), pltpu.SemaphoreType.DMA((7 * n,)),
                 pltpu.SemaphoreType.DMA((n,))] + [pltpu.VMEM(a.shape, a.dtype) for a in arrs])

    def _copy(self, a, k, block, to, src=None):
        dst = self.outs[a].at[4 * block[0] + 2 * block[1] + block[2]]
        return pltpu.make_async_remote_copy(
            src_ref=dst if src is None else src, dst_ref=dst,
            send_sem=self.send_sems.at[a * 7 + k], recv_sem=self.recv_sems.at[a * 7 + k],
            device_id=to, device_id_type=MESH)

    def _first(self):
        first = []
        for a in range(len(self.ins)):
            first.append(self._copy(a, 0, self.me, self.sibling, src=self.ins[a]))
            first += [self._copy(a, 1 + j, self.me, (*chip, self.c), src=self.ins[a])
                      for j, chip in enumerate(self.chips)]
        return first

    def _mine(self):
        me = self.me
        return [pltpu.make_async_copy(self.stage[a], self.outs[a].at[4 * me[0] + 2 * me[1] + me[2]],
                                      self.local_sems.at[a]) for a in range(len(self.ins))]

    def begin(self):
        for cp in self._first():
            cp.start()
        loads = [pltpu.make_async_copy(self.ins[a], self.stage[a], self.local_sems.at[a])
                 for a in range(len(self.ins))]
        for cp in loads:
            cp.start()
        for cp in loads:
            cp.wait()
        for cp in self._mine():
            cp.start()

    def finish(self):
        n, c, me, sibling = len(self.ins), self.c, self.me, self.sibling
        passed = []
        for j, chip in enumerate(self.chips):
            for a in range(n):
                self._copy(a, 1 + j, (*chip, c), me).wait_recv()
                fwd = self._copy(a, 4 + j, (*chip, c), sibling)
                fwd.start()
                passed.append(fwd)
        for a in range(n):
            self._copy(a, 0, sibling, me).wait_recv()
        for j, chip in enumerate(self.chips):
            for a in range(n):
                self._copy(a, 4 + j, (*chip, 1 - c), me).wait_recv()
        for cp in self._first() + passed:
            cp.wait_send()
        for cp in self._mine():
            cp.wait()


def _all_gather(arrs, out_shapes, name):
    n = len(arrs)

    def body(*refs):
        g = _Gather(refs[:n], refs[n:2 * n], refs[2 * n + 3:], *refs[2 * n:2 * n + 3])
        g.begin()
        g.finish()

    any_spec = pl.BlockSpec(memory_space=pl.ANY)
    return pl.pallas_call(
        body, name=name,
        out_shape=tuple(out_shapes),
        in_specs=[any_spec] * n,
        out_specs=tuple([any_spec] * n),
        scratch_shapes=_Gather.scratch(arrs),
    )(*arrs)


def _neighbour_chips():
    x, y, c = lax.axis_index("x"), lax.axis_index("y"), lax.axis_index("c")
    first = (jnp.where(c == 0, 1 - x, x), jnp.where(c == 0, y, 1 - y))
    second = (jnp.where(c == 0, x, 1 - x), jnp.where(c == 0, 1 - y, y))
    return first, second, (1 - x, 1 - y)


def _slice_order():
    x, y, c = lax.axis_index("x"), lax.axis_index("y"), lax.axis_index("c")
    nb1, nb2, diag = _neighbour_chips()
    slots = []
    for mine, theirs in ((nb1, nb2), (nb2, nb1), (diag, diag), ((x, y), (x, y))):
        slots += [2 * (2 * theirs[0] + theirs[1]) + 1 - c, 2 * (2 * mine[0] + mine[1]) + c]
    return jnp.stack(slots).astype(jnp.int32)


def _gw_in_pair(order, h, dproj, smalls, small_shapes4):
    kk, m = h.shape
    tk = min(kk, 2048)
    nk = kk // tk
    ncols = dproj.shape[1] // N_DEV
    n = len(smalls)

    def body(order_ref, h_ref, d_ref, *rest):
        ins = rest[:n]
        sums_hbm, parts_hbm = rest[n], rest[n + 1]
        sib = rest[n + 2:2 * n + 2]
        (acc, sendbuf, recvbuf, sumbuf, send_sems, recv_sems, local_sem, ssend, srecv,
         isend, irecv) = rest[2 * n + 2:]
        js, k = pl.program_id(0), pl.program_id(1)
        x, y, c = lax.axis_index("x"), lax.axis_index("y"), lax.axis_index("c")
        sibling = (x, y, 1 - c)
        my_chip = 2 * x + y
        nb1, nb2, _ = _neighbour_chips()
        near = [(*nb1, c), (*nb2, c)]

        def ici_copy(p, out_chip):
            peer = near[p] if isinstance(p, int) else tuple(jnp.where(p == 0, a, b) for a, b in zip(*near))
            return pltpu.make_async_remote_copy(
                src_ref=sumbuf.at[p], dst_ref=parts_hbm.at[out_chip],
                send_sem=isend.at[p], recv_sem=irecv.at[p], device_id=peer, device_id_type=MESH)

        def small_copies():
            return [pltpu.make_async_remote_copy(
                        src_ref=ins[a].at[2 * q + 1 - c], dst_ref=sib[a].at[q],
                        send_sem=ssend.at[a * 4 + q], recv_sem=srecv.at[a * 4 + q],
                        device_id=sibling, device_id_type=MESH)
                    for a in range(n) for q in range(4)]

        def slice_copy(p):
            return pltpu.make_async_remote_copy(
                src_ref=sendbuf, dst_ref=recvbuf.at[p], send_sem=send_sems.at[p], recv_sem=recv_sems.at[p],
                device_id=sibling, device_id_type=MESH)

        def sum_copy(p):
            return pltpu.make_async_copy(sumbuf.at[2], sums_hbm.at[order_ref[2 * p] // 2], local_sem)

        @pl.when((js == 0) & (k == 0))
        def _():
            for cp in small_copies():
                cp.start()

        def partial():
            return _dot_tn(h_ref[...], d_ref[...])

        if nk > 1:
            @pl.when(k == 0)
            def _():
                acc[...] = partial()
        if nk > 2:
            @pl.when((k > 0) & (k < nk - 1))
            def _():
                acc[...] += partial()

        def total():
            return partial() + acc[...] if nk > 1 else partial()

        p = js // 2

        @pl.when((js % 2 == 0) & (k == nk - 1))
        def _():
            @pl.when(p > 0)
            def _():
                slice_copy(p - 1).wait_send()
            sendbuf[...] = total().astype(BF16)
            slice_copy(p).start()

        @pl.when((js % 2 == 1) & (k == nk - 1))
        def _():
            slice_copy(p).wait_recv()

            @pl.when(p == 3)
            def _():
                sum_copy(2).wait()
            sumbuf[jnp.minimum(p, 2)] = (total() + recvbuf[p].astype(F32)).astype(BF16)

            @pl.when(p < 2)
            def _():
                ici_copy(p, my_chip).start()

            @pl.when(p >= 2)
            def _():
                sum_copy(p).start()

        @pl.when((js == N_DEV - 1) & (k == nk - 1))
        def _():
            slice_copy(3).wait_send()
            sum_copy(3).wait()
            for cp in small_copies():
                cp.wait()
            for p in range(2):
                ici_copy(p, 2 * near[p][0] + near[p][1]).wait_recv()
                ici_copy(p, my_chip).wait_send()

    any_spec = pl.BlockSpec(memory_space=pl.ANY)
    res = pl.pallas_call(
        body, name="gw_in_pair",
        grid_spec=pltpu.PrefetchScalarGridSpec(
            num_scalar_prefetch=1,
            grid=(N_DEV, nk),
            in_specs=[pl.BlockSpec((tk, m), lambda js, k, order_ref: (k, 0)),
                      pl.BlockSpec((tk, ncols), lambda js, k, order_ref: (k, order_ref[js]))] + [any_spec] * n,
            out_specs=(any_spec,) * (n + 2),
            scratch_shapes=[pltpu.VMEM((m, ncols), F32), pltpu.VMEM((m, ncols), BF16),
                            pltpu.VMEM((4, m, ncols), BF16), pltpu.VMEM((3, m, ncols), BF16),
                            pltpu.SemaphoreType.DMA((4,)), pltpu.SemaphoreType.DMA((4,)),
                            pltpu.SemaphoreType.DMA,
                            pltpu.SemaphoreType.DMA((4 * n,)), pltpu.SemaphoreType.DMA((4 * n,)),
                            pltpu.SemaphoreType.DMA((2,)), pltpu.SemaphoreType.DMA((2,))]),
        out_shape=(jax.ShapeDtypeStruct((4, m, ncols), BF16),) * 2 + tuple(small_shapes4),
        compiler_params=pltpu.CompilerParams(vmem_limit_bytes=VMEM_LIMIT),
    )(order, h, dproj, *smalls)
    return res[0], res[1], res[2:]


def _chip_copies(ins, outs, send_sems, recv_sems, local_sems, hops):
    n = len(ins)
    x, y, c = lax.axis_index("x"), lax.axis_index("y"), lax.axis_index("c")
    my_chip = 2 * x + y

    def peer_of(k):
        return ((1 - x) if (k >> 1) & 1 else x, (1 - y) if k & 1 else y, c)

    def copy(a, k, out_chip):
        peer = peer_of(k)
        return pltpu.make_async_remote_copy(
            src_ref=ins[a].at[2 * peer[0] + peer[1]], dst_ref=outs[a].at[out_chip],
            send_sem=send_sems.at[a * 3 + k - 1], recv_sem=recv_sems.at[a * 3 + k - 1],
            device_id=peer, device_id_type=MESH)

    sends = [copy(a, k, my_chip) for k in range(1, 4) for a in range(n) if k in hops[a]]
    arrivals = []
    for k in range(1, 4):
        peer = peer_of(k)
        arrivals += [copy(a, k, 2 * peer[0] + peer[1]) for a in range(n) if k in hops[a]]
    mine = [pltpu.make_async_copy(ins[a].at[my_chip], outs[a].at[my_chip], local_sems.at[a])
            for a in range(n)]
    return sends, arrivals, mine


def _pair_add(core, mines, theirs):
    n = len(mines)

    def body(core_ref, *refs):
        mine, sib, outs = refs[:n], refs[n:2 * n], refs[2 * n:]
        for a in range(n):
            for q in range(4):
                outs[a][q] = (mine[a][2 * q + core_ref[0]].astype(F32)
                              + sib[a][q].astype(F32)).astype(outs[a].dtype)

    return pl.pallas_call(
        body, name="pair_add",
        in_specs=[pl.BlockSpec(memory_space=pltpu.SMEM)] + [pl.BlockSpec(memory_space=pltpu.VMEM)] * (2 * n),
        out_shape=tuple(jax.ShapeDtypeStruct(t.shape, t.dtype) for t in theirs),
    )(core, *mines, *theirs)


def _mod_exchange(c8, w_ada, b_cols):
    cols = w_ada.shape[1]

    def body(c_ref, w_ref, b_ref, call_ref, mod_ref, msend, send1, recv1, send2, recv2):
        x, y, c = lax.axis_index("x"), lax.axis_index("y"), lax.axis_index("c")
        my_slot = 4 * x + 2 * y + c

        def peer_of(k):
            return ((1 - x) if (k >> 2) & 1 else x, (1 - y) if (k >> 1) & 1 else y, (1 - c) if k & 1 else c)

        def slot_of(dev):
            return 4 * dev[0] + 2 * dev[1] + dev[2]

        def exchange(src_of, dst_ref, send_sems, recv_sems):
            sends, arrivals = [], []
            for k in range(1, 8):
                peer = peer_of(k)
                sends.append(pltpu.make_async_remote_copy(
                    src_ref=src_of(slot_of(peer)), dst_ref=dst_ref.at[my_slot],
                    send_sem=send_sems.at[k - 1], recv_sem=recv_sems.at[k - 1],
                    device_id=peer, device_id_type=MESH))
                arrivals.append(pltpu.make_async_remote_copy(
                    src_ref=src_of(my_slot), dst_ref=dst_ref.at[slot_of(peer)],
                    send_sem=send_sems.at[k - 1], recv_sem=recv_sems.at[k - 1],
                    device_id=peer, device_id_type=MESH))
            for cp in sends:
                cp.start()
            for cp in arrivals:
                cp.wait_recv()
            for cp in sends:
                cp.wait_send()

        call_ref[my_slot] = c_ref[...]
        exchange(lambda s: c_ref, call_ref, send1, recv1)
        cv = call_ref[...].reshape(N_DEV * 8, c_ref.shape[1])
        act = cv * _sigmoid(cv)
        mod = jnp.dot(act, w_ref[...], preferred_element_type=F32,
                      precision=lax.Precision.HIGHEST) + b_ref[...]
        msend[...] = mod.reshape(N_DEV, 8, cols)
        mod_ref[my_slot] = msend[my_slot]
        exchange(lambda s: msend.at[s], mod_ref, send2, recv2)

    return pl.pallas_call(
        body, name="mod_exchange",
        out_shape=(jax.ShapeDtypeStruct((N_DEV, 8, c8.shape[1]), F32),
                   jax.ShapeDtypeStruct((N_DEV, 8, cols), F32)),
        scratch_shapes=[pltpu.VMEM((N_DEV, 8, cols), F32)] + [pltpu.SemaphoreType.DMA((7,))] * 4,
    )(c8, w_ada, b_cols)


def _w_ada_update(c_all_t, dmod_cols, w, m, v):
    rows, cols = w.shape
    tr = 256

    def body(c_ref, d_ref, w_ref, m_ref, v_ref, g_ref, dl_ref, nm_ref, nv_ref):
        cv = c_ref[...]
        g = jnp.dot(cv * _sigmoid(cv), d_ref[...], preferred_element_type=F32,
                    precision=lax.Precision.HIGHEST)
        g_ref[...] = g
        dl_ref[...], nm_ref[...], nv_ref[...] = _adam_step(g, w_ref[...], m_ref[...], v_ref[...])

    blk = pl.BlockSpec((tr, cols), lambda i: (i, 0))
    shp = jax.ShapeDtypeStruct((rows, cols), F32)
    return pl.pallas_call(
        body, name="adam_w_ada",
        grid=(rows // tr,),
        in_specs=[pl.BlockSpec((tr, c_all_t.shape[1]), lambda i: (i, 0)),
                  pl.BlockSpec(dmod_cols.shape, lambda i: (0, 0)), blk, blk, blk],
        out_specs=(blk, blk, blk, blk),
        out_shape=(shp, shp, shp, shp),
    )(c_all_t, dmod_cols, w, m, v)


def _shard_order():
    x, y, c = lax.axis_index("x"), lax.axis_index("y"), lax.axis_index("c")
    first, second, diag = _neighbour_chips()
    devs = [(x, y, c), (x, y, 1 - c), (*first, c), (*second, 1 - c), (*second, c), (*first, 1 - c),
            (*diag, c), (*diag, 1 - c)]
    return jnp.stack([4 * d[0] + 2 * d[1] + d[2] for d in devs]).astype(jnp.int32)


def _gather_proj(order, x2, mod3, w_shard, tm, ride=()):
    rows, kdim = x2.shape
    ncols = w_shard.shape[1]
    n_i = rows // tm
    per_seq = n_i // mod3.shape[0]
    ride_arrs, ride_shapes = ride if ride else ((), ())
    n_ride = len(ride_arrs)

    def body(order_ref, x_ref, mod_ref, mine_hbm, *rest):
        ride_ins = rest[:n_ride]
        o_ref, h_ref, all_hbm = rest[n_ride:n_ride + 3]
        ride_outs = rest[n_ride + 3:2 * n_ride + 3]
        wv, send_sems, recv_sems, local_sems, hs = rest[2 * n_ride + 3:2 * n_ride + 8]
        ride_scr = rest[2 * n_ride + 8:]
        j, i = pl.program_id(0), pl.program_id(1)
        c = lax.axis_index("c")
        me, sibling = (lax.axis_index("x"), lax.axis_index("y"), c), (lax.axis_index("x"), lax.axis_index("y"), 1 - c)
        nb1, nb2, diag = _neighbour_chips()

        def slot(dev):
            return 4 * dev[0] + 2 * dev[1] + dev[2]

        def copy(k, block, to, src=None, part=None):
            buf = wv.at[slot(block)]
            if part is not None:
                buf = buf.at[pl.ds(pl.multiple_of(part * (kdim // 2), kdim // 2), kdim // 2)]
            return pltpu.make_async_remote_copy(
                src_ref=buf if src is None else src, dst_ref=buf,
                send_sem=send_sems.at[k], recv_sem=recv_sems.at[k],
                device_id=to, device_id_type=MESH)

        def keep(step, block):
            s = slot(block)
            cols = pl.ds(pl.multiple_of((s % 2) * ncols, 128), ncols)
            return pltpu.make_async_copy(wv.at[s], all_hbm.at[s // 2, :, cols], local_sems.at[step])

        if n_ride:
            gather = _Gather(ride_ins, ride_outs, ride_scr[3:], *ride_scr[:3])
        to_sibling, to_nb1, to_nb2 = (copy(0, me, sibling, mine_hbm), copy(1, me, (*nb1, c), mine_hbm),
                                      copy(2, me, (*nb2, c), mine_hbm))
        relay1, relay2 = copy(3, (*nb2, c), (*nb1, c), part=c), copy(4, (*nb1, c), (*nb2, c), part=1 - c)
        pass_nb1, pass_nb2 = copy(5, (*nb1, c), sibling), copy(6, (*nb2, c), sibling)
        pass_d1, pass_d2 = copy(7, (*diag, c), sibling, part=c), copy(8, (*diag, c), sibling, part=1 - c)
        sends = [to_sibling, to_nb1, to_nb2, relay1, relay2, pass_nb1, pass_nb2, pass_d1, pass_d2]
        due = [
            (me, [], []),
            (sibling, [copy(0, sibling, me)], [[]]),
            ((*nb1, c), [copy(1, (*nb1, c), me)], [[pass_nb1, to_nb2]]),
            ((*nb2, 1 - c), [copy(5, (*nb2, 1 - c), me)], [[]]),
            ((*nb2, c), [copy(2, (*nb2, c), me)], [[pass_nb2, relay1, relay2]]),
            ((*nb1, 1 - c), [copy(6, (*nb1, 1 - c), me)], [[]]),
            ((*diag, c), [copy(3, (*diag, c), me, part=c), copy(4, (*diag, c), me, part=1 - c)],
             [[pass_d1], [pass_d2]]),
            ((*diag, 1 - c), [copy(7, (*diag, 1 - c), me, part=1 - c), copy(8, (*diag, 1 - c), me, part=c)],
             [[], []]),
        ]

        @pl.when((j == 0) & (i == 0))
        def _():
            to_sibling.start()
            to_nb1.start()
            load = pltpu.make_async_copy(mine_hbm, wv.at[slot(me)], local_sems.at[N_DEV])
            load.start()
            load.wait()
            keep(0, me).start()

        for step in range(1, N_DEV):
            block, arrivals, then = due[step]

            @pl.when((j == step) & (i == 0))
            def _():
                for arrival, follow in zip(arrivals, then):
                    arrival.wait_recv()
                    for cp in follow:
                        cp.start()
                keep(step, block).start()
                if n_ride and step == N_DEV - 2:
                    gather.begin()

        @pl.when(j == 0)
        def _():
            hb = (x_ref[...] * (1.0 + mod_ref[0, 1:2, :]) + mod_ref[0, 0:1, :]).astype(BF16)
            hs[i] = hb
            h_ref[...] = hb

        o_ref[...] = _dot(hs[i], wv[order_ref[j]]).astype(BF16)

        @pl.when((j == N_DEV - 1) & (i == n_i - 1))
        def _():
            for cp in sends:
                cp.wait_send()
            for step in range(N_DEV):
                keep(step, due[step][0]).wait()
            if n_ride:
                gather.finish()

    def first_pass(j, i):
        return jnp.where(j == 0, i, n_i - 1)

    any_spec = pl.BlockSpec(memory_space=pl.ANY)
    res = pl.pallas_call(
        body, name="gather_proj",
        grid_spec=pltpu.PrefetchScalarGridSpec(
            num_scalar_prefetch=1,
            grid=(N_DEV, n_i),
            in_specs=[pl.BlockSpec((tm, kdim), lambda j, i, order_ref: (first_pass(j, i), 0)),
                      pl.BlockSpec((1, 3, kdim), lambda j, i, order_ref: (first_pass(j, i) // per_seq, 0, 0)),
                      any_spec] + [any_spec] * n_ride,
            out_specs=(pl.BlockSpec((tm, ncols), lambda j, i, order_ref: (i, order_ref[j])),
                       pl.BlockSpec((tm, kdim), lambda j, i, order_ref: (first_pass(j, i), 0)), any_spec)
                      + (any_spec,) * n_ride,
            scratch_shapes=[pltpu.VMEM((N_DEV, kdim, ncols), BF16),
                            pltpu.SemaphoreType.DMA((9,)), pltpu.SemaphoreType.DMA((9,)),
                            pltpu.SemaphoreType.DMA((N_DEV + 1,)), pltpu.VMEM((n_i, tm, kdim), BF16)]
                           + (_Gather.scratch(ride_arrs) if n_ride else [])),
        out_shape=(jax.ShapeDtypeStruct((rows, N_DEV * ncols), BF16),
                   jax.ShapeDtypeStruct((rows, kdim), BF16),
                   jax.ShapeDtypeStruct((N_DEV // 2, kdim, 2 * ncols), BF16)) + tuple(ride_shapes),
        compiler_params=pltpu.CompilerParams(vmem_limit_bytes=VMEM_LIMIT),
    )(order, x2, mod3, w_shard, *ride_arrs)
    return res[0], res[1], res[2], res[3:]


SKEW_W = 512


def _bucket_maps():
    lanes = np.arange(SKEW_W)

    def buckets_of(steps):
        rows = []
        for dil in DILATIONS:
            dist = np.maximum(steps, 0) * dil
            nf = np.maximum(dist, 1).astype(np.float32)
            large = 16 + (np.log(nf / np.float32(16)) / np.float32(math.log(128.0))
                          * np.float32(16)).astype(np.int32)
            large = np.minimum(large, N_BUCKETS - 1)
            bucket = np.where(dist < 16, dist, large)
            rows.append(np.where((steps >= 0) & (steps <= N_STEPS), bucket, -1).astype(np.int32))
        return np.stack(rows)[:, None, :]

    a = np.arange(QB)[:, None]
    b = np.arange(2 * QB)[None, :]
    steps = a + QB - b
    band = (steps >= 0) & (steps <= N_STEPS)
    first = band & (b >= QB)
    masks = np.stack([first, band]).astype(np.int32)
    return buckets_of(QB - lanes), buckets_of(2 * QB - 1 - lanes), masks


def _bias_expand(rel_bias, lane_buckets, masks):
    def body(tab_ref, bk_ref, mk_ref, o_ref):
        for g in range(3):
            bk = bk_ref[g]
            for h in range(4):
                col = 4 * g + h
                per_offset = jnp.zeros((1, SKEW_W), F32)
                for k in range(N_BUCKETS):
                    per_offset = jnp.where(bk == k, tab_ref[k, col], per_offset)
                tile = pltpu.roll(jnp.broadcast_to(per_offset, (QB, SKEW_W)), 0, 1, stride=1, stride_axis=0)
                tile = tile[:, :2 * QB]
                o_ref[g, 0, h] = jnp.where(mk_ref[0] != 0, tile, NEG_INF)
                o_ref[g, 1, h] = jnp.where(mk_ref[1] != 0, tile, NEG_INF)

    return pl.pallas_call(
        body, name="bias_expand",
        in_specs=[pl.BlockSpec(memory_space=pltpu.SMEM),
                  pl.BlockSpec(memory_space=pltpu.VMEM),
                  pl.BlockSpec(memory_space=pltpu.VMEM)],
        out_shape=jax.ShapeDtypeStruct((3, 2, 4, QB, 2 * QB), F32),
    )(rel_bias, lane_buckets, masks)


def _bias_grad(ds1, ds2, ds3, lane_buckets):
    exchange = jnp.asarray(np.eye(QB, dtype=np.float32)[::-1].copy())

    def body(d1_ref, d2_ref, d3_ref, bk_ref, ex_ref, o_ref):
        for g, d_ref in enumerate((d1_ref, d2_ref, d3_ref)):
            bk = bk_ref[g]
            for h in range(4):
                flipped = jnp.dot(ex_ref[...], d_ref[h], preferred_element_type=F32,
                                  precision=lax.Precision.HIGHEST)
                padded = jnp.concatenate([flipped, jnp.zeros((QB, SKEW_W - 2 * QB), F32)], axis=1)
                skewed = pltpu.roll(padded, 0, 1, stride=1, stride_axis=0)
                per_offset = jnp.sum(skewed, axis=0, keepdims=True)
                for k in range(N_BUCKETS):
                    o_ref[k, 4 * g + h] = jnp.sum(jnp.where(bk == k, per_offset, 0.0))

    return pl.pallas_call(
        body, name="bias_grad",
        in_specs=[pl.BlockSpec(memory_space=pltpu.VMEM)] * 5,
        out_specs=pl.BlockSpec(memory_space=pltpu.SMEM),
        out_shape=jax.ShapeDtypeStruct((N_BUCKETS, N_HEADS), F32),
    )(ds1, ds2, ds3, lane_buckets, exchange)


def _scratch_sets(rows):
    return 4 if rows <= 512 else 1


def _unit_chunks(dil, size=16):
    units = [(h, r) for h in range(4) for r in range(dil)]
    return [units[i:i + size] for i in range(0, len(units), size)]


def _residue_rows(src_ref, copies, h, residue):
    buf = copies[h % len(copies)]
    buf[...] = src_ref[:, h * HD:(h + 1) * HD].astype(F32)
    return lambda r: buf[residue(r), :].astype(BF16)


def _attn_fwd(proj, bias, g):
    dil = DILATIONS[g]
    rows = QB * dil
    nsb = S // rows
    has_prev = nsb > 1

    def residue(r):
        return pl.ds(r, QB, stride=dil)

    n_sets = _scratch_sets(rows)
    n_in = 6 if has_prev else 4
    n_copied = (4 + (2 if has_prev else 0)) * n_sets

    def body(*refs):
        q_ref, kc_ref, vc_ref = refs[:3]
        kp_ref, vp_ref = refs[3:5] if has_prev else (None, None)
        b_ref = refs[n_in - 1]
        o_ref, l_ref = refs[n_in:n_in + 2]
        scr = list(refs[n_in + 2:])
        ls = [scr.pop(0) for _ in range(4)]
        copies = {name: [scr.pop(0) for _ in range(n_sets)]
                  for name in ("q", "kc", "vc", "o") + (("kp", "vp") if has_prev else ())}
        lane = lax.broadcasted_iota(jnp.int32, (QB, 128), 1)
        refs_of = {"q": q_ref, "kc": kc_ref, "vc": vc_ref, "kp": kp_ref, "vp": vp_ref}
        for chunk in _unit_chunks(dil):
            rows_of = {h: {name: _residue_rows(refs_of[name], copies[name], h, residue)
                           for name in refs_of if refs_of[name] is not None}
                       for h in sorted({h for h, _ in chunk})}

            def batch(name):
                return jnp.stack([rows_of[h][name](r) for h, r in chunk])

            q, k, v = batch("q"), batch("kc"), batch("vc")
            if has_prev:
                k = jnp.concatenate([batch("kp"), k], axis=1)
                v = jnp.concatenate([batch("vp"), v], axis=1)
                bias_b = jnp.stack([b_ref[h] for h, _ in chunk])
            else:
                bias_b = jnp.stack([b_ref[h, :, QB:] for h, _ in chunk])
            s = jnp.einsum("uqd,ukd->uqk", q, k, preferred_element_type=F32) * SCALE + bias_b
            m = jnp.max(s, axis=-1, keepdims=True)
            p = jnp.exp(s - m)
            l = jnp.sum(p, axis=-1, keepdims=True)
            o = jnp.einsum("uqk,ukd->uqd", p.astype(BF16), v, preferred_element_type=F32) / l
            lse = m + jnp.log(l)
            for i, (h, r) in enumerate(chunk):
                copies["o"][h % n_sets][residue(r), :] = o[i]
                ls[h][r * QB:(r + 1) * QB, :] = jnp.where(lane == h, lse[i], 0.0)
            for h in sorted({h for h, _ in chunk}):
                o_ref[:, h * HD:(h + 1) * HD] = copies["o"][h % n_sets][...]
        for r in range(dil):
            blk = slice(r * QB, (r + 1) * QB)
            l_ref[residue(r), :] = (ls[0][blk, :] + ls[1][blk, :]) + (ls[2][blk, :] + ls[3][blk, :])

    def row(b, n):
        return b * nsb + n

    def prev(b, n):
        return b * nsb + jnp.maximum(n - 1, 0)

    in_specs = [
        pl.BlockSpec((rows, GW), lambda b, n: (row(b, n), CB_Q + g)),
        pl.BlockSpec((rows, GW), lambda b, n: (row(b, n), CB_K + g)),
        pl.BlockSpec((rows, GW), lambda b, n: (row(b, n), CB_V + g)),
    ]
    args = [proj, proj, proj]
    scratch = [pltpu.VMEM((rows, 128), F32)] * (4 + n_copied)
    if has_prev:
        in_specs += [pl.BlockSpec((rows, GW), lambda b, n: (prev(b, n), CB_K + g)),
                     pl.BlockSpec((rows, GW), lambda b, n: (prev(b, n), CB_V + g))]
        args += [proj, proj]
    in_specs.append(pl.BlockSpec((None, None, 4, QB, 2 * QB),
                                 lambda b, n: (g, jnp.minimum(n, 1), 0, 0, 0)))
    args.append(bias)
    return pl.pallas_call(
        body, name=f"attn_fwd{g}",
        grid=(BL, nsb),
        in_specs=in_specs,
        out_specs=(pl.BlockSpec((rows, GW), lambda b, n: (row(b, n), 0)),
                   pl.BlockSpec((rows, 128), lambda b, n: (row(b, n), 0))),
        out_shape=(jax.ShapeDtypeStruct((T, GW), F32), jax.ShapeDtypeStruct((T, 128), F32)),
        scratch_shapes=scratch,
        compiler_params=pltpu.CompilerParams(vmem_limit_bytes=VMEM_LIMIT),
    )(*args)


def _attn_bwd(proj, d_out, stats, bias, dproj, g):
    dil = DILATIONS[g]
    rows = QB * dil
    nsb = S // rows
    has_prev = nsb > 1
    n_steps = nsb + 1 if has_prev else 1
    n_in = 7 + (2 if has_prev else 0)

    def residue(r):
        return pl.ds(r, QB, stride=dil)

    n_sets = _scratch_sets(rows)

    def body(*refs):
        q_ref, kc_ref, vc_ref, do_ref, st_ref, b_ref = refs[:6]
        kp_ref, vp_ref = refs[6:8] if has_prev else (None, None)
        out_ref, db_ref = refs[n_in], refs[n_in + 1]
        scr = list(refs[n_in + 2:])
        sq, sk, sv, sems = [scr.pop(0) for _ in range(4)]
        carry = scr.pop(0) if has_prev else None
        sts = scr.pop(0)
        copies = {name: [scr.pop(0) for _ in range(n_sets)]
                  for name in ("q", "kc", "vc", "do", "dq", "dk", "dv") + (("kp", "vp") if has_prev else ())}
        b, n = pl.program_id(0), pl.program_id(1)

        @pl.when((b == 0) & (n == 0))
        def _():
            db_ref[...] = jnp.zeros_like(db_ref)

        def finish(h, r, dq, dk, dv):
            for name, val in (("dq", dq), ("dk", dk), ("dv", dv)):
                copies[name][h % n_sets][residue(r), :] = val

        step = b * n_steps + n
        slot = step % 2

        def stage_copies(s, row0):
            return _column_copies([(sq.at[s], CB * (CB_Q + g)), (sk.at[s], CB * (CB_K + g)),
                                   (sv.at[s], CB * (CB_V + g))], out_ref, row0, sems.at[s])

        @pl.when((step >= 2) & ((step - 2) % n_steps >= (1 if has_prev else 0)))
        def _():
            for cp in stage_copies(slot, 0):
                cp.wait()

        def finish_head(h):
            sl = slice(h * HD, (h + 1) * HD)
            sq[slot, :, sl] = copies["dq"][h % n_sets][...].astype(BF16)
            sk[slot, :, sl] = copies["dk"][h % n_sets][...].astype(BF16)
            sv[slot, :, sl] = copies["dv"][h % n_sets][...].astype(BF16)

        def write_block(blk_idx):
            for cp in stage_copies(slot, pl.multiple_of(blk_idx * rows, rows)):
                cp.start()

            @pl.when(step == BL * n_steps - 1)
            def _():
                for s in range(2):
                    for cp in stage_copies(s, 0):
                        cp.wait()

        def carried(h, r):
            blk = slice(r * QB, (r + 1) * QB)
            return ((blk, slice(h * HD, (h + 1) * HD)), (blk, slice(GW + h * HD, GW + (h + 1) * HD)),
                    (blk, slice(2 * GW + h * HD, 2 * GW + (h + 1) * HD)))

        if has_prev:
            @pl.when(n == 0)
            def _():
                carry[...] = jnp.zeros_like(carry)

            @pl.when(n == nsb)
            def _():
                for h in range(4):
                    for r in range(dil):
                        cq, ck, cv = carried(h, r)
                        finish(h, r, carry[cq], carry[ck], carry[cv])
                    finish_head(h)
                write_block(b * nsb + nsb - 1)

        @pl.when(n < nsb)
        def _():
            for r in range(dil):
                sts[r * QB:(r + 1) * QB, :] = st_ref[residue(r), :]
            refs_of = {"q": q_ref, "kc": kc_ref, "vc": vc_ref, "do": do_ref, "kp": kp_ref, "vp": vp_ref}
            for chunk in _unit_chunks(dil):
                heads = sorted({h for h, _ in chunk})
                rows_of = {h: {name: _residue_rows(refs_of[name], copies[name], h, residue)
                               for name in refs_of if refs_of[name] is not None}
                           for h in heads}

                def batch(name):
                    return jnp.stack([rows_of[h][name](r) for h, r in chunk])

                q, k, v, do = batch("q"), batch("kc"), batch("vc"), batch("do")
                if has_prev:
                    k = jnp.concatenate([batch("kp"), k], axis=1)
                    v = jnp.concatenate([batch("vp"), v], axis=1)
                    bias_b = jnp.stack([b_ref[h] for h, _ in chunk])
                else:
                    bias_b = jnp.stack([b_ref[h, :, QB:] for h, _ in chunk])
                lse = jnp.stack([sts[r * QB:(r + 1) * QB, h:h + 1] for h, r in chunk])
                delta = jnp.stack([sts[r * QB:(r + 1) * QB, 4 + h:5 + h] for h, r in chunk])
                s = jnp.einsum("uqd,ukd->uqk", q, k, preferred_element_type=F32) * SCALE + bias_b
                p = jnp.exp(s - lse)
                ds = p * (jnp.einsum("uqd,ukd->uqk", do, v, preferred_element_type=F32) - delta)
                for h in heads:
                    mine = [ds[i] for i, (hh, _) in enumerate(chunk) if hh == h]
                    tot = mine[0]
                    for extra in mine[1:]:
                        tot = tot + extra
                    if has_prev:
                        db_ref[h] += tot
                    else:
                        db_ref[h, :, QB:] += tot
                dsb, pb = ds.astype(BF16), p.astype(BF16)
                dq = jnp.einsum("uqk,ukd->uqd", dsb, k, preferred_element_type=F32) * SCALE
                dk = jnp.einsum("uqk,uqd->ukd", dsb, q, preferred_element_type=F32) * SCALE
                dv = jnp.einsum("uqk,uqd->ukd", pb, do, preferred_element_type=F32)
                for i, (h, r) in enumerate(chunk):
                    if has_prev:
                        cq, ck, cv = carried(h, r)
                        finish(h, r, carry[cq], carry[ck] + dk[i, :QB], carry[cv] + dv[i, :QB])
                        carry[cq] = dq[i]
                        carry[ck] = dk[i, QB:]
                        carry[cv] = dv[i, QB:]
                    else:
                        finish(h, r, dq[i], dk[i], dv[i])
                for h in heads:
                    finish_head(h)
            if has_prev:
                @pl.when(n > 0)
                def _():
                    write_block(b * nsb + n - 1)
            else:
                write_block(b)

    def row(b, n):
        return b * nsb + jnp.minimum(n, nsb - 1)

    def prev(b, n):
        return b * nsb + jnp.maximum(jnp.minimum(n, nsb - 1) - 1, 0)

    in_specs = [
        pl.BlockSpec((rows, GW), lambda b, n: (row(b, n), CB_Q + g)),
        pl.BlockSpec((rows, GW), lambda b, n: (row(b, n), CB_K + g)),
        pl.BlockSpec((rows, GW), lambda b, n: (row(b, n), CB_V + g)),
        pl.BlockSpec((rows, GW), lambda b, n: (row(b, n), 0)),
        pl.BlockSpec((rows, 128), lambda b, n: (row(b, n), 0)),
        pl.BlockSpec((None, None, 4, QB, 2 * QB),
                     lambda b, n: (g, jnp.minimum(jnp.minimum(n, nsb - 1), 1), 0, 0, 0)),
    ]
    args = [proj, proj, proj, d_out, stats, bias]
    scratch = [pltpu.VMEM((2, rows, GW), BF16)] * 3 + [pltpu.SemaphoreType.DMA((2, 3))]
    if has_prev:
        in_specs += [pl.BlockSpec((rows, GW), lambda b, n: (prev(b, n), CB_K + g)),
                     pl.BlockSpec((rows, GW), lambda b, n: (prev(b, n), CB_V + g))]
        args += [proj, proj]
        scratch.append(pltpu.VMEM((rows, 3 * GW), F32))
    n_copied = (7 + (2 if has_prev else 0)) * n_sets
    scratch += [pltpu.VMEM((rows, 128), F32)] * (1 + n_copied)
    in_specs.append(pl.BlockSpec(memory_space=pl.ANY))
    args.append(dproj)
    return pl.pallas_call(
        body, name=f"attn_bwd{g}",
        grid=(BL, n_steps),
        in_specs=in_specs,
        out_specs=(pl.BlockSpec(memory_space=pl.ANY),
                   pl.BlockSpec((4, QB, 2 * QB), lambda b, n: (0, 0, 0))),
        out_shape=(jax.ShapeDtypeStruct((T, NCOL), BF16),
                   jax.ShapeDtypeStruct((4, QB, 2 * QB), F32)),
        scratch_shapes=scratch,
        input_output_aliases={len(args) - 1: 0},
        compiler_params=pltpu.CompilerParams(vmem_limit_bytes=VMEM_LIMIT),
    )(*args)


def _tail(x2, tgt2, mod3, o_g, lse_g, proj, w_ao, w_co, w_o, conv_w, conv_b, ln_g, ln_b):
    tm = 256
    per_seq = S // tm
    halo = 16

    def body(x_ref, t_ref, mod_ref, o1_ref, o2_ref, o3_ref, l1_ref, l2_ref, l3_ref,
             ga_ref, u_ref, bg_ref, cg_ref, gc_ref, ma_ref, mc_ref, up_ref, cp_ref,
             wao_ref, wco_ref, wo_ref, cw_ref, cb_ref, lg_ref, lb_ref,
             dproj_ref, dyc_ref, do_ref, st_ref, dxd_ref,
             gwo_ref, gwco_ref, gwao_ref, vec_ref,
             dga_s, dbg_s, dgm_s, sems, acc_o, acc_co, acc_ao):
        i = pl.program_id(0)
        bidx = i // per_seq
        first = (i % per_seq) == 0

        @pl.when(i == 0)
        def _():
            vec_ref[...] = jnp.zeros_like(vec_ref)

        slot = i % 2

        def column_copies(s, row0):
            return _column_copies([(dga_s.at[s], CB * CB_GA), (dbg_s.at[s], D * KB_BG), (dgm_s.at[s], D * KB_GC)],
                                  dproj_ref, row0, sems.at[s])

        @pl.when(i >= 2)
        def _():
            for cp in column_copies(slot, 0):
                cp.wait()

        l1, l2, l3 = l1_ref[...], l2_ref[...], l3_ref[...]
        mx = jnp.maximum(jnp.maximum(l1, l2), l3)
        e1, e2, e3 = jnp.exp(l1 - mx), jnp.exp(l2 - mx), jnp.exp(l3 - mx)
        esum = e1 + e2 + e3
        lse_tot = mx + jnp.log(esum)
        w1, w2, w3 = e1 / esum, e2 / esum, e3 / esum

        def per_head(wv):
            return jnp.concatenate([jnp.broadcast_to(wv[:, h:h + 1], (tm, HD)) for h in range(4)], axis=1)

        o = per_head(w1) * o1_ref[...] + per_head(w2) * o2_ref[...] + per_head(w3) * o3_ref[...]

        ga = ga_ref[...].astype(F32)
        sig_ga = _sigmoid(ga)
        silu_ga = ga * sig_ga
        a_in = (o * silu_ga).astype(BF16)
        a_out = _dot(a_in, wao_ref[...])

        u = u_ref[...].astype(F32)
        cg = cg_ref[...].astype(F32)
        z = cg * u
        zp = cp_ref[...].astype(F32) * up_ref[...].astype(F32)
        zp = jnp.where(first, 0.0, zp)
        zcat = jnp.concatenate([zp, z], axis=0)
        z1 = pltpu.roll(zcat, 1, 0)[halo:]
        z2 = pltpu.roll(zcat, 2, 0)[halo:]
        y_conv = cw_ref[0:1, :] * z2 + cw_ref[1:2, :] * z1 + cw_ref[2:3, :] * z + cb_ref[...]
        gc = gc_ref[...].astype(F32)
        sig_gc = _sigmoid(gc)
        silu_gc = gc * sig_gc
        bg = bg_ref[...].astype(F32)
        bg_yc = bg * y_conv
        s_in = (bg_yc * silu_gc).astype(BF16)
        s_out = _dot(s_in, wco_ref[...])

        sa = _sigmoid(ma_ref[...].astype(F32))
        sc = _sigmoid(mc_ref[...].astype(F32))
        merged = (sa * a_out + sc * s_out).astype(BF16)
        y = _dot(merged, wo_ref[...])
        gate1 = 1.0 + mod_ref[0, 2:3, :]
        xv = x_ref[...]
        resid = ALPHA * xv + gate1 * y
        mu = jnp.mean(resid, axis=1, keepdims=True)
        xc = resid - mu
        var = jnp.mean(xc * xc, axis=1, keepdims=True)
        rstd = lax.rsqrt(var + LN_EPS)
        xhat = xc * rstd
        lg = lg_ref[...]
        err = xhat * lg + lb_ref[...] - t_ref[...]
        vec_ref[3:4, :] += (0.5 / D) * jnp.sum(err * err, axis=0, keepdims=True)

        vec_ref[1:2, :] += (1.0 / D) * jnp.sum(err * xhat, axis=0, keepdims=True)
        vec_ref[2:3, :] += (1.0 / D) * jnp.sum(err, axis=0, keepdims=True)
        dxh = err * (lg * (1.0 / D))
        dres = rstd * (dxh - jnp.mean(dxh, axis=1, keepdims=True)
                       - xhat * jnp.mean(dxh * xhat, axis=1, keepdims=True))
        dxd_ref[...] = ALPHA * dres
        dgate = jnp.sum(dres * y, axis=0, keepdims=True)
        vec_ref[4:5, :] += jnp.where(bidx == 0, dgate, 0.0)
        vec_ref[5:6, :] += jnp.where(bidx == 1, dgate, 0.0)
        dy = (dres * gate1).astype(BF16)

        dmerged = _dot_nt(dy, wo_ref[...])
        da_out_f = dmerged * sa
        ds_out_f = dmerged * sc
        da_out = da_out_f.astype(BF16)
        ds_out = ds_out_f.astype(BF16)
        dgm_s[slot, :, 2 * D:3 * D] = (ds_out_f * s_out * (1.0 - sc)).astype(BF16)
        dgm_s[slot, :, D:2 * D] = (da_out_f * a_out * (1.0 - sa)).astype(BF16)
        da_in = _dot_nt(da_out, wao_ref[...])
        ds_in = _dot_nt(ds_out, wco_ref[...])

        d_o = da_in * silu_ga
        do_ref[...] = d_o.astype(BF16)
        dga_s[slot] = (da_in * o * (sig_ga + silu_ga * (1.0 - sig_ga))).astype(BF16)
        lane = lax.broadcasted_iota(jnp.int32, (tm, 128), 1)
        stats = lse_tot
        od = o * d_o
        for h in range(4):
            delta = jnp.sum(od[:, h * HD:(h + 1) * HD], axis=1, keepdims=True)
            stats = jnp.where(lane == 4 + h, delta, stats)
        st_ref[...] = stats

        ds_silu = ds_in * silu_gc
        dbg_s[slot] = (ds_silu * y_conv).astype(BF16)
        dyc = ds_silu * bg
        dyc_ref[...] = dyc
        vec_ref[0:1, :] += jnp.sum(dyc, axis=0, keepdims=True)
        dgm_s[slot, :, 0:D] = (ds_in * bg_yc * (sig_gc + silu_gc * (1.0 - sig_gc))).astype(BF16)

        @pl.when(i == 0)
        def _():
            acc_o[...] = jnp.zeros_like(acc_o)
            acc_co[...] = jnp.zeros_like(acc_co)
            acc_ao[...] = jnp.zeros_like(acc_ao)

        acc_o[...] += _dot_tn(merged, dy)
        acc_co[...] += _dot_tn(s_in, ds_out)
        acc_ao[...] += _dot_tn(a_in, da_out)

        for cp in column_copies(slot, pl.multiple_of(i * tm, tm)):
            cp.start()

        @pl.when(i == T // tm - 1)
        def _():
            gwo_ref[...] = acc_o[...].astype(BF16)
            gwco_ref[...] = acc_co[...].astype(BF16)
            gwao_ref[...] = acc_ao[...].astype(BF16)
            for s in range(2):
                for cp in column_copies(s, 0):
                    cp.wait()

    def tile(width, cblk=0):
        return pl.BlockSpec((tm, width), lambda i: (i, cblk))

    def whole(shape):
        return pl.BlockSpec(shape, lambda i: tuple(0 for _ in shape))

    def once(shape):
        return pl.BlockSpec(shape, lambda i: tuple(0 for _ in shape), pipeline_mode=pl.Buffered(1))

    prev_rows = lambda i: (jnp.maximum(i * (tm // halo) - 1, 0),)
    in_specs = [
        tile(D), tile(D), pl.BlockSpec((1, 3, D), lambda i: (i // per_seq, 0, 0)),
        tile(GW), tile(GW), tile(GW), tile(128), tile(128), tile(128),
        tile(GW, CB_GA), tile(D, KB_U), tile(D, KB_BG), tile(D, KB_CG), tile(D, KB_GC),
        tile(D, KB_MA), tile(D, KB_MC),
        pl.BlockSpec((halo, D), lambda i: (*prev_rows(i), KB_U)),
        pl.BlockSpec((halo, D), lambda i: (*prev_rows(i), KB_CG)),
        whole((GW, D)), whole((D, D)), whole((D, D)),
        whole((3, D)), whole((1, D)), whole((1, D)), whole((1, D)),
    ]
    out_specs = (
        pl.BlockSpec(memory_space=pl.ANY), tile(D), tile(GW), tile(128), tile(D),
        once((D, D)), once((D, D)), once((GW, D)),
        pl.BlockSpec((8, D), lambda i: (0, 0)),
    )
    out_shape = (
        jax.ShapeDtypeStruct((T, NCOL), BF16),
        jax.ShapeDtypeStruct((T, D), F32),
        jax.ShapeDtypeStruct((T, GW), BF16),
        jax.ShapeDtypeStruct((T, 128), F32),
        jax.ShapeDtypeStruct((T, D), F32),
        jax.ShapeDtypeStruct((D, D), BF16),
        jax.ShapeDtypeStruct((D, D), BF16),
        jax.ShapeDtypeStruct((GW, D), BF16),
        jax.ShapeDtypeStruct((8, D), F32),
    )
    return pl.pallas_call(
        body, name="tail",
        grid=(T // tm,),
        in_specs=in_specs, out_specs=out_specs, out_shape=out_shape,
        scratch_shapes=[pltpu.VMEM((2, tm, GW), BF16), pltpu.VMEM((2, tm, D), BF16), pltpu.VMEM((2, tm, 3 * D), BF16),
                        pltpu.SemaphoreType.DMA((2, 3)),
                        pltpu.VMEM((D, D), F32), pltpu.VMEM((D, D), F32), pltpu.VMEM((GW, D), F32)],
        compiler_params=pltpu.CompilerParams(vmem_limit_bytes=VMEM_LIMIT_TAIL),
    )(x2, tgt2, mod3, *o_g, *lse_g, proj, proj, proj, proj, proj, proj, proj, proj, proj,
      w_ao, w_co, w_o, conv_w, conv_b, ln_g, ln_b)


def _conv_bwd(dyc, proj, conv_w, dproj):
    tm = 512
    per_seq = S // tm

    def body(d_ref, dn_ref, u_ref, c_ref, cw_ref, _, dproj_ref, g_ref, du_s, dc_s, sems):
        i = pl.program_id(0)
        last = (i % per_seq) == per_seq - 1

        @pl.when(i == 0)
        def _():
            g_ref[...] = jnp.zeros_like(g_ref)

        slot = i % 2

        def column_copies(s, row0):
            return _column_copies([(du_s.at[s], D * KB_U), (dc_s.at[s], D * KB_CG)], dproj_ref, row0, sems.at[s])

        @pl.when(i >= 2)
        def _():
            for cp in column_copies(slot, 0):
                cp.wait()

        d = d_ref[...]
        dn = jnp.where(last, 0.0, dn_ref[...])
        dcat = jnp.concatenate([d, dn], axis=0)
        d1 = pltpu.roll(dcat, tm + 8 - 1, 0)[:tm]
        d2 = pltpu.roll(dcat, tm + 8 - 2, 0)[:tm]
        dz = cw_ref[2:3, :] * d + cw_ref[1:2, :] * d1 + cw_ref[0:1, :] * d2
        u = u_ref[...].astype(F32)
        cg = c_ref[...].astype(F32)
        du_s[slot] = (dz * cg).astype(BF16)
        dc_s[slot] = (dz * u).astype(BF16)
        for cp in column_copies(slot, pl.multiple_of(i * tm, tm)):
            cp.start()

        z = cg * u
        g_ref[0:1, :] += jnp.sum(d2 * z, axis=0, keepdims=True)
        g_ref[1:2, :] += jnp.sum(d1 * z, axis=0, keepdims=True)
        g_ref[2:3, :] += jnp.sum(d * z, axis=0, keepdims=True)

        @pl.when(i == T // tm - 1)
        def _():
            for s in range(2):
                for cp in column_copies(s, 0):
                    cp.wait()

    n_tiles = T // tm
    next_rows = lambda i: jnp.minimum((i + 1) * (tm // 8), T // 8 - 1)
    return pl.pallas_call(
        body, name="conv_bwd",
        grid=(n_tiles,),
        in_specs=[pl.BlockSpec((tm, D), lambda i: (i, 0)),
                  pl.BlockSpec((8, D), lambda i: (next_rows(i), 0)),
                  pl.BlockSpec((tm, D), lambda i: (i, KB_U)),
                  pl.BlockSpec((tm, D), lambda i: (i, KB_CG)),
                  pl.BlockSpec((3, D), lambda i: (0, 0)),
                  pl.BlockSpec(memory_space=pl.ANY)],
        out_specs=(pl.BlockSpec(memory_space=pl.ANY),
                   pl.BlockSpec((8, D), lambda i: (0, 0))),
        out_shape=(jax.ShapeDtypeStruct((T, NCOL), BF16),
                   jax.ShapeDtypeStruct((8, D), F32)),
        scratch_shapes=[pltpu.VMEM((2, tm, D), BF16), pltpu.VMEM((2, tm, D), BF16), pltpu.SemaphoreType.DMA((2, 2))],
        input_output_aliases={5: 0},
        compiler_params=pltpu.CompilerParams(vmem_limit_bytes=VMEM_LIMIT),
    )(dyc, dyc, proj, proj, conv_w, dproj)


def _dh_dx(dproj, w_in_all, x2, dxd, mod3, chip_sums, hops=(), parts0=None):
    tm = 512
    per_seq = S // tm
    n_pass, _, width = w_in_all.shape
    n = len(chip_sums)
    n_in = 5 + n + (0 if parts0 is None else 1)

    def body(*refs):
        d_ref, w_ref, x_ref, dxd_ref, mod_ref = refs[:5]
        ins = refs[5:5 + n]
        gx_ref, vec_ref = refs[n_in:n_in + 2]
        outs = refs[n_in + 2:n_in + 2 + n]
        acc, send_sems, recv_sems, local_sems = refs[n_in + 2 + n:]
        jj, i = pl.program_id(0), pl.program_id(1)

        @pl.when((i == 0) & (jj == 0))
        def _():
            vec_ref[...] = jnp.zeros_like(vec_ref)
            if n:
                sends, _, mine = _chip_copies(ins, outs, send_sems, recv_sems, local_sems, hops)
                for cp in sends + mine:
                    cp.start()

        if n:
            @pl.when((i == T // tm - 1) & (jj == n_pass - 1))
            def _():
                sends, arrivals, mine = _chip_copies(ins, outs, send_sems, recv_sems, local_sems, hops)
                for cp in arrivals:
                    cp.wait_recv()
                for cp in sends:
                    cp.wait_send()
                for cp in mine:
                    cp.wait()

        def partial():
            return _dot_nt(d_ref[...], w_ref[...])

        @pl.when(jj == 0)
        def _():
            acc[i] = partial()

        @pl.when((jj > 0) & (jj < n_pass - 1))
        def _():
            acc[i] += partial()

        @pl.when(jj == n_pass - 1)
        def _():
            dh = acc[i] + partial()
            bidx = i // per_seq
            gx_ref[...] = dxd_ref[...] + dh * (1.0 + mod_ref[0, 1:2, :])
            dshift = jnp.sum(dh, axis=0, keepdims=True)
            dscale = jnp.sum(dh * x_ref[...], axis=0, keepdims=True)
            vec_ref[0:1, :] += jnp.where(bidx == 0, dshift, 0.0)
            vec_ref[1:2, :] += jnp.where(bidx == 1, dshift, 0.0)
            vec_ref[2:3, :] += jnp.where(bidx == 0, dscale, 0.0)
            vec_ref[3:4, :] += jnp.where(bidx == 1, dscale, 0.0)

    def last_pass(jj, i):
        return jnp.where(jj == n_pass - 1, i, 0)

    any_spec = pl.BlockSpec(memory_space=pl.ANY)
    res = pl.pallas_call(
        body, name="dh_dx",
        grid=(n_pass, T // tm),
        in_specs=[
            pl.BlockSpec((tm, width), lambda jj, i: (i, jj)),
            pl.BlockSpec((None, D, width), lambda jj, i: (jj, 0, 0)),
            pl.BlockSpec((tm, D), lambda jj, i: (last_pass(jj, i), 0)),
            pl.BlockSpec((tm, D), lambda jj, i: (last_pass(jj, i), 0)),
            pl.BlockSpec((1, 3, D), lambda jj, i: (last_pass(jj, i) // per_seq, 0, 0))]
                 + [any_spec] * (n_in - 5),
        out_specs=(pl.BlockSpec((tm, D), lambda jj, i: (last_pass(jj, i), 0)),
                   pl.BlockSpec((8, D), lambda jj, i: (0, 0))) + (any_spec,) * n,
        out_shape=(jax.ShapeDtypeStruct((T, D), F32), jax.ShapeDtypeStruct((8, D), F32))
                  + tuple(jax.ShapeDtypeStruct(a.shape, a.dtype) for a in chip_sums),
        scratch_shapes=[pltpu.VMEM((T // tm, tm, D), F32), pltpu.SemaphoreType.DMA((max(3 * n, 1),)),
                        pltpu.SemaphoreType.DMA((max(3 * n, 1),)), pltpu.SemaphoreType.DMA((max(n, 1),))],
        input_output_aliases={} if parts0 is None else {5 + n: 2},
        compiler_params=pltpu.CompilerParams(vmem_limit_bytes=VMEM_LIMIT),
    )(dproj, w_in_all, x2, dxd, mod3, *chip_sums, *([] if parts0 is None else [parts0]))
    return res[0], res[1], res[2:]


def _adam_step(g, w, m, v):
    nm = ADAM_B1 * m + (1.0 - ADAM_B1) * g
    nv = ADAM_B2 * v + (1.0 - ADAM_B2) * (g * g)
    m_hat = nm / (1.0 - ADAM_B1 ** ADAM_STEP)
    v_hat = nv / (1.0 - ADAM_B2 ** ADAM_STEP)
    return -ADAM_LR * (m_hat / (jnp.sqrt(v_hat) + ADAM_EPS) + ADAM_WD * w), nm, nv


def _adamw(parts, w, m, v, name, row_tile=None):
    n_parts, rows, cols = parts.shape
    tr = rows if row_tile is None else row_tile

    def body(p_ref, w_ref, m_ref, v_ref, g_ref, d_ref, nm_ref, nv_ref):
        g = p_ref[0].astype(F32)
        for s in range(1, n_parts):
            g = g + p_ref[s].astype(F32)
        g_ref[...] = g
        d_ref[...], nm_ref[...], nv_ref[...] = _adam_step(g, w_ref[...], m_ref[...], v_ref[...])

    blk = pl.BlockSpec((tr, cols), lambda i: (i, 0))
    shp = jax.ShapeDtypeStruct((rows, cols), F32)
    return pl.pallas_call(
        body, name=name,
        grid=(rows // tr,),
        in_specs=[pl.BlockSpec((n_parts, tr, cols), lambda i: (0, i, 0)), blk, blk, blk],
        out_specs=(blk, blk, blk, blk),
        out_shape=(shp, shp, shp, shp),
        compiler_params=pltpu.CompilerParams(vmem_limit_bytes=VMEM_LIMIT),
    )(parts, w, m, v)


def _multi_adamw(parts_list, params, name):
    n = len(params)
    flat = [t for wmv in params for t in wmv]

    def body(*refs):
        parts, ins, outs = refs[:n], refs[n:4 * n], refs[4 * n:]
        for p in range(n):
            g = parts[p][0].astype(F32)
            for s in range(1, parts[p].shape[0]):
                g = g + parts[p][s].astype(F32)
            w_ref, m_ref, v_ref = ins[3 * p:3 * p + 3]
            g_ref, d_ref, nm_ref, nv_ref = outs[4 * p:4 * p + 4]
            g_ref[...] = g
            d_ref[...], nm_ref[...], nv_ref[...] = _adam_step(g, w_ref[...], m_ref[...], v_ref[...])

    out_shape = []
    for w, _, _ in params:
        out_shape += [jax.ShapeDtypeStruct(w.shape, F32)] * 4
    res = pl.pallas_call(body, name=name, out_shape=tuple(out_shape))(*parts_list, *flat)
    return [res[4 * p:4 * p + 4] for p in range(n)]


def _small_updates(small_g, dmod_all, rel_parts, params):
    flat = [t for wmv in params for t in wmv]

    def body(sg_ref, dm_ref, rp_ref, *refs):
        ins, outs = refs[:len(flat)], refs[len(flat):]

        def over_devices(row):
            tot = sg_ref[0, row:row + 1, :]
            for s in range(1, N_DEV):
                tot = tot + sg_ref[s, row:row + 1, :]
            return tot

        g_b_ada = dm_ref[0:1, :]
        for r in range(1, N_DEV * BL):
            g_b_ada = g_b_ada + dm_ref[r:r + 1, :]
        g_rel = rp_ref[0]
        for s in range(1, N_DEV):
            g_rel = g_rel + rp_ref[s]
        grads = [g_b_ada, over_devices(0), g_rel, over_devices(1), over_devices(2)]
        outs[0][...] = jnp.sum(over_devices(3), axis=1, keepdims=True)
        for p, g in enumerate(grads):
            w_ref, m_ref, v_ref = ins[3 * p:3 * p + 3]
            g_ref, d_ref, nm_ref, nv_ref = outs[1 + 4 * p:5 + 4 * p]
            g_ref[...] = g
            d_ref[...], nm_ref[...], nv_ref[...] = _adam_step(g, w_ref[...], m_ref[...], v_ref[...])

    out_shape = [jax.ShapeDtypeStruct((1, 1), F32)]
    for w, _, _ in params:
        out_shape += [jax.ShapeDtypeStruct(w.shape, F32)] * 4
    res = pl.pallas_call(body, name="small_updates", out_shape=tuple(out_shape))(small_g, dmod_all, rel_parts, *flat)
    return res[0], [res[1 + 4 * p:5 + 4 * p] for p in range(len(params))]


def _attn_fwd_dense(proj, bias):
    nq = 4
    rows = nq * QB
    nsb = S // rows

    def body(q_ref, k_ref, v_ref, kp_ref, vp_ref, b_ref, o_ref, l_ref, ls0, ls1, ls2, ls3):
        ls = [ls0, ls1, ls2, ls3]
        n = pl.program_id(1)
        lane = lax.broadcasted_iota(jnp.int32, (QB, 128), 1)
        units = [(h, j) for h in range(4) for j in range(nq)]

        def keys(cur_ref, prev_ref, h, j):
            sl = slice(h * HD, (h + 1) * HD)
            if j == 0:
                return jnp.concatenate([prev_ref[:, sl], cur_ref[0:QB, sl]], axis=0)
            return cur_ref[(j - 1) * QB:(j + 1) * QB, sl]

        q = jnp.stack([q_ref[j * QB:(j + 1) * QB, h * HD:(h + 1) * HD] for h, j in units])
        k = jnp.stack([keys(k_ref, kp_ref, h, j) for h, j in units])
        v = jnp.stack([keys(v_ref, vp_ref, h, j) for h, j in units])
        bias_b = jnp.stack([b_ref[jnp.minimum(n, 1), h] if j == 0 else b_ref[1, h] for h, j in units])
        s = jnp.einsum("uqd,ukd->uqk", q, k, preferred_element_type=F32) * SCALE + bias_b
        m = jnp.max(s, axis=-1, keepdims=True)
        p = jnp.exp(s - m)
        l = jnp.sum(p, axis=-1, keepdims=True)
        o = jnp.einsum("uqk,ukd->uqd", p.astype(BF16), v, preferred_element_type=F32) / l
        lse = m + jnp.log(l)
        for i, (h, j) in enumerate(units):
            o_ref[j * QB:(j + 1) * QB, h * HD:(h + 1) * HD] = o[i]
            ls[h][j * QB:(j + 1) * QB, :] = jnp.where(lane == h, lse[i], 0.0)
        l_ref[...] = (ls[0][...] + ls[1][...]) + (ls[2][...] + ls[3][...])

    def row(b, n):
        return b * nsb + n

    def prev(b, n):
        return jnp.maximum((b * nsb + n) * nq - 1, 0)

    in_specs = [
        pl.BlockSpec((rows, GW), lambda b, n: (row(b, n), CB_Q)),
        pl.BlockSpec((rows, GW), lambda b, n: (row(b, n), CB_K)),
        pl.BlockSpec((rows, GW), lambda b, n: (row(b, n), CB_V)),
        pl.BlockSpec((QB, GW), lambda b, n: (prev(b, n), CB_K)),
        pl.BlockSpec((QB, GW), lambda b, n: (prev(b, n), CB_V)),
        pl.BlockSpec((None, 2, 4, QB, 2 * QB), lambda b, n: (0, 0, 0, 0, 0)),
    ]
    return pl.pallas_call(
        body, name="attn_fwd0",
        grid=(BL, nsb),
        in_specs=in_specs,
        out_specs=(pl.BlockSpec((rows, GW), lambda b, n: (row(b, n), 0)),
                   pl.BlockSpec((rows, 128), lambda b, n: (row(b, n), 0))),
        out_shape=(jax.ShapeDtypeStruct((T, GW), F32), jax.ShapeDtypeStruct((T, 128), F32)),
        scratch_shapes=[pltpu.VMEM((rows, 128), F32)] * 4,
        compiler_params=pltpu.CompilerParams(vmem_limit_bytes=VMEM_LIMIT),
    )(proj, proj, proj, proj, proj, bias)


def _attn_bwd_dense(proj, d_out, stats, bias, dproj):
    nq = 4
    rows = nq * QB
    nsb = S // rows
    cols_q, cols_k, cols_v = CB * CB_Q, CB * CB_K, CB * CB_V

    def body(q_ref, k_ref, v_ref, do_ref, st_ref, kp_ref, vp_ref, b_ref, _, out_ref, db_ref,
             sq, sk, sv, carry, sems):
        b, n = pl.program_id(0), pl.program_id(1)
        units = [(h, j) for h in range(4) for j in range(nq)]

        @pl.when((b == 0) & (n == 0))
        def _():
            db_ref[...] = jnp.zeros_like(db_ref)

        @pl.when(n == 0)
        def _():
            carry[...] = jnp.zeros_like(carry)

        step = b * (nsb + 1) + n
        slot = step % 2

        def block_copies(s, position, block):
            part = pl.ds(position * QB, QB)
            return _column_copies([(sq.at[s, part], cols_q), (sk.at[s, part], cols_k), (sv.at[s, part], cols_v)],
                                  out_ref, pl.multiple_of(block * QB, QB), sems.at[s, position])

        def wait_blocks(s, positions):
            for position in positions:
                for cp in block_copies(s, position, 0):
                    cp.wait()

        before = (step - 2) % (nsb + 1)

        @pl.when((step >= 2) & (before > 0))
        def _():
            wait_blocks(slot, [0])

        @pl.when((step >= 2) & (before < nsb))
        def _():
            wait_blocks(slot, range(1, nq))

        def write(first_block, position, count):
            for j in range(count):
                for cp in block_copies(slot, position + j, first_block + j):
                    cp.start()

        @pl.when(n == nsb)
        def _():
            sq[slot, 0:QB, :] = carry[:, 0:GW].astype(BF16)
            sk[slot, 0:QB, :] = carry[:, GW:2 * GW].astype(BF16)
            sv[slot, 0:QB, :] = carry[:, 2 * GW:3 * GW].astype(BF16)
            write((b + 1) * nsb * nq - 1, 0, 1)

            @pl.when(b == BL - 1)
            def _():
                wait_blocks(slot, [0])
                wait_blocks(1 - slot, range(nq))

        @pl.when(n < nsb)
        def _():
            def keys(cur_ref, prev_ref, h, j):
                sl = slice(h * HD, (h + 1) * HD)
                if j == 0:
                    return jnp.concatenate([prev_ref[:, sl], cur_ref[0:QB, sl]], axis=0)
                return cur_ref[(j - 1) * QB:(j + 1) * QB, sl]

            def block(ref, h, j):
                return ref[j * QB:(j + 1) * QB, h * HD:(h + 1) * HD]

            q = jnp.stack([block(q_ref, h, j) for h, j in units])
            do = jnp.stack([block(do_ref, h, j) for h, j in units])
            k = jnp.stack([keys(k_ref, kp_ref, h, j) for h, j in units])
            v = jnp.stack([keys(v_ref, vp_ref, h, j) for h, j in units])
            bias_b = jnp.stack([b_ref[jnp.minimum(n, 1), h] if j == 0 else b_ref[1, h] for h, j in units])
            lse = jnp.stack([st_ref[j * QB:(j + 1) * QB, h:h + 1] for h, j in units])
            delta = jnp.stack([st_ref[j * QB:(j + 1) * QB, 4 + h:5 + h] for h, j in units])
            s = jnp.einsum("uqd,ukd->uqk", q, k, preferred_element_type=F32) * SCALE + bias_b
            p = jnp.exp(s - lse)
            ds = p * (jnp.einsum("uqd,ukd->uqk", do, v, preferred_element_type=F32) - delta)
            for h in range(4):
                tot = ds[h * nq]
                for j in range(1, nq):
                    tot = tot + ds[h * nq + j]
                db_ref[h] += tot
            dsb, pb = ds.astype(BF16), p.astype(BF16)
            dq = jnp.einsum("uqk,ukd->uqd", dsb, k, preferred_element_type=F32) * SCALE
            dk = jnp.einsum("uqk,uqd->ukd", dsb, q, preferred_element_type=F32) * SCALE
            dv = jnp.einsum("uqk,uqd->ukd", pb, do, preferred_element_type=F32)
            for h in range(4):
                sl = slice(h * HD, (h + 1) * HD)
                u0, last = h * nq, h * nq + nq - 1
                sq[slot, 0:QB, sl] = carry[:, sl].astype(BF16)
                sk[slot, 0:QB, sl] = (carry[:, GW + h * HD:GW + (h + 1) * HD] + dk[u0, :QB]).astype(BF16)
                sv[slot, 0:QB, sl] = (carry[:, 2 * GW + h * HD:2 * GW + (h + 1) * HD] + dv[u0, :QB]).astype(BF16)
                for j in range(nq - 1):
                    pos = slice((j + 1) * QB, (j + 2) * QB)
                    sq[slot, pos, sl] = dq[u0 + j].astype(BF16)
                    sk[slot, pos, sl] = (dk[u0 + j, QB:] + dk[u0 + j + 1, :QB]).astype(BF16)
                    sv[slot, pos, sl] = (dv[u0 + j, QB:] + dv[u0 + j + 1, :QB]).astype(BF16)
                carry[:, sl] = dq[last]
                carry[:, GW + h * HD:GW + (h + 1) * HD] = dk[last, QB:]
                carry[:, 2 * GW + h * HD:2 * GW + (h + 1) * HD] = dv[last, QB:]

            @pl.when(n == 0)
            def _():
                write(b * nsb * nq, 1, nq - 1)

            @pl.when(n > 0)
            def _():
                write((b * nsb + n) * nq - 1, 0, nq)

    def row(b, n):
        return b * nsb + jnp.minimum(n, nsb - 1)

    def prev(b, n):
        return jnp.maximum(row(b, n) * nq - 1, 0)

    in_specs = [
        pl.BlockSpec((rows, GW), lambda b, n: (row(b, n), CB_Q)),
        pl.BlockSpec((rows, GW), lambda b, n: (row(b, n), CB_K)),
        pl.BlockSpec((rows, GW), lambda b, n: (row(b, n), CB_V)),
        pl.BlockSpec((rows, GW), lambda b, n: (row(b, n), 0)),
        pl.BlockSpec((rows, 128), lambda b, n: (row(b, n), 0)),
        pl.BlockSpec((QB, GW), lambda b, n: (prev(b, n), CB_K)),
        pl.BlockSpec((QB, GW), lambda b, n: (prev(b, n), CB_V)),
        pl.BlockSpec((None, 2, 4, QB, 2 * QB), lambda b, n: (0, 0, 0, 0, 0)),
        pl.BlockSpec(memory_space=pl.ANY),
    ]
    return pl.pallas_call(
        body, name="attn_bwd0",
        grid=(BL, nsb + 1),
        in_specs=in_specs,
        out_specs=(pl.BlockSpec(memory_space=pl.ANY),
                   pl.BlockSpec((4, QB, 2 * QB), lambda b, n: (0, 0, 0))),
        out_shape=(jax.ShapeDtypeStruct((T, NCOL), BF16),
                   jax.ShapeDtypeStruct((4, QB, 2 * QB), F32)),
        scratch_shapes=[pltpu.VMEM((2, rows, GW), BF16)] * 3
                       + [pltpu.VMEM((QB, 3 * GW), F32), pltpu.SemaphoreType.DMA((2, nq, 3))],
        input_output_aliases={8: 0},
        compiler_params=pltpu.CompilerParams(vmem_limit_bytes=VMEM_LIMIT),
    )(proj, proj, proj, d_out, stats, proj, proj, bias, dproj)


def _attention_forward(proj, rel_bias):
    expand_lanes, grad_lanes, masks = (jnp.asarray(t) for t in _bucket_maps())
    bias = _bias_expand(rel_bias, expand_lanes, masks)
    fwd = [_attn_fwd_dense(proj, bias)] + [_attn_fwd(proj, bias, g) for g in (1, 2)]
    return bias, grad_lanes, [f[0] for f in fwd], [f[1] for f in fwd]


def _local_step(x2, tgt2, mod3, h, proj, attn, w_ao, w_co, w_o, conv_w, conv_b, ln_g, ln_b):
    bias, buckets, o_g, lse_g = attn

    (dproj, dyc, d_o, stats, dxd, gw_o, gw_co, gw_ao, tail_vec) = _tail(
        x2, tgt2, mod3, o_g, lse_g, proj, w_ao, w_co, w_o, conv_w, conv_b, ln_g, ln_b)

    dproj, db = _attn_bwd_dense(proj, d_o, stats, bias, dproj)
    dbias = [db]
    for g in (1, 2):
        dproj, db = _attn_bwd(proj, d_o, stats, bias, dproj, g)
        dbias.append(db)
    g_rel_bias = _bias_grad(*dbias, buckets)
    dproj, conv_vec = _conv_bwd(dyc, proj, conv_w, dproj)

    gw_ao = jnp.transpose(gw_ao.reshape(GW, N_DEV, D // N_DEV), (1, 0, 2))
    return dproj, dxd, gw_ao, gw_co, gw_o, conv_vec, g_rel_bias, tail_vec


def kernel(x, c, w_ada, b_ada, w_in, conv_w, conv_b, rel_bias, w_attn_out, w_conv_out, w_o, ln_g, ln_b, loss_target, m_w_ada, m_b_ada, m_w_in, m_conv_w, m_conv_b, m_rel_bias, m_w_attn_out, m_w_conv_out, m_w_o, m_ln_g, m_ln_b, v_w_ada, v_b_ada, v_w_in, v_conv_w, v_conv_b, v_rel_bias, v_w_attn_out, v_w_conv_out, v_w_o, v_ln_g, v_ln_b):
    me = _my_index()
    x2 = x.reshape(T, D)
    tgt2 = loss_target.reshape(T, D)

    b_cols = lax.dynamic_slice(b_ada, (0, me * ADA_SHARD), (1, ADA_SHARD))
    c_g, mod_in = _mod_exchange(jnp.pad(c, ((0, 8 - BL), (0, 0))), w_ada[0], b_cols)
    c_all = c_g[:, 0:BL, :].reshape(N_DEV * BL, D)
    mod3 = jnp.transpose(mod_in[:, 0:BL, :], (1, 0, 2)).reshape(BL, 3, D)

    rows_shape = jax.ShapeDtypeStruct((N_DEV, D // N_DEV, D), BF16)
    proj, h, w_in_all, (w_ao_g, w_co_g, w_o_g, conv_w_g) = _gather_proj(
        _shard_order(), x2, mod3, w_in[0].astype(BF16), 1024,
        ([w_attn_out[0].astype(BF16), w_conv_out[0].astype(BF16), w_o[0].astype(BF16), conv_w[0]],
         [jax.ShapeDtypeStruct((N_DEV, GW, D // N_DEV), BF16), rows_shape, rows_shape,
          jax.ShapeDtypeStruct((N_DEV, 3, D // N_DEV), F32)]))

    attn = _attention_forward(proj, rel_bias)
    w_ao_full = jnp.transpose(w_ao_g, (1, 0, 2)).reshape(GW, D)
    w_co_full = w_co_g.reshape(D, D)
    w_o_full = w_o_g.reshape(D, D)
    conv_w_full = jnp.transpose(conv_w_g, (1, 0, 2)).reshape(3, D)

    (dproj, dxd, gw_ao, gw_co, gw_o, conv_vec, g_rel_bias, tail_vec) = _local_step(
        x2, tgt2, mod3, h, proj, attn, w_ao_full, w_co_full, w_o_full,
        conv_w_full, conv_b, ln_g, ln_b)

    g_conv_w_blocks = jnp.transpose(conv_vec[0:3].reshape(3, N_DEV, D // N_DEV), (1, 0, 2))
    partials = [gw_ao, gw_co.reshape(N_DEV, D // N_DEV, D), gw_o.reshape(N_DEV, D // N_DEV, D), g_conv_w_blocks]
    w_in_sums, w_in_parts, sib = _gw_in_pair(
        _slice_order(), h, dproj, partials,
        [jax.ShapeDtypeStruct((4, GW, D // N_DEV), BF16),
         jax.ShapeDtypeStruct((4, D // N_DEV, D), BF16),
         jax.ShapeDtypeStruct((4, D // N_DEV, D), BF16),
         jax.ShapeDtypeStruct((4, 3, D // N_DEV), F32)])
    core = lax.axis_index("c").astype(jnp.int32).reshape(1)
    chip_sums = [w_in_sums] + list(_pair_add(core, partials, sib))
    hops = [(3,)] + [(1, 2, 3)] * 4
    grad_x, mod_vec, (r_in, r_ao, r_co, r_o, r_cw) = _dh_dx(
        dproj, w_in_all, x2, dxd, mod3, chip_sums, hops, w_in_parts)

    small = jnp.concatenate([
        tail_vec[0:4],
        jnp.pad(g_rel_bias.reshape(1, N_BUCKETS * N_HEADS), ((0, 0), (0, D - N_BUCKETS * N_HEADS))),
        jnp.zeros((3, D), F32)], axis=0)
    dmod = jnp.concatenate([mod_vec[0:2], mod_vec[2:4], tail_vec[4:6]], axis=1)
    small_g, dmod_g = _all_gather(
        [small, dmod],
        [jax.ShapeDtypeStruct((N_DEV, 8, D), F32), jax.ShapeDtypeStruct((N_DEV, BL, 3 * D), F32)],
        "gather_small")
    dmod_all = dmod_g.reshape(N_DEV * BL, 3 * D)
    small_names = ["b_ada", "conv_b", "rel_bias", "ln_g", "ln_b"]
    small_params = [(b_ada, m_b_ada, v_b_ada), (conv_b, m_conv_b, v_conv_b), (rel_bias, m_rel_bias, v_rel_bias),
                    (ln_g, m_ln_g, v_ln_g), (ln_b, m_ln_b, v_ln_b)]
    loss, small_res = _small_updates(
        small_g, dmod_all, small_g[:, 4, :N_BUCKETS * N_HEADS].reshape(N_DEV, N_BUCKETS, N_HEADS), small_params)
    loss = loss.reshape(())

    dmod_cols = lax.dynamic_slice(dmod_all, (0, me * ADA_SHARD), (N_DEV * BL, ADA_SHARD))
    res = {
        "w_ada": tuple(t[None] for t in _w_ada_update(jnp.transpose(c_all), dmod_cols,
                                                      w_ada[0], m_w_ada[0], v_w_ada[0])),
        "w_in": tuple(t[None] for t in _adamw(r_in, w_in[0], m_w_in[0], v_w_in[0], "adam_w_in", 128)),
    }
    mid_names = ["conv_w", "w_attn_out", "w_conv_out", "w_o"]
    mid_parts = [r_cw, r_ao, r_co, r_o]
    mid_full = [(conv_w, m_conv_w, v_conv_w), (w_attn_out, m_w_attn_out, v_w_attn_out),
                (w_conv_out, m_w_conv_out, v_w_conv_out), (w_o, m_w_o, v_w_o)]
    mid_res = _multi_adamw(mid_parts, [tuple(t[0] for t in wmv) for wmv in mid_full], "adam_mid")
    for nm, wmv, outs4 in zip(mid_names, mid_full, mid_res):
        res[nm] = tuple(t[None] for t in outs4)
    res.update(dict(zip(small_names, small_res)))
    order = ["w_ada", "b_ada", "w_in", "conv_w", "conv_b", "rel_bias", "w_attn_out", "w_conv_out",
             "w_o", "ln_g", "ln_b"]
    outs = [loss, grad_x.reshape(BL, S, D)]
    for k in range(4):
        outs += [res[name][k] for name in order]
    return tuple(outs)
```

```python
import math

import numpy as np
import jax
import jax.numpy as jnp
from jax import lax
from jax.experimental import pallas as pl
from jax.experimental.pallas import tpu as pltpu

F32 = jnp.float32
BF16 = jnp.bfloat16
MESH = pl.DeviceIdType.MESH

N_DEV = 8
D = 1024
S = 2048
BL = 2
T = BL * S
NCOL = 11264
SHARD = NCOL // N_DEV
CB = 512
NCB = NCOL // CB
HD = 128
GW = 512
QB = 128
DILATIONS = (1, 4, 16)
N_STEPS = 128
N_BUCKETS = 32
N_HEADS = 12
ALPHA = 2.0 ** 0.25
LN_EPS = 1e-5
NEG_INF = -1e30
SCALE = HD ** -0.5
ADA_SHARD = 3 * D // N_DEV

CB_Q, CB_K, CB_V, CB_GA = 0, 3, 6, 9
KB_U, KB_BG, KB_CG, KB_GC, KB_MA, KB_MC = 5, 6, 7, 8, 9, 10

ADAM_LR, ADAM_B1, ADAM_B2, ADAM_EPS, ADAM_WD, ADAM_STEP = 0.001, 0.9, 0.999, 1e-08, 0.01, 10

VMEM_LIMIT = 56 * 1024 * 1024
VMEM_LIMIT_TAIL = 62 * 1024 * 1024


def _dot(a, b):
    return jnp.dot(a, b, preferred_element_type=F32)


def _dot_nt(a, b):
    return lax.dot_general(a, b, (((1,), (1,)), ((), ())), preferred_element_type=F32)


def _dot_tn(a, b):
    return lax.dot_general(a, b, (((0,), (0,)), ((), ())), preferred_element_type=F32)


def _sigmoid(v):
    return 1.0 / (1.0 + jnp.exp(-v))


def _column_copies(pieces, dst_hbm, row0, sems):
    copies = []
    for k, (src, col0) in enumerate(pieces):
        rows, width = src.shape
        copies.append(pltpu.make_async_copy(
            src, dst_hbm.at[pl.ds(row0, rows), pl.ds(col0, width)], sems.at[k]))
    return copies


def _my_index():
    return 4 * lax.axis_index("x") + 2 * lax.axis_index("y") + lax.axis_index("c")


class _Gather:
    def __init__(self, ins, outs, stage, send_sems, recv_sems, local_sems):
        self.ins, self.outs, self.stage = ins, outs, stage
        self.send_sems, self.recv_sems, self.local_sems = send_sems, recv_sems, local_sems
        x, y, c = lax.axis_index("x"), lax.axis_index("y"), lax.axis_index("c")
        self.c = c
        self.me, self.sibling = (x, y, c), (x, y, 1 - c)
        self.chips = [(1 - x, y), (x, 1 - y), (1 - x, 1 - y)]

    @staticmethod
    def scratch(arrs):
        n = len(arrs)
        return ([pltpu.SemaphoreType.DMA((7 * n,)), pltpu.SemaphoreType.DMA((7 * n,)),
                 pltpu.SemaphoreType.DMA((n,))] + [pltpu.VMEM(a.shape, a.dtype) for a in arrs])

    def _copy(self, a, k, block, to, src=None):
        dst = self.outs[a].at[4 * block[0] + 2 * block[1] + block[2]]
        return pltpu.make_async_remote_copy(
            src_ref=dst if src is None else src, dst_ref=dst,
            send_sem=self.send_sems.at[a * 7 + k], recv_sem=self.recv_sems.at[a * 7 + k],
            device_id=to, device_id_type=MESH)

    def _first(self):
        first = []
        for a in range(len(self.ins)):
            first.append(self._copy(a, 0, self.me, self.sibling, src=self.ins[a]))
            first += [self._copy(a, 1 + j, self.me, (*chip, self.c), src=self.ins[a])
                      for j, chip in enumerate(self.chips)]
        return first

    def _mine(self):
        me = self.me
        return [pltpu.make_async_copy(self.stage[a], self.outs[a].at[4 * me[0] + 2 * me[1] + me[2]],
                                      self.local_sems.at[a]) for a in range(len(self.ins))]

    def begin(self):
        for cp in self._first():
            cp.start()
        loads = [pltpu.make_async_copy(self.ins[a], self.stage[a], self.local_sems.at[a])
                 for a in range(len(self.ins))]
        for cp in loads:
            cp.start()
        for cp in loads:
            cp.wait()
        for cp in self._mine():
            cp.start()

    def _passed(self):
        return [self._copy(a, 4 + j, (*chip, self.c), self.sibling)
                for j, chip in enumerate(self.chips) for a in range(len(self.ins))]

    def pass_on(self):
        for j, chip in enumerate(self.chips):
            for a in range(len(self.ins)):
                self._copy(a, 1 + j, (*chip, self.c), self.me).wait_recv()
        for cp in self._passed():
            cp.start()

    def finish(self):
        n, c, me, sibling = len(self.ins), self.c, self.me, self.sibling
        passed = self._passed()
        for a in range(n):
            self._copy(a, 0, sibling, me).wait_recv()
        for j, chip in enumerate(self.chips):
            for a in range(n):
                self._copy(a, 4 + j, (*chip, 1 - c), me).wait_recv()
        for cp in self._first() + passed:
            cp.wait_send()
        for cp in self._mine():
            cp.wait()


def _all_gather(arrs, out_shapes, name):
    n = len(arrs)

    def body(*refs):
        g = _Gather(refs[:n], refs[n:2 * n], refs[2 * n + 3:], *refs[2 * n:2 * n + 3])
        g.begin()
        g.pass_on()
        g.finish()

    any_spec = pl.BlockSpec(memory_space=pl.ANY)
    return pl.pallas_call(
        body, name=name,
        out_shape=tuple(out_shapes),
        in_specs=[any_spec] * n,
        out_specs=tuple([any_spec] * n),
        scratch_shapes=_Gather.scratch(arrs),
    )(*arrs)


def _neighbour_chips():
    x, y, c = lax.axis_index("x"), lax.axis_index("y"), lax.axis_index("c")
    first = (jnp.where(c == 0, 1 - x, x), jnp.where(c == 0, y, 1 - y))
    second = (jnp.where(c == 0, x, 1 - x), jnp.where(c == 0, 1 - y, y))
    return first, second, (1 - x, 1 - y)


def _slice_order():
    x, y, c = lax.axis_index("x"), lax.axis_index("y"), lax.axis_index("c")
    nb1, nb2, diag = _neighbour_chips()
    slots = []
    for mine, theirs in ((nb1, nb2), (nb2, nb1), (diag, diag), ((x, y), (x, y))):
        slots += [2 * (2 * theirs[0] + theirs[1]) + 1 - c, 2 * (2 * mine[0] + mine[1]) + c]
    return jnp.stack(slots).astype(jnp.int32)


def _gw_in_pair(order, h, dproj, smalls, small_shapes4):
    kk, m = h.shape
    tk = min(kk, 2048)
    nk = kk // tk
    ncols = dproj.shape[1] // N_DEV
    n = len(smalls)

    def body(order_ref, h_ref, d_ref, *rest):
        ins = rest[:n]
        sums_hbm, parts_hbm = rest[n], rest[n + 1]
        sib = rest[n + 2:2 * n + 2]
        (acc, sendbuf, recvbuf, sumbuf, send_sems, recv_sems, local_sem, ssend, srecv,
         isend, irecv) = rest[2 * n + 2:]
        js, k = pl.program_id(0), pl.program_id(1)
        x, y, c = lax.axis_index("x"), lax.axis_index("y"), lax.axis_index("c")
        sibling = (x, y, 1 - c)
        my_chip = 2 * x + y
        nb1, nb2, _ = _neighbour_chips()
        near = [(*nb1, c), (*nb2, c)]

        def ici_copy(p, out_chip):
            peer = near[p] if isinstance(p, int) else tuple(jnp.where(p == 0, a, b) for a, b in zip(*near))
            return pltpu.make_async_remote_copy(
                src_ref=sumbuf.at[p], dst_ref=parts_hbm.at[out_chip],
                send_sem=isend.at[p], recv_sem=irecv.at[p], device_id=peer, device_id_type=MESH)

        def small_copies():
            return [pltpu.make_async_remote_copy(
                        src_ref=ins[a].at[2 * q + 1 - c], dst_ref=sib[a].at[q],
                        send_sem=ssend.at[a * 4 + q], recv_sem=srecv.at[a * 4 + q],
                        device_id=sibling, device_id_type=MESH)
                    for a in range(n) for q in range(4)]

        def slice_copy(p):
            return pltpu.make_async_remote_copy(
                src_ref=sendbuf, dst_ref=recvbuf.at[p], send_sem=send_sems.at[p], recv_sem=recv_sems.at[p],
                device_id=sibling, device_id_type=MESH)

        def sum_copy(p):
            return pltpu.make_async_copy(sumbuf.at[2], sums_hbm.at[order_ref[2 * p] // 2], local_sem)

        @pl.when((js == 0) & (k == 0))
        def _():
            for cp in small_copies():
                cp.start()

        def partial():
            return _dot_tn(h_ref[...], d_ref[...])

        if nk > 1:
            @pl.when(k == 0)
            def _():
                acc[...] = partial()
        if nk > 2:
            @pl.when((k > 0) & (k < nk - 1))
            def _():
                acc[...] += partial()

        def total():
            return partial() + acc[...] if nk > 1 else partial()

        p = js // 2

        @pl.when((js % 2 == 0) & (k == nk - 1))
        def _():
            @pl.when(p > 0)
            def _():
                slice_copy(p - 1).wait_send()
            sendbuf[...] = total().astype(BF16)
            slice_copy(p).start()

        @pl.when((js % 2 == 1) & (k == nk - 1))
        def _():
            slice_copy(p).wait_recv()

            @pl.when(p == 3)
            def _():
                sum_copy(2).wait()
            sumbuf[jnp.minimum(p, 2)] = (total() + recvbuf[p].astype(F32)).astype(BF16)

            @pl.when(p < 2)
            def _():
                ici_copy(p, my_chip).start()

            @pl.when(p >= 2)
            def _():
                sum_copy(p).start()

        @pl.when((js == N_DEV - 1) & (k == nk - 1))
        def _():
            slice_copy(3).wait_send()
            sum_copy(3).wait()
            for cp in small_copies():
                cp.wait()
            for p in range(2):
                ici_copy(p, 2 * near[p][0] + near[p][1]).wait_recv()
                ici_copy(p, my_chip).wait_send()

    any_spec = pl.BlockSpec(memory_space=pl.ANY)
    res = pl.pallas_call(
        body, name="gw_in_pair",
        grid_spec=pltpu.PrefetchScalarGridSpec(
            num_scalar_prefetch=1,
            grid=(N_DEV, nk),
            in_specs=[pl.BlockSpec((tk, m), lambda js, k, order_ref: (k, 0)),
                      pl.BlockSpec((tk, ncols), lambda js, k, order_ref: (k, order_ref[js]))] + [any_spec] * n,
            out_specs=(any_spec,) * (n + 2),
            scratch_shapes=[pltpu.VMEM((m, ncols), F32), pltpu.VMEM((m, ncols), BF16),
                            pltpu.VMEM((4, m, ncols), BF16), pltpu.VMEM((3, m, ncols), BF16),
                            pltpu.SemaphoreType.DMA((4,)), pltpu.SemaphoreType.DMA((4,)),
                            pltpu.SemaphoreType.DMA,
                            pltpu.SemaphoreType.DMA((4 * n,)), pltpu.SemaphoreType.DMA((4 * n,)),
                            pltpu.SemaphoreType.DMA((2,)), pltpu.SemaphoreType.DMA((2,))]),
        out_shape=(jax.ShapeDtypeStruct((4, m, ncols), BF16),) * 2 + tuple(small_shapes4),
        compiler_params=pltpu.CompilerParams(vmem_limit_bytes=VMEM_LIMIT),
    )(order, h, dproj, *smalls)
    return res[0], res[1], res[2:]


def _chip_copies(ins, outs, send_sems, recv_sems, local_sems, hops):
    n = len(ins)
    x, y, c = lax.axis_index("x"), lax.axis_index("y"), lax.axis_index("c")
    my_chip = 2 * x + y

    def peer_of(k):
        return ((1 - x) if (k >> 1) & 1 else x, (1 - y) if k & 1 else y, c)

    def copy(a, k, out_chip):
        peer = peer_of(k)
        return pltpu.make_async_remote_copy(
            src_ref=ins[a].at[2 * peer[0] + peer[1]], dst_ref=outs[a].at[out_chip],
            send_sem=send_sems.at[a * 3 + k - 1], recv_sem=recv_sems.at[a * 3 + k - 1],
            device_id=peer, device_id_type=MESH)

    sends = [copy(a, k, my_chip) for k in range(1, 4) for a in range(n) if k in hops[a]]
    arrivals = []
    for k in range(1, 4):
        peer = peer_of(k)
        arrivals += [copy(a, k, 2 * peer[0] + peer[1]) for a in range(n) if k in hops[a]]
    mine = [pltpu.make_async_copy(ins[a].at[my_chip], outs[a].at[my_chip], local_sems.at[a])
            for a in range(n)]
    return sends, arrivals, mine


def _pair_add(core, mines, theirs):
    n = len(mines)

    def body(core_ref, *refs):
        mine, sib, outs = refs[:n], refs[n:2 * n], refs[2 * n:]
        for a in range(n):
            for q in range(4):
                outs[a][q] = (mine[a][2 * q + core_ref[0]].astype(F32)
                              + sib[a][q].astype(F32)).astype(outs[a].dtype)

    return pl.pallas_call(
        body, name="pair_add",
        in_specs=[pl.BlockSpec(memory_space=pltpu.SMEM)] + [pl.BlockSpec(memory_space=pltpu.VMEM)] * (2 * n),
        out_shape=tuple(jax.ShapeDtypeStruct(t.shape, t.dtype) for t in theirs),
    )(core, *mines, *theirs)


def _mod_exchange(c8, w_ada, b_cols):
    cols = w_ada.shape[1]

    def body(c_ref, w_ref, b_ref, call_ref, mod_ref, msend, send1, recv1, send2, recv2):
        x, y, c = lax.axis_index("x"), lax.axis_index("y"), lax.axis_index("c")
        my_slot = 4 * x + 2 * y + c

        def peer_of(k):
            return ((1 - x) if (k >> 2) & 1 else x, (1 - y) if (k >> 1) & 1 else y, (1 - c) if k & 1 else c)

        def slot_of(dev):
            return 4 * dev[0] + 2 * dev[1] + dev[2]

        def exchange(src_of, dst_ref, send_sems, recv_sems):
            sends, arrivals = [], []
            for k in range(1, 8):
                peer = peer_of(k)
                sends.append(pltpu.make_async_remote_copy(
                    src_ref=src_of(slot_of(peer)), dst_ref=dst_ref.at[my_slot],
                    send_sem=send_sems.at[k - 1], recv_sem=recv_sems.at[k - 1],
                    device_id=peer, device_id_type=MESH))
                arrivals.append(pltpu.make_async_remote_copy(
                    src_ref=src_of(my_slot), dst_ref=dst_ref.at[slot_of(peer)],
                    send_sem=send_sems.at[k - 1], recv_sem=recv_sems.at[k - 1],
                    device_id=peer, device_id_type=MESH))
            for cp in sends:
                cp.start()
            for cp in arrivals:
                cp.wait_recv()
            for cp in sends:
                cp.wait_send()

        call_ref[my_slot] = c_ref[...]
        exchange(lambda s: c_ref, call_ref, send1, recv1)
        cv = call_ref[...].reshape(N_DEV * 8, c_ref.shape[1])
        act = cv * _sigmoid(cv)
        mod = jnp.dot(act, w_ref[...], preferred_element_type=F32,
                      precision=lax.Precision.HIGHEST) + b_ref[...]
        msend[...] = mod.reshape(N_DEV, 8, cols)
        mod_ref[my_slot] = msend[my_slot]
        exchange(lambda s: msend.at[s], mod_ref, send2, recv2)

    return pl.pallas_call(
        body, name="mod_exchange",
        out_shape=(jax.ShapeDtypeStruct((N_DEV, 8, c8.shape[1]), F32),
                   jax.ShapeDtypeStruct((N_DEV, 8, cols), F32)),
        scratch_shapes=[pltpu.VMEM((N_DEV, 8, cols), F32)] + [pltpu.SemaphoreType.DMA((7,))] * 4,
    )(c8, w_ada, b_cols)


def _w_ada_update(c_all_t, dmod_cols, w, m, v):
    rows, cols = w.shape
    tr = 256

    def body(c_ref, d_ref, w_ref, m_ref, v_ref, g_ref, dl_ref, nm_ref, nv_ref):
        cv = c_ref[...]
        g = jnp.dot(cv * _sigmoid(cv), d_ref[...], preferred_element_type=F32,
                    precision=lax.Precision.HIGHEST)
        g_ref[...] = g
        dl_ref[...], nm_ref[...], nv_ref[...] = _adam_step(g, w_ref[...], m_ref[...], v_ref[...])

    blk = pl.BlockSpec((tr, cols), lambda i: (i, 0))
    shp = jax.ShapeDtypeStruct((rows, cols), F32)
    return pl.pallas_call(
        body, name="adam_w_ada",
        grid=(rows // tr,),
        in_specs=[pl.BlockSpec((tr, c_all_t.shape[1]), lambda i: (i, 0)),
                  pl.BlockSpec(dmod_cols.shape, lambda i: (0, 0)), blk, blk, blk],
        out_specs=(blk, blk, blk, blk),
        out_shape=(shp, shp, shp, shp),
    )(c_all_t, dmod_cols, w, m, v)


def _shard_order():
    x, y, c = lax.axis_index("x"), lax.axis_index("y"), lax.axis_index("c")
    first, second, diag = _neighbour_chips()
    devs = [(x, y, c), (x, y, 1 - c), (*first, c), (*second, 1 - c), (*second, c), (*first, 1 - c),
            (*diag, c), (*diag, 1 - c)]
    return jnp.stack([4 * d[0] + 2 * d[1] + d[2] for d in devs]).astype(jnp.int32)


def _gather_proj(order, x2, mod3, w_shard, tm, ride=()):
    rows, kdim = x2.shape
    ncols = w_shard.shape[1]
    n_i = rows // tm
    per_seq = n_i // mod3.shape[0]
    ride_arrs, ride_shapes = ride if ride else ((), ())
    n_ride = len(ride_arrs)

    def body(order_ref, x_ref, mod_ref, mine_hbm, *rest):
        ride_ins = rest[:n_ride]
        o_ref, h_ref, all_hbm = rest[n_ride:n_ride + 3]
        ride_outs = rest[n_ride + 3:2 * n_ride + 3]
        wv, send_sems, recv_sems, local_sems, hs = rest[2 * n_ride + 3:2 * n_ride + 8]
        ride_scr = rest[2 * n_ride + 8:]
        j, i = pl.program_id(0), pl.program_id(1)
        c = lax.axis_index("c")
        me, sibling = (lax.axis_index("x"), lax.axis_index("y"), c), (lax.axis_index("x"), lax.axis_index("y"), 1 - c)
        nb1, nb2, diag = _neighbour_chips()

        def slot(dev):
            return 4 * dev[0] + 2 * dev[1] + dev[2]

        def copy(k, block, to, src=None, part=None):
            buf = wv.at[slot(block)]
            if part is not None:
                buf = buf.at[pl.ds(pl.multiple_of(part * (kdim // 2), kdim // 2), kdim // 2)]
            return pltpu.make_async_remote_copy(
                src_ref=buf if src is None else src, dst_ref=buf,
                send_sem=send_sems.at[k], recv_sem=recv_sems.at[k],
                device_id=to, device_id_type=MESH)

        def keep(step, block):
            s = slot(block)
            cols = pl.ds(pl.multiple_of((s % 2) * ncols, 128), ncols)
            return pltpu.make_async_copy(wv.at[s], all_hbm.at[s // 2, :, cols], local_sems.at[step])

        if n_ride:
            gather = _Gather(ride_ins, ride_outs, ride_scr[3:], *ride_scr[:3])
        to_sibling, to_nb1, to_nb2 = (copy(0, me, sibling, mine_hbm), copy(1, me, (*nb1, c), mine_hbm),
                                      copy(2, me, (*nb2, c), mine_hbm))
        relay1, relay2 = copy(3, (*nb2, c), (*nb1, c), part=c), copy(4, (*nb1, c), (*nb2, c), part=1 - c)
        pass_nb1, pass_nb2 = copy(5, (*nb1, c), sibling), copy(6, (*nb2, c), sibling)
        pass_d1, pass_d2 = copy(7, (*diag, c), sibling, part=c), copy(8, (*diag, c), sibling, part=1 - c)
        sends = [to_sibling, to_nb1, to_nb2, relay1, relay2, pass_nb1, pass_nb2, pass_d1, pass_d2]
        due = [
            (me, [], []),
            (sibling, [copy(0, sibling, me)], [[]]),
            ((*nb1, c), [copy(1, (*nb1, c), me)], [[pass_nb1, to_nb2]]),
            ((*nb2, 1 - c), [copy(5, (*nb2, 1 - c), me)], [[]]),
            ((*nb2, c), [copy(2, (*nb2, c), me)], [[pass_nb2, relay1, relay2]]),
            ((*nb1, 1 - c), [copy(6, (*nb1, 1 - c), me)], [[]]),
            ((*diag, c), [copy(3, (*diag, c), me, part=c), copy(4, (*diag, c), me, part=1 - c)],
             [[pass_d1], [pass_d2]]),
            ((*diag, 1 - c), [copy(7, (*diag, 1 - c), me, part=1 - c), copy(8, (*diag, 1 - c), me, part=c)],
             [[], []]),
        ]

        @pl.when((j == 0) & (i == 0))
        def _():
            to_sibling.start()
            to_nb1.start()
            load = pltpu.make_async_copy(mine_hbm, wv.at[slot(me)], local_sems.at[N_DEV])
            load.start()
            load.wait()
            keep(0, me).start()

        for step in range(1, N_DEV):
            block, arrivals, then = due[step]

            @pl.when((j == step) & (i == 0))
            def _():
                for arrival, follow in zip(arrivals, then):
                    arrival.wait_recv()
                    for cp in follow:
                        cp.start()
                keep(step, block).start()
                if n_ride and step == N_DEV - 2:
                    gather.begin()

        if n_ride:
            @pl.when((j == N_DEV - 1) & (i == n_i - 1))
            def _():
                gather.pass_on()

        @pl.when(j == 0)
        def _():
            hb = (x_ref[...] * (1.0 + mod_ref[0, 1:2, :]) + mod_ref[0, 0:1, :]).astype(BF16)
            hs[i] = hb
            h_ref[...] = hb

        o_ref[...] = _dot(hs[i], wv[order_ref[j]]).astype(BF16)

        @pl.when((j == N_DEV - 1) & (i == n_i - 1))
        def _():
            for cp in sends:
                cp.wait_send()
            for step in range(N_DEV):
                keep(step, due[step][0]).wait()
            if n_ride:
                gather.finish()

    def first_pass(j, i):
        return jnp.where(j == 0, i, n_i - 1)

    any_spec = pl.BlockSpec(memory_space=pl.ANY)
    res = pl.pallas_call(
        body, name="gather_proj",
        grid_spec=pltpu.PrefetchScalarGridSpec(
            num_scalar_prefetch=1,
            grid=(N_DEV, n_i),
            in_specs=[pl.BlockSpec((tm, kdim), lambda j, i, order_ref: (first_pass(j, i), 0)),
                      pl.BlockSpec((1, 3, kdim), lambda j, i, order_ref: (first_pass(j, i) // per_seq, 0, 0)),
                      any_spec] + [any_spec] * n_ride,
            out_specs=(pl.BlockSpec((tm, ncols), lambda j, i, order_ref: (i, order_ref[j])),
                       pl.BlockSpec((tm, kdim), lambda j, i, order_ref: (first_pass(j, i), 0)), any_spec)
                      + (any_spec,) * n_ride,
            scratch_shapes=[pltpu.VMEM((N_DEV, kdim, ncols), BF16),
                            pltpu.SemaphoreType.DMA((9,)), pltpu.SemaphoreType.DMA((9,)),
                            pltpu.SemaphoreType.DMA((N_DEV + 1,)), pltpu.VMEM((n_i, tm, kdim), BF16)]
                           + (_Gather.scratch(ride_arrs) if n_ride else [])),
        out_shape=(jax.ShapeDtypeStruct((rows, N_DEV * ncols), BF16),
                   jax.ShapeDtypeStruct((rows, kdim), BF16),
                   jax.ShapeDtypeStruct((N_DEV // 2, kdim, 2 * ncols), BF16)) + tuple(ride_shapes),
        compiler_params=pltpu.CompilerParams(vmem_limit_bytes=VMEM_LIMIT),
    )(order, x2, mod3, w_shard, *ride_arrs)
    return res[0], res[1], res[2], res[3:]


SKEW_W = 512


def _bucket_maps():
    lanes = np.arange(SKEW_W)

    def buckets_of(steps):
        rows = []
        for dil in DILATIONS:
            dist = np.maximum(steps, 0) * dil
            nf = np.maximum(dist, 1).astype(np.float32)
            large = 16 + (np.log(nf / np.float32(16)) / np.float32(math.log(128.0))
                          * np.float32(16)).astype(np.int32)
            large = np.minimum(large, N_BUCKETS - 1)
            bucket = np.where(dist < 16, dist, large)
            rows.append(np.where((steps >= 0) & (steps <= N_STEPS), bucket, -1).astype(np.int32))
        return np.stack(rows)[:, None, :]

    a = np.arange(QB)[:, None]
    b = np.arange(2 * QB)[None, :]
    steps = a + QB - b
    band = (steps >= 0) & (steps <= N_STEPS)
    first = band & (b >= QB)
    masks = np.stack([first, band]).astype(np.int32)
    return buckets_of(QB - lanes), buckets_of(2 * QB - 1 - lanes), masks


def _bias_expand(rel_bias, lane_buckets, masks):
    def body(tab_ref, bk_ref, mk_ref, o_ref):
        for g in range(3):
            bk = bk_ref[g]
            for h in range(4):
                col = 4 * g + h
                per_offset = jnp.zeros((1, SKEW_W), F32)
                for k in range(N_BUCKETS):
                    per_offset = jnp.where(bk == k, tab_ref[k, col], per_offset)
                tile = pltpu.roll(jnp.broadcast_to(per_offset, (QB, SKEW_W)), 0, 1, stride=1, stride_axis=0)
                tile = tile[:, :2 * QB]
                o_ref[g, 0, h] = jnp.where(mk_ref[0] != 0, tile, NEG_INF)
                o_ref[g, 1, h] = jnp.where(mk_ref[1] != 0, tile, NEG_INF)

    return pl.pallas_call(
        body, name="bias_expand",
        in_specs=[pl.BlockSpec(memory_space=pltpu.SMEM),
                  pl.BlockSpec(memory_space=pltpu.VMEM),
                  pl.BlockSpec(memory_space=pltpu.VMEM)],
        out_shape=jax.ShapeDtypeStruct((3, 2, 4, QB, 2 * QB), F32),
    )(rel_bias, lane_buckets, masks)


def _bias_grad(ds1, ds2, ds3, lane_buckets):
    exchange = jnp.asarray(np.eye(QB, dtype=np.float32)[::-1].copy())

    def body(d1_ref, d2_ref, d3_ref, bk_ref, ex_ref, o_ref):
        for g, d_ref in enumerate((d1_ref, d2_ref, d3_ref)):
            bk = bk_ref[g]
            for h in range(4):
                flipped = jnp.dot(ex_ref[...], d_ref[h], preferred_element_type=F32,
                                  precision=lax.Precision.HIGHEST)
                padded = jnp.concatenate([flipped, jnp.zeros((QB, SKEW_W - 2 * QB), F32)], axis=1)
                skewed = pltpu.roll(padded, 0, 1, stride=1, stride_axis=0)
                per_offset = jnp.sum(skewed, axis=0, keepdims=True)
                for k in range(N_BUCKETS):
                    o_ref[k, 4 * g + h] = jnp.sum(jnp.where(bk == k, per_offset, 0.0))

    return pl.pallas_call(
        body, name="bias_grad",
        in_specs=[pl.BlockSpec(memory_space=pltpu.VMEM)] * 5,
        out_specs=pl.BlockSpec(memory_space=pltpu.SMEM),
        out_shape=jax.ShapeDtypeStruct((N_BUCKETS, N_HEADS), F32),
    )(ds1, ds2, ds3, lane_buckets, exchange)


def _scratch_sets(rows):
    return 4 if rows <= 512 else 1


def _unit_chunks(dil, size=16):
    units = [(h, r) for h in range(4) for r in range(dil)]
    return [units[i:i + size] for i in range(0, len(units), size)]


def _residue_rows(src_ref, copies, h, residue):
    buf = copies[h % len(copies)]
    buf[...] = src_ref[:, h * HD:(h + 1) * HD].astype(F32)
    return lambda r: buf[residue(r), :].astype(BF16)


def _attn_fwd(proj, bias, g):
    dil = DILATIONS[g]
    rows = QB * dil
    nsb = S // rows
    has_prev = nsb > 1

    def residue(r):
        return pl.ds(r, QB, stride=dil)

    n_sets = _scratch_sets(rows)
    n_in = 6 if has_prev else 4
    n_copied = (4 + (2 if has_prev else 0)) * n_sets

    def body(*refs):
        q_ref, kc_ref, vc_ref = refs[:3]
        kp_ref, vp_ref = refs[3:5] if has_prev else (None, None)
        b_ref = refs[n_in - 1]
        o_ref, l_ref = refs[n_in:n_in + 2]
        scr = list(refs[n_in + 2:])
        ls = [scr.pop(0) for _ in range(4)]
        copies = {name: [scr.pop(0) for _ in range(n_sets)]
                  for name in ("q", "kc", "vc", "o") + (("kp", "vp") if has_prev else ())}
        lane = lax.broadcasted_iota(jnp.int32, (QB, 128), 1)
        refs_of = {"q": q_ref, "kc": kc_ref, "vc": vc_ref, "kp": kp_ref, "vp": vp_ref}
        for chunk in _unit_chunks(dil):
            rows_of = {h: {name: _residue_rows(refs_of[name], copies[name], h, residue)
                           for name in refs_of if refs_of[name] is not None}
                       for h in sorted({h for h, _ in chunk})}

            def batch(name):
                return jnp.stack([rows_of[h][name](r) for h, r in chunk])

            q, k, v = batch("q"), batch("kc"), batch("vc")
            if has_prev:
                k = jnp.concatenate([batch("kp"), k], axis=1)
                v = jnp.concatenate([batch("vp"), v], axis=1)
                bias_b = jnp.stack([b_ref[h] for h, _ in chunk])
            else:
                bias_b = jnp.stack([b_ref[h, :, QB:] for h, _ in chunk])
            s = jnp.einsum("uqd,ukd->uqk", q, k, preferred_element_type=F32) * SCALE + bias_b
            m = jnp.max(s, axis=-1, keepdims=True)
            p = jnp.exp(s - m)
            l = jnp.sum(p, axis=-1, keepdims=True)
            o = jnp.einsum("uqk,ukd->uqd", p.astype(BF16), v, preferred_element_type=F32) / l
            lse = m + jnp.log(l)
            for i, (h, r) in enumerate(chunk):
                copies["o"][h % n_sets][residue(r), :] = o[i]
                ls[h][r * QB:(r + 1) * QB, :] = jnp.where(lane == h, lse[i], 0.0)
            for h in sorted({h for h, _ in chunk}):
                o_ref[:, h * HD:(h + 1) * HD] = copies["o"][h % n_sets][...]
        for r in range(dil):
            blk = slice(r * QB, (r + 1) * QB)
            l_ref[residue(r), :] = (ls[0][blk, :] + ls[1][blk, :]) + (ls[2][blk, :] + ls[3][blk, :])

    def row(b, n):
        return b * nsb + n

    def prev(b, n):
        return b * nsb + jnp.maximum(n - 1, 0)

    in_specs = [
        pl.BlockSpec((rows, GW), lambda b, n: (row(b, n), CB_Q + g)),
        pl.BlockSpec((rows, GW), lambda b, n: (row(b, n), CB_K + g)),
        pl.BlockSpec((rows, GW), lambda b, n: (row(b, n), CB_V + g)),
    ]
    args = [proj, proj, proj]
    scratch = [pltpu.VMEM((rows, 128), F32)] * (4 + n_copied)
    if has_prev:
        in_specs += [pl.BlockSpec((rows, GW), lambda b, n: (prev(b, n), CB_K + g)),
                     pl.BlockSpec((rows, GW), lambda b, n: (prev(b, n), CB_V + g))]
        args += [proj, proj]
    in_specs.append(pl.BlockSpec((None, None, 4, QB, 2 * QB),
                                 lambda b, n: (g, jnp.minimum(n, 1), 0, 0, 0)))
    args.append(bias)
    return pl.pallas_call(
        body, name=f"attn_fwd{g}",
        grid=(BL, nsb),
        in_specs=in_specs,
        out_specs=(pl.BlockSpec((rows, GW), lambda b, n: (row(b, n), 0)),
                   pl.BlockSpec((rows, 128), lambda b, n: (row(b, n), 0))),
        out_shape=(jax.ShapeDtypeStruct((T, GW), F32), jax.ShapeDtypeStruct((T, 128), F32)),
        scratch_shapes=scratch,
        compiler_params=pltpu.CompilerParams(vmem_limit_bytes=VMEM_LIMIT),
    )(*args)


def _attn_bwd(proj, d_out, stats, bias, dproj, g):
    dil = DILATIONS[g]
    rows = QB * dil
    nsb = S // rows
    has_prev = nsb > 1
    n_steps = nsb + 1 if has_prev else 1
    n_in = 7 + (2 if has_prev else 0)

    def residue(r):
        return pl.ds(r, QB, stride=dil)

    n_sets = _scratch_sets(rows)

    def body(*refs):
        q_ref, kc_ref, vc_ref, do_ref, st_ref, b_ref = refs[:6]
        kp_ref, vp_ref = refs[6:8] if has_prev else (None, None)
        out_ref, db_ref = refs[n_in], refs[n_in + 1]
        scr = list(refs[n_in + 2:])
        sq, sk, sv, sems = [scr.pop(0) for _ in range(4)]
        carry = scr.pop(0) if has_prev else None
        sts = scr.pop(0)
        copies = {name: [scr.pop(0) for _ in range(n_sets)]
                  for name in ("q", "kc", "vc", "do", "dq", "dk", "dv") + (("kp", "vp") if has_prev else ())}
        b, n = pl.program_id(0), pl.program_id(1)

        @pl.when((b == 0) & (n == 0))
        def _():
            db_ref[...] = jnp.zeros_like(db_ref)

        def finish(h, r, dq, dk, dv):
            for name, val in (("dq", dq), ("dk", dk), ("dv", dv)):
                copies[name][h % n_sets][residue(r), :] = val

        step = b * n_steps + n
        slot = step % 2

        def stage_copies(s, row0):
            return _column_copies([(sq.at[s], CB * (CB_Q + g)), (sk.at[s], CB * (CB_K + g)),
                                   (sv.at[s], CB * (CB_V + g))], out_ref, row0, sems.at[s])

        @pl.when((step >= 2) & ((step - 2) % n_steps >= (1 if has_prev else 0)))
        def _():
            for cp in stage_copies(slot, 0):
                cp.wait()

        def finish_head(h):
            sl = slice(h * HD, (h + 1) * HD)
            sq[slot, :, sl] = copies["dq"][h % n_sets][...].astype(BF16)
            sk[slot, :, sl] = copies["dk"][h % n_sets][...].astype(BF16)
            sv[slot, :, sl] = copies["dv"][h % n_sets][...].astype(BF16)

        def write_block(blk_idx):
            for cp in stage_copies(slot, pl.multiple_of(blk_idx * rows, rows)):
                cp.start()

            @pl.when(step == BL * n_steps - 1)
            def _():
                for s in range(2):
                    for cp in stage_copies(s, 0):
                        cp.wait()

        def carried(h, r):
            blk = slice(r * QB, (r + 1) * QB)
            return ((blk, slice(h * HD, (h + 1) * HD)), (blk, slice(GW + h * HD, GW + (h + 1) * HD)),
                    (blk, slice(2 * GW + h * HD, 2 * GW + (h + 1) * HD)))

        if has_prev:
            @pl.when(n == 0)
            def _():
                carry[...] = jnp.zeros_like(carry)

            @pl.when(n == nsb)
            def _():
                for h in range(4):
                    for r in range(dil):
                        cq, ck, cv = carried(h, r)
                        finish(h, r, carry[cq], carry[ck], carry[cv])
                    finish_head(h)
                write_block(b * nsb + nsb - 1)

        @pl.when(n < nsb)
        def _():
            for r in range(dil):
                sts[r * QB:(r + 1) * QB, :] = st_ref[residue(r), :]
            refs_of = {"q": q_ref, "kc": kc_ref, "vc": vc_ref, "do": do_ref, "kp": kp_ref, "vp": vp_ref}
            for chunk in _unit_chunks(dil):
                heads = sorted({h for h, _ in chunk})
                rows_of = {h: {name: _residue_rows(refs_of[name], copies[name], h, residue)
                               for name in refs_of if refs_of[name] is not None}
                           for h in heads}

                def batch(name):
                    return jnp.stack([rows_of[h][name](r) for h, r in chunk])

                q, k, v, do = batch("q"), batch("kc"), batch("vc"), batch("do")
                if has_prev:
                    k = jnp.concatenate([batch("kp"), k], axis=1)
                    v = jnp.concatenate([batch("vp"), v], axis=1)
                    bias_b = jnp.stack([b_ref[h] for h, _ in chunk])
                else:
                    bias_b = jnp.stack([b_ref[h, :, QB:] for h, _ in chunk])
                lse = jnp.stack([sts[r * QB:(r + 1) * QB, h:h + 1] for h, r in chunk])
                delta = jnp.stack([sts[r * QB:(r + 1) * QB, 4 + h:5 + h] for h, r in chunk])
                s = jnp.einsum("uqd,ukd->uqk", q, k, preferred_element_type=F32) * SCALE + bias_b
                p = jnp.exp(s - lse)
                ds = p * (jnp.einsum("uqd,ukd->uqk", do, v, preferred_element_type=F32) - delta)
                for h in heads:
                    mine = [ds[i] for i, (hh, _) in enumerate(chunk) if hh == h]
                    tot = mine[0]
                    for extra in mine[1:]:
                        tot = tot + extra
                    if has_prev:
                        db_ref[h] += tot
                    else:
                        db_ref[h, :, QB:] += tot
                dsb, pb = ds.astype(BF16), p.astype(BF16)
                dq = jnp.einsum("uqk,ukd->uqd", dsb, k, preferred_element_type=F32) * SCALE
                dk = jnp.einsum("uqk,uqd->ukd", dsb, q, preferred_element_type=F32) * SCALE
                dv = jnp.einsum("uqk,uqd->ukd", pb, do, preferred_element_type=F32)
                for i, (h, r) in enumerate(chunk):
                    if has_prev:
                        cq, ck, cv = carried(h, r)
                        finish(h, r, carry[cq], carry[ck] + dk[i, :QB], carry[cv] + dv[i, :QB])
                        carry[cq] = dq[i]
                        carry[ck] = dk[i, QB:]
                        carry[cv] = dv[i, QB:]
                    else:
                        finish(h, r, dq[i], dk[i], dv[i])
                for h in heads:
                    finish_head(h)
            if has_prev:
                @pl.when(n > 0)
                def _():
                    write_block(b * nsb + n - 1)
            else:
                write_block(b)

    def row(b, n):
        return b * nsb + jnp.minimum(n, nsb - 1)

    def prev(b, n):
        return b * nsb + jnp.maximum(jnp.minimum(n, nsb - 1) - 1, 0)

    in_specs = [
        pl.BlockSpec((rows, GW), lambda b, n: (row(b, n), CB_Q + g)),
        pl.BlockSpec((rows, GW), lambda b, n: (row(b, n), CB_K + g)),
        pl.BlockSpec((rows, GW), lambda b, n: (row(b, n), CB_V + g)),
        pl.BlockSpec((rows, GW), lambda b, n: (row(b, n), 0)),
        pl.BlockSpec((rows, 128), lambda b, n: (row(b, n), 0)),
        pl.BlockSpec((None, None, 4, QB, 2 * QB),
                     lambda b, n: (g, jnp.minimum(jnp.minimum(n, nsb - 1), 1), 0, 0, 0)),
    ]
    args = [proj, proj, proj, d_out, stats, bias]
    scratch = [pltpu.VMEM((2, rows, GW), BF16)] * 3 + [pltpu.SemaphoreType.DMA((2, 3))]
    if has_prev:
        in_specs += [pl.BlockSpec((rows, GW), lambda b, n: (prev(b, n), CB_K + g)),
                     pl.BlockSpec((rows, GW), lambda b, n: (prev(b, n), CB_V + g))]
        args += [proj, proj]
        scratch.append(pltpu.VMEM((rows, 3 * GW), F32))
    n_copied = (7 + (2 if has_prev else 0)) * n_sets
    scratch += [pltpu.VMEM((rows, 128), F32)] * (1 + n_copied)
    in_specs.append(pl.BlockSpec(memory_space=pl.ANY))
    args.append(dproj)
    return pl.pallas_call(
        body, name=f"attn_bwd{g}",
        grid=(BL, n_steps),
        in_specs=in_specs,
        out_specs=(pl.BlockSpec(memory_space=pl.ANY),
                   pl.BlockSpec((4, QB, 2 * QB), lambda b, n: (0, 0, 0))),
        out_shape=(jax.ShapeDtypeStruct((T, NCOL), BF16),
                   jax.ShapeDtypeStruct((4, QB, 2 * QB), F32)),
        scratch_shapes=scratch,
        input_output_aliases={len(args) - 1: 0},
        compiler_params=pltpu.CompilerParams(vmem_limit_bytes=VMEM_LIMIT),
    )(*args)


def _tail(x2, tgt2, mod3, o_g, lse_g, proj, w_ao, w_co, w_o, conv_w, conv_b, ln_g, ln_b):
    tm = 256
    per_seq = S // tm
    halo = 16

    def body(x_ref, t_ref, mod_ref, o1_ref, o2_ref, o3_ref, l1_ref, l2_ref, l3_ref,
             ga_ref, u_ref, bg_ref, cg_ref, gc_ref, ma_ref, mc_ref, up_ref, cp_ref,
             wao_ref, wco_ref, wo_ref, cw_ref, cb_ref, lg_ref, lb_ref,
             dproj_ref, dyc_ref, do_ref, st_ref, dxd_ref,
             gwo_ref, gwco_ref, gwao_ref, vec_ref,
             dga_s, dbg_s, dgm_s, sems, acc_o, acc_co, acc_ao):
        i = pl.program_id(0)
        bidx = i // per_seq
        first = (i % per_seq) == 0

        @pl.when(i == 0)
        def _():
            vec_ref[...] = jnp.zeros_like(vec_ref)

        slot = i % 2

        def column_copies(s, row0):
            return _column_copies([(dga_s.at[s], CB * CB_GA), (dbg_s.at[s], D * KB_BG), (dgm_s.at[s], D * KB_GC)],
                                  dproj_ref, row0, sems.at[s])

        @pl.when(i >= 2)
        def _():
            for cp in column_copies(slot, 0):
                cp.wait()

        l1, l2, l3 = l1_ref[...], l2_ref[...], l3_ref[...]
        mx = jnp.maximum(jnp.maximum(l1, l2), l3)
        e1, e2, e3 = jnp.exp(l1 - mx), jnp.exp(l2 - mx), jnp.exp(l3 - mx)
        esum = e1 + e2 + e3
        lse_tot = mx + jnp.log(esum)
        w1, w2, w3 = e1 / esum, e2 / esum, e3 / esum

        def per_head(wv):
            return jnp.concatenate([jnp.broadcast_to(wv[:, h:h + 1], (tm, HD)) for h in range(4)], axis=1)

        o = per_head(w1) * o1_ref[...] + per_head(w2) * o2_ref[...] + per_head(w3) * o3_ref[...]

        ga = ga_ref[...].astype(F32)
        sig_ga = _sigmoid(ga)
        silu_ga = ga * sig_ga
        a_in = (o * silu_ga).astype(BF16)
        a_out = _dot(a_in, wao_ref[...])

        u = u_ref[...].astype(F32)
        cg = cg_ref[...].astype(F32)
        z = cg * u
        zp = cp_ref[...].astype(F32) * up_ref[...].astype(F32)
        zp = jnp.where(first, 0.0, zp)
        zcat = jnp.concatenate([zp, z], axis=0)
        z1 = pltpu.roll(zcat, 1, 0)[halo:]
        z2 = pltpu.roll(zcat, 2, 0)[halo:]
        y_conv = cw_ref[0:1, :] * z2 + cw_ref[1:2, :] * z1 + cw_ref[2:3, :] * z + cb_ref[...]
        gc = gc_ref[...].astype(F32)
        sig_gc = _sigmoid(gc)
        silu_gc = gc * sig_gc
        bg = bg_ref[...].astype(F32)
        bg_yc = bg * y_conv
        s_in = (bg_yc * silu_gc).astype(BF16)
        s_out = _dot(s_in, wco_ref[...])

        sa = _sigmoid(ma_ref[...].astype(F32))
        sc = _sigmoid(mc_ref[...].astype(F32))
        merged = (sa * a_out + sc * s_out).astype(BF16)
        y = _dot(merged, wo_ref[...])
        gate1 = 1.0 + mod_ref[0, 2:3, :]
        xv = x_ref[...]
        resid = ALPHA * xv + gate1 * y
        mu = jnp.mean(resid, axis=1, keepdims=True)
        xc = resid - mu
        var = jnp.mean(xc * xc, axis=1, keepdims=True)
        rstd = lax.rsqrt(var + LN_EPS)
        xhat = xc * rstd
        lg = lg_ref[...]
        err = xhat * lg + lb_ref[...] - t_ref[...]
        vec_ref[3:4, :] += (0.5 / D) * jnp.sum(err * err, axis=0, keepdims=True)

        vec_ref[1:2, :] += (1.0 / D) * jnp.sum(err * xhat, axis=0, keepdims=True)
        vec_ref[2:3, :] += (1.0 / D) * jnp.sum(err, axis=0, keepdims=True)
        dxh = err * (lg * (1.0 / D))
        dres = rstd * (dxh - jnp.mean(dxh, axis=1, keepdims=True)
                       - xhat * jnp.mean(dxh * xhat, axis=1, keepdims=True))
        dxd_ref[...] = ALPHA * dres
        dgate = jnp.sum(dres * y, axis=0, keepdims=True)
        vec_ref[4:5, :] += jnp.where(bidx == 0, dgate, 0.0)
        vec_ref[5:6, :] += jnp.where(bidx == 1, dgate, 0.0)
        dy = (dres * gate1).astype(BF16)

        dmerged = _dot_nt(dy, wo_ref[...])
        da_out_f = dmerged * sa
        ds_out_f = dmerged * sc
        da_out = da_out_f.astype(BF16)
        ds_out = ds_out_f.astype(BF16)
        dgm_s[slot, :, 2 * D:3 * D] = (ds_out_f * s_out * (1.0 - sc)).astype(BF16)
        dgm_s[slot, :, D:2 * D] = (da_out_f * a_out * (1.0 - sa)).astype(BF16)
        da_in = _dot_nt(da_out, wao_ref[...])
        ds_in = _dot_nt(ds_out, wco_ref[...])

        d_o = da_in * silu_ga
        do_ref[...] = d_o.astype(BF16)
        dga_s[slot] = (da_in * o * (sig_ga + silu_ga * (1.0 - sig_ga))).astype(BF16)
        lane = lax.broadcasted_iota(jnp.int32, (tm, 128), 1)
        stats = lse_tot
        od = o * d_o
        for h in range(4):
            delta = jnp.sum(od[:, h * HD:(h + 1) * HD], axis=1, keepdims=True)
            stats = jnp.where(lane == 4 + h, delta, stats)
        st_ref[...] = stats

        ds_silu = ds_in * silu_gc
        dbg_s[slot] = (ds_silu * y_conv).astype(BF16)
        dyc = ds_silu * bg
        dyc_ref[...] = dyc
        vec_ref[0:1, :] += jnp.sum(dyc, axis=0, keepdims=True)
        dgm_s[slot, :, 0:D] = (ds_in * bg_yc * (sig_gc + silu_gc * (1.0 - sig_gc))).astype(BF16)

        @pl.when(i == 0)
        def _():
            acc_o[...] = jnp.zeros_like(acc_o)
            acc_co[...] = jnp.zeros_like(acc_co)
            acc_ao[...] = jnp.zeros_like(acc_ao)

        acc_o[...] += _dot_tn(merged, dy)
        acc_co[...] += _dot_tn(s_in, ds_out)
        acc_ao[...] += _dot_tn(a_in, da_out)

        for cp in column_copies(slot, pl.multiple_of(i * tm, tm)):
            cp.start()

        @pl.when(i == T // tm - 1)
        def _():
            gwo_ref[...] = acc_o[...].astype(BF16)
            gwco_ref[...] = acc_co[...].astype(BF16)
            gwao_ref[...] = acc_ao[...].astype(BF16)
            for s in range(2):
                for cp in column_copies(s, 0):
                    cp.wait()

    def tile(width, cblk=0):
        return pl.BlockSpec((tm, width), lambda i: (i, cblk))

    def whole(shape):
        return pl.BlockSpec(shape, lambda i: tuple(0 for _ in shape))

    def once(shape):
        return pl.BlockSpec(shape, lambda i: tuple(0 for _ in shape), pipeline_mode=pl.Buffered(1))

    prev_rows = lambda i: (jnp.maximum(i * (tm // halo) - 1, 0),)
    in_specs = [
        tile(D), tile(D), pl.BlockSpec((1, 3, D), lambda i: (i // per_seq, 0, 0)),
        tile(GW), tile(GW), tile(GW), tile(128), tile(128), tile(128),
        tile(GW, CB_GA), tile(D, KB_U), tile(D, KB_BG), tile(D, KB_CG), tile(D, KB_GC),
        tile(D, KB_MA), tile(D, KB_MC),
        pl.BlockSpec((halo, D), lambda i: (*prev_rows(i), KB_U)),
        pl.BlockSpec((halo, D), lambda i: (*prev_rows(i), KB_CG)),
        whole((GW, D)), whole((D, D)), whole((D, D)),
        whole((3, D)), whole((1, D)), whole((1, D)), whole((1, D)),
    ]
    out_specs = (
        pl.BlockSpec(memory_space=pl.ANY), tile(D), tile(GW), tile(128), tile(D),
        once((D, D)), once((D, D)), once((GW, D)),
        pl.BlockSpec((8, D), lambda i: (0, 0)),
    )
    out_shape = (
        jax.ShapeDtypeStruct((T, NCOL), BF16),
        jax.ShapeDtypeStruct((T, D), F32),
        jax.ShapeDtypeStruct((T, GW), BF16),
        jax.ShapeDtypeStruct((T, 128), F32),
        jax.ShapeDtypeStruct((T, D), F32),
        jax.ShapeDtypeStruct((D, D), BF16),
        jax.ShapeDtypeStruct((D, D), BF16),
        jax.ShapeDtypeStruct((GW, D), BF16),
        jax.ShapeDtypeStruct((8, D), F32),
    )
    return pl.pallas_call(
        body, name="tail",
        grid=(T // tm,),
        in_specs=in_specs, out_specs=out_specs, out_shape=out_shape,
        scratch_shapes=[pltpu.VMEM((2, tm, GW), BF16), pltpu.VMEM((2, tm, D), BF16), pltpu.VMEM((2, tm, 3 * D), BF16),
                        pltpu.SemaphoreType.DMA((2, 3)),
                        pltpu.VMEM((D, D), F32), pltpu.VMEM((D, D), F32), pltpu.VMEM((GW, D), F32)],
        compiler_params=pltpu.CompilerParams(vmem_limit_bytes=VMEM_LIMIT_TAIL),
    )(x2, tgt2, mod3, *o_g, *lse_g, proj, proj, proj, proj, proj, proj, proj, proj, proj,
      w_ao, w_co, w_o, conv_w, conv_b, ln_g, ln_b)


def _conv_bwd(dyc, proj, conv_w, dproj):
    tm = 512
    per_seq = S // tm

    def body(d_ref, dn_ref, u_ref, c_ref, cw_ref, _, dproj_ref, g_ref, du_s, dc_s, sems):
        i = pl.program_id(0)
        last = (i % per_seq) == per_seq - 1

        @pl.when(i == 0)
        def _():
            g_ref[...] = jnp.zeros_like(g_ref)

        slot = i % 2

        def column_copies(s, row0):
            return _column_copies([(du_s.at[s], D * KB_U), (dc_s.at[s], D * KB_CG)], dproj_ref, row0, sems.at[s])

        @pl.when(i >= 2)
        def _():
            for cp in column_copies(slot, 0):
                cp.wait()

        d = d_ref[...]
        dn = jnp.where(last, 0.0, dn_ref[...])
        dcat = jnp.concatenate([d, dn], axis=0)
        d1 = pltpu.roll(dcat, tm + 8 - 1, 0)[:tm]
        d2 = pltpu.roll(dcat, tm + 8 - 2, 0)[:tm]
        dz = cw_ref[2:3, :] * d + cw_ref[1:2, :] * d1 + cw_ref[0:1, :] * d2
        u = u_ref[...].astype(F32)
        cg = c_ref[...].astype(F32)
        du_s[slot] = (dz * cg).astype(BF16)
        dc_s[slot] = (dz * u).astype(BF16)
        for cp in column_copies(slot, pl.multiple_of(i * tm, tm)):
            cp.start()

        z = cg * u
        g_ref[0:1, :] += jnp.sum(d2 * z, axis=0, keepdims=True)
        g_ref[1:2, :] += jnp.sum(d1 * z, axis=0, keepdims=True)
        g_ref[2:3, :] += jnp.sum(d * z, axis=0, keepdims=True)

        @pl.when(i == T // tm - 1)
        def _():
            for s in range(2):
                for cp in column_copies(s, 0):
                    cp.wait()

    n_tiles = T // tm
    next_rows = lambda i: jnp.minimum((i + 1) * (tm // 8), T // 8 - 1)
    return pl.pallas_call(
        body, name="conv_bwd",
        grid=(n_tiles,),
        in_specs=[pl.BlockSpec((tm, D), lambda i: (i, 0)),
                  pl.BlockSpec((8, D), lambda i: (next_rows(i), 0)),
                  pl.BlockSpec((tm, D), lambda i: (i, KB_U)),
                  pl.BlockSpec((tm, D), lambda i: (i, KB_CG)),
                  pl.BlockSpec((3, D), lambda i: (0, 0)),
                  pl.BlockSpec(memory_space=pl.ANY)],
        out_specs=(pl.BlockSpec(memory_space=pl.ANY),
                   pl.BlockSpec((8, D), lambda i: (0, 0))),
        out_shape=(jax.ShapeDtypeStruct((T, NCOL), BF16),
                   jax.ShapeDtypeStruct((8, D), F32)),
        scratch_shapes=[pltpu.VMEM((2, tm, D), BF16), pltpu.VMEM((2, tm, D), BF16), pltpu.SemaphoreType.DMA((2, 2))],
        input_output_aliases={5: 0},
        compiler_params=pltpu.CompilerParams(vmem_limit_bytes=VMEM_LIMIT),
    )(dyc, dyc, proj, proj, conv_w, dproj)


def _dh_dx(dproj, w_in_all, x2, dxd, mod3, chip_sums, hops=(), parts0=None):
    tm = 512
    per_seq = S // tm
    n_pass, _, width = w_in_all.shape
    n = len(chip_sums)
    n_in = 5 + n + (0 if parts0 is None else 1)

    def body(*refs):
        d_ref, w_ref, x_ref, dxd_ref, mod_ref = refs[:5]
        ins = refs[5:5 + n]
        gx_ref, vec_ref = refs[n_in:n_in + 2]
        outs = refs[n_in + 2:n_in + 2 + n]
        acc, send_sems, recv_sems, local_sems = refs[n_in + 2 + n:]
        jj, i = pl.program_id(0), pl.program_id(1)

        @pl.when((i == 0) & (jj == 0))
        def _():
            vec_ref[...] = jnp.zeros_like(vec_ref)
            if n:
                sends, _, mine = _chip_copies(ins, outs, send_sems, recv_sems, local_sems, hops)
                for cp in sends + mine:
                    cp.start()

        if n:
            @pl.when((i == T // tm - 1) & (jj == n_pass - 1))
            def _():
                sends, arrivals, mine = _chip_copies(ins, outs, send_sems, recv_sems, local_sems, hops)
                for cp in arrivals:
                    cp.wait_recv()
                for cp in sends:
                    cp.wait_send()
                for cp in mine:
                    cp.wait()

        def partial():
            return _dot_nt(d_ref[...], w_ref[...])

        @pl.when(jj == 0)
        def _():
            acc[i] = partial()

        @pl.when((jj > 0) & (jj < n_pass - 1))
        def _():
            acc[i] += partial()

        @pl.when(jj == n_pass - 1)
        def _():
            dh = acc[i] + partial()
            bidx = i // per_seq
            gx_ref[...] = dxd_ref[...] + dh * (1.0 + mod_ref[0, 1:2, :])
            dshift = jnp.sum(dh, axis=0, keepdims=True)
            dscale = jnp.sum(dh * x_ref[...], axis=0, keepdims=True)
            vec_ref[0:1, :] += jnp.where(bidx == 0, dshift, 0.0)
            vec_ref[1:2, :] += jnp.where(bidx == 1, dshift, 0.0)
            vec_ref[2:3, :] += jnp.where(bidx == 0, dscale, 0.0)
            vec_ref[3:4, :] += jnp.where(bidx == 1, dscale, 0.0)

    def last_pass(jj, i):
        return jnp.where(jj == n_pass - 1, i, 0)

    any_spec = pl.BlockSpec(memory_space=pl.ANY)
    res = pl.pallas_call(
        body, name="dh_dx",
        grid=(n_pass, T // tm),
        in_specs=[
            pl.BlockSpec((tm, width), lambda jj, i: (i, jj)),
            pl.BlockSpec((None, D, width), lambda jj, i: (jj, 0, 0)),
            pl.BlockSpec((tm, D), lambda jj, i: (last_pass(jj, i), 0)),
            pl.BlockSpec((tm, D), lambda jj, i: (last_pass(jj, i), 0)),
            pl.BlockSpec((1, 3, D), lambda jj, i: (last_pass(jj, i) // per_seq, 0, 0))]
                 + [any_spec] * (n_in - 5),
        out_specs=(pl.BlockSpec((tm, D), lambda jj, i: (last_pass(jj, i), 0)),
                   pl.BlockSpec((8, D), lambda jj, i: (0, 0))) + (any_spec,) * n,
        out_shape=(jax.ShapeDtypeStruct((T, D), F32), jax.ShapeDtypeStruct((8, D), F32))
                  + tuple(jax.ShapeDtypeStruct(a.shape, a.dtype) for a in chip_sums),
        scratch_shapes=[pltpu.VMEM((T // tm, tm, D), F32), pltpu.SemaphoreType.DMA((max(3 * n, 1),)),
                        pltpu.SemaphoreType.DMA((max(3 * n, 1),)), pltpu.SemaphoreType.DMA((max(n, 1),))],
        input_output_aliases={} if parts0 is None else {5 + n: 2},
        compiler_params=pltpu.CompilerParams(vmem_limit_bytes=VMEM_LIMIT),
    )(dproj, w_in_all, x2, dxd, mod3, *chip_sums, *([] if parts0 is None else [parts0]))
    return res[0], res[1], res[2:]


def _adam_step(g, w, m, v):
    nm = ADAM_B1 * m + (1.0 - ADAM_B1) * g
    nv = ADAM_B2 * v + (1.0 - ADAM_B2) * (g * g)
    m_hat = nm / (1.0 - ADAM_B1 ** ADAM_STEP)
    v_hat = nv / (1.0 - ADAM_B2 ** ADAM_STEP)
    return -ADAM_LR * (m_hat / (jnp.sqrt(v_hat) + ADAM_EPS) + ADAM_WD * w), nm, nv


def _adamw(parts, w, m, v, name, row_tile=None):
    n_parts, rows, cols = parts.shape
    tr = rows if row_tile is None else row_tile

    def body(p_ref, w_ref, m_ref, v_ref, g_ref, d_ref, nm_ref, nv_ref):
        g = p_ref[0].astype(F32)
        for s in range(1, n_parts):
            g = g + p_ref[s].astype(F32)
        g_ref[...] = g
        d_ref[...], nm_ref[...], nv_ref[...] = _adam_step(g, w_ref[...], m_ref[...], v_ref[...])

    blk = pl.BlockSpec((tr, cols), lambda i: (i, 0))
    shp = jax.ShapeDtypeStruct((rows, cols), F32)
    return pl.pallas_call(
        body, name=name,
        grid=(rows // tr,),
        in_specs=[pl.BlockSpec((n_parts, tr, cols), lambda i: (0, i, 0)), blk, blk, blk],
        out_specs=(blk, blk, blk, blk),
        out_shape=(shp, shp, shp, shp),
        compiler_params=pltpu.CompilerParams(vmem_limit_bytes=VMEM_LIMIT),
    )(parts, w, m, v)


def _multi_adamw(parts_list, params, name):
    n = len(params)
    flat = [t for wmv in params for t in wmv]

    def body(*refs):
        parts, ins, outs = refs[:n], refs[n:4 * n], refs[4 * n:]
        for p in range(n):
            g = parts[p][0].astype(F32)
            for s in range(1, parts[p].shape[0]):
                g = g + parts[p][s].astype(F32)
            w_ref, m_ref, v_ref = ins[3 * p:3 * p + 3]
            g_ref, d_ref, nm_ref, nv_ref = outs[4 * p:4 * p + 4]
            g_ref[...] = g
            d_ref[...], nm_ref[...], nv_ref[...] = _adam_step(g, w_ref[...], m_ref[...], v_ref[...])

    out_shape = []
    for w, _, _ in params:
        out_shape += [jax.ShapeDtypeStruct(w.shape, F32)] * 4
    res = pl.pallas_call(body, name=name, out_shape=tuple(out_shape))(*parts_list, *flat)
    return [res[4 * p:4 * p + 4] for p in range(n)]


def _small_updates(small_g, dmod_all, rel_parts, params):
    flat = [t for wmv in params for t in wmv]

    def body(sg_ref, dm_ref, rp_ref, *refs):
        ins, outs = refs[:len(flat)], refs[len(flat):]

        def over_devices(row):
            tot = sg_ref[0, row:row + 1, :]
            for s in range(1, N_DEV):
                tot = tot + sg_ref[s, row:row + 1, :]
            return tot

        g_b_ada = dm_ref[0:1, :]
        for r in range(1, N_DEV * BL):
            g_b_ada = g_b_ada + dm_ref[r:r + 1, :]
        g_rel = rp_ref[0]
        for s in range(1, N_DEV):
            g_rel = g_rel + rp_ref[s]
        grads = [g_b_ada, over_devices(0), g_rel, over_devices(1), over_devices(2)]
        outs[0][...] = jnp.sum(over_devices(3), axis=1, keepdims=True)
        for p, g in enumerate(grads):
            w_ref, m_ref, v_ref = ins[3 * p:3 * p + 3]
            g_ref, d_ref, nm_ref, nv_ref = outs[1 + 4 * p:5 + 4 * p]
            g_ref[...] = g
            d_ref[...], nm_ref[...], nv_ref[...] = _adam_step(g, w_ref[...], m_ref[...], v_ref[...])

    out_shape = [jax.ShapeDtypeStruct((1, 1), F32)]
    for w, _, _ in params:
        out_shape += [jax.ShapeDtypeStruct(w.shape, F32)] * 4
    res = pl.pallas_call(body, name="small_updates", out_shape=tuple(out_shape))(small_g, dmod_all, rel_parts, *flat)
    return res[0], [res[1 + 4 * p:5 + 4 * p] for p in range(len(params))]


def _attn_fwd_dense(proj, bias):
    nq = 4
    rows = nq * QB
    nsb = S // rows

    def body(q_ref, k_ref, v_ref, kp_ref, vp_ref, b_ref, o_ref, l_ref, ls0, ls1, ls2, ls3):
        ls = [ls0, ls1, ls2, ls3]
        n = pl.program_id(1)
        lane = lax.broadcasted_iota(jnp.int32, (QB, 128), 1)
        units = [(h, j) for h in range(4) for j in range(nq)]

        def keys(cur_ref, prev_ref, h, j):
            sl = slice(h * HD, (h + 1) * HD)
            if j == 0:
                return jnp.concatenate([prev_ref[:, sl], cur_ref[0:QB, sl]], axis=0)
            return cur_ref[(j - 1) * QB:(j + 1) * QB, sl]

        q = jnp.stack([q_ref[j * QB:(j + 1) * QB, h * HD:(h + 1) * HD] for h, j in units])
        k = jnp.stack([keys(k_ref, kp_ref, h, j) for h, j in units])
        v = jnp.stack([keys(v_ref, vp_ref, h, j) for h, j in units])
        bias_b = jnp.stack([b_ref[jnp.minimum(n, 1), h] if j == 0 else b_ref[1, h] for h, j in units])
        s = jnp.einsum("uqd,ukd->uqk", q, k, preferred_element_type=F32) * SCALE + bias_b
        m = jnp.max(s, axis=-1, keepdims=True)
        p = jnp.exp(s - m)
        l = jnp.sum(p, axis=-1, keepdims=True)
        o = jnp.einsum("uqk,ukd->uqd", p.astype(BF16), v, preferred_element_type=F32) / l
        lse = m + jnp.log(l)
        for i, (h, j) in enumerate(units):
            o_ref[j * QB:(j + 1) * QB, h * HD:(h + 1) * HD] = o[i]
            ls[h][j * QB:(j + 1) * QB, :] = jnp.where(lane == h, lse[i], 0.0)
        l_ref[...] = (ls[0][...] + ls[1][...]) + (ls[2][...] + ls[3][...])

    def row(b, n):
        return b * nsb + n

    def prev(b, n):
        return jnp.maximum((b * nsb + n) * nq - 1, 0)

    in_specs = [
        pl.BlockSpec((rows, GW), lambda b, n: (row(b, n), CB_Q)),
        pl.BlockSpec((rows, GW), lambda b, n: (row(b, n), CB_K)),
        pl.BlockSpec((rows, GW), lambda b, n: (row(b, n), CB_V)),
        pl.BlockSpec((QB, GW), lambda b, n: (prev(b, n), CB_K)),
        pl.BlockSpec((QB, GW), lambda b, n: (prev(b, n), CB_V)),
        pl.BlockSpec((None, 2, 4, QB, 2 * QB), lambda b, n: (0, 0, 0, 0, 0)),
    ]
    return pl.pallas_call(
        body, name="attn_fwd0",
        grid=(BL, nsb),
        in_specs=in_specs,
        out_specs=(pl.BlockSpec((rows, GW), lambda b, n: (row(b, n), 0)),
                   pl.BlockSpec((rows, 128), lambda b, n: (row(b, n), 0))),
        out_shape=(jax.ShapeDtypeStruct((T, GW), F32), jax.ShapeDtypeStruct((T, 128), F32)),
        scratch_shapes=[pltpu.VMEM((rows, 128), F32)] * 4,
        compiler_params=pltpu.CompilerParams(vmem_limit_bytes=VMEM_LIMIT),
    )(proj, proj, proj, proj, proj, bias)


def _attn_bwd_dense(proj, d_out, stats, bias, dproj):
    nq = 4
    rows = nq * QB
    nsb = S // rows
    cols_q, cols_k, cols_v = CB * CB_Q, CB * CB_K, CB * CB_V

    def body(q_ref, k_ref, v_ref, do_ref, st_ref, kp_ref, vp_ref, b_ref, _, out_ref, db_ref,
             sq, sk, sv, carry, sems):
        b, n = pl.program_id(0), pl.program_id(1)
        units = [(h, j) for h in range(4) for j in range(nq)]

        @pl.when((b == 0) & (n == 0))
        def _():
            db_ref[...] = jnp.zeros_like(db_ref)

        @pl.when(n == 0)
        def _():
            carry[...] = jnp.zeros_like(carry)

        step = b * (nsb + 1) + n
        slot = step % 2

        def block_copies(s, position, block):
            part = pl.ds(position * QB, QB)
            return _column_copies([(sq.at[s, part], cols_q), (sk.at[s, part], cols_k), (sv.at[s, part], cols_v)],
                                  out_ref, pl.multiple_of(block * QB, QB), sems.at[s, position])

        def wait_blocks(s, positions):
            for position in positions:
                for cp in block_copies(s, position, 0):
                    cp.wait()

        before = (step - 2) % (nsb + 1)

        @pl.when((step >= 2) & (before > 0))
        def _():
            wait_blocks(slot, [0])

        @pl.when((step >= 2) & (before < nsb))
        def _():
            wait_blocks(slot, range(1, nq))

        def write(first_block, position, count):
            for j in range(count):
                for cp in block_copies(slot, position + j, first_block + j):
                    cp.start()

        @pl.when(n == nsb)
        def _():
            sq[slot, 0:QB, :] = carry[:, 0:GW].astype(BF16)
            sk[slot, 0:QB, :] = carry[:, GW:2 * GW].astype(BF16)
            sv[slot, 0:QB, :] = carry[:, 2 * GW:3 * GW].astype(BF16)
            write((b + 1) * nsb * nq - 1, 0, 1)

            @pl.when(b == BL - 1)
            def _():
                wait_blocks(slot, [0])
                wait_blocks(1 - slot, range(nq))

        @pl.when(n < nsb)
        def _():
            def keys(cur_ref, prev_ref, h, j):
                sl = slice(h * HD, (h + 1) * HD)
                if j == 0:
                    return jnp.concatenate([prev_ref[:, sl], cur_ref[0:QB, sl]], axis=0)
                return cur_ref[(j - 1) * QB:(j + 1) * QB, sl]

            def block(ref, h, j):
                return ref[j * QB:(j + 1) * QB, h * HD:(h + 1) * HD]

            q = jnp.stack([block(q_ref, h, j) for h, j in units])
            do = jnp.stack([block(do_ref, h, j) for h, j in units])
            k = jnp.stack([keys(k_ref, kp_ref, h, j) for h, j in units])
            v = jnp.stack([keys(v_ref, vp_ref, h, j) for h, j in units])
            bias_b = jnp.stack([b_ref[jnp.minimum(n, 1), h] if j == 0 else b_ref[1, h] for h, j in units])
            lse = jnp.stack([st_ref[j * QB:(j + 1) * QB, h:h + 1] for h, j in units])
            delta = jnp.stack([st_ref[j * QB:(j + 1) * QB, 4 + h:5 + h] for h, j in units])
            s = jnp.einsum("uqd,ukd->uqk", q, k, preferred_element_type=F32) * SCALE + bias_b
            p = jnp.exp(s - lse)
            ds = p * (jnp.einsum("uqd,ukd->uqk", do, v, preferred_element_type=F32) - delta)
            for h in range(4):
                tot = ds[h * nq]
                for j in range(1, nq):
                    tot = tot + ds[h * nq + j]
                db_ref[h] += tot
            dsb, pb = ds.astype(BF16), p.astype(BF16)
            dq = jnp.einsum("uqk,ukd->uqd", dsb, k, preferred_element_type=F32) * SCALE
            dk = jnp.einsum("uqk,uqd->ukd", dsb, q, preferred_element_type=F32) * SCALE
            dv = jnp.einsum("uqk,uqd->ukd", pb, do, preferred_element_type=F32)
            for h in range(4):
                sl = slice(h * HD, (h + 1) * HD)
                u0, last = h * nq, h * nq + nq - 1
                sq[slot, 0:QB, sl] = carry[:, sl].astype(BF16)
                sk[slot, 0:QB, sl] = (carry[:, GW + h * HD:GW + (h + 1) * HD] + dk[u0, :QB]).astype(BF16)
                sv[slot, 0:QB, sl] = (carry[:, 2 * GW + h * HD:2 * GW + (h + 1) * HD] + dv[u0, :QB]).astype(BF16)
                for j in range(nq - 1):
                    pos = slice((j + 1) * QB, (j + 2) * QB)
                    sq[slot, pos, sl] = dq[u0 + j].astype(BF16)
                    sk[slot, pos, sl] = (dk[u0 + j, QB:] + dk[u0 + j + 1, :QB]).astype(BF16)
                    sv[slot, pos, sl] = (dv[u0 + j, QB:] + dv[u0 + j + 1, :QB]).astype(BF16)
                carry[:, sl] = dq[last]
                carry[:, GW + h * HD:GW + (h + 1) * HD] = dk[last, QB:]
                carry[:, 2 * GW + h * HD:2 * GW + (h + 1) * HD] = dv[last, QB:]

            @pl.when(n == 0)
            def _():
                write(b * nsb * nq, 1, nq - 1)

            @pl.when(n > 0)
            def _():
                write((b * nsb + n) * nq - 1, 0, nq)

    def row(b, n):
        return b * nsb + jnp.minimum(n, nsb - 1)

    def prev(b, n):
        return jnp.maximum(row(b, n) * nq - 1, 0)

    in_specs = [
        pl.BlockSpec((rows, GW), lambda b, n: (row(b, n), CB_Q)),
        pl.BlockSpec((rows, GW), lambda b, n: (row(b, n), CB_K)),
        pl.BlockSpec((rows, GW), lambda b, n: (row(b, n), CB_V)),
        pl.BlockSpec((rows, GW), lambda b, n: (row(b, n), 0)),
        pl.BlockSpec((rows, 128), lambda b, n: (row(b, n), 0)),
        pl.BlockSpec((QB, GW), lambda b, n: (prev(b, n), CB_K)),
        pl.BlockSpec((QB, GW), lambda b, n: (prev(b, n), CB_V)),
        pl.BlockSpec((None, 2, 4, QB, 2 * QB), lambda b, n: (0, 0, 0, 0, 0)),
        pl.BlockSpec(memory_space=pl.ANY),
    ]
    return pl.pallas_call(
        body, name="attn_bwd0",
        grid=(BL, nsb + 1),
        in_specs=in_specs,
        out_specs=(pl.BlockSpec(memory_space=pl.ANY),
                   pl.BlockSpec((4, QB, 2 * QB), lambda b, n: (0, 0, 0))),
        out_shape=(jax.ShapeDtypeStruct((T, NCOL), BF16),
                   jax.ShapeDtypeStruct((4, QB, 2 * QB), F32)),
        scratch_shapes=[pltpu.VMEM((2, rows, GW), BF16)] * 3
                       + [pltpu.VMEM((QB, 3 * GW), F32), pltpu.SemaphoreType.DMA((2, nq, 3))],
        input_output_aliases={8: 0},
        compiler_params=pltpu.CompilerParams(vmem_limit_bytes=VMEM_LIMIT),
    )(proj, proj, proj, d_out, stats, proj, proj, bias, dproj)


def _attention_forward(proj, rel_bias):
    expand_lanes, grad_lanes, masks = (jnp.asarray(t) for t in _bucket_maps())
    bias = _bias_expand(rel_bias, expand_lanes, masks)
    fwd = [_attn_fwd_dense(proj, bias)] + [_attn_fwd(proj, bias, g) for g in (1, 2)]
    return bias, grad_lanes, [f[0] for f in fwd], [f[1] for f in fwd]


def _local_step(x2, tgt2, mod3, h, proj, attn, w_ao, w_co, w_o, conv_w, conv_b, ln_g, ln_b):
    bias, buckets, o_g, lse_g = attn

    (dproj, dyc, d_o, stats, dxd, gw_o, gw_co, gw_ao, tail_vec) = _tail(
        x2, tgt2, mod3, o_g, lse_g, proj, w_ao, w_co, w_o, conv_w, conv_b, ln_g, ln_b)

    dproj, db = _attn_bwd_dense(proj, d_o, stats, bias, dproj)
    dbias = [db]
    for g in (1, 2):
        dproj, db = _attn_bwd(proj, d_o, stats, bias, dproj, g)
        dbias.append(db)
    g_rel_bias = _bias_grad(*dbias, buckets)
    dproj, conv_vec = _conv_bwd(dyc, proj, conv_w, dproj)

    gw_ao = jnp.transpose(gw_ao.reshape(GW, N_DEV, D // N_DEV), (1, 0, 2))
    return dproj, dxd, gw_ao, gw_co, gw_o, conv_vec, g_rel_bias, tail_vec


def kernel(x, c, w_ada, b_ada, w_in, conv_w, conv_b, rel_bias, w_attn_out, w_conv_out, w_o, ln_g, ln_b, loss_target, m_w_ada, m_b_ada, m_w_in, m_conv_w, m_conv_b, m_rel_bias, m_w_attn_out, m_w_conv_out, m_w_o, m_ln_g, m_ln_b, v_w_ada, v_b_ada, v_w_in, v_conv_w, v_conv_b, v_rel_bias, v_w_attn_out, v_w_conv_out, v_w_o, v_ln_g, v_ln_b):
    me = _my_index()
    x2 = x.reshape(T, D)
    tgt2 = loss_target.reshape(T, D)

    b_cols = lax.dynamic_slice(b_ada, (0, me * ADA_SHARD), (1, ADA_SHARD))
    c_g, mod_in = _mod_exchange(jnp.pad(c, ((0, 8 - BL), (0, 0))), w_ada[0], b_cols)
    c_all = c_g[:, 0:BL, :].reshape(N_DEV * BL, D)
    mod3 = jnp.transpose(mod_in[:, 0:BL, :], (1, 0, 2)).reshape(BL, 3, D)

    rows_shape = jax.ShapeDtypeStruct((N_DEV, D // N_DEV, D), BF16)
    proj, h, w_in_all, (w_ao_g, w_co_g, w_o_g, conv_w_g) = _gather_proj(
        _shard_order(), x2, mod3, w_in[0].astype(BF16), 1024,
        ([w_attn_out[0].astype(BF16), w_conv_out[0].astype(BF16), w_o[0].astype(BF16), conv_w[0]],
         [jax.ShapeDtypeStruct((N_DEV, GW, D // N_DEV), BF16), rows_shape, rows_shape,
          jax.ShapeDtypeStruct((N_DEV, 3, D // N_DEV), F32)]))

    attn = _attention_forward(proj, rel_bias)
    w_ao_full = jnp.transpose(w_ao_g, (1, 0, 2)).reshape(GW, D)
    w_co_full = w_co_g.reshape(D, D)
    w_o_full = w_o_g.reshape(D, D)
    conv_w_full = jnp.transpose(conv_w_g, (1, 0, 2)).reshape(3, D)

    (dproj, dxd, gw_ao, gw_co, gw_o, conv_vec, g_rel_bias, tail_vec) = _local_step(
        x2, tgt2, mod3, h, proj, attn, w_ao_full, w_co_full, w_o_full,
        conv_w_full, conv_b, ln_g, ln_b)

    g_conv_w_blocks = jnp.transpose(conv_vec[0:3].reshape(3, N_DEV, D // N_DEV), (1, 0, 2))
    partials = [gw_ao, gw_co.reshape(N_DEV, D // N_DEV, D), gw_o.reshape(N_DEV, D // N_DEV, D), g_conv_w_blocks]
    w_in_sums, w_in_parts, sib = _gw_in_pair(
        _slice_order(), h, dproj, partials,
        [jax.ShapeDtypeStruct((4, GW, D // N_DEV), BF16),
         jax.ShapeDtypeStruct((4, D // N_DEV, D), BF16),
         jax.ShapeDtypeStruct((4, D // N_DEV, D), BF16),
         jax.ShapeDtypeStruct((4, 3, D // N_DEV), F32)])
    core = lax.axis_index("c").astype(jnp.int32).reshape(1)
    chip_sums = [w_in_sums] + list(_pair_add(core, partials, sib))
    hops = [(3,)] + [(1, 2, 3)] * 4
    grad_x, mod_vec, (r_in, r_ao, r_co, r_o, r_cw) = _dh_dx(
        dproj, w_in_all, x2, dxd, mod3, chip_sums, hops, w_in_parts)

    small = jnp.concatenate([
        tail_vec[0:4],
        jnp.pad(g_rel_bias.reshape(1, N_BUCKETS * N_HEADS), ((0, 0), (0, D - N_BUCKETS * N_HEADS))),
        jnp.zeros((3, D), F32)], axis=0)
    dmod = jnp.concatenate([mod_vec[0:2], mod_vec[2:4], tail_vec[4:6]], axis=1)
    small_g, dmod_g = _all_gather(
        [small, dmod],
        [jax.ShapeDtypeStruct((N_DEV, 8, D), F32), jax.ShapeDtypeStruct((N_DEV, BL, 3 * D), F32)],
        "gather_small")
    dmod_all = dmod_g.reshape(N_DEV * BL, 3 * D)
    small_names = ["b_ada", "conv_b", "rel_bias", "ln_g", "ln_b"]
    small_params = [(b_ada, m_b_ada, v_b_ada), (conv_b, m_conv_b, v_conv_b), (rel_bias, m_rel_bias, v_rel_bias),
                    (ln_g, m_ln_g, v_ln_g), (ln_b, m_ln_b, v_ln_b)]
    loss, small_res = _small_updates(
        small_g, dmod_all, small_g[:, 4, :N_BUCKETS * N_HEADS].reshape(N_DEV, N_BUCKETS, N_HEADS), small_params)
    loss = loss.reshape(())

    dmod_cols = lax.dynamic_slice(dmod_all, (0, me * ADA_SHARD), (N_DEV * BL, ADA_SHARD))
    res = {
        "w_ada": tuple(t[None] for t in _w_ada_update(jnp.transpose(c_all), dmod_cols,
                                                      w_ada[0], m_w_ada[0], v_w_ada[0])),
        "w_in": tuple(t[None] for t in _adamw(r_in, w_in[0], m_w_in[0], v_w_in[0], "adam_w_in", 128)),
    }
    mid_names = ["conv_w", "w_attn_out", "w_conv_out", "w_o"]
    mid_parts = [r_cw, r_ao, r_co, r_o]
    mid_full = [(conv_w, m_conv_w, v_conv_w), (w_attn_out, m_w_attn_out, v_w_attn_out),
                (w_conv_out, m_w_conv_out, v_w_conv_out), (w_o, m_w_o, v_w_o)]
    mid_res = _multi_adamw(mid_parts, [tuple(t[0] for t in wmv) for wmv in mid_full], "adam_mid")
    for nm, wmv, outs4 in zip(mid_names, mid_full, mid_res):
        res[nm] = tuple(t[None] for t in outs4)
    res.update(dict(zip(small_names, small_res)))
    order = ["w_ada", "b_ada", "w_in", "conv_w", "conv_b", "rel_bias", "w_attn_out", "w_conv_out",
             "w_o", "ln_g", "ln_b"]
    outs = [loss, grad_x.reshape(BL, S, D)]
    for k in range(4):
        outs += [res[name][k] for name in order]
    return tuple(outs)
```

```python
import math

import numpy as np
import jax
import jax.numpy as jnp
from jax import lax
from jax.experimental import pallas as pl
from jax.experimental.pallas import tpu as pltpu

F32 = jnp.float32
BF16 = jnp.bfloat16
MESH = pl.DeviceIdType.MESH

N_DEV = 8
D = 1024
S = 2048
BL = 2
T = BL * S
NCOL = 11264
SHARD = NCOL // N_DEV
CB = 512
NCB = NCOL // CB
HD = 128
GW = 512
QB = 128
DILATIONS = (1, 4, 16)
N_STEPS = 128
N_BUCKETS = 32
N_HEADS = 12
ALPHA = 2.0 ** 0.25
LN_EPS = 1e-5
NEG_INF = -1e30
SCALE = HD ** -0.5
ADA_SHARD = 3 * D // N_DEV

CB_Q, CB_K, CB_V, CB_GA = 0, 3, 6, 9
KB_U, KB_BG, KB_CG, KB_GC, KB_MA, KB_MC = 5, 6, 7, 8, 9, 10

ADAM_LR, ADAM_B1, ADAM_B2, ADAM_EPS, ADAM_WD, ADAM_STEP = 0.001, 0.9, 0.999, 1e-08, 0.01, 10

VMEM_LIMIT = 56 * 1024 * 1024
VMEM_LIMIT_TAIL = 62 * 1024 * 1024


def _dot(a, b):
    return jnp.dot(a, b, preferred_element_type=F32)


def _dot_nt(a, b):
    return lax.dot_general(a, b, (((1,), (1,)), ((), ())), preferred_element_type=F32)


def _dot_tn(a, b):
    return lax.dot_general(a, b, (((0,), (0,)), ((), ())), preferred_element_type=F32)


def _sigmoid(v):
    return 1.0 / (1.0 + jnp.exp(-v))


def _column_copies(pieces, dst_hbm, row0, sems):
    copies = []
    for k, (src, col0) in enumerate(pieces):
        rows, width = src.shape
        copies.append(pltpu.make_async_copy(
            src, dst_hbm.at[pl.ds(row0, rows), pl.ds(col0, width)], sems.at[k]))
    return copies


def _my_index():
    return 4 * lax.axis_index("x") + 2 * lax.axis_index("y") + lax.axis_index("c")


class _Gather:
    def __init__(self, ins, outs, stage, send_sems, recv_sems, local_sems):
        self.ins, self.outs, self.stage = ins, outs, stage
        self.send_sems, self.recv_sems, self.local_sems = send_sems, recv_sems, local_sems
        x, y, c = lax.axis_index("x"), lax.axis_index("y"), lax.axis_index("c")
        self.c = c
        self.me, self.sibling = (x, y, c), (x, y, 1 - c)
        self.chips = [(1 - x, y), (x, 1 - y), (1 - x, 1 - y)]

    @staticmethod
    def scratch(arrs):
        n = len(arrs)
        return ([pltpu.SemaphoreType.DMA((7 * n,)), pltpu.SemaphoreType.DMA((7 * n,)),
                 pltpu.SemaphoreType.DMA((n,))] + [pltpu.VMEM(a.shape, a.dtype) for a in arrs])

    def _copy(self, a, k, block, to, src=None):
        dst = self.outs[a].at[4 * block[0] + 2 * block[1] + block[2]]
        return pltpu.make_async_remote_copy(
            src_ref=dst if src is None else src, dst_ref=dst,
            send_sem=self.send_sems.at[a * 7 + k], recv_sem=self.recv_sems.at[a * 7 + k],
            device_id=to, device_id_type=MESH)

    def _first(self):
        first = []
        for a in range(len(self.ins)):
            first.append(self._copy(a, 0, self.me, self.sibling, src=self.ins[a]))
            first += [self._copy(a, 1 + j, self.me, (*chip, self.c), src=self.ins[a])
                      for j, chip in enumerate(self.chips)]
        return first

    def _mine(self):
        me = self.me
        return [pltpu.make_async_copy(self.stage[a], self.outs[a].at[4 * me[0] + 2 * me[1] + me[2]],
                                      self.local_sems.at[a]) for a in range(len(self.ins))]

    def begin(self):
        for cp in self._first():
            cp.start()
        loads = [pltpu.make_async_copy(self.ins[a], self.stage[a], self.local_sems.at[a])
                 for a in range(len(self.ins))]
        for cp in loads:
            cp.start()
        for cp in loads:
            cp.wait()
        for cp in self._mine():
            cp.start()

    def finish(self):
        n, c, me, sibling = len(self.ins), self.c, self.me, self.sibling
        passed = []
        for j, chip in enumerate(self.chips):
            for a in range(n):
                self._copy(a, 1 + j, (*chip, c), me).wait_recv()
                fwd = self._copy(a, 4 + j, (*chip, c), sibling)
                fwd.start()
                passed.append(fwd)
        for a in range(n):
            self._copy(a, 0, sibling, me).wait_recv()
        for j, chip in enumerate(self.chips):
            for a in range(n):
                self._copy(a, 4 + j, (*chip, 1 - c), me).wait_recv()
        for cp in self._first() + passed:
            cp.wait_send()
        for cp in self._mine():
            cp.wait()


def _all_gather(arrs, out_shapes, name):
    n = len(arrs)

    def body(*refs):
        g = _Gather(refs[:n], refs[n:2 * n], refs[2 * n + 3:], *refs[2 * n:2 * n + 3])
        g.begin()
        g.finish()

    any_spec = pl.BlockSpec(memory_space=pl.ANY)
    return pl.pallas_call(
        body, name=name,
        out_shape=tuple(out_shapes),
        in_specs=[any_spec] * n,
        out_specs=tuple([any_spec] * n),
        scratch_shapes=_Gather.scratch(arrs),
    )(*arrs)


def _neighbour_chips():
    x, y, c = lax.axis_index("x"), lax.axis_index("y"), lax.axis_index("c")
    first = (jnp.where(c == 0, 1 - x, x), jnp.where(c == 0, y, 1 - y))
    second = (jnp.where(c == 0, x, 1 - x), jnp.where(c == 0, 1 - y, y))
    return first, second, (1 - x, 1 - y)


def _slice_order():
    x, y, c = lax.axis_index("x"), lax.axis_index("y"), lax.axis_index("c")
    nb1, nb2, diag = _neighbour_chips()
    slots = []
    for mine, theirs in ((nb1, nb2), (nb2, nb1), (diag, diag), ((x, y), (x, y))):
        slots += [2 * (2 * theirs[0] + theirs[1]) + 1 - c, 2 * (2 * mine[0] + mine[1]) + c]
    return jnp.stack(slots).astype(jnp.int32)


def _gw_in_pair(order, h, dproj, smalls, small_shapes4):
    kk, m = h.shape
    tk = min(kk, 2048)
    nk = kk // tk
    ncols = dproj.shape[1] // N_DEV
    n = len(smalls)

    def body(order_ref, h_ref, d_ref, *rest):
        ins = rest[:n]
        sums_hbm, parts_hbm = rest[n], rest[n + 1]
        sib = rest[n + 2:2 * n + 2]
        (acc, sendbuf, recvbuf, sumbuf, send_sems, recv_sems, local_sem, ssend, srecv,
         isend, irecv) = rest[2 * n + 2:]
        js, k = pl.program_id(0), pl.program_id(1)
        x, y, c = lax.axis_index("x"), lax.axis_index("y"), lax.axis_index("c")
        sibling = (x, y, 1 - c)
        my_chip = 2 * x + y
        nb1, nb2, _ = _neighbour_chips()
        near = [(*nb1, c), (*nb2, c)]

        def ici_copy(p, out_chip):
            peer = near[p] if isinstance(p, int) else tuple(jnp.where(p == 0, a, b) for a, b in zip(*near))
            return pltpu.make_async_remote_copy(
                src_ref=sumbuf.at[p], dst_ref=parts_hbm.at[out_chip],
                send_sem=isend.at[p], recv_sem=irecv.at[p], device_id=peer, device_id_type=MESH)

        def small_copies():
            return [pltpu.make_async_remote_copy(
                        src_ref=ins[a].at[2 * q + 1 - c], dst_ref=sib[a].at[q],
                        send_sem=ssend.at[a * 4 + q], recv_sem=srecv.at[a * 4 + q],
                        device_id=sibling, device_id_type=MESH)
                    for a in range(n) for q in range(4)]

        def slice_copy(p):
            return pltpu.make_async_remote_copy(
                src_ref=sendbuf, dst_ref=recvbuf.at[p], send_sem=send_sems.at[p], recv_sem=recv_sems.at[p],
                device_id=sibling, device_id_type=MESH)

        def sum_copy(p):
            return pltpu.make_async_copy(sumbuf.at[2], sums_hbm.at[order_ref[2 * p] // 2], local_sem)

        @pl.when((js == 0) & (k == 0))
        def _():
            for cp in small_copies():
                cp.start()

        def partial():
            return _dot_tn(h_ref[...], d_ref[...])

        if nk > 1:
            @pl.when(k == 0)
            def _():
                acc[...] = partial()
        if nk > 2:
            @pl.when((k > 0) & (k < nk - 1))
            def _():
                acc[...] += partial()

        def total():
            return partial() + acc[...] if nk > 1 else partial()

        p = js // 2

        @pl.when((js % 2 == 0) & (k == nk - 1))
        def _():
            @pl.when(p > 0)
            def _():
                slice_copy(p - 1).wait_send()
            sendbuf[...] = total().astype(BF16)
            slice_copy(p).start()

        @pl.when((js % 2 == 1) & (k == nk - 1))
        def _():
            slice_copy(p).wait_recv()

            @pl.when(p == 3)
            def _():
                sum_copy(2).wait()
            sumbuf[jnp.minimum(p, 2)] = (total() + recvbuf[p].astype(F32)).astype(BF16)

            @pl.when(p < 2)
            def _():
                ici_copy(p, my_chip).start()

            @pl.when(p >= 2)
            def _():
                sum_copy(p).start()

        @pl.when((js == N_DEV - 1) & (k == nk - 1))
        def _():
            slice_copy(3).wait_send()
            sum_copy(3).wait()
            for cp in small_copies():
                cp.wait()
            for p in range(2):
                ici_copy(p, 2 * near[p][0] + near[p][1]).wait_recv()
                ici_copy(p, my_chip).wait_send()

    any_spec = pl.BlockSpec(memory_space=pl.ANY)
    res = pl.pallas_call(
        body, name="gw_in_pair",
        grid_spec=pltpu.PrefetchScalarGridSpec(
            num_scalar_prefetch=1,
            grid=(N_DEV, nk),
            in_specs=[pl.BlockSpec((tk, m), lambda js, k, order_ref: (k, 0)),
                      pl.BlockSpec((tk, ncols), lambda js, k, order_ref: (k, order_ref[js]))] + [any_spec] * n,
            out_specs=(any_spec,) * (n + 2),
            scratch_shapes=[pltpu.VMEM((m, ncols), F32), pltpu.VMEM((m, ncols), BF16),
                            pltpu.VMEM((4, m, ncols), BF16), pltpu.VMEM((3, m, ncols), BF16),
                            pltpu.SemaphoreType.DMA((4,)), pltpu.SemaphoreType.DMA((4,)),
                            pltpu.SemaphoreType.DMA,
                            pltpu.SemaphoreType.DMA((4 * n,)), pltpu.SemaphoreType.DMA((4 * n,)),
                            pltpu.SemaphoreType.DMA((2,)), pltpu.SemaphoreType.DMA((2,))]),
        out_shape=(jax.ShapeDtypeStruct((4, m, ncols), BF16),) * 2 + tuple(small_shapes4),
        compiler_params=pltpu.CompilerParams(vmem_limit_bytes=VMEM_LIMIT),
    )(order, h, dproj, *smalls)
    return res[0], res[1], res[2:]


def _chip_copies(ins, outs, send_sems, recv_sems, local_sems, hops):
    n = len(ins)
    x, y, c = lax.axis_index("x"), lax.axis_index("y"), lax.axis_index("c")
    my_chip = 2 * x + y

    def peer_of(k):
        return ((1 - x) if (k >> 1) & 1 else x, (1 - y) if k & 1 else y, c)

    def copy(a, k, out_chip):
        peer = peer_of(k)
        return pltpu.make_async_remote_copy(
            src_ref=ins[a].at[2 * peer[0] + peer[1]], dst_ref=outs[a].at[out_chip],
            send_sem=send_sems.at[a * 3 + k - 1], recv_sem=recv_sems.at[a * 3 + k - 1],
            device_id=peer, device_id_type=MESH)

    sends = [copy(a, k, my_chip) for k in range(1, 4) for a in range(n) if k in hops[a]]
    arrivals = []
    for k in range(1, 4):
        peer = peer_of(k)
        arrivals += [copy(a, k, 2 * peer[0] + peer[1]) for a in range(n) if k in hops[a]]
    mine = [pltpu.make_async_copy(ins[a].at[my_chip], outs[a].at[my_chip], local_sems.at[a])
            for a in range(n)]
    return sends, arrivals, mine


def _pair_add(core, mines, theirs):
    n = len(mines)

    def body(core_ref, *refs):
        mine, sib, outs = refs[:n], refs[n:2 * n], refs[2 * n:]
        for a in range(n):
            for q in range(4):
                outs[a][q] = (mine[a][2 * q + core_ref[0]].astype(F32)
                              + sib[a][q].astype(F32)).astype(outs[a].dtype)

    return pl.pallas_call(
        body, name="pair_add",
        in_specs=[pl.BlockSpec(memory_space=pltpu.SMEM)] + [pl.BlockSpec(memory_space=pltpu.VMEM)] * (2 * n),
        out_shape=tuple(jax.ShapeDtypeStruct(t.shape, t.dtype) for t in theirs),
    )(core, *mines, *theirs)


def _mod_exchange(c8, w_ada, b_cols):
    cols = w_ada.shape[1]

    def body(c_ref, w_ref, b_ref, call_ref, mod_ref, msend, send1, recv1, send2, recv2):
        x, y, c = lax.axis_index("x"), lax.axis_index("y"), lax.axis_index("c")
        my_slot = 4 * x + 2 * y + c

        def peer_of(k):
            return ((1 - x) if (k >> 2) & 1 else x, (1 - y) if (k >> 1) & 1 else y, (1 - c) if k & 1 else c)

        def slot_of(dev):
            return 4 * dev[0] + 2 * dev[1] + dev[2]

        def exchange(src_of, dst_ref, send_sems, recv_sems):
            sends, arrivals = [], []
            for k in range(1, 8):
                peer = peer_of(k)
                sends.append(pltpu.make_async_remote_copy(
                    src_ref=src_of(slot_of(peer)), dst_ref=dst_ref.at[my_slot],
                    send_sem=send_sems.at[k - 1], recv_sem=recv_sems.at[k - 1],
                    device_id=peer, device_id_type=MESH))
                arrivals.append(pltpu.make_async_remote_copy(
                    src_ref=src_of(my_slot), dst_ref=dst_ref.at[slot_of(peer)],
                    send_sem=send_sems.at[k - 1], recv_sem=recv_sems.at[k - 1],
                    device_id=peer, device_id_type=MESH))
            for cp in sends:
                cp.start()
            for cp in arrivals:
                cp.wait_recv()
            for cp in sends:
                cp.wait_send()

        call_ref[my_slot] = c_ref[...]
        exchange(lambda s: c_ref, call_ref, send1, recv1)
        cv = call_ref[...].reshape(N_DEV * 8, c_ref.shape[1])
        act = cv * _sigmoid(cv)
        mod = jnp.dot(act, w_ref[...], preferred_element_type=F32,
                      precision=lax.Precision.HIGHEST) + b_ref[...]
        msend[...] = mod.reshape(N_DEV, 8, cols)
        mod_ref[my_slot] = msend[my_slot]
        exchange(lambda s: msend.at[s], mod_ref, send2, recv2)

    return pl.pallas_call(
        body, name="mod_exchange",
        out_shape=(jax.ShapeDtypeStruct((N_DEV, 8, c8.shape[1]), F32),
                   jax.ShapeDtypeStruct((N_DEV, 8, cols), F32)),
        scratch_shapes=[pltpu.VMEM((N_DEV, 8, cols), F32)] + [pltpu.SemaphoreType.DMA((7,))] * 4,
    )(c8, w_ada, b_cols)


def _w_ada_update(c_all_t, dmod_cols, w, m, v):
    rows, cols = w.shape
    tr = 256

    def body(c_ref, d_ref, w_ref, m_ref, v_ref, g_ref, dl_ref, nm_ref, nv_ref):
        cv = c_ref[...]
        g = jnp.dot(cv * _sigmoid(cv), d_ref[...], preferred_element_type=F32,
                    precision=lax.Precision.HIGHEST)
        g_ref[...] = g
        dl_ref[...], nm_ref[...], nv_ref[...] = _adam_step(g, w_ref[...], m_ref[...], v_ref[...])

    blk = pl.BlockSpec((tr, cols), lambda i: (i, 0))
    shp = jax.ShapeDtypeStruct((rows, cols), F32)
    return pl.pallas_call(
        body, name="adam_w_ada",
        grid=(rows // tr,),
        in_specs=[pl.BlockSpec((tr, c_all_t.shape[1]), lambda i: (i, 0)),
                  pl.BlockSpec(dmod_cols.shape, lambda i: (0, 0)), blk, blk, blk],
        out_specs=(blk, blk, blk, blk),
        out_shape=(shp, shp, shp, shp),
    )(c_all_t, dmod_cols, w, m, v)


def _shard_order():
    x, y, c = lax.axis_index("x"), lax.axis_index("y"), lax.axis_index("c")
    first, second, diag = _neighbour_chips()
    devs = [(x, y, c), (x, y, 1 - c), (*first, c), (*second, 1 - c), (*second, c), (*first, 1 - c),
            (*diag, c), (*diag, 1 - c)]
    return jnp.stack([4 * d[0] + 2 * d[1] + d[2] for d in devs]).astype(jnp.int32)


def _gather_proj(order, x2, mod3, w_shard, tm, ride=()):
    rows, kdim = x2.shape
    ncols = w_shard.shape[1]
    n_i = rows // tm
    per_seq = n_i // mod3.shape[0]
    ride_arrs, ride_shapes = ride if ride else ((), ())
    n_ride = len(ride_arrs)

    def body(order_ref, x_ref, mod_ref, mine_hbm, *rest):
        ride_ins = rest[:n_ride]
        o_ref, h_ref, all_hbm = rest[n_ride:n_ride + 3]
        ride_outs = rest[n_ride + 3:2 * n_ride + 3]
        wv, send_sems, recv_sems, local_sems, hs = rest[2 * n_ride + 3:2 * n_ride + 8]
        ride_scr = rest[2 * n_ride + 8:]
        j, i = pl.program_id(0), pl.program_id(1)
        c = lax.axis_index("c")
        me, sibling = (lax.axis_index("x"), lax.axis_index("y"), c), (lax.axis_index("x"), lax.axis_index("y"), 1 - c)
        nb1, nb2, diag = _neighbour_chips()

        def slot(dev):
            return 4 * dev[0] + 2 * dev[1] + dev[2]

        def copy(k, block, to, src=None, part=None):
            buf = wv.at[slot(block)]
            if part is not None:
                buf = buf.at[pl.ds(pl.multiple_of(part * (kdim // 2), kdim // 2), kdim // 2)]
            return pltpu.make_async_remote_copy(
                src_ref=buf if src is None else src, dst_ref=buf,
                send_sem=send_sems.at[k], recv_sem=recv_sems.at[k],
                device_id=to, device_id_type=MESH)

        def keep(step, block):
            s = slot(block)
            cols = pl.ds(pl.multiple_of((s % 2) * ncols, 128), ncols)
            return pltpu.make_async_copy(wv.at[s], all_hbm.at[s // 2, :, cols], local_sems.at[step])

        if n_ride:
            gather = _Gather(ride_ins, ride_outs, ride_scr[3:], *ride_scr[:3])
        to_sibling, to_nb1, to_nb2 = (copy(0, me, sibling, mine_hbm), copy(1, me, (*nb1, c), mine_hbm),
                                      copy(2, me, (*nb2, c), mine_hbm))
        relay1, relay2 = copy(3, (*nb2, c), (*nb1, c), part=c), copy(4, (*nb1, c), (*nb2, c), part=1 - c)
        pass_nb1, pass_nb2 = copy(5, (*nb1, c), sibling), copy(6, (*nb2, c), sibling)
        pass_d1, pass_d2 = copy(7, (*diag, c), sibling, part=c), copy(8, (*diag, c), sibling, part=1 - c)
        sends = [to_sibling, to_nb1, to_nb2, relay1, relay2, pass_nb1, pass_nb2, pass_d1, pass_d2]
        due = [
            (me, [], []),
            (sibling, [copy(0, sibling, me)], [[]]),
            ((*nb1, c), [copy(1, (*nb1, c), me)], [[pass_nb1, to_nb2]]),
            ((*nb2, 1 - c), [copy(5, (*nb2, 1 - c), me)], [[]]),
            ((*nb2, c), [copy(2, (*nb2, c), me)], [[pass_nb2, relay1, relay2]]),
            ((*nb1, 1 - c), [copy(6, (*nb1, 1 - c), me)], [[]]),
            ((*diag, c), [copy(3, (*diag, c), me, part=c), copy(4, (*diag, c), me, part=1 - c)],
             [[pass_d1], [pass_d2]]),
            ((*diag, 1 - c), [copy(7, (*diag, 1 - c), me, part=1 - c), copy(8, (*diag, 1 - c), me, part=c)],
             [[], []]),
        ]

        @pl.when((j == 0) & (i == 0))
        def _():
            to_sibling.start()
            to_nb1.start()
            load = pltpu.make_async_copy(mine_hbm, wv.at[slot(me)], local_sems.at[N_DEV])
            load.start()
            load.wait()
            keep(0, me).start()

        for step in range(1, N_DEV):
            block, arrivals, then = due[step]

            @pl.when((j == step) & (i == 0))
            def _():
                for arrival, follow in zip(arrivals, then):
                    arrival.wait_recv()
                    for cp in follow:
                        cp.start()
                keep(step, block).start()
                if n_ride and step == N_DEV - 3:
                    gather.begin()

        @pl.when(j == 0)
        def _():
            hb = (x_ref[...] * (1.0 + mod_ref[0, 1:2, :]) + mod_ref[0, 0:1, :]).astype(BF16)
            hs[i] = hb
            h_ref[...] = hb

        o_ref[...] = _dot(hs[i], wv[order_ref[j]]).astype(BF16)

        @pl.when((j == N_DEV - 1) & (i == n_i - 1))
        def _():
            for cp in sends:
                cp.wait_send()
            for step in range(N_DEV):
                keep(step, due[step][0]).wait()
            if n_ride:
                gather.finish()

    def first_pass(j, i):
        return jnp.where(j == 0, i, n_i - 1)

    any_spec = pl.BlockSpec(memory_space=pl.ANY)
    res = pl.pallas_call(
        body, name="gather_proj",
        grid_spec=pltpu.PrefetchScalarGridSpec(
            num_scalar_prefetch=1,
            grid=(N_DEV, n_i),
            in_specs=[pl.BlockSpec((tm, kdim), lambda j, i, order_ref: (first_pass(j, i), 0)),
                      pl.BlockSpec((1, 3, kdim), lambda j, i, order_ref: (first_pass(j, i) // per_seq, 0, 0)),
                      any_spec] + [any_spec] * n_ride,
            out_specs=(pl.BlockSpec((tm, ncols), lambda j, i, order_ref: (i, order_ref[j])),
                       pl.BlockSpec((tm, kdim), lambda j, i, order_ref: (first_pass(j, i), 0)), any_spec)
                      + (any_spec,) * n_ride,
            scratch_shapes=[pltpu.VMEM((N_DEV, kdim, ncols), BF16),
                            pltpu.SemaphoreType.DMA((9,)), pltpu.SemaphoreType.DMA((9,)),
                            pltpu.SemaphoreType.DMA((N_DEV + 1,)), pltpu.VMEM((n_i, tm, kdim), BF16)]
                           + (_Gather.scratch(ride_arrs) if n_ride else [])),
        out_shape=(jax.ShapeDtypeStruct((rows, N_DEV * ncols), BF16),
                   jax.ShapeDtypeStruct((rows, kdim), BF16),
                   jax.ShapeDtypeStruct((N_DEV // 2, kdim, 2 * ncols), BF16)) + tuple(ride_shapes),
        compiler_params=pltpu.CompilerParams(vmem_limit_bytes=VMEM_LIMIT),
    )(order, x2, mod3, w_shard, *ride_arrs)
    return res[0], res[1], res[2], res[3:]


SKEW_W = 512


def _bucket_maps():
    lanes = np.arange(SKEW_W)

    def buckets_of(steps):
        rows = []
        for dil in DILATIONS:
            dist = np.maximum(steps, 0) * dil
            nf = np.maximum(dist, 1).astype(np.float32)
            large = 16 + (np.log(nf / np.float32(16)) / np.float32(math.log(128.0))
                          * np.float32(16)).astype(np.int32)
            large = np.minimum(large, N_BUCKETS - 1)
            bucket = np.where(dist < 16, dist, large)
            rows.append(np.where((steps >= 0) & (steps <= N_STEPS), bucket, -1).astype(np.int32))
        return np.stack(rows)[:, None, :]

    a = np.arange(QB)[:, None]
    b = np.arange(2 * QB)[None, :]
    steps = a + QB - b
    band = (steps >= 0) & (steps <= N_STEPS)
    first = band & (b >= QB)
    masks = np.stack([first, band]).astype(np.int32)
    return buckets_of(QB - lanes), buckets_of(2 * QB - 1 - lanes), masks


def _bias_expand(rel_bias, lane_buckets, masks):
    def body(tab_ref, bk_ref, mk_ref, o_ref):
        for g in range(3):
            bk = bk_ref[g]
            for h in range(4):
                col = 4 * g + h
                per_offset = jnp.zeros((1, SKEW_W), F32)
                for k in range(N_BUCKETS):
                    per_offset = jnp.where(bk == k, tab_ref[k, col], per_offset)
                tile = pltpu.roll(jnp.broadcast_to(per_offset, (QB, SKEW_W)), 0, 1, stride=1, stride_axis=0)
                tile = tile[:, :2 * QB]
                o_ref[g, 0, h] = jnp.where(mk_ref[0] != 0, tile, NEG_INF)
                o_ref[g, 1, h] = jnp.where(mk_ref[1] != 0, tile, NEG_INF)

    return pl.pallas_call(
        body, name="bias_expand",
        in_specs=[pl.BlockSpec(memory_space=pltpu.SMEM),
                  pl.BlockSpec(memory_space=pltpu.VMEM),
                  pl.BlockSpec(memory_space=pltpu.VMEM)],
        out_shape=jax.ShapeDtypeStruct((3, 2, 4, QB, 2 * QB), F32),
    )(rel_bias, lane_buckets, masks)


def _bias_grad(ds1, ds2, ds3, lane_buckets):
    exchange = jnp.asarray(np.eye(QB, dtype=np.float32)[::-1].copy())

    def body(d1_ref, d2_ref, d3_ref, bk_ref, ex_ref, o_ref):
        for g, d_ref in enumerate((d1_ref, d2_ref, d3_ref)):
            bk = bk_ref[g]
            for h in range(4):
                flipped = jnp.dot(ex_ref[...], d_ref[h], preferred_element_type=F32,
                                  precision=lax.Precision.HIGHEST)
                padded = jnp.concatenate([flipped, jnp.zeros((QB, SKEW_W - 2 * QB), F32)], axis=1)
                skewed = pltpu.roll(padded, 0, 1, stride=1, stride_axis=0)
                per_offset = jnp.sum(skewed, axis=0, keepdims=True)
                for k in range(N_BUCKETS):
                    o_ref[k, 4 * g + h] = jnp.sum(jnp.where(bk == k, per_offset, 0.0))

    return pl.pallas_call(
        body, name="bias_grad",
        in_specs=[pl.BlockSpec(memory_space=pltpu.VMEM)] * 5,
        out_specs=pl.BlockSpec(memory_space=pltpu.SMEM),
        out_shape=jax.ShapeDtypeStruct((N_BUCKETS, N_HEADS), F32),
    )(ds1, ds2, ds3, lane_buckets, exchange)


def _scratch_sets(rows):
    return 4 if rows <= 512 else 1


def _unit_chunks(dil, size=16):
    units = [(h, r) for h in range(4) for r in range(dil)]
    return [units[i:i + size] for i in range(0, len(units), size)]


def _residue_rows(src_ref, copies, h, residue):
    buf = copies[h % len(copies)]
    buf[...] = src_ref[:, h * HD:(h + 1) * HD].astype(F32)
    return lambda r: buf[residue(r), :].astype(BF16)


def _attn_fwd(proj, bias, g):
    dil = DILATIONS[g]
    rows = QB * dil
    nsb = S // rows
    has_prev = nsb > 1

    def residue(r):
        return pl.ds(r, QB, stride=dil)

    n_sets = _scratch_sets(rows)
    n_in = 6 if has_prev else 4
    n_copied = (4 + (2 if has_prev else 0)) * n_sets

    def body(*refs):
        q_ref, kc_ref, vc_ref = refs[:3]
        kp_ref, vp_ref = refs[3:5] if has_prev else (None, None)
        b_ref = refs[n_in - 1]
        o_ref, l_ref = refs[n_in:n_in + 2]
        scr = list(refs[n_in + 2:])
        ls = [scr.pop(0) for _ in range(4)]
        copies = {name: [scr.pop(0) for _ in range(n_sets)]
                  for name in ("q", "kc", "vc", "o") + (("kp", "vp") if has_prev else ())}
        lane = lax.broadcasted_iota(jnp.int32, (QB, 128), 1)
        refs_of = {"q": q_ref, "kc": kc_ref, "vc": vc_ref, "kp": kp_ref, "vp": vp_ref}
        for chunk in _unit_chunks(dil):
            rows_of = {h: {name: _residue_rows(refs_of[name], copies[name], h, residue)
                           for name in refs_of if refs_of[name] is not None}
                       for h in sorted({h for h, _ in chunk})}

            def batch(name):
                return jnp.stack([rows_of[h][name](r) for h, r in chunk])

            q, k, v = batch("q"), batch("kc"), batch("vc")
            if has_prev:
                k = jnp.concatenate([batch("kp"), k], axis=1)
                v = jnp.concatenate([batch("vp"), v], axis=1)
                bias_b = jnp.stack([b_ref[h] for h, _ in chunk])
            else:
                bias_b = jnp.stack([b_ref[h, :, QB:] for h, _ in chunk])
            s = jnp.einsum("uqd,ukd->uqk", q, k, preferred_element_type=F32) * SCALE + bias_b
            m = jnp.max(s, axis=-1, keepdims=True)
            p = jnp.exp(s - m)
            l = jnp.sum(p, axis=-1, keepdims=True)
            o = jnp.einsum("uqk,ukd->uqd", p.astype(BF16), v, preferred_element_type=F32) / l
            lse = m + jnp.log(l)
            for i, (h, r) in enumerate(chunk):
                copies["o"][h % n_sets][residue(r), :] = o[i]
                ls[h][r * QB:(r + 1) * QB, :] = jnp.where(lane == h, lse[i], 0.0)
            for h in sorted({h for h, _ in chunk}):
                o_ref[:, h * HD:(h + 1) * HD] = copies["o"][h % n_sets][...]
        for r in range(dil):
            blk = slice(r * QB, (r + 1) * QB)
            l_ref[residue(r), :] = (ls[0][blk, :] + ls[1][blk, :]) + (ls[2][blk, :] + ls[3][blk, :])

    def row(b, n):
        return b * nsb + n

    def prev(b, n):
        return b * nsb + jnp.maximum(n - 1, 0)

    in_specs = [
        pl.BlockSpec((rows, GW), lambda b, n: (row(b, n), CB_Q + g)),
        pl.BlockSpec((rows, GW), lambda b, n: (row(b, n), CB_K + g)),
        pl.BlockSpec((rows, GW), lambda b, n: (row(b, n), CB_V + g)),
    ]
    args = [proj, proj, proj]
    scratch = [pltpu.VMEM((rows, 128), F32)] * (4 + n_copied)
    if has_prev:
        in_specs += [pl.BlockSpec((rows, GW), lambda b, n: (prev(b, n), CB_K + g)),
                     pl.BlockSpec((rows, GW), lambda b, n: (prev(b, n), CB_V + g))]
        args += [proj, proj]
    in_specs.append(pl.BlockSpec((None, None, 4, QB, 2 * QB),
                                 lambda b, n: (g, jnp.minimum(n, 1), 0, 0, 0)))
    args.append(bias)
    return pl.pallas_call(
        body, name=f"attn_fwd{g}",
        grid=(BL, nsb),
        in_specs=in_specs,
        out_specs=(pl.BlockSpec((rows, GW), lambda b, n: (row(b, n), 0)),
                   pl.BlockSpec((rows, 128), lambda b, n: (row(b, n), 0))),
        out_shape=(jax.ShapeDtypeStruct((T, GW), F32), jax.ShapeDtypeStruct((T, 128), F32)),
        scratch_shapes=scratch,
        compiler_params=pltpu.CompilerParams(vmem_limit_bytes=VMEM_LIMIT),
    )(*args)


def _attn_bwd(proj, d_out, stats, bias, dproj, g):
    dil = DILATIONS[g]
    rows = QB * dil
    nsb = S // rows
    has_prev = nsb > 1
    n_steps = nsb + 1 if has_prev else 1
    n_in = 7 + (2 if has_prev else 0)

    def residue(r):
        return pl.ds(r, QB, stride=dil)

    n_sets = _scratch_sets(rows)

    def body(*refs):
        q_ref, kc_ref, vc_ref, do_ref, st_ref, b_ref = refs[:6]
        kp_ref, vp_ref = refs[6:8] if has_prev else (None, None)
        out_ref, db_ref = refs[n_in], refs[n_in + 1]
        scr = list(refs[n_in + 2:])
        sq, sk, sv, sems = [scr.pop(0) for _ in range(4)]
        carry = scr.pop(0) if has_prev else None
        sts = scr.pop(0)
        copies = {name: [scr.pop(0) for _ in range(n_sets)]
                  for name in ("q", "kc", "vc", "do", "dq", "dk", "dv") + (("kp", "vp") if has_prev else ())}
        b, n = pl.program_id(0), pl.program_id(1)

        @pl.when((b == 0) & (n == 0))
        def _():
            db_ref[...] = jnp.zeros_like(db_ref)

        def finish(h, r, dq, dk, dv):
            for name, val in (("dq", dq), ("dk", dk), ("dv", dv)):
                copies[name][h % n_sets][residue(r), :] = val

        step = b * n_steps + n
        slot = step % 2

        def stage_copies(s, row0):
            return _column_copies([(sq.at[s], CB * (CB_Q + g)), (sk.at[s], CB * (CB_K + g)),
                                   (sv.at[s], CB * (CB_V + g))], out_ref, row0, sems.at[s])

        @pl.when((step >= 2) & ((step - 2) % n_steps >= (1 if has_prev else 0)))
        def _():
            for cp in stage_copies(slot, 0):
                cp.wait()

        def finish_head(h):
            sl = slice(h * HD, (h + 1) * HD)
            sq[slot, :, sl] = copies["dq"][h % n_sets][...].astype(BF16)
            sk[slot, :, sl] = copies["dk"][h % n_sets][...].astype(BF16)
            sv[slot, :, sl] = copies["dv"][h % n_sets][...].astype(BF16)

        def write_block(blk_idx):
            for cp in stage_copies(slot, pl.multiple_of(blk_idx * rows, rows)):
                cp.start()

            @pl.when(step == BL * n_steps - 1)
            def _():
                for s in range(2):
                    for cp in stage_copies(s, 0):
                        cp.wait()

        def carried(h, r):
            blk = slice(r * QB, (r + 1) * QB)
            return ((blk, slice(h * HD, (h + 1) * HD)), (blk, slice(GW + h * HD, GW + (h + 1) * HD)),
                    (blk, slice(2 * GW + h * HD, 2 * GW + (h + 1) * HD)))

        if has_prev:
            @pl.when(n == 0)
            def _():
                carry[...] = jnp.zeros_like(carry)

            @pl.when(n == nsb)
            def _():
                for h in range(4):
                    for r in range(dil):
                        cq, ck, cv = carried(h, r)
                        finish(h, r, carry[cq], carry[ck], carry[cv])
                    finish_head(h)
                write_block(b * nsb + nsb - 1)

        @pl.when(n < nsb)
        def _():
            for r in range(dil):
                sts[r * QB:(r + 1) * QB, :] = st_ref[residue(r), :]
            refs_of = {"q": q_ref, "kc": kc_ref, "vc": vc_ref, "do": do_ref, "kp": kp_ref, "vp": vp_ref}
            for chunk in _unit_chunks(dil):
                heads = sorted({h for h, _ in chunk})
                rows_of = {h: {name: _residue_rows(refs_of[name], copies[name], h, residue)
                               for name in refs_of if refs_of[name] is not None}
                           for h in heads}

                def batch(name):
                    return jnp.stack([rows_of[h][name](r) for h, r in chunk])

                q, k, v, do = batch("q"), batch("kc"), batch("vc"), batch("do")
                if has_prev:
                    k = jnp.concatenate([batch("kp"), k], axis=1)
                    v = jnp.concatenate([batch("vp"), v], axis=1)
                    bias_b = jnp.stack([b_ref[h] for h, _ in chunk])
                else:
                    bias_b = jnp.stack([b_ref[h, :, QB:] for h, _ in chunk])
                lse = jnp.stack([sts[r * QB:(r + 1) * QB, h:h + 1] for h, r in chunk])
                delta = jnp.stack([sts[r * QB:(r + 1) * QB, 4 + h:5 + h] for h, r in chunk])
                s = jnp.einsum("uqd,ukd->uqk", q, k, preferred_element_type=F32) * SCALE + bias_b
                p = jnp.exp(s - lse)
                ds = p * (jnp.einsum("uqd,ukd->uqk", do, v, preferred_element_type=F32) - delta)
                for h in heads:
                    mine = [ds[i] for i, (hh, _) in enumerate(chunk) if hh == h]
                    tot = mine[0]
                    for extra in mine[1:]:
                        tot = tot + extra
                    if has_prev:
                        db_ref[h] += tot
                    else:
                        db_ref[h, :, QB:] += tot
                dsb, pb = ds.astype(BF16), p.astype(BF16)
                dq = jnp.einsum("uqk,ukd->uqd", dsb, k, preferred_element_type=F32) * SCALE
                dk = jnp.einsum("uqk,uqd->ukd", dsb, q, preferred_element_type=F32) * SCALE
                dv = jnp.einsum("uqk,uqd->ukd", pb, do, preferred_element_type=F32)
                for i, (h, r) in enumerate(chunk):
                    if has_prev:
                        cq, ck, cv = carried(h, r)
                        finish(h, r, carry[cq], carry[ck] + dk[i, :QB], carry[cv] + dv[i, :QB])
                        carry[cq] = dq[i]
                        carry[ck] = dk[i, QB:]
                        carry[cv] = dv[i, QB:]
                    else:
                        finish(h, r, dq[i], dk[i], dv[i])
                for h in heads:
                    finish_head(h)
            if has_prev:
                @pl.when(n > 0)
                def _():
                    write_block(b * nsb + n - 1)
            else:
                write_block(b)

    def row(b, n):
        return b * nsb + jnp.minimum(n, nsb - 1)

    def prev(b, n):
        return b * nsb + jnp.maximum(jnp.minimum(n, nsb - 1) - 1, 0)

    in_specs = [
        pl.BlockSpec((rows, GW), lambda b, n: (row(b, n), CB_Q + g)),
        pl.BlockSpec((rows, GW), lambda b, n: (row(b, n), CB_K + g)),
        pl.BlockSpec((rows, GW), lambda b, n: (row(b, n), CB_V + g)),
        pl.BlockSpec((rows, GW), lambda b, n: (row(b, n), 0)),
        pl.BlockSpec((rows, 128), lambda b, n: (row(b, n), 0)),
        pl.BlockSpec((None, None, 4, QB, 2 * QB),
                     lambda b, n: (g, jnp.minimum(jnp.minimum(n, nsb - 1), 1), 0, 0, 0)),
    ]
    args = [proj, proj, proj, d_out, stats, bias]
    scratch = [pltpu.VMEM((2, rows, GW), BF16)] * 3 + [pltpu.SemaphoreType.DMA((2, 3))]
    if has_prev:
        in_specs += [pl.BlockSpec((rows, GW), lambda b, n: (prev(b, n), CB_K + g)),
                     pl.BlockSpec((rows, GW), lambda b, n: (prev(b, n), CB_V + g))]
        args += [proj, proj]
        scratch.append(pltpu.VMEM((rows, 3 * GW), F32))
    n_copied = (7 + (2 if has_prev else 0)) * n_sets
    scratch += [pltpu.VMEM((rows, 128), F32)] * (1 + n_copied)
    in_specs.append(pl.BlockSpec(memory_space=pl.ANY))
    args.append(dproj)
    return pl.pallas_call(
        body, name=f"attn_bwd{g}",
        grid=(BL, n_steps),
        in_specs=in_specs,
        out_specs=(pl.BlockSpec(memory_space=pl.ANY),
                   pl.BlockSpec((4, QB, 2 * QB), lambda b, n: (0, 0, 0))),
        out_shape=(jax.ShapeDtypeStruct((T, NCOL), BF16),
                   jax.ShapeDtypeStruct((4, QB, 2 * QB), F32)),
        scratch_shapes=scratch,
        input_output_aliases={len(args) - 1: 0},
        compiler_params=pltpu.CompilerParams(vmem_limit_bytes=VMEM_LIMIT),
    )(*args)


def _tail(x2, tgt2, mod3, o_g, lse_g, proj, w_ao, w_co, w_o, conv_w, conv_b, ln_g, ln_b):
    tm = 256
    per_seq = S // tm
    halo = 16

    def body(x_ref, t_ref, mod_ref, o1_ref, o2_ref, o3_ref, l1_ref, l2_ref, l3_ref,
             ga_ref, u_ref, bg_ref, cg_ref, gc_ref, ma_ref, mc_ref, up_ref, cp_ref,
             wao_ref, wco_ref, wo_ref, cw_ref, cb_ref, lg_ref, lb_ref,
             dproj_ref, dyc_ref, do_ref, st_ref, dxd_ref,
             gwo_ref, gwco_ref, gwao_ref, vec_ref,
             dga_s, dbg_s, dgm_s, sems, acc_o, acc_co, acc_ao):
        i = pl.program_id(0)
        bidx = i // per_seq
        first = (i % per_seq) == 0

        @pl.when(i == 0)
        def _():
            vec_ref[...] = jnp.zeros_like(vec_ref)

        slot = i % 2

        def column_copies(s, row0):
            return _column_copies([(dga_s.at[s], CB * CB_GA), (dbg_s.at[s], D * KB_BG), (dgm_s.at[s], D * KB_GC)],
                                  dproj_ref, row0, sems.at[s])

        @pl.when(i >= 2)
        def _():
            for cp in column_copies(slot, 0):
                cp.wait()

        l1, l2, l3 = l1_ref[...], l2_ref[...], l3_ref[...]
        mx = jnp.maximum(jnp.maximum(l1, l2), l3)
        e1, e2, e3 = jnp.exp(l1 - mx), jnp.exp(l2 - mx), jnp.exp(l3 - mx)
        esum = e1 + e2 + e3
        lse_tot = mx + jnp.log(esum)
        w1, w2, w3 = e1 / esum, e2 / esum, e3 / esum

        def per_head(wv):
            return jnp.concatenate([jnp.broadcast_to(wv[:, h:h + 1], (tm, HD)) for h in range(4)], axis=1)

        o = per_head(w1) * o1_ref[...] + per_head(w2) * o2_ref[...] + per_head(w3) * o3_ref[...]

        ga = ga_ref[...].astype(F32)
        sig_ga = _sigmoid(ga)
        silu_ga = ga * sig_ga
        a_in = (o * silu_ga).astype(BF16)
        a_out = _dot(a_in, wao_ref[...])

        u = u_ref[...].astype(F32)
        cg = cg_ref[...].astype(F32)
        z = cg * u
        zp = cp_ref[...].astype(F32) * up_ref[...].astype(F32)
        zp = jnp.where(first, 0.0, zp)
        zcat = jnp.concatenate([zp, z], axis=0)
        z1 = pltpu.roll(zcat, 1, 0)[halo:]
        z2 = pltpu.roll(zcat, 2, 0)[halo:]
        y_conv = cw_ref[0:1, :] * z2 + cw_ref[1:2, :] * z1 + cw_ref[2:3, :] * z + cb_ref[...]
        gc = gc_ref[...].astype(F32)
        sig_gc = _sigmoid(gc)
        silu_gc = gc * sig_gc
        bg = bg_ref[...].astype(F32)
        bg_yc = bg * y_conv
        s_in = (bg_yc * silu_gc).astype(BF16)
        s_out = _dot(s_in, wco_ref[...])

        sa = _sigmoid(ma_ref[...].astype(F32))
        sc = _sigmoid(mc_ref[...].astype(F32))
        merged = (sa * a_out + sc * s_out).astype(BF16)
        y = _dot(merged, wo_ref[...])
        gate1 = 1.0 + mod_ref[0, 2:3, :]
        xv = x_ref[...]
        resid = ALPHA * xv + gate1 * y
        mu = jnp.mean(resid, axis=1, keepdims=True)
        xc = resid - mu
        var = jnp.mean(xc * xc, axis=1, keepdims=True)
        rstd = lax.rsqrt(var + LN_EPS)
        xhat = xc * rstd
        lg = lg_ref[...]
        err = xhat * lg + lb_ref[...] - t_ref[...]
        vec_ref[3:4, :] += (0.5 / D) * jnp.sum(err * err, axis=0, keepdims=True)

        vec_ref[1:2, :] += (1.0 / D) * jnp.sum(err * xhat, axis=0, keepdims=True)
        vec_ref[2:3, :] += (1.0 / D) * jnp.sum(err, axis=0, keepdims=True)
        dxh = err * (lg * (1.0 / D))
        dres = rstd * (dxh - jnp.mean(dxh, axis=1, keepdims=True)
                       - xhat * jnp.mean(dxh * xhat, axis=1, keepdims=True))
        dxd_ref[...] = ALPHA * dres
        dgate = jnp.sum(dres * y, axis=0, keepdims=True)
        vec_ref[4:5, :] += jnp.where(bidx == 0, dgate, 0.0)
        vec_ref[5:6, :] += jnp.where(bidx == 1, dgate, 0.0)
        dy = (dres * gate1).astype(BF16)

        dmerged = _dot_nt(dy, wo_ref[...])
        da_out_f = dmerged * sa
        ds_out_f = dmerged * sc
        da_out = da_out_f.astype(BF16)
        ds_out = ds_out_f.astype(BF16)
        dgm_s[slot, :, 2 * D:3 * D] = (ds_out_f * s_out * (1.0 - sc)).astype(BF16)
        dgm_s[slot, :, D:2 * D] = (da_out_f * a_out * (1.0 - sa)).astype(BF16)
        da_in = _dot_nt(da_out, wao_ref[...])
        ds_in = _dot_nt(ds_out, wco_ref[...])

        d_o = da_in * silu_ga
        do_ref[...] = d_o.astype(BF16)
        dga_s[slot] = (da_in * o * (sig_ga + silu_ga * (1.0 - sig_ga))).astype(BF16)
        lane = lax.broadcasted_iota(jnp.int32, (tm, 128), 1)
        stats = lse_tot
        od = o * d_o
        for h in range(4):
            delta = jnp.sum(od[:, h * HD:(h + 1) * HD], axis=1, keepdims=True)
            stats = jnp.where(lane == 4 + h, delta, stats)
        st_ref[...] = stats

        ds_silu = ds_in * silu_gc
        dbg_s[slot] = (ds_silu * y_conv).astype(BF16)
        dyc = ds_silu * bg
        dyc_ref[...] = dyc
        vec_ref[0:1, :] += jnp.sum(dyc, axis=0, keepdims=True)
        dgm_s[slot, :, 0:D] = (ds_in * bg_yc * (sig_gc + silu_gc * (1.0 - sig_gc))).astype(BF16)

        @pl.when(i == 0)
        def _():
            acc_o[...] = jnp.zeros_like(acc_o)
            acc_co[...] = jnp.zeros_like(acc_co)
            acc_ao[...] = jnp.zeros_like(acc_ao)

        acc_o[...] += _dot_tn(merged, dy)
        acc_co[...] += _dot_tn(s_in, ds_out)
        acc_ao[...] += _dot_tn(a_in, da_out)

        for cp in column_copies(slot, pl.multiple_of(i * tm, tm)):
            cp.start()

        @pl.when(i == T // tm - 1)
        def _():
            gwo_ref[...] = acc_o[...].astype(BF16)
            gwco_ref[...] = acc_co[...].astype(BF16)
            gwao_ref[...] = acc_ao[...].astype(BF16)
            for s in range(2):
                for cp in column_copies(s, 0):
                    cp.wait()

    def tile(width, cblk=0):
        return pl.BlockSpec((tm, width), lambda i: (i, cblk))

    def whole(shape):
        return pl.BlockSpec(shape, lambda i: tuple(0 for _ in shape))

    def once(shape):
        return pl.BlockSpec(shape, lambda i: tuple(0 for _ in shape), pipeline_mode=pl.Buffered(1))

    prev_rows = lambda i: (jnp.maximum(i * (tm // halo) - 1, 0),)
    in_specs = [
        tile(D), tile(D), pl.BlockSpec((1, 3, D), lambda i: (i // per_seq, 0, 0)),
        tile(GW), tile(GW), tile(GW), tile(128), tile(128), tile(128),
        tile(GW, CB_GA), tile(D, KB_U), tile(D, KB_BG), tile(D, KB_CG), tile(D, KB_GC),
        tile(D, KB_MA), tile(D, KB_MC),
        pl.BlockSpec((halo, D), lambda i: (*prev_rows(i), KB_U)),
        pl.BlockSpec((halo, D), lambda i: (*prev_rows(i), KB_CG)),
        whole((GW, D)), whole((D, D)), whole((D, D)),
        whole((3, D)), whole((1, D)), whole((1, D)), whole((1, D)),
    ]
    out_specs = (
        pl.BlockSpec(memory_space=pl.ANY), tile(D), tile(GW), tile(128), tile(D),
        once((D, D)), once((D, D)), once((GW, D)),
        pl.BlockSpec((8, D), lambda i: (0, 0)),
    )
    out_shape = (
        jax.ShapeDtypeStruct((T, NCOL), BF16),
        jax.ShapeDtypeStruct((T, D), F32),
        jax.ShapeDtypeStruct((T, GW), BF16),
        jax.ShapeDtypeStruct((T, 128), F32),
        jax.ShapeDtypeStruct((T, D), F32),
        jax.ShapeDtypeStruct((D, D), BF16),
        jax.ShapeDtypeStruct((D, D), BF16),
        jax.ShapeDtypeStruct((GW, D), BF16),
        jax.ShapeDtypeStruct((8, D), F32),
    )
    return pl.pallas_call(
        body, name="tail",
        grid=(T // tm,),
        in_specs=in_specs, out_specs=out_specs, out_shape=out_shape,
        scratch_shapes=[pltpu.VMEM((2, tm, GW), BF16), pltpu.VMEM((2, tm, D), BF16), pltpu.VMEM((2, tm, 3 * D), BF16),
                        pltpu.SemaphoreType.DMA((2, 3)),
                        pltpu.VMEM((D, D), F32), pltpu.VMEM((D, D), F32), pltpu.VMEM((GW, D), F32)],
        compiler_params=pltpu.CompilerParams(vmem_limit_bytes=VMEM_LIMIT_TAIL),
    )(x2, tgt2, mod3, *o_g, *lse_g, proj, proj, proj, proj, proj, proj, proj, proj, proj,
      w_ao, w_co, w_o, conv_w, conv_b, ln_g, ln_b)


def _conv_bwd(dyc, proj, conv_w, dproj):
    tm = 512
    per_seq = S // tm

    def body(d_ref, dn_ref, u_ref, c_ref, cw_ref, _, dproj_ref, g_ref, du_s, dc_s, sems):
        i = pl.program_id(0)
        last = (i % per_seq) == per_seq - 1

        @pl.when(i == 0)
        def _():
            g_ref[...] = jnp.zeros_like(g_ref)

        slot = i % 2

        def column_copies(s, row0):
            return _column_copies([(du_s.at[s], D * KB_U), (dc_s.at[s], D * KB_CG)], dproj_ref, row0, sems.at[s])

        @pl.when(i >= 2)
        def _():
            for cp in column_copies(slot, 0):
                cp.wait()

        d = d_ref[...]
        dn = jnp.where(last, 0.0, dn_ref[...])
        dcat = jnp.concatenate([d, dn], axis=0)
        d1 = pltpu.roll(dcat, tm + 8 - 1, 0)[:tm]
        d2 = pltpu.roll(dcat, tm + 8 - 2, 0)[:tm]
        dz = cw_ref[2:3, :] * d + cw_ref[1:2, :] * d1 + cw_ref[0:1, :] * d2
        u = u_ref[...].astype(F32)
        cg = c_ref[...].astype(F32)
        du_s[slot] = (dz * cg).astype(BF16)
        dc_s[slot] = (dz * u).astype(BF16)
        for cp in column_copies(slot, pl.multiple_of(i * tm, tm)):
            cp.start()

        z = cg * u
        g_ref[0:1, :] += jnp.sum(d2 * z, axis=0, keepdims=True)
        g_ref[1:2, :] += jnp.sum(d1 * z, axis=0, keepdims=True)
        g_ref[2:3, :] += jnp.sum(d * z, axis=0, keepdims=True)

        @pl.when(i == T // tm - 1)
        def _():
            for s in range(2):
                for cp in column_copies(s, 0):
                    cp.wait()

    n_tiles = T // tm
    next_rows = lambda i: jnp.minimum((i + 1) * (tm // 8), T // 8 - 1)
    return pl.pallas_call(
        body, name="conv_bwd",
        grid=(n_tiles,),
        in_specs=[pl.BlockSpec((tm, D), lambda i: (i, 0)),
                  pl.BlockSpec((8, D), lambda i: (next_rows(i), 0)),
                  pl.BlockSpec((tm, D), lambda i: (i, KB_U)),
                  pl.BlockSpec((tm, D), lambda i: (i, KB_CG)),
                  pl.BlockSpec((3, D), lambda i: (0, 0)),
                  pl.BlockSpec(memory_space=pl.ANY)],
        out_specs=(pl.BlockSpec(memory_space=pl.ANY),
                   pl.BlockSpec((8, D), lambda i: (0, 0))),
        out_shape=(jax.ShapeDtypeStruct((T, NCOL), BF16),
                   jax.ShapeDtypeStruct((8, D), F32)),
        scratch_shapes=[pltpu.VMEM((2, tm, D), BF16), pltpu.VMEM((2, tm, D), BF16), pltpu.SemaphoreType.DMA((2, 2))],
        input_output_aliases={5: 0},
        compiler_params=pltpu.CompilerParams(vmem_limit_bytes=VMEM_LIMIT),
    )(dyc, dyc, proj, proj, conv_w, dproj)


def _dh_dx(dproj, w_in_all, x2, dxd, mod3, chip_sums, hops=(), parts0=None):
    tm = 512
    per_seq = S // tm
    n_pass, _, width = w_in_all.shape
    n = len(chip_sums)
    n_in = 5 + n + (0 if parts0 is None else 1)

    def body(*refs):
        d_ref, w_ref, x_ref, dxd_ref, mod_ref = refs[:5]
        ins = refs[5:5 + n]
        gx_ref, vec_ref = refs[n_in:n_in + 2]
        outs = refs[n_in + 2:n_in + 2 + n]
        acc, send_sems, recv_sems, local_sems = refs[n_in + 2 + n:]
        jj, i = pl.program_id(0), pl.program_id(1)

        @pl.when((i == 0) & (jj == 0))
        def _():
            vec_ref[...] = jnp.zeros_like(vec_ref)
            if n:
                sends, _, mine = _chip_copies(ins, outs, send_sems, recv_sems, local_sems, hops)
                for cp in sends + mine:
                    cp.start()

        if n:
            @pl.when((i == T // tm - 1) & (jj == n_pass - 1))
            def _():
                sends, arrivals, mine = _chip_copies(ins, outs, send_sems, recv_sems, local_sems, hops)
                for cp in arrivals:
                    cp.wait_recv()
                for cp in sends:
                    cp.wait_send()
                for cp in mine:
                    cp.wait()

        def partial():
            return _dot_nt(d_ref[...], w_ref[...])

        @pl.when(jj == 0)
        def _():
            acc[i] = partial()

        @pl.when((jj > 0) & (jj < n_pass - 1))
        def _():
            acc[i] += partial()

        @pl.when(jj == n_pass - 1)
        def _():
            dh = acc[i] + partial()
            bidx = i // per_seq
            gx_ref[...] = dxd_ref[...] + dh * (1.0 + mod_ref[0, 1:2, :])
            dshift = jnp.sum(dh, axis=0, keepdims=True)
            dscale = jnp.sum(dh * x_ref[...], axis=0, keepdims=True)
            vec_ref[0:1, :] += jnp.where(bidx == 0, dshift, 0.0)
            vec_ref[1:2, :] += jnp.where(bidx == 1, dshift, 0.0)
            vec_ref[2:3, :] += jnp.where(bidx == 0, dscale, 0.0)
            vec_ref[3:4, :] += jnp.where(bidx == 1, dscale, 0.0)

    def last_pass(jj, i):
        return jnp.where(jj == n_pass - 1, i, 0)

    any_spec = pl.BlockSpec(memory_space=pl.ANY)
    res = pl.pallas_call(
        body, name="dh_dx",
        grid=(n_pass, T // tm),
        in_specs=[
            pl.BlockSpec((tm, width), lambda jj, i: (i, jj)),
            pl.BlockSpec((None, D, width), lambda jj, i: (jj, 0, 0)),
            pl.BlockSpec((tm, D), lambda jj, i: (last_pass(jj, i), 0)),
            pl.BlockSpec((tm, D), lambda jj, i: (last_pass(jj, i), 0)),
            pl.BlockSpec((1, 3, D), lambda jj, i: (last_pass(jj, i) // per_seq, 0, 0))]
                 + [any_spec] * (n_in - 5),
        out_specs=(pl.BlockSpec((tm, D), lambda jj, i: (last_pass(jj, i), 0)),
                   pl.BlockSpec((8, D), lambda jj, i: (0, 0))) + (any_spec,) * n,
        out_shape=(jax.ShapeDtypeStruct((T, D), F32), jax.ShapeDtypeStruct((8, D), F32))
                  + tuple(jax.ShapeDtypeStruct(a.shape, a.dtype) for a in chip_sums),
        scratch_shapes=[pltpu.VMEM((T // tm, tm, D), F32), pltpu.SemaphoreType.DMA((max(3 * n, 1),)),
                        pltpu.SemaphoreType.DMA((max(3 * n, 1),)), pltpu.SemaphoreType.DMA((max(n, 1),))],
        input_output_aliases={} if parts0 is None else {5 + n: 2},
        compiler_params=pltpu.CompilerParams(vmem_limit_bytes=VMEM_LIMIT),
    )(dproj, w_in_all, x2, dxd, mod3, *chip_sums, *([] if parts0 is None else [parts0]))
    return res[0], res[1], res[2:]


def _adam_step(g, w, m, v):
    nm = ADAM_B1 * m + (1.0 - ADAM_B1) * g
    nv = ADAM_B2 * v + (1.0 - ADAM_B2) * (g * g)
    m_hat = nm / (1.0 - ADAM_B1 ** ADAM_STEP)
    v_hat = nv / (1.0 - ADAM_B2 ** ADAM_STEP)
    return -ADAM_LR * (m_hat / (jnp.sqrt(v_hat) + ADAM_EPS) + ADAM_WD * w), nm, nv


def _adamw(parts, w, m, v, name, row_tile=None):
    n_parts, rows, cols = parts.shape
    tr = rows if row_tile is None else row_tile

    def body(p_ref, w_ref, m_ref, v_ref, g_ref, d_ref, nm_ref, nv_ref):
        g = p_ref[0].astype(F32)
        for s in range(1, n_parts):
            g = g + p_ref[s].astype(F32)
        g_ref[...] = g
        d_ref[...], nm_ref[...], nv_ref[...] = _adam_step(g, w_ref[...], m_ref[...], v_ref[...])

    blk = pl.BlockSpec((tr, cols), lambda i: (i, 0))
    shp = jax.ShapeDtypeStruct((rows, cols), F32)
    return pl.pallas_call(
        body, name=name,
        grid=(rows // tr,),
        in_specs=[pl.BlockSpec((n_parts, tr, cols), lambda i: (0, i, 0)), blk, blk, blk],
        out_specs=(blk, blk, blk, blk),
        out_shape=(shp, shp, shp, shp),
        compiler_params=pltpu.CompilerParams(vmem_limit_bytes=VMEM_LIMIT),
    )(parts, w, m, v)


def _multi_adamw(parts_list, params, name):
    n = len(params)
    flat = [t for wmv in params for t in wmv]

    def body(*refs):
        parts, ins, outs = refs[:n], refs[n:4 * n], refs[4 * n:]
        for p in range(n):
            g = parts[p][0].astype(F32)
            for s in range(1, parts[p].shape[0]):
                g = g + parts[p][s].astype(F32)
            w_ref, m_ref, v_ref = ins[3 * p:3 * p + 3]
            g_ref, d_ref, nm_ref, nv_ref = outs[4 * p:4 * p + 4]
            g_ref[...] = g
            d_ref[...], nm_ref[...], nv_ref[...] = _adam_step(g, w_ref[...], m_ref[...], v_ref[...])

    out_shape = []
    for w, _, _ in params:
        out_shape += [jax.ShapeDtypeStruct(w.shape, F32)] * 4
    res = pl.pallas_call(body, name=name, out_shape=tuple(out_shape))(*parts_list, *flat)
    return [res[4 * p:4 * p + 4] for p in range(n)]


def _small_updates(small_g, dmod_all, rel_parts, params):
    flat = [t for wmv in params for t in wmv]

    def body(sg_ref, dm_ref, rp_ref, *refs):
        ins, outs = refs[:len(flat)], refs[len(flat):]

        def over_devices(row):
            tot = sg_ref[0, row:row + 1, :]
            for s in range(1, N_DEV):
                tot = tot + sg_ref[s, row:row + 1, :]
            return tot

        g_b_ada = dm_ref[0:1, :]
        for r in range(1, N_DEV * BL):
            g_b_ada = g_b_ada + dm_ref[r:r + 1, :]
        g_rel = rp_ref[0]
        for s in range(1, N_DEV):
            g_rel = g_rel + rp_ref[s]
        grads = [g_b_ada, over_devices(0), g_rel, over_devices(1), over_devices(2)]
        outs[0][...] = jnp.sum(over_devices(3), axis=1, keepdims=True)
        for p, g in enumerate(grads):
            w_ref, m_ref, v_ref = ins[3 * p:3 * p + 3]
            g_ref, d_ref, nm_ref, nv_ref = outs[1 + 4 * p:5 + 4 * p]
            g_ref[...] = g
            d_ref[...], nm_ref[...], nv_ref[...] = _adam_step(g, w_ref[...], m_ref[...], v_ref[...])

    out_shape = [jax.ShapeDtypeStruct((1, 1), F32)]
    for w, _, _ in params:
        out_shape += [jax.ShapeDtypeStruct(w.shape, F32)] * 4
    res = pl.pallas_call(body, name="small_updates", out_shape=tuple(out_shape))(small_g, dmod_all, rel_parts, *flat)
    return res[0], [res[1 + 4 * p:5 + 4 * p] for p in range(len(params))]


def _attn_fwd_dense(proj, bias):
    nq = 4
    rows = nq * QB
    nsb = S // rows

    def body(q_ref, k_ref, v_ref, kp_ref, vp_ref, b_ref, o_ref, l_ref, ls0, ls1, ls2, ls3):
        ls = [ls0, ls1, ls2, ls3]
        n = pl.program_id(1)
        lane = lax.broadcasted_iota(jnp.int32, (QB, 128), 1)
        units = [(h, j) for h in range(4) for j in range(nq)]

        def keys(cur_ref, prev_ref, h, j):
            sl = slice(h * HD, (h + 1) * HD)
            if j == 0:
                return jnp.concatenate([prev_ref[:, sl], cur_ref[0:QB, sl]], axis=0)
            return cur_ref[(j - 1) * QB:(j + 1) * QB, sl]

        q = jnp.stack([q_ref[j * QB:(j + 1) * QB, h * HD:(h + 1) * HD] for h, j in units])
        k = jnp.stack([keys(k_ref, kp_ref, h, j) for h, j in units])
        v = jnp.stack([keys(v_ref, vp_ref, h, j) for h, j in units])
        bias_b = jnp.stack([b_ref[jnp.minimum(n, 1), h] if j == 0 else b_ref[1, h] for h, j in units])
        s = jnp.einsum("uqd,ukd->uqk", q, k, preferred_element_type=F32) * SCALE + bias_b
        m = jnp.max(s, axis=-1, keepdims=True)
        p = jnp.exp(s - m)
        l = jnp.sum(p, axis=-1, keepdims=True)
        o = jnp.einsum("uqk,ukd->uqd", p.astype(BF16), v, preferred_element_type=F32) / l
        lse = m + jnp.log(l)
        for i, (h, j) in enumerate(units):
            o_ref[j * QB:(j + 1) * QB, h * HD:(h + 1) * HD] = o[i]
            ls[h][j * QB:(j + 1) * QB, :] = jnp.where(lane == h, lse[i], 0.0)
        l_ref[...] = (ls[0][...] + ls[1][...]) + (ls[2][...] + ls[3][...])

    def row(b, n):
        return b * nsb + n

    def prev(b, n):
        return jnp.maximum((b * nsb + n) * nq - 1, 0)

    in_specs = [
        pl.BlockSpec((rows, GW), lambda b, n: (row(b, n), CB_Q)),
        pl.BlockSpec((rows, GW), lambda b, n: (row(b, n), CB_K)),
        pl.BlockSpec((rows, GW), lambda b, n: (row(b, n), CB_V)),
        pl.BlockSpec((QB, GW), lambda b, n: (prev(b, n), CB_K)),
        pl.BlockSpec((QB, GW), lambda b, n: (prev(b, n), CB_V)),
        pl.BlockSpec((None, 2, 4, QB, 2 * QB), lambda b, n: (0, 0, 0, 0, 0)),
    ]
    return pl.pallas_call(
        body, name="attn_fwd0",
        grid=(BL, nsb),
        in_specs=in_specs,
        out_specs=(pl.BlockSpec((rows, GW), lambda b, n: (row(b, n), 0)),
                   pl.BlockSpec((rows, 128), lambda b, n: (row(b, n), 0))),
        out_shape=(jax.ShapeDtypeStruct((T, GW), F32), jax.ShapeDtypeStruct((T, 128), F32)),
        scratch_shapes=[pltpu.VMEM((rows, 128), F32)] * 4,
        compiler_params=pltpu.CompilerParams(vmem_limit_bytes=VMEM_LIMIT),
    )(proj, proj, proj, proj, proj, bias)


def _attn_bwd_dense(proj, d_out, stats, bias, dproj):
    nq = 4
    rows = nq * QB
    nsb = S // rows
    cols_q, cols_k, cols_v = CB * CB_Q, CB * CB_K, CB * CB_V

    def body(q_ref, k_ref, v_ref, do_ref, st_ref, kp_ref, vp_ref, b_ref, _, out_ref, db_ref,
             sq, sk, sv, carry, sems):
        b, n = pl.program_id(0), pl.program_id(1)
        units = [(h, j) for h in range(4) for j in range(nq)]

        @pl.when((b == 0) & (n == 0))
        def _():
            db_ref[...] = jnp.zeros_like(db_ref)

        @pl.when(n == 0)
        def _():
            carry[...] = jnp.zeros_like(carry)

        step = b * (nsb + 1) + n
        slot = step % 2

        def block_copies(s, position, block):
            part = pl.ds(position * QB, QB)
            return _column_copies([(sq.at[s, part], cols_q), (sk.at[s, part], cols_k), (sv.at[s, part], cols_v)],
                                  out_ref, pl.multiple_of(block * QB, QB), sems.at[s, position])

        def wait_blocks(s, positions):
            for position in positions:
                for cp in block_copies(s, position, 0):
                    cp.wait()

        before = (step - 2) % (nsb + 1)

        @pl.when((step >= 2) & (before > 0))
        def _():
            wait_blocks(slot, [0])

        @pl.when((step >= 2) & (before < nsb))
        def _():
            wait_blocks(slot, range(1, nq))

        def write(first_block, position, count):
            for j in range(count):
                for cp in block_copies(slot, position + j, first_block + j):
                    cp.start()

        @pl.when(n == nsb)
        def _():
            sq[slot, 0:QB, :] = carry[:, 0:GW].astype(BF16)
            sk[slot, 0:QB, :] = carry[:, GW:2 * GW].astype(BF16)
            sv[slot, 0:QB, :] = carry[:, 2 * GW:3 * GW].astype(BF16)
            write((b + 1) * nsb * nq - 1, 0, 1)

            @pl.when(b == BL - 1)
            def _():
                wait_blocks(slot, [0])
                wait_blocks(1 - slot, range(nq))

        @pl.when(n < nsb)
        def _():
            def keys(cur_ref, prev_ref, h, j):
                sl = slice(h * HD, (h + 1) * HD)
                if j == 0:
                    return jnp.concatenate([prev_ref[:, sl], cur_ref[0:QB, sl]], axis=0)
                return cur_ref[(j - 1) * QB:(j + 1) * QB, sl]

            def block(ref, h, j):
                return ref[j * QB:(j + 1) * QB, h * HD:(h + 1) * HD]

            q = jnp.stack([block(q_ref, h, j) for h, j in units])
            do = jnp.stack([block(do_ref, h, j) for h, j in units])
            k = jnp.stack([keys(k_ref, kp_ref, h, j) for h, j in units])
            v = jnp.stack([keys(v_ref, vp_ref, h, j) for h, j in units])
            bias_b = jnp.stack([b_ref[jnp.minimum(n, 1), h] if j == 0 else b_ref[1, h] for h, j in units])
            lse = jnp.stack([st_ref[j * QB:(j + 1) * QB, h:h + 1] for h, j in units])
            delta = jnp.stack([st_ref[j * QB:(j + 1) * QB, 4 + h:5 + h] for h, j in units])
            s = jnp.einsum("uqd,ukd->uqk", q, k, preferred_element_type=F32) * SCALE + bias_b
            p = jnp.exp(s - lse)
            ds = p * (jnp.einsum("uqd,ukd->uqk", do, v, preferred_element_type=F32) - delta)
            for h in range(4):
                tot = ds[h * nq]
                for j in range(1, nq):
                    tot = tot + ds[h * nq + j]
                db_ref[h] += tot
            dsb, pb = ds.astype(BF16), p.astype(BF16)
            dq = jnp.einsum("uqk,ukd->uqd", dsb, k, preferred_element_type=F32) * SCALE
            dk = jnp.einsum("uqk,uqd->ukd", dsb, q, preferred_element_type=F32) * SCALE
            dv = jnp.einsum("uqk,uqd->ukd", pb, do, preferred_element_type=F32)
            for h in range(4):
                sl = slice(h * HD, (h + 1) * HD)
                u0, last = h * nq, h * nq + nq - 1
                sq[slot, 0:QB, sl] = carry[:, sl].astype(BF16)
                sk[slot, 0:QB, sl] = (carry[:, GW + h * HD:GW + (h + 1) * HD] + dk[u0, :QB]).astype(BF16)
                sv[slot, 0:QB, sl] = (carry[:, 2 * GW + h * HD:2 * GW + (h + 1) * HD] + dv[u0, :QB]).astype(BF16)
                for j in range(nq - 1):
                    pos = slice((j + 1) * QB, (j + 2) * QB)
                    sq[slot, pos, sl] = dq[u0 + j].astype(BF16)
                    sk[slot, pos, sl] = (dk[u0 + j, QB:] + dk[u0 + j + 1, :QB]).astype(BF16)
                    sv[slot, pos, sl] = (dv[u0 + j, QB:] + dv[u0 + j + 1, :QB]).astype(BF16)
                carry[:, sl] = dq[last]
                carry[:, GW + h * HD:GW + (h + 1) * HD] = dk[last, QB:]
                carry[:, 2 * GW + h * HD:2 * GW + (h + 1) * HD] = dv[last, QB:]

            @pl.when(n == 0)
            def _():
                write(b * nsb * nq, 1, nq - 1)

            @pl.when(n > 0)
            def _():
                write((b * nsb + n) * nq - 1, 0, nq)

    def row(b, n):
        return b * nsb + jnp.minimum(n, nsb - 1)

    def prev(b, n):
        return jnp.maximum(row(b, n) * nq - 1, 0)

    in_specs = [
        pl.BlockSpec((rows, GW), lambda b, n: (row(b, n), CB_Q)),
        pl.BlockSpec((rows, GW), lambda b, n: (row(b, n), CB_K)),
        pl.BlockSpec((rows, GW), lambda b, n: (row(b, n), CB_V)),
        pl.BlockSpec((rows, GW), lambda b, n: (row(b, n), 0)),
        pl.BlockSpec((rows, 128), lambda b, n: (row(b, n), 0)),
        pl.BlockSpec((QB, GW), lambda b, n: (prev(b, n), CB_K)),
        pl.BlockSpec((QB, GW), lambda b, n: (prev(b, n), CB_V)),
        pl.BlockSpec((None, 2, 4, QB, 2 * QB), lambda b, n: (0, 0, 0, 0, 0)),
        pl.BlockSpec(memory_space=pl.ANY),
    ]
    return pl.pallas_call(
        body, name="attn_bwd0",
        grid=(BL, nsb + 1),
        in_specs=in_specs,
        out_specs=(pl.BlockSpec(memory_space=pl.ANY),
                   pl.BlockSpec((4, QB, 2 * QB), lambda b, n: (0, 0, 0))),
        out_shape=(jax.ShapeDtypeStruct((T, NCOL), BF16),
                   jax.ShapeDtypeStruct((4, QB, 2 * QB), F32)),
        scratch_shapes=[pltpu.VMEM((2, rows, GW), BF16)] * 3
                       + [pltpu.VMEM((QB, 3 * GW), F32), pltpu.SemaphoreType.DMA((2, nq, 3))],
        input_output_aliases={8: 0},
        compiler_params=pltpu.CompilerParams(vmem_limit_bytes=VMEM_LIMIT),
    )(proj, proj, proj, d_out, stats, proj, proj, bias, dproj)


def _attention_forward(proj, rel_bias):
    expand_lanes, grad_lanes, masks = (jnp.asarray(t) for t in _bucket_maps())
    bias = _bias_expand(rel_bias, expand_lanes, masks)
    fwd = [_attn_fwd_dense(proj, bias)] + [_attn_fwd(proj, bias, g) for g in (1, 2)]
    return bias, grad_lanes, [f[0] for f in fwd], [f[1] for f in fwd]


def _local_step(x2, tgt2, mod3, h, proj, attn, w_ao, w_co, w_o, conv_w, conv_b, ln_g, ln_b):
    bias, buckets, o_g, lse_g = attn

    (dproj, dyc, d_o, stats, dxd, gw_o, gw_co, gw_ao, tail_vec) = _tail(
        x2, tgt2, mod3, o_g, lse_g, proj, w_ao, w_co, w_o, conv_w, conv_b, ln_g, ln_b)

    dproj, db = _attn_bwd_dense(proj, d_o, stats, bias, dproj)
    dbias = [db]
    for g in (1, 2):
        dproj, db = _attn_bwd(proj, d_o, stats, bias, dproj, g)
        dbias.append(db)
    g_rel_bias = _bias_grad(*dbias, buckets)
    dproj, conv_vec = _conv_bwd(dyc, proj, conv_w, dproj)

    gw_ao = jnp.transpose(gw_ao.reshape(GW, N_DEV, D // N_DEV), (1, 0, 2))
    return dproj, dxd, gw_ao, gw_co, gw_o, conv_vec, g_rel_bias, tail_vec


def kernel(x, c, w_ada, b_ada, w_in, conv_w, conv_b, rel_bias, w_attn_out, w_conv_out, w_o, ln_g, ln_b, loss_target, m_w_ada, m_b_ada, m_w_in, m_conv_w, m_conv_b, m_rel_bias, m_w_attn_out, m_w_conv_out, m_w_o, m_ln_g, m_ln_b, v_w_ada, v_b_ada, v_w_in, v_conv_w, v_conv_b, v_rel_bias, v_w_attn_out, v_w_conv_out, v_w_o, v_ln_g, v_ln_b):
    me = _my_index()
    x2 = x.reshape(T, D)
    tgt2 = loss_target.reshape(T, D)

    b_cols = lax.dynamic_slice(b_ada, (0, me * ADA_SHARD), (1, ADA_SHARD))
    c_g, mod_in = _mod_exchange(jnp.pad(c, ((0, 8 - BL), (0, 0))), w_ada[0], b_cols)
    c_all = c_g[:, 0:BL, :].reshape(N_DEV * BL, D)
    mod3 = jnp.transpose(mod_in[:, 0:BL, :], (1, 0, 2)).reshape(BL, 3, D)

    rows_shape = jax.ShapeDtypeStruct((N_DEV, D // N_DEV, D), BF16)
    proj, h, w_in_all, (w_ao_g, w_co_g, w_o_g, conv_w_g) = _gather_proj(
        _shard_order(), x2, mod3, w_in[0].astype(BF16), 1024,
        ([w_attn_out[0].astype(BF16), w_conv_out[0].astype(BF16), w_o[0].astype(BF16), conv_w[0]],
         [jax.ShapeDtypeStruct((N_DEV, GW, D // N_DEV), BF16), rows_shape, rows_shape,
          jax.ShapeDtypeStruct((N_DEV, 3, D // N_DEV), F32)]))

    attn = _attention_forward(proj, rel_bias)
    w_ao_full = jnp.transpose(w_ao_g, (1, 0, 2)).reshape(GW, D)
    w_co_full = w_co_g.reshape(D, D)
    w_o_full = w_o_g.reshape(D, D)
    conv_w_full = jnp.transpose(conv_w_g, (1, 0, 2)).reshape(3, D)

    (dproj, dxd, gw_ao, gw_co, gw_o, conv_vec, g_rel_bias, tail_vec) = _local_step(
        x2, tgt2, mod3, h, proj, attn, w_ao_full, w_co_full, w_o_full,
        conv_w_full, conv_b, ln_g, ln_b)

    g_conv_w_blocks = jnp.transpose(conv_vec[0:3].reshape(3, N_DEV, D // N_DEV), (1, 0, 2))
    partials = [gw_ao, gw_co.reshape(N_DEV, D // N_DEV, D), gw_o.reshape(N_DEV, D // N_DEV, D), g_conv_w_blocks]
    w_in_sums, w_in_parts, sib = _gw_in_pair(
        _slice_order(), h, dproj, partials,
        [jax.ShapeDtypeStruct((4, GW, D // N_DEV), BF16),
         jax.ShapeDtypeStruct((4, D // N_DEV, D), BF16),
         jax.ShapeDtypeStruct((4, D // N_DEV, D), BF16),
         jax.ShapeDtypeStruct((4, 3, D // N_DEV), F32)])
    core = lax.axis_index("c").astype(jnp.int32).reshape(1)
    chip_sums = [w_in_sums] + list(_pair_add(core, partials, sib))
    hops = [(3,)] + [(1, 2, 3)] * 4
    grad_x, mod_vec, (r_in, r_ao, r_co, r_o, r_cw) = _dh_dx(
        dproj, w_in_all, x2, dxd, mod3, chip_sums, hops, w_in_parts)

    small = jnp.concatenate([
        tail_vec[0:4],
        jnp.pad(g_rel_bias.reshape(1, N_BUCKETS * N_HEADS), ((0, 0), (0, D - N_BUCKETS * N_HEADS))),
        jnp.zeros((3, D), F32)], axis=0)
    dmod = jnp.concatenate([mod_vec[0:2], mod_vec[2:4], tail_vec[4:6]], axis=1)
    small_g, dmod_g = _all_gather(
        [small, dmod],
        [jax.ShapeDtypeStruct((N_DEV, 8, D), F32), jax.ShapeDtypeStruct((N_DEV, BL, 3 * D), F32)],
        "gather_small")
    dmod_all = dmod_g.reshape(N_DEV * BL, 3 * D)
    small_names = ["b_ada", "conv_b", "rel_bias", "ln_g", "ln_b"]
    small_params = [(b_ada, m_b_ada, v_b_ada), (conv_b, m_conv_b, v_conv_b), (rel_bias, m_rel_bias, v_rel_bias),
                    (ln_g, m_ln_g, v_ln_g), (ln_b, m_ln_b, v_ln_b)]
    loss, small_res = _small_updates(
        small_g, dmod_all, small_g[:, 4, :N_BUCKETS * N_HEADS].reshape(N_DEV, N_BUCKETS, N_HEADS), small_params)
    loss = loss.reshape(())

    dmod_cols = lax.dynamic_slice(dmod_all, (0, me * ADA_SHARD), (N_DEV * BL, ADA_SHARD))
    res = {
        "w_ada": tuple(t[None] for t in _w_ada_update(jnp.transpose(c_all), dmod_cols,
                                                      w_ada[0], m_w_ada[0], v_w_ada[0])),
        "w_in": tuple(t[None] for t in _adamw(r_in, w_in[0], m_w_in[0], v_w_in[0], "adam_w_in", 128)),
    }
    mid_names = ["conv_w", "w_attn_out", "w_conv_out", "w_o"]
    mid_parts = [r_cw, r_ao, r_co, r_o]
    mid_full = [(conv_w, m_conv_w, v_conv_w), (w_attn_out, m_w_attn_out, v_w_attn_out),
                (w_conv_out, m_w_conv_out, v_w_conv_out), (w_o, m_w_o, v_w_o)]
    mid_res = _multi_adamw(mid_parts, [tuple(t[0] for t in wmv) for wmv in mid_full], "adam_mid")
    for nm, wmv, outs4 in zip(mid_names, mid_full, mid_res):
        res[nm] = tuple(t[None] for t in outs4)
    res.update(dict(zip(small_names, small_res)))
    order = ["w_ada", "b_ada", "w_in", "conv_w", "conv_b", "rel_bias", "w_attn_out", "w_conv_out",
             "w_o", "ln_g", "ln_b"]
    outs = [loss, grad_x.reshape(BL, S, D)]
    for k in range(4):
        outs += [res[name][k] for name in order]
    return tuple(outs)
```

```python
import math

import numpy as np
import jax
import jax.numpy as jnp
from jax import lax
from jax.experimental import pallas as pl
from jax.experimental.pallas import tpu as pltpu

F32 = jnp.float32
BF16 = jnp.bfloat16
MESH = pl.DeviceIdType.MESH

N_DEV = 8
D = 1024
S = 2048
BL = 2
T = BL * S
NCOL = 11264
SHARD = NCOL // N_DEV
CB = 512
NCB = NCOL // CB
HD = 128
GW = 512
QB = 128
DILATIONS = (1, 4, 16)
N_STEPS = 128
N_BUCKETS = 32
N_HEADS = 12
ALPHA = 2.0 ** 0.25
LN_EPS = 1e-5
NEG_INF = -1e30
SCALE = HD ** -0.5
ADA_SHARD = 3 * D // N_DEV

CB_Q, CB_K, CB_V, CB_GA = 0, 3, 6, 9
KB_U, KB_BG, KB_CG, KB_GC, KB_MA, KB_MC = 5, 6, 7, 8, 9, 10

ADAM_LR, ADAM_B1, ADAM_B2, ADAM_EPS, ADAM_WD, ADAM_STEP = 0.001, 0.9, 0.999, 1e-08, 0.01, 10

VMEM_LIMIT = 56 * 1024 * 1024
VMEM_LIMIT_TAIL = 62 * 1024 * 1024


def _dot(a, b):
    return jnp.dot(a, b, preferred_element_type=F32)


def _dot_nt(a, b):
    return lax.dot_general(a, b, (((1,), (1,)), ((), ())), preferred_element_type=F32)


def _dot_tn(a, b):
    return lax.dot_general(a, b, (((0,), (0,)), ((), ())), preferred_element_type=F32)


def _sigmoid(v):
    return 1.0 / (1.0 + jnp.exp(-v))


def _column_copies(pieces, dst_hbm, row0, sems):
    copies = []
    for k, (src, col0) in enumerate(pieces):
        rows, width = src.shape
        copies.append(pltpu.make_async_copy(
            src, dst_hbm.at[pl.ds(row0, rows), pl.ds(col0, width)], sems.at[k]))
    return copies


def _my_index():
    return 4 * lax.axis_index("x") + 2 * lax.axis_index("y") + lax.axis_index("c")


class _Gather:
    def __init__(self, ins, outs, stage, send_sems, recv_sems, local_sems):
        self.ins, self.outs, self.stage = ins, outs, stage
        self.send_sems, self.recv_sems, self.local_sems = send_sems, recv_sems, local_sems
        x, y, c = lax.axis_index("x"), lax.axis_index("y"), lax.axis_index("c")
        self.c = c
        self.me, self.sibling = (x, y, c), (x, y, 1 - c)
        self.chips = [(1 - x, y), (x, 1 - y), (1 - x, 1 - y)]

    @staticmethod
    def scratch(arrs):
        n = len(arrs)
        return ([pltpu.SemaphoreType.DMA((7 * n,)), pltpu.SemaphoreType.DMA((7 * n,)),
                 pltpu.SemaphoreType.DMA((n,))] + [pltpu.VMEM(a.shape, a.dtype) for a in arrs])

    def _copy(self, a, k, block, to, src=None):
        dst = self.outs[a].at[4 * block[0] + 2 * block[1] + block[2]]
        return pltpu.make_async_remote_copy(
            src_ref=dst if src is None else src, dst_ref=dst,
            send_sem=self.send_sems.at[a * 7 + k], recv_sem=self.recv_sems.at[a * 7 + k],
            device_id=to, device_id_type=MESH)

    def _first(self):
        first = []
        for a in range(len(self.ins)):
            first.append(self._copy(a, 0, self.me, self.sibling, src=self.ins[a]))
            first += [self._copy(a, 1 + j, self.me, (*chip, self.c), src=self.ins[a])
                      for j, chip in enumerate(self.chips)]
        return first

    def _mine(self):
        me = self.me
        return [pltpu.make_async_copy(self.stage[a], self.outs[a].at[4 * me[0] + 2 * me[1] + me[2]],
                                      self.local_sems.at[a]) for a in range(len(self.ins))]

    def begin(self):
        for cp in self._first():
            cp.start()
        loads = [pltpu.make_async_copy(self.ins[a], self.stage[a], self.local_sems.at[a])
                 for a in range(len(self.ins))]
        for cp in loads:
            cp.start()
        for cp in loads:
            cp.wait()
        for cp in self._mine():
            cp.start()

    def finish(self):
        n, c, me, sibling = len(self.ins), self.c, self.me, self.sibling
        passed = []
        for j, chip in enumerate(self.chips):
            for a in range(n):
                self._copy(a, 1 + j, (*chip, c), me).wait_recv()
                fwd = self._copy(a, 4 + j, (*chip, c), sibling)
                fwd.start()
                passed.append(fwd)
        for a in range(n):
            self._copy(a, 0, sibling, me).wait_recv()
        for j, chip in enumerate(self.chips):
            for a in range(n):
                self._copy(a, 4 + j, (*chip, 1 - c), me).wait_recv()
        for cp in self._first() + passed:
            cp.wait_send()
        for cp in self._mine():
            cp.wait()


def _all_gather(arrs, out_shapes, name):
    n = len(arrs)

    def body(*refs):
        g = _Gather(refs[:n], refs[n:2 * n], refs[2 * n + 3:], *refs[2 * n:2 * n + 3])
        g.begin()
        g.finish()

    any_spec = pl.BlockSpec(memory_space=pl.ANY)
    return pl.pallas_call(
        body, name=name,
        out_shape=tuple(out_shapes),
        in_specs=[any_spec] * n,
        out_specs=tuple([any_spec] * n),
        scratch_shapes=_Gather.scratch(arrs),
    )(*arrs)


def _neighbour_chips():
    x, y, c = lax.axis_index("x"), lax.axis_index("y"), lax.axis_index("c")
    first = (jnp.where(c == 0, 1 - x, x), jnp.where(c == 0, y, 1 - y))
    second = (jnp.where(c == 0, x, 1 - x), jnp.where(c == 0, 1 - y, y))
    return first, second, (1 - x, 1 - y)


def _slice_order():
    x, y, c = lax.axis_index("x"), lax.axis_index("y"), lax.axis_index("c")
    nb1, nb2, diag = _neighbour_chips()
    slots = []
    for mine, theirs in ((nb1, nb2), (nb2, nb1), (diag, diag), ((x, y), (x, y))):
        slots += [2 * (2 * theirs[0] + theirs[1]) + 1 - c, 2 * (2 * mine[0] + mine[1]) + c]
    return jnp.stack(slots).astype(jnp.int32)


def _gw_in_pair(order, h, dproj, smalls, small_shapes4):
    kk, m = h.shape
    tk = min(kk, 2048)
    nk = kk // tk
    ncols = dproj.shape[1] // N_DEV
    n = len(smalls)

    def body(order_ref, h_ref, d_ref, *rest):
        ins = rest[:n]
        sums_hbm, parts_hbm = rest[n], rest[n + 1]
        sib = rest[n + 2:2 * n + 2]
        (acc, sendbuf, recvbuf, sumbuf, send_sems, recv_sems, local_sem, ssend, srecv,
         isend, irecv) = rest[2 * n + 2:]
        js, k = pl.program_id(0), pl.program_id(1)
        x, y, c = lax.axis_index("x"), lax.axis_index("y"), lax.axis_index("c")
        sibling = (x, y, 1 - c)
        my_chip = 2 * x + y
        nb1, nb2, _ = _neighbour_chips()
        near = [(*nb1, c), (*nb2, c)]

        def ici_copy(p, out_chip):
            peer = near[p] if isinstance(p, int) else tuple(jnp.where(p == 0, a, b) for a, b in zip(*near))
            return pltpu.make_async_remote_copy(
                src_ref=sumbuf.at[p], dst_ref=parts_hbm.at[out_chip],
                send_sem=isend.at[p], recv_sem=irecv.at[p], device_id=peer, device_id_type=MESH)

        def small_copies():
            return [pltpu.make_async_remote_copy(
                        src_ref=ins[a].at[2 * q + 1 - c], dst_ref=sib[a].at[q],
                        send_sem=ssend.at[a * 4 + q], recv_sem=srecv.at[a * 4 + q],
                        device_id=sibling, device_id_type=MESH)
                    for a in range(n) for q in range(4)]

        def slice_copy(p):
            return pltpu.make_async_remote_copy(
                src_ref=sendbuf, dst_ref=recvbuf.at[p], send_sem=send_sems.at[p], recv_sem=recv_sems.at[p],
                device_id=sibling, device_id_type=MESH)

        def sum_copy(p):
            return pltpu.make_async_copy(sumbuf.at[2], sums_hbm.at[order_ref[2 * p] // 2], local_sem)

        @pl.when((js == 0) & (k == 0))
        def _():
            for cp in small_copies():
                cp.start()

        def partial():
            return _dot_tn(h_ref[...], d_ref[...])

        if nk > 1:
            @pl.when(k == 0)
            def _():
                acc[...] = partial()
        if nk > 2:
            @pl.when((k > 0) & (k < nk - 1))
            def _():
                acc[...] += partial()

        def total():
            return partial() + acc[...] if nk > 1 else partial()

        p = js // 2

        @pl.when((js % 2 == 0) & (k == nk - 1))
        def _():
            @pl.when(p > 0)
            def _():
                slice_copy(p - 1).wait_send()
            sendbuf[...] = total().astype(BF16)
            slice_copy(p).start()

        @pl.when((js % 2 == 1) & (k == nk - 1))
        def _():
            slice_copy(p).wait_recv()

            @pl.when(p == 3)
            def _():
                sum_copy(2).wait()
            sumbuf[jnp.minimum(p, 2)] = (total() + recvbuf[p].astype(F32)).astype(BF16)

            @pl.when(p < 2)
            def _():
                ici_copy(p, my_chip).start()

            @pl.when(p >= 2)
            def _():
                sum_copy(p).start()

        @pl.when((js == N_DEV - 1) & (k == nk - 1))
        def _():
            slice_copy(3).wait_send()
            sum_copy(3).wait()
            for cp in small_copies():
                cp.wait()
            for p in range(2):
                ici_copy(p, 2 * near[p][0] + near[p][1]).wait_recv()
                ici_copy(p, my_chip).wait_send()

    any_spec = pl.BlockSpec(memory_space=pl.ANY)
    res = pl.pallas_call(
        body, name="gw_in_pair",
        grid_spec=pltpu.PrefetchScalarGridSpec(
            num_scalar_prefetch=1,
            grid=(N_DEV, nk),
            in_specs=[pl.BlockSpec((tk, m), lambda js, k, order_ref: (k, 0)),
                      pl.BlockSpec((tk, ncols), lambda js, k, order_ref: (k, order_ref[js]))] + [any_spec] * n,
            out_specs=(any_spec,) * (n + 2),
            scratch_shapes=[pltpu.VMEM((m, ncols), F32), pltpu.VMEM((m, ncols), BF16),
                            pltpu.VMEM((4, m, ncols), BF16), pltpu.VMEM((3, m, ncols), BF16),
                            pltpu.SemaphoreType.DMA((4,)), pltpu.SemaphoreType.DMA((4,)),
                            pltpu.SemaphoreType.DMA,
                            pltpu.SemaphoreType.DMA((4 * n,)), pltpu.SemaphoreType.DMA((4 * n,)),
                            pltpu.SemaphoreType.DMA((2,)), pltpu.SemaphoreType.DMA((2,))]),
        out_shape=(jax.ShapeDtypeStruct((4, m, ncols), BF16),) * 2 + tuple(small_shapes4),
        compiler_params=pltpu.CompilerParams(vmem_limit_bytes=VMEM_LIMIT),
    )(order, h, dproj, *smalls)
    return res[0], res[1], res[2:]


def _chip_copies(ins, outs, send_sems, recv_sems, local_sems, hops):
    n = len(ins)
    x, y, c = lax.axis_index("x"), lax.axis_index("y"), lax.axis_index("c")
    my_chip = 2 * x + y

    def peer_of(k):
        return ((1 - x) if (k >> 1) & 1 else x, (1 - y) if k & 1 else y, c)

    def copy(a, k, out_chip):
        peer = peer_of(k)
        return pltpu.make_async_remote_copy(
            src_ref=ins[a].at[2 * peer[0] + peer[1]], dst_ref=outs[a].at[out_chip],
            send_sem=send_sems.at[a * 3 + k - 1], recv_sem=recv_sems.at[a * 3 + k - 1],
            device_id=peer, device_id_type=MESH)

    sends = [copy(a, k, my_chip) for k in range(1, 4) for a in range(n) if k in hops[a]]
    arrivals = []
    for k in range(1, 4):
        peer = peer_of(k)
        arrivals += [copy(a, k, 2 * peer[0] + peer[1]) for a in range(n) if k in hops[a]]
    mine = [pltpu.make_async_copy(ins[a].at[my_chip], outs[a].at[my_chip], local_sems.at[a])
            for a in range(n)]
    return sends, arrivals, mine


def _pair_add(core, mines, theirs):
    n = len(mines)

    def body(core_ref, *refs):
        mine, sib, outs = refs[:n], refs[n:2 * n], refs[2 * n:]
        for a in range(n):
            for q in range(4):
                outs[a][q] = (mine[a][2 * q + core_ref[0]].astype(F32)
                              + sib[a][q].astype(F32)).astype(outs[a].dtype)

    return pl.pallas_call(
        body, name="pair_add",
        in_specs=[pl.BlockSpec(memory_space=pltpu.SMEM)] + [pl.BlockSpec(memory_space=pltpu.VMEM)] * (2 * n),
        out_shape=tuple(jax.ShapeDtypeStruct(t.shape, t.dtype) for t in theirs),
    )(core, *mines, *theirs)


def _mod_exchange(c8, w_ada, b_cols):
    cols = w_ada.shape[1]

    def body(c_ref, w_ref, b_ref, call_ref, mod_ref, msend, send1, recv1, send2, recv2):
        x, y, c = lax.axis_index("x"), lax.axis_index("y"), lax.axis_index("c")
        my_slot = 4 * x + 2 * y + c

        def peer_of(k):
            return ((1 - x) if (k >> 2) & 1 else x, (1 - y) if (k >> 1) & 1 else y, (1 - c) if k & 1 else c)

        def slot_of(dev):
            return 4 * dev[0] + 2 * dev[1] + dev[2]

        def exchange(src_of, dst_ref, send_sems, recv_sems):
            sends, arrivals = [], []
            for k in range(1, 8):
                peer = peer_of(k)
                sends.append(pltpu.make_async_remote_copy(
                    src_ref=src_of(slot_of(peer)), dst_ref=dst_ref.at[my_slot],
                    send_sem=send_sems.at[k - 1], recv_sem=recv_sems.at[k - 1],
                    device_id=peer, device_id_type=MESH))
                arrivals.append(pltpu.make_async_remote_copy(
                    src_ref=src_of(my_slot), dst_ref=dst_ref.at[slot_of(peer)],
                    send_sem=send_sems.at[k - 1], recv_sem=recv_sems.at[k - 1],
                    device_id=peer, device_id_type=MESH))
            for cp in sends:
                cp.start()
            for cp in arrivals:
                cp.wait_recv()
            for cp in sends:
                cp.wait_send()

        call_ref[my_slot] = c_ref[...]
        exchange(lambda s: c_ref, call_ref, send1, recv1)
        cv = call_ref[...].reshape(N_DEV * 8, c_ref.shape[1])
        act = cv * _sigmoid(cv)
        mod = jnp.dot(act, w_ref[...], preferred_element_type=F32,
                      precision=lax.Precision.HIGHEST) + b_ref[...]
        msend[...] = mod.reshape(N_DEV, 8, cols)
        mod_ref[my_slot] = msend[my_slot]
        exchange(lambda s: msend.at[s], mod_ref, send2, recv2)

    return pl.pallas_call(
        body, name="mod_exchange",
        out_shape=(jax.ShapeDtypeStruct((N_DEV, 8, c8.shape[1]), F32),
                   jax.ShapeDtypeStruct((N_DEV, 8, cols), F32)),
        scratch_shapes=[pltpu.VMEM((N_DEV, 8, cols), F32)] + [pltpu.SemaphoreType.DMA((7,))] * 4,
    )(c8, w_ada, b_cols)


def _w_ada_update(c_all_t, dmod_cols, w, m, v):
    rows, cols = w.shape
    tr = 256

    def body(c_ref, d_ref, w_ref, m_ref, v_ref, g_ref, dl_ref, nm_ref, nv_ref):
        cv = c_ref[...]
        g = jnp.dot(cv * _sigmoid(cv), d_ref[...], preferred_element_type=F32,
                    precision=lax.Precision.HIGHEST)
        g_ref[...] = g
        dl_ref[...], nm_ref[...], nv_ref[...] = _adam_step(g, w_ref[...], m_ref[...], v_ref[...])

    blk = pl.BlockSpec((tr, cols), lambda i: (i, 0))
    shp = jax.ShapeDtypeStruct((rows, cols), F32)
    return pl.pallas_call(
        body, name="adam_w_ada",
        grid=(rows // tr,),
        in_specs=[pl.BlockSpec((tr, c_all_t.shape[1]), lambda i: (i, 0)),
                  pl.BlockSpec(dmod_cols.shape, lambda i: (0, 0)), blk, blk, blk],
        out_specs=(blk, blk, blk, blk),
        out_shape=(shp, shp, shp, shp),
    )(c_all_t, dmod_cols, w, m, v)


def _shard_order():
    x, y, c = lax.axis_index("x"), lax.axis_index("y"), lax.axis_index("c")
    first, second, diag = _neighbour_chips()
    devs = [(x, y, c), (x, y, 1 - c), (*first, c), (*second, 1 - c), (*second, c), (*first, 1 - c),
            (*diag, c), (*diag, 1 - c)]
    return jnp.stack([4 * d[0] + 2 * d[1] + d[2] for d in devs]).astype(jnp.int32)


def _gather_proj(order, x2, mod3, w_shard, tm, ride=()):
    rows, kdim = x2.shape
    ncols = w_shard.shape[1]
    n_i = rows // tm
    per_seq = n_i // mod3.shape[0]
    ride_arrs, ride_shapes = ride if ride else ((), ())
    n_ride = len(ride_arrs)

    def body(order_ref, x_ref, mod_ref, mine_hbm, *rest):
        ride_ins = rest[:n_ride]
        o_ref, h_ref, all_hbm = rest[n_ride:n_ride + 3]
        ride_outs = rest[n_ride + 3:2 * n_ride + 3]
        wv, send_sems, recv_sems, local_sems, hs = rest[2 * n_ride + 3:2 * n_ride + 8]
        ride_scr = rest[2 * n_ride + 8:]
        j, i = pl.program_id(0), pl.program_id(1)
        c = lax.axis_index("c")
        me, sibling = (lax.axis_index("x"), lax.axis_index("y"), c), (lax.axis_index("x"), lax.axis_index("y"), 1 - c)
        nb1, nb2, diag = _neighbour_chips()

        def slot(dev):
            return 4 * dev[0] + 2 * dev[1] + dev[2]

        def copy(k, block, to, src=None, part=None):
            buf = wv.at[slot(block)]
            if part is not None:
                buf = buf.at[pl.ds(pl.multiple_of(part * (kdim // 2), kdim // 2), kdim // 2)]
            return pltpu.make_async_remote_copy(
                src_ref=buf if src is None else src, dst_ref=buf,
                send_sem=send_sems.at[k], recv_sem=recv_sems.at[k],
                device_id=to, device_id_type=MESH)

        def keep(step, block):
            s = slot(block)
            cols = pl.ds(pl.multiple_of((s % 2) * ncols, 128), ncols)
            return pltpu.make_async_copy(wv.at[s], all_hbm.at[s // 2, :, cols], local_sems.at[step])

        if n_ride:
            gather = _Gather(ride_ins, ride_outs, ride_scr[3:], *ride_scr[:3])
        to_sibling, to_nb1, to_nb2 = (copy(0, me, sibling, mine_hbm), copy(1, me, (*nb1, c), mine_hbm),
                                      copy(2, me, (*nb2, c), mine_hbm))
        relay1, relay2 = copy(3, (*nb2, c), (*nb1, c), part=c), copy(4, (*nb1, c), (*nb2, c), part=1 - c)
        pass_nb1, pass_nb2 = copy(5, (*nb1, c), sibling), copy(6, (*nb2, c), sibling)
        pass_d1, pass_d2 = copy(7, (*diag, c), sibling, part=c), copy(8, (*diag, c), sibling, part=1 - c)
        sends = [to_sibling, to_nb1, to_nb2, relay1, relay2, pass_nb1, pass_nb2, pass_d1, pass_d2]
        due = [
            (me, [], []),
            (sibling, [copy(0, sibling, me)], [[]]),
            ((*nb1, c), [copy(1, (*nb1, c), me)], [[pass_nb1, to_nb2]]),
            ((*nb2, 1 - c), [copy(5, (*nb2, 1 - c), me)], [[]]),
            ((*nb2, c), [copy(2, (*nb2, c), me)], [[pass_nb2, relay1, relay2]]),
            ((*nb1, 1 - c), [copy(6, (*nb1, 1 - c), me)], [[]]),
            ((*diag, c), [copy(3, (*diag, c), me, part=c), copy(4, (*diag, c), me, part=1 - c)],
             [[pass_d1], [pass_d2]]),
            ((*diag, 1 - c), [copy(7, (*diag, 1 - c), me, part=1 - c), copy(8, (*diag, 1 - c), me, part=c)],
             [[], []]),
        ]

        @pl.when((j == 0) & (i == 0))
        def _():
            to_sibling.start()
            to_nb1.start()
            load = pltpu.make_async_copy(mine_hbm, wv.at[slot(me)], local_sems.at[N_DEV])
            load.start()
            load.wait()
            keep(0, me).start()

        for step in range(1, N_DEV):
            block, arrivals, then = due[step]

            @pl.when((j == step) & (i == 0))
            def _():
                for arrival, follow in zip(arrivals, then):
                    arrival.wait_recv()
                    for cp in follow:
                        cp.start()
                keep(step, block).start()
                if n_ride and step == N_DEV - 2:
                    gather.begin()

        @pl.when(j == 0)
        def _():
            hb = (x_ref[...] * (1.0 + mod_ref[0, 1:2, :]) + mod_ref[0, 0:1, :]).astype(BF16)
            hs[i] = hb
            h_ref[...] = hb

        o_ref[...] = _dot(hs[i], wv[order_ref[j]]).astype(BF16)

        @pl.when((j == N_DEV - 1) & (i == n_i - 1))
        def _():
            for cp in sends:
                cp.wait_send()
            for step in range(N_DEV):
                keep(step, due[step][0]).wait()
            if n_ride:
                gather.finish()

    def first_pass(j, i):
        return jnp.where(j == 0, i, n_i - 1)

    any_spec = pl.BlockSpec(memory_space=pl.ANY)
    res = pl.pallas_call(
        body, name="gather_proj",
        grid_spec=pltpu.PrefetchScalarGridSpec(
            num_scalar_prefetch=1,
            grid=(N_DEV, n_i),
            in_specs=[pl.BlockSpec((tm, kdim), lambda j, i, order_ref: (first_pass(j, i), 0)),
                      pl.BlockSpec((1, 3, kdim), lambda j, i, order_ref: (first_pass(j, i) // per_seq, 0, 0)),
                      any_spec] + [any_spec] * n_ride,
            out_specs=(pl.BlockSpec((tm, ncols), lambda j, i, order_ref: (i, order_ref[j])),
                       pl.BlockSpec((tm, kdim), lambda j, i, order_ref: (first_pass(j, i), 0)), any_spec)
                      + (any_spec,) * n_ride,
            scratch_shapes=[pltpu.VMEM((N_DEV, kdim, ncols), BF16),
                            pltpu.SemaphoreType.DMA((9,)), pltpu.SemaphoreType.DMA((9,)),
                            pltpu.SemaphoreType.DMA((N_DEV + 1,)), pltpu.VMEM((n_i, tm, kdim), BF16)]
                           + (_Gather.scratch(ride_arrs) if n_ride else [])),
        out_shape=(jax.ShapeDtypeStruct((rows, N_DEV * ncols), BF16),
                   jax.ShapeDtypeStruct((rows, kdim), BF16),
                   jax.ShapeDtypeStruct((N_DEV // 2, kdim, 2 * ncols), BF16)) + tuple(ride_shapes),
        compiler_params=pltpu.CompilerParams(vmem_limit_bytes=VMEM_LIMIT),
    )(order, x2, mod3, w_shard, *ride_arrs)
    return res[0], res[1], res[2], res[3:]


SKEW_W = 512


def _bucket_maps():
    lanes = np.arange(SKEW_W)

    def buckets_of(steps):
        rows = []
        for dil in DILATIONS:
            dist = np.maximum(steps, 0) * dil
            nf = np.maximum(dist, 1).astype(np.float32)
            large = 16 + (np.log(nf / np.float32(16)) / np.float32(math.log(128.0))
                          * np.float32(16)).astype(np.int32)
            large = np.minimum(large, N_BUCKETS - 1)
            bucket = np.where(dist < 16, dist, large)
            rows.append(np.where((steps >= 0) & (steps <= N_STEPS), bucket, -1).astype(np.int32))
        return np.stack(rows)[:, None, :]

    a = np.arange(QB)[:, None]
    b = np.arange(2 * QB)[None, :]
    steps = a + QB - b
    band = (steps >= 0) & (steps <= N_STEPS)
    first = band & (b >= QB)
    masks = np.stack([first, band]).astype(np.int32)
    return buckets_of(QB - lanes), buckets_of(2 * QB - 1 - lanes), masks


def _bias_expand(rel_bias, lane_buckets, masks):
    def body(tab_ref, bk_ref, mk_ref, o_ref):
        for g in range(3):
            bk = bk_ref[g]
            for h in range(4):
                col = 4 * g + h
                per_offset = jnp.zeros((1, SKEW_W), F32)
                for k in range(N_BUCKETS):
                    per_offset = jnp.where(bk == k, tab_ref[k, col], per_offset)
                tile = pltpu.roll(jnp.broadcast_to(per_offset, (QB, SKEW_W)), 0, 1, stride=1, stride_axis=0)
                tile = tile[:, :2 * QB]
                o_ref[g, 0, h] = jnp.where(mk_ref[0] != 0, tile, NEG_INF)
                o_ref[g, 1, h] = jnp.where(mk_ref[1] != 0, tile, NEG_INF)

    return pl.pallas_call(
        body, name="bias_expand",
        in_specs=[pl.BlockSpec(memory_space=pltpu.SMEM),
                  pl.BlockSpec(memory_space=pltpu.VMEM),
                  pl.BlockSpec(memory_space=pltpu.VMEM)],
        out_shape=jax.ShapeDtypeStruct((3, 2, 4, QB, 2 * QB), F32),
    )(rel_bias, lane_buckets, masks)


def _bias_grad(ds1, ds2, ds3, lane_buckets):
    exchange = jnp.asarray(np.eye(QB, dtype=np.float32)[::-1].copy())

    def body(d1_ref, d2_ref, d3_ref, bk_ref, ex_ref, o_ref):
        for g, d_ref in enumerate((d1_ref, d2_ref, d3_ref)):
            bk = bk_ref[g]
            for h in range(4):
                flipped = jnp.dot(ex_ref[...], d_ref[h], preferred_element_type=F32,
                                  precision=lax.Precision.HIGHEST)
                padded = jnp.concatenate([flipped, jnp.zeros((QB, SKEW_W - 2 * QB), F32)], axis=1)
                skewed = pltpu.roll(padded, 0, 1, stride=1, stride_axis=0)
                per_offset = jnp.sum(skewed, axis=0, keepdims=True)
                for k in range(N_BUCKETS):
                    o_ref[k, 4 * g + h] = jnp.sum(jnp.where(bk == k, per_offset, 0.0))

    return pl.pallas_call(
        body, name="bias_grad",
        in_specs=[pl.BlockSpec(memory_space=pltpu.VMEM)] * 5,
        out_specs=pl.BlockSpec(memory_space=pltpu.SMEM),
        out_shape=jax.ShapeDtypeStruct((N_BUCKETS, N_HEADS), F32),
    )(ds1, ds2, ds3, lane_buckets, exchange)


def _scratch_sets(rows):
    return 4 if rows <= 512 else 1


def _unit_chunks(dil, size=16):
    units = [(h, r) for h in range(4) for r in range(dil)]
    return [units[i:i + size] for i in range(0, len(units), size)]


def _residue_rows(src_ref, copies, h, residue):
    buf = copies[h % len(copies)]
    buf[...] = src_ref[:, h * HD:(h + 1) * HD].astype(F32)
    return lambda r: buf[residue(r), :].astype(BF16)


def _attn_fwd(proj, bias, g):
    dil = DILATIONS[g]
    rows = QB * dil
    nsb = S // rows
    has_prev = nsb > 1

    def residue(r):
        return pl.ds(r, QB, stride=dil)

    n_sets = _scratch_sets(rows)
    n_in = 6 if has_prev else 4
    n_copied = (4 + (2 if has_prev else 0)) * n_sets

    def body(*refs):
        q_ref, kc_ref, vc_ref = refs[:3]
        kp_ref, vp_ref = refs[3:5] if has_prev else (None, None)
        b_ref = refs[n_in - 1]
        o_ref, l_ref = refs[n_in:n_in + 2]
        scr = list(refs[n_in + 2:])
        ls = [scr.pop(0) for _ in range(4)]
        copies = {name: [scr.pop(0) for _ in range(n_sets)]
                  for name in ("q", "kc", "vc", "o") + (("kp", "vp") if has_prev else ())}
        lane = lax.broadcasted_iota(jnp.int32, (QB, 128), 1)
        refs_of = {"q": q_ref, "kc": kc_ref, "vc": vc_ref, "kp": kp_ref, "vp": vp_ref}
        for chunk in _unit_chunks(dil):
            rows_of = {h: {name: _residue_rows(refs_of[name], copies[name], h, residue)
                           for name in refs_of if refs_of[name] is not None}
                       for h in sorted({h for h, _ in chunk})}

            def batch(name):
                return jnp.stack([rows_of[h][name](r) for h, r in chunk])

            q, k, v = batch("q"), batch("kc"), batch("vc")
            if has_prev:
                k = jnp.concatenate([batch("kp"), k], axis=1)
                v = jnp.concatenate([batch("vp"), v], axis=1)
                bias_b = jnp.stack([b_ref[h] for h, _ in chunk])
            else:
                bias_b = jnp.stack([b_ref[h, :, QB:] for h, _ in chunk])
            s = jnp.einsum("uqd,ukd->uqk", q, k, preferred_element_type=F32) * SCALE + bias_b
            m = jnp.max(s, axis=-1, keepdims=True)
            p = jnp.exp(s - m)
            l = jnp.sum(p, axis=-1, keepdims=True)
            o = jnp.einsum("uqk,ukd->uqd", p.astype(BF16), v, preferred_element_type=F32) / l
            lse = m + jnp.log(l)
            for i, (h, r) in enumerate(chunk):
                copies["o"][h % n_sets][residue(r), :] = o[i]
                ls[h][r * QB:(r + 1) * QB, :] = jnp.where(lane == h, lse[i], 0.0)
            for h in sorted({h for h, _ in chunk}):
                o_ref[:, h * HD:(h + 1) * HD] = copies["o"][h % n_sets][...]
        for r in range(dil):
            blk = slice(r * QB, (r + 1) * QB)
            l_ref[residue(r), :] = (ls[0][blk, :] + ls[1][blk, :]) + (ls[2][blk, :] + ls[3][blk, :])

    def row(b, n):
        return b * nsb + n

    def prev(b, n):
        return b * nsb + jnp.maximum(n - 1, 0)

    in_specs = [
        pl.BlockSpec((rows, GW), lambda b, n: (row(b, n), CB_Q + g)),
        pl.BlockSpec((rows, GW), lambda b, n: (row(b, n), CB_K + g)),
        pl.BlockSpec((rows, GW), lambda b, n: (row(b, n), CB_V + g)),
    ]
    args = [proj, proj, proj]
    scratch = [pltpu.VMEM((rows, 128), F32)] * (4 + n_copied)
    if has_prev:
        in_specs += [pl.BlockSpec((rows, GW), lambda b, n: (prev(b, n), CB_K + g)),
                     pl.BlockSpec((rows, GW), lambda b, n: (prev(b, n), CB_V + g))]
        args += [proj, proj]
    in_specs.append(pl.BlockSpec((None, None, 4, QB, 2 * QB),
                                 lambda b, n: (g, jnp.minimum(n, 1), 0, 0, 0)))
    args.append(bias)
    return pl.pallas_call(
        body, name=f"attn_fwd{g}",
        grid=(BL, nsb),
        in_specs=in_specs,
        out_specs=(pl.BlockSpec((rows, GW), lambda b, n: (row(b, n), 0)),
                   pl.BlockSpec((rows, 128), lambda b, n: (row(b, n), 0))),
        out_shape=(jax.ShapeDtypeStruct((T, GW), F32), jax.ShapeDtypeStruct((T, 128), F32)),
        scratch_shapes=scratch,
        compiler_params=pltpu.CompilerParams(vmem_limit_bytes=VMEM_LIMIT),
    )(*args)


def _attn_bwd(proj, d_out, stats, bias, dproj, g):
    dil = DILATIONS[g]
    rows = QB * dil
    nsb = S // rows
    has_prev = nsb > 1
    n_steps = nsb + 1 if has_prev else 1
    n_in = 7 + (2 if has_prev else 0)

    def residue(r):
        return pl.ds(r, QB, stride=dil)

    n_sets = _scratch_sets(rows)

    def body(*refs):
        q_ref, kc_ref, vc_ref, do_ref, st_ref, b_ref = refs[:6]
        kp_ref, vp_ref = refs[6:8] if has_prev else (None, None)
        out_ref, db_ref = refs[n_in], refs[n_in + 1]
        scr = list(refs[n_in + 2:])
        sq, sk, sv, sems = [scr.pop(0) for _ in range(4)]
        carry = scr.pop(0) if has_prev else None
        sts = scr.pop(0)
        copies = {name: [scr.pop(0) for _ in range(n_sets)]
                  for name in ("q", "kc", "vc", "do", "dq", "dk", "dv") + (("kp", "vp") if has_prev else ())}
        b, n = pl.program_id(0), pl.program_id(1)

        @pl.when((b == 0) & (n == 0))
        def _():
            db_ref[...] = jnp.zeros_like(db_ref)

        def finish(h, r, dq, dk, dv):
            for name, val in (("dq", dq), ("dk", dk), ("dv", dv)):
                copies[name][h % n_sets][residue(r), :] = val

        step = b * n_steps + n
        slot = step % 2

        def stage_copies(s, row0):
            return _column_copies([(sq.at[s], CB * (CB_Q + g)), (sk.at[s], CB * (CB_K + g)),
                                   (sv.at[s], CB * (CB_V + g))], out_ref, row0, sems.at[s])

        @pl.when((step >= 2) & ((step - 2) % n_steps >= (1 if has_prev else 0)))
        def _():
            for cp in stage_copies(slot, 0):
                cp.wait()

        def finish_head(h):
            sl = slice(h * HD, (h + 1) * HD)
            sq[slot, :, sl] = copies["dq"][h % n_sets][...].astype(BF16)
            sk[slot, :, sl] = copies["dk"][h % n_sets][...].astype(BF16)
            sv[slot, :, sl] = copies["dv"][h % n_sets][...].astype(BF16)

        def write_block(blk_idx):
            for cp in stage_copies(slot, pl.multiple_of(blk_idx * rows, rows)):
                cp.start()

            @pl.when(step == BL * n_steps - 1)
            def _():
                for s in range(2):
                    for cp in stage_copies(s, 0):
                        cp.wait()

        def carried(h, r):
            blk = slice(r * QB, (r + 1) * QB)
            return ((blk, slice(h * HD, (h + 1) * HD)), (blk, slice(GW + h * HD, GW + (h + 1) * HD)),
                    (blk, slice(2 * GW + h * HD, 2 * GW + (h + 1) * HD)))

        if has_prev:
            @pl.when(n == 0)
            def _():
                carry[...] = jnp.zeros_like(carry)

            @pl.when(n == nsb)
            def _():
                for h in range(4):
                    for r in range(dil):
                        cq, ck, cv = carried(h, r)
                        finish(h, r, carry[cq], carry[ck], carry[cv])
                    finish_head(h)
                write_block(b * nsb + nsb - 1)

        @pl.when(n < nsb)
        def _():
            for r in range(dil):
                sts[r * QB:(r + 1) * QB, :] = st_ref[residue(r), :]
            refs_of = {"q": q_ref, "kc": kc_ref, "vc": vc_ref, "do": do_ref, "kp": kp_ref, "vp": vp_ref}
            for chunk in _unit_chunks(dil):
                heads = sorted({h for h, _ in chunk})
                rows_of = {h: {name: _residue_rows(refs_of[name], copies[name], h, residue)
                               for name in refs_of if refs_of[name] is not None}
                           for h in heads}

                def batch(name):
                    return jnp.stack([rows_of[h][name](r) for h, r in chunk])

                q, k, v, do = batch("q"), batch("kc"), batch("vc"), batch("do")
                if has_prev:
                    k = jnp.concatenate([batch("kp"), k], axis=1)
                    v = jnp.concatenate([batch("vp"), v], axis=1)
                    bias_b = jnp.stack([b_ref[h] for h, _ in chunk])
                else:
                    bias_b = jnp.stack([b_ref[h, :, QB:] for h, _ in chunk])
                lse = jnp.stack([sts[r * QB:(r + 1) * QB, h:h + 1] for h, r in chunk])
                delta = jnp.stack([sts[r * QB:(r + 1) * QB, 4 + h:5 + h] for h, r in chunk])
                s = jnp.einsum("uqd,ukd->uqk", q, k, preferred_element_type=F32) * SCALE + bias_b
                p = jnp.exp(s - lse)
                ds = p * (jnp.einsum("uqd,ukd->uqk", do, v, preferred_element_type=F32) - delta)
                for h in heads:
                    mine = [ds[i] for i, (hh, _) in enumerate(chunk) if hh == h]
                    tot = mine[0]
                    for extra in mine[1:]:
                        tot = tot + extra
                    if has_prev:
                        db_ref[h] += tot
                    else:
                        db_ref[h, :, QB:] += tot
                dsb, pb = ds.astype(BF16), p.astype(BF16)
                dq = jnp.einsum("uqk,ukd->uqd", dsb, k, preferred_element_type=F32) * SCALE
                dk = jnp.einsum("uqk,uqd->ukd", dsb, q, preferred_element_type=F32) * SCALE
                dv = jnp.einsum("uqk,uqd->ukd", pb, do, preferred_element_type=F32)
                for i, (h, r) in enumerate(chunk):
                    if has_prev:
                        cq, ck, cv = carried(h, r)
                        finish(h, r, carry[cq], carry[ck] + dk[i, :QB], carry[cv] + dv[i, :QB])
                        carry[cq] = dq[i]
                        carry[ck] = dk[i, QB:]
                        carry[cv] = dv[i, QB:]
                    else:
                        finish(h, r, dq[i], dk[i], dv[i])
                for h in heads:
                    finish_head(h)
            if has_prev:
                @pl.when(n > 0)
                def _():
                    write_block(b * nsb + n - 1)
            else:
                write_block(b)

    def row(b, n):
        return b * nsb + jnp.minimum(n, nsb - 1)

    def prev(b, n):
        return b * nsb + jnp.maximum(jnp.minimum(n, nsb - 1) - 1, 0)

    in_specs = [
        pl.BlockSpec((rows, GW), lambda b, n: (row(b, n), CB_Q + g)),
        pl.BlockSpec((rows, GW), lambda b, n: (row(b, n), CB_K + g)),
        pl.BlockSpec((rows, GW), lambda b, n: (row(b, n), CB_V + g)),
        pl.BlockSpec((rows, GW), lambda b, n: (row(b, n), 0)),
        pl.BlockSpec((rows, 128), lambda b, n: (row(b, n), 0)),
        pl.BlockSpec((None, None, 4, QB, 2 * QB),
                     lambda b, n: (g, jnp.minimum(jnp.minimum(n, nsb - 1), 1), 0, 0, 0)),
    ]
    args = [proj, proj, proj, d_out, stats, bias]
    scratch = [pltpu.VMEM((2, rows, GW), BF16)] * 3 + [pltpu.SemaphoreType.DMA((2, 3))]
    if has_prev:
        in_specs += [pl.BlockSpec((rows, GW), lambda b, n: (prev(b, n), CB_K + g)),
                     pl.BlockSpec((rows, GW), lambda b, n: (prev(b, n), CB_V + g))]
        args += [proj, proj]
        scratch.append(pltpu.VMEM((rows, 3 * GW), F32))
    n_copied = (7 + (2 if has_prev else 0)) * n_sets
    scratch += [pltpu.VMEM((rows, 128), F32)] * (1 + n_copied)
    in_specs.append(pl.BlockSpec(memory_space=pl.ANY))
    args.append(dproj)
    return pl.pallas_call(
        body, name=f"attn_bwd{g}",
        grid=(BL, n_steps),
        in_specs=in_specs,
        out_specs=(pl.BlockSpec(memory_space=pl.ANY),
                   pl.BlockSpec((4, QB, 2 * QB), lambda b, n: (0, 0, 0))),
        out_shape=(jax.ShapeDtypeStruct((T, NCOL), BF16),
                   jax.ShapeDtypeStruct((4, QB, 2 * QB), F32)),
        scratch_shapes=scratch,
        input_output_aliases={len(args) - 1: 0},
        compiler_params=pltpu.CompilerParams(vmem_limit_bytes=VMEM_LIMIT),
    )(*args)


def _tail(x2, tgt2, mod3, o_g, lse_g, proj, w_ao, w_co, w_o, conv_w, conv_b, ln_g, ln_b):
    tm = 256
    per_seq = S // tm
    halo = 16

    def chain(i, first, x_ref, t_ref, mod_ref, o1_ref, o2_ref, o3_ref, l1_ref, l2_ref, l3_ref,
              ga_ref, u_ref, bg_ref, cg_ref, gc_ref, ma_ref, mc_ref, up_ref, cp_ref,
              wao_ref, wco_ref, wo_ref, cw_ref, cb_ref, lg_ref, lb_ref,
              dyc_ref, do_ref, st_ref, dxd_ref, vec_ref,
              dga_s, dbg_s, dgm_s, acc_o, acc_co, acc_ao):
        tm = x_ref.shape[0]
        bidx = i // per_seq
        slot = i % 2

        l1, l2, l3 = l1_ref[...], l2_ref[...], l3_ref[...]
        mx = jnp.maximum(jnp.maximum(l1, l2), l3)
        e1, e2, e3 = jnp.exp(l1 - mx), jnp.exp(l2 - mx), jnp.exp(l3 - mx)
        esum = e1 + e2 + e3
        lse_tot = mx + jnp.log(esum)
        w1, w2, w3 = e1 / esum, e2 / esum, e3 / esum

        def per_head(wv):
            return jnp.concatenate([jnp.broadcast_to(wv[:, h:h + 1], (tm, HD)) for h in range(4)], axis=1)

        o = per_head(w1) * o1_ref[...] + per_head(w2) * o2_ref[...] + per_head(w3) * o3_ref[...]

        ga = ga_ref[...].astype(F32)
        sig_ga = _sigmoid(ga)
        silu_ga = ga * sig_ga
        a_in = (o * silu_ga).astype(BF16)
        a_out = _dot(a_in, wao_ref[...])

        u = u_ref[...].astype(F32)
        cg = cg_ref[...].astype(F32)
        z = cg * u
        zp = cp_ref[...].astype(F32) * up_ref[...].astype(F32)
        zp = jnp.where(first, 0.0, zp)
        zcat = jnp.concatenate([zp, z], axis=0)
        z1 = pltpu.roll(zcat, 1, 0)[halo:]
        z2 = pltpu.roll(zcat, 2, 0)[halo:]
        y_conv = cw_ref[0:1, :] * z2 + cw_ref[1:2, :] * z1 + cw_ref[2:3, :] * z + cb_ref[...]
        gc = gc_ref[...].astype(F32)
        sig_gc = _sigmoid(gc)
        silu_gc = gc * sig_gc
        bg = bg_ref[...].astype(F32)
        bg_yc = bg * y_conv
        s_in = (bg_yc * silu_gc).astype(BF16)
        s_out = _dot(s_in, wco_ref[...])

        sa = _sigmoid(ma_ref[...].astype(F32))
        sc = _sigmoid(mc_ref[...].astype(F32))
        merged = (sa * a_out + sc * s_out).astype(BF16)
        y = _dot(merged, wo_ref[...])
        gate1 = 1.0 + mod_ref[0, 2:3, :]
        xv = x_ref[...]
        resid = ALPHA * xv + gate1 * y
        mu = jnp.mean(resid, axis=1, keepdims=True)
        xc = resid - mu
        var = jnp.mean(xc * xc, axis=1, keepdims=True)
        rstd = lax.rsqrt(var + LN_EPS)
        xhat = xc * rstd
        lg = lg_ref[...]
        err = xhat * lg + lb_ref[...] - t_ref[...]
        vec_ref[3:4, :] += (0.5 / D) * jnp.sum(err * err, axis=0, keepdims=True)

        vec_ref[1:2, :] += (1.0 / D) * jnp.sum(err * xhat, axis=0, keepdims=True)
        vec_ref[2:3, :] += (1.0 / D) * jnp.sum(err, axis=0, keepdims=True)
        dxh = err * (lg * (1.0 / D))
        dres = rstd * (dxh - jnp.mean(dxh, axis=1, keepdims=True)
                       - xhat * jnp.mean(dxh * xhat, axis=1, keepdims=True))
        dxd_ref[...] = ALPHA * dres
        dgate = jnp.sum(dres * y, axis=0, keepdims=True)
        vec_ref[4:5, :] += jnp.where(bidx == 0, dgate, 0.0)
        vec_ref[5:6, :] += jnp.where(bidx == 1, dgate, 0.0)
        dy = (dres * gate1).astype(BF16)

        dmerged = _dot_nt(dy, wo_ref[...])
        da_out_f = dmerged * sa
        ds_out_f = dmerged * sc
        da_out = da_out_f.astype(BF16)
        ds_out = ds_out_f.astype(BF16)
        dgm_s[slot, :, 2 * D:3 * D] = (ds_out_f * s_out * (1.0 - sc)).astype(BF16)
        dgm_s[slot, :, D:2 * D] = (da_out_f * a_out * (1.0 - sa)).astype(BF16)
        da_in = _dot_nt(da_out, wao_ref[...])
        ds_in = _dot_nt(ds_out, wco_ref[...])

        d_o = da_in * silu_ga
        do_ref[...] = d_o.astype(BF16)
        dga_s[slot] = (da_in * o * (sig_ga + silu_ga * (1.0 - sig_ga))).astype(BF16)
        lane = lax.broadcasted_iota(jnp.int32, (tm, 128), 1)
        stats = lse_tot
        od = o * d_o
        for h in range(4):
            delta = jnp.sum(od[:, h * HD:(h + 1) * HD], axis=1, keepdims=True)
            stats = jnp.where(lane == 4 + h, delta, stats)
        st_ref[...] = stats

        ds_silu = ds_in * silu_gc
        dbg_s[slot] = (ds_silu * y_conv).astype(BF16)
        dyc = ds_silu * bg
        dyc_ref[...] = dyc
        vec_ref[0:1, :] += jnp.sum(dyc, axis=0, keepdims=True)
        dgm_s[slot, :, 0:D] = (ds_in * bg_yc * (sig_gc + silu_gc * (1.0 - sig_gc))).astype(BF16)

        acc_o[...] += _dot_tn(merged, dy)
        acc_co[...] += _dot_tn(s_in, ds_out)
        acc_ao[...] += _dot_tn(a_in, da_out)

    def body(x_ref, t_ref, mod_ref, o1_ref, o2_ref, o3_ref, l1_ref, l2_ref, l3_ref,
             ga_ref, u_ref, bg_ref, cg_ref, gc_ref, ma_ref, mc_ref, up_ref, cp_ref,
             wao_ref, wco_ref, wo_ref, cw_ref, cb_ref, lg_ref, lb_ref,
             dproj_ref, dyc_ref, do_ref, st_ref, dxd_ref,
             gwo_ref, gwco_ref, gwao_ref, vec_ref,
             dga_s, dbg_s, dgm_s, sems, acc_o, acc_co, acc_ao):
        i = pl.program_id(0)

        @pl.when(i == 0)
        def _():
            vec_ref[...] = jnp.zeros_like(vec_ref)
            acc_o[...] = jnp.zeros_like(acc_o)
            acc_co[...] = jnp.zeros_like(acc_co)
            acc_ao[...] = jnp.zeros_like(acc_ao)

        slot = i % 2

        def column_copies(s, row0):
            return _column_copies([(dga_s.at[s], CB * CB_GA), (dbg_s.at[s], D * KB_BG), (dgm_s.at[s], D * KB_GC)],
                                  dproj_ref, row0, sems.at[s])

        @pl.when(i >= 2)
        def _():
            for cp in column_copies(slot, 0):
                cp.wait()

        hm = tm // 2
        for part in range(2):
            rs = pl.ds(part * hm, hm)
            before = pl.ds(part * hm - halo, halo)
            rows = lambda ref: ref.at[rs]
            chain(i, ((i % per_seq) == 0) if part == 0 else False,
                  rows(x_ref), rows(t_ref), mod_ref, rows(o1_ref), rows(o2_ref), rows(o3_ref),
                  rows(l1_ref), rows(l2_ref), rows(l3_ref),
                  rows(ga_ref), rows(u_ref), rows(bg_ref), rows(cg_ref), rows(gc_ref), rows(ma_ref), rows(mc_ref),
                  up_ref if part == 0 else u_ref.at[before], cp_ref if part == 0 else cg_ref.at[before],
                  wao_ref, wco_ref, wo_ref, cw_ref, cb_ref, lg_ref, lb_ref,
                  rows(dyc_ref), rows(do_ref), rows(st_ref), rows(dxd_ref), vec_ref,
                  dga_s.at[:, rs], dbg_s.at[:, rs], dgm_s.at[:, rs], acc_o, acc_co, acc_ao)

        for cp in column_copies(slot, pl.multiple_of(i * tm, tm)):
            cp.start()

        @pl.when(i == T // tm - 1)
        def _():
            gwo_ref[...] = acc_o[...].astype(BF16)
            gwco_ref[...] = acc_co[...].astype(BF16)
            gwao_ref[...] = acc_ao[...].astype(BF16)
            for s in range(2):
                for cp in column_copies(s, 0):
                    cp.wait()

    def tile(width, cblk=0):
        return pl.BlockSpec((tm, width), lambda i: (i, cblk))

    def whole(shape):
        return pl.BlockSpec(shape, lambda i: tuple(0 for _ in shape))

    def once(shape):
        return pl.BlockSpec(shape, lambda i: tuple(0 for _ in shape), pipeline_mode=pl.Buffered(1))

    prev_rows = lambda i: (jnp.maximum(i * (tm // halo) - 1, 0),)
    in_specs = [
        tile(D), tile(D), pl.BlockSpec((1, 3, D), lambda i: (i // per_seq, 0, 0)),
        tile(GW), tile(GW), tile(GW), tile(128), tile(128), tile(128),
        tile(GW, CB_GA), tile(D, KB_U), tile(D, KB_BG), tile(D, KB_CG), tile(D, KB_GC),
        tile(D, KB_MA), tile(D, KB_MC),
        pl.BlockSpec((halo, D), lambda i: (*prev_rows(i), KB_U)),
        pl.BlockSpec((halo, D), lambda i: (*prev_rows(i), KB_CG)),
        whole((GW, D)), whole((D, D)), whole((D, D)),
        whole((3, D)), whole((1, D)), whole((1, D)), whole((1, D)),
    ]
    out_specs = (
        pl.BlockSpec(memory_space=pl.ANY), tile(D), tile(GW), tile(128), tile(D),
        once((D, D)), once((D, D)), once((GW, D)),
        pl.BlockSpec((8, D), lambda i: (0, 0)),
    )
    out_shape = (
        jax.ShapeDtypeStruct((T, NCOL), BF16),
        jax.ShapeDtypeStruct((T, D), F32),
        jax.ShapeDtypeStruct((T, GW), BF16),
        jax.ShapeDtypeStruct((T, 128), F32),
        jax.ShapeDtypeStruct((T, D), F32),
        jax.ShapeDtypeStruct((D, D), BF16),
        jax.ShapeDtypeStruct((D, D), BF16),
        jax.ShapeDtypeStruct((GW, D), BF16),
        jax.ShapeDtypeStruct((8, D), F32),
    )
    return pl.pallas_call(
        body, name="tail",
        grid=(T // tm,),
        in_specs=in_specs, out_specs=out_specs, out_shape=out_shape,
        scratch_shapes=[pltpu.VMEM((2, tm, GW), BF16), pltpu.VMEM((2, tm, D), BF16), pltpu.VMEM((2, tm, 3 * D), BF16),
                        pltpu.SemaphoreType.DMA((2, 3)),
                        pltpu.VMEM((D, D), F32), pltpu.VMEM((D, D), F32), pltpu.VMEM((GW, D), F32)],
        compiler_params=pltpu.CompilerParams(vmem_limit_bytes=VMEM_LIMIT_TAIL),
    )(x2, tgt2, mod3, *o_g, *lse_g, proj, proj, proj, proj, proj, proj, proj, proj, proj,
      w_ao, w_co, w_o, conv_w, conv_b, ln_g, ln_b)


def _conv_bwd(dyc, proj, conv_w, dproj):
    tm = 512
    per_seq = S // tm

    def body(d_ref, dn_ref, u_ref, c_ref, cw_ref, _, dproj_ref, g_ref, du_s, dc_s, sems):
        i = pl.program_id(0)
        last = (i % per_seq) == per_seq - 1

        @pl.when(i == 0)
        def _():
            g_ref[...] = jnp.zeros_like(g_ref)

        slot = i % 2

        def column_copies(s, row0):
            return _column_copies([(du_s.at[s], D * KB_U), (dc_s.at[s], D * KB_CG)], dproj_ref, row0, sems.at[s])

        @pl.when(i >= 2)
        def _():
            for cp in column_copies(slot, 0):
                cp.wait()

        d = d_ref[...]
        dn = jnp.where(last, 0.0, dn_ref[...])
        dcat = jnp.concatenate([d, dn], axis=0)
        d1 = pltpu.roll(dcat, tm + 8 - 1, 0)[:tm]
        d2 = pltpu.roll(dcat, tm + 8 - 2, 0)[:tm]
        dz = cw_ref[2:3, :] * d + cw_ref[1:2, :] * d1 + cw_ref[0:1, :] * d2
        u = u_ref[...].astype(F32)
        cg = c_ref[...].astype(F32)
        du_s[slot] = (dz * cg).astype(BF16)
        dc_s[slot] = (dz * u).astype(BF16)
        for cp in column_copies(slot, pl.multiple_of(i * tm, tm)):
            cp.start()

        z = cg * u
        g_ref[0:1, :] += jnp.sum(d2 * z, axis=0, keepdims=True)
        g_ref[1:2, :] += jnp.sum(d1 * z, axis=0, keepdims=True)
        g_ref[2:3, :] += jnp.sum(d * z, axis=0, keepdims=True)

        @pl.when(i == T // tm - 1)
        def _():
            for s in range(2):
                for cp in column_copies(s, 0):
                    cp.wait()

    n_tiles = T // tm
    next_rows = lambda i: jnp.minimum((i + 1) * (tm // 8), T // 8 - 1)
    return pl.pallas_call(
        body, name="conv_bwd",
        grid=(n_tiles,),
        in_specs=[pl.BlockSpec((tm, D), lambda i: (i, 0)),
                  pl.BlockSpec((8, D), lambda i: (next_rows(i), 0)),
                  pl.BlockSpec((tm, D), lambda i: (i, KB_U)),
                  pl.BlockSpec((tm, D), lambda i: (i, KB_CG)),
                  pl.BlockSpec((3, D), lambda i: (0, 0)),
                  pl.BlockSpec(memory_space=pl.ANY)],
        out_specs=(pl.BlockSpec(memory_space=pl.ANY),
                   pl.BlockSpec((8, D), lambda i: (0, 0))),
        out_shape=(jax.ShapeDtypeStruct((T, NCOL), BF16),
                   jax.ShapeDtypeStruct((8, D), F32)),
        scratch_shapes=[pltpu.VMEM((2, tm, D), BF16), pltpu.VMEM((2, tm, D), BF16), pltpu.SemaphoreType.DMA((2, 2))],
        input_output_aliases={5: 0},
        compiler_params=pltpu.CompilerParams(vmem_limit_bytes=VMEM_LIMIT),
    )(dyc, dyc, proj, proj, conv_w, dproj)


def _dh_dx(dproj, w_in_all, x2, dxd, mod3, chip_sums, hops=(), parts0=None):
    tm = 512
    per_seq = S // tm
    n_pass, _, width = w_in_all.shape
    n = len(chip_sums)
    n_in = 5 + n + (0 if parts0 is None else 1)

    def body(*refs):
        d_ref, w_ref, x_ref, dxd_ref, mod_ref = refs[:5]
        ins = refs[5:5 + n]
        gx_ref, vec_ref = refs[n_in:n_in + 2]
        outs = refs[n_in + 2:n_in + 2 + n]
        acc, send_sems, recv_sems, local_sems = refs[n_in + 2 + n:]
        jj, i = pl.program_id(0), pl.program_id(1)

        @pl.when((i == 0) & (jj == 0))
        def _():
            vec_ref[...] = jnp.zeros_like(vec_ref)
            if n:
                sends, _, mine = _chip_copies(ins, outs, send_sems, recv_sems, local_sems, hops)
                for cp in sends + mine:
                    cp.start()

        if n:
            @pl.when((i == T // tm - 1) & (jj == n_pass - 1))
            def _():
                sends, arrivals, mine = _chip_copies(ins, outs, send_sems, recv_sems, local_sems, hops)
                for cp in arrivals:
                    cp.wait_recv()
                for cp in sends:
                    cp.wait_send()
                for cp in mine:
                    cp.wait()

        def partial():
            return _dot_nt(d_ref[...], w_ref[...])

        @pl.when(jj == 0)
        def _():
            acc[i] = partial()

        @pl.when((jj > 0) & (jj < n_pass - 1))
        def _():
            acc[i] += partial()

        @pl.when(jj == n_pass - 1)
        def _():
            dh = acc[i] + partial()
            bidx = i // per_seq
            gx_ref[...] = dxd_ref[...] + dh * (1.0 + mod_ref[0, 1:2, :])
            dshift = jnp.sum(dh, axis=0, keepdims=True)
            dscale = jnp.sum(dh * x_ref[...], axis=0, keepdims=True)
            vec_ref[0:1, :] += jnp.where(bidx == 0, dshift, 0.0)
            vec_ref[1:2, :] += jnp.where(bidx == 1, dshift, 0.0)
            vec_ref[2:3, :] += jnp.where(bidx == 0, dscale, 0.0)
            vec_ref[3:4, :] += jnp.where(bidx == 1, dscale, 0.0)

    def last_pass(jj, i):
        return jnp.where(jj == n_pass - 1, i, 0)

    any_spec = pl.BlockSpec(memory_space=pl.ANY)
    res = pl.pallas_call(
        body, name="dh_dx",
        grid=(n_pass, T // tm),
        in_specs=[
            pl.BlockSpec((tm, width), lambda jj, i: (i, jj)),
            pl.BlockSpec((None, D, width), lambda jj, i: (jj, 0, 0)),
            pl.BlockSpec((tm, D), lambda jj, i: (last_pass(jj, i), 0)),
            pl.BlockSpec((tm, D), lambda jj, i: (last_pass(jj, i), 0)),
            pl.BlockSpec((1, 3, D), lambda jj, i: (last_pass(jj, i) // per_seq, 0, 0))]
                 + [any_spec] * (n_in - 5),
        out_specs=(pl.BlockSpec((tm, D), lambda jj, i: (last_pass(jj, i), 0)),
                   pl.BlockSpec((8, D), lambda jj, i: (0, 0))) + (any_spec,) * n,
        out_shape=(jax.ShapeDtypeStruct((T, D), F32), jax.ShapeDtypeStruct((8, D), F32))
                  + tuple(jax.ShapeDtypeStruct(a.shape, a.dtype) for a in chip_sums),
        scratch_shapes=[pltpu.VMEM((T // tm, tm, D), F32), pltpu.SemaphoreType.DMA((max(3 * n, 1),)),
                        pltpu.SemaphoreType.DMA((max(3 * n, 1),)), pltpu.SemaphoreType.DMA((max(n, 1),))],
        input_output_aliases={} if parts0 is None else {5 + n: 2},
        compiler_params=pltpu.CompilerParams(vmem_limit_bytes=VMEM_LIMIT),
    )(dproj, w_in_all, x2, dxd, mod3, *chip_sums, *([] if parts0 is None else [parts0]))
    return res[0], res[1], res[2:]


def _adam_step(g, w, m, v):
    nm = ADAM_B1 * m + (1.0 - ADAM_B1) * g
    nv = ADAM_B2 * v + (1.0 - ADAM_B2) * (g * g)
    m_hat = nm / (1.0 - ADAM_B1 ** ADAM_STEP)
    v_hat = nv / (1.0 - ADAM_B2 ** ADAM_STEP)
    return -ADAM_LR * (m_hat / (jnp.sqrt(v_hat) + ADAM_EPS) + ADAM_WD * w), nm, nv


def _adamw(parts, w, m, v, name, row_tile=None):
    n_parts, rows, cols = parts.shape
    tr = rows if row_tile is None else row_tile

    def body(p_ref, w_ref, m_ref, v_ref, g_ref, d_ref, nm_ref, nv_ref):
        g = p_ref[0].astype(F32)
        for s in range(1, n_parts):
            g = g + p_ref[s].astype(F32)
        g_ref[...] = g
        d_ref[...], nm_ref[...], nv_ref[...] = _adam_step(g, w_ref[...], m_ref[...], v_ref[...])

    blk = pl.BlockSpec((tr, cols), lambda i: (i, 0))
    shp = jax.ShapeDtypeStruct((rows, cols), F32)
    return pl.pallas_call(
        body, name=name,
        grid=(rows // tr,),
        in_specs=[pl.BlockSpec((n_parts, tr, cols), lambda i: (0, i, 0)), blk, blk, blk],
        out_specs=(blk, blk, blk, blk),
        out_shape=(shp, shp, shp, shp),
        compiler_params=pltpu.CompilerParams(vmem_limit_bytes=VMEM_LIMIT),
    )(parts, w, m, v)


def _multi_adamw(parts_list, params, name):
    n = len(params)
    flat = [t for wmv in params for t in wmv]

    def body(*refs):
        parts, ins, outs = refs[:n], refs[n:4 * n], refs[4 * n:]
        for p in range(n):
            g = parts[p][0].astype(F32)
            for s in range(1, parts[p].shape[0]):
                g = g + parts[p][s].astype(F32)
            w_ref, m_ref, v_ref = ins[3 * p:3 * p + 3]
            g_ref, d_ref, nm_ref, nv_ref = outs[4 * p:4 * p + 4]
            g_ref[...] = g
            d_ref[...], nm_ref[...], nv_ref[...] = _adam_step(g, w_ref[...], m_ref[...], v_ref[...])

    out_shape = []
    for w, _, _ in params:
        out_shape += [jax.ShapeDtypeStruct(w.shape, F32)] * 4
    res = pl.pallas_call(body, name=name, out_shape=tuple(out_shape))(*parts_list, *flat)
    return [res[4 * p:4 * p + 4] for p in range(n)]


def _small_updates(small_g, dmod_all, rel_parts, params):
    flat = [t for wmv in params for t in wmv]

    def body(sg_ref, dm_ref, rp_ref, *refs):
        ins, outs = refs[:len(flat)], refs[len(flat):]

        def over_devices(row):
            tot = sg_ref[0, row:row + 1, :]
            for s in range(1, N_DEV):
                tot = tot + sg_ref[s, row:row + 1, :]
            return tot

        g_b_ada = dm_ref[0:1, :]
        for r in range(1, N_DEV * BL):
            g_b_ada = g_b_ada + dm_ref[r:r + 1, :]
        g_rel = rp_ref[0]
        for s in range(1, N_DEV):
            g_rel = g_rel + rp_ref[s]
        grads = [g_b_ada, over_devices(0), g_rel, over_devices(1), over_devices(2)]
        outs[0][...] = jnp.sum(over_devices(3), axis=1, keepdims=True)
        for p, g in enumerate(grads):
            w_ref, m_ref, v_ref = ins[3 * p:3 * p + 3]
            g_ref, d_ref, nm_ref, nv_ref = outs[1 + 4 * p:5 + 4 * p]
            g_ref[...] = g
            d_ref[...], nm_ref[...], nv_ref[...] = _adam_step(g, w_ref[...], m_ref[...], v_ref[...])

    out_shape = [jax.ShapeDtypeStruct((1, 1), F32)]
    for w, _, _ in params:
        out_shape += [jax.ShapeDtypeStruct(w.shape, F32)] * 4
    res = pl.pallas_call(body, name="small_updates", out_shape=tuple(out_shape))(small_g, dmod_all, rel_parts, *flat)
    return res[0], [res[1 + 4 * p:5 + 4 * p] for p in range(len(params))]


def _attn_fwd_dense(proj, bias):
    nq = 4
    rows = nq * QB
    nsb = S // rows

    def body(q_ref, k_ref, v_ref, kp_ref, vp_ref, b_ref, o_ref, l_ref, ls0, ls1, ls2, ls3):
        ls = [ls0, ls1, ls2, ls3]
        n = pl.program_id(1)
        lane = lax.broadcasted_iota(jnp.int32, (QB, 128), 1)
        units = [(h, j) for h in range(4) for j in range(nq)]

        def keys(cur_ref, prev_ref, h, j):
            sl = slice(h * HD, (h + 1) * HD)
            if j == 0:
                return jnp.concatenate([prev_ref[:, sl], cur_ref[0:QB, sl]], axis=0)
            return cur_ref[(j - 1) * QB:(j + 1) * QB, sl]

        q = jnp.stack([q_ref[j * QB:(j + 1) * QB, h * HD:(h + 1) * HD] for h, j in units])
        k = jnp.stack([keys(k_ref, kp_ref, h, j) for h, j in units])
        v = jnp.stack([keys(v_ref, vp_ref, h, j) for h, j in units])
        bias_b = jnp.stack([b_ref[jnp.minimum(n, 1), h] if j == 0 else b_ref[1, h] for h, j in units])
        s = jnp.einsum("uqd,ukd->uqk", q, k, preferred_element_type=F32) * SCALE + bias_b
        m = jnp.max(s, axis=-1, keepdims=True)
        p = jnp.exp(s - m)
        l = jnp.sum(p, axis=-1, keepdims=True)
        o = jnp.einsum("uqk,ukd->uqd", p.astype(BF16), v, preferred_element_type=F32) / l
        lse = m + jnp.log(l)
        for i, (h, j) in enumerate(units):
            o_ref[j * QB:(j + 1) * QB, h * HD:(h + 1) * HD] = o[i]
            ls[h][j * QB:(j + 1) * QB, :] = jnp.where(lane == h, lse[i], 0.0)
        l_ref[...] = (ls[0][...] + ls[1][...]) + (ls[2][...] + ls[3][...])

    def row(b, n):
        return b * nsb + n

    def prev(b, n):
        return jnp.maximum((b * nsb + n) * nq - 1, 0)

    in_specs = [
        pl.BlockSpec((rows, GW), lambda b, n: (row(b, n), CB_Q)),
        pl.BlockSpec((rows, GW), lambda b, n: (row(b, n), CB_K)),
        pl.BlockSpec((rows, GW), lambda b, n: (row(b, n), CB_V)),
        pl.BlockSpec((QB, GW), lambda b, n: (prev(b, n), CB_K)),
        pl.BlockSpec((QB, GW), lambda b, n: (prev(b, n), CB_V)),
        pl.BlockSpec((None, 2, 4, QB, 2 * QB), lambda b, n: (0, 0, 0, 0, 0)),
    ]
    return pl.pallas_call(
        body, name="attn_fwd0",
        grid=(BL, nsb),
        in_specs=in_specs,
        out_specs=(pl.BlockSpec((rows, GW), lambda b, n: (row(b, n), 0)),
                   pl.BlockSpec((rows, 128), lambda b, n: (row(b, n), 0))),
        out_shape=(jax.ShapeDtypeStruct((T, GW), F32), jax.ShapeDtypeStruct((T, 128), F32)),
        scratch_shapes=[pltpu.VMEM((rows, 128), F32)] * 4,
        compiler_params=pltpu.CompilerParams(vmem_limit_bytes=VMEM_LIMIT),
    )(proj, proj, proj, proj, proj, bias)


def _attn_bwd_dense(proj, d_out, stats, bias, dproj):
    nq = 4
    rows = nq * QB
    nsb = S // rows
    cols_q, cols_k, cols_v = CB * CB_Q, CB * CB_K, CB * CB_V

    def body(q_ref, k_ref, v_ref, do_ref, st_ref, kp_ref, vp_ref, b_ref, _, out_ref, db_ref,
             sq, sk, sv, carry, sems):
        b, n = pl.program_id(0), pl.program_id(1)
        units = [(h, j) for h in range(4) for j in range(nq)]

        @pl.when((b == 0) & (n == 0))
        def _():
            db_ref[...] = jnp.zeros_like(db_ref)

        @pl.when(n == 0)
        def _():
            carry[...] = jnp.zeros_like(carry)

        step = b * (nsb + 1) + n
        slot = step % 2

        def block_copies(s, position, block):
            part = pl.ds(position * QB, QB)
            return _column_copies([(sq.at[s, part], cols_q), (sk.at[s, part], cols_k), (sv.at[s, part], cols_v)],
                                  out_ref, pl.multiple_of(block * QB, QB), sems.at[s, position])

        def wait_blocks(s, positions):
            for position in positions:
                for cp in block_copies(s, position, 0):
                    cp.wait()

        before = (step - 2) % (nsb + 1)

        @pl.when((step >= 2) & (before > 0))
        def _():
            wait_blocks(slot, [0])

        @pl.when((step >= 2) & (before < nsb))
        def _():
            wait_blocks(slot, range(1, nq))

        def write(first_block, position, count):
            for j in range(count):
                for cp in block_copies(slot, position + j, first_block + j):
                    cp.start()

        @pl.when(n == nsb)
        def _():
            sq[slot, 0:QB, :] = carry[:, 0:GW].astype(BF16)
            sk[slot, 0:QB, :] = carry[:, GW:2 * GW].astype(BF16)
            sv[slot, 0:QB, :] = carry[:, 2 * GW:3 * GW].astype(BF16)
            write((b + 1) * nsb * nq - 1, 0, 1)

            @pl.when(b == BL - 1)
            def _():
                wait_blocks(slot, [0])
                wait_blocks(1 - slot, range(nq))

        @pl.when(n < nsb)
        def _():
            def keys(cur_ref, prev_ref, h, j):
                sl = slice(h * HD, (h + 1) * HD)
                if j == 0:
                    return jnp.concatenate([prev_ref[:, sl], cur_ref[0:QB, sl]], axis=0)
                return cur_ref[(j - 1) * QB:(j + 1) * QB, sl]

            def block(ref, h, j):
                return ref[j * QB:(j + 1) * QB, h * HD:(h + 1) * HD]

            q = jnp.stack([block(q_ref, h, j) for h, j in units])
            do = jnp.stack([block(do_ref, h, j) for h, j in units])
            k = jnp.stack([keys(k_ref, kp_ref, h, j) for h, j in units])
            v = jnp.stack([keys(v_ref, vp_ref, h, j) for h, j in units])
            bias_b = jnp.stack([b_ref[jnp.minimum(n, 1), h] if j == 0 else b_ref[1, h] for h, j in units])
            lse = jnp.stack([st_ref[j * QB:(j + 1) * QB, h:h + 1] for h, j in units])
            delta = jnp.stack([st_ref[j * QB:(j + 1) * QB, 4 + h:5 + h] for h, j in units])
            s = jnp.einsum("uqd,ukd->uqk", q, k, preferred_element_type=F32) * SCALE + bias_b
            p = jnp.exp(s - lse)
            ds = p * (jnp.einsum("uqd,ukd->uqk", do, v, preferred_element_type=F32) - delta)
            for h in range(4):
                tot = ds[h * nq]
                for j in range(1, nq):
                    tot = tot + ds[h * nq + j]
                db_ref[h] += tot
            dsb, pb = ds.astype(BF16), p.astype(BF16)
            dq = jnp.einsum("uqk,ukd->uqd", dsb, k, preferred_element_type=F32) * SCALE
            dk = jnp.einsum("uqk,uqd->ukd", dsb, q, preferred_element_type=F32) * SCALE
            dv = jnp.einsum("uqk,uqd->ukd", pb, do, preferred_element_type=F32)
            for h in range(4):
                sl = slice(h * HD, (h + 1) * HD)
                u0, last = h * nq, h * nq + nq - 1
                sq[slot, 0:QB, sl] = carry[:, sl].astype(BF16)
                sk[slot, 0:QB, sl] = (carry[:, GW + h * HD:GW + (h + 1) * HD] + dk[u0, :QB]).astype(BF16)
                sv[slot, 0:QB, sl] = (carry[:, 2 * GW + h * HD:2 * GW + (h + 1) * HD] + dv[u0, :QB]).astype(BF16)
                for j in range(nq - 1):
                    pos = slice((j + 1) * QB, (j + 2) * QB)
                    sq[slot, pos, sl] = dq[u0 + j].astype(BF16)
                    sk[slot, pos, sl] = (dk[u0 + j, QB:] + dk[u0 + j + 1, :QB]).astype(BF16)
                    sv[slot, pos, sl] = (dv[u0 + j, QB:] + dv[u0 + j + 1, :QB]).astype(BF16)
                carry[:, sl] = dq[last]
                carry[:, GW + h * HD:GW + (h + 1) * HD] = dk[last, QB:]
                carry[:, 2 * GW + h * HD:2 * GW + (h + 1) * HD] = dv[last, QB:]

            @pl.when(n == 0)
            def _():
                write(b * nsb * nq, 1, nq - 1)

            @pl.when(n > 0)
            def _():
                write((b * nsb + n) * nq - 1, 0, nq)

    def row(b, n):
        return b * nsb + jnp.minimum(n, nsb - 1)

    def prev(b, n):
        return jnp.maximum(row(b, n) * nq - 1, 0)

    in_specs = [
        pl.BlockSpec((rows, GW), lambda b, n: (row(b, n), CB_Q)),
        pl.BlockSpec((rows, GW), lambda b, n: (row(b, n), CB_K)),
        pl.BlockSpec((rows, GW), lambda b, n: (row(b, n), CB_V)),
        pl.BlockSpec((rows, GW), lambda b, n: (row(b, n), 0)),
        pl.BlockSpec((rows, 128), lambda b, n: (row(b, n), 0)),
        pl.BlockSpec((QB, GW), lambda b, n: (prev(b, n), CB_K)),
        pl.BlockSpec((QB, GW), lambda b, n: (prev(b, n), CB_V)),
        pl.BlockSpec((None, 2, 4, QB, 2 * QB), lambda b, n: (0, 0, 0, 0, 0)),
        pl.BlockSpec(memory_space=pl.ANY),
    ]
    return pl.pallas_call(
        body, name="attn_bwd0",
        grid=(BL, nsb + 1),
        in_specs=in_specs,
        out_specs=(pl.BlockSpec(memory_space=pl.ANY),
                   pl.BlockSpec((4, QB, 2 * QB), lambda b, n: (0, 0, 0))),
        out_shape=(jax.ShapeDtypeStruct((T, NCOL), BF16),
                   jax.ShapeDtypeStruct((4, QB, 2 * QB), F32)),
        scratch_shapes=[pltpu.VMEM((2, rows, GW), BF16)] * 3
                       + [pltpu.VMEM((QB, 3 * GW), F32), pltpu.SemaphoreType.DMA((2, nq, 3))],
        input_output_aliases={8: 0},
        compiler_params=pltpu.CompilerParams(vmem_limit_bytes=VMEM_LIMIT),
    )(proj, proj, proj, d_out, stats, proj, proj, bias, dproj)


def _attention_forward(proj, rel_bias):
    expand_lanes, grad_lanes, masks = (jnp.asarray(t) for t in _bucket_maps())
    bias = _bias_expand(rel_bias, expand_lanes, masks)
    fwd = [_attn_fwd_dense(proj, bias)] + [_attn_fwd(proj, bias, g) for g in (1, 2)]
    return bias, grad_lanes, [f[0] for f in fwd], [f[1] for f in fwd]


def _local_step(x2, tgt2, mod3, h, proj, attn, w_ao, w_co, w_o, conv_w, conv_b, ln_g, ln_b):
    bias, buckets, o_g, lse_g = attn

    (dproj, dyc, d_o, stats, dxd, gw_o, gw_co, gw_ao, tail_vec) = _tail(
        x2, tgt2, mod3, o_g, lse_g, proj, w_ao, w_co, w_o, conv_w, conv_b, ln_g, ln_b)

    dproj, db = _attn_bwd_dense(proj, d_o, stats, bias, dproj)
    dbias = [db]
    for g in (1, 2):
        dproj, db = _attn_bwd(proj, d_o, stats, bias, dproj, g)
        dbias.append(db)
    g_rel_bias = _bias_grad(*dbias, buckets)
    dproj, conv_vec = _conv_bwd(dyc, proj, conv_w, dproj)

    gw_ao = jnp.transpose(gw_ao.reshape(GW, N_DEV, D // N_DEV), (1, 0, 2))
    return dproj, dxd, gw_ao, gw_co, gw_o, conv_vec, g_rel_bias, tail_vec


def kernel(x, c, w_ada, b_ada, w_in, conv_w, conv_b, rel_bias, w_attn_out, w_conv_out, w_o, ln_g, ln_b, loss_target, m_w_ada, m_b_ada, m_w_in, m_conv_w, m_conv_b, m_rel_bias, m_w_attn_out, m_w_conv_out, m_w_o, m_ln_g, m_ln_b, v_w_ada, v_b_ada, v_w_in, v_conv_w, v_conv_b, v_rel_bias, v_w_attn_out, v_w_conv_out, v_w_o, v_ln_g, v_ln_b):
    me = _my_index()
    x2 = x.reshape(T, D)
    tgt2 = loss_target.reshape(T, D)

    b_cols = lax.dynamic_slice(b_ada, (0, me * ADA_SHARD), (1, ADA_SHARD))
    c_g, mod_in = _mod_exchange(jnp.pad(c, ((0, 8 - BL), (0, 0))), w_ada[0], b_cols)
    c_all = c_g[:, 0:BL, :].reshape(N_DEV * BL, D)
    mod3 = jnp.transpose(mod_in[:, 0:BL, :], (1, 0, 2)).reshape(BL, 3, D)

    rows_shape = jax.ShapeDtypeStruct((N_DEV, D // N_DEV, D), BF16)
    proj, h, w_in_all, (w_ao_g, w_co_g, w_o_g, conv_w_g) = _gather_proj(
        _shard_order(), x2, mod3, w_in[0].astype(BF16), 1024,
        ([w_attn_out[0].astype(BF16), w_conv_out[0].astype(BF16), w_o[0].astype(BF16), conv_w[0]],
         [jax.ShapeDtypeStruct((N_DEV, GW, D // N_DEV), BF16), rows_shape, rows_shape,
          jax.ShapeDtypeStruct((N_DEV, 3, D // N_DEV), F32)]))

    attn = _attention_forward(proj, rel_bias)
    w_ao_full = jnp.transpose(w_ao_g, (1, 0, 2)).reshape(GW, D)
    w_co_full = w_co_g.reshape(D, D)
    w_o_full = w_o_g.reshape(D, D)
    conv_w_full = jnp.transpose(conv_w_g, (1, 0, 2)).reshape(3, D)

    (dproj, dxd, gw_ao, gw_co, gw_o, conv_vec, g_rel_bias, tail_vec) = _local_step(
        x2, tgt2, mod3, h, proj, attn, w_ao_full, w_co_full, w_o_full,
        conv_w_full, conv_b, ln_g, ln_b)

    g_conv_w_blocks = jnp.transpose(conv_vec[0:3].reshape(3, N_DEV, D // N_DEV), (1, 0, 2))
    partials = [gw_ao, gw_co.reshape(N_DEV, D // N_DEV, D), gw_o.reshape(N_DEV, D // N_DEV, D), g_conv_w_blocks]
    w_in_sums, w_in_parts, sib = _gw_in_pair(
        _slice_order(), h, dproj, partials,
        [jax.ShapeDtypeStruct((4, GW, D // N_DEV), BF16),
         jax.ShapeDtypeStruct((4, D // N_DEV, D), BF16),
         jax.ShapeDtypeStruct((4, D // N_DEV, D), BF16),
         jax.ShapeDtypeStruct((4, 3, D // N_DEV), F32)])
    core = lax.axis_index("c").astype(jnp.int32).reshape(1)
    chip_sums = [w_in_sums] + list(_pair_add(core, partials, sib))
    hops = [(3,)] + [(1, 2, 3)] * 4
    grad_x, mod_vec, (r_in, r_ao, r_co, r_o, r_cw) = _dh_dx(
        dproj, w_in_all, x2, dxd, mod3, chip_sums, hops, w_in_parts)

    small = jnp.concatenate([
        tail_vec[0:4],
        jnp.pad(g_rel_bias.reshape(1, N_BUCKETS * N_HEADS), ((0, 0), (0, D - N_BUCKETS * N_HEADS))),
        jnp.zeros((3, D), F32)], axis=0)
    dmod = jnp.concatenate([mod_vec[0:2], mod_vec[2:4], tail_vec[4:6]], axis=1)
    small_g, dmod_g = _all_gather(
        [small, dmod],
        [jax.ShapeDtypeStruct((N_DEV, 8, D), F32), jax.ShapeDtypeStruct((N_DEV, BL, 3 * D), F32)],
        "gather_small")
    dmod_all = dmod_g.reshape(N_DEV * BL, 3 * D)
    small_names = ["b_ada", "conv_b", "rel_bias", "ln_g", "ln_b"]
    small_params = [(b_ada, m_b_ada, v_b_ada), (conv_b, m_conv_b, v_conv_b), (rel_bias, m_rel_bias, v_rel_bias),
                    (ln_g, m_ln_g, v_ln_g), (ln_b, m_ln_b, v_ln_b)]
    loss, small_res = _small_updates(
        small_g, dmod_all, small_g[:, 4, :N_BUCKETS * N_HEADS].reshape(N_DEV, N_BUCKETS, N_HEADS), small_params)
    loss = loss.reshape(())

    dmod_cols = lax.dynamic_slice(dmod_all, (0, me * ADA_SHARD), (N_DEV * BL, ADA_SHARD))
    res = {
        "w_ada": tuple(t[None] for t in _w_ada_update(jnp.transpose(c_all), dmod_cols,
                                                      w_ada[0], m_w_ada[0], v_w_ada[0])),
        "w_in": tuple(t[None] for t in _adamw(r_in, w_in[0], m_w_in[0], v_w_in[0], "adam_w_in", 128)),
    }
    mid_names = ["conv_w", "w_attn_out", "w_conv_out", "w_o"]
    mid_parts = [r_cw, r_ao, r_co, r_o]
    mid_full = [(conv_w, m_conv_w, v_conv_w), (w_attn_out, m_w_attn_out, v_w_attn_out),
                (w_conv_out, m_w_conv_out, v_w_conv_out), (w_o, m_w_o, v_w_o)]
    mid_res = _multi_adamw(mid_parts, [tuple(t[0] for t in wmv) for wmv in mid_full], "adam_mid")
    for nm, wmv, outs4 in zip(mid_names, mid_full, mid_res):
        res[nm] = tuple(t[None] for t in outs4)
    res.update(dict(zip(small_names, small_res)))
    order = ["w_ada", "b_ada", "w_in", "conv_w", "conv_b", "rel_bias", "w_attn_out", "w_conv_out",
             "w_o", "ln_g", "ln_b"]
    outs = [loss, grad_x.reshape(BL, S, D)]
    for k in range(4):
        outs += [res[name][k] for name in order]
    return tuple(outs)
```

```python
import math

import numpy as np
import jax
import jax.numpy as jnp
from jax import lax
from jax.experimental import pallas as pl
from jax.experimental.pallas import tpu as pltpu

F32 = jnp.float32
BF16 = jnp.bfloat16
MESH = pl.DeviceIdType.MESH

N_DEV = 8
D = 1024
S = 2048
BL = 2
T = BL * S
NCOL = 11264
SHARD = NCOL // N_DEV
CB = 512
NCB = NCOL // CB
HD = 128
GW = 512
QB = 128
DILATIONS = (1, 4, 16)
N_STEPS = 128
N_BUCKETS = 32
N_HEADS = 12
ALPHA = 2.0 ** 0.25
LN_EPS = 1e-5
NEG_INF = -1e30
SCALE = HD ** -0.5
ADA_SHARD = 3 * D // N_DEV

CB_Q, CB_K, CB_V, CB_GA = 0, 3, 6, 9
KB_U, KB_BG, KB_CG, KB_GC, KB_MA, KB_MC = 5, 6, 7, 8, 9, 10

ADAM_LR, ADAM_B1, ADAM_B2, ADAM_EPS, ADAM_WD, ADAM_STEP = 0.001, 0.9, 0.999, 1e-08, 0.01, 10

VMEM_LIMIT = 56 * 1024 * 1024
VMEM_LIMIT_TAIL = 62 * 1024 * 1024


def _dot(a, b):
    return jnp.dot(a, b, preferred_element_type=F32)


def _dot_nt(a, b):
    return lax.dot_general(a, b, (((1,), (1,)), ((), ())), preferred_element_type=F32)


def _dot_tn(a, b):
    return lax.dot_general(a, b, (((0,), (0,)), ((), ())), preferred_element_type=F32)


def _sigmoid(v):
    return 1.0 / (1.0 + jnp.exp(-v))


def _column_copies(pieces, dst_hbm, row0, sems):
    copies = []
    for k, (src, col0) in enumerate(pieces):
        rows, width = src.shape
        copies.append(pltpu.make_async_copy(
            src, dst_hbm.at[pl.ds(row0, rows), pl.ds(col0, width)], sems.at[k]))
    return copies


def _my_index():
    return 4 * lax.axis_index("x") + 2 * lax.axis_index("y") + lax.axis_index("c")


class _Gather:
    def __init__(self, ins, outs, stage, send_sems, recv_sems, local_sems):
        self.ins, self.outs, self.stage = ins, outs, stage
        self.send_sems, self.recv_sems, self.local_sems = send_sems, recv_sems, local_sems
        x, y, c = lax.axis_index("x"), lax.axis_index("y"), lax.axis_index("c")
        self.c = c
        self.me, self.sibling = (x, y, c), (x, y, 1 - c)
        self.chips = [(1 - x, y), (x, 1 - y), (1 - x, 1 - y)]

    @staticmethod
    def scratch(arrs):
        n = len(arrs)
        return ([pltpu.SemaphoreType.DMA((7 * n,)), pltpu.SemaphoreType.DMA((7 * n,)),
                 pltpu.SemaphoreType.DMA((n,))] + [pltpu.VMEM(a.shape, a.dtype) for a in arrs])

    def _copy(self, a, k, block, to, src=None):
        dst = self.outs[a].at[4 * block[0] + 2 * block[1] + block[2]]
        return pltpu.make_async_remote_copy(
            src_ref=dst if src is None else src, dst_ref=dst,
            send_sem=self.send_sems.at[a * 7 + k], recv_sem=self.recv_sems.at[a * 7 + k],
            device_id=to, device_id_type=MESH)

    def _first(self):
        first = []
        for a in range(len(self.ins)):
            first.append(self._copy(a, 0, self.me, self.sibling, src=self.ins[a]))
            first += [self._copy(a, 1 + j, self.me, (*chip, self.c), src=self.ins[a])
                      for j, chip in enumerate(self.chips)]
        return first

    def _mine(self):
        me = self.me
        return [pltpu.make_async_copy(self.stage[a], self.outs[a].at[4 * me[0] + 2 * me[1] + me[2]],
                                      self.local_sems.at[a]) for a in range(len(self.ins))]

    def begin(self):
        for cp in self._first():
            cp.start()
        loads = [pltpu.make_async_copy(self.ins[a], self.stage[a], self.local_sems.at[a])
                 for a in range(len(self.ins))]
        for cp in loads:
            cp.start()
        for cp in loads:
            cp.wait()
        for cp in self._mine():
            cp.start()

    def finish(self):
        n, c, me, sibling = len(self.ins), self.c, self.me, self.sibling
        passed = []
        for j, chip in enumerate(self.chips):
            for a in range(n):
                self._copy(a, 1 + j, (*chip, c), me).wait_recv()
                fwd = self._copy(a, 4 + j, (*chip, c), sibling)
                fwd.start()
                passed.append(fwd)
        for a in range(n):
            self._copy(a, 0, sibling, me).wait_recv()
        for j, chip in enumerate(self.chips):
            for a in range(n):
                self._copy(a, 4 + j, (*chip, 1 - c), me).wait_recv()
        for cp in self._first() + passed:
            cp.wait_send()
        for cp in self._mine():
            cp.wait()


def _all_gather(arrs, out_shapes, name):
    n = len(arrs)

    def body(*refs):
        g = _Gather(refs[:n], refs[n:2 * n], refs[2 * n + 3:], *refs[2 * n:2 * n + 3])
        g.begin()
        g.finish()

    any_spec = pl.BlockSpec(memory_space=pl.ANY)
    return pl.pallas_call(
        body, name=name,
        out_shape=tuple(out_shapes),
        in_specs=[any_spec] * n,
        out_specs=tuple([any_spec] * n),
        scratch_shapes=_Gather.scratch(arrs),
    )(*arrs)


def _neighbour_chips():
    x, y, c = lax.axis_index("x"), lax.axis_index("y"), lax.axis_index("c")
    first = (jnp.where(c == 0, 1 - x, x), jnp.where(c == 0, y, 1 - y))
    second = (jnp.where(c == 0, x, 1 - x), jnp.where(c == 0, 1 - y, y))
    return first, second, (1 - x, 1 - y)


def _slice_order():
    x, y, c = lax.axis_index("x"), lax.axis_index("y"), lax.axis_index("c")
    nb1, nb2, diag = _neighbour_chips()
    slots = []
    for mine, theirs in ((nb1, nb2), (nb2, nb1), (diag, diag), ((x, y), (x, y))):
        slots += [2 * (2 * theirs[0] + theirs[1]) + 1 - c, 2 * (2 * mine[0] + mine[1]) + c]
    return jnp.stack(slots).astype(jnp.int32)


def _gw_in_pair(order, h, dproj, smalls, small_shapes4):
    kk, m = h.shape
    tk = min(kk, 2048)
    nk = kk // tk
    ncols = dproj.shape[1] // N_DEV
    n = len(smalls)

    def body(order_ref, h_ref, d_ref, *rest):
        ins = rest[:n]
        sums_hbm, parts_hbm = rest[n], rest[n + 1]
        sib = rest[n + 2:2 * n + 2]
        (acc, sendbuf, recvbuf, sumbuf, send_sems, recv_sems, local_sem, ssend, srecv,
         isend, irecv) = rest[2 * n + 2:]
        js, k = pl.program_id(0), pl.program_id(1)
        x, y, c = lax.axis_index("x"), lax.axis_index("y"), lax.axis_index("c")
        sibling = (x, y, 1 - c)
        my_chip = 2 * x + y
        nb1, nb2, _ = _neighbour_chips()
        near = [(*nb1, c), (*nb2, c)]

        def ici_copy(p, out_chip):
            peer = near[p] if isinstance(p, int) else tuple(jnp.where(p == 0, a, b) for a, b in zip(*near))
            return pltpu.make_async_remote_copy(
                src_ref=sumbuf.at[p], dst_ref=parts_hbm.at[out_chip],
                send_sem=isend.at[p], recv_sem=irecv.at[p], device_id=peer, device_id_type=MESH)

        def small_copies():
            return [pltpu.make_async_remote_copy(
                        src_ref=ins[a].at[2 * q + 1 - c], dst_ref=sib[a].at[q],
                        send_sem=ssend.at[a * 4 + q], recv_sem=srecv.at[a * 4 + q],
                        device_id=sibling, device_id_type=MESH)
                    for a in range(n) for q in range(4)]

        def slice_copy(p):
            return pltpu.make_async_remote_copy(
                src_ref=sendbuf, dst_ref=recvbuf.at[p], send_sem=send_sems.at[p], recv_sem=recv_sems.at[p],
                device_id=sibling, device_id_type=MESH)

        def sum_copy(p):
            return pltpu.make_async_copy(sumbuf.at[2], sums_hbm.at[order_ref[2 * p] // 2], local_sem)

        @pl.when((js == 0) & (k == 0))
        def _():
            for cp in small_copies():
                cp.start()

        def partial():
            return _dot_tn(h_ref[...], d_ref[...])

        if nk > 1:
            @pl.when(k == 0)
            def _():
                acc[...] = partial()
        if nk > 2:
            @pl.when((k > 0) & (k < nk - 1))
            def _():
                acc[...] += partial()

        def total():
            return partial() + acc[...] if nk > 1 else partial()

        p = js // 2

        @pl.when((js % 2 == 0) & (k == nk - 1))
        def _():
            @pl.when(p > 0)
            def _():
                slice_copy(p - 1).wait_send()
            sendbuf[...] = total().astype(BF16)
            slice_copy(p).start()

        @pl.when((js % 2 == 1) & (k == nk - 1))
        def _():
            slice_copy(p).wait_recv()

            @pl.when(p == 3)
            def _():
                sum_copy(2).wait()
            sumbuf[jnp.minimum(p, 2)] = (total() + recvbuf[p].astype(F32)).astype(BF16)

            @pl.when(p < 2)
            def _():
                ici_copy(p, my_chip).start()

            @pl.when(p >= 2)
            def _():
                sum_copy(p).start()

        @pl.when((js == N_DEV - 1) & (k == nk - 1))
        def _():
            slice_copy(3).wait_send()
            sum_copy(3).wait()
            for cp in small_copies():
                cp.wait()
            for p in range(2):
                ici_copy(p, 2 * near[p][0] + near[p][1]).wait_recv()
                ici_copy(p, my_chip).wait_send()

    any_spec = pl.BlockSpec(memory_space=pl.ANY)
    res = pl.pallas_call(
        body, name="gw_in_pair",
        grid_spec=pltpu.PrefetchScalarGridSpec(
            num_scalar_prefetch=1,
            grid=(N_DEV, nk),
            in_specs=[pl.BlockSpec((tk, m), lambda js, k, order_ref: (k, 0)),
                      pl.BlockSpec((tk, ncols), lambda js, k, order_ref: (k, order_ref[js]))] + [any_spec] * n,
            out_specs=(any_spec,) * (n + 2),
            scratch_shapes=[pltpu.VMEM((m, ncols), F32), pltpu.VMEM((m, ncols), BF16),
                            pltpu.VMEM((4, m, ncols), BF16), pltpu.VMEM((3, m, ncols), BF16),
                            pltpu.SemaphoreType.DMA((4,)), pltpu.SemaphoreType.DMA((4,)),
                            pltpu.SemaphoreType.DMA,
                            pltpu.SemaphoreType.DMA((4 * n,)), pltpu.SemaphoreType.DMA((4 * n,)),
                            pltpu.SemaphoreType.DMA((2,)), pltpu.SemaphoreType.DMA((2,))]),
        out_shape=(jax.ShapeDtypeStruct((4, m, ncols), BF16),) * 2 + tuple(small_shapes4),
        compiler_params=pltpu.CompilerParams(vmem_limit_bytes=VMEM_LIMIT),
    )(order, h, dproj, *smalls)
    return res[0], res[1], res[2:]


def _chip_copies(ins, outs, send_sems, recv_sems, local_sems, hops):
    n = len(ins)
    x, y, c = lax.axis_index("x"), lax.axis_index("y"), lax.axis_index("c")
    my_chip = 2 * x + y

    def peer_of(k):
        return ((1 - x) if (k >> 1) & 1 else x, (1 - y) if k & 1 else y, c)

    def copy(a, k, out_chip):
        peer = peer_of(k)
        return pltpu.make_async_remote_copy(
            src_ref=ins[a].at[2 * peer[0] + peer[1]], dst_ref=outs[a].at[out_chip],
            send_sem=send_sems.at[a * 3 + k - 1], recv_sem=recv_sems.at[a * 3 + k - 1],
            device_id=peer, device_id_type=MESH)

    sends = [copy(a, k, my_chip) for k in range(1, 4) for a in range(n) if k in hops[a]]
    arrivals = []
    for k in range(1, 4):
        peer = peer_of(k)
        arrivals += [copy(a, k, 2 * peer[0] + peer[1]) for a in range(n) if k in hops[a]]
    mine = [pltpu.make_async_copy(ins[a].at[my_chip], outs[a].at[my_chip], local_sems.at[a])
            for a in range(n)]
    return sends, arrivals, mine


def _pair_add(core, mines, theirs):
    n = len(mines)

    def body(core_ref, *refs):
        mine, sib, outs = refs[:n], refs[n:2 * n], refs[2 * n:]
        for a in range(n):
            for q in range(4):
                outs[a][q] = (mine[a][2 * q + core_ref[0]].astype(F32)
                              + sib[a][q].astype(F32)).astype(outs[a].dtype)

    return pl.pallas_call(
        body, name="pair_add",
        in_specs=[pl.BlockSpec(memory_space=pltpu.SMEM)] + [pl.BlockSpec(memory_space=pltpu.VMEM)] * (2 * n),
        out_shape=tuple(jax.ShapeDtypeStruct(t.shape, t.dtype) for t in theirs),
    )(core, *mines, *theirs)


def _mod_exchange(c8, w_ada, b_cols):
    cols = w_ada.shape[1]

    def body(c_ref, w_ref, b_ref, call_ref, mod_ref, msend, send1, recv1, send2, recv2):
        x, y, c = lax.axis_index("x"), lax.axis_index("y"), lax.axis_index("c")
        my_slot = 4 * x + 2 * y + c

        def peer_of(k):
            return ((1 - x) if (k >> 2) & 1 else x, (1 - y) if (k >> 1) & 1 else y, (1 - c) if k & 1 else c)

        def slot_of(dev):
            return 4 * dev[0] + 2 * dev[1] + dev[2]

        def exchange(src_of, dst_ref, send_sems, recv_sems):
            sends, arrivals = [], []
            for k in range(1, 8):
                peer = peer_of(k)
                sends.append(pltpu.make_async_remote_copy(
                    src_ref=src_of(slot_of(peer)), dst_ref=dst_ref.at[my_slot],
                    send_sem=send_sems.at[k - 1], recv_sem=recv_sems.at[k - 1],
                    device_id=peer, device_id_type=MESH))
                arrivals.append(pltpu.make_async_remote_copy(
                    src_ref=src_of(my_slot), dst_ref=dst_ref.at[slot_of(peer)],
                    send_sem=send_sems.at[k - 1], recv_sem=recv_sems.at[k - 1],
                    device_id=peer, device_id_type=MESH))
            for cp in sends:
                cp.start()
            for cp in arrivals:
                cp.wait_recv()
            for cp in sends:
                cp.wait_send()

        call_ref[my_slot] = c_ref[...]
        exchange(lambda s: c_ref, call_ref, send1, recv1)
        cv = call_ref[...].reshape(N_DEV * 8, c_ref.shape[1])
        act = cv * _sigmoid(cv)
        mod = jnp.dot(act, w_ref[...], preferred_element_type=F32,
                      precision=lax.Precision.HIGHEST) + b_ref[...]
        msend[...] = mod.reshape(N_DEV, 8, cols)
        mod_ref[my_slot] = msend[my_slot]
        exchange(lambda s: msend.at[s], mod_ref, send2, recv2)

    return pl.pallas_call(
        body, name="mod_exchange",
        out_shape=(jax.ShapeDtypeStruct((N_DEV, 8, c8.shape[1]), F32),
                   jax.ShapeDtypeStruct((N_DEV, 8, cols), F32)),
        scratch_shapes=[pltpu.VMEM((N_DEV, 8, cols), F32)] + [pltpu.SemaphoreType.DMA((7,))] * 4,
    )(c8, w_ada, b_cols)


def _w_ada_update(c_all_t, dmod_cols, w, m, v):
    rows, cols = w.shape
    tr = 256

    def body(c_ref, d_ref, w_ref, m_ref, v_ref, g_ref, dl_ref, nm_ref, nv_ref):
        cv = c_ref[...]
        g = jnp.dot(cv * _sigmoid(cv), d_ref[...], preferred_element_type=F32,
                    precision=lax.Precision.HIGHEST)
        g_ref[...] = g
        dl_ref[...], nm_ref[...], nv_ref[...] = _adam_step(g, w_ref[...], m_ref[...], v_ref[...])

    blk = pl.BlockSpec((tr, cols), lambda i: (i, 0))
    shp = jax.ShapeDtypeStruct((rows, cols), F32)
    return pl.pallas_call(
        body, name="adam_w_ada",
        grid=(rows // tr,),
        in_specs=[pl.BlockSpec((tr, c_all_t.shape[1]), lambda i: (i, 0)),
                  pl.BlockSpec(dmod_cols.shape, lambda i: (0, 0)), blk, blk, blk],
        out_specs=(blk, blk, blk, blk),
        out_shape=(shp, shp, shp, shp),
    )(c_all_t, dmod_cols, w, m, v)


def _shard_order():
    x, y, c = lax.axis_index("x"), lax.axis_index("y"), lax.axis_index("c")
    first, second, diag = _neighbour_chips()
    devs = [(x, y, c), (x, y, 1 - c), (*first, c), (*second, 1 - c), (*second, c), (*first, 1 - c),
            (*diag, c), (*diag, 1 - c)]
    return jnp.stack([4 * d[0] + 2 * d[1] + d[2] for d in devs]).astype(jnp.int32)


def _gather_proj(order, x2, mod3, w_shard, tm, ride=()):
    rows, kdim = x2.shape
    ncols = w_shard.shape[1]
    n_i = rows // tm
    per_seq = n_i // mod3.shape[0]
    ride_arrs, ride_shapes = ride if ride else ((), ())
    n_ride = len(ride_arrs)

    def body(order_ref, x_ref, mod_ref, mine_hbm, *rest):
        ride_ins = rest[:n_ride]
        o_ref, h_ref, all_hbm = rest[n_ride:n_ride + 3]
        ride_outs = rest[n_ride + 3:2 * n_ride + 3]
        wv, send_sems, recv_sems, local_sems, hs = rest[2 * n_ride + 3:2 * n_ride + 8]
        ride_scr = rest[2 * n_ride + 8:]
        j, i = pl.program_id(0), pl.program_id(1)
        c = lax.axis_index("c")
        me, sibling = (lax.axis_index("x"), lax.axis_index("y"), c), (lax.axis_index("x"), lax.axis_index("y"), 1 - c)
        nb1, nb2, diag = _neighbour_chips()

        def slot(dev):
            return 4 * dev[0] + 2 * dev[1] + dev[2]

        def copy(k, block, to, src=None, part=None):
            buf = wv.at[slot(block)]
            if part is not None:
                buf = buf.at[pl.ds(pl.multiple_of(part * (kdim // 2), kdim // 2), kdim // 2)]
            return pltpu.make_async_remote_copy(
                src_ref=buf if src is None else src, dst_ref=buf,
                send_sem=send_sems.at[k], recv_sem=recv_sems.at[k],
                device_id=to, device_id_type=MESH)

        def keep(step, block):
            s = slot(block)
            cols = pl.ds(pl.multiple_of((s % 2) * ncols, 128), ncols)
            return pltpu.make_async_copy(wv.at[s], all_hbm.at[s // 2, :, cols], local_sems.at[step])

        if n_ride:
            gather = _Gather(ride_ins, ride_outs, ride_scr[3:], *ride_scr[:3])
        to_sibling, to_nb1, to_nb2 = (copy(0, me, sibling, mine_hbm), copy(1, me, (*nb1, c), mine_hbm),
                                      copy(2, me, (*nb2, c), mine_hbm))
        relay1, relay2 = copy(3, (*nb2, c), (*nb1, c), part=c), copy(4, (*nb1, c), (*nb2, c), part=1 - c)
        pass_nb1, pass_nb2 = copy(5, (*nb1, c), sibling), copy(6, (*nb2, c), sibling)
        pass_d1, pass_d2 = copy(7, (*diag, c), sibling, part=c), copy(8, (*diag, c), sibling, part=1 - c)
        sends = [to_sibling, to_nb1, to_nb2, relay1, relay2, pass_nb1, pass_nb2, pass_d1, pass_d2]
        due = [
            (me, [], []),
            (sibling, [copy(0, sibling, me)], [[]]),
            ((*nb1, c), [copy(1, (*nb1, c), me)], [[pass_nb1, to_nb2]]),
            ((*nb2, 1 - c), [copy(5, (*nb2, 1 - c), me)], [[]]),
            ((*nb2, c), [copy(2, (*nb2, c), me)], [[pass_nb2, relay1, relay2]]),
            ((*nb1, 1 - c), [copy(6, (*nb1, 1 - c), me)], [[]]),
            ((*diag, c), [copy(3, (*diag, c), me, part=c), copy(4, (*diag, c), me, part=1 - c)],
             [[pass_d1], [pass_d2]]),
            ((*diag, 1 - c), [copy(7, (*diag, 1 - c), me, part=1 - c), copy(8, (*diag, 1 - c), me, part=c)],
             [[], []]),
        ]

        @pl.when((j == 0) & (i == 0))
        def _():
            to_sibling.start()
            to_nb1.start()
            load = pltpu.make_async_copy(mine_hbm, wv.at[slot(me)], local_sems.at[N_DEV])
            load.start()
            load.wait()
            keep(0, me).start()

        for step in range(1, N_DEV):
            block, arrivals, then = due[step]

            @pl.when((j == step) & (i == 0))
            def _():
                for arrival, follow in zip(arrivals, then):
                    arrival.wait_recv()
                    for cp in follow:
                        cp.start()
                keep(step, block).start()
                if n_ride and step == N_DEV - 2:
                    gather.begin()

        @pl.when(j == 0)
        def _():
            hb = (x_ref[...] * (1.0 + mod_ref[0, 1:2, :]) + mod_ref[0, 0:1, :]).astype(BF16)
            hs[i] = hb
            h_ref[...] = hb

        o_ref[...] = _dot(hs[i], wv[order_ref[j]]).astype(BF16)

        @pl.when((j == N_DEV - 1) & (i == n_i - 1))
        def _():
            for cp in sends:
                cp.wait_send()
            for step in range(N_DEV):
                keep(step, due[step][0]).wait()
            if n_ride:
                gather.finish()

    def first_pass(j, i):
        return jnp.where(j == 0, i, n_i - 1)

    any_spec = pl.BlockSpec(memory_space=pl.ANY)
    res = pl.pallas_call(
        body, name="gather_proj",
        grid_spec=pltpu.PrefetchScalarGridSpec(
            num_scalar_prefetch=1,
            grid=(N_DEV, n_i),
            in_specs=[pl.BlockSpec((tm, kdim), lambda j, i, order_ref: (first_pass(j, i), 0)),
                      pl.BlockSpec((1, 3, kdim), lambda j, i, order_ref: (first_pass(j, i) // per_seq, 0, 0)),
                      any_spec] + [any_spec] * n_ride,
            out_specs=(pl.BlockSpec((tm, ncols), lambda j, i, order_ref: (i, order_ref[j])),
                       pl.BlockSpec((tm, kdim), lambda j, i, order_ref: (first_pass(j, i), 0)), any_spec)
                      + (any_spec,) * n_ride,
            scratch_shapes=[pltpu.VMEM((N_DEV, kdim, ncols), BF16),
                            pltpu.SemaphoreType.DMA((9,)), pltpu.SemaphoreType.DMA((9,)),
                            pltpu.SemaphoreType.DMA((N_DEV + 1,)), pltpu.VMEM((n_i, tm, kdim), BF16)]
                           + (_Gather.scratch(ride_arrs) if n_ride else [])),
        out_shape=(jax.ShapeDtypeStruct((rows, N_DEV * ncols), BF16),
                   jax.ShapeDtypeStruct((rows, kdim), BF16),
                   jax.ShapeDtypeStruct((N_DEV // 2, kdim, 2 * ncols), BF16)) + tuple(ride_shapes),
        compiler_params=pltpu.CompilerParams(vmem_limit_bytes=VMEM_LIMIT),
    )(order, x2, mod3, w_shard, *ride_arrs)
    return res[0], res[1], res[2], res[3:]


SKEW_W = 512


def _bucket_maps():
    lanes = np.arange(SKEW_W)

    def buckets_of(steps):
        rows = []
        for dil in DILATIONS:
            dist = np.maximum(steps, 0) * dil
            nf = np.maximum(dist, 1).astype(np.float32)
            large = 16 + (np.log(nf / np.float32(16)) / np.float32(math.log(128.0))
                          * np.float32(16)).astype(np.int32)
            large = np.minimum(large, N_BUCKETS - 1)
            bucket = np.where(dist < 16, dist, large)
            rows.append(np.where((steps >= 0) & (steps <= N_STEPS), bucket, -1).astype(np.int32))
        return np.stack(rows)[:, None, :]

    a = np.arange(QB)[:, None]
    b = np.arange(2 * QB)[None, :]
    steps = a + QB - b
    band = (steps >= 0) & (steps <= N_STEPS)
    first = band & (b >= QB)
    masks = np.stack([first, band]).astype(np.int32)
    return buckets_of(QB - lanes), buckets_of(2 * QB - 1 - lanes), masks


def _bias_expand(rel_bias, lane_buckets, masks):
    def body(tab_ref, bk_ref, mk_ref, o_ref):
        for g in range(3):
            bk = bk_ref[g]
            for h in range(4):
                col = 4 * g + h
                per_offset = jnp.zeros((1, SKEW_W), F32)
                for k in range(N_BUCKETS):
                    per_offset = jnp.where(bk == k, tab_ref[k, col], per_offset)
                tile = pltpu.roll(jnp.broadcast_to(per_offset, (QB, SKEW_W)), 0, 1, stride=1, stride_axis=0)
                tile = tile[:, :2 * QB]
                o_ref[g, 0, h] = jnp.where(mk_ref[0] != 0, tile, NEG_INF)
                o_ref[g, 1, h] = jnp.where(mk_ref[1] != 0, tile, NEG_INF)

    return pl.pallas_call(
        body, name="bias_expand",
        in_specs=[pl.BlockSpec(memory_space=pltpu.SMEM),
                  pl.BlockSpec(memory_space=pltpu.VMEM),
                  pl.BlockSpec(memory_space=pltpu.VMEM)],
        out_shape=jax.ShapeDtypeStruct((3, 2, 4, QB, 2 * QB), F32),
    )(rel_bias, lane_buckets, masks)


def _bias_grad(ds1, ds2, ds3, lane_buckets):
    exchange = jnp.asarray(np.eye(QB, dtype=np.float32)[::-1].copy())

    def body(d1_ref, d2_ref, d3_ref, bk_ref, ex_ref, o_ref):
        for g, d_ref in enumerate((d1_ref, d2_ref, d3_ref)):
            bk = bk_ref[g]
            for h in range(4):
                flipped = jnp.dot(ex_ref[...], d_ref[h], preferred_element_type=F32,
                                  precision=lax.Precision.HIGHEST)
                padded = jnp.concatenate([flipped, jnp.zeros((QB, SKEW_W - 2 * QB), F32)], axis=1)
                skewed = pltpu.roll(padded, 0, 1, stride=1, stride_axis=0)
                per_offset = jnp.sum(skewed, axis=0, keepdims=True)
                for k in range(N_BUCKETS):
                    o_ref[k, 4 * g + h] = jnp.sum(jnp.where(bk == k, per_offset, 0.0))

    return pl.pallas_call(
        body, name="bias_grad",
        in_specs=[pl.BlockSpec(memory_space=pltpu.VMEM)] * 5,
        out_specs=pl.BlockSpec(memory_space=pltpu.SMEM),
        out_shape=jax.ShapeDtypeStruct((N_BUCKETS, N_HEADS), F32),
    )(ds1, ds2, ds3, lane_buckets, exchange)


def _scratch_sets(rows):
    return 4 if rows <= 512 else 1


def _unit_chunks(dil, size=16):
    units = [(h, r) for h in range(4) for r in range(dil)]
    return [units[i:i + size] for i in range(0, len(units), size)]


def _residue_rows(src_ref, copies, h, residue):
    buf = copies[h % len(copies)]
    buf[...] = src_ref[:, h * HD:(h + 1) * HD].astype(F32)
    return lambda r: buf[residue(r), :].astype(BF16)


def _attn_fwd(proj, bias, g):
    dil = DILATIONS[g]
    rows = QB * dil
    nsb = S // rows
    has_prev = nsb > 1

    def residue(r):
        return pl.ds(r, QB, stride=dil)

    n_sets = _scratch_sets(rows)
    n_in = 6 if has_prev else 4
    n_copied = (4 + (2 if has_prev else 0)) * n_sets

    def body(*refs):
        q_ref, kc_ref, vc_ref = refs[:3]
        kp_ref, vp_ref = refs[3:5] if has_prev else (None, None)
        b_ref = refs[n_in - 1]
        o_ref, l_ref = refs[n_in:n_in + 2]
        scr = list(refs[n_in + 2:])
        ls = [scr.pop(0) for _ in range(4)]
        copies = {name: [scr.pop(0) for _ in range(n_sets)]
                  for name in ("q", "kc", "vc", "o") + (("kp", "vp") if has_prev else ())}
        lane = lax.broadcasted_iota(jnp.int32, (QB, 128), 1)
        refs_of = {"q": q_ref, "kc": kc_ref, "vc": vc_ref, "kp": kp_ref, "vp": vp_ref}
        for chunk in _unit_chunks(dil):
            rows_of = {h: {name: _residue_rows(refs_of[name], copies[name], h, residue)
                           for name in refs_of if refs_of[name] is not None}
                       for h in sorted({h for h, _ in chunk})}

            def batch(name):
                return jnp.stack([rows_of[h][name](r) for h, r in chunk])

            q, k, v = batch("q"), batch("kc"), batch("vc")
            if has_prev:
                k = jnp.concatenate([batch("kp"), k], axis=1)
                v = jnp.concatenate([batch("vp"), v], axis=1)
                bias_b = jnp.stack([b_ref[h] for h, _ in chunk])
            else:
                bias_b = jnp.stack([b_ref[h, :, QB:] for h, _ in chunk])
            s = jnp.einsum("uqd,ukd->uqk", q, k, preferred_element_type=F32) * SCALE + bias_b
            m = jnp.max(s, axis=-1, keepdims=True)
            p = jnp.exp(s - m)
            l = jnp.sum(p, axis=-1, keepdims=True)
            o = jnp.einsum("uqk,ukd->uqd", p.astype(BF16), v, preferred_element_type=F32) / l
            lse = m + jnp.log(l)
            for i, (h, r) in enumerate(chunk):
                copies["o"][h % n_sets][residue(r), :] = o[i]
                ls[h][r * QB:(r + 1) * QB, :] = jnp.where(lane == h, lse[i], 0.0)
            for h in sorted({h for h, _ in chunk}):
                o_ref[:, h * HD:(h + 1) * HD] = copies["o"][h % n_sets][...]
        for r in range(dil):
            blk = slice(r * QB, (r + 1) * QB)
            l_ref[residue(r), :] = (ls[0][blk, :] + ls[1][blk, :]) + (ls[2][blk, :] + ls[3][blk, :])

    def row(b, n):
        return b * nsb + n

    def prev(b, n):
        return b * nsb + jnp.maximum(n - 1, 0)

    in_specs = [
        pl.BlockSpec((rows, GW), lambda b, n: (row(b, n), CB_Q + g)),
        pl.BlockSpec((rows, GW), lambda b, n: (row(b, n), CB_K + g)),
        pl.BlockSpec((rows, GW), lambda b, n: (row(b, n), CB_V + g)),
    ]
    args = [proj, proj, proj]
    scratch = [pltpu.VMEM((rows, 128), F32)] * (4 + n_copied)
    if has_prev:
        in_specs += [pl.BlockSpec((rows, GW), lambda b, n: (prev(b, n), CB_K + g)),
                     pl.BlockSpec((rows, GW), lambda b, n: (prev(b, n), CB_V + g))]
        args += [proj, proj]
    in_specs.append(pl.BlockSpec((None, None, 4, QB, 2 * QB),
                                 lambda b, n: (g, jnp.minimum(n, 1), 0, 0, 0)))
    args.append(bias)
    return pl.pallas_call(
        body, name=f"attn_fwd{g}",
        grid=(BL, nsb),
        in_specs=in_specs,
        out_specs=(pl.BlockSpec((rows, GW), lambda b, n: (row(b, n), 0)),
                   pl.BlockSpec((rows, 128), lambda b, n: (row(b, n), 0))),
        out_shape=(jax.ShapeDtypeStruct((T, GW), F32), jax.ShapeDtypeStruct((T, 128), F32)),
        scratch_shapes=scratch,
        compiler_params=pltpu.CompilerParams(vmem_limit_bytes=VMEM_LIMIT),
    )(*args)


def _attn_bwd(proj, d_out, stats, bias, dproj, g):
    dil = DILATIONS[g]
    rows = QB * dil
    nsb = S // rows
    has_prev = nsb > 1
    n_steps = nsb + 1 if has_prev else 1
    n_in = 7 + (2 if has_prev else 0)

    def residue(r):
        return pl.ds(r, QB, stride=dil)

    n_sets = _scratch_sets(rows)

    def body(*refs):
        q_ref, kc_ref, vc_ref, do_ref, st_ref, b_ref = refs[:6]
        kp_ref, vp_ref = refs[6:8] if has_prev else (None, None)
        out_ref, db_ref = refs[n_in], refs[n_in + 1]
        scr = list(refs[n_in + 2:])
        sq, sk, sv, sems = [scr.pop(0) for _ in range(4)]
        carry = scr.pop(0) if has_prev else None
        sts = scr.pop(0)
        copies = {name: [scr.pop(0) for _ in range(n_sets)]
                  for name in ("q", "kc", "vc", "do", "dq", "dk", "dv") + (("kp", "vp") if has_prev else ())}
        b, n = pl.program_id(0), pl.program_id(1)

        @pl.when((b == 0) & (n == 0))
        def _():
            db_ref[...] = jnp.zeros_like(db_ref)

        def finish(h, r, dq, dk, dv):
            for name, val in (("dq", dq), ("dk", dk), ("dv", dv)):
                copies[name][h % n_sets][residue(r), :] = val

        step = b * n_steps + n
        slot = step % 2

        def stage_copies(s, row0):
            return _column_copies([(sq.at[s], CB * (CB_Q + g)), (sk.at[s], CB * (CB_K + g)),
                                   (sv.at[s], CB * (CB_V + g))], out_ref, row0, sems.at[s])

        @pl.when((step >= 2) & ((step - 2) % n_steps >= (1 if has_prev else 0)))
        def _():
            for cp in stage_copies(slot, 0):
                cp.wait()

        def finish_head(h):
            sl = slice(h * HD, (h + 1) * HD)
            sq[slot, :, sl] = copies["dq"][h % n_sets][...].astype(BF16)
            sk[slot, :, sl] = copies["dk"][h % n_sets][...].astype(BF16)
            sv[slot, :, sl] = copies["dv"][h % n_sets][...].astype(BF16)

        def write_block(blk_idx):
            for cp in stage_copies(slot, pl.multiple_of(blk_idx * rows, rows)):
                cp.start()

            @pl.when(step == BL * n_steps - 1)
            def _():
                for s in range(2):
                    for cp in stage_copies(s, 0):
                        cp.wait()

        def carried(h, r):
            blk = slice(r * QB, (r + 1) * QB)
            return ((blk, slice(h * HD, (h + 1) * HD)), (blk, slice(GW + h * HD, GW + (h + 1) * HD)),
                    (blk, slice(2 * GW + h * HD, 2 * GW + (h + 1) * HD)))

        if has_prev:
            @pl.when(n == 0)
            def _():
                carry[...] = jnp.zeros_like(carry)

            @pl.when(n == nsb)
            def _():
                for h in range(4):
                    for r in range(dil):
                        cq, ck, cv = carried(h, r)
                        finish(h, r, carry[cq], carry[ck], carry[cv])
                    finish_head(h)
                write_block(b * nsb + nsb - 1)

        @pl.when(n < nsb)
        def _():
            for r in range(dil):
                sts[r * QB:(r + 1) * QB, :] = st_ref[residue(r), :]
            refs_of = {"q": q_ref, "kc": kc_ref, "vc": vc_ref, "do": do_ref, "kp": kp_ref, "vp": vp_ref}
            for chunk in _unit_chunks(dil):
                heads = sorted({h for h, _ in chunk})
                rows_of = {h: {name: _residue_rows(refs_of[name], copies[name], h, residue)
                               for name in refs_of if refs_of[name] is not None}
                           for h in heads}

                def batch(name):
                    return jnp.stack([rows_of[h][name](r) for h, r in chunk])

                q, k, v, do = batch("q"), batch("kc"), batch("vc"), batch("do")
                if has_prev:
                    k = jnp.concatenate([batch("kp"), k], axis=1)
                    v = jnp.concatenate([batch("vp"), v], axis=1)
                    bias_b = jnp.stack([b_ref[h] for h, _ in chunk])
                else:
                    bias_b = jnp.stack([b_ref[h, :, QB:] for h, _ in chunk])
                lse = jnp.stack([sts[r * QB:(r + 1) * QB, h:h + 1] for h, r in chunk])
                delta = jnp.stack([sts[r * QB:(r + 1) * QB, 4 + h:5 + h] for h, r in chunk])
                s = jnp.einsum("uqd,ukd->uqk", q, k, preferred_element_type=F32) * SCALE + bias_b
                p = jnp.exp(s - lse)
                ds = p * (jnp.einsum("uqd,ukd->uqk", do, v, preferred_element_type=F32) - delta)
                for h in heads:
                    mine = [ds[i] for i, (hh, _) in enumerate(chunk) if hh == h]
                    tot = mine[0]
                    for extra in mine[1:]:
                        tot = tot + extra
                    if has_prev:
                        db_ref[h] += tot
                    else:
                        db_ref[h, :, QB:] += tot
                dsb, pb = ds.astype(BF16), p.astype(BF16)
                dq = jnp.einsum("uqk,ukd->uqd", dsb, k, preferred_element_type=F32) * SCALE
                dk = jnp.einsum("uqk,uqd->ukd", dsb, q, preferred_element_type=F32) * SCALE
                dv = jnp.einsum("uqk,uqd->ukd", pb, do, preferred_element_type=F32)
                for i, (h, r) in enumerate(chunk):
                    if has_prev:
                        cq, ck, cv = carried(h, r)
                        finish(h, r, carry[cq], carry[ck] + dk[i, :QB], carry[cv] + dv[i, :QB])
                        carry[cq] = dq[i]
                        carry[ck] = dk[i, QB:]
                        carry[cv] = dv[i, QB:]
                    else:
                        finish(h, r, dq[i], dk[i], dv[i])
                for h in heads:
                    finish_head(h)
            if has_prev:
                @pl.when(n > 0)
                def _():
                    write_block(b * nsb + n - 1)
            else:
                write_block(b)

    def row(b, n):
        return b * nsb + jnp.minimum(n, nsb - 1)

    def prev(b, n):
        return b * nsb + jnp.maximum(jnp.minimum(n, nsb - 1) - 1, 0)

    in_specs = [
        pl.BlockSpec((rows, GW), lambda b, n: (row(b, n), CB_Q + g)),
        pl.BlockSpec((rows, GW), lambda b, n: (row(b, n), CB_K + g)),
        pl.BlockSpec((rows, GW), lambda b, n: (row(b, n), CB_V + g)),
        pl.BlockSpec((rows, GW), lambda b, n: (row(b, n), 0)),
        pl.BlockSpec((rows, 128), lambda b, n: (row(b, n), 0)),
        pl.BlockSpec((None, None, 4, QB, 2 * QB),
                     lambda b, n: (g, jnp.minimum(jnp.minimum(n, nsb - 1), 1), 0, 0, 0)),
    ]
    args = [proj, proj, proj, d_out, stats, bias]
    scratch = [pltpu.VMEM((2, rows, GW), BF16)] * 3 + [pltpu.SemaphoreType.DMA((2, 3))]
    if has_prev:
        in_specs += [pl.BlockSpec((rows, GW), lambda b, n: (prev(b, n), CB_K + g)),
                     pl.BlockSpec((rows, GW), lambda b, n: (prev(b, n), CB_V + g))]
        args += [proj, proj]
        scratch.append(pltpu.VMEM((rows, 3 * GW), F32))
    n_copied = (7 + (2 if has_prev else 0)) * n_sets
    scratch += [pltpu.VMEM((rows, 128), F32)] * (1 + n_copied)
    in_specs.append(pl.BlockSpec(memory_space=pl.ANY))
    args.append(dproj)
    return pl.pallas_call(
        body, name=f"attn_bwd{g}",
        grid=(BL, n_steps),
        in_specs=in_specs,
        out_specs=(pl.BlockSpec(memory_space=pl.ANY),
                   pl.BlockSpec((4, QB, 2 * QB), lambda b, n: (0, 0, 0))),
        out_shape=(jax.ShapeDtypeStruct((T, NCOL), BF16),
                   jax.ShapeDtypeStruct((4, QB, 2 * QB), F32)),
        scratch_shapes=scratch,
        input_output_aliases={len(args) - 1: 0},
        compiler_params=pltpu.CompilerParams(vmem_limit_bytes=VMEM_LIMIT),
    )(*args)


def _tail(x2, tgt2, mod3, o_g, lse_g, proj, w_ao, w_co, w_o, conv_w, conv_b, ln_g, ln_b):
    tm = 256
    per_seq = S // tm
    halo = 16

    def body(x_ref, t_ref, mod_ref, o1_ref, o2_ref, o3_ref, l1_ref, l2_ref, l3_ref,
             ga_ref, u_ref, bg_ref, cg_ref, gc_ref, ma_ref, mc_ref, up_ref, cp_ref,
             wao_ref, wco_ref, wo_ref, cw_ref, cb_ref, lg_ref, lb_ref,
             dproj_ref, dyc_ref, do_ref, st_ref, dxd_ref,
             gwo_ref, gwco_ref, gwao_ref, vec_ref,
             dga_s, dbg_s, dgm_s, sems, acc_o, acc_co, acc_ao):
        i = pl.program_id(0)
        bidx = i // per_seq
        first = (i % per_seq) == 0

        @pl.when(i == 0)
        def _():
            vec_ref[...] = jnp.zeros_like(vec_ref)
            acc_o[...] = jnp.zeros_like(acc_o)
            acc_co[...] = jnp.zeros_like(acc_co)
            acc_ao[...] = jnp.zeros_like(acc_ao)

        slot = i % 2

        def column_copies(s, row0):
            return _column_copies([(dga_s.at[s], CB * CB_GA), (dbg_s.at[s], D * KB_BG), (dgm_s.at[s], D * KB_GC)],
                                  dproj_ref, row0, sems.at[s])

        @pl.when(i >= 2)
        def _():
            for cp in column_copies(slot, 0):
                cp.wait()

        l1, l2, l3 = l1_ref[...], l2_ref[...], l3_ref[...]
        mx = jnp.maximum(jnp.maximum(l1, l2), l3)
        e1, e2, e3 = jnp.exp(l1 - mx), jnp.exp(l2 - mx), jnp.exp(l3 - mx)
        esum = e1 + e2 + e3
        lse_tot = mx + jnp.log(esum)
        w1, w2, w3 = e1 / esum, e2 / esum, e3 / esum

        def per_head(wv):
            return jnp.concatenate([jnp.broadcast_to(wv[:, h:h + 1], (tm, HD)) for h in range(4)], axis=1)

        o = per_head(w1) * o1_ref[...] + per_head(w2) * o2_ref[...] + per_head(w3) * o3_ref[...]

        ga = ga_ref[...].astype(F32)
        sig_ga = _sigmoid(ga)
        silu_ga = ga * sig_ga
        a_in = (o * silu_ga).astype(BF16)
        a_out = _dot(a_in, wao_ref[...])

        u = u_ref[...].astype(F32)
        cg = cg_ref[...].astype(F32)
        z = cg * u
        zp = cp_ref[...].astype(F32) * up_ref[...].astype(F32)
        zp = jnp.where(first, 0.0, zp)
        zcat = jnp.concatenate([zp, z], axis=0)
        z1 = pltpu.roll(zcat, 1, 0)[halo:]
        z2 = pltpu.roll(zcat, 2, 0)[halo:]
        y_conv = cw_ref[0:1, :] * z2 + cw_ref[1:2, :] * z1 + cw_ref[2:3, :] * z + cb_ref[...]
        gc = gc_ref[...].astype(F32)
        sig_gc = _sigmoid(gc)
        silu_gc = gc * sig_gc
        bg = bg_ref[...].astype(F32)
        bg_yc = bg * y_conv
        s_in = (bg_yc * silu_gc).astype(BF16)
        s_out = _dot(s_in, wco_ref[...])

        sa = _sigmoid(ma_ref[...].astype(F32))
        sc = _sigmoid(mc_ref[...].astype(F32))
        merged = (sa * a_out + sc * s_out).astype(BF16)
        y = _dot(merged, wo_ref[...])
        gate1 = 1.0 + mod_ref[0, 2:3, :]
        xv = x_ref[...]
        resid = ALPHA * xv + gate1 * y
        mu = jnp.mean(resid, axis=1, keepdims=True)
        xc = resid - mu
        var = jnp.mean(xc * xc, axis=1, keepdims=True)
        rstd = lax.rsqrt(var + LN_EPS)
        xhat = xc * rstd
        lg = lg_ref[...]
        err = xhat * lg + lb_ref[...] - t_ref[...]
        vec_ref[3:4, :] += (0.5 / D) * jnp.sum(err * err, axis=0, keepdims=True)

        vec_ref[1:2, :] += (1.0 / D) * jnp.sum(err * xhat, axis=0, keepdims=True)
        vec_ref[2:3, :] += (1.0 / D) * jnp.sum(err, axis=0, keepdims=True)
        dxh = err * (lg * (1.0 / D))
        dres = rstd * (dxh - jnp.mean(dxh, axis=1, keepdims=True)
                       - xhat * jnp.mean(dxh * xhat, axis=1, keepdims=True))
        dxd_ref[...] = ALPHA * dres
        dgate = jnp.sum(dres * y, axis=0, keepdims=True)
        vec_ref[4:5, :] += jnp.where(bidx == 0, dgate, 0.0)
        vec_ref[5:6, :] += jnp.where(bidx == 1, dgate, 0.0)
        dy = (dres * gate1).astype(BF16)

        dmerged = _dot_nt(dy, wo_ref[...])
        da_out_f = dmerged * sa
        ds_out_f = dmerged * sc
        da_out = da_out_f.astype(BF16)
        ds_out = ds_out_f.astype(BF16)
        dgm_s[slot, :, 2 * D:3 * D] = (ds_out_f * s_out * (1.0 - sc)).astype(BF16)
        dgm_s[slot, :, D:2 * D] = (da_out_f * a_out * (1.0 - sa)).astype(BF16)
        da_in = _dot_nt(da_out, wao_ref[...])
        ds_in = _dot_nt(ds_out, wco_ref[...])

        d_o = da_in * silu_ga
        do_ref[...] = d_o.astype(BF16)
        dga_s[slot] = (da_in * o * (sig_ga + silu_ga * (1.0 - sig_ga))).astype(BF16)
        lane = lax.broadcasted_iota(jnp.int32, (tm, 128), 1)
        stats = lse_tot
        od = o * d_o
        for h in range(4):
            delta = jnp.sum(od[:, h * HD:(h + 1) * HD], axis=1, keepdims=True)
            stats = jnp.where(lane == 4 + h, delta, stats)
        st_ref[...] = stats

        ds_silu = ds_in * silu_gc
        dbg_s[slot] = (ds_silu * y_conv).astype(BF16)
        dyc = ds_silu * bg
        dyc_ref[...] = dyc
        vec_ref[0:1, :] += jnp.sum(dyc, axis=0, keepdims=True)
        dgm_s[slot, :, 0:D] = (ds_in * bg_yc * (sig_gc + silu_gc * (1.0 - sig_gc))).astype(BF16)

        acc_o[...] += _dot_tn(merged, dy)
        acc_co[...] += _dot_tn(s_in, ds_out)
        acc_ao[...] += _dot_tn(a_in, da_out)

        for cp in column_copies(slot, pl.multiple_of(i * tm, tm)):
            cp.start()

        @pl.when(i == T // tm - 1)
        def _():
            gwo_ref[...] = acc_o[...].astype(BF16)
            gwco_ref[...] = acc_co[...].astype(BF16)
            gwao_ref[...] = acc_ao[...].astype(BF16)
            for s in range(2):
                for cp in column_copies(s, 0):
                    cp.wait()

    def tile(width, cblk=0):
        return pl.BlockSpec((tm, width), lambda i: (i, cblk))

    def whole(shape):
        return pl.BlockSpec(shape, lambda i: tuple(0 for _ in shape))

    def once(shape):
        return pl.BlockSpec(shape, lambda i: tuple(0 for _ in shape), pipeline_mode=pl.Buffered(1))

    prev_rows = lambda i: (jnp.maximum(i * (tm // halo) - 1, 0),)
    in_specs = [
        tile(D), tile(D), pl.BlockSpec((1, 3, D), lambda i: (i // per_seq, 0, 0)),
        tile(GW), tile(GW), tile(GW), tile(128), tile(128), tile(128),
        tile(GW, CB_GA), tile(D, KB_U), tile(D, KB_BG), tile(D, KB_CG), tile(D, KB_GC),
        tile(D, KB_MA), tile(D, KB_MC),
        pl.BlockSpec((halo, D), lambda i: (*prev_rows(i), KB_U)),
        pl.BlockSpec((halo, D), lambda i: (*prev_rows(i), KB_CG)),
        whole((GW, D)), whole((D, D)), whole((D, D)),
        whole((3, D)), whole((1, D)), whole((1, D)), whole((1, D)),
    ]
    out_specs = (
        pl.BlockSpec(memory_space=pl.ANY), tile(D), tile(GW), tile(128), tile(D),
        once((D, D)), once((D, D)), once((GW, D)),
        pl.BlockSpec((8, D), lambda i: (0, 0)),
    )
    out_shape = (
        jax.ShapeDtypeStruct((T, NCOL), BF16),
        jax.ShapeDtypeStruct((T, D), F32),
        jax.ShapeDtypeStruct((T, GW), BF16),
        jax.ShapeDtypeStruct((T, 128), F32),
        jax.ShapeDtypeStruct((T, D), F32),
        jax.ShapeDtypeStruct((D, D), BF16),
        jax.ShapeDtypeStruct((D, D), BF16),
        jax.ShapeDtypeStruct((GW, D), BF16),
        jax.ShapeDtypeStruct((8, D), F32),
    )
    return pl.pallas_call(
        body, name="tail",
        grid=(T // tm,),
        in_specs=in_specs, out_specs=out_specs, out_shape=out_shape,
        scratch_shapes=[pltpu.VMEM((2, tm, GW), BF16), pltpu.VMEM((2, tm, D), BF16), pltpu.VMEM((2, tm, 3 * D), BF16),
                        pltpu.SemaphoreType.DMA((2, 3)),
                        pltpu.VMEM((D, D), F32), pltpu.VMEM((D, D), F32), pltpu.VMEM((GW, D), F32)],
        compiler_params=pltpu.CompilerParams(vmem_limit_bytes=VMEM_LIMIT_TAIL),
    )(x2, tgt2, mod3, *o_g, *lse_g, proj, proj, proj, proj, proj, proj, proj, proj, proj,
      w_ao, w_co, w_o, conv_w, conv_b, ln_g, ln_b)


def _conv_bwd(dyc, proj, conv_w, dproj):
    tm = 512
    per_seq = S // tm

    def body(d_ref, dn_ref, u_ref, c_ref, cw_ref, _, dproj_ref, g_ref, du_s, dc_s, sems):
        i = pl.program_id(0)
        last = (i % per_seq) == per_seq - 1

        @pl.when(i == 0)
        def _():
            g_ref[...] = jnp.zeros_like(g_ref)

        slot = i % 2

        def column_copies(s, row0):
            return _column_copies([(du_s.at[s], D * KB_U), (dc_s.at[s], D * KB_CG)], dproj_ref, row0, sems.at[s])

        @pl.when(i >= 2)
        def _():
            for cp in column_copies(slot, 0):
                cp.wait()

        d = d_ref[...]
        dn = jnp.where(last, 0.0, dn_ref[...])
        dcat = jnp.concatenate([d, dn], axis=0)
        d1 = pltpu.roll(dcat, tm + 8 - 1, 0)[:tm]
        d2 = pltpu.roll(dcat, tm + 8 - 2, 0)[:tm]
        dz = cw_ref[2:3, :] * d + cw_ref[1:2, :] * d1 + cw_ref[0:1, :] * d2
        u = u_ref[...].astype(F32)
        cg = c_ref[...].astype(F32)
        du_s[slot] = (dz * cg).astype(BF16)
        dc_s[slot] = (dz * u).astype(BF16)
        for cp in column_copies(slot, pl.multiple_of(i * tm, tm)):
            cp.start()

        z = cg * u
        g_ref[0:1, :] += jnp.sum(d2 * z, axis=0, keepdims=True)
        g_ref[1:2, :] += jnp.sum(d1 * z, axis=0, keepdims=True)
        g_ref[2:3, :] += jnp.sum(d * z, axis=0, keepdims=True)

        @pl.when(i == T // tm - 1)
        def _():
            for s in range(2):
                for cp in column_copies(s, 0):
                    cp.wait()

    n_tiles = T // tm
    next_rows = lambda i: jnp.minimum((i + 1) * (tm // 8), T // 8 - 1)
    return pl.pallas_call(
        body, name="conv_bwd",
        grid=(n_tiles,),
        in_specs=[pl.BlockSpec((tm, D), lambda i: (i, 0)),
                  pl.BlockSpec((8, D), lambda i: (next_rows(i), 0)),
                  pl.BlockSpec((tm, D), lambda i: (i, KB_U)),
                  pl.BlockSpec((tm, D), lambda i: (i, KB_CG)),
                  pl.BlockSpec((3, D), lambda i: (0, 0)),
                  pl.BlockSpec(memory_space=pl.ANY)],
        out_specs=(pl.BlockSpec(memory_space=pl.ANY),
                   pl.BlockSpec((8, D), lambda i: (0, 0))),
        out_shape=(jax.ShapeDtypeStruct((T, NCOL), BF16),
                   jax.ShapeDtypeStruct((8, D), F32)),
        scratch_shapes=[pltpu.VMEM((2, tm, D), BF16), pltpu.VMEM((2, tm, D), BF16), pltpu.SemaphoreType.DMA((2, 2))],
        input_output_aliases={5: 0},
        compiler_params=pltpu.CompilerParams(vmem_limit_bytes=VMEM_LIMIT),
    )(dyc, dyc, proj, proj, conv_w, dproj)


def _dh_dx(dproj, w_in_all, x2, dxd, mod3, chip_sums, hops=(), parts0=None):
    tm = 512
    per_seq = S // tm
    n_pass, _, width = w_in_all.shape
    n = len(chip_sums)
    n_in = 5 + n + (0 if parts0 is None else 1)

    def body(*refs):
        d_ref, w_ref, x_ref, dxd_ref, mod_ref = refs[:5]
        ins = refs[5:5 + n]
        gx_ref, vec_ref = refs[n_in:n_in + 2]
        outs = refs[n_in + 2:n_in + 2 + n]
        acc, send_sems, recv_sems, local_sems = refs[n_in + 2 + n:]
        jj, i = pl.program_id(0), pl.program_id(1)

        @pl.when((i == 0) & (jj == 0))
        def _():
            vec_ref[...] = jnp.zeros_like(vec_ref)
            if n:
                sends, _, mine = _chip_copies(ins, outs, send_sems, recv_sems, local_sems, hops)
                for cp in sends + mine:
                    cp.start()

        if n:
            @pl.when((i == T // tm - 1) & (jj == n_pass - 1))
            def _():
                sends, arrivals, mine = _chip_copies(ins, outs, send_sems, recv_sems, local_sems, hops)
                for cp in arrivals:
                    cp.wait_recv()
                for cp in sends:
                    cp.wait_send()
                for cp in mine:
                    cp.wait()

        def partial():
            return _dot_nt(d_ref[...], w_ref[...])

        @pl.when(jj == 0)
        def _():
            acc[i] = partial()

        @pl.when((jj > 0) & (jj < n_pass - 1))
        def _():
            acc[i] += partial()

        @pl.when(jj == n_pass - 1)
        def _():
            dh = acc[i] + partial()
            bidx = i // per_seq
            gx_ref[...] = dxd_ref[...] + dh * (1.0 + mod_ref[0, 1:2, :])
            dshift = jnp.sum(dh, axis=0, keepdims=True)
            dscale = jnp.sum(dh * x_ref[...], axis=0, keepdims=True)
            vec_ref[0:1, :] += jnp.where(bidx == 0, dshift, 0.0)
            vec_ref[1:2, :] += jnp.where(bidx == 1, dshift, 0.0)
            vec_ref[2:3, :] += jnp.where(bidx == 0, dscale, 0.0)
            vec_ref[3:4, :] += jnp.where(bidx == 1, dscale, 0.0)

    def last_pass(jj, i):
        return jnp.where(jj == n_pass - 1, i, 0)

    any_spec = pl.BlockSpec(memory_space=pl.ANY)
    res = pl.pallas_call(
        body, name="dh_dx",
        grid=(n_pass, T // tm),
        in_specs=[
            pl.BlockSpec((tm, width), lambda jj, i: (i, jj)),
            pl.BlockSpec((None, D, width), lambda jj, i: (jj, 0, 0)),
            pl.BlockSpec((tm, D), lambda jj, i: (last_pass(jj, i), 0)),
            pl.BlockSpec((tm, D), lambda jj, i: (last_pass(jj, i), 0)),
            pl.BlockSpec((1, 3, D), lambda jj, i: (last_pass(jj, i) // per_seq, 0, 0))]
                 + [any_spec] * (n_in - 5),
        out_specs=(pl.BlockSpec((tm, D), lambda jj, i: (last_pass(jj, i), 0)),
                   pl.BlockSpec((8, D), lambda jj, i: (0, 0))) + (any_spec,) * n,
        out_shape=(jax.ShapeDtypeStruct((T, D), F32), jax.ShapeDtypeStruct((8, D), F32))
                  + tuple(jax.ShapeDtypeStruct(a.shape, a.dtype) for a in chip_sums),
        scratch_shapes=[pltpu.VMEM((T // tm, tm, D), F32), pltpu.SemaphoreType.DMA((max(3 * n, 1),)),
                        pltpu.SemaphoreType.DMA((max(3 * n, 1),)), pltpu.SemaphoreType.DMA((max(n, 1),))],
        input_output_aliases={} if parts0 is None else {5 + n: 2},
        compiler_params=pltpu.CompilerParams(vmem_limit_bytes=VMEM_LIMIT),
    )(dproj, w_in_all, x2, dxd, mod3, *chip_sums, *([] if parts0 is None else [parts0]))
    return res[0], res[1], res[2:]


def _adam_step(g, w, m, v):
    nm = ADAM_B1 * m + (1.0 - ADAM_B1) * g
    nv = ADAM_B2 * v + (1.0 - ADAM_B2) * (g * g)
    m_hat = nm / (1.0 - ADAM_B1 ** ADAM_STEP)
    v_hat = nv / (1.0 - ADAM_B2 ** ADAM_STEP)
    return -ADAM_LR * (m_hat / (jnp.sqrt(v_hat) + ADAM_EPS) + ADAM_WD * w), nm, nv


def _adamw(parts, w, m, v, name, row_tile=None):
    n_parts, rows, cols = parts.shape
    tr = rows if row_tile is None else row_tile

    def body(p_ref, w_ref, m_ref, v_ref, g_ref, d_ref, nm_ref, nv_ref):
        g = p_ref[0].astype(F32)
        for s in range(1, n_parts):
            g = g + p_ref[s].astype(F32)
        g_ref[...] = g
        d_ref[...], nm_ref[...], nv_ref[...] = _adam_step(g, w_ref[...], m_ref[...], v_ref[...])

    blk = pl.BlockSpec((tr, cols), lambda i: (i, 0))
    shp = jax.ShapeDtypeStruct((rows, cols), F32)
    return pl.pallas_call(
        body, name=name,
        grid=(rows // tr,),
        in_specs=[pl.BlockSpec((n_parts, tr, cols), lambda i: (0, i, 0)), blk, blk, blk],
        out_specs=(blk, blk, blk, blk),
        out_shape=(shp, shp, shp, shp),
        compiler_params=pltpu.CompilerParams(vmem_limit_bytes=VMEM_LIMIT),
    )(parts, w, m, v)


def _multi_adamw(parts_list, params, name):
    n = len(params)
    flat = [t for wmv in params for t in wmv]

    def body(*refs):
        parts, ins, outs = refs[:n], refs[n:4 * n], refs[4 * n:]
        for p in range(n):
            g = parts[p][0].astype(F32)
            for s in range(1, parts[p].shape[0]):
                g = g + parts[p][s].astype(F32)
            w_ref, m_ref, v_ref = ins[3 * p:3 * p + 3]
            g_ref, d_ref, nm_ref, nv_ref = outs[4 * p:4 * p + 4]
            g_ref[...] = g
            d_ref[...], nm_ref[...], nv_ref[...] = _adam_step(g, w_ref[...], m_ref[...], v_ref[...])

    out_shape = []
    for w, _, _ in params:
        out_shape += [jax.ShapeDtypeStruct(w.shape, F32)] * 4
    res = pl.pallas_call(body, name=name, out_shape=tuple(out_shape))(*parts_list, *flat)
    return [res[4 * p:4 * p + 4] for p in range(n)]


def _small_updates(small_g, dmod_all, rel_parts, params):
    flat = [t for wmv in params for t in wmv]

    def body(sg_ref, dm_ref, rp_ref, *refs):
        ins, outs = refs[:len(flat)], refs[len(flat):]

        def over_devices(row):
            tot = sg_ref[0, row:row + 1, :]
            for s in range(1, N_DEV):
                tot = tot + sg_ref[s, row:row + 1, :]
            return tot

        g_b_ada = dm_ref[0:1, :]
        for r in range(1, N_DEV * BL):
            g_b_ada = g_b_ada + dm_ref[r:r + 1, :]
        g_rel = rp_ref[0]
        for s in range(1, N_DEV):
            g_rel = g_rel + rp_ref[s]
        grads = [g_b_ada, over_devices(0), g_rel, over_devices(1), over_devices(2)]
        outs[0][...] = jnp.sum(over_devices(3), axis=1, keepdims=True)
        for p, g in enumerate(grads):
            w_ref, m_ref, v_ref = ins[3 * p:3 * p + 3]
            g_ref, d_ref, nm_ref, nv_ref = outs[1 + 4 * p:5 + 4 * p]
            g_ref[...] = g
            d_ref[...], nm_ref[...], nv_ref[...] = _adam_step(g, w_ref[...], m_ref[...], v_ref[...])

    out_shape = [jax.ShapeDtypeStruct((1, 1), F32)]
    for w, _, _ in params:
        out_shape += [jax.ShapeDtypeStruct(w.shape, F32)] * 4
    res = pl.pallas_call(body, name="small_updates", out_shape=tuple(out_shape))(small_g, dmod_all, rel_parts, *flat)
    return res[0], [res[1 + 4 * p:5 + 4 * p] for p in range(len(params))]


def _attn_fwd_dense(proj, bias):
    nq = 4
    rows = nq * QB
    nsb = S // rows

    def body(q_ref, k_ref, v_ref, kp_ref, vp_ref, b_ref, o_ref, l_ref, ls0, ls1, ls2, ls3):
        ls = [ls0, ls1, ls2, ls3]
        n = pl.program_id(1)
        lane = lax.broadcasted_iota(jnp.int32, (QB, 128), 1)
        units = [(h, j) for h in range(4) for j in range(nq)]

        def keys(cur_ref, prev_ref, h, j):
            sl = slice(h * HD, (h + 1) * HD)
            if j == 0:
                return jnp.concatenate([prev_ref[:, sl], cur_ref[0:QB, sl]], axis=0)
            return cur_ref[(j - 1) * QB:(j + 1) * QB, sl]

        q = jnp.stack([q_ref[j * QB:(j + 1) * QB, h * HD:(h + 1) * HD] for h, j in units])
        k = jnp.stack([keys(k_ref, kp_ref, h, j) for h, j in units])
        v = jnp.stack([keys(v_ref, vp_ref, h, j) for h, j in units])
        bias_b = jnp.stack([b_ref[jnp.minimum(n, 1), h] if j == 0 else b_ref[1, h] for h, j in units])
        s = jnp.einsum("uqd,ukd->uqk", q, k, preferred_element_type=F32) * SCALE + bias_b
        m = jnp.max(s, axis=-1, keepdims=True)
        p = jnp.exp(s - m)
        l = jnp.sum(p, axis=-1, keepdims=True)
        o = jnp.einsum("uqk,ukd->uqd", p.astype(BF16), v, preferred_element_type=F32) / l
        lse = m + jnp.log(l)
        for i, (h, j) in enumerate(units):
            o_ref[j * QB:(j + 1) * QB, h * HD:(h + 1) * HD] = o[i]
            ls[h][j * QB:(j + 1) * QB, :] = jnp.where(lane == h, lse[i], 0.0)
        l_ref[...] = (ls[0][...] + ls[1][...]) + (ls[2][...] + ls[3][...])

    def row(b, n):
        return b * nsb + n

    def prev(b, n):
        return jnp.maximum((b * nsb + n) * nq - 1, 0)

    in_specs = [
        pl.BlockSpec((rows, GW), lambda b, n: (row(b, n), CB_Q)),
        pl.BlockSpec((rows, GW), lambda b, n: (row(b, n), CB_K)),
        pl.BlockSpec((rows, GW), lambda b, n: (row(b, n), CB_V)),
        pl.BlockSpec((QB, GW), lambda b, n: (prev(b, n), CB_K)),
        pl.BlockSpec((QB, GW), lambda b, n: (prev(b, n), CB_V)),
        pl.BlockSpec((None, 2, 4, QB, 2 * QB), lambda b, n: (0, 0, 0, 0, 0)),
    ]
    return pl.pallas_call(
        body, name="attn_fwd0",
        grid=(BL, nsb),
        in_specs=in_specs,
        out_specs=(pl.BlockSpec((rows, GW), lambda b, n: (row(b, n), 0)),
                   pl.BlockSpec((rows, 128), lambda b, n: (row(b, n), 0))),
        out_shape=(jax.ShapeDtypeStruct((T, GW), F32), jax.ShapeDtypeStruct((T, 128), F32)),
        scratch_shapes=[pltpu.VMEM((rows, 128), F32)] * 4,
        compiler_params=pltpu.CompilerParams(vmem_limit_bytes=VMEM_LIMIT),
    )(proj, proj, proj, proj, proj, bias)


def _attn_bwd_dense(proj, d_out, stats, bias, dproj):
    nq = 4
    rows = nq * QB
    nsb = S // rows
    cols_q, cols_k, cols_v = CB * CB_Q, CB * CB_K, CB * CB_V

    def body(q_ref, k_ref, v_ref, do_ref, st_ref, kp_ref, vp_ref, b_ref, _, out_ref, db_ref,
             sq, sk, sv, carry, sems):
        b, n = pl.program_id(0), pl.program_id(1)
        units = [(h, j) for h in range(4) for j in range(nq)]

        @pl.when((b == 0) & (n == 0))
        def _():
            db_ref[...] = jnp.zeros_like(db_ref)

        @pl.when(n == 0)
        def _():
            carry[...] = jnp.zeros_like(carry)

        step = b * (nsb + 1) + n
        slot = step % 2

        def block_copies(s, position, block):
            part = pl.ds(position * QB, QB)
            return _column_copies([(sq.at[s, part], cols_q), (sk.at[s, part], cols_k), (sv.at[s, part], cols_v)],
                                  out_ref, pl.multiple_of(block * QB, QB), sems.at[s, position])

        def wait_blocks(s, positions):
            for position in positions:
                for cp in block_copies(s, position, 0):
                    cp.wait()

        before = (step - 2) % (nsb + 1)

        @pl.when((step >= 2) & (before > 0))
        def _():
            wait_blocks(slot, [0])

        @pl.when((step >= 2) & (before < nsb))
        def _():
            wait_blocks(slot, range(1, nq))

        def write(first_block, position, count):
            for j in range(count):
                for cp in block_copies(slot, position + j, first_block + j):
                    cp.start()

        @pl.when(n == nsb)
        def _():
            sq[slot, 0:QB, :] = carry[:, 0:GW].astype(BF16)
            sk[slot, 0:QB, :] = carry[:, GW:2 * GW].astype(BF16)
            sv[slot, 0:QB, :] = carry[:, 2 * GW:3 * GW].astype(BF16)
            write((b + 1) * nsb * nq - 1, 0, 1)

            @pl.when(b == BL - 1)
            def _():
                wait_blocks(slot, [0])
                wait_blocks(1 - slot, range(nq))

        @pl.when(n < nsb)
        def _():
            def keys(cur_ref, prev_ref, h, j):
                sl = slice(h * HD, (h + 1) * HD)
                if j == 0:
                    return jnp.concatenate([prev_ref[:, sl], cur_ref[0:QB, sl]], axis=0)
                return cur_ref[(j - 1) * QB:(j + 1) * QB, sl]

            def block(ref, h, j):
                return ref[j * QB:(j + 1) * QB, h * HD:(h + 1) * HD]

            q = jnp.stack([block(q_ref, h, j) for h, j in units])
            do = jnp.stack([block(do_ref, h, j) for h, j in units])
            k = jnp.stack([keys(k_ref, kp_ref, h, j) for h, j in units])
            v = jnp.stack([keys(v_ref, vp_ref, h, j) for h, j in units])
            bias_b = jnp.stack([b_ref[jnp.minimum(n, 1), h] if j == 0 else b_ref[1, h] for h, j in units])
            lse = jnp.stack([st_ref[j * QB:(j + 1) * QB, h:h + 1] for h, j in units])
            delta = jnp.stack([st_ref[j * QB:(j + 1) * QB, 4 + h:5 + h] for h, j in units])
            s = jnp.einsum("uqd,ukd->uqk", q, k, preferred_element_type=F32) * SCALE + bias_b
            p = jnp.exp(s - lse)
            ds = p * (jnp.einsum("uqd,ukd->uqk", do, v, preferred_element_type=F32) - delta)
            for h in range(4):
                tot = ds[h * nq]
                for j in range(1, nq):
                    tot = tot + ds[h * nq + j]
                db_ref[h] += tot
            dsb, pb = ds.astype(BF16), p.astype(BF16)
            dq = jnp.einsum("uqk,ukd->uqd", dsb, k, preferred_element_type=F32) * SCALE
            dk = jnp.einsum("uqk,uqd->ukd", dsb, q, preferred_element_type=F32) * SCALE
            dv = jnp.einsum("uqk,uqd->ukd", pb, do, preferred_element_type=F32)
            for h in range(4):
                sl = slice(h * HD, (h + 1) * HD)
                u0, last = h * nq, h * nq + nq - 1
                sq[slot, 0:QB, sl] = carry[:, sl].astype(BF16)
                sk[slot, 0:QB, sl] = (carry[:, GW + h * HD:GW + (h + 1) * HD] + dk[u0, :QB]).astype(BF16)
                sv[slot, 0:QB, sl] = (carry[:, 2 * GW + h * HD:2 * GW + (h + 1) * HD] + dv[u0, :QB]).astype(BF16)
                for j in range(nq - 1):
                    pos = slice((j + 1) * QB, (j + 2) * QB)
                    sq[slot, pos, sl] = dq[u0 + j].astype(BF16)
                    sk[slot, pos, sl] = (dk[u0 + j, QB:] + dk[u0 + j + 1, :QB]).astype(BF16)
                    sv[slot, pos, sl] = (dv[u0 + j, QB:] + dv[u0 + j + 1, :QB]).astype(BF16)
                carry[:, sl] = dq[last]
                carry[:, GW + h * HD:GW + (h + 1) * HD] = dk[last, QB:]
                carry[:, 2 * GW + h * HD:2 * GW + (h + 1) * HD] = dv[last, QB:]

            @pl.when(n == 0)
            def _():
                write(b * nsb * nq, 1, nq - 1)

            @pl.when(n > 0)
            def _():
                write((b * nsb + n) * nq - 1, 0, nq)

    def row(b, n):
        return b * nsb + jnp.minimum(n, nsb - 1)

    def prev(b, n):
        return jnp.maximum(row(b, n) * nq - 1, 0)

    in_specs = [
        pl.BlockSpec((rows, GW), lambda b, n: (row(b, n), CB_Q)),
        pl.BlockSpec((rows, GW), lambda b, n: (row(b, n), CB_K)),
        pl.BlockSpec((rows, GW), lambda b, n: (row(b, n), CB_V)),
        pl.BlockSpec((rows, GW), lambda b, n: (row(b, n), 0)),
        pl.BlockSpec((rows, 128), lambda b, n: (row(b, n), 0)),
        pl.BlockSpec((QB, GW), lambda b, n: (prev(b, n), CB_K)),
        pl.BlockSpec((QB, GW), lambda b, n: (prev(b, n), CB_V)),
        pl.BlockSpec((None, 2, 4, QB, 2 * QB), lambda b, n: (0, 0, 0, 0, 0)),
        pl.BlockSpec(memory_space=pl.ANY),
    ]
    return pl.pallas_call(
        body, name="attn_bwd0",
        grid=(BL, nsb + 1),
        in_specs=in_specs,
        out_specs=(pl.BlockSpec(memory_space=pl.ANY),
                   pl.BlockSpec((4, QB, 2 * QB), lambda b, n: (0, 0, 0))),
        out_shape=(jax.ShapeDtypeStruct((T, NCOL), BF16),
                   jax.ShapeDtypeStruct((4, QB, 2 * QB), F32)),
        scratch_shapes=[pltpu.VMEM((2, rows, GW), BF16)] * 3
                       + [pltpu.VMEM((QB, 3 * GW), F32), pltpu.SemaphoreType.DMA((2, nq, 3))],
        input_output_aliases={8: 0},
        compiler_params=pltpu.CompilerParams(vmem_limit_bytes=VMEM_LIMIT),
    )(proj, proj, proj, d_out, stats, proj, proj, bias, dproj)


def _attention_forward(proj, rel_bias):
    expand_lanes, grad_lanes, masks = (jnp.asarray(t) for t in _bucket_maps())
    bias = _bias_expand(rel_bias, expand_lanes, masks)
    fwd = [_attn_fwd_dense(proj, bias)] + [_attn_fwd(proj, bias, g) for g in (1, 2)]
    return bias, grad_lanes, [f[0] for f in fwd], [f[1] for f in fwd]


def _local_step(x2, tgt2, mod3, h, proj, attn, w_ao, w_co, w_o, conv_w, conv_b, ln_g, ln_b):
    bias, buckets, o_g, lse_g = attn

    (dproj, dyc, d_o, stats, dxd, gw_o, gw_co, gw_ao, tail_vec) = _tail(
        x2, tgt2, mod3, o_g, lse_g, proj, w_ao, w_co, w_o, conv_w, conv_b, ln_g, ln_b)

    dproj, db = _attn_bwd_dense(proj, d_o, stats, bias, dproj)
    dbias = [db]
    for g in (1, 2):
        dproj, db = _attn_bwd(proj, d_o, stats, bias, dproj, g)
        dbias.append(db)
    g_rel_bias = _bias_grad(*dbias, buckets)
    dproj, conv_vec = _conv_bwd(dyc, proj, conv_w, dproj)

    gw_ao = jnp.transpose(gw_ao.reshape(GW, N_DEV, D // N_DEV), (1, 0, 2))
    return dproj, dxd, gw_ao, gw_co, gw_o, conv_vec, g_rel_bias, tail_vec


def kernel(x, c, w_ada, b_ada, w_in, conv_w, conv_b, rel_bias, w_attn_out, w_conv_out, w_o, ln_g, ln_b, loss_target, m_w_ada, m_b_ada, m_w_in, m_conv_w, m_conv_b, m_rel_bias, m_w_attn_out, m_w_conv_out, m_w_o, m_ln_g, m_ln_b, v_w_ada, v_b_ada, v_w_in, v_conv_w, v_conv_b, v_rel_bias, v_w_attn_out, v_w_conv_out, v_w_o, v_ln_g, v_ln_b):
    me = _my_index()
    x2 = x.reshape(T, D)
    tgt2 = loss_target.reshape(T, D)

    b_cols = lax.dynamic_slice(b_ada, (0, me * ADA_SHARD), (1, ADA_SHARD))
    c_g, mod_in = _mod_exchange(jnp.pad(c, ((0, 8 - BL), (0, 0))), w_ada[0], b_cols)
    c_all = c_g[:, 0:BL, :].reshape(N_DEV * BL, D)
    mod3 = jnp.transpose(mod_in[:, 0:BL, :], (1, 0, 2)).reshape(BL, 3, D)

    rows_shape = jax.ShapeDtypeStruct((N_DEV, D // N_DEV, D), BF16)
    proj, h, w_in_all, (w_ao_g, w_co_g, w_o_g, conv_w_g) = _gather_proj(
        _shard_order(), x2, mod3, w_in[0].astype(BF16), 1024,
        ([w_attn_out[0].astype(BF16), w_conv_out[0].astype(BF16), w_o[0].astype(BF16), conv_w[0]],
         [jax.ShapeDtypeStruct((N_DEV, GW, D // N_DEV), BF16), rows_shape, rows_shape,
          jax.ShapeDtypeStruct((N_DEV, 3, D // N_DEV), F32)]))

    attn = _attention_forward(proj, rel_bias)
    w_ao_full = jnp.transpose(w_ao_g, (1, 0, 2)).reshape(GW, D)
    w_co_full = w_co_g.reshape(D, D)
    w_o_full = w_o_g.reshape(D, D)
    conv_w_full = jnp.transpose(conv_w_g, (1, 0, 2)).reshape(3, D)

    (dproj, dxd, gw_ao, gw_co, gw_o, conv_vec, g_rel_bias, tail_vec) = _local_step(
        x2, tgt2, mod3, h, proj, attn, w_ao_full, w_co_full, w_o_full,
        conv_w_full, conv_b, ln_g, ln_b)

    g_conv_w_blocks = jnp.transpose(conv_vec[0:3].reshape(3, N_DEV, D // N_DEV), (1, 0, 2))
    partials = [gw_ao, gw_co.reshape(N_DEV, D // N_DEV, D), gw_o.reshape(N_DEV, D // N_DEV, D), g_conv_w_blocks]
    w_in_sums, w_in_parts, sib = _gw_in_pair(
        _slice_order(), h, dproj, partials,
        [jax.ShapeDtypeStruct((4, GW, D // N_DEV), BF16),
         jax.ShapeDtypeStruct((4, D // N_DEV, D), BF16),
         jax.ShapeDtypeStruct((4, D // N_DEV, D), BF16),
         jax.ShapeDtypeStruct((4, 3, D // N_DEV), F32)])
    core = lax.axis_index("c").astype(jnp.int32).reshape(1)
    chip_sums = [w_in_sums] + list(_pair_add(core, partials, sib))
    hops = [(3,)] + [(1, 2, 3)] * 4
    grad_x, mod_vec, (r_in, r_ao, r_co, r_o, r_cw) = _dh_dx(
        dproj, w_in_all, x2, dxd, mod3, chip_sums, hops, w_in_parts)

    small = jnp.concatenate([
        tail_vec[0:4],
        jnp.pad(g_rel_bias.reshape(1, N_BUCKETS * N_HEADS), ((0, 0), (0, D - N_BUCKETS * N_HEADS))),
        jnp.zeros((3, D), F32)], axis=0)
    dmod = jnp.concatenate([mod_vec[0:2], mod_vec[2:4], tail_vec[4:6]], axis=1)
    small_g, dmod_g = _all_gather(
        [small, dmod],
        [jax.ShapeDtypeStruct((N_DEV, 8, D), F32), jax.ShapeDtypeStruct((N_DEV, BL, 3 * D), F32)],
        "gather_small")
    dmod_all = dmod_g.reshape(N_DEV * BL, 3 * D)
    small_names = ["b_ada", "conv_b", "rel_bias", "ln_g", "ln_b"]
    small_params = [(b_ada, m_b_ada, v_b_ada), (conv_b, m_conv_b, v_conv_b), (rel_bias, m_rel_bias, v_rel_bias),
                    (ln_g, m_ln_g, v_ln_g), (ln_b, m_ln_b, v_ln_b)]
    loss, small_res = _small_updates(
        small_g, dmod_all, small_g[:, 4, :N_BUCKETS * N_HEADS].reshape(N_DEV, N_BUCKETS, N_HEADS), small_params)
    loss = loss.reshape(())

    dmod_cols = lax.dynamic_slice(dmod_all, (0, me * ADA_SHARD), (N_DEV * BL, ADA_SHARD))
    res = {
        "w_ada": tuple(t[None] for t in _w_ada_update(jnp.transpose(c_all), dmod_cols,
                                                      w_ada[0], m_w_ada[0], v_w_ada[0])),
        "w_in": tuple(t[None] for t in _adamw(r_in, w_in[0], m_w_in[0], v_w_in[0], "adam_w_in", 128)),
    }
    mid_names = ["conv_w", "w_attn_out", "w_conv_out", "w_o"]
    mid_parts = [r_cw, r_ao, r_co, r_o]
    mid_full = [(conv_w, m_conv_w, v_conv_w), (w_attn_out, m_w_attn_out, v_w_attn_out),
                (w_conv_out, m_w_conv_out, v_w_conv_out), (w_o, m_w_o, v_w_o)]
    mid_res = _multi_adamw(mid_parts, [tuple(t[0] for t in wmv) for wmv in mid_full], "adam_mid")
    for nm, wmv, outs4 in zip(mid_names, mid_full, mid_res):
        res[nm] = tuple(t[None] for t in outs4)
    res.update(dict(zip(small_names, small_res)))
    order = ["w_ada", "b_ada", "w_in", "conv_w", "conv_b", "rel_bias", "w_attn_out", "w_conv_out",
             "w_o", "ln_g", "ln_b"]
    outs = [loss, grad_x.reshape(BL, S, D)]
    for k in range(4):
        outs += [res[name][k] for name in order]
    return tuple(outs)
```

```python
import math

import numpy as np
import jax
import jax.numpy as jnp
from jax import lax
from jax.experimental import pallas as pl
from jax.experimental.pallas import tpu as pltpu

F32 = jnp.float32
BF16 = jnp.bfloat16
MESH = pl.DeviceIdType.MESH

N_DEV = 8
D = 1024
S = 2048
BL = 2
T = BL * S
NCOL = 11264
SHARD = NCOL // N_DEV
CB = 512
NCB = NCOL // CB
HD = 128
GW = 512
QB = 128
DILATIONS = (1, 4, 16)
N_STEPS = 128
N_BUCKETS = 32
N_HEADS = 12
ALPHA = 2.0 ** 0.25
LN_EPS = 1e-5
NEG_INF = -1e30
SCALE = HD ** -0.5
ADA_SHARD = 3 * D // N_DEV

CB_Q, CB_K, CB_V, CB_GA = 0, 3, 6, 9
KB_U, KB_BG, KB_CG, KB_GC, KB_MA, KB_MC = 5, 6, 7, 8, 9, 10

ADAM_LR, ADAM_B1, ADAM_B2, ADAM_EPS, ADAM_WD, ADAM_STEP = 0.001, 0.9, 0.999, 1e-08, 0.01, 10

VMEM_LIMIT = 56 * 1024 * 1024
VMEM_LIMIT_TAIL = 62 * 1024 * 1024


def _dot(a, b):
    return jnp.dot(a, b, preferred_element_type=F32)


def _dot_nt(a, b):
    return lax.dot_general(a, b, (((1,), (1,)), ((), ())), preferred_element_type=F32)


def _dot_tn(a, b):
    return lax.dot_general(a, b, (((0,), (0,)), ((), ())), preferred_element_type=F32)


def _sigmoid(v):
    return 1.0 / (1.0 + jnp.exp(-v))


def _column_copies(pieces, dst_hbm, row0, sems):
    copies = []
    for k, (src, col0) in enumerate(pieces):
        rows, width = src.shape
        copies.append(pltpu.make_async_copy(
            src, dst_hbm.at[pl.ds(row0, rows), pl.ds(col0, width)], sems.at[k]))
    return copies


def _my_index():
    return 4 * lax.axis_index("x") + 2 * lax.axis_index("y") + lax.axis_index("c")


class _Gather:
    def __init__(self, ins, outs, stage, send_sems, recv_sems, local_sems):
        self.ins, self.outs, self.stage = ins, outs, stage
        self.send_sems, self.recv_sems, self.local_sems = send_sems, recv_sems, local_sems
        x, y, c = lax.axis_index("x"), lax.axis_index("y"), lax.axis_index("c")
        self.c = c
        self.me, self.sibling = (x, y, c), (x, y, 1 - c)
        self.chips = [(1 - x, y), (x, 1 - y), (1 - x, 1 - y)]

    @staticmethod
    def scratch(arrs):
        n = len(arrs)
        return ([pltpu.SemaphoreType.DMA((7 * n,)), pltpu.SemaphoreType.DMA((7 * n,)),
                 pltpu.SemaphoreType.DMA((n,))] + [pltpu.VMEM(a.shape, a.dtype) for a in arrs])

    def _copy(self, a, k, block, to, src=None):
        dst = self.outs[a].at[4 * block[0] + 2 * block[1] + block[2]]
        return pltpu.make_async_remote_copy(
            src_ref=dst if src is None else src, dst_ref=dst,
            send_sem=self.send_sems.at[a * 7 + k], recv_sem=self.recv_sems.at[a * 7 + k],
            device_id=to, device_id_type=MESH)

    def _first(self):
        first = []
        for a in range(len(self.ins)):
            first.append(self._copy(a, 0, self.me, self.sibling, src=self.ins[a]))
            first += [self._copy(a, 1 + j, self.me, (*chip, self.c), src=self.ins[a])
                      for j, chip in enumerate(self.chips)]
        return first

    def _mine(self):
        me = self.me
        return [pltpu.make_async_copy(self.stage[a], self.outs[a].at[4 * me[0] + 2 * me[1] + me[2]],
                                      self.local_sems.at[a]) for a in range(len(self.ins))]

    def begin(self):
        for cp in self._first():
            cp.start()
        loads = [pltpu.make_async_copy(self.ins[a], self.stage[a], self.local_sems.at[a])
                 for a in range(len(self.ins))]
        for cp in loads:
            cp.start()
        for cp in loads:
            cp.wait()
        for cp in self._mine():
            cp.start()

    def finish(self):
        n, c, me, sibling = len(self.ins), self.c, self.me, self.sibling
        passed = []
        for j, chip in enumerate(self.chips):
            for a in range(n):
                self._copy(a, 1 + j, (*chip, c), me).wait_recv()
                fwd = self._copy(a, 4 + j, (*chip, c), sibling)
                fwd.start()
                passed.append(fwd)
        for a in range(n):
            self._copy(a, 0, sibling, me).wait_recv()
        for j, chip in enumerate(self.chips):
            for a in range(n):
                self._copy(a, 4 + j, (*chip, 1 - c), me).wait_recv()
        for cp in self._first() + passed:
            cp.wait_send()
        for cp in self._mine():
            cp.wait()


def _all_gather(arrs, out_shapes, name):
    n = len(arrs)

    def body(*refs):
        g = _Gather(refs[:n], refs[n:2 * n], refs[2 * n + 3:], *refs[2 * n:2 * n + 3])
        g.begin()
        g.finish()

    any_spec = pl.BlockSpec(memory_space=pl.ANY)
    return pl.pallas_call(
        body, name=name,
        out_shape=tuple(out_shapes),
        in_specs=[any_spec] * n,
        out_specs=tuple([any_spec] * n),
        scratch_shapes=_Gather.scratch(arrs),
    )(*arrs)


def _neighbour_chips():
    x, y, c = lax.axis_index("x"), lax.axis_index("y"), lax.axis_index("c")
    first = (jnp.where(c == 0, 1 - x, x), jnp.where(c == 0, y, 1 - y))
    second = (jnp.where(c == 0, x, 1 - x), jnp.where(c == 0, 1 - y, y))
    return first, second, (1 - x, 1 - y)


def _slice_order():
    x, y, c = lax.axis_index("x"), lax.axis_index("y"), lax.axis_index("c")
    nb1, nb2, diag = _neighbour_chips()
    slots = []
    for mine, theirs in ((nb1, nb2), (nb2, nb1), (diag, diag), ((x, y), (x, y))):
        slots += [2 * (2 * theirs[0] + theirs[1]) + 1 - c, 2 * (2 * mine[0] + mine[1]) + c]
    return jnp.stack(slots).astype(jnp.int32)


def _gw_in_pair(order, h, dproj, smalls, small_shapes4):
    kk, m = h.shape
    tk = min(kk, 2048)
    nk = kk // tk
    ncols = dproj.shape[1] // N_DEV
    n = len(smalls)

    def body(order_ref, h_ref, d_ref, *rest):
        ins = rest[:n]
        sums_hbm, parts_hbm = rest[n], rest[n + 1]
        sib = rest[n + 2:2 * n + 2]
        (acc, sendbuf, recvbuf, sumbuf, send_sems, recv_sems, local_sem, ssend, srecv,
         isend, irecv) = rest[2 * n + 2:]
        js, k = pl.program_id(0), pl.program_id(1)
        x, y, c = lax.axis_index("x"), lax.axis_index("y"), lax.axis_index("c")
        sibling = (x, y, 1 - c)
        my_chip = 2 * x + y
        nb1, nb2, _ = _neighbour_chips()
        near = [(*nb1, c), (*nb2, c)]

        def ici_copy(p, out_chip):
            peer = near[p] if isinstance(p, int) else tuple(jnp.where(p == 0, a, b) for a, b in zip(*near))
            return pltpu.make_async_remote_copy(
                src_ref=sumbuf.at[p], dst_ref=parts_hbm.at[out_chip],
                send_sem=isend.at[p], recv_sem=irecv.at[p], device_id=peer, device_id_type=MESH)

        def small_copies():
            return [pltpu.make_async_remote_copy(
                        src_ref=ins[a].at[2 * q + 1 - c], dst_ref=sib[a].at[q],
                        send_sem=ssend.at[a * 4 + q], recv_sem=srecv.at[a * 4 + q],
                        device_id=sibling, device_id_type=MESH)
                    for a in range(n) for q in range(4)]

        def slice_copy(p):
            return pltpu.make_async_remote_copy(
                src_ref=sendbuf, dst_ref=recvbuf.at[p], send_sem=send_sems.at[p], recv_sem=recv_sems.at[p],
                device_id=sibling, device_id_type=MESH)

        def sum_copy(p):
            return pltpu.make_async_copy(sumbuf.at[2], sums_hbm.at[order_ref[2 * p] // 2], local_sem)

        @pl.when((js == 0) & (k == 0))
        def _():
            for cp in small_copies():
                cp.start()

        def partial():
            return _dot_tn(h_ref[...], d_ref[...])

        if nk > 1:
            @pl.when(k == 0)
            def _():
                acc[...] = partial()
        if nk > 2:
            @pl.when((k > 0) & (k < nk - 1))
            def _():
                acc[...] += partial()

        def total():
            return partial() + acc[...] if nk > 1 else partial()

        p = js // 2

        @pl.when((js % 2 == 0) & (k == nk - 1))
        def _():
            @pl.when(p > 0)
            def _():
                slice_copy(p - 1).wait_send()
            sendbuf[...] = total().astype(BF16)
            slice_copy(p).start()

        @pl.when((js % 2 == 1) & (k == nk - 1))
        def _():
            slice_copy(p).wait_recv()

            @pl.when(p == 3)
            def _():
                sum_copy(2).wait()
            sumbuf[jnp.minimum(p, 2)] = (total() + recvbuf[p].astype(F32)).astype(BF16)

            @pl.when(p < 2)
            def _():
                ici_copy(p, my_chip).start()

            @pl.when(p >= 2)
            def _():
                sum_copy(p).start()

        @pl.when((js == N_DEV - 1) & (k == nk - 1))
        def _():
            slice_copy(3).wait_send()
            sum_copy(3).wait()
            for cp in small_copies():
                cp.wait()
            for p in range(2):
                ici_copy(p, 2 * near[p][0] + near[p][1]).wait_recv()
                ici_copy(p, my_chip).wait_send()

    any_spec = pl.BlockSpec(memory_space=pl.ANY)
    res = pl.pallas_call(
        body, name="gw_in_pair",
        grid_spec=pltpu.PrefetchScalarGridSpec(
            num_scalar_prefetch=1,
            grid=(N_DEV, nk),
            in_specs=[pl.BlockSpec((tk, m), lambda js, k, order_ref: (k, 0)),
                      pl.BlockSpec((tk, ncols), lambda js, k, order_ref: (k, order_ref[js]))] + [any_spec] * n,
            out_specs=(any_spec,) * (n + 2),
            scratch_shapes=[pltpu.VMEM((m, ncols), F32), pltpu.VMEM((m, ncols), BF16),
                            pltpu.VMEM((4, m, ncols), BF16), pltpu.VMEM((3, m, ncols), BF16),
                            pltpu.SemaphoreType.DMA((4,)), pltpu.SemaphoreType.DMA((4,)),
                            pltpu.SemaphoreType.DMA,
                            pltpu.SemaphoreType.DMA((4 * n,)), pltpu.SemaphoreType.DMA((4 * n,)),
                            pltpu.SemaphoreType.DMA((2,)), pltpu.SemaphoreType.DMA((2,))]),
        out_shape=(jax.ShapeDtypeStruct((4, m, ncols), BF16),) * 2 + tuple(small_shapes4),
        compiler_params=pltpu.CompilerParams(vmem_limit_bytes=VMEM_LIMIT),
    )(order, h, dproj, *smalls)
    return res[0], res[1], res[2:]


def _chip_copies(ins, outs, send_sems, recv_sems, local_sems, hops):
    n = len(ins)
    x, y, c = lax.axis_index("x"), lax.axis_index("y"), lax.axis_index("c")
    my_chip = 2 * x + y

    def peer_of(k):
        return ((1 - x) if (k >> 1) & 1 else x, (1 - y) if k & 1 else y, c)

    def copy(a, k, out_chip):
        peer = peer_of(k)
        return pltpu.make_async_remote_copy(
            src_ref=ins[a].at[2 * peer[0] + peer[1]], dst_ref=outs[a].at[out_chip],
            send_sem=send_sems.at[a * 3 + k - 1], recv_sem=recv_sems.at[a * 3 + k - 1],
            device_id=peer, device_id_type=MESH)

    sends = [copy(a, k, my_chip) for k in range(1, 4) for a in range(n) if k in hops[a]]
    arrivals = []
    for k in range(1, 4):
        peer = peer_of(k)
        arrivals += [copy(a, k, 2 * peer[0] + peer[1]) for a in range(n) if k in hops[a]]
    mine = [pltpu.make_async_copy(ins[a].at[my_chip], outs[a].at[my_chip], local_sems.at[a])
            for a in range(n)]
    return sends, arrivals, mine


def _pair_add(core, mines, theirs):
    n = len(mines)

    def body(core_ref, *refs):
        mine, sib, outs = refs[:n], refs[n:2 * n], refs[2 * n:]
        for a in range(n):
            for q in range(4):
                outs[a][q] = (mine[a][2 * q + core_ref[0]].astype(F32)
                              + sib[a][q].astype(F32)).astype(outs[a].dtype)

    return pl.pallas_call(
        body, name="pair_add",
        in_specs=[pl.BlockSpec(memory_space=pltpu.SMEM)] + [pl.BlockSpec(memory_space=pltpu.VMEM)] * (2 * n),
        out_shape=tuple(jax.ShapeDtypeStruct(t.shape, t.dtype) for t in theirs),
    )(core, *mines, *theirs)


def _mod_exchange(c8, w_ada, b_cols):
    cols = w_ada.shape[1]

    def body(c_ref, w_ref, b_ref, call_ref, mod_ref, msend, send1, recv1, send2, recv2):
        x, y, c = lax.axis_index("x"), lax.axis_index("y"), lax.axis_index("c")
        my_slot = 4 * x + 2 * y + c

        def peer_of(k):
            return ((1 - x) if (k >> 2) & 1 else x, (1 - y) if (k >> 1) & 1 else y, (1 - c) if k & 1 else c)

        def slot_of(dev):
            return 4 * dev[0] + 2 * dev[1] + dev[2]

        def exchange(src_of, dst_ref, send_sems, recv_sems):
            sends, arrivals = [], []
            for k in range(1, 8):
                peer = peer_of(k)
                sends.append(pltpu.make_async_remote_copy(
                    src_ref=src_of(slot_of(peer)), dst_ref=dst_ref.at[my_slot],
                    send_sem=send_sems.at[k - 1], recv_sem=recv_sems.at[k - 1],
                    device_id=peer, device_id_type=MESH))
                arrivals.append(pltpu.make_async_remote_copy(
                    src_ref=src_of(my_slot), dst_ref=dst_ref.at[slot_of(peer)],
                    send_sem=send_sems.at[k - 1], recv_sem=recv_sems.at[k - 1],
                    device_id=peer, device_id_type=MESH))
            for cp in sends:
                cp.start()
            for cp in arrivals:
                cp.wait_recv()
            for cp in sends:
                cp.wait_send()

        call_ref[my_slot] = c_ref[...]
        exchange(lambda s: c_ref, call_ref, send1, recv1)
        cv = call_ref[...].reshape(N_DEV * 8, c_ref.shape[1])
        act = cv * _sigmoid(cv)
        mod = jnp.dot(act, w_ref[...], preferred_element_type=F32,
                      precision=lax.Precision.HIGHEST) + b_ref[...]
        msend[...] = mod.reshape(N_DEV, 8, cols)
        mod_ref[my_slot] = msend[my_slot]
        exchange(lambda s: msend.at[s], mod_ref, send2, recv2)

    return pl.pallas_call(
        body, name="mod_exchange",
        out_shape=(jax.ShapeDtypeStruct((N_DEV, 8, c8.shape[1]), F32),
                   jax.ShapeDtypeStruct((N_DEV, 8, cols), F32)),
        scratch_shapes=[pltpu.VMEM((N_DEV, 8, cols), F32)] + [pltpu.SemaphoreType.DMA((7,))] * 4,
    )(c8, w_ada, b_cols)


def _w_ada_update(c_all_t, dmod_cols, w, m, v):
    rows, cols = w.shape
    tr = 256

    def body(c_ref, d_ref, w_ref, m_ref, v_ref, g_ref, dl_ref, nm_ref, nv_ref):
        cv = c_ref[...]
        g = jnp.dot(cv * _sigmoid(cv), d_ref[...], preferred_element_type=F32,
                    precision=lax.Precision.HIGHEST)
        g_ref[...] = g
        dl_ref[...], nm_ref[...], nv_ref[...] = _adam_step(g, w_ref[...], m_ref[...], v_ref[...])

    blk = pl.BlockSpec((tr, cols), lambda i: (i, 0))
    shp = jax.ShapeDtypeStruct((rows, cols), F32)
    return pl.pallas_call(
        body, name="adam_w_ada",
        grid=(rows // tr,),
        in_specs=[pl.BlockSpec((tr, c_all_t.shape[1]), lambda i: (i, 0)),
                  pl.BlockSpec(dmod_cols.shape, lambda i: (0, 0)), blk, blk, blk],
        out_specs=(blk, blk, blk, blk),
        out_shape=(shp, shp, shp, shp),
    )(c_all_t, dmod_cols, w, m, v)


def _shard_order():
    x, y, c = lax.axis_index("x"), lax.axis_index("y"), lax.axis_index("c")
    first, second, diag = _neighbour_chips()
    devs = [(x, y, c), (x, y, 1 - c), (*first, c), (*second, 1 - c), (*second, c), (*first, 1 - c),
            (*diag, c), (*diag, 1 - c)]
    return jnp.stack([4 * d[0] + 2 * d[1] + d[2] for d in devs]).astype(jnp.int32)


def _gather_proj(order, x2, mod3, w_shard, tm, ride=()):
    rows, kdim = x2.shape
    ncols = w_shard.shape[1]
    n_i = rows // tm
    per_seq = n_i // mod3.shape[0]
    ride_arrs, ride_shapes = ride if ride else ((), ())
    n_ride = len(ride_arrs)

    def body(order_ref, x_ref, mod_ref, mine_hbm, *rest):
        ride_ins = rest[:n_ride]
        o_ref, h_ref, all_hbm = rest[n_ride:n_ride + 3]
        ride_outs = rest[n_ride + 3:2 * n_ride + 3]
        wv, send_sems, recv_sems, local_sems, hs = rest[2 * n_ride + 3:2 * n_ride + 8]
        ride_scr = rest[2 * n_ride + 8:]
        j, i = pl.program_id(0), pl.program_id(1)
        c = lax.axis_index("c")
        me, sibling = (lax.axis_index("x"), lax.axis_index("y"), c), (lax.axis_index("x"), lax.axis_index("y"), 1 - c)
        nb1, nb2, diag = _neighbour_chips()

        def slot(dev):
            return 4 * dev[0] + 2 * dev[1] + dev[2]

        def copy(k, block, to, src=None, part=None):
            buf = wv.at[slot(block)]
            if part is not None:
                buf = buf.at[pl.ds(pl.multiple_of(part * (kdim // 2), kdim // 2), kdim // 2)]
            return pltpu.make_async_remote_copy(
                src_ref=buf if src is None else src, dst_ref=buf,
                send_sem=send_sems.at[k], recv_sem=recv_sems.at[k],
                device_id=to, device_id_type=MESH)

        def keep(step, block):
            s = slot(block)
            cols = pl.ds(pl.multiple_of((s % 2) * ncols, 128), ncols)
            return pltpu.make_async_copy(wv.at[s], all_hbm.at[s // 2, :, cols], local_sems.at[step])

        if n_ride:
            gather = _Gather(ride_ins, ride_outs, ride_scr[3:], *ride_scr[:3])
        to_sibling, to_nb1, to_nb2 = (copy(0, me, sibling, mine_hbm), copy(1, me, (*nb1, c), mine_hbm),
                                      copy(2, me, (*nb2, c), mine_hbm))
        relay1, relay2 = copy(3, (*nb2, c), (*nb1, c), part=c), copy(4, (*nb1, c), (*nb2, c), part=1 - c)
        pass_nb1, pass_nb2 = copy(5, (*nb1, c), sibling), copy(6, (*nb2, c), sibling)
        pass_d1, pass_d2 = copy(7, (*diag, c), sibling, part=c), copy(8, (*diag, c), sibling, part=1 - c)
        sends = [to_sibling, to_nb1, to_nb2, relay1, relay2, pass_nb1, pass_nb2, pass_d1, pass_d2]
        due = [
            (me, [], []),
            (sibling, [copy(0, sibling, me)], [[]]),
            ((*nb1, c), [copy(1, (*nb1, c), me)], [[pass_nb1, to_nb2]]),
            ((*nb2, 1 - c), [copy(5, (*nb2, 1 - c), me)], [[]]),
            ((*nb2, c), [copy(2, (*nb2, c), me)], [[pass_nb2, relay1, relay2]]),
            ((*nb1, 1 - c), [copy(6, (*nb1, 1 - c), me)], [[]]),
            ((*diag, c), [copy(3, (*diag, c), me, part=c), copy(4, (*diag, c), me, part=1 - c)],
             [[pass_d1], [pass_d2]]),
            ((*diag, 1 - c), [copy(7, (*diag, 1 - c), me, part=1 - c), copy(8, (*diag, 1 - c), me, part=c)],
             [[], []]),
        ]

        @pl.when((j == 0) & (i == 0))
        def _():
            to_sibling.start()
            to_nb1.start()
            load = pltpu.make_async_copy(mine_hbm, wv.at[slot(me)], local_sems.at[N_DEV])
            load.start()
            load.wait()
            keep(0, me).start()

        for step in range(1, N_DEV):
            block, arrivals, then = due[step]

            @pl.when((j == step) & (i == 0))
            def _():
                for arrival, follow in zip(arrivals, then):
                    arrival.wait_recv()
                    for cp in follow:
                        cp.start()
                keep(step, block).start()
                if n_ride and step == N_DEV - 2:
                    gather.begin()

        @pl.when(j == 0)
        def _():
            hb = (x_ref[...] * (1.0 + mod_ref[0, 1:2, :]) + mod_ref[0, 0:1, :]).astype(BF16)
            hs[i] = hb
            h_ref[...] = hb

        o_ref[...] = _dot(hs[i], wv[order_ref[j]]).astype(BF16)

        @pl.when((j == N_DEV - 1) & (i == n_i - 1))
        def _():
            for cp in sends:
                cp.wait_send()
            for step in range(N_DEV):
                keep(step, due[step][0]).wait()
            if n_ride:
                gather.finish()

    def first_pass(j, i):
        return jnp.where(j == 0, i, n_i - 1)

    any_spec = pl.BlockSpec(memory_space=pl.ANY)
    res = pl.pallas_call(
        body, name="gather_proj",
        grid_spec=pltpu.PrefetchScalarGridSpec(
            num_scalar_prefetch=1,
            grid=(N_DEV, n_i),
            in_specs=[pl.BlockSpec((tm, kdim), lambda j, i, order_ref: (first_pass(j, i), 0)),
                      pl.BlockSpec((1, 3, kdim), lambda j, i, order_ref: (first_pass(j, i) // per_seq, 0, 0)),
                      any_spec] + [any_spec] * n_ride,
            out_specs=(pl.BlockSpec((tm, ncols), lambda j, i, order_ref: (i, order_ref[j])),
                       pl.BlockSpec((tm, kdim), lambda j, i, order_ref: (first_pass(j, i), 0)), any_spec)
                      + (any_spec,) * n_ride,
            scratch_shapes=[pltpu.VMEM((N_DEV, kdim, ncols), BF16),
                            pltpu.SemaphoreType.DMA((9,)), pltpu.SemaphoreType.DMA((9,)),
                            pltpu.SemaphoreType.DMA((N_DEV + 1,)), pltpu.VMEM((n_i, tm, kdim), BF16)]
                           + (_Gather.scratch(ride_arrs) if n_ride else [])),
        out_shape=(jax.ShapeDtypeStruct((rows, N_DEV * ncols), BF16),
                   jax.ShapeDtypeStruct((rows, kdim), BF16),
                   jax.ShapeDtypeStruct((N_DEV // 2, kdim, 2 * ncols), BF16)) + tuple(ride_shapes),
        compiler_params=pltpu.CompilerParams(vmem_limit_bytes=VMEM_LIMIT),
    )(order, x2, mod3, w_shard, *ride_arrs)
    return res[0], res[1], res[2], res[3:]


SKEW_W = 512


def _bucket_maps():
    lanes = np.arange(SKEW_W)

    def buckets_of(steps):
        rows = []
        for dil in DILATIONS:
            dist = np.maximum(steps, 0) * dil
            nf = np.maximum(dist, 1).astype(np.float32)
            large = 16 + (np.log(nf / np.float32(16)) / np.float32(math.log(128.0))
                          * np.float32(16)).astype(np.int32)
            large = np.minimum(large, N_BUCKETS - 1)
            bucket = np.where(dist < 16, dist, large)
            rows.append(np.where((steps >= 0) & (steps <= N_STEPS), bucket, -1).astype(np.int32))
        return np.stack(rows)[:, None, :]

    a = np.arange(QB)[:, None]
    b = np.arange(2 * QB)[None, :]
    steps = a + QB - b
    band = (steps >= 0) & (steps <= N_STEPS)
    first = band & (b >= QB)
    masks = np.stack([first, band]).astype(np.int32)
    return buckets_of(QB - lanes), buckets_of(2 * QB - 1 - lanes), masks


def _bias_expand(rel_bias, lane_buckets, masks):
    def body(tab_ref, bk_ref, mk_ref, o_ref):
        for g in range(3):
            bk = bk_ref[g]
            for h in range(4):
                col = 4 * g + h
                per_offset = jnp.zeros((1, SKEW_W), F32)
                for k in range(N_BUCKETS):
                    per_offset = jnp.where(bk == k, tab_ref[k, col], per_offset)
                tile = pltpu.roll(jnp.broadcast_to(per_offset, (QB, SKEW_W)), 0, 1, stride=1, stride_axis=0)
                tile = tile[:, :2 * QB]
                o_ref[g, 0, h] = jnp.where(mk_ref[0] != 0, tile, NEG_INF)
                o_ref[g, 1, h] = jnp.where(mk_ref[1] != 0, tile, NEG_INF)

    return pl.pallas_call(
        body, name="bias_expand",
        in_specs=[pl.BlockSpec(memory_space=pltpu.SMEM),
                  pl.BlockSpec(memory_space=pltpu.VMEM),
                  pl.BlockSpec(memory_space=pltpu.VMEM)],
        out_shape=jax.ShapeDtypeStruct((3, 2, 4, QB, 2 * QB), F32),
    )(rel_bias, lane_buckets, masks)


def _bias_grad(ds1, ds2, ds3, lane_buckets):
    exchange = jnp.asarray(np.eye(QB, dtype=np.float32)[::-1].copy())

    def body(d1_ref, d2_ref, d3_ref, bk_ref, ex_ref, o_ref):
        for g, d_ref in enumerate((d1_ref, d2_ref, d3_ref)):
            bk = bk_ref[g]
            for h in range(4):
                flipped = jnp.dot(ex_ref[...], d_ref[h], preferred_element_type=F32,
                                  precision=lax.Precision.HIGHEST)
                padded = jnp.concatenate([flipped, jnp.zeros((QB, SKEW_W - 2 * QB), F32)], axis=1)
                skewed = pltpu.roll(padded, 0, 1, stride=1, stride_axis=0)
                per_offset = jnp.sum(skewed, axis=0, keepdims=True)
                for k in range(N_BUCKETS):
                    o_ref[k, 4 * g + h] = jnp.sum(jnp.where(bk == k, per_offset, 0.0))

    return pl.pallas_call(
        body, name="bias_grad",
        in_specs=[pl.BlockSpec(memory_space=pltpu.VMEM)] * 5,
        out_specs=pl.BlockSpec(memory_space=pltpu.SMEM),
        out_shape=jax.ShapeDtypeStruct((N_BUCKETS, N_HEADS), F32),
    )(ds1, ds2, ds3, lane_buckets, exchange)


def _scratch_sets(rows):
    return 4 if rows <= 512 else 1


def _unit_chunks(dil, size=16):
    units = [(h, r) for h in range(4) for r in range(dil)]
    return [units[i:i + size] for i in range(0, len(units), size)]


def _residue_rows(src_ref, copies, h, residue):
    buf = copies[h % len(copies)]
    buf[...] = src_ref[:, h * HD:(h + 1) * HD].astype(F32)
    return lambda r: buf[residue(r), :].astype(BF16)


def _attn_fwd(proj, bias, g):
    dil = DILATIONS[g]
    rows = QB * dil
    nsb = S // rows
    has_prev = nsb > 1

    def residue(r):
        return pl.ds(r, QB, stride=dil)

    n_sets = _scratch_sets(rows)
    n_in = 6 if has_prev else 4
    n_copied = (4 + (2 if has_prev else 0)) * n_sets

    def body(*refs):
        q_ref, kc_ref, vc_ref = refs[:3]
        kp_ref, vp_ref = refs[3:5] if has_prev else (None, None)
        b_ref = refs[n_in - 1]
        o_ref, l_ref = refs[n_in:n_in + 2]
        scr = list(refs[n_in + 2:])
        ls = [scr.pop(0) for _ in range(4)]
        copies = {name: [scr.pop(0) for _ in range(n_sets)]
                  for name in ("q", "kc", "vc", "o") + (("kp", "vp") if has_prev else ())}
        lane = lax.broadcasted_iota(jnp.int32, (QB, 128), 1)
        refs_of = {"q": q_ref, "kc": kc_ref, "vc": vc_ref, "kp": kp_ref, "vp": vp_ref}
        for chunk in _unit_chunks(dil):
            rows_of = {h: {name: _residue_rows(refs_of[name], copies[name], h, residue)
                           for name in refs_of if refs_of[name] is not None}
                       for h in sorted({h for h, _ in chunk})}

            def batch(name):
                return jnp.stack([rows_of[h][name](r) for h, r in chunk])

            q, k, v = batch("q"), batch("kc"), batch("vc")
            if has_prev:
                k = jnp.concatenate([batch("kp"), k], axis=1)
                v = jnp.concatenate([batch("vp"), v], axis=1)
                bias_b = jnp.stack([b_ref[h] for h, _ in chunk])
            else:
                bias_b = jnp.stack([b_ref[h, :, QB:] for h, _ in chunk])
            s = jnp.einsum("uqd,ukd->uqk", q, k, preferred_element_type=F32) * SCALE + bias_b
            m = jnp.max(s, axis=-1, keepdims=True)
            p = jnp.exp(s - m)
            l = jnp.sum(p, axis=-1, keepdims=True)
            o = jnp.einsum("uqk,ukd->uqd", p.astype(BF16), v, preferred_element_type=F32) / l
            lse = m + jnp.log(l)
            for i, (h, r) in enumerate(chunk):
                copies["o"][h % n_sets][residue(r), :] = o[i]
                ls[h][r * QB:(r + 1) * QB, :] = jnp.where(lane == h, lse[i], 0.0)
            for h in sorted({h for h, _ in chunk}):
                o_ref[:, h * HD:(h + 1) * HD] = copies["o"][h % n_sets][...]
        for r in range(dil):
            blk = slice(r * QB, (r + 1) * QB)
            l_ref[residue(r), :] = (ls[0][blk, :] + ls[1][blk, :]) + (ls[2][blk, :] + ls[3][blk, :])

    def row(b, n):
        return b * nsb + n

    def prev(b, n):
        return b * nsb + jnp.maximum(n - 1, 0)

    in_specs = [
        pl.BlockSpec((rows, GW), lambda b, n: (row(b, n), CB_Q + g)),
        pl.BlockSpec((rows, GW), lambda b, n: (row(b, n), CB_K + g)),
        pl.BlockSpec((rows, GW), lambda b, n: (row(b, n), CB_V + g)),
    ]
    args = [proj, proj, proj]
    scratch = [pltpu.VMEM((rows, 128), F32)] * (4 + n_copied)
    if has_prev:
        in_specs += [pl.BlockSpec((rows, GW), lambda b, n: (prev(b, n), CB_K + g)),
                     pl.BlockSpec((rows, GW), lambda b, n: (prev(b, n), CB_V + g))]
        args += [proj, proj]
    in_specs.append(pl.BlockSpec((None, None, 4, QB, 2 * QB),
                                 lambda b, n: (g, jnp.minimum(n, 1), 0, 0, 0)))
    args.append(bias)
    return pl.pallas_call(
        body, name=f"attn_fwd{g}",
        grid=(BL, nsb),
        in_specs=in_specs,
        out_specs=(pl.BlockSpec((rows, GW), lambda b, n: (row(b, n), 0)),
                   pl.BlockSpec((rows, 128), lambda b, n: (row(b, n), 0))),
        out_shape=(jax.ShapeDtypeStruct((T, GW), F32), jax.ShapeDtypeStruct((T, 128), F32)),
        scratch_shapes=scratch,
        compiler_params=pltpu.CompilerParams(vmem_limit_bytes=VMEM_LIMIT),
    )(*args)


def _attn_bwd(proj, d_out, stats, bias, dproj, g):
    dil = DILATIONS[g]
    rows = QB * dil
    nsb = S // rows
    has_prev = nsb > 1
    n_steps = nsb + 1 if has_prev else 1
    n_in = 7 + (2 if has_prev else 0)

    def residue(r):
        return pl.ds(r, QB, stride=dil)

    n_sets = _scratch_sets(rows)

    def body(*refs):
        q_ref, kc_ref, vc_ref, do_ref, st_ref, b_ref = refs[:6]
        kp_ref, vp_ref = refs[6:8] if has_prev else (None, None)
        out_ref, db_ref = refs[n_in], refs[n_in + 1]
        scr = list(refs[n_in + 2:])
        sq, sk, sv, sems = [scr.pop(0) for _ in range(4)]
        carry = scr.pop(0) if has_prev else None
        sts = scr.pop(0)
        copies = {name: [scr.pop(0) for _ in range(n_sets)]
                  for name in ("q", "kc", "vc", "do", "dq", "dk", "dv") + (("kp", "vp") if has_prev else ())}
        b, n = pl.program_id(0), pl.program_id(1)

        @pl.when((b == 0) & (n == 0))
        def _():
            db_ref[...] = jnp.zeros_like(db_ref)

        def finish(h, r, dq, dk, dv):
            for name, val in (("dq", dq), ("dk", dk), ("dv", dv)):
                copies[name][h % n_sets][residue(r), :] = val

        step = b * n_steps + n
        slot = step % 2

        def stage_copies(s, row0):
            return _column_copies([(sq.at[s], CB * (CB_Q + g)), (sk.at[s], CB * (CB_K + g)),
                                   (sv.at[s], CB * (CB_V + g))], out_ref, row0, sems.at[s])

        @pl.when((step >= 2) & ((step - 2) % n_steps >= (1 if has_prev else 0)))
        def _():
            for cp in stage_copies(slot, 0):
                cp.wait()

        def finish_head(h):
            sl = slice(h * HD, (h + 1) * HD)
            sq[slot, :, sl] = copies["dq"][h % n_sets][...].astype(BF16)
            sk[slot, :, sl] = copies["dk"][h % n_sets][...].astype(BF16)
            sv[slot, :, sl] = copies["dv"][h % n_sets][...].astype(BF16)

        def write_block(blk_idx):
            for cp in stage_copies(slot, pl.multiple_of(blk_idx * rows, rows)):
                cp.start()

            @pl.when(step == BL * n_steps - 1)
            def _():
                for s in range(2):
                    for cp in stage_copies(s, 0):
                        cp.wait()

        def carried(h, r):
            blk = slice(r * QB, (r + 1) * QB)
            return ((blk, slice(h * HD, (h + 1) * HD)), (blk, slice(GW + h * HD, GW + (h + 1) * HD)),
                    (blk, slice(2 * GW + h * HD, 2 * GW + (h + 1) * HD)))

        if has_prev:
            @pl.when(n == 0)
            def _():
                carry[...] = jnp.zeros_like(carry)

            @pl.when(n == nsb)
            def _():
                for h in range(4):
                    for r in range(dil):
                        cq, ck, cv = carried(h, r)
                        finish(h, r, carry[cq], carry[ck], carry[cv])
                    finish_head(h)
                write_block(b * nsb + nsb - 1)

        @pl.when(n < nsb)
        def _():
            for r in range(dil):
                sts[r * QB:(r + 1) * QB, :] = st_ref[residue(r), :]
            refs_of = {"q": q_ref, "kc": kc_ref, "vc": vc_ref, "do": do_ref, "kp": kp_ref, "vp": vp_ref}
            for chunk in _unit_chunks(dil):
                heads = sorted({h for h, _ in chunk})
                rows_of = {h: {name: _residue_rows(refs_of[name], copies[name], h, residue)
                               for name in refs_of if refs_of[name] is not None}
                           for h in heads}

                def batch(name):
                    return jnp.stack([rows_of[h][name](r) for h, r in chunk])

                q, k, v, do = batch("q"), batch("kc"), batch("vc"), batch("do")
                if has_prev:
                    k = jnp.concatenate([batch("kp"), k], axis=1)
                    v = jnp.concatenate([batch("vp"), v], axis=1)
                    bias_b = jnp.stack([b_ref[h] for h, _ in chunk])
                else:
                    bias_b = jnp.stack([b_ref[h, :, QB:] for h, _ in chunk])
                lse = jnp.stack([sts[r * QB:(r + 1) * QB, h:h + 1] for h, r in chunk])
                delta = jnp.stack([sts[r * QB:(r + 1) * QB, 4 + h:5 + h] for h, r in chunk])
                s = jnp.einsum("uqd,ukd->uqk", q, k, preferred_element_type=F32) * SCALE + bias_b
                p = jnp.exp(s - lse)
                ds = p * (jnp.einsum("uqd,ukd->uqk", do, v, preferred_element_type=F32) - delta)
                for h in heads:
                    mine = [ds[i] for i, (hh, _) in enumerate(chunk) if hh == h]
                    tot = mine[0]
                    for extra in mine[1:]:
                        tot = tot + extra
                    if has_prev:
                        db_ref[h] += tot
                    else:
                        db_ref[h, :, QB:] += tot
                dsb, pb = ds.astype(BF16), p.astype(BF16)
                dq = jnp.einsum("uqk,ukd->uqd", dsb, k, preferred_element_type=F32) * SCALE
                dk = jnp.einsum("uqk,uqd->ukd", dsb, q, preferred_element_type=F32) * SCALE
                dv = jnp.einsum("uqk,uqd->ukd", pb, do, preferred_element_type=F32)
                for i, (h, r) in enumerate(chunk):
                    if has_prev:
                        cq, ck, cv = carried(h, r)
                        finish(h, r, carry[cq], carry[ck] + dk[i, :QB], carry[cv] + dv[i, :QB])
                        carry[cq] = dq[i]
                        carry[ck] = dk[i, QB:]
                        carry[cv] = dv[i, QB:]
                    else:
                        finish(h, r, dq[i], dk[i], dv[i])
                for h in heads:
                    finish_head(h)
            if has_prev:
                @pl.when(n > 0)
                def _():
                    write_block(b * nsb + n - 1)
            else:
                write_block(b)

    def row(b, n):
        return b * nsb + jnp.minimum(n, nsb - 1)

    def prev(b, n):
        return b * nsb + jnp.maximum(jnp.minimum(n, nsb - 1) - 1, 0)

    in_specs = [
        pl.BlockSpec((rows, GW), lambda b, n: (row(b, n), CB_Q + g)),
        pl.BlockSpec((rows, GW), lambda b, n: (row(b, n), CB_K + g)),
        pl.BlockSpec((rows, GW), lambda b, n: (row(b, n), CB_V + g)),
        pl.BlockSpec((rows, GW), lambda b, n: (row(b, n), 0)),
        pl.BlockSpec((rows, 128), lambda b, n: (row(b, n), 0)),
        pl.BlockSpec((None, None, 4, QB, 2 * QB),
                     lambda b, n: (g, jnp.minimum(jnp.minimum(n, nsb - 1), 1), 0, 0, 0)),
    ]
    args = [proj, proj, proj, d_out, stats, bias]
    scratch = [pltpu.VMEM((2, rows, GW), BF16)] * 3 + [pltpu.SemaphoreType.DMA((2, 3))]
    if has_prev:
        in_specs += [pl.BlockSpec((rows, GW), lambda b, n: (prev(b, n), CB_K + g)),
                     pl.BlockSpec((rows, GW), lambda b, n: (prev(b, n), CB_V + g))]
        args += [proj, proj]
        scratch.append(pltpu.VMEM((rows, 3 * GW), F32))
    n_copied = (7 + (2 if has_prev else 0)) * n_sets
    scratch += [pltpu.VMEM((rows, 128), F32)] * (1 + n_copied)
    in_specs.append(pl.BlockSpec(memory_space=pl.ANY))
    args.append(dproj)
    return pl.pallas_call(
        body, name=f"attn_bwd{g}",
        grid=(BL, n_steps),
        in_specs=in_specs,
        out_specs=(pl.BlockSpec(memory_space=pl.ANY),
                   pl.BlockSpec((4, QB, 2 * QB), lambda b, n: (0, 0, 0))),
        out_shape=(jax.ShapeDtypeStruct((T, NCOL), BF16),
                   jax.ShapeDtypeStruct((4, QB, 2 * QB), F32)),
        scratch_shapes=scratch,
        input_output_aliases={len(args) - 1: 0},
        compiler_params=pltpu.CompilerParams(vmem_limit_bytes=VMEM_LIMIT),
    )(*args)


def _tail(x2, tgt2, mod3, o_g, lse_g, proj, w_ao, w_co, w_o, conv_w, conv_b, ln_g, ln_b):
    tm = 256
    per_seq = S // tm
    halo = 16

    def body(x_ref, t_ref, mod_ref, o1_ref, o2_ref, o3_ref, l1_ref, l2_ref, l3_ref,
             ga_ref, u_ref, bg_ref, cg_ref, gc_ref, ma_ref, mc_ref, up_ref, cp_ref,
             wao_ref, wco_ref, wo_ref, cw_ref, cb_ref, lg_ref, lb_ref,
             dproj_ref, dyc_ref, do_ref, st_ref, dxd_ref,
             gwo_ref, gwco_ref, gwao_ref, vec_ref,
             dga_s, dbg_s, dgm_s, sems, acc_o, acc_co, acc_ao):
        i = pl.program_id(0)
        bidx = i // per_seq
        first = (i % per_seq) == 0

        @pl.when(i == 0)
        def _():
            vec_ref[...] = jnp.zeros_like(vec_ref)
            acc_o[...] = jnp.zeros_like(acc_o)
            acc_co[...] = jnp.zeros_like(acc_co)
            acc_ao[...] = jnp.zeros_like(acc_ao)

        slot = i % 2

        def column_copies(s, row0):
            return _column_copies([(dga_s.at[s], CB * CB_GA), (dbg_s.at[s], D * KB_BG), (dgm_s.at[s], D * KB_GC)],
                                  dproj_ref, row0, sems.at[s])

        @pl.when(i >= 2)
        def _():
            for cp in column_copies(slot, 0):
                cp.wait()

        l1, l2, l3 = l1_ref[...], l2_ref[...], l3_ref[...]
        mx = jnp.maximum(jnp.maximum(l1, l2), l3)
        e1, e2, e3 = jnp.exp(l1 - mx), jnp.exp(l2 - mx), jnp.exp(l3 - mx)
        esum = e1 + e2 + e3
        lse_tot = mx + jnp.log(esum)
        w1, w2, w3 = e1 / esum, e2 / esum, e3 / esum

        def per_head(wv):
            return jnp.concatenate([jnp.broadcast_to(wv[:, h:h + 1], (tm, HD)) for h in range(4)], axis=1)

        o = per_head(w1) * o1_ref[...] + per_head(w2) * o2_ref[...] + per_head(w3) * o3_ref[...]

        ga = ga_ref[...].astype(F32)
        sig_ga = _sigmoid(ga)
        silu_ga = ga * sig_ga
        a_in = (o * silu_ga).astype(BF16)
        a_out = _dot(a_in, wao_ref[...])

        u = u_ref[...].astype(F32)
        cg = cg_ref[...].astype(F32)
        z = cg * u
        zp = cp_ref[...].astype(F32) * up_ref[...].astype(F32)
        zp = jnp.where(first, 0.0, zp)
        zcat = jnp.concatenate([zp, z], axis=0)
        z1 = pltpu.roll(zcat, 1, 0)[halo:]
        z2 = pltpu.roll(zcat, 2, 0)[halo:]
        y_conv = cw_ref[0:1, :] * z2 + cw_ref[1:2, :] * z1 + cw_ref[2:3, :] * z + cb_ref[...]
        gc = gc_ref[...].astype(F32)
        sig_gc = _sigmoid(gc)
        silu_gc = gc * sig_gc
        bg = bg_ref[...].astype(F32)
        bg_yc = bg * y_conv
        s_in = (bg_yc * silu_gc).astype(BF16)
        s_out = _dot(s_in, wco_ref[...])

        sa = _sigmoid(ma_ref[...].astype(F32))
        sc = _sigmoid(mc_ref[...].astype(F32))
        merged = (sa * a_out + sc * s_out).astype(BF16)
        y = _dot(merged, wo_ref[...])
        gate1 = 1.0 + mod_ref[0, 2:3, :]
        xv = x_ref[...]
        resid = ALPHA * xv + gate1 * y
        mu = jnp.mean(resid, axis=1, keepdims=True)
        xc = resid - mu
        var = jnp.mean(xc * xc, axis=1, keepdims=True)
        rstd = lax.rsqrt(var + LN_EPS)
        xhat = xc * rstd
        lg = lg_ref[...]
        err = xhat * lg + lb_ref[...] - t_ref[...]
        vec_ref[3:4, :] += (0.5 / D) * jnp.sum(err * err, axis=0, keepdims=True)

        vec_ref[1:2, :] += (1.0 / D) * jnp.sum(err * xhat, axis=0, keepdims=True)
        vec_ref[2:3, :] += (1.0 / D) * jnp.sum(err, axis=0, keepdims=True)
        dxh = err * (lg * (1.0 / D))
        dres = rstd * (dxh - jnp.mean(dxh, axis=1, keepdims=True)
                       - xhat * jnp.mean(dxh * xhat, axis=1, keepdims=True))
        dxd_ref[...] = ALPHA * dres
        dgate = jnp.sum(dres * y, axis=0, keepdims=True)
        vec_ref[4:5, :] += jnp.where(bidx == 0, dgate, 0.0)
        vec_ref[5:6, :] += jnp.where(bidx == 1, dgate, 0.0)
        dy = (dres * gate1).astype(BF16)

        dmerged = _dot_nt(dy, wo_ref[...])
        da_out_f = dmerged * sa
        ds_out_f = dmerged * sc
        da_out = da_out_f.astype(BF16)
        ds_out = ds_out_f.astype(BF16)
        dgm_s[slot, :, 2 * D:3 * D] = (ds_out_f * s_out * (1.0 - sc)).astype(BF16)
        dgm_s[slot, :, D:2 * D] = (da_out_f * a_out * (1.0 - sa)).astype(BF16)
        da_in = _dot_nt(da_out, wao_ref[...])
        ds_in = _dot_nt(ds_out, wco_ref[...])

        acc_o[...] += _dot_tn(merged, dy)
        acc_co[...] += _dot_tn(s_in, ds_out)
        acc_ao[...] += _dot_tn(a_in, da_out)

        d_o = da_in * silu_ga
        do_ref[...] = d_o.astype(BF16)
        dga_s[slot] = (da_in * o * (sig_ga + silu_ga * (1.0 - sig_ga))).astype(BF16)
        lane = lax.broadcasted_iota(jnp.int32, (tm, 128), 1)
        stats = lse_tot
        od = o * d_o
        for h in range(4):
            delta = jnp.sum(od[:, h * HD:(h + 1) * HD], axis=1, keepdims=True)
            stats = jnp.where(lane == 4 + h, delta, stats)
        st_ref[...] = stats

        ds_silu = ds_in * silu_gc
        dbg_s[slot] = (ds_silu * y_conv).astype(BF16)
        dyc = ds_silu * bg
        dyc_ref[...] = dyc
        vec_ref[0:1, :] += jnp.sum(dyc, axis=0, keepdims=True)
        dgm_s[slot, :, 0:D] = (ds_in * bg_yc * (sig_gc + silu_gc * (1.0 - sig_gc))).astype(BF16)


        for cp in column_copies(slot, pl.multiple_of(i * tm, tm)):
            cp.start()

        @pl.when(i == T // tm - 1)
        def _():
            gwo_ref[...] = acc_o[...].astype(BF16)
            gwco_ref[...] = acc_co[...].astype(BF16)
            gwao_ref[...] = acc_ao[...].astype(BF16)
            for s in range(2):
                for cp in column_copies(s, 0):
                    cp.wait()

    def tile(width, cblk=0):
        return pl.BlockSpec((tm, width), lambda i: (i, cblk))

    def whole(shape):
        return pl.BlockSpec(shape, lambda i: tuple(0 for _ in shape))

    def once(shape):
        return pl.BlockSpec(shape, lambda i: tuple(0 for _ in shape), pipeline_mode=pl.Buffered(1))

    prev_rows = lambda i: (jnp.maximum(i * (tm // halo) - 1, 0),)
    in_specs = [
        tile(D), tile(D), pl.BlockSpec((1, 3, D), lambda i: (i // per_seq, 0, 0)),
        tile(GW), tile(GW), tile(GW), tile(128), tile(128), tile(128),
        tile(GW, CB_GA), tile(D, KB_U), tile(D, KB_BG), tile(D, KB_CG), tile(D, KB_GC),
        tile(D, KB_MA), tile(D, KB_MC),
        pl.BlockSpec((halo, D), lambda i: (*prev_rows(i), KB_U)),
        pl.BlockSpec((halo, D), lambda i: (*prev_rows(i), KB_CG)),
        whole((GW, D)), whole((D, D)), whole((D, D)),
        whole((3, D)), whole((1, D)), whole((1, D)), whole((1, D)),
    ]
    out_specs = (
        pl.BlockSpec(memory_space=pl.ANY), tile(D), tile(GW), tile(128), tile(D),
        once((D, D)), once((D, D)), once((GW, D)),
        pl.BlockSpec((8, D), lambda i: (0, 0)),
    )
    out_shape = (
        jax.ShapeDtypeStruct((T, NCOL), BF16),
        jax.ShapeDtypeStruct((T, D), F32),
        jax.ShapeDtypeStruct((T, GW), BF16),
        jax.ShapeDtypeStruct((T, 128), F32),
        jax.ShapeDtypeStruct((T, D), F32),
        jax.ShapeDtypeStruct((D, D), BF16),
        jax.ShapeDtypeStruct((D, D), BF16),
        jax.ShapeDtypeStruct((GW, D), BF16),
        jax.ShapeDtypeStruct((8, D), F32),
    )
    return pl.pallas_call(
        body, name="tail",
        grid=(T // tm,),
        in_specs=in_specs, out_specs=out_specs, out_shape=out_shape,
        scratch_shapes=[pltpu.VMEM((2, tm, GW), BF16), pltpu.VMEM((2, tm, D), BF16), pltpu.VMEM((2, tm, 3 * D), BF16),
                        pltpu.SemaphoreType.DMA((2, 3)),
                        pltpu.VMEM((D, D), F32), pltpu.VMEM((D, D), F32), pltpu.VMEM((GW, D), F32)],
        compiler_params=pltpu.CompilerParams(vmem_limit_bytes=VMEM_LIMIT_TAIL),
    )(x2, tgt2, mod3, *o_g, *lse_g, proj, proj, proj, proj, proj, proj, proj, proj, proj,
      w_ao, w_co, w_o, conv_w, conv_b, ln_g, ln_b)


def _conv_bwd(dyc, proj, conv_w, dproj):
    tm = 512
    per_seq = S // tm

    def body(d_ref, dn_ref, u_ref, c_ref, cw_ref, _, dproj_ref, g_ref, du_s, dc_s, sems):
        i = pl.program_id(0)
        last = (i % per_seq) == per_seq - 1

        @pl.when(i == 0)
        def _():
            g_ref[...] = jnp.zeros_like(g_ref)

        slot = i % 2

        def column_copies(s, row0):
            return _column_copies([(du_s.at[s], D * KB_U), (dc_s.at[s], D * KB_CG)], dproj_ref, row0, sems.at[s])

        @pl.when(i >= 2)
        def _():
            for cp in column_copies(slot, 0):
                cp.wait()

        d = d_ref[...]
        dn = jnp.where(last, 0.0, dn_ref[...])
        dcat = jnp.concatenate([d, dn], axis=0)
        d1 = pltpu.roll(dcat, tm + 8 - 1, 0)[:tm]
        d2 = pltpu.roll(dcat, tm + 8 - 2, 0)[:tm]
        dz = cw_ref[2:3, :] * d + cw_ref[1:2, :] * d1 + cw_ref[0:1, :] * d2
        u = u_ref[...].astype(F32)
        cg = c_ref[...].astype(F32)
        du_s[slot] = (dz * cg).astype(BF16)
        dc_s[slot] = (dz * u).astype(BF16)
        for cp in column_copies(slot, pl.multiple_of(i * tm, tm)):
            cp.start()

        z = cg * u
        g_ref[0:1, :] += jnp.sum(d2 * z, axis=0, keepdims=True)
        g_ref[1:2, :] += jnp.sum(d1 * z, axis=0, keepdims=True)
        g_ref[2:3, :] += jnp.sum(d * z, axis=0, keepdims=True)

        @pl.when(i == T // tm - 1)
        def _():
            for s in range(2):
                for cp in column_copies(s, 0):
                    cp.wait()

    n_tiles = T // tm
    next_rows = lambda i: jnp.minimum((i + 1) * (tm // 8), T // 8 - 1)
    return pl.pallas_call(
        body, name="conv_bwd",
        grid=(n_tiles,),
        in_specs=[pl.BlockSpec((tm, D), lambda i: (i, 0)),
                  pl.BlockSpec((8, D), lambda i: (next_rows(i), 0)),
                  pl.BlockSpec((tm, D), lambda i: (i, KB_U)),
                  pl.BlockSpec((tm, D), lambda i: (i, KB_CG)),
                  pl.BlockSpec((3, D), lambda i: (0, 0)),
                  pl.BlockSpec(memory_space=pl.ANY)],
        out_specs=(pl.BlockSpec(memory_space=pl.ANY),
                   pl.BlockSpec((8, D), lambda i: (0, 0))),
        out_shape=(jax.ShapeDtypeStruct((T, NCOL), BF16),
                   jax.ShapeDtypeStruct((8, D), F32)),
        scratch_shapes=[pltpu.VMEM((2, tm, D), BF16), pltpu.VMEM((2, tm, D), BF16), pltpu.SemaphoreType.DMA((2, 2))],
        input_output_aliases={5: 0},
        compiler_params=pltpu.CompilerParams(vmem_limit_bytes=VMEM_LIMIT),
    )(dyc, dyc, proj, proj, conv_w, dproj)


def _dh_dx(dproj, w_in_all, x2, dxd, mod3, chip_sums, hops=(), parts0=None):
    tm = 512
    per_seq = S // tm
    n_pass, _, width = w_in_all.shape
    n = len(chip_sums)
    n_in = 5 + n + (0 if parts0 is None else 1)

    def body(*refs):
        d_ref, w_ref, x_ref, dxd_ref, mod_ref = refs[:5]
        ins = refs[5:5 + n]
        gx_ref, vec_ref = refs[n_in:n_in + 2]
        outs = refs[n_in + 2:n_in + 2 + n]
        acc, send_sems, recv_sems, local_sems = refs[n_in + 2 + n:]
        jj, i = pl.program_id(0), pl.program_id(1)

        @pl.when((i == 0) & (jj == 0))
        def _():
            vec_ref[...] = jnp.zeros_like(vec_ref)
            if n:
                sends, _, mine = _chip_copies(ins, outs, send_sems, recv_sems, local_sems, hops)
                for cp in sends + mine:
                    cp.start()

        if n:
            @pl.when((i == T // tm - 1) & (jj == n_pass - 1))
            def _():
                sends, arrivals, mine = _chip_copies(ins, outs, send_sems, recv_sems, local_sems, hops)
                for cp in arrivals:
                    cp.wait_recv()
                for cp in sends:
                    cp.wait_send()
                for cp in mine:
                    cp.wait()

        def partial():
            return _dot_nt(d_ref[...], w_ref[...])

        @pl.when(jj == 0)
        def _():
            acc[i] = partial()

        @pl.when((jj > 0) & (jj < n_pass - 1))
        def _():
            acc[i] += partial()

        @pl.when(jj == n_pass - 1)
        def _():
            dh = acc[i] + partial()
            bidx = i // per_seq
            gx_ref[...] = dxd_ref[...] + dh * (1.0 + mod_ref[0, 1:2, :])
            dshift = jnp.sum(dh, axis=0, keepdims=True)
            dscale = jnp.sum(dh * x_ref[...], axis=0, keepdims=True)
            vec_ref[0:1, :] += jnp.where(bidx == 0, dshift, 0.0)
            vec_ref[1:2, :] += jnp.where(bidx == 1, dshift, 0.0)
            vec_ref[2:3, :] += jnp.where(bidx == 0, dscale, 0.0)
            vec_ref[3:4, :] += jnp.where(bidx == 1, dscale, 0.0)

    def last_pass(jj, i):
        return jnp.where(jj == n_pass - 1, i, 0)

    any_spec = pl.BlockSpec(memory_space=pl.ANY)
    res = pl.pallas_call(
        body, name="dh_dx",
        grid=(n_pass, T // tm),
        in_specs=[
            pl.BlockSpec((tm, width), lambda jj, i: (i, jj)),
            pl.BlockSpec((None, D, width), lambda jj, i: (jj, 0, 0)),
            pl.BlockSpec((tm, D), lambda jj, i: (last_pass(jj, i), 0)),
            pl.BlockSpec((tm, D), lambda jj, i: (last_pass(jj, i), 0)),
            pl.BlockSpec((1, 3, D), lambda jj, i: (last_pass(jj, i) // per_seq, 0, 0))]
                 + [any_spec] * (n_in - 5),
        out_specs=(pl.BlockSpec((tm, D), lambda jj, i: (last_pass(jj, i), 0)),
                   pl.BlockSpec((8, D), lambda jj, i: (0, 0))) + (any_spec,) * n,
        out_shape=(jax.ShapeDtypeStruct((T, D), F32), jax.ShapeDtypeStruct((8, D), F32))
                  + tuple(jax.ShapeDtypeStruct(a.shape, a.dtype) for a in chip_sums),
        scratch_shapes=[pltpu.VMEM((T // tm, tm, D), F32), pltpu.SemaphoreType.DMA((max(3 * n, 1),)),
                        pltpu.SemaphoreType.DMA((max(3 * n, 1),)), pltpu.SemaphoreType.DMA((max(n, 1),))],
        input_output_aliases={} if parts0 is None else {5 + n: 2},
        compiler_params=pltpu.CompilerParams(vmem_limit_bytes=VMEM_LIMIT),
    )(dproj, w_in_all, x2, dxd, mod3, *chip_sums, *([] if parts0 is None else [parts0]))
    return res[0], res[1], res[2:]


def _adam_step(g, w, m, v):
    nm = ADAM_B1 * m + (1.0 - ADAM_B1) * g
    nv = ADAM_B2 * v + (1.0 - ADAM_B2) * (g * g)
    m_hat = nm / (1.0 - ADAM_B1 ** ADAM_STEP)
    v_hat = nv / (1.0 - ADAM_B2 ** ADAM_STEP)
    return -ADAM_LR * (m_hat / (jnp.sqrt(v_hat) + ADAM_EPS) + ADAM_WD * w), nm, nv


def _adamw(parts, w, m, v, name, row_tile=None):
    n_parts, rows, cols = parts.shape
    tr = rows if row_tile is None else row_tile

    def body(p_ref, w_ref, m_ref, v_ref, g_ref, d_ref, nm_ref, nv_ref):
        g = p_ref[0].astype(F32)
        for s in range(1, n_parts):
            g = g + p_ref[s].astype(F32)
        g_ref[...] = g
        d_ref[...], nm_ref[...], nv_ref[...] = _adam_step(g, w_ref[...], m_ref[...], v_ref[...])

    blk = pl.BlockSpec((tr, cols), lambda i: (i, 0))
    shp = jax.ShapeDtypeStruct((rows, cols), F32)
    return pl.pallas_call(
        body, name=name,
        grid=(rows // tr,),
        in_specs=[pl.BlockSpec((n_parts, tr, cols), lambda i: (0, i, 0)), blk, blk, blk],
        out_specs=(blk, blk, blk, blk),
        out_shape=(shp, shp, shp, shp),
        compiler_params=pltpu.CompilerParams(vmem_limit_bytes=VMEM_LIMIT),
    )(parts, w, m, v)


def _multi_adamw(parts_list, params, name):
    n = len(params)
    flat = [t for wmv in params for t in wmv]

    def body(*refs):
        parts, ins, outs = refs[:n], refs[n:4 * n], refs[4 * n:]
        for p in range(n):
            g = parts[p][0].astype(F32)
            for s in range(1, parts[p].shape[0]):
                g = g + parts[p][s].astype(F32)
            w_ref, m_ref, v_ref = ins[3 * p:3 * p + 3]
            g_ref, d_ref, nm_ref, nv_ref = outs[4 * p:4 * p + 4]
            g_ref[...] = g
            d_ref[...], nm_ref[...], nv_ref[...] = _adam_step(g, w_ref[...], m_ref[...], v_ref[...])

    out_shape = []
    for w, _, _ in params:
        out_shape += [jax.ShapeDtypeStruct(w.shape, F32)] * 4
    res = pl.pallas_call(body, name=name, out_shape=tuple(out_shape))(*parts_list, *flat)
    return [res[4 * p:4 * p + 4] for p in range(n)]


def _small_updates(small_g, dmod_all, rel_parts, params):
    flat = [t for wmv in params for t in wmv]

    def body(sg_ref, dm_ref, rp_ref, *refs):
        ins, outs = refs[:len(flat)], refs[len(flat):]

        def over_devices(row):
            tot = sg_ref[0, row:row + 1, :]
            for s in range(1, N_DEV):
                tot = tot + sg_ref[s, row:row + 1, :]
            return tot

        g_b_ada = dm_ref[0:1, :]
        for r in range(1, N_DEV * BL):
            g_b_ada = g_b_ada + dm_ref[r:r + 1, :]
        g_rel = rp_ref[0]
        for s in range(1, N_DEV):
            g_rel = g_rel + rp_ref[s]
        grads = [g_b_ada, over_devices(0), g_rel, over_devices(1), over_devices(2)]
        outs[0][...] = jnp.sum(over_devices(3), axis=1, keepdims=True)
        for p, g in enumerate(grads):
            w_ref, m_ref, v_ref = ins[3 * p:3 * p + 3]
            g_ref, d_ref, nm_ref, nv_ref = outs[1 + 4 * p:5 + 4 * p]
            g_ref[...] = g
            d_ref[...], nm_ref[...], nv_ref[...] = _adam_step(g, w_ref[...], m_ref[...], v_ref[...])

    out_shape = [jax.ShapeDtypeStruct((1, 1), F32)]
    for w, _, _ in params:
        out_shape += [jax.ShapeDtypeStruct(w.shape, F32)] * 4
    res = pl.pallas_call(body, name="small_updates", out_shape=tuple(out_shape))(small_g, dmod_all, rel_parts, *flat)
    return res[0], [res[1 + 4 * p:5 + 4 * p] for p in range(len(params))]


def _attn_fwd_dense(proj, bias):
    nq = 4
    rows = nq * QB
    nsb = S // rows

    def body(q_ref, k_ref, v_ref, kp_ref, vp_ref, b_ref, o_ref, l_ref, ls0, ls1, ls2, ls3):
        ls = [ls0, ls1, ls2, ls3]
        n = pl.program_id(1)
        lane = lax.broadcasted_iota(jnp.int32, (QB, 128), 1)
        units = [(h, j) for h in range(4) for j in range(nq)]

        def keys(cur_ref, prev_ref, h, j):
            sl = slice(h * HD, (h + 1) * HD)
            if j == 0:
                return jnp.concatenate([prev_ref[:, sl], cur_ref[0:QB, sl]], axis=0)
            return cur_ref[(j - 1) * QB:(j + 1) * QB, sl]

        q = jnp.stack([q_ref[j * QB:(j + 1) * QB, h * HD:(h + 1) * HD] for h, j in units])
        k = jnp.stack([keys(k_ref, kp_ref, h, j) for h, j in units])
        v = jnp.stack([keys(v_ref, vp_ref, h, j) for h, j in units])
        bias_b = jnp.stack([b_ref[jnp.minimum(n, 1), h] if j == 0 else b_ref[1, h] for h, j in units])
        s = jnp.einsum("uqd,ukd->uqk", q, k, preferred_element_type=F32) * SCALE + bias_b
        m = jnp.max(s, axis=-1, keepdims=True)
        p = jnp.exp(s - m)
        l = jnp.sum(p, axis=-1, keepdims=True)
        o = jnp.einsum("uqk,ukd->uqd", p.astype(BF16), v, preferred_element_type=F32) / l
        lse = m + jnp.log(l)
        for i, (h, j) in enumerate(units):
            o_ref[j * QB:(j + 1) * QB, h * HD:(h + 1) * HD] = o[i]
            ls[h][j * QB:(j + 1) * QB, :] = jnp.where(lane == h, lse[i], 0.0)
        l_ref[...] = (ls[0][...] + ls[1][...]) + (ls[2][...] + ls[3][...])

    def row(b, n):
        return b * nsb + n

    def prev(b, n):
        return jnp.maximum((b * nsb + n) * nq - 1, 0)

    in_specs = [
        pl.BlockSpec((rows, GW), lambda b, n: (row(b, n), CB_Q)),
        pl.BlockSpec((rows, GW), lambda b, n: (row(b, n), CB_K)),
        pl.BlockSpec((rows, GW), lambda b, n: (row(b, n), CB_V)),
        pl.BlockSpec((QB, GW), lambda b, n: (prev(b, n), CB_K)),
        pl.BlockSpec((QB, GW), lambda b, n: (prev(b, n), CB_V)),
        pl.BlockSpec((None, 2, 4, QB, 2 * QB), lambda b, n: (0, 0, 0, 0, 0)),
    ]
    return pl.pallas_call(
        body, name="attn_fwd0",
        grid=(BL, nsb),
        in_specs=in_specs,
        out_specs=(pl.BlockSpec((rows, GW), lambda b, n: (row(b, n), 0)),
                   pl.BlockSpec((rows, 128), lambda b, n: (row(b, n), 0))),
        out_shape=(jax.ShapeDtypeStruct((T, GW), F32), jax.ShapeDtypeStruct((T, 128), F32)),
        scratch_shapes=[pltpu.VMEM((rows, 128), F32)] * 4,
        compiler_params=pltpu.CompilerParams(vmem_limit_bytes=VMEM_LIMIT),
    )(proj, proj, proj, proj, proj, bias)


def _attn_bwd_dense(proj, d_out, stats, bias, dproj):
    nq = 4
    rows = nq * QB
    nsb = S // rows
    cols_q, cols_k, cols_v = CB * CB_Q, CB * CB_K, CB * CB_V

    def body(q_ref, k_ref, v_ref, do_ref, st_ref, kp_ref, vp_ref, b_ref, _, out_ref, db_ref,
             sq, sk, sv, carry, sems):
        b, n = pl.program_id(0), pl.program_id(1)
        units = [(h, j) for h in range(4) for j in range(nq)]

        @pl.when((b == 0) & (n == 0))
        def _():
            db_ref[...] = jnp.zeros_like(db_ref)

        @pl.when(n == 0)
        def _():
            carry[...] = jnp.zeros_like(carry)

        step = b * (nsb + 1) + n
        slot = step % 2

        def block_copies(s, position, block):
            part = pl.ds(position * QB, QB)
            return _column_copies([(sq.at[s, part], cols_q), (sk.at[s, part], cols_k), (sv.at[s, part], cols_v)],
                                  out_ref, pl.multiple_of(block * QB, QB), sems.at[s, position])

        def wait_blocks(s, positions):
            for position in positions:
                for cp in block_copies(s, position, 0):
                    cp.wait()

        before = (step - 2) % (nsb + 1)

        @pl.when((step >= 2) & (before > 0))
        def _():
            wait_blocks(slot, [0])

        @pl.when((step >= 2) & (before < nsb))
        def _():
            wait_blocks(slot, range(1, nq))

        def write(first_block, position, count):
            for j in range(count):
                for cp in block_copies(slot, position + j, first_block + j):
                    cp.start()

        @pl.when(n == nsb)
        def _():
            sq[slot, 0:QB, :] = carry[:, 0:GW].astype(BF16)
            sk[slot, 0:QB, :] = carry[:, GW:2 * GW].astype(BF16)
            sv[slot, 0:QB, :] = carry[:, 2 * GW:3 * GW].astype(BF16)
            write((b + 1) * nsb * nq - 1, 0, 1)

            @pl.when(b == BL - 1)
            def _():
                wait_blocks(slot, [0])
                wait_blocks(1 - slot, range(nq))

        @pl.when(n < nsb)
        def _():
            def keys(cur_ref, prev_ref, h, j):
                sl = slice(h * HD, (h + 1) * HD)
                if j == 0:
                    return jnp.concatenate([prev_ref[:, sl], cur_ref[0:QB, sl]], axis=0)
                return cur_ref[(j - 1) * QB:(j + 1) * QB, sl]

            def block(ref, h, j):
                return ref[j * QB:(j + 1) * QB, h * HD:(h + 1) * HD]

            q = jnp.stack([block(q_ref, h, j) for h, j in units])
            do = jnp.stack([block(do_ref, h, j) for h, j in units])
            k = jnp.stack([keys(k_ref, kp_ref, h, j) for h, j in units])
            v = jnp.stack([keys(v_ref, vp_ref, h, j) for h, j in units])
            bias_b = jnp.stack([b_ref[jnp.minimum(n, 1), h] if j == 0 else b_ref[1, h] for h, j in units])
            lse = jnp.stack([st_ref[j * QB:(j + 1) * QB, h:h + 1] for h, j in units])
            delta = jnp.stack([st_ref[j * QB:(j + 1) * QB, 4 + h:5 + h] for h, j in units])
            s = jnp.einsum("uqd,ukd->uqk", q, k, preferred_element_type=F32) * SCALE + bias_b
            p = jnp.exp(s - lse)
            ds = p * (jnp.einsum("uqd,ukd->uqk", do, v, preferred_element_type=F32) - delta)
            for h in range(4):
                tot = ds[h * nq]
                for j in range(1, nq):
                    tot = tot + ds[h * nq + j]
                db_ref[h] += tot
            dsb, pb = ds.astype(BF16), p.astype(BF16)
            dq = jnp.einsum("uqk,ukd->uqd", dsb, k, preferred_element_type=F32) * SCALE
            dk = jnp.einsum("uqk,uqd->ukd", dsb, q, preferred_element_type=F32) * SCALE
            dv = jnp.einsum("uqk,uqd->ukd", pb, do, preferred_element_type=F32)
            for h in range(4):
                sl = slice(h * HD, (h + 1) * HD)
                u0, last = h * nq, h * nq + nq - 1
                sq[slot, 0:QB, sl] = carry[:, sl].astype(BF16)
                sk[slot, 0:QB, sl] = (carry[:, GW + h * HD:GW + (h + 1) * HD] + dk[u0, :QB]).astype(BF16)
                sv[slot, 0:QB, sl] = (carry[:, 2 * GW + h * HD:2 * GW + (h + 1) * HD] + dv[u0, :QB]).astype(BF16)
                for j in range(nq - 1):
                    pos = slice((j + 1) * QB, (j + 2) * QB)
                    sq[slot, pos, sl] = dq[u0 + j].astype(BF16)
                    sk[slot, pos, sl] = (dk[u0 + j, QB:] + dk[u0 + j + 1, :QB]).astype(BF16)
                    sv[slot, pos, sl] = (dv[u0 + j, QB:] + dv[u0 + j + 1, :QB]).astype(BF16)
                carry[:, sl] = dq[last]
                carry[:, GW + h * HD:GW + (h + 1) * HD] = dk[last, QB:]
                carry[:, 2 * GW + h * HD:2 * GW + (h + 1) * HD] = dv[last, QB:]

            @pl.when(n == 0)
            def _():
                write(b * nsb * nq, 1, nq - 1)

            @pl.when(n > 0)
            def _():
                write((b * nsb + n) * nq - 1, 0, nq)

    def row(b, n):
        return b * nsb + jnp.minimum(n, nsb - 1)

    def prev(b, n):
        return jnp.maximum(row(b, n) * nq - 1, 0)

    in_specs = [
        pl.BlockSpec((rows, GW), lambda b, n: (row(b, n), CB_Q)),
        pl.BlockSpec((rows, GW), lambda b, n: (row(b, n), CB_K)),
        pl.BlockSpec((rows, GW), lambda b, n: (row(b, n), CB_V)),
        pl.BlockSpec((rows, GW), lambda b, n: (row(b, n), 0)),
        pl.BlockSpec((rows, 128), lambda b, n: (row(b, n), 0)),
        pl.BlockSpec((QB, GW), lambda b, n: (prev(b, n), CB_K)),
        pl.BlockSpec((QB, GW), lambda b, n: (prev(b, n), CB_V)),
        pl.BlockSpec((None, 2, 4, QB, 2 * QB), lambda b, n: (0, 0, 0, 0, 0)),
        pl.BlockSpec(memory_space=pl.ANY),
    ]
    return pl.pallas_call(
        body, name="attn_bwd0",
        grid=(BL, nsb + 1),
        in_specs=in_specs,
        out_specs=(pl.BlockSpec(memory_space=pl.ANY),
                   pl.BlockSpec((4, QB, 2 * QB), lambda b, n: (0, 0, 0))),
        out_shape=(jax.ShapeDtypeStruct((T, NCOL), BF16),
                   jax.ShapeDtypeStruct((4, QB, 2 * QB), F32)),
        scratch_shapes=[pltpu.VMEM((2, rows, GW), BF16)] * 3
                       + [pltpu.VMEM((QB, 3 * GW), F32), pltpu.SemaphoreType.DMA((2, nq, 3))],
        input_output_aliases={8: 0},
        compiler_params=pltpu.CompilerParams(vmem_limit_bytes=VMEM_LIMIT),
    )(proj, proj, proj, d_out, stats, proj, proj, bias, dproj)


def _attention_forward(proj, rel_bias):
    expand_lanes, grad_lanes, masks = (jnp.asarray(t) for t in _bucket_maps())
    bias = _bias_expand(rel_bias, expand_lanes, masks)
    fwd = [_attn_fwd_dense(proj, bias)] + [_attn_fwd(proj, bias, g) for g in (1, 2)]
    return bias, grad_lanes, [f[0] for f in fwd], [f[1] for f in fwd]


def _local_step(x2, tgt2, mod3, h, proj, attn, w_ao, w_co, w_o, conv_w, conv_b, ln_g, ln_b):
    bias, buckets, o_g, lse_g = attn

    (dproj, dyc, d_o, stats, dxd, gw_o, gw_co, gw_ao, tail_vec) = _tail(
        x2, tgt2, mod3, o_g, lse_g, proj, w_ao, w_co, w_o, conv_w, conv_b, ln_g, ln_b)

    dproj, db = _attn_bwd_dense(proj, d_o, stats, bias, dproj)
    dbias = [db]
    for g in (1, 2):
        dproj, db = _attn_bwd(proj, d_o, stats, bias, dproj, g)
        dbias.append(db)
    g_rel_bias = _bias_grad(*dbias, buckets)
    dproj, conv_vec = _conv_bwd(dyc, proj, conv_w, dproj)

    gw_ao = jnp.transpose(gw_ao.reshape(GW, N_DEV, D // N_DEV), (1, 0, 2))
    return dproj, dxd, gw_ao, gw_co, gw_o, conv_vec, g_rel_bias, tail_vec


def kernel(x, c, w_ada, b_ada, w_in, conv_w, conv_b, rel_bias, w_attn_out, w_conv_out, w_o, ln_g, ln_b, loss_target, m_w_ada, m_b_ada, m_w_in, m_conv_w, m_conv_b, m_rel_bias, m_w_attn_out, m_w_conv_out, m_w_o, m_ln_g, m_ln_b, v_w_ada, v_b_ada, v_w_in, v_conv_w, v_conv_b, v_rel_bias, v_w_attn_out, v_w_conv_out, v_w_o, v_ln_g, v_ln_b):
    me = _my_index()
    x2 = x.reshape(T, D)
    tgt2 = loss_target.reshape(T, D)

    b_cols = lax.dynamic_slice(b_ada, (0, me * ADA_SHARD), (1, ADA_SHARD))
    c_g, mod_in = _mod_exchange(jnp.pad(c, ((0, 8 - BL), (0, 0))), w_ada[0], b_cols)
    c_all = c_g[:, 0:BL, :].reshape(N_DEV * BL, D)
    mod3 = jnp.transpose(mod_in[:, 0:BL, :], (1, 0, 2)).reshape(BL, 3, D)

    rows_shape = jax.ShapeDtypeStruct((N_DEV, D // N_DEV, D), BF16)
    proj, h, w_in_all, (w_ao_g, w_co_g, w_o_g, conv_w_g) = _gather_proj(
        _shard_order(), x2, mod3, w_in[0].astype(BF16), 1024,
        ([w_attn_out[0].astype(BF16), w_conv_out[0].astype(BF16), w_o[0].astype(BF16), conv_w[0]],
         [jax.ShapeDtypeStruct((N_DEV, GW, D // N_DEV), BF16), rows_shape, rows_shape,
          jax.ShapeDtypeStruct((N_DEV, 3, D // N_DEV), F32)]))

    attn = _attention_forward(proj, rel_bias)
    w_ao_full = jnp.transpose(w_ao_g, (1, 0, 2)).reshape(GW, D)
    w_co_full = w_co_g.reshape(D, D)
    w_o_full = w_o_g.reshape(D, D)
    conv_w_full = jnp.transpose(conv_w_g, (1, 0, 2)).reshape(3, D)

    (dproj, dxd, gw_ao, gw_co, gw_o, conv_vec, g_rel_bias, tail_vec) = _local_step(
        x2, tgt2, mod3, h, proj, attn, w_ao_full, w_co_full, w_o_full,
        conv_w_full, conv_b, ln_g, ln_b)

    g_conv_w_blocks = jnp.transpose(conv_vec[0:3].reshape(3, N_DEV, D // N_DEV), (1, 0, 2))
    partials = [gw_ao, gw_co.reshape(N_DEV, D // N_DEV, D), gw_o.reshape(N_DEV, D // N_DEV, D), g_conv_w_blocks]
    w_in_sums, w_in_parts, sib = _gw_in_pair(
        _slice_order(), h, dproj, partials,
        [jax.ShapeDtypeStruct((4, GW, D // N_DEV), BF16),
         jax.ShapeDtypeStruct((4, D // N_DEV, D), BF16),
         jax.ShapeDtypeStruct((4, D // N_DEV, D), BF16),
         jax.ShapeDtypeStruct((4, 3, D // N_DEV), F32)])
    core = lax.axis_index("c").astype(jnp.int32).reshape(1)
    chip_sums = [w_in_sums] + list(_pair_add(core, partials, sib))
    hops = [(3,)] + [(1, 2, 3)] * 4
    grad_x, mod_vec, (r_in, r_ao, r_co, r_o, r_cw) = _dh_dx(
        dproj, w_in_all, x2, dxd, mod3, chip_sums, hops, w_in_parts)

    small = jnp.concatenate([
        tail_vec[0:4],
        jnp.pad(g_rel_bias.reshape(1, N_BUCKETS * N_HEADS), ((0, 0), (0, D - N_BUCKETS * N_HEADS))),
        jnp.zeros((3, D), F32)], axis=0)
    dmod = jnp.concatenate([mod_vec[0:2], mod_vec[2:4], tail_vec[4:6]], axis=1)
    small_g, dmod_g = _all_gather(
        [small, dmod],
        [jax.ShapeDtypeStruct((N_DEV, 8, D), F32), jax.ShapeDtypeStruct((N_DEV, BL, 3 * D), F32)],
        "gather_small")
    dmod_all = dmod_g.reshape(N_DEV * BL, 3 * D)
    small_names = ["b_ada", "conv_b", "rel_bias", "ln_g", "ln_b"]
    small_params = [(b_ada, m_b_ada, v_b_ada), (conv_b, m_conv_b, v_conv_b), (rel_bias, m_rel_bias, v_rel_bias),
                    (ln_g, m_ln_g, v_ln_g), (ln_b, m_ln_b, v_ln_b)]
    loss, small_res = _small_updates(
        small_g, dmod_all, small_g[:, 4, :N_BUCKETS * N_HEADS].reshape(N_DEV, N_BUCKETS, N_HEADS), small_params)
    loss = loss.reshape(())

    dmod_cols = lax.dynamic_slice(dmod_all, (0, me * ADA_SHARD), (N_DEV * BL, ADA_SHARD))
    res = {
        "w_ada": tuple(t[None] for t in _w_ada_update(jnp.transpose(c_all), dmod_cols,
                                                      w_ada[0], m_w_ada[0], v_w_ada[0])),
        "w_in": tuple(t[None] for t in _adamw(r_in, w_in[0], m_w_in[0], v_w_in[0], "adam_w_in", 128)),
    }
    mid_names = ["conv_w", "w_attn_out", "w_conv_out", "w_o"]
    mid_parts = [r_cw, r_ao, r_co, r_o]
    mid_full = [(conv_w, m_conv_w, v_conv_w), (w_attn_out, m_w_attn_out, v_w_attn_out),
                (w_conv_out, m_w_conv_out, v_w_conv_out), (w_o, m_w_o, v_w_o)]
    mid_res = _multi_adamw(mid_parts, [tuple(t[0] for t in wmv) for wmv in mid_full], "adam_mid")
    for nm, wmv, outs4 in zip(mid_names, mid_full, mid_res):
        res[nm] = tuple(t[None] for t in outs4)
    res.update(dict(zip(small_names, small_res)))
    order = ["w_ada", "b_ada", "w_in", "conv_w", "conv_b", "rel_bias", "w_attn_out", "w_conv_out",
             "w_o", "ln_g", "ln_b"]
    outs = [loss, grad_x.reshape(BL, S, D)]
    for k in range(4):
        outs += [res[name][k] for name in order]
    return tuple(outs)
```
